```python
import math
import jax, jax.numpy as jnp
from jax import lax
import numpy as np


D_MODEL = 1024
BATCH = 8
SEQ = 4096
DEPTH = 1

CTX_LEN = 256
GRID_W = 64
RET_WIDTH = 512
LRU_WIDTH = 512
MIX_WIDTH = RET_WIDTH + LRU_WIDTH
RET_HEADS = 4
RET_HEAD_DIM = RET_WIDTH // RET_HEADS
RET_CHUNK = 128
LRU_BLOCKS = 8
LRU_BLOCK_DIM = LRU_WIDTH // LRU_BLOCKS
LRU_C = 8.0
CONV_WIDTH = 4
MLP_HIDDEN = 4 * D_MODEL
ROPE_BASE = 10000.0
NORM_EPS = 1e-6
N_MOD = 6
IN_COLS = 4 * RET_WIDTH + 2 * LRU_WIDTH
IN_SPLITS = (RET_WIDTH, 2 * RET_WIDTH, 3 * RET_WIDTH, 4 * RET_WIDTH, 4 * RET_WIDTH + LRU_WIDTH)

kernel_name = "hybrid_retention_rglru_dit_block"


def rmsnorm(x, g):
    xf = x.astype(jnp.float32)
    y = xf * lax.rsqrt(jnp.mean(xf * xf, axis=-1, keepdims=True) + NORM_EPS)
    return (y * g.astype(jnp.float32)).astype(x.dtype)


def modulate(h, shift, scale):
    return h * (1.0 + scale) + shift


def to_heads(a):
    b, t, _ = a.shape
    return a.reshape(b, t, RET_HEADS, RET_HEAD_DIM).astype(jnp.float32)


def head_groupnorm(y):
    mu = jnp.mean(y, axis=-1, keepdims=True)
    yc = y - mu
    var = jnp.mean(yc * yc, axis=-1, keepdims=True)
    return yc * lax.rsqrt(var + NORM_EPS)


def axial_rotary_tables(t_len):
    rows = t_len // GRID_W
    row = jnp.repeat(jnp.arange(rows, dtype=jnp.float32), GRID_W)
    col = jnp.tile(jnp.arange(GRID_W, dtype=jnp.float32), rows)
    n_freq = RET_HEAD_DIM // 4
    inv = ROPE_BASE ** (-jnp.arange(n_freq, dtype=jnp.float32) / n_freq)
    ang = jnp.concatenate([row[:, None] * inv, col[:, None] * inv], axis=-1)
    return jnp.cos(ang), jnp.sin(ang)


def apply_rotary(x, cos, sin):
    half = RET_HEAD_DIM // 2
    x1, x2 = x[..., :half], x[..., half:]
    c = cos[None, :, None, :]
    s = sin[None, :, None, :]
    return jnp.concatenate([x1 * c - x2 * s, x2 * c + x1 * s], axis=-1)


def retention_scan(q, k, v, log_g, s0):
    b, t, h, dh = q.shape
    n = t // RET_CHUNK

    def chunks(a):
        return a.reshape(b, n, RET_CHUNK, h, dh).transpose(0, 3, 1, 2, 4)

    qc, kc, vc = chunks(q), chunks(k), chunks(v)
    pos = jnp.arange(RET_CHUNK, dtype=jnp.float32)
    lg = log_g[:, None]
    rel = pos[:, None] - pos[None, :]
    decay = jnp.where(rel >= 0, jnp.exp(lg[:, :, None] * jnp.maximum(rel, 0.0)), 0.0)
    scores = jnp.einsum('bhncd,bhnmd->bhncm', qc, kc) * decay[None, :, None]
    intra = jnp.einsum('bhncm,bhnme->bhnce', scores, vc)
    w_state = jnp.exp(lg * (RET_CHUNK - 1.0 - pos))
    u = jnp.einsum('bhncd,bhnce->bhnde', kc * w_state[None, :, None, :, None], vc)
    g_chunk = jnp.exp(lg * RET_CHUNK)[None, :, :, None]

    def step(s, u_n):
        return g_chunk * s + u_n, s

    _, s_prev = lax.scan(step, s0, u.transpose(2, 0, 1, 3, 4))
    s_prev = s_prev.transpose(1, 2, 0, 3, 4)
    w_query = jnp.exp(lg * (pos + 1.0))
    cross = jnp.einsum('bhncd,bhnde->bhnce', qc * w_query[None, :, None, :, None], s_prev)
    return (intra + cross).transpose(0, 2, 3, 1, 4).reshape(b, t, h, dh)


def retention_bidir(q, k, v, log_g_f, log_g_b, s0_f, s0_b):
    o_f = retention_scan(q, k, v, log_g_f, s0_f)
    o_b = retention_scan(q[:, ::-1], k[:, ::-1], v[:, ::-1], log_g_b, s0_b)[:, ::-1]
    return o_f + o_b


def retention_final_state(k, v, log_g, reverse):
    l_len = k.shape[1]
    pos = jnp.arange(l_len, dtype=jnp.float32)
    steps = pos if reverse else (l_len - 1.0 - pos)
    w = jnp.exp(log_g[:, None] * steps[None, :])
    return jnp.einsum('blhd,blhe->bhde', k * w.T[None, :, :, None], v)


def centred_depthwise_conv(x, w, bias):
    t_len = x.shape[1]
    left = (CONV_WIDTH - 1) // 2
    right = CONV_WIDTH - 1 - left
    xp = jnp.pad(x, ((0, 0), (left, right), (0, 0)))
    out = bias + xp[:, 0:t_len] * w[0]
    for j in range(1, CONV_WIDTH):
        out = out + xp[:, j:j + t_len] * w[j]
    return out


def linear_scan(a, b, h0, reverse):
    if reverse:
        a, b = a[:, ::-1], b[:, ::-1]
    b = b.at[:, 0].add(a[:, 0] * h0)

    def combine(lhs, rhs):
        return lhs[0] * rhs[0], rhs[0] * lhs[1] + rhs[1]

    _, h = lax.associative_scan(combine, (a, b), axis=1)
    final = h[:, -1]
    if reverse:
        h = h[:, ::-1]
    return h, final


def rglru_direction(xc, w_a, b_a, w_x, b_x, lam, h0, reverse):
    b, t, _ = xc.shape
    xf = xc.astype(jnp.float32)
    xb = xf.reshape(b, t, LRU_BLOCKS, LRU_BLOCK_DIM)
    r = jax.nn.sigmoid(jnp.einsum('btnd,nde->btne', xb, w_a.astype(jnp.float32)).reshape(b, t, LRU_WIDTH) + b_a.astype(jnp.float32))
    i = jax.nn.sigmoid(jnp.einsum('btnd,nde->btne', xb, w_x.astype(jnp.float32)).reshape(b, t, LRU_WIDTH) + b_x.astype(jnp.float32))
    log_a = -LRU_C * r * jax.nn.softplus(-lam.astype(jnp.float32))
    a = jnp.exp(log_a)
    inp = jnp.sqrt(-jnp.expm1(2.0 * log_a)) * (i * xf)
    return linear_scan(a, inp, h0, reverse)


def mix_output(o_ret, g, h_lru, gate, w_out):
    b, t = g.shape[0], g.shape[1]
    ret = jax.nn.silu(g.astype(jnp.float32)) * head_groupnorm(o_ret).reshape(b, t, RET_WIDTH)
    lru = h_lru * jax.nn.gelu(gate.astype(jnp.float32))
    return jnp.concatenate([ret, lru], axis=-1).astype(w_out.dtype) @ w_out


def squared_relu_mlp(h, w1, w2):
    return jnp.square(jax.nn.relu(h @ w1)) @ w2


def _fwd_setup_inputs(seed: int = 0) -> dict:
    key = jax.random.key(seed)
    ks = jax.random.split(key, 24)
    f32 = jnp.float32

    def nrm(k, shape, scale):
        return jax.random.normal(k, shape, f32) * scale

    x = nrm(ks[0], (BATCH, SEQ, D_MODEL), 1.0)
    c = nrm(ks[1], (BATCH, D_MODEL), 1.0)
    ctx = nrm(ks[2], (BATCH, CTX_LEN, D_MODEL), 1.0)
    c_ctx = nrm(ks[3], (D_MODEL,), 1.0)
    w_ada = nrm(ks[4], (DEPTH, D_MODEL, N_MOD * D_MODEL), 0.5 * D_MODEL ** -0.5)
    b_ada = nrm(ks[5], (DEPTH, N_MOD * D_MODEL), 0.01)
    norm1_g = 1.0 + nrm(ks[6], (DEPTH, D_MODEL), 0.02)
    norm2_g = 1.0 + nrm(ks[7], (DEPTH, D_MODEL), 0.02)
    w_in = nrm(ks[8], (DEPTH, D_MODEL, IN_COLS), D_MODEL ** -0.5)
    gamma = 1.0 - 2.0 ** (-5.0 - jnp.arange(RET_HEADS, dtype=f32))
    ret_decay = jnp.log(gamma) - jnp.log1p(-gamma) + nrm(ks[9], (DEPTH, 2, RET_HEADS), 0.05)
    conv_w = nrm(ks[10], (DEPTH, CONV_WIDTH, LRU_WIDTH), CONV_WIDTH ** -0.5)
    conv_b = nrm(ks[11], (DEPTH, LRU_WIDTH), 0.01)
    lru_wa = nrm(ks[12], (DEPTH, 2, LRU_BLOCKS, LRU_BLOCK_DIM, LRU_BLOCK_DIM), LRU_BLOCK_DIM ** -0.5)
    lru_ba = nrm(ks[13], (DEPTH, 2, LRU_WIDTH), 0.01)
    lru_wx = nrm(ks[14], (DEPTH, 2, LRU_BLOCKS, LRU_BLOCK_DIM, LRU_BLOCK_DIM), LRU_BLOCK_DIM ** -0.5)
    lru_bx = nrm(ks[15], (DEPTH, 2, LRU_WIDTH), 0.01)
    u = jax.random.uniform(ks[16], (DEPTH, 2, LRU_WIDTH), f32, 0.9, 0.999)
    a0 = u ** (1.0 / LRU_C)
    lru_lambda = jnp.log(a0) - jnp.log1p(-a0)
    w_out = nrm(ks[17], (DEPTH, MIX_WIDTH, D_MODEL), MIX_WIDTH ** -0.5)
    w_mlp1 = nrm(ks[18], (DEPTH, D_MODEL, MLP_HIDDEN), D_MODEL ** -0.5)
    w_mlp2 = nrm(ks[19], (DEPTH, MLP_HIDDEN, D_MODEL), MLP_HIDDEN ** -0.5)
    final_g = 1.0 + nrm(ks[20], (D_MODEL,), 0.02)
    return dict(x=x, c=c, ctx=ctx, c_ctx=c_ctx, w_ada=w_ada, b_ada=b_ada, norm1_g=norm1_g,
                norm2_g=norm2_g, w_in=w_in, ret_decay=ret_decay, conv_w=conv_w, conv_b=conv_b,
                lru_wa=lru_wa, lru_ba=lru_ba, lru_wx=lru_wx, lru_bx=lru_bx, lru_lambda=lru_lambda,
                w_out=w_out, w_mlp1=w_mlp1, w_mlp2=w_mlp2, final_g=final_g)


def _fwd_reference(x, c, ctx, c_ctx, w_ada, b_ada, norm1_g, norm2_g, w_in, ret_decay, conv_w, conv_b,
              lru_wa, lru_ba, lru_wx, lru_bx, lru_lambda, w_out, w_mlp1, w_mlp2, final_g):
    b, t_len, _ = x.shape
    cos, sin = axial_rotary_tables(t_len)
    k_scale = RET_HEAD_DIM ** -0.5
    silu_c = jax.nn.silu(c)
    silu_cc = jax.nn.silu(c_ctx)
    for l in range(DEPTH):
        last = l == DEPTH - 1
        mod = silu_c @ w_ada[l] + b_ada[l]
        mod_c = silu_cc @ w_ada[l] + b_ada[l]
        sh1, sc1, g1, sh2, sc2, g2 = [m[:, None] for m in jnp.split(mod, N_MOD, axis=-1)]
        csh1, csc1, cg1, csh2, csc2, cg2 = jnp.split(mod_c, N_MOD, axis=-1)
        w_q, w_k, w_v, w_g, w_x, w_gate = jnp.split(w_in[l], IN_SPLITS, axis=1)
        lg_f = jax.nn.log_sigmoid(ret_decay[l, 0].astype(jnp.float32))
        lg_b = jax.nn.log_sigmoid(ret_decay[l, 1].astype(jnp.float32))
        lru_f = (lru_wa[l, 0], lru_ba[l, 0], lru_wx[l, 0], lru_bx[l, 0], lru_lambda[l, 0])
        lru_b = (lru_wa[l, 1], lru_ba[l, 1], lru_wx[l, 1], lru_bx[l, 1], lru_lambda[l, 1])

        hc = modulate(rmsnorm(ctx, norm1_g[l]), csh1, csc1)
        kc = to_heads(hc @ w_k) * k_scale
        vc = to_heads(hc @ w_v)
        s_f = retention_final_state(kc, vc, lg_f, False)
        s_b = retention_final_state(kc, vc, lg_b, True)
        xcc = centred_depthwise_conv(hc @ w_x, conv_w[l], conv_b[l])
        zero_h = jnp.zeros((b, LRU_WIDTH), jnp.float32)
        hcf, lru_sf = rglru_direction(xcc, *lru_f, zero_h, False)
        hcb, lru_sb = rglru_direction(xcc, *lru_b, zero_h, True)
        if not last:
            qc = to_heads(hc @ w_q)
            zero_s = jnp.zeros((b, RET_HEADS, RET_HEAD_DIM, RET_HEAD_DIM), jnp.float32)
            o_c = retention_bidir(qc, kc, vc, lg_f, lg_b, zero_s, zero_s)
            yc = mix_output(o_c, hc @ w_g, hcf + hcb, hc @ w_gate, w_out[l])
            ctx_next = ctx + cg1 * yc
            hc2 = modulate(rmsnorm(ctx_next, norm2_g[l]), csh2, csc2)
            ctx_next = ctx_next + cg2 * squared_relu_mlp(hc2, w_mlp1[l], w_mlp2[l])

        h = modulate(rmsnorm(x, norm1_g[l]), sh1, sc1)
        q, k, v, g, xr, gate = jnp.split(h @ w_in[l], IN_SPLITS, axis=-1)
        q = apply_rotary(to_heads(q), cos, sin)
        k = apply_rotary(to_heads(k), cos, sin) * k_scale
        o = retention_bidir(q, k, to_heads(v), lg_f, lg_b, s_f, s_b)
        xcl = centred_depthwise_conv(xr, conv_w[l], conv_b[l])
        hf, _ = rglru_direction(xcl, *lru_f, lru_sf, False)
        hb, _ = rglru_direction(xcl, *lru_b, lru_sb, True)
        y = mix_output(o, g, hf + hb, gate, w_out[l])
        x = x + g1 * y
        h2 = modulate(rmsnorm(x, norm2_g[l]), sh2, sc2)
        x = x + g2 * squared_relu_mlp(h2, w_mlp1[l], w_mlp2[l])
        if not last:
            ctx = ctx_next
    return rmsnorm(x, final_g)


import jax as _jax
import jax.numpy as _jnp

TWIN_FORMAT = 'train_step'
FWD_PARAMS = ['x', 'c', 'ctx', 'c_ctx', 'w_ada', 'b_ada', 'norm1_g', 'norm2_g', 'w_in', 'ret_decay', 'conv_w', 'conv_b', 'lru_wa', 'lru_ba', 'lru_wx', 'lru_bx', 'lru_lambda', 'w_out', 'w_mlp1', 'w_mlp2', 'final_g']
TWIN_WEIGHTS = ['c_ctx', 'w_ada', 'b_ada', 'norm1_g', 'norm2_g', 'w_in', 'ret_decay', 'conv_w', 'conv_b', 'lru_wa', 'lru_ba', 'lru_wx', 'lru_bx', 'lru_lambda', 'w_out', 'w_mlp1', 'w_mlp2', 'final_g']
TWIN_DIFF_INPUT = 'x'
TWIN_INPUTS = ['x', 'c', 'ctx', 'c_ctx', 'w_ada', 'b_ada', 'norm1_g', 'norm2_g', 'w_in', 'ret_decay', 'conv_w', 'conv_b', 'lru_wa', 'lru_ba', 'lru_wx', 'lru_bx', 'lru_lambda', 'w_out', 'w_mlp1', 'w_mlp2', 'final_g', 'loss_target', 'm_c_ctx', 'm_w_ada', 'm_b_ada', 'm_norm1_g', 'm_norm2_g', 'm_w_in', 'm_ret_decay', 'm_conv_w', 'm_conv_b', 'm_lru_wa', 'm_lru_ba', 'm_lru_wx', 'm_lru_bx', 'm_lru_lambda', 'm_w_out', 'm_w_mlp1', 'm_w_mlp2', 'm_final_g', 'v_c_ctx', 'v_w_ada', 'v_b_ada', 'v_norm1_g', 'v_norm2_g', 'v_w_in', 'v_ret_decay', 'v_conv_w', 'v_conv_b', 'v_lru_wa', 'v_lru_ba', 'v_lru_wx', 'v_lru_bx', 'v_lru_lambda', 'v_w_out', 'v_w_mlp1', 'v_w_mlp2', 'v_final_g']
TWIN_OUTPUTS = ['loss', 'grad_x', 'grad_c_ctx', 'grad_w_ada', 'grad_b_ada', 'grad_norm1_g', 'grad_norm2_g', 'grad_w_in', 'grad_ret_decay', 'grad_conv_w', 'grad_conv_b', 'grad_lru_wa', 'grad_lru_ba', 'grad_lru_wx', 'grad_lru_bx', 'grad_lru_lambda', 'grad_w_out', 'grad_w_mlp1', 'grad_w_mlp2', 'grad_final_g', 'delta_c_ctx', 'delta_w_ada', 'delta_b_ada', 'delta_norm1_g', 'delta_norm2_g', 'delta_w_in', 'delta_ret_decay', 'delta_conv_w', 'delta_conv_b', 'delta_lru_wa', 'delta_lru_ba', 'delta_lru_wx', 'delta_lru_bx', 'delta_lru_lambda', 'delta_w_out', 'delta_w_mlp1', 'delta_w_mlp2', 'delta_final_g', 'new_m_c_ctx', 'new_m_w_ada', 'new_m_b_ada', 'new_m_norm1_g', 'new_m_norm2_g', 'new_m_w_in', 'new_m_ret_decay', 'new_m_conv_w', 'new_m_conv_b', 'new_m_lru_wa', 'new_m_lru_ba', 'new_m_lru_wx', 'new_m_lru_bx', 'new_m_lru_lambda', 'new_m_w_out', 'new_m_w_mlp1', 'new_m_w_mlp2', 'new_m_final_g', 'new_v_c_ctx', 'new_v_w_ada', 'new_v_b_ada', 'new_v_norm1_g', 'new_v_norm2_g', 'new_v_w_in', 'new_v_ret_decay', 'new_v_conv_w', 'new_v_conv_b', 'new_v_lru_wa', 'new_v_lru_ba', 'new_v_lru_wx', 'new_v_lru_bx', 'new_v_lru_lambda', 'new_v_w_out', 'new_v_w_mlp1', 'new_v_w_mlp2', 'new_v_final_g']
TWIN_LEAF_KINDS = {'loss': 'loss', 'grad_x': 'grad_x', 'grad_c_ctx': 'grad_w', 'grad_w_ada': 'grad_w', 'grad_b_ada': 'grad_w', 'grad_norm1_g': 'grad_w', 'grad_norm2_g': 'grad_w', 'grad_w_in': 'grad_w', 'grad_ret_decay': 'grad_w', 'grad_conv_w': 'grad_w', 'grad_conv_b': 'grad_w', 'grad_lru_wa': 'grad_w', 'grad_lru_ba': 'grad_w', 'grad_lru_wx': 'grad_w', 'grad_lru_bx': 'grad_w', 'grad_lru_lambda': 'grad_w', 'grad_w_out': 'grad_w', 'grad_w_mlp1': 'grad_w', 'grad_w_mlp2': 'grad_w', 'grad_final_g': 'grad_w', 'delta_c_ctx': 'delta_w', 'delta_w_ada': 'delta_w', 'delta_b_ada': 'delta_w', 'delta_norm1_g': 'delta_w', 'delta_norm2_g': 'delta_w', 'delta_w_in': 'delta_w', 'delta_ret_decay': 'delta_w', 'delta_conv_w': 'delta_w', 'delta_conv_b': 'delta_w', 'delta_lru_wa': 'delta_w', 'delta_lru_ba': 'delta_w', 'delta_lru_wx': 'delta_w', 'delta_lru_bx': 'delta_w', 'delta_lru_lambda': 'delta_w', 'delta_w_out': 'delta_w', 'delta_w_mlp1': 'delta_w', 'delta_w_mlp2': 'delta_w', 'delta_final_g': 'delta_w', 'new_m_c_ctx': 'new_m', 'new_m_w_ada': 'new_m', 'new_m_b_ada': 'new_m', 'new_m_norm1_g': 'new_m', 'new_m_norm2_g': 'new_m', 'new_m_w_in': 'new_m', 'new_m_ret_decay': 'new_m', 'new_m_conv_w': 'new_m', 'new_m_conv_b': 'new_m', 'new_m_lru_wa': 'new_m', 'new_m_lru_ba': 'new_m', 'new_m_lru_wx': 'new_m', 'new_m_lru_bx': 'new_m', 'new_m_lru_lambda': 'new_m', 'new_m_w_out': 'new_m', 'new_m_w_mlp1': 'new_m', 'new_m_w_mlp2': 'new_m', 'new_m_final_g': 'new_m', 'new_v_c_ctx': 'new_v', 'new_v_w_ada': 'new_v', 'new_v_b_ada': 'new_v', 'new_v_norm1_g': 'new_v', 'new_v_norm2_g': 'new_v', 'new_v_w_in': 'new_v', 'new_v_ret_decay': 'new_v', 'new_v_conv_w': 'new_v', 'new_v_conv_b': 'new_v', 'new_v_lru_wa': 'new_v', 'new_v_lru_ba': 'new_v', 'new_v_lru_wx': 'new_v', 'new_v_lru_bx': 'new_v', 'new_v_lru_lambda': 'new_v', 'new_v_w_out': 'new_v', 'new_v_w_mlp1': 'new_v', 'new_v_w_mlp2': 'new_v', 'new_v_final_g': 'new_v'}


def _forward(args):
    return _fwd_reference(*[args[k] for k in FWD_PARAMS])


def _output_shape():
    out = _jax.eval_shape(lambda: _forward(_fwd_setup_inputs(0)))
    return out.shape, out.dtype

N_MICROBATCH = 1
ADAM_LR = 0.001
ADAM_B1 = 0.9
ADAM_B2 = 0.999
ADAM_EPS = 1e-08
ADAM_WD = 0.01
ADAM_STEP = 10
PER_EXAMPLE_BATCH_AXIS = {'x': 0, 'c': 0, 'ctx': 0, 'loss_target': 0}
SHARED_INPUTS = []
_WEIGHT_DTYPES = {'c_ctx': _jnp.float32, 'w_ada': _jnp.float32, 'b_ada': _jnp.float32, 'norm1_g': _jnp.float32, 'norm2_g': _jnp.float32, 'w_in': _jnp.float32, 'ret_decay': _jnp.float32, 'conv_w': _jnp.float32, 'conv_b': _jnp.float32, 'lru_wa': _jnp.float32, 'lru_ba': _jnp.float32, 'lru_wx': _jnp.float32, 'lru_bx': _jnp.float32, 'lru_lambda': _jnp.float32, 'w_out': _jnp.float32, 'w_mlp1': _jnp.float32, 'w_mlp2': _jnp.float32, 'final_g': _jnp.float32}
MOMENT_SCALE = {'c_ctx': 2.988717e-02, 'w_ada': 1.496809e-01, 'b_ada': 2.373945e-01, 'norm1_g': 1.077351e-01, 'norm2_g': 7.669787e-02, 'w_in': 8.283675e-02, 'ret_decay': 1.962062e-01, 'conv_w': 1.226331e-01, 'conv_b': 3.936333e-01, 'lru_wa': 1.094648e-02, 'lru_ba': 1.302133e-02, 'lru_wx': 2.008509e-02, 'lru_bx': 2.499132e-02, 'lru_lambda': 2.682136e-02, 'w_out': 1.023166e-01, 'w_mlp1': 3.975745e-02, 'w_mlp2': 6.947790e-02, 'final_g': 3.249799e+01}


def _to_microbatches(a, axis):
    t = _jnp.moveaxis(a, axis, 0)
    t = t.reshape((N_MICROBATCH, t.shape[0] // N_MICROBATCH) + t.shape[1:])
    return _jnp.moveaxis(t, 1, axis + 1)


def setup_inputs(seed: int = 0) -> dict:
    inp = _fwd_setup_inputs(seed)
    key = _jax.random.fold_in(_jax.random.key(seed), 7919)
    shape, _ = _output_shape()
    out = dict(inp)
    out["loss_target"] = _jax.random.normal(_jax.random.fold_in(key, 0), shape, _jnp.float32)
    for i, name in enumerate(TWIN_WEIGHTS):
        w = inp[name].astype(_jnp.float32)
        if MOMENT_SCALE is None:
            s = _jnp.sqrt(_jnp.mean(_jnp.square(w)) + 1e-30)
        else:
            s = MOMENT_SCALE[name]
        km, kv = _jax.random.split(_jax.random.fold_in(key, i + 1))
        out[name] = w
        out["m_" + name] = s * _jax.random.normal(km, w.shape, _jnp.float32)
        out["v_" + name] = (s * s) * _jax.random.uniform(kv, w.shape, _jnp.float32, 0.5, 1.5)
    if N_MICROBATCH > 1:
        for name, axis in PER_EXAMPLE_BATCH_AXIS.items():
            out[name] = _to_microbatches(out[name], axis)
    return {'x': out['x'], 'c': out['c'], 'ctx': out['ctx'], 'c_ctx': out['c_ctx'], 'w_ada': out['w_ada'], 'b_ada': out['b_ada'], 'norm1_g': out['norm1_g'], 'norm2_g': out['norm2_g'], 'w_in': out['w_in'], 'ret_decay': out['ret_decay'], 'conv_w': out['conv_w'], 'conv_b': out['conv_b'], 'lru_wa': out['lru_wa'], 'lru_ba': out['lru_ba'], 'lru_wx': out['lru_wx'], 'lru_bx': out['lru_bx'], 'lru_lambda': out['lru_lambda'], 'w_out': out['w_out'], 'w_mlp1': out['w_mlp1'], 'w_mlp2': out['w_mlp2'], 'final_g': out['final_g'], 'loss_target': out['loss_target'], 'm_c_ctx': out['m_c_ctx'], 'm_w_ada': out['m_w_ada'], 'm_b_ada': out['m_b_ada'], 'm_norm1_g': out['m_norm1_g'], 'm_norm2_g': out['m_norm2_g'], 'm_w_in': out['m_w_in'], 'm_ret_decay': out['m_ret_decay'], 'm_conv_w': out['m_conv_w'], 'm_conv_b': out['m_conv_b'], 'm_lru_wa': out['m_lru_wa'], 'm_lru_ba': out['m_lru_ba'], 'm_lru_wx': out['m_lru_wx'], 'm_lru_bx': out['m_lru_bx'], 'm_lru_lambda': out['m_lru_lambda'], 'm_w_out': out['m_w_out'], 'm_w_mlp1': out['m_w_mlp1'], 'm_w_mlp2': out['m_w_mlp2'], 'm_final_g': out['m_final_g'], 'v_c_ctx': out['v_c_ctx'], 'v_w_ada': out['v_w_ada'], 'v_b_ada': out['v_b_ada'], 'v_norm1_g': out['v_norm1_g'], 'v_norm2_g': out['v_norm2_g'], 'v_w_in': out['v_w_in'], 'v_ret_decay': out['v_ret_decay'], 'v_conv_w': out['v_conv_w'], 'v_conv_b': out['v_conv_b'], 'v_lru_wa': out['v_lru_wa'], 'v_lru_ba': out['v_lru_ba'], 'v_lru_wx': out['v_lru_wx'], 'v_lru_bx': out['v_lru_bx'], 'v_lru_lambda': out['v_lru_lambda'], 'v_w_out': out['v_w_out'], 'v_w_mlp1': out['v_w_mlp1'], 'v_w_mlp2': out['v_w_mlp2'], 'v_final_g': out['v_final_g']}


def _loss(weights, diff, rest, loss_target):
    with _jax.named_scope("forward"):
        args = {**rest, TWIN_DIFF_INPUT: diff, **{k: w.astype(_WEIGHT_DTYPES[k]) for k, w in weights.items()}}
        y = _forward(args)
    with _jax.named_scope("loss_head"):
        err = _jnp.square(y.astype(_jnp.float32) - loss_target)
        return 0.5 * _jnp.sum(_jnp.mean(err, axis=-1)) if err.ndim else 0.5 * err


def _adamw(w, g, m, v):
    m = ADAM_B1 * m + (1.0 - ADAM_B1) * g
    v = ADAM_B2 * v + (1.0 - ADAM_B2) * _jnp.square(g)
    m_hat = m / (1.0 - ADAM_B1 ** ADAM_STEP)
    v_hat = v / (1.0 - ADAM_B2 ** ADAM_STEP)
    delta = -ADAM_LR * (m_hat / (_jnp.sqrt(v_hat) + ADAM_EPS) + ADAM_WD * w)
    return delta, m, v


def reference(x, c, ctx, c_ctx, w_ada, b_ada, norm1_g, norm2_g, w_in, ret_decay, conv_w, conv_b, lru_wa, lru_ba, lru_wx, lru_bx, lru_lambda, w_out, w_mlp1, w_mlp2, final_g, loss_target, m_c_ctx, m_w_ada, m_b_ada, m_norm1_g, m_norm2_g, m_w_in, m_ret_decay, m_conv_w, m_conv_b, m_lru_wa, m_lru_ba, m_lru_wx, m_lru_bx, m_lru_lambda, m_w_out, m_w_mlp1, m_w_mlp2, m_final_g, v_c_ctx, v_w_ada, v_b_ada, v_norm1_g, v_norm2_g, v_w_in, v_ret_decay, v_conv_w, v_conv_b, v_lru_wa, v_lru_ba, v_lru_wx, v_lru_bx, v_lru_lambda, v_w_out, v_w_mlp1, v_w_mlp2, v_final_g):
    given = dict(x=x, c=c, ctx=ctx, c_ctx=c_ctx, w_ada=w_ada, b_ada=b_ada, norm1_g=norm1_g, norm2_g=norm2_g, w_in=w_in, ret_decay=ret_decay, conv_w=conv_w, conv_b=conv_b, lru_wa=lru_wa, lru_ba=lru_ba, lru_wx=lru_wx, lru_bx=lru_bx, lru_lambda=lru_lambda, w_out=w_out, w_mlp1=w_mlp1, w_mlp2=w_mlp2, final_g=final_g, loss_target=loss_target, m_c_ctx=m_c_ctx, m_w_ada=m_w_ada, m_b_ada=m_b_ada, m_norm1_g=m_norm1_g, m_norm2_g=m_norm2_g, m_w_in=m_w_in, m_ret_decay=m_ret_decay, m_conv_w=m_conv_w, m_conv_b=m_conv_b, m_lru_wa=m_lru_wa, m_lru_ba=m_lru_ba, m_lru_wx=m_lru_wx, m_lru_bx=m_lru_bx, m_lru_lambda=m_lru_lambda, m_w_out=m_w_out, m_w_mlp1=m_w_mlp1, m_w_mlp2=m_w_mlp2, m_final_g=m_final_g, v_c_ctx=v_c_ctx, v_w_ada=v_w_ada, v_b_ada=v_b_ada, v_norm1_g=v_norm1_g, v_norm2_g=v_norm2_g, v_w_in=v_w_in, v_ret_decay=v_ret_decay, v_conv_w=v_conv_w, v_conv_b=v_conv_b, v_lru_wa=v_lru_wa, v_lru_ba=v_lru_ba, v_lru_wx=v_lru_wx, v_lru_bx=v_lru_bx, v_lru_lambda=v_lru_lambda, v_w_out=v_w_out, v_w_mlp1=v_w_mlp1, v_w_mlp2=v_w_mlp2, v_final_g=v_final_g)
    weights = {n: given[n] for n in TWIN_WEIGHTS}
    shared = {n: given[n] for n in SHARED_INPUTS}
    per_example = {n: given[n] for n in ['x', 'c', 'ctx']}
    grad_fn = _jax.value_and_grad(_loss, argnums=(0, 1))

    def one_microbatch(ex, loss_target):
        ex = dict(ex)
        diff = ex.pop(TWIN_DIFF_INPUT)
        return grad_fn(weights, diff, {**shared, **ex}, loss_target)

    if N_MICROBATCH == 1:
        loss, (grad_w, grad_x) = one_microbatch(per_example, given["loss_target"])
    else:
        def body(carry, xs):
            loss_sum, grad_sum = carry
            l_k, (gw_k, gx_k) = one_microbatch(xs[0], xs[1])
            with _jax.named_scope("update"):
                return (loss_sum + l_k, _jax.tree.map(_jnp.add, grad_sum, gw_k)), gx_k

        init = (_jnp.zeros((), _jnp.float32), _jax.tree.map(_jnp.zeros_like, weights))
        (loss, grad_w), grad_x = _jax.lax.scan(body, init, (per_example, given["loss_target"]))
    with _jax.named_scope("update"):
        delta_w, new_m, new_v = {}, {}, {}
        for n in TWIN_WEIGHTS:
            delta_w[n], new_m[n], new_v[n] = _adamw(weights[n], grad_w[n], given["m_" + n], given["v_" + n])
    return (loss, grad_x, *[grad_w[n] for n in TWIN_WEIGHTS], *[delta_w[n] for n in TWIN_WEIGHTS],
            *[new_m[n] for n in TWIN_WEIGHTS], *[new_v[n] for n in TWIN_WEIGHTS])
```

```python
import jax
import jax.numpy as jnp
from jax import lax
from jax.experimental import pallas as pl
from jax.experimental.pallas import tpu as pltpu

F32 = jnp.float32
BF16 = jnp.bfloat16
AXES = ("x", "y", "c")
N_DEV = 8
MESH = pl.DeviceIdType.MESH

HEADS = 4
HEAD_DIM = 128
CHUNK = 128
RET_W = HEADS * HEAD_DIM
LRU_W = 512
LRU_BLOCKS = 8
LRU_BD = LRU_W // LRU_BLOCKS
LRU_C = 8.0
EPS = 1e-6
K_SCALE = HEAD_DIM ** -0.5
ROPE_BASE = 10000.0
GRID_W = 64
TM = 256
SUB = 8

ADAM_LR = 0.001
ADAM_B1 = 0.9
ADAM_B2 = 0.999
ADAM_EPS = 1e-08
ADAM_WD = 0.01
ADAM_STEP = 10

R_CSH1, R_CSC1, R_SH1, R_SC1, R_G1, R_SH2, R_SC2, R_G2 = range(8)


def _pcall(body, **kw):
    return pl.pallas_call(body, **kw)


def _cp(vmem_mb=48):
    return pltpu.CompilerParams(vmem_limit_bytes=vmem_mb << 20)


def _sds(shape, dtype):
    return jax.ShapeDtypeStruct(shape, dtype)


def _dot(a, b):
    return jnp.dot(a.astype(BF16), b.astype(BF16), preferred_element_type=F32)


def _dot_nt(a, b):
    return lax.dot_general(a.astype(BF16), b.astype(BF16), (((1,), (1,)), ((), ())), preferred_element_type=F32)


def _dot_tn(a, b):
    return lax.dot_general(a.astype(BF16), b.astype(BF16), (((0,), (0,)), ((), ())), preferred_element_type=F32)


def _sigmoid(x):
    return jax.nn.sigmoid(x)


def _gelu(x):
    return 0.5 * x * (1.0 + jnp.tanh(0.7978845608028654 * (x + 0.044715 * x * x * x)))


def _dgelu(x):
    t = jnp.tanh(0.7978845608028654 * (x + 0.044715 * x * x * x))
    return 0.5 * (1.0 + t) + 0.5 * x * (1.0 - t * t) * 0.7978845608028654 * (1.0 + 3.0 * 0.044715 * x * x)


def _one_minus_exp(x):
    p = x * (1.0 + x * (1.0 / 2) * (1.0 + x * (1.0 / 3) * (1.0 + x * (1.0 / 4) * (1.0 + x * (1.0 / 5) * (
        1.0 + x * (1.0 / 6) * (1.0 + x * (1.0 / 7) * (1.0 + x * (1.0 / 8))))))))
    return -jnp.where(jnp.abs(x) < 0.3, p, jnp.exp(x) - 1.0)


def _rows_iota(shape):
    return lax.broadcasted_iota(jnp.int32, shape, 0)


def _tile_order(dirn, s, cb, nb):
    if dirn == 0:
        return s
    return jnp.where(s < cb, cb - 1 - s, nb - 1 - (s - cb))


def _all_gather(x, name):
    def body(x_ref, out_ref, send_sems, recv_sems, local_sem):
        mx, my, mc = lax.axis_index("x"), lax.axis_index("y"), lax.axis_index("c")
        me, sibling = (mx, my, mc), (mx, my, 1 - mc)
        chips = [(1 - mx, my), (mx, 1 - my), (1 - mx, 1 - my)]

        def slot(px, py, pc):
            return out_ref.at[4 * px + 2 * py + pc]

        def copy(k, block, to, src=None):
            return pltpu.make_async_remote_copy(
                src_ref=slot(*block) if src is None else src, dst_ref=slot(*block),
                send_sem=send_sems.at[k], recv_sem=recv_sems.at[k], device_id=to, device_id_type=MESH)

        mine = pltpu.make_async_copy(x_ref, slot(*me), local_sem)
        mine.start()
        first = [copy(0, me, sibling, src=x_ref)]
        first += [copy(1 + j, me, (*chip, mc), src=x_ref) for j, chip in enumerate(chips)]
        for cp in first:
            cp.start()
        passed = [copy(4 + j, (*chip, mc), sibling) for j, chip in enumerate(chips)]
        for j, chip in enumerate(chips):
            copy(1 + j, (*chip, mc), me).wait_recv()
            passed[j].start()
        copy(0, sibling, me).wait_recv()
        for j, chip in enumerate(chips):
            copy(4 + j, (*chip, 1 - mc), me).wait_recv()
        for cp in first + passed:
            cp.wait_send()
        mine.wait()

    return _pcall(
        body, name=name, out_shape=_sds((N_DEV,) + x.shape, x.dtype),
        in_specs=[pl.BlockSpec(memory_space=pl.ANY)], out_specs=pl.BlockSpec(memory_space=pl.ANY),
        scratch_shapes=[pltpu.SemaphoreType.DMA((7,)), pltpu.SemaphoreType.DMA((7,)), pltpu.SemaphoreType.DMA(())],
    )(x)


def _all_to_all(g, name):
    def body(g_ref, out_ref, send_sems, recv_sems, local_sem):
        mx, my, mc = lax.axis_index("x"), lax.axis_index("y"), lax.axis_index("c")
        me = 4 * mx + 2 * my + mc
        mine = pltpu.make_async_copy(g_ref.at[me], out_ref.at[me], local_sem)
        mine.start()
        copies = []
        for k in range(1, N_DEV):
            px = 1 - mx if (k >> 2) & 1 else mx
            py = 1 - my if (k >> 1) & 1 else my
            pc = 1 - mc if k & 1 else mc
            cp = pltpu.make_async_remote_copy(
                src_ref=g_ref.at[4 * px + 2 * py + pc], dst_ref=out_ref.at[me],
                send_sem=send_sems.at[k - 1], recv_sem=recv_sems.at[k - 1],
                device_id=(px, py, pc), device_id_type=MESH)
            cp.start()
            copies.append(cp)
        for cp in copies:
            cp.wait()
        mine.wait()

    return _pcall(
        body, name=name, out_shape=_sds(g.shape, g.dtype),
        in_specs=[pl.BlockSpec(memory_space=pl.ANY)], out_specs=pl.BlockSpec(memory_space=pl.ANY),
        scratch_shapes=[pltpu.SemaphoreType.DMA((7,)), pltpu.SemaphoreType.DMA((7,)), pltpu.SemaphoreType.DMA(())],
    )(g)


def _mm_nn(a, w, out_dtype, name, square_lhs=False, relu_out=False):
    m, k = a.shape
    nb, _, bn = w.shape

    def body(a_ref, w_ref, o_ref):
        av = a_ref[...]
        if square_lhs:
            av = av.astype(F32)
            av = (av * av).astype(BF16)
        for j in range(nb):
            r = jnp.dot(av, w_ref[j], preferred_element_type=F32)
            if relu_out:
                r = jnp.maximum(r, 0.0)
            o_ref[:, j * bn:(j + 1) * bn] = r.astype(out_dtype)

    return _pcall(
        body, name=name, grid=(m // TM,),
        in_specs=[pl.BlockSpec((TM, k), lambda i: (i, 0)), pl.BlockSpec((nb, k, bn), lambda i: (0, 0, 0))],
        out_specs=pl.BlockSpec((TM, nb * bn), lambda i: (i, 0)),
        out_shape=_sds((m, nb * bn), out_dtype), compiler_params=_cp(),
    )(a, w)


def _mm_nt(dy, w, out_dtype, name, relu_mul=None):
    m = dy.shape[0]
    nb, k, bn = w.shape

    def body(*refs):
        if relu_mul is None:
            dy_ref, w_ref, o_ref = refs
        else:
            dy_ref, w_ref, r_ref, o_ref = refs
        acc = _dot_nt(dy_ref[:, 0:bn], w_ref[0])
        for j in range(1, nb):
            acc = acc + _dot_nt(dy_ref[:, j * bn:(j + 1) * bn], w_ref[j])
        if relu_mul is not None:
            acc = acc * (2.0 * r_ref[...].astype(F32))
        o_ref[...] = acc.astype(out_dtype)

    in_specs = [pl.BlockSpec((TM, nb * bn), lambda i: (i, 0)), pl.BlockSpec((nb, k, bn), lambda i: (0, 0, 0))]
    args = [dy, w]
    if relu_mul is not None:
        in_specs.append(pl.BlockSpec((TM, k), lambda i: (i, 0)))
        args.append(relu_mul)
    return _pcall(
        body, name=name, grid=(m // TM,), in_specs=in_specs,
        out_specs=pl.BlockSpec((TM, k), lambda i: (i, 0)),
        out_shape=_sds((m, k), out_dtype), compiler_params=_cp(),
    )(*args)


def _mm_tn(a, b, name, col_blocks, block, out_dtype, tm, square_lhs=False):
    m, k = a.shape
    nn = b.shape[1]
    steps = m // tm
    if col_blocks:
        nblk, acc_shape = nn // block, (k, block)
        a_spec = pl.BlockSpec((tm, k), lambda j, s: (s, 0))
        b_spec = pl.BlockSpec((tm, block), lambda j, s: (s, j))
    else:
        nblk, acc_shape = k // block, (block, nn)
        a_spec = pl.BlockSpec((tm, block), lambda j, s: (s, j))
        b_spec = pl.BlockSpec((tm, nn), lambda j, s: (s, 0))

    def body(a_ref, b_ref, o_ref, acc):
        s = pl.program_id(1)

        @pl.when(s == 0)
        def _():
            acc[...] = jnp.zeros_like(acc)

        av = a_ref[...]
        if square_lhs:
            av = av.astype(F32)
            av = (av * av).astype(BF16)
        acc[...] += _dot_tn(av, b_ref[...])

        @pl.when(s == steps - 1)
        def _():
            o_ref[...] = acc[...].astype(out_dtype)

    return _pcall(
        body, name=name, grid=(nblk, steps), in_specs=[a_spec, b_spec],
        out_specs=pl.BlockSpec((None,) + acc_shape, lambda j, s: (j, 0, 0)),
        out_shape=_sds((nblk,) + acc_shape, out_dtype),
        scratch_shapes=[pltpu.VMEM(acc_shape, F32)], compiler_params=_cp(),
    )(a, b)


def _mod_part(s16, w_ada, b_cols):
    def body(s_ref, w_ref, b_ref, o_ref):
        o_ref[...] = _dot(s_ref[...], w_ref[...]) + b_ref[...]

    return _pcall(body, name="mod_part", out_shape=_sds((s16.shape[0], w_ada.shape[1]), F32),
                  compiler_params=_cp())(s16, w_ada, b_cols)


def _ada_bwd(s16, dm_cols, w_ada):
    def body(s_ref, d_ref, w_ref, gw_ref, ds_ref):
        gw_ref[...] = _dot_tn(s_ref[...], d_ref[...])
        ds_ref[...] = _dot_nt(d_ref[...], w_ref[...])

    return _pcall(body, name="ada_bwd",
                  out_shape=(_sds(w_ada.shape, F32), _sds(s16.shape, F32)), compiler_params=_cp())(s16, dm_cols, w_ada)


def _norm1_fwd(ctx, x, g, modrows, cb):
    l_len, d = ctx.shape
    nb = (l_len + x.shape[0]) // TM

    def body(ctx_ref, x_ref, g_ref, m_ref, o_ref):
        is_ctx = pl.program_id(0) < cb
        xin = jnp.where(is_ctx, ctx_ref[...], x_ref[...])
        sh = jnp.where(is_ctx, m_ref[R_CSH1:R_CSH1 + 1, :], m_ref[R_SH1:R_SH1 + 1, :])
        sc = jnp.where(is_ctx, m_ref[R_CSC1:R_CSC1 + 1, :], m_ref[R_SC1:R_SC1 + 1, :])
        ms = jnp.mean(xin * xin, axis=-1, keepdims=True)
        n = xin * lax.rsqrt(ms + EPS) * g_ref[...]
        o_ref[...] = (n * (1.0 + sc) + sh).astype(BF16)

    return _pcall(
        body, name="norm1_fwd", grid=(nb,),
        in_specs=[pl.BlockSpec((TM, d), lambda i: (jnp.minimum(i, cb - 1), 0)),
                  pl.BlockSpec((TM, d), lambda i: (jnp.maximum(i - cb, 0), 0)),
                  pl.BlockSpec((1, d), lambda i: (0, 0)), pl.BlockSpec((8, d), lambda i: (0, 0))],
        out_specs=pl.BlockSpec((TM, d), lambda i: (i, 0)),
        out_shape=_sds((nb * TM, d), BF16), compiler_params=_cp(),
    )(ctx, x, g, modrows)


def _mix_fwd(o_f, o_b, p, h_f, h_b, cb, t_len):
    def body(of_ref, ob_ref, g_ref, gate_ref, hf_ref, hb_ref, mix_ref):
        o = of_ref[...] + ob_ref[...]
        g = g_ref[...]
        sg = g * _sigmoid(g)
        for hh in range(HEADS):
            sl = slice(hh * HEAD_DIM, (hh + 1) * HEAD_DIM)
            oh = o[:, sl]
            yc = oh - jnp.mean(oh, axis=-1, keepdims=True)
            var = jnp.mean(yc * yc, axis=-1, keepdims=True)
            mix_ref[:, sl] = (sg[:, sl] * (yc * lax.rsqrt(var + EPS))).astype(BF16)
        mix_ref[:, RET_W:] = ((hf_ref[...] + hb_ref[...]) * _gelu(gate_ref[...])).astype(BF16)

    row = lambda i: (i + cb, 0)
    return _pcall(
        body, name="mix_fwd", grid=(t_len // TM,),
        in_specs=[pl.BlockSpec((TM, RET_W), row), pl.BlockSpec((TM, RET_W), row),
                  pl.BlockSpec((TM, RET_W), lambda i: (i + cb, 3)), pl.BlockSpec((TM, LRU_W), lambda i: (i + cb, 5)),
                  pl.BlockSpec((TM, LRU_W), row), pl.BlockSpec((TM, LRU_W), row)],
        out_specs=pl.BlockSpec((TM, RET_W + LRU_W), lambda i: (i, 0)),
        out_shape=_sds((t_len, RET_W + LRU_W), BF16), compiler_params=_cp(),
    )(o_f, o_b, p, p, h_f, h_b)


def _res_norm2(x, y, g, modrows):
    t_len, d = x.shape

    def body(x_ref, y_ref, g_ref, m_ref, x1_ref, h2_ref):
        x1 = x_ref[...] + m_ref[R_G1:R_G1 + 1, :] * y_ref[...]
        ms = jnp.mean(x1 * x1, axis=-1, keepdims=True)
        n = x1 * lax.rsqrt(ms + EPS) * g_ref[...]
        x1_ref[...] = x1
        h2_ref[...] = (n * (1.0 + m_ref[R_SC2:R_SC2 + 1, :]) + m_ref[R_SH2:R_SH2 + 1, :]).astype(BF16)

    t = pl.BlockSpec((TM, d), lambda i: (i, 0))
    return _pcall(
        body, name="res_norm2", grid=(t_len // TM,),
        in_specs=[t, t, pl.BlockSpec((1, d), lambda i: (0, 0)), pl.BlockSpec((8, d), lambda i: (0, 0))],
        out_specs=(t, t), out_shape=(_sds((t_len, d), F32), _sds((t_len, d), BF16)), compiler_params=_cp(),
    )(x, y, g, modrows)


def _final_fwd_bwd(x1, z, target, fg, modrows):
    t_len, d = x1.shape

    def body(x1_ref, z_ref, t_ref, fg_ref, m_ref, dx2_ref, dz_ref, acc_ref):
        @pl.when(pl.program_id(0) == 0)
        def _():
            acc_ref[...] = jnp.zeros_like(acc_ref)

        g2 = m_ref[R_G2:R_G2 + 1, :]
        z = z_ref[...]
        x2 = x1_ref[...] + g2 * z
        rstd = lax.rsqrt(jnp.mean(x2 * x2, axis=-1, keepdims=True) + EPS)
        xh = x2 * rstd
        fg = fg_ref[...]
        e = xh * fg - t_ref[...]
        dy = e * (1.0 / d)
        dxh = dy * fg
        dx2 = rstd * (dxh - xh * jnp.mean(dxh * xh, axis=-1, keepdims=True))
        dx2_ref[...] = dx2
        dz_ref[...] = (g2 * dx2).astype(BF16)
        acc_ref[0:1, :] += jnp.sum(dy * xh, axis=0, keepdims=True)
        acc_ref[1:2, :] += jnp.sum(dx2 * z, axis=0, keepdims=True)
        acc_ref[2:3, :] += jnp.sum(e * e, axis=0, keepdims=True)

    t = pl.BlockSpec((TM, d), lambda i: (i, 0))
    return _pcall(
        body, name="final_fwd_bwd", grid=(t_len // TM,),
        in_specs=[t, t, t, pl.BlockSpec((1, d), lambda i: (0, 0)), pl.BlockSpec((8, d), lambda i: (0, 0))],
        out_specs=(t, t, pl.BlockSpec((8, d), lambda i: (0, 0))),
        out_shape=(_sds((t_len, d), F32), _sds((t_len, d), BF16), _sds((8, d), F32)), compiler_params=_cp(),
    )(x1, z, target, fg, modrows)


def _bwd_norm2(dh2, x1, dx2, y, g, modrows):
    t_len, d = x1.shape

    def body(dh2_ref, x1_ref, dx2_ref, y_ref, g_ref, m_ref, dx1_ref, dy_ref, acc_ref):
        @pl.when(pl.program_id(0) == 0)
        def _():
            acc_ref[...] = jnp.zeros_like(acc_ref)

        x1 = x1_ref[...]
        rstd = lax.rsqrt(jnp.mean(x1 * x1, axis=-1, keepdims=True) + EPS)
        xh = x1 * rstd
        gn = g_ref[...]
        dh2 = dh2_ref[...]
        dn = dh2 * (1.0 + m_ref[R_SC2:R_SC2 + 1, :])
        dxh = dn * gn
        dx1 = dx2_ref[...] + rstd * (dxh - xh * jnp.mean(dxh * xh, axis=-1, keepdims=True))
        dx1_ref[...] = dx1
        dy_ref[...] = (m_ref[R_G1:R_G1 + 1, :] * dx1).astype(BF16)
        acc_ref[0:1, :] += jnp.sum(dh2, axis=0, keepdims=True)
        acc_ref[1:2, :] += jnp.sum(dh2 * xh * gn, axis=0, keepdims=True)
        acc_ref[2:3, :] += jnp.sum(dn * xh, axis=0, keepdims=True)
        acc_ref[3:4, :] += jnp.sum(dx1 * y_ref[...], axis=0, keepdims=True)

    t = pl.BlockSpec((TM, d), lambda i: (i, 0))
    return _pcall(
        body, name="bwd_norm2", grid=(t_len // TM,),
        in_specs=[t, t, t, t, pl.BlockSpec((1, d), lambda i: (0, 0)), pl.BlockSpec((8, d), lambda i: (0, 0))],
        out_specs=(t, t, pl.BlockSpec((8, d), lambda i: (0, 0))),
        out_shape=(_sds((t_len, d), F32), _sds((t_len, d), BF16), _sds((8, d), F32)), compiler_params=_cp(),
    )(dh2, x1, dx2, y, g, modrows)


def _mix_bwd(dmix, o_f, o_b, p, h_f, h_b, cb):
    n = o_f.shape[0]

    def body(dm_ref, of_ref, ob_ref, g_ref, gate_ref, hf_ref, hb_ref, do_ref, dg_ref, dgate_ref, dhs_ref):
        is_ctx = pl.program_id(0) < cb

        @pl.when(is_ctx)
        def _():
            for r in (do_ref, dg_ref, dgate_ref, dhs_ref):
                r[...] = jnp.zeros_like(r)

        @pl.when(jnp.logical_not(is_ctx))
        def _():
            o = of_ref[...] + ob_ref[...]
            g = g_ref[...]
            s = _sigmoid(g)
            sg = g * s
            dsg = s * (1.0 + g * (1.0 - s))
            for hh in range(HEADS):
                sl = slice(hh * HEAD_DIM, (hh + 1) * HEAD_DIM)
                oh = o[:, sl]
                yc = oh - jnp.mean(oh, axis=-1, keepdims=True)
                rs = lax.rsqrt(jnp.mean(yc * yc, axis=-1, keepdims=True) + EPS)
                gn = yc * rs
                dret = dm_ref[:, sl]
                dgn = dret * sg[:, sl]
                dg_ref[:, sl] = dret * gn * dsg[:, sl]
                do_ref[:, sl] = rs * (dgn - jnp.mean(dgn, axis=-1, keepdims=True)
                                      - gn * jnp.mean(dgn * gn, axis=-1, keepdims=True))
            dlru = dm_ref[:, RET_W:]
            gate = gate_ref[...]
            dhs_ref[...] = dlru * _gelu(gate)
            dgate_ref[...] = dlru * (hf_ref[...] + hb_ref[...]) * _dgelu(gate)

    row = lambda i: (i, 0)
    t = pl.BlockSpec((TM, RET_W), row)
    return _pcall(
        body, name="mix_bwd", grid=(n // TM,),
        in_specs=[pl.BlockSpec((TM, RET_W + LRU_W), lambda i: (jnp.maximum(i - cb, 0), 0)), t, t,
                  pl.BlockSpec((TM, RET_W), lambda i: (i, 3)), pl.BlockSpec((TM, LRU_W), lambda i: (i, 5)), t, t],
        out_specs=(t, t, t, t), out_shape=tuple(_sds((n, RET_W), F32) for _ in range(4)), compiler_params=_cp(),
    )(dmix, o_f, o_b, p, p, h_f, h_b)


def _bwd_norm1(dhn, ctx, x, dx1, g, modrows, cb):
    t_len, d = x.shape
    nb = dhn.shape[0] // TM

    def body(dh_ref, ctx_ref, x_ref, dx1_ref, g_ref, m_ref, gx_ref, acc_ref):
        is_ctx = pl.program_id(0) < cb

        @pl.when(pl.program_id(0) == 0)
        def _():
            acc_ref[...] = jnp.zeros_like(acc_ref)

        xin = jnp.where(is_ctx, ctx_ref[...], x_ref[...])
        sc = jnp.where(is_ctx, m_ref[R_CSC1:R_CSC1 + 1, :], m_ref[R_SC1:R_SC1 + 1, :])
        rstd = lax.rsqrt(jnp.mean(xin * xin, axis=-1, keepdims=True) + EPS)
        xh = xin * rstd
        gn = g_ref[...]
        dh = dh_ref[...]
        dn = dh * (1.0 + sc)
        dxh = dn * gn
        gx_ref[...] = dx1_ref[...] + rstd * (dxh - xh * jnp.mean(dxh * xh, axis=-1, keepdims=True))
        s0 = jnp.sum(dh, axis=0, keepdims=True)
        s1 = jnp.sum(dh * xh * gn, axis=0, keepdims=True)
        acc_ref[4:5, :] += jnp.sum(dn * xh, axis=0, keepdims=True)

        @pl.when(is_ctx)
        def _():
            acc_ref[0:1, :] += s0
            acc_ref[1:2, :] += s1

        @pl.when(jnp.logical_not(is_ctx))
        def _():
            acc_ref[2:3, :] += s0
            acc_ref[3:4, :] += s1

    lat = pl.BlockSpec((TM, d), lambda i: (jnp.maximum(i - cb, 0), 0))
    return _pcall(
        body, name="bwd_norm1", grid=(nb,),
        in_specs=[pl.BlockSpec((TM, d), lambda i: (i, 0)), pl.BlockSpec((TM, d), lambda i: (jnp.minimum(i, cb - 1), 0)),
                  lat, lat, pl.BlockSpec((1, d), lambda i: (0, 0)), pl.BlockSpec((8, d), lambda i: (0, 0))],
        out_specs=(lat, pl.BlockSpec((8, d), lambda i: (0, 0))),
        out_shape=(_sds((t_len, d), F32), _sds((8, d), F32)), compiler_params=_cp(),
    )(dhn, ctx, x, dx1, g, modrows)


def _rot(x, cf, ss):
    return x * cf + pltpu.roll(x, HEAD_DIM // 2, 1) * ss


def _decay(lgv, dirn):
    ii = lax.broadcasted_iota(jnp.int32, (CHUNK, CHUNK), 0)
    jj = lax.broadcasted_iota(jnp.int32, (CHUNK, CHUNK), 1)
    rel = ii - jj if dirn == 0 else jj - ii
    relf = jnp.maximum(rel, 0).astype(F32)
    dm = jnp.where(rel >= 0, jnp.exp(lgv * relf), 0.0)
    pos = ii.astype(F32)
    if dirn == 0:
        cq, cs = pos + 1.0, (CHUNK - 1.0) - pos
    else:
        cq, cs = CHUNK - pos, pos
    return dm, relf, jnp.exp(lgv * cq), jnp.exp(lgv * cs), cq, cs


def _ret_specs(dirn, cc, nc, step_of):
    row = lambda h, s: _tile_order(dirn, step_of(s), cc, nc)
    qkv = [pl.BlockSpec((CHUNK, HEAD_DIM), lambda h, s, o=o: (row(h, s), o + h)) for o in (0, HEADS, 2 * HEADS)]
    tab = [pl.BlockSpec((CHUNK, HEAD_DIM), lambda h, s: (row(h, s), 0))] * 2
    per_head = pl.BlockSpec((CHUNK, HEAD_DIM), lambda h, s: (row(h, s), h))
    state = pl.BlockSpec((None, None, CHUNK, HEAD_DIM), lambda h, s: (h, step_of(s), 0, 0))
    return qkv, tab, per_head, state


def _ret_fwd(p, cosf, sins, lg, dirn, cc):
    n = p.shape[0]
    nc = n // CHUNK
    qkv, tab, per_head, state = _ret_specs(dirn, cc, nc, lambda s: s)

    def body(lg_ref, q_ref, k_ref, v_ref, c_ref, s_ref, o_ref, sp_ref, st):
        h = pl.program_id(0)

        @pl.when(pl.program_id(1) == 0)
        def _():
            st[...] = jnp.zeros_like(st)

        lgv = lg_ref[dirn, h]
        cf, ss = c_ref[...], s_ref[...]
        q = _rot(q_ref[...], cf, ss)
        k = _rot(k_ref[...], cf, ss) * K_SCALE
        v = v_ref[...]
        dm, _, wq, ws, _, _ = _decay(lgv, dirn)
        sp = st[...]
        sp_ref[...] = sp
        o_ref[...] = _dot(_dot_nt(q, k) * dm, v) + _dot(q * wq, sp)
        st[...] = jnp.exp(lgv * CHUNK) * sp + _dot_tn(k * ws, v)

    return _pcall(
        body, name=f"ret_fwd{dirn}", grid=(HEADS, nc),
        in_specs=[pl.BlockSpec(memory_space=pltpu.SMEM)] + qkv + tab,
        out_specs=(per_head, state),
        out_shape=(_sds((n, RET_W), F32), _sds((HEADS, nc, CHUNK, HEAD_DIM), F32)),
        scratch_shapes=[pltpu.VMEM((CHUNK, HEAD_DIM), F32)], compiler_params=_cp(),
    )(lg, p, p, p, cosf, sins)


def _ret_bwd(p, cosf, sins, lg, do, s_prev, dirn, cc):
    n = p.shape[0]
    nc = n // CHUNK
    qkv, tab, per_head, state = _ret_specs(dirn, cc, nc, lambda s: nc - 1 - s)

    def body(lg_ref, q_ref, k_ref, v_ref, c_ref, s_ref, do_ref, sp_ref, dq_ref, dk_ref, dv_ref, dlg_ref, dst):
        h = pl.program_id(0)

        @pl.when(pl.program_id(1) == 0)
        def _():
            dst[...] = jnp.zeros_like(dst)
            dlg_ref[...] = jnp.zeros_like(dlg_ref)

        lgv = lg_ref[dirn, h]
        cf, ss = c_ref[...], s_ref[...]
        q = _rot(q_ref[...], cf, ss)
        k = _rot(k_ref[...], cf, ss) * K_SCALE
        v = v_ref[...]
        dov = do_ref[...]
        dm, relf, wq, ws, cq, cs = _decay(lgv, dirn)
        a = _dot_nt(q, k)
        sc = a * dm
        sp = sp_ref[...]
        dsn = dst[...]
        gc = jnp.exp(lgv * CHUNK)
        g1 = _dot_nt(dov, sp)
        da = _dot_nt(dov, v) * dm
        dq = g1 * wq + _dot(da, k)
        h1 = _dot_nt(v, dsn)
        dk = _dot_tn(da, q) + h1 * ws
        dv_ref[...] = _dot_tn(sc, dov) + _dot(k * ws, dsn)
        dst[...] = gc * dsn + _dot_tn(q * wq, dov)
        term = da * a * relf + q * g1 * (wq * cq) + k * h1 * (ws * cs) + sp * dsn * (CHUNK * gc)
        dlg_ref[0:1, :] += jnp.sum(term, axis=0, keepdims=True)
        dq_ref[...] = dq * cf + pltpu.roll(dq * ss, HEAD_DIM // 2, 1)
        dk_ref[...] = (dk * cf + pltpu.roll(dk * ss, HEAD_DIM // 2, 1)) * K_SCALE

    return _pcall(
        body, name=f"ret_bwd{dirn}", grid=(HEADS, nc),
        in_specs=[pl.BlockSpec(memory_space=pltpu.SMEM)] + qkv + tab + [per_head, state],
        out_specs=(per_head, per_head, per_head, pl.BlockSpec((None, 8, HEAD_DIM), lambda h, s: (h, 0, 0))),
        out_shape=(_sds((n, RET_W), F32),) * 3 + (_sds((HEADS, 8, HEAD_DIM), F32),),
        scratch_shapes=[pltpu.VMEM((CHUNK, HEAD_DIM), F32)], compiler_params=_cp(),
    )(lg, p, p, p, cosf, sins, do, s_prev)


def _shift_rows(cur, prev8, next8, k, seg_start, seg_end):
    tm = cur.shape[0]
    rows = _rows_iota(cur.shape)
    if k < 0:
        out = pltpu.roll(cur, -k, 0)
        for j in range(-k):
            halo = jnp.where(seg_start, 0.0, prev8[SUB + k + j:SUB + k + j + 1, :])
            out = jnp.where(rows == j, halo, out)
    else:
        out = pltpu.roll(cur, tm - k, 0)
        for j in range(k):
            halo = jnp.where(seg_end, 0.0, next8[j:j + 1, :])
            out = jnp.where(rows == tm - k + j, halo, out)
    return out


def _seg_flags(t, cb, nb):
    return jnp.logical_or(t == 0, t == cb), jnp.logical_or(t == cb - 1, t == nb - 1)


def _halo_specs(tile_of, n_rows, col):
    per = TM // SUB
    return [pl.BlockSpec((TM, LRU_W), lambda s: (tile_of(s), col)),
            pl.BlockSpec((SUB, LRU_W), lambda s: (jnp.maximum(tile_of(s) * per - 1, 0), col)),
            pl.BlockSpec((SUB, LRU_W), lambda s: (jnp.minimum((tile_of(s) + 1) * per, n_rows // SUB - 1), col))]


def _lru_gates(xr, prev8, next8, seg_start, seg_end, cw_ref, cb_ref, wg_ref, bg_ref, sp_ref):
    xm1 = _shift_rows(xr, prev8, next8, -1, seg_start, seg_end)
    xp1 = _shift_rows(xr, prev8, next8, 1, seg_start, seg_end)
    xp2 = _shift_rows(xr, prev8, next8, 2, seg_start, seg_end)
    xc = cb_ref[...] + xm1 * cw_ref[0:1, :] + xr * cw_ref[1:2, :] + xp1 * cw_ref[2:3, :] + xp2 * cw_ref[3:4, :]
    pre = _dot(xc, wg_ref[...]) + bg_ref[...]
    r = _sigmoid(pre[:, :LRU_W])
    i = _sigmoid(pre[:, LRU_W:])
    la = (-LRU_C) * r * sp_ref[...]
    a = jnp.exp(la)
    sq = jnp.sqrt(_one_minus_exp(2.0 * la))
    return xc, r, i, a, sq


def _scan_tile(a, b, ascending, a_sc, b_sc, carry, out_ref):
    tm = a.shape[0]
    r8 = _rows_iota(a.shape) % SUB
    for k in (1, 2, 4):
        if ascending:
            m = r8 >= k
            a_s, b_s = pltpu.roll(a, k, 0), pltpu.roll(b, k, 0)
        else:
            m = r8 < SUB - k
            a_s, b_s = pltpu.roll(a, tm - k, 0), pltpu.roll(b, tm - k, 0)
        b = a * jnp.where(m, b_s, 0.0) + b
        a = a * jnp.where(m, a_s, 1.0)
    a_sc[...] = a
    b_sc[...] = b
    nsub = tm // SUB

    def step(j, c):
        off = pl.multiple_of((j if ascending else nsub - 1 - j) * SUB, SUB)
        hb = a_sc[pl.ds(off, SUB), :] * c + b_sc[pl.ds(off, SUB), :]
        out_ref[pl.ds(off, SUB), :] = hb
        last = hb[SUB - 1:SUB, :] if ascending else hb[0:1, :]
        return jnp.broadcast_to(last, c.shape)

    carry[...] = lax.fori_loop(0, nsub, step, carry[...])


def _lru_fwd(p, wg, bg, sp, cw, cbias, dirn, cb):
    n = p.shape[0]
    nb = n // TM
    tile_of = lambda s: _tile_order(dirn, s, cb, nb)

    def body(x_ref, xp_ref, xn_ref, wg_ref, bg_ref, sp_ref, cw_ref, cb_ref, h_ref, cin_ref, carry, a_sc, b_sc):
        s = pl.program_id(0)

        @pl.when(s == 0)
        def _():
            carry[...] = jnp.zeros_like(carry)

        seg_start, seg_end = _seg_flags(tile_of(s), cb, nb)
        xc, r, i, a, sq = _lru_gates(x_ref[...], xp_ref[...], xn_ref[...], seg_start, seg_end,
                                     cw_ref, cb_ref, wg_ref, bg_ref, sp_ref)
        cin_ref[...] = carry[...]
        _scan_tile(a, sq * (i * xc), dirn == 0, a_sc, b_sc, carry, h_ref)

    full = lambda shape: pl.BlockSpec(shape, lambda s: (0,) * len(shape))
    return _pcall(
        body, name=f"lru_fwd{dirn}", grid=(nb,),
        in_specs=_halo_specs(tile_of, n, 4) + [full((LRU_W, 2 * LRU_W)), full((1, 2 * LRU_W)), full((1, LRU_W)),
                                               full((4, LRU_W)), full((1, LRU_W))],
        out_specs=(pl.BlockSpec((TM, LRU_W), lambda s: (tile_of(s), 0)),
                   pl.BlockSpec((None, SUB, LRU_W), lambda s: (tile_of(s), 0, 0))),
        out_shape=(_sds((n, LRU_W), F32), _sds((nb, SUB, LRU_W), F32)),
        scratch_shapes=[pltpu.VMEM((SUB, LRU_W), F32), pltpu.VMEM((TM, LRU_W), F32), pltpu.VMEM((TM, LRU_W), F32)],
        compiler_params=_cp(),
    )(p, p, p, wg, bg, sp, cw, cbias)


def _lru_bwd(p, wg, bg, sp, cw, cbias, h, cin, dhs, dirn, cb):
    n = p.shape[0]
    nb = n // TM
    tile_of = lambda s: _tile_order(dirn, nb - 1 - s, cb, nb)

    def body(x_ref, xp_ref, xn_ref, wg_ref, bg_ref, sp_ref, cw_ref, cb_ref, h_ref, cin_ref, dhs_ref,
             dxc_ref, dwg_ref, acc_ref, carry, a_sc, b_sc, mu_sc):
        s = pl.program_id(0)

        @pl.when(s == 0)
        def _():
            carry[...] = jnp.zeros_like(carry)
            dwg_ref[...] = jnp.zeros_like(dwg_ref)
            acc_ref[...] = jnp.zeros_like(acc_ref)

        seg_start, seg_end = _seg_flags(tile_of(s), cb, nb)
        xc, r, i, a, sq = _lru_gates(x_ref[...], xp_ref[...], xn_ref[...], seg_start, seg_end,
                                     cw_ref, cb_ref, wg_ref, bg_ref, sp_ref)
        rows = _rows_iota(a.shape)
        hv = h_ref[...]
        dh = dhs_ref[...]
        mu_next = carry[0:1, :]
        _scan_tile(a, a * dh, dirn == 1, a_sc, b_sc, carry, mu_sc)
        mu = mu_sc[...]
        if dirn == 0:
            hprev = jnp.where(rows == 0, cin_ref[0:1, :], pltpu.roll(hv, 1, 0))
            lam = dh + jnp.where(rows == TM - 1, mu_next, pltpu.roll(mu, TM - 1, 0))
        else:
            hprev = jnp.where(rows == TM - 1, cin_ref[0:1, :], pltpu.roll(hv, TM - 1, 0))
            lam = dh + jnp.where(rows == 0, mu_next, pltpu.roll(mu, 1, 0))
        ds = lam * (i * xc)
        di = lam * (sq * xc)
        dla = lam * hprev * a - ds * (a * a) / jnp.maximum(sq, 1e-20)
        dpr = dla * ((-LRU_C) * sp_ref[...]) * r * (1.0 - r)
        dpi = di * i * (1.0 - i)
        dpre = jnp.concatenate([dpr, dpi], axis=1)
        dxc_ref[...] = lam * (sq * i) + _dot_nt(dpre, wg_ref[...])
        dwg_ref[...] += _dot_tn(xc, dpre)
        acc_ref[0:1, :] += jnp.sum(dpre, axis=0, keepdims=True)
        acc_ref[1:2, 0:LRU_W] += jnp.sum(dla * ((-LRU_C) * r), axis=0, keepdims=True)

    full = lambda shape: pl.BlockSpec(shape, lambda s: (0,) * len(shape))
    tile = pl.BlockSpec((TM, LRU_W), lambda s: (tile_of(s), 0))
    return _pcall(
        body, name=f"lru_bwd{dirn}", grid=(nb,),
        in_specs=_halo_specs(tile_of, n, 4) + [full((LRU_W, 2 * LRU_W)), full((1, 2 * LRU_W)), full((1, LRU_W)),
                                               full((4, LRU_W)), full((1, LRU_W)), tile,
                                               pl.BlockSpec((None, SUB, LRU_W), lambda s: (tile_of(s), 0, 0)), tile],
        out_specs=(tile, full((LRU_W, 2 * LRU_W)), full((8, 2 * LRU_W))),
        out_shape=(_sds((n, LRU_W), F32), _sds((LRU_W, 2 * LRU_W), F32), _sds((8, 2 * LRU_W), F32)),
        scratch_shapes=[pltpu.VMEM((SUB, LRU_W), F32)] + [pltpu.VMEM((TM, LRU_W), F32)] * 3,
        compiler_params=_cp(),
    )(p, p, p, wg, bg, sp, cw, cbias, h, cin, dhs)


def _assemble_dp(dqs, dks, dvs, dg, dgate, dxcs, p, cw, cb):
    n = p.shape[0]
    nb = n // TM
    tile_of = lambda s: s

    def body(dqf, dqb, dkf, dkb, dvf, dvb, dg_ref, dgate_ref, cf, pf, nf, cb_, pb, nb_, x_ref, xp_ref, xn_ref,
             cw_ref, dp_ref, acc_ref):
        s = pl.program_id(0)

        @pl.when(s == 0)
        def _():
            acc_ref[...] = jnp.zeros_like(acc_ref)

        seg_start, seg_end = _seg_flags(s, cb, nb)
        dp_ref[:, 0:RET_W] = (dqf[...] + dqb[...]).astype(BF16)
        dp_ref[:, RET_W:2 * RET_W] = (dkf[...] + dkb[...]).astype(BF16)
        dp_ref[:, 2 * RET_W:3 * RET_W] = (dvf[...] + dvb[...]).astype(BF16)
        dp_ref[:, 3 * RET_W:4 * RET_W] = dg_ref[...].astype(BF16)
        dxc = cf[...] + cb_[...]
        dprev = pf[...] + pb[...]
        dnext = nf[...] + nb_[...]
        dxr = (_shift_rows(dxc, dprev, dnext, 1, seg_start, seg_end) * cw_ref[0:1, :] + dxc * cw_ref[1:2, :]
               + _shift_rows(dxc, dprev, dnext, -1, seg_start, seg_end) * cw_ref[2:3, :]
               + _shift_rows(dxc, dprev, dnext, -2, seg_start, seg_end) * cw_ref[3:4, :])
        dp_ref[:, 4 * RET_W:4 * RET_W + LRU_W] = dxr.astype(BF16)
        dp_ref[:, 4 * RET_W + LRU_W:] = dgate_ref[...].astype(BF16)
        xr, xp, xn = x_ref[...], xp_ref[...], xn_ref[...]
        for j, k in enumerate((-1, 0, 1, 2)):
            xs = xr if k == 0 else _shift_rows(xr, xp, xn, k, seg_start, seg_end)
            acc_ref[j:j + 1, :] += jnp.sum(dxc * xs, axis=0, keepdims=True)
        acc_ref[4:5, :] += jnp.sum(dxc, axis=0, keepdims=True)

    t = pl.BlockSpec((TM, RET_W), lambda s: (s, 0))
    return _pcall(
        body, name="assemble_dp", grid=(nb,),
        in_specs=[t] * 8 + _halo_specs(tile_of, n, 0) * 2 + _halo_specs(tile_of, n, 4)
        + [pl.BlockSpec((4, LRU_W), lambda s: (0, 0))],
        out_specs=(pl.BlockSpec((TM, 4 * RET_W + 2 * LRU_W), lambda s: (s, 0)), pl.BlockSpec((8, LRU_W), lambda s: (0, 0))),
        out_shape=(_sds((n, 4 * RET_W + 2 * LRU_W), BF16), _sds((8, LRU_W), F32)), compiler_params=_cp(),
    )(dqs[0], dqs[1], dks[0], dks[1], dvs[0], dvs[1], dg, dgate, dxcs[0], dxcs[0], dxcs[0], dxcs[1], dxcs[1], dxcs[1],
      p, p, p, cw)


def _sum_adam(parts, w, m, v, name):
    nparts, r, c = parts.shape
    tr = min(r, 128)
    bc1 = 1.0 - ADAM_B1 ** ADAM_STEP
    bc2 = 1.0 - ADAM_B2 ** ADAM_STEP

    def body(p_ref, w_ref, m_ref, v_ref, g_ref, d_ref, nm_ref, nv_ref):
        g = p_ref[0].astype(F32)
        for j in range(1, nparts):
            g = g + p_ref[j].astype(F32)
        nm = ADAM_B1 * m_ref[...] + (1.0 - ADAM_B1) * g
        nv = ADAM_B2 * v_ref[...] + (1.0 - ADAM_B2) * (g * g)
        g_ref[...] = g
        nm_ref[...] = nm
        nv_ref[...] = nv
        d_ref[...] = (-ADAM_LR) * ((nm / bc1) / (jnp.sqrt(nv / bc2) + ADAM_EPS) + ADAM_WD * w_ref[...])

    t = pl.BlockSpec((tr, c), lambda i: (i, 0))
    return _pcall(
        body, name=name, grid=(r // tr,),
        in_specs=[pl.BlockSpec((nparts, tr, c), lambda i: (0, i, 0)), t, t, t],
        out_specs=(t, t, t, t), out_shape=(_sds((r, c), F32),) * 4, compiler_params=_cp(),
    )(parts, w, m, v)


def _sum_parts(parts, name):
    nparts, r, c = parts.shape

    def body(p_ref, o_ref):
        g = p_ref[0]
        for j in range(1, nparts):
            g = g + p_ref[j]
        o_ref[...] = g

    return _pcall(body, name=name, out_shape=_sds((r, c), parts.dtype), compiler_params=_cp())(parts)


def _rot_tables(l_len, t_len):
    rows = t_len // GRID_W
    row = jnp.repeat(jnp.arange(rows, dtype=F32), GRID_W)
    col = jnp.tile(jnp.arange(GRID_W, dtype=F32), rows)
    n_freq = HEAD_DIM // 4
    inv = ROPE_BASE ** (-jnp.arange(n_freq, dtype=F32) / n_freq)
    ang = jnp.concatenate([row[:, None] * inv, col[:, None] * inv], axis=-1)
    cos, sin = jnp.cos(ang), jnp.sin(ang)
    cosf = jnp.concatenate([jnp.ones((l_len, HEAD_DIM), F32), jnp.concatenate([cos, cos], axis=-1)], axis=0)
    sins = jnp.concatenate([jnp.zeros((l_len, HEAD_DIM), F32), jnp.concatenate([-sin, sin], axis=-1)], axis=0)
    return cosf, sins


def _block_diag(w):
    eye = jnp.eye(LRU_BLOCKS, dtype=w.dtype)
    return (w[:, :, None, :] * eye[:, None, :, None]).reshape(LRU_W, LRU_W)


def _diag_blocks(mat):
    m4 = mat.reshape(LRU_BLOCKS, LRU_BD, LRU_BLOCKS, LRU_BD)
    return jnp.stack([m4[b, :, b, :] for b in range(LRU_BLOCKS)])


def _pack(fields):
    flat = [f.reshape(-1).astype(F32) for f in fields]
    offs, o = [], 0
    for f in flat:
        offs.append((o, f.shape[0]))
        o += f.shape[0]
    pad = (-o) % (128 * 128)
    if pad:
        flat.append(jnp.zeros((pad,), F32))
    return jnp.concatenate(flat).reshape(-1, 128), offs


def _unpack(packed, offs, shapes):
    flat = packed.reshape(-1)
    return [flat[o:o + n].reshape(s) for (o, n), s in zip(offs, shapes)]


def _silu(x):
    return x * jax.nn.sigmoid(x)


def kernel(x, c, ctx, c_ctx, w_ada, b_ada, norm1_g, norm2_g, w_in, ret_decay, conv_w, conv_b, lru_wa, lru_ba, lru_wx, lru_bx, lru_lambda, w_out, w_mlp1, w_mlp2, final_g, loss_target, m_c_ctx, m_w_ada, m_b_ada, m_norm1_g, m_norm2_g, m_w_in, m_ret_decay, m_conv_w, m_conv_b, m_lru_wa, m_lru_ba, m_lru_wx, m_lru_bx, m_lru_lambda, m_w_out, m_w_mlp1, m_w_mlp2, m_final_g, v_c_ctx, v_w_ada, v_b_ada, v_norm1_g, v_norm2_g, v_w_in, v_ret_decay, v_conv_w, v_conv_b, v_lru_wa, v_lru_ba, v_lru_wx, v_lru_bx, v_lru_lambda, v_w_out, v_w_mlp1, v_w_mlp2, v_final_g):
    t_len, d = x.shape[1], x.shape[2]
    l_len = ctx.shape[1]
    cb, cc = l_len // TM, l_len // CHUNK
    me = 4 * lax.axis_index("x") + 2 * lax.axis_index("y") + lax.axis_index("c")
    x2d, ctx2d, tgt2d = x[0], ctx[0], loss_target[0]
    ada_cols = w_ada.shape[2]
    wa2d = w_ada[0]

    sc_loc = conv_w.shape[2]
    pack_a = jnp.zeros((8, d), F32)
    pack_a = pack_a.at[0].set(_silu(c[0]))
    pack_a = pack_a.at[1, :4 * sc_loc].set(conv_w[0].reshape(-1))
    pack_a = pack_a.at[2, :2 * sc_loc].set(lru_ba[0].reshape(-1))
    pack_a = pack_a.at[3, :2 * sc_loc].set(lru_bx[0].reshape(-1))
    pack_a = pack_a.at[4, :2 * sc_loc].set(lru_lambda[0].reshape(-1))
    all_a = _all_gather(pack_a, "gather_small_in")
    s16 = jnp.zeros((16, d), F32).at[0:8].set(all_a[:, 0, :]).at[8].set(_silu(c_ctx))

    def unshard(row, k):
        return all_a[:, row, :k * sc_loc].reshape(N_DEV, k, sc_loc).transpose(1, 0, 2).reshape(k, N_DEV * sc_loc)

    conv_w_full = unshard(1, 4)
    ba_full, bx_full, lam_full = unshard(2, 2), unshard(3, 2), unshard(4, 2)

    b_cols = lax.dynamic_slice(b_ada, (0, me * ada_cols), (1, ada_cols))
    mod_parts = _all_gather(_mod_part(s16, wa2d, b_cols), "gather_mod")
    mod_all = mod_parts.transpose(1, 0, 2).reshape(16, N_DEV * ada_cols)
    mod_me = lax.dynamic_slice(mod_all, (me, 0), (1, 6 * d)).reshape(6, d)
    mod_c = mod_all[8].reshape(6, d)
    modrows = jnp.concatenate([mod_c[0:2], mod_me], axis=0)

    win_g = _all_gather(w_in[0].astype(BF16), "gather_w_in")
    wout_g = _all_gather(w_out[0].astype(BF16), "gather_w_out").reshape(1, d, d)
    w1_g = _all_gather(w_mlp1[0].astype(BF16), "gather_w_mlp1")
    w2_g = _all_gather(w_mlp2[0].astype(BF16), "gather_w_mlp2").reshape(1, 4 * d, d)

    lg = jax.nn.log_sigmoid(ret_decay[0])
    sp = jax.nn.softplus(-lam_full)
    wg = [jnp.concatenate([_block_diag(lru_wa[0, dd]), _block_diag(lru_wx[0, dd])], axis=1).astype(BF16) for dd in (0, 1)]
    bg = [jnp.concatenate([ba_full[dd], bx_full[dd]])[None, :] for dd in (0, 1)]
    cosf, sins = _rot_tables(l_len, t_len)

    hn = _norm1_fwd(ctx2d, x2d, norm1_g, modrows, cb)
    p = _mm_nn(hn, win_g, F32, "mm_in")
    o, s_prev, h, cin = [None, None], [None, None], [None, None], [None, None]
    for dd in (0, 1):
        o[dd], s_prev[dd] = _ret_fwd(p, cosf, sins, lg, dd, cc)
        h[dd], cin[dd] = _lru_fwd(p, wg[dd], bg[dd], sp[dd:dd + 1], conv_w_full, conv_b, dd, cb)
    mix = _mix_fwd(o[0], o[1], p, h[0], h[1], cb, t_len)
    y = _mm_nn(mix, wout_g, F32, "mm_out")
    x1, h2 = _res_norm2(x2d, y, norm2_g, modrows)
    r = _mm_nn(h2, w1_g, BF16, "mm_mlp1", relu_out=True)
    z = _mm_nn(r, w2_g, F32, "mm_mlp2", square_lhs=True)
    dx2, dz, facc = _final_fwd_bwd(x1, z, tgt2d, final_g[None, :], modrows)
    loss = lax.psum(0.5 / d * jnp.sum(facc[2]), AXES)

    du = _mm_nt(dz, w2_g, BF16, "mm_da2", relu_mul=r)
    gw2 = _mm_tn(r, dz, "mm_dw2", False, 2 * d, BF16, 512, square_lhs=True).reshape(N_DEV, -1, d)
    gw1 = _mm_tn(h2, du, "mm_dw1", True, w1_g.shape[2], BF16, 512)
    dh2 = _mm_nt(du, w1_g, F32, "mm_dh2")
    dx1, dy, n2acc = _bwd_norm2(dh2, x1, dx2, y, norm2_g, modrows)
    dmix = _mm_nt(dy, wout_g, F32, "mm_dmix")
    gwo = _mm_tn(mix, dy, "mm_dwout", False, d, BF16, 512).reshape(N_DEV, -1, d)
    do, dg, dgate, dhs = _mix_bwd(dmix, o[0], o[1], p, h[0], h[1], cb)
    dqs, dks, dvs, dlgs, dxcs, dwgs, laccs = [], [], [], [], [], [], []
    for dd in (0, 1):
        dq_, dk_, dv_, dlg_ = _ret_bwd(p, cosf, sins, lg, do, s_prev[dd], dd, cc)
        dxc_, dwg_, lacc_ = _lru_bwd(p, wg[dd], bg[dd], sp[dd:dd + 1], conv_w_full, conv_b, h[dd], cin[dd], dhs, dd, cb)
        dqs.append(dq_); dks.append(dk_); dvs.append(dv_); dlgs.append(dlg_)
        dxcs.append(dxc_); dwgs.append(dwg_); laccs.append(lacc_)
    dp, cacc = _assemble_dp(dqs, dks, dvs, dg, dgate, dxcs, p, conv_w_full, cb)
    gwi = _mm_tn(hn, dp, "mm_dwin", True, win_g.shape[2], BF16, TM)
    dhn = _mm_nt(dp, win_g, F32, "mm_dhn")
    grad_x, n1acc = _bwd_norm1(dhn, ctx2d, x2d, dx1, norm1_g, modrows, cb)

    dlg = jnp.stack([jnp.sum(dlgs[dd][:, 0, :], axis=-1) for dd in (0, 1)])
    d_wa = jnp.stack([_diag_blocks(dwgs[dd][:, :LRU_W]) for dd in (0, 1)])
    d_wx = jnp.stack([_diag_blocks(dwgs[dd][:, LRU_W:]) for dd in (0, 1)])
    d_ba = jnp.stack([laccs[dd][0, :LRU_W] for dd in (0, 1)])
    d_bx = jnp.stack([laccs[dd][0, LRU_W:] for dd in (0, 1)])
    d_sp = jnp.stack([laccs[dd][1, :LRU_W] for dd in (0, 1)])
    fields = [n1acc[0:5], n2acc[0:4], facc[0:2], cacc[0:5], dlg, d_wa, d_wx, d_ba, d_bx, d_sp]
    shapes = [f.shape for f in fields]
    pack_b, offs = _pack(fields)
    all_b = _all_gather(pack_b, "gather_small_grads")
    tot = _unpack(_sum_parts(all_b, "sum_small_grads"), offs, shapes)
    t_n1, t_n2, t_f, t_conv, t_dlg, t_wa, t_wx, t_ba, t_bx, t_sp = tot
    per_dev = [_unpack(all_b[j], offs[:3], shapes[:3]) for j in range(N_DEV)]
    dm_rows = [jnp.concatenate([a1[2], a1[3], a2[3], a2[0], a2[1], af[1]]) for a1, a2, af in per_dev]
    dm_c = jnp.concatenate([t_n1[0], t_n1[1], jnp.zeros((4 * d,), F32)])
    dm16 = jnp.zeros((16, 6 * d), F32).at[0:8].set(jnp.stack(dm_rows)).at[8].set(dm_c)
    g_b_ada = jnp.sum(dm16, axis=0)[None, :]
    dm_cols = lax.dynamic_slice(dm16, (0, me * ada_cols), (16, ada_cols))
    g_w_ada, ds16 = _ada_bwd(s16, dm_cols, wa2d)
    ds_all = _all_gather(ds16[8:16], "gather_dsilu")
    dsilu_cc = _sum_parts(ds_all, "sum_dsilu")[0]
    sg_cc = jax.nn.sigmoid(c_ctx)
    g_c_ctx = dsilu_cc * (sg_cc * (1.0 + c_ctx * (1.0 - sg_cc)))

    g_ret_decay = (t_dlg * jax.nn.sigmoid(-ret_decay[0]))[None]
    g_lambda_full = -t_sp * jax.nn.sigmoid(-lam_full)

    def my_cols(full):
        return lax.dynamic_slice(full, (0, me * sc_loc), (full.shape[0], sc_loc))[None]

    small_g = dict(
        c_ctx=g_c_ctx, b_ada=g_b_ada, norm1_g=t_n1[4][None], norm2_g=t_n2[2][None], ret_decay=g_ret_decay,
        conv_w=my_cols(t_conv[0:4]), conv_b=t_conv[4][None], lru_wa=t_wa[None], lru_ba=my_cols(t_ba),
        lru_wx=t_wx[None], lru_bx=my_cols(t_bx), lru_lambda=my_cols(g_lambda_full), final_g=t_f[0])
    small_w = dict(c_ctx=c_ctx, b_ada=b_ada, norm1_g=norm1_g, norm2_g=norm2_g, ret_decay=ret_decay, conv_w=conv_w,
                   conv_b=conv_b, lru_wa=lru_wa, lru_ba=lru_ba, lru_wx=lru_wx, lru_bx=lru_bx, lru_lambda=lru_lambda,
                   final_g=final_g)
    small_m = dict(c_ctx=m_c_ctx, b_ada=m_b_ada, norm1_g=m_norm1_g, norm2_g=m_norm2_g, ret_decay=m_ret_decay,
                   conv_w=m_conv_w, conv_b=m_conv_b, lru_wa=m_lru_wa, lru_ba=m_lru_ba, lru_wx=m_lru_wx,
                   lru_bx=m_lru_bx, lru_lambda=m_lru_lambda, final_g=m_final_g)
    small_v = dict(c_ctx=v_c_ctx, b_ada=v_b_ada, norm1_g=v_norm1_g, norm2_g=v_norm2_g, ret_decay=v_ret_decay,
                   conv_w=v_conv_w, conv_b=v_conv_b, lru_wa=v_lru_wa, lru_ba=v_lru_ba, lru_wx=v_lru_wx,
                   lru_bx=v_lru_bx, lru_lambda=v_lru_lambda, final_g=v_final_g)
    names = list(small_w)
    sshapes = [small_w[k].shape for k in names]
    pg, soffs = _pack([small_g[k].reshape(small_w[k].shape) for k in names])
    pw, _ = _pack([small_w[k] for k in names])
    pm, _ = _pack([small_m[k] for k in names])
    pv, _ = _pack([small_v[k] for k in names])
    sg_, sd_, sm_, sv_ = _sum_adam(pg[None], pw, pm, pv, "adam_small")
    res = {}
    for k, g_, d_, m_, v_ in zip(names, *[_unpack(a, soffs, sshapes) for a in (sg_, sd_, sm_, sv_)]):
        res[k] = (g_, d_, m_, v_)

    def big(parts, w, m, v, name):
        out = _sum_adam(parts, w[0], m[0], v[0], name)
        return tuple(a[None] for a in out)

    res["w_ada"] = big(g_w_ada[None], w_ada, m_w_ada, v_w_ada, "adam_w_ada")
    res["w_in"] = big(_all_to_all(gwi, "a2a_w_in"), w_in, m_w_in, v_w_in, "adam_w_in")
    res["w_out"] = big(_all_to_all(gwo, "a2a_w_out"), w_out, m_w_out, v_w_out, "adam_w_out")
    res["w_mlp1"] = big(_all_to_all(gw1, "a2a_w_mlp1"), w_mlp1, m_w_mlp1, v_w_mlp1, "adam_w_mlp1")
    res["w_mlp2"] = big(_all_to_all(gw2, "a2a_w_mlp2"), w_mlp2, m_w_mlp2, v_w_mlp2, "adam_w_mlp2")

    order = ["c_ctx", "w_ada", "b_ada", "norm1_g", "norm2_g", "w_in", "ret_decay", "conv_w", "conv_b", "lru_wa", "lru_ba",
             "lru_wx", "lru_bx", "lru_lambda", "w_out", "w_mlp1", "w_mlp2", "final_g"]
    outs = [loss, grad_x[None]]
    for j in range(4):
        outs += [res[k][j] for k in order]
    return tuple(outs)
```

```python
import jax
import jax.numpy as jnp
from jax import lax
from jax.experimental import pallas as pl
from jax.experimental.pallas import tpu as pltpu

F32 = jnp.float32
BF16 = jnp.bfloat16
AXES = ("x", "y", "c")
N_DEV = 8
MESH = pl.DeviceIdType.MESH

HEADS = 4
HEAD_DIM = 128
CHUNK = 128
RET_W = HEADS * HEAD_DIM
LRU_W = 512
LRU_BLOCKS = 8
LRU_BD = LRU_W // LRU_BLOCKS
LRU_C = 8.0
EPS = 1e-6
K_SCALE = HEAD_DIM ** -0.5
ROPE_BASE = 10000.0
GRID_W = 64
TM = 256
SUB = 8

ADAM_LR = 0.001
ADAM_B1 = 0.9
ADAM_B2 = 0.999
ADAM_EPS = 1e-08
ADAM_WD = 0.01
ADAM_STEP = 10

R_CSH1, R_CSC1, R_SH1, R_SC1, R_G1, R_SH2, R_SC2, R_G2 = range(8)


def _pcall(body, **kw):
    return pl.pallas_call(body, **kw)


def _cp(vmem_mb=48):
    return pltpu.CompilerParams(vmem_limit_bytes=vmem_mb << 20)


def _sds(shape, dtype):
    return jax.ShapeDtypeStruct(shape, dtype)


def _dot(a, b):
    return jnp.dot(a.astype(BF16), b.astype(BF16), preferred_element_type=F32)


def _dot_nt(a, b):
    return lax.dot_general(a.astype(BF16), b.astype(BF16), (((1,), (1,)), ((), ())), preferred_element_type=F32)


def _dot_tn(a, b):
    return lax.dot_general(a.astype(BF16), b.astype(BF16), (((0,), (0,)), ((), ())), preferred_element_type=F32)


def _sigmoid(x):
    return jax.nn.sigmoid(x)


def _gelu(x):
    return 0.5 * x * (1.0 + jnp.tanh(0.7978845608028654 * (x + 0.044715 * x * x * x)))


def _dgelu(x):
    t = jnp.tanh(0.7978845608028654 * (x + 0.044715 * x * x * x))
    return 0.5 * (1.0 + t) + 0.5 * x * (1.0 - t * t) * 0.7978845608028654 * (1.0 + 3.0 * 0.044715 * x * x)


def _one_minus_exp(x):
    p = x * (1.0 + x * (1.0 / 2) * (1.0 + x * (1.0 / 3) * (1.0 + x * (1.0 / 4) * (1.0 + x * (1.0 / 5) * (
        1.0 + x * (1.0 / 6) * (1.0 + x * (1.0 / 7) * (1.0 + x * (1.0 / 8))))))))
    return -jnp.where(jnp.abs(x) < 0.3, p, jnp.exp(x) - 1.0)


def _rows_iota(shape):
    return lax.broadcasted_iota(jnp.int32, shape, 0)


def _tile_order(dirn, s, cb, nb):
    if dirn == 0:
        return s
    return jnp.where(s < cb, cb - 1 - s, nb - 1 - (s - cb))


def _all_gather(x, name):
    def body(x_ref, out_ref, send_sems, recv_sems, local_sem):
        mx, my, mc = lax.axis_index("x"), lax.axis_index("y"), lax.axis_index("c")
        me, sibling = (mx, my, mc), (mx, my, 1 - mc)
        chips = [(1 - mx, my), (mx, 1 - my), (1 - mx, 1 - my)]

        def slot(px, py, pc):
            return out_ref.at[4 * px + 2 * py + pc]

        def copy(k, block, to, src=None):
            return pltpu.make_async_remote_copy(
                src_ref=slot(*block) if src is None else src, dst_ref=slot(*block),
                send_sem=send_sems.at[k], recv_sem=recv_sems.at[k], device_id=to, device_id_type=MESH)

        mine = pltpu.make_async_copy(x_ref, slot(*me), local_sem)
        mine.start()
        first = [copy(0, me, sibling, src=x_ref)]
        first += [copy(1 + j, me, (*chip, mc), src=x_ref) for j, chip in enumerate(chips)]
        for cp in first:
            cp.start()
        passed = [copy(4 + j, (*chip, mc), sibling) for j, chip in enumerate(chips)]
        for j, chip in enumerate(chips):
            copy(1 + j, (*chip, mc), me).wait_recv()
            passed[j].start()
        copy(0, sibling, me).wait_recv()
        for j, chip in enumerate(chips):
            copy(4 + j, (*chip, 1 - mc), me).wait_recv()
        for cp in first + passed:
            cp.wait_send()
        mine.wait()

    return _pcall(
        body, name=name, out_shape=_sds((N_DEV,) + x.shape, x.dtype),
        in_specs=[pl.BlockSpec(memory_space=pl.ANY)], out_specs=pl.BlockSpec(memory_space=pl.ANY),
        scratch_shapes=[pltpu.SemaphoreType.DMA((7,)), pltpu.SemaphoreType.DMA((7,)), pltpu.SemaphoreType.DMA(())],
    )(x)


def _all_to_all(g, name):
    def body(g_ref, out_ref, send_sems, recv_sems, local_sem):
        mx, my, mc = lax.axis_index("x"), lax.axis_index("y"), lax.axis_index("c")
        me = 4 * mx + 2 * my + mc
        mine = pltpu.make_async_copy(g_ref.at[me], out_ref.at[me], local_sem)
        mine.start()
        copies = []
        for k in range(1, N_DEV):
            px = 1 - mx if (k >> 2) & 1 else mx
            py = 1 - my if (k >> 1) & 1 else my
            pc = 1 - mc if k & 1 else mc
            cp = pltpu.make_async_remote_copy(
                src_ref=g_ref.at[4 * px + 2 * py + pc], dst_ref=out_ref.at[me],
                send_sem=send_sems.at[k - 1], recv_sem=recv_sems.at[k - 1],
                device_id=(px, py, pc), device_id_type=MESH)
            cp.start()
            copies.append(cp)
        for cp in copies:
            cp.wait()
        mine.wait()

    return _pcall(
        body, name=name, out_shape=_sds(g.shape, g.dtype),
        in_specs=[pl.BlockSpec(memory_space=pl.ANY)], out_specs=pl.BlockSpec(memory_space=pl.ANY),
        scratch_shapes=[pltpu.SemaphoreType.DMA((7,)), pltpu.SemaphoreType.DMA((7,)), pltpu.SemaphoreType.DMA(())],
    )(g)


def _mm_nn(a, w, out_dtype, name, square_lhs=False, relu_out=False):
    m, k = a.shape
    nb, _, bn = w.shape

    def body(a_ref, w_ref, o_ref):
        av = a_ref[...]
        if square_lhs:
            av = av * av
        for j in range(nb):
            r = jnp.dot(av, w_ref[j], preferred_element_type=F32)
            if relu_out:
                r = jnp.maximum(r, 0.0)
            o_ref[:, j * bn:(j + 1) * bn] = r.astype(out_dtype)

    return _pcall(
        body, name=name, grid=(m // TM,),
        in_specs=[pl.BlockSpec((TM, k), lambda i: (i, 0)), pl.BlockSpec((nb, k, bn), lambda i: (0, 0, 0))],
        out_specs=pl.BlockSpec((TM, nb * bn), lambda i: (i, 0)),
        out_shape=_sds((m, nb * bn), out_dtype), compiler_params=_cp(),
    )(a, w)


def _mm_nt(dy, w, out_dtype, name, relu_mul=None):
    m = dy.shape[0]
    nb, k, bn = w.shape

    def body(*refs):
        if relu_mul is None:
            dy_ref, w_ref, o_ref = refs
        else:
            dy_ref, w_ref, r_ref, o_ref = refs
        acc = _dot_nt(dy_ref[:, 0:bn], w_ref[0])
        for j in range(1, nb):
            acc = acc + _dot_nt(dy_ref[:, j * bn:(j + 1) * bn], w_ref[j])
        if relu_mul is not None:
            acc = acc * (2.0 * r_ref[...].astype(F32))
        o_ref[...] = acc.astype(out_dtype)

    in_specs = [pl.BlockSpec((TM, nb * bn), lambda i: (i, 0)), pl.BlockSpec((nb, k, bn), lambda i: (0, 0, 0))]
    args = [dy, w]
    if relu_mul is not None:
        in_specs.append(pl.BlockSpec((TM, k), lambda i: (i, 0)))
        args.append(relu_mul)
    return _pcall(
        body, name=name, grid=(m // TM,), in_specs=in_specs,
        out_specs=pl.BlockSpec((TM, k), lambda i: (i, 0)),
        out_shape=_sds((m, k), out_dtype), compiler_params=_cp(),
    )(*args)


def _mm_tn(a, b, name, col_blocks, block, out_dtype, tm, square_lhs=False):
    m, k = a.shape
    nn = b.shape[1]
    steps = m // tm
    if col_blocks:
        nblk, acc_shape = nn // block, (k, block)
        a_spec = pl.BlockSpec((tm, k), lambda j, s: (s, 0))
        b_spec = pl.BlockSpec((tm, block), lambda j, s: (s, j))
    else:
        nblk, acc_shape = k // block, (block, nn)
        a_spec = pl.BlockSpec((tm, block), lambda j, s: (s, j))
        b_spec = pl.BlockSpec((tm, nn), lambda j, s: (s, 0))

    def body(a_ref, b_ref, o_ref, acc):
        s = pl.program_id(1)

        @pl.when(s == 0)
        def _():
            acc[...] = jnp.zeros_like(acc)

        av = a_ref[...]
        if square_lhs:
            av = av.astype(F32)
            av = (av * av).astype(BF16)
        acc[...] += _dot_tn(av, b_ref[...])

        @pl.when(s == steps - 1)
        def _():
            o_ref[...] = acc[...].astype(out_dtype)

    return _pcall(
        body, name=name, grid=(nblk, steps), in_specs=[a_spec, b_spec],
        out_specs=pl.BlockSpec((None,) + acc_shape, lambda j, s: (j, 0, 0)),
        out_shape=_sds((nblk,) + acc_shape, out_dtype),
        scratch_shapes=[pltpu.VMEM(acc_shape, F32)], compiler_params=_cp(),
    )(a, b)


def _mm_wgrad(at, b, name, bn, out_dtype, transpose_out=False, square_rhs=False):
    k, m = at.shape
    nblk = b.shape[1] // bn

    def body(a_ref, b_ref, o_ref):
        bv = b_ref[...]
        if square_rhs:
            bv = bv * bv
        r = jnp.dot(a_ref[...], bv, preferred_element_type=F32)
        o_ref[...] = (r.T if transpose_out else r).astype(out_dtype)

    out_block = (bn, k) if transpose_out else (k, bn)
    return _pcall(
        body, name=name, grid=(nblk,),
        in_specs=[pl.BlockSpec((k, m), lambda j: (0, 0)), pl.BlockSpec((m, bn), lambda j: (0, j))],
        out_specs=pl.BlockSpec((None,) + out_block, lambda j: (j, 0, 0)),
        out_shape=_sds((nblk,) + out_block, out_dtype), compiler_params=_cp(),
    )(at, b)


def _mod_part(s16, w_ada, b_cols):
    def body(s_ref, w_ref, b_ref, o_ref):
        o_ref[...] = _dot(s_ref[...], w_ref[...]) + b_ref[...]

    return _pcall(body, name="mod_part", out_shape=_sds((s16.shape[0], w_ada.shape[1]), F32),
                  compiler_params=_cp())(s16, w_ada, b_cols)


def _ada_bwd(s16, dm_cols, w_ada):
    def body(s_ref, d_ref, w_ref, gw_ref, ds_ref):
        gw_ref[...] = _dot_tn(s_ref[...], d_ref[...])
        ds_ref[...] = _dot_nt(d_ref[...], w_ref[...])

    return _pcall(body, name="ada_bwd",
                  out_shape=(_sds(w_ada.shape, F32), _sds(s16.shape, F32)), compiler_params=_cp())(s16, dm_cols, w_ada)


def _norm1_fwd(ctx, x, g, modrows, cb):
    l_len, d = ctx.shape
    nb = (l_len + x.shape[0]) // TM

    def body(ctx_ref, x_ref, g_ref, m_ref, o_ref, ot_ref):
        is_ctx = pl.program_id(0) < cb
        xin = jnp.where(is_ctx, ctx_ref[...], x_ref[...])
        sh = jnp.where(is_ctx, m_ref[R_CSH1:R_CSH1 + 1, :], m_ref[R_SH1:R_SH1 + 1, :])
        sc = jnp.where(is_ctx, m_ref[R_CSC1:R_CSC1 + 1, :], m_ref[R_SC1:R_SC1 + 1, :])
        ms = jnp.mean(xin * xin, axis=-1, keepdims=True)
        n = xin * lax.rsqrt(ms + EPS) * g_ref[...]
        hn = n * (1.0 + sc) + sh
        o_ref[...] = hn.astype(BF16)
        ot_ref[...] = hn.T.astype(BF16)

    return _pcall(
        body, name="norm1_fwd", grid=(nb,),
        in_specs=[pl.BlockSpec((TM, d), lambda i: (jnp.minimum(i, cb - 1), 0)),
                  pl.BlockSpec((TM, d), lambda i: (jnp.maximum(i - cb, 0), 0)),
                  pl.BlockSpec((1, d), lambda i: (0, 0)), pl.BlockSpec((8, d), lambda i: (0, 0))],
        out_specs=(pl.BlockSpec((TM, d), lambda i: (i, 0)), pl.BlockSpec((d, TM), lambda i: (0, i))),
        out_shape=(_sds((nb * TM, d), BF16), _sds((d, nb * TM), BF16)), compiler_params=_cp(),
    )(ctx, x, g, modrows)


def _mix_fwd(o_f, o_b, p, h_f, h_b, cb, t_len):
    def body(of_ref, ob_ref, g_ref, gate_ref, hf_ref, hb_ref, mix_ref):
        o = of_ref[...] + ob_ref[...]
        g = g_ref[...]
        sg = g * _sigmoid(g)
        for hh in range(HEADS):
            sl = slice(hh * HEAD_DIM, (hh + 1) * HEAD_DIM)
            oh = o[:, sl]
            yc = oh - jnp.mean(oh, axis=-1, keepdims=True)
            var = jnp.mean(yc * yc, axis=-1, keepdims=True)
            mix_ref[:, sl] = (sg[:, sl] * (yc * lax.rsqrt(var + EPS))).astype(BF16)
        mix_ref[:, RET_W:] = ((hf_ref[...] + hb_ref[...]) * _gelu(gate_ref[...])).astype(BF16)

    row = lambda i: (i + cb, 0)
    return _pcall(
        body, name="mix_fwd", grid=(t_len // TM,),
        in_specs=[pl.BlockSpec((TM, RET_W), row), pl.BlockSpec((TM, RET_W), row),
                  pl.BlockSpec((TM, RET_W), lambda i: (i + cb, 3)), pl.BlockSpec((TM, LRU_W), lambda i: (i + cb, 5)),
                  pl.BlockSpec((TM, LRU_W), row), pl.BlockSpec((TM, LRU_W), row)],
        out_specs=pl.BlockSpec((TM, RET_W + LRU_W), lambda i: (i, 0)),
        out_shape=_sds((t_len, RET_W + LRU_W), BF16), compiler_params=_cp(),
    )(o_f, o_b, p, p, h_f, h_b)


def _res_norm2(x, y, g, modrows):
    t_len, d = x.shape

    def body(x_ref, y_ref, g_ref, m_ref, x1_ref, h2_ref, h2t_ref):
        x1 = x_ref[...] + m_ref[R_G1:R_G1 + 1, :] * y_ref[...]
        ms = jnp.mean(x1 * x1, axis=-1, keepdims=True)
        n = x1 * lax.rsqrt(ms + EPS) * g_ref[...]
        x1_ref[...] = x1
        h2 = n * (1.0 + m_ref[R_SC2:R_SC2 + 1, :]) + m_ref[R_SH2:R_SH2 + 1, :]
        h2_ref[...] = h2.astype(BF16)
        h2t_ref[...] = h2.T.astype(BF16)

    t = pl.BlockSpec((TM, d), lambda i: (i, 0))
    tt = pl.BlockSpec((d, TM), lambda i: (0, i))
    return _pcall(
        body, name="res_norm2", grid=(t_len // TM,),
        in_specs=[t, t, pl.BlockSpec((1, d), lambda i: (0, 0)), pl.BlockSpec((8, d), lambda i: (0, 0))],
        out_specs=(t, t, tt), out_shape=(_sds((t_len, d), F32), _sds((t_len, d), BF16), _sds((d, t_len), BF16)),
        compiler_params=_cp(),
    )(x, y, g, modrows)


def _final_fwd_bwd(x1, z, target, fg, modrows):
    t_len, d = x1.shape

    def body(x1_ref, z_ref, t_ref, fg_ref, m_ref, dx2_ref, dz_ref, dzt_ref, acc_ref):
        @pl.when(pl.program_id(0) == 0)
        def _():
            acc_ref[...] = jnp.zeros_like(acc_ref)

        g2 = m_ref[R_G2:R_G2 + 1, :]
        z = z_ref[...]
        x2 = x1_ref[...] + g2 * z
        rstd = lax.rsqrt(jnp.mean(x2 * x2, axis=-1, keepdims=True) + EPS)
        xh = x2 * rstd
        fg = fg_ref[...]
        e = xh * fg - t_ref[...]
        dy = e * (1.0 / d)
        dxh = dy * fg
        dx2 = rstd * (dxh - xh * jnp.mean(dxh * xh, axis=-1, keepdims=True))
        dx2_ref[...] = dx2
        dz = g2 * dx2
        dz_ref[...] = dz.astype(BF16)
        dzt_ref[...] = dz.T.astype(BF16)
        acc_ref[0:1, :] += jnp.sum(dy * xh, axis=0, keepdims=True)
        acc_ref[1:2, :] += jnp.sum(dx2 * z, axis=0, keepdims=True)
        acc_ref[2:3, :] += jnp.sum(e * e, axis=0, keepdims=True)

    t = pl.BlockSpec((TM, d), lambda i: (i, 0))
    return _pcall(
        body, name="final_fwd_bwd", grid=(t_len // TM,),
        in_specs=[t, t, t, pl.BlockSpec((1, d), lambda i: (0, 0)), pl.BlockSpec((8, d), lambda i: (0, 0))],
        out_specs=(t, t, pl.BlockSpec((d, TM), lambda i: (0, i)), pl.BlockSpec((8, d), lambda i: (0, 0))),
        out_shape=(_sds((t_len, d), F32), _sds((t_len, d), BF16), _sds((d, t_len), BF16), _sds((8, d), F32)),
        compiler_params=_cp(),
    )(x1, z, target, fg, modrows)


def _bwd_norm2(dh2, x1, dx2, y, g, modrows):
    t_len, d = x1.shape

    def body(dh2_ref, x1_ref, dx2_ref, y_ref, g_ref, m_ref, dx1_ref, dy_ref, acc_ref):
        @pl.when(pl.program_id(0) == 0)
        def _():
            acc_ref[...] = jnp.zeros_like(acc_ref)

        x1 = x1_ref[...]
        rstd = lax.rsqrt(jnp.mean(x1 * x1, axis=-1, keepdims=True) + EPS)
        xh = x1 * rstd
        gn = g_ref[...]
        dh2 = dh2_ref[...]
        dn = dh2 * (1.0 + m_ref[R_SC2:R_SC2 + 1, :])
        dxh = dn * gn
        dx1 = dx2_ref[...] + rstd * (dxh - xh * jnp.mean(dxh * xh, axis=-1, keepdims=True))
        dx1_ref[...] = dx1
        dy_ref[...] = (m_ref[R_G1:R_G1 + 1, :] * dx1).astype(BF16)
        acc_ref[0:1, :] += jnp.sum(dh2, axis=0, keepdims=True)
        acc_ref[1:2, :] += jnp.sum(dh2 * xh * gn, axis=0, keepdims=True)
        acc_ref[2:3, :] += jnp.sum(dn * xh, axis=0, keepdims=True)
        acc_ref[3:4, :] += jnp.sum(dx1 * y_ref[...], axis=0, keepdims=True)

    t = pl.BlockSpec((TM, d), lambda i: (i, 0))
    return _pcall(
        body, name="bwd_norm2", grid=(t_len // TM,),
        in_specs=[t, t, t, t, pl.BlockSpec((1, d), lambda i: (0, 0)), pl.BlockSpec((8, d), lambda i: (0, 0))],
        out_specs=(t, t, pl.BlockSpec((8, d), lambda i: (0, 0))),
        out_shape=(_sds((t_len, d), F32), _sds((t_len, d), BF16), _sds((8, d), F32)), compiler_params=_cp(),
    )(dh2, x1, dx2, y, g, modrows)


def _mix_bwd(dmix, o_f, o_b, p, h_f, h_b, cb):
    n = o_f.shape[0]

    def body(dm_ref, of_ref, ob_ref, g_ref, gate_ref, hf_ref, hb_ref, do_ref, dg_ref, dgate_ref, dhs_ref):
        is_ctx = pl.program_id(0) < cb

        @pl.when(is_ctx)
        def _():
            for r in (do_ref, dg_ref, dgate_ref, dhs_ref):
                r[...] = jnp.zeros_like(r)

        @pl.when(jnp.logical_not(is_ctx))
        def _():
            o = of_ref[...] + ob_ref[...]
            g = g_ref[...]
            s = _sigmoid(g)
            sg = g * s
            dsg = s * (1.0 + g * (1.0 - s))
            for hh in range(HEADS):
                sl = slice(hh * HEAD_DIM, (hh + 1) * HEAD_DIM)
                oh = o[:, sl]
                yc = oh - jnp.mean(oh, axis=-1, keepdims=True)
                rs = lax.rsqrt(jnp.mean(yc * yc, axis=-1, keepdims=True) + EPS)
                gn = yc * rs
                dret = dm_ref[:, sl]
                dgn = dret * sg[:, sl]
                dg_ref[:, sl] = dret * gn * dsg[:, sl]
                do_ref[:, sl] = rs * (dgn - jnp.mean(dgn, axis=-1, keepdims=True)
                                      - gn * jnp.mean(dgn * gn, axis=-1, keepdims=True))
            dlru = dm_ref[:, RET_W:]
            gate = gate_ref[...]
            dhs_ref[...] = dlru * _gelu(gate)
            dgate_ref[...] = dlru * (hf_ref[...] + hb_ref[...]) * _dgelu(gate)

    row = lambda i: (i, 0)
    t = pl.BlockSpec((TM, RET_W), row)
    return _pcall(
        body, name="mix_bwd", grid=(n // TM,),
        in_specs=[pl.BlockSpec((TM, RET_W + LRU_W), lambda i: (jnp.maximum(i - cb, 0), 0)), t, t,
                  pl.BlockSpec((TM, RET_W), lambda i: (i, 3)), pl.BlockSpec((TM, LRU_W), lambda i: (i, 5)), t, t],
        out_specs=(t, t, t, t), out_shape=tuple(_sds((n, RET_W), F32) for _ in range(4)), compiler_params=_cp(),
    )(dmix, o_f, o_b, p, p, h_f, h_b)


def _bwd_norm1(dhn, ctx, x, dx1, g, modrows, cb):
    t_len, d = x.shape
    nb = dhn.shape[0] // TM

    def body(dh_ref, ctx_ref, x_ref, dx1_ref, g_ref, m_ref, gx_ref, acc_ref):
        is_ctx = pl.program_id(0) < cb

        @pl.when(pl.program_id(0) == 0)
        def _():
            acc_ref[...] = jnp.zeros_like(acc_ref)

        xin = jnp.where(is_ctx, ctx_ref[...], x_ref[...])
        sc = jnp.where(is_ctx, m_ref[R_CSC1:R_CSC1 + 1, :], m_ref[R_SC1:R_SC1 + 1, :])
        rstd = lax.rsqrt(jnp.mean(xin * xin, axis=-1, keepdims=True) + EPS)
        xh = xin * rstd
        gn = g_ref[...]
        dh = dh_ref[...]
        dn = dh * (1.0 + sc)
        dxh = dn * gn
        gx_ref[...] = dx1_ref[...] + rstd * (dxh - xh * jnp.mean(dxh * xh, axis=-1, keepdims=True))
        s0 = jnp.sum(dh, axis=0, keepdims=True)
        s1 = jnp.sum(dh * xh * gn, axis=0, keepdims=True)
        acc_ref[4:5, :] += jnp.sum(dn * xh, axis=0, keepdims=True)

        @pl.when(is_ctx)
        def _():
            acc_ref[0:1, :] += s0
            acc_ref[1:2, :] += s1

        @pl.when(jnp.logical_not(is_ctx))
        def _():
            acc_ref[2:3, :] += s0
            acc_ref[3:4, :] += s1

    lat = pl.BlockSpec((TM, d), lambda i: (jnp.maximum(i - cb, 0), 0))
    return _pcall(
        body, name="bwd_norm1", grid=(nb,),
        in_specs=[pl.BlockSpec((TM, d), lambda i: (i, 0)), pl.BlockSpec((TM, d), lambda i: (jnp.minimum(i, cb - 1), 0)),
                  lat, lat, pl.BlockSpec((1, d), lambda i: (0, 0)), pl.BlockSpec((8, d), lambda i: (0, 0))],
        out_specs=(lat, pl.BlockSpec((8, d), lambda i: (0, 0))),
        out_shape=(_sds((t_len, d), F32), _sds((8, d), F32)), compiler_params=_cp(),
    )(dhn, ctx, x, dx1, g, modrows)


def _rot(x, cf, ss):
    return x * cf + pltpu.roll(x, HEAD_DIM // 2, 1) * ss


def _decay_exponents(dirn):
    ii = lax.broadcasted_iota(jnp.int32, (CHUNK, CHUNK), 0)
    jj = lax.broadcasted_iota(jnp.int32, (CHUNK, CHUNK), 1)
    rel = ii - jj if dirn == 0 else jj - ii
    pos = ii.astype(F32)
    if dirn == 0:
        cq, cs = pos + 1.0, (CHUNK - 1.0) - pos
    else:
        cq, cs = CHUNK - pos, pos
    return rel, jnp.maximum(rel, 0).astype(F32), cq, cs


def _store_decay(lg_ref, dec):
    for dirn in (0, 1):
        rel, relf, cq, cs = _decay_exponents(dirn)
        for h in range(HEADS):
            lgv = lg_ref[dirn, h]
            dec[dirn, h, 0] = jnp.where(rel >= 0, jnp.exp(lgv * relf), 0.0)
            dec[dirn, h, 1] = jnp.exp(lgv * cq)
            dec[dirn, h, 2] = jnp.exp(lgv * cs)


def _ret_rows(cc, nc, step_of):
    return [lambda s, dirn=dirn: _tile_order(dirn, step_of(s), cc, nc) for dirn in (0, 1)]


def _ret_in_specs(rows):
    specs = []
    for row in rows:
        specs += [pl.BlockSpec((CHUNK, RET_W), lambda s, o=o, row=row: (row(s), o)) for o in (0, 1, 2)]
        specs += [pl.BlockSpec((CHUNK, HEAD_DIM), lambda s, row=row: (row(s), 0))] * 2
    return specs


def _ret_fwd(p, cosf, sins, lg, cc):
    n = p.shape[0]
    nc = n // CHUNK
    rows = _ret_rows(cc, nc, lambda s: s)

    def body(lg_ref, q0, k0, v0, c0, s0, q1, k1, v1, c1, s1, o0, o1, sp0, sp1, st, dec):
        @pl.when(pl.program_id(0) == 0)
        def _():
            st[...] = jnp.zeros_like(st)
            _store_decay(lg_ref, dec)

        for dirn, (q_ref, k_ref, v_ref, c_ref, s_ref, o_ref, sp_ref) in enumerate(
                ((q0, k0, v0, c0, s0, o0, sp0), (q1, k1, v1, c1, s1, o1, sp1))):
            cf, ss = c_ref[...], s_ref[...]
            for h in range(HEADS):
                sl = slice(h * HEAD_DIM, (h + 1) * HEAD_DIM)
                q = _rot(q_ref[:, sl], cf, ss)
                k = _rot(k_ref[:, sl], cf, ss) * K_SCALE
                v = v_ref[:, sl]
                sp = st[dirn, h]
                sp_ref[h] = sp
                o_ref[:, sl] = _dot(_dot_nt(q, k) * dec[dirn, h, 0], v) + _dot(q * dec[dirn, h, 1], sp)
                st[dirn, h] = jnp.exp(lg_ref[dirn, h] * CHUNK) * sp + _dot_tn(k * dec[dirn, h, 2], v)

    o_specs = [pl.BlockSpec((CHUNK, RET_W), lambda s, row=row: (row(s), 0)) for row in rows]
    state = pl.BlockSpec((None, HEADS, CHUNK, HEAD_DIM), lambda s: (s, 0, 0, 0))
    return _pcall(
        body, name="ret_fwd", grid=(nc,),
        in_specs=[pl.BlockSpec(memory_space=pltpu.SMEM)] + _ret_in_specs(rows),
        out_specs=(o_specs[0], o_specs[1], state, state),
        out_shape=(_sds((n, RET_W), F32),) * 2 + (_sds((nc, HEADS, CHUNK, HEAD_DIM), F32),) * 2,
        scratch_shapes=[pltpu.VMEM((2, HEADS, CHUNK, HEAD_DIM), F32), pltpu.VMEM((2, HEADS, 3, CHUNK, CHUNK), F32)],
        compiler_params=_cp(),
    )(lg, p, p, p, cosf, sins, p, p, p, cosf, sins)


def _ret_bwd(p, cosf, sins, lg, do, s_prev, cc):
    n = p.shape[0]
    nc = n // CHUNK
    rows = _ret_rows(cc, nc, lambda s: nc - 1 - s)

    def body(lg_ref, q0, k0, v0, c0, s0, q1, k1, v1, c1, s1, do0, do1, sp0, sp1,
             dq0, dk0, dv0, dq1, dk1, dv1, dlg_ref, dst, dec):
        @pl.when(pl.program_id(0) == 0)
        def _():
            dst[...] = jnp.zeros_like(dst)
            dlg_ref[...] = jnp.zeros_like(dlg_ref)
            _store_decay(lg_ref, dec)

        for dirn, (q_ref, k_ref, v_ref, c_ref, s_ref, do_ref, sp_ref, dq_ref, dk_ref, dv_ref) in enumerate(
                ((q0, k0, v0, c0, s0, do0, sp0, dq0, dk0, dv0), (q1, k1, v1, c1, s1, do1, sp1, dq1, dk1, dv1))):
            cf, ss = c_ref[...], s_ref[...]
            _, relf, cq, cs = _decay_exponents(dirn)
            for h in range(HEADS):
                sl = slice(h * HEAD_DIM, (h + 1) * HEAD_DIM)
                q = _rot(q_ref[:, sl], cf, ss)
                k = _rot(k_ref[:, sl], cf, ss) * K_SCALE
                v = v_ref[:, sl]
                dov = do_ref[:, sl]
                dm, wq, ws = dec[dirn, h, 0], dec[dirn, h, 1], dec[dirn, h, 2]
                a = _dot_nt(q, k)
                sp = sp_ref[h]
                dsn = dst[dirn, h]
                gc = jnp.exp(lg_ref[dirn, h] * CHUNK)
                g1 = _dot_nt(dov, sp)
                da = _dot_nt(dov, v) * dm
                dq = g1 * wq + _dot(da, k)
                h1 = _dot_nt(v, dsn)
                dk = _dot_tn(da, q) + h1 * ws
                dv_ref[:, sl] = _dot_tn(a * dm, dov) + _dot(k * ws, dsn)
                dst[dirn, h] = gc * dsn + _dot_tn(q * wq, dov)
                term = da * a * relf + q * g1 * (wq * cq) + k * h1 * (ws * cs) + sp * dsn * (CHUNK * gc)
                dlg_ref[dirn * HEADS + h, 0:1, :] += jnp.sum(term, axis=0, keepdims=True)
                dq_ref[:, sl] = dq * cf + pltpu.roll(dq * ss, HEAD_DIM // 2, 1)
                dk_ref[:, sl] = (dk * cf + pltpu.roll(dk * ss, HEAD_DIM // 2, 1)) * K_SCALE

    wide = [pl.BlockSpec((CHUNK, RET_W), lambda s, row=row: (row(s), 0)) for row in rows]
    state = pl.BlockSpec((None, HEADS, CHUNK, HEAD_DIM), lambda s: (nc - 1 - s, 0, 0, 0))
    return _pcall(
        body, name="ret_bwd", grid=(nc,),
        in_specs=[pl.BlockSpec(memory_space=pltpu.SMEM)] + _ret_in_specs(rows) + wide + [state, state],
        out_specs=(wide[0],) * 3 + (wide[1],) * 3 + (pl.BlockSpec((2 * HEADS, 8, HEAD_DIM), lambda s: (0, 0, 0)),),
        out_shape=(_sds((n, RET_W), F32),) * 6 + (_sds((2 * HEADS, 8, HEAD_DIM), F32),),
        scratch_shapes=[pltpu.VMEM((2, HEADS, CHUNK, HEAD_DIM), F32), pltpu.VMEM((2, HEADS, 3, CHUNK, CHUNK), F32)],
        compiler_params=_cp(),
    )(lg, p, p, p, cosf, sins, p, p, p, cosf, sins, do, do, s_prev[0], s_prev[1])


def _shift_rows(cur, prev8, next8, k, seg_start, seg_end):
    tm = cur.shape[0]
    rows = _rows_iota(cur.shape)
    if k < 0:
        out = pltpu.roll(cur, -k, 0)
        for j in range(-k):
            halo = jnp.where(seg_start, 0.0, prev8[SUB + k + j:SUB + k + j + 1, :])
            out = jnp.where(rows == j, halo, out)
    else:
        out = pltpu.roll(cur, tm - k, 0)
        for j in range(k):
            halo = jnp.where(seg_end, 0.0, next8[j:j + 1, :])
            out = jnp.where(rows == tm - k + j, halo, out)
    return out


def _seg_flags(t, cb, nb):
    return jnp.logical_or(t == 0, t == cb), jnp.logical_or(t == cb - 1, t == nb - 1)


def _halo_specs(tile_of, n_rows, col):
    per = TM // SUB
    return [pl.BlockSpec((TM, LRU_W), lambda s: (tile_of(s), col)),
            pl.BlockSpec((SUB, LRU_W), lambda s: (jnp.maximum(tile_of(s) * per - 1, 0), col)),
            pl.BlockSpec((SUB, LRU_W), lambda s: (jnp.minimum((tile_of(s) + 1) * per, n_rows // SUB - 1), col))]


def _lru_gates(xr, prev8, next8, seg_start, seg_end, cw_ref, cb_ref, wg_ref, bg_ref, sp_ref):
    xm1 = _shift_rows(xr, prev8, next8, -1, seg_start, seg_end)
    xp1 = _shift_rows(xr, prev8, next8, 1, seg_start, seg_end)
    xp2 = _shift_rows(xr, prev8, next8, 2, seg_start, seg_end)
    xc = cb_ref[...] + xm1 * cw_ref[0:1, :] + xr * cw_ref[1:2, :] + xp1 * cw_ref[2:3, :] + xp2 * cw_ref[3:4, :]
    pre = _dot(xc, wg_ref[...]) + bg_ref[...]
    r = _sigmoid(pre[:, :LRU_W])
    i = _sigmoid(pre[:, LRU_W:])
    la = (-LRU_C) * r * sp_ref[...]
    a = jnp.exp(la)
    sq = jnp.sqrt(_one_minus_exp(2.0 * la))
    return xc, r, i, a, sq


def _scan_tile(a, b, ascending, a_sc, b_sc, carry, out_ref):
    tm = a.shape[0]
    r8 = _rows_iota(a.shape) % SUB
    for k in (1, 2, 4):
        if ascending:
            m = r8 >= k
            a_s, b_s = pltpu.roll(a, k, 0), pltpu.roll(b, k, 0)
        else:
            m = r8 < SUB - k
            a_s, b_s = pltpu.roll(a, tm - k, 0), pltpu.roll(b, tm - k, 0)
        b = a * jnp.where(m, b_s, 0.0) + b
        a = a * jnp.where(m, a_s, 1.0)
    a_sc[...] = a
    b_sc[...] = b
    nsub = tm // SUB

    def step(j, c):
        off = pl.multiple_of((j if ascending else nsub - 1 - j) * SUB, SUB)
        hb = a_sc[pl.ds(off, SUB), :] * c + b_sc[pl.ds(off, SUB), :]
        out_ref[pl.ds(off, SUB), :] = hb
        last = hb[SUB - 1:SUB, :] if ascending else hb[0:1, :]
        return jnp.broadcast_to(last, c.shape)

    carry[...] = lax.fori_loop(0, nsub, step, carry[...])


def _lru_fwd(p, wg, bg, sp, cw, cbias, dirn, cb):
    n = p.shape[0]
    nb = n // TM
    tile_of = lambda s: _tile_order(dirn, s, cb, nb)

    def body(x_ref, xp_ref, xn_ref, wg_ref, bg_ref, sp_ref, cw_ref, cb_ref, h_ref, cin_ref, carry, a_sc, b_sc):
        s = pl.program_id(0)

        @pl.when(s == 0)
        def _():
            carry[...] = jnp.zeros_like(carry)

        seg_start, seg_end = _seg_flags(tile_of(s), cb, nb)
        xc, r, i, a, sq = _lru_gates(x_ref[...], xp_ref[...], xn_ref[...], seg_start, seg_end,
                                     cw_ref, cb_ref, wg_ref, bg_ref, sp_ref)
        cin_ref[...] = carry[...]
        _scan_tile(a, sq * (i * xc), dirn == 0, a_sc, b_sc, carry, h_ref)

    full = lambda shape: pl.BlockSpec(shape, lambda s: (0,) * len(shape))
    return _pcall(
        body, name=f"lru_fwd{dirn}", grid=(nb,),
        in_specs=_halo_specs(tile_of, n, 4) + [full((LRU_W, 2 * LRU_W)), full((1, 2 * LRU_W)), full((1, LRU_W)),
                                               full((4, LRU_W)), full((1, LRU_W))],
        out_specs=(pl.BlockSpec((TM, LRU_W), lambda s: (tile_of(s), 0)),
                   pl.BlockSpec((None, SUB, LRU_W), lambda s: (tile_of(s), 0, 0))),
        out_shape=(_sds((n, LRU_W), F32), _sds((nb, SUB, LRU_W), F32)),
        scratch_shapes=[pltpu.VMEM((SUB, LRU_W), F32), pltpu.VMEM((TM, LRU_W), F32), pltpu.VMEM((TM, LRU_W), F32)],
        compiler_params=_cp(),
    )(p, p, p, wg, bg, sp, cw, cbias)


def _lru_bwd(p, wg, bg, sp, cw, cbias, h, cin, dhs, dirn, cb):
    n = p.shape[0]
    nb = n // TM
    tile_of = lambda s: _tile_order(dirn, nb - 1 - s, cb, nb)

    def body(x_ref, xp_ref, xn_ref, wg_ref, bg_ref, sp_ref, cw_ref, cb_ref, h_ref, cin_ref, dhs_ref,
             dxc_ref, dwg_ref, acc_ref, carry, a_sc, b_sc, mu_sc):
        s = pl.program_id(0)

        @pl.when(s == 0)
        def _():
            carry[...] = jnp.zeros_like(carry)
            dwg_ref[...] = jnp.zeros_like(dwg_ref)
            acc_ref[...] = jnp.zeros_like(acc_ref)

        seg_start, seg_end = _seg_flags(tile_of(s), cb, nb)
        xc, r, i, a, sq = _lru_gates(x_ref[...], xp_ref[...], xn_ref[...], seg_start, seg_end,
                                     cw_ref, cb_ref, wg_ref, bg_ref, sp_ref)
        rows = _rows_iota(a.shape)
        hv = h_ref[...]
        dh = dhs_ref[...]
        mu_next = carry[0:1, :]
        _scan_tile(a, a * dh, dirn == 1, a_sc, b_sc, carry, mu_sc)
        mu = mu_sc[...]
        if dirn == 0:
            hprev = jnp.where(rows == 0, cin_ref[0:1, :], pltpu.roll(hv, 1, 0))
            lam = dh + jnp.where(rows == TM - 1, mu_next, pltpu.roll(mu, TM - 1, 0))
        else:
            hprev = jnp.where(rows == TM - 1, cin_ref[0:1, :], pltpu.roll(hv, TM - 1, 0))
            lam = dh + jnp.where(rows == 0, mu_next, pltpu.roll(mu, 1, 0))
        ds = lam * (i * xc)
        di = lam * (sq * xc)
        dla = lam * hprev * a - ds * (a * a) / jnp.maximum(sq, 1e-20)
        dpr = dla * ((-LRU_C) * sp_ref[...]) * r * (1.0 - r)
        dpi = di * i * (1.0 - i)
        dpre = jnp.concatenate([dpr, dpi], axis=1)
        dxc_ref[...] = lam * (sq * i) + _dot_nt(dpre, wg_ref[...])
        dwg_ref[...] += _dot_tn(xc, dpre)
        acc_ref[0:1, :] += jnp.sum(dpre, axis=0, keepdims=True)
        acc_ref[1:2, 0:LRU_W] += jnp.sum(dla * ((-LRU_C) * r), axis=0, keepdims=True)

    full = lambda shape: pl.BlockSpec(shape, lambda s: (0,) * len(shape))
    tile = pl.BlockSpec((TM, LRU_W), lambda s: (tile_of(s), 0))
    return _pcall(
        body, name=f"lru_bwd{dirn}", grid=(nb,),
        in_specs=_halo_specs(tile_of, n, 4) + [full((LRU_W, 2 * LRU_W)), full((1, 2 * LRU_W)), full((1, LRU_W)),
                                               full((4, LRU_W)), full((1, LRU_W)), tile,
                                               pl.BlockSpec((None, SUB, LRU_W), lambda s: (tile_of(s), 0, 0)), tile],
        out_specs=(tile, full((LRU_W, 2 * LRU_W)), full((8, 2 * LRU_W))),
        out_shape=(_sds((n, LRU_W), F32), _sds((LRU_W, 2 * LRU_W), F32), _sds((8, 2 * LRU_W), F32)),
        scratch_shapes=[pltpu.VMEM((SUB, LRU_W), F32)] + [pltpu.VMEM((TM, LRU_W), F32)] * 3,
        compiler_params=_cp(),
    )(p, p, p, wg, bg, sp, cw, cbias, h, cin, dhs)


def _assemble_dp(dqs, dks, dvs, dg, dgate, dxcs, p, cw, cb):
    n = p.shape[0]
    nb = n // TM
    tile_of = lambda s: s

    def body(dqf, dqb, dkf, dkb, dvf, dvb, dg_ref, dgate_ref, cf, pf, nf, cb_, pb, nb_, x_ref, xp_ref, xn_ref,
             cw_ref, dp_ref, acc_ref):
        s = pl.program_id(0)

        @pl.when(s == 0)
        def _():
            acc_ref[...] = jnp.zeros_like(acc_ref)

        seg_start, seg_end = _seg_flags(s, cb, nb)
        dp_ref[:, 0:RET_W] = (dqf[...] + dqb[...]).astype(BF16)
        dp_ref[:, RET_W:2 * RET_W] = (dkf[...] + dkb[...]).astype(BF16)
        dp_ref[:, 2 * RET_W:3 * RET_W] = (dvf[...] + dvb[...]).astype(BF16)
        dp_ref[:, 3 * RET_W:4 * RET_W] = dg_ref[...].astype(BF16)
        dxc = cf[...] + cb_[...]
        dprev = pf[...] + pb[...]
        dnext = nf[...] + nb_[...]
        dxr = (_shift_rows(dxc, dprev, dnext, 1, seg_start, seg_end) * cw_ref[0:1, :] + dxc * cw_ref[1:2, :]
               + _shift_rows(dxc, dprev, dnext, -1, seg_start, seg_end) * cw_ref[2:3, :]
               + _shift_rows(dxc, dprev, dnext, -2, seg_start, seg_end) * cw_ref[3:4, :])
        dp_ref[:, 4 * RET_W:4 * RET_W + LRU_W] = dxr.astype(BF16)
        dp_ref[:, 4 * RET_W + LRU_W:] = dgate_ref[...].astype(BF16)
        xr, xp, xn = x_ref[...], xp_ref[...], xn_ref[...]
        for j, k in enumerate((-1, 0, 1, 2)):
            xs = xr if k == 0 else _shift_rows(xr, xp, xn, k, seg_start, seg_end)
            acc_ref[j:j + 1, :] += jnp.sum(dxc * xs, axis=0, keepdims=True)
        acc_ref[4:5, :] += jnp.sum(dxc, axis=0, keepdims=True)

    t = pl.BlockSpec((TM, RET_W), lambda s: (s, 0))
    return _pcall(
        body, name="assemble_dp", grid=(nb,),
        in_specs=[t] * 8 + _halo_specs(tile_of, n, 0) * 2 + _halo_specs(tile_of, n, 4)
        + [pl.BlockSpec((4, LRU_W), lambda s: (0, 0))],
        out_specs=(pl.BlockSpec((TM, 4 * RET_W + 2 * LRU_W), lambda s: (s, 0)), pl.BlockSpec((8, LRU_W), lambda s: (0, 0))),
        out_shape=(_sds((n, 4 * RET_W + 2 * LRU_W), BF16), _sds((8, LRU_W), F32)), compiler_params=_cp(),
    )(dqs[0], dqs[1], dks[0], dks[1], dvs[0], dvs[1], dg, dgate, dxcs[0], dxcs[0], dxcs[0], dxcs[1], dxcs[1], dxcs[1],
      p, p, p, cw)


def _sum_adam(parts, w, m, v, name):
    nparts, r, c = parts.shape
    tr = min(r, 128)
    bc1 = 1.0 - ADAM_B1 ** ADAM_STEP
    bc2 = 1.0 - ADAM_B2 ** ADAM_STEP

    def body(p_ref, w_ref, m_ref, v_ref, g_ref, d_ref, nm_ref, nv_ref):
        g = p_ref[0].astype(F32)
        for j in range(1, nparts):
            g = g + p_ref[j].astype(F32)
        nm = ADAM_B1 * m_ref[...] + (1.0 - ADAM_B1) * g
        nv = ADAM_B2 * v_ref[...] + (1.0 - ADAM_B2) * (g * g)
        g_ref[...] = g
        nm_ref[...] = nm
        nv_ref[...] = nv
        d_ref[...] = (-ADAM_LR) * ((nm / bc1) / (jnp.sqrt(nv / bc2) + ADAM_EPS) + ADAM_WD * w_ref[...])

    t = pl.BlockSpec((tr, c), lambda i: (i, 0))
    return _pcall(
        body, name=name, grid=(r // tr,),
        in_specs=[pl.BlockSpec((nparts, tr, c), lambda i: (0, i, 0)), t, t, t],
        out_specs=(t, t, t, t), out_shape=(_sds((r, c), F32),) * 4, compiler_params=_cp(),
    )(parts, w, m, v)


def _sum_parts(parts, name):
    nparts, r, c = parts.shape

    def body(p_ref, o_ref):
        g = p_ref[0]
        for j in range(1, nparts):
            g = g + p_ref[j]
        o_ref[...] = g

    return _pcall(body, name=name, out_shape=_sds((r, c), parts.dtype), compiler_params=_cp())(parts)


def _rot_tables(l_len, t_len):
    rows = t_len // GRID_W
    row = jnp.repeat(jnp.arange(rows, dtype=F32), GRID_W)
    col = jnp.tile(jnp.arange(GRID_W, dtype=F32), rows)
    n_freq = HEAD_DIM // 4
    inv = ROPE_BASE ** (-jnp.arange(n_freq, dtype=F32) / n_freq)
    ang = jnp.concatenate([row[:, None] * inv, col[:, None] * inv], axis=-1)
    cos, sin = jnp.cos(ang), jnp.sin(ang)
    cosf = jnp.concatenate([jnp.ones((l_len, HEAD_DIM), F32), jnp.concatenate([cos, cos], axis=-1)], axis=0)
    sins = jnp.concatenate([jnp.zeros((l_len, HEAD_DIM), F32), jnp.concatenate([-sin, sin], axis=-1)], axis=0)
    return cosf, sins


def _block_diag(w):
    eye = jnp.eye(LRU_BLOCKS, dtype=w.dtype)
    return (w[:, :, None, :] * eye[:, None, :, None]).reshape(LRU_W, LRU_W)


def _diag_blocks(mat):
    m4 = mat.reshape(LRU_BLOCKS, LRU_BD, LRU_BLOCKS, LRU_BD)
    return jnp.stack([m4[b, :, b, :] for b in range(LRU_BLOCKS)])


def _pack(fields):
    parts, offs, o = [], [], 0
    for f in fields:
        flat = f.reshape(-1).astype(F32)
        n = flat.shape[0]
        npad = -(-n // 1024) * 1024
        if npad > n:
            flat = jnp.concatenate([flat, jnp.zeros((npad - n,), F32)])
        parts.append(flat.reshape(-1, 128))
        offs.append((o, n))
        o += npad // 128
    if o % 128:
        parts.append(jnp.zeros((128 - o % 128, 128), F32))
    return jnp.concatenate(parts, axis=0), offs


def _unpack(packed, offs, shapes):
    return [packed[o:o + -(-n // 1024) * 8].reshape(-1)[:n].reshape(s) for (o, n), s in zip(offs, shapes)]


def _silu(x):
    return x * jax.nn.sigmoid(x)


def kernel(x, c, ctx, c_ctx, w_ada, b_ada, norm1_g, norm2_g, w_in, ret_decay, conv_w, conv_b, lru_wa, lru_ba, lru_wx, lru_bx, lru_lambda, w_out, w_mlp1, w_mlp2, final_g, loss_target, m_c_ctx, m_w_ada, m_b_ada, m_norm1_g, m_norm2_g, m_w_in, m_ret_decay, m_conv_w, m_conv_b, m_lru_wa, m_lru_ba, m_lru_wx, m_lru_bx, m_lru_lambda, m_w_out, m_w_mlp1, m_w_mlp2, m_final_g, v_c_ctx, v_w_ada, v_b_ada, v_norm1_g, v_norm2_g, v_w_in, v_ret_decay, v_conv_w, v_conv_b, v_lru_wa, v_lru_ba, v_lru_wx, v_lru_bx, v_lru_lambda, v_w_out, v_w_mlp1, v_w_mlp2, v_final_g):
    t_len, d = x.shape[1], x.shape[2]
    l_len = ctx.shape[1]
    cb, cc = l_len // TM, l_len // CHUNK
    me = 4 * lax.axis_index("x") + 2 * lax.axis_index("y") + lax.axis_index("c")
    x2d, ctx2d, tgt2d = x[0], ctx[0], loss_target[0]
    ada_cols = w_ada.shape[2]
    wa2d = w_ada[0]

    sc_loc = conv_w.shape[2]
    pack_a = jnp.zeros((8, d), F32)
    pack_a = pack_a.at[0].set(_silu(c[0]))
    pack_a = pack_a.at[1, :4 * sc_loc].set(conv_w[0].reshape(-1))
    pack_a = pack_a.at[2, :2 * sc_loc].set(lru_ba[0].reshape(-1))
    pack_a = pack_a.at[3, :2 * sc_loc].set(lru_bx[0].reshape(-1))
    pack_a = pack_a.at[4, :2 * sc_loc].set(lru_lambda[0].reshape(-1))
    all_a = _all_gather(pack_a, "gather_small_in")
    s16 = jnp.zeros((16, d), F32).at[0:8].set(all_a[:, 0, :]).at[8].set(_silu(c_ctx))

    def unshard(row, k):
        return all_a[:, row, :k * sc_loc].reshape(N_DEV, k, sc_loc).transpose(1, 0, 2).reshape(k, N_DEV * sc_loc)

    conv_w_full = unshard(1, 4)
    ba_full, bx_full, lam_full = unshard(2, 2), unshard(3, 2), unshard(4, 2)

    b_cols = lax.dynamic_slice(b_ada, (0, me * ada_cols), (1, ada_cols))
    mod_parts = _all_gather(_mod_part(s16, wa2d, b_cols), "gather_mod")
    mod_all = mod_parts.transpose(1, 0, 2).reshape(16, N_DEV * ada_cols)
    mod_me = lax.dynamic_slice(mod_all, (me, 0), (1, 6 * d)).reshape(6, d)
    mod_c = mod_all[8].reshape(6, d)
    modrows = jnp.concatenate([mod_c[0:2], mod_me], axis=0)

    win_g = _all_gather(w_in[0].astype(BF16), "gather_w_in")
    wout_g = _all_gather(w_out[0].astype(BF16), "gather_w_out").reshape(1, d, d)
    w1_g = _all_gather(w_mlp1[0].astype(BF16), "gather_w_mlp1")
    w2_g = _all_gather(w_mlp2[0].astype(BF16), "gather_w_mlp2").reshape(1, 4 * d, d)

    lg = jax.nn.log_sigmoid(ret_decay[0])
    sp = jax.nn.softplus(-lam_full)
    wg = [jnp.concatenate([_block_diag(lru_wa[0, dd]), _block_diag(lru_wx[0, dd])], axis=1).astype(BF16) for dd in (0, 1)]
    bg = [jnp.concatenate([ba_full[dd], bx_full[dd]])[None, :] for dd in (0, 1)]
    cosf, sins = _rot_tables(l_len, t_len)

    hn, hnt = _norm1_fwd(ctx2d, x2d, norm1_g, modrows, cb)
    p = _mm_nn(hn, win_g, F32, "mm_in")
    h, cin = [None, None], [None, None]
    o0, o1, sp0, sp1 = _ret_fwd(p, cosf, sins, lg, cc)
    o, s_prev = [o0, o1], [sp0, sp1]
    for dd in (0, 1):
        h[dd], cin[dd] = _lru_fwd(p, wg[dd], bg[dd], sp[dd:dd + 1], conv_w_full, conv_b, dd, cb)
    mix = _mix_fwd(o[0], o[1], p, h[0], h[1], cb, t_len)
    y = _mm_nn(mix, wout_g, F32, "mm_out")
    x1, h2, h2t = _res_norm2(x2d, y, norm2_g, modrows)
    r = _mm_nn(h2, w1_g, BF16, "mm_mlp1", relu_out=True)
    z = _mm_nn(r, w2_g, F32, "mm_mlp2", square_lhs=True)
    dx2, dz, dzt, facc = _final_fwd_bwd(x1, z, tgt2d, final_g[None, :], modrows)
    loss = lax.psum(0.5 / d * jnp.sum(facc[2]), AXES)

    du = _mm_nt(dz, w2_g, BF16, "mm_da2", relu_mul=r)
    gw2 = _mm_wgrad(dzt, r, "mm_dw2", w2_g.shape[1] // N_DEV, BF16, transpose_out=True, square_rhs=True)
    gw1 = _mm_wgrad(h2t, du, "mm_dw1", w1_g.shape[2], BF16)
    dh2 = _mm_nt(du, w1_g, F32, "mm_dh2")
    dx1, dy, n2acc = _bwd_norm2(dh2, x1, dx2, y, norm2_g, modrows)
    dmix = _mm_nt(dy, wout_g, F32, "mm_dmix")
    gwo = _mm_tn(mix, dy, "mm_dwout", False, d, BF16, 512).reshape(N_DEV, -1, d)
    do, dg, dgate, dhs = _mix_bwd(dmix, o[0], o[1], p, h[0], h[1], cb)
    dxcs, dwgs, laccs = [], [], []
    dq0, dk0, dv0, dq1, dk1, dv1, dlg_lanes = _ret_bwd(p, cosf, sins, lg, do, s_prev, cc)
    dqs, dks, dvs = [dq0, dq1], [dk0, dk1], [dv0, dv1]
    for dd in (0, 1):
        dxc_, dwg_, lacc_ = _lru_bwd(p, wg[dd], bg[dd], sp[dd:dd + 1], conv_w_full, conv_b, h[dd], cin[dd], dhs, dd, cb)
        dxcs.append(dxc_); dwgs.append(dwg_); laccs.append(lacc_)
    dp, cacc = _assemble_dp(dqs, dks, dvs, dg, dgate, dxcs, p, conv_w_full, cb)
    gwi = _mm_wgrad(hnt, dp, "mm_dwin", win_g.shape[2], BF16)
    dhn = _mm_nt(dp, win_g, F32, "mm_dhn")
    grad_x, n1acc = _bwd_norm1(dhn, ctx2d, x2d, dx1, norm1_g, modrows, cb)

    dlg = jnp.sum(dlg_lanes[:, 0, :], axis=-1).reshape(2, HEADS)
    d_wa = jnp.stack([_diag_blocks(dwgs[dd][:, :LRU_W]) for dd in (0, 1)])
    d_wx = jnp.stack([_diag_blocks(dwgs[dd][:, LRU_W:]) for dd in (0, 1)])
    d_ba = jnp.stack([laccs[dd][0, :LRU_W] for dd in (0, 1)])
    d_bx = jnp.stack([laccs[dd][0, LRU_W:] for dd in (0, 1)])
    d_sp = jnp.stack([laccs[dd][1, :LRU_W] for dd in (0, 1)])
    fields = [n1acc[0:5], n2acc[0:4], facc[0:2], cacc[0:5], dlg, d_wa, d_wx, d_ba, d_bx, d_sp]
    shapes = [f.shape for f in fields]
    pack_b, offs = _pack(fields)
    all_b = _all_gather(pack_b, "gather_small_grads")
    tot = _unpack(_sum_parts(all_b, "sum_small_grads"), offs, shapes)
    t_n1, t_n2, t_f, t_conv, t_dlg, t_wa, t_wx, t_ba, t_bx, t_sp = tot
    per_dev = [_unpack(all_b[j], offs[:3], shapes[:3]) for j in range(N_DEV)]
    dm_rows = [jnp.concatenate([a1[2], a1[3], a2[3], a2[0], a2[1], af[1]]) for a1, a2, af in per_dev]
    dm_c = jnp.concatenate([t_n1[0], t_n1[1], jnp.zeros((4 * d,), F32)])
    dm16 = jnp.zeros((16, 6 * d), F32).at[0:8].set(jnp.stack(dm_rows)).at[8].set(dm_c)
    g_b_ada = jnp.sum(dm16, axis=0)[None, :]
    dm_cols = lax.dynamic_slice(dm16, (0, me * ada_cols), (16, ada_cols))
    g_w_ada, ds16 = _ada_bwd(s16, dm_cols, wa2d)
    ds_all = _all_gather(ds16[8:16], "gather_dsilu")
    dsilu_cc = _sum_parts(ds_all, "sum_dsilu")[0]
    sg_cc = jax.nn.sigmoid(c_ctx)
    g_c_ctx = dsilu_cc * (sg_cc * (1.0 + c_ctx * (1.0 - sg_cc)))

    g_ret_decay = (t_dlg * jax.nn.sigmoid(-ret_decay[0]))[None]
    g_lambda_full = -t_sp * jax.nn.sigmoid(-lam_full)

    def my_cols(full):
        return lax.dynamic_slice(full, (0, me * sc_loc), (full.shape[0], sc_loc))[None]

    small_g = dict(
        c_ctx=g_c_ctx, b_ada=g_b_ada, norm1_g=t_n1[4][None], norm2_g=t_n2[2][None], ret_decay=g_ret_decay,
        conv_w=my_cols(t_conv[0:4]), conv_b=t_conv[4][None], lru_wa=t_wa[None], lru_ba=my_cols(t_ba),
        lru_wx=t_wx[None], lru_bx=my_cols(t_bx), lru_lambda=my_cols(g_lambda_full), final_g=t_f[0])
    small_w = dict(c_ctx=c_ctx, b_ada=b_ada, norm1_g=norm1_g, norm2_g=norm2_g, ret_decay=ret_decay, conv_w=conv_w,
                   conv_b=conv_b, lru_wa=lru_wa, lru_ba=lru_ba, lru_wx=lru_wx, lru_bx=lru_bx, lru_lambda=lru_lambda,
                   final_g=final_g)
    small_m = dict(c_ctx=m_c_ctx, b_ada=m_b_ada, norm1_g=m_norm1_g, norm2_g=m_norm2_g, ret_decay=m_ret_decay,
                   conv_w=m_conv_w, conv_b=m_conv_b, lru_wa=m_lru_wa, lru_ba=m_lru_ba, lru_wx=m_lru_wx,
                   lru_bx=m_lru_bx, lru_lambda=m_lru_lambda, final_g=m_final_g)
    small_v = dict(c_ctx=v_c_ctx, b_ada=v_b_ada, norm1_g=v_norm1_g, norm2_g=v_norm2_g, ret_decay=v_ret_decay,
                   conv_w=v_conv_w, conv_b=v_conv_b, lru_wa=v_lru_wa, lru_ba=v_lru_ba, lru_wx=v_lru_wx,
                   lru_bx=v_lru_bx, lru_lambda=v_lru_lambda, final_g=v_final_g)
    names = list(small_w)
    sshapes = [small_w[k].shape for k in names]
    pg, soffs = _pack([small_g[k].reshape(small_w[k].shape) for k in names])
    pw, _ = _pack([small_w[k] for k in names])
    pm, _ = _pack([small_m[k] for k in names])
    pv, _ = _pack([small_v[k] for k in names])
    sg_, sd_, sm_, sv_ = _sum_adam(pg[None], pw, pm, pv, "adam_small")
    res = {}
    for k, g_, d_, m_, v_ in zip(names, *[_unpack(a, soffs, sshapes) for a in (sg_, sd_, sm_, sv_)]):
        res[k] = (g_, d_, m_, v_)

    def big(parts, w, m, v, name):
        out = _sum_adam(parts, w[0], m[0], v[0], name)
        return tuple(a[None] for a in out)

    res["w_ada"] = big(g_w_ada[None], w_ada, m_w_ada, v_w_ada, "adam_w_ada")
    res["w_in"] = big(_all_to_all(gwi, "a2a_w_in"), w_in, m_w_in, v_w_in, "adam_w_in")
    res["w_out"] = big(_all_to_all(gwo, "a2a_w_out"), w_out, m_w_out, v_w_out, "adam_w_out")
    res["w_mlp1"] = big(_all_to_all(gw1, "a2a_w_mlp1"), w_mlp1, m_w_mlp1, v_w_mlp1, "adam_w_mlp1")
    res["w_mlp2"] = big(_all_to_all(gw2, "a2a_w_mlp2"), w_mlp2, m_w_mlp2, v_w_mlp2, "adam_w_mlp2")

    order = ["c_ctx", "w_ada", "b_ada", "norm1_g", "norm2_g", "w_in", "ret_decay", "conv_w", "conv_b", "lru_wa", "lru_ba",
             "lru_wx", "lru_bx", "lru_lambda", "w_out", "w_mlp1", "w_mlp2", "final_g"]
    outs = [loss, grad_x[None]]
    for j in range(4):
        outs += [res[k][j] for k in order]
    return tuple(outs)
```

```python
import jax
import jax.numpy as jnp
from jax import lax
from jax.experimental import pallas as pl
from jax.experimental.pallas import tpu as pltpu

F32 = jnp.float32
BF16 = jnp.bfloat16
AXES = ("x", "y", "c")
N_DEV = 8
MESH = pl.DeviceIdType.MESH

HEADS = 4
HEAD_DIM = 128
CHUNK = 128
RET_W = HEADS * HEAD_DIM
LRU_W = 512
LRU_BLOCKS = 8
LRU_BD = LRU_W // LRU_BLOCKS
LRU_C = 8.0
EPS = 1e-6
K_SCALE = HEAD_DIM ** -0.5
ROPE_BASE = 10000.0
GRID_W = 64
TM = 256
SUB = 8

ADAM_LR = 0.001
ADAM_B1 = 0.9
ADAM_B2 = 0.999
ADAM_EPS = 1e-08
ADAM_WD = 0.01
ADAM_STEP = 10

R_CSH1, R_CSC1, R_SH1, R_SC1, R_G1, R_SH2, R_SC2, R_G2 = range(8)


def _pcall(body, **kw):
    return pl.pallas_call(body, **kw)


def _cp(vmem_mb=48):
    return pltpu.CompilerParams(vmem_limit_bytes=vmem_mb << 20)


def _sds(shape, dtype):
    return jax.ShapeDtypeStruct(shape, dtype)


def _dot(a, b):
    return jnp.dot(a.astype(BF16), b.astype(BF16), preferred_element_type=F32)


def _dot_nt(a, b):
    return lax.dot_general(a.astype(BF16), b.astype(BF16), (((1,), (1,)), ((), ())), preferred_element_type=F32)


def _dot_tn(a, b):
    return lax.dot_general(a.astype(BF16), b.astype(BF16), (((0,), (0,)), ((), ())), preferred_element_type=F32)


def _sigmoid(x):
    return jax.nn.sigmoid(x)


def _gelu(x):
    return 0.5 * x * (1.0 + jnp.tanh(0.7978845608028654 * (x + 0.044715 * x * x * x)))


def _dgelu(x):
    t = jnp.tanh(0.7978845608028654 * (x + 0.044715 * x * x * x))
    return 0.5 * (1.0 + t) + 0.5 * x * (1.0 - t * t) * 0.7978845608028654 * (1.0 + 3.0 * 0.044715 * x * x)


def _one_minus_exp(x):
    p = x * (1.0 + x * (1.0 / 2) * (1.0 + x * (1.0 / 3) * (1.0 + x * (1.0 / 4) * (1.0 + x * (1.0 / 5) * (
        1.0 + x * (1.0 / 6) * (1.0 + x * (1.0 / 7) * (1.0 + x * (1.0 / 8))))))))
    return -jnp.where(jnp.abs(x) < 0.3, p, jnp.exp(x) - 1.0)


def _rows_iota(shape):
    return lax.broadcasted_iota(jnp.int32, shape, 0)


def _tile_order(dirn, s, cb, nb):
    if dirn == 0:
        return s
    return jnp.where(s < cb, cb - 1 - s, nb - 1 - (s - cb))


_SEMS = [pltpu.SemaphoreType.DMA((7,)), pltpu.SemaphoreType.DMA((7,)), pltpu.SemaphoreType.DMA(())]
_ANY = pl.BlockSpec(memory_space=pl.ANY)


def _gather_copies(x_ref, out_ref, send_sems, recv_sems, local_sem):
    mx, my, mc = lax.axis_index("x"), lax.axis_index("y"), lax.axis_index("c")
    me, sibling = (mx, my, mc), (mx, my, 1 - mc)
    chips = [(1 - mx, my), (mx, 1 - my), (1 - mx, 1 - my)]

    def slot(px, py, pc):
        return out_ref.at[4 * px + 2 * py + pc]

    def copy(k, block, to, src=None):
        return pltpu.make_async_remote_copy(
            src_ref=slot(*block) if src is None else src, dst_ref=slot(*block),
            send_sem=send_sems.at[k], recv_sem=recv_sems.at[k], device_id=to, device_id_type=MESH)

    mine = pltpu.make_async_copy(x_ref, slot(*me), local_sem)
    first = [copy(0, me, sibling, src=x_ref)] + [copy(1 + j, me, (*chip, mc), src=x_ref) for j, chip in enumerate(chips)]
    passed = [copy(4 + j, (*chip, mc), sibling) for j, chip in enumerate(chips)]
    recv_ici = [copy(1 + j, (*chip, mc), me) for j, chip in enumerate(chips)]
    recv_d2d = [copy(0, sibling, me)] + [copy(4 + j, (*chip, 1 - mc), me) for j, chip in enumerate(chips)]
    return mine, first, passed, recv_ici, recv_d2d


def _gather_start(*refs):
    mine, first, _, _, _ = _gather_copies(*refs)
    mine.start()
    for cp in first:
        cp.start()


def _gather_finish(*refs):
    mine, first, passed, recv_ici, recv_d2d = _gather_copies(*refs)
    for landed, onward in zip(recv_ici, passed):
        landed.wait_recv()
        onward.start()
    for landed in recv_d2d:
        landed.wait_recv()
    for cp in first + passed:
        cp.wait_send()
    mine.wait()


def _a2a_copies(g_ref, out_ref, send_sems, recv_sems, local_sem):
    mx, my, mc = lax.axis_index("x"), lax.axis_index("y"), lax.axis_index("c")
    me = 4 * mx + 2 * my + mc
    mine = pltpu.make_async_copy(g_ref.at[me], out_ref.at[me], local_sem)
    copies = []
    for k in range(1, N_DEV):
        px = 1 - mx if (k >> 2) & 1 else mx
        py = 1 - my if (k >> 1) & 1 else my
        pc = 1 - mc if k & 1 else mc
        copies.append(pltpu.make_async_remote_copy(
            src_ref=g_ref.at[4 * px + 2 * py + pc], dst_ref=out_ref.at[me],
            send_sem=send_sems.at[k - 1], recv_sem=recv_sems.at[k - 1],
            device_id=(px, py, pc), device_id_type=MESH))
    return mine, copies


def _a2a_start(*refs):
    mine, copies = _a2a_copies(*refs)
    mine.start()
    for cp in copies:
        cp.start()


def _a2a_finish(*refs):
    mine, copies = _a2a_copies(*refs)
    for cp in copies:
        cp.wait()
    mine.wait()


_EXCHANGES = {"gather": (_gather_start, _gather_finish), "a2a": (_a2a_start, _a2a_finish)}


def _exchange_shape(kind, src):
    return _sds((N_DEV,) + src.shape if kind == "gather" else src.shape, src.dtype)


def _all_gather(x, name):
    def body(x_ref, out_ref, *sems):
        _gather_start(x_ref, out_ref, *sems)
        _gather_finish(x_ref, out_ref, *sems)

    return _pcall(body, name=name, out_shape=_exchange_shape("gather", x), in_specs=[_ANY], out_specs=_ANY,
                  scratch_shapes=list(_SEMS))(x)


def _pcall_ride(body, ride, args, *, name, grid, in_specs, out_specs, out_shape, scratch_shapes=(), compiler_params=None):
    if ride is None:
        out = _pcall(body, name=name, grid=grid, in_specs=in_specs, out_specs=out_specs, out_shape=out_shape,
                     scratch_shapes=list(scratch_shapes), compiler_params=compiler_params)(*args)
        return out, None
    kind, src = ride
    start, finish = _EXCHANGES[kind]
    single = not isinstance(out_shape, (tuple, list))
    out_specs_t = (out_specs,) if single else tuple(out_specs)
    out_shape_t = (out_shape,) if single else tuple(out_shape)
    n_in, n_out, n_sc = len(in_specs), len(out_shape_t), len(scratch_shapes)

    def wrapped(*refs):
        ins, src_ref = refs[:n_in], refs[n_in]
        outs, dst_ref = refs[n_in + 1:n_in + 1 + n_out], refs[n_in + 1 + n_out]
        scratch = refs[n_in + 2 + n_out:n_in + 2 + n_out + n_sc]
        sems = refs[n_in + 2 + n_out + n_sc:]
        first = pl.program_id(0) == 0
        last = pl.program_id(0) == grid[0] - 1
        for ax in range(1, len(grid)):
            first = jnp.logical_and(first, pl.program_id(ax) == 0)
            last = jnp.logical_and(last, pl.program_id(ax) == grid[ax] - 1)

        @pl.when(first)
        def _():
            start(src_ref, dst_ref, *sems)

        body(*ins, *outs, *scratch)

        @pl.when(last)
        def _():
            finish(src_ref, dst_ref, *sems)

    res = _pcall(wrapped, name=name, grid=grid, in_specs=list(in_specs) + [_ANY], out_specs=out_specs_t + (_ANY,),
                 out_shape=out_shape_t + (_exchange_shape(kind, src),),
                 scratch_shapes=list(scratch_shapes) + list(_SEMS), compiler_params=compiler_params)(*args, src)
    return (res[0] if single else tuple(res[:-1])), res[-1]


def _mm_nn(a, w, out_dtype, name, square_lhs=False, relu_out=False, ride=None):
    m, k = a.shape
    nb, _, bn = w.shape

    def body(a_ref, w_ref, o_ref):
        av = a_ref[...]
        if square_lhs:
            av = av * av
        for j in range(nb):
            r = jnp.dot(av, w_ref[j], preferred_element_type=F32)
            if relu_out:
                r = jnp.maximum(r, 0.0)
            o_ref[:, j * bn:(j + 1) * bn] = r.astype(out_dtype)

    out, ex = _pcall_ride(
        body, ride, (a, w), name=name, grid=(m // TM,),
        in_specs=[pl.BlockSpec((TM, k), lambda i: (i, 0)), pl.BlockSpec((nb, k, bn), lambda i: (0, 0, 0))],
        out_specs=pl.BlockSpec((TM, nb * bn), lambda i: (i, 0)),
        out_shape=_sds((m, nb * bn), out_dtype), compiler_params=_cp())
    return out if ride is None else (out, ex)


def _mm_nt(dy, w, out_dtype, name, relu_mul=None, ride=None):
    m = dy.shape[0]
    nb, k, bn = w.shape

    def body(*refs):
        if relu_mul is None:
            dy_ref, w_ref, o_ref = refs
        else:
            dy_ref, w_ref, r_ref, o_ref = refs
        acc = _dot_nt(dy_ref[:, 0:bn], w_ref[0])
        for j in range(1, nb):
            acc = acc + _dot_nt(dy_ref[:, j * bn:(j + 1) * bn], w_ref[j])
        if relu_mul is not None:
            acc = acc * (2.0 * r_ref[...].astype(F32))
        o_ref[...] = acc.astype(out_dtype)

    in_specs = [pl.BlockSpec((TM, nb * bn), lambda i: (i, 0)), pl.BlockSpec((nb, k, bn), lambda i: (0, 0, 0))]
    args = [dy, w]
    if relu_mul is not None:
        in_specs.append(pl.BlockSpec((TM, k), lambda i: (i, 0)))
        args.append(relu_mul)
    out, ex = _pcall_ride(
        body, ride, args, name=name, grid=(m // TM,), in_specs=in_specs,
        out_specs=pl.BlockSpec((TM, k), lambda i: (i, 0)),
        out_shape=_sds((m, k), out_dtype), compiler_params=_cp())
    return out if ride is None else (out, ex)


def _mm_tn(a, b, name, col_blocks, block, out_dtype, tm, square_lhs=False):
    m, k = a.shape
    nn = b.shape[1]
    steps = m // tm
    if col_blocks:
        nblk, acc_shape = nn // block, (k, block)
        a_spec = pl.BlockSpec((tm, k), lambda j, s: (s, 0))
        b_spec = pl.BlockSpec((tm, block), lambda j, s: (s, j))
    else:
        nblk, acc_shape = k // block, (block, nn)
        a_spec = pl.BlockSpec((tm, block), lambda j, s: (s, j))
        b_spec = pl.BlockSpec((tm, nn), lambda j, s: (s, 0))

    def body(a_ref, b_ref, o_ref, acc):
        s = pl.program_id(1)

        @pl.when(s == 0)
        def _():
            acc[...] = jnp.zeros_like(acc)

        av = a_ref[...]
        if square_lhs:
            av = av.astype(F32)
            av = (av * av).astype(BF16)
        acc[...] += _dot_tn(av, b_ref[...])

        @pl.when(s == steps - 1)
        def _():
            o_ref[...] = acc[...].astype(out_dtype)

    return _pcall(
        body, name=name, grid=(nblk, steps), in_specs=[a_spec, b_spec],
        out_specs=pl.BlockSpec((None,) + acc_shape, lambda j, s: (j, 0, 0)),
        out_shape=_sds((nblk,) + acc_shape, out_dtype),
        scratch_shapes=[pltpu.VMEM(acc_shape, F32)], compiler_params=_cp(),
    )(a, b)


def _mm_wgrad(at, b, name, bn, out_dtype, transpose_out=False, square_rhs=False):
    k, m = at.shape
    nblk = b.shape[1] // bn

    def body(a_ref, b_ref, o_ref):
        bv = b_ref[...]
        if square_rhs:
            bv = bv * bv
        r = jnp.dot(a_ref[...], bv, preferred_element_type=F32)
        o_ref[...] = (r.T if transpose_out else r).astype(out_dtype)

    out_block = (bn, k) if transpose_out else (k, bn)
    return _pcall(
        body, name=name, grid=(nblk,),
        in_specs=[pl.BlockSpec((k, m), lambda j: (0, 0)), pl.BlockSpec((m, bn), lambda j: (0, j))],
        out_specs=pl.BlockSpec((None,) + out_block, lambda j: (j, 0, 0)),
        out_shape=_sds((nblk,) + out_block, out_dtype), compiler_params=_cp(),
    )(at, b)


def _mod_part(s16, w_ada, b_cols):
    def body(s_ref, w_ref, b_ref, o_ref):
        o_ref[...] = _dot(s_ref[...], w_ref[...]) + b_ref[...]

    return _pcall(body, name="mod_part", out_shape=_sds((s16.shape[0], w_ada.shape[1]), F32),
                  compiler_params=_cp())(s16, w_ada, b_cols)


def _ada_bwd(s16, dm_cols, w_ada):
    def body(s_ref, d_ref, w_ref, gw_ref, ds_ref):
        gw_ref[...] = _dot_tn(s_ref[...], d_ref[...])
        ds_ref[...] = _dot_nt(d_ref[...], w_ref[...])

    return _pcall(body, name="ada_bwd",
                  out_shape=(_sds(w_ada.shape, F32), _sds(s16.shape, F32)), compiler_params=_cp())(s16, dm_cols, w_ada)


def _norm1_fwd(ctx, x, g, modrows, cb, ride=None):
    l_len, d = ctx.shape
    nb = (l_len + x.shape[0]) // TM

    def body(ctx_ref, x_ref, g_ref, m_ref, o_ref, ot_ref):
        is_ctx = pl.program_id(0) < cb
        xin = jnp.where(is_ctx, ctx_ref[...], x_ref[...])
        sh = jnp.where(is_ctx, m_ref[R_CSH1:R_CSH1 + 1, :], m_ref[R_SH1:R_SH1 + 1, :])
        sc = jnp.where(is_ctx, m_ref[R_CSC1:R_CSC1 + 1, :], m_ref[R_SC1:R_SC1 + 1, :])
        ms = jnp.mean(xin * xin, axis=-1, keepdims=True)
        n = xin * lax.rsqrt(ms + EPS) * g_ref[...]
        hn = n * (1.0 + sc) + sh
        o_ref[...] = hn.astype(BF16)
        ot_ref[...] = hn.T.astype(BF16)

    return _pcall_ride(
        body, ride, (ctx, x, g, modrows), name="norm1_fwd", grid=(nb,),
        in_specs=[pl.BlockSpec((TM, d), lambda i: (jnp.minimum(i, cb - 1), 0)),
                  pl.BlockSpec((TM, d), lambda i: (jnp.maximum(i - cb, 0), 0)),
                  pl.BlockSpec((1, d), lambda i: (0, 0)), pl.BlockSpec((8, d), lambda i: (0, 0))],
        out_specs=(pl.BlockSpec((TM, d), lambda i: (i, 0)), pl.BlockSpec((d, TM), lambda i: (0, i))),
        out_shape=(_sds((nb * TM, d), BF16), _sds((d, nb * TM), BF16)), compiler_params=_cp())


def _mix_fwd(o_f, o_b, p, h_f, h_b, cb, t_len):
    def body(of_ref, ob_ref, g_ref, gate_ref, hf_ref, hb_ref, mix_ref):
        o = of_ref[...] + ob_ref[...]
        g = g_ref[...]
        sg = g * _sigmoid(g)
        for hh in range(HEADS):
            sl = slice(hh * HEAD_DIM, (hh + 1) * HEAD_DIM)
            oh = o[:, sl]
            yc = oh - jnp.mean(oh, axis=-1, keepdims=True)
            var = jnp.mean(yc * yc, axis=-1, keepdims=True)
            mix_ref[:, sl] = (sg[:, sl] * (yc * lax.rsqrt(var + EPS))).astype(BF16)
        mix_ref[:, RET_W:] = ((hf_ref[...] + hb_ref[...]) * _gelu(gate_ref[...])).astype(BF16)

    row = lambda i: (i + cb, 0)
    return _pcall(
        body, name="mix_fwd", grid=(t_len // TM,),
        in_specs=[pl.BlockSpec((TM, RET_W), row), pl.BlockSpec((TM, RET_W), row),
                  pl.BlockSpec((TM, RET_W), lambda i: (i + cb, 3)), pl.BlockSpec((TM, LRU_W), lambda i: (i + cb, 5)),
                  pl.BlockSpec((TM, LRU_W), row), pl.BlockSpec((TM, LRU_W), row)],
        out_specs=pl.BlockSpec((TM, RET_W + LRU_W), lambda i: (i, 0)),
        out_shape=_sds((t_len, RET_W + LRU_W), BF16), compiler_params=_cp(),
    )(o_f, o_b, p, p, h_f, h_b)


def _res_norm2(x, y, g, modrows):
    t_len, d = x.shape

    def body(x_ref, y_ref, g_ref, m_ref, x1_ref, h2_ref, h2t_ref):
        x1 = x_ref[...] + m_ref[R_G1:R_G1 + 1, :] * y_ref[...]
        ms = jnp.mean(x1 * x1, axis=-1, keepdims=True)
        n = x1 * lax.rsqrt(ms + EPS) * g_ref[...]
        x1_ref[...] = x1
        h2 = n * (1.0 + m_ref[R_SC2:R_SC2 + 1, :]) + m_ref[R_SH2:R_SH2 + 1, :]
        h2_ref[...] = h2.astype(BF16)
        h2t_ref[...] = h2.T.astype(BF16)

    t = pl.BlockSpec((TM, d), lambda i: (i, 0))
    tt = pl.BlockSpec((d, TM), lambda i: (0, i))
    return _pcall(
        body, name="res_norm2", grid=(t_len // TM,),
        in_specs=[t, t, pl.BlockSpec((1, d), lambda i: (0, 0)), pl.BlockSpec((8, d), lambda i: (0, 0))],
        out_specs=(t, t, tt), out_shape=(_sds((t_len, d), F32), _sds((t_len, d), BF16), _sds((d, t_len), BF16)),
        compiler_params=_cp(),
    )(x, y, g, modrows)


def _final_fwd_bwd(x1, z, target, fg, modrows):
    t_len, d = x1.shape

    def body(x1_ref, z_ref, t_ref, fg_ref, m_ref, dx2_ref, dz_ref, dzt_ref, acc_ref):
        @pl.when(pl.program_id(0) == 0)
        def _():
            acc_ref[...] = jnp.zeros_like(acc_ref)

        g2 = m_ref[R_G2:R_G2 + 1, :]
        z = z_ref[...]
        x2 = x1_ref[...] + g2 * z
        rstd = lax.rsqrt(jnp.mean(x2 * x2, axis=-1, keepdims=True) + EPS)
        xh = x2 * rstd
        fg = fg_ref[...]
        e = xh * fg - t_ref[...]
        dy = e * (1.0 / d)
        dxh = dy * fg
        dx2 = rstd * (dxh - xh * jnp.mean(dxh * xh, axis=-1, keepdims=True))
        dx2_ref[...] = dx2
        dz = g2 * dx2
        dz_ref[...] = dz.astype(BF16)
        dzt_ref[...] = dz.T.astype(BF16)
        acc_ref[0:1, :] += jnp.sum(dy * xh, axis=0, keepdims=True)
        acc_ref[1:2, :] += jnp.sum(dx2 * z, axis=0, keepdims=True)
        acc_ref[2:3, :] += jnp.sum(e * e, axis=0, keepdims=True)

    t = pl.BlockSpec((TM, d), lambda i: (i, 0))
    return _pcall(
        body, name="final_fwd_bwd", grid=(t_len // TM,),
        in_specs=[t, t, t, pl.BlockSpec((1, d), lambda i: (0, 0)), pl.BlockSpec((8, d), lambda i: (0, 0))],
        out_specs=(t, t, pl.BlockSpec((d, TM), lambda i: (0, i)), pl.BlockSpec((8, d), lambda i: (0, 0))),
        out_shape=(_sds((t_len, d), F32), _sds((t_len, d), BF16), _sds((d, t_len), BF16), _sds((8, d), F32)),
        compiler_params=_cp(),
    )(x1, z, target, fg, modrows)


def _bwd_norm2(dh2, x1, dx2, y, g, modrows):
    t_len, d = x1.shape

    def body(dh2_ref, x1_ref, dx2_ref, y_ref, g_ref, m_ref, dx1_ref, dy_ref, acc_ref):
        @pl.when(pl.program_id(0) == 0)
        def _():
            acc_ref[...] = jnp.zeros_like(acc_ref)

        x1 = x1_ref[...]
        rstd = lax.rsqrt(jnp.mean(x1 * x1, axis=-1, keepdims=True) + EPS)
        xh = x1 * rstd
        gn = g_ref[...]
        dh2 = dh2_ref[...]
        dn = dh2 * (1.0 + m_ref[R_SC2:R_SC2 + 1, :])
        dxh = dn * gn
        dx1 = dx2_ref[...] + rstd * (dxh - xh * jnp.mean(dxh * xh, axis=-1, keepdims=True))
        dx1_ref[...] = dx1
        dy_ref[...] = (m_ref[R_G1:R_G1 + 1, :] * dx1).astype(BF16)
        acc_ref[0:1, :] += jnp.sum(dh2, axis=0, keepdims=True)
        acc_ref[1:2, :] += jnp.sum(dh2 * xh * gn, axis=0, keepdims=True)
        acc_ref[2:3, :] += jnp.sum(dn * xh, axis=0, keepdims=True)
        acc_ref[3:4, :] += jnp.sum(dx1 * y_ref[...], axis=0, keepdims=True)

    t = pl.BlockSpec((TM, d), lambda i: (i, 0))
    return _pcall(
        body, name="bwd_norm2", grid=(t_len // TM,),
        in_specs=[t, t, t, t, pl.BlockSpec((1, d), lambda i: (0, 0)), pl.BlockSpec((8, d), lambda i: (0, 0))],
        out_specs=(t, t, pl.BlockSpec((8, d), lambda i: (0, 0))),
        out_shape=(_sds((t_len, d), F32), _sds((t_len, d), BF16), _sds((8, d), F32)), compiler_params=_cp(),
    )(dh2, x1, dx2, y, g, modrows)


def _mix_bwd(dmix, o_f, o_b, p, h_f, h_b, cb):
    n = o_f.shape[0]

    def body(dm_ref, of_ref, ob_ref, g_ref, gate_ref, hf_ref, hb_ref, do_ref, dg_ref, dgate_ref, dhs_ref):
        is_ctx = pl.program_id(0) < cb

        @pl.when(is_ctx)
        def _():
            for r in (do_ref, dg_ref, dgate_ref, dhs_ref):
                r[...] = jnp.zeros_like(r)

        @pl.when(jnp.logical_not(is_ctx))
        def _():
            o = of_ref[...] + ob_ref[...]
            g = g_ref[...]
            s = _sigmoid(g)
            sg = g * s
            dsg = s * (1.0 + g * (1.0 - s))
            for hh in range(HEADS):
                sl = slice(hh * HEAD_DIM, (hh + 1) * HEAD_DIM)
                oh = o[:, sl]
                yc = oh - jnp.mean(oh, axis=-1, keepdims=True)
                rs = lax.rsqrt(jnp.mean(yc * yc, axis=-1, keepdims=True) + EPS)
                gn = yc * rs
                dret = dm_ref[:, sl]
                dgn = dret * sg[:, sl]
                dg_ref[:, sl] = dret * gn * dsg[:, sl]
                do_ref[:, sl] = rs * (dgn - jnp.mean(dgn, axis=-1, keepdims=True)
                                      - gn * jnp.mean(dgn * gn, axis=-1, keepdims=True))
            dlru = dm_ref[:, RET_W:]
            gate = gate_ref[...]
            dhs_ref[...] = dlru * _gelu(gate)
            dgate_ref[...] = dlru * (hf_ref[...] + hb_ref[...]) * _dgelu(gate)

    row = lambda i: (i, 0)
    t = pl.BlockSpec((TM, RET_W), row)
    return _pcall(
        body, name="mix_bwd", grid=(n // TM,),
        in_specs=[pl.BlockSpec((TM, RET_W + LRU_W), lambda i: (jnp.maximum(i - cb, 0), 0)), t, t,
                  pl.BlockSpec((TM, RET_W), lambda i: (i, 3)), pl.BlockSpec((TM, LRU_W), lambda i: (i, 5)), t, t],
        out_specs=(t, t, t, t), out_shape=tuple(_sds((n, RET_W), F32) for _ in range(4)), compiler_params=_cp(),
    )(dmix, o_f, o_b, p, p, h_f, h_b)


def _bwd_norm1(dhn, ctx, x, dx1, g, modrows, cb):
    t_len, d = x.shape
    nb = dhn.shape[0] // TM

    def body(dh_ref, ctx_ref, x_ref, dx1_ref, g_ref, m_ref, gx_ref, acc_ref):
        is_ctx = pl.program_id(0) < cb

        @pl.when(pl.program_id(0) == 0)
        def _():
            acc_ref[...] = jnp.zeros_like(acc_ref)

        xin = jnp.where(is_ctx, ctx_ref[...], x_ref[...])
        sc = jnp.where(is_ctx, m_ref[R_CSC1:R_CSC1 + 1, :], m_ref[R_SC1:R_SC1 + 1, :])
        rstd = lax.rsqrt(jnp.mean(xin * xin, axis=-1, keepdims=True) + EPS)
        xh = xin * rstd
        gn = g_ref[...]
        dh = dh_ref[...]
        dn = dh * (1.0 + sc)
        dxh = dn * gn
        gx_ref[...] = dx1_ref[...] + rstd * (dxh - xh * jnp.mean(dxh * xh, axis=-1, keepdims=True))
        s0 = jnp.sum(dh, axis=0, keepdims=True)
        s1 = jnp.sum(dh * xh * gn, axis=0, keepdims=True)
        acc_ref[4:5, :] += jnp.sum(dn * xh, axis=0, keepdims=True)

        @pl.when(is_ctx)
        def _():
            acc_ref[0:1, :] += s0
            acc_ref[1:2, :] += s1

        @pl.when(jnp.logical_not(is_ctx))
        def _():
            acc_ref[2:3, :] += s0
            acc_ref[3:4, :] += s1

    lat = pl.BlockSpec((TM, d), lambda i: (jnp.maximum(i - cb, 0), 0))
    return _pcall(
        body, name="bwd_norm1", grid=(nb,),
        in_specs=[pl.BlockSpec((TM, d), lambda i: (i, 0)), pl.BlockSpec((TM, d), lambda i: (jnp.minimum(i, cb - 1), 0)),
                  lat, lat, pl.BlockSpec((1, d), lambda i: (0, 0)), pl.BlockSpec((8, d), lambda i: (0, 0))],
        out_specs=(lat, pl.BlockSpec((8, d), lambda i: (0, 0))),
        out_shape=(_sds((t_len, d), F32), _sds((8, d), F32)), compiler_params=_cp(),
    )(dhn, ctx, x, dx1, g, modrows)


def _rot(x, cf, ss):
    return x * cf + pltpu.roll(x, HEAD_DIM // 2, 1) * ss


def _decay_exponents(dirn):
    ii = lax.broadcasted_iota(jnp.int32, (CHUNK, CHUNK), 0)
    jj = lax.broadcasted_iota(jnp.int32, (CHUNK, CHUNK), 1)
    rel = ii - jj if dirn == 0 else jj - ii
    pos = ii.astype(F32)
    if dirn == 0:
        cq, cs = pos + 1.0, (CHUNK - 1.0) - pos
    else:
        cq, cs = CHUNK - pos, pos
    return rel, jnp.maximum(rel, 0).astype(F32), cq, cs


def _store_decay(lg_ref, dec):
    for dirn in (0, 1):
        rel, relf, cq, cs = _decay_exponents(dirn)
        for h in range(HEADS):
            lgv = lg_ref[dirn, h]
            dec[dirn, h, 0] = jnp.where(rel >= 0, jnp.exp(lgv * relf), 0.0)
            dec[dirn, h, 1] = jnp.exp(lgv * cq)
            dec[dirn, h, 2] = jnp.exp(lgv * cs)


def _ret_rows(cc, nc, step_of):
    return [lambda s, dirn=dirn: _tile_order(dirn, step_of(s), cc, nc) for dirn in (0, 1)]


def _ret_in_specs(rows):
    specs = []
    for row in rows:
        specs += [pl.BlockSpec((CHUNK, RET_W), lambda s, o=o, row=row: (row(s), o)) for o in (0, 1, 2)]
        specs += [pl.BlockSpec((CHUNK, HEAD_DIM), lambda s, row=row: (row(s), 0))] * 2
    return specs


def _ret_fwd(p, cosf, sins, lg, cc, ride=None):
    n = p.shape[0]
    nc = n // CHUNK
    rows = _ret_rows(cc, nc, lambda s: s)

    def body(lg_ref, q0, k0, v0, c0, s0, q1, k1, v1, c1, s1, o0, o1, sp0, sp1, st, dec):
        @pl.when(pl.program_id(0) == 0)
        def _():
            st[...] = jnp.zeros_like(st)
            _store_decay(lg_ref, dec)

        for dirn, (q_ref, k_ref, v_ref, c_ref, s_ref, o_ref, sp_ref) in enumerate(
                ((q0, k0, v0, c0, s0, o0, sp0), (q1, k1, v1, c1, s1, o1, sp1))):
            cf, ss = c_ref[...], s_ref[...]
            for h in range(HEADS):
                sl = slice(h * HEAD_DIM, (h + 1) * HEAD_DIM)
                q = _rot(q_ref[:, sl], cf, ss)
                k = _rot(k_ref[:, sl], cf, ss) * K_SCALE
                v = v_ref[:, sl]
                sp = st[dirn, h]
                sp_ref[h] = sp
                o_ref[:, sl] = _dot(_dot_nt(q, k) * dec[dirn, h, 0], v) + _dot(q * dec[dirn, h, 1], sp)
                st[dirn, h] = jnp.exp(lg_ref[dirn, h] * CHUNK) * sp + _dot_tn(k * dec[dirn, h, 2], v)

    o_specs = [pl.BlockSpec((CHUNK, RET_W), lambda s, row=row: (row(s), 0)) for row in rows]
    state = pl.BlockSpec((None, HEADS, CHUNK, HEAD_DIM), lambda s: (s, 0, 0, 0))
    return _pcall_ride(
        body, ride, (lg, p, p, p, cosf, sins, p, p, p, cosf, sins), name="ret_fwd", grid=(nc,),
        in_specs=[pl.BlockSpec(memory_space=pltpu.SMEM)] + _ret_in_specs(rows),
        out_specs=(o_specs[0], o_specs[1], state, state),
        out_shape=(_sds((n, RET_W), F32),) * 2 + (_sds((nc, HEADS, CHUNK, HEAD_DIM), F32),) * 2,
        scratch_shapes=[pltpu.VMEM((2, HEADS, CHUNK, HEAD_DIM), F32), pltpu.VMEM((2, HEADS, 3, CHUNK, CHUNK), F32)],
        compiler_params=_cp())


def _ret_bwd(p, cosf, sins, lg, do, s_prev, cc, ride=None):
    n = p.shape[0]
    nc = n // CHUNK
    rows = _ret_rows(cc, nc, lambda s: nc - 1 - s)

    def body(lg_ref, q0, k0, v0, c0, s0, q1, k1, v1, c1, s1, do0, do1, sp0, sp1,
             dq0, dk0, dv0, dq1, dk1, dv1, dlg_ref, dst, dec):
        @pl.when(pl.program_id(0) == 0)
        def _():
            dst[...] = jnp.zeros_like(dst)
            dlg_ref[...] = jnp.zeros_like(dlg_ref)
            _store_decay(lg_ref, dec)

        for dirn, (q_ref, k_ref, v_ref, c_ref, s_ref, do_ref, sp_ref, dq_ref, dk_ref, dv_ref) in enumerate(
                ((q0, k0, v0, c0, s0, do0, sp0, dq0, dk0, dv0), (q1, k1, v1, c1, s1, do1, sp1, dq1, dk1, dv1))):
            cf, ss = c_ref[...], s_ref[...]
            _, relf, cq, cs = _decay_exponents(dirn)
            for h in range(HEADS):
                sl = slice(h * HEAD_DIM, (h + 1) * HEAD_DIM)
                q = _rot(q_ref[:, sl], cf, ss)
                k = _rot(k_ref[:, sl], cf, ss) * K_SCALE
                v = v_ref[:, sl]
                dov = do_ref[:, sl]
                dm, wq, ws = dec[dirn, h, 0], dec[dirn, h, 1], dec[dirn, h, 2]
                a = _dot_nt(q, k)
                sp = sp_ref[h]
                dsn = dst[dirn, h]
                gc = jnp.exp(lg_ref[dirn, h] * CHUNK)
                g1 = _dot_nt(dov, sp)
                da = _dot_nt(dov, v) * dm
                dq = g1 * wq + _dot(da, k)
                h1 = _dot_nt(v, dsn)
                dk = _dot_tn(da, q) + h1 * ws
                dv_ref[:, sl] = _dot_tn(a * dm, dov) + _dot(k * ws, dsn)
                dst[dirn, h] = gc * dsn + _dot_tn(q * wq, dov)
                term = da * a * relf + q * g1 * (wq * cq) + k * h1 * (ws * cs) + sp * dsn * (CHUNK * gc)
                dlg_ref[dirn * HEADS + h, 0:1, :] += jnp.sum(term, axis=0, keepdims=True)
                dq_ref[:, sl] = dq * cf + pltpu.roll(dq * ss, HEAD_DIM // 2, 1)
                dk_ref[:, sl] = (dk * cf + pltpu.roll(dk * ss, HEAD_DIM // 2, 1)) * K_SCALE

    wide = [pl.BlockSpec((CHUNK, RET_W), lambda s, row=row: (row(s), 0)) for row in rows]
    state = pl.BlockSpec((None, HEADS, CHUNK, HEAD_DIM), lambda s: (nc - 1 - s, 0, 0, 0))
    return _pcall_ride(
        body, ride, (lg, p, p, p, cosf, sins, p, p, p, cosf, sins, do, do, s_prev[0], s_prev[1]), name="ret_bwd", grid=(nc,),
        in_specs=[pl.BlockSpec(memory_space=pltpu.SMEM)] + _ret_in_specs(rows) + wide + [state, state],
        out_specs=(wide[0],) * 3 + (wide[1],) * 3 + (pl.BlockSpec((2 * HEADS, 8, HEAD_DIM), lambda s: (0, 0, 0)),),
        out_shape=(_sds((n, RET_W), F32),) * 6 + (_sds((2 * HEADS, 8, HEAD_DIM), F32),),
        scratch_shapes=[pltpu.VMEM((2, HEADS, CHUNK, HEAD_DIM), F32), pltpu.VMEM((2, HEADS, 3, CHUNK, CHUNK), F32)],
        compiler_params=_cp())


def _shift_rows(cur, prev8, next8, k, seg_start, seg_end):
    tm = cur.shape[0]
    rows = _rows_iota(cur.shape)
    if k < 0:
        out = pltpu.roll(cur, -k, 0)
        for j in range(-k):
            halo = jnp.where(seg_start, 0.0, prev8[SUB + k + j:SUB + k + j + 1, :])
            out = jnp.where(rows == j, halo, out)
    else:
        out = pltpu.roll(cur, tm - k, 0)
        for j in range(k):
            halo = jnp.where(seg_end, 0.0, next8[j:j + 1, :])
            out = jnp.where(rows == tm - k + j, halo, out)
    return out


def _seg_flags(t, cb, nb):
    return jnp.logical_or(t == 0, t == cb), jnp.logical_or(t == cb - 1, t == nb - 1)


def _halo_specs(tile_of, n_rows, col):
    per = TM // SUB
    return [pl.BlockSpec((TM, LRU_W), lambda s: (tile_of(s), col)),
            pl.BlockSpec((SUB, LRU_W), lambda s: (jnp.maximum(tile_of(s) * per - 1, 0), col)),
            pl.BlockSpec((SUB, LRU_W), lambda s: (jnp.minimum((tile_of(s) + 1) * per, n_rows // SUB - 1), col))]


def _lru_gates(xr, prev8, next8, seg_start, seg_end, cw_ref, cb_ref, wg_ref, bg_ref, sp_ref):
    xm1 = _shift_rows(xr, prev8, next8, -1, seg_start, seg_end)
    xp1 = _shift_rows(xr, prev8, next8, 1, seg_start, seg_end)
    xp2 = _shift_rows(xr, prev8, next8, 2, seg_start, seg_end)
    xc = cb_ref[...] + xm1 * cw_ref[0:1, :] + xr * cw_ref[1:2, :] + xp1 * cw_ref[2:3, :] + xp2 * cw_ref[3:4, :]
    pre = _dot(xc, wg_ref[...]) + bg_ref[...]
    r = _sigmoid(pre[:, :LRU_W])
    i = _sigmoid(pre[:, LRU_W:])
    la = (-LRU_C) * r * sp_ref[...]
    a = jnp.exp(la)
    sq = jnp.sqrt(_one_minus_exp(2.0 * la))
    return xc, r, i, a, sq


def _scan_tile(a, b, ascending, a_sc, b_sc, carry, out_ref):
    tm = a.shape[0]
    r8 = _rows_iota(a.shape) % SUB
    for k in (1, 2, 4):
        if ascending:
            m = r8 >= k
            a_s, b_s = pltpu.roll(a, k, 0), pltpu.roll(b, k, 0)
        else:
            m = r8 < SUB - k
            a_s, b_s = pltpu.roll(a, tm - k, 0), pltpu.roll(b, tm - k, 0)
        b = a * jnp.where(m, b_s, 0.0) + b
        a = a * jnp.where(m, a_s, 1.0)
    a_sc[...] = a
    b_sc[...] = b
    nsub = tm // SUB

    def step(j, c):
        off = pl.multiple_of((j if ascending else nsub - 1 - j) * SUB, SUB)
        hb = a_sc[pl.ds(off, SUB), :] * c + b_sc[pl.ds(off, SUB), :]
        out_ref[pl.ds(off, SUB), :] = hb
        last = hb[SUB - 1:SUB, :] if ascending else hb[0:1, :]
        return jnp.broadcast_to(last, c.shape)

    carry[...] = lax.fori_loop(0, nsub, step, carry[...])


def _lru_fwd(p, wg, bg, sp, cw, cbias, dirn, cb, ride=None):
    n = p.shape[0]
    nb = n // TM
    tile_of = lambda s: _tile_order(dirn, s, cb, nb)

    def body(x_ref, xp_ref, xn_ref, wg_ref, bg_ref, sp_ref, cw_ref, cb_ref, h_ref, cin_ref, carry, a_sc, b_sc):
        s = pl.program_id(0)

        @pl.when(s == 0)
        def _():
            carry[...] = jnp.zeros_like(carry)

        seg_start, seg_end = _seg_flags(tile_of(s), cb, nb)
        xc, r, i, a, sq = _lru_gates(x_ref[...], xp_ref[...], xn_ref[...], seg_start, seg_end,
                                     cw_ref, cb_ref, wg_ref, bg_ref, sp_ref)
        cin_ref[...] = carry[...]
        _scan_tile(a, sq * (i * xc), dirn == 0, a_sc, b_sc, carry, h_ref)

    full = lambda shape: pl.BlockSpec(shape, lambda s: (0,) * len(shape))
    return _pcall_ride(
        body, ride, (p, p, p, wg, bg, sp, cw, cbias), name=f"lru_fwd{dirn}", grid=(nb,),
        in_specs=_halo_specs(tile_of, n, 4) + [full((LRU_W, 2 * LRU_W)), full((1, 2 * LRU_W)), full((1, LRU_W)),
                                               full((4, LRU_W)), full((1, LRU_W))],
        out_specs=(pl.BlockSpec((TM, LRU_W), lambda s: (tile_of(s), 0)),
                   pl.BlockSpec((None, SUB, LRU_W), lambda s: (tile_of(s), 0, 0))),
        out_shape=(_sds((n, LRU_W), F32), _sds((nb, SUB, LRU_W), F32)),
        scratch_shapes=[pltpu.VMEM((SUB, LRU_W), F32), pltpu.VMEM((TM, LRU_W), F32), pltpu.VMEM((TM, LRU_W), F32)],
        compiler_params=_cp())


def _lru_bwd(p, wg, bg, sp, cw, cbias, h, cin, dhs, dirn, cb, ride=None):
    n = p.shape[0]
    nb = n // TM
    tile_of = lambda s: _tile_order(dirn, nb - 1 - s, cb, nb)

    def body(x_ref, xp_ref, xn_ref, wg_ref, bg_ref, sp_ref, cw_ref, cb_ref, h_ref, cin_ref, dhs_ref,
             dxc_ref, dwg_ref, acc_ref, carry, a_sc, b_sc, mu_sc):
        s = pl.program_id(0)

        @pl.when(s == 0)
        def _():
            carry[...] = jnp.zeros_like(carry)
            dwg_ref[...] = jnp.zeros_like(dwg_ref)
            acc_ref[...] = jnp.zeros_like(acc_ref)

        seg_start, seg_end = _seg_flags(tile_of(s), cb, nb)
        xc, r, i, a, sq = _lru_gates(x_ref[...], xp_ref[...], xn_ref[...], seg_start, seg_end,
                                     cw_ref, cb_ref, wg_ref, bg_ref, sp_ref)
        rows = _rows_iota(a.shape)
        hv = h_ref[...]
        dh = dhs_ref[...]
        mu_next = carry[0:1, :]
        _scan_tile(a, a * dh, dirn == 1, a_sc, b_sc, carry, mu_sc)
        mu = mu_sc[...]
        if dirn == 0:
            hprev = jnp.where(rows == 0, cin_ref[0:1, :], pltpu.roll(hv, 1, 0))
            lam = dh + jnp.where(rows == TM - 1, mu_next, pltpu.roll(mu, TM - 1, 0))
        else:
            hprev = jnp.where(rows == TM - 1, cin_ref[0:1, :], pltpu.roll(hv, TM - 1, 0))
            lam = dh + jnp.where(rows == 0, mu_next, pltpu.roll(mu, 1, 0))
        ds = lam * (i * xc)
        di = lam * (sq * xc)
        dla = lam * hprev * a - ds * (a * a) / jnp.maximum(sq, 1e-20)
        dpr = dla * ((-LRU_C) * sp_ref[...]) * r * (1.0 - r)
        dpi = di * i * (1.0 - i)
        dpre = jnp.concatenate([dpr, dpi], axis=1)
        dxc_ref[...] = lam * (sq * i) + _dot_nt(dpre, wg_ref[...])
        dwg_ref[...] += _dot_tn(xc, dpre)
        acc_ref[0:1, :] += jnp.sum(dpre, axis=0, keepdims=True)
        acc_ref[1:2, 0:LRU_W] += jnp.sum(dla * ((-LRU_C) * r), axis=0, keepdims=True)

    full = lambda shape: pl.BlockSpec(shape, lambda s: (0,) * len(shape))
    tile = pl.BlockSpec((TM, LRU_W), lambda s: (tile_of(s), 0))
    return _pcall_ride(
        body, ride, (p, p, p, wg, bg, sp, cw, cbias, h, cin, dhs), name=f"lru_bwd{dirn}", grid=(nb,),
        in_specs=_halo_specs(tile_of, n, 4) + [full((LRU_W, 2 * LRU_W)), full((1, 2 * LRU_W)), full((1, LRU_W)),
                                               full((4, LRU_W)), full((1, LRU_W)), tile,
                                               pl.BlockSpec((None, SUB, LRU_W), lambda s: (tile_of(s), 0, 0)), tile],
        out_specs=(tile, full((LRU_W, 2 * LRU_W)), full((8, 2 * LRU_W))),
        out_shape=(_sds((n, LRU_W), F32), _sds((LRU_W, 2 * LRU_W), F32), _sds((8, 2 * LRU_W), F32)),
        scratch_shapes=[pltpu.VMEM((SUB, LRU_W), F32)] + [pltpu.VMEM((TM, LRU_W), F32)] * 3,
        compiler_params=_cp())


def _assemble_dp(dqs, dks, dvs, dg, dgate, dxcs, p, cw, cb, ride=None):
    n = p.shape[0]
    nb = n // TM
    tile_of = lambda s: s

    def body(dqf, dqb, dkf, dkb, dvf, dvb, dg_ref, dgate_ref, cf, pf, nf, cb_, pb, nb_, x_ref, xp_ref, xn_ref,
             cw_ref, dp_ref, acc_ref):
        s = pl.program_id(0)

        @pl.when(s == 0)
        def _():
            acc_ref[...] = jnp.zeros_like(acc_ref)

        seg_start, seg_end = _seg_flags(s, cb, nb)
        dp_ref[:, 0:RET_W] = (dqf[...] + dqb[...]).astype(BF16)
        dp_ref[:, RET_W:2 * RET_W] = (dkf[...] + dkb[...]).astype(BF16)
        dp_ref[:, 2 * RET_W:3 * RET_W] = (dvf[...] + dvb[...]).astype(BF16)
        dp_ref[:, 3 * RET_W:4 * RET_W] = dg_ref[...].astype(BF16)
        dxc = cf[...] + cb_[...]
        dprev = pf[...] + pb[...]
        dnext = nf[...] + nb_[...]
        dxr = (_shift_rows(dxc, dprev, dnext, 1, seg_start, seg_end) * cw_ref[0:1, :] + dxc * cw_ref[1:2, :]
               + _shift_rows(dxc, dprev, dnext, -1, seg_start, seg_end) * cw_ref[2:3, :]
               + _shift_rows(dxc, dprev, dnext, -2, seg_start, seg_end) * cw_ref[3:4, :])
        dp_ref[:, 4 * RET_W:4 * RET_W + LRU_W] = dxr.astype(BF16)
        dp_ref[:, 4 * RET_W + LRU_W:] = dgate_ref[...].astype(BF16)
        xr, xp, xn = x_ref[...], xp_ref[...], xn_ref[...]
        for j, k in enumerate((-1, 0, 1, 2)):
            xs = xr if k == 0 else _shift_rows(xr, xp, xn, k, seg_start, seg_end)
            acc_ref[j:j + 1, :] += jnp.sum(dxc * xs, axis=0, keepdims=True)
        acc_ref[4:5, :] += jnp.sum(dxc, axis=0, keepdims=True)

    t = pl.BlockSpec((TM, RET_W), lambda s: (s, 0))
    args = (dqs[0], dqs[1], dks[0], dks[1], dvs[0], dvs[1], dg, dgate, dxcs[0], dxcs[0], dxcs[0], dxcs[1], dxcs[1], dxcs[1],
            p, p, p, cw)
    return _pcall_ride(
        body, ride, args, name="assemble_dp", grid=(nb,),
        in_specs=[t] * 8 + _halo_specs(tile_of, n, 0) * 2 + _halo_specs(tile_of, n, 4)
        + [pl.BlockSpec((4, LRU_W), lambda s: (0, 0))],
        out_specs=(pl.BlockSpec((TM, 4 * RET_W + 2 * LRU_W), lambda s: (s, 0)), pl.BlockSpec((8, LRU_W), lambda s: (0, 0))),
        out_shape=(_sds((n, 4 * RET_W + 2 * LRU_W), BF16), _sds((8, LRU_W), F32)), compiler_params=_cp())


def _sum_adam(parts, w, m, v, name):
    nparts, r, c = parts.shape
    tr = min(r, 128)
    bc1 = 1.0 - ADAM_B1 ** ADAM_STEP
    bc2 = 1.0 - ADAM_B2 ** ADAM_STEP

    def body(p_ref, w_ref, m_ref, v_ref, g_ref, d_ref, nm_ref, nv_ref):
        g = p_ref[0].astype(F32)
        for j in range(1, nparts):
            g = g + p_ref[j].astype(F32)
        nm = ADAM_B1 * m_ref[...] + (1.0 - ADAM_B1) * g
        nv = ADAM_B2 * v_ref[...] + (1.0 - ADAM_B2) * (g * g)
        g_ref[...] = g
        nm_ref[...] = nm
        nv_ref[...] = nv
        d_ref[...] = (-ADAM_LR) * ((nm / bc1) / (jnp.sqrt(nv / bc2) + ADAM_EPS) + ADAM_WD * w_ref[...])

    t = pl.BlockSpec((tr, c), lambda i: (i, 0))
    return _pcall(
        body, name=name, grid=(r // tr,),
        in_specs=[pl.BlockSpec((nparts, tr, c), lambda i: (0, i, 0)), t, t, t],
        out_specs=(t, t, t, t), out_shape=(_sds((r, c), F32),) * 4, compiler_params=_cp(),
    )(parts, w, m, v)


def _sum_parts(parts, name):
    nparts, r, c = parts.shape

    def body(p_ref, o_ref):
        g = p_ref[0]
        for j in range(1, nparts):
            g = g + p_ref[j]
        o_ref[...] = g

    return _pcall(body, name=name, out_shape=_sds((r, c), parts.dtype), compiler_params=_cp())(parts)


def _rot_tables(l_len, t_len):
    rows = t_len // GRID_W
    row = jnp.repeat(jnp.arange(rows, dtype=F32), GRID_W)
    col = jnp.tile(jnp.arange(GRID_W, dtype=F32), rows)
    n_freq = HEAD_DIM // 4
    inv = ROPE_BASE ** (-jnp.arange(n_freq, dtype=F32) / n_freq)
    ang = jnp.concatenate([row[:, None] * inv, col[:, None] * inv], axis=-1)
    cos, sin = jnp.cos(ang), jnp.sin(ang)
    cosf = jnp.concatenate([jnp.ones((l_len, HEAD_DIM), F32), jnp.concatenate([cos, cos], axis=-1)], axis=0)
    sins = jnp.concatenate([jnp.zeros((l_len, HEAD_DIM), F32), jnp.concatenate([-sin, sin], axis=-1)], axis=0)
    return cosf, sins


def _block_diag(w):
    eye = jnp.eye(LRU_BLOCKS, dtype=w.dtype)
    return (w[:, :, None, :] * eye[:, None, :, None]).reshape(LRU_W, LRU_W)


def _diag_blocks(mat):
    m4 = mat.reshape(LRU_BLOCKS, LRU_BD, LRU_BLOCKS, LRU_BD)
    return jnp.stack([m4[b, :, b, :] for b in range(LRU_BLOCKS)])


def _pack(fields):
    parts, offs, o = [], [], 0
    for f in fields:
        flat = f.reshape(-1).astype(F32)
        n = flat.shape[0]
        npad = -(-n // 1024) * 1024
        if npad > n:
            flat = jnp.concatenate([flat, jnp.zeros((npad - n,), F32)])
        parts.append(flat.reshape(-1, 128))
        offs.append((o, n))
        o += npad // 128
    if o % 128:
        parts.append(jnp.zeros((128 - o % 128, 128), F32))
    return jnp.concatenate(parts, axis=0), offs


def _unpack(packed, offs, shapes):
    return [packed[o:o + -(-n // 1024) * 8].reshape(-1)[:n].reshape(s) for (o, n), s in zip(offs, shapes)]


def _silu(x):
    return x * jax.nn.sigmoid(x)


def kernel(x, c, ctx, c_ctx, w_ada, b_ada, norm1_g, norm2_g, w_in, ret_decay, conv_w, conv_b, lru_wa, lru_ba, lru_wx, lru_bx, lru_lambda, w_out, w_mlp1, w_mlp2, final_g, loss_target, m_c_ctx, m_w_ada, m_b_ada, m_norm1_g, m_norm2_g, m_w_in, m_ret_decay, m_conv_w, m_conv_b, m_lru_wa, m_lru_ba, m_lru_wx, m_lru_bx, m_lru_lambda, m_w_out, m_w_mlp1, m_w_mlp2, m_final_g, v_c_ctx, v_w_ada, v_b_ada, v_norm1_g, v_norm2_g, v_w_in, v_ret_decay, v_conv_w, v_conv_b, v_lru_wa, v_lru_ba, v_lru_wx, v_lru_bx, v_lru_lambda, v_w_out, v_w_mlp1, v_w_mlp2, v_final_g):
    t_len, d = x.shape[1], x.shape[2]
    l_len = ctx.shape[1]
    cb, cc = l_len // TM, l_len // CHUNK
    me = 4 * lax.axis_index("x") + 2 * lax.axis_index("y") + lax.axis_index("c")
    x2d, ctx2d, tgt2d = x[0], ctx[0], loss_target[0]
    ada_cols = w_ada.shape[2]
    wa2d = w_ada[0]

    sc_loc = conv_w.shape[2]
    pack_a = jnp.zeros((8, d), F32)
    pack_a = pack_a.at[0].set(_silu(c[0]))
    pack_a = pack_a.at[1, :4 * sc_loc].set(conv_w[0].reshape(-1))
    pack_a = pack_a.at[2, :2 * sc_loc].set(lru_ba[0].reshape(-1))
    pack_a = pack_a.at[3, :2 * sc_loc].set(lru_bx[0].reshape(-1))
    pack_a = pack_a.at[4, :2 * sc_loc].set(lru_lambda[0].reshape(-1))
    all_a = _all_gather(pack_a, "gather_small_in")
    s16 = jnp.zeros((16, d), F32).at[0:8].set(all_a[:, 0, :]).at[8].set(_silu(c_ctx))

    def unshard(row, k):
        return all_a[:, row, :k * sc_loc].reshape(N_DEV, k, sc_loc).transpose(1, 0, 2).reshape(k, N_DEV * sc_loc)

    conv_w_full = unshard(1, 4)
    ba_full, bx_full, lam_full = unshard(2, 2), unshard(3, 2), unshard(4, 2)

    b_cols = lax.dynamic_slice(b_ada, (0, me * ada_cols), (1, ada_cols))
    mod_parts = _all_gather(_mod_part(s16, wa2d, b_cols), "gather_mod")
    mod_all = mod_parts.transpose(1, 0, 2).reshape(16, N_DEV * ada_cols)
    mod_me = lax.dynamic_slice(mod_all, (me, 0), (1, 6 * d)).reshape(6, d)
    mod_c = mod_all[8].reshape(6, d)
    modrows = jnp.concatenate([mod_c[0:2], mod_me], axis=0)

    lg = jax.nn.log_sigmoid(ret_decay[0])
    sp = jax.nn.softplus(-lam_full)
    wg = [jnp.concatenate([_block_diag(lru_wa[0, dd]), _block_diag(lru_wx[0, dd])], axis=1).astype(BF16) for dd in (0, 1)]
    bg = [jnp.concatenate([ba_full[dd], bx_full[dd]])[None, :] for dd in (0, 1)]
    cosf, sins = _rot_tables(l_len, t_len)

    (hn, hnt), win_g = _norm1_fwd(ctx2d, x2d, norm1_g, modrows, cb, ride=("gather", w_in[0].astype(BF16)))
    p, wout_g = _mm_nn(hn, win_g, F32, "mm_in", ride=("gather", w_out[0].astype(BF16)))
    wout_g = wout_g.reshape(1, d, d)
    (o0, o1, sp0, sp1), w1_g = _ret_fwd(p, cosf, sins, lg, cc, ride=("gather", w_mlp1[0].astype(BF16)))
    o, s_prev = [o0, o1], [sp0, sp1]
    (h0, cin0), w2_g = _lru_fwd(p, wg[0], bg[0], sp[0:1], conv_w_full, conv_b, 0, cb, ride=("gather", w_mlp2[0].astype(BF16)))
    w2_g = w2_g.reshape(1, 4 * d, d)
    (h1, cin1), _ = _lru_fwd(p, wg[1], bg[1], sp[1:2], conv_w_full, conv_b, 1, cb)
    h, cin = [h0, h1], [cin0, cin1]
    mix = _mix_fwd(o[0], o[1], p, h[0], h[1], cb, t_len)
    y = _mm_nn(mix, wout_g, F32, "mm_out")
    x1, h2, h2t = _res_norm2(x2d, y, norm2_g, modrows)
    r = _mm_nn(h2, w1_g, BF16, "mm_mlp1", relu_out=True)
    z = _mm_nn(r, w2_g, F32, "mm_mlp2", square_lhs=True)
    dx2, dz, dzt, facc = _final_fwd_bwd(x1, z, tgt2d, final_g[None, :], modrows)
    loss = lax.psum(0.5 / d * jnp.sum(facc[2]), AXES)

    du = _mm_nt(dz, w2_g, BF16, "mm_da2", relu_mul=r)
    gw2 = _mm_wgrad(dzt, r, "mm_dw2", w2_g.shape[1] // N_DEV, BF16, transpose_out=True, square_rhs=True)
    gw1 = _mm_wgrad(h2t, du, "mm_dw1", w1_g.shape[2], BF16)
    dh2 = _mm_nt(du, w1_g, F32, "mm_dh2")
    dx1, dy, n2acc = _bwd_norm2(dh2, x1, dx2, y, norm2_g, modrows)
    dmix = _mm_nt(dy, wout_g, F32, "mm_dmix")
    gwo = _mm_tn(mix, dy, "mm_dwout", False, d, BF16, 512).reshape(N_DEV, -1, d)
    do, dg, dgate, dhs = _mix_bwd(dmix, o[0], o[1], p, h[0], h[1], cb)
    dxcs, dwgs, laccs, rides, got = [], [], [], [("a2a", gw1), ("a2a", gwo)], []
    (dq0, dk0, dv0, dq1, dk1, dv1, dlg_lanes), gw2_all = _ret_bwd(p, cosf, sins, lg, do, s_prev, cc, ride=("a2a", gw2))
    dqs, dks, dvs = [dq0, dq1], [dk0, dk1], [dv0, dv1]
    for dd in (0, 1):
        (dxc_, dwg_, lacc_), got_ = _lru_bwd(p, wg[dd], bg[dd], sp[dd:dd + 1], conv_w_full, conv_b, h[dd], cin[dd], dhs, dd, cb,
                                             ride=rides[dd])
        dxcs.append(dxc_); dwgs.append(dwg_); laccs.append(lacc_); got.append(got_)
    gw1_all, gwo_all = got
    (dp, cacc), _ = _assemble_dp(dqs, dks, dvs, dg, dgate, dxcs, p, conv_w_full, cb)
    gwi = _mm_wgrad(hnt, dp, "mm_dwin", win_g.shape[2], BF16)
    dhn, gwi_all = _mm_nt(dp, win_g, F32, "mm_dhn", ride=("a2a", gwi))
    grad_x, n1acc = _bwd_norm1(dhn, ctx2d, x2d, dx1, norm1_g, modrows, cb)

    dlg = jnp.sum(dlg_lanes[:, 0, :], axis=-1).reshape(2, HEADS)
    d_wa = jnp.stack([_diag_blocks(dwgs[dd][:, :LRU_W]) for dd in (0, 1)])
    d_wx = jnp.stack([_diag_blocks(dwgs[dd][:, LRU_W:]) for dd in (0, 1)])
    d_ba = jnp.stack([laccs[dd][0, :LRU_W] for dd in (0, 1)])
    d_bx = jnp.stack([laccs[dd][0, LRU_W:] for dd in (0, 1)])
    d_sp = jnp.stack([laccs[dd][1, :LRU_W] for dd in (0, 1)])
    fields = [n1acc[0:5], n2acc[0:4], facc[0:2], cacc[0:5], dlg, d_wa, d_wx, d_ba, d_bx, d_sp]
    shapes = [f.shape for f in fields]
    pack_b, offs = _pack(fields)
    all_b = _all_gather(pack_b, "gather_small_grads")
    tot = _unpack(_sum_parts(all_b, "sum_small_grads"), offs, shapes)
    t_n1, t_n2, t_f, t_conv, t_dlg, t_wa, t_wx, t_ba, t_bx, t_sp = tot
    per_dev = [_unpack(all_b[j], offs[:3], shapes[:3]) for j in range(N_DEV)]
    dm_rows = [jnp.concatenate([a1[2], a1[3], a2[3], a2[0], a2[1], af[1]]) for a1, a2, af in per_dev]
    dm_c = jnp.concatenate([t_n1[0], t_n1[1], jnp.zeros((4 * d,), F32)])
    dm16 = jnp.zeros((16, 6 * d), F32).at[0:8].set(jnp.stack(dm_rows)).at[8].set(dm_c)
    g_b_ada = jnp.sum(dm16, axis=0)[None, :]
    dm_cols = lax.dynamic_slice(dm16, (0, me * ada_cols), (16, ada_cols))
    g_w_ada, ds16 = _ada_bwd(s16, dm_cols, wa2d)
    ds_all = _all_gather(ds16[8:16], "gather_dsilu")
    dsilu_cc = _sum_parts(ds_all, "sum_dsilu")[0]
    sg_cc = jax.nn.sigmoid(c_ctx)
    g_c_ctx = dsilu_cc * (sg_cc * (1.0 + c_ctx * (1.0 - sg_cc)))

    g_ret_decay = (t_dlg * jax.nn.sigmoid(-ret_decay[0]))[None]
    g_lambda_full = -t_sp * jax.nn.sigmoid(-lam_full)

    def my_cols(full):
        return lax.dynamic_slice(full, (0, me * sc_loc), (full.shape[0], sc_loc))[None]

    small_g = dict(
        c_ctx=g_c_ctx, b_ada=g_b_ada, norm1_g=t_n1[4][None], norm2_g=t_n2[2][None], ret_decay=g_ret_decay,
        conv_w=my_cols(t_conv[0:4]), conv_b=t_conv[4][None], lru_wa=t_wa[None], lru_ba=my_cols(t_ba),
        lru_wx=t_wx[None], lru_bx=my_cols(t_bx), lru_lambda=my_cols(g_lambda_full), final_g=t_f[0])
    small_w = dict(c_ctx=c_ctx, b_ada=b_ada, norm1_g=norm1_g, norm2_g=norm2_g, ret_decay=ret_decay, conv_w=conv_w,
                   conv_b=conv_b, lru_wa=lru_wa, lru_ba=lru_ba, lru_wx=lru_wx, lru_bx=lru_bx, lru_lambda=lru_lambda,
                   final_g=final_g)
    small_m = dict(c_ctx=m_c_ctx, b_ada=m_b_ada, norm1_g=m_norm1_g, norm2_g=m_norm2_g, ret_decay=m_ret_decay,
                   conv_w=m_conv_w, conv_b=m_conv_b, lru_wa=m_lru_wa, lru_ba=m_lru_ba, lru_wx=m_lru_wx,
                   lru_bx=m_lru_bx, lru_lambda=m_lru_lambda, final_g=m_final_g)
    small_v = dict(c_ctx=v_c_ctx, b_ada=v_b_ada, norm1_g=v_norm1_g, norm2_g=v_norm2_g, ret_decay=v_ret_decay,
                   conv_w=v_conv_w, conv_b=v_conv_b, lru_wa=v_lru_wa, lru_ba=v_lru_ba, lru_wx=v_lru_wx,
                   lru_bx=v_lru_bx, lru_lambda=v_lru_lambda, final_g=v_final_g)
    names = list(small_w)
    sshapes = [small_w[k].shape for k in names]
    pg, soffs = _pack([small_g[k].reshape(small_w[k].shape) for k in names])
    pw, _ = _pack([small_w[k] for k in names])
    pm, _ = _pack([small_m[k] for k in names])
    pv, _ = _pack([small_v[k] for k in names])
    sg_, sd_, sm_, sv_ = _sum_adam(pg[None], pw, pm, pv, "adam_small")
    res = {}
    for k, g_, d_, m_, v_ in zip(names, *[_unpack(a, soffs, sshapes) for a in (sg_, sd_, sm_, sv_)]):
        res[k] = (g_, d_, m_, v_)

    def big(parts, w, m, v, name):
        out = _sum_adam(parts, w[0], m[0], v[0], name)
        return tuple(a[None] for a in out)

    res["w_ada"] = big(g_w_ada[None], w_ada, m_w_ada, v_w_ada, "adam_w_ada")
    res["w_in"] = big(gwi_all, w_in, m_w_in, v_w_in, "adam_w_in")
    res["w_out"] = big(gwo_all, w_out, m_w_out, v_w_out, "adam_w_out")
    res["w_mlp1"] = big(gw1_all, w_mlp1, m_w_mlp1, v_w_mlp1, "adam_w_mlp1")
    res["w_mlp2"] = big(gw2_all, w_mlp2, m_w_mlp2, v_w_mlp2, "adam_w_mlp2")

    order = ["c_ctx", "w_ada", "b_ada", "norm1_g", "norm2_g", "w_in", "ret_decay", "conv_w", "conv_b", "lru_wa", "lru_ba",
             "lru_wx", "lru_bx", "lru_lambda", "w_out", "w_mlp1", "w_mlp2", "final_g"]
    outs = [loss, grad_x[None]]
    for j in range(4):
        outs += [res[k][j] for k in order]
    return tuple(outs)
```

```python
import jax
import jax.numpy as jnp
from jax import lax
from jax.experimental import pallas as pl
from jax.experimental.pallas import tpu as pltpu

F32 = jnp.float32
BF16 = jnp.bfloat16
AXES = ("x", "y", "c")
N_DEV = 8
MESH = pl.DeviceIdType.MESH

HEADS = 4
HEAD_DIM = 128
CHUNK = 128
RET_W = HEADS * HEAD_DIM
LRU_W = 512
LRU_BLOCKS = 8
LRU_BD = LRU_W // LRU_BLOCKS
LRU_C = 8.0
EPS = 1e-6
K_SCALE = HEAD_DIM ** -0.5
ROPE_BASE = 10000.0
GRID_W = 64
TM = 256
SUB = 8

ADAM_LR = 0.001
ADAM_B1 = 0.9
ADAM_B2 = 0.999
ADAM_EPS = 1e-08
ADAM_WD = 0.01
ADAM_STEP = 10

R_CSH1, R_CSC1, R_SH1, R_SC1, R_G1, R_SH2, R_SC2, R_G2 = range(8)


def _pcall(body, **kw):
    return pl.pallas_call(body, **kw)


def _cp(vmem_mb=48):
    return pltpu.CompilerParams(vmem_limit_bytes=vmem_mb << 20)


def _sds(shape, dtype):
    return jax.ShapeDtypeStruct(shape, dtype)


def _dot(a, b):
    return jnp.dot(a.astype(BF16), b.astype(BF16), preferred_element_type=F32)


def _dot_nt(a, b):
    return lax.dot_general(a.astype(BF16), b.astype(BF16), (((1,), (1,)), ((), ())), preferred_element_type=F32)


def _dot_tn(a, b):
    return lax.dot_general(a.astype(BF16), b.astype(BF16), (((0,), (0,)), ((), ())), preferred_element_type=F32)


def _sigmoid(x):
    return jax.nn.sigmoid(x)


def _gelu(x):
    return 0.5 * x * (1.0 + jnp.tanh(0.7978845608028654 * (x + 0.044715 * x * x * x)))


def _dgelu(x):
    t = jnp.tanh(0.7978845608028654 * (x + 0.044715 * x * x * x))
    return 0.5 * (1.0 + t) + 0.5 * x * (1.0 - t * t) * 0.7978845608028654 * (1.0 + 3.0 * 0.044715 * x * x)


def _rows_iota(shape):
    return lax.broadcasted_iota(jnp.int32, shape, 0)


def _tile_order(dirn, s, cb, nb):
    if dirn == 0:
        return s
    return jnp.where(s < cb, cb - 1 - s, nb - 1 - (s - cb))


_SEMS = [pltpu.SemaphoreType.DMA((7,)), pltpu.SemaphoreType.DMA((7,)), pltpu.SemaphoreType.DMA(())]
_ANY = pl.BlockSpec(memory_space=pl.ANY)


def _gather_copies(x_ref, out_ref, send_sems, recv_sems, local_sem):
    mx, my, mc = lax.axis_index("x"), lax.axis_index("y"), lax.axis_index("c")
    me, sibling = (mx, my, mc), (mx, my, 1 - mc)
    chips = [(1 - mx, my), (mx, 1 - my), (1 - mx, 1 - my)]

    def slot(px, py, pc):
        return out_ref.at[4 * px + 2 * py + pc]

    def copy(k, block, to, src=None):
        return pltpu.make_async_remote_copy(
            src_ref=slot(*block) if src is None else src, dst_ref=slot(*block),
            send_sem=send_sems.at[k], recv_sem=recv_sems.at[k], device_id=to, device_id_type=MESH)

    mine = pltpu.make_async_copy(x_ref, slot(*me), local_sem)
    first = [copy(0, me, sibling, src=x_ref)] + [copy(1 + j, me, (*chip, mc), src=x_ref) for j, chip in enumerate(chips)]
    passed = [copy(4 + j, (*chip, mc), sibling) for j, chip in enumerate(chips)]
    recv_ici = [copy(1 + j, (*chip, mc), me) for j, chip in enumerate(chips)]
    recv_d2d = [copy(0, sibling, me)] + [copy(4 + j, (*chip, 1 - mc), me) for j, chip in enumerate(chips)]
    return mine, first, passed, recv_ici, recv_d2d


def _gather_start(*refs):
    mine, first, _, _, _ = _gather_copies(*refs)
    mine.start()
    for cp in first:
        cp.start()


def _gather_finish(*refs):
    mine, first, passed, recv_ici, recv_d2d = _gather_copies(*refs)
    for landed, onward in zip(recv_ici, passed):
        landed.wait_recv()
        onward.start()
    for landed in recv_d2d:
        landed.wait_recv()
    for cp in first + passed:
        cp.wait_send()
    mine.wait()


def _a2a_copies(g_ref, out_ref, send_sems, recv_sems, local_sem):
    mx, my, mc = lax.axis_index("x"), lax.axis_index("y"), lax.axis_index("c")
    me = 4 * mx + 2 * my + mc
    mine = pltpu.make_async_copy(g_ref.at[me], out_ref.at[me], local_sem)
    copies = []
    for k in range(1, N_DEV):
        px = 1 - mx if (k >> 2) & 1 else mx
        py = 1 - my if (k >> 1) & 1 else my
        pc = 1 - mc if k & 1 else mc
        copies.append(pltpu.make_async_remote_copy(
            src_ref=g_ref.at[4 * px + 2 * py + pc], dst_ref=out_ref.at[me],
            send_sem=send_sems.at[k - 1], recv_sem=recv_sems.at[k - 1],
            device_id=(px, py, pc), device_id_type=MESH))
    return mine, copies


def _a2a_start(*refs):
    mine, copies = _a2a_copies(*refs)
    mine.start()
    for cp in copies:
        cp.start()


def _a2a_finish(*refs):
    mine, copies = _a2a_copies(*refs)
    for cp in copies:
        cp.wait()
    mine.wait()


_EXCHANGES = {"gather": (_gather_start, _gather_finish), "a2a": (_a2a_start, _a2a_finish)}


def _exchange_shape(kind, src):
    return _sds((N_DEV,) + src.shape if kind == "gather" else src.shape, src.dtype)


def _all_gather(x, name):
    def body(x_ref, out_ref, *sems):
        _gather_start(x_ref, out_ref, *sems)
        _gather_finish(x_ref, out_ref, *sems)

    return _pcall(body, name=name, out_shape=_exchange_shape("gather", x), in_specs=[_ANY], out_specs=_ANY,
                  scratch_shapes=list(_SEMS))(x)


def _pcall_ride(body, ride, args, *, name, grid, in_specs, out_specs, out_shape, scratch_shapes=(), compiler_params=None):
    if ride is None:
        out = _pcall(body, name=name, grid=grid, in_specs=in_specs, out_specs=out_specs, out_shape=out_shape,
                     scratch_shapes=list(scratch_shapes), compiler_params=compiler_params)(*args)
        return out, None
    kind, src = ride
    start, finish = _EXCHANGES[kind]
    single = not isinstance(out_shape, (tuple, list))
    out_specs_t = (out_specs,) if single else tuple(out_specs)
    out_shape_t = (out_shape,) if single else tuple(out_shape)
    n_in, n_out, n_sc = len(in_specs), len(out_shape_t), len(scratch_shapes)

    def wrapped(*refs):
        ins, src_ref = refs[:n_in], refs[n_in]
        outs, dst_ref = refs[n_in + 1:n_in + 1 + n_out], refs[n_in + 1 + n_out]
        scratch = refs[n_in + 2 + n_out:n_in + 2 + n_out + n_sc]
        sems = refs[n_in + 2 + n_out + n_sc:]
        first = pl.program_id(0) == 0
        last = pl.program_id(0) == grid[0] - 1
        for ax in range(1, len(grid)):
            first = jnp.logical_and(first, pl.program_id(ax) == 0)
            last = jnp.logical_and(last, pl.program_id(ax) == grid[ax] - 1)

        @pl.when(first)
        def _():
            start(src_ref, dst_ref, *sems)

        body(*ins, *outs, *scratch)

        @pl.when(last)
        def _():
            finish(src_ref, dst_ref, *sems)

    res = _pcall(wrapped, name=name, grid=grid, in_specs=list(in_specs) + [_ANY], out_specs=out_specs_t + (_ANY,),
                 out_shape=out_shape_t + (_exchange_shape(kind, src),),
                 scratch_shapes=list(scratch_shapes) + list(_SEMS), compiler_params=compiler_params)(*args, src)
    return (res[0] if single else tuple(res[:-1])), res[-1]


def _mm_nn(a, w, out_dtype, name, square_lhs=False, relu_out=False, ride=None):
    m, k = a.shape
    nb, _, bn = w.shape

    def body(a_ref, w_ref, o_ref):
        av = a_ref[...]
        if square_lhs:
            av = av * av
        for j in range(nb):
            r = jnp.dot(av, w_ref[j], preferred_element_type=F32)
            if relu_out:
                r = jnp.maximum(r, 0.0)
            o_ref[:, j * bn:(j + 1) * bn] = r.astype(out_dtype)

    out, ex = _pcall_ride(
        body, ride, (a, w), name=name, grid=(m // TM,),
        in_specs=[pl.BlockSpec((TM, k), lambda i: (i, 0)), pl.BlockSpec((nb, k, bn), lambda i: (0, 0, 0))],
        out_specs=pl.BlockSpec((TM, nb * bn), lambda i: (i, 0)),
        out_shape=_sds((m, nb * bn), out_dtype), compiler_params=_cp())
    return out if ride is None else (out, ex)


def _mm_nt(dy, w, out_dtype, name, relu_mul=None, ride=None):
    m = dy.shape[0]
    nb, k, bn = w.shape

    def body(*refs):
        if relu_mul is None:
            dy_ref, w_ref, o_ref = refs
        else:
            dy_ref, w_ref, r_ref, o_ref = refs
        acc = _dot_nt(dy_ref[:, 0:bn], w_ref[0])
        for j in range(1, nb):
            acc = acc + _dot_nt(dy_ref[:, j * bn:(j + 1) * bn], w_ref[j])
        if relu_mul is not None:
            acc = acc * (2.0 * r_ref[...].astype(F32))
        o_ref[...] = acc.astype(out_dtype)

    in_specs = [pl.BlockSpec((TM, nb * bn), lambda i: (i, 0)), pl.BlockSpec((nb, k, bn), lambda i: (0, 0, 0))]
    args = [dy, w]
    if relu_mul is not None:
        in_specs.append(pl.BlockSpec((TM, k), lambda i: (i, 0)))
        args.append(relu_mul)
    out, ex = _pcall_ride(
        body, ride, args, name=name, grid=(m // TM,), in_specs=in_specs,
        out_specs=pl.BlockSpec((TM, k), lambda i: (i, 0)),
        out_shape=_sds((m, k), out_dtype), compiler_params=_cp())
    return out if ride is None else (out, ex)


def _mm_tn(a, b, name, col_blocks, block, out_dtype, tm, square_lhs=False):
    m, k = a.shape
    nn = b.shape[1]
    steps = m // tm
    if col_blocks:
        nblk, acc_shape = nn // block, (k, block)
        a_spec = pl.BlockSpec((tm, k), lambda j, s: (s, 0))
        b_spec = pl.BlockSpec((tm, block), lambda j, s: (s, j))
    else:
        nblk, acc_shape = k // block, (block, nn)
        a_spec = pl.BlockSpec((tm, block), lambda j, s: (s, j))
        b_spec = pl.BlockSpec((tm, nn), lambda j, s: (s, 0))

    def body(a_ref, b_ref, o_ref, acc):
        s = pl.program_id(1)

        @pl.when(s == 0)
        def _():
            acc[...] = jnp.zeros_like(acc)

        av = a_ref[...]
        if square_lhs:
            av = av.astype(F32)
            av = (av * av).astype(BF16)
        acc[...] += _dot_tn(av, b_ref[...])

        @pl.when(s == steps - 1)
        def _():
            o_ref[...] = acc[...].astype(out_dtype)

    return _pcall(
        body, name=name, grid=(nblk, steps), in_specs=[a_spec, b_spec],
        out_specs=pl.BlockSpec((None,) + acc_shape, lambda j, s: (j, 0, 0)),
        out_shape=_sds((nblk,) + acc_shape, out_dtype),
        scratch_shapes=[pltpu.VMEM(acc_shape, F32)], compiler_params=_cp(),
    )(a, b)


def _mm_wgrad(at, b, name, bn, out_dtype, transpose_out=False, square_rhs=False):
    k, m = at.shape
    nblk = b.shape[1] // bn

    def body(a_ref, b_ref, o_ref):
        bv = b_ref[...]
        if square_rhs:
            bv = bv * bv
        r = jnp.dot(a_ref[...], bv, preferred_element_type=F32)
        o_ref[...] = (r.T if transpose_out else r).astype(out_dtype)

    out_block = (bn, k) if transpose_out else (k, bn)
    return _pcall(
        body, name=name, grid=(nblk,),
        in_specs=[pl.BlockSpec((k, m), lambda j: (0, 0)), pl.BlockSpec((m, bn), lambda j: (0, j))],
        out_specs=pl.BlockSpec((None,) + out_block, lambda j: (j, 0, 0)),
        out_shape=_sds((nblk,) + out_block, out_dtype), compiler_params=_cp(),
    )(at, b)


def _mod_part(s16, w_ada, b_cols):
    def body(s_ref, w_ref, b_ref, o_ref):
        o_ref[...] = _dot(s_ref[...], w_ref[...]) + b_ref[...]

    return _pcall(body, name="mod_part", out_shape=_sds((s16.shape[0], w_ada.shape[1]), F32),
                  compiler_params=_cp())(s16, w_ada, b_cols)


def _ada_bwd(s16, dm_cols, w_ada):
    def body(s_ref, d_ref, w_ref, gw_ref, ds_ref):
        gw_ref[...] = _dot_tn(s_ref[...], d_ref[...])
        ds_ref[...] = _dot_nt(d_ref[...], w_ref[...])

    return _pcall(body, name="ada_bwd",
                  out_shape=(_sds(w_ada.shape, F32), _sds(s16.shape, F32)), compiler_params=_cp())(s16, dm_cols, w_ada)


def _norm1_fwd(ctx, x, g, modrows, cb, ride=None):
    l_len, d = ctx.shape
    nb = (l_len + x.shape[0]) // TM

    def body(ctx_ref, x_ref, g_ref, m_ref, o_ref, ot_ref):
        is_ctx = pl.program_id(0) < cb
        xin = jnp.where(is_ctx, ctx_ref[...], x_ref[...])
        sh = jnp.where(is_ctx, m_ref[R_CSH1:R_CSH1 + 1, :], m_ref[R_SH1:R_SH1 + 1, :])
        sc = jnp.where(is_ctx, m_ref[R_CSC1:R_CSC1 + 1, :], m_ref[R_SC1:R_SC1 + 1, :])
        ms = jnp.mean(xin * xin, axis=-1, keepdims=True)
        n = xin * lax.rsqrt(ms + EPS) * g_ref[...]
        hn = n * (1.0 + sc) + sh
        o_ref[...] = hn.astype(BF16)
        ot_ref[...] = hn.T.astype(BF16)

    return _pcall_ride(
        body, ride, (ctx, x, g, modrows), name="norm1_fwd", grid=(nb,),
        in_specs=[pl.BlockSpec((TM, d), lambda i: (jnp.minimum(i, cb - 1), 0)),
                  pl.BlockSpec((TM, d), lambda i: (jnp.maximum(i - cb, 0), 0)),
                  pl.BlockSpec((1, d), lambda i: (0, 0)), pl.BlockSpec((8, d), lambda i: (0, 0))],
        out_specs=(pl.BlockSpec((TM, d), lambda i: (i, 0)), pl.BlockSpec((d, TM), lambda i: (0, i))),
        out_shape=(_sds((nb * TM, d), BF16), _sds((d, nb * TM), BF16)), compiler_params=_cp())


def _mix_fwd(o_f, o_b, p, h_f, h_b, cb, t_len):
    def body(of_ref, ob_ref, g_ref, gate_ref, hf_ref, hb_ref, mix_ref):
        o = of_ref[...] + ob_ref[...]
        g = g_ref[...]
        sg = g * _sigmoid(g)
        for hh in range(HEADS):
            sl = slice(hh * HEAD_DIM, (hh + 1) * HEAD_DIM)
            oh = o[:, sl]
            yc = oh - jnp.mean(oh, axis=-1, keepdims=True)
            var = jnp.mean(yc * yc, axis=-1, keepdims=True)
            mix_ref[:, sl] = (sg[:, sl] * (yc * lax.rsqrt(var + EPS))).astype(BF16)
        mix_ref[:, RET_W:] = ((hf_ref[...] + hb_ref[...]) * _gelu(gate_ref[...])).astype(BF16)

    row = lambda i: (i + cb, 0)
    return _pcall(
        body, name="mix_fwd", grid=(t_len // TM,),
        in_specs=[pl.BlockSpec((TM, RET_W), row), pl.BlockSpec((TM, RET_W), row),
                  pl.BlockSpec((TM, RET_W), lambda i: (i + cb, 3)), pl.BlockSpec((TM, LRU_W), lambda i: (i + cb, 5)),
                  pl.BlockSpec((TM, LRU_W), row), pl.BlockSpec((TM, LRU_W), row)],
        out_specs=pl.BlockSpec((TM, RET_W + LRU_W), lambda i: (i, 0)),
        out_shape=_sds((t_len, RET_W + LRU_W), BF16), compiler_params=_cp(),
    )(o_f, o_b, p, p, h_f, h_b)


def _res_norm2(x, y, g, modrows):
    t_len, d = x.shape

    def body(x_ref, y_ref, g_ref, m_ref, x1_ref, h2_ref, h2t_ref):
        x1 = x_ref[...] + m_ref[R_G1:R_G1 + 1, :] * y_ref[...]
        ms = jnp.mean(x1 * x1, axis=-1, keepdims=True)
        n = x1 * lax.rsqrt(ms + EPS) * g_ref[...]
        x1_ref[...] = x1
        h2 = n * (1.0 + m_ref[R_SC2:R_SC2 + 1, :]) + m_ref[R_SH2:R_SH2 + 1, :]
        h2_ref[...] = h2.astype(BF16)
        h2t_ref[...] = h2.T.astype(BF16)

    t = pl.BlockSpec((TM, d), lambda i: (i, 0))
    tt = pl.BlockSpec((d, TM), lambda i: (0, i))
    return _pcall(
        body, name="res_norm2", grid=(t_len // TM,),
        in_specs=[t, t, pl.BlockSpec((1, d), lambda i: (0, 0)), pl.BlockSpec((8, d), lambda i: (0, 0))],
        out_specs=(t, t, tt), out_shape=(_sds((t_len, d), F32), _sds((t_len, d), BF16), _sds((d, t_len), BF16)),
        compiler_params=_cp(),
    )(x, y, g, modrows)


def _final_fwd_bwd(x1, z, target, fg, modrows):
    t_len, d = x1.shape

    def body(x1_ref, z_ref, t_ref, fg_ref, m_ref, dx2_ref, dz_ref, dzt_ref, acc_ref):
        @pl.when(pl.program_id(0) == 0)
        def _():
            acc_ref[...] = jnp.zeros_like(acc_ref)

        g2 = m_ref[R_G2:R_G2 + 1, :]
        z = z_ref[...]
        x2 = x1_ref[...] + g2 * z
        rstd = lax.rsqrt(jnp.mean(x2 * x2, axis=-1, keepdims=True) + EPS)
        xh = x2 * rstd
        fg = fg_ref[...]
        e = xh * fg - t_ref[...]
        dy = e * (1.0 / d)
        dxh = dy * fg
        dx2 = rstd * (dxh - xh * jnp.mean(dxh * xh, axis=-1, keepdims=True))
        dx2_ref[...] = dx2
        dz = g2 * dx2
        dz_ref[...] = dz.astype(BF16)
        dzt_ref[...] = dz.T.astype(BF16)
        acc_ref[0:1, :] += jnp.sum(dy * xh, axis=0, keepdims=True)
        acc_ref[1:2, :] += jnp.sum(dx2 * z, axis=0, keepdims=True)
        acc_ref[2:3, :] += jnp.sum(e * e, axis=0, keepdims=True)

    t = pl.BlockSpec((TM, d), lambda i: (i, 0))
    return _pcall(
        body, name="final_fwd_bwd", grid=(t_len // TM,),
        in_specs=[t, t, t, pl.BlockSpec((1, d), lambda i: (0, 0)), pl.BlockSpec((8, d), lambda i: (0, 0))],
        out_specs=(t, t, pl.BlockSpec((d, TM), lambda i: (0, i)), pl.BlockSpec((8, d), lambda i: (0, 0))),
        out_shape=(_sds((t_len, d), F32), _sds((t_len, d), BF16), _sds((d, t_len), BF16), _sds((8, d), F32)),
        compiler_params=_cp(),
    )(x1, z, target, fg, modrows)


def _bwd_norm2(dh2, x1, dx2, y, g, modrows):
    t_len, d = x1.shape

    def body(dh2_ref, x1_ref, dx2_ref, y_ref, g_ref, m_ref, dx1_ref, dy_ref, acc_ref):
        @pl.when(pl.program_id(0) == 0)
        def _():
            acc_ref[...] = jnp.zeros_like(acc_ref)

        x1 = x1_ref[...]
        rstd = lax.rsqrt(jnp.mean(x1 * x1, axis=-1, keepdims=True) + EPS)
        xh = x1 * rstd
        gn = g_ref[...]
        dh2 = dh2_ref[...]
        dn = dh2 * (1.0 + m_ref[R_SC2:R_SC2 + 1, :])
        dxh = dn * gn
        dx1 = dx2_ref[...] + rstd * (dxh - xh * jnp.mean(dxh * xh, axis=-1, keepdims=True))
        dx1_ref[...] = dx1
        dy_ref[...] = (m_ref[R_G1:R_G1 + 1, :] * dx1).astype(BF16)
        acc_ref[0:1, :] += jnp.sum(dh2, axis=0, keepdims=True)
        acc_ref[1:2, :] += jnp.sum(dh2 * xh * gn, axis=0, keepdims=True)
        acc_ref[2:3, :] += jnp.sum(dn * xh, axis=0, keepdims=True)
        acc_ref[3:4, :] += jnp.sum(dx1 * y_ref[...], axis=0, keepdims=True)

    t = pl.BlockSpec((TM, d), lambda i: (i, 0))
    return _pcall(
        body, name="bwd_norm2", grid=(t_len // TM,),
        in_specs=[t, t, t, t, pl.BlockSpec((1, d), lambda i: (0, 0)), pl.BlockSpec((8, d), lambda i: (0, 0))],
        out_specs=(t, t, pl.BlockSpec((8, d), lambda i: (0, 0))),
        out_shape=(_sds((t_len, d), F32), _sds((t_len, d), BF16), _sds((8, d), F32)), compiler_params=_cp(),
    )(dh2, x1, dx2, y, g, modrows)


def _mix_bwd(dmix, o_f, o_b, p, h_f, h_b, cb):
    n = o_f.shape[0]

    def body(dm_ref, of_ref, ob_ref, g_ref, gate_ref, hf_ref, hb_ref, do_ref, dg_ref, dgate_ref, dhs_ref):
        is_ctx = pl.program_id(0) < cb

        @pl.when(is_ctx)
        def _():
            for r in (do_ref, dg_ref, dgate_ref, dhs_ref):
                r[...] = jnp.zeros_like(r)

        @pl.when(jnp.logical_not(is_ctx))
        def _():
            o = of_ref[...] + ob_ref[...]
            g = g_ref[...]
            s = _sigmoid(g)
            sg = g * s
            dsg = s * (1.0 + g * (1.0 - s))
            for hh in range(HEADS):
                sl = slice(hh * HEAD_DIM, (hh + 1) * HEAD_DIM)
                oh = o[:, sl]
                yc = oh - jnp.mean(oh, axis=-1, keepdims=True)
                rs = lax.rsqrt(jnp.mean(yc * yc, axis=-1, keepdims=True) + EPS)
                gn = yc * rs
                dret = dm_ref[:, sl]
                dgn = dret * sg[:, sl]
                dg_ref[:, sl] = dret * gn * dsg[:, sl]
                do_ref[:, sl] = rs * (dgn - jnp.mean(dgn, axis=-1, keepdims=True)
                                      - gn * jnp.mean(dgn * gn, axis=-1, keepdims=True))
            dlru = dm_ref[:, RET_W:]
            gate = gate_ref[...]
            dhs_ref[...] = dlru * _gelu(gate)
            dgate_ref[...] = dlru * (hf_ref[...] + hb_ref[...]) * _dgelu(gate)

    row = lambda i: (i, 0)
    t = pl.BlockSpec((TM, RET_W), row)
    return _pcall(
        body, name="mix_bwd", grid=(n // TM,),
        in_specs=[pl.BlockSpec((TM, RET_W + LRU_W), lambda i: (jnp.maximum(i - cb, 0), 0)), t, t,
                  pl.BlockSpec((TM, RET_W), lambda i: (i, 3)), pl.BlockSpec((TM, LRU_W), lambda i: (i, 5)), t, t],
        out_specs=(t, t, t, t), out_shape=tuple(_sds((n, RET_W), F32) for _ in range(4)), compiler_params=_cp(),
    )(dmix, o_f, o_b, p, p, h_f, h_b)


def _bwd_norm1(dhn, ctx, x, dx1, g, modrows, cb):
    t_len, d = x.shape
    nb = dhn.shape[0] // TM

    def body(dh_ref, ctx_ref, x_ref, dx1_ref, g_ref, m_ref, gx_ref, acc_ref):
        is_ctx = pl.program_id(0) < cb

        @pl.when(pl.program_id(0) == 0)
        def _():
            acc_ref[...] = jnp.zeros_like(acc_ref)

        xin = jnp.where(is_ctx, ctx_ref[...], x_ref[...])
        sc = jnp.where(is_ctx, m_ref[R_CSC1:R_CSC1 + 1, :], m_ref[R_SC1:R_SC1 + 1, :])
        rstd = lax.rsqrt(jnp.mean(xin * xin, axis=-1, keepdims=True) + EPS)
        xh = xin * rstd
        gn = g_ref[...]
        dh = dh_ref[...]
        dn = dh * (1.0 + sc)
        dxh = dn * gn
        gx_ref[...] = dx1_ref[...] + rstd * (dxh - xh * jnp.mean(dxh * xh, axis=-1, keepdims=True))
        s0 = jnp.sum(dh, axis=0, keepdims=True)
        s1 = jnp.sum(dh * xh * gn, axis=0, keepdims=True)
        acc_ref[4:5, :] += jnp.sum(dn * xh, axis=0, keepdims=True)

        @pl.when(is_ctx)
        def _():
            acc_ref[0:1, :] += s0
            acc_ref[1:2, :] += s1

        @pl.when(jnp.logical_not(is_ctx))
        def _():
            acc_ref[2:3, :] += s0
            acc_ref[3:4, :] += s1

    lat = pl.BlockSpec((TM, d), lambda i: (jnp.maximum(i - cb, 0), 0))
    return _pcall(
        body, name="bwd_norm1", grid=(nb,),
        in_specs=[pl.BlockSpec((TM, d), lambda i: (i, 0)), pl.BlockSpec((TM, d), lambda i: (jnp.minimum(i, cb - 1), 0)),
                  lat, lat, pl.BlockSpec((1, d), lambda i: (0, 0)), pl.BlockSpec((8, d), lambda i: (0, 0))],
        out_specs=(lat, pl.BlockSpec((8, d), lambda i: (0, 0))),
        out_shape=(_sds((t_len, d), F32), _sds((8, d), F32)), compiler_params=_cp(),
    )(dhn, ctx, x, dx1, g, modrows)


def _rot(x, cf, ss):
    return x * cf + pltpu.roll(x, HEAD_DIM // 2, 1) * ss


def _decay_exponents(dirn):
    ii = lax.broadcasted_iota(jnp.int32, (CHUNK, CHUNK), 0)
    jj = lax.broadcasted_iota(jnp.int32, (CHUNK, CHUNK), 1)
    rel = ii - jj if dirn == 0 else jj - ii
    pos = ii.astype(F32)
    if dirn == 0:
        cq, cs = pos + 1.0, (CHUNK - 1.0) - pos
    else:
        cq, cs = CHUNK - pos, pos
    return rel, jnp.maximum(rel, 0).astype(F32), cq, cs


def _store_decay(lg_ref, dec):
    for dirn in (0, 1):
        rel, relf, cq, cs = _decay_exponents(dirn)
        for h in range(HEADS):
            lgv = lg_ref[dirn, h]
            dec[dirn, h, 0] = jnp.where(rel >= 0, jnp.exp(lgv * relf), 0.0)
            dec[dirn, h, 1] = jnp.exp(lgv * cq)
            dec[dirn, h, 2] = jnp.exp(lgv * cs)


def _ret_rows(cc, nc, step_of):
    return [lambda s, dirn=dirn: _tile_order(dirn, step_of(s), cc, nc) for dirn in (0, 1)]


def _ret_in_specs(rows):
    specs = []
    for row in rows:
        specs += [pl.BlockSpec((CHUNK, RET_W), lambda s, o=o, row=row: (row(s), o)) for o in (0, 1, 2)]
        specs += [pl.BlockSpec((CHUNK, HEAD_DIM), lambda s, row=row: (row(s), 0))] * 2
    return specs


def _ret_fwd(p, cosf, sins, lg, cc, ride=None):
    n = p.shape[0]
    nc = n // CHUNK
    rows = _ret_rows(cc, nc, lambda s: s)

    def body(lg_ref, q0, k0, v0, c0, s0, q1, k1, v1, c1, s1, o0, o1, sp0, sp1, st, dec):
        @pl.when(pl.program_id(0) == 0)
        def _():
            st[...] = jnp.zeros_like(st)
            _store_decay(lg_ref, dec)

        for dirn, (q_ref, k_ref, v_ref, c_ref, s_ref, o_ref, sp_ref) in enumerate(
                ((q0, k0, v0, c0, s0, o0, sp0), (q1, k1, v1, c1, s1, o1, sp1))):
            cf, ss = c_ref[...], s_ref[...]
            for h in range(HEADS):
                sl = slice(h * HEAD_DIM, (h + 1) * HEAD_DIM)
                q = _rot(q_ref[:, sl], cf, ss)
                k = _rot(k_ref[:, sl], cf, ss) * K_SCALE
                v = v_ref[:, sl]
                sp = st[dirn, h]
                sp_ref[h] = sp
                o_ref[:, sl] = _dot(_dot_nt(q, k) * dec[dirn, h, 0], v) + _dot(q * dec[dirn, h, 1], sp)
                st[dirn, h] = jnp.exp(lg_ref[dirn, h] * CHUNK) * sp + _dot_tn(k * dec[dirn, h, 2], v)

    o_specs = [pl.BlockSpec((CHUNK, RET_W), lambda s, row=row: (row(s), 0)) for row in rows]
    state = pl.BlockSpec((None, HEADS, CHUNK, HEAD_DIM), lambda s: (s, 0, 0, 0))
    return _pcall_ride(
        body, ride, (lg, p, p, p, cosf, sins, p, p, p, cosf, sins), name="ret_fwd", grid=(nc,),
        in_specs=[pl.BlockSpec(memory_space=pltpu.SMEM)] + _ret_in_specs(rows),
        out_specs=(o_specs[0], o_specs[1], state, state),
        out_shape=(_sds((n, RET_W), F32),) * 2 + (_sds((nc, HEADS, CHUNK, HEAD_DIM), F32),) * 2,
        scratch_shapes=[pltpu.VMEM((2, HEADS, CHUNK, HEAD_DIM), F32), pltpu.VMEM((2, HEADS, 3, CHUNK, CHUNK), F32)],
        compiler_params=_cp())


def _ret_bwd(p, cosf, sins, lg, do, s_prev, cc, ride=None):
    n = p.shape[0]
    nc = n // CHUNK
    rows = _ret_rows(cc, nc, lambda s: nc - 1 - s)

    def body(lg_ref, q0, k0, v0, c0, s0, q1, k1, v1, c1, s1, do0, do1, sp0, sp1,
             dq0, dk0, dv0, dq1, dk1, dv1, dlg_ref, dst, dec):
        @pl.when(pl.program_id(0) == 0)
        def _():
            dst[...] = jnp.zeros_like(dst)
            dlg_ref[...] = jnp.zeros_like(dlg_ref)
            _store_decay(lg_ref, dec)

        for dirn, (q_ref, k_ref, v_ref, c_ref, s_ref, do_ref, sp_ref, dq_ref, dk_ref, dv_ref) in enumerate(
                ((q0, k0, v0, c0, s0, do0, sp0, dq0, dk0, dv0), (q1, k1, v1, c1, s1, do1, sp1, dq1, dk1, dv1))):
            cf, ss = c_ref[...], s_ref[...]
            _, relf, cq, cs = _decay_exponents(dirn)
            for h in range(HEADS):
                sl = slice(h * HEAD_DIM, (h + 1) * HEAD_DIM)
                q = _rot(q_ref[:, sl], cf, ss)
                k = _rot(k_ref[:, sl], cf, ss) * K_SCALE
                v = v_ref[:, sl]
                dov = do_ref[:, sl]
                dm, wq, ws = dec[dirn, h, 0], dec[dirn, h, 1], dec[dirn, h, 2]
                a = _dot_nt(q, k)
                sp = sp_ref[h]
                dsn = dst[dirn, h]
                gc = jnp.exp(lg_ref[dirn, h] * CHUNK)
                g1 = _dot_nt(dov, sp)
                da = _dot_nt(dov, v) * dm
                dq = g1 * wq + _dot(da, k)
                h1 = _dot_nt(v, dsn)
                dk = _dot_tn(da, q) + h1 * ws
                dv_ref[:, sl] = _dot_tn(a * dm, dov) + _dot(k * ws, dsn)
                dst[dirn, h] = gc * dsn + _dot_tn(q * wq, dov)
                term = da * a * relf + q * g1 * (wq * cq) + k * h1 * (ws * cs) + sp * dsn * (CHUNK * gc)
                dlg_ref[dirn * HEADS + h:dirn * HEADS + h + 1, 0:HEAD_DIM] += jnp.sum(term, axis=0, keepdims=True)
                dq_ref[:, sl] = dq * cf + pltpu.roll(dq * ss, HEAD_DIM // 2, 1)
                dk_ref[:, sl] = (dk * cf + pltpu.roll(dk * ss, HEAD_DIM // 2, 1)) * K_SCALE

    wide = [pl.BlockSpec((CHUNK, RET_W), lambda s, row=row: (row(s), 0)) for row in rows]
    state = pl.BlockSpec((None, HEADS, CHUNK, HEAD_DIM), lambda s: (nc - 1 - s, 0, 0, 0))
    return _pcall_ride(
        body, ride, (lg, p, p, p, cosf, sins, p, p, p, cosf, sins, do, do, s_prev[0], s_prev[1]), name="ret_bwd", grid=(nc,),
        in_specs=[pl.BlockSpec(memory_space=pltpu.SMEM)] + _ret_in_specs(rows) + wide + [state, state],
        out_specs=(wide[0],) * 3 + (wide[1],) * 3 + (pl.BlockSpec((2 * HEADS, 8 * HEAD_DIM), lambda s: (0, 0)),),
        out_shape=(_sds((n, RET_W), F32),) * 6 + (_sds((2 * HEADS, 8 * HEAD_DIM), F32),),
        scratch_shapes=[pltpu.VMEM((2, HEADS, CHUNK, HEAD_DIM), F32), pltpu.VMEM((2, HEADS, 3, CHUNK, CHUNK), F32)],
        compiler_params=_cp())


def _shift_rows(cur, prev8, next8, k, seg_start, seg_end):
    tm = cur.shape[0]
    rows = _rows_iota(cur.shape)
    if k < 0:
        out = pltpu.roll(cur, -k, 0)
        for j in range(-k):
            halo = jnp.where(seg_start, 0.0, prev8[SUB + k + j:SUB + k + j + 1, :])
            out = jnp.where(rows == j, halo, out)
    else:
        out = pltpu.roll(cur, tm - k, 0)
        for j in range(k):
            halo = jnp.where(seg_end, 0.0, next8[j:j + 1, :])
            out = jnp.where(rows == tm - k + j, halo, out)
    return out


def _seg_flags(t, cb, nb):
    return jnp.logical_or(t == 0, t == cb), jnp.logical_or(t == cb - 1, t == nb - 1)


def _halo_specs(tile_of, n_rows, col):
    per = TM // SUB
    return [pl.BlockSpec((TM, LRU_W), lambda s: (tile_of(s), col)),
            pl.BlockSpec((SUB, LRU_W), lambda s: (jnp.maximum(tile_of(s) * per - 1, 0), col)),
            pl.BlockSpec((SUB, LRU_W), lambda s: (jnp.minimum((tile_of(s) + 1) * per, n_rows // SUB - 1), col))]


def _lru_gates(xr, prev8, next8, seg_start, seg_end, cw_ref, cb_ref, wg_ref, bg_ref, sp_ref):
    xm1 = _shift_rows(xr, prev8, next8, -1, seg_start, seg_end)
    xp1 = _shift_rows(xr, prev8, next8, 1, seg_start, seg_end)
    xp2 = _shift_rows(xr, prev8, next8, 2, seg_start, seg_end)
    xc = cb_ref[...] + xm1 * cw_ref[0:1, :] + xr * cw_ref[1:2, :] + xp1 * cw_ref[2:3, :] + xp2 * cw_ref[3:4, :]
    pre = _dot(xc, wg_ref[...]) + bg_ref[...]
    r = _sigmoid(pre[:, :LRU_W])
    i = _sigmoid(pre[:, LRU_W:])
    la = (-LRU_C) * r * sp_ref[...]
    a = jnp.exp(la)
    th = jnp.tanh(la)
    sq = jnp.sqrt(-2.0 * th / (1.0 - th))
    return xc, r, i, a, sq


def _scan_tile(a, b, ascending, a_sc, b_sc, carry, out_ref):
    tm, w = a.shape
    nsub = tm // SUB
    a = a.reshape(nsub, SUB, w)
    b = b.reshape(nsub, SUB, w)
    r8 = lax.broadcasted_iota(jnp.int32, a.shape, 1)
    for k in (1, 2, 4):
        if ascending:
            m = r8 >= k
            a_s, b_s = pltpu.roll(a, k, 1), pltpu.roll(b, k, 1)
        else:
            m = r8 < SUB - k
            a_s, b_s = pltpu.roll(a, SUB - k, 1), pltpu.roll(b, SUB - k, 1)
        b = a * jnp.where(m, b_s, 0.0) + b
        a = a * jnp.where(m, a_s, 1.0)
    a_sc[...] = a.reshape(tm, w)
    b_sc[...] = b.reshape(tm, w)

    def step(j, c):
        off = pl.multiple_of((j if ascending else nsub - 1 - j) * SUB, SUB)
        hb = a_sc[pl.ds(off, SUB), :] * c + b_sc[pl.ds(off, SUB), :]
        out_ref[pl.ds(off, SUB), :] = hb
        last = hb[SUB - 1:SUB, :] if ascending else hb[0:1, :]
        return jnp.broadcast_to(last, c.shape)

    carry[...] = lax.fori_loop(0, nsub, step, carry[...])


def _lru_fwd(p, wg, bg, sp, cw, cbias, dirn, cb, ride=None):
    n = p.shape[0]
    nb = n // TM
    tile_of = lambda s: _tile_order(dirn, s, cb, nb)

    def body(x_ref, xp_ref, xn_ref, wg_ref, bg_ref, sp_ref, cw_ref, cb_ref, h_ref, cin_ref, carry, a_sc, b_sc):
        s = pl.program_id(0)

        @pl.when(s == 0)
        def _():
            carry[...] = jnp.zeros_like(carry)

        seg_start, seg_end = _seg_flags(tile_of(s), cb, nb)
        xc, r, i, a, sq = _lru_gates(x_ref[...], xp_ref[...], xn_ref[...], seg_start, seg_end,
                                     cw_ref, cb_ref, wg_ref, bg_ref, sp_ref)
        cin_ref[...] = carry[...]
        _scan_tile(a, sq * (i * xc), dirn == 0, a_sc, b_sc, carry, h_ref)

    full = lambda shape: pl.BlockSpec(shape, lambda s: (0,) * len(shape))
    return _pcall_ride(
        body, ride, (p, p, p, wg, bg, sp, cw, cbias), name=f"lru_fwd{dirn}", grid=(nb,),
        in_specs=_halo_specs(tile_of, n, 4) + [full((LRU_W, 2 * LRU_W)), full((1, 2 * LRU_W)), full((1, LRU_W)),
                                               full((4, LRU_W)), full((1, LRU_W))],
        out_specs=(pl.BlockSpec((TM, LRU_W), lambda s: (tile_of(s), 0)),
                   pl.BlockSpec((None, SUB, LRU_W), lambda s: (tile_of(s), 0, 0))),
        out_shape=(_sds((n, LRU_W), F32), _sds((nb, SUB, LRU_W), F32)),
        scratch_shapes=[pltpu.VMEM((SUB, LRU_W), F32), pltpu.VMEM((TM, LRU_W), F32), pltpu.VMEM((TM, LRU_W), F32)],
        compiler_params=_cp())


def _lru_bwd(p, wg, bg, sp, cw, cbias, h, cin, dhs, dirn, cb, ride=None):
    n = p.shape[0]
    nb = n // TM
    tile_of = lambda s: _tile_order(dirn, nb - 1 - s, cb, nb)

    def body(x_ref, xp_ref, xn_ref, wg_ref, bg_ref, sp_ref, cw_ref, cb_ref, h_ref, cin_ref, dhs_ref,
             dxc_ref, dwd_ref, acc_ref, carry, a_sc, b_sc, mu_sc, dwg_ref):
        s = pl.program_id(0)

        @pl.when(s == 0)
        def _():
            carry[...] = jnp.zeros_like(carry)
            dwg_ref[...] = jnp.zeros_like(dwg_ref)
            acc_ref[...] = jnp.zeros_like(acc_ref)

        seg_start, seg_end = _seg_flags(tile_of(s), cb, nb)
        xc, r, i, a, sq = _lru_gates(x_ref[...], xp_ref[...], xn_ref[...], seg_start, seg_end,
                                     cw_ref, cb_ref, wg_ref, bg_ref, sp_ref)
        rows = _rows_iota(a.shape)
        hv = h_ref[...]
        dh = dhs_ref[...]
        mu_next = carry[0:1, :]
        _scan_tile(a, a * dh, dirn == 1, a_sc, b_sc, carry, mu_sc)
        mu = mu_sc[...]
        if dirn == 0:
            hprev = jnp.where(rows == 0, cin_ref[0:1, :], pltpu.roll(hv, 1, 0))
            lam = dh + jnp.where(rows == TM - 1, mu_next, pltpu.roll(mu, TM - 1, 0))
        else:
            hprev = jnp.where(rows == TM - 1, cin_ref[0:1, :], pltpu.roll(hv, TM - 1, 0))
            lam = dh + jnp.where(rows == 0, mu_next, pltpu.roll(mu, 1, 0))
        ds = lam * (i * xc)
        di = lam * (sq * xc)
        dla = lam * hprev * a - ds * (a * a) / jnp.maximum(sq, 1e-20)
        dpr = dla * ((-LRU_C) * sp_ref[...]) * r * (1.0 - r)
        dpi = di * i * (1.0 - i)
        dpre = jnp.concatenate([dpr, dpi], axis=1)
        dxc_ref[...] = lam * (sq * i) + _dot_nt(dpre, wg_ref[...])
        dwg_ref[...] += _dot_tn(xc, dpre)
        acc_ref[0:1, :] += jnp.sum(dpre, axis=0, keepdims=True)
        acc_ref[1:2, 0:LRU_W] += jnp.sum(dla * ((-LRU_C) * r), axis=0, keepdims=True)

        @pl.when(s == nb - 1)
        def _():
            low = lax.broadcasted_iota(jnp.int32, (LRU_BD, 2 * LRU_BD), 1) < LRU_BD
            for half in (0, LRU_W):
                for m in range(LRU_BLOCKS // 2):
                    lanes = slice(half + 2 * LRU_BD * m, half + 2 * LRU_BD * (m + 1))
                    even = dwg_ref[2 * m * LRU_BD:(2 * m + 1) * LRU_BD, lanes]
                    odd = dwg_ref[(2 * m + 1) * LRU_BD:(2 * m + 2) * LRU_BD, lanes]
                    dwd_ref[:, lanes] = jnp.where(low, even, odd)

    full = lambda shape: pl.BlockSpec(shape, lambda s: (0,) * len(shape))
    tile = pl.BlockSpec((TM, LRU_W), lambda s: (tile_of(s), 0))
    return _pcall_ride(
        body, ride, (p, p, p, wg, bg, sp, cw, cbias, h, cin, dhs), name=f"lru_bwd{dirn}", grid=(nb,),
        in_specs=_halo_specs(tile_of, n, 4) + [full((LRU_W, 2 * LRU_W)), full((1, 2 * LRU_W)), full((1, LRU_W)),
                                               full((4, LRU_W)), full((1, LRU_W)), tile,
                                               pl.BlockSpec((None, SUB, LRU_W), lambda s: (tile_of(s), 0, 0)), tile],
        out_specs=(tile, full((LRU_BD, 2 * LRU_W)), full((8, 2 * LRU_W))),
        out_shape=(_sds((n, LRU_W), F32), _sds((LRU_BD, 2 * LRU_W), F32), _sds((8, 2 * LRU_W), F32)),
        scratch_shapes=[pltpu.VMEM((SUB, LRU_W), F32)] + [pltpu.VMEM((TM, LRU_W), F32)] * 3
        + [pltpu.VMEM((LRU_W, 2 * LRU_W), F32)],
        compiler_params=_cp())


def _assemble_dp(dqs, dks, dvs, dg, dgate, dxcs, p, cw, cb, ride=None):
    n = p.shape[0]
    nb = n // TM
    tile_of = lambda s: s

    def body(dqf, dqb, dkf, dkb, dvf, dvb, dg_ref, dgate_ref, cf, pf, nf, cb_, pb, nb_, x_ref, xp_ref, xn_ref,
             cw_ref, dp_ref, acc_ref):
        s = pl.program_id(0)

        @pl.when(s == 0)
        def _():
            acc_ref[...] = jnp.zeros_like(acc_ref)

        seg_start, seg_end = _seg_flags(s, cb, nb)
        dp_ref[:, 0:RET_W] = (dqf[...] + dqb[...]).astype(BF16)
        dp_ref[:, RET_W:2 * RET_W] = (dkf[...] + dkb[...]).astype(BF16)
        dp_ref[:, 2 * RET_W:3 * RET_W] = (dvf[...] + dvb[...]).astype(BF16)
        dp_ref[:, 3 * RET_W:4 * RET_W] = dg_ref[...].astype(BF16)
        dxc = cf[...] + cb_[...]
        dprev = pf[...] + pb[...]
        dnext = nf[...] + nb_[...]
        dxr = (_shift_rows(dxc, dprev, dnext, 1, seg_start, seg_end) * cw_ref[0:1, :] + dxc * cw_ref[1:2, :]
               + _shift_rows(dxc, dprev, dnext, -1, seg_start, seg_end) * cw_ref[2:3, :]
               + _shift_rows(dxc, dprev, dnext, -2, seg_start, seg_end) * cw_ref[3:4, :])
        dp_ref[:, 4 * RET_W:4 * RET_W + LRU_W] = dxr.astype(BF16)
        dp_ref[:, 4 * RET_W + LRU_W:] = dgate_ref[...].astype(BF16)
        xr, xp, xn = x_ref[...], xp_ref[...], xn_ref[...]
        for j, k in enumerate((-1, 0, 1, 2)):
            xs = xr if k == 0 else _shift_rows(xr, xp, xn, k, seg_start, seg_end)
            acc_ref[j:j + 1, 0:LRU_W] += jnp.sum(dxc * xs, axis=0, keepdims=True)
        acc_ref[4:5, 0:LRU_W] += jnp.sum(dxc, axis=0, keepdims=True)

    t = pl.BlockSpec((TM, RET_W), lambda s: (s, 0))
    args = (dqs[0], dqs[1], dks[0], dks[1], dvs[0], dvs[1], dg, dgate, dxcs[0], dxcs[0], dxcs[0], dxcs[1], dxcs[1], dxcs[1],
            p, p, p, cw)
    return _pcall_ride(
        body, ride, args, name="assemble_dp", grid=(nb,),
        in_specs=[t] * 8 + _halo_specs(tile_of, n, 0) * 2 + _halo_specs(tile_of, n, 4)
        + [pl.BlockSpec((4, LRU_W), lambda s: (0, 0))],
        out_specs=(pl.BlockSpec((TM, 4 * RET_W + 2 * LRU_W), lambda s: (s, 0)), pl.BlockSpec((8, 2 * LRU_W), lambda s: (0, 0))),
        out_shape=(_sds((n, 4 * RET_W + 2 * LRU_W), BF16), _sds((8, 2 * LRU_W), F32)), compiler_params=_cp())


def _adamw(g, w, m, v):
    nm = ADAM_B1 * m + (1.0 - ADAM_B1) * g
    nv = ADAM_B2 * v + (1.0 - ADAM_B2) * (g * g)
    m_hat = nm / (1.0 - ADAM_B1 ** ADAM_STEP)
    v_hat = nv / (1.0 - ADAM_B2 ** ADAM_STEP)
    return (-ADAM_LR) * (m_hat / (jnp.sqrt(v_hat) + ADAM_EPS) + ADAM_WD * w), nm, nv


def _adam_many(items, name):
    n = len(items)

    def body(*refs):
        for i in range(n):
            g, w, m, v = (r[...] for r in refs[4 * i:4 * i + 4])
            for o_ref, val in zip(refs[4 * n + 3 * i:4 * n + 3 * i + 3], _adamw(g, w, m, v)):
                o_ref[...] = val

    out_shape = tuple(_sds(it[1].shape, F32) for it in items for _ in range(3))
    res = _pcall(body, name=name, out_shape=out_shape, compiler_params=_cp())(*[a for it in items for a in it])
    return [tuple(res[3 * i:3 * i + 3]) for i in range(n)]


def _sum_adam(parts, w, m, v, name):
    nparts, r, c = parts.shape
    tr = min(r, 128)

    def body(p_ref, w_ref, m_ref, v_ref, g_ref, d_ref, nm_ref, nv_ref):
        g = p_ref[0].astype(F32)
        for j in range(1, nparts):
            g = g + p_ref[j].astype(F32)
        g_ref[...] = g
        d_ref[...], nm_ref[...], nv_ref[...] = _adamw(g, w_ref[...], m_ref[...], v_ref[...])

    t = pl.BlockSpec((tr, c), lambda i: (i, 0))
    return _pcall(
        body, name=name, grid=(r // tr,),
        in_specs=[pl.BlockSpec((nparts, tr, c), lambda i: (0, i, 0)), t, t, t],
        out_specs=(t, t, t, t), out_shape=(_sds((r, c), F32),) * 4, compiler_params=_cp(),
    )(parts, w, m, v)


def _sum_parts(parts, name):
    nparts, r, c = parts.shape

    def body(p_ref, o_ref):
        g = p_ref[0]
        for j in range(1, nparts):
            g = g + p_ref[j]
        o_ref[...] = g

    return _pcall(body, name=name, out_shape=_sds((r, c), parts.dtype), compiler_params=_cp())(parts)


def _rot_tables(l_len, t_len):
    rows = t_len // GRID_W
    row = jnp.repeat(jnp.arange(rows, dtype=F32), GRID_W)
    col = jnp.tile(jnp.arange(GRID_W, dtype=F32), rows)
    n_freq = HEAD_DIM // 4
    inv = ROPE_BASE ** (-jnp.arange(n_freq, dtype=F32) / n_freq)
    ang = jnp.concatenate([row[:, None] * inv, col[:, None] * inv], axis=-1)
    cos, sin = jnp.cos(ang), jnp.sin(ang)
    cosf = jnp.concatenate([jnp.ones((l_len, HEAD_DIM), F32), jnp.concatenate([cos, cos], axis=-1)], axis=0)
    sins = jnp.concatenate([jnp.zeros((l_len, HEAD_DIM), F32), jnp.concatenate([-sin, sin], axis=-1)], axis=0)
    return cosf, sins


def _block_diag(w):
    eye = jnp.eye(LRU_BLOCKS, dtype=w.dtype)
    return (w[:, :, None, :] * eye[:, None, :, None]).reshape(LRU_W, LRU_W)


def _blocks_from_lanes(dwd_half):
    return dwd_half.reshape(LRU_BD, LRU_BLOCKS, LRU_BD).transpose(1, 0, 2)


def _silu(x):
    return x * jax.nn.sigmoid(x)


def kernel(x, c, ctx, c_ctx, w_ada, b_ada, norm1_g, norm2_g, w_in, ret_decay, conv_w, conv_b, lru_wa, lru_ba, lru_wx, lru_bx, lru_lambda, w_out, w_mlp1, w_mlp2, final_g, loss_target, m_c_ctx, m_w_ada, m_b_ada, m_norm1_g, m_norm2_g, m_w_in, m_ret_decay, m_conv_w, m_conv_b, m_lru_wa, m_lru_ba, m_lru_wx, m_lru_bx, m_lru_lambda, m_w_out, m_w_mlp1, m_w_mlp2, m_final_g, v_c_ctx, v_w_ada, v_b_ada, v_norm1_g, v_norm2_g, v_w_in, v_ret_decay, v_conv_w, v_conv_b, v_lru_wa, v_lru_ba, v_lru_wx, v_lru_bx, v_lru_lambda, v_w_out, v_w_mlp1, v_w_mlp2, v_final_g):
    t_len, d = x.shape[1], x.shape[2]
    l_len = ctx.shape[1]
    cb, cc = l_len // TM, l_len // CHUNK
    me = 4 * lax.axis_index("x") + 2 * lax.axis_index("y") + lax.axis_index("c")
    x2d, ctx2d, tgt2d = x[0], ctx[0], loss_target[0]
    ada_cols = w_ada.shape[2]
    wa2d = w_ada[0]

    sc_loc = conv_w.shape[2]
    pack_a = jnp.zeros((8, d), F32)
    pack_a = pack_a.at[0].set(_silu(c[0]))
    pack_a = pack_a.at[1, :4 * sc_loc].set(conv_w[0].reshape(-1))
    pack_a = pack_a.at[2, :2 * sc_loc].set(lru_ba[0].reshape(-1))
    pack_a = pack_a.at[3, :2 * sc_loc].set(lru_bx[0].reshape(-1))
    pack_a = pack_a.at[4, :2 * sc_loc].set(lru_lambda[0].reshape(-1))
    all_a = _all_gather(pack_a, "gather_small_in")
    s16 = jnp.zeros((16, d), F32).at[0:8].set(all_a[:, 0, :]).at[8].set(_silu(c_ctx))

    def unshard(row, k):
        return all_a[:, row, :k * sc_loc].reshape(N_DEV, k, sc_loc).transpose(1, 0, 2).reshape(k, N_DEV * sc_loc)

    conv_w_full = unshard(1, 4)
    ba_full, bx_full, lam_full = unshard(2, 2), unshard(3, 2), unshard(4, 2)

    b_cols = lax.dynamic_slice(b_ada, (0, me * ada_cols), (1, ada_cols))
    mod_parts = _all_gather(_mod_part(s16, wa2d, b_cols), "gather_mod")
    mod_all = mod_parts.transpose(1, 0, 2).reshape(16, N_DEV * ada_cols)
    mod_me = lax.dynamic_slice(mod_all, (me, 0), (1, 6 * d)).reshape(6, d)
    mod_c = mod_all[8].reshape(6, d)
    modrows = jnp.concatenate([mod_c[0:2], mod_me], axis=0)

    lg = jax.nn.log_sigmoid(ret_decay[0])
    sp = jax.nn.softplus(-lam_full)
    wg = [jnp.concatenate([_block_diag(lru_wa[0, dd]), _block_diag(lru_wx[0, dd])], axis=1).astype(BF16) for dd in (0, 1)]
    bg = [jnp.concatenate([ba_full[dd], bx_full[dd]])[None, :] for dd in (0, 1)]
    cosf, sins = _rot_tables(l_len, t_len)

    (hn, hnt), win_g = _norm1_fwd(ctx2d, x2d, norm1_g, modrows, cb, ride=("gather", w_in[0].astype(BF16)))
    p, wout_g = _mm_nn(hn, win_g, F32, "mm_in", ride=("gather", w_out[0].astype(BF16)))
    wout_g = wout_g.reshape(1, d, d)
    (o0, o1, sp0, sp1), w1_g = _ret_fwd(p, cosf, sins, lg, cc, ride=("gather", w_mlp1[0].astype(BF16)))
    o, s_prev = [o0, o1], [sp0, sp1]
    (h0, cin0), w2_g = _lru_fwd(p, wg[0], bg[0], sp[0:1], conv_w_full, conv_b, 0, cb, ride=("gather", w_mlp2[0].astype(BF16)))
    w2_g = w2_g.reshape(1, 4 * d, d)
    (h1, cin1), _ = _lru_fwd(p, wg[1], bg[1], sp[1:2], conv_w_full, conv_b, 1, cb)
    h, cin = [h0, h1], [cin0, cin1]
    mix = _mix_fwd(o[0], o[1], p, h[0], h[1], cb, t_len)
    y = _mm_nn(mix, wout_g, F32, "mm_out")
    x1, h2, h2t = _res_norm2(x2d, y, norm2_g, modrows)
    r = _mm_nn(h2, w1_g, BF16, "mm_mlp1", relu_out=True)
    z = _mm_nn(r, w2_g, F32, "mm_mlp2", square_lhs=True)
    dx2, dz, dzt, facc = _final_fwd_bwd(x1, z, tgt2d, final_g[None, :], modrows)

    du = _mm_nt(dz, w2_g, BF16, "mm_da2", relu_mul=r)
    gw2 = _mm_wgrad(dzt, r, "mm_dw2", w2_g.shape[1] // N_DEV, BF16, transpose_out=True, square_rhs=True)
    gw1 = _mm_wgrad(h2t, du, "mm_dw1", w1_g.shape[2], BF16)
    dh2 = _mm_nt(du, w1_g, F32, "mm_dh2")
    dx1, dy, n2acc = _bwd_norm2(dh2, x1, dx2, y, norm2_g, modrows)
    dmix = _mm_nt(dy, wout_g, F32, "mm_dmix")
    gwo = _mm_tn(mix, dy, "mm_dwout", False, d, BF16, 512).reshape(N_DEV, -1, d)
    do, dg, dgate, dhs = _mix_bwd(dmix, o[0], o[1], p, h[0], h[1], cb)
    dxcs, dwgs, laccs, rides, got = [], [], [], [("a2a", gw1), ("a2a", gwo)], []
    (dq0, dk0, dv0, dq1, dk1, dv1, dlg_lanes), gw2_all = _ret_bwd(p, cosf, sins, lg, do, s_prev, cc, ride=("a2a", gw2))
    dqs, dks, dvs = [dq0, dq1], [dk0, dk1], [dv0, dv1]
    for dd in (0, 1):
        (dxc_, dwg_, lacc_), got_ = _lru_bwd(p, wg[dd], bg[dd], sp[dd:dd + 1], conv_w_full, conv_b, h[dd], cin[dd], dhs, dd, cb,
                                             ride=rides[dd])
        dxcs.append(dxc_); dwgs.append(dwg_); laccs.append(lacc_); got.append(got_)
    gw1_all, gwo_all = got
    (dp, cacc), _ = _assemble_dp(dqs, dks, dvs, dg, dgate, dxcs, p, conv_w_full, cb)
    gwi = _mm_wgrad(hnt, dp, "mm_dwin", win_g.shape[2], BF16)
    dhn, gwi_all = _mm_nt(dp, win_g, F32, "mm_dhn", ride=("a2a", gwi))
    grad_x, n1acc = _bwd_norm1(dhn, ctx2d, x2d, dx1, norm1_g, modrows, cb)

    pack_b = jnp.concatenate([n1acc, n2acc, facc, cacc, laccs[0], laccs[1], dlg_lanes, dwgs[0], dwgs[1]], axis=0)
    all_b = _all_gather(pack_b, "gather_small_grads")
    tot = _sum_parts(all_b, "sum_small_grads")
    t_n1, t_n2, t_f, t_conv, t_dlg = tot[0:8], tot[8:16], tot[16:24], tot[24:32, :LRU_W], tot[48:56, :HEAD_DIM]
    t_l = [tot[32:40], tot[40:48]]
    t_dwd = [tot[56:56 + LRU_BD], tot[56 + LRU_BD:56 + 2 * LRU_BD]]
    loss = (0.5 / d) * jnp.sum(t_f[2])
    t_wa = jnp.stack([_blocks_from_lanes(t_dwd[dd][:, :LRU_W]) for dd in (0, 1)])
    t_wx = jnp.stack([_blocks_from_lanes(t_dwd[dd][:, LRU_W:]) for dd in (0, 1)])
    t_ba = jnp.stack([t_l[dd][0, :LRU_W] for dd in (0, 1)])
    t_bx = jnp.stack([t_l[dd][0, LRU_W:] for dd in (0, 1)])
    t_sp = jnp.stack([t_l[dd][1, :LRU_W] for dd in (0, 1)])
    dm_rows = jnp.stack([all_b[:, i, :] for i in (2, 3, 11, 8, 9, 17)], axis=1).reshape(N_DEV, 6 * d)
    dm_c = jnp.concatenate([t_n1[0], t_n1[1], jnp.zeros((4 * d,), F32)])
    dm16 = jnp.zeros((16, 6 * d), F32).at[0:8].set(dm_rows).at[8].set(dm_c)
    g_b_ada = jnp.sum(dm16, axis=0)[None, :]
    dm_cols = lax.dynamic_slice(dm16, (0, me * ada_cols), (16, ada_cols))
    g_w_ada, ds16 = _ada_bwd(s16, dm_cols, wa2d)
    ds_all = _all_gather(ds16[8:16], "gather_dsilu")
    dsilu_cc = _sum_parts(ds_all, "sum_dsilu")[0]
    sg_cc = jax.nn.sigmoid(c_ctx)
    g_c_ctx = dsilu_cc * (sg_cc * (1.0 + c_ctx * (1.0 - sg_cc)))

    g_ret_decay = jnp.sum(t_dlg, axis=-1).reshape(2, HEADS) * jax.nn.sigmoid(-ret_decay[0])
    g_lambda_full = -t_sp * jax.nn.sigmoid(-lam_full)

    def my_cols(full):
        return lax.dynamic_slice(full, (0, me * sc_loc), (full.shape[0], sc_loc))

    small_g = dict(
        c_ctx=g_c_ctx[None], b_ada=g_b_ada, norm1_g=t_n1[4:5], norm2_g=t_n2[2:3], ret_decay=g_ret_decay,
        conv_w=my_cols(t_conv[0:4]), conv_b=t_conv[4:5], lru_wa=t_wa.reshape(-1, LRU_BD), lru_ba=my_cols(t_ba),
        lru_wx=t_wx.reshape(-1, LRU_BD), lru_bx=my_cols(t_bx), lru_lambda=my_cols(g_lambda_full), final_g=t_f[0:1])
    small = dict(
        c_ctx=(c_ctx, m_c_ctx, v_c_ctx), b_ada=(b_ada, m_b_ada, v_b_ada), norm1_g=(norm1_g, m_norm1_g, v_norm1_g),
        norm2_g=(norm2_g, m_norm2_g, v_norm2_g), ret_decay=(ret_decay, m_ret_decay, v_ret_decay),
        conv_w=(conv_w, m_conv_w, v_conv_w), conv_b=(conv_b, m_conv_b, v_conv_b), lru_wa=(lru_wa, m_lru_wa, v_lru_wa),
        lru_ba=(lru_ba, m_lru_ba, v_lru_ba), lru_wx=(lru_wx, m_lru_wx, v_lru_wx), lru_bx=(lru_bx, m_lru_bx, v_lru_bx),
        lru_lambda=(lru_lambda, m_lru_lambda, v_lru_lambda), final_g=(final_g, m_final_g, v_final_g))
    names = list(small)
    items = [(small_g[k],) + tuple(a.reshape(small_g[k].shape) for a in small[k]) for k in names]
    res = {}
    for k, it, (d_, m_, v_) in zip(names, items, _adam_many(items, "adam_small")):
        shape = small[k][0].shape
        res[k] = tuple(a.reshape(shape) for a in (it[0], d_, m_, v_))

    def big(parts, w, m, v, name):
        out = _sum_adam(parts, w[0], m[0], v[0], name)
        return tuple(a[None] for a in out)

    res["w_ada"] = big(g_w_ada[None], w_ada, m_w_ada, v_w_ada, "adam_w_ada")
    res["w_in"] = big(gwi_all, w_in, m_w_in, v_w_in, "adam_w_in")
    res["w_out"] = big(gwo_all, w_out, m_w_out, v_w_out, "adam_w_out")
    res["w_mlp1"] = big(gw1_all, w_mlp1, m_w_mlp1, v_w_mlp1, "adam_w_mlp1")
    res["w_mlp2"] = big(gw2_all, w_mlp2, m_w_mlp2, v_w_mlp2, "adam_w_mlp2")

    order = ["c_ctx", "w_ada", "b_ada", "norm1_g", "norm2_g", "w_in", "ret_decay", "conv_w", "conv_b", "lru_wa", "lru_ba",
             "lru_wx", "lru_bx", "lru_lambda", "w_out", "w_mlp1", "w_mlp2", "final_g"]
    outs = [loss, grad_x[None]]
    for j in range(4):
        outs += [res[k][j] for k in order]
    return tuple(outs)
```

```python
import jax
import jax.numpy as jnp
from jax import lax
from jax.experimental import pallas as pl
from jax.experimental.pallas import tpu as pltpu

F32 = jnp.float32
BF16 = jnp.bfloat16
AXES = ("x", "y", "c")
N_DEV = 8
MESH = pl.DeviceIdType.MESH

HEADS = 4
HEAD_DIM = 128
CHUNK = 128
RET_W = HEADS * HEAD_DIM
LRU_W = 512
LRU_BLOCKS = 8
LRU_BD = LRU_W // LRU_BLOCKS
LRU_C = 8.0
EPS = 1e-6
K_SCALE = HEAD_DIM ** -0.5
ROPE_BASE = 10000.0
GRID_W = 64
TM = 256
SUB = 8

ADAM_LR = 0.001
ADAM_B1 = 0.9
ADAM_B2 = 0.999
ADAM_EPS = 1e-08
ADAM_WD = 0.01
ADAM_STEP = 10

R_CSH1, R_CSC1, R_SH1, R_SC1, R_G1, R_SH2, R_SC2, R_G2 = range(8)


def _pcall(body, **kw):
    return pl.pallas_call(body, **kw)


def _cp(vmem_mb=48):
    return pltpu.CompilerParams(vmem_limit_bytes=vmem_mb << 20)


def _sds(shape, dtype):
    return jax.ShapeDtypeStruct(shape, dtype)


def _dot(a, b):
    return jnp.dot(a.astype(BF16), b.astype(BF16), preferred_element_type=F32)


def _dot_nt(a, b):
    return lax.dot_general(a.astype(BF16), b.astype(BF16), (((1,), (1,)), ((), ())), preferred_element_type=F32)


def _dot_tn(a, b):
    return lax.dot_general(a.astype(BF16), b.astype(BF16), (((0,), (0,)), ((), ())), preferred_element_type=F32)


def _sigmoid(x):
    return jax.nn.sigmoid(x)


def _gelu(x):
    return 0.5 * x * (1.0 + jnp.tanh(0.7978845608028654 * (x + 0.044715 * x * x * x)))


def _dgelu(x):
    t = jnp.tanh(0.7978845608028654 * (x + 0.044715 * x * x * x))
    return 0.5 * (1.0 + t) + 0.5 * x * (1.0 - t * t) * 0.7978845608028654 * (1.0 + 3.0 * 0.044715 * x * x)


def _rows_iota(shape):
    return lax.broadcasted_iota(jnp.int32, shape, 0)


def _tile_order(dirn, s, cb, nb):
    if dirn == 0:
        return s
    return jnp.where(s < cb, cb - 1 - s, nb - 1 - (s - cb))


_SEMS = [pltpu.SemaphoreType.DMA((7,)), pltpu.SemaphoreType.DMA((7,)), pltpu.SemaphoreType.DMA(())]
_ANY = pl.BlockSpec(memory_space=pl.ANY)


def _gather_copies(x_ref, out_ref, send_sems, recv_sems, local_sem):
    mx, my, mc = lax.axis_index("x"), lax.axis_index("y"), lax.axis_index("c")
    me, sibling = (mx, my, mc), (mx, my, 1 - mc)
    chips = [(1 - mx, my), (mx, 1 - my), (1 - mx, 1 - my)]

    def slot(px, py, pc):
        return out_ref.at[4 * px + 2 * py + pc]

    def copy(k, block, to, src=None):
        return pltpu.make_async_remote_copy(
            src_ref=slot(*block) if src is None else src, dst_ref=slot(*block),
            send_sem=send_sems.at[k], recv_sem=recv_sems.at[k], device_id=to, device_id_type=MESH)

    mine = pltpu.make_async_copy(x_ref, slot(*me), local_sem)
    first = [copy(0, me, sibling, src=x_ref)] + [copy(1 + j, me, (*chip, mc), src=x_ref) for j, chip in enumerate(chips)]
    passed = [copy(4 + j, (*chip, mc), sibling) for j, chip in enumerate(chips)]
    recv_ici = [copy(1 + j, (*chip, mc), me) for j, chip in enumerate(chips)]
    recv_d2d = [copy(0, sibling, me)] + [copy(4 + j, (*chip, 1 - mc), me) for j, chip in enumerate(chips)]
    return mine, first, passed, recv_ici, recv_d2d


def _gather_start(*refs):
    mine, first, _, _, _ = _gather_copies(*refs)
    mine.start()
    for cp in first:
        cp.start()


def _gather_finish(*refs):
    mine, first, passed, recv_ici, recv_d2d = _gather_copies(*refs)
    for landed, onward in zip(recv_ici, passed):
        landed.wait_recv()
        onward.start()
    for landed in recv_d2d:
        landed.wait_recv()
    for cp in first + passed:
        cp.wait_send()
    mine.wait()


def _a2a_copies(g_ref, out_ref, send_sems, recv_sems, local_sem):
    mx, my, mc = lax.axis_index("x"), lax.axis_index("y"), lax.axis_index("c")
    me = 4 * mx + 2 * my + mc
    mine = pltpu.make_async_copy(g_ref.at[me], out_ref.at[me], local_sem)
    copies = []
    for k in range(1, N_DEV):
        px = 1 - mx if (k >> 2) & 1 else mx
        py = 1 - my if (k >> 1) & 1 else my
        pc = 1 - mc if k & 1 else mc
        copies.append(pltpu.make_async_remote_copy(
            src_ref=g_ref.at[4 * px + 2 * py + pc], dst_ref=out_ref.at[me],
            send_sem=send_sems.at[k - 1], recv_sem=recv_sems.at[k - 1],
            device_id=(px, py, pc), device_id_type=MESH))
    return mine, copies


def _a2a_start(*refs):
    mine, copies = _a2a_copies(*refs)
    mine.start()
    for cp in copies:
        cp.start()


def _a2a_finish(*refs):
    mine, copies = _a2a_copies(*refs)
    for cp in copies:
        cp.wait()
    mine.wait()


_EXCHANGES = {"gather": (_gather_start, _gather_finish), "a2a": (_a2a_start, _a2a_finish)}


def _exchange_shape(kind, src):
    return _sds((N_DEV,) + src.shape if kind == "gather" else src.shape, src.dtype)


def _all_gather(x, name):
    def body(x_ref, out_ref, *sems):
        _gather_start(x_ref, out_ref, *sems)
        _gather_finish(x_ref, out_ref, *sems)

    return _pcall(body, name=name, out_shape=_exchange_shape("gather", x), in_specs=[_ANY], out_specs=_ANY,
                  scratch_shapes=list(_SEMS))(x)


def _pcall_ride(body, ride, args, *, name, grid, in_specs, out_specs, out_shape, scratch_shapes=(), compiler_params=None):
    if ride is None:
        out = _pcall(body, name=name, grid=grid, in_specs=in_specs, out_specs=out_specs, out_shape=out_shape,
                     scratch_shapes=list(scratch_shapes), compiler_params=compiler_params)(*args)
        return out, None
    kind, src = ride
    start, finish = _EXCHANGES[kind]
    single = not isinstance(out_shape, (tuple, list))
    out_specs_t = (out_specs,) if single else tuple(out_specs)
    out_shape_t = (out_shape,) if single else tuple(out_shape)
    n_in, n_out, n_sc = len(in_specs), len(out_shape_t), len(scratch_shapes)

    def wrapped(*refs):
        ins, src_ref = refs[:n_in], refs[n_in]
        outs, dst_ref = refs[n_in + 1:n_in + 1 + n_out], refs[n_in + 1 + n_out]
        scratch = refs[n_in + 2 + n_out:n_in + 2 + n_out + n_sc]
        sems = refs[n_in + 2 + n_out + n_sc:]
        first = pl.program_id(0) == 0
        last = pl.program_id(0) == grid[0] - 1
        for ax in range(1, len(grid)):
            first = jnp.logical_and(first, pl.program_id(ax) == 0)
            last = jnp.logical_and(last, pl.program_id(ax) == grid[ax] - 1)

        @pl.when(first)
        def _():
            start(src_ref, dst_ref, *sems)

        body(*ins, *outs, *scratch)

        @pl.when(last)
        def _():
            finish(src_ref, dst_ref, *sems)

    res = _pcall(wrapped, name=name, grid=grid, in_specs=list(in_specs) + [_ANY], out_specs=out_specs_t + (_ANY,),
                 out_shape=out_shape_t + (_exchange_shape(kind, src),),
                 scratch_shapes=list(scratch_shapes) + list(_SEMS), compiler_params=compiler_params)(*args, src)
    return (res[0] if single else tuple(res[:-1])), res[-1]


def _mm_nn(a, w, out_dtype, name, square_lhs=False, relu_out=False, ride=None):
    m, k = a.shape
    nb, _, bn = w.shape

    def body(a_ref, w_ref, o_ref):
        av = a_ref[...]
        if square_lhs:
            av = av * av
        for j in range(nb):
            r = jnp.dot(av, w_ref[j], preferred_element_type=F32)
            if relu_out:
                r = jnp.maximum(r, 0.0)
            o_ref[:, j * bn:(j + 1) * bn] = r.astype(out_dtype)

    out, ex = _pcall_ride(
        body, ride, (a, w), name=name, grid=(m // TM,),
        in_specs=[pl.BlockSpec((TM, k), lambda i: (i, 0)), pl.BlockSpec((nb, k, bn), lambda i: (0, 0, 0))],
        out_specs=pl.BlockSpec((TM, nb * bn), lambda i: (i, 0)),
        out_shape=_sds((m, nb * bn), out_dtype), compiler_params=_cp())
    return out if ride is None else (out, ex)


def _mm_nt(dy, w, out_dtype, name, relu_mul=None, ride=None):
    m = dy.shape[0]
    nb, k, bn = w.shape

    def body(*refs):
        if relu_mul is None:
            dy_ref, w_ref, o_ref, wt = refs
        else:
            dy_ref, w_ref, r_ref, o_ref, wt = refs

        @pl.when(pl.program_id(0) == 0)
        def _():
            for j in range(nb):
                wt[j * bn:(j + 1) * bn, :] = w_ref[j].T

        acc = jnp.dot(dy_ref[...], wt[...], preferred_element_type=F32)
        if relu_mul is not None:
            acc = acc * (2.0 * r_ref[...].astype(F32))
        o_ref[...] = acc.astype(out_dtype)

    in_specs = [pl.BlockSpec((TM, nb * bn), lambda i: (i, 0)), pl.BlockSpec((nb, k, bn), lambda i: (0, 0, 0))]
    args = [dy, w]
    if relu_mul is not None:
        in_specs.append(pl.BlockSpec((TM, k), lambda i: (i, 0)))
        args.append(relu_mul)
    out, ex = _pcall_ride(
        body, ride, args, name=name, grid=(m // TM,), in_specs=in_specs,
        out_specs=pl.BlockSpec((TM, k), lambda i: (i, 0)),
        out_shape=_sds((m, k), out_dtype), scratch_shapes=[pltpu.VMEM((nb * bn, k), BF16)], compiler_params=_cp())
    return out if ride is None else (out, ex)


def _mm_tn(a, b, name, col_blocks, block, out_dtype, tm, square_lhs=False):
    m, k = a.shape
    nn = b.shape[1]
    steps = m // tm
    if col_blocks:
        nblk, acc_shape = nn // block, (k, block)
        a_spec = pl.BlockSpec((tm, k), lambda j, s: (s, 0))
        b_spec = pl.BlockSpec((tm, block), lambda j, s: (s, j))
    else:
        nblk, acc_shape = k // block, (block, nn)
        a_spec = pl.BlockSpec((tm, block), lambda j, s: (s, j))
        b_spec = pl.BlockSpec((tm, nn), lambda j, s: (s, 0))

    def body(a_ref, b_ref, o_ref, acc):
        s = pl.program_id(1)

        @pl.when(s == 0)
        def _():
            acc[...] = jnp.zeros_like(acc)

        av = a_ref[...]
        if square_lhs:
            av = av.astype(F32)
            av = (av * av).astype(BF16)
        acc[...] += _dot_tn(av, b_ref[...])

        @pl.when(s == steps - 1)
        def _():
            o_ref[...] = acc[...].astype(out_dtype)

    return _pcall(
        body, name=name, grid=(nblk, steps), in_specs=[a_spec, b_spec],
        out_specs=pl.BlockSpec((None,) + acc_shape, lambda j, s: (j, 0, 0)),
        out_shape=_sds((nblk,) + acc_shape, out_dtype),
        scratch_shapes=[pltpu.VMEM(acc_shape, F32)], compiler_params=_cp(),
    )(a, b)


def _mm_wgrad(at, b, name, bn, out_dtype, transpose_out=False, square_rhs=False, halves=False):
    k, m = at.shape
    nblk = b.shape[1] // bn
    rows, cols = (bn, k) if transpose_out else (k, bn)
    nout = 2 if halves else 1
    per = rows // nout

    def body(a_ref, b_ref, *o_refs):
        bv = b_ref[...]
        if square_rhs:
            bv = bv * bv
        r = jnp.dot(a_ref[...], bv, preferred_element_type=F32)
        r = (r.T if transpose_out else r).astype(out_dtype)
        for i, o_ref in enumerate(o_refs):
            o_ref[...] = r[i * per:(i + 1) * per, :]

    out = _pcall(
        body, name=name, grid=(nblk,),
        in_specs=[pl.BlockSpec((k, m), lambda j: (0, 0)), pl.BlockSpec((m, bn), lambda j: (0, j))],
        out_specs=tuple(pl.BlockSpec((None, per, cols), lambda j: (j, 0, 0)) for _ in range(nout)),
        out_shape=tuple(_sds((nblk, per, cols), out_dtype) for _ in range(nout)), compiler_params=_cp(),
    )(at, b)
    return out if halves else out[0]


def _mod_part(s16, w_ada, b_cols):
    def body(s_ref, w_ref, b_ref, o_ref):
        o_ref[...] = _dot(s_ref[...], w_ref[...]) + b_ref[...]

    return _pcall(body, name="mod_part", out_shape=_sds((s16.shape[0], w_ada.shape[1]), F32),
                  compiler_params=_cp())(s16, w_ada, b_cols)


def _ada_bwd(s16, dm_cols, w_ada):
    def body(s_ref, d_ref, w_ref, gw_ref, ds_ref):
        gw_ref[...] = _dot_tn(s_ref[...], d_ref[...])
        ds_ref[...] = _dot_nt(d_ref[...], w_ref[...])

    return _pcall(body, name="ada_bwd",
                  out_shape=(_sds(w_ada.shape, F32), _sds(s16.shape, F32)), compiler_params=_cp())(s16, dm_cols, w_ada)


def _norm1_fwd(ctx, x, g, modrows, cb, ride=None):
    l_len, d = ctx.shape
    nb = (l_len + x.shape[0]) // TM

    def body(ctx_ref, x_ref, g_ref, m_ref, o_ref, ot_ref):
        is_ctx = pl.program_id(0) < cb
        xin = jnp.where(is_ctx, ctx_ref[...], x_ref[...])
        sh = jnp.where(is_ctx, m_ref[R_CSH1:R_CSH1 + 1, :], m_ref[R_SH1:R_SH1 + 1, :])
        sc = jnp.where(is_ctx, m_ref[R_CSC1:R_CSC1 + 1, :], m_ref[R_SC1:R_SC1 + 1, :])
        ms = jnp.mean(xin * xin, axis=-1, keepdims=True)
        n = xin * lax.rsqrt(ms + EPS) * g_ref[...]
        hn = n * (1.0 + sc) + sh
        o_ref[...] = hn.astype(BF16)
        ot_ref[...] = hn.T.astype(BF16)

    return _pcall_ride(
        body, ride, (ctx, x, g, modrows), name="norm1_fwd", grid=(nb,),
        in_specs=[pl.BlockSpec((TM, d), lambda i: (jnp.minimum(i, cb - 1), 0)),
                  pl.BlockSpec((TM, d), lambda i: (jnp.maximum(i - cb, 0), 0)),
                  pl.BlockSpec((1, d), lambda i: (0, 0)), pl.BlockSpec((8, d), lambda i: (0, 0))],
        out_specs=(pl.BlockSpec((TM, d), lambda i: (i, 0)), pl.BlockSpec((d, TM), lambda i: (0, i))),
        out_shape=(_sds((nb * TM, d), BF16), _sds((d, nb * TM), BF16)), compiler_params=_cp())


def _mix_fwd(o_f, o_b, p, h_f, h_b, cb, t_len):
    def body(of_ref, ob_ref, g_ref, gate_ref, hf_ref, hb_ref, mix_ref):
        o = of_ref[...] + ob_ref[...]
        g = g_ref[...]
        sg = g * _sigmoid(g)
        for hh in range(HEADS):
            sl = slice(hh * HEAD_DIM, (hh + 1) * HEAD_DIM)
            oh = o[:, sl]
            yc = oh - jnp.mean(oh, axis=-1, keepdims=True)
            var = jnp.mean(yc * yc, axis=-1, keepdims=True)
            mix_ref[:, sl] = (sg[:, sl] * (yc * lax.rsqrt(var + EPS))).astype(BF16)
        mix_ref[:, RET_W:] = ((hf_ref[...] + hb_ref[...]) * _gelu(gate_ref[...])).astype(BF16)

    row = lambda i: (i + cb, 0)
    return _pcall(
        body, name="mix_fwd", grid=(t_len // TM,),
        in_specs=[pl.BlockSpec((TM, RET_W), row), pl.BlockSpec((TM, RET_W), row),
                  pl.BlockSpec((TM, RET_W), lambda i: (i + cb, 3)), pl.BlockSpec((TM, LRU_W), lambda i: (i + cb, 5)),
                  pl.BlockSpec((TM, LRU_W), row), pl.BlockSpec((TM, LRU_W), row)],
        out_specs=pl.BlockSpec((TM, RET_W + LRU_W), lambda i: (i, 0)),
        out_shape=_sds((t_len, RET_W + LRU_W), BF16), compiler_params=_cp(),
    )(o_f, o_b, p, p, h_f, h_b)


def _res_norm2(x, y, g, modrows):
    t_len, d = x.shape

    def body(x_ref, y_ref, g_ref, m_ref, x1_ref, h2_ref, h2t_ref):
        x1 = x_ref[...] + m_ref[R_G1:R_G1 + 1, :] * y_ref[...]
        ms = jnp.mean(x1 * x1, axis=-1, keepdims=True)
        n = x1 * lax.rsqrt(ms + EPS) * g_ref[...]
        x1_ref[...] = x1
        h2 = n * (1.0 + m_ref[R_SC2:R_SC2 + 1, :]) + m_ref[R_SH2:R_SH2 + 1, :]
        h2_ref[...] = h2.astype(BF16)
        h2t_ref[...] = h2.T.astype(BF16)

    t = pl.BlockSpec((TM, d), lambda i: (i, 0))
    tt = pl.BlockSpec((d, TM), lambda i: (0, i))
    return _pcall(
        body, name="res_norm2", grid=(t_len // TM,),
        in_specs=[t, t, pl.BlockSpec((1, d), lambda i: (0, 0)), pl.BlockSpec((8, d), lambda i: (0, 0))],
        out_specs=(t, t, tt), out_shape=(_sds((t_len, d), F32), _sds((t_len, d), BF16), _sds((d, t_len), BF16)),
        compiler_params=_cp(),
    )(x, y, g, modrows)


def _final_fwd_bwd(x1, z, target, fg, modrows):
    t_len, d = x1.shape

    def body(x1_ref, z_ref, t_ref, fg_ref, m_ref, dx2_ref, dz_ref, dzt_ref, acc_ref):
        @pl.when(pl.program_id(0) == 0)
        def _():
            acc_ref[...] = jnp.zeros_like(acc_ref)

        g2 = m_ref[R_G2:R_G2 + 1, :]
        z = z_ref[...]
        x2 = x1_ref[...] + g2 * z
        rstd = lax.rsqrt(jnp.mean(x2 * x2, axis=-1, keepdims=True) + EPS)
        xh = x2 * rstd
        fg = fg_ref[...]
        e = xh * fg - t_ref[...]
        dy = e * (1.0 / d)
        dxh = dy * fg
        dx2 = rstd * (dxh - xh * jnp.mean(dxh * xh, axis=-1, keepdims=True))
        dx2_ref[...] = dx2
        dz = g2 * dx2
        dz_ref[...] = dz.astype(BF16)
        dzt_ref[...] = dz.T.astype(BF16)
        acc_ref[0:1, :] += jnp.sum(dy * xh, axis=0, keepdims=True)
        acc_ref[1:2, :] += jnp.sum(dx2 * z, axis=0, keepdims=True)
        acc_ref[2:3, :] += jnp.sum(e * e, axis=0, keepdims=True)

    t = pl.BlockSpec((TM, d), lambda i: (i, 0))
    return _pcall(
        body, name="final_fwd_bwd", grid=(t_len // TM,),
        in_specs=[t, t, t, pl.BlockSpec((1, d), lambda i: (0, 0)), pl.BlockSpec((8, d), lambda i: (0, 0))],
        out_specs=(t, t, pl.BlockSpec((d, TM), lambda i: (0, i)), pl.BlockSpec((8, d), lambda i: (0, 0))),
        out_shape=(_sds((t_len, d), F32), _sds((t_len, d), BF16), _sds((d, t_len), BF16), _sds((8, d), F32)),
        compiler_params=_cp(),
    )(x1, z, target, fg, modrows)


def _bwd_norm2(dh2, x1, dx2, y, g, modrows):
    t_len, d = x1.shape

    def body(dh2_ref, x1_ref, dx2_ref, y_ref, g_ref, m_ref, dx1_ref, dy_ref, acc_ref):
        @pl.when(pl.program_id(0) == 0)
        def _():
            acc_ref[...] = jnp.zeros_like(acc_ref)

        x1 = x1_ref[...]
        rstd = lax.rsqrt(jnp.mean(x1 * x1, axis=-1, keepdims=True) + EPS)
        xh = x1 * rstd
        gn = g_ref[...]
        dh2 = dh2_ref[...]
        dn = dh2 * (1.0 + m_ref[R_SC2:R_SC2 + 1, :])
        dxh = dn * gn
        dx1 = dx2_ref[...] + rstd * (dxh - xh * jnp.mean(dxh * xh, axis=-1, keepdims=True))
        dx1_ref[...] = dx1
        dy_ref[...] = (m_ref[R_G1:R_G1 + 1, :] * dx1).astype(BF16)
        acc_ref[0:1, :] += jnp.sum(dh2, axis=0, keepdims=True)
        acc_ref[1:2, :] += jnp.sum(dh2 * xh * gn, axis=0, keepdims=True)
        acc_ref[2:3, :] += jnp.sum(dn * xh, axis=0, keepdims=True)
        acc_ref[3:4, :] += jnp.sum(dx1 * y_ref[...], axis=0, keepdims=True)

    t = pl.BlockSpec((TM, d), lambda i: (i, 0))
    return _pcall(
        body, name="bwd_norm2", grid=(t_len // TM,),
        in_specs=[t, t, t, t, pl.BlockSpec((1, d), lambda i: (0, 0)), pl.BlockSpec((8, d), lambda i: (0, 0))],
        out_specs=(t, t, pl.BlockSpec((8, d), lambda i: (0, 0))),
        out_shape=(_sds((t_len, d), F32), _sds((t_len, d), BF16), _sds((8, d), F32)), compiler_params=_cp(),
    )(dh2, x1, dx2, y, g, modrows)


def _mix_bwd(dmix, o_f, o_b, p, h_f, h_b, cb):
    n = o_f.shape[0]

    def body(dm_ref, of_ref, ob_ref, g_ref, gate_ref, hf_ref, hb_ref, do_ref, dg_ref, dgate_ref, dhs_ref):
        is_ctx = pl.program_id(0) < cb

        @pl.when(is_ctx)
        def _():
            for r in (do_ref, dg_ref, dgate_ref, dhs_ref):
                r[...] = jnp.zeros_like(r)

        @pl.when(jnp.logical_not(is_ctx))
        def _():
            o = of_ref[...] + ob_ref[...]
            g = g_ref[...]
            s = _sigmoid(g)
            sg = g * s
            dsg = s * (1.0 + g * (1.0 - s))
            for hh in range(HEADS):
                sl = slice(hh * HEAD_DIM, (hh + 1) * HEAD_DIM)
                oh = o[:, sl]
                yc = oh - jnp.mean(oh, axis=-1, keepdims=True)
                rs = lax.rsqrt(jnp.mean(yc * yc, axis=-1, keepdims=True) + EPS)
                gn = yc * rs
                dret = dm_ref[:, sl]
                dgn = dret * sg[:, sl]
                dg_ref[:, sl] = dret * gn * dsg[:, sl]
                do_ref[:, sl] = rs * (dgn - jnp.mean(dgn, axis=-1, keepdims=True)
                                      - gn * jnp.mean(dgn * gn, axis=-1, keepdims=True))
            dlru = dm_ref[:, RET_W:]
            gate = gate_ref[...]
            dhs_ref[...] = dlru * _gelu(gate)
            dgate_ref[...] = dlru * (hf_ref[...] + hb_ref[...]) * _dgelu(gate)

    row = lambda i: (i, 0)
    t = pl.BlockSpec((TM, RET_W), row)
    return _pcall(
        body, name="mix_bwd", grid=(n // TM,),
        in_specs=[pl.BlockSpec((TM, RET_W + LRU_W), lambda i: (jnp.maximum(i - cb, 0), 0)), t, t,
                  pl.BlockSpec((TM, RET_W), lambda i: (i, 3)), pl.BlockSpec((TM, LRU_W), lambda i: (i, 5)), t, t],
        out_specs=(t, t, t, t), out_shape=tuple(_sds((n, RET_W), F32) for _ in range(4)), compiler_params=_cp(),
    )(dmix, o_f, o_b, p, p, h_f, h_b)


def _bwd_norm1(dhn, ctx, x, dx1, g, modrows, cb, ride=None):
    t_len, d = x.shape
    nb = dhn.shape[0] // TM

    def body(dh_ref, ctx_ref, x_ref, dx1_ref, g_ref, m_ref, gx_ref, acc_ref):
        is_ctx = pl.program_id(0) < cb

        @pl.when(pl.program_id(0) == 0)
        def _():
            acc_ref[...] = jnp.zeros_like(acc_ref)

        xin = jnp.where(is_ctx, ctx_ref[...], x_ref[...])
        sc = jnp.where(is_ctx, m_ref[R_CSC1:R_CSC1 + 1, :], m_ref[R_SC1:R_SC1 + 1, :])
        rstd = lax.rsqrt(jnp.mean(xin * xin, axis=-1, keepdims=True) + EPS)
        xh = xin * rstd
        gn = g_ref[...]
        dh = dh_ref[...]
        dn = dh * (1.0 + sc)
        dxh = dn * gn
        gx_ref[...] = dx1_ref[...] + rstd * (dxh - xh * jnp.mean(dxh * xh, axis=-1, keepdims=True))
        s0 = jnp.sum(dh, axis=0, keepdims=True)
        s1 = jnp.sum(dh * xh * gn, axis=0, keepdims=True)
        acc_ref[4:5, :] += jnp.sum(dn * xh, axis=0, keepdims=True)

        @pl.when(is_ctx)
        def _():
            acc_ref[0:1, :] += s0
            acc_ref[1:2, :] += s1

        @pl.when(jnp.logical_not(is_ctx))
        def _():
            acc_ref[2:3, :] += s0
            acc_ref[3:4, :] += s1

    lat = pl.BlockSpec((TM, d), lambda i: (jnp.maximum(i - cb, 0), 0))
    return _pcall_ride(
        body, ride, (dhn, ctx, x, dx1, g, modrows), name="bwd_norm1", grid=(nb,),
        in_specs=[pl.BlockSpec((TM, d), lambda i: (i, 0)), pl.BlockSpec((TM, d), lambda i: (jnp.minimum(i, cb - 1), 0)),
                  lat, lat, pl.BlockSpec((1, d), lambda i: (0, 0)), pl.BlockSpec((8, d), lambda i: (0, 0))],
        out_specs=(lat, pl.BlockSpec((8, d), lambda i: (0, 0))),
        out_shape=(_sds((t_len, d), F32), _sds((8, d), F32)), compiler_params=_cp())


def _rot(x, cf, ss):
    return x * cf + pltpu.roll(x, HEAD_DIM // 2, 1) * ss


def _decay_exponents(dirn):
    ii = lax.broadcasted_iota(jnp.int32, (CHUNK, CHUNK), 0)
    jj = lax.broadcasted_iota(jnp.int32, (CHUNK, CHUNK), 1)
    rel = ii - jj if dirn == 0 else jj - ii
    pos = ii.astype(F32)
    if dirn == 0:
        cq, cs = pos + 1.0, (CHUNK - 1.0) - pos
    else:
        cq, cs = CHUNK - pos, pos
    return rel, jnp.maximum(rel, 0).astype(F32), cq, cs


def _store_decay(lg_ref, dec):
    for dirn in (0, 1):
        rel, relf, cq, cs = _decay_exponents(dirn)
        for h in range(HEADS):
            lgv = lg_ref[dirn, h]
            dec[dirn, h, 0] = jnp.where(rel >= 0, jnp.exp(lgv * relf), 0.0)
            dec[dirn, h, 1] = jnp.exp(lgv * cq)
            dec[dirn, h, 2] = jnp.exp(lgv * cs)


def _ret_rows(cc, nc, step_of):
    return [lambda s, dirn=dirn: _tile_order(dirn, step_of(s), cc, nc) for dirn in (0, 1)]


def _ret_in_specs(rows):
    specs = []
    for row in rows:
        specs += [pl.BlockSpec((CHUNK, RET_W), lambda s, o=o, row=row: (row(s), o)) for o in (0, 1, 2)]
        specs += [pl.BlockSpec((CHUNK, HEAD_DIM), lambda s, row=row: (row(s), 0))] * 2
    return specs


def _ret_fwd(p, cosf, sins, lg, cc, ride=None):
    n = p.shape[0]
    nc = n // CHUNK
    rows = _ret_rows(cc, nc, lambda s: s)

    def body(lg_ref, q0, k0, v0, c0, s0, q1, k1, v1, c1, s1, o0, o1, sp0, sp1, st, dec):
        @pl.when(pl.program_id(0) == 0)
        def _():
            st[...] = jnp.zeros_like(st)
            _store_decay(lg_ref, dec)

        for dirn, (q_ref, k_ref, v_ref, c_ref, s_ref, o_ref, sp_ref) in enumerate(
                ((q0, k0, v0, c0, s0, o0, sp0), (q1, k1, v1, c1, s1, o1, sp1))):
            cf, ss = c_ref[...], s_ref[...]
            for h in range(HEADS):
                sl = slice(h * HEAD_DIM, (h + 1) * HEAD_DIM)
                q = _rot(q_ref[:, sl], cf, ss)
                k = _rot(k_ref[:, sl], cf, ss) * K_SCALE
                v = v_ref[:, sl]
                sp = st[dirn, h]
                sp_ref[h] = sp
                o_ref[:, sl] = _dot(_dot_nt(q, k) * dec[dirn, h, 0], v) + _dot(q * dec[dirn, h, 1], sp)
                st[dirn, h] = jnp.exp(lg_ref[dirn, h] * CHUNK) * sp + _dot_tn(k * dec[dirn, h, 2], v)

    o_specs = [pl.BlockSpec((CHUNK, RET_W), lambda s, row=row: (row(s), 0)) for row in rows]
    state = pl.BlockSpec((None, HEADS, CHUNK, HEAD_DIM), lambda s: (s, 0, 0, 0))
    return _pcall_ride(
        body, ride, (lg, p, p, p, cosf, sins, p, p, p, cosf, sins), name="ret_fwd", grid=(nc,),
        in_specs=[pl.BlockSpec(memory_space=pltpu.SMEM)] + _ret_in_specs(rows),
        out_specs=(o_specs[0], o_specs[1], state, state),
        out_shape=(_sds((n, RET_W), F32),) * 2 + (_sds((nc, HEADS, CHUNK, HEAD_DIM), F32),) * 2,
        scratch_shapes=[pltpu.VMEM((2, HEADS, CHUNK, HEAD_DIM), F32), pltpu.VMEM((2, HEADS, 3, CHUNK, CHUNK), F32)],
        compiler_params=_cp())


def _ret_bwd(p, cosf, sins, lg, do, s_prev, cc, ride=None):
    n = p.shape[0]
    nc = n // CHUNK
    rows = _ret_rows(cc, nc, lambda s: nc - 1 - s)

    def body(lg_ref, q0, k0, v0, c0, s0, q1, k1, v1, c1, s1, do0, do1, sp0, sp1,
             dq0, dk0, dv0, dq1, dk1, dv1, dlg_ref, dst, dec):
        @pl.when(pl.program_id(0) == 0)
        def _():
            dst[...] = jnp.zeros_like(dst)
            dlg_ref[...] = jnp.zeros_like(dlg_ref)
            _store_decay(lg_ref, dec)

        for dirn, (q_ref, k_ref, v_ref, c_ref, s_ref, do_ref, sp_ref, dq_ref, dk_ref, dv_ref) in enumerate(
                ((q0, k0, v0, c0, s0, do0, sp0, dq0, dk0, dv0), (q1, k1, v1, c1, s1, do1, sp1, dq1, dk1, dv1))):
            cf, ss = c_ref[...], s_ref[...]
            _, relf, cq, cs = _decay_exponents(dirn)
            for h in range(HEADS):
                sl = slice(h * HEAD_DIM, (h + 1) * HEAD_DIM)
                q = _rot(q_ref[:, sl], cf, ss)
                k = _rot(k_ref[:, sl], cf, ss) * K_SCALE
                v = v_ref[:, sl]
                dov = do_ref[:, sl]
                dm, wq, ws = dec[dirn, h, 0], dec[dirn, h, 1], dec[dirn, h, 2]
                a = _dot_nt(q, k)
                sp = sp_ref[h]
                dsn = dst[dirn, h]
                gc = jnp.exp(lg_ref[dirn, h] * CHUNK)
                g1 = _dot_nt(dov, sp)
                da = _dot_nt(dov, v) * dm
                dq = g1 * wq + _dot(da, k)
                h1 = _dot_nt(v, dsn)
                dk = _dot_tn(da, q) + h1 * ws
                dv_ref[:, sl] = _dot_tn(a * dm, dov) + _dot(k * ws, dsn)
                dst[dirn, h] = gc * dsn + _dot_tn(q * wq, dov)
                term = da * a * relf + q * g1 * (wq * cq) + k * h1 * (ws * cs) + sp * dsn * (CHUNK * gc)
                dlg_ref[dirn * HEADS + h:dirn * HEADS + h + 1, 0:HEAD_DIM] += jnp.sum(term, axis=0, keepdims=True)
                dq_ref[:, sl] = dq * cf + pltpu.roll(dq * ss, HEAD_DIM // 2, 1)
                dk_ref[:, sl] = (dk * cf + pltpu.roll(dk * ss, HEAD_DIM // 2, 1)) * K_SCALE

    wide = [pl.BlockSpec((CHUNK, RET_W), lambda s, row=row: (row(s), 0)) for row in rows]
    state = pl.BlockSpec((None, HEADS, CHUNK, HEAD_DIM), lambda s: (nc - 1 - s, 0, 0, 0))
    return _pcall_ride(
        body, ride, (lg, p, p, p, cosf, sins, p, p, p, cosf, sins, do, do, s_prev[0], s_prev[1]), name="ret_bwd", grid=(nc,),
        in_specs=[pl.BlockSpec(memory_space=pltpu.SMEM)] + _ret_in_specs(rows) + wide + [state, state],
        out_specs=(wide[0],) * 3 + (wide[1],) * 3 + (pl.BlockSpec((2 * HEADS, 8 * HEAD_DIM), lambda s: (0, 0)),),
        out_shape=(_sds((n, RET_W), F32),) * 6 + (_sds((2 * HEADS, 8 * HEAD_DIM), F32),),
        scratch_shapes=[pltpu.VMEM((2, HEADS, CHUNK, HEAD_DIM), F32), pltpu.VMEM((2, HEADS, 3, CHUNK, CHUNK), F32)],
        compiler_params=_cp())


def _shift_rows(cur, prev8, next8, k, seg_start, seg_end):
    tm = cur.shape[0]
    rows = _rows_iota(cur.shape)
    if k < 0:
        out = pltpu.roll(cur, -k, 0)
        for j in range(-k):
            halo = jnp.where(seg_start, 0.0, prev8[SUB + k + j:SUB + k + j + 1, :])
            out = jnp.where(rows == j, halo, out)
    else:
        out = pltpu.roll(cur, tm - k, 0)
        for j in range(k):
            halo = jnp.where(seg_end, 0.0, next8[j:j + 1, :])
            out = jnp.where(rows == tm - k + j, halo, out)
    return out


def _seg_flags(t, cb, nb):
    return jnp.logical_or(t == 0, t == cb), jnp.logical_or(t == cb - 1, t == nb - 1)


def _halo_specs(tile_of, n_rows, col):
    per = TM // SUB
    return [pl.BlockSpec((TM, LRU_W), lambda s: (tile_of(s), col)),
            pl.BlockSpec((SUB, LRU_W), lambda s: (jnp.maximum(tile_of(s) * per - 1, 0), col)),
            pl.BlockSpec((SUB, LRU_W), lambda s: (jnp.minimum((tile_of(s) + 1) * per, n_rows // SUB - 1), col))]


def _lru_gates(xr, prev8, next8, seg_start, seg_end, cw_ref, cb_ref, wg_ref, bg_ref, sp_ref):
    xm1 = _shift_rows(xr, prev8, next8, -1, seg_start, seg_end)
    xp1 = _shift_rows(xr, prev8, next8, 1, seg_start, seg_end)
    xp2 = _shift_rows(xr, prev8, next8, 2, seg_start, seg_end)
    xc = cb_ref[...] + xm1 * cw_ref[0:1, :] + xr * cw_ref[1:2, :] + xp1 * cw_ref[2:3, :] + xp2 * cw_ref[3:4, :]
    pre = _dot(xc, wg_ref[...]) + bg_ref[...]
    r = _sigmoid(pre[:, :LRU_W])
    i = _sigmoid(pre[:, LRU_W:])
    la = (-LRU_C) * r * sp_ref[...]
    a = jnp.exp(la)
    th = jnp.tanh(la)
    sq = jnp.sqrt(-2.0 * th / (1.0 - th))
    return xc, r, i, a, sq


def _scan_tile(a, b, ascending, a_sc, b_sc, carry, out_ref):
    tm, w = a.shape
    nsub = tm // SUB
    a = a.reshape(nsub, SUB, w)
    b = b.reshape(nsub, SUB, w)
    r8 = lax.broadcasted_iota(jnp.int32, a.shape, 1)
    for k in (1, 2, 4):
        if ascending:
            m = r8 >= k
            a_s, b_s = pltpu.roll(a, k, 1), pltpu.roll(b, k, 1)
        else:
            m = r8 < SUB - k
            a_s, b_s = pltpu.roll(a, SUB - k, 1), pltpu.roll(b, SUB - k, 1)
        b = a * jnp.where(m, b_s, 0.0) + b
        a = a * jnp.where(m, a_s, 1.0)
    a_sc[...] = a.reshape(tm, w)
    b_sc[...] = b.reshape(tm, w)

    def step(j, c):
        off = pl.multiple_of((j if ascending else nsub - 1 - j) * SUB, SUB)
        hb = a_sc[pl.ds(off, SUB), :] * c + b_sc[pl.ds(off, SUB), :]
        out_ref[pl.ds(off, SUB), :] = hb
        last = hb[SUB - 1:SUB, :] if ascending else hb[0:1, :]
        return jnp.broadcast_to(last, c.shape)

    carry[...] = lax.fori_loop(0, nsub, step, carry[...])


def _lru_fwd(p, wg, bg, sp, cw, cbias, dirn, cb, ride=None):
    n = p.shape[0]
    nb = n // TM
    tile_of = lambda s: _tile_order(dirn, s, cb, nb)

    def body(x_ref, xp_ref, xn_ref, wg_ref, bg_ref, sp_ref, cw_ref, cb_ref, h_ref, cin_ref, carry, a_sc, b_sc):
        s = pl.program_id(0)

        @pl.when(s == 0)
        def _():
            carry[...] = jnp.zeros_like(carry)

        seg_start, seg_end = _seg_flags(tile_of(s), cb, nb)
        xc, r, i, a, sq = _lru_gates(x_ref[...], xp_ref[...], xn_ref[...], seg_start, seg_end,
                                     cw_ref, cb_ref, wg_ref, bg_ref, sp_ref)
        cin_ref[...] = carry[...]
        _scan_tile(a, sq * (i * xc), dirn == 0, a_sc, b_sc, carry, h_ref)

    full = lambda shape: pl.BlockSpec(shape, lambda s: (0,) * len(shape))
    return _pcall_ride(
        body, ride, (p, p, p, wg, bg, sp, cw, cbias), name=f"lru_fwd{dirn}", grid=(nb,),
        in_specs=_halo_specs(tile_of, n, 4) + [full((LRU_W, 2 * LRU_W)), full((1, 2 * LRU_W)), full((1, LRU_W)),
                                               full((4, LRU_W)), full((1, LRU_W))],
        out_specs=(pl.BlockSpec((TM, LRU_W), lambda s: (tile_of(s), 0)),
                   pl.BlockSpec((None, SUB, LRU_W), lambda s: (tile_of(s), 0, 0))),
        out_shape=(_sds((n, LRU_W), F32), _sds((nb, SUB, LRU_W), F32)),
        scratch_shapes=[pltpu.VMEM((SUB, LRU_W), F32), pltpu.VMEM((TM, LRU_W), F32), pltpu.VMEM((TM, LRU_W), F32)],
        compiler_params=_cp())


def _lru_bwd(p, wg, bg, sp, cw, cbias, h, cin, dhs, dirn, cb, ride=None):
    n = p.shape[0]
    nb = n // TM
    tile_of = lambda s: _tile_order(dirn, nb - 1 - s, cb, nb)

    def body(x_ref, xp_ref, xn_ref, wg_ref, bg_ref, sp_ref, cw_ref, cb_ref, h_ref, cin_ref, dhs_ref,
             dxc_ref, dwd_ref, acc_ref, carry, a_sc, b_sc, mu_sc, dwg_ref):
        s = pl.program_id(0)

        @pl.when(s == 0)
        def _():
            carry[...] = jnp.zeros_like(carry)
            dwg_ref[...] = jnp.zeros_like(dwg_ref)
            acc_ref[...] = jnp.zeros_like(acc_ref)

        seg_start, seg_end = _seg_flags(tile_of(s), cb, nb)
        xc, r, i, a, sq = _lru_gates(x_ref[...], xp_ref[...], xn_ref[...], seg_start, seg_end,
                                     cw_ref, cb_ref, wg_ref, bg_ref, sp_ref)
        rows = _rows_iota(a.shape)
        hv = h_ref[...]
        dh = dhs_ref[...]
        mu_next = carry[0:1, :]
        _scan_tile(a, a * dh, dirn == 1, a_sc, b_sc, carry, mu_sc)
        mu = mu_sc[...]
        if dirn == 0:
            hprev = jnp.where(rows == 0, cin_ref[0:1, :], pltpu.roll(hv, 1, 0))
            lam = dh + jnp.where(rows == TM - 1, mu_next, pltpu.roll(mu, TM - 1, 0))
        else:
            hprev = jnp.where(rows == TM - 1, cin_ref[0:1, :], pltpu.roll(hv, TM - 1, 0))
            lam = dh + jnp.where(rows == 0, mu_next, pltpu.roll(mu, 1, 0))
        ds = lam * (i * xc)
        di = lam * (sq * xc)
        dla = lam * hprev * a - ds * (a * a) / jnp.maximum(sq, 1e-20)
        dpr = dla * ((-LRU_C) * sp_ref[...]) * r * (1.0 - r)
        dpi = di * i * (1.0 - i)
        dpre = jnp.concatenate([dpr, dpi], axis=1)
        dxc_ref[...] = lam * (sq * i) + _dot_nt(dpre, wg_ref[...])
        dwg_ref[...] += _dot_tn(xc, dpre)
        acc_ref[0:1, :] += jnp.sum(dpre, axis=0, keepdims=True)
        acc_ref[1:2, 0:LRU_W] += jnp.sum(dla * ((-LRU_C) * r), axis=0, keepdims=True)

        @pl.when(s == nb - 1)
        def _():
            low = lax.broadcasted_iota(jnp.int32, (LRU_BD, 2 * LRU_BD), 1) < LRU_BD
            for half in (0, LRU_W):
                for m in range(LRU_BLOCKS // 2):
                    lanes = slice(half + 2 * LRU_BD * m, half + 2 * LRU_BD * (m + 1))
                    even = dwg_ref[2 * m * LRU_BD:(2 * m + 1) * LRU_BD, lanes]
                    odd = dwg_ref[(2 * m + 1) * LRU_BD:(2 * m + 2) * LRU_BD, lanes]
                    dwd_ref[:, lanes] = jnp.where(low, even, odd)

    full = lambda shape: pl.BlockSpec(shape, lambda s: (0,) * len(shape))
    tile = pl.BlockSpec((TM, LRU_W), lambda s: (tile_of(s), 0))
    return _pcall_ride(
        body, ride, (p, p, p, wg, bg, sp, cw, cbias, h, cin, dhs), name=f"lru_bwd{dirn}", grid=(nb,),
        in_specs=_halo_specs(tile_of, n, 4) + [full((LRU_W, 2 * LRU_W)), full((1, 2 * LRU_W)), full((1, LRU_W)),
                                               full((4, LRU_W)), full((1, LRU_W)), tile,
                                               pl.BlockSpec((None, SUB, LRU_W), lambda s: (tile_of(s), 0, 0)), tile],
        out_specs=(tile, full((LRU_BD, 2 * LRU_W)), full((8, 2 * LRU_W))),
        out_shape=(_sds((n, LRU_W), F32), _sds((LRU_BD, 2 * LRU_W), F32), _sds((8, 2 * LRU_W), F32)),
        scratch_shapes=[pltpu.VMEM((SUB, LRU_W), F32)] + [pltpu.VMEM((TM, LRU_W), F32)] * 3
        + [pltpu.VMEM((LRU_W, 2 * LRU_W), F32)],
        compiler_params=_cp())


def _assemble_dp(dqs, dks, dvs, dg, dgate, dxcs, p, cw, cb, ride=None):
    n = p.shape[0]
    nb = n // TM
    tile_of = lambda s: s

    def body(dqf, dqb, dkf, dkb, dvf, dvb, dg_ref, dgate_ref, cf, pf, nf, cb_, pb, nb_, x_ref, xp_ref, xn_ref,
             cw_ref, dp_ref, acc_ref):
        s = pl.program_id(0)

        @pl.when(s == 0)
        def _():
            acc_ref[...] = jnp.zeros_like(acc_ref)

        seg_start, seg_end = _seg_flags(s, cb, nb)
        dp_ref[:, 0:RET_W] = (dqf[...] + dqb[...]).astype(BF16)
        dp_ref[:, RET_W:2 * RET_W] = (dkf[...] + dkb[...]).astype(BF16)
        dp_ref[:, 2 * RET_W:3 * RET_W] = (dvf[...] + dvb[...]).astype(BF16)
        dp_ref[:, 3 * RET_W:4 * RET_W] = dg_ref[...].astype(BF16)
        dxc = cf[...] + cb_[...]
        dprev = pf[...] + pb[...]
        dnext = nf[...] + nb_[...]
        dxr = (_shift_rows(dxc, dprev, dnext, 1, seg_start, seg_end) * cw_ref[0:1, :] + dxc * cw_ref[1:2, :]
               + _shift_rows(dxc, dprev, dnext, -1, seg_start, seg_end) * cw_ref[2:3, :]
               + _shift_rows(dxc, dprev, dnext, -2, seg_start, seg_end) * cw_ref[3:4, :])
        dp_ref[:, 4 * RET_W:4 * RET_W + LRU_W] = dxr.astype(BF16)
        dp_ref[:, 4 * RET_W + LRU_W:] = dgate_ref[...].astype(BF16)
        xr, xp, xn = x_ref[...], xp_ref[...], xn_ref[...]
        for j, k in enumerate((-1, 0, 1, 2)):
            xs = xr if k == 0 else _shift_rows(xr, xp, xn, k, seg_start, seg_end)
            acc_ref[j:j + 1, 0:LRU_W] += jnp.sum(dxc * xs, axis=0, keepdims=True)
        acc_ref[4:5, 0:LRU_W] += jnp.sum(dxc, axis=0, keepdims=True)

    t = pl.BlockSpec((TM, RET_W), lambda s: (s, 0))
    args = (dqs[0], dqs[1], dks[0], dks[1], dvs[0], dvs[1], dg, dgate, dxcs[0], dxcs[0], dxcs[0], dxcs[1], dxcs[1], dxcs[1],
            p, p, p, cw)
    return _pcall_ride(
        body, ride, args, name="assemble_dp", grid=(nb,),
        in_specs=[t] * 8 + _halo_specs(tile_of, n, 0) * 2 + _halo_specs(tile_of, n, 4)
        + [pl.BlockSpec((4, LRU_W), lambda s: (0, 0))],
        out_specs=(pl.BlockSpec((TM, 4 * RET_W + 2 * LRU_W), lambda s: (s, 0)), pl.BlockSpec((8, 2 * LRU_W), lambda s: (0, 0))),
        out_shape=(_sds((n, 4 * RET_W + 2 * LRU_W), BF16), _sds((8, 2 * LRU_W), F32)), compiler_params=_cp())


def _adamw(g, w, m, v):
    nm = ADAM_B1 * m + (1.0 - ADAM_B1) * g
    nv = ADAM_B2 * v + (1.0 - ADAM_B2) * (g * g)
    m_hat = nm / (1.0 - ADAM_B1 ** ADAM_STEP)
    v_hat = nv / (1.0 - ADAM_B2 ** ADAM_STEP)
    return (-ADAM_LR) * (m_hat / (jnp.sqrt(v_hat) + ADAM_EPS) + ADAM_WD * w), nm, nv


def _adam_many(items, name):
    n = len(items)

    def body(*refs):
        for i in range(n):
            g, w, m, v = (r[...] for r in refs[4 * i:4 * i + 4])
            for o_ref, val in zip(refs[4 * n + 3 * i:4 * n + 3 * i + 3], _adamw(g, w, m, v)):
                o_ref[...] = val

    out_shape = tuple(_sds(it[1].shape, F32) for it in items for _ in range(3))
    res = _pcall(body, name=name, out_shape=out_shape, compiler_params=_cp())(*[a for it in items for a in it])
    return [tuple(res[3 * i:3 * i + 3]) for i in range(n)]


def _sum_adam(parts_list, w, m, v, name):
    nparts, _, c = parts_list[0].shape
    r = w.shape[0]
    tr = min(parts_list[0].shape[1], 128)
    starts, o = [], 0
    for pa in parts_list:
        starts.append(o)
        o += pa.shape[1] // tr
    nseg = len(parts_list)

    def body(*refs):
        p_refs = refs[:nseg]
        w_ref, m_ref, v_ref, g_ref, d_ref, nm_ref, nv_ref = refs[nseg:]
        i = pl.program_id(0)
        for s, p_ref in enumerate(p_refs):
            end = starts[s + 1] if s + 1 < nseg else r // tr

            @pl.when(jnp.logical_and(i >= starts[s], i < end))
            def _():
                g = p_ref[0].astype(F32)
                for j in range(1, nparts):
                    g = g + p_ref[j].astype(F32)
                g_ref[...] = g
                d_ref[...], nm_ref[...], nv_ref[...] = _adamw(g, w_ref[...], m_ref[...], v_ref[...])

    def seg_spec(s):
        last = parts_list[s].shape[1] // tr - 1
        return pl.BlockSpec((nparts, tr, c), lambda i: (0, jnp.clip(i - starts[s], 0, last), 0))

    t = pl.BlockSpec((tr, c), lambda i: (i, 0))
    return _pcall(
        body, name=name, grid=(r // tr,),
        in_specs=[seg_spec(s) for s in range(nseg)] + [t, t, t],
        out_specs=(t, t, t, t), out_shape=(_sds((r, c), F32),) * 4, compiler_params=_cp(),
    )(*parts_list, w, m, v)


def _sum_parts(parts, name):
    nparts, r, c = parts.shape

    def body(p_ref, o_ref):
        g = p_ref[0]
        for j in range(1, nparts):
            g = g + p_ref[j]
        o_ref[...] = g

    return _pcall(body, name=name, out_shape=_sds((r, c), parts.dtype), compiler_params=_cp())(parts)


def _rot_tables(l_len, t_len):
    rows = t_len // GRID_W
    row = jnp.repeat(jnp.arange(rows, dtype=F32), GRID_W)
    col = jnp.tile(jnp.arange(GRID_W, dtype=F32), rows)
    n_freq = HEAD_DIM // 4
    inv = ROPE_BASE ** (-jnp.arange(n_freq, dtype=F32) / n_freq)
    ang = jnp.concatenate([row[:, None] * inv, col[:, None] * inv], axis=-1)
    cos, sin = jnp.cos(ang), jnp.sin(ang)
    cosf = jnp.concatenate([jnp.ones((l_len, HEAD_DIM), F32), jnp.concatenate([cos, cos], axis=-1)], axis=0)
    sins = jnp.concatenate([jnp.zeros((l_len, HEAD_DIM), F32), jnp.concatenate([-sin, sin], axis=-1)], axis=0)
    return cosf, sins


def _block_diag(w):
    eye = jnp.eye(LRU_BLOCKS, dtype=w.dtype)
    return (w[:, :, None, :] * eye[:, None, :, None]).reshape(LRU_W, LRU_W)


def _blocks_from_lanes(dwd_half):
    return dwd_half.reshape(LRU_BD, LRU_BLOCKS, LRU_BD).transpose(1, 0, 2)


def _silu(x):
    return x * jax.nn.sigmoid(x)


def kernel(x, c, ctx, c_ctx, w_ada, b_ada, norm1_g, norm2_g, w_in, ret_decay, conv_w, conv_b, lru_wa, lru_ba, lru_wx, lru_bx, lru_lambda, w_out, w_mlp1, w_mlp2, final_g, loss_target, m_c_ctx, m_w_ada, m_b_ada, m_norm1_g, m_norm2_g, m_w_in, m_ret_decay, m_conv_w, m_conv_b, m_lru_wa, m_lru_ba, m_lru_wx, m_lru_bx, m_lru_lambda, m_w_out, m_w_mlp1, m_w_mlp2, m_final_g, v_c_ctx, v_w_ada, v_b_ada, v_norm1_g, v_norm2_g, v_w_in, v_ret_decay, v_conv_w, v_conv_b, v_lru_wa, v_lru_ba, v_lru_wx, v_lru_bx, v_lru_lambda, v_w_out, v_w_mlp1, v_w_mlp2, v_final_g):
    t_len, d = x.shape[1], x.shape[2]
    l_len = ctx.shape[1]
    cb, cc = l_len // TM, l_len // CHUNK
    me = 4 * lax.axis_index("x") + 2 * lax.axis_index("y") + lax.axis_index("c")
    x2d, ctx2d, tgt2d = x[0], ctx[0], loss_target[0]
    ada_cols = w_ada.shape[2]
    wa2d = w_ada[0]

    sc_loc = conv_w.shape[2]
    pack_a = jnp.zeros((8, d), F32)
    pack_a = pack_a.at[0].set(_silu(c[0]))
    pack_a = pack_a.at[1, :4 * sc_loc].set(conv_w[0].reshape(-1))
    pack_a = pack_a.at[2, :2 * sc_loc].set(lru_ba[0].reshape(-1))
    pack_a = pack_a.at[3, :2 * sc_loc].set(lru_bx[0].reshape(-1))
    pack_a = pack_a.at[4, :2 * sc_loc].set(lru_lambda[0].reshape(-1))
    all_a = _all_gather(pack_a, "gather_small_in")
    s16 = jnp.zeros((16, d), F32).at[0:8].set(all_a[:, 0, :]).at[8].set(_silu(c_ctx))

    def unshard(row, k):
        return all_a[:, row, :k * sc_loc].reshape(N_DEV, k, sc_loc).transpose(1, 0, 2).reshape(k, N_DEV * sc_loc)

    conv_w_full = unshard(1, 4)
    ba_full, bx_full, lam_full = unshard(2, 2), unshard(3, 2), unshard(4, 2)

    b_cols = lax.dynamic_slice(b_ada, (0, me * ada_cols), (1, ada_cols))
    mod_parts = _all_gather(_mod_part(s16, wa2d, b_cols), "gather_mod")
    mod_all = mod_parts.transpose(1, 0, 2).reshape(16, N_DEV * ada_cols)
    mod_me = lax.dynamic_slice(mod_all, (me, 0), (1, 6 * d)).reshape(6, d)
    mod_c = mod_all[8].reshape(6, d)
    modrows = jnp.concatenate([mod_c[0:2], mod_me], axis=0)

    lg = jax.nn.log_sigmoid(ret_decay[0])
    sp = jax.nn.softplus(-lam_full)
    wg = [jnp.concatenate([_block_diag(lru_wa[0, dd]), _block_diag(lru_wx[0, dd])], axis=1).astype(BF16) for dd in (0, 1)]
    bg = [jnp.concatenate([ba_full[dd], bx_full[dd]])[None, :] for dd in (0, 1)]
    cosf, sins = _rot_tables(l_len, t_len)

    (hn, hnt), win_g = _norm1_fwd(ctx2d, x2d, norm1_g, modrows, cb, ride=("gather", w_in[0].astype(BF16)))
    p, wout_g = _mm_nn(hn, win_g, F32, "mm_in", ride=("gather", w_out[0].astype(BF16)))
    wout_g = wout_g.reshape(1, d, d)
    (o0, o1, sp0, sp1), w1_g = _ret_fwd(p, cosf, sins, lg, cc, ride=("gather", w_mlp1[0].astype(BF16)))
    o, s_prev = [o0, o1], [sp0, sp1]
    (h0, cin0), _ = _lru_fwd(p, wg[0], bg[0], sp[0:1], conv_w_full, conv_b, 0, cb)
    (h1, cin1), _ = _lru_fwd(p, wg[1], bg[1], sp[1:2], conv_w_full, conv_b, 1, cb)
    h, cin = [h0, h1], [cin0, cin1]
    mix = _mix_fwd(o[0], o[1], p, h[0], h[1], cb, t_len)
    y = _mm_nn(mix, wout_g, F32, "mm_out")
    x1, h2, h2t = _res_norm2(x2d, y, norm2_g, modrows)
    r, w2_g = _mm_nn(h2, w1_g, BF16, "mm_mlp1", relu_out=True, ride=("gather", w_mlp2[0].astype(BF16)))
    w2_g = w2_g.reshape(1, 4 * d, d)
    z = _mm_nn(r, w2_g, F32, "mm_mlp2", square_lhs=True)
    dx2, dz, dzt, facc = _final_fwd_bwd(x1, z, tgt2d, final_g[None, :], modrows)

    du = _mm_nt(dz, w2_g, BF16, "mm_da2", relu_mul=r)
    gw2 = _mm_wgrad(dzt, r, "mm_dw2", w2_g.shape[1] // N_DEV, BF16, transpose_out=True, square_rhs=True)
    gw1_lo, gw1_hi = _mm_wgrad(h2t, du, "mm_dw1", w1_g.shape[2], BF16, halves=True)
    dh2 = _mm_nt(du, w1_g, F32, "mm_dh2")
    dx1, dy, n2acc = _bwd_norm2(dh2, x1, dx2, y, norm2_g, modrows)
    dmix = _mm_nt(dy, wout_g, F32, "mm_dmix")
    gwo = _mm_tn(mix, dy, "mm_dwout", False, d, BF16, 512).reshape(N_DEV, -1, d)
    do, dg, dgate, dhs = _mix_bwd(dmix, o[0], o[1], p, h[0], h[1], cb)
    dxcs, dwgs, laccs, rides, gw1_all = [], [], [], [("a2a", gw1_lo), ("a2a", gw1_hi)], []
    (dq0, dk0, dv0, dq1, dk1, dv1, dlg_lanes), gw2_all = _ret_bwd(p, cosf, sins, lg, do, s_prev, cc, ride=("a2a", gw2))
    dqs, dks, dvs = [dq0, dq1], [dk0, dk1], [dv0, dv1]
    for dd in (0, 1):
        (dxc_, dwg_, lacc_), got_ = _lru_bwd(p, wg[dd], bg[dd], sp[dd:dd + 1], conv_w_full, conv_b, h[dd], cin[dd], dhs, dd, cb,
                                             ride=rides[dd])
        dxcs.append(dxc_); dwgs.append(dwg_); laccs.append(lacc_); gw1_all.append(got_)
    (dp, cacc), gwo_all = _assemble_dp(dqs, dks, dvs, dg, dgate, dxcs, p, conv_w_full, cb, ride=("a2a", gwo))
    gwi_lo, gwi_hi = _mm_wgrad(hnt, dp, "mm_dwin", win_g.shape[2], BF16, halves=True)
    dhn, gwi_lo_all = _mm_nt(dp, win_g, F32, "mm_dhn", ride=("a2a", gwi_lo))
    (grad_x, n1acc), gwi_hi_all = _bwd_norm1(dhn, ctx2d, x2d, dx1, norm1_g, modrows, cb, ride=("a2a", gwi_hi))

    pack_b = jnp.concatenate([n1acc, n2acc, facc, cacc, laccs[0], laccs[1], dlg_lanes, dwgs[0], dwgs[1]], axis=0)
    all_b = _all_gather(pack_b, "gather_small_grads")
    tot = _sum_parts(all_b, "sum_small_grads")
    t_n1, t_n2, t_f, t_conv, t_dlg = tot[0:8], tot[8:16], tot[16:24], tot[24:32, :LRU_W], tot[48:56, :HEAD_DIM]
    t_l = [tot[32:40], tot[40:48]]
    t_dwd = [tot[56:56 + LRU_BD], tot[56 + LRU_BD:56 + 2 * LRU_BD]]
    loss = (0.5 / d) * jnp.sum(t_f[2])
    t_wa = jnp.stack([_blocks_from_lanes(t_dwd[dd][:, :LRU_W]) for dd in (0, 1)])
    t_wx = jnp.stack([_blocks_from_lanes(t_dwd[dd][:, LRU_W:]) for dd in (0, 1)])
    t_ba = jnp.stack([t_l[dd][0, :LRU_W] for dd in (0, 1)])
    t_bx = jnp.stack([t_l[dd][0, LRU_W:] for dd in (0, 1)])
    t_sp = jnp.stack([t_l[dd][1, :LRU_W] for dd in (0, 1)])
    dm_rows = jnp.stack([all_b[:, i, :] for i in (2, 3, 11, 8, 9, 17)], axis=1).reshape(N_DEV, 6 * d)
    dm_c = jnp.concatenate([t_n1[0], t_n1[1], jnp.zeros((4 * d,), F32)])
    dm16 = jnp.zeros((16, 6 * d), F32).at[0:8].set(dm_rows).at[8].set(dm_c)
    g_b_ada = jnp.sum(dm16, axis=0)[None, :]
    dm_cols = lax.dynamic_slice(dm16, (0, me * ada_cols), (16, ada_cols))
    g_w_ada, ds16 = _ada_bwd(s16, dm_cols, wa2d)
    ds_all = _all_gather(ds16[8:16], "gather_dsilu")
    dsilu_cc = _sum_parts(ds_all, "sum_dsilu")[0]
    sg_cc = jax.nn.sigmoid(c_ctx)
    g_c_ctx = dsilu_cc * (sg_cc * (1.0 + c_ctx * (1.0 - sg_cc)))

    g_ret_decay = jnp.sum(t_dlg, axis=-1).reshape(2, HEADS) * jax.nn.sigmoid(-ret_decay[0])
    g_lambda_full = -t_sp * jax.nn.sigmoid(-lam_full)

    def my_cols(full):
        return lax.dynamic_slice(full, (0, me * sc_loc), (full.shape[0], sc_loc))

    small_g = dict(
        c_ctx=g_c_ctx[None], b_ada=g_b_ada, norm1_g=t_n1[4:5], norm2_g=t_n2[2:3], ret_decay=g_ret_decay,
        conv_w=my_cols(t_conv[0:4]), conv_b=t_conv[4:5], lru_wa=t_wa.reshape(-1, LRU_BD), lru_ba=my_cols(t_ba),
        lru_wx=t_wx.reshape(-1, LRU_BD), lru_bx=my_cols(t_bx), lru_lambda=my_cols(g_lambda_full), final_g=t_f[0:1])
    small = dict(
        c_ctx=(c_ctx, m_c_ctx, v_c_ctx), b_ada=(b_ada, m_b_ada, v_b_ada), norm1_g=(norm1_g, m_norm1_g, v_norm1_g),
        norm2_g=(norm2_g, m_norm2_g, v_norm2_g), ret_decay=(ret_decay, m_ret_decay, v_ret_decay),
        conv_w=(conv_w, m_conv_w, v_conv_w), conv_b=(conv_b, m_conv_b, v_conv_b), lru_wa=(lru_wa, m_lru_wa, v_lru_wa),
        lru_ba=(lru_ba, m_lru_ba, v_lru_ba), lru_wx=(lru_wx, m_lru_wx, v_lru_wx), lru_bx=(lru_bx, m_lru_bx, v_lru_bx),
        lru_lambda=(lru_lambda, m_lru_lambda, v_lru_lambda), final_g=(final_g, m_final_g, v_final_g))
    names = list(small)
    items = [(small_g[k],) + tuple(a.reshape(small_g[k].shape) for a in small[k]) for k in names]
    res = {}
    for k, it, (d_, m_, v_) in zip(names, items, _adam_many(items, "adam_small")):
        shape = small[k][0].shape
        res[k] = tuple(a.reshape(shape) for a in (it[0], d_, m_, v_))

    def big(parts, w, m, v, name):
        out = _sum_adam(parts, w[0], m[0], v[0], name)
        return tuple(a[None] for a in out)

    res["w_ada"] = big([g_w_ada[None]], w_ada, m_w_ada, v_w_ada, "adam_w_ada")
    res["w_in"] = big([gwi_lo_all, gwi_hi_all], w_in, m_w_in, v_w_in, "adam_w_in")
    res["w_out"] = big([gwo_all], w_out, m_w_out, v_w_out, "adam_w_out")
    res["w_mlp1"] = big(gw1_all, w_mlp1, m_w_mlp1, v_w_mlp1, "adam_w_mlp1")
    res["w_mlp2"] = big([gw2_all], w_mlp2, m_w_mlp2, v_w_mlp2, "adam_w_mlp2")

    order = ["c_ctx", "w_ada", "b_ada", "norm1_g", "norm2_g", "w_in", "ret_decay", "conv_w", "conv_b", "lru_wa", "lru_ba",
             "lru_wx", "lru_bx", "lru_lambda", "w_out", "w_mlp1", "w_mlp2", "final_g"]
    outs = [loss, grad_x[None]]
    for j in range(4):
        outs += [res[k][j] for k in order]
    return tuple(outs)
```

```python
import jax
import jax.numpy as jnp
from jax import lax
from jax.experimental import pallas as pl
from jax.experimental.pallas import tpu as pltpu

F32 = jnp.float32
BF16 = jnp.bfloat16
AXES = ("x", "y", "c")
N_DEV = 8
MESH = pl.DeviceIdType.MESH

HEADS = 4
HEAD_DIM = 128
CHUNK = 128
RET_W = HEADS * HEAD_DIM
LRU_W = 512
LRU_BLOCKS = 8
LRU_BD = LRU_W // LRU_BLOCKS
LRU_C = 8.0
EPS = 1e-6
K_SCALE = HEAD_DIM ** -0.5
ROPE_BASE = 10000.0
GRID_W = 64
TM = 256
SUB = 8

ADAM_LR = 0.001
ADAM_B1 = 0.9
ADAM_B2 = 0.999
ADAM_EPS = 1e-08
ADAM_WD = 0.01
ADAM_STEP = 10

COL_G, COL_XR, COL_GATE = 0, 1, 2

R_CSH1, R_CSC1, R_SH1, R_SC1, R_G1, R_SH2, R_SC2, R_G2 = range(8)


def _pcall(body, **kw):
    return pl.pallas_call(body, **kw)


def _cp(vmem_mb=48):
    return pltpu.CompilerParams(vmem_limit_bytes=vmem_mb << 20)


def _sds(shape, dtype):
    return jax.ShapeDtypeStruct(shape, dtype)


def _dot(a, b):
    return jnp.dot(a.astype(BF16), b.astype(BF16), preferred_element_type=F32)


def _dot_nt(a, b):
    return lax.dot_general(a.astype(BF16), b.astype(BF16), (((1,), (1,)), ((), ())), preferred_element_type=F32)


def _dot_tn(a, b):
    return lax.dot_general(a.astype(BF16), b.astype(BF16), (((0,), (0,)), ((), ())), preferred_element_type=F32)


def _sigmoid(x):
    return jax.nn.sigmoid(x)


def _gelu(x):
    return 0.5 * x * (1.0 + jnp.tanh(0.7978845608028654 * (x + 0.044715 * x * x * x)))


def _dgelu(x):
    t = jnp.tanh(0.7978845608028654 * (x + 0.044715 * x * x * x))
    return 0.5 * (1.0 + t) + 0.5 * x * (1.0 - t * t) * 0.7978845608028654 * (1.0 + 3.0 * 0.044715 * x * x)


def _rows_iota(shape):
    return lax.broadcasted_iota(jnp.int32, shape, 0)


def _tile_order(dirn, s, cb, nb):
    if dirn == 0:
        return s
    return jnp.where(s < cb, cb - 1 - s, nb - 1 - (s - cb))


_SEMS = [pltpu.SemaphoreType.DMA((7,)), pltpu.SemaphoreType.DMA((7,)), pltpu.SemaphoreType.DMA(())]
_ANY = pl.BlockSpec(memory_space=pl.ANY)


def _gather_copies(x_ref, out_ref, send_sems, recv_sems, local_sem):
    mx, my, mc = lax.axis_index("x"), lax.axis_index("y"), lax.axis_index("c")
    me, sibling = (mx, my, mc), (mx, my, 1 - mc)
    chips = [(1 - mx, my), (mx, 1 - my), (1 - mx, 1 - my)]

    def slot(px, py, pc):
        return out_ref.at[4 * px + 2 * py + pc]

    def copy(k, block, to, src=None):
        return pltpu.make_async_remote_copy(
            src_ref=slot(*block) if src is None else src, dst_ref=slot(*block),
            send_sem=send_sems.at[k], recv_sem=recv_sems.at[k], device_id=to, device_id_type=MESH)

    mine = pltpu.make_async_copy(x_ref, slot(*me), local_sem)
    first = [copy(0, me, sibling, src=x_ref)] + [copy(1 + j, me, (*chip, mc), src=x_ref) for j, chip in enumerate(chips)]
    passed = [copy(4 + j, (*chip, mc), sibling) for j, chip in enumerate(chips)]
    recv_ici = [copy(1 + j, (*chip, mc), me) for j, chip in enumerate(chips)]
    recv_d2d = [copy(0, sibling, me)] + [copy(4 + j, (*chip, 1 - mc), me) for j, chip in enumerate(chips)]
    return mine, first, passed, recv_ici, recv_d2d


def _gather_start(*refs):
    mine, first, _, _, _ = _gather_copies(*refs)
    mine.start()
    for cp in first:
        cp.start()


def _gather_finish(*refs):
    mine, first, passed, recv_ici, recv_d2d = _gather_copies(*refs)
    for landed, onward in zip(recv_ici, passed):
        landed.wait_recv()
        onward.start()
    for landed in recv_d2d:
        landed.wait_recv()
    for cp in first + passed:
        cp.wait_send()
    mine.wait()


def _a2a_copies(g_ref, out_ref, send_sems, recv_sems, local_sem):
    mx, my, mc = lax.axis_index("x"), lax.axis_index("y"), lax.axis_index("c")
    me = 4 * mx + 2 * my + mc
    mine = pltpu.make_async_copy(g_ref.at[me], out_ref.at[me], local_sem)
    copies = []
    for k in range(1, N_DEV):
        px = 1 - mx if (k >> 2) & 1 else mx
        py = 1 - my if (k >> 1) & 1 else my
        pc = 1 - mc if k & 1 else mc
        copies.append(pltpu.make_async_remote_copy(
            src_ref=g_ref.at[4 * px + 2 * py + pc], dst_ref=out_ref.at[me],
            send_sem=send_sems.at[k - 1], recv_sem=recv_sems.at[k - 1],
            device_id=(px, py, pc), device_id_type=MESH))
    return mine, copies


def _a2a_start(*refs):
    mine, copies = _a2a_copies(*refs)
    mine.start()
    for cp in copies:
        cp.start()


def _a2a_finish(*refs):
    mine, copies = _a2a_copies(*refs)
    for cp in copies:
        cp.wait()
    mine.wait()


_EXCHANGES = {"gather": (_gather_start, _gather_finish), "a2a": (_a2a_start, _a2a_finish)}


def _exchange_shape(kind, src):
    return _sds((N_DEV,) + src.shape if kind == "gather" else src.shape, src.dtype)


def _all_gather(x, name):
    def body(x_ref, out_ref, *sems):
        _gather_start(x_ref, out_ref, *sems)
        _gather_finish(x_ref, out_ref, *sems)

    return _pcall(body, name=name, out_shape=_exchange_shape("gather", x), in_specs=[_ANY], out_specs=_ANY,
                  scratch_shapes=list(_SEMS))(x)


def _pcall_ride(body, ride, args, *, name, grid, in_specs, out_specs, out_shape, scratch_shapes=(), compiler_params=None):
    if ride is None:
        out = _pcall(body, name=name, grid=grid, in_specs=in_specs, out_specs=out_specs, out_shape=out_shape,
                     scratch_shapes=list(scratch_shapes), compiler_params=compiler_params)(*args)
        return out, None
    kind, src = ride
    start, finish = _EXCHANGES[kind]
    single = not isinstance(out_shape, (tuple, list))
    out_specs_t = (out_specs,) if single else tuple(out_specs)
    out_shape_t = (out_shape,) if single else tuple(out_shape)
    n_in, n_out, n_sc = len(in_specs), len(out_shape_t), len(scratch_shapes)

    def wrapped(*refs):
        ins, src_ref = refs[:n_in], refs[n_in]
        outs, dst_ref = refs[n_in + 1:n_in + 1 + n_out], refs[n_in + 1 + n_out]
        scratch = refs[n_in + 2 + n_out:n_in + 2 + n_out + n_sc]
        sems = refs[n_in + 2 + n_out + n_sc:]
        first = pl.program_id(0) == 0
        last = pl.program_id(0) == grid[0] - 1
        for ax in range(1, len(grid)):
            first = jnp.logical_and(first, pl.program_id(ax) == 0)
            last = jnp.logical_and(last, pl.program_id(ax) == grid[ax] - 1)

        @pl.when(first)
        def _():
            start(src_ref, dst_ref, *sems)

        body(*ins, *outs, *scratch)

        @pl.when(last)
        def _():
            finish(src_ref, dst_ref, *sems)

    res = _pcall(wrapped, name=name, grid=grid, in_specs=list(in_specs) + [_ANY], out_specs=out_specs_t + (_ANY,),
                 out_shape=out_shape_t + (_exchange_shape(kind, src),),
                 scratch_shapes=list(scratch_shapes) + list(_SEMS), compiler_params=compiler_params)(*args, src)
    return (res[0] if single else tuple(res[:-1])), res[-1]


def _mm_nn(a, w, out_dtype, name, square_lhs=False, relu_out=False, ride=None):
    m, k = a.shape
    nb, _, bn = w.shape

    def body(a_ref, w_ref, o_ref):
        av = a_ref[...]
        if square_lhs:
            av = av * av
        for j in range(nb):
            r = jnp.dot(av, w_ref[j], preferred_element_type=F32)
            if relu_out:
                r = jnp.maximum(r, 0.0)
            o_ref[:, j * bn:(j + 1) * bn] = r.astype(out_dtype)

    out, ex = _pcall_ride(
        body, ride, (a, w), name=name, grid=(m // TM,),
        in_specs=[pl.BlockSpec((TM, k), lambda i: (i, 0)), pl.BlockSpec((nb, k, bn), lambda i: (0, 0, 0))],
        out_specs=pl.BlockSpec((TM, nb * bn), lambda i: (i, 0)),
        out_shape=_sds((m, nb * bn), out_dtype), compiler_params=_cp())
    return out if ride is None else (out, ex)


def _mm_nt(dy, w, out_dtype, name, relu_mul=None, ride=None):
    m = dy.shape[0]
    nb, k, bn = w.shape

    def body(*refs):
        if relu_mul is None:
            dy_ref, w_ref, o_ref, wt = refs
        else:
            dy_ref, w_ref, r_ref, o_ref, wt = refs

        @pl.when(pl.program_id(0) == 0)
        def _():
            for j in range(nb):
                wt[j * bn:(j + 1) * bn, :] = w_ref[j].T

        acc = jnp.dot(dy_ref[...], wt[...], preferred_element_type=F32)
        if relu_mul is not None:
            acc = acc * (2.0 * r_ref[...].astype(F32))
        o_ref[...] = acc.astype(out_dtype)

    in_specs = [pl.BlockSpec((TM, nb * bn), lambda i: (i, 0)), pl.BlockSpec((nb, k, bn), lambda i: (0, 0, 0))]
    args = [dy, w]
    if relu_mul is not None:
        in_specs.append(pl.BlockSpec((TM, k), lambda i: (i, 0)))
        args.append(relu_mul)
    out, ex = _pcall_ride(
        body, ride, args, name=name, grid=(m // TM,), in_specs=in_specs,
        out_specs=pl.BlockSpec((TM, k), lambda i: (i, 0)),
        out_shape=_sds((m, k), out_dtype), scratch_shapes=[pltpu.VMEM((nb * bn, k), BF16)], compiler_params=_cp())
    return out if ride is None else (out, ex)


def _mm_tn(a, b, name, col_blocks, block, out_dtype, tm, square_lhs=False):
    m, k = a.shape
    nn = b.shape[1]
    steps = m // tm
    if col_blocks:
        nblk, acc_shape = nn // block, (k, block)
        a_spec = pl.BlockSpec((tm, k), lambda j, s: (s, 0))
        b_spec = pl.BlockSpec((tm, block), lambda j, s: (s, j))
    else:
        nblk, acc_shape = k // block, (block, nn)
        a_spec = pl.BlockSpec((tm, block), lambda j, s: (s, j))
        b_spec = pl.BlockSpec((tm, nn), lambda j, s: (s, 0))

    def body(a_ref, b_ref, o_ref, acc):
        s = pl.program_id(1)

        @pl.when(s == 0)
        def _():
            acc[...] = jnp.zeros_like(acc)

        av = a_ref[...]
        if square_lhs:
            av = av.astype(F32)
            av = (av * av).astype(BF16)
        acc[...] += _dot_tn(av, b_ref[...])

        @pl.when(s == steps - 1)
        def _():
            o_ref[...] = acc[...].astype(out_dtype)

    return _pcall(
        body, name=name, grid=(nblk, steps), in_specs=[a_spec, b_spec],
        out_specs=pl.BlockSpec((None,) + acc_shape, lambda j, s: (j, 0, 0)),
        out_shape=_sds((nblk,) + acc_shape, out_dtype),
        scratch_shapes=[pltpu.VMEM(acc_shape, F32)], compiler_params=_cp(),
    )(a, b)


def _mm_wgrad(at, b, name, bn, out_dtype, transpose_out=False, square_rhs=False, halves=False, ride=None):
    k, m = at.shape
    nblk = b.shape[1] // bn
    rows, cols = (bn, k) if transpose_out else (k, bn)
    nout = 2 if halves else 1
    per = rows // nout

    def body(a_ref, b_ref, *o_refs):
        bv = b_ref[...]
        if square_rhs:
            bv = bv * bv
        r = jnp.dot(a_ref[...], bv, preferred_element_type=F32)
        r = (r.T if transpose_out else r).astype(out_dtype)
        for i, o_ref in enumerate(o_refs):
            o_ref[...] = r[i * per:(i + 1) * per, :]

    out, ex = _pcall_ride(
        body, ride, (at, b), name=name, grid=(nblk,),
        in_specs=[pl.BlockSpec((k, m), lambda j: (0, 0)), pl.BlockSpec((m, bn), lambda j: (0, j))],
        out_specs=tuple(pl.BlockSpec((None, per, cols), lambda j: (j, 0, 0)) for _ in range(nout)),
        out_shape=tuple(_sds((nblk, per, cols), out_dtype) for _ in range(nout)), compiler_params=_cp())
    out = out if halves else out[0]
    return out if ride is None else (out, ex)


def _mm_in(hn, w, cosf, sins, ride):
    m, k = hn.shape
    nb, _, bn = w.shape

    def body(a_ref, w_ref, c_ref, s_ref, qkv_ref, rest_ref, pt):
        av = a_ref[...]
        for j in range(nb):
            pt[:, j * bn:(j + 1) * bn] = jnp.dot(av, w_ref[j], preferred_element_type=F32)
        cf, ss = c_ref[...], s_ref[...]
        for h in range(HEADS):
            sq = slice(h * HEAD_DIM, (h + 1) * HEAD_DIM)
            sk = slice(RET_W + h * HEAD_DIM, RET_W + (h + 1) * HEAD_DIM)
            qkv_ref[:, sq] = _rot(pt[:, sq], cf, ss).astype(BF16)
            qkv_ref[:, sk] = (_rot(pt[:, sk], cf, ss) * K_SCALE).astype(BF16)
        qkv_ref[:, 2 * RET_W:] = pt[:, 2 * RET_W:3 * RET_W].astype(BF16)
        rest_ref[...] = pt[:, 3 * RET_W:]

    tab = pl.BlockSpec((TM, HEAD_DIM), lambda i: (i, 0))
    wide = pl.BlockSpec((TM, 3 * RET_W), lambda i: (i, 0))
    return _pcall_ride(
        body, ride, (hn, w, cosf, sins), name="mm_in", grid=(m // TM,),
        in_specs=[pl.BlockSpec((TM, k), lambda i: (i, 0)), pl.BlockSpec((nb, k, bn), lambda i: (0, 0, 0)), tab, tab],
        out_specs=(wide, wide), out_shape=(_sds((m, 3 * RET_W), BF16), _sds((m, nb * bn - 3 * RET_W), F32)),
        scratch_shapes=[pltpu.VMEM((TM, nb * bn), F32)], compiler_params=_cp())


def _mod_part(s16, w_ada, b_cols):
    def body(s_ref, w_ref, b_ref, o_ref):
        o_ref[...] = _dot(s_ref[...], w_ref[...]) + b_ref[...]

    return _pcall(body, name="mod_part", out_shape=_sds((s16.shape[0], w_ada.shape[1]), F32),
                  compiler_params=_cp())(s16, w_ada, b_cols)


def _ada_bwd(s16, dm_cols, w_ada):
    def body(s_ref, d_ref, w_ref, gw_ref, ds_ref):
        gw_ref[...] = _dot_tn(s_ref[...], d_ref[...])
        ds_ref[...] = _dot_nt(d_ref[...], w_ref[...])

    return _pcall(body, name="ada_bwd",
                  out_shape=(_sds(w_ada.shape, F32), _sds(s16.shape, F32)), compiler_params=_cp())(s16, dm_cols, w_ada)


def _norm1_fwd(ctx, x, g, modrows, cb, ride=None):
    l_len, d = ctx.shape
    nb = (l_len + x.shape[0]) // TM

    def body(ctx_ref, x_ref, g_ref, m_ref, o_ref, ot_ref):
        is_ctx = pl.program_id(0) < cb
        xin = jnp.where(is_ctx, ctx_ref[...], x_ref[...])
        sh = jnp.where(is_ctx, m_ref[R_CSH1:R_CSH1 + 1, :], m_ref[R_SH1:R_SH1 + 1, :])
        sc = jnp.where(is_ctx, m_ref[R_CSC1:R_CSC1 + 1, :], m_ref[R_SC1:R_SC1 + 1, :])
        ms = jnp.mean(xin * xin, axis=-1, keepdims=True)
        n = xin * lax.rsqrt(ms + EPS) * g_ref[...]
        hn = n * (1.0 + sc) + sh
        o_ref[...] = hn.astype(BF16)
        ot_ref[...] = hn.T.astype(BF16)

    return _pcall_ride(
        body, ride, (ctx, x, g, modrows), name="norm1_fwd", grid=(nb,),
        in_specs=[pl.BlockSpec((TM, d), lambda i: (jnp.minimum(i, cb - 1), 0)),
                  pl.BlockSpec((TM, d), lambda i: (jnp.maximum(i - cb, 0), 0)),
                  pl.BlockSpec((1, d), lambda i: (0, 0)), pl.BlockSpec((8, d), lambda i: (0, 0))],
        out_specs=(pl.BlockSpec((TM, d), lambda i: (i, 0)), pl.BlockSpec((d, TM), lambda i: (0, i))),
        out_shape=(_sds((nb * TM, d), BF16), _sds((d, nb * TM), BF16)), compiler_params=_cp())


def _mix_fwd(o_f, o_b, p, h_f, h_b, cb, t_len):
    def body(of_ref, ob_ref, g_ref, gate_ref, hf_ref, hb_ref, mix_ref):
        o = of_ref[...] + ob_ref[...]
        g = g_ref[...]
        sg = g * _sigmoid(g)
        for hh in range(HEADS):
            sl = slice(hh * HEAD_DIM, (hh + 1) * HEAD_DIM)
            oh = o[:, sl]
            yc = oh - jnp.mean(oh, axis=-1, keepdims=True)
            var = jnp.mean(yc * yc, axis=-1, keepdims=True)
            mix_ref[:, sl] = (sg[:, sl] * (yc * lax.rsqrt(var + EPS))).astype(BF16)
        mix_ref[:, RET_W:] = ((hf_ref[...] + hb_ref[...]) * _gelu(gate_ref[...])).astype(BF16)

    row = lambda i: (i + cb, 0)
    return _pcall(
        body, name="mix_fwd", grid=(t_len // TM,),
        in_specs=[pl.BlockSpec((TM, RET_W), row), pl.BlockSpec((TM, RET_W), row),
                  pl.BlockSpec((TM, RET_W), lambda i: (i + cb, COL_G)), pl.BlockSpec((TM, LRU_W), lambda i: (i + cb, COL_GATE)),
                  pl.BlockSpec((TM, LRU_W), row), pl.BlockSpec((TM, LRU_W), row)],
        out_specs=pl.BlockSpec((TM, RET_W + LRU_W), lambda i: (i, 0)),
        out_shape=_sds((t_len, RET_W + LRU_W), BF16), compiler_params=_cp(),
    )(o_f, o_b, p, p, h_f, h_b)


def _res_norm2(x, y, g, modrows):
    t_len, d = x.shape

    def body(x_ref, y_ref, g_ref, m_ref, x1_ref, h2_ref, h2t_ref):
        x1 = x_ref[...] + m_ref[R_G1:R_G1 + 1, :] * y_ref[...]
        ms = jnp.mean(x1 * x1, axis=-1, keepdims=True)
        n = x1 * lax.rsqrt(ms + EPS) * g_ref[...]
        x1_ref[...] = x1
        h2 = n * (1.0 + m_ref[R_SC2:R_SC2 + 1, :]) + m_ref[R_SH2:R_SH2 + 1, :]
        h2_ref[...] = h2.astype(BF16)
        h2t_ref[...] = h2.T.astype(BF16)

    t = pl.BlockSpec((TM, d), lambda i: (i, 0))
    tt = pl.BlockSpec((d, TM), lambda i: (0, i))
    return _pcall(
        body, name="res_norm2", grid=(t_len // TM,),
        in_specs=[t, t, pl.BlockSpec((1, d), lambda i: (0, 0)), pl.BlockSpec((8, d), lambda i: (0, 0))],
        out_specs=(t, t, tt), out_shape=(_sds((t_len, d), F32), _sds((t_len, d), BF16), _sds((d, t_len), BF16)),
        compiler_params=_cp(),
    )(x, y, g, modrows)


def _final_fwd_bwd(x1, z, target, fg, modrows):
    t_len, d = x1.shape

    def body(x1_ref, z_ref, t_ref, fg_ref, m_ref, dx2_ref, dz_ref, dzt_ref, acc_ref):
        @pl.when(pl.program_id(0) == 0)
        def _():
            acc_ref[...] = jnp.zeros_like(acc_ref)

        g2 = m_ref[R_G2:R_G2 + 1, :]
        z = z_ref[...]
        x2 = x1_ref[...] + g2 * z
        rstd = lax.rsqrt(jnp.mean(x2 * x2, axis=-1, keepdims=True) + EPS)
        xh = x2 * rstd
        fg = fg_ref[...]
        e = xh * fg - t_ref[...]
        dy = e * (1.0 / d)
        dxh = dy * fg
        dx2 = rstd * (dxh - xh * jnp.mean(dxh * xh, axis=-1, keepdims=True))
        dx2_ref[...] = dx2
        dz = g2 * dx2
        dz_ref[...] = dz.astype(BF16)
        dzt_ref[...] = dz.T.astype(BF16)
        acc_ref[0:1, :] += jnp.sum(dy * xh, axis=0, keepdims=True)
        acc_ref[1:2, :] += jnp.sum(dx2 * z, axis=0, keepdims=True)
        acc_ref[2:3, :] += jnp.sum(e * e, axis=0, keepdims=True)

    t = pl.BlockSpec((TM, d), lambda i: (i, 0))
    return _pcall(
        body, name="final_fwd_bwd", grid=(t_len // TM,),
        in_specs=[t, t, t, pl.BlockSpec((1, d), lambda i: (0, 0)), pl.BlockSpec((8, d), lambda i: (0, 0))],
        out_specs=(t, t, pl.BlockSpec((d, TM), lambda i: (0, i)), pl.BlockSpec((8, d), lambda i: (0, 0))),
        out_shape=(_sds((t_len, d), F32), _sds((t_len, d), BF16), _sds((d, t_len), BF16), _sds((8, d), F32)),
        compiler_params=_cp(),
    )(x1, z, target, fg, modrows)


def _bwd_norm2(dh2, x1, dx2, y, g, modrows):
    t_len, d = x1.shape

    def body(dh2_ref, x1_ref, dx2_ref, y_ref, g_ref, m_ref, dx1_ref, dy_ref, acc_ref):
        @pl.when(pl.program_id(0) == 0)
        def _():
            acc_ref[...] = jnp.zeros_like(acc_ref)

        x1 = x1_ref[...]
        rstd = lax.rsqrt(jnp.mean(x1 * x1, axis=-1, keepdims=True) + EPS)
        xh = x1 * rstd
        gn = g_ref[...]
        dh2 = dh2_ref[...]
        dn = dh2 * (1.0 + m_ref[R_SC2:R_SC2 + 1, :])
        dxh = dn * gn
        dx1 = dx2_ref[...] + rstd * (dxh - xh * jnp.mean(dxh * xh, axis=-1, keepdims=True))
        dx1_ref[...] = dx1
        dy_ref[...] = (m_ref[R_G1:R_G1 + 1, :] * dx1).astype(BF16)
        acc_ref[0:1, :] += jnp.sum(dh2, axis=0, keepdims=True)
        acc_ref[1:2, :] += jnp.sum(dh2 * xh * gn, axis=0, keepdims=True)
        acc_ref[2:3, :] += jnp.sum(dn * xh, axis=0, keepdims=True)
        acc_ref[3:4, :] += jnp.sum(dx1 * y_ref[...], axis=0, keepdims=True)

    t = pl.BlockSpec((TM, d), lambda i: (i, 0))
    return _pcall(
        body, name="bwd_norm2", grid=(t_len // TM,),
        in_specs=[t, t, t, t, pl.BlockSpec((1, d), lambda i: (0, 0)), pl.BlockSpec((8, d), lambda i: (0, 0))],
        out_specs=(t, t, pl.BlockSpec((8, d), lambda i: (0, 0))),
        out_shape=(_sds((t_len, d), F32), _sds((t_len, d), BF16), _sds((8, d), F32)), compiler_params=_cp(),
    )(dh2, x1, dx2, y, g, modrows)


def _mix_bwd(dmix, o_f, o_b, p, h_f, h_b, cb):
    n = o_f.shape[0]

    def body(dm_ref, of_ref, ob_ref, g_ref, gate_ref, hf_ref, hb_ref, do_ref, dg_ref, dgate_ref, dhs_ref):
        is_ctx = pl.program_id(0) < cb

        @pl.when(is_ctx)
        def _():
            for r in (do_ref, dg_ref, dgate_ref, dhs_ref):
                r[...] = jnp.zeros_like(r)

        @pl.when(jnp.logical_not(is_ctx))
        def _():
            o = of_ref[...] + ob_ref[...]
            g = g_ref[...]
            s = _sigmoid(g)
            sg = g * s
            dsg = s * (1.0 + g * (1.0 - s))
            for hh in range(HEADS):
                sl = slice(hh * HEAD_DIM, (hh + 1) * HEAD_DIM)
                oh = o[:, sl]
                yc = oh - jnp.mean(oh, axis=-1, keepdims=True)
                rs = lax.rsqrt(jnp.mean(yc * yc, axis=-1, keepdims=True) + EPS)
                gn = yc * rs
                dret = dm_ref[:, sl]
                dgn = dret * sg[:, sl]
                dg_ref[:, sl] = dret * gn * dsg[:, sl]
                do_ref[:, sl] = rs * (dgn - jnp.mean(dgn, axis=-1, keepdims=True)
                                      - gn * jnp.mean(dgn * gn, axis=-1, keepdims=True))
            dlru = dm_ref[:, RET_W:]
            gate = gate_ref[...]
            dhs_ref[...] = dlru * _gelu(gate)
            dgate_ref[...] = dlru * (hf_ref[...] + hb_ref[...]) * _dgelu(gate)

    row = lambda i: (i, 0)
    t = pl.BlockSpec((TM, RET_W), row)
    return _pcall(
        body, name="mix_bwd", grid=(n // TM,),
        in_specs=[pl.BlockSpec((TM, RET_W + LRU_W), lambda i: (jnp.maximum(i - cb, 0), 0)), t, t,
                  pl.BlockSpec((TM, RET_W), lambda i: (i, COL_G)), pl.BlockSpec((TM, LRU_W), lambda i: (i, COL_GATE)), t, t],
        out_specs=(t, t, t, t), out_shape=tuple(_sds((n, RET_W), F32) for _ in range(4)), compiler_params=_cp(),
    )(dmix, o_f, o_b, p, p, h_f, h_b)


def _bwd_norm1(dhn, ctx, x, dx1, g, modrows, cb, ride=None):
    t_len, d = x.shape
    nb = dhn.shape[0] // TM

    def body(dh_ref, ctx_ref, x_ref, dx1_ref, g_ref, m_ref, gx_ref, acc_ref):
        is_ctx = pl.program_id(0) < cb

        @pl.when(pl.program_id(0) == 0)
        def _():
            acc_ref[...] = jnp.zeros_like(acc_ref)

        xin = jnp.where(is_ctx, ctx_ref[...], x_ref[...])
        sc = jnp.where(is_ctx, m_ref[R_CSC1:R_CSC1 + 1, :], m_ref[R_SC1:R_SC1 + 1, :])
        rstd = lax.rsqrt(jnp.mean(xin * xin, axis=-1, keepdims=True) + EPS)
        xh = xin * rstd
        gn = g_ref[...]
        dh = dh_ref[...]
        dn = dh * (1.0 + sc)
        dxh = dn * gn
        gx_ref[...] = dx1_ref[...] + rstd * (dxh - xh * jnp.mean(dxh * xh, axis=-1, keepdims=True))
        s0 = jnp.sum(dh, axis=0, keepdims=True)
        s1 = jnp.sum(dh * xh * gn, axis=0, keepdims=True)
        acc_ref[4:5, :] += jnp.sum(dn * xh, axis=0, keepdims=True)

        @pl.when(is_ctx)
        def _():
            acc_ref[0:1, :] += s0
            acc_ref[1:2, :] += s1

        @pl.when(jnp.logical_not(is_ctx))
        def _():
            acc_ref[2:3, :] += s0
            acc_ref[3:4, :] += s1

    lat = pl.BlockSpec((TM, d), lambda i: (jnp.maximum(i - cb, 0), 0))
    return _pcall_ride(
        body, ride, (dhn, ctx, x, dx1, g, modrows), name="bwd_norm1", grid=(nb,),
        in_specs=[pl.BlockSpec((TM, d), lambda i: (i, 0)), pl.BlockSpec((TM, d), lambda i: (jnp.minimum(i, cb - 1), 0)),
                  lat, lat, pl.BlockSpec((1, d), lambda i: (0, 0)), pl.BlockSpec((8, d), lambda i: (0, 0))],
        out_specs=(lat, pl.BlockSpec((8, d), lambda i: (0, 0))),
        out_shape=(_sds((t_len, d), F32), _sds((8, d), F32)), compiler_params=_cp())


def _rot(x, cf, ss):
    return x * cf + pltpu.roll(x, HEAD_DIM // 2, 1) * ss


def _decay_exponents(dirn):
    ii = lax.broadcasted_iota(jnp.int32, (CHUNK, CHUNK), 0)
    jj = lax.broadcasted_iota(jnp.int32, (CHUNK, CHUNK), 1)
    rel = ii - jj if dirn == 0 else jj - ii
    pos = ii.astype(F32)
    if dirn == 0:
        cq, cs = pos + 1.0, (CHUNK - 1.0) - pos
    else:
        cq, cs = CHUNK - pos, pos
    return rel, jnp.maximum(rel, 0).astype(F32), cq, cs


def _store_decay(lg_ref, dec):
    for dirn in (0, 1):
        rel, relf, cq, cs = _decay_exponents(dirn)
        for h in range(HEADS):
            lgv = lg_ref[dirn, h]
            dec[dirn, h, 0] = jnp.where(rel >= 0, jnp.exp(lgv * relf), 0.0)
            dec[dirn, h, 1] = jnp.exp(lgv * cq)
            dec[dirn, h, 2] = jnp.exp(lgv * cs)


def _ret_rows(cc, nc, step_of):
    return [lambda s, dirn=dirn: _tile_order(dirn, step_of(s), cc, nc) for dirn in (0, 1)]


def _ret_in_specs(rows):
    specs = []
    for row in rows:
        specs += [pl.BlockSpec((CHUNK, RET_W), lambda s, o=o, row=row: (row(s), o)) for o in (0, 1, 2)]
    return specs


def _ret_fwd(qkv, lg, cc, ride=None):
    n = qkv.shape[0]
    nc = n // CHUNK
    rows = _ret_rows(cc, nc, lambda s: s)

    def body(lg_ref, q0, k0, v0, q1, k1, v1, o0, o1, sp0, sp1, st, dec):
        @pl.when(pl.program_id(0) == 0)
        def _():
            st[...] = jnp.zeros_like(st)
            _store_decay(lg_ref, dec)

        for dirn, (q_ref, k_ref, v_ref, o_ref, sp_ref) in enumerate(((q0, k0, v0, o0, sp0), (q1, k1, v1, o1, sp1))):
            for h in range(HEADS):
                sl = slice(h * HEAD_DIM, (h + 1) * HEAD_DIM)
                q, k, v = q_ref[:, sl], k_ref[:, sl], v_ref[:, sl]
                sp = st[dirn, h]
                sp_ref[h] = sp
                o_ref[:, sl] = _dot(_dot_nt(q, k) * dec[dirn, h, 0], v) + _dot(q * dec[dirn, h, 1], sp)
                st[dirn, h] = jnp.exp(lg_ref[dirn, h] * CHUNK) * sp + _dot_tn(k * dec[dirn, h, 2], v)

    o_specs = [pl.BlockSpec((CHUNK, RET_W), lambda s, row=row: (row(s), 0)) for row in rows]
    state = pl.BlockSpec((None, HEADS, CHUNK, HEAD_DIM), lambda s: (s, 0, 0, 0))
    return _pcall_ride(
        body, ride, (lg,) + (qkv,) * 6, name="ret_fwd", grid=(nc,),
        in_specs=[pl.BlockSpec(memory_space=pltpu.SMEM)] + _ret_in_specs(rows),
        out_specs=(o_specs[0], o_specs[1], state, state),
        out_shape=(_sds((n, RET_W), F32),) * 2 + (_sds((nc, HEADS, CHUNK, HEAD_DIM), F32),) * 2,
        scratch_shapes=[pltpu.VMEM((2, HEADS, CHUNK, HEAD_DIM), F32), pltpu.VMEM((2, HEADS, 3, CHUNK, CHUNK), F32)],
        compiler_params=_cp())


def _ret_bwd(qkv, lg, do, s_prev, cc, ride=None):
    n = qkv.shape[0]
    nc = n // CHUNK
    rows = _ret_rows(cc, nc, lambda s: nc - 1 - s)

    def body(lg_ref, q0, k0, v0, q1, k1, v1, do0, do1, sp0, sp1, dq0, dk0, dv0, dq1, dk1, dv1, dlg_ref, dst, dec):
        @pl.when(pl.program_id(0) == 0)
        def _():
            dst[...] = jnp.zeros_like(dst)
            dlg_ref[...] = jnp.zeros_like(dlg_ref)
            _store_decay(lg_ref, dec)

        for dirn, (q_ref, k_ref, v_ref, do_ref, sp_ref, dq_ref, dk_ref, dv_ref) in enumerate(
                ((q0, k0, v0, do0, sp0, dq0, dk0, dv0), (q1, k1, v1, do1, sp1, dq1, dk1, dv1))):
            _, relf, cq, cs = _decay_exponents(dirn)
            for h in range(HEADS):
                sl = slice(h * HEAD_DIM, (h + 1) * HEAD_DIM)
                q, k, v = q_ref[:, sl], k_ref[:, sl], v_ref[:, sl]
                dov = do_ref[:, sl]
                dm, wq, ws = dec[dirn, h, 0], dec[dirn, h, 1], dec[dirn, h, 2]
                a = _dot_nt(q, k)
                sp = sp_ref[h]
                dsn = dst[dirn, h]
                gc = jnp.exp(lg_ref[dirn, h] * CHUNK)
                g1 = _dot_nt(dov, sp)
                da = _dot_nt(dov, v) * dm
                dq = g1 * wq + _dot(da, k)
                h1 = _dot_nt(v, dsn)
                dk = _dot_tn(da, q) + h1 * ws
                dv_ref[:, sl] = _dot_tn(a * dm, dov) + _dot(k * ws, dsn)
                dst[dirn, h] = gc * dsn + _dot_tn(q * wq, dov)
                term = da * a * relf + q * g1 * (wq * cq) + k * h1 * (ws * cs) + sp * dsn * (CHUNK * gc)
                dlg_ref[dirn * HEADS + h:dirn * HEADS + h + 1, 0:HEAD_DIM] += jnp.sum(term, axis=0, keepdims=True)
                dq_ref[:, sl] = dq
                dk_ref[:, sl] = dk

    wide = [pl.BlockSpec((CHUNK, RET_W), lambda s, row=row: (row(s), 0)) for row in rows]
    state = pl.BlockSpec((None, HEADS, CHUNK, HEAD_DIM), lambda s: (nc - 1 - s, 0, 0, 0))
    return _pcall_ride(
        body, ride, (lg,) + (qkv,) * 6 + (do, do, s_prev[0], s_prev[1]), name="ret_bwd", grid=(nc,),
        in_specs=[pl.BlockSpec(memory_space=pltpu.SMEM)] + _ret_in_specs(rows) + wide + [state, state],
        out_specs=(wide[0],) * 3 + (wide[1],) * 3 + (pl.BlockSpec((2 * HEADS, 8 * HEAD_DIM), lambda s: (0, 0)),),
        out_shape=(_sds((n, RET_W), F32),) * 6 + (_sds((2 * HEADS, 8 * HEAD_DIM), F32),),
        scratch_shapes=[pltpu.VMEM((2, HEADS, CHUNK, HEAD_DIM), F32), pltpu.VMEM((2, HEADS, 3, CHUNK, CHUNK), F32)],
        compiler_params=_cp())


def _shift_rows(cur, prev8, next8, k, seg_start, seg_end):
    tm = cur.shape[0]
    rows = _rows_iota(cur.shape)
    if k < 0:
        out = pltpu.roll(cur, -k, 0)
        for j in range(-k):
            halo = jnp.where(seg_start, 0.0, prev8[SUB + k + j:SUB + k + j + 1, :])
            out = jnp.where(rows == j, halo, out)
    else:
        out = pltpu.roll(cur, tm - k, 0)
        for j in range(k):
            halo = jnp.where(seg_end, 0.0, next8[j:j + 1, :])
            out = jnp.where(rows == tm - k + j, halo, out)
    return out


def _seg_flags(t, cb, nb):
    return jnp.logical_or(t == 0, t == cb), jnp.logical_or(t == cb - 1, t == nb - 1)


def _halo_specs(tile_of, n_rows, col):
    per = TM // SUB
    return [pl.BlockSpec((TM, LRU_W), lambda s: (tile_of(s), col)),
            pl.BlockSpec((SUB, LRU_W), lambda s: (jnp.maximum(tile_of(s) * per - 1, 0), col)),
            pl.BlockSpec((SUB, LRU_W), lambda s: (jnp.minimum((tile_of(s) + 1) * per, n_rows // SUB - 1), col))]


def _lru_gates(xr, prev8, next8, seg_start, seg_end, cw_ref, cb_ref, wg_ref, bg_ref, sp_ref):
    xm1 = _shift_rows(xr, prev8, next8, -1, seg_start, seg_end)
    xp1 = _shift_rows(xr, prev8, next8, 1, seg_start, seg_end)
    xp2 = _shift_rows(xr, prev8, next8, 2, seg_start, seg_end)
    xc = cb_ref[...] + xm1 * cw_ref[0:1, :] + xr * cw_ref[1:2, :] + xp1 * cw_ref[2:3, :] + xp2 * cw_ref[3:4, :]
    pre = _dot(xc, wg_ref[...]) + bg_ref[...]
    r = _sigmoid(pre[:, :LRU_W])
    i = _sigmoid(pre[:, LRU_W:])
    la = (-LRU_C) * r * sp_ref[...]
    a = jnp.exp(la)
    th = jnp.tanh(la)
    sq = jnp.sqrt(-2.0 * th / (1.0 - th))
    return xc, r, i, a, sq


def _scan_tile(a, b, ascending, a_sc, b_sc, carry, out_ref):
    tm, w = a.shape
    nsub = tm // SUB
    a = a.reshape(nsub, SUB, w)
    b = b.reshape(nsub, SUB, w)
    r8 = lax.broadcasted_iota(jnp.int32, a.shape, 1)
    for k in (1, 2, 4):
        if ascending:
            m = r8 >= k
            a_s, b_s = pltpu.roll(a, k, 1), pltpu.roll(b, k, 1)
        else:
            m = r8 < SUB - k
            a_s, b_s = pltpu.roll(a, SUB - k, 1), pltpu.roll(b, SUB - k, 1)
        b = a * jnp.where(m, b_s, 0.0) + b
        a = a * jnp.where(m, a_s, 1.0)
    a_sc[...] = a.reshape(tm, w)
    b_sc[...] = b.reshape(tm, w)

    def step(j, c):
        off = pl.multiple_of((j if ascending else nsub - 1 - j) * SUB, SUB)
        hb = a_sc[pl.ds(off, SUB), :] * c + b_sc[pl.ds(off, SUB), :]
        out_ref[pl.ds(off, SUB), :] = hb
        last = hb[SUB - 1:SUB, :] if ascending else hb[0:1, :]
        return jnp.broadcast_to(last, c.shape)

    carry[...] = lax.fori_loop(0, nsub, step, carry[...])


def _lru_fwd(p, wg, bg, sp, cw, cbias, dirn, cb, ride=None):
    n = p.shape[0]
    nb = n // TM
    tile_of = lambda s: _tile_order(dirn, s, cb, nb)

    def body(x_ref, xp_ref, xn_ref, wg_ref, bg_ref, sp_ref, cw_ref, cb_ref, h_ref, cin_ref, carry, a_sc, b_sc):
        s = pl.program_id(0)

        @pl.when(s == 0)
        def _():
            carry[...] = jnp.zeros_like(carry)

        seg_start, seg_end = _seg_flags(tile_of(s), cb, nb)
        xc, r, i, a, sq = _lru_gates(x_ref[...], xp_ref[...], xn_ref[...], seg_start, seg_end,
                                     cw_ref, cb_ref, wg_ref, bg_ref, sp_ref)
        cin_ref[...] = carry[...]
        _scan_tile(a, sq * (i * xc), dirn == 0, a_sc, b_sc, carry, h_ref)

    full = lambda shape: pl.BlockSpec(shape, lambda s: (0,) * len(shape))
    return _pcall_ride(
        body, ride, (p, p, p, wg, bg, sp, cw, cbias), name=f"lru_fwd{dirn}", grid=(nb,),
        in_specs=_halo_specs(tile_of, n, COL_XR) + [full((LRU_W, 2 * LRU_W)), full((1, 2 * LRU_W)), full((1, LRU_W)),
                                               full((4, LRU_W)), full((1, LRU_W))],
        out_specs=(pl.BlockSpec((TM, LRU_W), lambda s: (tile_of(s), 0)),
                   pl.BlockSpec((None, SUB, LRU_W), lambda s: (tile_of(s), 0, 0))),
        out_shape=(_sds((n, LRU_W), F32), _sds((nb, SUB, LRU_W), F32)),
        scratch_shapes=[pltpu.VMEM((SUB, LRU_W), F32), pltpu.VMEM((TM, LRU_W), F32), pltpu.VMEM((TM, LRU_W), F32)],
        compiler_params=_cp())


def _lru_bwd(p, wg, bg, sp, cw, cbias, h, cin, dhs, dirn, cb, ride=None):
    n = p.shape[0]
    nb = n // TM
    tile_of = lambda s: _tile_order(dirn, nb - 1 - s, cb, nb)

    def body(x_ref, xp_ref, xn_ref, wg_ref, bg_ref, sp_ref, cw_ref, cb_ref, h_ref, cin_ref, dhs_ref,
             dxc_ref, dwd_ref, acc_ref, carry, a_sc, b_sc, mu_sc, dwg_ref):
        s = pl.program_id(0)

        @pl.when(s == 0)
        def _():
            carry[...] = jnp.zeros_like(carry)
            dwg_ref[...] = jnp.zeros_like(dwg_ref)
            acc_ref[...] = jnp.zeros_like(acc_ref)

        seg_start, seg_end = _seg_flags(tile_of(s), cb, nb)
        xc, r, i, a, sq = _lru_gates(x_ref[...], xp_ref[...], xn_ref[...], seg_start, seg_end,
                                     cw_ref, cb_ref, wg_ref, bg_ref, sp_ref)
        rows = _rows_iota(a.shape)
        hv = h_ref[...]
        dh = dhs_ref[...]
        mu_next = carry[0:1, :]
        _scan_tile(a, a * dh, dirn == 1, a_sc, b_sc, carry, mu_sc)
        mu = mu_sc[...]
        if dirn == 0:
            hprev = jnp.where(rows == 0, cin_ref[0:1, :], pltpu.roll(hv, 1, 0))
            lam = dh + jnp.where(rows == TM - 1, mu_next, pltpu.roll(mu, TM - 1, 0))
        else:
            hprev = jnp.where(rows == TM - 1, cin_ref[0:1, :], pltpu.roll(hv, TM - 1, 0))
            lam = dh + jnp.where(rows == 0, mu_next, pltpu.roll(mu, 1, 0))
        ds = lam * (i * xc)
        di = lam * (sq * xc)
        dla = lam * hprev * a - ds * (a * a) / jnp.maximum(sq, 1e-20)
        dpr = dla * ((-LRU_C) * sp_ref[...]) * r * (1.0 - r)
        dpi = di * i * (1.0 - i)
        dpre = jnp.concatenate([dpr, dpi], axis=1)
        dxc_ref[...] = lam * (sq * i) + _dot_nt(dpre, wg_ref[...])
        dwg_ref[...] += _dot_tn(xc, dpre)
        acc_ref[0:1, :] += jnp.sum(dpre, axis=0, keepdims=True)
        acc_ref[1:2, 0:LRU_W] += jnp.sum(dla * ((-LRU_C) * r), axis=0, keepdims=True)

        @pl.when(s == nb - 1)
        def _():
            low = lax.broadcasted_iota(jnp.int32, (LRU_BD, 2 * LRU_BD), 1) < LRU_BD
            for half in (0, LRU_W):
                for m in range(LRU_BLOCKS // 2):
                    lanes = slice(half + 2 * LRU_BD * m, half + 2 * LRU_BD * (m + 1))
                    even = dwg_ref[2 * m * LRU_BD:(2 * m + 1) * LRU_BD, lanes]
                    odd = dwg_ref[(2 * m + 1) * LRU_BD:(2 * m + 2) * LRU_BD, lanes]
                    dwd_ref[:, lanes] = jnp.where(low, even, odd)

    full = lambda shape: pl.BlockSpec(shape, lambda s: (0,) * len(shape))
    tile = pl.BlockSpec((TM, LRU_W), lambda s: (tile_of(s), 0))
    return _pcall_ride(
        body, ride, (p, p, p, wg, bg, sp, cw, cbias, h, cin, dhs), name=f"lru_bwd{dirn}", grid=(nb,),
        in_specs=_halo_specs(tile_of, n, COL_XR) + [full((LRU_W, 2 * LRU_W)), full((1, 2 * LRU_W)), full((1, LRU_W)),
                                               full((4, LRU_W)), full((1, LRU_W)), tile,
                                               pl.BlockSpec((None, SUB, LRU_W), lambda s: (tile_of(s), 0, 0)), tile],
        out_specs=(tile, full((LRU_BD, 2 * LRU_W)), full((8, 2 * LRU_W))),
        out_shape=(_sds((n, LRU_W), F32), _sds((LRU_BD, 2 * LRU_W), F32), _sds((8, 2 * LRU_W), F32)),
        scratch_shapes=[pltpu.VMEM((SUB, LRU_W), F32)] + [pltpu.VMEM((TM, LRU_W), F32)] * 3
        + [pltpu.VMEM((LRU_W, 2 * LRU_W), F32)],
        compiler_params=_cp())


def _assemble_dp(dqs, dks, dvs, dg, dgate, dxcs, p, cw, cosf, sins, cb, ride=None):
    n = p.shape[0]
    nb = n // TM
    tile_of = lambda s: s

    def body(dqf, dqb, dkf, dkb, dvf, dvb, dg_ref, dgate_ref, cf, pf, nf, cb_, pb, nb_, x_ref, xp_ref, xn_ref,
             cw_ref, cos_ref, sin_ref, dp_ref, acc_ref):
        s = pl.program_id(0)

        @pl.when(s == 0)
        def _():
            acc_ref[...] = jnp.zeros_like(acc_ref)

        seg_start, seg_end = _seg_flags(s, cb, nb)
        dq = dqf[...] + dqb[...]
        dk = dkf[...] + dkb[...]
        cosv, sinv = cos_ref[...], sin_ref[...]
        for h in range(HEADS):
            sl = slice(h * HEAD_DIM, (h + 1) * HEAD_DIM)
            sk = slice(RET_W + h * HEAD_DIM, RET_W + (h + 1) * HEAD_DIM)
            dp_ref[:, sl] = (dq[:, sl] * cosv + pltpu.roll(dq[:, sl] * sinv, HEAD_DIM // 2, 1)).astype(BF16)
            dp_ref[:, sk] = ((dk[:, sl] * cosv + pltpu.roll(dk[:, sl] * sinv, HEAD_DIM // 2, 1)) * K_SCALE).astype(BF16)
        dp_ref[:, 2 * RET_W:3 * RET_W] = (dvf[...] + dvb[...]).astype(BF16)
        dp_ref[:, 3 * RET_W:4 * RET_W] = dg_ref[...].astype(BF16)
        dxc = cf[...] + cb_[...]
        dprev = pf[...] + pb[...]
        dnext = nf[...] + nb_[...]
        dxr = (_shift_rows(dxc, dprev, dnext, 1, seg_start, seg_end) * cw_ref[0:1, :] + dxc * cw_ref[1:2, :]
               + _shift_rows(dxc, dprev, dnext, -1, seg_start, seg_end) * cw_ref[2:3, :]
               + _shift_rows(dxc, dprev, dnext, -2, seg_start, seg_end) * cw_ref[3:4, :])
        dp_ref[:, 4 * RET_W:4 * RET_W + LRU_W] = dxr.astype(BF16)
        dp_ref[:, 4 * RET_W + LRU_W:] = dgate_ref[...].astype(BF16)
        xr, xp, xn = x_ref[...], xp_ref[...], xn_ref[...]
        for j, k in enumerate((-1, 0, 1, 2)):
            xs = xr if k == 0 else _shift_rows(xr, xp, xn, k, seg_start, seg_end)
            acc_ref[j:j + 1, 0:LRU_W] += jnp.sum(dxc * xs, axis=0, keepdims=True)
        acc_ref[4:5, 0:LRU_W] += jnp.sum(dxc, axis=0, keepdims=True)

    t = pl.BlockSpec((TM, RET_W), lambda s: (s, 0))
    args = (dqs[0], dqs[1], dks[0], dks[1], dvs[0], dvs[1], dg, dgate, dxcs[0], dxcs[0], dxcs[0], dxcs[1], dxcs[1], dxcs[1],
            p, p, p, cw, cosf, sins)
    tab = pl.BlockSpec((TM, HEAD_DIM), lambda s: (s, 0))
    return _pcall_ride(
        body, ride, args, name="assemble_dp", grid=(nb,),
        in_specs=[t] * 8 + _halo_specs(tile_of, n, 0) * 2 + _halo_specs(tile_of, n, COL_XR)
        + [pl.BlockSpec((4, LRU_W), lambda s: (0, 0)), tab, tab],
        out_specs=(pl.BlockSpec((TM, 4 * RET_W + 2 * LRU_W), lambda s: (s, 0)), pl.BlockSpec((8, 2 * LRU_W), lambda s: (0, 0))),
        out_shape=(_sds((n, 4 * RET_W + 2 * LRU_W), BF16), _sds((8, 2 * LRU_W), F32)), compiler_params=_cp())


def _adamw(g, w, m, v):
    nm = ADAM_B1 * m + (1.0 - ADAM_B1) * g
    nv = ADAM_B2 * v + (1.0 - ADAM_B2) * (g * g)
    m_hat = nm / (1.0 - ADAM_B1 ** ADAM_STEP)
    v_hat = nv / (1.0 - ADAM_B2 ** ADAM_STEP)
    return (-ADAM_LR) * (m_hat / (jnp.sqrt(v_hat) + ADAM_EPS) + ADAM_WD * w), nm, nv


def _adam_many(items, name):
    n = len(items)

    def body(*refs):
        for i in range(n):
            g, w, m, v = (r[...] for r in refs[4 * i:4 * i + 4])
            for o_ref, val in zip(refs[4 * n + 3 * i:4 * n + 3 * i + 3], _adamw(g, w, m, v)):
                o_ref[...] = val

    out_shape = tuple(_sds(it[1].shape, F32) for it in items for _ in range(3))
    res = _pcall(body, name=name, out_shape=out_shape, compiler_params=_cp())(*[a for it in items for a in it])
    return [tuple(res[3 * i:3 * i + 3]) for i in range(n)]


def _sum_adam(parts_list, w, m, v, name):
    nparts, _, c = parts_list[0].shape
    r = w.shape[0]
    tr = min(parts_list[0].shape[1], 128)
    starts, o = [], 0
    for pa in parts_list:
        starts.append(o)
        o += pa.shape[1] // tr
    nseg = len(parts_list)

    def body(*refs):
        p_refs = refs[:nseg]
        w_ref, m_ref, v_ref, g_ref, d_ref, nm_ref, nv_ref = refs[nseg:]
        i = pl.program_id(0)
        for s, p_ref in enumerate(p_refs):
            end = starts[s + 1] if s + 1 < nseg else r // tr

            @pl.when(jnp.logical_and(i >= starts[s], i < end))
            def _():
                g = p_ref[0].astype(F32)
                for j in range(1, nparts):
                    g = g + p_ref[j].astype(F32)
                g_ref[...] = g
                d_ref[...], nm_ref[...], nv_ref[...] = _adamw(g, w_ref[...], m_ref[...], v_ref[...])

    def seg_spec(s):
        last = parts_list[s].shape[1] // tr - 1
        return pl.BlockSpec((nparts, tr, c), lambda i: (0, jnp.clip(i - starts[s], 0, last), 0))

    t = pl.BlockSpec((tr, c), lambda i: (i, 0))
    return _pcall(
        body, name=name, grid=(r // tr,),
        in_specs=[seg_spec(s) for s in range(nseg)] + [t, t, t],
        out_specs=(t, t, t, t), out_shape=(_sds((r, c), F32),) * 4, compiler_params=_cp(),
    )(*parts_list, w, m, v)


def _sum_parts(parts, name):
    nparts, r, c = parts.shape

    def body(p_ref, o_ref):
        g = p_ref[0]
        for j in range(1, nparts):
            g = g + p_ref[j]
        o_ref[...] = g

    return _pcall(body, name=name, out_shape=_sds((r, c), parts.dtype), compiler_params=_cp())(parts)


def _rot_tables(l_len, t_len):
    rows = t_len // GRID_W
    row = jnp.repeat(jnp.arange(rows, dtype=F32), GRID_W)
    col = jnp.tile(jnp.arange(GRID_W, dtype=F32), rows)
    n_freq = HEAD_DIM // 4
    inv = ROPE_BASE ** (-jnp.arange(n_freq, dtype=F32) / n_freq)
    ang = jnp.concatenate([row[:, None] * inv, col[:, None] * inv], axis=-1)
    cos, sin = jnp.cos(ang), jnp.sin(ang)
    cosf = jnp.concatenate([jnp.ones((l_len, HEAD_DIM), F32), jnp.concatenate([cos, cos], axis=-1)], axis=0)
    sins = jnp.concatenate([jnp.zeros((l_len, HEAD_DIM), F32), jnp.concatenate([-sin, sin], axis=-1)], axis=0)
    return cosf, sins


def _block_diag(w):
    eye = jnp.eye(LRU_BLOCKS, dtype=w.dtype)
    return (w[:, :, None, :] * eye[:, None, :, None]).reshape(LRU_W, LRU_W)


def _blocks_from_lanes(dwd_half):
    return dwd_half.reshape(LRU_BD, LRU_BLOCKS, LRU_BD).transpose(1, 0, 2)


def _silu(x):
    return x * jax.nn.sigmoid(x)


def kernel(x, c, ctx, c_ctx, w_ada, b_ada, norm1_g, norm2_g, w_in, ret_decay, conv_w, conv_b, lru_wa, lru_ba, lru_wx, lru_bx, lru_lambda, w_out, w_mlp1, w_mlp2, final_g, loss_target, m_c_ctx, m_w_ada, m_b_ada, m_norm1_g, m_norm2_g, m_w_in, m_ret_decay, m_conv_w, m_conv_b, m_lru_wa, m_lru_ba, m_lru_wx, m_lru_bx, m_lru_lambda, m_w_out, m_w_mlp1, m_w_mlp2, m_final_g, v_c_ctx, v_w_ada, v_b_ada, v_norm1_g, v_norm2_g, v_w_in, v_ret_decay, v_conv_w, v_conv_b, v_lru_wa, v_lru_ba, v_lru_wx, v_lru_bx, v_lru_lambda, v_w_out, v_w_mlp1, v_w_mlp2, v_final_g):
    t_len, d = x.shape[1], x.shape[2]
    l_len = ctx.shape[1]
    cb, cc = l_len // TM, l_len // CHUNK
    me = 4 * lax.axis_index("x") + 2 * lax.axis_index("y") + lax.axis_index("c")
    x2d, ctx2d, tgt2d = x[0], ctx[0], loss_target[0]
    ada_cols = w_ada.shape[2]
    wa2d = w_ada[0]

    sc_loc = conv_w.shape[2]
    pack_a = jnp.zeros((8, d), F32)
    pack_a = pack_a.at[0].set(_silu(c[0]))
    pack_a = pack_a.at[1, :4 * sc_loc].set(conv_w[0].reshape(-1))
    pack_a = pack_a.at[2, :2 * sc_loc].set(lru_ba[0].reshape(-1))
    pack_a = pack_a.at[3, :2 * sc_loc].set(lru_bx[0].reshape(-1))
    pack_a = pack_a.at[4, :2 * sc_loc].set(lru_lambda[0].reshape(-1))
    all_a = _all_gather(pack_a, "gather_small_in")
    s16 = jnp.zeros((16, d), F32).at[0:8].set(all_a[:, 0, :]).at[8].set(_silu(c_ctx))

    def unshard(row, k):
        return all_a[:, row, :k * sc_loc].reshape(N_DEV, k, sc_loc).transpose(1, 0, 2).reshape(k, N_DEV * sc_loc)

    conv_w_full = unshard(1, 4)
    ba_full, bx_full, lam_full = unshard(2, 2), unshard(3, 2), unshard(4, 2)

    b_cols = lax.dynamic_slice(b_ada, (0, me * ada_cols), (1, ada_cols))
    mod_parts = _all_gather(_mod_part(s16, wa2d, b_cols), "gather_mod")
    mod_all = mod_parts.transpose(1, 0, 2).reshape(16, N_DEV * ada_cols)
    mod_me = lax.dynamic_slice(mod_all, (me, 0), (1, 6 * d)).reshape(6, d)
    mod_c = mod_all[8].reshape(6, d)
    modrows = jnp.concatenate([mod_c[0:2], mod_me], axis=0)

    lg = jax.nn.log_sigmoid(ret_decay[0])
    sp = jax.nn.softplus(-lam_full)
    wg = [jnp.concatenate([_block_diag(lru_wa[0, dd]), _block_diag(lru_wx[0, dd])], axis=1).astype(BF16) for dd in (0, 1)]
    bg = [jnp.concatenate([ba_full[dd], bx_full[dd]])[None, :] for dd in (0, 1)]
    cosf, sins = _rot_tables(l_len, t_len)

    (hn, hnt), win_g = _norm1_fwd(ctx2d, x2d, norm1_g, modrows, cb, ride=("gather", w_in[0].astype(BF16)))
    (qkv, p), wout_g = _mm_in(hn, win_g, cosf, sins, ride=("gather", w_out[0].astype(BF16)))
    wout_g = wout_g.reshape(1, d, d)
    (o0, o1, sp0, sp1), w1_g = _ret_fwd(qkv, lg, cc, ride=("gather", w_mlp1[0].astype(BF16)))
    o, s_prev = [o0, o1], [sp0, sp1]
    (h0, cin0), _ = _lru_fwd(p, wg[0], bg[0], sp[0:1], conv_w_full, conv_b, 0, cb)
    (h1, cin1), _ = _lru_fwd(p, wg[1], bg[1], sp[1:2], conv_w_full, conv_b, 1, cb)
    h, cin = [h0, h1], [cin0, cin1]
    mix = _mix_fwd(o[0], o[1], p, h[0], h[1], cb, t_len)
    y = _mm_nn(mix, wout_g, F32, "mm_out")
    x1, h2, h2t = _res_norm2(x2d, y, norm2_g, modrows)
    r, w2_g = _mm_nn(h2, w1_g, BF16, "mm_mlp1", relu_out=True, ride=("gather", w_mlp2[0].astype(BF16)))
    w2_g = w2_g.reshape(1, 4 * d, d)
    z = _mm_nn(r, w2_g, F32, "mm_mlp2", square_lhs=True)
    dx2, dz, dzt, facc = _final_fwd_bwd(x1, z, tgt2d, final_g[None, :], modrows)

    du = _mm_nt(dz, w2_g, BF16, "mm_da2", relu_mul=r)
    gw2 = _mm_wgrad(dzt, r, "mm_dw2", w2_g.shape[1] // N_DEV, BF16, transpose_out=True, square_rhs=True)
    gw1_lo, gw1_hi = _mm_wgrad(h2t, du, "mm_dw1", w1_g.shape[2], BF16, halves=True)
    dh2 = _mm_nt(du, w1_g, F32, "mm_dh2")
    dx1, dy, n2acc = _bwd_norm2(dh2, x1, dx2, y, norm2_g, modrows)
    dmix = _mm_nt(dy, wout_g, F32, "mm_dmix")
    gwo = _mm_tn(mix, dy, "mm_dwout", False, d, BF16, 512).reshape(N_DEV, -1, d)
    do, dg, dgate, dhs = _mix_bwd(dmix, o[0], o[1], p, h[0], h[1], cb)
    dxcs, dwgs, laccs, rides, gw1_all = [], [], [], [("a2a", gw1_lo), ("a2a", gw1_hi)], []
    (dq0, dk0, dv0, dq1, dk1, dv1, dlg_lanes), gw2_all = _ret_bwd(qkv, lg, do, s_prev, cc, ride=("a2a", gw2))
    dqs, dks, dvs = [dq0, dq1], [dk0, dk1], [dv0, dv1]
    for dd in (0, 1):
        (dxc_, dwg_, lacc_), got_ = _lru_bwd(p, wg[dd], bg[dd], sp[dd:dd + 1], conv_w_full, conv_b, h[dd], cin[dd], dhs, dd, cb,
                                             ride=rides[dd])
        dxcs.append(dxc_); dwgs.append(dwg_); laccs.append(lacc_); gw1_all.append(got_)
    (dp, cacc), gwo_all = _assemble_dp(dqs, dks, dvs, dg, dgate, dxcs, p, conv_w_full, cosf, sins, cb, ride=("a2a", gwo))
    pack_b = jnp.concatenate([n2acc, facc, cacc, laccs[0], laccs[1], dlg_lanes, dwgs[0], dwgs[1]], axis=0)
    (gwi_lo, gwi_hi), all_b = _mm_wgrad(hnt, dp, "mm_dwin", win_g.shape[2], BF16, halves=True,
                                        ride=("gather", pack_b))
    dhn, gwi_lo_all = _mm_nt(dp, win_g, F32, "mm_dhn", ride=("a2a", gwi_lo))
    (grad_x, n1acc), gwi_hi_all = _bwd_norm1(dhn, ctx2d, x2d, dx1, norm1_g, modrows, cb, ride=("a2a", gwi_hi))
    all_n1 = _all_gather(n1acc, "gather_norm1_grads")
    tot = _sum_parts(all_b, "sum_small_grads")
    t_n1 = _sum_parts(all_n1, "sum_norm1_grads")
    t_n2, t_f, t_conv, t_dlg = tot[0:8], tot[8:16], tot[16:24, :LRU_W], tot[40:48, :HEAD_DIM]
    t_l = [tot[24:32], tot[32:40]]
    t_dwd = [tot[48:48 + LRU_BD], tot[48 + LRU_BD:48 + 2 * LRU_BD]]
    loss = (0.5 / d) * jnp.sum(t_f[2])
    t_wa = jnp.stack([_blocks_from_lanes(t_dwd[dd][:, :LRU_W]) for dd in (0, 1)])
    t_wx = jnp.stack([_blocks_from_lanes(t_dwd[dd][:, LRU_W:]) for dd in (0, 1)])
    t_ba = jnp.stack([t_l[dd][0, :LRU_W] for dd in (0, 1)])
    t_bx = jnp.stack([t_l[dd][0, LRU_W:] for dd in (0, 1)])
    t_sp = jnp.stack([t_l[dd][1, :LRU_W] for dd in (0, 1)])
    dm_rows = jnp.stack([all_n1[:, 2, :], all_n1[:, 3, :], all_b[:, 3, :], all_b[:, 0, :], all_b[:, 1, :], all_b[:, 9, :]],
                        axis=1).reshape(N_DEV, 6 * d)
    dm_c = jnp.concatenate([t_n1[0], t_n1[1], jnp.zeros((4 * d,), F32)])
    dm16 = jnp.zeros((16, 6 * d), F32).at[0:8].set(dm_rows).at[8].set(dm_c)
    g_b_ada = jnp.sum(dm16, axis=0)[None, :]
    dm_cols = lax.dynamic_slice(dm16, (0, me * ada_cols), (16, ada_cols))
    g_w_ada, ds16 = _ada_bwd(s16, dm_cols, wa2d)
    ds_all = _all_gather(ds16[8:16], "gather_dsilu")
    dsilu_cc = _sum_parts(ds_all, "sum_dsilu")[0]
    sg_cc = jax.nn.sigmoid(c_ctx)
    g_c_ctx = dsilu_cc * (sg_cc * (1.0 + c_ctx * (1.0 - sg_cc)))

    g_ret_decay = jnp.sum(t_dlg, axis=-1).reshape(2, HEADS) * jax.nn.sigmoid(-ret_decay[0])
    g_lambda_full = -t_sp * jax.nn.sigmoid(-lam_full)

    def my_cols(full):
        return lax.dynamic_slice(full, (0, me * sc_loc), (full.shape[0], sc_loc))

    small_g = dict(
        c_ctx=g_c_ctx[None], b_ada=g_b_ada, norm1_g=t_n1[4:5], norm2_g=t_n2[2:3], ret_decay=g_ret_decay,
        conv_w=my_cols(t_conv[0:4]), conv_b=t_conv[4:5], lru_wa=t_wa.reshape(-1, LRU_BD), lru_ba=my_cols(t_ba),
        lru_wx=t_wx.reshape(-1, LRU_BD), lru_bx=my_cols(t_bx), lru_lambda=my_cols(g_lambda_full), final_g=t_f[0:1])
    small = dict(
        c_ctx=(c_ctx, m_c_ctx, v_c_ctx), b_ada=(b_ada, m_b_ada, v_b_ada), norm1_g=(norm1_g, m_norm1_g, v_norm1_g),
        norm2_g=(norm2_g, m_norm2_g, v_norm2_g), ret_decay=(ret_decay, m_ret_decay, v_ret_decay),
        conv_w=(conv_w, m_conv_w, v_conv_w), conv_b=(conv_b, m_conv_b, v_conv_b), lru_wa=(lru_wa, m_lru_wa, v_lru_wa),
        lru_ba=(lru_ba, m_lru_ba, v_lru_ba), lru_wx=(lru_wx, m_lru_wx, v_lru_wx), lru_bx=(lru_bx, m_lru_bx, v_lru_bx),
        lru_lambda=(lru_lambda, m_lru_lambda, v_lru_lambda), final_g=(final_g, m_final_g, v_final_g))
    names = list(small)
    items = [(small_g[k],) + tuple(a.reshape(small_g[k].shape) for a in small[k]) for k in names]
    res = {}
    for k, it, (d_, m_, v_) in zip(names, items, _adam_many(items, "adam_small")):
        shape = small[k][0].shape
        res[k] = tuple(a.reshape(shape) for a in (it[0], d_, m_, v_))

    def big(parts, w, m, v, name):
        out = _sum_adam(parts, w[0], m[0], v[0], name)
        return tuple(a[None] for a in out)

    res["w_ada"] = big([g_w_ada[None]], w_ada, m_w_ada, v_w_ada, "adam_w_ada")
    res["w_in"] = big([gwi_lo_all, gwi_hi_all], w_in, m_w_in, v_w_in, "adam_w_in")
    res["w_out"] = big([gwo_all], w_out, m_w_out, v_w_out, "adam_w_out")
    res["w_mlp1"] = big(gw1_all, w_mlp1, m_w_mlp1, v_w_mlp1, "adam_w_mlp1")
    res["w_mlp2"] = big([gw2_all], w_mlp2, m_w_mlp2, v_w_mlp2, "adam_w_mlp2")

    order = ["c_ctx", "w_ada", "b_ada", "norm1_g", "norm2_g", "w_in", "ret_decay", "conv_w", "conv_b", "lru_wa", "lru_ba",
             "lru_wx", "lru_bx", "lru_lambda", "w_out", "w_mlp1", "w_mlp2", "final_g"]
    outs = [loss, grad_x[None]]
    for j in range(4):
        outs += [res[k][j] for k in order]
    return tuple(outs)
```

```python
import jax
import jax.numpy as jnp
from jax import lax
from jax.experimental import pallas as pl
from jax.experimental.pallas import tpu as pltpu

F32 = jnp.float32
BF16 = jnp.bfloat16
AXES = ("x", "y", "c")
N_DEV = 8
MESH = pl.DeviceIdType.MESH

HEADS = 4
HEAD_DIM = 128
CHUNK = 128
RET_W = HEADS * HEAD_DIM
LRU_W = 512
LRU_BLOCKS = 8
LRU_BD = LRU_W // LRU_BLOCKS
LRU_C = 8.0
EPS = 1e-6
K_SCALE = HEAD_DIM ** -0.5
ROPE_BASE = 10000.0
GRID_W = 64
TM = 256
TL = 512
SUB = 8

ADAM_LR = 0.001
ADAM_B1 = 0.9
ADAM_B2 = 0.999
ADAM_EPS = 1e-08
ADAM_WD = 0.01
ADAM_STEP = 10

COL_G, COL_XR, COL_GATE = 0, 1, 2

R_CSH1, R_CSC1, R_SH1, R_SC1, R_G1, R_SH2, R_SC2, R_G2 = range(8)


def _pcall(body, **kw):
    return pl.pallas_call(body, **kw)


def _cp(vmem_mb=48):
    return pltpu.CompilerParams(vmem_limit_bytes=vmem_mb << 20)


def _sds(shape, dtype):
    return jax.ShapeDtypeStruct(shape, dtype)


def _dot(a, b):
    return jnp.dot(a.astype(BF16), b.astype(BF16), preferred_element_type=F32)


def _dot_nt(a, b):
    return lax.dot_general(a.astype(BF16), b.astype(BF16), (((1,), (1,)), ((), ())), preferred_element_type=F32)


def _dot_tn(a, b):
    return lax.dot_general(a.astype(BF16), b.astype(BF16), (((0,), (0,)), ((), ())), preferred_element_type=F32)


def _sigmoid(x):
    return jax.nn.sigmoid(x)


def _gelu(x):
    return 0.5 * x * (1.0 + jnp.tanh(0.7978845608028654 * (x + 0.044715 * x * x * x)))


def _dgelu(x):
    t = jnp.tanh(0.7978845608028654 * (x + 0.044715 * x * x * x))
    return 0.5 * (1.0 + t) + 0.5 * x * (1.0 - t * t) * 0.7978845608028654 * (1.0 + 3.0 * 0.044715 * x * x)


def _rows_iota(shape):
    return lax.broadcasted_iota(jnp.int32, shape, 0)


def _tile_order(dirn, s, cb, nb):
    if dirn == 0:
        return s
    return jnp.where(s < cb, cb - 1 - s, nb - 1 - (s - cb))


_SEMS = [pltpu.SemaphoreType.DMA((7,)), pltpu.SemaphoreType.DMA((7,)), pltpu.SemaphoreType.DMA(())]
_ANY = pl.BlockSpec(memory_space=pl.ANY)


def _gather_copies(x_ref, out_ref, send_sems, recv_sems, local_sem):
    mx, my, mc = lax.axis_index("x"), lax.axis_index("y"), lax.axis_index("c")
    me, sibling = (mx, my, mc), (mx, my, 1 - mc)
    chips = [(1 - mx, my), (mx, 1 - my), (1 - mx, 1 - my)]

    def slot(px, py, pc):
        return out_ref.at[4 * px + 2 * py + pc]

    def copy(k, block, to, src=None):
        return pltpu.make_async_remote_copy(
            src_ref=slot(*block) if src is None else src, dst_ref=slot(*block),
            send_sem=send_sems.at[k], recv_sem=recv_sems.at[k], device_id=to, device_id_type=MESH)

    mine = pltpu.make_async_copy(x_ref, slot(*me), local_sem)
    first = [copy(0, me, sibling, src=x_ref)] + [copy(1 + j, me, (*chip, mc), src=x_ref) for j, chip in enumerate(chips)]
    passed = [copy(4 + j, (*chip, mc), sibling) for j, chip in enumerate(chips)]
    recv_ici = [copy(1 + j, (*chip, mc), me) for j, chip in enumerate(chips)]
    recv_d2d = [copy(0, sibling, me)] + [copy(4 + j, (*chip, 1 - mc), me) for j, chip in enumerate(chips)]
    return mine, first, passed, recv_ici, recv_d2d


def _gather_start(*refs):
    mine, first, _, _, _ = _gather_copies(*refs)
    mine.start()
    for cp in first:
        cp.start()


def _gather_finish(*refs):
    mine, first, passed, recv_ici, recv_d2d = _gather_copies(*refs)
    for landed, onward in zip(recv_ici, passed):
        landed.wait_recv()
        onward.start()
    for landed in recv_d2d:
        landed.wait_recv()
    for cp in first + passed:
        cp.wait_send()
    mine.wait()


def _a2a_copies(g_ref, out_ref, send_sems, recv_sems, local_sem):
    mx, my, mc = lax.axis_index("x"), lax.axis_index("y"), lax.axis_index("c")
    me = 4 * mx + 2 * my + mc
    mine = pltpu.make_async_copy(g_ref.at[me], out_ref.at[me], local_sem)
    copies = []
    for k in range(1, N_DEV):
        px = 1 - mx if (k >> 2) & 1 else mx
        py = 1 - my if (k >> 1) & 1 else my
        pc = 1 - mc if k & 1 else mc
        copies.append(pltpu.make_async_remote_copy(
            src_ref=g_ref.at[4 * px + 2 * py + pc], dst_ref=out_ref.at[me],
            send_sem=send_sems.at[k - 1], recv_sem=recv_sems.at[k - 1],
            device_id=(px, py, pc), device_id_type=MESH))
    return mine, copies


def _a2a_start(*refs):
    mine, copies = _a2a_copies(*refs)
    mine.start()
    for cp in copies:
        cp.start()


def _a2a_finish(*refs):
    mine, copies = _a2a_copies(*refs)
    for cp in copies:
        cp.wait()
    mine.wait()


_EXCHANGES = {"gather": (_gather_start, _gather_finish), "a2a": (_a2a_start, _a2a_finish)}


def _exchange_shape(kind, src):
    return _sds((N_DEV,) + src.shape if kind == "gather" else src.shape, src.dtype)


def _all_gather(x, name):
    def body(x_ref, out_ref, *sems):
        _gather_start(x_ref, out_ref, *sems)
        _gather_finish(x_ref, out_ref, *sems)

    return _pcall(body, name=name, out_shape=_exchange_shape("gather", x), in_specs=[_ANY], out_specs=_ANY,
                  scratch_shapes=list(_SEMS))(x)


def _pcall_ride(body, ride, args, *, name, grid, in_specs, out_specs, out_shape, scratch_shapes=(), compiler_params=None):
    if ride is None:
        out = _pcall(body, name=name, grid=grid, in_specs=in_specs, out_specs=out_specs, out_shape=out_shape,
                     scratch_shapes=list(scratch_shapes), compiler_params=compiler_params)(*args)
        return out, None
    kind, src = ride
    start, finish = _EXCHANGES[kind]
    single = not isinstance(out_shape, (tuple, list))
    out_specs_t = (out_specs,) if single else tuple(out_specs)
    out_shape_t = (out_shape,) if single else tuple(out_shape)
    n_in, n_out, n_sc = len(in_specs), len(out_shape_t), len(scratch_shapes)

    def wrapped(*refs):
        ins, src_ref = refs[:n_in], refs[n_in]
        outs, dst_ref = refs[n_in + 1:n_in + 1 + n_out], refs[n_in + 1 + n_out]
        scratch = refs[n_in + 2 + n_out:n_in + 2 + n_out + n_sc]
        sems = refs[n_in + 2 + n_out + n_sc:]
        first = pl.program_id(0) == 0
        last = pl.program_id(0) == grid[0] - 1
        for ax in range(1, len(grid)):
            first = jnp.logical_and(first, pl.program_id(ax) == 0)
            last = jnp.logical_and(last, pl.program_id(ax) == grid[ax] - 1)

        @pl.when(first)
        def _():
            start(src_ref, dst_ref, *sems)

        body(*ins, *outs, *scratch)

        @pl.when(last)
        def _():
            finish(src_ref, dst_ref, *sems)

    res = _pcall(wrapped, name=name, grid=grid, in_specs=list(in_specs) + [_ANY], out_specs=out_specs_t + (_ANY,),
                 out_shape=out_shape_t + (_exchange_shape(kind, src),),
                 scratch_shapes=list(scratch_shapes) + list(_SEMS), compiler_params=compiler_params)(*args, src)
    return (res[0] if single else tuple(res[:-1])), res[-1]


class _Epilogue:
    def __init__(self, fn, args, in_specs, out_specs, out_shape, steps=None, lhs_map=None):
        self.fn, self.args, self.in_specs, self.out_specs, self.out_shape = fn, tuple(args), list(in_specs), out_specs, out_shape
        self.steps, self.lhs_map = steps, lhs_map


def _mm_nn(a, w, out_dtype, name, square_lhs=False, relu_out=False, ride=None, tm=TM, epi=None):
    m, k = a.shape
    nb, _, bn = w.shape
    tm = min(tm, m)
    n_extra = 0 if epi is None else len(epi.args)

    def body(*refs):
        a_ref, w_ref = refs[:2]
        av = a_ref[...]
        if square_lhs:
            av = av * av
        if epi is not None:
            assert nb == 1
            epi.fn(jnp.dot(av, w_ref[0], preferred_element_type=F32), *refs[2:])
            return
        for j in range(nb):
            r = jnp.dot(av, w_ref[j], preferred_element_type=F32)
            if relu_out:
                r = jnp.maximum(r, 0.0)
            refs[2][:, j * bn:(j + 1) * bn] = r.astype(out_dtype)

    in_specs = [pl.BlockSpec((tm, k), lambda i: (i, 0)), pl.BlockSpec((nb, k, bn), lambda i: (0, 0, 0))]
    if epi is None:
        args, out_specs, out_shape = (a, w), pl.BlockSpec((tm, nb * bn), lambda i: (i, 0)), _sds((m, nb * bn), out_dtype)
    else:
        args, out_specs, out_shape = (a, w) + epi.args, epi.out_specs, epi.out_shape
        in_specs += epi.in_specs
    out, ex = _pcall_ride(body, ride, args, name=name, grid=(m // tm,), in_specs=in_specs, out_specs=out_specs,
                          out_shape=out_shape, compiler_params=_cp())
    return out if ride is None else (out, ex)


def _mm_nt(dy, w, out_dtype, name, relu_mul=None, ride=None, tm=TM, epi=None):
    m = dy.shape[0]
    nb, k, bn = w.shape
    tm = min(tm, m)
    n_extra = (0 if relu_mul is None else 1) + (0 if epi is None else len(epi.args))

    def body(*refs):
        dy_ref, w_ref = refs[:2]
        extra, outs, wt = refs[2:2 + n_extra], refs[2 + n_extra:-1], refs[-1]

        @pl.when(pl.program_id(0) == 0)
        def _():
            for j in range(nb):
                wt[j * bn:(j + 1) * bn, :] = w_ref[j].T

        acc = jnp.dot(dy_ref[...], wt[...], preferred_element_type=F32)
        if epi is not None:
            epi.fn(acc, *extra, *outs)
            return
        if relu_mul is not None:
            acc = acc * (2.0 * extra[0][...].astype(F32))
        outs[0][...] = acc.astype(out_dtype)

    lhs_map = (lambda i: (i, 0)) if epi is None or epi.lhs_map is None else epi.lhs_map
    in_specs = [pl.BlockSpec((tm, nb * bn), lhs_map), pl.BlockSpec((nb, k, bn), lambda i: (0, 0, 0))]
    args = [dy, w]
    if relu_mul is not None:
        in_specs.append(pl.BlockSpec((tm, k), lambda i: (i, 0)))
        args.append(relu_mul)
    steps = m // tm
    if epi is None:
        out_specs, out_shape = pl.BlockSpec((tm, k), lambda i: (i, 0)), _sds((m, k), out_dtype)
    else:
        args += list(epi.args)
        in_specs += epi.in_specs
        out_specs, out_shape = epi.out_specs, epi.out_shape
        steps = steps if epi.steps is None else epi.steps
    out, ex = _pcall_ride(
        body, ride, args, name=name, grid=(steps,), in_specs=in_specs, out_specs=out_specs, out_shape=out_shape,
        scratch_shapes=[pltpu.VMEM((nb * bn, k), BF16)], compiler_params=_cp())
    return out if ride is None else (out, ex)


def _mm_tn(a, b, name, col_blocks, block, out_dtype, tm, square_lhs=False):
    m, k = a.shape
    nn = b.shape[1]
    steps = m // tm
    if col_blocks:
        nblk, acc_shape = nn // block, (k, block)
        a_spec = pl.BlockSpec((tm, k), lambda j, s: (s, 0))
        b_spec = pl.BlockSpec((tm, block), lambda j, s: (s, j))
    else:
        nblk, acc_shape = k // block, (block, nn)
        a_spec = pl.BlockSpec((tm, block), lambda j, s: (s, j))
        b_spec = pl.BlockSpec((tm, nn), lambda j, s: (s, 0))

    def body(a_ref, b_ref, o_ref, acc):
        s = pl.program_id(1)

        @pl.when(s == 0)
        def _():
            acc[...] = jnp.zeros_like(acc)

        av = a_ref[...]
        if square_lhs:
            av = av.astype(F32)
            av = (av * av).astype(BF16)
        acc[...] += _dot_tn(av, b_ref[...])

        @pl.when(s == steps - 1)
        def _():
            o_ref[...] = acc[...].astype(out_dtype)

    return _pcall(
        body, name=name, grid=(nblk, steps), in_specs=[a_spec, b_spec],
        out_specs=pl.BlockSpec((None,) + acc_shape, lambda j, s: (j, 0, 0)),
        out_shape=_sds((nblk,) + acc_shape, out_dtype),
        scratch_shapes=[pltpu.VMEM(acc_shape, F32)], compiler_params=_cp(),
    )(a, b)


def _mm_wgrad(at, b, name, bn, out_dtype, transpose_out=False, square_rhs=False, halves=False, ride=None):
    k, m = at.shape
    nblk = b.shape[1] // bn
    rows, cols = (bn, k) if transpose_out else (k, bn)
    nout = 2 if halves else 1
    per = rows // nout

    def body(a_ref, b_ref, *o_refs):
        bv = b_ref[...]
        if square_rhs:
            bv = bv * bv
        r = jnp.dot(a_ref[...], bv, preferred_element_type=F32)
        r = (r.T if transpose_out else r).astype(out_dtype)
        for i, o_ref in enumerate(o_refs):
            o_ref[...] = r[i * per:(i + 1) * per, :]

    out, ex = _pcall_ride(
        body, ride, (at, b), name=name, grid=(nblk,),
        in_specs=[pl.BlockSpec((k, m), lambda j: (0, 0)), pl.BlockSpec((m, bn), lambda j: (0, j))],
        out_specs=tuple(pl.BlockSpec((None, per, cols), lambda j: (j, 0, 0)) for _ in range(nout)),
        out_shape=tuple(_sds((nblk, per, cols), out_dtype) for _ in range(nout)), compiler_params=_cp())
    out = out if halves else out[0]
    return out if ride is None else (out, ex)


def _mm_in(hn, w, cosf, sins, ride):
    m, k = hn.shape
    nb, _, bn = w.shape

    def body(a_ref, w_ref, c_ref, s_ref, qkv_ref, rest_ref, pt):
        av = a_ref[...]
        for j in range(nb):
            pt[:, j * bn:(j + 1) * bn] = jnp.dot(av, w_ref[j], preferred_element_type=F32)
        cf, ss = c_ref[...], s_ref[...]
        for h in range(HEADS):
            sq = slice(h * HEAD_DIM, (h + 1) * HEAD_DIM)
            sk = slice(RET_W + h * HEAD_DIM, RET_W + (h + 1) * HEAD_DIM)
            qkv_ref[:, sq] = _rot(pt[:, sq], cf, ss).astype(BF16)
            qkv_ref[:, sk] = (_rot(pt[:, sk], cf, ss) * K_SCALE).astype(BF16)
        qkv_ref[:, 2 * RET_W:] = pt[:, 2 * RET_W:3 * RET_W].astype(BF16)
        rest_ref[...] = pt[:, 3 * RET_W:]

    tab = pl.BlockSpec((TM, HEAD_DIM), lambda i: (i, 0))
    wide = pl.BlockSpec((TM, 3 * RET_W), lambda i: (i, 0))
    return _pcall_ride(
        body, ride, (hn, w, cosf, sins), name="mm_in", grid=(m // TM,),
        in_specs=[pl.BlockSpec((TM, k), lambda i: (i, 0)), pl.BlockSpec((nb, k, bn), lambda i: (0, 0, 0)), tab, tab],
        out_specs=(wide, wide), out_shape=(_sds((m, 3 * RET_W), BF16), _sds((m, nb * bn - 3 * RET_W), F32)),
        scratch_shapes=[pltpu.VMEM((TM, nb * bn), F32)], compiler_params=_cp())


def _mod_part(s16, w_ada, b_cols):
    def body(s_ref, w_ref, b_ref, o_ref):
        o_ref[...] = _dot(s_ref[...], w_ref[...]) + b_ref[...]

    return _pcall(body, name="mod_part", out_shape=_sds((s16.shape[0], w_ada.shape[1]), F32),
                  compiler_params=_cp())(s16, w_ada, b_cols)


def _ada_bwd(s16, dm_cols, w_ada):
    def body(s_ref, d_ref, w_ref, gw_ref, ds_ref):
        gw_ref[...] = _dot_tn(s_ref[...], d_ref[...])
        ds_ref[...] = _dot_nt(d_ref[...], w_ref[...])

    return _pcall(body, name="ada_bwd",
                  out_shape=(_sds(w_ada.shape, F32), _sds(s16.shape, F32)), compiler_params=_cp())(s16, dm_cols, w_ada)


def _norm1_fwd(ctx, x, g, modrows, cb, ride=None):
    l_len, d = ctx.shape
    nb = (l_len + x.shape[0]) // TM

    def body(ctx_ref, x_ref, g_ref, m_ref, o_ref, ot_ref):
        is_ctx = pl.program_id(0) < cb
        xin = jnp.where(is_ctx, ctx_ref[...], x_ref[...])
        sh = jnp.where(is_ctx, m_ref[R_CSH1:R_CSH1 + 1, :], m_ref[R_SH1:R_SH1 + 1, :])
        sc = jnp.where(is_ctx, m_ref[R_CSC1:R_CSC1 + 1, :], m_ref[R_SC1:R_SC1 + 1, :])
        ms = jnp.mean(xin * xin, axis=-1, keepdims=True)
        n = xin * lax.rsqrt(ms + EPS) * g_ref[...]
        hn = n * (1.0 + sc) + sh
        o_ref[...] = hn.astype(BF16)
        ot_ref[...] = hn.T.astype(BF16)

    return _pcall_ride(
        body, ride, (ctx, x, g, modrows), name="norm1_fwd", grid=(nb,),
        in_specs=[pl.BlockSpec((TM, d), lambda i: (jnp.minimum(i, cb - 1), 0)),
                  pl.BlockSpec((TM, d), lambda i: (jnp.maximum(i - cb, 0), 0)),
                  pl.BlockSpec((1, d), lambda i: (0, 0)), pl.BlockSpec((8, d), lambda i: (0, 0))],
        out_specs=(pl.BlockSpec((TM, d), lambda i: (i, 0)), pl.BlockSpec((d, TM), lambda i: (0, i))),
        out_shape=(_sds((nb * TM, d), BF16), _sds((d, nb * TM), BF16)), compiler_params=_cp())


def _mix_fwd(o_f, o_b, p, h_f, h_b, cb, t_len):
    def body(of_ref, ob_ref, g_ref, gate_ref, hf_ref, hb_ref, mix_ref):
        o = of_ref[...] + ob_ref[...]
        g = g_ref[...]
        sg = g * _sigmoid(g)
        for hh in range(HEADS):
            sl = slice(hh * HEAD_DIM, (hh + 1) * HEAD_DIM)
            oh = o[:, sl]
            yc = oh - jnp.mean(oh, axis=-1, keepdims=True)
            var = jnp.mean(yc * yc, axis=-1, keepdims=True)
            mix_ref[:, sl] = (sg[:, sl] * (yc * lax.rsqrt(var + EPS))).astype(BF16)
        mix_ref[:, RET_W:] = ((hf_ref[...] + hb_ref[...]) * _gelu(gate_ref[...])).astype(BF16)

    row = lambda i: (i + cb, 0)
    return _pcall(
        body, name="mix_fwd", grid=(t_len // TM,),
        in_specs=[pl.BlockSpec((TM, RET_W), row), pl.BlockSpec((TM, RET_W), row),
                  pl.BlockSpec((TM, RET_W), lambda i: (i + cb, COL_G)), pl.BlockSpec((TM, LRU_W), lambda i: (i + cb, COL_GATE)),
                  pl.BlockSpec((TM, LRU_W), row), pl.BlockSpec((TM, LRU_W), row)],
        out_specs=pl.BlockSpec((TM, RET_W + LRU_W), lambda i: (i, 0)),
        out_shape=_sds((t_len, RET_W + LRU_W), BF16), compiler_params=_cp(),
    )(o_f, o_b, p, p, h_f, h_b)


def _res_norm2(x, y, g, modrows):
    t_len, d = x.shape

    def body(x_ref, y_ref, g_ref, m_ref, x1_ref, h2_ref, h2t_ref):
        x1 = x_ref[...] + m_ref[R_G1:R_G1 + 1, :] * y_ref[...]
        ms = jnp.mean(x1 * x1, axis=-1, keepdims=True)
        n = x1 * lax.rsqrt(ms + EPS) * g_ref[...]
        x1_ref[...] = x1
        h2 = n * (1.0 + m_ref[R_SC2:R_SC2 + 1, :]) + m_ref[R_SH2:R_SH2 + 1, :]
        h2_ref[...] = h2.astype(BF16)
        h2t_ref[...] = h2.T.astype(BF16)

    t = pl.BlockSpec((TL, d), lambda i: (i, 0))
    tt = pl.BlockSpec((d, TL), lambda i: (0, i))
    return _pcall(
        body, name="res_norm2", grid=(t_len // TL,),
        in_specs=[t, t, pl.BlockSpec((1, d), lambda i: (0, 0)), pl.BlockSpec((8, d), lambda i: (0, 0))],
        out_specs=(t, t, tt), out_shape=(_sds((t_len, d), F32), _sds((t_len, d), BF16), _sds((d, t_len), BF16)),
        compiler_params=_cp(),
    )(x, y, g, modrows)


def _final_epilogue(x1, target, fg, modrows, tm):
    t_len, d = x1.shape

    def fn(z, x1_ref, t_ref, fg_ref, m_ref, dx2_ref, dz_ref, dzt_ref, acc_ref):
        @pl.when(pl.program_id(0) == 0)
        def _():
            acc_ref[...] = jnp.zeros_like(acc_ref)

        g2 = m_ref[R_G2:R_G2 + 1, :]
        x2 = x1_ref[...] + g2 * z
        rstd = lax.rsqrt(jnp.mean(x2 * x2, axis=-1, keepdims=True) + EPS)
        xh = x2 * rstd
        fg = fg_ref[...]
        e = xh * fg - t_ref[...]
        dy = e * (1.0 / d)
        dxh = dy * fg
        dx2 = rstd * (dxh - xh * jnp.mean(dxh * xh, axis=-1, keepdims=True))
        dx2_ref[...] = dx2
        dz = g2 * dx2
        dz_ref[...] = dz.astype(BF16)
        dzt_ref[...] = dz.T.astype(BF16)
        acc_ref[0:1, :] += jnp.sum(dy * xh, axis=0, keepdims=True)
        acc_ref[1:2, :] += jnp.sum(dx2 * z, axis=0, keepdims=True)
        acc_ref[2:3, :] += jnp.sum(e * e, axis=0, keepdims=True)

    t = pl.BlockSpec((tm, d), lambda i: (i, 0))
    return _Epilogue(
        fn, (x1, target, fg, modrows),
        in_specs=[t, t, pl.BlockSpec((1, d), lambda i: (0, 0)), pl.BlockSpec((8, d), lambda i: (0, 0))],
        out_specs=(t, t, pl.BlockSpec((d, tm), lambda i: (0, i)), pl.BlockSpec((8, d), lambda i: (0, 0))),
        out_shape=(_sds((t_len, d), F32), _sds((t_len, d), BF16), _sds((d, t_len), BF16), _sds((8, d), F32)))


def _bwd_norm2_epilogue(x1, dx2, y, g, modrows, tm):
    t_len, d = x1.shape

    def fn(dh2, x1_ref, dx2_ref, y_ref, g_ref, m_ref, dx1_ref, dy_ref, acc_ref):
        @pl.when(pl.program_id(0) == 0)
        def _():
            acc_ref[...] = jnp.zeros_like(acc_ref)

        x1 = x1_ref[...]
        rstd = lax.rsqrt(jnp.mean(x1 * x1, axis=-1, keepdims=True) + EPS)
        xh = x1 * rstd
        gn = g_ref[...]
        dn = dh2 * (1.0 + m_ref[R_SC2:R_SC2 + 1, :])
        dxh = dn * gn
        dx1 = dx2_ref[...] + rstd * (dxh - xh * jnp.mean(dxh * xh, axis=-1, keepdims=True))
        dx1_ref[...] = dx1
        dy_ref[...] = (m_ref[R_G1:R_G1 + 1, :] * dx1).astype(BF16)
        acc_ref[0:1, :] += jnp.sum(dh2, axis=0, keepdims=True)
        acc_ref[1:2, :] += jnp.sum(dh2 * xh * gn, axis=0, keepdims=True)
        acc_ref[2:3, :] += jnp.sum(dn * xh, axis=0, keepdims=True)
        acc_ref[3:4, :] += jnp.sum(dx1 * y_ref[...], axis=0, keepdims=True)

    t = pl.BlockSpec((tm, d), lambda i: (i, 0))
    return _Epilogue(
        fn, (x1, dx2, y, g, modrows),
        in_specs=[t, t, t, pl.BlockSpec((1, d), lambda i: (0, 0)), pl.BlockSpec((8, d), lambda i: (0, 0))],
        out_specs=(t, t, pl.BlockSpec((8, d), lambda i: (0, 0))),
        out_shape=(_sds((t_len, d), F32), _sds((t_len, d), BF16), _sds((8, d), F32)))


def _mix_bwd_epilogue(o_f, o_b, p, h_f, h_b, cb):
    n = o_f.shape[0]

    def fn(dm, of_ref, ob_ref, g_ref, gate_ref, hf_ref, hb_ref, do_ref, dg_ref, dgate_ref, dhs_ref):
        is_ctx = pl.program_id(0) < cb

        @pl.when(is_ctx)
        def _():
            for r in (do_ref, dg_ref, dgate_ref, dhs_ref):
                r[...] = jnp.zeros_like(r)

        @pl.when(jnp.logical_not(is_ctx))
        def _():
            o = of_ref[...] + ob_ref[...]
            g = g_ref[...]
            s = _sigmoid(g)
            sg = g * s
            dsg = s * (1.0 + g * (1.0 - s))
            for hh in range(HEADS):
                sl = slice(hh * HEAD_DIM, (hh + 1) * HEAD_DIM)
                oh = o[:, sl]
                yc = oh - jnp.mean(oh, axis=-1, keepdims=True)
                rs = lax.rsqrt(jnp.mean(yc * yc, axis=-1, keepdims=True) + EPS)
                gn = yc * rs
                dret = dm[:, sl]
                dgn = dret * sg[:, sl]
                dg_ref[:, sl] = dret * gn * dsg[:, sl]
                do_ref[:, sl] = rs * (dgn - jnp.mean(dgn, axis=-1, keepdims=True)
                                      - gn * jnp.mean(dgn * gn, axis=-1, keepdims=True))
            dlru = dm[:, RET_W:]
            gate = gate_ref[...]
            dhs_ref[...] = dlru * _gelu(gate)
            dgate_ref[...] = dlru * (hf_ref[...] + hb_ref[...]) * _dgelu(gate)

    t = pl.BlockSpec((TM, RET_W), lambda i: (i, 0))
    return _Epilogue(
        fn, (o_f, o_b, p, p, h_f, h_b),
        in_specs=[t, t, pl.BlockSpec((TM, RET_W), lambda i: (i, COL_G)), pl.BlockSpec((TM, LRU_W), lambda i: (i, COL_GATE)), t, t],
        out_specs=(t, t, t, t), out_shape=tuple(_sds((n, RET_W), F32) for _ in range(4)),
        steps=n // TM, lhs_map=lambda i: (jnp.maximum(i - cb, 0), 0))


def _bwd_norm1_epilogue(ctx, x, dx1, g, modrows, cb):
    t_len, d = x.shape

    def fn(dh, ctx_ref, x_ref, dx1_ref, g_ref, m_ref, gx_ref, acc_ref):
        is_ctx = pl.program_id(0) < cb

        @pl.when(pl.program_id(0) == 0)
        def _():
            acc_ref[...] = jnp.zeros_like(acc_ref)

        xin = jnp.where(is_ctx, ctx_ref[...], x_ref[...])
        sc = jnp.where(is_ctx, m_ref[R_CSC1:R_CSC1 + 1, :], m_ref[R_SC1:R_SC1 + 1, :])
        rstd = lax.rsqrt(jnp.mean(xin * xin, axis=-1, keepdims=True) + EPS)
        xh = xin * rstd
        gn = g_ref[...]
        dn = dh * (1.0 + sc)
        dxh = dn * gn
        gx_ref[...] = dx1_ref[...] + rstd * (dxh - xh * jnp.mean(dxh * xh, axis=-1, keepdims=True))
        s0 = jnp.sum(dh, axis=0, keepdims=True)
        s1 = jnp.sum(dh * xh * gn, axis=0, keepdims=True)
        acc_ref[4:5, :] += jnp.sum(dn * xh, axis=0, keepdims=True)

        @pl.when(is_ctx)
        def _():
            acc_ref[0:1, :] += s0
            acc_ref[1:2, :] += s1

        @pl.when(jnp.logical_not(is_ctx))
        def _():
            acc_ref[2:3, :] += s0
            acc_ref[3:4, :] += s1

    lat = pl.BlockSpec((TM, d), lambda i: (jnp.maximum(i - cb, 0), 0))
    return _Epilogue(
        fn, (ctx, x, dx1, g, modrows),
        in_specs=[pl.BlockSpec((TM, d), lambda i: (jnp.minimum(i, cb - 1), 0)),
                  lat, lat, pl.BlockSpec((1, d), lambda i: (0, 0)), pl.BlockSpec((8, d), lambda i: (0, 0))],
        out_specs=(lat, pl.BlockSpec((8, d), lambda i: (0, 0))),
        out_shape=(_sds((t_len, d), F32), _sds((8, d), F32)))


def _rot(x, cf, ss):
    return x * cf + pltpu.roll(x, HEAD_DIM // 2, 1) * ss


def _decay_exponents(dirn):
    ii = lax.broadcasted_iota(jnp.int32, (CHUNK, CHUNK), 0)
    jj = lax.broadcasted_iota(jnp.int32, (CHUNK, CHUNK), 1)
    rel = ii - jj if dirn == 0 else jj - ii
    pos = ii.astype(F32)
    if dirn == 0:
        cq, cs = pos + 1.0, (CHUNK - 1.0) - pos
    else:
        cq, cs = CHUNK - pos, pos
    return rel, jnp.maximum(rel, 0).astype(F32), cq, cs


def _store_decay(lg_ref, dec):
    for dirn in (0, 1):
        rel, relf, cq, cs = _decay_exponents(dirn)
        for h in range(HEADS):
            lgv = lg_ref[dirn, h]
            dec[dirn, h, 0] = jnp.where(rel >= 0, jnp.exp(lgv * relf), 0.0)
            dec[dirn, h, 1] = jnp.exp(lgv * cq)
            dec[dirn, h, 2] = jnp.exp(lgv * cs)


def _ret_rows(cc, nc, step_of):
    return [lambda s, dirn=dirn: _tile_order(dirn, step_of(s), cc, nc) for dirn in (0, 1)]


def _ret_in_specs(rows):
    specs = []
    for row in rows:
        specs += [pl.BlockSpec((CHUNK, RET_W), lambda s, o=o, row=row: (row(s), o)) for o in (0, 1, 2)]
    return specs


def _ret_fwd(qkv, lg, cc, ride=None):
    n = qkv.shape[0]
    nc = n // CHUNK
    rows = _ret_rows(cc, nc, lambda s: s)

    def body(lg_ref, q0, k0, v0, q1, k1, v1, o0, o1, sp0, sp1, st, dec):
        @pl.when(pl.program_id(0) == 0)
        def _():
            st[...] = jnp.zeros_like(st)
            _store_decay(lg_ref, dec)

        for dirn, (q_ref, k_ref, v_ref, o_ref, sp_ref) in enumerate(((q0, k0, v0, o0, sp0), (q1, k1, v1, o1, sp1))):
            for h in range(HEADS):
                sl = slice(h * HEAD_DIM, (h + 1) * HEAD_DIM)
                q, k, v = q_ref[:, sl], k_ref[:, sl], v_ref[:, sl]
                sp = st[dirn, h]
                sp_ref[h] = sp
                o_ref[:, sl] = _dot(_dot_nt(q, k) * dec[dirn, h, 0], v) + _dot(q * dec[dirn, h, 1], sp)
                st[dirn, h] = jnp.exp(lg_ref[dirn, h] * CHUNK) * sp + _dot_tn(k * dec[dirn, h, 2], v)

    o_specs = [pl.BlockSpec((CHUNK, RET_W), lambda s, row=row: (row(s), 0)) for row in rows]
    state = pl.BlockSpec((None, HEADS, CHUNK, HEAD_DIM), lambda s: (s, 0, 0, 0))
    return _pcall_ride(
        body, ride, (lg,) + (qkv,) * 6, name="ret_fwd", grid=(nc,),
        in_specs=[pl.BlockSpec(memory_space=pltpu.SMEM)] + _ret_in_specs(rows),
        out_specs=(o_specs[0], o_specs[1], state, state),
        out_shape=(_sds((n, RET_W), F32),) * 2 + (_sds((nc, HEADS, CHUNK, HEAD_DIM), F32),) * 2,
        scratch_shapes=[pltpu.VMEM((2, HEADS, CHUNK, HEAD_DIM), F32), pltpu.VMEM((2, HEADS, 3, CHUNK, CHUNK), F32)],
        compiler_params=_cp())


def _ret_bwd(qkv, lg, do, s_prev, cc, ride=None):
    n = qkv.shape[0]
    nc = n // CHUNK
    rows = _ret_rows(cc, nc, lambda s: nc - 1 - s)

    def body(lg_ref, q0, k0, v0, q1, k1, v1, do0, do1, sp0, sp1, dq0, dk0, dv0, dq1, dk1, dv1, dlg_ref, dst, dec):
        @pl.when(pl.program_id(0) == 0)
        def _():
            dst[...] = jnp.zeros_like(dst)
            dlg_ref[...] = jnp.zeros_like(dlg_ref)
            _store_decay(lg_ref, dec)

        for dirn, (q_ref, k_ref, v_ref, do_ref, sp_ref, dq_ref, dk_ref, dv_ref) in enumerate(
                ((q0, k0, v0, do0, sp0, dq0, dk0, dv0), (q1, k1, v1, do1, sp1, dq1, dk1, dv1))):
            _, relf, cq, cs = _decay_exponents(dirn)
            for h in range(HEADS):
                sl = slice(h * HEAD_DIM, (h + 1) * HEAD_DIM)
                q, k, v = q_ref[:, sl], k_ref[:, sl], v_ref[:, sl]
                dov = do_ref[:, sl]
                dm, wq, ws = dec[dirn, h, 0], dec[dirn, h, 1], dec[dirn, h, 2]
                a = _dot_nt(q, k)
                sp = sp_ref[h]
                dsn = dst[dirn, h]
                gc = jnp.exp(lg_ref[dirn, h] * CHUNK)
                g1 = _dot_nt(dov, sp)
                da = _dot_nt(dov, v) * dm
                dq = g1 * wq + _dot(da, k)
                h1 = _dot_nt(v, dsn)
                dk = _dot_tn(da, q) + h1 * ws
                dv_ref[:, sl] = _dot_tn(a * dm, dov) + _dot(k * ws, dsn)
                dst[dirn, h] = gc * dsn + _dot_tn(q * wq, dov)
                term = da * a * relf + q * g1 * (wq * cq) + k * h1 * (ws * cs) + sp * dsn * (CHUNK * gc)
                dlg_ref[dirn * HEADS + h:dirn * HEADS + h + 1, 0:HEAD_DIM] += jnp.sum(term, axis=0, keepdims=True)
                dq_ref[:, sl] = dq
                dk_ref[:, sl] = dk

    wide = [pl.BlockSpec((CHUNK, RET_W), lambda s, row=row: (row(s), 0)) for row in rows]
    state = pl.BlockSpec((None, HEADS, CHUNK, HEAD_DIM), lambda s: (nc - 1 - s, 0, 0, 0))
    return _pcall_ride(
        body, ride, (lg,) + (qkv,) * 6 + (do, do, s_prev[0], s_prev[1]), name="ret_bwd", grid=(nc,),
        in_specs=[pl.BlockSpec(memory_space=pltpu.SMEM)] + _ret_in_specs(rows) + wide + [state, state],
        out_specs=(wide[0],) * 3 + (wide[1],) * 3 + (pl.BlockSpec((2 * HEADS, 8 * HEAD_DIM), lambda s: (0, 0)),),
        out_shape=(_sds((n, RET_W), F32),) * 6 + (_sds((2 * HEADS, 8 * HEAD_DIM), F32),),
        scratch_shapes=[pltpu.VMEM((2, HEADS, CHUNK, HEAD_DIM), F32), pltpu.VMEM((2, HEADS, 3, CHUNK, CHUNK), F32)],
        compiler_params=_cp())


def _shift_rows(cur, prev8, next8, k, seg_start, seg_end):
    tm = cur.shape[0]
    rows = _rows_iota(cur.shape)
    if k < 0:
        out = pltpu.roll(cur, -k, 0)
        for j in range(-k):
            halo = jnp.where(seg_start, 0.0, prev8[SUB + k + j:SUB + k + j + 1, :])
            out = jnp.where(rows == j, halo, out)
    else:
        out = pltpu.roll(cur, tm - k, 0)
        for j in range(k):
            halo = jnp.where(seg_end, 0.0, next8[j:j + 1, :])
            out = jnp.where(rows == tm - k + j, halo, out)
    return out


def _seg_flags(t, cb, nb):
    return jnp.logical_or(t == 0, t == cb), jnp.logical_or(t == cb - 1, t == nb - 1)


def _halo_specs(tile_of, n_rows, col):
    per = TM // SUB
    return [pl.BlockSpec((TM, LRU_W), lambda s: (tile_of(s), col)),
            pl.BlockSpec((SUB, LRU_W), lambda s: (jnp.maximum(tile_of(s) * per - 1, 0), col)),
            pl.BlockSpec((SUB, LRU_W), lambda s: (jnp.minimum((tile_of(s) + 1) * per, n_rows // SUB - 1), col))]


def _lru_gates(xr, prev8, next8, seg_start, seg_end, cw_ref, cb_ref, wg_ref, bg_ref, sp_ref):
    xm1 = _shift_rows(xr, prev8, next8, -1, seg_start, seg_end)
    xp1 = _shift_rows(xr, prev8, next8, 1, seg_start, seg_end)
    xp2 = _shift_rows(xr, prev8, next8, 2, seg_start, seg_end)
    xc = cb_ref[...] + xm1 * cw_ref[0:1, :] + xr * cw_ref[1:2, :] + xp1 * cw_ref[2:3, :] + xp2 * cw_ref[3:4, :]
    pre = _dot(xc, wg_ref[...]) + bg_ref[...]
    r = _sigmoid(pre[:, :LRU_W])
    i = _sigmoid(pre[:, LRU_W:])
    la = (-LRU_C) * r * sp_ref[...]
    a = jnp.exp(la)
    th = jnp.tanh(la)
    sq = jnp.sqrt(-2.0 * th / (1.0 - th))
    return xc, r, i, a, sq


def _scan_tile(a, b, ascending, a_sc, b_sc, carry, out_ref):
    tm, w = a.shape
    nsub = tm // SUB
    a = a.reshape(nsub, SUB, w)
    b = b.reshape(nsub, SUB, w)
    r8 = lax.broadcasted_iota(jnp.int32, a.shape, 1)
    for k in (1, 2, 4):
        if ascending:
            m = r8 >= k
            a_s, b_s = pltpu.roll(a, k, 1), pltpu.roll(b, k, 1)
        else:
            m = r8 < SUB - k
            a_s, b_s = pltpu.roll(a, SUB - k, 1), pltpu.roll(b, SUB - k, 1)
        b = a * jnp.where(m, b_s, 0.0) + b
        a = a * jnp.where(m, a_s, 1.0)
    a_sc[...] = a.reshape(tm, w)
    b_sc[...] = b.reshape(tm, w)

    def step(j, c):
        off = pl.multiple_of((j if ascending else nsub - 1 - j) * SUB, SUB)
        hb = a_sc[pl.ds(off, SUB), :] * c + b_sc[pl.ds(off, SUB), :]
        out_ref[pl.ds(off, SUB), :] = hb
        last = hb[SUB - 1:SUB, :] if ascending else hb[0:1, :]
        return jnp.broadcast_to(last, c.shape)

    carry[...] = lax.fori_loop(0, nsub, step, carry[...])


def _lru_fwd(p, wg, bg, sp, cw, cbias, dirn, cb, ride=None):
    n = p.shape[0]
    nb = n // TM
    tile_of = lambda s: _tile_order(dirn, s, cb, nb)

    def body(x_ref, xp_ref, xn_ref, wg_ref, bg_ref, sp_ref, cw_ref, cb_ref, h_ref, cin_ref, carry, a_sc, b_sc):
        s = pl.program_id(0)

        @pl.when(s == 0)
        def _():
            carry[...] = jnp.zeros_like(carry)

        seg_start, seg_end = _seg_flags(tile_of(s), cb, nb)
        xc, r, i, a, sq = _lru_gates(x_ref[...], xp_ref[...], xn_ref[...], seg_start, seg_end,
                                     cw_ref, cb_ref, wg_ref, bg_ref, sp_ref)
        cin_ref[...] = carry[...]
        _scan_tile(a, sq * (i * xc), dirn == 0, a_sc, b_sc, carry, h_ref)

    full = lambda shape: pl.BlockSpec(shape, lambda s: (0,) * len(shape))
    return _pcall_ride(
        body, ride, (p, p, p, wg, bg, sp, cw, cbias), name=f"lru_fwd{dirn}", grid=(nb,),
        in_specs=_halo_specs(tile_of, n, COL_XR) + [full((LRU_W, 2 * LRU_W)), full((1, 2 * LRU_W)), full((1, LRU_W)),
                                               full((4, LRU_W)), full((1, LRU_W))],
        out_specs=(pl.BlockSpec((TM, LRU_W), lambda s: (tile_of(s), 0)),
                   pl.BlockSpec((None, SUB, LRU_W), lambda s: (tile_of(s), 0, 0))),
        out_shape=(_sds((n, LRU_W), F32), _sds((nb, SUB, LRU_W), F32)),
        scratch_shapes=[pltpu.VMEM((SUB, LRU_W), F32), pltpu.VMEM((TM, LRU_W), F32), pltpu.VMEM((TM, LRU_W), F32)],
        compiler_params=_cp())


def _lru_bwd(p, wg, bg, sp, cw, cbias, h, cin, dhs, dirn, cb, ride=None):
    n = p.shape[0]
    nb = n // TM
    tile_of = lambda s: _tile_order(dirn, nb - 1 - s, cb, nb)

    def body(x_ref, xp_ref, xn_ref, wg_ref, bg_ref, sp_ref, cw_ref, cb_ref, h_ref, cin_ref, dhs_ref,
             dxc_ref, dwd_ref, acc_ref, carry, a_sc, b_sc, mu_sc, dwg_ref):
        s = pl.program_id(0)

        @pl.when(s == 0)
        def _():
            carry[...] = jnp.zeros_like(carry)
            dwg_ref[...] = jnp.zeros_like(dwg_ref)
            acc_ref[...] = jnp.zeros_like(acc_ref)

        seg_start, seg_end = _seg_flags(tile_of(s), cb, nb)
        xc, r, i, a, sq = _lru_gates(x_ref[...], xp_ref[...], xn_ref[...], seg_start, seg_end,
                                     cw_ref, cb_ref, wg_ref, bg_ref, sp_ref)
        rows = _rows_iota(a.shape)
        hv = h_ref[...]
        dh = dhs_ref[...]
        mu_next = carry[0:1, :]
        _scan_tile(a, a * dh, dirn == 1, a_sc, b_sc, carry, mu_sc)
        mu = mu_sc[...]
        if dirn == 0:
            hprev = jnp.where(rows == 0, cin_ref[0:1, :], pltpu.roll(hv, 1, 0))
            lam = dh + jnp.where(rows == TM - 1, mu_next, pltpu.roll(mu, TM - 1, 0))
        else:
            hprev = jnp.where(rows == TM - 1, cin_ref[0:1, :], pltpu.roll(hv, TM - 1, 0))
            lam = dh + jnp.where(rows == 0, mu_next, pltpu.roll(mu, 1, 0))
        ds = lam * (i * xc)
        di = lam * (sq * xc)
        dla = lam * hprev * a - ds * (a * a) / jnp.maximum(sq, 1e-20)
        dpr = dla * ((-LRU_C) * sp_ref[...]) * r * (1.0 - r)
        dpi = di * i * (1.0 - i)
        dpre = jnp.concatenate([dpr, dpi], axis=1)
        dxc_ref[...] = lam * (sq * i) + _dot_nt(dpre, wg_ref[...])
        dwg_ref[...] += _dot_tn(xc, dpre)
        acc_ref[0:1, :] += jnp.sum(dpre, axis=0, keepdims=True)
        acc_ref[1:2, 0:LRU_W] += jnp.sum(dla * ((-LRU_C) * r), axis=0, keepdims=True)

        @pl.when(s == nb - 1)
        def _():
            low = lax.broadcasted_iota(jnp.int32, (LRU_BD, 2 * LRU_BD), 1) < LRU_BD
            for half in (0, LRU_W):
                for m in range(LRU_BLOCKS // 2):
                    lanes = slice(half + 2 * LRU_BD * m, half + 2 * LRU_BD * (m + 1))
                    even = dwg_ref[2 * m * LRU_BD:(2 * m + 1) * LRU_BD, lanes]
                    odd = dwg_ref[(2 * m + 1) * LRU_BD:(2 * m + 2) * LRU_BD, lanes]
                    dwd_ref[:, lanes] = jnp.where(low, even, odd)

    full = lambda shape: pl.BlockSpec(shape, lambda s: (0,) * len(shape))
    tile = pl.BlockSpec((TM, LRU_W), lambda s: (tile_of(s), 0))
    return _pcall_ride(
        body, ride, (p, p, p, wg, bg, sp, cw, cbias, h, cin, dhs), name=f"lru_bwd{dirn}", grid=(nb,),
        in_specs=_halo_specs(tile_of, n, COL_XR) + [full((LRU_W, 2 * LRU_W)), full((1, 2 * LRU_W)), full((1, LRU_W)),
                                               full((4, LRU_W)), full((1, LRU_W)), tile,
                                               pl.BlockSpec((None, SUB, LRU_W), lambda s: (tile_of(s), 0, 0)), tile],
        out_specs=(tile, full((LRU_BD, 2 * LRU_W)), full((8, 2 * LRU_W))),
        out_shape=(_sds((n, LRU_W), F32), _sds((LRU_BD, 2 * LRU_W), F32), _sds((8, 2 * LRU_W), F32)),
        scratch_shapes=[pltpu.VMEM((SUB, LRU_W), F32)] + [pltpu.VMEM((TM, LRU_W), F32)] * 3
        + [pltpu.VMEM((LRU_W, 2 * LRU_W), F32)],
        compiler_params=_cp())


def _assemble_dp(dqs, dks, dvs, dg, dgate, dxcs, p, cw, cosf, sins, cb, ride=None):
    n = p.shape[0]
    nb = n // TM
    tile_of = lambda s: s

    def body(dqf, dqb, dkf, dkb, dvf, dvb, dg_ref, dgate_ref, cf, pf, nf, cb_, pb, nb_, x_ref, xp_ref, xn_ref,
             cw_ref, cos_ref, sin_ref, dp_ref, acc_ref):
        s = pl.program_id(0)

        @pl.when(s == 0)
        def _():
            acc_ref[...] = jnp.zeros_like(acc_ref)

        seg_start, seg_end = _seg_flags(s, cb, nb)
        dq = dqf[...] + dqb[...]
        dk = dkf[...] + dkb[...]
        cosv, sinv = cos_ref[...], sin_ref[...]
        for h in range(HEADS):
            sl = slice(h * HEAD_DIM, (h + 1) * HEAD_DIM)
            sk = slice(RET_W + h * HEAD_DIM, RET_W + (h + 1) * HEAD_DIM)
            dp_ref[:, sl] = (dq[:, sl] * cosv + pltpu.roll(dq[:, sl] * sinv, HEAD_DIM // 2, 1)).astype(BF16)
            dp_ref[:, sk] = ((dk[:, sl] * cosv + pltpu.roll(dk[:, sl] * sinv, HEAD_DIM // 2, 1)) * K_SCALE).astype(BF16)
        dp_ref[:, 2 * RET_W:3 * RET_W] = (dvf[...] + dvb[...]).astype(BF16)
        dp_ref[:, 3 * RET_W:4 * RET_W] = dg_ref[...].astype(BF16)
        dxc = cf[...] + cb_[...]
        dprev = pf[...] + pb[...]
        dnext = nf[...] + nb_[...]
        dxr = (_shift_rows(dxc, dprev, dnext, 1, seg_start, seg_end) * cw_ref[0:1, :] + dxc * cw_ref[1:2, :]
               + _shift_rows(dxc, dprev, dnext, -1, seg_start, seg_end) * cw_ref[2:3, :]
               + _shift_rows(dxc, dprev, dnext, -2, seg_start, seg_end) * cw_ref[3:4, :])
        dp_ref[:, 4 * RET_W:4 * RET_W + LRU_W] = dxr.astype(BF16)
        dp_ref[:, 4 * RET_W + LRU_W:] = dgate_ref[...].astype(BF16)
        xr, xp, xn = x_ref[...], xp_ref[...], xn_ref[...]
        for j, k in enumerate((-1, 0, 1, 2)):
            xs = xr if k == 0 else _shift_rows(xr, xp, xn, k, seg_start, seg_end)
            acc_ref[j:j + 1, 0:LRU_W] += jnp.sum(dxc * xs, axis=0, keepdims=True)
        acc_ref[4:5, 0:LRU_W] += jnp.sum(dxc, axis=0, keepdims=True)

    t = pl.BlockSpec((TM, RET_W), lambda s: (s, 0))
    args = (dqs[0], dqs[1], dks[0], dks[1], dvs[0], dvs[1], dg, dgate, dxcs[0], dxcs[0], dxcs[0], dxcs[1], dxcs[1], dxcs[1],
            p, p, p, cw, cosf, sins)
    tab = pl.BlockSpec((TM, HEAD_DIM), lambda s: (s, 0))
    return _pcall_ride(
        body, ride, args, name="assemble_dp", grid=(nb,),
        in_specs=[t] * 8 + _halo_specs(tile_of, n, 0) * 2 + _halo_specs(tile_of, n, COL_XR)
        + [pl.BlockSpec((4, LRU_W), lambda s: (0, 0)), tab, tab],
        out_specs=(pl.BlockSpec((TM, 4 * RET_W + 2 * LRU_W), lambda s: (s, 0)), pl.BlockSpec((8, 2 * LRU_W), lambda s: (0, 0))),
        out_shape=(_sds((n, 4 * RET_W + 2 * LRU_W), BF16), _sds((8, 2 * LRU_W), F32)), compiler_params=_cp())


def _adamw(g, w, m, v):
    nm = ADAM_B1 * m + (1.0 - ADAM_B1) * g
    nv = ADAM_B2 * v + (1.0 - ADAM_B2) * (g * g)
    m_hat = nm / (1.0 - ADAM_B1 ** ADAM_STEP)
    v_hat = nv / (1.0 - ADAM_B2 ** ADAM_STEP)
    return (-ADAM_LR) * (m_hat / (jnp.sqrt(v_hat) + ADAM_EPS) + ADAM_WD * w), nm, nv


def _adam_many(items, name):
    n = len(items)

    def body(*refs):
        for i in range(n):
            g, w, m, v = (r[...] for r in refs[4 * i:4 * i + 4])
            for o_ref, val in zip(refs[4 * n + 3 * i:4 * n + 3 * i + 3], _adamw(g, w, m, v)):
                o_ref[...] = val

    out_shape = tuple(_sds(it[1].shape, F32) for it in items for _ in range(3))
    res = _pcall(body, name=name, out_shape=out_shape, compiler_params=_cp())(*[a for it in items for a in it])
    return [tuple(res[3 * i:3 * i + 3]) for i in range(n)]


def _sum_adam(parts_list, w, m, v, name):
    nparts, _, c = parts_list[0].shape
    r = w.shape[0]
    tr = min(parts_list[0].shape[1], 128)
    starts, o = [], 0
    for pa in parts_list:
        starts.append(o)
        o += pa.shape[1] // tr
    nseg = len(parts_list)

    def body(*refs):
        p_refs = refs[:nseg]
        w_ref, m_ref, v_ref, g_ref, d_ref, nm_ref, nv_ref = refs[nseg:]
        i = pl.program_id(0)
        for s, p_ref in enumerate(p_refs):
            end = starts[s + 1] if s + 1 < nseg else r // tr

            @pl.when(jnp.logical_and(i >= starts[s], i < end))
            def _():
                g = p_ref[0].astype(F32)
                for j in range(1, nparts):
                    g = g + p_ref[j].astype(F32)
                g_ref[...] = g
                d_ref[...], nm_ref[...], nv_ref[...] = _adamw(g, w_ref[...], m_ref[...], v_ref[...])

    def seg_spec(s):
        last = parts_list[s].shape[1] // tr - 1
        return pl.BlockSpec((nparts, tr, c), lambda i: (0, jnp.clip(i - starts[s], 0, last), 0))

    t = pl.BlockSpec((tr, c), lambda i: (i, 0))
    return _pcall(
        body, name=name, grid=(r // tr,),
        in_specs=[seg_spec(s) for s in range(nseg)] + [t, t, t],
        out_specs=(t, t, t, t), out_shape=(_sds((r, c), F32),) * 4, compiler_params=_cp(),
    )(*parts_list, w, m, v)


def _sum_parts(parts, name):
    nparts, r, c = parts.shape

    def body(p_ref, o_ref):
        g = p_ref[0]
        for j in range(1, nparts):
            g = g + p_ref[j]
        o_ref[...] = g

    return _pcall(body, name=name, out_shape=_sds((r, c), parts.dtype), compiler_params=_cp())(parts)


def _rot_tables(l_len, t_len):
    rows = t_len // GRID_W
    n_freq = HEAD_DIM // 4
    inv = ROPE_BASE ** (-jnp.arange(n_freq, dtype=F32) / n_freq)
    ang_r = jnp.arange(rows, dtype=F32)[:, None] * inv
    ang_c = jnp.arange(GRID_W, dtype=F32)[:, None] * inv
    cos = jnp.concatenate([jnp.repeat(jnp.cos(ang_r), GRID_W, axis=0), jnp.tile(jnp.cos(ang_c), (rows, 1))], axis=-1)
    sin = jnp.concatenate([jnp.repeat(jnp.sin(ang_r), GRID_W, axis=0), jnp.tile(jnp.sin(ang_c), (rows, 1))], axis=-1)
    cosf = jnp.concatenate([jnp.ones((l_len, HEAD_DIM), F32), jnp.concatenate([cos, cos], axis=-1)], axis=0)
    sins = jnp.concatenate([jnp.zeros((l_len, HEAD_DIM), F32), jnp.concatenate([-sin, sin], axis=-1)], axis=0)
    return cosf, sins


def _block_diag(w):
    eye = jnp.eye(LRU_BLOCKS, dtype=w.dtype)
    return (w[:, :, None, :] * eye[:, None, :, None]).reshape(LRU_W, LRU_W)


def _blocks_from_lanes(dwd_half):
    return dwd_half.reshape(LRU_BD, LRU_BLOCKS, LRU_BD).transpose(1, 0, 2)


def _silu(x):
    return x * jax.nn.sigmoid(x)


def kernel(x, c, ctx, c_ctx, w_ada, b_ada, norm1_g, norm2_g, w_in, ret_decay, conv_w, conv_b, lru_wa, lru_ba, lru_wx, lru_bx, lru_lambda, w_out, w_mlp1, w_mlp2, final_g, loss_target, m_c_ctx, m_w_ada, m_b_ada, m_norm1_g, m_norm2_g, m_w_in, m_ret_decay, m_conv_w, m_conv_b, m_lru_wa, m_lru_ba, m_lru_wx, m_lru_bx, m_lru_lambda, m_w_out, m_w_mlp1, m_w_mlp2, m_final_g, v_c_ctx, v_w_ada, v_b_ada, v_norm1_g, v_norm2_g, v_w_in, v_ret_decay, v_conv_w, v_conv_b, v_lru_wa, v_lru_ba, v_lru_wx, v_lru_bx, v_lru_lambda, v_w_out, v_w_mlp1, v_w_mlp2, v_final_g):
    t_len, d = x.shape[1], x.shape[2]
    l_len = ctx.shape[1]
    cb, cc = l_len // TM, l_len // CHUNK
    me = 4 * lax.axis_index("x") + 2 * lax.axis_index("y") + lax.axis_index("c")
    x2d, ctx2d, tgt2d = x[0], ctx[0], loss_target[0]
    ada_cols = w_ada.shape[2]
    wa2d = w_ada[0]

    sc_loc = conv_w.shape[2]
    pack_a = jnp.zeros((8, d), F32)
    pack_a = pack_a.at[0].set(_silu(c[0]))
    pack_a = pack_a.at[1, :4 * sc_loc].set(conv_w[0].reshape(-1))
    pack_a = pack_a.at[2, :2 * sc_loc].set(lru_ba[0].reshape(-1))
    pack_a = pack_a.at[3, :2 * sc_loc].set(lru_bx[0].reshape(-1))
    pack_a = pack_a.at[4, :2 * sc_loc].set(lru_lambda[0].reshape(-1))
    all_a = _all_gather(pack_a, "gather_small_in")
    s16 = jnp.zeros((16, d), F32).at[0:8].set(all_a[:, 0, :]).at[8].set(_silu(c_ctx))

    def unshard(row, k):
        return all_a[:, row, :k * sc_loc].reshape(N_DEV, k, sc_loc).transpose(1, 0, 2).reshape(k, N_DEV * sc_loc)

    conv_w_full = unshard(1, 4)
    ba_full, bx_full, lam_full = unshard(2, 2), unshard(3, 2), unshard(4, 2)

    b_cols = lax.dynamic_slice(b_ada, (0, me * ada_cols), (1, ada_cols))
    mod_parts = _all_gather(_mod_part(s16, wa2d, b_cols), "gather_mod")
    mod_all = mod_parts.transpose(1, 0, 2).reshape(16, N_DEV * ada_cols)
    mod_me = lax.dynamic_slice(mod_all, (me, 0), (1, 6 * d)).reshape(6, d)
    mod_c = mod_all[8].reshape(6, d)
    modrows = jnp.concatenate([mod_c[0:2], mod_me], axis=0)

    lg = jax.nn.log_sigmoid(ret_decay[0])
    sp = jax.nn.softplus(-lam_full)
    wg = [jnp.concatenate([_block_diag(lru_wa[0, dd]), _block_diag(lru_wx[0, dd])], axis=1).astype(BF16) for dd in (0, 1)]
    bg = [jnp.concatenate([ba_full[dd], bx_full[dd]])[None, :] for dd in (0, 1)]
    cosf, sins = _rot_tables(l_len, t_len)

    (hn, hnt), win_g = _norm1_fwd(ctx2d, x2d, norm1_g, modrows, cb, ride=("gather", w_in[0].astype(BF16)))
    (qkv, p), wout_g = _mm_in(hn, win_g, cosf, sins, ride=("gather", w_out[0].astype(BF16)))
    wout_g = wout_g.reshape(1, d, d)
    (o0, o1, sp0, sp1), w1_g = _ret_fwd(qkv, lg, cc, ride=("gather", w_mlp1[0].astype(BF16)))
    o, s_prev = [o0, o1], [sp0, sp1]
    (h0, cin0), _ = _lru_fwd(p, wg[0], bg[0], sp[0:1], conv_w_full, conv_b, 0, cb)
    (h1, cin1), _ = _lru_fwd(p, wg[1], bg[1], sp[1:2], conv_w_full, conv_b, 1, cb)
    h, cin = [h0, h1], [cin0, cin1]
    mix = _mix_fwd(o[0], o[1], p, h[0], h[1], cb, t_len)
    y = _mm_nn(mix, wout_g, F32, "mm_out", tm=TL)
    x1, h2, h2t = _res_norm2(x2d, y, norm2_g, modrows)
    r, w2_g = _mm_nn(h2, w1_g, BF16, "mm_mlp1", relu_out=True, ride=("gather", w_mlp2[0].astype(BF16)), tm=TL)
    w2_g = w2_g.reshape(1, 4 * d, d)
    dx2, dz, dzt, facc = _mm_nn(r, w2_g, F32, "mm_mlp2_final", square_lhs=True,
                                epi=_final_epilogue(x1, tgt2d, final_g[None, :], modrows, TM))

    du = _mm_nt(dz, w2_g, BF16, "mm_da2", relu_mul=r)
    gw2 = _mm_wgrad(dzt, r, "mm_dw2", w2_g.shape[1] // N_DEV, BF16, transpose_out=True, square_rhs=True)
    gw1_lo, gw1_hi = _mm_wgrad(h2t, du, "mm_dw1", w1_g.shape[2], BF16, halves=True)
    dx1, dy, n2acc = _mm_nt(du, w1_g, F32, "mm_dh2_norm2", epi=_bwd_norm2_epilogue(x1, dx2, y, norm2_g, modrows, TM))
    gwo = _mm_tn(mix, dy, "mm_dwout", False, d, BF16, 512).reshape(N_DEV, -1, d)
    do, dg, dgate, dhs = _mm_nt(dy, wout_g, F32, "mm_dmix_mix", epi=_mix_bwd_epilogue(o[0], o[1], p, h[0], h[1], cb))
    dxcs, dwgs, laccs, rides, gw1_all = [], [], [], [("a2a", gw1_lo), ("a2a", gw1_hi)], []
    (dq0, dk0, dv0, dq1, dk1, dv1, dlg_lanes), gw2_all = _ret_bwd(qkv, lg, do, s_prev, cc, ride=("a2a", gw2))
    dqs, dks, dvs = [dq0, dq1], [dk0, dk1], [dv0, dv1]
    for dd in (0, 1):
        (dxc_, dwg_, lacc_), got_ = _lru_bwd(p, wg[dd], bg[dd], sp[dd:dd + 1], conv_w_full, conv_b, h[dd], cin[dd], dhs, dd, cb,
                                             ride=rides[dd])
        dxcs.append(dxc_); dwgs.append(dwg_); laccs.append(lacc_); gw1_all.append(got_)
    (dp, cacc), gwo_all = _assemble_dp(dqs, dks, dvs, dg, dgate, dxcs, p, conv_w_full, cosf, sins, cb, ride=("a2a", gwo))
    pack_b = jnp.concatenate([n2acc, facc, cacc, laccs[0], laccs[1], dlg_lanes, dwgs[0], dwgs[1]], axis=0)
    gwi, all_b = _mm_wgrad(hnt, dp, "mm_dwin", win_g.shape[2], BF16, ride=("gather", pack_b))
    (grad_x, n1acc), gwi_all = _mm_nt(dp, win_g, F32, "mm_dhn_norm1", ride=("a2a", gwi),
                                      epi=_bwd_norm1_epilogue(ctx2d, x2d, dx1, norm1_g, modrows, cb))
    all_n1 = _all_gather(n1acc, "gather_norm1_grads")
    tot = _sum_parts(all_b, "sum_small_grads")
    t_n1 = _sum_parts(all_n1, "sum_norm1_grads")
    t_n2, t_f, t_conv, t_dlg = tot[0:8], tot[8:16], tot[16:24, :LRU_W], tot[40:48, :HEAD_DIM]
    t_l = [tot[24:32], tot[32:40]]
    t_dwd = [tot[48:48 + LRU_BD], tot[48 + LRU_BD:48 + 2 * LRU_BD]]
    loss = (0.5 / d) * jnp.sum(t_f[2])
    t_wa = jnp.stack([_blocks_from_lanes(t_dwd[dd][:, :LRU_W]) for dd in (0, 1)])
    t_wx = jnp.stack([_blocks_from_lanes(t_dwd[dd][:, LRU_W:]) for dd in (0, 1)])
    t_ba = jnp.stack([t_l[dd][0, :LRU_W] for dd in (0, 1)])
    t_bx = jnp.stack([t_l[dd][0, LRU_W:] for dd in (0, 1)])
    t_sp = jnp.stack([t_l[dd][1, :LRU_W] for dd in (0, 1)])
    dm_rows = jnp.stack([all_n1[:, 2, :], all_n1[:, 3, :], all_b[:, 3, :], all_b[:, 0, :], all_b[:, 1, :], all_b[:, 9, :]],
                        axis=1).reshape(N_DEV, 6 * d)
    dm_c = jnp.concatenate([t_n1[0], t_n1[1], jnp.zeros((4 * d,), F32)])
    dm16 = jnp.zeros((16, 6 * d), F32).at[0:8].set(dm_rows).at[8].set(dm_c)
    g_b_ada = jnp.sum(dm16, axis=0)[None, :]
    dm_cols = lax.dynamic_slice(dm16, (0, me * ada_cols), (16, ada_cols))
    g_w_ada, ds16 = _ada_bwd(s16, dm_cols, wa2d)
    ds_all = _all_gather(ds16[8:16], "gather_dsilu")
    dsilu_cc = _sum_parts(ds_all, "sum_dsilu")[0]
    sg_cc = jax.nn.sigmoid(c_ctx)
    g_c_ctx = dsilu_cc * (sg_cc * (1.0 + c_ctx * (1.0 - sg_cc)))

    g_ret_decay = jnp.sum(t_dlg, axis=-1).reshape(2, HEADS) * jax.nn.sigmoid(-ret_decay[0])
    g_lambda_full = -t_sp * jax.nn.sigmoid(-lam_full)

    def my_cols(full):
        return lax.dynamic_slice(full, (0, me * sc_loc), (full.shape[0], sc_loc))

    small_g = dict(
        c_ctx=g_c_ctx[None], b_ada=g_b_ada, norm1_g=t_n1[4:5], norm2_g=t_n2[2:3], ret_decay=g_ret_decay,
        conv_w=my_cols(t_conv[0:4]), conv_b=t_conv[4:5], lru_wa=t_wa.reshape(-1, LRU_BD), lru_ba=my_cols(t_ba),
        lru_wx=t_wx.reshape(-1, LRU_BD), lru_bx=my_cols(t_bx), lru_lambda=my_cols(g_lambda_full), final_g=t_f[0:1])
    small = dict(
        c_ctx=(c_ctx, m_c_ctx, v_c_ctx), b_ada=(b_ada, m_b_ada, v_b_ada), norm1_g=(norm1_g, m_norm1_g, v_norm1_g),
        norm2_g=(norm2_g, m_norm2_g, v_norm2_g), ret_decay=(ret_decay, m_ret_decay, v_ret_decay),
        conv_w=(conv_w, m_conv_w, v_conv_w), conv_b=(conv_b, m_conv_b, v_conv_b), lru_wa=(lru_wa, m_lru_wa, v_lru_wa),
        lru_ba=(lru_ba, m_lru_ba, v_lru_ba), lru_wx=(lru_wx, m_lru_wx, v_lru_wx), lru_bx=(lru_bx, m_lru_bx, v_lru_bx),
        lru_lambda=(lru_lambda, m_lru_lambda, v_lru_lambda), final_g=(final_g, m_final_g, v_final_g))
    names = list(small)
    items = [(small_g[k],) + tuple(a.reshape(small_g[k].shape) for a in small[k]) for k in names]
    res = {}
    for k, it, (d_, m_, v_) in zip(names, items, _adam_many(items, "adam_small")):
        shape = small[k][0].shape
        res[k] = tuple(a.reshape(shape) for a in (it[0], d_, m_, v_))

    def big(parts, w, m, v, name):
        out = _sum_adam(parts, w[0], m[0], v[0], name)
        return tuple(a[None] for a in out)

    res["w_ada"] = big([g_w_ada[None]], w_ada, m_w_ada, v_w_ada, "adam_w_ada")
    res["w_in"] = big([gwi_all], w_in, m_w_in, v_w_in, "adam_w_in")
    res["w_out"] = big([gwo_all], w_out, m_w_out, v_w_out, "adam_w_out")
    res["w_mlp1"] = big(gw1_all, w_mlp1, m_w_mlp1, v_w_mlp1, "adam_w_mlp1")
    res["w_mlp2"] = big([gw2_all], w_mlp2, m_w_mlp2, v_w_mlp2, "adam_w_mlp2")

    order = ["c_ctx", "w_ada", "b_ada", "norm1_g", "norm2_g", "w_in", "ret_decay", "conv_w", "conv_b", "lru_wa", "lru_ba",
             "lru_wx", "lru_bx", "lru_lambda", "w_out", "w_mlp1", "w_mlp2", "final_g"]
    outs = [loss, grad_x[None]]
    for j in range(4):
        outs += [res[k][j] for k in order]
    return tuple(outs)
```

```python
import jax
import jax.numpy as jnp
from jax import lax
from jax.experimental import pallas as pl
from jax.experimental.pallas import tpu as pltpu

F32 = jnp.float32
BF16 = jnp.bfloat16
AXES = ("x", "y", "c")
N_DEV = 8
MESH = pl.DeviceIdType.MESH

HEADS = 4
HEAD_DIM = 128
CHUNK = 128
RET_W = HEADS * HEAD_DIM
LRU_W = 512
LRU_BLOCKS = 8
LRU_BD = LRU_W // LRU_BLOCKS
LRU_C = 8.0
EPS = 1e-6
K_SCALE = HEAD_DIM ** -0.5
ROPE_BASE = 10000.0
GRID_W = 64
TM = 256
TL = 512
SUB = 8

ADAM_LR = 0.001
ADAM_B1 = 0.9
ADAM_B2 = 0.999
ADAM_EPS = 1e-08
ADAM_WD = 0.01
ADAM_STEP = 10

COL_G, COL_XR, COL_GATE = 0, 1, 2

R_CSH1, R_CSC1, R_SH1, R_SC1, R_G1, R_SH2, R_SC2, R_G2 = range(8)


def _pcall(body, **kw):
    return pl.pallas_call(body, **kw)


def _cp(vmem_mb=48):
    return pltpu.CompilerParams(vmem_limit_bytes=vmem_mb << 20)


def _sds(shape, dtype):
    return jax.ShapeDtypeStruct(shape, dtype)


def _dot(a, b):
    return jnp.dot(a.astype(BF16), b.astype(BF16), preferred_element_type=F32)


def _dot_nt(a, b):
    return lax.dot_general(a.astype(BF16), b.astype(BF16), (((1,), (1,)), ((), ())), preferred_element_type=F32)


def _dot_tn(a, b):
    return lax.dot_general(a.astype(BF16), b.astype(BF16), (((0,), (0,)), ((), ())), preferred_element_type=F32)


def _sigmoid(x):
    return jax.nn.sigmoid(x)


def _gelu(x):
    return 0.5 * x * (1.0 + jnp.tanh(0.7978845608028654 * (x + 0.044715 * x * x * x)))


def _dgelu(x):
    t = jnp.tanh(0.7978845608028654 * (x + 0.044715 * x * x * x))
    return 0.5 * (1.0 + t) + 0.5 * x * (1.0 - t * t) * 0.7978845608028654 * (1.0 + 3.0 * 0.044715 * x * x)


def _rows_iota(shape):
    return lax.broadcasted_iota(jnp.int32, shape, 0)


def _tile_order(dirn, s, cb, nb):
    if dirn == 0:
        return s
    return jnp.where(s < cb, cb - 1 - s, nb - 1 - (s - cb))


_SEMS = [pltpu.SemaphoreType.DMA((7,)), pltpu.SemaphoreType.DMA((7,)), pltpu.SemaphoreType.DMA(())]
_ANY = pl.BlockSpec(memory_space=pl.ANY)


def _gather_copies(x_ref, out_ref, send_sems, recv_sems, local_sem):
    mx, my, mc = lax.axis_index("x"), lax.axis_index("y"), lax.axis_index("c")
    me, sibling = (mx, my, mc), (mx, my, 1 - mc)
    chips = [(1 - mx, my), (mx, 1 - my), (1 - mx, 1 - my)]

    def slot(px, py, pc):
        return out_ref.at[4 * px + 2 * py + pc]

    def copy(k, block, to, src=None):
        return pltpu.make_async_remote_copy(
            src_ref=slot(*block) if src is None else src, dst_ref=slot(*block),
            send_sem=send_sems.at[k], recv_sem=recv_sems.at[k], device_id=to, device_id_type=MESH)

    mine = pltpu.make_async_copy(x_ref, slot(*me), local_sem)
    first = [copy(0, me, sibling, src=x_ref)] + [copy(1 + j, me, (*chip, mc), src=x_ref) for j, chip in enumerate(chips)]
    passed = [copy(4 + j, (*chip, mc), sibling) for j, chip in enumerate(chips)]
    recv_ici = [copy(1 + j, (*chip, mc), me) for j, chip in enumerate(chips)]
    recv_d2d = [copy(0, sibling, me)] + [copy(4 + j, (*chip, 1 - mc), me) for j, chip in enumerate(chips)]
    return mine, first, passed, recv_ici, recv_d2d


def _gather_start(*refs):
    mine, first, _, _, _ = _gather_copies(*refs)
    mine.start()
    for cp in first:
        cp.start()


def _gather_finish(*refs):
    mine, first, passed, recv_ici, recv_d2d = _gather_copies(*refs)
    for landed, onward in zip(recv_ici, passed):
        landed.wait_recv()
        onward.start()
    for landed in recv_d2d:
        landed.wait_recv()
    for cp in first + passed:
        cp.wait_send()
    mine.wait()


def _a2a_copies(g_ref, out_ref, send_sems, recv_sems, local_sem):
    mx, my, mc = lax.axis_index("x"), lax.axis_index("y"), lax.axis_index("c")
    me = 4 * mx + 2 * my + mc
    mine = pltpu.make_async_copy(g_ref.at[me], out_ref.at[me], local_sem)
    copies = []
    for k in range(1, N_DEV):
        px = 1 - mx if (k >> 2) & 1 else mx
        py = 1 - my if (k >> 1) & 1 else my
        pc = 1 - mc if k & 1 else mc
        copies.append(pltpu.make_async_remote_copy(
            src_ref=g_ref.at[4 * px + 2 * py + pc], dst_ref=out_ref.at[me],
            send_sem=send_sems.at[k - 1], recv_sem=recv_sems.at[k - 1],
            device_id=(px, py, pc), device_id_type=MESH))
    return mine, copies


def _a2a_start(*refs):
    mine, copies = _a2a_copies(*refs)
    mine.start()
    for cp in copies:
        cp.start()


def _a2a_finish(*refs):
    mine, copies = _a2a_copies(*refs)
    for cp in copies:
        cp.wait()
    mine.wait()


_EXCHANGES = {"gather": (_gather_start, _gather_finish), "a2a": (_a2a_start, _a2a_finish)}


def _exchange_shape(kind, src):
    return _sds((N_DEV,) + src.shape if kind == "gather" else src.shape, src.dtype)


def _all_gather(x, name):
    def body(x_ref, out_ref, *sems):
        _gather_start(x_ref, out_ref, *sems)
        _gather_finish(x_ref, out_ref, *sems)

    return _pcall(body, name=name, out_shape=_exchange_shape("gather", x), in_specs=[_ANY], out_specs=_ANY,
                  scratch_shapes=list(_SEMS))(x)


def _pcall_ride(body, ride, args, *, name, grid, in_specs, out_specs, out_shape, scratch_shapes=(), compiler_params=None):
    if ride is None:
        out = _pcall(body, name=name, grid=grid, in_specs=in_specs, out_specs=out_specs, out_shape=out_shape,
                     scratch_shapes=list(scratch_shapes), compiler_params=compiler_params)(*args)
        return out, None
    kind, src = ride
    start, finish = _EXCHANGES[kind]
    single = not isinstance(out_shape, (tuple, list))
    out_specs_t = (out_specs,) if single else tuple(out_specs)
    out_shape_t = (out_shape,) if single else tuple(out_shape)
    n_in, n_out, n_sc = len(in_specs), len(out_shape_t), len(scratch_shapes)

    def wrapped(*refs):
        ins, src_ref = refs[:n_in], refs[n_in]
        outs, dst_ref = refs[n_in + 1:n_in + 1 + n_out], refs[n_in + 1 + n_out]
        scratch = refs[n_in + 2 + n_out:n_in + 2 + n_out + n_sc]
        sems = refs[n_in + 2 + n_out + n_sc:]
        first = pl.program_id(0) == 0
        last = pl.program_id(0) == grid[0] - 1
        for ax in range(1, len(grid)):
            first = jnp.logical_and(first, pl.program_id(ax) == 0)
            last = jnp.logical_and(last, pl.program_id(ax) == grid[ax] - 1)

        @pl.when(first)
        def _():
            start(src_ref, dst_ref, *sems)

        body(*ins, *outs, *scratch)

        @pl.when(last)
        def _():
            finish(src_ref, dst_ref, *sems)

    res = _pcall(wrapped, name=name, grid=grid, in_specs=list(in_specs) + [_ANY], out_specs=out_specs_t + (_ANY,),
                 out_shape=out_shape_t + (_exchange_shape(kind, src),),
                 scratch_shapes=list(scratch_shapes) + list(_SEMS), compiler_params=compiler_params)(*args, src)
    return (res[0] if single else tuple(res[:-1])), res[-1]


SUB_ROWS = TM


class _Epilogue:
    def __init__(self, fn, args, in_specs, out_specs, out_shape, steps=None, lhs_map=None):
        self.fn, self.args, self.in_specs, self.out_specs, self.out_shape = fn, tuple(args), list(in_specs), out_specs, out_shape
        self.steps, self.lhs_map = steps, lhs_map


def _mm_nn(a, w, out_dtype, name, square_lhs=False, relu_out=False, ride=None, tm=TM, epi=None):
    m, k = a.shape
    nb, _, bn = w.shape
    tm = min(tm, m)
    n_extra = 0 if epi is None else len(epi.args)

    def body(*refs):
        a_ref, w_ref = refs[:2]
        av = a_ref[...]
        if square_lhs:
            av = av * av
        if epi is not None:
            assert nb == 1
            for r0 in range(0, tm, SUB_ROWS):
                rows = slice(r0, r0 + SUB_ROWS)
                epi.fn(jnp.dot(av[rows], w_ref[0], preferred_element_type=F32), rows, *refs[2:])
            return
        for j in range(nb):
            r = jnp.dot(av, w_ref[j], preferred_element_type=F32)
            if relu_out:
                r = jnp.maximum(r, 0.0)
            refs[2][:, j * bn:(j + 1) * bn] = r.astype(out_dtype)

    in_specs = [pl.BlockSpec((tm, k), lambda i: (i, 0)), pl.BlockSpec((nb, k, bn), lambda i: (0, 0, 0))]
    if epi is None:
        args, out_specs, out_shape = (a, w), pl.BlockSpec((tm, nb * bn), lambda i: (i, 0)), _sds((m, nb * bn), out_dtype)
    else:
        args, out_specs, out_shape = (a, w) + epi.args, epi.out_specs, epi.out_shape
        in_specs += epi.in_specs
    out, ex = _pcall_ride(body, ride, args, name=name, grid=(m // tm,), in_specs=in_specs, out_specs=out_specs,
                          out_shape=out_shape, compiler_params=_cp())
    return out if ride is None else (out, ex)


def _mm_nt(dy, w, out_dtype, name, relu_mul=None, ride=None, tm=TM, epi=None):
    m = dy.shape[0]
    nb, k, bn = w.shape
    tm = min(tm, m)
    n_extra = (0 if relu_mul is None else 1) + (0 if epi is None else len(epi.args))

    def body(*refs):
        dy_ref, w_ref = refs[:2]
        extra, outs, wt = refs[2:2 + n_extra], refs[2 + n_extra:-1], refs[-1]

        @pl.when(pl.program_id(0) == 0)
        def _():
            for j in range(nb):
                wt[j * bn:(j + 1) * bn, :] = w_ref[j].T

        if epi is not None:
            for r0 in range(0, tm, SUB_ROWS):
                rows = slice(r0, r0 + SUB_ROWS)
                epi.fn(jnp.dot(dy_ref[rows, :], wt[...], preferred_element_type=F32), rows, *extra, *outs)
            return
        acc = jnp.dot(dy_ref[...], wt[...], preferred_element_type=F32)
        if relu_mul is not None:
            acc = acc * (2.0 * extra[0][...].astype(F32))
        outs[0][...] = acc.astype(out_dtype)

    lhs_map = (lambda i: (i, 0)) if epi is None or epi.lhs_map is None else epi.lhs_map
    in_specs = [pl.BlockSpec((tm, nb * bn), lhs_map), pl.BlockSpec((nb, k, bn), lambda i: (0, 0, 0))]
    args = [dy, w]
    if relu_mul is not None:
        in_specs.append(pl.BlockSpec((tm, k), lambda i: (i, 0)))
        args.append(relu_mul)
    steps = m // tm
    if epi is None:
        out_specs, out_shape = pl.BlockSpec((tm, k), lambda i: (i, 0)), _sds((m, k), out_dtype)
    else:
        args += list(epi.args)
        in_specs += epi.in_specs
        out_specs, out_shape = epi.out_specs, epi.out_shape
        steps = steps if epi.steps is None else epi.steps
    out, ex = _pcall_ride(
        body, ride, args, name=name, grid=(steps,), in_specs=in_specs, out_specs=out_specs, out_shape=out_shape,
        scratch_shapes=[pltpu.VMEM((nb * bn, k), BF16)], compiler_params=_cp())
    return out if ride is None else (out, ex)


def _mm_tn(a, b, name, col_blocks, block, out_dtype, tm, square_lhs=False):
    m, k = a.shape
    nn = b.shape[1]
    steps = m // tm
    if col_blocks:
        nblk, acc_shape = nn // block, (k, block)
        a_spec = pl.BlockSpec((tm, k), lambda j, s: (s, 0))
        b_spec = pl.BlockSpec((tm, block), lambda j, s: (s, j))
    else:
        nblk, acc_shape = k // block, (block, nn)
        a_spec = pl.BlockSpec((tm, block), lambda j, s: (s, j))
        b_spec = pl.BlockSpec((tm, nn), lambda j, s: (s, 0))

    def body(a_ref, b_ref, o_ref, acc):
        s = pl.program_id(1)

        @pl.when(s == 0)
        def _():
            acc[...] = jnp.zeros_like(acc)

        av = a_ref[...]
        if square_lhs:
            av = av.astype(F32)
            av = (av * av).astype(BF16)
        acc[...] += _dot_tn(av, b_ref[...])

        @pl.when(s == steps - 1)
        def _():
            o_ref[...] = acc[...].astype(out_dtype)

    return _pcall(
        body, name=name, grid=(nblk, steps), in_specs=[a_spec, b_spec],
        out_specs=pl.BlockSpec((None,) + acc_shape, lambda j, s: (j, 0, 0)),
        out_shape=_sds((nblk,) + acc_shape, out_dtype),
        scratch_shapes=[pltpu.VMEM(acc_shape, F32)], compiler_params=_cp(),
    )(a, b)


def _mm_wgrad(at, b, name, bn, out_dtype, transpose_out=False, square_rhs=False, halves=False, ride=None):
    k, m = at.shape
    nblk = b.shape[1] // bn
    rows, cols = (bn, k) if transpose_out else (k, bn)
    nout = 2 if halves else 1
    per = rows // nout

    def body(a_ref, b_ref, *o_refs):
        bv = b_ref[...]
        if square_rhs:
            bv = bv * bv
        r = jnp.dot(a_ref[...], bv, preferred_element_type=F32)
        r = (r.T if transpose_out else r).astype(out_dtype)
        for i, o_ref in enumerate(o_refs):
            o_ref[...] = r[i * per:(i + 1) * per, :]

    out, ex = _pcall_ride(
        body, ride, (at, b), name=name, grid=(nblk,),
        in_specs=[pl.BlockSpec((k, m), lambda j: (0, 0)), pl.BlockSpec((m, bn), lambda j: (0, j))],
        out_specs=tuple(pl.BlockSpec((None, per, cols), lambda j: (j, 0, 0)) for _ in range(nout)),
        out_shape=tuple(_sds((nblk, per, cols), out_dtype) for _ in range(nout)), compiler_params=_cp())
    out = out if halves else out[0]
    return out if ride is None else (out, ex)


def _mm_in(hn, w, cosf, sins, ride):
    m, k = hn.shape
    nb, _, bn = w.shape

    def body(a_ref, w_ref, c_ref, s_ref, qkv_ref, rest_ref, pt):
        av = a_ref[...]
        for j in range(nb):
            pt[:, j * bn:(j + 1) * bn] = jnp.dot(av, w_ref[j], preferred_element_type=F32)
        cf, ss = c_ref[...], s_ref[...]
        for h in range(HEADS):
            sq = slice(h * HEAD_DIM, (h + 1) * HEAD_DIM)
            sk = slice(RET_W + h * HEAD_DIM, RET_W + (h + 1) * HEAD_DIM)
            qkv_ref[:, sq] = _rot(pt[:, sq], cf, ss).astype(BF16)
            qkv_ref[:, sk] = (_rot(pt[:, sk], cf, ss) * K_SCALE).astype(BF16)
        qkv_ref[:, 2 * RET_W:] = pt[:, 2 * RET_W:3 * RET_W].astype(BF16)
        rest_ref[...] = pt[:, 3 * RET_W:]

    tab = pl.BlockSpec((TM, HEAD_DIM), lambda i: (i, 0))
    wide = pl.BlockSpec((TM, 3 * RET_W), lambda i: (i, 0))
    return _pcall_ride(
        body, ride, (hn, w, cosf, sins), name="mm_in", grid=(m // TM,),
        in_specs=[pl.BlockSpec((TM, k), lambda i: (i, 0)), pl.BlockSpec((nb, k, bn), lambda i: (0, 0, 0)), tab, tab],
        out_specs=(wide, wide), out_shape=(_sds((m, 3 * RET_W), BF16), _sds((m, nb * bn - 3 * RET_W), F32)),
        scratch_shapes=[pltpu.VMEM((TM, nb * bn), F32)], compiler_params=_cp())


def _mod_part(s16, w_ada, b_cols):
    def body(s_ref, w_ref, b_ref, o_ref):
        o_ref[...] = _dot(s_ref[...], w_ref[...]) + b_ref[...]

    return _pcall(body, name="mod_part", out_shape=_sds((s16.shape[0], w_ada.shape[1]), F32),
                  compiler_params=_cp())(s16, w_ada, b_cols)


def _ada_bwd(s16, dm_cols, w_ada):
    def body(s_ref, d_ref, w_ref, gw_ref, ds_ref):
        gw_ref[...] = _dot_tn(s_ref[...], d_ref[...])
        ds_ref[...] = _dot_nt(d_ref[...], w_ref[...])

    return _pcall(body, name="ada_bwd",
                  out_shape=(_sds(w_ada.shape, F32), _sds(s16.shape, F32)), compiler_params=_cp())(s16, dm_cols, w_ada)


def _norm1_fwd(ctx, x, g, modrows, cb, ride=None):
    l_len, d = ctx.shape
    nb = (l_len + x.shape[0]) // TM

    def body(ctx_ref, x_ref, g_ref, m_ref, o_ref, ot_ref):
        is_ctx = pl.program_id(0) < cb
        xin = jnp.where(is_ctx, ctx_ref[...], x_ref[...])
        sh = jnp.where(is_ctx, m_ref[R_CSH1:R_CSH1 + 1, :], m_ref[R_SH1:R_SH1 + 1, :])
        sc = jnp.where(is_ctx, m_ref[R_CSC1:R_CSC1 + 1, :], m_ref[R_SC1:R_SC1 + 1, :])
        ms = jnp.mean(xin * xin, axis=-1, keepdims=True)
        n = xin * lax.rsqrt(ms + EPS) * g_ref[...]
        hn = n * (1.0 + sc) + sh
        o_ref[...] = hn.astype(BF16)
        ot_ref[...] = hn.T.astype(BF16)

    return _pcall_ride(
        body, ride, (ctx, x, g, modrows), name="norm1_fwd", grid=(nb,),
        in_specs=[pl.BlockSpec((TM, d), lambda i: (jnp.minimum(i, cb - 1), 0)),
                  pl.BlockSpec((TM, d), lambda i: (jnp.maximum(i - cb, 0), 0)),
                  pl.BlockSpec((1, d), lambda i: (0, 0)), pl.BlockSpec((8, d), lambda i: (0, 0))],
        out_specs=(pl.BlockSpec((TM, d), lambda i: (i, 0)), pl.BlockSpec((d, TM), lambda i: (0, i))),
        out_shape=(_sds((nb * TM, d), BF16), _sds((d, nb * TM), BF16)), compiler_params=_cp())


def _mix_fwd(o_f, o_b, p, h_f, h_b, cb, t_len):
    def body(of_ref, ob_ref, g_ref, gate_ref, hf_ref, hb_ref, mix_ref):
        o = of_ref[...] + ob_ref[...]
        g = g_ref[...]
        sg = g * _sigmoid(g)
        for hh in range(HEADS):
            sl = slice(hh * HEAD_DIM, (hh + 1) * HEAD_DIM)
            oh = o[:, sl]
            yc = oh - jnp.mean(oh, axis=-1, keepdims=True)
            var = jnp.mean(yc * yc, axis=-1, keepdims=True)
            mix_ref[:, sl] = (sg[:, sl] * (yc * lax.rsqrt(var + EPS))).astype(BF16)
        mix_ref[:, RET_W:] = ((hf_ref[...] + hb_ref[...]) * _gelu(gate_ref[...])).astype(BF16)

    row = lambda i: (i + cb, 0)
    return _pcall(
        body, name="mix_fwd", grid=(t_len // TM,),
        in_specs=[pl.BlockSpec((TM, RET_W), row), pl.BlockSpec((TM, RET_W), row),
                  pl.BlockSpec((TM, RET_W), lambda i: (i + cb, COL_G)), pl.BlockSpec((TM, LRU_W), lambda i: (i + cb, COL_GATE)),
                  pl.BlockSpec((TM, LRU_W), row), pl.BlockSpec((TM, LRU_W), row)],
        out_specs=pl.BlockSpec((TM, RET_W + LRU_W), lambda i: (i, 0)),
        out_shape=_sds((t_len, RET_W + LRU_W), BF16), compiler_params=_cp(),
    )(o_f, o_b, p, p, h_f, h_b)


def _res_norm2_epilogue(x, g, modrows, tm):
    t_len, d = x.shape
    tm = min(tm, t_len)

    def fn(y, rows, x_ref, g_ref, m_ref, y_ref, x1_ref, h2_ref, h2t_ref):
        y_ref[rows, :] = y
        x1 = x_ref[rows, :] + m_ref[R_G1:R_G1 + 1, :] * y
        ms = jnp.mean(x1 * x1, axis=-1, keepdims=True)
        n = x1 * lax.rsqrt(ms + EPS) * g_ref[...]
        x1_ref[rows, :] = x1
        h2 = n * (1.0 + m_ref[R_SC2:R_SC2 + 1, :]) + m_ref[R_SH2:R_SH2 + 1, :]
        h2_ref[rows, :] = h2.astype(BF16)
        h2t_ref[:, rows] = h2.T.astype(BF16)

    t = pl.BlockSpec((tm, d), lambda i: (i, 0))
    return _Epilogue(
        fn, (x, g, modrows),
        in_specs=[t, pl.BlockSpec((1, d), lambda i: (0, 0)), pl.BlockSpec((8, d), lambda i: (0, 0))],
        out_specs=(t, t, t, pl.BlockSpec((d, tm), lambda i: (0, i))),
        out_shape=(_sds((t_len, d), F32), _sds((t_len, d), F32), _sds((t_len, d), BF16), _sds((d, t_len), BF16)))


def _zero_at_start(acc_ref, rows):
    if rows.start == 0:
        @pl.when(pl.program_id(0) == 0)
        def _():
            acc_ref[...] = jnp.zeros_like(acc_ref)


def _final_epilogue(x1, target, fg, modrows, tm):
    t_len, d = x1.shape

    def fn(z, rows, x1_ref, t_ref, fg_ref, m_ref, dx2_ref, dz_ref, dzt_ref, acc_ref):
        _zero_at_start(acc_ref, rows)
        g2 = m_ref[R_G2:R_G2 + 1, :]
        x2 = x1_ref[rows, :] + g2 * z
        rstd = lax.rsqrt(jnp.mean(x2 * x2, axis=-1, keepdims=True) + EPS)
        xh = x2 * rstd
        fg = fg_ref[...]
        e = xh * fg - t_ref[rows, :]
        dy = e * (1.0 / d)
        dxh = dy * fg
        dx2 = rstd * (dxh - xh * jnp.mean(dxh * xh, axis=-1, keepdims=True))
        dx2_ref[rows, :] = dx2
        dz = g2 * dx2
        dz_ref[rows, :] = dz.astype(BF16)
        dzt_ref[:, rows] = dz.T.astype(BF16)
        acc_ref[0:1, :] += jnp.sum(dy * xh, axis=0, keepdims=True)
        acc_ref[1:2, :] += jnp.sum(dx2 * z, axis=0, keepdims=True)
        acc_ref[2:3, :] += jnp.sum(e * e, axis=0, keepdims=True)

    t = pl.BlockSpec((tm, d), lambda i: (i, 0))
    return _Epilogue(
        fn, (x1, target, fg, modrows),
        in_specs=[t, t, pl.BlockSpec((1, d), lambda i: (0, 0)), pl.BlockSpec((8, d), lambda i: (0, 0))],
        out_specs=(t, t, pl.BlockSpec((d, tm), lambda i: (0, i)), pl.BlockSpec((8, d), lambda i: (0, 0))),
        out_shape=(_sds((t_len, d), F32), _sds((t_len, d), BF16), _sds((d, t_len), BF16), _sds((8, d), F32)))


def _bwd_norm2_epilogue(x1, dx2, y, g, modrows, tm):
    t_len, d = x1.shape

    def fn(dh2, rows, x1_ref, dx2_ref, y_ref, g_ref, m_ref, dx1_ref, dy_ref, acc_ref):
        _zero_at_start(acc_ref, rows)
        x1 = x1_ref[rows, :]
        rstd = lax.rsqrt(jnp.mean(x1 * x1, axis=-1, keepdims=True) + EPS)
        xh = x1 * rstd
        gn = g_ref[...]
        dn = dh2 * (1.0 + m_ref[R_SC2:R_SC2 + 1, :])
        dxh = dn * gn
        dx1 = dx2_ref[rows, :] + rstd * (dxh - xh * jnp.mean(dxh * xh, axis=-1, keepdims=True))
        dx1_ref[rows, :] = dx1
        dy_ref[rows, :] = (m_ref[R_G1:R_G1 + 1, :] * dx1).astype(BF16)
        acc_ref[0:1, :] += jnp.sum(dh2, axis=0, keepdims=True)
        acc_ref[1:2, :] += jnp.sum(dh2 * xh * gn, axis=0, keepdims=True)
        acc_ref[2:3, :] += jnp.sum(dn * xh, axis=0, keepdims=True)
        acc_ref[3:4, :] += jnp.sum(dx1 * y_ref[rows, :], axis=0, keepdims=True)

    t = pl.BlockSpec((tm, d), lambda i: (i, 0))
    return _Epilogue(
        fn, (x1, dx2, y, g, modrows),
        in_specs=[t, t, t, pl.BlockSpec((1, d), lambda i: (0, 0)), pl.BlockSpec((8, d), lambda i: (0, 0))],
        out_specs=(t, t, pl.BlockSpec((8, d), lambda i: (0, 0))),
        out_shape=(_sds((t_len, d), F32), _sds((t_len, d), BF16), _sds((8, d), F32)))


def _mix_bwd_epilogue(o_f, o_b, p, h_f, h_b, cb):
    n = o_f.shape[0]

    def fn(dm, rows, of_ref, ob_ref, g_ref, gate_ref, hf_ref, hb_ref, do_ref, dg_ref, dgate_ref, dhs_ref):
        is_ctx = pl.program_id(0) < cb

        @pl.when(is_ctx)
        def _():
            for r in (do_ref, dg_ref, dgate_ref, dhs_ref):
                r[rows, :] = jnp.zeros((rows.stop - rows.start, r.shape[1]), r.dtype)

        @pl.when(jnp.logical_not(is_ctx))
        def _():
            o = of_ref[rows, :] + ob_ref[rows, :]
            g = g_ref[rows, :]
            s = _sigmoid(g)
            sg = g * s
            dsg = s * (1.0 + g * (1.0 - s))
            for hh in range(HEADS):
                sl = slice(hh * HEAD_DIM, (hh + 1) * HEAD_DIM)
                oh = o[:, sl]
                yc = oh - jnp.mean(oh, axis=-1, keepdims=True)
                rs = lax.rsqrt(jnp.mean(yc * yc, axis=-1, keepdims=True) + EPS)
                gn = yc * rs
                dret = dm[:, sl]
                dgn = dret * sg[:, sl]
                dg_ref[rows, sl] = dret * gn * dsg[:, sl]
                do_ref[rows, sl] = rs * (dgn - jnp.mean(dgn, axis=-1, keepdims=True)
                                         - gn * jnp.mean(dgn * gn, axis=-1, keepdims=True))
            dlru = dm[:, RET_W:]
            gate = gate_ref[rows, :]
            dhs_ref[rows, :] = dlru * _gelu(gate)
            dgate_ref[rows, :] = dlru * (hf_ref[rows, :] + hb_ref[rows, :]) * _dgelu(gate)

    t = pl.BlockSpec((TM, RET_W), lambda i: (i, 0))
    return _Epilogue(
        fn, (o_f, o_b, p, p, h_f, h_b),
        in_specs=[t, t, pl.BlockSpec((TM, RET_W), lambda i: (i, COL_G)), pl.BlockSpec((TM, LRU_W), lambda i: (i, COL_GATE)), t, t],
        out_specs=(t, t, t, t), out_shape=tuple(_sds((n, RET_W), F32) for _ in range(4)),
        steps=n // TM, lhs_map=lambda i: (jnp.maximum(i - cb, 0), 0))


def _bwd_norm1_epilogue(ctx, x, dx1, g, modrows, cb):
    t_len, d = x.shape

    def fn(dh, rows, ctx_ref, x_ref, dx1_ref, g_ref, m_ref, gx_ref, acc_ref):
        is_ctx = pl.program_id(0) < cb
        _zero_at_start(acc_ref, rows)
        xin = jnp.where(is_ctx, ctx_ref[rows, :], x_ref[rows, :])
        sc = jnp.where(is_ctx, m_ref[R_CSC1:R_CSC1 + 1, :], m_ref[R_SC1:R_SC1 + 1, :])
        rstd = lax.rsqrt(jnp.mean(xin * xin, axis=-1, keepdims=True) + EPS)
        xh = xin * rstd
        gn = g_ref[...]
        dn = dh * (1.0 + sc)
        dxh = dn * gn
        gx_ref[rows, :] = dx1_ref[rows, :] + rstd * (dxh - xh * jnp.mean(dxh * xh, axis=-1, keepdims=True))
        s0 = jnp.sum(dh, axis=0, keepdims=True)
        s1 = jnp.sum(dh * xh * gn, axis=0, keepdims=True)
        acc_ref[4:5, :] += jnp.sum(dn * xh, axis=0, keepdims=True)

        @pl.when(is_ctx)
        def _():
            acc_ref[0:1, :] += s0
            acc_ref[1:2, :] += s1

        @pl.when(jnp.logical_not(is_ctx))
        def _():
            acc_ref[2:3, :] += s0
            acc_ref[3:4, :] += s1

    lat = pl.BlockSpec((TM, d), lambda i: (jnp.maximum(i - cb, 0), 0))
    return _Epilogue(
        fn, (ctx, x, dx1, g, modrows),
        in_specs=[pl.BlockSpec((TM, d), lambda i: (jnp.minimum(i, cb - 1), 0)),
                  lat, lat, pl.BlockSpec((1, d), lambda i: (0, 0)), pl.BlockSpec((8, d), lambda i: (0, 0))],
        out_specs=(lat, pl.BlockSpec((8, d), lambda i: (0, 0))),
        out_shape=(_sds((t_len, d), F32), _sds((8, d), F32)))


def _rot(x, cf, ss):
    return x * cf + pltpu.roll(x, HEAD_DIM // 2, 1) * ss


def _decay_exponents(dirn):
    ii = lax.broadcasted_iota(jnp.int32, (CHUNK, CHUNK), 0)
    jj = lax.broadcasted_iota(jnp.int32, (CHUNK, CHUNK), 1)
    rel = ii - jj if dirn == 0 else jj - ii
    pos = ii.astype(F32)
    if dirn == 0:
        cq, cs = pos + 1.0, (CHUNK - 1.0) - pos
    else:
        cq, cs = CHUNK - pos, pos
    return rel, jnp.maximum(rel, 0).astype(F32), cq, cs


def _store_decay(lg_ref, dec):
    for dirn in (0, 1):
        rel, relf, cq, cs = _decay_exponents(dirn)
        for h in range(HEADS):
            lgv = lg_ref[dirn, h]
            dec[dirn, h, 0] = jnp.where(rel >= 0, jnp.exp(lgv * relf), 0.0)
            dec[dirn, h, 1] = jnp.exp(lgv * cq)
            dec[dirn, h, 2] = jnp.exp(lgv * cs)


def _ret_rows(cc, nc, step_of):
    return [lambda s, dirn=dirn: _tile_order(dirn, step_of(s), cc, nc) for dirn in (0, 1)]


def _ret_in_specs(rows):
    specs = []
    for row in rows:
        specs += [pl.BlockSpec((CHUNK, RET_W), lambda s, o=o, row=row: (row(s), o)) for o in (0, 1, 2)]
    return specs


def _ret_fwd(qkv, lg, cc, ride=None):
    n = qkv.shape[0]
    nc = n // CHUNK
    rows = _ret_rows(cc, nc, lambda s: s)

    def body(lg_ref, q0, k0, v0, q1, k1, v1, o0, o1, sp0, sp1, st, dec):
        @pl.when(pl.program_id(0) == 0)
        def _():
            st[...] = jnp.zeros_like(st)
            _store_decay(lg_ref, dec)

        for dirn, (q_ref, k_ref, v_ref, o_ref, sp_ref) in enumerate(((q0, k0, v0, o0, sp0), (q1, k1, v1, o1, sp1))):
            for h in range(HEADS):
                sl = slice(h * HEAD_DIM, (h + 1) * HEAD_DIM)
                q, k, v = q_ref[:, sl], k_ref[:, sl], v_ref[:, sl]
                sp = st[dirn, h]
                sp_ref[h] = sp
                o_ref[:, sl] = _dot(_dot_nt(q, k) * dec[dirn, h, 0], v) + _dot(q * dec[dirn, h, 1], sp)
                st[dirn, h] = jnp.exp(lg_ref[dirn, h] * CHUNK) * sp + _dot_tn(k * dec[dirn, h, 2], v)

    o_specs = [pl.BlockSpec((CHUNK, RET_W), lambda s, row=row: (row(s), 0)) for row in rows]
    state = pl.BlockSpec((None, HEADS, CHUNK, HEAD_DIM), lambda s: (s, 0, 0, 0))
    return _pcall_ride(
        body, ride, (lg,) + (qkv,) * 6, name="ret_fwd", grid=(nc,),
        in_specs=[pl.BlockSpec(memory_space=pltpu.SMEM)] + _ret_in_specs(rows),
        out_specs=(o_specs[0], o_specs[1], state, state),
        out_shape=(_sds((n, RET_W), F32),) * 2 + (_sds((nc, HEADS, CHUNK, HEAD_DIM), F32),) * 2,
        scratch_shapes=[pltpu.VMEM((2, HEADS, CHUNK, HEAD_DIM), F32), pltpu.VMEM((2, HEADS, 3, CHUNK, CHUNK), F32)],
        compiler_params=_cp())


def _ret_bwd(qkv, lg, do, s_prev, cc, ride=None):
    n = qkv.shape[0]
    nc = n // CHUNK
    rows = _ret_rows(cc, nc, lambda s: nc - 1 - s)

    def body(lg_ref, q0, k0, v0, q1, k1, v1, do0, do1, sp0, sp1, dq0, dk0, dv0, dq1, dk1, dv1, dlg_ref, dst, dec):
        @pl.when(pl.program_id(0) == 0)
        def _():
            dst[...] = jnp.zeros_like(dst)
            dlg_ref[...] = jnp.zeros_like(dlg_ref)
            _store_decay(lg_ref, dec)

        for dirn, (q_ref, k_ref, v_ref, do_ref, sp_ref, dq_ref, dk_ref, dv_ref) in enumerate(
                ((q0, k0, v0, do0, sp0, dq0, dk0, dv0), (q1, k1, v1, do1, sp1, dq1, dk1, dv1))):
            _, relf, cq, cs = _decay_exponents(dirn)
            for h in range(HEADS):
                sl = slice(h * HEAD_DIM, (h + 1) * HEAD_DIM)
                q, k, v = q_ref[:, sl], k_ref[:, sl], v_ref[:, sl]
                dov = do_ref[:, sl]
                dm, wq, ws = dec[dirn, h, 0], dec[dirn, h, 1], dec[dirn, h, 2]
                a = _dot_nt(q, k)
                sp = sp_ref[h]
                dsn = dst[dirn, h]
                gc = jnp.exp(lg_ref[dirn, h] * CHUNK)
                g1 = _dot_nt(dov, sp)
                da = _dot_nt(dov, v) * dm
                dq = g1 * wq + _dot(da, k)
                h1 = _dot_nt(v, dsn)
                dk = _dot_tn(da, q) + h1 * ws
                dv_ref[:, sl] = _dot_tn(a * dm, dov) + _dot(k * ws, dsn)
                dst[dirn, h] = gc * dsn + _dot_tn(q * wq, dov)
                term = da * a * relf + q * g1 * (wq * cq) + k * h1 * (ws * cs) + sp * dsn * (CHUNK * gc)
                dlg_ref[dirn * HEADS + h:dirn * HEADS + h + 1, 0:HEAD_DIM] += jnp.sum(term, axis=0, keepdims=True)
                dq_ref[:, sl] = dq
                dk_ref[:, sl] = dk

    wide = [pl.BlockSpec((CHUNK, RET_W), lambda s, row=row: (row(s), 0)) for row in rows]
    state = pl.BlockSpec((None, HEADS, CHUNK, HEAD_DIM), lambda s: (nc - 1 - s, 0, 0, 0))
    return _pcall_ride(
        body, ride, (lg,) + (qkv,) * 6 + (do, do, s_prev[0], s_prev[1]), name="ret_bwd", grid=(nc,),
        in_specs=[pl.BlockSpec(memory_space=pltpu.SMEM)] + _ret_in_specs(rows) + wide + [state, state],
        out_specs=(wide[0],) * 3 + (wide[1],) * 3 + (pl.BlockSpec((2 * HEADS, 8 * HEAD_DIM), lambda s: (0, 0)),),
        out_shape=(_sds((n, RET_W), F32),) * 6 + (_sds((2 * HEADS, 8 * HEAD_DIM), F32),),
        scratch_shapes=[pltpu.VMEM((2, HEADS, CHUNK, HEAD_DIM), F32), pltpu.VMEM((2, HEADS, 3, CHUNK, CHUNK), F32)],
        compiler_params=_cp())


def _shift_rows(cur, prev8, next8, k, seg_start, seg_end):
    tm = cur.shape[0]
    rows = _rows_iota(cur.shape)
    if k < 0:
        out = pltpu.roll(cur, -k, 0)
        for j in range(-k):
            halo = jnp.where(seg_start, 0.0, prev8[SUB + k + j:SUB + k + j + 1, :])
            out = jnp.where(rows == j, halo, out)
    else:
        out = pltpu.roll(cur, tm - k, 0)
        for j in range(k):
            halo = jnp.where(seg_end, 0.0, next8[j:j + 1, :])
            out = jnp.where(rows == tm - k + j, halo, out)
    return out


def _seg_flags(t, cb, nb):
    return jnp.logical_or(t == 0, t == cb), jnp.logical_or(t == cb - 1, t == nb - 1)


def _halo_specs(tile_of, n_rows, col):
    per = TM // SUB
    return [pl.BlockSpec((TM, LRU_W), lambda s: (tile_of(s), col)),
            pl.BlockSpec((SUB, LRU_W), lambda s: (jnp.maximum(tile_of(s) * per - 1, 0), col)),
            pl.BlockSpec((SUB, LRU_W), lambda s: (jnp.minimum((tile_of(s) + 1) * per, n_rows // SUB - 1), col))]


def _lru_gates(xr, prev8, next8, seg_start, seg_end, cw_ref, cb_ref, wg_ref, bg_ref, sp_ref):
    xm1 = _shift_rows(xr, prev8, next8, -1, seg_start, seg_end)
    xp1 = _shift_rows(xr, prev8, next8, 1, seg_start, seg_end)
    xp2 = _shift_rows(xr, prev8, next8, 2, seg_start, seg_end)
    xc = cb_ref[...] + xm1 * cw_ref[0:1, :] + xr * cw_ref[1:2, :] + xp1 * cw_ref[2:3, :] + xp2 * cw_ref[3:4, :]
    pre = _dot(xc, wg_ref[...]) + bg_ref[...]
    r = _sigmoid(pre[:, :LRU_W])
    i = _sigmoid(pre[:, LRU_W:])
    la = (-LRU_C) * r * sp_ref[...]
    a = jnp.exp(la)
    th = jnp.tanh(la)
    sq = jnp.sqrt(-2.0 * th / (1.0 - th))
    return xc, r, i, a, sq


def _scan_tile(a, b, ascending, a_sc, b_sc, carry, out_ref):
    tm, w = a.shape
    nsub = tm // SUB
    a = a.reshape(nsub, SUB, w)
    b = b.reshape(nsub, SUB, w)
    r8 = lax.broadcasted_iota(jnp.int32, a.shape, 1)
    for k in (1, 2, 4):
        if ascending:
            m = r8 >= k
            a_s, b_s = pltpu.roll(a, k, 1), pltpu.roll(b, k, 1)
        else:
            m = r8 < SUB - k
            a_s, b_s = pltpu.roll(a, SUB - k, 1), pltpu.roll(b, SUB - k, 1)
        b = a * jnp.where(m, b_s, 0.0) + b
        a = a * jnp.where(m, a_s, 1.0)
    a_sc[...] = a.reshape(tm, w)
    b_sc[...] = b.reshape(tm, w)

    def step(j, c):
        off = pl.multiple_of((j if ascending else nsub - 1 - j) * SUB, SUB)
        hb = a_sc[pl.ds(off, SUB), :] * c + b_sc[pl.ds(off, SUB), :]
        out_ref[pl.ds(off, SUB), :] = hb
        last = hb[SUB - 1:SUB, :] if ascending else hb[0:1, :]
        return jnp.broadcast_to(last, c.shape)

    carry[...] = lax.fori_loop(0, nsub, step, carry[...])


def _lru_fwd(p, wg, bg, sp, cw, cbias, dirn, cb, ride=None):
    n = p.shape[0]
    nb = n // TM
    tile_of = lambda s: _tile_order(dirn, s, cb, nb)

    def body(x_ref, xp_ref, xn_ref, wg_ref, bg_ref, sp_ref, cw_ref, cb_ref, h_ref, cin_ref, carry, a_sc, b_sc):
        s = pl.program_id(0)

        @pl.when(s == 0)
        def _():
            carry[...] = jnp.zeros_like(carry)

        seg_start, seg_end = _seg_flags(tile_of(s), cb, nb)
        xc, r, i, a, sq = _lru_gates(x_ref[...], xp_ref[...], xn_ref[...], seg_start, seg_end,
                                     cw_ref, cb_ref, wg_ref, bg_ref, sp_ref)
        cin_ref[...] = carry[...]
        _scan_tile(a, sq * (i * xc), dirn == 0, a_sc, b_sc, carry, h_ref)

    full = lambda shape: pl.BlockSpec(shape, lambda s: (0,) * len(shape))
    return _pcall_ride(
        body, ride, (p, p, p, wg, bg, sp, cw, cbias), name=f"lru_fwd{dirn}", grid=(nb,),
        in_specs=_halo_specs(tile_of, n, COL_XR) + [full((LRU_W, 2 * LRU_W)), full((1, 2 * LRU_W)), full((1, LRU_W)),
                                               full((4, LRU_W)), full((1, LRU_W))],
        out_specs=(pl.BlockSpec((TM, LRU_W), lambda s: (tile_of(s), 0)),
                   pl.BlockSpec((None, SUB, LRU_W), lambda s: (tile_of(s), 0, 0))),
        out_shape=(_sds((n, LRU_W), F32), _sds((nb, SUB, LRU_W), F32)),
        scratch_shapes=[pltpu.VMEM((SUB, LRU_W), F32), pltpu.VMEM((TM, LRU_W), F32), pltpu.VMEM((TM, LRU_W), F32)],
        compiler_params=_cp())


def _lru_bwd(p, wg, bg, sp, cw, cbias, h, cin, dhs, dirn, cb, ride=None):
    n = p.shape[0]
    nb = n // TM
    tile_of = lambda s: _tile_order(dirn, nb - 1 - s, cb, nb)

    def body(x_ref, xp_ref, xn_ref, wg_ref, bg_ref, sp_ref, cw_ref, cb_ref, h_ref, cin_ref, dhs_ref,
             dxc_ref, dwd_ref, acc_ref, carry, a_sc, b_sc, mu_sc, dwg_ref):
        s = pl.program_id(0)

        @pl.when(s == 0)
        def _():
            carry[...] = jnp.zeros_like(carry)
            dwg_ref[...] = jnp.zeros_like(dwg_ref)
            acc_ref[...] = jnp.zeros_like(acc_ref)

        seg_start, seg_end = _seg_flags(tile_of(s), cb, nb)
        xc, r, i, a, sq = _lru_gates(x_ref[...], xp_ref[...], xn_ref[...], seg_start, seg_end,
                                     cw_ref, cb_ref, wg_ref, bg_ref, sp_ref)
        rows = _rows_iota(a.shape)
        hv = h_ref[...]
        dh = dhs_ref[...]
        mu_next = carry[0:1, :]
        _scan_tile(a, a * dh, dirn == 1, a_sc, b_sc, carry, mu_sc)
        mu = mu_sc[...]
        if dirn == 0:
            hprev = jnp.where(rows == 0, cin_ref[0:1, :], pltpu.roll(hv, 1, 0))
            lam = dh + jnp.where(rows == TM - 1, mu_next, pltpu.roll(mu, TM - 1, 0))
        else:
            hprev = jnp.where(rows == TM - 1, cin_ref[0:1, :], pltpu.roll(hv, TM - 1, 0))
            lam = dh + jnp.where(rows == 0, mu_next, pltpu.roll(mu, 1, 0))
        ds = lam * (i * xc)
        di = lam * (sq * xc)
        dla = lam * hprev * a - ds * (a * a) / jnp.maximum(sq, 1e-20)
        dpr = dla * ((-LRU_C) * sp_ref[...]) * r * (1.0 - r)
        dpi = di * i * (1.0 - i)
        dpre = jnp.concatenate([dpr, dpi], axis=1)
        dxc_ref[...] = lam * (sq * i) + _dot_nt(dpre, wg_ref[...])
        dwg_ref[...] += _dot_tn(xc, dpre)
        acc_ref[0:1, :] += jnp.sum(dpre, axis=0, keepdims=True)
        acc_ref[1:2, 0:LRU_W] += jnp.sum(dla * ((-LRU_C) * r), axis=0, keepdims=True)

        @pl.when(s == nb - 1)
        def _():
            low = lax.broadcasted_iota(jnp.int32, (LRU_BD, 2 * LRU_BD), 1) < LRU_BD
            for half in (0, LRU_W):
                for m in range(LRU_BLOCKS // 2):
                    lanes = slice(half + 2 * LRU_BD * m, half + 2 * LRU_BD * (m + 1))
                    even = dwg_ref[2 * m * LRU_BD:(2 * m + 1) * LRU_BD, lanes]
                    odd = dwg_ref[(2 * m + 1) * LRU_BD:(2 * m + 2) * LRU_BD, lanes]
                    dwd_ref[:, lanes] = jnp.where(low, even, odd)

    full = lambda shape: pl.BlockSpec(shape, lambda s: (0,) * len(shape))
    tile = pl.BlockSpec((TM, LRU_W), lambda s: (tile_of(s), 0))
    return _pcall_ride(
        body, ride, (p, p, p, wg, bg, sp, cw, cbias, h, cin, dhs), name=f"lru_bwd{dirn}", grid=(nb,),
        in_specs=_halo_specs(tile_of, n, COL_XR) + [full((LRU_W, 2 * LRU_W)), full((1, 2 * LRU_W)), full((1, LRU_W)),
                                               full((4, LRU_W)), full((1, LRU_W)), tile,
                                               pl.BlockSpec((None, SUB, LRU_W), lambda s: (tile_of(s), 0, 0)), tile],
        out_specs=(tile, full((LRU_BD, 2 * LRU_W)), full((8, 2 * LRU_W))),
        out_shape=(_sds((n, LRU_W), F32), _sds((LRU_BD, 2 * LRU_W), F32), _sds((8, 2 * LRU_W), F32)),
        scratch_shapes=[pltpu.VMEM((SUB, LRU_W), F32)] + [pltpu.VMEM((TM, LRU_W), F32)] * 3
        + [pltpu.VMEM((LRU_W, 2 * LRU_W), F32)],
        compiler_params=_cp())


def _assemble_dp(dqs, dks, dvs, dg, dgate, dxcs, p, cw, cosf, sins, cb, ride=None):
    n = p.shape[0]
    nb = n // TM
    tile_of = lambda s: s

    def body(dqf, dqb, dkf, dkb, dvf, dvb, dg_ref, dgate_ref, cf, pf, nf, cb_, pb, nb_, x_ref, xp_ref, xn_ref,
             cw_ref, cos_ref, sin_ref, dp_ref, acc_ref):
        s = pl.program_id(0)

        @pl.when(s == 0)
        def _():
            acc_ref[...] = jnp.zeros_like(acc_ref)

        seg_start, seg_end = _seg_flags(s, cb, nb)
        dq = dqf[...] + dqb[...]
        dk = dkf[...] + dkb[...]
        cosv, sinv = cos_ref[...], sin_ref[...]
        for h in range(HEADS):
            sl = slice(h * HEAD_DIM, (h + 1) * HEAD_DIM)
            sk = slice(RET_W + h * HEAD_DIM, RET_W + (h + 1) * HEAD_DIM)
            dp_ref[:, sl] = (dq[:, sl] * cosv + pltpu.roll(dq[:, sl] * sinv, HEAD_DIM // 2, 1)).astype(BF16)
            dp_ref[:, sk] = ((dk[:, sl] * cosv + pltpu.roll(dk[:, sl] * sinv, HEAD_DIM // 2, 1)) * K_SCALE).astype(BF16)
        dp_ref[:, 2 * RET_W:3 * RET_W] = (dvf[...] + dvb[...]).astype(BF16)
        dp_ref[:, 3 * RET_W:4 * RET_W] = dg_ref[...].astype(BF16)
        dxc = cf[...] + cb_[...]
        dprev = pf[...] + pb[...]
        dnext = nf[...] + nb_[...]
        dxr = (_shift_rows(dxc, dprev, dnext, 1, seg_start, seg_end) * cw_ref[0:1, :] + dxc * cw_ref[1:2, :]
               + _shift_rows(dxc, dprev, dnext, -1, seg_start, seg_end) * cw_ref[2:3, :]
               + _shift_rows(dxc, dprev, dnext, -2, seg_start, seg_end) * cw_ref[3:4, :])
        dp_ref[:, 4 * RET_W:4 * RET_W + LRU_W] = dxr.astype(BF16)
        dp_ref[:, 4 * RET_W + LRU_W:] = dgate_ref[...].astype(BF16)
        xr, xp, xn = x_ref[...], xp_ref[...], xn_ref[...]
        for j, k in enumerate((-1, 0, 1, 2)):
            xs = xr if k == 0 else _shift_rows(xr, xp, xn, k, seg_start, seg_end)
            acc_ref[j:j + 1, 0:LRU_W] += jnp.sum(dxc * xs, axis=0, keepdims=True)
        acc_ref[4:5, 0:LRU_W] += jnp.sum(dxc, axis=0, keepdims=True)

    t = pl.BlockSpec((TM, RET_W), lambda s: (s, 0))
    args = (dqs[0], dqs[1], dks[0], dks[1], dvs[0], dvs[1], dg, dgate, dxcs[0], dxcs[0], dxcs[0], dxcs[1], dxcs[1], dxcs[1],
            p, p, p, cw, cosf, sins)
    tab = pl.BlockSpec((TM, HEAD_DIM), lambda s: (s, 0))
    return _pcall_ride(
        body, ride, args, name="assemble_dp", grid=(nb,),
        in_specs=[t] * 8 + _halo_specs(tile_of, n, 0) * 2 + _halo_specs(tile_of, n, COL_XR)
        + [pl.BlockSpec((4, LRU_W), lambda s: (0, 0)), tab, tab],
        out_specs=(pl.BlockSpec((TM, 4 * RET_W + 2 * LRU_W), lambda s: (s, 0)), pl.BlockSpec((8, 2 * LRU_W), lambda s: (0, 0))),
        out_shape=(_sds((n, 4 * RET_W + 2 * LRU_W), BF16), _sds((8, 2 * LRU_W), F32)), compiler_params=_cp())


def _adamw(g, w, m, v):
    nm = ADAM_B1 * m + (1.0 - ADAM_B1) * g
    nv = ADAM_B2 * v + (1.0 - ADAM_B2) * (g * g)
    m_hat = nm / (1.0 - ADAM_B1 ** ADAM_STEP)
    v_hat = nv / (1.0 - ADAM_B2 ** ADAM_STEP)
    return (-ADAM_LR) * (m_hat / (jnp.sqrt(v_hat) + ADAM_EPS) + ADAM_WD * w), nm, nv


def _adam_many(items, name):
    n = len(items)

    def body(*refs):
        for i in range(n):
            g, w, m, v = (r[...] for r in refs[4 * i:4 * i + 4])
            for o_ref, val in zip(refs[4 * n + 3 * i:4 * n + 3 * i + 3], _adamw(g, w, m, v)):
                o_ref[...] = val

    out_shape = tuple(_sds(it[1].shape, F32) for it in items for _ in range(3))
    res = _pcall(body, name=name, out_shape=out_shape, compiler_params=_cp())(*[a for it in items for a in it])
    return [tuple(res[3 * i:3 * i + 3]) for i in range(n)]


def _sum_adam(parts_list, w, m, v, name):
    nparts, _, c = parts_list[0].shape
    r = w.shape[0]
    tr = min(parts_list[0].shape[1], 128)
    starts, o = [], 0
    for pa in parts_list:
        starts.append(o)
        o += pa.shape[1] // tr
    nseg = len(parts_list)

    def body(*refs):
        p_refs = refs[:nseg]
        w_ref, m_ref, v_ref, g_ref, d_ref, nm_ref, nv_ref = refs[nseg:]
        i = pl.program_id(0)
        for s, p_ref in enumerate(p_refs):
            end = starts[s + 1] if s + 1 < nseg else r // tr

            @pl.when(jnp.logical_and(i >= starts[s], i < end))
            def _():
                g = p_ref[0].astype(F32)
                for j in range(1, nparts):
                    g = g + p_ref[j].astype(F32)
                g_ref[...] = g
                d_ref[...], nm_ref[...], nv_ref[...] = _adamw(g, w_ref[...], m_ref[...], v_ref[...])

    def seg_spec(s):
        last = parts_list[s].shape[1] // tr - 1
        return pl.BlockSpec((nparts, tr, c), lambda i: (0, jnp.clip(i - starts[s], 0, last), 0))

    t = pl.BlockSpec((tr, c), lambda i: (i, 0))
    return _pcall(
        body, name=name, grid=(r // tr,),
        in_specs=[seg_spec(s) for s in range(nseg)] + [t, t, t],
        out_specs=(t, t, t, t), out_shape=(_sds((r, c), F32),) * 4, compiler_params=_cp(),
    )(*parts_list, w, m, v)


def _sum_parts(parts, name):
    nparts, r, c = parts.shape

    def body(p_ref, o_ref):
        g = p_ref[0]
        for j in range(1, nparts):
            g = g + p_ref[j]
        o_ref[...] = g

    return _pcall(body, name=name, out_shape=_sds((r, c), parts.dtype), compiler_params=_cp())(parts)


def _rot_tables(l_len, t_len):
    rows = t_len // GRID_W
    n_freq = HEAD_DIM // 4
    inv = ROPE_BASE ** (-jnp.arange(n_freq, dtype=F32) / n_freq)
    ang_r = jnp.arange(rows, dtype=F32)[:, None] * inv
    ang_c = jnp.arange(GRID_W, dtype=F32)[:, None] * inv
    cos = jnp.concatenate([jnp.repeat(jnp.cos(ang_r), GRID_W, axis=0), jnp.tile(jnp.cos(ang_c), (rows, 1))], axis=-1)
    sin = jnp.concatenate([jnp.repeat(jnp.sin(ang_r), GRID_W, axis=0), jnp.tile(jnp.sin(ang_c), (rows, 1))], axis=-1)
    cosf = jnp.concatenate([jnp.ones((l_len, HEAD_DIM), F32), jnp.concatenate([cos, cos], axis=-1)], axis=0)
    sins = jnp.concatenate([jnp.zeros((l_len, HEAD_DIM), F32), jnp.concatenate([-sin, sin], axis=-1)], axis=0)
    return cosf, sins


def _block_diag(w):
    eye = jnp.eye(LRU_BLOCKS, dtype=w.dtype)
    return (w[:, :, None, :] * eye[:, None, :, None]).reshape(LRU_W, LRU_W)


def _blocks_from_lanes(dwd_half):
    return dwd_half.reshape(LRU_BD, LRU_BLOCKS, LRU_BD).transpose(1, 0, 2)


def _silu(x):
    return x * jax.nn.sigmoid(x)


def kernel(x, c, ctx, c_ctx, w_ada, b_ada, norm1_g, norm2_g, w_in, ret_decay, conv_w, conv_b, lru_wa, lru_ba, lru_wx, lru_bx, lru_lambda, w_out, w_mlp1, w_mlp2, final_g, loss_target, m_c_ctx, m_w_ada, m_b_ada, m_norm1_g, m_norm2_g, m_w_in, m_ret_decay, m_conv_w, m_conv_b, m_lru_wa, m_lru_ba, m_lru_wx, m_lru_bx, m_lru_lambda, m_w_out, m_w_mlp1, m_w_mlp2, m_final_g, v_c_ctx, v_w_ada, v_b_ada, v_norm1_g, v_norm2_g, v_w_in, v_ret_decay, v_conv_w, v_conv_b, v_lru_wa, v_lru_ba, v_lru_wx, v_lru_bx, v_lru_lambda, v_w_out, v_w_mlp1, v_w_mlp2, v_final_g):
    t_len, d = x.shape[1], x.shape[2]
    l_len = ctx.shape[1]
    cb, cc = l_len // TM, l_len // CHUNK
    me = 4 * lax.axis_index("x") + 2 * lax.axis_index("y") + lax.axis_index("c")
    x2d, ctx2d, tgt2d = x[0], ctx[0], loss_target[0]
    ada_cols = w_ada.shape[2]
    wa2d = w_ada[0]

    sc_loc = conv_w.shape[2]
    pack_a = jnp.zeros((8, d), F32)
    pack_a = pack_a.at[0].set(_silu(c[0]))
    pack_a = pack_a.at[1, :4 * sc_loc].set(conv_w[0].reshape(-1))
    pack_a = pack_a.at[2, :2 * sc_loc].set(lru_ba[0].reshape(-1))
    pack_a = pack_a.at[3, :2 * sc_loc].set(lru_bx[0].reshape(-1))
    pack_a = pack_a.at[4, :2 * sc_loc].set(lru_lambda[0].reshape(-1))
    all_a = _all_gather(pack_a, "gather_small_in")
    s16 = jnp.zeros((16, d), F32).at[0:8].set(all_a[:, 0, :]).at[8].set(_silu(c_ctx))

    def unshard(row, k):
        return all_a[:, row, :k * sc_loc].reshape(N_DEV, k, sc_loc).transpose(1, 0, 2).reshape(k, N_DEV * sc_loc)

    conv_w_full = unshard(1, 4)
    ba_full, bx_full, lam_full = unshard(2, 2), unshard(3, 2), unshard(4, 2)

    b_cols = lax.dynamic_slice(b_ada, (0, me * ada_cols), (1, ada_cols))
    mod_parts = _all_gather(_mod_part(s16, wa2d, b_cols), "gather_mod")
    mod_all = mod_parts.transpose(1, 0, 2).reshape(16, N_DEV * ada_cols)
    mod_me = lax.dynamic_slice(mod_all, (me, 0), (1, 6 * d)).reshape(6, d)
    mod_c = mod_all[8].reshape(6, d)
    modrows = jnp.concatenate([mod_c[0:2], mod_me], axis=0)

    lg = jax.nn.log_sigmoid(ret_decay[0])
    sp = jax.nn.softplus(-lam_full)
    wg = [jnp.concatenate([_block_diag(lru_wa[0, dd]), _block_diag(lru_wx[0, dd])], axis=1).astype(BF16) for dd in (0, 1)]
    bg = [jnp.concatenate([ba_full[dd], bx_full[dd]])[None, :] for dd in (0, 1)]
    cosf, sins = _rot_tables(l_len, t_len)

    (hn, hnt), win_g = _norm1_fwd(ctx2d, x2d, norm1_g, modrows, cb, ride=("gather", w_in[0].astype(BF16)))
    (qkv, p), w1_g = _mm_in(hn, win_g, cosf, sins, ride=("gather", w_mlp1[0].astype(BF16)))
    (o0, o1, sp0, sp1), w2_g = _ret_fwd(qkv, lg, cc, ride=("gather", w_mlp2[0].astype(BF16)))
    w2_g = w2_g.reshape(1, 4 * d, d)
    o, s_prev = [o0, o1], [sp0, sp1]
    (h0, cin0), wout_g = _lru_fwd(p, wg[0], bg[0], sp[0:1], conv_w_full, conv_b, 0, cb, ride=("gather", w_out[0].astype(BF16)))
    wout_g = wout_g.reshape(1, d, d)
    (h1, cin1), _ = _lru_fwd(p, wg[1], bg[1], sp[1:2], conv_w_full, conv_b, 1, cb)
    h, cin = [h0, h1], [cin0, cin1]
    mix = _mix_fwd(o[0], o[1], p, h[0], h[1], cb, t_len)
    y, x1, h2, h2t = _mm_nn(mix, wout_g, F32, "mm_out_norm2", tm=TL, epi=_res_norm2_epilogue(x2d, norm2_g, modrows, TL))
    r = _mm_nn(h2, w1_g, BF16, "mm_mlp1", relu_out=True, tm=TL)
    dx2, dz, dzt, facc = _mm_nn(r, w2_g, F32, "mm_mlp2_final", square_lhs=True,
                                epi=_final_epilogue(x1, tgt2d, final_g[None, :], modrows, TM))

    du = _mm_nt(dz, w2_g, BF16, "mm_da2", relu_mul=r)
    gw2 = _mm_wgrad(dzt, r, "mm_dw2", w2_g.shape[1] // N_DEV, BF16, transpose_out=True, square_rhs=True)
    gw1_lo, gw1_hi = _mm_wgrad(h2t, du, "mm_dw1", w1_g.shape[2], BF16, halves=True)
    (dx1, dy, n2acc), gw1_lo_all = _mm_nt(du, w1_g, F32, "mm_dh2_norm2", ride=("a2a", gw1_lo),
                                          epi=_bwd_norm2_epilogue(x1, dx2, y, norm2_g, modrows, TM))
    gwo = _mm_tn(mix, dy, "mm_dwout", False, d, BF16, 512).reshape(N_DEV, -1, d)
    (do, dg, dgate, dhs), gw1_hi_all = _mm_nt(dy, wout_g, F32, "mm_dmix_mix", ride=("a2a", gw1_hi),
                                              epi=_mix_bwd_epilogue(o[0], o[1], p, h[0], h[1], cb))
    gw1_all = [gw1_lo_all, gw1_hi_all]
    dxcs, dwgs, laccs, rides = [], [], [], [("a2a", gwo), None]
    (dq0, dk0, dv0, dq1, dk1, dv1, dlg_lanes), gw2_all = _ret_bwd(qkv, lg, do, s_prev, cc, ride=("a2a", gw2))
    dqs, dks, dvs = [dq0, dq1], [dk0, dk1], [dv0, dv1]
    for dd in (0, 1):
        (dxc_, dwg_, lacc_), got_ = _lru_bwd(p, wg[dd], bg[dd], sp[dd:dd + 1], conv_w_full, conv_b, h[dd], cin[dd], dhs, dd, cb,
                                             ride=rides[dd])
        dxcs.append(dxc_); dwgs.append(dwg_); laccs.append(lacc_)
        gwo_all = got_ if dd == 0 else gwo_all
    (dp, cacc), _ = _assemble_dp(dqs, dks, dvs, dg, dgate, dxcs, p, conv_w_full, cosf, sins, cb)
    pack_b = jnp.concatenate([n2acc, facc, cacc, laccs[0], laccs[1], dlg_lanes, dwgs[0], dwgs[1]], axis=0)
    gwi, all_b = _mm_wgrad(hnt, dp, "mm_dwin", win_g.shape[2], BF16, ride=("gather", pack_b))
    (grad_x, n1acc), gwi_all = _mm_nt(dp, win_g, F32, "mm_dhn_norm1", ride=("a2a", gwi),
                                      epi=_bwd_norm1_epilogue(ctx2d, x2d, dx1, norm1_g, modrows, cb))
    all_n1 = _all_gather(n1acc, "gather_norm1_grads")
    tot = _sum_parts(all_b, "sum_small_grads")
    t_n1 = _sum_parts(all_n1, "sum_norm1_grads")
    t_n2, t_f, t_conv, t_dlg = tot[0:8], tot[8:16], tot[16:24, :LRU_W], tot[40:48, :HEAD_DIM]
    t_l = [tot[24:32], tot[32:40]]
    t_dwd = [tot[48:48 + LRU_BD], tot[48 + LRU_BD:48 + 2 * LRU_BD]]
    loss = (0.5 / d) * jnp.sum(t_f[2])
    t_wa = jnp.stack([_blocks_from_lanes(t_dwd[dd][:, :LRU_W]) for dd in (0, 1)])
    t_wx = jnp.stack([_blocks_from_lanes(t_dwd[dd][:, LRU_W:]) for dd in (0, 1)])
    t_ba = jnp.stack([t_l[dd][0, :LRU_W] for dd in (0, 1)])
    t_bx = jnp.stack([t_l[dd][0, LRU_W:] for dd in (0, 1)])
    t_sp = jnp.stack([t_l[dd][1, :LRU_W] for dd in (0, 1)])
    dm_rows = jnp.stack([all_n1[:, 2, :], all_n1[:, 3, :], all_b[:, 3, :], all_b[:, 0, :], all_b[:, 1, :], all_b[:, 9, :]],
                        axis=1).reshape(N_DEV, 6 * d)
    dm_c = jnp.concatenate([t_n1[0], t_n1[1], jnp.zeros((4 * d,), F32)])
    dm16 = jnp.zeros((16, 6 * d), F32).at[0:8].set(dm_rows).at[8].set(dm_c)
    g_b_ada = jnp.sum(dm16, axis=0)[None, :]
    dm_cols = lax.dynamic_slice(dm16, (0, me * ada_cols), (16, ada_cols))
    g_w_ada, ds16 = _ada_bwd(s16, dm_cols, wa2d)
    ds_all = _all_gather(ds16[8:16], "gather_dsilu")
    dsilu_cc = _sum_parts(ds_all, "sum_dsilu")[0]
    sg_cc = jax.nn.sigmoid(c_ctx)
    g_c_ctx = dsilu_cc * (sg_cc * (1.0 + c_ctx * (1.0 - sg_cc)))

    g_ret_decay = jnp.sum(t_dlg, axis=-1).reshape(2, HEADS) * jax.nn.sigmoid(-ret_decay[0])
    g_lambda_full = -t_sp * jax.nn.sigmoid(-lam_full)

    def my_cols(full):
        return lax.dynamic_slice(full, (0, me * sc_loc), (full.shape[0], sc_loc))

    small_g = dict(
        c_ctx=g_c_ctx[None], b_ada=g_b_ada, norm1_g=t_n1[4:5], norm2_g=t_n2[2:3], ret_decay=g_ret_decay,
        conv_w=my_cols(t_conv[0:4]), conv_b=t_conv[4:5], lru_wa=t_wa.reshape(-1, LRU_BD), lru_ba=my_cols(t_ba),
        lru_wx=t_wx.reshape(-1, LRU_BD), lru_bx=my_cols(t_bx), lru_lambda=my_cols(g_lambda_full), final_g=t_f[0:1])
    small = dict(
        c_ctx=(c_ctx, m_c_ctx, v_c_ctx), b_ada=(b_ada, m_b_ada, v_b_ada), norm1_g=(norm1_g, m_norm1_g, v_norm1_g),
        norm2_g=(norm2_g, m_norm2_g, v_norm2_g), ret_decay=(ret_decay, m_ret_decay, v_ret_decay),
        conv_w=(conv_w, m_conv_w, v_conv_w), conv_b=(conv_b, m_conv_b, v_conv_b), lru_wa=(lru_wa, m_lru_wa, v_lru_wa),
        lru_ba=(lru_ba, m_lru_ba, v_lru_ba), lru_wx=(lru_wx, m_lru_wx, v_lru_wx), lru_bx=(lru_bx, m_lru_bx, v_lru_bx),
        lru_lambda=(lru_lambda, m_lru_lambda, v_lru_lambda), final_g=(final_g, m_final_g, v_final_g))
    names = list(small)
    items = [(small_g[k],) + tuple(a.reshape(small_g[k].shape) for a in small[k]) for k in names]
    res = {}
    for k, it, (d_, m_, v_) in zip(names, items, _adam_many(items, "adam_small")):
        shape = small[k][0].shape
        res[k] = tuple(a.reshape(shape) for a in (it[0], d_, m_, v_))

    def big(parts, w, m, v, name):
        out = _sum_adam(parts, w[0], m[0], v[0], name)
        return tuple(a[None] for a in out)

    res["w_ada"] = big([g_w_ada[None]], w_ada, m_w_ada, v_w_ada, "adam_w_ada")
    res["w_in"] = big([gwi_all], w_in, m_w_in, v_w_in, "adam_w_in")
    res["w_out"] = big([gwo_all], w_out, m_w_out, v_w_out, "adam_w_out")
    res["w_mlp1"] = big(gw1_all, w_mlp1, m_w_mlp1, v_w_mlp1, "adam_w_mlp1")
    res["w_mlp2"] = big([gw2_all], w_mlp2, m_w_mlp2, v_w_mlp2, "adam_w_mlp2")

    order = ["c_ctx", "w_ada", "b_ada", "norm1_g", "norm2_g", "w_in", "ret_decay", "conv_w", "conv_b", "lru_wa", "lru_ba",
             "lru_wx", "lru_bx", "lru_lambda", "w_out", "w_mlp1", "w_mlp2", "final_g"]
    outs = [loss, grad_x[None]]
    for j in range(4):
        outs += [res[k][j] for k in order]
    return tuple(outs)
```

```python
import jax
import jax.numpy as jnp
from jax import lax
from jax.experimental import pallas as pl
from jax.experimental.pallas import tpu as pltpu

F32 = jnp.float32
BF16 = jnp.bfloat16
AXES = ("x", "y", "c")
N_DEV = 8
MESH = pl.DeviceIdType.MESH

HEADS = 4
HEAD_DIM = 128
CHUNK = 128
RET_W = HEADS * HEAD_DIM
LRU_W = 512
LRU_BLOCKS = 8
LRU_BD = LRU_W // LRU_BLOCKS
LRU_C = 8.0
EPS = 1e-6
K_SCALE = HEAD_DIM ** -0.5
ROPE_BASE = 10000.0
GRID_W = 64
TM = 256
TL = 512
SUB = 8

ADAM_LR = 0.001
ADAM_B1 = 0.9
ADAM_B2 = 0.999
ADAM_EPS = 1e-08
ADAM_WD = 0.01
ADAM_STEP = 10

COL_G, COL_XR, COL_GATE = 0, 1, 2

R_CSH1, R_CSC1, R_SH1, R_SC1, R_G1, R_SH2, R_SC2, R_G2 = range(8)


def _pcall(body, **kw):
    return pl.pallas_call(body, **kw)


def _cp(vmem_mb=48):
    return pltpu.CompilerParams(vmem_limit_bytes=vmem_mb << 20)


def _sds(shape, dtype):
    return jax.ShapeDtypeStruct(shape, dtype)


def _dot(a, b):
    return jnp.dot(a.astype(BF16), b.astype(BF16), preferred_element_type=F32)


def _dot_nt(a, b):
    return lax.dot_general(a.astype(BF16), b.astype(BF16), (((1,), (1,)), ((), ())), preferred_element_type=F32)


def _dot_tn(a, b):
    return lax.dot_general(a.astype(BF16), b.astype(BF16), (((0,), (0,)), ((), ())), preferred_element_type=F32)


def _sigmoid(x):
    return jax.nn.sigmoid(x)


def _gelu(x):
    return 0.5 * x * (1.0 + jnp.tanh(0.7978845608028654 * (x + 0.044715 * x * x * x)))


def _dgelu(x):
    t = jnp.tanh(0.7978845608028654 * (x + 0.044715 * x * x * x))
    return 0.5 * (1.0 + t) + 0.5 * x * (1.0 - t * t) * 0.7978845608028654 * (1.0 + 3.0 * 0.044715 * x * x)


def _rows_iota(shape):
    return lax.broadcasted_iota(jnp.int32, shape, 0)


def _tile_order(dirn, s, cb, nb):
    if dirn == 0:
        return s
    return jnp.where(s < cb, cb - 1 - s, nb - 1 - (s - cb))


_SEMS = [pltpu.SemaphoreType.DMA((7,)), pltpu.SemaphoreType.DMA((7,)), pltpu.SemaphoreType.DMA(())]
_ANY = pl.BlockSpec(memory_space=pl.ANY)


def _gather_copies(x_ref, out_ref, send_sems, recv_sems, local_sem):
    mx, my, mc = lax.axis_index("x"), lax.axis_index("y"), lax.axis_index("c")
    me, sibling = (mx, my, mc), (mx, my, 1 - mc)
    chips = [(1 - mx, my), (mx, 1 - my), (1 - mx, 1 - my)]

    def slot(px, py, pc):
        return out_ref.at[4 * px + 2 * py + pc]

    def copy(k, block, to, src=None):
        return pltpu.make_async_remote_copy(
            src_ref=slot(*block) if src is None else src, dst_ref=slot(*block),
            send_sem=send_sems.at[k], recv_sem=recv_sems.at[k], device_id=to, device_id_type=MESH)

    mine = pltpu.make_async_copy(x_ref, slot(*me), local_sem)
    first = [copy(0, me, sibling, src=x_ref)] + [copy(1 + j, me, (*chip, mc), src=x_ref) for j, chip in enumerate(chips)]
    passed = [copy(4 + j, (*chip, mc), sibling) for j, chip in enumerate(chips)]
    recv_ici = [copy(1 + j, (*chip, mc), me) for j, chip in enumerate(chips)]
    recv_d2d = [copy(0, sibling, me)] + [copy(4 + j, (*chip, 1 - mc), me) for j, chip in enumerate(chips)]
    return mine, first, passed, recv_ici, recv_d2d


def _gather_start(*refs):
    mine, first, _, _, _ = _gather_copies(*refs)
    mine.start()
    for cp in first:
        cp.start()


def _gather_finish(*refs):
    mine, first, passed, recv_ici, recv_d2d = _gather_copies(*refs)
    for landed, onward in zip(recv_ici, passed):
        landed.wait_recv()
        onward.start()
    for landed in recv_d2d:
        landed.wait_recv()
    for cp in first + passed:
        cp.wait_send()
    mine.wait()


def _a2a_copies(g_ref, out_ref, send_sems, recv_sems, local_sem):
    mx, my, mc = lax.axis_index("x"), lax.axis_index("y"), lax.axis_index("c")
    me = 4 * mx + 2 * my + mc
    mine = pltpu.make_async_copy(g_ref.at[me], out_ref.at[me], local_sem)
    copies = []
    for k in range(1, N_DEV):
        px = 1 - mx if (k >> 2) & 1 else mx
        py = 1 - my if (k >> 1) & 1 else my
        pc = 1 - mc if k & 1 else mc
        copies.append(pltpu.make_async_remote_copy(
            src_ref=g_ref.at[4 * px + 2 * py + pc], dst_ref=out_ref.at[me],
            send_sem=send_sems.at[k - 1], recv_sem=recv_sems.at[k - 1],
            device_id=(px, py, pc), device_id_type=MESH))
    return mine, copies


def _a2a_start(*refs):
    mine, copies = _a2a_copies(*refs)
    mine.start()
    for cp in copies:
        cp.start()


def _a2a_finish(*refs):
    mine, copies = _a2a_copies(*refs)
    for cp in copies:
        cp.wait()
    mine.wait()


_EXCHANGES = {"gather": (_gather_start, _gather_finish), "a2a": (_a2a_start, _a2a_finish)}


def _exchange_shape(kind, src):
    return _sds((N_DEV,) + src.shape if kind == "gather" else src.shape, src.dtype)


def _all_gather_small(x, name):
    def body(x_ref, out_ref, send_sems, recv_sems, local_sem):
        mx, my, mc = lax.axis_index("x"), lax.axis_index("y"), lax.axis_index("c")
        me = 4 * mx + 2 * my + mc
        mine = pltpu.make_async_copy(x_ref, out_ref.at[me], local_sem)
        mine.start()
        copies = []
        for k in range(1, N_DEV):
            peer = (1 - mx if (k >> 2) & 1 else mx, 1 - my if (k >> 1) & 1 else my, 1 - mc if k & 1 else mc)
            copies.append(pltpu.make_async_remote_copy(
                src_ref=x_ref, dst_ref=out_ref.at[me], send_sem=send_sems.at[k - 1], recv_sem=recv_sems.at[k - 1],
                device_id=peer, device_id_type=MESH))
            copies[-1].start()
        for cp in copies:
            cp.wait()
        mine.wait()

    return _pcall(body, name=name, out_shape=_exchange_shape("gather", x), in_specs=[_ANY], out_specs=_ANY,
                  scratch_shapes=list(_SEMS))(x)


def _pcall_ride(body, ride, args, *, name, grid, in_specs, out_specs, out_shape, scratch_shapes=(), compiler_params=None):
    if ride is None:
        out = _pcall(body, name=name, grid=grid, in_specs=in_specs, out_specs=out_specs, out_shape=out_shape,
                     scratch_shapes=list(scratch_shapes), compiler_params=compiler_params)(*args)
        return out, None
    kind, src = ride
    start, finish = _EXCHANGES[kind]
    single = not isinstance(out_shape, (tuple, list))
    out_specs_t = (out_specs,) if single else tuple(out_specs)
    out_shape_t = (out_shape,) if single else tuple(out_shape)
    n_in, n_out, n_sc = len(in_specs), len(out_shape_t), len(scratch_shapes)

    def wrapped(*refs):
        ins, src_ref = refs[:n_in], refs[n_in]
        outs, dst_ref = refs[n_in + 1:n_in + 1 + n_out], refs[n_in + 1 + n_out]
        scratch = refs[n_in + 2 + n_out:n_in + 2 + n_out + n_sc]
        sems = refs[n_in + 2 + n_out + n_sc:]
        first = pl.program_id(0) == 0
        last = pl.program_id(0) == grid[0] - 1
        for ax in range(1, len(grid)):
            first = jnp.logical_and(first, pl.program_id(ax) == 0)
            last = jnp.logical_and(last, pl.program_id(ax) == grid[ax] - 1)

        @pl.when(first)
        def _():
            start(src_ref, dst_ref, *sems)

        body(*ins, *outs, *scratch)

        @pl.when(last)
        def _():
            finish(src_ref, dst_ref, *sems)

    res = _pcall(wrapped, name=name, grid=grid, in_specs=list(in_specs) + [_ANY], out_specs=out_specs_t + (_ANY,),
                 out_shape=out_shape_t + (_exchange_shape(kind, src),),
                 scratch_shapes=list(scratch_shapes) + list(_SEMS), compiler_params=compiler_params)(*args, src)
    return (res[0] if single else tuple(res[:-1])), res[-1]


SUB_ROWS = TM


class _Epilogue:
    def __init__(self, fn, args, in_specs, out_specs, out_shape, steps=None, lhs_map=None):
        self.fn, self.args, self.in_specs, self.out_specs, self.out_shape = fn, tuple(args), list(in_specs), out_specs, out_shape
        self.steps, self.lhs_map = steps, lhs_map


def _mm_nn(a, w, out_dtype, name, square_lhs=False, relu_out=False, ride=None, tm=TM, epi=None):
    m, k = a.shape
    nb, _, bn = w.shape
    tm = min(tm, m)
    n_extra = 0 if epi is None else len(epi.args)

    def body(*refs):
        a_ref, w_ref = refs[:2]
        av = a_ref[...]
        if square_lhs:
            av = av * av
        if epi is not None:
            assert nb == 1
            for r0 in range(0, tm, SUB_ROWS):
                rows = slice(r0, r0 + SUB_ROWS)
                epi.fn(jnp.dot(av[rows], w_ref[0], preferred_element_type=F32), rows, *refs[2:])
            return
        for j in range(nb):
            r = jnp.dot(av, w_ref[j], preferred_element_type=F32)
            if relu_out:
                r = jnp.maximum(r, 0.0)
            refs[2][:, j * bn:(j + 1) * bn] = r.astype(out_dtype)

    in_specs = [pl.BlockSpec((tm, k), lambda i: (i, 0)), pl.BlockSpec((nb, k, bn), lambda i: (0, 0, 0))]
    if epi is None:
        args, out_specs, out_shape = (a, w), pl.BlockSpec((tm, nb * bn), lambda i: (i, 0)), _sds((m, nb * bn), out_dtype)
    else:
        args, out_specs, out_shape = (a, w) + epi.args, epi.out_specs, epi.out_shape
        in_specs += epi.in_specs
    out, ex = _pcall_ride(body, ride, args, name=name, grid=(m // tm,), in_specs=in_specs, out_specs=out_specs,
                          out_shape=out_shape, compiler_params=_cp())
    return out if ride is None else (out, ex)


def _mm_nt(dy, w, out_dtype, name, relu_mul=None, ride=None, tm=TM, epi=None):
    m = dy.shape[0]
    nb, k, bn = w.shape
    tm = min(tm, m)
    n_extra = (0 if relu_mul is None else 1) + (0 if epi is None else len(epi.args))

    def body(*refs):
        dy_ref, w_ref = refs[:2]
        extra, outs, wt = refs[2:2 + n_extra], refs[2 + n_extra:-1], refs[-1]

        @pl.when(pl.program_id(0) == 0)
        def _():
            for j in range(nb):
                wt[j * bn:(j + 1) * bn, :] = w_ref[j].T

        if epi is not None:
            for r0 in range(0, tm, SUB_ROWS):
                rows = slice(r0, r0 + SUB_ROWS)
                epi.fn(jnp.dot(dy_ref[rows, :], wt[...], preferred_element_type=F32), rows, *extra, *outs)
            return
        acc = jnp.dot(dy_ref[...], wt[...], preferred_element_type=F32)
        if relu_mul is not None:
            acc = acc * (2.0 * extra[0][...].astype(F32))
        outs[0][...] = acc.astype(out_dtype)

    lhs_map = (lambda i: (i, 0)) if epi is None or epi.lhs_map is None else epi.lhs_map
    in_specs = [pl.BlockSpec((tm, nb * bn), lhs_map), pl.BlockSpec((nb, k, bn), lambda i: (0, 0, 0))]
    args = [dy, w]
    if relu_mul is not None:
        in_specs.append(pl.BlockSpec((tm, k), lambda i: (i, 0)))
        args.append(relu_mul)
    steps = m // tm
    if epi is None:
        out_specs, out_shape = pl.BlockSpec((tm, k), lambda i: (i, 0)), _sds((m, k), out_dtype)
    else:
        args += list(epi.args)
        in_specs += epi.in_specs
        out_specs, out_shape = epi.out_specs, epi.out_shape
        steps = steps if epi.steps is None else epi.steps
    out, ex = _pcall_ride(
        body, ride, args, name=name, grid=(steps,), in_specs=in_specs, out_specs=out_specs, out_shape=out_shape,
        scratch_shapes=[pltpu.VMEM((nb * bn, k), BF16)], compiler_params=_cp())
    return out if ride is None else (out, ex)


def _mm_tn(a, b, name, col_blocks, block, out_dtype, tm, square_lhs=False):
    m, k = a.shape
    nn = b.shape[1]
    steps = m // tm
    if col_blocks:
        nblk, acc_shape = nn // block, (k, block)
        a_spec = pl.BlockSpec((tm, k), lambda j, s: (s, 0))
        b_spec = pl.BlockSpec((tm, block), lambda j, s: (s, j))
    else:
        nblk, acc_shape = k // block, (block, nn)
        a_spec = pl.BlockSpec((tm, block), lambda j, s: (s, j))
        b_spec = pl.BlockSpec((tm, nn), lambda j, s: (s, 0))

    def body(a_ref, b_ref, o_ref, acc):
        s = pl.program_id(1)

        @pl.when(s == 0)
        def _():
            acc[...] = jnp.zeros_like(acc)

        av = a_ref[...]
        if square_lhs:
            av = av.astype(F32)
            av = (av * av).astype(BF16)
        acc[...] += _dot_tn(av, b_ref[...])

        @pl.when(s == steps - 1)
        def _():
            o_ref[...] = acc[...].astype(out_dtype)

    return _pcall(
        body, name=name, grid=(nblk, steps), in_specs=[a_spec, b_spec],
        out_specs=pl.BlockSpec((None,) + acc_shape, lambda j, s: (j, 0, 0)),
        out_shape=_sds((nblk,) + acc_shape, out_dtype),
        scratch_shapes=[pltpu.VMEM(acc_shape, F32)], compiler_params=_cp(),
    )(a, b)


def _mm_wgrad(at, b, name, bn, out_dtype, transpose_out=False, square_rhs=False, halves=False, ride=None):
    k, m = at.shape
    nblk = b.shape[1] // bn
    rows, cols = (bn, k) if transpose_out else (k, bn)
    nout = 2 if halves else 1
    per = rows // nout

    def body(a_ref, b_ref, *o_refs):
        bv = b_ref[...]
        if square_rhs:
            bv = bv * bv
        r = jnp.dot(a_ref[...], bv, preferred_element_type=F32)
        r = (r.T if transpose_out else r).astype(out_dtype)
        for i, o_ref in enumerate(o_refs):
            o_ref[...] = r[i * per:(i + 1) * per, :]

    out, ex = _pcall_ride(
        body, ride, (at, b), name=name, grid=(nblk,),
        in_specs=[pl.BlockSpec((k, m), lambda j: (0, 0)), pl.BlockSpec((m, bn), lambda j: (0, j))],
        out_specs=tuple(pl.BlockSpec((None, per, cols), lambda j: (j, 0, 0)) for _ in range(nout)),
        out_shape=tuple(_sds((nblk, per, cols), out_dtype) for _ in range(nout)), compiler_params=_cp())
    out = out if halves else out[0]
    return out if ride is None else (out, ex)


def _mm_in(hn, w, cosf, sins, ride):
    m, k = hn.shape
    nb, _, bn = w.shape

    def body(a_ref, w_ref, c_ref, s_ref, qkv_ref, rest_ref, pt):
        av = a_ref[...]
        for j in range(nb):
            pt[:, j * bn:(j + 1) * bn] = jnp.dot(av, w_ref[j], preferred_element_type=F32)
        cf, ss = c_ref[...], s_ref[...]
        for h in range(HEADS):
            sq = slice(h * HEAD_DIM, (h + 1) * HEAD_DIM)
            sk = slice(RET_W + h * HEAD_DIM, RET_W + (h + 1) * HEAD_DIM)
            qkv_ref[:, sq] = _rot(pt[:, sq], cf, ss).astype(BF16)
            qkv_ref[:, sk] = (_rot(pt[:, sk], cf, ss) * K_SCALE).astype(BF16)
        qkv_ref[:, 2 * RET_W:] = pt[:, 2 * RET_W:3 * RET_W].astype(BF16)
        rest_ref[...] = pt[:, 3 * RET_W:]

    tab = pl.BlockSpec((TM, HEAD_DIM), lambda i: (i, 0))
    wide = pl.BlockSpec((TM, 3 * RET_W), lambda i: (i, 0))
    return _pcall_ride(
        body, ride, (hn, w, cosf, sins), name="mm_in", grid=(m // TM,),
        in_specs=[pl.BlockSpec((TM, k), lambda i: (i, 0)), pl.BlockSpec((nb, k, bn), lambda i: (0, 0, 0)), tab, tab],
        out_specs=(wide, wide), out_shape=(_sds((m, 3 * RET_W), BF16), _sds((m, nb * bn - 3 * RET_W), F32)),
        scratch_shapes=[pltpu.VMEM((TM, nb * bn), F32)], compiler_params=_cp())


def _mod_part(s16, w_ada, b_cols):
    def body(s_ref, w_ref, b_ref, o_ref):
        o_ref[...] = _dot(s_ref[...], w_ref[...]) + b_ref[...]

    return _pcall(body, name="mod_part", out_shape=_sds((s16.shape[0], w_ada.shape[1]), F32),
                  compiler_params=_cp())(s16, w_ada, b_cols)


def _ada_bwd(s16, dm_cols, w_ada):
    def body(s_ref, d_ref, w_ref, gw_ref, ds_ref):
        gw_ref[...] = _dot_tn(s_ref[...], d_ref[...])
        ds_ref[...] = _dot_nt(d_ref[...], w_ref[...])

    return _pcall(body, name="ada_bwd",
                  out_shape=(_sds(w_ada.shape, F32), _sds(s16.shape, F32)), compiler_params=_cp())(s16, dm_cols, w_ada)


def _norm1_fwd(ctx, x, g, modrows, cb, ride=None):
    l_len, d = ctx.shape
    nb = (l_len + x.shape[0]) // TM

    def body(ctx_ref, x_ref, g_ref, m_ref, o_ref, ot_ref):
        is_ctx = pl.program_id(0) < cb
        xin = jnp.where(is_ctx, ctx_ref[...], x_ref[...])
        sh = jnp.where(is_ctx, m_ref[R_CSH1:R_CSH1 + 1, :], m_ref[R_SH1:R_SH1 + 1, :])
        sc = jnp.where(is_ctx, m_ref[R_CSC1:R_CSC1 + 1, :], m_ref[R_SC1:R_SC1 + 1, :])
        ms = jnp.mean(xin * xin, axis=-1, keepdims=True)
        n = xin * lax.rsqrt(ms + EPS) * g_ref[...]
        hn = n * (1.0 + sc) + sh
        o_ref[...] = hn.astype(BF16)
        ot_ref[...] = hn.T.astype(BF16)

    return _pcall_ride(
        body, ride, (ctx, x, g, modrows), name="norm1_fwd", grid=(nb,),
        in_specs=[pl.BlockSpec((TM, d), lambda i: (jnp.minimum(i, cb - 1), 0)),
                  pl.BlockSpec((TM, d), lambda i: (jnp.maximum(i - cb, 0), 0)),
                  pl.BlockSpec((1, d), lambda i: (0, 0)), pl.BlockSpec((8, d), lambda i: (0, 0))],
        out_specs=(pl.BlockSpec((TM, d), lambda i: (i, 0)), pl.BlockSpec((d, TM), lambda i: (0, i))),
        out_shape=(_sds((nb * TM, d), BF16), _sds((d, nb * TM), BF16)), compiler_params=_cp())


def _mix_fwd(o_f, o_b, p, h_f, h_b, cb, t_len):
    def body(of_ref, ob_ref, g_ref, gate_ref, hf_ref, hb_ref, mix_ref):
        o = of_ref[...] + ob_ref[...]
        g = g_ref[...]
        sg = g * _sigmoid(g)
        for hh in range(HEADS):
            sl = slice(hh * HEAD_DIM, (hh + 1) * HEAD_DIM)
            oh = o[:, sl]
            yc = oh - jnp.mean(oh, axis=-1, keepdims=True)
            var = jnp.mean(yc * yc, axis=-1, keepdims=True)
            mix_ref[:, sl] = (sg[:, sl] * (yc * lax.rsqrt(var + EPS))).astype(BF16)
        mix_ref[:, RET_W:] = ((hf_ref[...] + hb_ref[...]) * _gelu(gate_ref[...])).astype(BF16)

    row = lambda i: (i + cb, 0)
    return _pcall(
        body, name="mix_fwd", grid=(t_len // TM,),
        in_specs=[pl.BlockSpec((TM, RET_W), row), pl.BlockSpec((TM, RET_W), row),
                  pl.BlockSpec((TM, RET_W), lambda i: (i + cb, COL_G)), pl.BlockSpec((TM, LRU_W), lambda i: (i + cb, COL_GATE)),
                  pl.BlockSpec((TM, LRU_W), row), pl.BlockSpec((TM, LRU_W), row)],
        out_specs=pl.BlockSpec((TM, RET_W + LRU_W), lambda i: (i, 0)),
        out_shape=_sds((t_len, RET_W + LRU_W), BF16), compiler_params=_cp(),
    )(o_f, o_b, p, p, h_f, h_b)


def _res_norm2_epilogue(x, g, modrows, tm):
    t_len, d = x.shape
    tm = min(tm, t_len)

    def fn(y, rows, x_ref, g_ref, m_ref, y_ref, x1_ref, h2_ref, h2t_ref):
        y_ref[rows, :] = y
        x1 = x_ref[rows, :] + m_ref[R_G1:R_G1 + 1, :] * y
        ms = jnp.mean(x1 * x1, axis=-1, keepdims=True)
        n = x1 * lax.rsqrt(ms + EPS) * g_ref[...]
        x1_ref[rows, :] = x1
        h2 = n * (1.0 + m_ref[R_SC2:R_SC2 + 1, :]) + m_ref[R_SH2:R_SH2 + 1, :]
        h2_ref[rows, :] = h2.astype(BF16)
        h2t_ref[:, rows] = h2.T.astype(BF16)

    t = pl.BlockSpec((tm, d), lambda i: (i, 0))
    return _Epilogue(
        fn, (x, g, modrows),
        in_specs=[t, pl.BlockSpec((1, d), lambda i: (0, 0)), pl.BlockSpec((8, d), lambda i: (0, 0))],
        out_specs=(t, t, t, pl.BlockSpec((d, tm), lambda i: (0, i))),
        out_shape=(_sds((t_len, d), F32), _sds((t_len, d), F32), _sds((t_len, d), BF16), _sds((d, t_len), BF16)))


def _zero_at_start(acc_ref, rows):
    if rows.start == 0:
        @pl.when(pl.program_id(0) == 0)
        def _():
            acc_ref[...] = jnp.zeros_like(acc_ref)


def _final_epilogue(x1, target, fg, modrows, tm):
    t_len, d = x1.shape

    def fn(z, rows, x1_ref, t_ref, fg_ref, m_ref, dx2_ref, dz_ref, dzt_ref, acc_ref):
        _zero_at_start(acc_ref, rows)
        g2 = m_ref[R_G2:R_G2 + 1, :]
        x2 = x1_ref[rows, :] + g2 * z
        rstd = lax.rsqrt(jnp.mean(x2 * x2, axis=-1, keepdims=True) + EPS)
        xh = x2 * rstd
        fg = fg_ref[...]
        e = xh * fg - t_ref[rows, :]
        dy = e * (1.0 / d)
        dxh = dy * fg
        dx2 = rstd * (dxh - xh * jnp.mean(dxh * xh, axis=-1, keepdims=True))
        dx2_ref[rows, :] = dx2
        dz = g2 * dx2
        dz_ref[rows, :] = dz.astype(BF16)
        dzt_ref[:, rows] = dz.T.astype(BF16)
        acc_ref[0:1, :] += jnp.sum(dy * xh, axis=0, keepdims=True)
        acc_ref[1:2, :] += jnp.sum(dx2 * z, axis=0, keepdims=True)
        acc_ref[2:3, :] += jnp.sum(e * e, axis=0, keepdims=True)

    t = pl.BlockSpec((tm, d), lambda i: (i, 0))
    return _Epilogue(
        fn, (x1, target, fg, modrows),
        in_specs=[t, t, pl.BlockSpec((1, d), lambda i: (0, 0)), pl.BlockSpec((8, d), lambda i: (0, 0))],
        out_specs=(t, t, pl.BlockSpec((d, tm), lambda i: (0, i)), pl.BlockSpec((8, d), lambda i: (0, 0))),
        out_shape=(_sds((t_len, d), F32), _sds((t_len, d), BF16), _sds((d, t_len), BF16), _sds((8, d), F32)))


def _bwd_norm2_epilogue(x1, dx2, y, g, modrows, tm):
    t_len, d = x1.shape

    def fn(dh2, rows, x1_ref, dx2_ref, y_ref, g_ref, m_ref, dx1_ref, dy_ref, acc_ref):
        _zero_at_start(acc_ref, rows)
        x1 = x1_ref[rows, :]
        rstd = lax.rsqrt(jnp.mean(x1 * x1, axis=-1, keepdims=True) + EPS)
        xh = x1 * rstd
        gn = g_ref[...]
        dn = dh2 * (1.0 + m_ref[R_SC2:R_SC2 + 1, :])
        dxh = dn * gn
        dx1 = dx2_ref[rows, :] + rstd * (dxh - xh * jnp.mean(dxh * xh, axis=-1, keepdims=True))
        dx1_ref[rows, :] = dx1
        dy_ref[rows, :] = (m_ref[R_G1:R_G1 + 1, :] * dx1).astype(BF16)
        acc_ref[0:1, :] += jnp.sum(dh2, axis=0, keepdims=True)
        acc_ref[1:2, :] += jnp.sum(dh2 * xh * gn, axis=0, keepdims=True)
        acc_ref[2:3, :] += jnp.sum(dn * xh, axis=0, keepdims=True)
        acc_ref[3:4, :] += jnp.sum(dx1 * y_ref[rows, :], axis=0, keepdims=True)

    t = pl.BlockSpec((tm, d), lambda i: (i, 0))
    return _Epilogue(
        fn, (x1, dx2, y, g, modrows),
        in_specs=[t, t, t, pl.BlockSpec((1, d), lambda i: (0, 0)), pl.BlockSpec((8, d), lambda i: (0, 0))],
        out_specs=(t, t, pl.BlockSpec((8, d), lambda i: (0, 0))),
        out_shape=(_sds((t_len, d), F32), _sds((t_len, d), BF16), _sds((8, d), F32)))


def _mix_bwd_epilogue(o_f, o_b, p, h_f, h_b, cb):
    n = o_f.shape[0]

    def fn(dm, rows, of_ref, ob_ref, g_ref, gate_ref, hf_ref, hb_ref, do_ref, dg_ref, dgate_ref, dhs_ref):
        is_ctx = pl.program_id(0) < cb

        @pl.when(is_ctx)
        def _():
            for r in (do_ref, dg_ref, dgate_ref, dhs_ref):
                r[rows, :] = jnp.zeros((rows.stop - rows.start, r.shape[1]), r.dtype)

        @pl.when(jnp.logical_not(is_ctx))
        def _():
            o = of_ref[rows, :] + ob_ref[rows, :]
            g = g_ref[rows, :]
            s = _sigmoid(g)
            sg = g * s
            dsg = s * (1.0 + g * (1.0 - s))
            for hh in range(HEADS):
                sl = slice(hh * HEAD_DIM, (hh + 1) * HEAD_DIM)
                oh = o[:, sl]
                yc = oh - jnp.mean(oh, axis=-1, keepdims=True)
                rs = lax.rsqrt(jnp.mean(yc * yc, axis=-1, keepdims=True) + EPS)
                gn = yc * rs
                dret = dm[:, sl]
                dgn = dret * sg[:, sl]
                dg_ref[rows, sl] = dret * gn * dsg[:, sl]
                do_ref[rows, sl] = rs * (dgn - jnp.mean(dgn, axis=-1, keepdims=True)
                                         - gn * jnp.mean(dgn * gn, axis=-1, keepdims=True))
            dlru = dm[:, RET_W:]
            gate = gate_ref[rows, :]
            dhs_ref[rows, :] = dlru * _gelu(gate)
            dgate_ref[rows, :] = dlru * (hf_ref[rows, :] + hb_ref[rows, :]) * _dgelu(gate)

    t = pl.BlockSpec((TM, RET_W), lambda i: (i, 0))
    return _Epilogue(
        fn, (o_f, o_b, p, p, h_f, h_b),
        in_specs=[t, t, pl.BlockSpec((TM, RET_W), lambda i: (i, COL_G)), pl.BlockSpec((TM, LRU_W), lambda i: (i, COL_GATE)), t, t],
        out_specs=(t, t, t, t), out_shape=tuple(_sds((n, RET_W), F32) for _ in range(4)),
        steps=n // TM, lhs_map=lambda i: (jnp.maximum(i - cb, 0), 0))


def _bwd_norm1_epilogue(ctx, x, dx1, g, modrows, cb):
    t_len, d = x.shape

    def fn(dh, rows, ctx_ref, x_ref, dx1_ref, g_ref, m_ref, gx_ref, acc_ref):
        is_ctx = pl.program_id(0) < cb
        _zero_at_start(acc_ref, rows)
        xin = jnp.where(is_ctx, ctx_ref[rows, :], x_ref[rows, :])
        sc = jnp.where(is_ctx, m_ref[R_CSC1:R_CSC1 + 1, :], m_ref[R_SC1:R_SC1 + 1, :])
        rstd = lax.rsqrt(jnp.mean(xin * xin, axis=-1, keepdims=True) + EPS)
        xh = xin * rstd
        gn = g_ref[...]
        dn = dh * (1.0 + sc)
        dxh = dn * gn
        gx_ref[rows, :] = dx1_ref[rows, :] + rstd * (dxh - xh * jnp.mean(dxh * xh, axis=-1, keepdims=True))
        s0 = jnp.sum(dh, axis=0, keepdims=True)
        s1 = jnp.sum(dh * xh * gn, axis=0, keepdims=True)
        acc_ref[4:5, :] += jnp.sum(dn * xh, axis=0, keepdims=True)

        @pl.when(is_ctx)
        def _():
            acc_ref[0:1, :] += s0
            acc_ref[1:2, :] += s1

        @pl.when(jnp.logical_not(is_ctx))
        def _():
            acc_ref[2:3, :] += s0
            acc_ref[3:4, :] += s1

    lat = pl.BlockSpec((TM, d), lambda i: (jnp.maximum(i - cb, 0), 0))
    return _Epilogue(
        fn, (ctx, x, dx1, g, modrows),
        in_specs=[pl.BlockSpec((TM, d), lambda i: (jnp.minimum(i, cb - 1), 0)),
                  lat, lat, pl.BlockSpec((1, d), lambda i: (0, 0)), pl.BlockSpec((8, d), lambda i: (0, 0))],
        out_specs=(lat, pl.BlockSpec((8, d), lambda i: (0, 0))),
        out_shape=(_sds((t_len, d), F32), _sds((8, d), F32)))


def _rot(x, cf, ss):
    return x * cf + pltpu.roll(x, HEAD_DIM // 2, 1) * ss


def _decay_exponents(dirn):
    ii = lax.broadcasted_iota(jnp.int32, (CHUNK, CHUNK), 0)
    jj = lax.broadcasted_iota(jnp.int32, (CHUNK, CHUNK), 1)
    rel = ii - jj if dirn == 0 else jj - ii
    pos = ii.astype(F32)
    if dirn == 0:
        cq, cs = pos + 1.0, (CHUNK - 1.0) - pos
    else:
        cq, cs = CHUNK - pos, pos
    return rel, jnp.maximum(rel, 0).astype(F32), cq, cs


def _store_decay(lg_ref, dec, relf_ref=None):
    for dirn in (0, 1):
        rel, relf, cq, cs = _decay_exponents(dirn)
        if relf_ref is not None:
            relf_ref[dirn] = relf
        for h in range(HEADS):
            lgv = lg_ref[dirn, h]
            wq, ws = jnp.exp(lgv * cq), jnp.exp(lgv * cs)
            dec[dirn, h, 0] = jnp.where(rel >= 0, jnp.exp(lgv * relf), 0.0)
            dec[dirn, h, 1] = wq
            dec[dirn, h, 2] = ws
            if relf_ref is not None:
                dec[dirn, h, 3] = wq * cq
                dec[dirn, h, 4] = ws * cs


def _ret_rows(cc, nc, step_of):
    return [lambda s, dirn=dirn: _tile_order(dirn, step_of(s), cc, nc) for dirn in (0, 1)]


def _ret_in_specs(rows):
    specs = []
    for row in rows:
        specs += [pl.BlockSpec((CHUNK, RET_W), lambda s, o=o, row=row: (row(s), o)) for o in (0, 1, 2)]
    return specs


def _ret_fwd(qkv, lg, cc, ride=None):
    n = qkv.shape[0]
    nc = n // CHUNK
    rows = _ret_rows(cc, nc, lambda s: s)

    def body(lg_ref, q0, k0, v0, q1, k1, v1, o0, o1, sp0, sp1, st, dec):
        @pl.when(pl.program_id(0) == 0)
        def _():
            st[...] = jnp.zeros_like(st)
            _store_decay(lg_ref, dec)

        for dirn, (q_ref, k_ref, v_ref, o_ref, sp_ref) in enumerate(((q0, k0, v0, o0, sp0), (q1, k1, v1, o1, sp1))):
            for h in range(HEADS):
                sl = slice(h * HEAD_DIM, (h + 1) * HEAD_DIM)
                q, k, v = q_ref[:, sl], k_ref[:, sl], v_ref[:, sl]
                sp = st[dirn, h]
                sp_ref[h] = sp
                o_ref[:, sl] = _dot(_dot_nt(q, k) * dec[dirn, h, 0], v) + _dot(q * dec[dirn, h, 1], sp)
                st[dirn, h] = jnp.exp(lg_ref[dirn, h] * CHUNK) * sp + _dot_tn(k * dec[dirn, h, 2], v)

    o_specs = [pl.BlockSpec((CHUNK, RET_W), lambda s, row=row: (row(s), 0)) for row in rows]
    state = pl.BlockSpec((None, HEADS, CHUNK, HEAD_DIM), lambda s: (s, 0, 0, 0))
    return _pcall_ride(
        body, ride, (lg,) + (qkv,) * 6, name="ret_fwd", grid=(nc,),
        in_specs=[pl.BlockSpec(memory_space=pltpu.SMEM)] + _ret_in_specs(rows),
        out_specs=(o_specs[0], o_specs[1], state, state),
        out_shape=(_sds((n, RET_W), F32),) * 2 + (_sds((nc, HEADS, CHUNK, HEAD_DIM), F32),) * 2,
        scratch_shapes=[pltpu.VMEM((2, HEADS, CHUNK, HEAD_DIM), F32), pltpu.VMEM((2, HEADS, 3, CHUNK, CHUNK), F32)],
        compiler_params=_cp())


def _ret_bwd(qkv, lg, do, s_prev, cc, ride=None):
    n = qkv.shape[0]
    nc = n // CHUNK
    rows = _ret_rows(cc, nc, lambda s: nc - 1 - s)

    def body(lg_ref, q0, k0, v0, q1, k1, v1, do0, do1, sp0, sp1, dq0, dk0, dv0, dq1, dk1, dv1, dlg_ref, dst, dec, relf):
        @pl.when(pl.program_id(0) == 0)
        def _():
            dst[...] = jnp.zeros_like(dst)
            dlg_ref[...] = jnp.zeros_like(dlg_ref)
            _store_decay(lg_ref, dec, relf)

        for dirn, (q_ref, k_ref, v_ref, do_ref, sp_ref, dq_ref, dk_ref, dv_ref) in enumerate(
                ((q0, k0, v0, do0, sp0, dq0, dk0, dv0), (q1, k1, v1, do1, sp1, dq1, dk1, dv1))):
            for h in range(HEADS):
                sl = slice(h * HEAD_DIM, (h + 1) * HEAD_DIM)
                q, k, v = q_ref[:, sl], k_ref[:, sl], v_ref[:, sl]
                dov = do_ref[:, sl]
                dm, wq, ws = dec[dirn, h, 0], dec[dirn, h, 1], dec[dirn, h, 2]
                a = _dot_nt(q, k)
                sp = sp_ref[h]
                dsn = dst[dirn, h]
                gc = jnp.exp(lg_ref[dirn, h] * CHUNK)
                g1 = _dot_nt(dov, sp)
                da = _dot_nt(dov, v) * dm
                dq = g1 * wq + _dot(da, k)
                h1 = _dot_nt(v, dsn)
                dk = _dot_tn(da, q) + h1 * ws
                dv_ref[:, sl] = _dot_tn(a * dm, dov) + _dot(k * ws, dsn)
                dst[dirn, h] = gc * dsn + _dot_tn(q * wq, dov)
                term = da * a * relf[dirn] + q * g1 * dec[dirn, h, 3] + k * h1 * dec[dirn, h, 4] + sp * dsn * (CHUNK * gc)
                dlg_ref[dirn * HEADS + h:dirn * HEADS + h + 1, 0:HEAD_DIM] += jnp.sum(term, axis=0, keepdims=True)
                dq_ref[:, sl] = dq
                dk_ref[:, sl] = dk

    wide = [pl.BlockSpec((CHUNK, RET_W), lambda s, row=row: (row(s), 0)) for row in rows]
    state = pl.BlockSpec((None, HEADS, CHUNK, HEAD_DIM), lambda s: (nc - 1 - s, 0, 0, 0))
    return _pcall_ride(
        body, ride, (lg,) + (qkv,) * 6 + (do, do, s_prev[0], s_prev[1]), name="ret_bwd", grid=(nc,),
        in_specs=[pl.BlockSpec(memory_space=pltpu.SMEM)] + _ret_in_specs(rows) + wide + [state, state],
        out_specs=(wide[0],) * 3 + (wide[1],) * 3 + (pl.BlockSpec((2 * HEADS, 8 * HEAD_DIM), lambda s: (0, 0)),),
        out_shape=(_sds((n, RET_W), F32),) * 6 + (_sds((2 * HEADS, 8 * HEAD_DIM), F32),),
        scratch_shapes=[pltpu.VMEM((2, HEADS, CHUNK, HEAD_DIM), F32), pltpu.VMEM((2, HEADS, 5, CHUNK, CHUNK), F32),
                        pltpu.VMEM((2, CHUNK, CHUNK), F32)],
        compiler_params=_cp())


def _shift_rows(cur, prev8, next8, k, seg_start, seg_end):
    tm = cur.shape[0]
    rows = _rows_iota(cur.shape)
    if k < 0:
        out = pltpu.roll(cur, -k, 0)
        for j in range(-k):
            halo = jnp.where(seg_start, 0.0, prev8[SUB + k + j:SUB + k + j + 1, :])
            out = jnp.where(rows == j, halo, out)
    else:
        out = pltpu.roll(cur, tm - k, 0)
        for j in range(k):
            halo = jnp.where(seg_end, 0.0, next8[j:j + 1, :])
            out = jnp.where(rows == tm - k + j, halo, out)
    return out


def _seg_flags(t, cb, nb):
    return jnp.logical_or(t == 0, t == cb), jnp.logical_or(t == cb - 1, t == nb - 1)


def _halo_specs(tile_of, n_rows, col):
    per = TM // SUB
    return [pl.BlockSpec((TM, LRU_W), lambda s: (tile_of(s), col)),
            pl.BlockSpec((SUB, LRU_W), lambda s: (jnp.maximum(tile_of(s) * per - 1, 0), col)),
            pl.BlockSpec((SUB, LRU_W), lambda s: (jnp.minimum((tile_of(s) + 1) * per, n_rows // SUB - 1), col))]


def _lru_gates(xr, prev8, next8, seg_start, seg_end, cw_ref, cb_ref, wg_ref, bg_ref, sp_ref):
    xm1 = _shift_rows(xr, prev8, next8, -1, seg_start, seg_end)
    xp1 = _shift_rows(xr, prev8, next8, 1, seg_start, seg_end)
    xp2 = _shift_rows(xr, prev8, next8, 2, seg_start, seg_end)
    xc = cb_ref[...] + xm1 * cw_ref[0:1, :] + xr * cw_ref[1:2, :] + xp1 * cw_ref[2:3, :] + xp2 * cw_ref[3:4, :]
    pre = _dot(xc, wg_ref[...]) + bg_ref[...]
    r = _sigmoid(pre[:, :LRU_W])
    i = _sigmoid(pre[:, LRU_W:])
    la = (-LRU_C) * r * sp_ref[...]
    a = jnp.exp(la)
    th = jnp.tanh(la)
    sq = jnp.sqrt(-2.0 * th / (1.0 - th))
    return xc, r, i, a, sq


def _scan_tile(a, b, ascending, a_sc, b_sc, carry, out_ref):
    tm, w = a.shape
    nsub = tm // SUB
    a = a.reshape(nsub, SUB, w)
    b = b.reshape(nsub, SUB, w)
    r8 = lax.broadcasted_iota(jnp.int32, a.shape, 1)
    for k in (1, 2, 4):
        if ascending:
            m = r8 >= k
            a_s, b_s = pltpu.roll(a, k, 1), pltpu.roll(b, k, 1)
        else:
            m = r8 < SUB - k
            a_s, b_s = pltpu.roll(a, SUB - k, 1), pltpu.roll(b, SUB - k, 1)
        b = a * jnp.where(m, b_s, 0.0) + b
        a = a * jnp.where(m, a_s, 1.0)
    a_sc[...] = a.reshape(tm, w)
    b_sc[...] = b.reshape(tm, w)

    def step(j, c):
        off = pl.multiple_of((j if ascending else nsub - 1 - j) * SUB, SUB)
        hb = a_sc[pl.ds(off, SUB), :] * c + b_sc[pl.ds(off, SUB), :]
        out_ref[pl.ds(off, SUB), :] = hb
        last = hb[SUB - 1:SUB, :] if ascending else hb[0:1, :]
        return jnp.broadcast_to(last, c.shape)

    carry[...] = lax.fori_loop(0, nsub, step, carry[...])


def _lru_fwd(p, wg, bg, sp, cw, cbias, dirn, cb, ride=None):
    n = p.shape[0]
    nb = n // TM
    tile_of = lambda s: _tile_order(dirn, s, cb, nb)

    def body(x_ref, xp_ref, xn_ref, wg_ref, bg_ref, sp_ref, cw_ref, cb_ref, h_ref, cin_ref, carry, a_sc, b_sc):
        s = pl.program_id(0)

        @pl.when(s == 0)
        def _():
            carry[...] = jnp.zeros_like(carry)

        seg_start, seg_end = _seg_flags(tile_of(s), cb, nb)
        xc, r, i, a, sq = _lru_gates(x_ref[...], xp_ref[...], xn_ref[...], seg_start, seg_end,
                                     cw_ref, cb_ref, wg_ref, bg_ref, sp_ref)
        cin_ref[...] = carry[...]
        _scan_tile(a, sq * (i * xc), dirn == 0, a_sc, b_sc, carry, h_ref)

    full = lambda shape: pl.BlockSpec(shape, lambda s: (0,) * len(shape))
    return _pcall_ride(
        body, ride, (p, p, p, wg, bg, sp, cw, cbias), name=f"lru_fwd{dirn}", grid=(nb,),
        in_specs=_halo_specs(tile_of, n, COL_XR) + [full((LRU_W, 2 * LRU_W)), full((1, 2 * LRU_W)), full((1, LRU_W)),
                                               full((4, LRU_W)), full((1, LRU_W))],
        out_specs=(pl.BlockSpec((TM, LRU_W), lambda s: (tile_of(s), 0)),
                   pl.BlockSpec((None, SUB, LRU_W), lambda s: (tile_of(s), 0, 0))),
        out_shape=(_sds((n, LRU_W), F32), _sds((nb, SUB, LRU_W), F32)),
        scratch_shapes=[pltpu.VMEM((SUB, LRU_W), F32), pltpu.VMEM((TM, LRU_W), F32), pltpu.VMEM((TM, LRU_W), F32)],
        compiler_params=_cp())


def _lru_bwd(p, wg, bg, sp, cw, cbias, h, cin, dhs, dirn, cb, ride=None):
    n = p.shape[0]
    nb = n // TM
    tile_of = lambda s: _tile_order(dirn, nb - 1 - s, cb, nb)

    def body(x_ref, xp_ref, xn_ref, wg_ref, bg_ref, sp_ref, cw_ref, cb_ref, h_ref, cin_ref, dhs_ref,
             dxc_ref, dwd_ref, acc_ref, carry, a_sc, b_sc, mu_sc, dwg_ref):
        s = pl.program_id(0)

        @pl.when(s == 0)
        def _():
            carry[...] = jnp.zeros_like(carry)
            dwg_ref[...] = jnp.zeros_like(dwg_ref)
            acc_ref[...] = jnp.zeros_like(acc_ref)

        seg_start, seg_end = _seg_flags(tile_of(s), cb, nb)
        xc, r, i, a, sq = _lru_gates(x_ref[...], xp_ref[...], xn_ref[...], seg_start, seg_end,
                                     cw_ref, cb_ref, wg_ref, bg_ref, sp_ref)
        rows = _rows_iota(a.shape)
        hv = h_ref[...]
        dh = dhs_ref[...]
        mu_next = carry[0:1, :]
        _scan_tile(a, a * dh, dirn == 1, a_sc, b_sc, carry, mu_sc)
        mu = mu_sc[...]
        if dirn == 0:
            hprev = jnp.where(rows == 0, cin_ref[0:1, :], pltpu.roll(hv, 1, 0))
            lam = dh + jnp.where(rows == TM - 1, mu_next, pltpu.roll(mu, TM - 1, 0))
        else:
            hprev = jnp.where(rows == TM - 1, cin_ref[0:1, :], pltpu.roll(hv, TM - 1, 0))
            lam = dh + jnp.where(rows == 0, mu_next, pltpu.roll(mu, 1, 0))
        ds = lam * (i * xc)
        di = lam * (sq * xc)
        dla = lam * hprev * a - ds * (a * a) / jnp.maximum(sq, 1e-20)
        dpr = dla * ((-LRU_C) * sp_ref[...]) * r * (1.0 - r)
        dpi = di * i * (1.0 - i)
        dpre = jnp.concatenate([dpr, dpi], axis=1)
        dxc_ref[...] = lam * (sq * i) + _dot_nt(dpre, wg_ref[...])
        dwg_ref[...] += _dot_tn(xc, dpre)
        acc_ref[0:1, :] += jnp.sum(dpre, axis=0, keepdims=True)
        acc_ref[1:2, 0:LRU_W] += jnp.sum(dla * ((-LRU_C) * r), axis=0, keepdims=True)

        @pl.when(s == nb - 1)
        def _():
            low = lax.broadcasted_iota(jnp.int32, (LRU_BD, 2 * LRU_BD), 1) < LRU_BD
            for half in (0, LRU_W):
                for m in range(LRU_BLOCKS // 2):
                    lanes = slice(half + 2 * LRU_BD * m, half + 2 * LRU_BD * (m + 1))
                    even = dwg_ref[2 * m * LRU_BD:(2 * m + 1) * LRU_BD, lanes]
                    odd = dwg_ref[(2 * m + 1) * LRU_BD:(2 * m + 2) * LRU_BD, lanes]
                    dwd_ref[:, lanes] = jnp.where(low, even, odd)

    full = lambda shape: pl.BlockSpec(shape, lambda s: (0,) * len(shape))
    tile = pl.BlockSpec((TM, LRU_W), lambda s: (tile_of(s), 0))
    return _pcall_ride(
        body, ride, (p, p, p, wg, bg, sp, cw, cbias, h, cin, dhs), name=f"lru_bwd{dirn}", grid=(nb,),
        in_specs=_halo_specs(tile_of, n, COL_XR) + [full((LRU_W, 2 * LRU_W)), full((1, 2 * LRU_W)), full((1, LRU_W)),
                                               full((4, LRU_W)), full((1, LRU_W)), tile,
                                               pl.BlockSpec((None, SUB, LRU_W), lambda s: (tile_of(s), 0, 0)), tile],
        out_specs=(tile, full((LRU_BD, 2 * LRU_W)), full((8, 2 * LRU_W))),
        out_shape=(_sds((n, LRU_W), F32), _sds((LRU_BD, 2 * LRU_W), F32), _sds((8, 2 * LRU_W), F32)),
        scratch_shapes=[pltpu.VMEM((SUB, LRU_W), F32)] + [pltpu.VMEM((TM, LRU_W), F32)] * 3
        + [pltpu.VMEM((LRU_W, 2 * LRU_W), F32)],
        compiler_params=_cp())


def _assemble_dp(dqs, dks, dvs, dg, dgate, dxcs, p, cw, cosf, sins, cb, ride=None):
    n = p.shape[0]
    nb = n // TM
    tile_of = lambda s: s

    def body(dqf, dqb, dkf, dkb, dvf, dvb, dg_ref, dgate_ref, cf, pf, nf, cb_, pb, nb_, x_ref, xp_ref, xn_ref,
             cw_ref, cos_ref, sin_ref, dp_ref, acc_ref):
        s = pl.program_id(0)

        @pl.when(s == 0)
        def _():
            acc_ref[...] = jnp.zeros_like(acc_ref)

        seg_start, seg_end = _seg_flags(s, cb, nb)
        dq = dqf[...] + dqb[...]
        dk = dkf[...] + dkb[...]
        cosv, sinv = cos_ref[...], sin_ref[...]
        for h in range(HEADS):
            sl = slice(h * HEAD_DIM, (h + 1) * HEAD_DIM)
            sk = slice(RET_W + h * HEAD_DIM, RET_W + (h + 1) * HEAD_DIM)
            dp_ref[:, sl] = (dq[:, sl] * cosv + pltpu.roll(dq[:, sl] * sinv, HEAD_DIM // 2, 1)).astype(BF16)
            dp_ref[:, sk] = ((dk[:, sl] * cosv + pltpu.roll(dk[:, sl] * sinv, HEAD_DIM // 2, 1)) * K_SCALE).astype(BF16)
        dp_ref[:, 2 * RET_W:3 * RET_W] = (dvf[...] + dvb[...]).astype(BF16)
        dp_ref[:, 3 * RET_W:4 * RET_W] = dg_ref[...].astype(BF16)
        dxc = cf[...] + cb_[...]
        dprev = pf[...] + pb[...]
        dnext = nf[...] + nb_[...]
        dxr = (_shift_rows(dxc, dprev, dnext, 1, seg_start, seg_end) * cw_ref[0:1, :] + dxc * cw_ref[1:2, :]
               + _shift_rows(dxc, dprev, dnext, -1, seg_start, seg_end) * cw_ref[2:3, :]
               + _shift_rows(dxc, dprev, dnext, -2, seg_start, seg_end) * cw_ref[3:4, :])
        dp_ref[:, 4 * RET_W:4 * RET_W + LRU_W] = dxr.astype(BF16)
        dp_ref[:, 4 * RET_W + LRU_W:] = dgate_ref[...].astype(BF16)
        xr, xp, xn = x_ref[...], xp_ref[...], xn_ref[...]
        for j, k in enumerate((-1, 0, 1, 2)):
            xs = xr if k == 0 else _shift_rows(xr, xp, xn, k, seg_start, seg_end)
            acc_ref[j:j + 1, 0:LRU_W] += jnp.sum(dxc * xs, axis=0, keepdims=True)
        acc_ref[4:5, 0:LRU_W] += jnp.sum(dxc, axis=0, keepdims=True)

    t = pl.BlockSpec((TM, RET_W), lambda s: (s, 0))
    args = (dqs[0], dqs[1], dks[0], dks[1], dvs[0], dvs[1], dg, dgate, dxcs[0], dxcs[0], dxcs[0], dxcs[1], dxcs[1], dxcs[1],
            p, p, p, cw, cosf, sins)
    tab = pl.BlockSpec((TM, HEAD_DIM), lambda s: (s, 0))
    return _pcall_ride(
        body, ride, args, name="assemble_dp", grid=(nb,),
        in_specs=[t] * 8 + _halo_specs(tile_of, n, 0) * 2 + _halo_specs(tile_of, n, COL_XR)
        + [pl.BlockSpec((4, LRU_W), lambda s: (0, 0)), tab, tab],
        out_specs=(pl.BlockSpec((TM, 4 * RET_W + 2 * LRU_W), lambda s: (s, 0)), pl.BlockSpec((8, 2 * LRU_W), lambda s: (0, 0))),
        out_shape=(_sds((n, 4 * RET_W + 2 * LRU_W), BF16), _sds((8, 2 * LRU_W), F32)), compiler_params=_cp())


def _adamw(g, w, m, v):
    nm = ADAM_B1 * m + (1.0 - ADAM_B1) * g
    nv = ADAM_B2 * v + (1.0 - ADAM_B2) * (g * g)
    m_hat = nm / (1.0 - ADAM_B1 ** ADAM_STEP)
    v_hat = nv / (1.0 - ADAM_B2 ** ADAM_STEP)
    return (-ADAM_LR) * (m_hat / (jnp.sqrt(v_hat) + ADAM_EPS) + ADAM_WD * w), nm, nv


def _adam_many(items, name):
    n = len(items)

    def body(*refs):
        for i in range(n):
            g, w, m, v = (r[...] for r in refs[4 * i:4 * i + 4])
            for o_ref, val in zip(refs[4 * n + 3 * i:4 * n + 3 * i + 3], _adamw(g, w, m, v)):
                o_ref[...] = val

    out_shape = tuple(_sds(it[1].shape, F32) for it in items for _ in range(3))
    res = _pcall(body, name=name, out_shape=out_shape, compiler_params=_cp())(*[a for it in items for a in it])
    return [tuple(res[3 * i:3 * i + 3]) for i in range(n)]


def _sum_adam(parts_list, w, m, v, name):
    nparts, _, c = parts_list[0].shape
    r = w.shape[0]
    tr = min(parts_list[0].shape[1], 128)
    starts, o = [], 0
    for pa in parts_list:
        starts.append(o)
        o += pa.shape[1] // tr
    nseg = len(parts_list)

    def body(*refs):
        p_refs = refs[:nseg]
        w_ref, m_ref, v_ref, g_ref, d_ref, nm_ref, nv_ref = refs[nseg:]
        i = pl.program_id(0)
        for s, p_ref in enumerate(p_refs):
            end = starts[s + 1] if s + 1 < nseg else r // tr

            @pl.when(jnp.logical_and(i >= starts[s], i < end))
            def _():
                g = p_ref[0].astype(F32)
                for j in range(1, nparts):
                    g = g + p_ref[j].astype(F32)
                g_ref[...] = g
                d_ref[...], nm_ref[...], nv_ref[...] = _adamw(g, w_ref[...], m_ref[...], v_ref[...])

    def seg_spec(s):
        last = parts_list[s].shape[1] // tr - 1
        return pl.BlockSpec((nparts, tr, c), lambda i: (0, jnp.clip(i - starts[s], 0, last), 0))

    t = pl.BlockSpec((tr, c), lambda i: (i, 0))
    return _pcall(
        body, name=name, grid=(r // tr,),
        in_specs=[seg_spec(s) for s in range(nseg)] + [t, t, t],
        out_specs=(t, t, t, t), out_shape=(_sds((r, c), F32),) * 4, compiler_params=_cp(),
    )(*parts_list, w, m, v)


def _sum_parts(parts, name):
    nparts, r, c = parts.shape

    def body(p_ref, o_ref):
        g = p_ref[0]
        for j in range(1, nparts):
            g = g + p_ref[j]
        o_ref[...] = g

    return _pcall(body, name=name, out_shape=_sds((r, c), parts.dtype), compiler_params=_cp())(parts)


def _rot_tables(l_len, t_len):
    rows = t_len // GRID_W
    n_freq = HEAD_DIM // 4
    inv = ROPE_BASE ** (-jnp.arange(n_freq, dtype=F32) / n_freq)
    ang_r = jnp.arange(rows, dtype=F32)[:, None] * inv
    ang_c = jnp.arange(GRID_W, dtype=F32)[:, None] * inv
    cos = jnp.concatenate([jnp.repeat(jnp.cos(ang_r), GRID_W, axis=0), jnp.tile(jnp.cos(ang_c), (rows, 1))], axis=-1)
    sin = jnp.concatenate([jnp.repeat(jnp.sin(ang_r), GRID_W, axis=0), jnp.tile(jnp.sin(ang_c), (rows, 1))], axis=-1)
    cosf = jnp.concatenate([jnp.ones((l_len, HEAD_DIM), F32), jnp.concatenate([cos, cos], axis=-1)], axis=0)
    sins = jnp.concatenate([jnp.zeros((l_len, HEAD_DIM), F32), jnp.concatenate([-sin, sin], axis=-1)], axis=0)
    return cosf, sins


def _block_diag(w):
    eye = jnp.eye(LRU_BLOCKS, dtype=w.dtype)
    return (w[:, :, None, :] * eye[:, None, :, None]).reshape(LRU_W, LRU_W)


def _blocks_from_lanes(dwd_half):
    return dwd_half.reshape(LRU_BD, LRU_BLOCKS, LRU_BD).transpose(1, 0, 2)


def _silu(x):
    return x * jax.nn.sigmoid(x)


def kernel(x, c, ctx, c_ctx, w_ada, b_ada, norm1_g, norm2_g, w_in, ret_decay, conv_w, conv_b, lru_wa, lru_ba, lru_wx, lru_bx, lru_lambda, w_out, w_mlp1, w_mlp2, final_g, loss_target, m_c_ctx, m_w_ada, m_b_ada, m_norm1_g, m_norm2_g, m_w_in, m_ret_decay, m_conv_w, m_conv_b, m_lru_wa, m_lru_ba, m_lru_wx, m_lru_bx, m_lru_lambda, m_w_out, m_w_mlp1, m_w_mlp2, m_final_g, v_c_ctx, v_w_ada, v_b_ada, v_norm1_g, v_norm2_g, v_w_in, v_ret_decay, v_conv_w, v_conv_b, v_lru_wa, v_lru_ba, v_lru_wx, v_lru_bx, v_lru_lambda, v_w_out, v_w_mlp1, v_w_mlp2, v_final_g):
    t_len, d = x.shape[1], x.shape[2]
    l_len = ctx.shape[1]
    cb, cc = l_len // TM, l_len // CHUNK
    me = 4 * lax.axis_index("x") + 2 * lax.axis_index("y") + lax.axis_index("c")
    x2d, ctx2d, tgt2d = x[0], ctx[0], loss_target[0]
    ada_cols = w_ada.shape[2]
    wa2d = w_ada[0]

    sc_loc = conv_w.shape[2]
    pack_a = jnp.zeros((8, d), F32)
    pack_a = pack_a.at[0].set(_silu(c[0]))
    pack_a = pack_a.at[1, :4 * sc_loc].set(conv_w[0].reshape(-1))
    pack_a = pack_a.at[2, :2 * sc_loc].set(lru_ba[0].reshape(-1))
    pack_a = pack_a.at[3, :2 * sc_loc].set(lru_bx[0].reshape(-1))
    pack_a = pack_a.at[4, :2 * sc_loc].set(lru_lambda[0].reshape(-1))
    all_a = _all_gather_small(pack_a, "gather_small_in")
    s16 = jnp.zeros((16, d), F32).at[0:8].set(all_a[:, 0, :]).at[8].set(_silu(c_ctx))

    def unshard(row, k):
        return all_a[:, row, :k * sc_loc].reshape(N_DEV, k, sc_loc).transpose(1, 0, 2).reshape(k, N_DEV * sc_loc)

    conv_w_full = unshard(1, 4)
    ba_full, bx_full, lam_full = unshard(2, 2), unshard(3, 2), unshard(4, 2)

    b_cols = lax.dynamic_slice(b_ada, (0, me * ada_cols), (1, ada_cols))
    mod_parts = _all_gather_small(_mod_part(s16, wa2d, b_cols), "gather_mod")
    mod_all = mod_parts.transpose(1, 0, 2).reshape(16, N_DEV * ada_cols)
    mod_me = lax.dynamic_slice(mod_all, (me, 0), (1, 6 * d)).reshape(6, d)
    mod_c = mod_all[8].reshape(6, d)
    modrows = jnp.concatenate([mod_c[0:2], mod_me], axis=0)

    lg = jax.nn.log_sigmoid(ret_decay[0])
    sp = jax.nn.softplus(-lam_full)
    wg = [jnp.concatenate([_block_diag(lru_wa[0, dd]), _block_diag(lru_wx[0, dd])], axis=1).astype(BF16) for dd in (0, 1)]
    bg = [jnp.concatenate([ba_full[dd], bx_full[dd]])[None, :] for dd in (0, 1)]
    cosf, sins = _rot_tables(l_len, t_len)

    (hn, hnt), win_g = _norm1_fwd(ctx2d, x2d, norm1_g, modrows, cb, ride=("gather", w_in[0].astype(BF16)))
    (qkv, p), w1_g = _mm_in(hn, win_g, cosf, sins, ride=("gather", w_mlp1[0].astype(BF16)))
    (o0, o1, sp0, sp1), w2_g = _ret_fwd(qkv, lg, cc, ride=("gather", w_mlp2[0].astype(BF16)))
    w2_g = w2_g.reshape(1, 4 * d, d)
    o, s_prev = [o0, o1], [sp0, sp1]
    (h0, cin0), wout_g = _lru_fwd(p, wg[0], bg[0], sp[0:1], conv_w_full, conv_b, 0, cb, ride=("gather", w_out[0].astype(BF16)))
    wout_g = wout_g.reshape(1, d, d)
    (h1, cin1), _ = _lru_fwd(p, wg[1], bg[1], sp[1:2], conv_w_full, conv_b, 1, cb)
    h, cin = [h0, h1], [cin0, cin1]
    mix = _mix_fwd(o[0], o[1], p, h[0], h[1], cb, t_len)
    y, x1, h2, h2t = _mm_nn(mix, wout_g, F32, "mm_out_norm2", tm=TL, epi=_res_norm2_epilogue(x2d, norm2_g, modrows, TL))
    r = _mm_nn(h2, w1_g, BF16, "mm_mlp1", relu_out=True, tm=TL)
    dx2, dz, dzt, facc = _mm_nn(r, w2_g, F32, "mm_mlp2_final", square_lhs=True,
                                epi=_final_epilogue(x1, tgt2d, final_g[None, :], modrows, TM))

    du = _mm_nt(dz, w2_g, BF16, "mm_da2", relu_mul=r)
    gw2 = _mm_wgrad(dzt, r, "mm_dw2", w2_g.shape[1] // N_DEV, BF16, transpose_out=True, square_rhs=True)
    gw1_lo, gw1_hi = _mm_wgrad(h2t, du, "mm_dw1", w1_g.shape[2], BF16, halves=True)
    (dx1, dy, n2acc), gw1_lo_all = _mm_nt(du, w1_g, F32, "mm_dh2_norm2", ride=("a2a", gw1_lo),
                                          epi=_bwd_norm2_epilogue(x1, dx2, y, norm2_g, modrows, TM))
    gwo = _mm_tn(mix, dy, "mm_dwout", False, d, BF16, 512).reshape(N_DEV, -1, d)
    (do, dg, dgate, dhs), gw1_hi_all = _mm_nt(dy, wout_g, F32, "mm_dmix_mix", ride=("a2a", gw1_hi),
                                              epi=_mix_bwd_epilogue(o[0], o[1], p, h[0], h[1], cb))
    gw1_all = [gw1_lo_all, gw1_hi_all]
    dxcs, dwgs, laccs, rides = [], [], [], [("a2a", gwo), None]
    (dq0, dk0, dv0, dq1, dk1, dv1, dlg_lanes), gw2_all = _ret_bwd(qkv, lg, do, s_prev, cc, ride=("a2a", gw2))
    dqs, dks, dvs = [dq0, dq1], [dk0, dk1], [dv0, dv1]
    for dd in (0, 1):
        (dxc_, dwg_, lacc_), got_ = _lru_bwd(p, wg[dd], bg[dd], sp[dd:dd + 1], conv_w_full, conv_b, h[dd], cin[dd], dhs, dd, cb,
                                             ride=rides[dd])
        dxcs.append(dxc_); dwgs.append(dwg_); laccs.append(lacc_)
        gwo_all = got_ if dd == 0 else gwo_all
    (dp, cacc), _ = _assemble_dp(dqs, dks, dvs, dg, dgate, dxcs, p, conv_w_full, cosf, sins, cb)
    pack_b = jnp.concatenate([n2acc, facc, cacc, laccs[0], laccs[1], dlg_lanes, dwgs[0], dwgs[1]], axis=0)
    gwi, all_b = _mm_wgrad(hnt, dp, "mm_dwin", win_g.shape[2], BF16, ride=("gather", pack_b))
    (grad_x, n1acc), gwi_all = _mm_nt(dp, win_g, F32, "mm_dhn_norm1", ride=("a2a", gwi),
                                      epi=_bwd_norm1_epilogue(ctx2d, x2d, dx1, norm1_g, modrows, cb))
    all_n1 = _all_gather_small(n1acc, "gather_norm1_grads")
    tot = _sum_parts(all_b, "sum_small_grads")
    t_n1 = _sum_parts(all_n1, "sum_norm1_grads")
    t_n2, t_f, t_conv, t_dlg = tot[0:8], tot[8:16], tot[16:24, :LRU_W], tot[40:48, :HEAD_DIM]
    t_l = [tot[24:32], tot[32:40]]
    t_dwd = [tot[48:48 + LRU_BD], tot[48 + LRU_BD:48 + 2 * LRU_BD]]
    loss = (0.5 / d) * jnp.sum(t_f[2])
    t_wa = jnp.stack([_blocks_from_lanes(t_dwd[dd][:, :LRU_W]) for dd in (0, 1)])
    t_wx = jnp.stack([_blocks_from_lanes(t_dwd[dd][:, LRU_W:]) for dd in (0, 1)])
    t_ba = jnp.stack([t_l[dd][0, :LRU_W] for dd in (0, 1)])
    t_bx = jnp.stack([t_l[dd][0, LRU_W:] for dd in (0, 1)])
    t_sp = jnp.stack([t_l[dd][1, :LRU_W] for dd in (0, 1)])
    dm_rows = jnp.stack([all_n1[:, 2, :], all_n1[:, 3, :], all_b[:, 3, :], all_b[:, 0, :], all_b[:, 1, :], all_b[:, 9, :]],
                        axis=1).reshape(N_DEV, 6 * d)
    dm_c = jnp.concatenate([t_n1[0], t_n1[1], jnp.zeros((4 * d,), F32)])
    dm16 = jnp.zeros((16, 6 * d), F32).at[0:8].set(dm_rows).at[8].set(dm_c)
    g_b_ada = jnp.sum(dm16, axis=0)[None, :]
    dm_cols = lax.dynamic_slice(dm16, (0, me * ada_cols), (16, ada_cols))
    g_w_ada, ds16 = _ada_bwd(s16, dm_cols, wa2d)
    ds_all = _all_gather_small(ds16[8:16], "gather_dsilu")
    dsilu_cc = _sum_parts(ds_all, "sum_dsilu")[0]
    sg_cc = jax.nn.sigmoid(c_ctx)
    g_c_ctx = dsilu_cc * (sg_cc * (1.0 + c_ctx * (1.0 - sg_cc)))

    g_ret_decay = jnp.sum(t_dlg, axis=-1).reshape(2, HEADS) * jax.nn.sigmoid(-ret_decay[0])
    g_lambda_full = -t_sp * jax.nn.sigmoid(-lam_full)

    def my_cols(full):
        return lax.dynamic_slice(full, (0, me * sc_loc), (full.shape[0], sc_loc))

    small_g = dict(
        c_ctx=g_c_ctx[None], b_ada=g_b_ada, norm1_g=t_n1[4:5], norm2_g=t_n2[2:3], ret_decay=g_ret_decay,
        conv_w=my_cols(t_conv[0:4]), conv_b=t_conv[4:5], lru_wa=t_wa.reshape(-1, LRU_BD), lru_ba=my_cols(t_ba),
        lru_wx=t_wx.reshape(-1, LRU_BD), lru_bx=my_cols(t_bx), lru_lambda=my_cols(g_lambda_full), final_g=t_f[0:1])
    small = dict(
        c_ctx=(c_ctx, m_c_ctx, v_c_ctx), b_ada=(b_ada, m_b_ada, v_b_ada), norm1_g=(norm1_g, m_norm1_g, v_norm1_g),
        norm2_g=(norm2_g, m_norm2_g, v_norm2_g), ret_decay=(ret_decay, m_ret_decay, v_ret_decay),
        conv_w=(conv_w, m_conv_w, v_conv_w), conv_b=(conv_b, m_conv_b, v_conv_b), lru_wa=(lru_wa, m_lru_wa, v_lru_wa),
        lru_ba=(lru_ba, m_lru_ba, v_lru_ba), lru_wx=(lru_wx, m_lru_wx, v_lru_wx), lru_bx=(lru_bx, m_lru_bx, v_lru_bx),
        lru_lambda=(lru_lambda, m_lru_lambda, v_lru_lambda), final_g=(final_g, m_final_g, v_final_g))
    names = list(small)
    items = [(small_g[k],) + tuple(a.reshape(small_g[k].shape) for a in small[k]) for k in names]
    res = {}
    for k, it, (d_, m_, v_) in zip(names, items, _adam_many(items, "adam_small")):
        shape = small[k][0].shape
        res[k] = tuple(a.reshape(shape) for a in (it[0], d_, m_, v_))

    def big(parts, w, m, v, name):
        out = _sum_adam(parts, w[0], m[0], v[0], name)
        return tuple(a[None] for a in out)

    res["w_ada"] = big([g_w_ada[None]], w_ada, m_w_ada, v_w_ada, "adam_w_ada")
    res["w_in"] = big([gwi_all], w_in, m_w_in, v_w_in, "adam_w_in")
    res["w_out"] = big([gwo_all], w_out, m_w_out, v_w_out, "adam_w_out")
    res["w_mlp1"] = big(gw1_all, w_mlp1, m_w_mlp1, v_w_mlp1, "adam_w_mlp1")
    res["w_mlp2"] = big([gw2_all], w_mlp2, m_w_mlp2, v_w_mlp2, "adam_w_mlp2")

    order = ["c_ctx", "w_ada", "b_ada", "norm1_g", "norm2_g", "w_in", "ret_decay", "conv_w", "conv_b", "lru_wa", "lru_ba",
             "lru_wx", "lru_bx", "lru_lambda", "w_out", "w_mlp1", "w_mlp2", "final_g"]
    outs = [loss, grad_x[None]]
    for j in range(4):
        outs += [res[k][j] for k in order]
    return tuple(outs)
```

```python
import jax
import jax.numpy as jnp
from jax import lax
from jax.experimental import pallas as pl
from jax.experimental.pallas import tpu as pltpu

F32 = jnp.float32
BF16 = jnp.bfloat16
AXES = ("x", "y", "c")
N_DEV = 8
MESH = pl.DeviceIdType.MESH

HEADS = 4
HEAD_DIM = 128
CHUNK = 128
RET_W = HEADS * HEAD_DIM
LRU_W = 512
LRU_BLOCKS = 8
LRU_BD = LRU_W // LRU_BLOCKS
LRU_C = 8.0
EPS = 1e-6
K_SCALE = HEAD_DIM ** -0.5
ROPE_BASE = 10000.0
GRID_W = 64
TM = 256
TL = 512
SUB = 8

ADAM_LR = 0.001
ADAM_B1 = 0.9
ADAM_B2 = 0.999
ADAM_EPS = 1e-08
ADAM_WD = 0.01
ADAM_STEP = 10

COL_G, COL_XR, COL_GATE = 0, 1, 2

R_CSH1, R_CSC1, R_SH1, R_SC1, R_G1, R_SH2, R_SC2, R_G2 = range(8)


def _pcall(body, **kw):
    return pl.pallas_call(body, **kw)


def _cp(vmem_mb=48):
    return pltpu.CompilerParams(vmem_limit_bytes=vmem_mb << 20)


def _sds(shape, dtype):
    return jax.ShapeDtypeStruct(shape, dtype)


def _dot(a, b):
    return jnp.dot(a.astype(BF16), b.astype(BF16), preferred_element_type=F32)


def _dot_nt(a, b):
    return lax.dot_general(a.astype(BF16), b.astype(BF16), (((1,), (1,)), ((), ())), preferred_element_type=F32)


def _dot_tn(a, b):
    return lax.dot_general(a.astype(BF16), b.astype(BF16), (((0,), (0,)), ((), ())), preferred_element_type=F32)


def _sigmoid(x):
    return 0.5 * jnp.tanh(0.5 * x) + 0.5


def _gelu(x):
    return 0.5 * x * (1.0 + jnp.tanh(0.7978845608028654 * (x + 0.044715 * x * x * x)))


def _dgelu(x):
    t = jnp.tanh(0.7978845608028654 * (x + 0.044715 * x * x * x))
    return 0.5 * (1.0 + t) + 0.5 * x * (1.0 - t * t) * 0.7978845608028654 * (1.0 + 3.0 * 0.044715 * x * x)


def _rows_iota(shape):
    return lax.broadcasted_iota(jnp.int32, shape, 0)


def _tile_order(dirn, s, cb, nb):
    if dirn == 0:
        return s
    return jnp.where(s < cb, cb - 1 - s, nb - 1 - (s - cb))


_SEMS = [pltpu.SemaphoreType.DMA((7,)), pltpu.SemaphoreType.DMA((7,)), pltpu.SemaphoreType.DMA(())]
_ANY = pl.BlockSpec(memory_space=pl.ANY)


def _gather_copies(x_ref, out_ref, send_sems, recv_sems, local_sem):
    mx, my, mc = lax.axis_index("x"), lax.axis_index("y"), lax.axis_index("c")
    me, sibling = (mx, my, mc), (mx, my, 1 - mc)
    chips = [(1 - mx, my), (mx, 1 - my), (1 - mx, 1 - my)]

    def slot(px, py, pc):
        return out_ref.at[4 * px + 2 * py + pc]

    def copy(k, block, to, src=None):
        return pltpu.make_async_remote_copy(
            src_ref=slot(*block) if src is None else src, dst_ref=slot(*block),
            send_sem=send_sems.at[k], recv_sem=recv_sems.at[k], device_id=to, device_id_type=MESH)

    mine = pltpu.make_async_copy(x_ref, slot(*me), local_sem)
    first = [copy(0, me, sibling, src=x_ref)] + [copy(1 + j, me, (*chip, mc), src=x_ref) for j, chip in enumerate(chips)]
    passed = [copy(4 + j, (*chip, mc), sibling) for j, chip in enumerate(chips)]
    recv_ici = [copy(1 + j, (*chip, mc), me) for j, chip in enumerate(chips)]
    recv_d2d = [copy(0, sibling, me)] + [copy(4 + j, (*chip, 1 - mc), me) for j, chip in enumerate(chips)]
    return mine, first, passed, recv_ici, recv_d2d


def _gather_start(*refs):
    mine, first, _, _, _ = _gather_copies(*refs)
    mine.start()
    for cp in first:
        cp.start()


def _gather_finish(*refs):
    mine, first, passed, recv_ici, recv_d2d = _gather_copies(*refs)
    for landed, onward in zip(recv_ici, passed):
        landed.wait_recv()
        onward.start()
    for landed in recv_d2d:
        landed.wait_recv()
    for cp in first + passed:
        cp.wait_send()
    mine.wait()


def _a2a_copies(g_ref, out_ref, send_sems, recv_sems, local_sem):
    mx, my, mc = lax.axis_index("x"), lax.axis_index("y"), lax.axis_index("c")
    me = 4 * mx + 2 * my + mc
    mine = pltpu.make_async_copy(g_ref.at[me], out_ref.at[me], local_sem)
    copies = []
    for k in range(1, N_DEV):
        px = 1 - mx if (k >> 2) & 1 else mx
        py = 1 - my if (k >> 1) & 1 else my
        pc = 1 - mc if k & 1 else mc
        copies.append(pltpu.make_async_remote_copy(
            src_ref=g_ref.at[4 * px + 2 * py + pc], dst_ref=out_ref.at[me],
            send_sem=send_sems.at[k - 1], recv_sem=recv_sems.at[k - 1],
            device_id=(px, py, pc), device_id_type=MESH))
    return mine, copies


def _a2a_start(*refs):
    mine, copies = _a2a_copies(*refs)
    mine.start()
    for cp in copies:
        cp.start()


def _a2a_finish(*refs):
    mine, copies = _a2a_copies(*refs)
    for cp in copies:
        cp.wait()
    mine.wait()


_EXCHANGES = {"gather": (_gather_start, _gather_finish), "a2a": (_a2a_start, _a2a_finish)}


def _exchange_shape(kind, src):
    return _sds((N_DEV,) + src.shape if kind == "gather" else src.shape, src.dtype)


def _all_gather_small(x, name):
    def body(x_ref, out_ref, send_sems, recv_sems, local_sem):
        mx, my, mc = lax.axis_index("x"), lax.axis_index("y"), lax.axis_index("c")
        me = 4 * mx + 2 * my + mc
        mine = pltpu.make_async_copy(x_ref, out_ref.at[me], local_sem)
        mine.start()
        copies = []
        for k in range(1, N_DEV):
            peer = (1 - mx if (k >> 2) & 1 else mx, 1 - my if (k >> 1) & 1 else my, 1 - mc if k & 1 else mc)
            copies.append(pltpu.make_async_remote_copy(
                src_ref=x_ref, dst_ref=out_ref.at[me], send_sem=send_sems.at[k - 1], recv_sem=recv_sems.at[k - 1],
                device_id=peer, device_id_type=MESH))
            copies[-1].start()
        for cp in copies:
            cp.wait()
        mine.wait()

    return _pcall(body, name=name, out_shape=_exchange_shape("gather", x), in_specs=[_ANY], out_specs=_ANY,
                  scratch_shapes=list(_SEMS))(x)


def _pcall_ride(body, ride, args, *, name, grid, in_specs, out_specs, out_shape, scratch_shapes=(), compiler_params=None):
    if ride is None:
        out = _pcall(body, name=name, grid=grid, in_specs=in_specs, out_specs=out_specs, out_shape=out_shape,
                     scratch_shapes=list(scratch_shapes), compiler_params=compiler_params)(*args)
        return out, None
    kind, src = ride
    start, finish = _EXCHANGES[kind]
    single = not isinstance(out_shape, (tuple, list))
    out_specs_t = (out_specs,) if single else tuple(out_specs)
    out_shape_t = (out_shape,) if single else tuple(out_shape)
    n_in, n_out, n_sc = len(in_specs), len(out_shape_t), len(scratch_shapes)

    def wrapped(*refs):
        ins, src_ref = refs[:n_in], refs[n_in]
        outs, dst_ref = refs[n_in + 1:n_in + 1 + n_out], refs[n_in + 1 + n_out]
        scratch = refs[n_in + 2 + n_out:n_in + 2 + n_out + n_sc]
        sems = refs[n_in + 2 + n_out + n_sc:]
        first = pl.program_id(0) == 0
        last = pl.program_id(0) == grid[0] - 1
        for ax in range(1, len(grid)):
            first = jnp.logical_and(first, pl.program_id(ax) == 0)
            last = jnp.logical_and(last, pl.program_id(ax) == grid[ax] - 1)

        @pl.when(first)
        def _():
            start(src_ref, dst_ref, *sems)

        body(*ins, *outs, *scratch)

        @pl.when(last)
        def _():
            finish(src_ref, dst_ref, *sems)

    res = _pcall(wrapped, name=name, grid=grid, in_specs=list(in_specs) + [_ANY], out_specs=out_specs_t + (_ANY,),
                 out_shape=out_shape_t + (_exchange_shape(kind, src),),
                 scratch_shapes=list(scratch_shapes) + list(_SEMS), compiler_params=compiler_params)(*args, src)
    return (res[0] if single else tuple(res[:-1])), res[-1]


SUB_ROWS = TM


class _Epilogue:
    def __init__(self, fn, args, in_specs, out_specs, out_shape, steps=None, lhs_map=None):
        self.fn, self.args, self.in_specs, self.out_specs, self.out_shape = fn, tuple(args), list(in_specs), out_specs, out_shape
        self.steps, self.lhs_map = steps, lhs_map


def _mm_nn(a, w, out_dtype, name, square_lhs=False, relu_out=False, ride=None, tm=TM, epi=None):
    m, k = a.shape
    nb, _, bn = w.shape
    tm = min(tm, m)
    n_extra = 0 if epi is None else len(epi.args)

    def body(*refs):
        a_ref, w_ref = refs[:2]
        av = a_ref[...]
        if square_lhs:
            av = av * av
        if epi is not None:
            assert nb == 1
            for r0 in range(0, tm, SUB_ROWS):
                rows = slice(r0, r0 + SUB_ROWS)
                epi.fn(jnp.dot(av[rows], w_ref[0], preferred_element_type=F32), rows, *refs[2:])
            return
        for j in range(nb):
            r = jnp.dot(av, w_ref[j], preferred_element_type=F32)
            if relu_out:
                r = jnp.maximum(r, 0.0)
            refs[2][:, j * bn:(j + 1) * bn] = r.astype(out_dtype)

    in_specs = [pl.BlockSpec((tm, k), lambda i: (i, 0)), pl.BlockSpec((nb, k, bn), lambda i: (0, 0, 0))]
    if epi is None:
        args, out_specs, out_shape = (a, w), pl.BlockSpec((tm, nb * bn), lambda i: (i, 0)), _sds((m, nb * bn), out_dtype)
    else:
        args, out_specs, out_shape = (a, w) + epi.args, epi.out_specs, epi.out_shape
        in_specs += epi.in_specs
    out, ex = _pcall_ride(body, ride, args, name=name, grid=(m // tm,), in_specs=in_specs, out_specs=out_specs,
                          out_shape=out_shape, compiler_params=_cp())
    return out if ride is None else (out, ex)


def _mm_nt(dy, w, out_dtype, name, relu_mul=None, ride=None, tm=TM, epi=None):
    m = dy.shape[0]
    nb, k, bn = w.shape
    tm = min(tm, m)
    n_extra = (0 if relu_mul is None else 1) + (0 if epi is None else len(epi.args))

    def body(*refs):
        dy_ref, w_ref = refs[:2]
        extra, outs, wt = refs[2:2 + n_extra], refs[2 + n_extra:-1], refs[-1]

        @pl.when(pl.program_id(0) == 0)
        def _():
            for j in range(nb):
                wt[j * bn:(j + 1) * bn, :] = w_ref[j].T

        if epi is not None:
            for r0 in range(0, tm, SUB_ROWS):
                rows = slice(r0, r0 + SUB_ROWS)
                epi.fn(jnp.dot(dy_ref[rows, :], wt[...], preferred_element_type=F32), rows, *extra, *outs)
            return
        acc = jnp.dot(dy_ref[...], wt[...], preferred_element_type=F32)
        if relu_mul is not None:
            acc = acc * (2.0 * extra[0][...].astype(F32))
        outs[0][...] = acc.astype(out_dtype)

    lhs_map = (lambda i: (i, 0)) if epi is None or epi.lhs_map is None else epi.lhs_map
    in_specs = [pl.BlockSpec((tm, nb * bn), lhs_map), pl.BlockSpec((nb, k, bn), lambda i: (0, 0, 0))]
    args = [dy, w]
    if relu_mul is not None:
        in_specs.append(pl.BlockSpec((tm, k), lambda i: (i, 0)))
        args.append(relu_mul)
    steps = m // tm
    if epi is None:
        out_specs, out_shape = pl.BlockSpec((tm, k), lambda i: (i, 0)), _sds((m, k), out_dtype)
    else:
        args += list(epi.args)
        in_specs += epi.in_specs
        out_specs, out_shape = epi.out_specs, epi.out_shape
        steps = steps if epi.steps is None else epi.steps
    out, ex = _pcall_ride(
        body, ride, args, name=name, grid=(steps,), in_specs=in_specs, out_specs=out_specs, out_shape=out_shape,
        scratch_shapes=[pltpu.VMEM((nb * bn, k), BF16)], compiler_params=_cp())
    return out if ride is None else (out, ex)


def _mm_tn(a, b, name, col_blocks, block, out_dtype, tm, square_lhs=False):
    m, k = a.shape
    nn = b.shape[1]
    steps = m // tm
    if col_blocks:
        nblk, acc_shape = nn // block, (k, block)
        a_spec = pl.BlockSpec((tm, k), lambda j, s: (s, 0))
        b_spec = pl.BlockSpec((tm, block), lambda j, s: (s, j))
    else:
        nblk, acc_shape = k // block, (block, nn)
        a_spec = pl.BlockSpec((tm, block), lambda j, s: (s, j))
        b_spec = pl.BlockSpec((tm, nn), lambda j, s: (s, 0))

    def body(a_ref, b_ref, o_ref, acc):
        s = pl.program_id(1)

        @pl.when(s == 0)
        def _():
            acc[...] = jnp.zeros_like(acc)

        av = a_ref[...]
        if square_lhs:
            av = av.astype(F32)
            av = (av * av).astype(BF16)
        acc[...] += _dot_tn(av, b_ref[...])

        @pl.when(s == steps - 1)
        def _():
            o_ref[...] = acc[...].astype(out_dtype)

    return _pcall(
        body, name=name, grid=(nblk, steps), in_specs=[a_spec, b_spec],
        out_specs=pl.BlockSpec((None,) + acc_shape, lambda j, s: (j, 0, 0)),
        out_shape=_sds((nblk,) + acc_shape, out_dtype),
        scratch_shapes=[pltpu.VMEM(acc_shape, F32)], compiler_params=_cp(),
    )(a, b)


def _mm_wgrad(at, b, name, bn, out_dtype, transpose_out=False, square_rhs=False, halves=False, ride=None):
    k, m = at.shape
    nblk = b.shape[1] // bn
    rows, cols = (bn, k) if transpose_out else (k, bn)
    nout = 2 if halves else 1
    per = rows // nout

    def body(a_ref, b_ref, *o_refs):
        bv = b_ref[...]
        if square_rhs:
            bv = bv * bv
        r = jnp.dot(a_ref[...], bv, preferred_element_type=F32)
        r = (r.T if transpose_out else r).astype(out_dtype)
        for i, o_ref in enumerate(o_refs):
            o_ref[...] = r[i * per:(i + 1) * per, :]

    out, ex = _pcall_ride(
        body, ride, (at, b), name=name, grid=(nblk,),
        in_specs=[pl.BlockSpec((k, m), lambda j: (0, 0)), pl.BlockSpec((m, bn), lambda j: (0, j))],
        out_specs=tuple(pl.BlockSpec((None, per, cols), lambda j: (j, 0, 0)) for _ in range(nout)),
        out_shape=tuple(_sds((nblk, per, cols), out_dtype) for _ in range(nout)), compiler_params=_cp())
    out = out if halves else out[0]
    return out if ride is None else (out, ex)


def _mm_in(hn, w, cosf, sins, ride):
    m, k = hn.shape
    nb, _, bn = w.shape

    def body(a_ref, w_ref, c_ref, s_ref, qkv_ref, rest_ref, pt):
        av = a_ref[...]
        for j in range(nb):
            pt[:, j * bn:(j + 1) * bn] = jnp.dot(av, w_ref[j], preferred_element_type=F32)
        cf, ss = c_ref[...], s_ref[...]
        for h in range(HEADS):
            sq = slice(h * HEAD_DIM, (h + 1) * HEAD_DIM)
            sk = slice(RET_W + h * HEAD_DIM, RET_W + (h + 1) * HEAD_DIM)
            qkv_ref[:, sq] = _rot(pt[:, sq], cf, ss).astype(BF16)
            qkv_ref[:, sk] = (_rot(pt[:, sk], cf, ss) * K_SCALE).astype(BF16)
        qkv_ref[:, 2 * RET_W:] = pt[:, 2 * RET_W:3 * RET_W].astype(BF16)
        rest_ref[...] = pt[:, 3 * RET_W:]

    tab = pl.BlockSpec((TM, HEAD_DIM), lambda i: (i, 0))
    wide = pl.BlockSpec((TM, 3 * RET_W), lambda i: (i, 0))
    return _pcall_ride(
        body, ride, (hn, w, cosf, sins), name="mm_in", grid=(m // TM,),
        in_specs=[pl.BlockSpec((TM, k), lambda i: (i, 0)), pl.BlockSpec((nb, k, bn), lambda i: (0, 0, 0)), tab, tab],
        out_specs=(wide, wide), out_shape=(_sds((m, 3 * RET_W), BF16), _sds((m, nb * bn - 3 * RET_W), F32)),
        scratch_shapes=[pltpu.VMEM((TM, nb * bn), F32)], compiler_params=_cp())


def _mod_part(s16, w_ada, b_cols):
    def body(s_ref, w_ref, b_ref, o_ref):
        o_ref[...] = _dot(s_ref[...], w_ref[...]) + b_ref[...]

    return _pcall(body, name="mod_part", out_shape=_sds((s16.shape[0], w_ada.shape[1]), F32),
                  compiler_params=_cp())(s16, w_ada, b_cols)


def _ada_bwd(s16, dm_cols, w_ada):
    def body(s_ref, d_ref, w_ref, gw_ref, ds_ref):
        gw_ref[...] = _dot_tn(s_ref[...], d_ref[...])
        ds_ref[...] = _dot_nt(d_ref[...], w_ref[...])

    return _pcall(body, name="ada_bwd",
                  out_shape=(_sds(w_ada.shape, F32), _sds(s16.shape, F32)), compiler_params=_cp())(s16, dm_cols, w_ada)


def _norm1_fwd(ctx, x, g, modrows, cb, ride=None):
    l_len, d = ctx.shape
    nb = (l_len + x.shape[0]) // TM

    def body(ctx_ref, x_ref, g_ref, m_ref, o_ref, ot_ref):
        is_ctx = pl.program_id(0) < cb
        xin = jnp.where(is_ctx, ctx_ref[...], x_ref[...])
        sh = jnp.where(is_ctx, m_ref[R_CSH1:R_CSH1 + 1, :], m_ref[R_SH1:R_SH1 + 1, :])
        sc = jnp.where(is_ctx, m_ref[R_CSC1:R_CSC1 + 1, :], m_ref[R_SC1:R_SC1 + 1, :])
        ms = jnp.mean(xin * xin, axis=-1, keepdims=True)
        n = xin * lax.rsqrt(ms + EPS) * g_ref[...]
        hn = n * (1.0 + sc) + sh
        o_ref[...] = hn.astype(BF16)
        ot_ref[...] = hn.T.astype(BF16)

    return _pcall_ride(
        body, ride, (ctx, x, g, modrows), name="norm1_fwd", grid=(nb,),
        in_specs=[pl.BlockSpec((TM, d), lambda i: (jnp.minimum(i, cb - 1), 0)),
                  pl.BlockSpec((TM, d), lambda i: (jnp.maximum(i - cb, 0), 0)),
                  pl.BlockSpec((1, d), lambda i: (0, 0)), pl.BlockSpec((8, d), lambda i: (0, 0))],
        out_specs=(pl.BlockSpec((TM, d), lambda i: (i, 0)), pl.BlockSpec((d, TM), lambda i: (0, i))),
        out_shape=(_sds((nb * TM, d), BF16), _sds((d, nb * TM), BF16)), compiler_params=_cp())


def _mix_fwd(o_f, o_b, p, h_f, h_b, cb, t_len):
    def body(of_ref, ob_ref, g_ref, gate_ref, hf_ref, hb_ref, mix_ref):
        o = of_ref[...] + ob_ref[...]
        g = g_ref[...]
        sg = g * _sigmoid(g)
        for hh in range(HEADS):
            sl = slice(hh * HEAD_DIM, (hh + 1) * HEAD_DIM)
            oh = o[:, sl]
            yc = oh - jnp.mean(oh, axis=-1, keepdims=True)
            var = jnp.mean(yc * yc, axis=-1, keepdims=True)
            mix_ref[:, sl] = (sg[:, sl] * (yc * lax.rsqrt(var + EPS))).astype(BF16)
        mix_ref[:, RET_W:] = ((hf_ref[...] + hb_ref[...]) * _gelu(gate_ref[...])).astype(BF16)

    row = lambda i: (i + cb, 0)
    return _pcall(
        body, name="mix_fwd", grid=(t_len // TM,),
        in_specs=[pl.BlockSpec((TM, RET_W), row), pl.BlockSpec((TM, RET_W), row),
                  pl.BlockSpec((TM, RET_W), lambda i: (i + cb, COL_G)), pl.BlockSpec((TM, LRU_W), lambda i: (i + cb, COL_GATE)),
                  pl.BlockSpec((TM, LRU_W), row), pl.BlockSpec((TM, LRU_W), row)],
        out_specs=pl.BlockSpec((TM, RET_W + LRU_W), lambda i: (i, 0)),
        out_shape=_sds((t_len, RET_W + LRU_W), BF16), compiler_params=_cp(),
    )(o_f, o_b, p, p, h_f, h_b)


def _res_norm2_epilogue(x, g, modrows, tm):
    t_len, d = x.shape
    tm = min(tm, t_len)

    def fn(y, rows, x_ref, g_ref, m_ref, y_ref, x1_ref, h2_ref, h2t_ref):
        y_ref[rows, :] = y
        x1 = x_ref[rows, :] + m_ref[R_G1:R_G1 + 1, :] * y
        ms = jnp.mean(x1 * x1, axis=-1, keepdims=True)
        n = x1 * lax.rsqrt(ms + EPS) * g_ref[...]
        x1_ref[rows, :] = x1
        h2 = n * (1.0 + m_ref[R_SC2:R_SC2 + 1, :]) + m_ref[R_SH2:R_SH2 + 1, :]
        h2_ref[rows, :] = h2.astype(BF16)
        h2t_ref[:, rows] = h2.T.astype(BF16)

    t = pl.BlockSpec((tm, d), lambda i: (i, 0))
    return _Epilogue(
        fn, (x, g, modrows),
        in_specs=[t, pl.BlockSpec((1, d), lambda i: (0, 0)), pl.BlockSpec((8, d), lambda i: (0, 0))],
        out_specs=(t, t, t, pl.BlockSpec((d, tm), lambda i: (0, i))),
        out_shape=(_sds((t_len, d), F32), _sds((t_len, d), F32), _sds((t_len, d), BF16), _sds((d, t_len), BF16)))


def _zero_at_start(acc_ref, rows):
    if rows.start == 0:
        @pl.when(pl.program_id(0) == 0)
        def _():
            acc_ref[...] = jnp.zeros_like(acc_ref)


def _final_epilogue(x1, target, fg, modrows, tm):
    t_len, d = x1.shape

    def fn(z, rows, x1_ref, t_ref, fg_ref, m_ref, dx2_ref, dz_ref, dzt_ref, acc_ref):
        _zero_at_start(acc_ref, rows)
        g2 = m_ref[R_G2:R_G2 + 1, :]
        x2 = x1_ref[rows, :] + g2 * z
        rstd = lax.rsqrt(jnp.mean(x2 * x2, axis=-1, keepdims=True) + EPS)
        xh = x2 * rstd
        fg = fg_ref[...]
        e = xh * fg - t_ref[rows, :]
        dy = e * (1.0 / d)
        dxh = dy * fg
        dx2 = rstd * (dxh - xh * jnp.mean(dxh * xh, axis=-1, keepdims=True))
        dx2_ref[rows, :] = dx2
        dz = g2 * dx2
        dz_ref[rows, :] = dz.astype(BF16)
        dzt_ref[:, rows] = dz.T.astype(BF16)
        acc_ref[0:1, :] += jnp.sum(dy * xh, axis=0, keepdims=True)
        acc_ref[1:2, :] += jnp.sum(dx2 * z, axis=0, keepdims=True)
        acc_ref[2:3, :] += jnp.sum(e * e, axis=0, keepdims=True)

    t = pl.BlockSpec((tm, d), lambda i: (i, 0))
    return _Epilogue(
        fn, (x1, target, fg, modrows),
        in_specs=[t, t, pl.BlockSpec((1, d), lambda i: (0, 0)), pl.BlockSpec((8, d), lambda i: (0, 0))],
        out_specs=(t, t, pl.BlockSpec((d, tm), lambda i: (0, i)), pl.BlockSpec((8, d), lambda i: (0, 0))),
        out_shape=(_sds((t_len, d), F32), _sds((t_len, d), BF16), _sds((d, t_len), BF16), _sds((8, d), F32)))


def _bwd_norm2_epilogue(x1, dx2, y, g, modrows, tm):
    t_len, d = x1.shape

    def fn(dh2, rows, x1_ref, dx2_ref, y_ref, g_ref, m_ref, dx1_ref, dy_ref, acc_ref):
        _zero_at_start(acc_ref, rows)
        x1 = x1_ref[rows, :]
        rstd = lax.rsqrt(jnp.mean(x1 * x1, axis=-1, keepdims=True) + EPS)
        xh = x1 * rstd
        gn = g_ref[...]
        dn = dh2 * (1.0 + m_ref[R_SC2:R_SC2 + 1, :])
        dxh = dn * gn
        dx1 = dx2_ref[rows, :] + rstd * (dxh - xh * jnp.mean(dxh * xh, axis=-1, keepdims=True))
        dx1_ref[rows, :] = dx1
        dy_ref[rows, :] = (m_ref[R_G1:R_G1 + 1, :] * dx1).astype(BF16)
        acc_ref[0:1, :] += jnp.sum(dh2, axis=0, keepdims=True)
        acc_ref[1:2, :] += jnp.sum(dh2 * xh * gn, axis=0, keepdims=True)
        acc_ref[2:3, :] += jnp.sum(dn * xh, axis=0, keepdims=True)
        acc_ref[3:4, :] += jnp.sum(dx1 * y_ref[rows, :], axis=0, keepdims=True)

    t = pl.BlockSpec((tm, d), lambda i: (i, 0))
    return _Epilogue(
        fn, (x1, dx2, y, g, modrows),
        in_specs=[t, t, t, pl.BlockSpec((1, d), lambda i: (0, 0)), pl.BlockSpec((8, d), lambda i: (0, 0))],
        out_specs=(t, t, pl.BlockSpec((8, d), lambda i: (0, 0))),
        out_shape=(_sds((t_len, d), F32), _sds((t_len, d), BF16), _sds((8, d), F32)))


def _mix_bwd_epilogue(o_f, o_b, p, h_f, h_b, cb):
    n = o_f.shape[0]

    def fn(dm, rows, of_ref, ob_ref, g_ref, gate_ref, hf_ref, hb_ref, do_ref, dg_ref, dgate_ref, dhs_ref):
        is_ctx = pl.program_id(0) < cb

        @pl.when(is_ctx)
        def _():
            for r in (do_ref, dg_ref, dgate_ref, dhs_ref):
                r[rows, :] = jnp.zeros((rows.stop - rows.start, r.shape[1]), r.dtype)

        @pl.when(jnp.logical_not(is_ctx))
        def _():
            o = of_ref[rows, :] + ob_ref[rows, :]
            g = g_ref[rows, :]
            s = _sigmoid(g)
            sg = g * s
            dsg = s * (1.0 + g * (1.0 - s))
            for hh in range(HEADS):
                sl = slice(hh * HEAD_DIM, (hh + 1) * HEAD_DIM)
                oh = o[:, sl]
                yc = oh - jnp.mean(oh, axis=-1, keepdims=True)
                rs = lax.rsqrt(jnp.mean(yc * yc, axis=-1, keepdims=True) + EPS)
                gn = yc * rs
                dret = dm[:, sl]
                dgn = dret * sg[:, sl]
                dg_ref[rows, sl] = (dret * gn * dsg[:, sl]).astype(BF16)
                do_ref[rows, sl] = (rs * (dgn - jnp.mean(dgn, axis=-1, keepdims=True)
                                          - gn * jnp.mean(dgn * gn, axis=-1, keepdims=True))).astype(BF16)
            dlru = dm[:, RET_W:]
            gate = gate_ref[rows, :]
            dhs_ref[rows, :] = dlru * _gelu(gate)
            dgate_ref[rows, :] = (dlru * (hf_ref[rows, :] + hb_ref[rows, :]) * _dgelu(gate)).astype(BF16)

    t = pl.BlockSpec((TM, RET_W), lambda i: (i, 0))
    return _Epilogue(
        fn, (o_f, o_b, p, p, h_f, h_b),
        in_specs=[t, t, pl.BlockSpec((TM, RET_W), lambda i: (i, COL_G)), pl.BlockSpec((TM, LRU_W), lambda i: (i, COL_GATE)), t, t],
        out_specs=(t, t, t, t), out_shape=(_sds((n, RET_W), BF16),) * 3 + (_sds((n, RET_W), F32),),
        steps=n // TM, lhs_map=lambda i: (jnp.maximum(i - cb, 0), 0))


def _bwd_norm1_epilogue(ctx, x, dx1, g, modrows, cb):
    t_len, d = x.shape

    def fn(dh, rows, ctx_ref, x_ref, dx1_ref, g_ref, m_ref, gx_ref, acc_ref):
        is_ctx = pl.program_id(0) < cb
        _zero_at_start(acc_ref, rows)
        xin = jnp.where(is_ctx, ctx_ref[rows, :], x_ref[rows, :])
        sc = jnp.where(is_ctx, m_ref[R_CSC1:R_CSC1 + 1, :], m_ref[R_SC1:R_SC1 + 1, :])
        rstd = lax.rsqrt(jnp.mean(xin * xin, axis=-1, keepdims=True) + EPS)
        xh = xin * rstd
        gn = g_ref[...]
        dn = dh * (1.0 + sc)
        dxh = dn * gn
        gx_ref[rows, :] = dx1_ref[rows, :] + rstd * (dxh - xh * jnp.mean(dxh * xh, axis=-1, keepdims=True))
        s0 = jnp.sum(dh, axis=0, keepdims=True)
        s1 = jnp.sum(dh * xh * gn, axis=0, keepdims=True)
        acc_ref[4:5, :] += jnp.sum(dn * xh, axis=0, keepdims=True)

        @pl.when(is_ctx)
        def _():
            acc_ref[0:1, :] += s0
            acc_ref[1:2, :] += s1

        @pl.when(jnp.logical_not(is_ctx))
        def _():
            acc_ref[2:3, :] += s0
            acc_ref[3:4, :] += s1

    lat = pl.BlockSpec((TM, d), lambda i: (jnp.maximum(i - cb, 0), 0))
    return _Epilogue(
        fn, (ctx, x, dx1, g, modrows),
        in_specs=[pl.BlockSpec((TM, d), lambda i: (jnp.minimum(i, cb - 1), 0)),
                  lat, lat, pl.BlockSpec((1, d), lambda i: (0, 0)), pl.BlockSpec((8, d), lambda i: (0, 0))],
        out_specs=(lat, pl.BlockSpec((8, d), lambda i: (0, 0))),
        out_shape=(_sds((t_len, d), F32), _sds((8, d), F32)))


def _rot(x, cf, ss):
    return x * cf + pltpu.roll(x, HEAD_DIM // 2, 1) * ss


def _decay_exponents(dirn):
    ii = lax.broadcasted_iota(jnp.int32, (CHUNK, CHUNK), 0)
    jj = lax.broadcasted_iota(jnp.int32, (CHUNK, CHUNK), 1)
    rel = ii - jj if dirn == 0 else jj - ii
    pos = ii.astype(F32)
    if dirn == 0:
        cq, cs = pos + 1.0, (CHUNK - 1.0) - pos
    else:
        cq, cs = CHUNK - pos, pos
    return rel, jnp.maximum(rel, 0).astype(F32), cq, cs


def _store_decay(lg_ref, dec, relf_ref=None):
    for dirn in (0, 1):
        rel, relf, cq, cs = _decay_exponents(dirn)
        if relf_ref is not None:
            relf_ref[dirn] = relf
        for h in range(HEADS):
            lgv = lg_ref[dirn, h]
            wq, ws = jnp.exp(lgv * cq), jnp.exp(lgv * cs)
            dec[dirn, h, 0] = jnp.where(rel >= 0, jnp.exp(lgv * relf), 0.0)
            dec[dirn, h, 1] = wq
            dec[dirn, h, 2] = ws
            if relf_ref is not None:
                dec[dirn, h, 3] = wq * cq
                dec[dirn, h, 4] = ws * cs


def _ret_rows(cc, nc, step_of):
    return [lambda s, dirn=dirn: _tile_order(dirn, step_of(s), cc, nc) for dirn in (0, 1)]


def _ret_in_specs(rows):
    specs = []
    for row in rows:
        specs += [pl.BlockSpec((CHUNK, RET_W), lambda s, o=o, row=row: (row(s), o)) for o in (0, 1, 2)]
    return specs


def _ret_fwd(qkv, lg, cc, ride=None):
    n = qkv.shape[0]
    nc = n // CHUNK
    rows = _ret_rows(cc, nc, lambda s: s)

    def body(lg_ref, q0, k0, v0, q1, k1, v1, o0, o1, sp0, sp1, st, dec):
        @pl.when(pl.program_id(0) == 0)
        def _():
            st[...] = jnp.zeros_like(st)
            _store_decay(lg_ref, dec)

        for dirn, (q_ref, k_ref, v_ref, o_ref, sp_ref) in enumerate(((q0, k0, v0, o0, sp0), (q1, k1, v1, o1, sp1))):
            for h in range(HEADS):
                sl = slice(h * HEAD_DIM, (h + 1) * HEAD_DIM)
                q, k, v = q_ref[:, sl], k_ref[:, sl], v_ref[:, sl]
                sp = st[dirn, h]
                sp_ref[h] = sp
                o_ref[:, sl] = _dot(_dot_nt(q, k) * dec[dirn, h, 0], v) + _dot(q * dec[dirn, h, 1], sp)
                st[dirn, h] = jnp.exp(lg_ref[dirn, h] * CHUNK) * sp + _dot_tn(k * dec[dirn, h, 2], v)

    o_specs = [pl.BlockSpec((CHUNK, RET_W), lambda s, row=row: (row(s), 0)) for row in rows]
    state = pl.BlockSpec((None, HEADS, CHUNK, HEAD_DIM), lambda s: (s, 0, 0, 0))
    return _pcall_ride(
        body, ride, (lg,) + (qkv,) * 6, name="ret_fwd", grid=(nc,),
        in_specs=[pl.BlockSpec(memory_space=pltpu.SMEM)] + _ret_in_specs(rows),
        out_specs=(o_specs[0], o_specs[1], state, state),
        out_shape=(_sds((n, RET_W), F32),) * 2 + (_sds((nc, HEADS, CHUNK, HEAD_DIM), F32),) * 2,
        scratch_shapes=[pltpu.VMEM((2, HEADS, CHUNK, HEAD_DIM), F32), pltpu.VMEM((2, HEADS, 3, CHUNK, CHUNK), F32)],
        compiler_params=_cp())


def _ret_bwd(qkv, lg, do, s_prev, cc, ride=None):
    n = qkv.shape[0]
    nc = n // CHUNK
    rows = _ret_rows(cc, nc, lambda s: nc - 1 - s)

    def body(lg_ref, q0, k0, v0, q1, k1, v1, do0, do1, sp0, sp1, dq0, dk0, dv0, dq1, dk1, dv1, dlg_ref, dst, dec, relf):
        @pl.when(pl.program_id(0) == 0)
        def _():
            dst[...] = jnp.zeros_like(dst)
            dlg_ref[...] = jnp.zeros_like(dlg_ref)
            _store_decay(lg_ref, dec, relf)

        for dirn, (q_ref, k_ref, v_ref, do_ref, sp_ref, dq_ref, dk_ref, dv_ref) in enumerate(
                ((q0, k0, v0, do0, sp0, dq0, dk0, dv0), (q1, k1, v1, do1, sp1, dq1, dk1, dv1))):
            for h in range(HEADS):
                sl = slice(h * HEAD_DIM, (h + 1) * HEAD_DIM)
                q, k, v = q_ref[:, sl], k_ref[:, sl], v_ref[:, sl]
                dov = do_ref[:, sl]
                dm, wq, ws = dec[dirn, h, 0], dec[dirn, h, 1], dec[dirn, h, 2]
                a = _dot_nt(q, k)
                sp = sp_ref[h]
                dsn = dst[dirn, h]
                gc = jnp.exp(lg_ref[dirn, h] * CHUNK)
                g1 = _dot_nt(dov, sp)
                da = _dot_nt(dov, v) * dm
                dq = g1 * wq + _dot(da, k)
                h1 = _dot_nt(v, dsn)
                dk = _dot_tn(da, q) + h1 * ws
                dv_ref[:, sl] = (_dot_tn(a * dm, dov) + _dot(k * ws, dsn)).astype(BF16)
                dst[dirn, h] = gc * dsn + _dot_tn(q * wq, dov)
                term = da * a * relf[dirn] + q * g1 * dec[dirn, h, 3] + k * h1 * dec[dirn, h, 4] + sp * dsn * (CHUNK * gc)
                dlg_ref[dirn * HEADS + h:dirn * HEADS + h + 1, 0:HEAD_DIM] += jnp.sum(term, axis=0, keepdims=True)
                dq_ref[:, sl] = dq.astype(BF16)
                dk_ref[:, sl] = dk.astype(BF16)

    wide = [pl.BlockSpec((CHUNK, RET_W), lambda s, row=row: (row(s), 0)) for row in rows]
    state = pl.BlockSpec((None, HEADS, CHUNK, HEAD_DIM), lambda s: (nc - 1 - s, 0, 0, 0))
    return _pcall_ride(
        body, ride, (lg,) + (qkv,) * 6 + (do, do, s_prev[0], s_prev[1]), name="ret_bwd", grid=(nc,),
        in_specs=[pl.BlockSpec(memory_space=pltpu.SMEM)] + _ret_in_specs(rows) + wide + [state, state],
        out_specs=(wide[0],) * 3 + (wide[1],) * 3 + (pl.BlockSpec((2 * HEADS, 8 * HEAD_DIM), lambda s: (0, 0)),),
        out_shape=(_sds((n, RET_W), BF16),) * 6 + (_sds((2 * HEADS, 8 * HEAD_DIM), F32),),
        scratch_shapes=[pltpu.VMEM((2, HEADS, CHUNK, HEAD_DIM), F32), pltpu.VMEM((2, HEADS, 5, CHUNK, CHUNK), F32),
                        pltpu.VMEM((2, CHUNK, CHUNK), F32)],
        compiler_params=_cp())


def _shift_rows(cur, prev8, next8, k, seg_start, seg_end):
    tm = cur.shape[0]
    rows = _rows_iota(cur.shape)
    if k < 0:
        out = pltpu.roll(cur, -k, 0)
        for j in range(-k):
            halo = jnp.where(seg_start, 0.0, prev8[SUB + k + j:SUB + k + j + 1, :])
            out = jnp.where(rows == j, halo, out)
    else:
        out = pltpu.roll(cur, tm - k, 0)
        for j in range(k):
            halo = jnp.where(seg_end, 0.0, next8[j:j + 1, :])
            out = jnp.where(rows == tm - k + j, halo, out)
    return out


def _seg_flags(t, cb, nb):
    return jnp.logical_or(t == 0, t == cb), jnp.logical_or(t == cb - 1, t == nb - 1)


def _halo_specs(tile_of, n_rows, col):
    per = TM // SUB
    return [pl.BlockSpec((TM, LRU_W), lambda s: (tile_of(s), col)),
            pl.BlockSpec((SUB, LRU_W), lambda s: (jnp.maximum(tile_of(s) * per - 1, 0), col)),
            pl.BlockSpec((SUB, LRU_W), lambda s: (jnp.minimum((tile_of(s) + 1) * per, n_rows // SUB - 1), col))]


def _lru_gates(xr, prev8, next8, seg_start, seg_end, cw_ref, cb_ref, wg_ref, bg_ref, sp_ref):
    xm1 = _shift_rows(xr, prev8, next8, -1, seg_start, seg_end)
    xp1 = _shift_rows(xr, prev8, next8, 1, seg_start, seg_end)
    xp2 = _shift_rows(xr, prev8, next8, 2, seg_start, seg_end)
    xc = cb_ref[...] + xm1 * cw_ref[0:1, :] + xr * cw_ref[1:2, :] + xp1 * cw_ref[2:3, :] + xp2 * cw_ref[3:4, :]
    pre = _dot(xc, wg_ref[...]) + bg_ref[...]
    r = _sigmoid(pre[:, :LRU_W])
    i = _sigmoid(pre[:, LRU_W:])
    la = (-LRU_C) * r * sp_ref[...]
    a = jnp.exp(la)
    th = jnp.tanh(la)
    sq = jnp.sqrt(-2.0 * th / (1.0 - th))
    return xc, r, i, a, sq


def _scan_tile(a, b, ascending, a_sc, b_sc, carry, out_ref):
    tm, w = a.shape
    nsub = tm // SUB
    a = a.reshape(nsub, SUB, w)
    b = b.reshape(nsub, SUB, w)
    r8 = lax.broadcasted_iota(jnp.int32, a.shape, 1)
    for k in (1, 2, 4):
        if ascending:
            m = r8 >= k
            a_s, b_s = pltpu.roll(a, k, 1), pltpu.roll(b, k, 1)
        else:
            m = r8 < SUB - k
            a_s, b_s = pltpu.roll(a, SUB - k, 1), pltpu.roll(b, SUB - k, 1)
        b = a * jnp.where(m, b_s, 0.0) + b
        a = a * jnp.where(m, a_s, 1.0)
    a_sc[...] = a.reshape(tm, w)
    b_sc[...] = b.reshape(tm, w)

    def step(j, c):
        off = pl.multiple_of((j if ascending else nsub - 1 - j) * SUB, SUB)
        hb = a_sc[pl.ds(off, SUB), :] * c + b_sc[pl.ds(off, SUB), :]
        out_ref[pl.ds(off, SUB), :] = hb
        last = hb[SUB - 1:SUB, :] if ascending else hb[0:1, :]
        return jnp.broadcast_to(last, c.shape)

    carry[...] = lax.fori_loop(0, nsub, step, carry[...])


def _lru_fwd(p, wg, bg, sp, cw, cbias, dirn, cb, ride=None):
    n = p.shape[0]
    nb = n // TM
    tile_of = lambda s: _tile_order(dirn, s, cb, nb)

    def body(x_ref, xp_ref, xn_ref, wg_ref, bg_ref, sp_ref, cw_ref, cb_ref, h_ref, cin_ref, carry, a_sc, b_sc):
        s = pl.program_id(0)

        @pl.when(s == 0)
        def _():
            carry[...] = jnp.zeros_like(carry)

        seg_start, seg_end = _seg_flags(tile_of(s), cb, nb)
        xc, r, i, a, sq = _lru_gates(x_ref[...], xp_ref[...], xn_ref[...], seg_start, seg_end,
                                     cw_ref, cb_ref, wg_ref, bg_ref, sp_ref)
        cin_ref[...] = carry[...]
        _scan_tile(a, sq * (i * xc), dirn == 0, a_sc, b_sc, carry, h_ref)

    full = lambda shape: pl.BlockSpec(shape, lambda s: (0,) * len(shape))
    return _pcall_ride(
        body, ride, (p, p, p, wg, bg, sp, cw, cbias), name=f"lru_fwd{dirn}", grid=(nb,),
        in_specs=_halo_specs(tile_of, n, COL_XR) + [full((LRU_W, 2 * LRU_W)), full((1, 2 * LRU_W)), full((1, LRU_W)),
                                               full((4, LRU_W)), full((1, LRU_W))],
        out_specs=(pl.BlockSpec((TM, LRU_W), lambda s: (tile_of(s), 0)),
                   pl.BlockSpec((None, SUB, LRU_W), lambda s: (tile_of(s), 0, 0))),
        out_shape=(_sds((n, LRU_W), F32), _sds((nb, SUB, LRU_W), F32)),
        scratch_shapes=[pltpu.VMEM((SUB, LRU_W), F32), pltpu.VMEM((TM, LRU_W), F32), pltpu.VMEM((TM, LRU_W), F32)],
        compiler_params=_cp())


def _lru_bwd(p, wg, bg, sp, cw, cbias, h, cin, dhs, dirn, cb, ride=None):
    n = p.shape[0]
    nb = n // TM
    tile_of = lambda s: _tile_order(dirn, nb - 1 - s, cb, nb)

    def body(x_ref, xp_ref, xn_ref, wg_ref, bg_ref, sp_ref, cw_ref, cb_ref, h_ref, cin_ref, dhs_ref,
             dxc_ref, dwd_ref, acc_ref, carry, a_sc, b_sc, mu_sc, dwg_ref):
        s = pl.program_id(0)

        @pl.when(s == 0)
        def _():
            carry[...] = jnp.zeros_like(carry)
            dwg_ref[...] = jnp.zeros_like(dwg_ref)
            acc_ref[...] = jnp.zeros_like(acc_ref)

        seg_start, seg_end = _seg_flags(tile_of(s), cb, nb)
        xc, r, i, a, sq = _lru_gates(x_ref[...], xp_ref[...], xn_ref[...], seg_start, seg_end,
                                     cw_ref, cb_ref, wg_ref, bg_ref, sp_ref)
        rows = _rows_iota(a.shape)
        hv = h_ref[...]
        dh = dhs_ref[...]
        mu_next = carry[0:1, :]
        _scan_tile(a, a * dh, dirn == 1, a_sc, b_sc, carry, mu_sc)
        mu = mu_sc[...]
        if dirn == 0:
            hprev = jnp.where(rows == 0, cin_ref[0:1, :], pltpu.roll(hv, 1, 0))
            lam = dh + jnp.where(rows == TM - 1, mu_next, pltpu.roll(mu, TM - 1, 0))
        else:
            hprev = jnp.where(rows == TM - 1, cin_ref[0:1, :], pltpu.roll(hv, TM - 1, 0))
            lam = dh + jnp.where(rows == 0, mu_next, pltpu.roll(mu, 1, 0))
        ds = lam * (i * xc)
        di = lam * (sq * xc)
        dla = lam * hprev * a - ds * (a * a) / jnp.maximum(sq, 1e-20)
        dpr = dla * ((-LRU_C) * sp_ref[...]) * r * (1.0 - r)
        dpi = di * i * (1.0 - i)
        dpre = jnp.concatenate([dpr, dpi], axis=1)
        dxc_ref[...] = lam * (sq * i) + _dot_nt(dpre, wg_ref[...])
        dwg_ref[...] += _dot_tn(xc, dpre)
        acc_ref[0:1, :] += jnp.sum(dpre, axis=0, keepdims=True)
        acc_ref[1:2, 0:LRU_W] += jnp.sum(dla * ((-LRU_C) * r), axis=0, keepdims=True)

        @pl.when(s == nb - 1)
        def _():
            low = lax.broadcasted_iota(jnp.int32, (LRU_BD, 2 * LRU_BD), 1) < LRU_BD
            for half in (0, LRU_W):
                for m in range(LRU_BLOCKS // 2):
                    lanes = slice(half + 2 * LRU_BD * m, half + 2 * LRU_BD * (m + 1))
                    even = dwg_ref[2 * m * LRU_BD:(2 * m + 1) * LRU_BD, lanes]
                    odd = dwg_ref[(2 * m + 1) * LRU_BD:(2 * m + 2) * LRU_BD, lanes]
                    dwd_ref[:, lanes] = jnp.where(low, even, odd)

    full = lambda shape: pl.BlockSpec(shape, lambda s: (0,) * len(shape))
    tile = pl.BlockSpec((TM, LRU_W), lambda s: (tile_of(s), 0))
    return _pcall_ride(
        body, ride, (p, p, p, wg, bg, sp, cw, cbias, h, cin, dhs), name=f"lru_bwd{dirn}", grid=(nb,),
        in_specs=_halo_specs(tile_of, n, COL_XR) + [full((LRU_W, 2 * LRU_W)), full((1, 2 * LRU_W)), full((1, LRU_W)),
                                               full((4, LRU_W)), full((1, LRU_W)), tile,
                                               pl.BlockSpec((None, SUB, LRU_W), lambda s: (tile_of(s), 0, 0)), tile],
        out_specs=(tile, full((LRU_BD, 2 * LRU_W)), full((8, 2 * LRU_W))),
        out_shape=(_sds((n, LRU_W), F32), _sds((LRU_BD, 2 * LRU_W), F32), _sds((8, 2 * LRU_W), F32)),
        scratch_shapes=[pltpu.VMEM((SUB, LRU_W), F32)] + [pltpu.VMEM((TM, LRU_W), F32)] * 3
        + [pltpu.VMEM((LRU_W, 2 * LRU_W), F32)],
        compiler_params=_cp())


def _assemble_dp(dqs, dks, dvs, dg, dgate, dxcs, p, cw, cosf, sins, cb, ride=None):
    n = p.shape[0]
    nb = n // TM
    tile_of = lambda s: s

    def body(dqf, dqb, dkf, dkb, dvf, dvb, dg_ref, dgate_ref, cf, pf, nf, cb_, pb, nb_, x_ref, xp_ref, xn_ref,
             cw_ref, cos_ref, sin_ref, dp_ref, acc_ref):
        s = pl.program_id(0)

        @pl.when(s == 0)
        def _():
            acc_ref[...] = jnp.zeros_like(acc_ref)

        seg_start, seg_end = _seg_flags(s, cb, nb)
        dq = dqf[...].astype(F32) + dqb[...].astype(F32)
        dk = dkf[...].astype(F32) + dkb[...].astype(F32)
        cosv, sinv = cos_ref[...], sin_ref[...]
        for h in range(HEADS):
            sl = slice(h * HEAD_DIM, (h + 1) * HEAD_DIM)
            sk = slice(RET_W + h * HEAD_DIM, RET_W + (h + 1) * HEAD_DIM)
            dp_ref[:, sl] = (dq[:, sl] * cosv + pltpu.roll(dq[:, sl] * sinv, HEAD_DIM // 2, 1)).astype(BF16)
            dp_ref[:, sk] = ((dk[:, sl] * cosv + pltpu.roll(dk[:, sl] * sinv, HEAD_DIM // 2, 1)) * K_SCALE).astype(BF16)
        dp_ref[:, 2 * RET_W:3 * RET_W] = (dvf[...].astype(F32) + dvb[...].astype(F32)).astype(BF16)
        dp_ref[:, 3 * RET_W:4 * RET_W] = dg_ref[...].astype(BF16)
        dxc = cf[...] + cb_[...]
        dprev = pf[...] + pb[...]
        dnext = nf[...] + nb_[...]
        dxr = (_shift_rows(dxc, dprev, dnext, 1, seg_start, seg_end) * cw_ref[0:1, :] + dxc * cw_ref[1:2, :]
               + _shift_rows(dxc, dprev, dnext, -1, seg_start, seg_end) * cw_ref[2:3, :]
               + _shift_rows(dxc, dprev, dnext, -2, seg_start, seg_end) * cw_ref[3:4, :])
        dp_ref[:, 4 * RET_W:4 * RET_W + LRU_W] = dxr.astype(BF16)
        dp_ref[:, 4 * RET_W + LRU_W:] = dgate_ref[...].astype(BF16)
        xr, xp, xn = x_ref[...], xp_ref[...], xn_ref[...]
        for j, k in enumerate((-1, 0, 1, 2)):
            xs = xr if k == 0 else _shift_rows(xr, xp, xn, k, seg_start, seg_end)
            acc_ref[j:j + 1, 0:LRU_W] += jnp.sum(dxc * xs, axis=0, keepdims=True)
        acc_ref[4:5, 0:LRU_W] += jnp.sum(dxc, axis=0, keepdims=True)

    t = pl.BlockSpec((TM, RET_W), lambda s: (s, 0))
    args = (dqs[0], dqs[1], dks[0], dks[1], dvs[0], dvs[1], dg, dgate, dxcs[0], dxcs[0], dxcs[0], dxcs[1], dxcs[1], dxcs[1],
            p, p, p, cw, cosf, sins)
    tab = pl.BlockSpec((TM, HEAD_DIM), lambda s: (s, 0))
    return _pcall_ride(
        body, ride, args, name="assemble_dp", grid=(nb,),
        in_specs=[t] * 8 + _halo_specs(tile_of, n, 0) * 2 + _halo_specs(tile_of, n, COL_XR)
        + [pl.BlockSpec((4, LRU_W), lambda s: (0, 0)), tab, tab],
        out_specs=(pl.BlockSpec((TM, 4 * RET_W + 2 * LRU_W), lambda s: (s, 0)), pl.BlockSpec((8, 2 * LRU_W), lambda s: (0, 0))),
        out_shape=(_sds((n, 4 * RET_W + 2 * LRU_W), BF16), _sds((8, 2 * LRU_W), F32)), compiler_params=_cp())


def _adamw(g, w, m, v):
    nm = ADAM_B1 * m + (1.0 - ADAM_B1) * g
    nv = ADAM_B2 * v + (1.0 - ADAM_B2) * (g * g)
    m_hat = nm / (1.0 - ADAM_B1 ** ADAM_STEP)
    v_hat = nv / (1.0 - ADAM_B2 ** ADAM_STEP)
    return (-ADAM_LR) * (m_hat / (jnp.sqrt(v_hat) + ADAM_EPS) + ADAM_WD * w), nm, nv


def _adam_many(items, name):
    n = len(items)

    def body(*refs):
        for i in range(n):
            g, w, m, v = (r[...] for r in refs[4 * i:4 * i + 4])
            for o_ref, val in zip(refs[4 * n + 3 * i:4 * n + 3 * i + 3], _adamw(g, w, m, v)):
                o_ref[...] = val

    out_shape = tuple(_sds(it[1].shape, F32) for it in items for _ in range(3))
    res = _pcall(body, name=name, out_shape=out_shape, compiler_params=_cp())(*[a for it in items for a in it])
    return [tuple(res[3 * i:3 * i + 3]) for i in range(n)]


def _sum_adam(parts_list, w, m, v, name):
    nparts, _, c = parts_list[0].shape
    r = w.shape[0]
    tr = min(parts_list[0].shape[1], 128)
    starts, o = [], 0
    for pa in parts_list:
        starts.append(o)
        o += pa.shape[1] // tr
    nseg = len(parts_list)

    def body(*refs):
        p_refs = refs[:nseg]
        w_ref, m_ref, v_ref, g_ref, d_ref, nm_ref, nv_ref = refs[nseg:]
        i = pl.program_id(0)
        for s, p_ref in enumerate(p_refs):
            end = starts[s + 1] if s + 1 < nseg else r // tr

            @pl.when(jnp.logical_and(i >= starts[s], i < end))
            def _():
                g = p_ref[0].astype(F32)
                for j in range(1, nparts):
                    g = g + p_ref[j].astype(F32)
                g_ref[...] = g
                d_ref[...], nm_ref[...], nv_ref[...] = _adamw(g, w_ref[...], m_ref[...], v_ref[...])

    def seg_spec(s):
        last = parts_list[s].shape[1] // tr - 1
        return pl.BlockSpec((nparts, tr, c), lambda i: (0, jnp.clip(i - starts[s], 0, last), 0))

    t = pl.BlockSpec((tr, c), lambda i: (i, 0))
    return _pcall(
        body, name=name, grid=(r // tr,),
        in_specs=[seg_spec(s) for s in range(nseg)] + [t, t, t],
        out_specs=(t, t, t, t), out_shape=(_sds((r, c), F32),) * 4, compiler_params=_cp(),
    )(*parts_list, w, m, v)


def _sum_parts(parts, name):
    nparts, r, c = parts.shape

    def body(p_ref, o_ref):
        g = p_ref[0]
        for j in range(1, nparts):
            g = g + p_ref[j]
        o_ref[...] = g

    return _pcall(body, name=name, out_shape=_sds((r, c), parts.dtype), compiler_params=_cp())(parts)


def _rot_tables(l_len, t_len):
    rows = t_len // GRID_W
    n_freq = HEAD_DIM // 4
    inv = ROPE_BASE ** (-jnp.arange(n_freq, dtype=F32) / n_freq)
    ang_r = jnp.arange(rows, dtype=F32)[:, None] * inv
    ang_c = jnp.arange(GRID_W, dtype=F32)[:, None] * inv
    cos = jnp.concatenate([jnp.repeat(jnp.cos(ang_r), GRID_W, axis=0), jnp.tile(jnp.cos(ang_c), (rows, 1))], axis=-1)
    sin = jnp.concatenate([jnp.repeat(jnp.sin(ang_r), GRID_W, axis=0), jnp.tile(jnp.sin(ang_c), (rows, 1))], axis=-1)
    cosf = jnp.concatenate([jnp.ones((l_len, HEAD_DIM), F32), jnp.concatenate([cos, cos], axis=-1)], axis=0)
    sins = jnp.concatenate([jnp.zeros((l_len, HEAD_DIM), F32), jnp.concatenate([-sin, sin], axis=-1)], axis=0)
    return cosf, sins


def _block_diag(w):
    eye = jnp.eye(LRU_BLOCKS, dtype=w.dtype)
    return (w[:, :, None, :] * eye[:, None, :, None]).reshape(LRU_W, LRU_W)


def _blocks_from_lanes(dwd_half):
    return dwd_half.reshape(LRU_BD, LRU_BLOCKS, LRU_BD).transpose(1, 0, 2)


def _silu(x):
    return x * jax.nn.sigmoid(x)


def kernel(x, c, ctx, c_ctx, w_ada, b_ada, norm1_g, norm2_g, w_in, ret_decay, conv_w, conv_b, lru_wa, lru_ba, lru_wx, lru_bx, lru_lambda, w_out, w_mlp1, w_mlp2, final_g, loss_target, m_c_ctx, m_w_ada, m_b_ada, m_norm1_g, m_norm2_g, m_w_in, m_ret_decay, m_conv_w, m_conv_b, m_lru_wa, m_lru_ba, m_lru_wx, m_lru_bx, m_lru_lambda, m_w_out, m_w_mlp1, m_w_mlp2, m_final_g, v_c_ctx, v_w_ada, v_b_ada, v_norm1_g, v_norm2_g, v_w_in, v_ret_decay, v_conv_w, v_conv_b, v_lru_wa, v_lru_ba, v_lru_wx, v_lru_bx, v_lru_lambda, v_w_out, v_w_mlp1, v_w_mlp2, v_final_g):
    t_len, d = x.shape[1], x.shape[2]
    l_len = ctx.shape[1]
    cb, cc = l_len // TM, l_len // CHUNK
    me = 4 * lax.axis_index("x") + 2 * lax.axis_index("y") + lax.axis_index("c")
    x2d, ctx2d, tgt2d = x[0], ctx[0], loss_target[0]
    ada_cols = w_ada.shape[2]
    wa2d = w_ada[0]

    sc_loc = conv_w.shape[2]
    pack_a = jnp.zeros((8, d), F32)
    pack_a = pack_a.at[0].set(_silu(c[0]))
    pack_a = pack_a.at[1, :4 * sc_loc].set(conv_w[0].reshape(-1))
    pack_a = pack_a.at[2, :2 * sc_loc].set(lru_ba[0].reshape(-1))
    pack_a = pack_a.at[3, :2 * sc_loc].set(lru_bx[0].reshape(-1))
    pack_a = pack_a.at[4, :2 * sc_loc].set(lru_lambda[0].reshape(-1))
    all_a = _all_gather_small(pack_a, "gather_small_in")
    s16 = jnp.zeros((16, d), F32).at[0:8].set(all_a[:, 0, :]).at[8].set(_silu(c_ctx))

    def unshard(row, k):
        return all_a[:, row, :k * sc_loc].reshape(N_DEV, k, sc_loc).transpose(1, 0, 2).reshape(k, N_DEV * sc_loc)

    conv_w_full = unshard(1, 4)
    ba_full, bx_full, lam_full = unshard(2, 2), unshard(3, 2), unshard(4, 2)

    b_cols = lax.dynamic_slice(b_ada, (0, me * ada_cols), (1, ada_cols))
    mod_parts = _all_gather_small(_mod_part(s16, wa2d, b_cols), "gather_mod")
    mod_all = mod_parts.transpose(1, 0, 2).reshape(16, N_DEV * ada_cols)
    mod_me = lax.dynamic_slice(mod_all, (me, 0), (1, 6 * d)).reshape(6, d)
    mod_c = mod_all[8].reshape(6, d)
    modrows = jnp.concatenate([mod_c[0:2], mod_me], axis=0)

    lg = jax.nn.log_sigmoid(ret_decay[0])
    sp = jax.nn.softplus(-lam_full)
    wg = [jnp.concatenate([_block_diag(lru_wa[0, dd]), _block_diag(lru_wx[0, dd])], axis=1).astype(BF16) for dd in (0, 1)]
    bg = [jnp.concatenate([ba_full[dd], bx_full[dd]])[None, :] for dd in (0, 1)]
    cosf, sins = _rot_tables(l_len, t_len)

    (hn, hnt), win_g = _norm1_fwd(ctx2d, x2d, norm1_g, modrows, cb, ride=("gather", w_in[0].astype(BF16)))
    (qkv, p), w1_g = _mm_in(hn, win_g, cosf, sins, ride=("gather", w_mlp1[0].astype(BF16)))
    (o0, o1, sp0, sp1), w2_g = _ret_fwd(qkv, lg, cc, ride=("gather", w_mlp2[0].astype(BF16)))
    w2_g = w2_g.reshape(1, 4 * d, d)
    o, s_prev = [o0, o1], [sp0, sp1]
    (h0, cin0), wout_g = _lru_fwd(p, wg[0], bg[0], sp[0:1], conv_w_full, conv_b, 0, cb, ride=("gather", w_out[0].astype(BF16)))
    wout_g = wout_g.reshape(1, d, d)
    (h1, cin1), _ = _lru_fwd(p, wg[1], bg[1], sp[1:2], conv_w_full, conv_b, 1, cb)
    h, cin = [h0, h1], [cin0, cin1]
    mix = _mix_fwd(o[0], o[1], p, h[0], h[1], cb, t_len)
    y, x1, h2, h2t = _mm_nn(mix, wout_g, F32, "mm_out_norm2", tm=TL, epi=_res_norm2_epilogue(x2d, norm2_g, modrows, TL))
    r = _mm_nn(h2, w1_g, BF16, "mm_mlp1", relu_out=True, tm=TL)
    dx2, dz, dzt, facc = _mm_nn(r, w2_g, F32, "mm_mlp2_final", square_lhs=True,
                                epi=_final_epilogue(x1, tgt2d, final_g[None, :], modrows, TM))

    du = _mm_nt(dz, w2_g, BF16, "mm_da2", relu_mul=r)
    gw2 = _mm_wgrad(dzt, r, "mm_dw2", w2_g.shape[1] // N_DEV, BF16, transpose_out=True, square_rhs=True)
    gw1_lo, gw1_hi = _mm_wgrad(h2t, du, "mm_dw1", w1_g.shape[2], BF16, halves=True)
    (dx1, dy, n2acc), gw1_lo_all = _mm_nt(du, w1_g, F32, "mm_dh2_norm2", ride=("a2a", gw1_lo),
                                          epi=_bwd_norm2_epilogue(x1, dx2, y, norm2_g, modrows, TM))
    gwo = _mm_tn(mix, dy, "mm_dwout", False, d, BF16, 512).reshape(N_DEV, -1, d)
    (do, dg, dgate, dhs), gw1_hi_all = _mm_nt(dy, wout_g, F32, "mm_dmix_mix", ride=("a2a", gw1_hi),
                                              epi=_mix_bwd_epilogue(o[0], o[1], p, h[0], h[1], cb))
    gw1_all = [gw1_lo_all, gw1_hi_all]
    dxcs, dwgs, laccs, rides = [], [], [], [("a2a", gwo), None]
    (dq0, dk0, dv0, dq1, dk1, dv1, dlg_lanes), gw2_all = _ret_bwd(qkv, lg, do, s_prev, cc, ride=("a2a", gw2))
    dqs, dks, dvs = [dq0, dq1], [dk0, dk1], [dv0, dv1]
    for dd in (0, 1):
        (dxc_, dwg_, lacc_), got_ = _lru_bwd(p, wg[dd], bg[dd], sp[dd:dd + 1], conv_w_full, conv_b, h[dd], cin[dd], dhs, dd, cb,
                                             ride=rides[dd])
        dxcs.append(dxc_); dwgs.append(dwg_); laccs.append(lacc_)
        gwo_all = got_ if dd == 0 else gwo_all
    (dp, cacc), _ = _assemble_dp(dqs, dks, dvs, dg, dgate, dxcs, p, conv_w_full, cosf, sins, cb)
    pack_b = jnp.concatenate([n2acc, facc, cacc, laccs[0], laccs[1], dlg_lanes, dwgs[0], dwgs[1]], axis=0)
    gwi, all_b = _mm_wgrad(hnt, dp, "mm_dwin", win_g.shape[2], BF16, ride=("gather", pack_b))
    (grad_x, n1acc), gwi_all = _mm_nt(dp, win_g, F32, "mm_dhn_norm1", ride=("a2a", gwi),
                                      epi=_bwd_norm1_epilogue(ctx2d, x2d, dx1, norm1_g, modrows, cb))
    all_n1 = _all_gather_small(n1acc, "gather_norm1_grads")
    tot = _sum_parts(all_b, "sum_small_grads")
    t_n1 = _sum_parts(all_n1, "sum_norm1_grads")
    t_n2, t_f, t_conv, t_dlg = tot[0:8], tot[8:16], tot[16:24, :LRU_W], tot[40:48, :HEAD_DIM]
    t_l = [tot[24:32], tot[32:40]]
    t_dwd = [tot[48:48 + LRU_BD], tot[48 + LRU_BD:48 + 2 * LRU_BD]]
    loss = (0.5 / d) * jnp.sum(t_f[2])
    t_wa = jnp.stack([_blocks_from_lanes(t_dwd[dd][:, :LRU_W]) for dd in (0, 1)])
    t_wx = jnp.stack([_blocks_from_lanes(t_dwd[dd][:, LRU_W:]) for dd in (0, 1)])
    t_ba = jnp.stack([t_l[dd][0, :LRU_W] for dd in (0, 1)])
    t_bx = jnp.stack([t_l[dd][0, LRU_W:] for dd in (0, 1)])
    t_sp = jnp.stack([t_l[dd][1, :LRU_W] for dd in (0, 1)])
    dm_rows = jnp.stack([all_n1[:, 2, :], all_n1[:, 3, :], all_b[:, 3, :], all_b[:, 0, :], all_b[:, 1, :], all_b[:, 9, :]],
                        axis=1).reshape(N_DEV, 6 * d)
    dm_c = jnp.concatenate([t_n1[0], t_n1[1], jnp.zeros((4 * d,), F32)])
    dm16 = jnp.zeros((16, 6 * d), F32).at[0:8].set(dm_rows).at[8].set(dm_c)
    g_b_ada = jnp.sum(dm16, axis=0)[None, :]
    dm_cols = lax.dynamic_slice(dm16, (0, me * ada_cols), (16, ada_cols))
    g_w_ada, ds16 = _ada_bwd(s16, dm_cols, wa2d)
    ds_all = _all_gather_small(ds16[8:16], "gather_dsilu")
    dsilu_cc = _sum_parts(ds_all, "sum_dsilu")[0]
    sg_cc = jax.nn.sigmoid(c_ctx)
    g_c_ctx = dsilu_cc * (sg_cc * (1.0 + c_ctx * (1.0 - sg_cc)))

    g_ret_decay = jnp.sum(t_dlg, axis=-1).reshape(2, HEADS) * jax.nn.sigmoid(-ret_decay[0])
    g_lambda_full = -t_sp * jax.nn.sigmoid(-lam_full)

    def my_cols(full):
        return lax.dynamic_slice(full, (0, me * sc_loc), (full.shape[0], sc_loc))

    small_g = dict(
        c_ctx=g_c_ctx[None], b_ada=g_b_ada, norm1_g=t_n1[4:5], norm2_g=t_n2[2:3], ret_decay=g_ret_decay,
        conv_w=my_cols(t_conv[0:4]), conv_b=t_conv[4:5], lru_wa=t_wa.reshape(-1, LRU_BD), lru_ba=my_cols(t_ba),
        lru_wx=t_wx.reshape(-1, LRU_BD), lru_bx=my_cols(t_bx), lru_lambda=my_cols(g_lambda_full), final_g=t_f[0:1])
    small = dict(
        c_ctx=(c_ctx, m_c_ctx, v_c_ctx), b_ada=(b_ada, m_b_ada, v_b_ada), norm1_g=(norm1_g, m_norm1_g, v_norm1_g),
        norm2_g=(norm2_g, m_norm2_g, v_norm2_g), ret_decay=(ret_decay, m_ret_decay, v_ret_decay),
        conv_w=(conv_w, m_conv_w, v_conv_w), conv_b=(conv_b, m_conv_b, v_conv_b), lru_wa=(lru_wa, m_lru_wa, v_lru_wa),
        lru_ba=(lru_ba, m_lru_ba, v_lru_ba), lru_wx=(lru_wx, m_lru_wx, v_lru_wx), lru_bx=(lru_bx, m_lru_bx, v_lru_bx),
        lru_lambda=(lru_lambda, m_lru_lambda, v_lru_lambda), final_g=(final_g, m_final_g, v_final_g))
    names = list(small)
    items = [(small_g[k],) + tuple(a.reshape(small_g[k].shape) for a in small[k]) for k in names]
    res = {}
    for k, it, (d_, m_, v_) in zip(names, items, _adam_many(items, "adam_small")):
        shape = small[k][0].shape
        res[k] = tuple(a.reshape(shape) for a in (it[0], d_, m_, v_))

    def big(parts, w, m, v, name):
        out = _sum_adam(parts, w[0], m[0], v[0], name)
        return tuple(a[None] for a in out)

    res["w_ada"] = big([g_w_ada[None]], w_ada, m_w_ada, v_w_ada, "adam_w_ada")
    res["w_in"] = big([gwi_all], w_in, m_w_in, v_w_in, "adam_w_in")
    res["w_out"] = big([gwo_all], w_out, m_w_out, v_w_out, "adam_w_out")
    res["w_mlp1"] = big(gw1_all, w_mlp1, m_w_mlp1, v_w_mlp1, "adam_w_mlp1")
    res["w_mlp2"] = big([gw2_all], w_mlp2, m_w_mlp2, v_w_mlp2, "adam_w_mlp2")

    order = ["c_ctx", "w_ada", "b_ada", "norm1_g", "norm2_g", "w_in", "ret_decay", "conv_w", "conv_b", "lru_wa", "lru_ba",
             "lru_wx", "lru_bx", "lru_lambda", "w_out", "w_mlp1", "w_mlp2", "final_g"]
    outs = [loss, grad_x[None]]
    for j in range(4):
        outs += [res[k][j] for k in order]
    return tuple(outs)
```

```python
import jax
import jax.numpy as jnp
from jax import lax
from jax.experimental import pallas as pl
from jax.experimental.pallas import tpu as pltpu

F32 = jnp.float32
BF16 = jnp.bfloat16
AXES = ("x", "y", "c")
N_DEV = 8
MESH = pl.DeviceIdType.MESH

HEADS = 4
HEAD_DIM = 128
CHUNK = 128
RET_W = HEADS * HEAD_DIM
LRU_W = 512
LRU_BLOCKS = 8
LRU_BD = LRU_W // LRU_BLOCKS
LRU_C = 8.0
EPS = 1e-6
K_SCALE = HEAD_DIM ** -0.5
ROPE_BASE = 10000.0
GRID_W = 64
TM = 256
TL = 512
SUB = 8

ADAM_LR = 0.001
ADAM_B1 = 0.9
ADAM_B2 = 0.999
ADAM_EPS = 1e-08
ADAM_WD = 0.01
ADAM_STEP = 10

COL_G, COL_XR, COL_GATE = 0, 1, 2

R_CSH1, R_CSC1, R_SH1, R_SC1, R_G1, R_SH2, R_SC2, R_G2 = range(8)


def _pcall(body, **kw):
    return pl.pallas_call(body, **kw)


def _cp(vmem_mb=48):
    return pltpu.CompilerParams(vmem_limit_bytes=vmem_mb << 20)


def _sds(shape, dtype):
    return jax.ShapeDtypeStruct(shape, dtype)


def _dot(a, b):
    return jnp.dot(a.astype(BF16), b.astype(BF16), preferred_element_type=F32)


def _dot_nt(a, b):
    return lax.dot_general(a.astype(BF16), b.astype(BF16), (((1,), (1,)), ((), ())), preferred_element_type=F32)


def _dot_tn(a, b):
    return lax.dot_general(a.astype(BF16), b.astype(BF16), (((0,), (0,)), ((), ())), preferred_element_type=F32)


def _sigmoid(x):
    return 0.5 * jnp.tanh(0.5 * x) + 0.5


def _gelu(x):
    return 0.5 * x * (1.0 + jnp.tanh(0.7978845608028654 * (x + 0.044715 * x * x * x)))


def _dgelu(x):
    t = jnp.tanh(0.7978845608028654 * (x + 0.044715 * x * x * x))
    return 0.5 * (1.0 + t) + 0.5 * x * (1.0 - t * t) * 0.7978845608028654 * (1.0 + 3.0 * 0.044715 * x * x)


def _rows_iota(shape):
    return lax.broadcasted_iota(jnp.int32, shape, 0)


def _tile_order(dirn, s, cb, nb):
    if dirn == 0:
        return s
    return jnp.where(s < cb, cb - 1 - s, nb - 1 - (s - cb))


_SEMS = [pltpu.SemaphoreType.DMA((7,)), pltpu.SemaphoreType.DMA((7,)), pltpu.SemaphoreType.DMA(())]
_ANY = pl.BlockSpec(memory_space=pl.ANY)


def _gather_copies(x_ref, out_ref, send_sems, recv_sems, local_sem):
    mx, my, mc = lax.axis_index("x"), lax.axis_index("y"), lax.axis_index("c")
    me, sibling = (mx, my, mc), (mx, my, 1 - mc)
    chips = [(1 - mx, my), (mx, 1 - my), (1 - mx, 1 - my)]

    def slot(px, py, pc):
        return out_ref.at[4 * px + 2 * py + pc]

    def copy(k, block, to, src=None):
        return pltpu.make_async_remote_copy(
            src_ref=slot(*block) if src is None else src, dst_ref=slot(*block),
            send_sem=send_sems.at[k], recv_sem=recv_sems.at[k], device_id=to, device_id_type=MESH)

    mine = pltpu.make_async_copy(x_ref, slot(*me), local_sem)
    first = [copy(0, me, sibling, src=x_ref)] + [copy(1 + j, me, (*chip, mc), src=x_ref) for j, chip in enumerate(chips)]
    passed = [copy(4 + j, (*chip, mc), sibling) for j, chip in enumerate(chips)]
    recv_ici = [copy(1 + j, (*chip, mc), me) for j, chip in enumerate(chips)]
    recv_d2d = [copy(0, sibling, me)] + [copy(4 + j, (*chip, 1 - mc), me) for j, chip in enumerate(chips)]
    return mine, first, passed, recv_ici, recv_d2d


def _gather_start(*refs):
    mine, first, _, _, _ = _gather_copies(*refs)
    mine.start()
    for cp in first:
        cp.start()


def _gather_finish(*refs):
    mine, first, passed, recv_ici, recv_d2d = _gather_copies(*refs)
    for landed, onward in zip(recv_ici, passed):
        landed.wait_recv()
        onward.start()
    for landed in recv_d2d:
        landed.wait_recv()
    for cp in first + passed:
        cp.wait_send()
    mine.wait()


def _a2a_copies(g_ref, out_ref, send_sems, recv_sems, local_sem):
    mx, my, mc = lax.axis_index("x"), lax.axis_index("y"), lax.axis_index("c")
    me = 4 * mx + 2 * my + mc
    mine = pltpu.make_async_copy(g_ref.at[me], out_ref.at[me], local_sem)
    copies = []
    for k in range(1, N_DEV):
        px = 1 - mx if (k >> 2) & 1 else mx
        py = 1 - my if (k >> 1) & 1 else my
        pc = 1 - mc if k & 1 else mc
        copies.append(pltpu.make_async_remote_copy(
            src_ref=g_ref.at[4 * px + 2 * py + pc], dst_ref=out_ref.at[me],
            send_sem=send_sems.at[k - 1], recv_sem=recv_sems.at[k - 1],
            device_id=(px, py, pc), device_id_type=MESH))
    return mine, copies


def _a2a_start(*refs):
    mine, copies = _a2a_copies(*refs)
    mine.start()
    for cp in copies:
        cp.start()


def _a2a_finish(*refs):
    mine, copies = _a2a_copies(*refs)
    for cp in copies:
        cp.wait()
    mine.wait()


_EXCHANGES = {"gather": (_gather_start, _gather_finish), "a2a": (_a2a_start, _a2a_finish)}


def _exchange_shape(kind, src):
    return _sds((N_DEV,) + src.shape if kind == "gather" else src.shape, src.dtype)


def _all_gather_small(x, name):
    def body(x_ref, out_ref, send_sems, recv_sems, local_sem):
        mx, my, mc = lax.axis_index("x"), lax.axis_index("y"), lax.axis_index("c")
        me = 4 * mx + 2 * my + mc
        mine = pltpu.make_async_copy(x_ref, out_ref.at[me], local_sem)
        mine.start()
        copies = []
        for k in range(1, N_DEV):
            peer = (1 - mx if (k >> 2) & 1 else mx, 1 - my if (k >> 1) & 1 else my, 1 - mc if k & 1 else mc)
            copies.append(pltpu.make_async_remote_copy(
                src_ref=x_ref, dst_ref=out_ref.at[me], send_sem=send_sems.at[k - 1], recv_sem=recv_sems.at[k - 1],
                device_id=peer, device_id_type=MESH))
            copies[-1].start()
        for cp in copies:
            cp.wait()
        mine.wait()

    return _pcall(body, name=name, out_shape=_exchange_shape("gather", x), in_specs=[_ANY], out_specs=_ANY,
                  scratch_shapes=list(_SEMS))(x)


def _pcall_ride(body, ride, args, *, name, grid, in_specs, out_specs, out_shape, scratch_shapes=(), compiler_params=None):
    if ride is None:
        out = _pcall(body, name=name, grid=grid, in_specs=in_specs, out_specs=out_specs, out_shape=out_shape,
                     scratch_shapes=list(scratch_shapes), compiler_params=compiler_params)(*args)
        return out, None
    kind, src = ride
    start, finish = _EXCHANGES[kind]
    single = not isinstance(out_shape, (tuple, list))
    out_specs_t = (out_specs,) if single else tuple(out_specs)
    out_shape_t = (out_shape,) if single else tuple(out_shape)
    n_in, n_out, n_sc = len(in_specs), len(out_shape_t), len(scratch_shapes)

    def wrapped(*refs):
        ins, src_ref = refs[:n_in], refs[n_in]
        outs, dst_ref = refs[n_in + 1:n_in + 1 + n_out], refs[n_in + 1 + n_out]
        scratch = refs[n_in + 2 + n_out:n_in + 2 + n_out + n_sc]
        sems = refs[n_in + 2 + n_out + n_sc:]
        first = pl.program_id(0) == 0
        last = pl.program_id(0) == grid[0] - 1
        for ax in range(1, len(grid)):
            first = jnp.logical_and(first, pl.program_id(ax) == 0)
            last = jnp.logical_and(last, pl.program_id(ax) == grid[ax] - 1)

        @pl.when(first)
        def _():
            start(src_ref, dst_ref, *sems)

        body(*ins, *outs, *scratch)

        @pl.when(last)
        def _():
            finish(src_ref, dst_ref, *sems)

    res = _pcall(wrapped, name=name, grid=grid, in_specs=list(in_specs) + [_ANY], out_specs=out_specs_t + (_ANY,),
                 out_shape=out_shape_t + (_exchange_shape(kind, src),),
                 scratch_shapes=list(scratch_shapes) + list(_SEMS), compiler_params=compiler_params)(*args, src)
    return (res[0] if single else tuple(res[:-1])), res[-1]


SUB_ROWS = TM


class _Epilogue:
    def __init__(self, fn, args, in_specs, out_specs, out_shape, steps=None, lhs_map=None):
        self.fn, self.args, self.in_specs, self.out_specs, self.out_shape = fn, tuple(args), list(in_specs), out_specs, out_shape
        self.steps, self.lhs_map = steps, lhs_map


def _mm_nn(a, w, out_dtype, name, square_lhs=False, relu_out=False, ride=None, tm=TM, epi=None, vmem_mb=48):
    m, k = a.shape
    nb, _, bn = w.shape
    tm = min(tm, m)
    n_extra = 0 if epi is None else len(epi.args)

    def body(*refs):
        a_ref, w_ref = refs[:2]
        av = a_ref[...]
        if square_lhs:
            av = av * av
        if epi is not None:
            assert nb == 1
            for r0 in range(0, tm, SUB_ROWS):
                rows = slice(r0, r0 + SUB_ROWS)
                epi.fn(jnp.dot(av[rows], w_ref[0], preferred_element_type=F32), rows, *refs[2:])
            return
        for j in range(nb):
            r = jnp.dot(av, w_ref[j], preferred_element_type=F32)
            if relu_out:
                r = jnp.maximum(r, 0.0)
            refs[2][:, j * bn:(j + 1) * bn] = r.astype(out_dtype)

    in_specs = [pl.BlockSpec((tm, k), lambda i: (i, 0)), pl.BlockSpec((nb, k, bn), lambda i: (0, 0, 0))]
    if epi is None:
        args, out_specs, out_shape = (a, w), pl.BlockSpec((tm, nb * bn), lambda i: (i, 0)), _sds((m, nb * bn), out_dtype)
    else:
        args, out_specs, out_shape = (a, w) + epi.args, epi.out_specs, epi.out_shape
        in_specs += epi.in_specs
    out, ex = _pcall_ride(body, ride, args, name=name, grid=(m // tm,), in_specs=in_specs, out_specs=out_specs,
                          out_shape=out_shape, compiler_params=_cp(vmem_mb))
    return out if ride is None else (out, ex)


def _mm_nt(dy, w, out_dtype, name, relu_mul=None, ride=None, tm=TM, epi=None, vmem_mb=48):
    m = dy.shape[0]
    nb, k, bn = w.shape
    tm = min(tm, m)
    n_extra = (0 if relu_mul is None else 1) + (0 if epi is None else len(epi.args))

    def body(*refs):
        dy_ref, w_ref = refs[:2]
        extra, outs, wt = refs[2:2 + n_extra], refs[2 + n_extra:-1], refs[-1]

        @pl.when(pl.program_id(0) == 0)
        def _():
            for j in range(nb):
                wt[j * bn:(j + 1) * bn, :] = w_ref[j].T

        if epi is not None:
            for r0 in range(0, tm, SUB_ROWS):
                rows = slice(r0, r0 + SUB_ROWS)
                epi.fn(jnp.dot(dy_ref[rows, :], wt[...], preferred_element_type=F32), rows, *extra, *outs)
            return
        acc = jnp.dot(dy_ref[...], wt[...], preferred_element_type=F32)
        if relu_mul is not None:
            acc = acc * (2.0 * extra[0][...].astype(F32))
        outs[0][...] = acc.astype(out_dtype)

    lhs_map = (lambda i: (i, 0)) if epi is None or epi.lhs_map is None else epi.lhs_map
    in_specs = [pl.BlockSpec((tm, nb * bn), lhs_map), pl.BlockSpec((nb, k, bn), lambda i: (0, 0, 0))]
    args = [dy, w]
    if relu_mul is not None:
        in_specs.append(pl.BlockSpec((tm, k), lambda i: (i, 0)))
        args.append(relu_mul)
    steps = m // tm
    if epi is None:
        out_specs, out_shape = pl.BlockSpec((tm, k), lambda i: (i, 0)), _sds((m, k), out_dtype)
    else:
        args += list(epi.args)
        in_specs += epi.in_specs
        out_specs, out_shape = epi.out_specs, epi.out_shape
        steps = steps if epi.steps is None else epi.steps
    out, ex = _pcall_ride(
        body, ride, args, name=name, grid=(steps,), in_specs=in_specs, out_specs=out_specs, out_shape=out_shape,
        scratch_shapes=[pltpu.VMEM((nb * bn, k), BF16)], compiler_params=_cp(vmem_mb))
    return out if ride is None else (out, ex)


def _mm_tn(a, b, name, col_blocks, block, out_dtype, tm, square_lhs=False):
    m, k = a.shape
    nn = b.shape[1]
    steps = m // tm
    if col_blocks:
        nblk, acc_shape = nn // block, (k, block)
        a_spec = pl.BlockSpec((tm, k), lambda j, s: (s, 0))
        b_spec = pl.BlockSpec((tm, block), lambda j, s: (s, j))
    else:
        nblk, acc_shape = k // block, (block, nn)
        a_spec = pl.BlockSpec((tm, block), lambda j, s: (s, j))
        b_spec = pl.BlockSpec((tm, nn), lambda j, s: (s, 0))

    def body(a_ref, b_ref, o_ref, acc):
        s = pl.program_id(1)

        @pl.when(s == 0)
        def _():
            acc[...] = jnp.zeros_like(acc)

        av = a_ref[...]
        if square_lhs:
            av = av.astype(F32)
            av = (av * av).astype(BF16)
        acc[...] += _dot_tn(av, b_ref[...])

        @pl.when(s == steps - 1)
        def _():
            o_ref[...] = acc[...].astype(out_dtype)

    return _pcall(
        body, name=name, grid=(nblk, steps), in_specs=[a_spec, b_spec],
        out_specs=pl.BlockSpec((None,) + acc_shape, lambda j, s: (j, 0, 0)),
        out_shape=_sds((nblk,) + acc_shape, out_dtype),
        scratch_shapes=[pltpu.VMEM(acc_shape, F32)], compiler_params=_cp(),
    )(a, b)


def _mm_wgrad(at, b, name, bn, out_dtype, transpose_out=False, square_rhs=False, halves=False, ride=None):
    k, m = at.shape
    nblk = b.shape[1] // bn
    rows, cols = (bn, k) if transpose_out else (k, bn)
    nout = 2 if halves else 1
    per = rows // nout

    def body(a_ref, b_ref, *o_refs):
        bv = b_ref[...]
        if square_rhs:
            bv = bv * bv
        r = jnp.dot(a_ref[...], bv, preferred_element_type=F32)
        r = (r.T if transpose_out else r).astype(out_dtype)
        for i, o_ref in enumerate(o_refs):
            o_ref[...] = r[i * per:(i + 1) * per, :]

    out, ex = _pcall_ride(
        body, ride, (at, b), name=name, grid=(nblk,),
        in_specs=[pl.BlockSpec((k, m), lambda j: (0, 0)), pl.BlockSpec((m, bn), lambda j: (0, j))],
        out_specs=tuple(pl.BlockSpec((None, per, cols), lambda j: (j, 0, 0)) for _ in range(nout)),
        out_shape=tuple(_sds((nblk, per, cols), out_dtype) for _ in range(nout)), compiler_params=_cp())
    out = out if halves else out[0]
    return out if ride is None else (out, ex)


def _mm_in(hn, w, cosf, sins, ride):
    m, k = hn.shape
    nb, _, bn = w.shape

    def body(a_ref, w_ref, c_ref, s_ref, qkv_ref, rest_ref, pt):
        av = a_ref[...]
        for j in range(nb):
            pt[:, j * bn:(j + 1) * bn] = jnp.dot(av, w_ref[j], preferred_element_type=F32)
        cf, ss = c_ref[...], s_ref[...]
        for h in range(HEADS):
            sq = slice(h * HEAD_DIM, (h + 1) * HEAD_DIM)
            sk = slice(RET_W + h * HEAD_DIM, RET_W + (h + 1) * HEAD_DIM)
            qkv_ref[:, sq] = _rot(pt[:, sq], cf, ss).astype(BF16)
            qkv_ref[:, sk] = (_rot(pt[:, sk], cf, ss) * K_SCALE).astype(BF16)
        qkv_ref[:, 2 * RET_W:] = pt[:, 2 * RET_W:3 * RET_W].astype(BF16)
        rest_ref[...] = pt[:, 3 * RET_W:]

    tab = pl.BlockSpec((TM, HEAD_DIM), lambda i: (i, 0))
    wide = pl.BlockSpec((TM, 3 * RET_W), lambda i: (i, 0))
    return _pcall_ride(
        body, ride, (hn, w, cosf, sins), name="mm_in", grid=(m // TM,),
        in_specs=[pl.BlockSpec((TM, k), lambda i: (i, 0)), pl.BlockSpec((nb, k, bn), lambda i: (0, 0, 0)), tab, tab],
        out_specs=(wide, wide), out_shape=(_sds((m, 3 * RET_W), BF16), _sds((m, nb * bn - 3 * RET_W), F32)),
        scratch_shapes=[pltpu.VMEM((TM, nb * bn), F32)], compiler_params=_cp())


def _mod_part(s16, w_ada, b_cols):
    def body(s_ref, w_ref, b_ref, o_ref):
        o_ref[...] = _dot(s_ref[...], w_ref[...]) + b_ref[...]

    return _pcall(body, name="mod_part", out_shape=_sds((s16.shape[0], w_ada.shape[1]), F32),
                  compiler_params=_cp())(s16, w_ada, b_cols)


def _ada_bwd(s16, dm_cols, w_ada):
    def body(s_ref, d_ref, w_ref, gw_ref, ds_ref):
        gw_ref[...] = _dot_tn(s_ref[...], d_ref[...])
        ds_ref[...] = _dot_nt(d_ref[...], w_ref[...])

    return _pcall(body, name="ada_bwd",
                  out_shape=(_sds(w_ada.shape, F32), _sds(s16.shape, F32)), compiler_params=_cp())(s16, dm_cols, w_ada)


def _norm1_fwd(ctx, x, g, modrows, cb, ride=None):
    l_len, d = ctx.shape
    nb = (l_len + x.shape[0]) // TM

    def body(ctx_ref, x_ref, g_ref, m_ref, o_ref, ot_ref):
        is_ctx = pl.program_id(0) < cb
        xin = jnp.where(is_ctx, ctx_ref[...], x_ref[...])
        sh = jnp.where(is_ctx, m_ref[R_CSH1:R_CSH1 + 1, :], m_ref[R_SH1:R_SH1 + 1, :])
        sc = jnp.where(is_ctx, m_ref[R_CSC1:R_CSC1 + 1, :], m_ref[R_SC1:R_SC1 + 1, :])
        ms = jnp.mean(xin * xin, axis=-1, keepdims=True)
        n = xin * lax.rsqrt(ms + EPS) * g_ref[...]
        hn = n * (1.0 + sc) + sh
        o_ref[...] = hn.astype(BF16)
        ot_ref[...] = hn.T.astype(BF16)

    return _pcall_ride(
        body, ride, (ctx, x, g, modrows), name="norm1_fwd", grid=(nb,),
        in_specs=[pl.BlockSpec((TM, d), lambda i: (jnp.minimum(i, cb - 1), 0)),
                  pl.BlockSpec((TM, d), lambda i: (jnp.maximum(i - cb, 0), 0)),
                  pl.BlockSpec((1, d), lambda i: (0, 0)), pl.BlockSpec((8, d), lambda i: (0, 0))],
        out_specs=(pl.BlockSpec((TM, d), lambda i: (i, 0)), pl.BlockSpec((d, TM), lambda i: (0, i))),
        out_shape=(_sds((nb * TM, d), BF16), _sds((d, nb * TM), BF16)), compiler_params=_cp())


def _mix_fwd(o_f, o_b, p, h_f, h_b, cb, t_len):
    def body(of_ref, ob_ref, g_ref, gate_ref, hf_ref, hb_ref, mix_ref):
        o = of_ref[...] + ob_ref[...]
        g = g_ref[...]
        sg = g * _sigmoid(g)
        for hh in range(HEADS):
            sl = slice(hh * HEAD_DIM, (hh + 1) * HEAD_DIM)
            oh = o[:, sl]
            yc = oh - jnp.mean(oh, axis=-1, keepdims=True)
            var = jnp.mean(yc * yc, axis=-1, keepdims=True)
            mix_ref[:, sl] = (sg[:, sl] * (yc * lax.rsqrt(var + EPS))).astype(BF16)
        mix_ref[:, RET_W:] = ((hf_ref[...] + hb_ref[...]) * _gelu(gate_ref[...])).astype(BF16)

    row = lambda i: (i + cb, 0)
    return _pcall(
        body, name="mix_fwd", grid=(t_len // TM,),
        in_specs=[pl.BlockSpec((TM, RET_W), row), pl.BlockSpec((TM, RET_W), row),
                  pl.BlockSpec((TM, RET_W), lambda i: (i + cb, COL_G)), pl.BlockSpec((TM, LRU_W), lambda i: (i + cb, COL_GATE)),
                  pl.BlockSpec((TM, LRU_W), row), pl.BlockSpec((TM, LRU_W), row)],
        out_specs=pl.BlockSpec((TM, RET_W + LRU_W), lambda i: (i, 0)),
        out_shape=_sds((t_len, RET_W + LRU_W), BF16), compiler_params=_cp(),
    )(o_f, o_b, p, p, h_f, h_b)


def _res_norm2_epilogue(x, g, modrows, tm):
    t_len, d = x.shape
    tm = min(tm, t_len)

    def fn(y, rows, x_ref, g_ref, m_ref, y_ref, x1_ref, h2_ref, h2t_ref):
        y_ref[rows, :] = y
        x1 = x_ref[rows, :] + m_ref[R_G1:R_G1 + 1, :] * y
        ms = jnp.mean(x1 * x1, axis=-1, keepdims=True)
        n = x1 * lax.rsqrt(ms + EPS) * g_ref[...]
        x1_ref[rows, :] = x1
        h2 = n * (1.0 + m_ref[R_SC2:R_SC2 + 1, :]) + m_ref[R_SH2:R_SH2 + 1, :]
        h2_ref[rows, :] = h2.astype(BF16)
        h2t_ref[:, rows] = h2.T.astype(BF16)

    t = pl.BlockSpec((tm, d), lambda i: (i, 0))
    return _Epilogue(
        fn, (x, g, modrows),
        in_specs=[t, pl.BlockSpec((1, d), lambda i: (0, 0)), pl.BlockSpec((8, d), lambda i: (0, 0))],
        out_specs=(t, t, t, pl.BlockSpec((d, tm), lambda i: (0, i))),
        out_shape=(_sds((t_len, d), F32), _sds((t_len, d), F32), _sds((t_len, d), BF16), _sds((d, t_len), BF16)))


def _zero_at_start(acc_ref, rows):
    if rows.start == 0:
        @pl.when(pl.program_id(0) == 0)
        def _():
            acc_ref[...] = jnp.zeros_like(acc_ref)


def _final_epilogue(x1, target, fg, modrows, tm):
    t_len, d = x1.shape

    def fn(z, rows, x1_ref, t_ref, fg_ref, m_ref, dx2_ref, dz_ref, dzt_ref, acc_ref):
        _zero_at_start(acc_ref, rows)
        g2 = m_ref[R_G2:R_G2 + 1, :]
        x2 = x1_ref[rows, :] + g2 * z
        rstd = lax.rsqrt(jnp.mean(x2 * x2, axis=-1, keepdims=True) + EPS)
        xh = x2 * rstd
        fg = fg_ref[...]
        e = xh * fg - t_ref[rows, :]
        dy = e * (1.0 / d)
        dxh = dy * fg
        dx2 = rstd * (dxh - xh * jnp.mean(dxh * xh, axis=-1, keepdims=True))
        dx2_ref[rows, :] = dx2
        dz = g2 * dx2
        dz_ref[rows, :] = dz.astype(BF16)
        dzt_ref[:, rows] = dz.T.astype(BF16)
        acc_ref[0:1, :] += jnp.sum(dy * xh, axis=0, keepdims=True)
        acc_ref[1:2, :] += jnp.sum(dx2 * z, axis=0, keepdims=True)
        acc_ref[2:3, :] += jnp.sum(e * e, axis=0, keepdims=True)

    t = pl.BlockSpec((tm, d), lambda i: (i, 0))
    return _Epilogue(
        fn, (x1, target, fg, modrows),
        in_specs=[t, t, pl.BlockSpec((1, d), lambda i: (0, 0)), pl.BlockSpec((8, d), lambda i: (0, 0))],
        out_specs=(t, t, pl.BlockSpec((d, tm), lambda i: (0, i)), pl.BlockSpec((8, d), lambda i: (0, 0))),
        out_shape=(_sds((t_len, d), F32), _sds((t_len, d), BF16), _sds((d, t_len), BF16), _sds((8, d), F32)))


def _bwd_norm2_epilogue(x1, dx2, y, g, modrows, tm):
    t_len, d = x1.shape

    def fn(dh2, rows, x1_ref, dx2_ref, y_ref, g_ref, m_ref, dx1_ref, dy_ref, acc_ref):
        _zero_at_start(acc_ref, rows)
        x1 = x1_ref[rows, :]
        rstd = lax.rsqrt(jnp.mean(x1 * x1, axis=-1, keepdims=True) + EPS)
        xh = x1 * rstd
        gn = g_ref[...]
        dn = dh2 * (1.0 + m_ref[R_SC2:R_SC2 + 1, :])
        dxh = dn * gn
        dx1 = dx2_ref[rows, :] + rstd * (dxh - xh * jnp.mean(dxh * xh, axis=-1, keepdims=True))
        dx1_ref[rows, :] = dx1
        dy_ref[rows, :] = (m_ref[R_G1:R_G1 + 1, :] * dx1).astype(BF16)
        acc_ref[0:1, :] += jnp.sum(dh2, axis=0, keepdims=True)
        acc_ref[1:2, :] += jnp.sum(dh2 * xh * gn, axis=0, keepdims=True)
        acc_ref[2:3, :] += jnp.sum(dn * xh, axis=0, keepdims=True)
        acc_ref[3:4, :] += jnp.sum(dx1 * y_ref[rows, :], axis=0, keepdims=True)

    t = pl.BlockSpec((tm, d), lambda i: (i, 0))
    return _Epilogue(
        fn, (x1, dx2, y, g, modrows),
        in_specs=[t, t, t, pl.BlockSpec((1, d), lambda i: (0, 0)), pl.BlockSpec((8, d), lambda i: (0, 0))],
        out_specs=(t, t, pl.BlockSpec((8, d), lambda i: (0, 0))),
        out_shape=(_sds((t_len, d), F32), _sds((t_len, d), BF16), _sds((8, d), F32)))


def _mix_bwd_epilogue(o_f, o_b, p, h_f, h_b, cb):
    n = o_f.shape[0]

    def fn(dm, rows, of_ref, ob_ref, g_ref, gate_ref, hf_ref, hb_ref, do_ref, dg_ref, dgate_ref, dhs_ref):
        is_ctx = pl.program_id(0) < cb

        @pl.when(is_ctx)
        def _():
            for r in (do_ref, dg_ref, dgate_ref, dhs_ref):
                r[rows, :] = jnp.zeros((rows.stop - rows.start, r.shape[1]), r.dtype)

        @pl.when(jnp.logical_not(is_ctx))
        def _():
            o = of_ref[rows, :] + ob_ref[rows, :]
            g = g_ref[rows, :]
            s = _sigmoid(g)
            sg = g * s
            dsg = s * (1.0 + g * (1.0 - s))
            for hh in range(HEADS):
                sl = slice(hh * HEAD_DIM, (hh + 1) * HEAD_DIM)
                oh = o[:, sl]
                yc = oh - jnp.mean(oh, axis=-1, keepdims=True)
                rs = lax.rsqrt(jnp.mean(yc * yc, axis=-1, keepdims=True) + EPS)
                gn = yc * rs
                dret = dm[:, sl]
                dgn = dret * sg[:, sl]
                dg_ref[rows, sl] = (dret * gn * dsg[:, sl]).astype(BF16)
                do_ref[rows, sl] = (rs * (dgn - jnp.mean(dgn, axis=-1, keepdims=True)
                                          - gn * jnp.mean(dgn * gn, axis=-1, keepdims=True))).astype(BF16)
            dlru = dm[:, RET_W:]
            gate = gate_ref[rows, :]
            dhs_ref[rows, :] = dlru * _gelu(gate)
            dgate_ref[rows, :] = (dlru * (hf_ref[rows, :] + hb_ref[rows, :]) * _dgelu(gate)).astype(BF16)

    t = pl.BlockSpec((TM, RET_W), lambda i: (i, 0))
    return _Epilogue(
        fn, (o_f, o_b, p, p, h_f, h_b),
        in_specs=[t, t, pl.BlockSpec((TM, RET_W), lambda i: (i, COL_G)), pl.BlockSpec((TM, LRU_W), lambda i: (i, COL_GATE)), t, t],
        out_specs=(t, t, t, t), out_shape=(_sds((n, RET_W), BF16),) * 3 + (_sds((n, RET_W), F32),),
        steps=n // TM, lhs_map=lambda i: (jnp.maximum(i - cb, 0), 0))


def _bwd_norm1_epilogue(ctx, x, dx1, g, modrows, cb):
    t_len, d = x.shape

    def fn(dh, rows, ctx_ref, x_ref, dx1_ref, g_ref, m_ref, gx_ref, acc_ref):
        is_ctx = pl.program_id(0) < cb
        _zero_at_start(acc_ref, rows)
        xin = jnp.where(is_ctx, ctx_ref[rows, :], x_ref[rows, :])
        sc = jnp.where(is_ctx, m_ref[R_CSC1:R_CSC1 + 1, :], m_ref[R_SC1:R_SC1 + 1, :])
        rstd = lax.rsqrt(jnp.mean(xin * xin, axis=-1, keepdims=True) + EPS)
        xh = xin * rstd
        gn = g_ref[...]
        dn = dh * (1.0 + sc)
        dxh = dn * gn
        gx_ref[rows, :] = dx1_ref[rows, :] + rstd * (dxh - xh * jnp.mean(dxh * xh, axis=-1, keepdims=True))
        s0 = jnp.sum(dh, axis=0, keepdims=True)
        s1 = jnp.sum(dh * xh * gn, axis=0, keepdims=True)
        acc_ref[4:5, :] += jnp.sum(dn * xh, axis=0, keepdims=True)

        @pl.when(is_ctx)
        def _():
            acc_ref[0:1, :] += s0
            acc_ref[1:2, :] += s1

        @pl.when(jnp.logical_not(is_ctx))
        def _():
            acc_ref[2:3, :] += s0
            acc_ref[3:4, :] += s1

    lat = pl.BlockSpec((TM, d), lambda i: (jnp.maximum(i - cb, 0), 0))
    return _Epilogue(
        fn, (ctx, x, dx1, g, modrows),
        in_specs=[pl.BlockSpec((TM, d), lambda i: (jnp.minimum(i, cb - 1), 0)),
                  lat, lat, pl.BlockSpec((1, d), lambda i: (0, 0)), pl.BlockSpec((8, d), lambda i: (0, 0))],
        out_specs=(lat, pl.BlockSpec((8, d), lambda i: (0, 0))),
        out_shape=(_sds((t_len, d), F32), _sds((8, d), F32)))


def _rot(x, cf, ss):
    return x * cf + pltpu.roll(x, HEAD_DIM // 2, 1) * ss


def _decay_exponents(dirn):
    ii = lax.broadcasted_iota(jnp.int32, (CHUNK, CHUNK), 0)
    jj = lax.broadcasted_iota(jnp.int32, (CHUNK, CHUNK), 1)
    rel = ii - jj if dirn == 0 else jj - ii
    pos = ii.astype(F32)
    if dirn == 0:
        cq, cs = pos + 1.0, (CHUNK - 1.0) - pos
    else:
        cq, cs = CHUNK - pos, pos
    return rel, jnp.maximum(rel, 0).astype(F32), cq, cs


def _store_decay(lg_ref, dec, relf_ref=None):
    for dirn in (0, 1):
        rel, relf, cq, cs = _decay_exponents(dirn)
        if relf_ref is not None:
            relf_ref[dirn] = relf
        for h in range(HEADS):
            lgv = lg_ref[dirn, h]
            wq, ws = jnp.exp(lgv * cq), jnp.exp(lgv * cs)
            dec[dirn, h, 0] = jnp.where(rel >= 0, jnp.exp(lgv * relf), 0.0)
            dec[dirn, h, 1] = wq
            dec[dirn, h, 2] = ws
            if relf_ref is not None:
                dec[dirn, h, 3] = wq * cq
                dec[dirn, h, 4] = ws * cs


def _ret_rows(cc, nc, step_of):
    return [lambda s, dirn=dirn: _tile_order(dirn, step_of(s), cc, nc) for dirn in (0, 1)]


def _ret_in_specs(rows):
    specs = []
    for row in rows:
        specs += [pl.BlockSpec((CHUNK, RET_W), lambda s, o=o, row=row: (row(s), o)) for o in (0, 1, 2)]
    return specs


def _ret_fwd(qkv, lg, cc, ride=None):
    n = qkv.shape[0]
    nc = n // CHUNK
    rows = _ret_rows(cc, nc, lambda s: s)

    def body(lg_ref, q0, k0, v0, q1, k1, v1, o0, o1, sp0, sp1, st, dec):
        @pl.when(pl.program_id(0) == 0)
        def _():
            st[...] = jnp.zeros_like(st)
            _store_decay(lg_ref, dec)

        for dirn, (q_ref, k_ref, v_ref, o_ref, sp_ref) in enumerate(((q0, k0, v0, o0, sp0), (q1, k1, v1, o1, sp1))):
            for h in range(HEADS):
                sl = slice(h * HEAD_DIM, (h + 1) * HEAD_DIM)
                q, k, v = q_ref[:, sl], k_ref[:, sl], v_ref[:, sl]
                sp = st[dirn, h]
                sp_ref[h] = sp
                o_ref[:, sl] = _dot(_dot_nt(q, k) * dec[dirn, h, 0], v) + _dot(q * dec[dirn, h, 1], sp)
                st[dirn, h] = jnp.exp(lg_ref[dirn, h] * CHUNK) * sp + _dot_tn(k * dec[dirn, h, 2], v)

    o_specs = [pl.BlockSpec((CHUNK, RET_W), lambda s, row=row: (row(s), 0)) for row in rows]
    state = pl.BlockSpec((None, HEADS, CHUNK, HEAD_DIM), lambda s: (s, 0, 0, 0))
    return _pcall_ride(
        body, ride, (lg,) + (qkv,) * 6, name="ret_fwd", grid=(nc,),
        in_specs=[pl.BlockSpec(memory_space=pltpu.SMEM)] + _ret_in_specs(rows),
        out_specs=(o_specs[0], o_specs[1], state, state),
        out_shape=(_sds((n, RET_W), F32),) * 2 + (_sds((nc, HEADS, CHUNK, HEAD_DIM), F32),) * 2,
        scratch_shapes=[pltpu.VMEM((2, HEADS, CHUNK, HEAD_DIM), F32), pltpu.VMEM((2, HEADS, 3, CHUNK, CHUNK), F32)],
        compiler_params=_cp())


def _ret_bwd(qkv, lg, do, s_prev, cc, ride=None):
    n = qkv.shape[0]
    nc = n // CHUNK
    rows = _ret_rows(cc, nc, lambda s: nc - 1 - s)

    def body(lg_ref, q0, k0, v0, q1, k1, v1, do0, do1, sp0, sp1, dq0, dk0, dv0, dq1, dk1, dv1, dlg_ref, dst, dec, relf):
        @pl.when(pl.program_id(0) == 0)
        def _():
            dst[...] = jnp.zeros_like(dst)
            dlg_ref[...] = jnp.zeros_like(dlg_ref)
            _store_decay(lg_ref, dec, relf)

        for dirn, (q_ref, k_ref, v_ref, do_ref, sp_ref, dq_ref, dk_ref, dv_ref) in enumerate(
                ((q0, k0, v0, do0, sp0, dq0, dk0, dv0), (q1, k1, v1, do1, sp1, dq1, dk1, dv1))):
            for h in range(HEADS):
                sl = slice(h * HEAD_DIM, (h + 1) * HEAD_DIM)
                q, k, v = q_ref[:, sl], k_ref[:, sl], v_ref[:, sl]
                dov = do_ref[:, sl]
                dm, wq, ws = dec[dirn, h, 0], dec[dirn, h, 1], dec[dirn, h, 2]
                a = _dot_nt(q, k)
                sp = sp_ref[h]
                dsn = dst[dirn, h]
                gc = jnp.exp(lg_ref[dirn, h] * CHUNK)
                g1 = _dot_nt(dov, sp)
                da = _dot_nt(dov, v) * dm
                dq = g1 * wq + _dot(da, k)
                h1 = _dot_nt(v, dsn)
                dk = _dot_tn(da, q) + h1 * ws
                dv_ref[:, sl] = (_dot_tn(a * dm, dov) + _dot(k * ws, dsn)).astype(BF16)
                dst[dirn, h] = gc * dsn + _dot_tn(q * wq, dov)
                term = da * a * relf[dirn] + q * g1 * dec[dirn, h, 3] + k * h1 * dec[dirn, h, 4] + sp * dsn * (CHUNK * gc)
                dlg_ref[dirn * HEADS + h:dirn * HEADS + h + 1, 0:HEAD_DIM] += jnp.sum(term, axis=0, keepdims=True)
                dq_ref[:, sl] = dq.astype(BF16)
                dk_ref[:, sl] = dk.astype(BF16)

    wide = [pl.BlockSpec((CHUNK, RET_W), lambda s, row=row: (row(s), 0)) for row in rows]
    state = pl.BlockSpec((None, HEADS, CHUNK, HEAD_DIM), lambda s: (nc - 1 - s, 0, 0, 0))
    return _pcall_ride(
        body, ride, (lg,) + (qkv,) * 6 + (do, do, s_prev[0], s_prev[1]), name="ret_bwd", grid=(nc,),
        in_specs=[pl.BlockSpec(memory_space=pltpu.SMEM)] + _ret_in_specs(rows) + wide + [state, state],
        out_specs=(wide[0],) * 3 + (wide[1],) * 3 + (pl.BlockSpec((2 * HEADS, 8 * HEAD_DIM), lambda s: (0, 0)),),
        out_shape=(_sds((n, RET_W), BF16),) * 6 + (_sds((2 * HEADS, 8 * HEAD_DIM), F32),),
        scratch_shapes=[pltpu.VMEM((2, HEADS, CHUNK, HEAD_DIM), F32), pltpu.VMEM((2, HEADS, 5, CHUNK, CHUNK), F32),
                        pltpu.VMEM((2, CHUNK, CHUNK), F32)],
        compiler_params=_cp())


def _shift_rows(cur, prev8, next8, k, seg_start, seg_end):
    tm = cur.shape[0]
    rows = _rows_iota(cur.shape)
    if k < 0:
        out = pltpu.roll(cur, -k, 0)
        for j in range(-k):
            halo = jnp.where(seg_start, 0.0, prev8[SUB + k + j:SUB + k + j + 1, :])
            out = jnp.where(rows == j, halo, out)
    else:
        out = pltpu.roll(cur, tm - k, 0)
        for j in range(k):
            halo = jnp.where(seg_end, 0.0, next8[j:j + 1, :])
            out = jnp.where(rows == tm - k + j, halo, out)
    return out


def _seg_flags(t, cb, nb):
    return jnp.logical_or(t == 0, t == cb), jnp.logical_or(t == cb - 1, t == nb - 1)


def _halo_specs(tile_of, n_rows, col):
    per = TM // SUB
    return [pl.BlockSpec((TM, LRU_W), lambda s: (tile_of(s), col)),
            pl.BlockSpec((SUB, LRU_W), lambda s: (jnp.maximum(tile_of(s) * per - 1, 0), col)),
            pl.BlockSpec((SUB, LRU_W), lambda s: (jnp.minimum((tile_of(s) + 1) * per, n_rows // SUB - 1), col))]


def _lru_gates(xr, prev8, next8, seg_start, seg_end, cw_ref, cb_ref, wg_ref, bg_ref, sp_ref):
    xm1 = _shift_rows(xr, prev8, next8, -1, seg_start, seg_end)
    xp1 = _shift_rows(xr, prev8, next8, 1, seg_start, seg_end)
    xp2 = _shift_rows(xr, prev8, next8, 2, seg_start, seg_end)
    xc = cb_ref[...] + xm1 * cw_ref[0:1, :] + xr * cw_ref[1:2, :] + xp1 * cw_ref[2:3, :] + xp2 * cw_ref[3:4, :]
    pre = _dot(xc, wg_ref[...]) + bg_ref[...]
    r = _sigmoid(pre[:, :LRU_W])
    i = _sigmoid(pre[:, LRU_W:])
    la = (-LRU_C) * r * sp_ref[...]
    a = jnp.exp(la)
    th = jnp.tanh(la)
    sq = jnp.sqrt(-2.0 * th / (1.0 - th))
    return xc, r, i, a, sq


def _scan_tile(a, b, ascending, a_sc, b_sc, carry, out_ref):
    tm, w = a.shape
    nsub = tm // SUB
    a = a.reshape(nsub, SUB, w)
    b = b.reshape(nsub, SUB, w)
    r8 = lax.broadcasted_iota(jnp.int32, a.shape, 1)
    for k in (1, 2, 4):
        if ascending:
            m = r8 >= k
            a_s, b_s = pltpu.roll(a, k, 1), pltpu.roll(b, k, 1)
        else:
            m = r8 < SUB - k
            a_s, b_s = pltpu.roll(a, SUB - k, 1), pltpu.roll(b, SUB - k, 1)
        b = a * jnp.where(m, b_s, 0.0) + b
        a = a * jnp.where(m, a_s, 1.0)
    a_sc[...] = a.reshape(tm, w)
    b_sc[...] = b.reshape(tm, w)

    def step(j, c):
        off = pl.multiple_of((j if ascending else nsub - 1 - j) * SUB, SUB)
        hb = a_sc[pl.ds(off, SUB), :] * c + b_sc[pl.ds(off, SUB), :]
        out_ref[pl.ds(off, SUB), :] = hb
        last = hb[SUB - 1:SUB, :] if ascending else hb[0:1, :]
        return jnp.broadcast_to(last, c.shape)

    carry[...] = lax.fori_loop(0, nsub, step, carry[...], unroll=8)


def _lru_fwd(p, wg, bg, sp, cw, cbias, dirn, cb, ride=None):
    n = p.shape[0]
    nb = n // TM
    tile_of = lambda s: _tile_order(dirn, s, cb, nb)

    def body(x_ref, xp_ref, xn_ref, wg_ref, bg_ref, sp_ref, cw_ref, cb_ref, h_ref, cin_ref, carry, a_sc, b_sc):
        s = pl.program_id(0)

        @pl.when(s == 0)
        def _():
            carry[...] = jnp.zeros_like(carry)

        seg_start, seg_end = _seg_flags(tile_of(s), cb, nb)
        xc, r, i, a, sq = _lru_gates(x_ref[...], xp_ref[...], xn_ref[...], seg_start, seg_end,
                                     cw_ref, cb_ref, wg_ref, bg_ref, sp_ref)
        cin_ref[...] = carry[...]
        _scan_tile(a, sq * (i * xc), dirn == 0, a_sc, b_sc, carry, h_ref)

    full = lambda shape: pl.BlockSpec(shape, lambda s: (0,) * len(shape))
    return _pcall_ride(
        body, ride, (p, p, p, wg, bg, sp, cw, cbias), name=f"lru_fwd{dirn}", grid=(nb,),
        in_specs=_halo_specs(tile_of, n, COL_XR) + [full((LRU_W, 2 * LRU_W)), full((1, 2 * LRU_W)), full((1, LRU_W)),
                                               full((4, LRU_W)), full((1, LRU_W))],
        out_specs=(pl.BlockSpec((TM, LRU_W), lambda s: (tile_of(s), 0)),
                   pl.BlockSpec((None, SUB, LRU_W), lambda s: (tile_of(s), 0, 0))),
        out_shape=(_sds((n, LRU_W), F32), _sds((nb, SUB, LRU_W), F32)),
        scratch_shapes=[pltpu.VMEM((SUB, LRU_W), F32), pltpu.VMEM((TM, LRU_W), F32), pltpu.VMEM((TM, LRU_W), F32)],
        compiler_params=_cp())


def _lru_bwd(p, wg, bg, sp, cw, cbias, h, cin, dhs, dirn, cb, ride=None):
    n = p.shape[0]
    nb = n // TM
    tile_of = lambda s: _tile_order(dirn, nb - 1 - s, cb, nb)

    def body(x_ref, xp_ref, xn_ref, wg_ref, bg_ref, sp_ref, cw_ref, cb_ref, h_ref, cin_ref, dhs_ref,
             dxc_ref, dwd_ref, acc_ref, carry, a_sc, b_sc, mu_sc, dwg_ref):
        s = pl.program_id(0)

        @pl.when(s == 0)
        def _():
            carry[...] = jnp.zeros_like(carry)
            dwg_ref[...] = jnp.zeros_like(dwg_ref)
            acc_ref[...] = jnp.zeros_like(acc_ref)

        seg_start, seg_end = _seg_flags(tile_of(s), cb, nb)
        xc, r, i, a, sq = _lru_gates(x_ref[...], xp_ref[...], xn_ref[...], seg_start, seg_end,
                                     cw_ref, cb_ref, wg_ref, bg_ref, sp_ref)
        rows = _rows_iota(a.shape)
        hv = h_ref[...]
        dh = dhs_ref[...]
        mu_next = carry[0:1, :]
        _scan_tile(a, a * dh, dirn == 1, a_sc, b_sc, carry, mu_sc)
        mu = mu_sc[...]
        if dirn == 0:
            hprev = jnp.where(rows == 0, cin_ref[0:1, :], pltpu.roll(hv, 1, 0))
            lam = dh + jnp.where(rows == TM - 1, mu_next, pltpu.roll(mu, TM - 1, 0))
        else:
            hprev = jnp.where(rows == TM - 1, cin_ref[0:1, :], pltpu.roll(hv, TM - 1, 0))
            lam = dh + jnp.where(rows == 0, mu_next, pltpu.roll(mu, 1, 0))
        ds = lam * (i * xc)
        di = lam * (sq * xc)
        dla = lam * hprev * a - ds * (a * a) / jnp.maximum(sq, 1e-20)
        dpr = dla * ((-LRU_C) * sp_ref[...]) * r * (1.0 - r)
        dpi = di * i * (1.0 - i)
        dpre = jnp.concatenate([dpr, dpi], axis=1)
        dxc_ref[...] = lam * (sq * i) + _dot_nt(dpre, wg_ref[...])
        dwg_ref[...] += _dot_tn(xc, dpre)
        acc_ref[0:1, :] += jnp.sum(dpre, axis=0, keepdims=True)
        acc_ref[1:2, 0:LRU_W] += jnp.sum(dla * ((-LRU_C) * r), axis=0, keepdims=True)

        @pl.when(s == nb - 1)
        def _():
            low = lax.broadcasted_iota(jnp.int32, (LRU_BD, 2 * LRU_BD), 1) < LRU_BD
            for half in (0, LRU_W):
                for m in range(LRU_BLOCKS // 2):
                    lanes = slice(half + 2 * LRU_BD * m, half + 2 * LRU_BD * (m + 1))
                    even = dwg_ref[2 * m * LRU_BD:(2 * m + 1) * LRU_BD, lanes]
                    odd = dwg_ref[(2 * m + 1) * LRU_BD:(2 * m + 2) * LRU_BD, lanes]
                    dwd_ref[:, lanes] = jnp.where(low, even, odd)

    full = lambda shape: pl.BlockSpec(shape, lambda s: (0,) * len(shape))
    tile = pl.BlockSpec((TM, LRU_W), lambda s: (tile_of(s), 0))
    return _pcall_ride(
        body, ride, (p, p, p, wg, bg, sp, cw, cbias, h, cin, dhs), name=f"lru_bwd{dirn}", grid=(nb,),
        in_specs=_halo_specs(tile_of, n, COL_XR) + [full((LRU_W, 2 * LRU_W)), full((1, 2 * LRU_W)), full((1, LRU_W)),
                                               full((4, LRU_W)), full((1, LRU_W)), tile,
                                               pl.BlockSpec((None, SUB, LRU_W), lambda s: (tile_of(s), 0, 0)), tile],
        out_specs=(tile, full((LRU_BD, 2 * LRU_W)), full((8, 2 * LRU_W))),
        out_shape=(_sds((n, LRU_W), F32), _sds((LRU_BD, 2 * LRU_W), F32), _sds((8, 2 * LRU_W), F32)),
        scratch_shapes=[pltpu.VMEM((SUB, LRU_W), F32)] + [pltpu.VMEM((TM, LRU_W), F32)] * 3
        + [pltpu.VMEM((LRU_W, 2 * LRU_W), F32)],
        compiler_params=_cp())


def _assemble_dp(dqs, dks, dvs, dg, dgate, dxcs, p, cw, cosf, sins, cb, ride=None):
    n = p.shape[0]
    nb = n // TM
    tile_of = lambda s: s

    def body(dqf, dqb, dkf, dkb, dvf, dvb, dg_ref, dgate_ref, cf, pf, nf, cb_, pb, nb_, x_ref, xp_ref, xn_ref,
             cw_ref, cos_ref, sin_ref, dp_ref, acc_ref):
        s = pl.program_id(0)

        @pl.when(s == 0)
        def _():
            acc_ref[...] = jnp.zeros_like(acc_ref)

        seg_start, seg_end = _seg_flags(s, cb, nb)
        dq = dqf[...].astype(F32) + dqb[...].astype(F32)
        dk = dkf[...].astype(F32) + dkb[...].astype(F32)
        cosv, sinv = cos_ref[...], sin_ref[...]
        for h in range(HEADS):
            sl = slice(h * HEAD_DIM, (h + 1) * HEAD_DIM)
            sk = slice(RET_W + h * HEAD_DIM, RET_W + (h + 1) * HEAD_DIM)
            dp_ref[:, sl] = (dq[:, sl] * cosv + pltpu.roll(dq[:, sl] * sinv, HEAD_DIM // 2, 1)).astype(BF16)
            dp_ref[:, sk] = ((dk[:, sl] * cosv + pltpu.roll(dk[:, sl] * sinv, HEAD_DIM // 2, 1)) * K_SCALE).astype(BF16)
        dp_ref[:, 2 * RET_W:3 * RET_W] = (dvf[...].astype(F32) + dvb[...].astype(F32)).astype(BF16)
        dp_ref[:, 3 * RET_W:4 * RET_W] = dg_ref[...].astype(BF16)
        dxc = cf[...] + cb_[...]
        dprev = pf[...] + pb[...]
        dnext = nf[...] + nb_[...]
        dxr = (_shift_rows(dxc, dprev, dnext, 1, seg_start, seg_end) * cw_ref[0:1, :] + dxc * cw_ref[1:2, :]
               + _shift_rows(dxc, dprev, dnext, -1, seg_start, seg_end) * cw_ref[2:3, :]
               + _shift_rows(dxc, dprev, dnext, -2, seg_start, seg_end) * cw_ref[3:4, :])
        dp_ref[:, 4 * RET_W:4 * RET_W + LRU_W] = dxr.astype(BF16)
        dp_ref[:, 4 * RET_W + LRU_W:] = dgate_ref[...].astype(BF16)
        xr, xp, xn = x_ref[...], xp_ref[...], xn_ref[...]
        for j, k in enumerate((-1, 0, 1, 2)):
            xs = xr if k == 0 else _shift_rows(xr, xp, xn, k, seg_start, seg_end)
            acc_ref[j:j + 1, 0:LRU_W] += jnp.sum(dxc * xs, axis=0, keepdims=True)
        acc_ref[4:5, 0:LRU_W] += jnp.sum(dxc, axis=0, keepdims=True)

    t = pl.BlockSpec((TM, RET_W), lambda s: (s, 0))
    args = (dqs[0], dqs[1], dks[0], dks[1], dvs[0], dvs[1], dg, dgate, dxcs[0], dxcs[0], dxcs[0], dxcs[1], dxcs[1], dxcs[1],
            p, p, p, cw, cosf, sins)
    tab = pl.BlockSpec((TM, HEAD_DIM), lambda s: (s, 0))
    return _pcall_ride(
        body, ride, args, name="assemble_dp", grid=(nb,),
        in_specs=[t] * 8 + _halo_specs(tile_of, n, 0) * 2 + _halo_specs(tile_of, n, COL_XR)
        + [pl.BlockSpec((4, LRU_W), lambda s: (0, 0)), tab, tab],
        out_specs=(pl.BlockSpec((TM, 4 * RET_W + 2 * LRU_W), lambda s: (s, 0)), pl.BlockSpec((8, 2 * LRU_W), lambda s: (0, 0))),
        out_shape=(_sds((n, 4 * RET_W + 2 * LRU_W), BF16), _sds((8, 2 * LRU_W), F32)), compiler_params=_cp())


def _adamw(g, w, m, v):
    nm = ADAM_B1 * m + (1.0 - ADAM_B1) * g
    nv = ADAM_B2 * v + (1.0 - ADAM_B2) * (g * g)
    m_hat = nm / (1.0 - ADAM_B1 ** ADAM_STEP)
    v_hat = nv / (1.0 - ADAM_B2 ** ADAM_STEP)
    return (-ADAM_LR) * (m_hat / (jnp.sqrt(v_hat) + ADAM_EPS) + ADAM_WD * w), nm, nv


def _adam_many(items, name):
    n = len(items)

    def body(*refs):
        for i in range(n):
            g, w, m, v = (r[...] for r in refs[4 * i:4 * i + 4])
            for o_ref, val in zip(refs[4 * n + 3 * i:4 * n + 3 * i + 3], _adamw(g, w, m, v)):
                o_ref[...] = val

    out_shape = tuple(_sds(it[1].shape, F32) for it in items for _ in range(3))
    res = _pcall(body, name=name, out_shape=out_shape, compiler_params=_cp())(*[a for it in items for a in it])
    return [tuple(res[3 * i:3 * i + 3]) for i in range(n)]


def _sum_adam(parts_list, w, m, v, name):
    nparts, _, c = parts_list[0].shape
    r = w.shape[0]
    tr = min(parts_list[0].shape[1], 128)
    starts, o = [], 0
    for pa in parts_list:
        starts.append(o)
        o += pa.shape[1] // tr
    nseg = len(parts_list)

    def body(*refs):
        p_refs = refs[:nseg]
        w_ref, m_ref, v_ref, g_ref, d_ref, nm_ref, nv_ref = refs[nseg:]
        i = pl.program_id(0)
        for s, p_ref in enumerate(p_refs):
            end = starts[s + 1] if s + 1 < nseg else r // tr

            @pl.when(jnp.logical_and(i >= starts[s], i < end))
            def _():
                g = p_ref[0].astype(F32)
                for j in range(1, nparts):
                    g = g + p_ref[j].astype(F32)
                g_ref[...] = g
                d_ref[...], nm_ref[...], nv_ref[...] = _adamw(g, w_ref[...], m_ref[...], v_ref[...])

    def seg_spec(s):
        last = parts_list[s].shape[1] // tr - 1
        return pl.BlockSpec((nparts, tr, c), lambda i: (0, jnp.clip(i - starts[s], 0, last), 0))

    t = pl.BlockSpec((tr, c), lambda i: (i, 0))
    return _pcall(
        body, name=name, grid=(r // tr,),
        in_specs=[seg_spec(s) for s in range(nseg)] + [t, t, t],
        out_specs=(t, t, t, t), out_shape=(_sds((r, c), F32),) * 4, compiler_params=_cp(),
    )(*parts_list, w, m, v)


def _sum_parts(parts, name):
    nparts, r, c = parts.shape

    def body(p_ref, o_ref):
        g = p_ref[0]
        for j in range(1, nparts):
            g = g + p_ref[j]
        o_ref[...] = g

    return _pcall(body, name=name, out_shape=_sds((r, c), parts.dtype), compiler_params=_cp())(parts)


def _rot_tables(l_len, t_len):
    rows = t_len // GRID_W
    n_freq = HEAD_DIM // 4
    inv = ROPE_BASE ** (-jnp.arange(n_freq, dtype=F32) / n_freq)
    ang_r = jnp.arange(rows, dtype=F32)[:, None] * inv
    ang_c = jnp.arange(GRID_W, dtype=F32)[:, None] * inv
    cos = jnp.concatenate([jnp.repeat(jnp.cos(ang_r), GRID_W, axis=0), jnp.tile(jnp.cos(ang_c), (rows, 1))], axis=-1)
    sin = jnp.concatenate([jnp.repeat(jnp.sin(ang_r), GRID_W, axis=0), jnp.tile(jnp.sin(ang_c), (rows, 1))], axis=-1)
    cosf = jnp.concatenate([jnp.ones((l_len, HEAD_DIM), F32), jnp.concatenate([cos, cos], axis=-1)], axis=0)
    sins = jnp.concatenate([jnp.zeros((l_len, HEAD_DIM), F32), jnp.concatenate([-sin, sin], axis=-1)], axis=0)
    return cosf, sins


def _block_diag(w):
    eye = jnp.eye(LRU_BLOCKS, dtype=w.dtype)
    return (w[:, :, None, :] * eye[:, None, :, None]).reshape(LRU_W, LRU_W)


def _blocks_from_lanes(dwd_half):
    return dwd_half.reshape(LRU_BD, LRU_BLOCKS, LRU_BD).transpose(1, 0, 2)


def _silu(x):
    return x * jax.nn.sigmoid(x)


def kernel(x, c, ctx, c_ctx, w_ada, b_ada, norm1_g, norm2_g, w_in, ret_decay, conv_w, conv_b, lru_wa, lru_ba, lru_wx, lru_bx, lru_lambda, w_out, w_mlp1, w_mlp2, final_g, loss_target, m_c_ctx, m_w_ada, m_b_ada, m_norm1_g, m_norm2_g, m_w_in, m_ret_decay, m_conv_w, m_conv_b, m_lru_wa, m_lru_ba, m_lru_wx, m_lru_bx, m_lru_lambda, m_w_out, m_w_mlp1, m_w_mlp2, m_final_g, v_c_ctx, v_w_ada, v_b_ada, v_norm1_g, v_norm2_g, v_w_in, v_ret_decay, v_conv_w, v_conv_b, v_lru_wa, v_lru_ba, v_lru_wx, v_lru_bx, v_lru_lambda, v_w_out, v_w_mlp1, v_w_mlp2, v_final_g):
    t_len, d = x.shape[1], x.shape[2]
    l_len = ctx.shape[1]
    cb, cc = l_len // TM, l_len // CHUNK
    me = 4 * lax.axis_index("x") + 2 * lax.axis_index("y") + lax.axis_index("c")
    x2d, ctx2d, tgt2d = x[0], ctx[0], loss_target[0]
    ada_cols = w_ada.shape[2]
    wa2d = w_ada[0]

    sc_loc = conv_w.shape[2]
    pack_a = jnp.zeros((8, d), F32)
    pack_a = pack_a.at[0].set(_silu(c[0]))
    pack_a = pack_a.at[1, :4 * sc_loc].set(conv_w[0].reshape(-1))
    pack_a = pack_a.at[2, :2 * sc_loc].set(lru_ba[0].reshape(-1))
    pack_a = pack_a.at[3, :2 * sc_loc].set(lru_bx[0].reshape(-1))
    pack_a = pack_a.at[4, :2 * sc_loc].set(lru_lambda[0].reshape(-1))
    all_a = _all_gather_small(pack_a, "gather_small_in")
    s16 = jnp.zeros((16, d), F32).at[0:8].set(all_a[:, 0, :]).at[8].set(_silu(c_ctx))

    def unshard(row, k):
        return all_a[:, row, :k * sc_loc].reshape(N_DEV, k, sc_loc).transpose(1, 0, 2).reshape(k, N_DEV * sc_loc)

    conv_w_full = unshard(1, 4)
    ba_full, bx_full, lam_full = unshard(2, 2), unshard(3, 2), unshard(4, 2)

    b_cols = lax.dynamic_slice(b_ada, (0, me * ada_cols), (1, ada_cols))
    mod_parts = _all_gather_small(_mod_part(s16, wa2d, b_cols), "gather_mod")
    mod_all = mod_parts.transpose(1, 0, 2).reshape(16, N_DEV * ada_cols)
    mod_me = lax.dynamic_slice(mod_all, (me, 0), (1, 6 * d)).reshape(6, d)
    mod_c = mod_all[8].reshape(6, d)
    modrows = jnp.concatenate([mod_c[0:2], mod_me], axis=0)

    lg = jax.nn.log_sigmoid(ret_decay[0])
    sp = jax.nn.softplus(-lam_full)
    wg = [jnp.concatenate([_block_diag(lru_wa[0, dd]), _block_diag(lru_wx[0, dd])], axis=1).astype(BF16) for dd in (0, 1)]
    bg = [jnp.concatenate([ba_full[dd], bx_full[dd]])[None, :] for dd in (0, 1)]
    cosf, sins = _rot_tables(l_len, t_len)

    (hn, hnt), win_g = _norm1_fwd(ctx2d, x2d, norm1_g, modrows, cb, ride=("gather", w_in[0].astype(BF16)))
    (qkv, p), w1_g = _mm_in(hn, win_g, cosf, sins, ride=("gather", w_mlp1[0].astype(BF16)))
    (o0, o1, sp0, sp1), w2_g = _ret_fwd(qkv, lg, cc, ride=("gather", w_mlp2[0].astype(BF16)))
    w2_g = w2_g.reshape(1, 4 * d, d)
    o, s_prev = [o0, o1], [sp0, sp1]
    (h0, cin0), wout_g = _lru_fwd(p, wg[0], bg[0], sp[0:1], conv_w_full, conv_b, 0, cb, ride=("gather", w_out[0].astype(BF16)))
    wout_g = wout_g.reshape(1, d, d)
    (h1, cin1), _ = _lru_fwd(p, wg[1], bg[1], sp[1:2], conv_w_full, conv_b, 1, cb)
    h, cin = [h0, h1], [cin0, cin1]
    mix = _mix_fwd(o[0], o[1], p, h[0], h[1], cb, t_len)
    y, x1, h2, h2t = _mm_nn(mix, wout_g, F32, "mm_out_norm2", tm=TL, epi=_res_norm2_epilogue(x2d, norm2_g, modrows, TL))
    r = _mm_nn(h2, w1_g, BF16, "mm_mlp1", relu_out=True, tm=TL)
    dx2, dz, dzt, facc = _mm_nn(r, w2_g, F32, "mm_mlp2_final", square_lhs=True, tm=TL, vmem_mb=58,
                                epi=_final_epilogue(x1, tgt2d, final_g[None, :], modrows, TL))

    du = _mm_nt(dz, w2_g, BF16, "mm_da2", relu_mul=r, tm=TL, vmem_mb=58)
    gw2 = _mm_wgrad(dzt, r, "mm_dw2", w2_g.shape[1] // N_DEV, BF16, transpose_out=True, square_rhs=True)
    gw1_lo, gw1_hi = _mm_wgrad(h2t, du, "mm_dw1", w1_g.shape[2], BF16, halves=True)
    (dx1, dy, n2acc), gw1_lo_all = _mm_nt(du, w1_g, F32, "mm_dh2_norm2", ride=("a2a", gw1_lo), tm=TL, vmem_mb=60,
                                          epi=_bwd_norm2_epilogue(x1, dx2, y, norm2_g, modrows, TL))
    gwo = _mm_tn(mix, dy, "mm_dwout", False, d, BF16, 512).reshape(N_DEV, -1, d)
    (do, dg, dgate, dhs), gw1_hi_all = _mm_nt(dy, wout_g, F32, "mm_dmix_mix", ride=("a2a", gw1_hi),
                                              epi=_mix_bwd_epilogue(o[0], o[1], p, h[0], h[1], cb))
    gw1_all = [gw1_lo_all, gw1_hi_all]
    dxcs, dwgs, laccs, rides = [], [], [], [("a2a", gwo), None]
    (dq0, dk0, dv0, dq1, dk1, dv1, dlg_lanes), gw2_all = _ret_bwd(qkv, lg, do, s_prev, cc, ride=("a2a", gw2))
    dqs, dks, dvs = [dq0, dq1], [dk0, dk1], [dv0, dv1]
    for dd in (0, 1):
        (dxc_, dwg_, lacc_), got_ = _lru_bwd(p, wg[dd], bg[dd], sp[dd:dd + 1], conv_w_full, conv_b, h[dd], cin[dd], dhs, dd, cb,
                                             ride=rides[dd])
        dxcs.append(dxc_); dwgs.append(dwg_); laccs.append(lacc_)
        gwo_all = got_ if dd == 0 else gwo_all
    (dp, cacc), _ = _assemble_dp(dqs, dks, dvs, dg, dgate, dxcs, p, conv_w_full, cosf, sins, cb)
    pack_b = jnp.concatenate([n2acc, facc, cacc, laccs[0], laccs[1], dlg_lanes, dwgs[0], dwgs[1]], axis=0)
    gwi, all_b = _mm_wgrad(hnt, dp, "mm_dwin", win_g.shape[2], BF16, ride=("gather", pack_b))
    (grad_x, n1acc), gwi_all = _mm_nt(dp, win_g, F32, "mm_dhn_norm1", ride=("a2a", gwi),
                                      epi=_bwd_norm1_epilogue(ctx2d, x2d, dx1, norm1_g, modrows, cb))
    all_n1 = _all_gather_small(n1acc, "gather_norm1_grads")
    tot = _sum_parts(all_b, "sum_small_grads")
    t_n1 = _sum_parts(all_n1, "sum_norm1_grads")
    t_n2, t_f, t_conv, t_dlg = tot[0:8], tot[8:16], tot[16:24, :LRU_W], tot[40:48, :HEAD_DIM]
    t_l = [tot[24:32], tot[32:40]]
    t_dwd = [tot[48:48 + LRU_BD], tot[48 + LRU_BD:48 + 2 * LRU_BD]]
    loss = (0.5 / d) * jnp.sum(t_f[2])
    t_wa = jnp.stack([_blocks_from_lanes(t_dwd[dd][:, :LRU_W]) for dd in (0, 1)])
    t_wx = jnp.stack([_blocks_from_lanes(t_dwd[dd][:, LRU_W:]) for dd in (0, 1)])
    t_ba = jnp.stack([t_l[dd][0, :LRU_W] for dd in (0, 1)])
    t_bx = jnp.stack([t_l[dd][0, LRU_W:] for dd in (0, 1)])
    t_sp = jnp.stack([t_l[dd][1, :LRU_W] for dd in (0, 1)])
    dm_rows = jnp.stack([all_n1[:, 2, :], all_n1[:, 3, :], all_b[:, 3, :], all_b[:, 0, :], all_b[:, 1, :], all_b[:, 9, :]],
                        axis=1).reshape(N_DEV, 6 * d)
    dm_c = jnp.concatenate([t_n1[0], t_n1[1], jnp.zeros((4 * d,), F32)])
    dm16 = jnp.zeros((16, 6 * d), F32).at[0:8].set(dm_rows).at[8].set(dm_c)
    g_b_ada = jnp.sum(dm16, axis=0)[None, :]
    dm_cols = lax.dynamic_slice(dm16, (0, me * ada_cols), (16, ada_cols))
    g_w_ada, ds16 = _ada_bwd(s16, dm_cols, wa2d)
    ds_all = _all_gather_small(ds16[8:16], "gather_dsilu")
    dsilu_cc = _sum_parts(ds_all, "sum_dsilu")[0]
    sg_cc = jax.nn.sigmoid(c_ctx)
    g_c_ctx = dsilu_cc * (sg_cc * (1.0 + c_ctx * (1.0 - sg_cc)))

    g_ret_decay = jnp.sum(t_dlg, axis=-1).reshape(2, HEADS) * jax.nn.sigmoid(-ret_decay[0])
    g_lambda_full = -t_sp * jax.nn.sigmoid(-lam_full)

    def my_cols(full):
        return lax.dynamic_slice(full, (0, me * sc_loc), (full.shape[0], sc_loc))

    small_g = dict(
        c_ctx=g_c_ctx[None], b_ada=g_b_ada, norm1_g=t_n1[4:5], norm2_g=t_n2[2:3], ret_decay=g_ret_decay,
        conv_w=my_cols(t_conv[0:4]), conv_b=t_conv[4:5], lru_wa=t_wa.reshape(-1, LRU_BD), lru_ba=my_cols(t_ba),
        lru_wx=t_wx.reshape(-1, LRU_BD), lru_bx=my_cols(t_bx), lru_lambda=my_cols(g_lambda_full), final_g=t_f[0:1])
    small = dict(
        c_ctx=(c_ctx, m_c_ctx, v_c_ctx), b_ada=(b_ada, m_b_ada, v_b_ada), norm1_g=(norm1_g, m_norm1_g, v_norm1_g),
        norm2_g=(norm2_g, m_norm2_g, v_norm2_g), ret_decay=(ret_decay, m_ret_decay, v_ret_decay),
        conv_w=(conv_w, m_conv_w, v_conv_w), conv_b=(conv_b, m_conv_b, v_conv_b), lru_wa=(lru_wa, m_lru_wa, v_lru_wa),
        lru_ba=(lru_ba, m_lru_ba, v_lru_ba), lru_wx=(lru_wx, m_lru_wx, v_lru_wx), lru_bx=(lru_bx, m_lru_bx, v_lru_bx),
        lru_lambda=(lru_lambda, m_lru_lambda, v_lru_lambda), final_g=(final_g, m_final_g, v_final_g))
    names = list(small)
    items = [(small_g[k],) + tuple(a.reshape(small_g[k].shape) for a in small[k]) for k in names]
    res = {}
    for k, it, (d_, m_, v_) in zip(names, items, _adam_many(items, "adam_small")):
        shape = small[k][0].shape
        res[k] = tuple(a.reshape(shape) for a in (it[0], d_, m_, v_))

    def big(parts, w, m, v, name):
        out = _sum_adam(parts, w[0], m[0], v[0], name)
        return tuple(a[None] for a in out)

    res["w_ada"] = big([g_w_ada[None]], w_ada, m_w_ada, v_w_ada, "adam_w_ada")
    res["w_in"] = big([gwi_all], w_in, m_w_in, v_w_in, "adam_w_in")
    res["w_out"] = big([gwo_all], w_out, m_w_out, v_w_out, "adam_w_out")
    res["w_mlp1"] = big(gw1_all, w_mlp1, m_w_mlp1, v_w_mlp1, "adam_w_mlp1")
    res["w_mlp2"] = big([gw2_all], w_mlp2, m_w_mlp2, v_w_mlp2, "adam_w_mlp2")

    order = ["c_ctx", "w_ada", "b_ada", "norm1_g", "norm2_g", "w_in", "ret_decay", "conv_w", "conv_b", "lru_wa", "lru_ba",
             "lru_wx", "lru_bx", "lru_lambda", "w_out", "w_mlp1", "w_mlp2", "final_g"]
    outs = [loss, grad_x[None]]
    for j in range(4):
        outs += [res[k][j] for k in order]
    return tuple(outs)
```

```python
import jax
import jax.numpy as jnp
from jax import lax
from jax.experimental import pallas as pl
from jax.experimental.pallas import tpu as pltpu

F32 = jnp.float32
BF16 = jnp.bfloat16
AXES = ("x", "y", "c")
N_DEV = 8
MESH = pl.DeviceIdType.MESH

HEADS = 4
HEAD_DIM = 128
CHUNK = 128
RET_W = HEADS * HEAD_DIM
LRU_W = 512
LRU_BLOCKS = 8
LRU_BD = LRU_W // LRU_BLOCKS
LRU_C = 8.0
EPS = 1e-6
K_SCALE = HEAD_DIM ** -0.5
ROPE_BASE = 10000.0
GRID_W = 64
TM = 256
TL = 512
SUB = 8

ADAM_LR = 0.001
ADAM_B1 = 0.9
ADAM_B2 = 0.999
ADAM_EPS = 1e-08
ADAM_WD = 0.01
ADAM_STEP = 10

COL_G, COL_XR, COL_GATE = 0, 1, 2

R_CSH1, R_CSC1, R_SH1, R_SC1, R_G1, R_SH2, R_SC2, R_G2 = range(8)


def _pcall(body, **kw):
    return pl.pallas_call(body, **kw)


def _cp(vmem_mb=48):
    return pltpu.CompilerParams(vmem_limit_bytes=vmem_mb << 20)


def _sds(shape, dtype):
    return jax.ShapeDtypeStruct(shape, dtype)


def _dot(a, b):
    return jnp.dot(a.astype(BF16), b.astype(BF16), preferred_element_type=F32)


def _dot_nt(a, b):
    return lax.dot_general(a.astype(BF16), b.astype(BF16), (((1,), (1,)), ((), ())), preferred_element_type=F32)


def _dot_tn(a, b):
    return lax.dot_general(a.astype(BF16), b.astype(BF16), (((0,), (0,)), ((), ())), preferred_element_type=F32)


def _sigmoid(x):
    return 0.5 * jnp.tanh(0.5 * x) + 0.5


def _gelu(x):
    return 0.5 * x * (1.0 + jnp.tanh(0.7978845608028654 * (x + 0.044715 * x * x * x)))


def _dgelu(x):
    t = jnp.tanh(0.7978845608028654 * (x + 0.044715 * x * x * x))
    return 0.5 * (1.0 + t) + 0.5 * x * (1.0 - t * t) * 0.7978845608028654 * (1.0 + 3.0 * 0.044715 * x * x)


def _rows_iota(shape):
    return lax.broadcasted_iota(jnp.int32, shape, 0)


def _tile_order(dirn, s, cb, nb):
    if dirn == 0:
        return s
    return jnp.where(s < cb, cb - 1 - s, nb - 1 - (s - cb))


_SEMS = [pltpu.SemaphoreType.DMA((7,)), pltpu.SemaphoreType.DMA((7,)), pltpu.SemaphoreType.DMA(())]
_ANY = pl.BlockSpec(memory_space=pl.ANY)


def _gather_copies(x_ref, out_ref, send_sems, recv_sems, local_sem):
    mx, my, mc = lax.axis_index("x"), lax.axis_index("y"), lax.axis_index("c")
    me, sibling = (mx, my, mc), (mx, my, 1 - mc)
    chips = [(1 - mx, my), (mx, 1 - my), (1 - mx, 1 - my)]

    def slot(px, py, pc):
        return out_ref.at[4 * px + 2 * py + pc]

    def copy(k, block, to, src=None):
        return pltpu.make_async_remote_copy(
            src_ref=slot(*block) if src is None else src, dst_ref=slot(*block),
            send_sem=send_sems.at[k], recv_sem=recv_sems.at[k], device_id=to, device_id_type=MESH)

    mine = pltpu.make_async_copy(x_ref, slot(*me), local_sem)
    first = [copy(0, me, sibling, src=x_ref)] + [copy(1 + j, me, (*chip, mc), src=x_ref) for j, chip in enumerate(chips)]
    passed = [copy(4 + j, (*chip, mc), sibling) for j, chip in enumerate(chips)]
    recv_ici = [copy(1 + j, (*chip, mc), me) for j, chip in enumerate(chips)]
    recv_d2d = [copy(0, sibling, me)] + [copy(4 + j, (*chip, 1 - mc), me) for j, chip in enumerate(chips)]
    return mine, first, passed, recv_ici, recv_d2d


def _gather_start(*refs):
    mine, first, _, _, _ = _gather_copies(*refs)
    mine.start()
    for cp in first:
        cp.start()


def _gather_finish(*refs):
    mine, first, passed, recv_ici, recv_d2d = _gather_copies(*refs)
    for landed, onward in zip(recv_ici, passed):
        landed.wait_recv()
        onward.start()
    for landed in recv_d2d:
        landed.wait_recv()
    for cp in first + passed:
        cp.wait_send()
    mine.wait()


def _a2a_copies(g_ref, out_ref, send_sems, recv_sems, local_sem):
    mx, my, mc = lax.axis_index("x"), lax.axis_index("y"), lax.axis_index("c")
    me = 4 * mx + 2 * my + mc
    mine = pltpu.make_async_copy(g_ref.at[me], out_ref.at[me], local_sem)
    copies = []
    for k in range(1, N_DEV):
        px = 1 - mx if (k >> 2) & 1 else mx
        py = 1 - my if (k >> 1) & 1 else my
        pc = 1 - mc if k & 1 else mc
        copies.append(pltpu.make_async_remote_copy(
            src_ref=g_ref.at[4 * px + 2 * py + pc], dst_ref=out_ref.at[me],
            send_sem=send_sems.at[k - 1], recv_sem=recv_sems.at[k - 1],
            device_id=(px, py, pc), device_id_type=MESH))
    return mine, copies


def _a2a_start(*refs):
    mine, copies = _a2a_copies(*refs)
    mine.start()
    for cp in copies:
        cp.start()


def _a2a_finish(*refs):
    mine, copies = _a2a_copies(*refs)
    for cp in copies:
        cp.wait()
    mine.wait()


_EXCHANGES = {"gather": (_gather_start, _gather_finish), "a2a": (_a2a_start, _a2a_finish)}


def _exchange_shape(kind, src):
    return _sds((N_DEV,) + src.shape if kind == "gather" else src.shape, src.dtype)


def _all_gather_small(x, name):
    def body(x_ref, out_ref, send_sems, recv_sems, local_sem):
        mx, my, mc = lax.axis_index("x"), lax.axis_index("y"), lax.axis_index("c")
        me = 4 * mx + 2 * my + mc
        mine = pltpu.make_async_copy(x_ref, out_ref.at[me], local_sem)
        mine.start()
        copies = []
        for k in range(1, N_DEV):
            peer = (1 - mx if (k >> 2) & 1 else mx, 1 - my if (k >> 1) & 1 else my, 1 - mc if k & 1 else mc)
            copies.append(pltpu.make_async_remote_copy(
                src_ref=x_ref, dst_ref=out_ref.at[me], send_sem=send_sems.at[k - 1], recv_sem=recv_sems.at[k - 1],
                device_id=peer, device_id_type=MESH))
            copies[-1].start()
        for cp in copies:
            cp.wait()
        mine.wait()

    return _pcall(body, name=name, out_shape=_exchange_shape("gather", x), in_specs=[_ANY], out_specs=_ANY,
                  scratch_shapes=list(_SEMS))(x)


def _pcall_ride(body, ride, args, *, name, grid, in_specs, out_specs, out_shape, scratch_shapes=(), compiler_params=None):
    if ride is None:
        out = _pcall(body, name=name, grid=grid, in_specs=in_specs, out_specs=out_specs, out_shape=out_shape,
                     scratch_shapes=list(scratch_shapes), compiler_params=compiler_params)(*args)
        return out, None
    kind, src = ride
    start, finish = _EXCHANGES[kind]
    single = not isinstance(out_shape, (tuple, list))
    out_specs_t = (out_specs,) if single else tuple(out_specs)
    out_shape_t = (out_shape,) if single else tuple(out_shape)
    n_in, n_out, n_sc = len(in_specs), len(out_shape_t), len(scratch_shapes)

    def wrapped(*refs):
        ins, src_ref = refs[:n_in], refs[n_in]
        outs, dst_ref = refs[n_in + 1:n_in + 1 + n_out], refs[n_in + 1 + n_out]
        scratch = refs[n_in + 2 + n_out:n_in + 2 + n_out + n_sc]
        sems = refs[n_in + 2 + n_out + n_sc:]
        first = pl.program_id(0) == 0
        last = pl.program_id(0) == grid[0] - 1
        for ax in range(1, len(grid)):
            first = jnp.logical_and(first, pl.program_id(ax) == 0)
            last = jnp.logical_and(last, pl.program_id(ax) == grid[ax] - 1)

        @pl.when(first)
        def _():
            start(src_ref, dst_ref, *sems)

        body(*ins, *outs, *scratch)

        @pl.when(last)
        def _():
            finish(src_ref, dst_ref, *sems)

    res = _pcall(wrapped, name=name, grid=grid, in_specs=list(in_specs) + [_ANY], out_specs=out_specs_t + (_ANY,),
                 out_shape=out_shape_t + (_exchange_shape(kind, src),),
                 scratch_shapes=list(scratch_shapes) + list(_SEMS), compiler_params=compiler_params)(*args, src)
    return (res[0] if single else tuple(res[:-1])), res[-1]


SUB_ROWS = TM


class _Epilogue:
    def __init__(self, fn, args, in_specs, out_specs, out_shape, steps=None, lhs_map=None):
        self.fn, self.args, self.in_specs, self.out_specs, self.out_shape = fn, tuple(args), list(in_specs), out_specs, out_shape
        self.steps, self.lhs_map = steps, lhs_map


def _mm_nn(a, w, out_dtype, name, square_lhs=False, relu_out=False, ride=None, tm=TM, epi=None, vmem_mb=48):
    m, k = a.shape
    nb, _, bn = w.shape
    tm = min(tm, m)
    n_extra = 0 if epi is None else len(epi.args)

    def body(*refs):
        a_ref, w_ref = refs[:2]
        av = a_ref[...]
        if square_lhs:
            av = av * av
        if epi is not None:
            assert nb == 1
            for r0 in range(0, tm, SUB_ROWS):
                rows = slice(r0, r0 + SUB_ROWS)
                epi.fn(jnp.dot(av[rows], w_ref[0], preferred_element_type=F32), rows, *refs[2:])
            return
        for j in range(nb):
            r = jnp.dot(av, w_ref[j], preferred_element_type=F32)
            if relu_out:
                r = jnp.maximum(r, 0.0)
            refs[2][:, j * bn:(j + 1) * bn] = r.astype(out_dtype)

    in_specs = [pl.BlockSpec((tm, k), lambda i: (i, 0)), pl.BlockSpec((nb, k, bn), lambda i: (0, 0, 0))]
    if epi is None:
        args, out_specs, out_shape = (a, w), pl.BlockSpec((tm, nb * bn), lambda i: (i, 0)), _sds((m, nb * bn), out_dtype)
    else:
        args, out_specs, out_shape = (a, w) + epi.args, epi.out_specs, epi.out_shape
        in_specs += epi.in_specs
    out, ex = _pcall_ride(body, ride, args, name=name, grid=(m // tm,), in_specs=in_specs, out_specs=out_specs,
                          out_shape=out_shape, compiler_params=_cp(vmem_mb))
    return out if ride is None else (out, ex)


def _mm_nt(dy, w, out_dtype, name, relu_mul=None, ride=None, tm=TM, epi=None, vmem_mb=48):
    m = dy.shape[0]
    nb, k, bn = w.shape
    tm = min(tm, m)
    n_extra = (0 if relu_mul is None else 1) + (0 if epi is None else len(epi.args))

    def body(*refs):
        dy_ref, w_ref = refs[:2]
        extra, outs, wt = refs[2:2 + n_extra], refs[2 + n_extra:-1], refs[-1]

        @pl.when(pl.program_id(0) == 0)
        def _():
            for j in range(nb):
                wt[j * bn:(j + 1) * bn, :] = w_ref[j].T

        if epi is not None:
            for r0 in range(0, tm, SUB_ROWS):
                rows = slice(r0, r0 + SUB_ROWS)
                epi.fn(jnp.dot(dy_ref[rows, :], wt[...], preferred_element_type=F32), rows, *extra, *outs)
            return
        acc = jnp.dot(dy_ref[...], wt[...], preferred_element_type=F32)
        if relu_mul is not None:
            acc = acc * (2.0 * extra[0][...].astype(F32))
        outs[0][...] = acc.astype(out_dtype)

    lhs_map = (lambda i: (i, 0)) if epi is None or epi.lhs_map is None else epi.lhs_map
    in_specs = [pl.BlockSpec((tm, nb * bn), lhs_map), pl.BlockSpec((nb, k, bn), lambda i: (0, 0, 0))]
    args = [dy, w]
    if relu_mul is not None:
        in_specs.append(pl.BlockSpec((tm, k), lambda i: (i, 0)))
        args.append(relu_mul)
    steps = m // tm
    if epi is None:
        out_specs, out_shape = pl.BlockSpec((tm, k), lambda i: (i, 0)), _sds((m, k), out_dtype)
    else:
        args += list(epi.args)
        in_specs += epi.in_specs
        out_specs, out_shape = epi.out_specs, epi.out_shape
        steps = steps if epi.steps is None else epi.steps
    out, ex = _pcall_ride(
        body, ride, args, name=name, grid=(steps,), in_specs=in_specs, out_specs=out_specs, out_shape=out_shape,
        scratch_shapes=[pltpu.VMEM((nb * bn, k), BF16)], compiler_params=_cp(vmem_mb))
    return out if ride is None else (out, ex)


def _mm_tn(a, b, name, col_blocks, block, out_dtype, tm, square_lhs=False):
    m, k = a.shape
    nn = b.shape[1]
    steps = m // tm
    if col_blocks:
        nblk, acc_shape = nn // block, (k, block)
        a_spec = pl.BlockSpec((tm, k), lambda j, s: (s, 0))
        b_spec = pl.BlockSpec((tm, block), lambda j, s: (s, j))
    else:
        nblk, acc_shape = k // block, (block, nn)
        a_spec = pl.BlockSpec((tm, block), lambda j, s: (s, j))
        b_spec = pl.BlockSpec((tm, nn), lambda j, s: (s, 0))

    def body(a_ref, b_ref, o_ref, acc):
        s = pl.program_id(1)

        @pl.when(s == 0)
        def _():
            acc[...] = jnp.zeros_like(acc)

        av = a_ref[...]
        if square_lhs:
            av = av.astype(F32)
            av = (av * av).astype(BF16)
        acc[...] += _dot_tn(av, b_ref[...])

        @pl.when(s == steps - 1)
        def _():
            o_ref[...] = acc[...].astype(out_dtype)

    return _pcall(
        body, name=name, grid=(nblk, steps), in_specs=[a_spec, b_spec],
        out_specs=pl.BlockSpec((None,) + acc_shape, lambda j, s: (j, 0, 0)),
        out_shape=_sds((nblk,) + acc_shape, out_dtype),
        scratch_shapes=[pltpu.VMEM(acc_shape, F32)], compiler_params=_cp(),
    )(a, b)


def _mm_wgrad(at, b, name, bn, out_dtype, transpose_out=False, square_rhs=False, halves=False, ride=None):
    k, m = at.shape
    nblk = b.shape[1] // bn
    rows, cols = (bn, k) if transpose_out else (k, bn)
    nout = 2 if halves else 1
    per = rows // nout

    def body(a_ref, b_ref, *o_refs):
        bv = b_ref[...]
        if square_rhs:
            bv = bv * bv
        r = jnp.dot(a_ref[...], bv, preferred_element_type=F32)
        r = (r.T if transpose_out else r).astype(out_dtype)
        for i, o_ref in enumerate(o_refs):
            o_ref[...] = r[i * per:(i + 1) * per, :]

    out, ex = _pcall_ride(
        body, ride, (at, b), name=name, grid=(nblk,),
        in_specs=[pl.BlockSpec((k, m), lambda j: (0, 0)), pl.BlockSpec((m, bn), lambda j: (0, j))],
        out_specs=tuple(pl.BlockSpec((None, per, cols), lambda j: (j, 0, 0)) for _ in range(nout)),
        out_shape=tuple(_sds((nblk, per, cols), out_dtype) for _ in range(nout)), compiler_params=_cp())
    out = out if halves else out[0]
    return out if ride is None else (out, ex)


def _mm_in(hn, w, cosf, sins, ride):
    m, k = hn.shape
    nb, _, bn = w.shape

    def body(a_ref, w_ref, c_ref, s_ref, qkv_ref, rest_ref, pt):
        av = a_ref[...]
        for j in range(nb):
            pt[:, j * bn:(j + 1) * bn] = jnp.dot(av, w_ref[j], preferred_element_type=F32)
        cf, ss = c_ref[...], s_ref[...]
        for h in range(HEADS):
            sq = slice(h * HEAD_DIM, (h + 1) * HEAD_DIM)
            sk = slice(RET_W + h * HEAD_DIM, RET_W + (h + 1) * HEAD_DIM)
            qkv_ref[:, sq] = _rot(pt[:, sq], cf, ss).astype(BF16)
            qkv_ref[:, sk] = (_rot(pt[:, sk], cf, ss) * K_SCALE).astype(BF16)
        qkv_ref[:, 2 * RET_W:] = pt[:, 2 * RET_W:3 * RET_W].astype(BF16)
        rest_ref[...] = pt[:, 3 * RET_W:]

    tab = pl.BlockSpec((TM, HEAD_DIM), lambda i: (i, 0))
    wide = pl.BlockSpec((TM, 3 * RET_W), lambda i: (i, 0))
    return _pcall_ride(
        body, ride, (hn, w, cosf, sins), name="mm_in", grid=(m // TM,),
        in_specs=[pl.BlockSpec((TM, k), lambda i: (i, 0)), pl.BlockSpec((nb, k, bn), lambda i: (0, 0, 0)), tab, tab],
        out_specs=(wide, wide), out_shape=(_sds((m, 3 * RET_W), BF16), _sds((m, nb * bn - 3 * RET_W), F32)),
        scratch_shapes=[pltpu.VMEM((TM, nb * bn), F32)], compiler_params=_cp())


def _mod_part(s16, w_ada, b_cols):
    def body(s_ref, w_ref, b_ref, o_ref):
        o_ref[...] = _dot(s_ref[...], w_ref[...]) + b_ref[...]

    return _pcall(body, name="mod_part", out_shape=_sds((s16.shape[0], w_ada.shape[1]), F32),
                  compiler_params=_cp())(s16, w_ada, b_cols)


def _ada_bwd(s16, dm_cols, w_ada):
    def body(s_ref, d_ref, w_ref, gw_ref, ds_ref):
        gw_ref[...] = _dot_tn(s_ref[...], d_ref[...])
        ds_ref[...] = _dot_nt(d_ref[...], w_ref[...])

    return _pcall(body, name="ada_bwd",
                  out_shape=(_sds(w_ada.shape, F32), _sds(s16.shape, F32)), compiler_params=_cp())(s16, dm_cols, w_ada)


def _norm1_fwd(ctx, x, g, modrows, cb, ride=None):
    l_len, d = ctx.shape
    nb = (l_len + x.shape[0]) // TM

    def body(ctx_ref, x_ref, g_ref, m_ref, o_ref, ot_ref):
        is_ctx = pl.program_id(0) < cb
        xin = jnp.where(is_ctx, ctx_ref[...], x_ref[...])
        sh = jnp.where(is_ctx, m_ref[R_CSH1:R_CSH1 + 1, :], m_ref[R_SH1:R_SH1 + 1, :])
        sc = jnp.where(is_ctx, m_ref[R_CSC1:R_CSC1 + 1, :], m_ref[R_SC1:R_SC1 + 1, :])
        ms = jnp.mean(xin * xin, axis=-1, keepdims=True)
        n = xin * lax.rsqrt(ms + EPS) * g_ref[...]
        hn = n * (1.0 + sc) + sh
        o_ref[...] = hn.astype(BF16)
        ot_ref[...] = hn.T.astype(BF16)

    return _pcall_ride(
        body, ride, (ctx, x, g, modrows), name="norm1_fwd", grid=(nb,),
        in_specs=[pl.BlockSpec((TM, d), lambda i: (jnp.minimum(i, cb - 1), 0)),
                  pl.BlockSpec((TM, d), lambda i: (jnp.maximum(i - cb, 0), 0)),
                  pl.BlockSpec((1, d), lambda i: (0, 0)), pl.BlockSpec((8, d), lambda i: (0, 0))],
        out_specs=(pl.BlockSpec((TM, d), lambda i: (i, 0)), pl.BlockSpec((d, TM), lambda i: (0, i))),
        out_shape=(_sds((nb * TM, d), BF16), _sds((d, nb * TM), BF16)), compiler_params=_cp())


def _mix_fwd(o_f, o_b, p, h_f, h_b, cb, t_len):
    def body(of_ref, ob_ref, g_ref, gate_ref, hf_ref, hb_ref, mix_ref):
        o = of_ref[...] + ob_ref[...]
        g = g_ref[...]
        sg = g * _sigmoid(g)
        for hh in range(HEADS):
            sl = slice(hh * HEAD_DIM, (hh + 1) * HEAD_DIM)
            oh = o[:, sl]
            yc = oh - jnp.mean(oh, axis=-1, keepdims=True)
            var = jnp.mean(yc * yc, axis=-1, keepdims=True)
            mix_ref[:, sl] = (sg[:, sl] * (yc * lax.rsqrt(var + EPS))).astype(BF16)
        mix_ref[:, RET_W:] = ((hf_ref[...] + hb_ref[...]) * _gelu(gate_ref[...])).astype(BF16)

    row = lambda i: (i + cb, 0)
    return _pcall(
        body, name="mix_fwd", grid=(t_len // TM,),
        in_specs=[pl.BlockSpec((TM, RET_W), row), pl.BlockSpec((TM, RET_W), row),
                  pl.BlockSpec((TM, RET_W), lambda i: (i + cb, COL_G)), pl.BlockSpec((TM, LRU_W), lambda i: (i + cb, COL_GATE)),
                  pl.BlockSpec((TM, LRU_W), row), pl.BlockSpec((TM, LRU_W), row)],
        out_specs=pl.BlockSpec((TM, RET_W + LRU_W), lambda i: (i, 0)),
        out_shape=_sds((t_len, RET_W + LRU_W), BF16), compiler_params=_cp(),
    )(o_f, o_b, p, p, h_f, h_b)


def _res_norm2_epilogue(x, g, modrows, tm):
    t_len, d = x.shape
    tm = min(tm, t_len)

    def fn(y, rows, x_ref, g_ref, m_ref, y_ref, x1_ref, h2_ref, h2t_ref):
        y_ref[rows, :] = y
        x1 = x_ref[rows, :] + m_ref[R_G1:R_G1 + 1, :] * y
        ms = jnp.mean(x1 * x1, axis=-1, keepdims=True)
        n = x1 * lax.rsqrt(ms + EPS) * g_ref[...]
        x1_ref[rows, :] = x1
        h2 = n * (1.0 + m_ref[R_SC2:R_SC2 + 1, :]) + m_ref[R_SH2:R_SH2 + 1, :]
        h2_ref[rows, :] = h2.astype(BF16)
        h2t_ref[:, rows] = h2.T.astype(BF16)

    t = pl.BlockSpec((tm, d), lambda i: (i, 0))
    return _Epilogue(
        fn, (x, g, modrows),
        in_specs=[t, pl.BlockSpec((1, d), lambda i: (0, 0)), pl.BlockSpec((8, d), lambda i: (0, 0))],
        out_specs=(t, t, t, pl.BlockSpec((d, tm), lambda i: (0, i))),
        out_shape=(_sds((t_len, d), F32), _sds((t_len, d), F32), _sds((t_len, d), BF16), _sds((d, t_len), BF16)))


def _zero_at_start(acc_ref, rows):
    if rows.start == 0:
        @pl.when(pl.program_id(0) == 0)
        def _():
            acc_ref[...] = jnp.zeros_like(acc_ref)


def _final_epilogue(x1, target, fg, modrows, tm):
    t_len, d = x1.shape

    def fn(z, rows, x1_ref, t_ref, fg_ref, m_ref, dx2_ref, dz_ref, dzt_ref, acc_ref):
        _zero_at_start(acc_ref, rows)
        g2 = m_ref[R_G2:R_G2 + 1, :]
        x2 = x1_ref[rows, :] + g2 * z
        rstd = lax.rsqrt(jnp.mean(x2 * x2, axis=-1, keepdims=True) + EPS)
        xh = x2 * rstd
        fg = fg_ref[...]
        e = xh * fg - t_ref[rows, :]
        dy = e * (1.0 / d)
        dxh = dy * fg
        dx2 = rstd * (dxh - xh * jnp.mean(dxh * xh, axis=-1, keepdims=True))
        dx2_ref[rows, :] = dx2
        dz = g2 * dx2
        dz_ref[rows, :] = dz.astype(BF16)
        dzt_ref[:, rows] = dz.T.astype(BF16)
        acc_ref[0:1, :] += jnp.sum(dy * xh, axis=0, keepdims=True)
        acc_ref[1:2, :] += jnp.sum(dx2 * z, axis=0, keepdims=True)
        acc_ref[2:3, :] += jnp.sum(e * e, axis=0, keepdims=True)

    t = pl.BlockSpec((tm, d), lambda i: (i, 0))
    return _Epilogue(
        fn, (x1, target, fg, modrows),
        in_specs=[t, t, pl.BlockSpec((1, d), lambda i: (0, 0)), pl.BlockSpec((8, d), lambda i: (0, 0))],
        out_specs=(t, t, pl.BlockSpec((d, tm), lambda i: (0, i)), pl.BlockSpec((8, d), lambda i: (0, 0))),
        out_shape=(_sds((t_len, d), F32), _sds((t_len, d), BF16), _sds((d, t_len), BF16), _sds((8, d), F32)))


def _bwd_norm2_epilogue(x1, dx2, y, g, modrows, tm):
    t_len, d = x1.shape

    def fn(dh2, rows, x1_ref, dx2_ref, y_ref, g_ref, m_ref, dx1_ref, dy_ref, acc_ref):
        _zero_at_start(acc_ref, rows)
        x1 = x1_ref[rows, :]
        rstd = lax.rsqrt(jnp.mean(x1 * x1, axis=-1, keepdims=True) + EPS)
        xh = x1 * rstd
        gn = g_ref[...]
        dn = dh2 * (1.0 + m_ref[R_SC2:R_SC2 + 1, :])
        dxh = dn * gn
        dx1 = dx2_ref[rows, :] + rstd * (dxh - xh * jnp.mean(dxh * xh, axis=-1, keepdims=True))
        dx1_ref[rows, :] = dx1
        dy_ref[rows, :] = (m_ref[R_G1:R_G1 + 1, :] * dx1).astype(BF16)
        acc_ref[0:1, :] += jnp.sum(dh2, axis=0, keepdims=True)
        acc_ref[1:2, :] += jnp.sum(dh2 * xh * gn, axis=0, keepdims=True)
        acc_ref[2:3, :] += jnp.sum(dn * xh, axis=0, keepdims=True)
        acc_ref[3:4, :] += jnp.sum(dx1 * y_ref[rows, :], axis=0, keepdims=True)

    t = pl.BlockSpec((tm, d), lambda i: (i, 0))
    return _Epilogue(
        fn, (x1, dx2, y, g, modrows),
        in_specs=[t, t, t, pl.BlockSpec((1, d), lambda i: (0, 0)), pl.BlockSpec((8, d), lambda i: (0, 0))],
        out_specs=(t, t, pl.BlockSpec((8, d), lambda i: (0, 0))),
        out_shape=(_sds((t_len, d), F32), _sds((t_len, d), BF16), _sds((8, d), F32)))


def _mix_bwd_epilogue(o_f, o_b, p, h_f, h_b, cb):
    n = o_f.shape[0]

    def fn(dm, rows, of_ref, ob_ref, g_ref, gate_ref, hf_ref, hb_ref, do_ref, dg_ref, dgate_ref, dhs_ref):
        is_ctx = pl.program_id(0) < cb

        @pl.when(is_ctx)
        def _():
            for r in (do_ref, dg_ref, dgate_ref, dhs_ref):
                r[rows, :] = jnp.zeros((rows.stop - rows.start, r.shape[1]), r.dtype)

        @pl.when(jnp.logical_not(is_ctx))
        def _():
            o = of_ref[rows, :] + ob_ref[rows, :]
            g = g_ref[rows, :]
            s = _sigmoid(g)
            sg = g * s
            dsg = s * (1.0 + g * (1.0 - s))
            for hh in range(HEADS):
                sl = slice(hh * HEAD_DIM, (hh + 1) * HEAD_DIM)
                oh = o[:, sl]
                yc = oh - jnp.mean(oh, axis=-1, keepdims=True)
                rs = lax.rsqrt(jnp.mean(yc * yc, axis=-1, keepdims=True) + EPS)
                gn = yc * rs
                dret = dm[:, sl]
                dgn = dret * sg[:, sl]
                dg_ref[rows, sl] = (dret * gn * dsg[:, sl]).astype(BF16)
                do_ref[rows, sl] = (rs * (dgn - jnp.mean(dgn, axis=-1, keepdims=True)
                                          - gn * jnp.mean(dgn * gn, axis=-1, keepdims=True))).astype(BF16)
            dlru = dm[:, RET_W:]
            gate = gate_ref[rows, :]
            dhs_ref[rows, :] = dlru * _gelu(gate)
            dgate_ref[rows, :] = (dlru * (hf_ref[rows, :] + hb_ref[rows, :]) * _dgelu(gate)).astype(BF16)

    t = pl.BlockSpec((TM, RET_W), lambda i: (i, 0))
    return _Epilogue(
        fn, (o_f, o_b, p, p, h_f, h_b),
        in_specs=[t, t, pl.BlockSpec((TM, RET_W), lambda i: (i, COL_G)), pl.BlockSpec((TM, LRU_W), lambda i: (i, COL_GATE)), t, t],
        out_specs=(t, t, t, t), out_shape=(_sds((n, RET_W), BF16),) * 3 + (_sds((n, RET_W), F32),),
        steps=n // TM, lhs_map=lambda i: (jnp.maximum(i - cb, 0), 0))


def _bwd_norm1_epilogue(ctx, x, dx1, g, modrows, cb):
    t_len, d = x.shape

    def fn(dh, rows, ctx_ref, x_ref, dx1_ref, g_ref, m_ref, gx_ref, acc_ref):
        is_ctx = pl.program_id(0) < cb
        _zero_at_start(acc_ref, rows)
        xin = jnp.where(is_ctx, ctx_ref[rows, :], x_ref[rows, :])
        sc = jnp.where(is_ctx, m_ref[R_CSC1:R_CSC1 + 1, :], m_ref[R_SC1:R_SC1 + 1, :])
        rstd = lax.rsqrt(jnp.mean(xin * xin, axis=-1, keepdims=True) + EPS)
        xh = xin * rstd
        gn = g_ref[...]
        dn = dh * (1.0 + sc)
        dxh = dn * gn
        gx_ref[rows, :] = dx1_ref[rows, :] + rstd * (dxh - xh * jnp.mean(dxh * xh, axis=-1, keepdims=True))
        s0 = jnp.sum(dh, axis=0, keepdims=True)
        s1 = jnp.sum(dh * xh * gn, axis=0, keepdims=True)
        acc_ref[4:5, :] += jnp.sum(dn * xh, axis=0, keepdims=True)

        @pl.when(is_ctx)
        def _():
            acc_ref[0:1, :] += s0
            acc_ref[1:2, :] += s1

        @pl.when(jnp.logical_not(is_ctx))
        def _():
            acc_ref[2:3, :] += s0
            acc_ref[3:4, :] += s1

    lat = pl.BlockSpec((TM, d), lambda i: (jnp.maximum(i - cb, 0), 0))
    return _Epilogue(
        fn, (ctx, x, dx1, g, modrows),
        in_specs=[pl.BlockSpec((TM, d), lambda i: (jnp.minimum(i, cb - 1), 0)),
                  lat, lat, pl.BlockSpec((1, d), lambda i: (0, 0)), pl.BlockSpec((8, d), lambda i: (0, 0))],
        out_specs=(lat, pl.BlockSpec((8, d), lambda i: (0, 0))),
        out_shape=(_sds((t_len, d), F32), _sds((8, d), F32)))


def _rot(x, cf, ss):
    return x * cf + pltpu.roll(x, HEAD_DIM // 2, 1) * ss


def _decay_exponents(dirn):
    ii = lax.broadcasted_iota(jnp.int32, (CHUNK, CHUNK), 0)
    jj = lax.broadcasted_iota(jnp.int32, (CHUNK, CHUNK), 1)
    rel = ii - jj if dirn == 0 else jj - ii
    pos = ii.astype(F32)
    if dirn == 0:
        cq, cs = pos + 1.0, (CHUNK - 1.0) - pos
    else:
        cq, cs = CHUNK - pos, pos
    return rel, jnp.maximum(rel, 0).astype(F32), cq, cs


def _store_decay(lg_ref, dec, relf_ref=None):
    for dirn in (0, 1):
        rel, relf, cq, cs = _decay_exponents(dirn)
        if relf_ref is not None:
            relf_ref[dirn] = relf
        for h in range(HEADS):
            lgv = lg_ref[dirn, h]
            wq, ws = jnp.exp(lgv * cq), jnp.exp(lgv * cs)
            dec[dirn, h, 0] = jnp.where(rel >= 0, jnp.exp(lgv * relf), 0.0)
            dec[dirn, h, 1] = wq
            dec[dirn, h, 2] = ws
            if relf_ref is not None:
                dec[dirn, h, 3] = wq * cq
                dec[dirn, h, 4] = ws * cs


def _ret_rows(cc, nc, step_of):
    return [lambda s, dirn=dirn: _tile_order(dirn, step_of(s), cc, nc) for dirn in (0, 1)]


def _ret_in_specs(rows):
    specs = []
    for row in rows:
        specs += [pl.BlockSpec((CHUNK, RET_W), lambda s, o=o, row=row: (row(s), o)) for o in (0, 1, 2)]
    return specs


def _ret_fwd(qkv, lg, cc, ride=None):
    n = qkv.shape[0]
    nc = n // CHUNK
    rows = _ret_rows(cc, nc, lambda s: s)

    def body(lg_ref, q0, k0, v0, q1, k1, v1, o0, o1, sp0, sp1, st, dec):
        @pl.when(pl.program_id(0) == 0)
        def _():
            st[...] = jnp.zeros_like(st)
            _store_decay(lg_ref, dec)

        refs = ((q0, k0, v0, o0, sp0), (q1, k1, v1, o1, sp1))
        chains = [(dirn, h, slice(h * HEAD_DIM, (h + 1) * HEAD_DIM)) for dirn in (0, 1) for h in range(HEADS)]
        scores, cross, update = [], [], []
        for dirn, h, sl in chains:
            q_ref, k_ref, v_ref, _, sp_ref = refs[dirn]
            q, k, v = q_ref[:, sl], k_ref[:, sl], v_ref[:, sl]
            sp = st[dirn, h]
            sp_ref[h] = sp
            scores.append(_dot_nt(q, k))
            cross.append(_dot(q * dec[dirn, h, 1], sp))
            update.append(_dot_tn(k * dec[dirn, h, 2], v))
        masked = [(a * dec[dirn, h, 0]).astype(BF16) for a, (dirn, h, _) in zip(scores, chains)]
        intra = [_dot(sc, refs[dirn][2][:, sl]) for sc, (dirn, h, sl) in zip(masked, chains)]
        for (dirn, h, sl), o_in, o_cr, upd in zip(chains, intra, cross, update):
            refs[dirn][3][:, sl] = o_in + o_cr
            st[dirn, h] = jnp.exp(lg_ref[dirn, h] * CHUNK) * st[dirn, h] + upd

    o_specs = [pl.BlockSpec((CHUNK, RET_W), lambda s, row=row: (row(s), 0)) for row in rows]
    state = pl.BlockSpec((None, HEADS, CHUNK, HEAD_DIM), lambda s: (s, 0, 0, 0))
    return _pcall_ride(
        body, ride, (lg,) + (qkv,) * 6, name="ret_fwd", grid=(nc,),
        in_specs=[pl.BlockSpec(memory_space=pltpu.SMEM)] + _ret_in_specs(rows),
        out_specs=(o_specs[0], o_specs[1], state, state),
        out_shape=(_sds((n, RET_W), F32),) * 2 + (_sds((nc, HEADS, CHUNK, HEAD_DIM), F32),) * 2,
        scratch_shapes=[pltpu.VMEM((2, HEADS, CHUNK, HEAD_DIM), F32), pltpu.VMEM((2, HEADS, 3, CHUNK, CHUNK), F32)],
        compiler_params=_cp())


def _ret_bwd(qkv, lg, do, s_prev, cc, ride=None):
    n = qkv.shape[0]
    nc = n // CHUNK
    rows = _ret_rows(cc, nc, lambda s: nc - 1 - s)

    def body(lg_ref, q0, k0, v0, q1, k1, v1, do0, do1, sp0, sp1, dq0, dk0, dv0, dq1, dk1, dv1, dlg_ref, dst, dec, relf):
        @pl.when(pl.program_id(0) == 0)
        def _():
            dst[...] = jnp.zeros_like(dst)
            dlg_ref[...] = jnp.zeros_like(dlg_ref)
            _store_decay(lg_ref, dec, relf)

        refs = ((q0, k0, v0, do0, sp0, dq0, dk0, dv0), (q1, k1, v1, do1, sp1, dq1, dk1, dv1))
        chains = [(dirn, h, slice(h * HEAD_DIM, (h + 1) * HEAD_DIM)) for dirn in (0, 1) for h in range(HEADS)]

        def tiles(dirn, sl):
            q_ref, k_ref, v_ref, do_ref = refs[dirn][:4]
            return q_ref[:, sl], k_ref[:, sl], v_ref[:, sl], do_ref[:, sl]

        a_s, g1_s, da_s, h1_s = [], [], [], []
        for dirn, h, sl in chains:
            q, k, v, dov = tiles(dirn, sl)
            a_s.append(_dot_nt(q, k))
            g1_s.append(_dot_nt(dov, refs[dirn][4][h]))
            da_s.append(_dot_nt(dov, v))
            h1_s.append(_dot_nt(v, dst[dirn, h]))
        da_s = [da * dec[dirn, h, 0] for da, (dirn, h, _) in zip(da_s, chains)]
        dq_s, dk_s, dv_s, ds_s = [], [], [], []
        for (dirn, h, sl), a, da in zip(chains, a_s, da_s):
            q, k, v, dov = tiles(dirn, sl)
            dq_s.append(_dot(da, k))
            dk_s.append(_dot_tn(da, q))
            dv_s.append(_dot_tn(a * dec[dirn, h, 0], dov) + _dot(k * dec[dirn, h, 2], dst[dirn, h]))
            ds_s.append(_dot_tn(q * dec[dirn, h, 1], dov))
        for (dirn, h, sl), a, g1, da, h1, dq2, dk2, dv, ds2 in zip(chains, a_s, g1_s, da_s, h1_s, dq_s, dk_s, dv_s, ds_s):
            q, k, _, _ = tiles(dirn, sl)
            dq_ref, dk_ref, dv_ref = refs[dirn][5:]
            sp, dsn = refs[dirn][4][h], dst[dirn, h]
            gc = jnp.exp(lg_ref[dirn, h] * CHUNK)
            dq_ref[:, sl] = (g1 * dec[dirn, h, 1] + dq2).astype(BF16)
            dk_ref[:, sl] = (dk2 + h1 * dec[dirn, h, 2]).astype(BF16)
            dv_ref[:, sl] = dv.astype(BF16)
            term = da * a * relf[dirn] + q * g1 * dec[dirn, h, 3] + k * h1 * dec[dirn, h, 4] + sp * dsn * (CHUNK * gc)
            dlg_ref[dirn * HEADS + h:dirn * HEADS + h + 1, 0:HEAD_DIM] += jnp.sum(term, axis=0, keepdims=True)
            dst[dirn, h] = gc * dsn + ds2

    wide = [pl.BlockSpec((CHUNK, RET_W), lambda s, row=row: (row(s), 0)) for row in rows]
    state = pl.BlockSpec((None, HEADS, CHUNK, HEAD_DIM), lambda s: (nc - 1 - s, 0, 0, 0))
    return _pcall_ride(
        body, ride, (lg,) + (qkv,) * 6 + (do, do, s_prev[0], s_prev[1]), name="ret_bwd", grid=(nc,),
        in_specs=[pl.BlockSpec(memory_space=pltpu.SMEM)] + _ret_in_specs(rows) + wide + [state, state],
        out_specs=(wide[0],) * 3 + (wide[1],) * 3 + (pl.BlockSpec((2 * HEADS, 8 * HEAD_DIM), lambda s: (0, 0)),),
        out_shape=(_sds((n, RET_W), BF16),) * 6 + (_sds((2 * HEADS, 8 * HEAD_DIM), F32),),
        scratch_shapes=[pltpu.VMEM((2, HEADS, CHUNK, HEAD_DIM), F32), pltpu.VMEM((2, HEADS, 5, CHUNK, CHUNK), F32),
                        pltpu.VMEM((2, CHUNK, CHUNK), F32)],
        compiler_params=_cp())


def _shift_rows(cur, prev8, next8, k, seg_start, seg_end):
    tm = cur.shape[0]
    rows = _rows_iota(cur.shape)
    if k < 0:
        out = pltpu.roll(cur, -k, 0)
        for j in range(-k):
            halo = jnp.where(seg_start, 0.0, prev8[SUB + k + j:SUB + k + j + 1, :])
            out = jnp.where(rows == j, halo, out)
    else:
        out = pltpu.roll(cur, tm - k, 0)
        for j in range(k):
            halo = jnp.where(seg_end, 0.0, next8[j:j + 1, :])
            out = jnp.where(rows == tm - k + j, halo, out)
    return out


def _seg_flags(t, cb, nb):
    return jnp.logical_or(t == 0, t == cb), jnp.logical_or(t == cb - 1, t == nb - 1)


def _halo_specs(tile_of, n_rows, col):
    per = TM // SUB
    return [pl.BlockSpec((TM, LRU_W), lambda s: (tile_of(s), col)),
            pl.BlockSpec((SUB, LRU_W), lambda s: (jnp.maximum(tile_of(s) * per - 1, 0), col)),
            pl.BlockSpec((SUB, LRU_W), lambda s: (jnp.minimum((tile_of(s) + 1) * per, n_rows // SUB - 1), col))]


def _lru_gates(xr, prev8, next8, seg_start, seg_end, cw_ref, cb_ref, wg_ref, bg_ref, sp_ref):
    xm1 = _shift_rows(xr, prev8, next8, -1, seg_start, seg_end)
    xp1 = _shift_rows(xr, prev8, next8, 1, seg_start, seg_end)
    xp2 = _shift_rows(xr, prev8, next8, 2, seg_start, seg_end)
    xc = cb_ref[...] + xm1 * cw_ref[0:1, :] + xr * cw_ref[1:2, :] + xp1 * cw_ref[2:3, :] + xp2 * cw_ref[3:4, :]
    pre = _dot(xc, wg_ref[...]) + bg_ref[...]
    r = _sigmoid(pre[:, :LRU_W])
    i = _sigmoid(pre[:, LRU_W:])
    la = (-LRU_C) * r * sp_ref[...]
    a = jnp.exp(la)
    th = jnp.tanh(la)
    sq = jnp.sqrt(-2.0 * th / (1.0 - th))
    return xc, r, i, a, sq


def _scan_tile(a, b, ascending, a_sc, b_sc, carry, out_ref):
    tm, w = a.shape
    nsub = tm // SUB
    a = a.reshape(nsub, SUB, w)
    b = b.reshape(nsub, SUB, w)
    r8 = lax.broadcasted_iota(jnp.int32, a.shape, 1)
    for k in (1, 2, 4):
        if ascending:
            m = r8 >= k
            a_s, b_s = pltpu.roll(a, k, 1), pltpu.roll(b, k, 1)
        else:
            m = r8 < SUB - k
            a_s, b_s = pltpu.roll(a, SUB - k, 1), pltpu.roll(b, SUB - k, 1)
        b = a * jnp.where(m, b_s, 0.0) + b
        a = a * jnp.where(m, a_s, 1.0)
    a_sc[...] = a.reshape(tm, w)
    b_sc[...] = b.reshape(tm, w)

    def step(j, c):
        off = pl.multiple_of((j if ascending else nsub - 1 - j) * SUB, SUB)
        hb = a_sc[pl.ds(off, SUB), :] * c + b_sc[pl.ds(off, SUB), :]
        out_ref[pl.ds(off, SUB), :] = hb
        last = hb[SUB - 1:SUB, :] if ascending else hb[0:1, :]
        return jnp.broadcast_to(last, c.shape)

    carry[...] = lax.fori_loop(0, nsub, step, carry[...], unroll=8)


def _lru_fwd(p, wg, bg, sp, cw, cbias, dirn, cb, ride=None):
    n = p.shape[0]
    nb = n // TM
    tile_of = lambda s: _tile_order(dirn, s, cb, nb)

    def body(x_ref, xp_ref, xn_ref, wg_ref, bg_ref, sp_ref, cw_ref, cb_ref, h_ref, cin_ref, carry, a_sc, b_sc):
        s = pl.program_id(0)

        @pl.when(s == 0)
        def _():
            carry[...] = jnp.zeros_like(carry)

        seg_start, seg_end = _seg_flags(tile_of(s), cb, nb)
        xc, r, i, a, sq = _lru_gates(x_ref[...], xp_ref[...], xn_ref[...], seg_start, seg_end,
                                     cw_ref, cb_ref, wg_ref, bg_ref, sp_ref)
        cin_ref[...] = carry[...]
        _scan_tile(a, sq * (i * xc), dirn == 0, a_sc, b_sc, carry, h_ref)

    full = lambda shape: pl.BlockSpec(shape, lambda s: (0,) * len(shape))
    return _pcall_ride(
        body, ride, (p, p, p, wg, bg, sp, cw, cbias), name=f"lru_fwd{dirn}", grid=(nb,),
        in_specs=_halo_specs(tile_of, n, COL_XR) + [full((LRU_W, 2 * LRU_W)), full((1, 2 * LRU_W)), full((1, LRU_W)),
                                               full((4, LRU_W)), full((1, LRU_W))],
        out_specs=(pl.BlockSpec((TM, LRU_W), lambda s: (tile_of(s), 0)),
                   pl.BlockSpec((None, SUB, LRU_W), lambda s: (tile_of(s), 0, 0))),
        out_shape=(_sds((n, LRU_W), F32), _sds((nb, SUB, LRU_W), F32)),
        scratch_shapes=[pltpu.VMEM((SUB, LRU_W), F32), pltpu.VMEM((TM, LRU_W), F32), pltpu.VMEM((TM, LRU_W), F32)],
        compiler_params=_cp())


def _lru_bwd(p, wg, bg, sp, cw, cbias, h, cin, dhs, dirn, cb, ride=None):
    n = p.shape[0]
    nb = n // TM
    tile_of = lambda s: _tile_order(dirn, nb - 1 - s, cb, nb)

    def body(x_ref, xp_ref, xn_ref, wg_ref, bg_ref, sp_ref, cw_ref, cb_ref, h_ref, cin_ref, dhs_ref,
             dxc_ref, dwd_ref, acc_ref, carry, a_sc, b_sc, mu_sc, dwg_ref):
        s = pl.program_id(0)

        @pl.when(s == 0)
        def _():
            carry[...] = jnp.zeros_like(carry)
            dwg_ref[...] = jnp.zeros_like(dwg_ref)
            acc_ref[...] = jnp.zeros_like(acc_ref)

        seg_start, seg_end = _seg_flags(tile_of(s), cb, nb)
        xc, r, i, a, sq = _lru_gates(x_ref[...], xp_ref[...], xn_ref[...], seg_start, seg_end,
                                     cw_ref, cb_ref, wg_ref, bg_ref, sp_ref)
        rows = _rows_iota(a.shape)
        hv = h_ref[...]
        dh = dhs_ref[...]
        mu_next = carry[0:1, :]
        _scan_tile(a, a * dh, dirn == 1, a_sc, b_sc, carry, mu_sc)
        mu = mu_sc[...]
        if dirn == 0:
            hprev = jnp.where(rows == 0, cin_ref[0:1, :], pltpu.roll(hv, 1, 0))
            lam = dh + jnp.where(rows == TM - 1, mu_next, pltpu.roll(mu, TM - 1, 0))
        else:
            hprev = jnp.where(rows == TM - 1, cin_ref[0:1, :], pltpu.roll(hv, TM - 1, 0))
            lam = dh + jnp.where(rows == 0, mu_next, pltpu.roll(mu, 1, 0))
        ds = lam * (i * xc)
        di = lam * (sq * xc)
        dla = lam * hprev * a - ds * (a * a) / jnp.maximum(sq, 1e-20)
        dpr = dla * ((-LRU_C) * sp_ref[...]) * r * (1.0 - r)
        dpi = di * i * (1.0 - i)
        dpre = jnp.concatenate([dpr, dpi], axis=1)
        dxc_ref[...] = lam * (sq * i) + _dot_nt(dpre, wg_ref[...])
        dwg_ref[...] += _dot_tn(xc, dpre)
        acc_ref[0:1, :] += jnp.sum(dpre, axis=0, keepdims=True)
        acc_ref[1:2, 0:LRU_W] += jnp.sum(dla * ((-LRU_C) * r), axis=0, keepdims=True)

        @pl.when(s == nb - 1)
        def _():
            low = lax.broadcasted_iota(jnp.int32, (LRU_BD, 2 * LRU_BD), 1) < LRU_BD
            for half in (0, LRU_W):
                for m in range(LRU_BLOCKS // 2):
                    lanes = slice(half + 2 * LRU_BD * m, half + 2 * LRU_BD * (m + 1))
                    even = dwg_ref[2 * m * LRU_BD:(2 * m + 1) * LRU_BD, lanes]
                    odd = dwg_ref[(2 * m + 1) * LRU_BD:(2 * m + 2) * LRU_BD, lanes]
                    dwd_ref[:, lanes] = jnp.where(low, even, odd)

    full = lambda shape: pl.BlockSpec(shape, lambda s: (0,) * len(shape))
    tile = pl.BlockSpec((TM, LRU_W), lambda s: (tile_of(s), 0))
    return _pcall_ride(
        body, ride, (p, p, p, wg, bg, sp, cw, cbias, h, cin, dhs), name=f"lru_bwd{dirn}", grid=(nb,),
        in_specs=_halo_specs(tile_of, n, COL_XR) + [full((LRU_W, 2 * LRU_W)), full((1, 2 * LRU_W)), full((1, LRU_W)),
                                               full((4, LRU_W)), full((1, LRU_W)), tile,
                                               pl.BlockSpec((None, SUB, LRU_W), lambda s: (tile_of(s), 0, 0)), tile],
        out_specs=(tile, full((LRU_BD, 2 * LRU_W)), full((8, 2 * LRU_W))),
        out_shape=(_sds((n, LRU_W), F32), _sds((LRU_BD, 2 * LRU_W), F32), _sds((8, 2 * LRU_W), F32)),
        scratch_shapes=[pltpu.VMEM((SUB, LRU_W), F32)] + [pltpu.VMEM((TM, LRU_W), F32)] * 3
        + [pltpu.VMEM((LRU_W, 2 * LRU_W), F32)],
        compiler_params=_cp())


def _assemble_dp(dqs, dks, dvs, dg, dgate, dxcs, p, cw, cosf, sins, cb, ride=None):
    n = p.shape[0]
    nb = n // TM
    tile_of = lambda s: s

    def body(dqf, dqb, dkf, dkb, dvf, dvb, dg_ref, dgate_ref, cf, pf, nf, cb_, pb, nb_, x_ref, xp_ref, xn_ref,
             cw_ref, cos_ref, sin_ref, dp_ref, acc_ref):
        s = pl.program_id(0)

        @pl.when(s == 0)
        def _():
            acc_ref[...] = jnp.zeros_like(acc_ref)

        seg_start, seg_end = _seg_flags(s, cb, nb)
        dq = dqf[...].astype(F32) + dqb[...].astype(F32)
        dk = dkf[...].astype(F32) + dkb[...].astype(F32)
        cosv, sinv = cos_ref[...], sin_ref[...]
        for h in range(HEADS):
            sl = slice(h * HEAD_DIM, (h + 1) * HEAD_DIM)
            sk = slice(RET_W + h * HEAD_DIM, RET_W + (h + 1) * HEAD_DIM)
            dp_ref[:, sl] = (dq[:, sl] * cosv + pltpu.roll(dq[:, sl] * sinv, HEAD_DIM // 2, 1)).astype(BF16)
            dp_ref[:, sk] = ((dk[:, sl] * cosv + pltpu.roll(dk[:, sl] * sinv, HEAD_DIM // 2, 1)) * K_SCALE).astype(BF16)
        dp_ref[:, 2 * RET_W:3 * RET_W] = (dvf[...].astype(F32) + dvb[...].astype(F32)).astype(BF16)
        dp_ref[:, 3 * RET_W:4 * RET_W] = dg_ref[...].astype(BF16)
        dxc = cf[...] + cb_[...]
        dprev = pf[...] + pb[...]
        dnext = nf[...] + nb_[...]
        dxr = (_shift_rows(dxc, dprev, dnext, 1, seg_start, seg_end) * cw_ref[0:1, :] + dxc * cw_ref[1:2, :]
               + _shift_rows(dxc, dprev, dnext, -1, seg_start, seg_end) * cw_ref[2:3, :]
               + _shift_rows(dxc, dprev, dnext, -2, seg_start, seg_end) * cw_ref[3:4, :])
        dp_ref[:, 4 * RET_W:4 * RET_W + LRU_W] = dxr.astype(BF16)
        dp_ref[:, 4 * RET_W + LRU_W:] = dgate_ref[...].astype(BF16)
        xr, xp, xn = x_ref[...], xp_ref[...], xn_ref[...]
        for j, k in enumerate((-1, 0, 1, 2)):
            xs = xr if k == 0 else _shift_rows(xr, xp, xn, k, seg_start, seg_end)
            acc_ref[j:j + 1, 0:LRU_W] += jnp.sum(dxc * xs, axis=0, keepdims=True)
        acc_ref[4:5, 0:LRU_W] += jnp.sum(dxc, axis=0, keepdims=True)

    t = pl.BlockSpec((TM, RET_W), lambda s: (s, 0))
    args = (dqs[0], dqs[1], dks[0], dks[1], dvs[0], dvs[1], dg, dgate, dxcs[0], dxcs[0], dxcs[0], dxcs[1], dxcs[1], dxcs[1],
            p, p, p, cw, cosf, sins)
    tab = pl.BlockSpec((TM, HEAD_DIM), lambda s: (s, 0))
    return _pcall_ride(
        body, ride, args, name="assemble_dp", grid=(nb,),
        in_specs=[t] * 8 + _halo_specs(tile_of, n, 0) * 2 + _halo_specs(tile_of, n, COL_XR)
        + [pl.BlockSpec((4, LRU_W), lambda s: (0, 0)), tab, tab],
        out_specs=(pl.BlockSpec((TM, 4 * RET_W + 2 * LRU_W), lambda s: (s, 0)), pl.BlockSpec((8, 2 * LRU_W), lambda s: (0, 0))),
        out_shape=(_sds((n, 4 * RET_W + 2 * LRU_W), BF16), _sds((8, 2 * LRU_W), F32)), compiler_params=_cp())


def _adamw(g, w, m, v):
    nm = ADAM_B1 * m + (1.0 - ADAM_B1) * g
    nv = ADAM_B2 * v + (1.0 - ADAM_B2) * (g * g)
    m_hat = nm / (1.0 - ADAM_B1 ** ADAM_STEP)
    v_hat = nv / (1.0 - ADAM_B2 ** ADAM_STEP)
    return (-ADAM_LR) * (m_hat / (jnp.sqrt(v_hat) + ADAM_EPS) + ADAM_WD * w), nm, nv


def _adam_many(items, name):
    n = len(items)

    def body(*refs):
        for i in range(n):
            g, w, m, v = (r[...] for r in refs[4 * i:4 * i + 4])
            for o_ref, val in zip(refs[4 * n + 3 * i:4 * n + 3 * i + 3], _adamw(g, w, m, v)):
                o_ref[...] = val

    out_shape = tuple(_sds(it[1].shape, F32) for it in items for _ in range(3))
    res = _pcall(body, name=name, out_shape=out_shape, compiler_params=_cp())(*[a for it in items for a in it])
    return [tuple(res[3 * i:3 * i + 3]) for i in range(n)]


def _sum_adam(parts_list, w, m, v, name):
    nparts, _, c = parts_list[0].shape
    r = w.shape[0]
    tr = min(parts_list[0].shape[1], 128)
    starts, o = [], 0
    for pa in parts_list:
        starts.append(o)
        o += pa.shape[1] // tr
    nseg = len(parts_list)

    def body(*refs):
        p_refs = refs[:nseg]
        w_ref, m_ref, v_ref, g_ref, d_ref, nm_ref, nv_ref = refs[nseg:]
        i = pl.program_id(0)
        for s, p_ref in enumerate(p_refs):
            end = starts[s + 1] if s + 1 < nseg else r // tr

            @pl.when(jnp.logical_and(i >= starts[s], i < end))
            def _():
                g = p_ref[0].astype(F32)
                for j in range(1, nparts):
                    g = g + p_ref[j].astype(F32)
                g_ref[...] = g
                d_ref[...], nm_ref[...], nv_ref[...] = _adamw(g, w_ref[...], m_ref[...], v_ref[...])

    def seg_spec(s):
        last = parts_list[s].shape[1] // tr - 1
        return pl.BlockSpec((nparts, tr, c), lambda i: (0, jnp.clip(i - starts[s], 0, last), 0))

    t = pl.BlockSpec((tr, c), lambda i: (i, 0))
    return _pcall(
        body, name=name, grid=(r // tr,),
        in_specs=[seg_spec(s) for s in range(nseg)] + [t, t, t],
        out_specs=(t, t, t, t), out_shape=(_sds((r, c), F32),) * 4, compiler_params=_cp(),
    )(*parts_list, w, m, v)


def _sum_parts(parts, name):
    nparts, r, c = parts.shape

    def body(p_ref, o_ref):
        g = p_ref[0]
        for j in range(1, nparts):
            g = g + p_ref[j]
        o_ref[...] = g

    return _pcall(body, name=name, out_shape=_sds((r, c), parts.dtype), compiler_params=_cp())(parts)


def _rot_tables(l_len, t_len):
    rows = t_len // GRID_W
    n_freq = HEAD_DIM // 4
    inv = ROPE_BASE ** (-jnp.arange(n_freq, dtype=F32) / n_freq)
    ang_r = jnp.arange(rows, dtype=F32)[:, None] * inv
    ang_c = jnp.arange(GRID_W, dtype=F32)[:, None] * inv
    cos = jnp.concatenate([jnp.repeat(jnp.cos(ang_r), GRID_W, axis=0), jnp.tile(jnp.cos(ang_c), (rows, 1))], axis=-1)
    sin = jnp.concatenate([jnp.repeat(jnp.sin(ang_r), GRID_W, axis=0), jnp.tile(jnp.sin(ang_c), (rows, 1))], axis=-1)
    cosf = jnp.concatenate([jnp.ones((l_len, HEAD_DIM), F32), jnp.concatenate([cos, cos], axis=-1)], axis=0)
    sins = jnp.concatenate([jnp.zeros((l_len, HEAD_DIM), F32), jnp.concatenate([-sin, sin], axis=-1)], axis=0)
    return cosf, sins


def _block_diag(w):
    eye = jnp.eye(LRU_BLOCKS, dtype=w.dtype)
    return (w[:, :, None, :] * eye[:, None, :, None]).reshape(LRU_W, LRU_W)


def _blocks_from_lanes(dwd_half):
    return dwd_half.reshape(LRU_BD, LRU_BLOCKS, LRU_BD).transpose(1, 0, 2)


def _silu(x):
    return x * jax.nn.sigmoid(x)


def kernel(x, c, ctx, c_ctx, w_ada, b_ada, norm1_g, norm2_g, w_in, ret_decay, conv_w, conv_b, lru_wa, lru_ba, lru_wx, lru_bx, lru_lambda, w_out, w_mlp1, w_mlp2, final_g, loss_target, m_c_ctx, m_w_ada, m_b_ada, m_norm1_g, m_norm2_g, m_w_in, m_ret_decay, m_conv_w, m_conv_b, m_lru_wa, m_lru_ba, m_lru_wx, m_lru_bx, m_lru_lambda, m_w_out, m_w_mlp1, m_w_mlp2, m_final_g, v_c_ctx, v_w_ada, v_b_ada, v_norm1_g, v_norm2_g, v_w_in, v_ret_decay, v_conv_w, v_conv_b, v_lru_wa, v_lru_ba, v_lru_wx, v_lru_bx, v_lru_lambda, v_w_out, v_w_mlp1, v_w_mlp2, v_final_g):
    t_len, d = x.shape[1], x.shape[2]
    l_len = ctx.shape[1]
    cb, cc = l_len // TM, l_len // CHUNK
    me = 4 * lax.axis_index("x") + 2 * lax.axis_index("y") + lax.axis_index("c")
    x2d, ctx2d, tgt2d = x[0], ctx[0], loss_target[0]
    ada_cols = w_ada.shape[2]
    wa2d = w_ada[0]

    sc_loc = conv_w.shape[2]
    pack_a = jnp.zeros((8, d), F32)
    pack_a = pack_a.at[0].set(_silu(c[0]))
    pack_a = pack_a.at[1, :4 * sc_loc].set(conv_w[0].reshape(-1))
    pack_a = pack_a.at[2, :2 * sc_loc].set(lru_ba[0].reshape(-1))
    pack_a = pack_a.at[3, :2 * sc_loc].set(lru_bx[0].reshape(-1))
    pack_a = pack_a.at[4, :2 * sc_loc].set(lru_lambda[0].reshape(-1))
    all_a = _all_gather_small(pack_a, "gather_small_in")
    s16 = jnp.zeros((16, d), F32).at[0:8].set(all_a[:, 0, :]).at[8].set(_silu(c_ctx))

    def unshard(row, k):
        return all_a[:, row, :k * sc_loc].reshape(N_DEV, k, sc_loc).transpose(1, 0, 2).reshape(k, N_DEV * sc_loc)

    conv_w_full = unshard(1, 4)
    ba_full, bx_full, lam_full = unshard(2, 2), unshard(3, 2), unshard(4, 2)

    b_cols = lax.dynamic_slice(b_ada, (0, me * ada_cols), (1, ada_cols))
    mod_parts = _all_gather_small(_mod_part(s16, wa2d, b_cols), "gather_mod")
    mod_all = mod_parts.transpose(1, 0, 2).reshape(16, N_DEV * ada_cols)
    mod_me = lax.dynamic_slice(mod_all, (me, 0), (1, 6 * d)).reshape(6, d)
    mod_c = mod_all[8].reshape(6, d)
    modrows = jnp.concatenate([mod_c[0:2], mod_me], axis=0)

    lg = jax.nn.log_sigmoid(ret_decay[0])
    sp = jax.nn.softplus(-lam_full)
    wg = [jnp.concatenate([_block_diag(lru_wa[0, dd]), _block_diag(lru_wx[0, dd])], axis=1).astype(BF16) for dd in (0, 1)]
    bg = [jnp.concatenate([ba_full[dd], bx_full[dd]])[None, :] for dd in (0, 1)]
    cosf, sins = _rot_tables(l_len, t_len)

    (hn, hnt), win_g = _norm1_fwd(ctx2d, x2d, norm1_g, modrows, cb, ride=("gather", w_in[0].astype(BF16)))
    (qkv, p), w1_g = _mm_in(hn, win_g, cosf, sins, ride=("gather", w_mlp1[0].astype(BF16)))
    (o0, o1, sp0, sp1), w2_g = _ret_fwd(qkv, lg, cc, ride=("gather", w_mlp2[0].astype(BF16)))
    w2_g = w2_g.reshape(1, 4 * d, d)
    o, s_prev = [o0, o1], [sp0, sp1]
    (h0, cin0), wout_g = _lru_fwd(p, wg[0], bg[0], sp[0:1], conv_w_full, conv_b, 0, cb, ride=("gather", w_out[0].astype(BF16)))
    wout_g = wout_g.reshape(1, d, d)
    (h1, cin1), _ = _lru_fwd(p, wg[1], bg[1], sp[1:2], conv_w_full, conv_b, 1, cb)
    h, cin = [h0, h1], [cin0, cin1]
    mix = _mix_fwd(o[0], o[1], p, h[0], h[1], cb, t_len)
    y, x1, h2, h2t = _mm_nn(mix, wout_g, F32, "mm_out_norm2", tm=TL, epi=_res_norm2_epilogue(x2d, norm2_g, modrows, TL))
    r = _mm_nn(h2, w1_g, BF16, "mm_mlp1", relu_out=True, tm=TL)
    dx2, dz, dzt, facc = _mm_nn(r, w2_g, F32, "mm_mlp2_final", square_lhs=True, tm=TL, vmem_mb=58,
                                epi=_final_epilogue(x1, tgt2d, final_g[None, :], modrows, TL))

    du = _mm_nt(dz, w2_g, BF16, "mm_da2", relu_mul=r, tm=TL, vmem_mb=58)
    gw2 = _mm_wgrad(dzt, r, "mm_dw2", w2_g.shape[1] // N_DEV, BF16, transpose_out=True, square_rhs=True)
    gw1_lo, gw1_hi = _mm_wgrad(h2t, du, "mm_dw1", w1_g.shape[2], BF16, halves=True)
    (dx1, dy, n2acc), gw1_lo_all = _mm_nt(du, w1_g, F32, "mm_dh2_norm2", ride=("a2a", gw1_lo), tm=TL, vmem_mb=60,
                                          epi=_bwd_norm2_epilogue(x1, dx2, y, norm2_g, modrows, TL))
    gwo = _mm_tn(mix, dy, "mm_dwout", False, d, BF16, 512).reshape(N_DEV, -1, d)
    (do, dg, dgate, dhs), gw1_hi_all = _mm_nt(dy, wout_g, F32, "mm_dmix_mix", ride=("a2a", gw1_hi),
                                              epi=_mix_bwd_epilogue(o[0], o[1], p, h[0], h[1], cb))
    gw1_all = [gw1_lo_all, gw1_hi_all]
    dxcs, dwgs, laccs, rides = [], [], [], [("a2a", gwo), None]
    (dq0, dk0, dv0, dq1, dk1, dv1, dlg_lanes), gw2_all = _ret_bwd(qkv, lg, do, s_prev, cc, ride=("a2a", gw2))
    dqs, dks, dvs = [dq0, dq1], [dk0, dk1], [dv0, dv1]
    for dd in (0, 1):
        (dxc_, dwg_, lacc_), got_ = _lru_bwd(p, wg[dd], bg[dd], sp[dd:dd + 1], conv_w_full, conv_b, h[dd], cin[dd], dhs, dd, cb,
                                             ride=rides[dd])
        dxcs.append(dxc_); dwgs.append(dwg_); laccs.append(lacc_)
        gwo_all = got_ if dd == 0 else gwo_all
    (dp, cacc), _ = _assemble_dp(dqs, dks, dvs, dg, dgate, dxcs, p, conv_w_full, cosf, sins, cb)
    pack_b = jnp.concatenate([n2acc, facc, cacc, laccs[0], laccs[1], dlg_lanes, dwgs[0], dwgs[1]], axis=0)
    gwi, all_b = _mm_wgrad(hnt, dp, "mm_dwin", win_g.shape[2], BF16, ride=("gather", pack_b))
    (grad_x, n1acc), gwi_all = _mm_nt(dp, win_g, F32, "mm_dhn_norm1", ride=("a2a", gwi),
                                      epi=_bwd_norm1_epilogue(ctx2d, x2d, dx1, norm1_g, modrows, cb))
    all_n1 = _all_gather_small(n1acc, "gather_norm1_grads")
    tot = _sum_parts(all_b, "sum_small_grads")
    t_n1 = _sum_parts(all_n1, "sum_norm1_grads")
    t_n2, t_f, t_conv, t_dlg = tot[0:8], tot[8:16], tot[16:24, :LRU_W], tot[40:48, :HEAD_DIM]
    t_l = [tot[24:32], tot[32:40]]
    t_dwd = [tot[48:48 + LRU_BD], tot[48 + LRU_BD:48 + 2 * LRU_BD]]
    loss = (0.5 / d) * jnp.sum(t_f[2])
    t_wa = jnp.stack([_blocks_from_lanes(t_dwd[dd][:, :LRU_W]) for dd in (0, 1)])
    t_wx = jnp.stack([_blocks_from_lanes(t_dwd[dd][:, LRU_W:]) for dd in (0, 1)])
    t_ba = jnp.stack([t_l[dd][0, :LRU_W] for dd in (0, 1)])
    t_bx = jnp.stack([t_l[dd][0, LRU_W:] for dd in (0, 1)])
    t_sp = jnp.stack([t_l[dd][1, :LRU_W] for dd in (0, 1)])
    dm_rows = jnp.stack([all_n1[:, 2, :], all_n1[:, 3, :], all_b[:, 3, :], all_b[:, 0, :], all_b[:, 1, :], all_b[:, 9, :]],
                        axis=1).reshape(N_DEV, 6 * d)
    dm_c = jnp.concatenate([t_n1[0], t_n1[1], jnp.zeros((4 * d,), F32)])
    dm16 = jnp.zeros((16, 6 * d), F32).at[0:8].set(dm_rows).at[8].set(dm_c)
    g_b_ada = jnp.sum(dm16, axis=0)[None, :]
    dm_cols = lax.dynamic_slice(dm16, (0, me * ada_cols), (16, ada_cols))
    g_w_ada, ds16 = _ada_bwd(s16, dm_cols, wa2d)
    ds_all = _all_gather_small(ds16[8:16], "gather_dsilu")
    dsilu_cc = _sum_parts(ds_all, "sum_dsilu")[0]
    sg_cc = jax.nn.sigmoid(c_ctx)
    g_c_ctx = dsilu_cc * (sg_cc * (1.0 + c_ctx * (1.0 - sg_cc)))

    g_ret_decay = jnp.sum(t_dlg, axis=-1).reshape(2, HEADS) * jax.nn.sigmoid(-ret_decay[0])
    g_lambda_full = -t_sp * jax.nn.sigmoid(-lam_full)

    def my_cols(full):
        return lax.dynamic_slice(full, (0, me * sc_loc), (full.shape[0], sc_loc))

    small_g = dict(
        c_ctx=g_c_ctx[None], b_ada=g_b_ada, norm1_g=t_n1[4:5], norm2_g=t_n2[2:3], ret_decay=g_ret_decay,
        conv_w=my_cols(t_conv[0:4]), conv_b=t_conv[4:5], lru_wa=t_wa.reshape(-1, LRU_BD), lru_ba=my_cols(t_ba),
        lru_wx=t_wx.reshape(-1, LRU_BD), lru_bx=my_cols(t_bx), lru_lambda=my_cols(g_lambda_full), final_g=t_f[0:1])
    small = dict(
        c_ctx=(c_ctx, m_c_ctx, v_c_ctx), b_ada=(b_ada, m_b_ada, v_b_ada), norm1_g=(norm1_g, m_norm1_g, v_norm1_g),
        norm2_g=(norm2_g, m_norm2_g, v_norm2_g), ret_decay=(ret_decay, m_ret_decay, v_ret_decay),
        conv_w=(conv_w, m_conv_w, v_conv_w), conv_b=(conv_b, m_conv_b, v_conv_b), lru_wa=(lru_wa, m_lru_wa, v_lru_wa),
        lru_ba=(lru_ba, m_lru_ba, v_lru_ba), lru_wx=(lru_wx, m_lru_wx, v_lru_wx), lru_bx=(lru_bx, m_lru_bx, v_lru_bx),
        lru_lambda=(lru_lambda, m_lru_lambda, v_lru_lambda), final_g=(final_g, m_final_g, v_final_g))
    names = list(small)
    items = [(small_g[k],) + tuple(a.reshape(small_g[k].shape) for a in small[k]) for k in names]
    res = {}
    for k, it, (d_, m_, v_) in zip(names, items, _adam_many(items, "adam_small")):
        shape = small[k][0].shape
        res[k] = tuple(a.reshape(shape) for a in (it[0], d_, m_, v_))

    def big(parts, w, m, v, name):
        out = _sum_adam(parts, w[0], m[0], v[0], name)
        return tuple(a[None] for a in out)

    res["w_ada"] = big([g_w_ada[None]], w_ada, m_w_ada, v_w_ada, "adam_w_ada")
    res["w_in"] = big([gwi_all], w_in, m_w_in, v_w_in, "adam_w_in")
    res["w_out"] = big([gwo_all], w_out, m_w_out, v_w_out, "adam_w_out")
    res["w_mlp1"] = big(gw1_all, w_mlp1, m_w_mlp1, v_w_mlp1, "adam_w_mlp1")
    res["w_mlp2"] = big([gw2_all], w_mlp2, m_w_mlp2, v_w_mlp2, "adam_w_mlp2")

    order = ["c_ctx", "w_ada", "b_ada", "norm1_g", "norm2_g", "w_in", "ret_decay", "conv_w", "conv_b", "lru_wa", "lru_ba",
             "lru_wx", "lru_bx", "lru_lambda", "w_out", "w_mlp1", "w_mlp2", "final_g"]
    outs = [loss, grad_x[None]]
    for j in range(4):
        outs += [res[k][j] for k in order]
    return tuple(outs)
```

```python
import jax
import jax.numpy as jnp
from jax import lax
from jax.experimental import pallas as pl
from jax.experimental.pallas import tpu as pltpu

F32 = jnp.float32
BF16 = jnp.bfloat16
AXES = ("x", "y", "c")
N_DEV = 8
MESH = pl.DeviceIdType.MESH

HEADS = 4
HEAD_DIM = 128
CHUNK = 128
RET_W = HEADS * HEAD_DIM
LRU_W = 512
LRU_BLOCKS = 8
LRU_BD = LRU_W // LRU_BLOCKS
LRU_C = 8.0
EPS = 1e-6
K_SCALE = HEAD_DIM ** -0.5
ROPE_BASE = 10000.0
GRID_W = 64
TM = 256
TL = 512
SUB = 8

ADAM_LR = 0.001
ADAM_B1 = 0.9
ADAM_B2 = 0.999
ADAM_EPS = 1e-08
ADAM_WD = 0.01
ADAM_STEP = 10

COL_G, COL_XR, COL_GATE = 0, 1, 2

R_CSH1, R_CSC1, R_SH1, R_SC1, R_G1, R_SH2, R_SC2, R_G2 = range(8)


def _pcall(body, **kw):
    return pl.pallas_call(body, **kw)


def _cp(vmem_mb=48):
    return pltpu.CompilerParams(vmem_limit_bytes=vmem_mb << 20)


def _sds(shape, dtype):
    return jax.ShapeDtypeStruct(shape, dtype)


def _dot(a, b):
    return jnp.dot(a.astype(BF16), b.astype(BF16), preferred_element_type=F32)


def _dot_nt(a, b):
    return lax.dot_general(a.astype(BF16), b.astype(BF16), (((1,), (1,)), ((), ())), preferred_element_type=F32)


def _dot_tn(a, b):
    return lax.dot_general(a.astype(BF16), b.astype(BF16), (((0,), (0,)), ((), ())), preferred_element_type=F32)


def _sigmoid(x):
    return 0.5 * jnp.tanh(0.5 * x) + 0.5


def _gelu(x):
    return 0.5 * x * (1.0 + jnp.tanh(0.7978845608028654 * (x + 0.044715 * x * x * x)))


def _dgelu(x):
    t = jnp.tanh(0.7978845608028654 * (x + 0.044715 * x * x * x))
    return 0.5 * (1.0 + t) + 0.5 * x * (1.0 - t * t) * 0.7978845608028654 * (1.0 + 3.0 * 0.044715 * x * x)


def _rows_iota(shape):
    return lax.broadcasted_iota(jnp.int32, shape, 0)


def _tile_order(dirn, s, cb, nb):
    if dirn == 0:
        return s
    return jnp.where(s < cb, cb - 1 - s, nb - 1 - (s - cb))


_SEMS = [pltpu.SemaphoreType.DMA((7,)), pltpu.SemaphoreType.DMA((7,)), pltpu.SemaphoreType.DMA(())]
_ANY = pl.BlockSpec(memory_space=pl.ANY)


def _gather_copies(x_ref, out_ref, send_sems, recv_sems, local_sem):
    mx, my, mc = lax.axis_index("x"), lax.axis_index("y"), lax.axis_index("c")
    me, sibling = (mx, my, mc), (mx, my, 1 - mc)
    chips = [(1 - mx, my), (mx, 1 - my), (1 - mx, 1 - my)]

    def slot(px, py, pc):
        return out_ref.at[4 * px + 2 * py + pc]

    def copy(k, block, to, src=None):
        return pltpu.make_async_remote_copy(
            src_ref=slot(*block) if src is None else src, dst_ref=slot(*block),
            send_sem=send_sems.at[k], recv_sem=recv_sems.at[k], device_id=to, device_id_type=MESH)

    mine = pltpu.make_async_copy(x_ref, slot(*me), local_sem)
    first = [copy(0, me, sibling, src=x_ref)] + [copy(1 + j, me, (*chip, mc), src=x_ref) for j, chip in enumerate(chips)]
    passed = [copy(4 + j, (*chip, mc), sibling) for j, chip in enumerate(chips)]
    recv_ici = [copy(1 + j, (*chip, mc), me) for j, chip in enumerate(chips)]
    recv_d2d = [copy(0, sibling, me)] + [copy(4 + j, (*chip, 1 - mc), me) for j, chip in enumerate(chips)]
    return mine, first, passed, recv_ici, recv_d2d


def _gather_start(*refs):
    mine, first, _, _, _ = _gather_copies(*refs)
    mine.start()
    for cp in first:
        cp.start()


def _gather_finish(*refs):
    mine, first, passed, recv_ici, recv_d2d = _gather_copies(*refs)
    for landed, onward in zip(recv_ici, passed):
        landed.wait_recv()
        onward.start()
    for landed in recv_d2d:
        landed.wait_recv()
    for cp in first + passed:
        cp.wait_send()
    mine.wait()


def _a2a_copies(g_ref, out_ref, send_sems, recv_sems, local_sem):
    mx, my, mc = lax.axis_index("x"), lax.axis_index("y"), lax.axis_index("c")
    me = 4 * mx + 2 * my + mc
    mine = pltpu.make_async_copy(g_ref.at[me], out_ref.at[me], local_sem)
    copies = []
    for k in range(1, N_DEV):
        px = 1 - mx if (k >> 2) & 1 else mx
        py = 1 - my if (k >> 1) & 1 else my
        pc = 1 - mc if k & 1 else mc
        copies.append(pltpu.make_async_remote_copy(
            src_ref=g_ref.at[4 * px + 2 * py + pc], dst_ref=out_ref.at[me],
            send_sem=send_sems.at[k - 1], recv_sem=recv_sems.at[k - 1],
            device_id=(px, py, pc), device_id_type=MESH))
    return mine, copies


def _a2a_start(*refs):
    mine, copies = _a2a_copies(*refs)
    mine.start()
    for cp in copies:
        cp.start()


def _a2a_finish(*refs):
    mine, copies = _a2a_copies(*refs)
    for cp in copies:
        cp.wait()
    mine.wait()


_EXCHANGES = {"gather": (_gather_start, _gather_finish), "a2a": (_a2a_start, _a2a_finish)}


def _exchange_shape(kind, src):
    return _sds((N_DEV,) + src.shape if kind == "gather" else src.shape, src.dtype)


def _all_gather_small(x, name):
    def body(x_ref, out_ref, send_sems, recv_sems, local_sem):
        mx, my, mc = lax.axis_index("x"), lax.axis_index("y"), lax.axis_index("c")
        me = 4 * mx + 2 * my + mc
        mine = pltpu.make_async_copy(x_ref, out_ref.at[me], local_sem)
        mine.start()
        copies = []
        for k in range(1, N_DEV):
            peer = (1 - mx if (k >> 2) & 1 else mx, 1 - my if (k >> 1) & 1 else my, 1 - mc if k & 1 else mc)
            copies.append(pltpu.make_async_remote_copy(
                src_ref=x_ref, dst_ref=out_ref.at[me], send_sem=send_sems.at[k - 1], recv_sem=recv_sems.at[k - 1],
                device_id=peer, device_id_type=MESH))
            copies[-1].start()
        for cp in copies:
            cp.wait()
        mine.wait()

    return _pcall(body, name=name, out_shape=_exchange_shape("gather", x), in_specs=[_ANY], out_specs=_ANY,
                  scratch_shapes=list(_SEMS))(x)


def _pcall_ride(body, ride, args, *, name, grid, in_specs, out_specs, out_shape, scratch_shapes=(), compiler_params=None):
    if ride is None:
        out = _pcall(body, name=name, grid=grid, in_specs=in_specs, out_specs=out_specs, out_shape=out_shape,
                     scratch_shapes=list(scratch_shapes), compiler_params=compiler_params)(*args)
        return out, None
    kind, src = ride
    start, finish = _EXCHANGES[kind]
    single = not isinstance(out_shape, (tuple, list))
    out_specs_t = (out_specs,) if single else tuple(out_specs)
    out_shape_t = (out_shape,) if single else tuple(out_shape)
    n_in, n_out, n_sc = len(in_specs), len(out_shape_t), len(scratch_shapes)

    def wrapped(*refs):
        ins, src_ref = refs[:n_in], refs[n_in]
        outs, dst_ref = refs[n_in + 1:n_in + 1 + n_out], refs[n_in + 1 + n_out]
        scratch = refs[n_in + 2 + n_out:n_in + 2 + n_out + n_sc]
        sems = refs[n_in + 2 + n_out + n_sc:]
        first = pl.program_id(0) == 0
        last = pl.program_id(0) == grid[0] - 1
        for ax in range(1, len(grid)):
            first = jnp.logical_and(first, pl.program_id(ax) == 0)
            last = jnp.logical_and(last, pl.program_id(ax) == grid[ax] - 1)

        @pl.when(first)
        def _():
            start(src_ref, dst_ref, *sems)

        body(*ins, *outs, *scratch)

        @pl.when(last)
        def _():
            finish(src_ref, dst_ref, *sems)

    res = _pcall(wrapped, name=name, grid=grid, in_specs=list(in_specs) + [_ANY], out_specs=out_specs_t + (_ANY,),
                 out_shape=out_shape_t + (_exchange_shape(kind, src),),
                 scratch_shapes=list(scratch_shapes) + list(_SEMS), compiler_params=compiler_params)(*args, src)
    return (res[0] if single else tuple(res[:-1])), res[-1]


SUB_ROWS = TM


class _Epilogue:
    def __init__(self, fn, args, in_specs, out_specs, out_shape, steps=None, lhs_map=None):
        self.fn, self.args, self.in_specs, self.out_specs, self.out_shape = fn, tuple(args), list(in_specs), out_specs, out_shape
        self.steps, self.lhs_map = steps, lhs_map


def _mm_nn(a, w, out_dtype, name, square_lhs=False, relu_out=False, ride=None, tm=TM, epi=None, vmem_mb=48):
    m, k = a.shape
    nb, _, bn = w.shape
    tm = min(tm, m)
    n_extra = 0 if epi is None else len(epi.args)

    def body(*refs):
        a_ref, w_ref = refs[:2]
        av = a_ref[...]
        if square_lhs:
            av = av * av
        if epi is not None:
            assert nb == 1
            for r0 in range(0, tm, SUB_ROWS):
                rows = slice(r0, r0 + SUB_ROWS)
                epi.fn(jnp.dot(av[rows], w_ref[0], preferred_element_type=F32), rows, *refs[2:])
            return
        for j in range(nb):
            r = jnp.dot(av, w_ref[j], preferred_element_type=F32)
            if relu_out:
                r = jnp.maximum(r, 0.0)
            refs[2][:, j * bn:(j + 1) * bn] = r.astype(out_dtype)

    in_specs = [pl.BlockSpec((tm, k), lambda i: (i, 0)), pl.BlockSpec((nb, k, bn), lambda i: (0, 0, 0))]
    if epi is None:
        args, out_specs, out_shape = (a, w), pl.BlockSpec((tm, nb * bn), lambda i: (i, 0)), _sds((m, nb * bn), out_dtype)
    else:
        args, out_specs, out_shape = (a, w) + epi.args, epi.out_specs, epi.out_shape
        in_specs += epi.in_specs
    out, ex = _pcall_ride(body, ride, args, name=name, grid=(m // tm,), in_specs=in_specs, out_specs=out_specs,
                          out_shape=out_shape, compiler_params=_cp(vmem_mb))
    return out if ride is None else (out, ex)


def _mm_nt(dy, w, out_dtype, name, relu_mul=None, ride=None, tm=TM, epi=None, vmem_mb=48):
    m = dy.shape[0]
    nb, k, bn = w.shape
    tm = min(tm, m)
    n_extra = (0 if relu_mul is None else 1) + (0 if epi is None else len(epi.args))

    def body(*refs):
        dy_ref, w_ref = refs[:2]
        extra, outs, wt = refs[2:2 + n_extra], refs[2 + n_extra:-1], refs[-1]

        @pl.when(pl.program_id(0) == 0)
        def _():
            for j in range(nb):
                wt[j * bn:(j + 1) * bn, :] = w_ref[j].T

        if epi is not None:
            for r0 in range(0, tm, SUB_ROWS):
                rows = slice(r0, r0 + SUB_ROWS)
                epi.fn(jnp.dot(dy_ref[rows, :], wt[...], preferred_element_type=F32), rows, *extra, *outs)
            return
        acc = jnp.dot(dy_ref[...], wt[...], preferred_element_type=F32)
        if relu_mul is not None:
            acc = acc * (2.0 * extra[0][...].astype(F32))
        outs[0][...] = acc.astype(out_dtype)

    lhs_map = (lambda i: (i, 0)) if epi is None or epi.lhs_map is None else epi.lhs_map
    in_specs = [pl.BlockSpec((tm, nb * bn), lhs_map), pl.BlockSpec((nb, k, bn), lambda i: (0, 0, 0))]
    args = [dy, w]
    if relu_mul is not None:
        in_specs.append(pl.BlockSpec((tm, k), lambda i: (i, 0)))
        args.append(relu_mul)
    steps = m // tm
    if epi is None:
        out_specs, out_shape = pl.BlockSpec((tm, k), lambda i: (i, 0)), _sds((m, k), out_dtype)
    else:
        args += list(epi.args)
        in_specs += epi.in_specs
        out_specs, out_shape = epi.out_specs, epi.out_shape
        steps = steps if epi.steps is None else epi.steps
    out, ex = _pcall_ride(
        body, ride, args, name=name, grid=(steps,), in_specs=in_specs, out_specs=out_specs, out_shape=out_shape,
        scratch_shapes=[pltpu.VMEM((nb * bn, k), BF16)], compiler_params=_cp(vmem_mb))
    return out if ride is None else (out, ex)


def _mm_tn(a, b, name, col_blocks, block, out_dtype, tm, square_lhs=False):
    m, k = a.shape
    nn = b.shape[1]
    steps = m // tm
    if col_blocks:
        nblk, acc_shape = nn // block, (k, block)
        a_spec = pl.BlockSpec((tm, k), lambda j, s: (s, 0))
        b_spec = pl.BlockSpec((tm, block), lambda j, s: (s, j))
    else:
        nblk, acc_shape = k // block, (block, nn)
        a_spec = pl.BlockSpec((tm, block), lambda j, s: (s, j))
        b_spec = pl.BlockSpec((tm, nn), lambda j, s: (s, 0))

    def body(a_ref, b_ref, o_ref, acc):
        s = pl.program_id(1)

        @pl.when(s == 0)
        def _():
            acc[...] = jnp.zeros_like(acc)

        av = a_ref[...]
        if square_lhs:
            av = av.astype(F32)
            av = (av * av).astype(BF16)
        acc[...] += _dot_tn(av, b_ref[...])

        @pl.when(s == steps - 1)
        def _():
            o_ref[...] = acc[...].astype(out_dtype)

    return _pcall(
        body, name=name, grid=(nblk, steps), in_specs=[a_spec, b_spec],
        out_specs=pl.BlockSpec((None,) + acc_shape, lambda j, s: (j, 0, 0)),
        out_shape=_sds((nblk,) + acc_shape, out_dtype),
        scratch_shapes=[pltpu.VMEM(acc_shape, F32)], compiler_params=_cp(),
    )(a, b)


def _mm_wgrad(at, b, name, bn, out_dtype, transpose_out=False, square_rhs=False, halves=False, ride=None):
    k, m = at.shape
    nblk = b.shape[1] // bn
    rows, cols = (bn, k) if transpose_out else (k, bn)
    nout = 2 if halves else 1
    per = rows // nout

    def body(a_ref, b_ref, *o_refs):
        bv = b_ref[...]
        if square_rhs:
            bv = bv * bv
        r = jnp.dot(a_ref[...], bv, preferred_element_type=F32)
        r = (r.T if transpose_out else r).astype(out_dtype)
        for i, o_ref in enumerate(o_refs):
            o_ref[...] = r[i * per:(i + 1) * per, :]

    out, ex = _pcall_ride(
        body, ride, (at, b), name=name, grid=(nblk,),
        in_specs=[pl.BlockSpec((k, m), lambda j: (0, 0)), pl.BlockSpec((m, bn), lambda j: (0, j))],
        out_specs=tuple(pl.BlockSpec((None, per, cols), lambda j: (j, 0, 0)) for _ in range(nout)),
        out_shape=tuple(_sds((nblk, per, cols), out_dtype) for _ in range(nout)), compiler_params=_cp())
    out = out if halves else out[0]
    return out if ride is None else (out, ex)


def _mm_in(hn, w, cosf, sins, ride):
    m, k = hn.shape
    nb, _, bn = w.shape

    def body(a_ref, w_ref, c_ref, s_ref, qkv_ref, rest_ref, pt):
        av = a_ref[...]
        for j in range(nb):
            pt[:, j * bn:(j + 1) * bn] = jnp.dot(av, w_ref[j], preferred_element_type=F32)
        cf, ss = c_ref[...], s_ref[...]
        for h in range(HEADS):
            sq = slice(h * HEAD_DIM, (h + 1) * HEAD_DIM)
            sk = slice(RET_W + h * HEAD_DIM, RET_W + (h + 1) * HEAD_DIM)
            qkv_ref[:, sq] = _rot(pt[:, sq], cf, ss).astype(BF16)
            qkv_ref[:, sk] = (_rot(pt[:, sk], cf, ss) * K_SCALE).astype(BF16)
        qkv_ref[:, 2 * RET_W:] = pt[:, 2 * RET_W:3 * RET_W].astype(BF16)
        rest_ref[...] = pt[:, 3 * RET_W:]

    tab = pl.BlockSpec((TM, HEAD_DIM), lambda i: (i, 0))
    wide = pl.BlockSpec((TM, 3 * RET_W), lambda i: (i, 0))
    return _pcall_ride(
        body, ride, (hn, w, cosf, sins), name="mm_in", grid=(m // TM,),
        in_specs=[pl.BlockSpec((TM, k), lambda i: (i, 0)), pl.BlockSpec((nb, k, bn), lambda i: (0, 0, 0)), tab, tab],
        out_specs=(wide, wide), out_shape=(_sds((m, 3 * RET_W), BF16), _sds((m, nb * bn - 3 * RET_W), F32)),
        scratch_shapes=[pltpu.VMEM((TM, nb * bn), F32)], compiler_params=_cp())


def _mod_part(s16, w_ada, b_cols):
    def body(s_ref, w_ref, b_ref, o_ref):
        o_ref[...] = _dot(s_ref[...], w_ref[...]) + b_ref[...]

    return _pcall(body, name="mod_part", out_shape=_sds((s16.shape[0], w_ada.shape[1]), F32),
                  compiler_params=_cp())(s16, w_ada, b_cols)


def _ada_bwd(s16, dm_cols, w_ada):
    def body(s_ref, d_ref, w_ref, gw_ref, ds_ref):
        gw_ref[...] = _dot_tn(s_ref[...], d_ref[...])
        ds_ref[...] = _dot_nt(d_ref[...], w_ref[...])

    return _pcall(body, name="ada_bwd",
                  out_shape=(_sds(w_ada.shape, F32), _sds(s16.shape, F32)), compiler_params=_cp())(s16, dm_cols, w_ada)


def _norm1_fwd(ctx, x, g, modrows, cb, ride=None):
    l_len, d = ctx.shape
    nb = (l_len + x.shape[0]) // TM

    def body(ctx_ref, x_ref, g_ref, m_ref, o_ref, ot_ref):
        is_ctx = pl.program_id(0) < cb
        xin = jnp.where(is_ctx, ctx_ref[...], x_ref[...])
        sh = jnp.where(is_ctx, m_ref[R_CSH1:R_CSH1 + 1, :], m_ref[R_SH1:R_SH1 + 1, :])
        sc = jnp.where(is_ctx, m_ref[R_CSC1:R_CSC1 + 1, :], m_ref[R_SC1:R_SC1 + 1, :])
        ms = jnp.mean(xin * xin, axis=-1, keepdims=True)
        n = xin * lax.rsqrt(ms + EPS) * g_ref[...]
        hn = n * (1.0 + sc) + sh
        o_ref[...] = hn.astype(BF16)
        ot_ref[...] = hn.T.astype(BF16)

    return _pcall_ride(
        body, ride, (ctx, x, g, modrows), name="norm1_fwd", grid=(nb,),
        in_specs=[pl.BlockSpec((TM, d), lambda i: (jnp.minimum(i, cb - 1), 0)),
                  pl.BlockSpec((TM, d), lambda i: (jnp.maximum(i - cb, 0), 0)),
                  pl.BlockSpec((1, d), lambda i: (0, 0)), pl.BlockSpec((8, d), lambda i: (0, 0))],
        out_specs=(pl.BlockSpec((TM, d), lambda i: (i, 0)), pl.BlockSpec((d, TM), lambda i: (0, i))),
        out_shape=(_sds((nb * TM, d), BF16), _sds((d, nb * TM), BF16)), compiler_params=_cp())


def _mix_fwd(o_f, o_b, p, h_f, h_b, cb, t_len):
    def body(of_ref, ob_ref, g_ref, gate_ref, hf_ref, hb_ref, mix_ref):
        o = of_ref[...] + ob_ref[...]
        g = g_ref[...]
        sg = g * _sigmoid(g)
        for hh in range(HEADS):
            sl = slice(hh * HEAD_DIM, (hh + 1) * HEAD_DIM)
            oh = o[:, sl]
            yc = oh - jnp.mean(oh, axis=-1, keepdims=True)
            var = jnp.mean(yc * yc, axis=-1, keepdims=True)
            mix_ref[:, sl] = (sg[:, sl] * (yc * lax.rsqrt(var + EPS))).astype(BF16)
        mix_ref[:, RET_W:] = ((hf_ref[...] + hb_ref[...]) * _gelu(gate_ref[...])).astype(BF16)

    row = lambda i: (i + cb, 0)
    return _pcall(
        body, name="mix_fwd", grid=(t_len // TM,),
        in_specs=[pl.BlockSpec((TM, RET_W), row), pl.BlockSpec((TM, RET_W), row),
                  pl.BlockSpec((TM, RET_W), lambda i: (i + cb, COL_G)), pl.BlockSpec((TM, LRU_W), lambda i: (i + cb, COL_GATE)),
                  pl.BlockSpec((TM, LRU_W), row), pl.BlockSpec((TM, LRU_W), row)],
        out_specs=pl.BlockSpec((TM, RET_W + LRU_W), lambda i: (i, 0)),
        out_shape=_sds((t_len, RET_W + LRU_W), BF16), compiler_params=_cp(),
    )(o_f, o_b, p, p, h_f, h_b)


def _res_norm2_epilogue(x, g, modrows, tm):
    t_len, d = x.shape
    tm = min(tm, t_len)

    def fn(y, rows, x_ref, g_ref, m_ref, y_ref, x1_ref, h2_ref, h2t_ref):
        y_ref[rows, :] = y
        x1 = x_ref[rows, :] + m_ref[R_G1:R_G1 + 1, :] * y
        ms = jnp.mean(x1 * x1, axis=-1, keepdims=True)
        n = x1 * lax.rsqrt(ms + EPS) * g_ref[...]
        x1_ref[rows, :] = x1
        h2 = n * (1.0 + m_ref[R_SC2:R_SC2 + 1, :]) + m_ref[R_SH2:R_SH2 + 1, :]
        h2_ref[rows, :] = h2.astype(BF16)
        h2t_ref[:, rows] = h2.T.astype(BF16)

    t = pl.BlockSpec((tm, d), lambda i: (i, 0))
    return _Epilogue(
        fn, (x, g, modrows),
        in_specs=[t, pl.BlockSpec((1, d), lambda i: (0, 0)), pl.BlockSpec((8, d), lambda i: (0, 0))],
        out_specs=(t, t, t, pl.BlockSpec((d, tm), lambda i: (0, i))),
        out_shape=(_sds((t_len, d), F32), _sds((t_len, d), F32), _sds((t_len, d), BF16), _sds((d, t_len), BF16)))


def _zero_at_start(acc_ref, rows):
    if rows.start == 0:
        @pl.when(pl.program_id(0) == 0)
        def _():
            acc_ref[...] = jnp.zeros_like(acc_ref)


def _final_epilogue(x1, target, fg, modrows, tm):
    t_len, d = x1.shape

    def fn(z, rows, x1_ref, t_ref, fg_ref, m_ref, dx2_ref, dz_ref, dzt_ref, acc_ref):
        _zero_at_start(acc_ref, rows)
        g2 = m_ref[R_G2:R_G2 + 1, :]
        x2 = x1_ref[rows, :] + g2 * z
        rstd = lax.rsqrt(jnp.mean(x2 * x2, axis=-1, keepdims=True) + EPS)
        xh = x2 * rstd
        fg = fg_ref[...]
        e = xh * fg - t_ref[rows, :]
        dy = e * (1.0 / d)
        dxh = dy * fg
        dx2 = rstd * (dxh - xh * jnp.mean(dxh * xh, axis=-1, keepdims=True))
        dx2_ref[rows, :] = dx2
        dz = g2 * dx2
        dz_ref[rows, :] = dz.astype(BF16)
        dzt_ref[:, rows] = dz.T.astype(BF16)
        acc_ref[0:1, :] += jnp.sum(dy * xh, axis=0, keepdims=True)
        acc_ref[1:2, :] += jnp.sum(dx2 * z, axis=0, keepdims=True)
        acc_ref[2:3, :] += jnp.sum(e * e, axis=0, keepdims=True)

    t = pl.BlockSpec((tm, d), lambda i: (i, 0))
    return _Epilogue(
        fn, (x1, target, fg, modrows),
        in_specs=[t, t, pl.BlockSpec((1, d), lambda i: (0, 0)), pl.BlockSpec((8, d), lambda i: (0, 0))],
        out_specs=(t, t, pl.BlockSpec((d, tm), lambda i: (0, i)), pl.BlockSpec((8, d), lambda i: (0, 0))),
        out_shape=(_sds((t_len, d), F32), _sds((t_len, d), BF16), _sds((d, t_len), BF16), _sds((8, d), F32)))


def _bwd_norm2_epilogue(x1, dx2, y, g, modrows, tm):
    t_len, d = x1.shape

    def fn(dh2, rows, x1_ref, dx2_ref, y_ref, g_ref, m_ref, dx1_ref, dy_ref, acc_ref):
        _zero_at_start(acc_ref, rows)
        x1 = x1_ref[rows, :]
        rstd = lax.rsqrt(jnp.mean(x1 * x1, axis=-1, keepdims=True) + EPS)
        xh = x1 * rstd
        gn = g_ref[...]
        dn = dh2 * (1.0 + m_ref[R_SC2:R_SC2 + 1, :])
        dxh = dn * gn
        dx1 = dx2_ref[rows, :] + rstd * (dxh - xh * jnp.mean(dxh * xh, axis=-1, keepdims=True))
        dx1_ref[rows, :] = dx1
        dy_ref[rows, :] = (m_ref[R_G1:R_G1 + 1, :] * dx1).astype(BF16)
        acc_ref[0:1, :] += jnp.sum(dh2, axis=0, keepdims=True)
        acc_ref[1:2, :] += jnp.sum(dh2 * xh * gn, axis=0, keepdims=True)
        acc_ref[2:3, :] += jnp.sum(dn * xh, axis=0, keepdims=True)
        acc_ref[3:4, :] += jnp.sum(dx1 * y_ref[rows, :], axis=0, keepdims=True)

    t = pl.BlockSpec((tm, d), lambda i: (i, 0))
    return _Epilogue(
        fn, (x1, dx2, y, g, modrows),
        in_specs=[t, t, t, pl.BlockSpec((1, d), lambda i: (0, 0)), pl.BlockSpec((8, d), lambda i: (0, 0))],
        out_specs=(t, t, pl.BlockSpec((8, d), lambda i: (0, 0))),
        out_shape=(_sds((t_len, d), F32), _sds((t_len, d), BF16), _sds((8, d), F32)))


def _mix_bwd_epilogue(o_f, o_b, p, h_f, h_b, cb):
    n = o_f.shape[0]

    def fn(dm, rows, of_ref, ob_ref, g_ref, gate_ref, hf_ref, hb_ref, do_ref, dg_ref, dgate_ref, dhs_ref):
        is_ctx = pl.program_id(0) < cb

        @pl.when(is_ctx)
        def _():
            for r in (do_ref, dg_ref, dgate_ref, dhs_ref):
                r[rows, :] = jnp.zeros((rows.stop - rows.start, r.shape[1]), r.dtype)

        @pl.when(jnp.logical_not(is_ctx))
        def _():
            o = of_ref[rows, :] + ob_ref[rows, :]
            g = g_ref[rows, :]
            s = _sigmoid(g)
            sg = g * s
            dsg = s * (1.0 + g * (1.0 - s))
            for hh in range(HEADS):
                sl = slice(hh * HEAD_DIM, (hh + 1) * HEAD_DIM)
                oh = o[:, sl]
                yc = oh - jnp.mean(oh, axis=-1, keepdims=True)
                rs = lax.rsqrt(jnp.mean(yc * yc, axis=-1, keepdims=True) + EPS)
                gn = yc * rs
                dret = dm[:, sl]
                dgn = dret * sg[:, sl]
                dg_ref[rows, sl] = (dret * gn * dsg[:, sl]).astype(BF16)
                do_ref[rows, sl] = (rs * (dgn - jnp.mean(dgn, axis=-1, keepdims=True)
                                          - gn * jnp.mean(dgn * gn, axis=-1, keepdims=True))).astype(BF16)
            dlru = dm[:, RET_W:]
            gate = gate_ref[rows, :]
            dhs_ref[rows, :] = dlru * _gelu(gate)
            dgate_ref[rows, :] = (dlru * (hf_ref[rows, :] + hb_ref[rows, :]) * _dgelu(gate)).astype(BF16)

    t = pl.BlockSpec((TM, RET_W), lambda i: (i, 0))
    return _Epilogue(
        fn, (o_f, o_b, p, p, h_f, h_b),
        in_specs=[t, t, pl.BlockSpec((TM, RET_W), lambda i: (i, COL_G)), pl.BlockSpec((TM, LRU_W), lambda i: (i, COL_GATE)), t, t],
        out_specs=(t, t, t, t), out_shape=(_sds((n, RET_W), BF16),) * 3 + (_sds((n, RET_W), F32),),
        steps=n // TM, lhs_map=lambda i: (jnp.maximum(i - cb, 0), 0))


def _bwd_norm1_epilogue(ctx, x, dx1, g, modrows, cb):
    t_len, d = x.shape

    def fn(dh, rows, ctx_ref, x_ref, dx1_ref, g_ref, m_ref, gx_ref, acc_ref):
        is_ctx = pl.program_id(0) < cb
        _zero_at_start(acc_ref, rows)
        xin = jnp.where(is_ctx, ctx_ref[rows, :], x_ref[rows, :])
        sc = jnp.where(is_ctx, m_ref[R_CSC1:R_CSC1 + 1, :], m_ref[R_SC1:R_SC1 + 1, :])
        rstd = lax.rsqrt(jnp.mean(xin * xin, axis=-1, keepdims=True) + EPS)
        xh = xin * rstd
        gn = g_ref[...]
        dn = dh * (1.0 + sc)
        dxh = dn * gn
        gx_ref[rows, :] = dx1_ref[rows, :] + rstd * (dxh - xh * jnp.mean(dxh * xh, axis=-1, keepdims=True))
        s0 = jnp.sum(dh, axis=0, keepdims=True)
        s1 = jnp.sum(dh * xh * gn, axis=0, keepdims=True)
        acc_ref[4:5, :] += jnp.sum(dn * xh, axis=0, keepdims=True)

        @pl.when(is_ctx)
        def _():
            acc_ref[0:1, :] += s0
            acc_ref[1:2, :] += s1

        @pl.when(jnp.logical_not(is_ctx))
        def _():
            acc_ref[2:3, :] += s0
            acc_ref[3:4, :] += s1

    lat = pl.BlockSpec((TM, d), lambda i: (jnp.maximum(i - cb, 0), 0))
    return _Epilogue(
        fn, (ctx, x, dx1, g, modrows),
        in_specs=[pl.BlockSpec((TM, d), lambda i: (jnp.minimum(i, cb - 1), 0)),
                  lat, lat, pl.BlockSpec((1, d), lambda i: (0, 0)), pl.BlockSpec((8, d), lambda i: (0, 0))],
        out_specs=(lat, pl.BlockSpec((8, d), lambda i: (0, 0))),
        out_shape=(_sds((t_len, d), F32), _sds((8, d), F32)))


def _rot(x, cf, ss):
    return x * cf + pltpu.roll(x, HEAD_DIM // 2, 1) * ss


def _decay_exponents(dirn):
    ii = lax.broadcasted_iota(jnp.int32, (CHUNK, CHUNK), 0)
    jj = lax.broadcasted_iota(jnp.int32, (CHUNK, CHUNK), 1)
    rel = ii - jj if dirn == 0 else jj - ii
    pos = ii.astype(F32)
    if dirn == 0:
        cq, cs = pos + 1.0, (CHUNK - 1.0) - pos
    else:
        cq, cs = CHUNK - pos, pos
    return rel, jnp.maximum(rel, 0).astype(F32), cq, cs


def _store_decay(lg_ref, dec, relf_ref=None):
    for dirn in (0, 1):
        rel, relf, cq, cs = _decay_exponents(dirn)
        if relf_ref is not None:
            relf_ref[dirn] = relf
        for h in range(HEADS):
            lgv = lg_ref[dirn, h]
            wq, ws = jnp.exp(lgv * cq), jnp.exp(lgv * cs)
            dec[dirn, h, 0] = jnp.where(rel >= 0, jnp.exp(lgv * relf), 0.0)
            dec[dirn, h, 1] = wq
            dec[dirn, h, 2] = ws
            if relf_ref is not None:
                dec[dirn, h, 3] = wq * cq
                dec[dirn, h, 4] = ws * cs


def _ret_rows(cc, nc, step_of):
    return [lambda s, dirn=dirn: _tile_order(dirn, step_of(s), cc, nc) for dirn in (0, 1)]


def _ret_in_specs(rows):
    specs = []
    for row in rows:
        specs += [pl.BlockSpec((CHUNK, RET_W), lambda s, o=o, row=row: (row(s), o)) for o in (0, 1, 2)]
    return specs


def _ret_fwd(qkv, lg, cc, ride=None):
    n = qkv.shape[0]
    nc = n // CHUNK
    rows = _ret_rows(cc, nc, lambda s: s)

    def body(lg_ref, q0, k0, v0, q1, k1, v1, o0, o1, sp0, sp1, st, dec):
        @pl.when(pl.program_id(0) == 0)
        def _():
            st[...] = jnp.zeros_like(st)
            _store_decay(lg_ref, dec)

        refs = ((q0, k0, v0, o0, sp0), (q1, k1, v1, o1, sp1))
        chains = [(dirn, h, slice(h * HEAD_DIM, (h + 1) * HEAD_DIM)) for dirn in (0, 1) for h in range(HEADS)]
        scores, cross, update = [], [], []
        for dirn, h, sl in chains:
            q_ref, k_ref, v_ref, _, sp_ref = refs[dirn]
            q, k, v = q_ref[:, sl], k_ref[:, sl], v_ref[:, sl]
            sp = st[dirn, h]
            sp_ref[h] = sp
            scores.append(_dot_nt(q, k))
            cross.append(_dot(q * dec[dirn, h, 1], sp))
            update.append(_dot_tn(k * dec[dirn, h, 2], v))
        masked = [(a * dec[dirn, h, 0]).astype(BF16) for a, (dirn, h, _) in zip(scores, chains)]
        intra = [_dot(sc, refs[dirn][2][:, sl]) for sc, (dirn, h, sl) in zip(masked, chains)]
        for (dirn, h, sl), o_in, o_cr, upd in zip(chains, intra, cross, update):
            refs[dirn][3][:, sl] = o_in + o_cr
            st[dirn, h] = jnp.exp(lg_ref[dirn, h] * CHUNK) * st[dirn, h] + upd

    o_specs = [pl.BlockSpec((CHUNK, RET_W), lambda s, row=row: (row(s), 0)) for row in rows]
    state = pl.BlockSpec((None, HEADS, CHUNK, HEAD_DIM), lambda s: (s, 0, 0, 0))
    return _pcall_ride(
        body, ride, (lg,) + (qkv,) * 6, name="ret_fwd", grid=(nc,),
        in_specs=[pl.BlockSpec(memory_space=pltpu.SMEM)] + _ret_in_specs(rows),
        out_specs=(o_specs[0], o_specs[1], state, state),
        out_shape=(_sds((n, RET_W), F32),) * 2 + (_sds((nc, HEADS, CHUNK, HEAD_DIM), F32),) * 2,
        scratch_shapes=[pltpu.VMEM((2, HEADS, CHUNK, HEAD_DIM), F32), pltpu.VMEM((2, HEADS, 3, CHUNK, CHUNK), F32)],
        compiler_params=_cp())


def _ret_bwd(qkv, lg, do, s_prev, cc, ride=None):
    n = qkv.shape[0]
    nc = n // CHUNK
    rows = _ret_rows(cc, nc, lambda s: nc - 1 - s)

    def body(lg_ref, q0, k0, v0, q1, k1, v1, do0, do1, sp0, sp1, dq0, dk0, dv0, dq1, dk1, dv1, dlg_ref, dst, dec, relf):
        @pl.when(pl.program_id(0) == 0)
        def _():
            dst[...] = jnp.zeros_like(dst)
            dlg_ref[...] = jnp.zeros_like(dlg_ref)
            _store_decay(lg_ref, dec, relf)

        refs = ((q0, k0, v0, do0, sp0, dq0, dk0, dv0), (q1, k1, v1, do1, sp1, dq1, dk1, dv1))
        chains = [(dirn, h, slice(h * HEAD_DIM, (h + 1) * HEAD_DIM)) for dirn in (0, 1) for h in range(HEADS)]

        def tiles(dirn, sl):
            q_ref, k_ref, v_ref, do_ref = refs[dirn][:4]
            return q_ref[:, sl], k_ref[:, sl], v_ref[:, sl], do_ref[:, sl]

        a_s, g1_s, da_s, h1_s = [], [], [], []
        for dirn, h, sl in chains:
            q, k, v, dov = tiles(dirn, sl)
            a_s.append(_dot_nt(q, k))
            g1_s.append(_dot_nt(dov, refs[dirn][4][h]))
            da_s.append(_dot_nt(dov, v))
            h1_s.append(_dot_nt(v, dst[dirn, h]))
        da_s = [da * dec[dirn, h, 0] for da, (dirn, h, _) in zip(da_s, chains)]
        dq_s, dk_s, dv_s, ds_s = [], [], [], []
        for (dirn, h, sl), a, da in zip(chains, a_s, da_s):
            q, k, v, dov = tiles(dirn, sl)
            dq_s.append(_dot(da, k))
            dk_s.append(_dot_tn(da, q))
            dv_s.append(_dot_tn(a * dec[dirn, h, 0], dov) + _dot(k * dec[dirn, h, 2], dst[dirn, h]))
            ds_s.append(_dot_tn(q * dec[dirn, h, 1], dov))
        for (dirn, h, sl), a, g1, da, h1, dq2, dk2, dv, ds2 in zip(chains, a_s, g1_s, da_s, h1_s, dq_s, dk_s, dv_s, ds_s):
            q, k, _, _ = tiles(dirn, sl)
            dq_ref, dk_ref, dv_ref = refs[dirn][5:]
            sp, dsn = refs[dirn][4][h], dst[dirn, h]
            gc = jnp.exp(lg_ref[dirn, h] * CHUNK)
            dq_ref[:, sl] = (g1 * dec[dirn, h, 1] + dq2).astype(BF16)
            dk_ref[:, sl] = (dk2 + h1 * dec[dirn, h, 2]).astype(BF16)
            dv_ref[:, sl] = dv.astype(BF16)
            term = da * a * relf[dirn] + q * g1 * dec[dirn, h, 3] + k * h1 * dec[dirn, h, 4] + sp * dsn * (CHUNK * gc)
            dlg_ref[dirn * HEADS + h:dirn * HEADS + h + 1, 0:HEAD_DIM] += jnp.sum(term, axis=0, keepdims=True)
            dst[dirn, h] = gc * dsn + ds2

    wide = [pl.BlockSpec((CHUNK, RET_W), lambda s, row=row: (row(s), 0)) for row in rows]
    state = pl.BlockSpec((None, HEADS, CHUNK, HEAD_DIM), lambda s: (nc - 1 - s, 0, 0, 0))
    return _pcall_ride(
        body, ride, (lg,) + (qkv,) * 6 + (do, do, s_prev[0], s_prev[1]), name="ret_bwd", grid=(nc,),
        in_specs=[pl.BlockSpec(memory_space=pltpu.SMEM)] + _ret_in_specs(rows) + wide + [state, state],
        out_specs=(wide[0],) * 3 + (wide[1],) * 3 + (pl.BlockSpec((2 * HEADS, 8 * HEAD_DIM), lambda s: (0, 0)),),
        out_shape=(_sds((n, RET_W), BF16),) * 6 + (_sds((2 * HEADS, 8 * HEAD_DIM), F32),),
        scratch_shapes=[pltpu.VMEM((2, HEADS, CHUNK, HEAD_DIM), F32), pltpu.VMEM((2, HEADS, 5, CHUNK, CHUNK), F32),
                        pltpu.VMEM((2, CHUNK, CHUNK), F32)],
        compiler_params=_cp())


def _shift_rows(cur, prev8, next8, k, seg_start, seg_end):
    tm = cur.shape[0]
    rows = _rows_iota(cur.shape)
    if k < 0:
        out = pltpu.roll(cur, -k, 0)
        for j in range(-k):
            halo = jnp.where(seg_start, 0.0, prev8[SUB + k + j:SUB + k + j + 1, :])
            out = jnp.where(rows == j, halo, out)
    else:
        out = pltpu.roll(cur, tm - k, 0)
        for j in range(k):
            halo = jnp.where(seg_end, 0.0, next8[j:j + 1, :])
            out = jnp.where(rows == tm - k + j, halo, out)
    return out


def _seg_flags(t, cb, nb):
    return jnp.logical_or(t == 0, t == cb), jnp.logical_or(t == cb - 1, t == nb - 1)


def _halo_specs(tile_of, n_rows, col):
    per = TM // SUB
    return [pl.BlockSpec((TM, LRU_W), lambda s: (tile_of(s), col)),
            pl.BlockSpec((SUB, LRU_W), lambda s: (jnp.maximum(tile_of(s) * per - 1, 0), col)),
            pl.BlockSpec((SUB, LRU_W), lambda s: (jnp.minimum((tile_of(s) + 1) * per, n_rows // SUB - 1), col))]


def _lru_gates(xr, prev8, next8, seg_start, seg_end, cw_ref, cb_ref, wg_ref, bg_ref, sp_ref):
    xm1 = _shift_rows(xr, prev8, next8, -1, seg_start, seg_end)
    xp1 = _shift_rows(xr, prev8, next8, 1, seg_start, seg_end)
    xp2 = _shift_rows(xr, prev8, next8, 2, seg_start, seg_end)
    xc = cb_ref[...] + xm1 * cw_ref[0:1, :] + xr * cw_ref[1:2, :] + xp1 * cw_ref[2:3, :] + xp2 * cw_ref[3:4, :]
    pre = _dot(xc, wg_ref[...]) + bg_ref[...]
    r = _sigmoid(pre[:, :LRU_W])
    i = _sigmoid(pre[:, LRU_W:])
    la = (-LRU_C) * r * sp_ref[...]
    a = jnp.exp(la)
    th = jnp.tanh(la)
    sq = jnp.sqrt(-2.0 * th / (1.0 - th))
    return xc, r, i, a, sq


def _scan_tile(a, b, ascending, a_sc, b_sc, carry, out_ref):
    tm, w = a.shape
    nsub = tm // SUB
    a = a.reshape(nsub, SUB, w)
    b = b.reshape(nsub, SUB, w)
    r8 = lax.broadcasted_iota(jnp.int32, a.shape, 1)
    for k in (1, 2, 4):
        if ascending:
            m = r8 >= k
            a_s, b_s = pltpu.roll(a, k, 1), pltpu.roll(b, k, 1)
        else:
            m = r8 < SUB - k
            a_s, b_s = pltpu.roll(a, SUB - k, 1), pltpu.roll(b, SUB - k, 1)
        b = a * jnp.where(m, b_s, 0.0) + b
        a = a * jnp.where(m, a_s, 1.0)
    a_sc[...] = a.reshape(tm, w)
    b_sc[...] = b.reshape(tm, w)

    def step(j, c):
        off = pl.multiple_of((j if ascending else nsub - 1 - j) * SUB, SUB)
        hb = a_sc[pl.ds(off, SUB), :] * c + b_sc[pl.ds(off, SUB), :]
        out_ref[pl.ds(off, SUB), :] = hb
        last = hb[SUB - 1:SUB, :] if ascending else hb[0:1, :]
        return jnp.broadcast_to(last, c.shape)

    carry[...] = lax.fori_loop(0, nsub, step, carry[...], unroll=8)


def _lru_fwd(p, wg, bg, sp, cw, cbias, dirn, cb, ride=None):
    n = p.shape[0]
    nb = n // TM
    tile_of = lambda s: _tile_order(dirn, s, cb, nb)

    def body(x_ref, xp_ref, xn_ref, wg_ref, bg_ref, sp_ref, cw_ref, cb_ref, h_ref, cin_ref, carry, a_sc, b_sc):
        s = pl.program_id(0)

        @pl.when(s == 0)
        def _():
            carry[...] = jnp.zeros_like(carry)

        seg_start, seg_end = _seg_flags(tile_of(s), cb, nb)
        xc, r, i, a, sq = _lru_gates(x_ref[...], xp_ref[...], xn_ref[...], seg_start, seg_end,
                                     cw_ref, cb_ref, wg_ref, bg_ref, sp_ref)
        cin_ref[...] = carry[...]
        _scan_tile(a, sq * (i * xc), dirn == 0, a_sc, b_sc, carry, h_ref)

    full = lambda shape: pl.BlockSpec(shape, lambda s: (0,) * len(shape))
    return _pcall_ride(
        body, ride, (p, p, p, wg, bg, sp, cw, cbias), name=f"lru_fwd{dirn}", grid=(nb,),
        in_specs=_halo_specs(tile_of, n, COL_XR) + [full((LRU_W, 2 * LRU_W)), full((1, 2 * LRU_W)), full((1, LRU_W)),
                                               full((4, LRU_W)), full((1, LRU_W))],
        out_specs=(pl.BlockSpec((TM, LRU_W), lambda s: (tile_of(s), 0)),
                   pl.BlockSpec((None, SUB, LRU_W), lambda s: (tile_of(s), 0, 0))),
        out_shape=(_sds((n, LRU_W), F32), _sds((nb, SUB, LRU_W), F32)),
        scratch_shapes=[pltpu.VMEM((SUB, LRU_W), F32), pltpu.VMEM((TM, LRU_W), F32), pltpu.VMEM((TM, LRU_W), F32)],
        compiler_params=_cp())


def _lru_bwd(p, wg, bg, sp, cw, cbias, h, cin, dhs, dirn, cb, ride=None):
    n = p.shape[0]
    nb = n // TM
    tile_of = lambda s: _tile_order(dirn, nb - 1 - s, cb, nb)

    def body(x_ref, xp_ref, xn_ref, wg_ref, bg_ref, sp_ref, cw_ref, cb_ref, h_ref, cin_ref, dhs_ref,
             dxc_ref, dwd_ref, acc_ref, carry, a_sc, b_sc, mu_sc, dwg_ref):
        s = pl.program_id(0)

        @pl.when(s == 0)
        def _():
            carry[...] = jnp.zeros_like(carry)
            dwg_ref[...] = jnp.zeros_like(dwg_ref)
            acc_ref[...] = jnp.zeros_like(acc_ref)

        seg_start, seg_end = _seg_flags(tile_of(s), cb, nb)
        xc, r, i, a, sq = _lru_gates(x_ref[...], xp_ref[...], xn_ref[...], seg_start, seg_end,
                                     cw_ref, cb_ref, wg_ref, bg_ref, sp_ref)
        rows = _rows_iota(a.shape)
        hv = h_ref[...]
        dh = dhs_ref[...]
        mu_next = carry[0:1, :]
        _scan_tile(a, a * dh, dirn == 1, a_sc, b_sc, carry, mu_sc)
        mu = mu_sc[...]
        if dirn == 0:
            hprev = jnp.where(rows == 0, cin_ref[0:1, :], pltpu.roll(hv, 1, 0))
            lam = dh + jnp.where(rows == TM - 1, mu_next, pltpu.roll(mu, TM - 1, 0))
        else:
            hprev = jnp.where(rows == TM - 1, cin_ref[0:1, :], pltpu.roll(hv, TM - 1, 0))
            lam = dh + jnp.where(rows == 0, mu_next, pltpu.roll(mu, 1, 0))
        ds = lam * (i * xc)
        di = lam * (sq * xc)
        dla = lam * hprev * a - ds * (a * a) / jnp.maximum(sq, 1e-20)
        dpr = dla * ((-LRU_C) * sp_ref[...]) * r * (1.0 - r)
        dpi = di * i * (1.0 - i)
        dpre = jnp.concatenate([dpr, dpi], axis=1)
        dxc_ref[...] = lam * (sq * i) + _dot_nt(dpre, wg_ref[...])
        dwg_ref[...] += _dot_tn(xc, dpre)
        acc_ref[0:1, :] += jnp.sum(dpre, axis=0, keepdims=True)
        acc_ref[1:2, 0:LRU_W] += jnp.sum(dla * ((-LRU_C) * r), axis=0, keepdims=True)

        @pl.when(s == nb - 1)
        def _():
            low = lax.broadcasted_iota(jnp.int32, (LRU_BD, 2 * LRU_BD), 1) < LRU_BD
            for half in (0, LRU_W):
                for m in range(LRU_BLOCKS // 2):
                    lanes = slice(half + 2 * LRU_BD * m, half + 2 * LRU_BD * (m + 1))
                    even = dwg_ref[2 * m * LRU_BD:(2 * m + 1) * LRU_BD, lanes]
                    odd = dwg_ref[(2 * m + 1) * LRU_BD:(2 * m + 2) * LRU_BD, lanes]
                    dwd_ref[:, lanes] = jnp.where(low, even, odd)

    full = lambda shape: pl.BlockSpec(shape, lambda s: (0,) * len(shape))
    tile = pl.BlockSpec((TM, LRU_W), lambda s: (tile_of(s), 0))
    return _pcall_ride(
        body, ride, (p, p, p, wg, bg, sp, cw, cbias, h, cin, dhs), name=f"lru_bwd{dirn}", grid=(nb,),
        in_specs=_halo_specs(tile_of, n, COL_XR) + [full((LRU_W, 2 * LRU_W)), full((1, 2 * LRU_W)), full((1, LRU_W)),
                                               full((4, LRU_W)), full((1, LRU_W)), tile,
                                               pl.BlockSpec((None, SUB, LRU_W), lambda s: (tile_of(s), 0, 0)), tile],
        out_specs=(tile, full((LRU_BD, 2 * LRU_W)), full((8, 2 * LRU_W))),
        out_shape=(_sds((n, LRU_W), F32), _sds((LRU_BD, 2 * LRU_W), F32), _sds((8, 2 * LRU_W), F32)),
        scratch_shapes=[pltpu.VMEM((SUB, LRU_W), F32)] + [pltpu.VMEM((TM, LRU_W), F32)] * 3
        + [pltpu.VMEM((LRU_W, 2 * LRU_W), F32)],
        compiler_params=_cp())


def _assemble_dp(dqs, dks, dvs, dg, dgate, dxcs, p, cw, cosf, sins, cb, ride=None):
    n = p.shape[0]
    nb = n // TM
    tile_of = lambda s: s

    def body(dqf, dqb, dkf, dkb, dvf, dvb, dg_ref, dgate_ref, cf, pf, nf, cb_, pb, nb_, x_ref, xp_ref, xn_ref,
             cw_ref, cos_ref, sin_ref, dp_ref, acc_ref):
        s = pl.program_id(0)

        @pl.when(s == 0)
        def _():
            acc_ref[...] = jnp.zeros_like(acc_ref)

        seg_start, seg_end = _seg_flags(s, cb, nb)
        dq = dqf[...].astype(F32) + dqb[...].astype(F32)
        dk = dkf[...].astype(F32) + dkb[...].astype(F32)
        cosv, sinv = cos_ref[...], sin_ref[...]
        for h in range(HEADS):
            sl = slice(h * HEAD_DIM, (h + 1) * HEAD_DIM)
            sk = slice(RET_W + h * HEAD_DIM, RET_W + (h + 1) * HEAD_DIM)
            dp_ref[:, sl] = (dq[:, sl] * cosv + pltpu.roll(dq[:, sl] * sinv, HEAD_DIM // 2, 1)).astype(BF16)
            dp_ref[:, sk] = ((dk[:, sl] * cosv + pltpu.roll(dk[:, sl] * sinv, HEAD_DIM // 2, 1)) * K_SCALE).astype(BF16)
        dp_ref[:, 2 * RET_W:3 * RET_W] = (dvf[...].astype(F32) + dvb[...].astype(F32)).astype(BF16)
        dp_ref[:, 3 * RET_W:4 * RET_W] = dg_ref[...].astype(BF16)
        dxc = cf[...] + cb_[...]
        dprev = pf[...] + pb[...]
        dnext = nf[...] + nb_[...]
        dxr = (_shift_rows(dxc, dprev, dnext, 1, seg_start, seg_end) * cw_ref[0:1, :] + dxc * cw_ref[1:2, :]
               + _shift_rows(dxc, dprev, dnext, -1, seg_start, seg_end) * cw_ref[2:3, :]
               + _shift_rows(dxc, dprev, dnext, -2, seg_start, seg_end) * cw_ref[3:4, :])
        dp_ref[:, 4 * RET_W:4 * RET_W + LRU_W] = dxr.astype(BF16)
        dp_ref[:, 4 * RET_W + LRU_W:] = dgate_ref[...].astype(BF16)
        xr, xp, xn = x_ref[...], xp_ref[...], xn_ref[...]
        for j, k in enumerate((-1, 0, 1, 2)):
            xs = xr if k == 0 else _shift_rows(xr, xp, xn, k, seg_start, seg_end)
            acc_ref[j:j + 1, 0:LRU_W] += jnp.sum(dxc * xs, axis=0, keepdims=True)
        acc_ref[4:5, 0:LRU_W] += jnp.sum(dxc, axis=0, keepdims=True)

    t = pl.BlockSpec((TM, RET_W), lambda s: (s, 0))
    args = (dqs[0], dqs[1], dks[0], dks[1], dvs[0], dvs[1], dg, dgate, dxcs[0], dxcs[0], dxcs[0], dxcs[1], dxcs[1], dxcs[1],
            p, p, p, cw, cosf, sins)
    tab = pl.BlockSpec((TM, HEAD_DIM), lambda s: (s, 0))
    return _pcall_ride(
        body, ride, args, name="assemble_dp", grid=(nb,),
        in_specs=[t] * 8 + _halo_specs(tile_of, n, 0) * 2 + _halo_specs(tile_of, n, COL_XR)
        + [pl.BlockSpec((4, LRU_W), lambda s: (0, 0)), tab, tab],
        out_specs=(pl.BlockSpec((TM, 4 * RET_W + 2 * LRU_W), lambda s: (s, 0)), pl.BlockSpec((8, 2 * LRU_W), lambda s: (0, 0))),
        out_shape=(_sds((n, 4 * RET_W + 2 * LRU_W), BF16), _sds((8, 2 * LRU_W), F32)), compiler_params=_cp())


def _adamw(g, w, m, v):
    nm = ADAM_B1 * m + (1.0 - ADAM_B1) * g
    nv = ADAM_B2 * v + (1.0 - ADAM_B2) * (g * g)
    m_hat = nm / (1.0 - ADAM_B1 ** ADAM_STEP)
    v_hat = nv / (1.0 - ADAM_B2 ** ADAM_STEP)
    return (-ADAM_LR) * (m_hat / (jnp.sqrt(v_hat) + ADAM_EPS) + ADAM_WD * w), nm, nv


def _adam_many(items, name):
    n = len(items)

    def body(*refs):
        for i in range(n):
            g, w, m, v = (r[...] for r in refs[4 * i:4 * i + 4])
            for o_ref, val in zip(refs[4 * n + 3 * i:4 * n + 3 * i + 3], _adamw(g, w, m, v)):
                o_ref[...] = val

    out_shape = tuple(_sds(it[1].shape, F32) for it in items for _ in range(3))
    res = _pcall(body, name=name, out_shape=out_shape, compiler_params=_cp())(*[a for it in items for a in it])
    return [tuple(res[3 * i:3 * i + 3]) for i in range(n)]


def _sum_adam(parts_list, w, m, v, name):
    nparts, _, c = parts_list[0].shape
    r = w.shape[0]
    tr = min(parts_list[0].shape[1], 128)
    starts, o = [], 0
    for pa in parts_list:
        starts.append(o)
        o += pa.shape[1] // tr
    nseg = len(parts_list)

    def body(*refs):
        p_refs = refs[:nseg]
        w_ref, m_ref, v_ref, g_ref, d_ref, nm_ref, nv_ref = refs[nseg:]
        i = pl.program_id(0)
        for s, p_ref in enumerate(p_refs):
            end = starts[s + 1] if s + 1 < nseg else r // tr

            @pl.when(jnp.logical_and(i >= starts[s], i < end))
            def _():
                g = p_ref[0].astype(F32)
                for j in range(1, nparts):
                    g = g + p_ref[j].astype(F32)
                g_ref[...] = g
                d_ref[...], nm_ref[...], nv_ref[...] = _adamw(g, w_ref[...], m_ref[...], v_ref[...])

    def seg_spec(s):
        last = parts_list[s].shape[1] // tr - 1
        return pl.BlockSpec((nparts, tr, c), lambda i: (0, jnp.clip(i - starts[s], 0, last), 0))

    t = pl.BlockSpec((tr, c), lambda i: (i, 0))
    return _pcall(
        body, name=name, grid=(r // tr,),
        in_specs=[seg_spec(s) for s in range(nseg)] + [t, t, t],
        out_specs=(t, t, t, t), out_shape=(_sds((r, c), F32),) * 4, compiler_params=_cp(),
    )(*parts_list, w, m, v)


def _sum_parts(parts, name):
    nparts, r, c = parts.shape

    def body(p_ref, o_ref):
        g = p_ref[0]
        for j in range(1, nparts):
            g = g + p_ref[j]
        o_ref[...] = g

    return _pcall(body, name=name, out_shape=_sds((r, c), parts.dtype), compiler_params=_cp())(parts)


def _rot_tables(l_len, t_len):
    rows = t_len // GRID_W
    n_freq = HEAD_DIM // 4
    inv = ROPE_BASE ** (-jnp.arange(n_freq, dtype=F32) / n_freq)
    ang_r = jnp.arange(rows, dtype=F32)[:, None] * inv
    ang_c = jnp.arange(GRID_W, dtype=F32)[:, None] * inv
    cos = jnp.concatenate([jnp.repeat(jnp.cos(ang_r), GRID_W, axis=0), jnp.tile(jnp.cos(ang_c), (rows, 1))], axis=-1)
    sin = jnp.concatenate([jnp.repeat(jnp.sin(ang_r), GRID_W, axis=0), jnp.tile(jnp.sin(ang_c), (rows, 1))], axis=-1)
    cosf = jnp.concatenate([jnp.ones((l_len, HEAD_DIM), F32), jnp.concatenate([cos, cos], axis=-1)], axis=0)
    sins = jnp.concatenate([jnp.zeros((l_len, HEAD_DIM), F32), jnp.concatenate([-sin, sin], axis=-1)], axis=0)
    return cosf, sins


def _block_diag(w):
    eye = jnp.eye(LRU_BLOCKS, dtype=w.dtype)
    return (w[:, :, None, :] * eye[:, None, :, None]).reshape(LRU_W, LRU_W)


def _blocks_from_lanes(dwd_half):
    return dwd_half.reshape(LRU_BD, LRU_BLOCKS, LRU_BD).transpose(1, 0, 2)


def _silu(x):
    return x * jax.nn.sigmoid(x)


def kernel(x, c, ctx, c_ctx, w_ada, b_ada, norm1_g, norm2_g, w_in, ret_decay, conv_w, conv_b, lru_wa, lru_ba, lru_wx, lru_bx, lru_lambda, w_out, w_mlp1, w_mlp2, final_g, loss_target, m_c_ctx, m_w_ada, m_b_ada, m_norm1_g, m_norm2_g, m_w_in, m_ret_decay, m_conv_w, m_conv_b, m_lru_wa, m_lru_ba, m_lru_wx, m_lru_bx, m_lru_lambda, m_w_out, m_w_mlp1, m_w_mlp2, m_final_g, v_c_ctx, v_w_ada, v_b_ada, v_norm1_g, v_norm2_g, v_w_in, v_ret_decay, v_conv_w, v_conv_b, v_lru_wa, v_lru_ba, v_lru_wx, v_lru_bx, v_lru_lambda, v_w_out, v_w_mlp1, v_w_mlp2, v_final_g):
    t_len, d = x.shape[1], x.shape[2]
    l_len = ctx.shape[1]
    cb, cc = l_len // TM, l_len // CHUNK
    me = 4 * lax.axis_index("x") + 2 * lax.axis_index("y") + lax.axis_index("c")
    x2d, ctx2d, tgt2d = x[0], ctx[0], loss_target[0]
    ada_cols = w_ada.shape[2]
    wa2d = w_ada[0]

    sc_loc = conv_w.shape[2]
    pack_a = jnp.zeros((8, d), F32)
    pack_a = pack_a.at[0].set(_silu(c[0]))
    pack_a = pack_a.at[1, :4 * sc_loc].set(conv_w[0].reshape(-1))
    pack_a = pack_a.at[2, :2 * sc_loc].set(lru_ba[0].reshape(-1))
    pack_a = pack_a.at[3, :2 * sc_loc].set(lru_bx[0].reshape(-1))
    pack_a = pack_a.at[4, :2 * sc_loc].set(lru_lambda[0].reshape(-1))
    all_a = _all_gather_small(pack_a, "gather_small_in")
    s16 = jnp.zeros((16, d), F32).at[0:8].set(all_a[:, 0, :]).at[8].set(_silu(c_ctx))

    def unshard(row, k):
        return all_a[:, row, :k * sc_loc].reshape(N_DEV, k, sc_loc).transpose(1, 0, 2).reshape(k, N_DEV * sc_loc)

    conv_w_full = unshard(1, 4)
    ba_full, bx_full, lam_full = unshard(2, 2), unshard(3, 2), unshard(4, 2)

    b_cols = lax.dynamic_slice(b_ada, (0, me * ada_cols), (1, ada_cols))
    mod_parts = _all_gather_small(_mod_part(s16, wa2d, b_cols), "gather_mod")
    mod_all = mod_parts.transpose(1, 0, 2).reshape(16, N_DEV * ada_cols)
    mod_me = lax.dynamic_slice(mod_all, (me, 0), (1, 6 * d)).reshape(6, d)
    mod_c = mod_all[8].reshape(6, d)
    modrows = jnp.concatenate([mod_c[0:2], mod_me], axis=0)

    lg = jax.nn.log_sigmoid(ret_decay[0])
    sp = jax.nn.softplus(-lam_full)
    wg = [jnp.concatenate([_block_diag(lru_wa[0, dd]), _block_diag(lru_wx[0, dd])], axis=1).astype(BF16) for dd in (0, 1)]
    bg = [jnp.concatenate([ba_full[dd], bx_full[dd]])[None, :] for dd in (0, 1)]
    cosf, sins = _rot_tables(l_len, t_len)

    (hn, hnt), win_g = _norm1_fwd(ctx2d, x2d, norm1_g, modrows, cb, ride=("gather", w_in[0].astype(BF16)))
    (qkv, p), w1_g = _mm_in(hn, win_g, cosf, sins, ride=("gather", w_mlp1[0].astype(BF16)))
    (o0, o1, sp0, sp1), wout_g = _ret_fwd(qkv, lg, cc, ride=("gather", w_out[0].astype(BF16)))
    wout_g = wout_g.reshape(1, d, d)
    o, s_prev = [o0, o1], [sp0, sp1]
    (h0, cin0), _ = _lru_fwd(p, wg[0], bg[0], sp[0:1], conv_w_full, conv_b, 0, cb)
    (h1, cin1), _ = _lru_fwd(p, wg[1], bg[1], sp[1:2], conv_w_full, conv_b, 1, cb)
    h, cin = [h0, h1], [cin0, cin1]
    mix = _mix_fwd(o[0], o[1], p, h[0], h[1], cb, t_len)
    y, x1, h2, h2t = _mm_nn(mix, wout_g, F32, "mm_out_norm2", tm=TL, epi=_res_norm2_epilogue(x2d, norm2_g, modrows, TL))
    r, w2_g = _mm_nn(h2, w1_g, BF16, "mm_mlp1", relu_out=True, tm=TL, ride=("gather", w_mlp2[0].astype(BF16)))
    w2_g = w2_g.reshape(1, 4 * d, d)
    dx2, dz, dzt, facc = _mm_nn(r, w2_g, F32, "mm_mlp2_final", square_lhs=True, tm=TL, vmem_mb=58,
                                epi=_final_epilogue(x1, tgt2d, final_g[None, :], modrows, TL))

    du = _mm_nt(dz, w2_g, BF16, "mm_da2", relu_mul=r, tm=TL, vmem_mb=58)
    gw2_lo, gw2_hi = _mm_wgrad(dzt, r, "mm_dw2", w2_g.shape[1] // N_DEV, BF16, transpose_out=True, square_rhs=True,
                               halves=True)
    (gw1_lo, gw1_hi), gw2_lo_all = _mm_wgrad(h2t, du, "mm_dw1", w1_g.shape[2], BF16, halves=True,
                                             ride=("a2a", gw2_lo))
    (dx1, dy, n2acc), gw2_hi_all = _mm_nt(du, w1_g, F32, "mm_dh2_norm2", ride=("a2a", gw2_hi), tm=TL, vmem_mb=60,
                                          epi=_bwd_norm2_epilogue(x1, dx2, y, norm2_g, modrows, TL))
    gwo = _mm_tn(mix, dy, "mm_dwout", False, d, BF16, 512).reshape(N_DEV, -1, d)
    (do, dg, dgate, dhs), gw1_lo_all = _mm_nt(dy, wout_g, F32, "mm_dmix_mix", ride=("a2a", gw1_lo),
                                              epi=_mix_bwd_epilogue(o[0], o[1], p, h[0], h[1], cb))
    dxcs, dwgs, laccs, rides = [], [], [], [("a2a", gwo), None]
    (dq0, dk0, dv0, dq1, dk1, dv1, dlg_lanes), gw1_hi_all = _ret_bwd(qkv, lg, do, s_prev, cc, ride=("a2a", gw1_hi))
    gw1_all, gw2_all = [gw1_lo_all, gw1_hi_all], [gw2_lo_all, gw2_hi_all]
    dqs, dks, dvs = [dq0, dq1], [dk0, dk1], [dv0, dv1]
    for dd in (0, 1):
        (dxc_, dwg_, lacc_), got_ = _lru_bwd(p, wg[dd], bg[dd], sp[dd:dd + 1], conv_w_full, conv_b, h[dd], cin[dd], dhs, dd, cb,
                                             ride=rides[dd])
        dxcs.append(dxc_); dwgs.append(dwg_); laccs.append(lacc_)
        gwo_all = got_ if dd == 0 else gwo_all
    (dp, cacc), _ = _assemble_dp(dqs, dks, dvs, dg, dgate, dxcs, p, conv_w_full, cosf, sins, cb)
    pack_b = jnp.concatenate([n2acc, facc, cacc, laccs[0], laccs[1], dlg_lanes, dwgs[0], dwgs[1]], axis=0)
    gwi, all_b = _mm_wgrad(hnt, dp, "mm_dwin", win_g.shape[2], BF16, ride=("gather", pack_b))
    (grad_x, n1acc), gwi_all = _mm_nt(dp, win_g, F32, "mm_dhn_norm1", ride=("a2a", gwi),
                                      epi=_bwd_norm1_epilogue(ctx2d, x2d, dx1, norm1_g, modrows, cb))
    all_n1 = _all_gather_small(n1acc, "gather_norm1_grads")
    tot = _sum_parts(all_b, "sum_small_grads")
    t_n1 = _sum_parts(all_n1, "sum_norm1_grads")
    t_n2, t_f, t_conv, t_dlg = tot[0:8], tot[8:16], tot[16:24, :LRU_W], tot[40:48, :HEAD_DIM]
    t_l = [tot[24:32], tot[32:40]]
    t_dwd = [tot[48:48 + LRU_BD], tot[48 + LRU_BD:48 + 2 * LRU_BD]]
    loss = (0.5 / d) * jnp.sum(t_f[2])
    t_wa = jnp.stack([_blocks_from_lanes(t_dwd[dd][:, :LRU_W]) for dd in (0, 1)])
    t_wx = jnp.stack([_blocks_from_lanes(t_dwd[dd][:, LRU_W:]) for dd in (0, 1)])
    t_ba = jnp.stack([t_l[dd][0, :LRU_W] for dd in (0, 1)])
    t_bx = jnp.stack([t_l[dd][0, LRU_W:] for dd in (0, 1)])
    t_sp = jnp.stack([t_l[dd][1, :LRU_W] for dd in (0, 1)])
    dm_rows = jnp.stack([all_n1[:, 2, :], all_n1[:, 3, :], all_b[:, 3, :], all_b[:, 0, :], all_b[:, 1, :], all_b[:, 9, :]],
                        axis=1).reshape(N_DEV, 6 * d)
    dm_c = jnp.concatenate([t_n1[0], t_n1[1], jnp.zeros((4 * d,), F32)])
    dm16 = jnp.zeros((16, 6 * d), F32).at[0:8].set(dm_rows).at[8].set(dm_c)
    g_b_ada = jnp.sum(dm16, axis=0)[None, :]
    dm_cols = lax.dynamic_slice(dm16, (0, me * ada_cols), (16, ada_cols))
    g_w_ada, ds16 = _ada_bwd(s16, dm_cols, wa2d)
    ds_all = _all_gather_small(ds16[8:16], "gather_dsilu")
    dsilu_cc = _sum_parts(ds_all, "sum_dsilu")[0]
    sg_cc = jax.nn.sigmoid(c_ctx)
    g_c_ctx = dsilu_cc * (sg_cc * (1.0 + c_ctx * (1.0 - sg_cc)))

    g_ret_decay = jnp.sum(t_dlg, axis=-1).reshape(2, HEADS) * jax.nn.sigmoid(-ret_decay[0])
    g_lambda_full = -t_sp * jax.nn.sigmoid(-lam_full)

    def my_cols(full):
        return lax.dynamic_slice(full, (0, me * sc_loc), (full.shape[0], sc_loc))

    small_g = dict(
        c_ctx=g_c_ctx[None], b_ada=g_b_ada, norm1_g=t_n1[4:5], norm2_g=t_n2[2:3], ret_decay=g_ret_decay,
        conv_w=my_cols(t_conv[0:4]), conv_b=t_conv[4:5], lru_wa=t_wa.reshape(-1, LRU_BD), lru_ba=my_cols(t_ba),
        lru_wx=t_wx.reshape(-1, LRU_BD), lru_bx=my_cols(t_bx), lru_lambda=my_cols(g_lambda_full), final_g=t_f[0:1])
    small = dict(
        c_ctx=(c_ctx, m_c_ctx, v_c_ctx), b_ada=(b_ada, m_b_ada, v_b_ada), norm1_g=(norm1_g, m_norm1_g, v_norm1_g),
        norm2_g=(norm2_g, m_norm2_g, v_norm2_g), ret_decay=(ret_decay, m_ret_decay, v_ret_decay),
        conv_w=(conv_w, m_conv_w, v_conv_w), conv_b=(conv_b, m_conv_b, v_conv_b), lru_wa=(lru_wa, m_lru_wa, v_lru_wa),
        lru_ba=(lru_ba, m_lru_ba, v_lru_ba), lru_wx=(lru_wx, m_lru_wx, v_lru_wx), lru_bx=(lru_bx, m_lru_bx, v_lru_bx),
        lru_lambda=(lru_lambda, m_lru_lambda, v_lru_lambda), final_g=(final_g, m_final_g, v_final_g))
    names = list(small)
    items = [(small_g[k],) + tuple(a.reshape(small_g[k].shape) for a in small[k]) for k in names]
    res = {}
    for k, it, (d_, m_, v_) in zip(names, items, _adam_many(items, "adam_small")):
        shape = small[k][0].shape
        res[k] = tuple(a.reshape(shape) for a in (it[0], d_, m_, v_))

    def big(parts, w, m, v, name):
        out = _sum_adam(parts, w[0], m[0], v[0], name)
        return tuple(a[None] for a in out)

    res["w_ada"] = big([g_w_ada[None]], w_ada, m_w_ada, v_w_ada, "adam_w_ada")
    res["w_in"] = big([gwi_all], w_in, m_w_in, v_w_in, "adam_w_in")
    res["w_out"] = big([gwo_all], w_out, m_w_out, v_w_out, "adam_w_out")
    res["w_mlp1"] = big(gw1_all, w_mlp1, m_w_mlp1, v_w_mlp1, "adam_w_mlp1")
    res["w_mlp2"] = big(gw2_all, w_mlp2, m_w_mlp2, v_w_mlp2, "adam_w_mlp2")

    order = ["c_ctx", "w_ada", "b_ada", "norm1_g", "norm2_g", "w_in", "ret_decay", "conv_w", "conv_b", "lru_wa", "lru_ba",
             "lru_wx", "lru_bx", "lru_lambda", "w_out", "w_mlp1", "w_mlp2", "final_g"]
    outs = [loss, grad_x[None]]
    for j in range(4):
        outs += [res[k][j] for k in order]
    return tuple(outs)
```

```python
import jax
import jax.numpy as jnp
from jax import lax
from jax.experimental import pallas as pl
from jax.experimental.pallas import tpu as pltpu

F32 = jnp.float32
BF16 = jnp.bfloat16
AXES = ("x", "y", "c")
N_DEV = 8
MESH = pl.DeviceIdType.MESH

HEADS = 4
HEAD_DIM = 128
CHUNK = 128
RET_W = HEADS * HEAD_DIM
LRU_W = 512
LRU_BLOCKS = 8
LRU_BD = LRU_W // LRU_BLOCKS
LRU_C = 8.0
EPS = 1e-6
K_SCALE = HEAD_DIM ** -0.5
ROPE_BASE = 10000.0
GRID_W = 64
TM = 256
TL = 512
SUB = 8

ADAM_LR = 0.001
ADAM_B1 = 0.9
ADAM_B2 = 0.999
ADAM_EPS = 1e-08
ADAM_WD = 0.01
ADAM_STEP = 10

COL_G, COL_XR, COL_GATE = 0, 1, 2

R_CSH1, R_CSC1, R_SH1, R_SC1, R_G1, R_SH2, R_SC2, R_G2 = range(8)


def _pcall(body, **kw):
    return pl.pallas_call(body, **kw)


def _cp(vmem_mb=48):
    return pltpu.CompilerParams(vmem_limit_bytes=vmem_mb << 20)


def _sds(shape, dtype):
    return jax.ShapeDtypeStruct(shape, dtype)


def _dot(a, b):
    return jnp.dot(a.astype(BF16), b.astype(BF16), preferred_element_type=F32)


def _dot_nt(a, b):
    return lax.dot_general(a.astype(BF16), b.astype(BF16), (((1,), (1,)), ((), ())), preferred_element_type=F32)


def _dot_tn(a, b):
    return lax.dot_general(a.astype(BF16), b.astype(BF16), (((0,), (0,)), ((), ())), preferred_element_type=F32)


def _sigmoid(x):
    return 0.5 * jnp.tanh(0.5 * x) + 0.5


def _gelu(x):
    return 0.5 * x * (1.0 + jnp.tanh(0.7978845608028654 * (x + 0.044715 * x * x * x)))


def _dgelu(x):
    t = jnp.tanh(0.7978845608028654 * (x + 0.044715 * x * x * x))
    return 0.5 * (1.0 + t) + 0.5 * x * (1.0 - t * t) * 0.7978845608028654 * (1.0 + 3.0 * 0.044715 * x * x)


def _rows_iota(shape):
    return lax.broadcasted_iota(jnp.int32, shape, 0)


def _tile_order(dirn, s, cb, nb):
    if dirn == 0:
        return s
    return jnp.where(s < cb, cb - 1 - s, nb - 1 - (s - cb))


_SEMS = [pltpu.SemaphoreType.DMA((7,)), pltpu.SemaphoreType.DMA((7,)), pltpu.SemaphoreType.DMA(())]
_ANY = pl.BlockSpec(memory_space=pl.ANY)


def _gather_copies(x_ref, out_ref, send_sems, recv_sems, local_sem):
    mx, my, mc = lax.axis_index("x"), lax.axis_index("y"), lax.axis_index("c")
    me, sibling = (mx, my, mc), (mx, my, 1 - mc)
    chips = [(1 - mx, my), (mx, 1 - my), (1 - mx, 1 - my)]

    def slot(px, py, pc):
        return out_ref.at[4 * px + 2 * py + pc]

    def copy(k, block, to, src=None):
        return pltpu.make_async_remote_copy(
            src_ref=slot(*block) if src is None else src, dst_ref=slot(*block),
            send_sem=send_sems.at[k], recv_sem=recv_sems.at[k], device_id=to, device_id_type=MESH)

    mine = pltpu.make_async_copy(x_ref, slot(*me), local_sem)
    first = [copy(0, me, sibling, src=x_ref)] + [copy(1 + j, me, (*chip, mc), src=x_ref) for j, chip in enumerate(chips)]
    passed = [copy(4 + j, (*chip, mc), sibling) for j, chip in enumerate(chips)]
    recv_ici = [copy(1 + j, (*chip, mc), me) for j, chip in enumerate(chips)]
    recv_d2d = [copy(0, sibling, me)] + [copy(4 + j, (*chip, 1 - mc), me) for j, chip in enumerate(chips)]
    return mine, first, passed, recv_ici, recv_d2d


def _gather_start(*refs):
    mine, first, _, _, _ = _gather_copies(*refs)
    mine.start()
    for cp in first:
        cp.start()


def _gather_finish(*refs):
    mine, first, passed, recv_ici, recv_d2d = _gather_copies(*refs)
    for landed, onward in zip(recv_ici, passed):
        landed.wait_recv()
        onward.start()
    for landed in recv_d2d:
        landed.wait_recv()
    for cp in first + passed:
        cp.wait_send()
    mine.wait()


def _a2a_copies(g_ref, out_ref, send_sems, recv_sems, local_sem):
    mx, my, mc = lax.axis_index("x"), lax.axis_index("y"), lax.axis_index("c")
    me = 4 * mx + 2 * my + mc
    mine = pltpu.make_async_copy(g_ref.at[me], out_ref.at[me], local_sem)
    copies = []
    for k in range(1, N_DEV):
        px = 1 - mx if (k >> 2) & 1 else mx
        py = 1 - my if (k >> 1) & 1 else my
        pc = 1 - mc if k & 1 else mc
        copies.append(pltpu.make_async_remote_copy(
            src_ref=g_ref.at[4 * px + 2 * py + pc], dst_ref=out_ref.at[me],
            send_sem=send_sems.at[k - 1], recv_sem=recv_sems.at[k - 1],
            device_id=(px, py, pc), device_id_type=MESH))
    return mine, copies


def _a2a_start(*refs):
    mine, copies = _a2a_copies(*refs)
    mine.start()
    for cp in copies:
        cp.start()


def _a2a_finish(*refs):
    mine, copies = _a2a_copies(*refs)
    for cp in copies:
        cp.wait()
    mine.wait()


_EXCHANGES = {"gather": (_gather_start, _gather_finish), "a2a": (_a2a_start, _a2a_finish)}


def _exchange_shape(kind, src):
    return _sds((N_DEV,) + src.shape if kind == "gather" else src.shape, src.dtype)


def _all_gather_small(x, name):
    def body(x_ref, out_ref, send_sems, recv_sems, local_sem):
        mx, my, mc = lax.axis_index("x"), lax.axis_index("y"), lax.axis_index("c")
        me = 4 * mx + 2 * my + mc
        mine = pltpu.make_async_copy(x_ref, out_ref.at[me], local_sem)
        mine.start()
        copies = []
        for k in range(1, N_DEV):
            peer = (1 - mx if (k >> 2) & 1 else mx, 1 - my if (k >> 1) & 1 else my, 1 - mc if k & 1 else mc)
            copies.append(pltpu.make_async_remote_copy(
                src_ref=x_ref, dst_ref=out_ref.at[me], send_sem=send_sems.at[k - 1], recv_sem=recv_sems.at[k - 1],
                device_id=peer, device_id_type=MESH))
            copies[-1].start()
        for cp in copies:
            cp.wait()
        mine.wait()

    return _pcall(body, name=name, out_shape=_exchange_shape("gather", x), in_specs=[_ANY], out_specs=_ANY,
                  scratch_shapes=list(_SEMS))(x)


def _pcall_ride(body, ride, args, *, name, grid, in_specs, out_specs, out_shape, scratch_shapes=(), compiler_params=None):
    if ride is None:
        out = _pcall(body, name=name, grid=grid, in_specs=in_specs, out_specs=out_specs, out_shape=out_shape,
                     scratch_shapes=list(scratch_shapes), compiler_params=compiler_params)(*args)
        return out, None
    kind, src = ride
    start, finish = _EXCHANGES[kind]
    single = not isinstance(out_shape, (tuple, list))
    out_specs_t = (out_specs,) if single else tuple(out_specs)
    out_shape_t = (out_shape,) if single else tuple(out_shape)
    n_in, n_out, n_sc = len(in_specs), len(out_shape_t), len(scratch_shapes)

    def wrapped(*refs):
        ins, src_ref = refs[:n_in], refs[n_in]
        outs, dst_ref = refs[n_in + 1:n_in + 1 + n_out], refs[n_in + 1 + n_out]
        scratch = refs[n_in + 2 + n_out:n_in + 2 + n_out + n_sc]
        sems = refs[n_in + 2 + n_out + n_sc:]
        first = pl.program_id(0) == 0
        last = pl.program_id(0) == grid[0] - 1
        for ax in range(1, len(grid)):
            first = jnp.logical_and(first, pl.program_id(ax) == 0)
            last = jnp.logical_and(last, pl.program_id(ax) == grid[ax] - 1)

        @pl.when(first)
        def _():
            start(src_ref, dst_ref, *sems)

        body(*ins, *outs, *scratch)

        @pl.when(last)
        def _():
            finish(src_ref, dst_ref, *sems)

    res = _pcall(wrapped, name=name, grid=grid, in_specs=list(in_specs) + [_ANY], out_specs=out_specs_t + (_ANY,),
                 out_shape=out_shape_t + (_exchange_shape(kind, src),),
                 scratch_shapes=list(scratch_shapes) + list(_SEMS), compiler_params=compiler_params)(*args, src)
    return (res[0] if single else tuple(res[:-1])), res[-1]


SUB_ROWS = TM


class _Epilogue:
    def __init__(self, fn, args, in_specs, out_specs, out_shape, steps=None, lhs_map=None):
        self.fn, self.args, self.in_specs, self.out_specs, self.out_shape = fn, tuple(args), list(in_specs), out_specs, out_shape
        self.steps, self.lhs_map = steps, lhs_map


def _mm_nn(a, w, out_dtype, name, square_lhs=False, relu_out=False, ride=None, tm=TM, epi=None, vmem_mb=48):
    m, k = a.shape
    nb, _, bn = w.shape
    tm = min(tm, m)
    n_extra = 0 if epi is None else len(epi.args)

    def body(*refs):
        a_ref, w_ref = refs[:2]
        av = a_ref[...]
        if square_lhs:
            av = av * av
        if epi is not None:
            assert nb == 1
            for r0 in range(0, tm, SUB_ROWS):
                rows = slice(r0, r0 + SUB_ROWS)
                epi.fn(jnp.dot(av[rows], w_ref[0], preferred_element_type=F32), rows, *refs[2:])
            return
        for j in range(nb):
            r = jnp.dot(av, w_ref[j], preferred_element_type=F32)
            if relu_out:
                r = jnp.maximum(r, 0.0)
            refs[2][:, j * bn:(j + 1) * bn] = r.astype(out_dtype)

    in_specs = [pl.BlockSpec((tm, k), lambda i: (i, 0)), pl.BlockSpec((nb, k, bn), lambda i: (0, 0, 0))]
    if epi is None:
        args, out_specs, out_shape = (a, w), pl.BlockSpec((tm, nb * bn), lambda i: (i, 0)), _sds((m, nb * bn), out_dtype)
    else:
        args, out_specs, out_shape = (a, w) + epi.args, epi.out_specs, epi.out_shape
        in_specs += epi.in_specs
    out, ex = _pcall_ride(body, ride, args, name=name, grid=(m // tm,), in_specs=in_specs, out_specs=out_specs,
                          out_shape=out_shape, compiler_params=_cp(vmem_mb))
    return out if ride is None else (out, ex)


def _mm_nt(dy, w, out_dtype, name, relu_mul=None, ride=None, tm=TM, epi=None, vmem_mb=48):
    m = dy.shape[0]
    nb, k, bn = w.shape
    tm = min(tm, m)
    n_extra = (0 if relu_mul is None else 1) + (0 if epi is None else len(epi.args))

    def body(*refs):
        dy_ref, w_ref = refs[:2]
        extra, outs, wt = refs[2:2 + n_extra], refs[2 + n_extra:-1], refs[-1]

        @pl.when(pl.program_id(0) == 0)
        def _():
            for j in range(nb):
                wt[j * bn:(j + 1) * bn, :] = w_ref[j].T

        if epi is not None:
            for r0 in range(0, tm, SUB_ROWS):
                rows = slice(r0, r0 + SUB_ROWS)
                epi.fn(jnp.dot(dy_ref[rows, :], wt[...], preferred_element_type=F32), rows, *extra, *outs)
            return
        acc = jnp.dot(dy_ref[...], wt[...], preferred_element_type=F32)
        if relu_mul is not None:
            acc = acc * (2.0 * extra[0][...].astype(F32))
        outs[0][...] = acc.astype(out_dtype)

    lhs_map = (lambda i: (i, 0)) if epi is None or epi.lhs_map is None else epi.lhs_map
    in_specs = [pl.BlockSpec((tm, nb * bn), lhs_map), pl.BlockSpec((nb, k, bn), lambda i: (0, 0, 0))]
    args = [dy, w]
    if relu_mul is not None:
        in_specs.append(pl.BlockSpec((tm, k), lambda i: (i, 0)))
        args.append(relu_mul)
    steps = m // tm
    if epi is None:
        out_specs, out_shape = pl.BlockSpec((tm, k), lambda i: (i, 0)), _sds((m, k), out_dtype)
    else:
        args += list(epi.args)
        in_specs += epi.in_specs
        out_specs, out_shape = epi.out_specs, epi.out_shape
        steps = steps if epi.steps is None else epi.steps
    out, ex = _pcall_ride(
        body, ride, args, name=name, grid=(steps,), in_specs=in_specs, out_specs=out_specs, out_shape=out_shape,
        scratch_shapes=[pltpu.VMEM((nb * bn, k), BF16)], compiler_params=_cp(vmem_mb))
    return out if ride is None else (out, ex)


def _mm_tn(a, b, name, col_blocks, block, out_dtype, tm, square_lhs=False):
    m, k = a.shape
    nn = b.shape[1]
    steps = m // tm
    if col_blocks:
        nblk, acc_shape = nn // block, (k, block)
        a_spec = pl.BlockSpec((tm, k), lambda j, s: (s, 0))
        b_spec = pl.BlockSpec((tm, block), lambda j, s: (s, j))
    else:
        nblk, acc_shape = k // block, (block, nn)
        a_spec = pl.BlockSpec((tm, block), lambda j, s: (s, j))
        b_spec = pl.BlockSpec((tm, nn), lambda j, s: (s, 0))

    def body(a_ref, b_ref, o_ref, acc):
        s = pl.program_id(1)

        @pl.when(s == 0)
        def _():
            acc[...] = jnp.zeros_like(acc)

        av = a_ref[...]
        if square_lhs:
            av = av.astype(F32)
            av = (av * av).astype(BF16)
        acc[...] += _dot_tn(av, b_ref[...])

        @pl.when(s == steps - 1)
        def _():
            o_ref[...] = acc[...].astype(out_dtype)

    return _pcall(
        body, name=name, grid=(nblk, steps), in_specs=[a_spec, b_spec],
        out_specs=pl.BlockSpec((None,) + acc_shape, lambda j, s: (j, 0, 0)),
        out_shape=_sds((nblk,) + acc_shape, out_dtype),
        scratch_shapes=[pltpu.VMEM(acc_shape, F32)], compiler_params=_cp(),
    )(a, b)


def _mm_wgrad(at, b, name, bn, out_dtype, transpose_out=False, square_rhs=False, halves=False, ride=None):
    k, m = at.shape
    nblk = b.shape[1] // bn
    rows, cols = (bn, k) if transpose_out else (k, bn)
    nout = 2 if halves else 1
    per = rows // nout

    def body(a_ref, b_ref, *o_refs):
        bv = b_ref[...]
        if square_rhs:
            bv = bv * bv
        r = jnp.dot(a_ref[...], bv, preferred_element_type=F32)
        r = (r.T if transpose_out else r).astype(out_dtype)
        for i, o_ref in enumerate(o_refs):
            o_ref[...] = r[i * per:(i + 1) * per, :]

    out, ex = _pcall_ride(
        body, ride, (at, b), name=name, grid=(nblk,),
        in_specs=[pl.BlockSpec((k, m), lambda j: (0, 0)), pl.BlockSpec((m, bn), lambda j: (0, j))],
        out_specs=tuple(pl.BlockSpec((None, per, cols), lambda j: (j, 0, 0)) for _ in range(nout)),
        out_shape=tuple(_sds((nblk, per, cols), out_dtype) for _ in range(nout)), compiler_params=_cp())
    out = out if halves else out[0]
    return out if ride is None else (out, ex)


def _mm_in(hn, w, cosf, sins, ride):
    m, k = hn.shape
    nb, _, bn = w.shape

    def body(a_ref, w_ref, c_ref, s_ref, qkv_ref, rest_ref, pt):
        av = a_ref[...]
        for j in range(nb):
            pt[:, j * bn:(j + 1) * bn] = jnp.dot(av, w_ref[j], preferred_element_type=F32)
        cf, ss = c_ref[...], s_ref[...]
        for h in range(HEADS):
            sq = slice(h * HEAD_DIM, (h + 1) * HEAD_DIM)
            sk = slice(RET_W + h * HEAD_DIM, RET_W + (h + 1) * HEAD_DIM)
            qkv_ref[:, sq] = _rot(pt[:, sq], cf, ss).astype(BF16)
            qkv_ref[:, sk] = (_rot(pt[:, sk], cf, ss) * K_SCALE).astype(BF16)
        qkv_ref[:, 2 * RET_W:] = pt[:, 2 * RET_W:3 * RET_W].astype(BF16)
        rest_ref[...] = pt[:, 3 * RET_W:]

    tab = pl.BlockSpec((TM, HEAD_DIM), lambda i: (i, 0))
    wide = pl.BlockSpec((TM, 3 * RET_W), lambda i: (i, 0))
    return _pcall_ride(
        body, ride, (hn, w, cosf, sins), name="mm_in", grid=(m // TM,),
        in_specs=[pl.BlockSpec((TM, k), lambda i: (i, 0)), pl.BlockSpec((nb, k, bn), lambda i: (0, 0, 0)), tab, tab],
        out_specs=(wide, wide), out_shape=(_sds((m, 3 * RET_W), BF16), _sds((m, nb * bn - 3 * RET_W), F32)),
        scratch_shapes=[pltpu.VMEM((TM, nb * bn), F32)], compiler_params=_cp())


def _mod_part(s16, w_ada, b_cols):
    def body(s_ref, w_ref, b_ref, o_ref):
        o_ref[...] = _dot(s_ref[...], w_ref[...]) + b_ref[...]

    return _pcall(body, name="mod_part", out_shape=_sds((s16.shape[0], w_ada.shape[1]), F32),
                  compiler_params=_cp())(s16, w_ada, b_cols)


def _ada_bwd(s16, dm_cols, w_ada):
    def body(s_ref, d_ref, w_ref, gw_ref, ds_ref):
        gw_ref[...] = _dot_tn(s_ref[...], d_ref[...])
        ds_ref[...] = _dot_nt(d_ref[...], w_ref[...])

    return _pcall(body, name="ada_bwd",
                  out_shape=(_sds(w_ada.shape, F32), _sds(s16.shape, F32)), compiler_params=_cp())(s16, dm_cols, w_ada)


def _norm1_fwd(ctx, x, g, modrows, cb, ride=None):
    l_len, d = ctx.shape
    nb = (l_len + x.shape[0]) // TM

    def body(ctx_ref, x_ref, g_ref, m_ref, o_ref, ot_ref):
        is_ctx = pl.program_id(0) < cb
        xin = jnp.where(is_ctx, ctx_ref[...], x_ref[...])
        sh = jnp.where(is_ctx, m_ref[R_CSH1:R_CSH1 + 1, :], m_ref[R_SH1:R_SH1 + 1, :])
        sc = jnp.where(is_ctx, m_ref[R_CSC1:R_CSC1 + 1, :], m_ref[R_SC1:R_SC1 + 1, :])
        ms = jnp.mean(xin * xin, axis=-1, keepdims=True)
        n = xin * lax.rsqrt(ms + EPS) * g_ref[...]
        hn = n * (1.0 + sc) + sh
        o_ref[...] = hn.astype(BF16)
        ot_ref[...] = hn.T.astype(BF16)

    return _pcall_ride(
        body, ride, (ctx, x, g, modrows), name="norm1_fwd", grid=(nb,),
        in_specs=[pl.BlockSpec((TM, d), lambda i: (jnp.minimum(i, cb - 1), 0)),
                  pl.BlockSpec((TM, d), lambda i: (jnp.maximum(i - cb, 0), 0)),
                  pl.BlockSpec((1, d), lambda i: (0, 0)), pl.BlockSpec((8, d), lambda i: (0, 0))],
        out_specs=(pl.BlockSpec((TM, d), lambda i: (i, 0)), pl.BlockSpec((d, TM), lambda i: (0, i))),
        out_shape=(_sds((nb * TM, d), BF16), _sds((d, nb * TM), BF16)), compiler_params=_cp())


def _mix_fwd(o_f, o_b, p, h_f, h_b, cb, t_len):
    def body(of_ref, ob_ref, g_ref, gate_ref, hf_ref, hb_ref, mix_ref):
        o = of_ref[...] + ob_ref[...]
        g = g_ref[...]
        sg = g * _sigmoid(g)
        for hh in range(HEADS):
            sl = slice(hh * HEAD_DIM, (hh + 1) * HEAD_DIM)
            oh = o[:, sl]
            yc = oh - jnp.mean(oh, axis=-1, keepdims=True)
            var = jnp.mean(yc * yc, axis=-1, keepdims=True)
            mix_ref[:, sl] = (sg[:, sl] * (yc * lax.rsqrt(var + EPS))).astype(BF16)
        mix_ref[:, RET_W:] = ((hf_ref[...] + hb_ref[...]) * _gelu(gate_ref[...])).astype(BF16)

    row = lambda i: (i + cb, 0)
    return _pcall(
        body, name="mix_fwd", grid=(t_len // TM,),
        in_specs=[pl.BlockSpec((TM, RET_W), row), pl.BlockSpec((TM, RET_W), row),
                  pl.BlockSpec((TM, RET_W), lambda i: (i + cb, COL_G)), pl.BlockSpec((TM, LRU_W), lambda i: (i + cb, COL_GATE)),
                  pl.BlockSpec((TM, LRU_W), row), pl.BlockSpec((TM, LRU_W), row)],
        out_specs=pl.BlockSpec((TM, RET_W + LRU_W), lambda i: (i, 0)),
        out_shape=_sds((t_len, RET_W + LRU_W), BF16), compiler_params=_cp(),
    )(o_f, o_b, p, p, h_f, h_b)


def _res_norm2_epilogue(x, g, modrows, tm):
    t_len, d = x.shape
    tm = min(tm, t_len)

    def fn(y, rows, x_ref, g_ref, m_ref, y_ref, x1_ref, h2_ref, h2t_ref):
        y_ref[rows, :] = y
        x1 = x_ref[rows, :] + m_ref[R_G1:R_G1 + 1, :] * y
        ms = jnp.mean(x1 * x1, axis=-1, keepdims=True)
        n = x1 * lax.rsqrt(ms + EPS) * g_ref[...]
        x1_ref[rows, :] = x1
        h2 = n * (1.0 + m_ref[R_SC2:R_SC2 + 1, :]) + m_ref[R_SH2:R_SH2 + 1, :]
        h2_ref[rows, :] = h2.astype(BF16)
        h2t_ref[:, rows] = h2.T.astype(BF16)

    t = pl.BlockSpec((tm, d), lambda i: (i, 0))
    return _Epilogue(
        fn, (x, g, modrows),
        in_specs=[t, pl.BlockSpec((1, d), lambda i: (0, 0)), pl.BlockSpec((8, d), lambda i: (0, 0))],
        out_specs=(t, t, t, pl.BlockSpec((d, tm), lambda i: (0, i))),
        out_shape=(_sds((t_len, d), F32), _sds((t_len, d), F32), _sds((t_len, d), BF16), _sds((d, t_len), BF16)))


def _zero_at_start(acc_ref, rows):
    if rows.start == 0:
        @pl.when(pl.program_id(0) == 0)
        def _():
            acc_ref[...] = jnp.zeros_like(acc_ref)


def _final_epilogue(x1, target, fg, modrows, tm):
    t_len, d = x1.shape

    def fn(z, rows, x1_ref, t_ref, fg_ref, m_ref, dx2_ref, dz_ref, dzt_ref, acc_ref):
        _zero_at_start(acc_ref, rows)
        g2 = m_ref[R_G2:R_G2 + 1, :]
        x2 = x1_ref[rows, :] + g2 * z
        rstd = lax.rsqrt(jnp.mean(x2 * x2, axis=-1, keepdims=True) + EPS)
        xh = x2 * rstd
        fg = fg_ref[...]
        e = xh * fg - t_ref[rows, :]
        dy = e * (1.0 / d)
        dxh = dy * fg
        dx2 = rstd * (dxh - xh * jnp.mean(dxh * xh, axis=-1, keepdims=True))
        dx2_ref[rows, :] = dx2
        dz = g2 * dx2
        dz_ref[rows, :] = dz.astype(BF16)
        dzt_ref[:, rows] = dz.T.astype(BF16)
        acc_ref[0:1, :] += jnp.sum(dy * xh, axis=0, keepdims=True)
        acc_ref[1:2, :] += jnp.sum(dx2 * z, axis=0, keepdims=True)
        acc_ref[2:3, :] += jnp.sum(e * e, axis=0, keepdims=True)

    t = pl.BlockSpec((tm, d), lambda i: (i, 0))
    return _Epilogue(
        fn, (x1, target, fg, modrows),
        in_specs=[t, t, pl.BlockSpec((1, d), lambda i: (0, 0)), pl.BlockSpec((8, d), lambda i: (0, 0))],
        out_specs=(t, t, pl.BlockSpec((d, tm), lambda i: (0, i)), pl.BlockSpec((8, d), lambda i: (0, 0))),
        out_shape=(_sds((t_len, d), F32), _sds((t_len, d), BF16), _sds((d, t_len), BF16), _sds((8, d), F32)))


def _bwd_norm2_epilogue(x1, dx2, y, g, modrows, tm):
    t_len, d = x1.shape

    def fn(dh2, rows, x1_ref, dx2_ref, y_ref, g_ref, m_ref, dx1_ref, dy_ref, acc_ref):
        _zero_at_start(acc_ref, rows)
        x1 = x1_ref[rows, :]
        rstd = lax.rsqrt(jnp.mean(x1 * x1, axis=-1, keepdims=True) + EPS)
        xh = x1 * rstd
        gn = g_ref[...]
        dn = dh2 * (1.0 + m_ref[R_SC2:R_SC2 + 1, :])
        dxh = dn * gn
        dx1 = dx2_ref[rows, :] + rstd * (dxh - xh * jnp.mean(dxh * xh, axis=-1, keepdims=True))
        dx1_ref[rows, :] = dx1
        dy_ref[rows, :] = (m_ref[R_G1:R_G1 + 1, :] * dx1).astype(BF16)
        acc_ref[0:1, :] += jnp.sum(dh2, axis=0, keepdims=True)
        acc_ref[1:2, :] += jnp.sum(dh2 * xh * gn, axis=0, keepdims=True)
        acc_ref[2:3, :] += jnp.sum(dn * xh, axis=0, keepdims=True)
        acc_ref[3:4, :] += jnp.sum(dx1 * y_ref[rows, :], axis=0, keepdims=True)

    t = pl.BlockSpec((tm, d), lambda i: (i, 0))
    return _Epilogue(
        fn, (x1, dx2, y, g, modrows),
        in_specs=[t, t, t, pl.BlockSpec((1, d), lambda i: (0, 0)), pl.BlockSpec((8, d), lambda i: (0, 0))],
        out_specs=(t, t, pl.BlockSpec((8, d), lambda i: (0, 0))),
        out_shape=(_sds((t_len, d), F32), _sds((t_len, d), BF16), _sds((8, d), F32)))


def _mix_bwd_epilogue(o_f, o_b, p, h_f, h_b, cb):
    n = o_f.shape[0]

    def fn(dm, rows, of_ref, ob_ref, g_ref, gate_ref, hf_ref, hb_ref, do_ref, dg_ref, dgate_ref, dhs_ref):
        is_ctx = pl.program_id(0) < cb

        @pl.when(is_ctx)
        def _():
            for r in (do_ref, dg_ref, dgate_ref, dhs_ref):
                r[rows, :] = jnp.zeros((rows.stop - rows.start, r.shape[1]), r.dtype)

        @pl.when(jnp.logical_not(is_ctx))
        def _():
            o = of_ref[rows, :] + ob_ref[rows, :]
            g = g_ref[rows, :]
            s = _sigmoid(g)
            sg = g * s
            dsg = s * (1.0 + g * (1.0 - s))
            for hh in range(HEADS):
                sl = slice(hh * HEAD_DIM, (hh + 1) * HEAD_DIM)
                oh = o[:, sl]
                yc = oh - jnp.mean(oh, axis=-1, keepdims=True)
                rs = lax.rsqrt(jnp.mean(yc * yc, axis=-1, keepdims=True) + EPS)
                gn = yc * rs
                dret = dm[:, sl]
                dgn = dret * sg[:, sl]
                dg_ref[rows, sl] = (dret * gn * dsg[:, sl]).astype(BF16)
                do_ref[rows, sl] = (rs * (dgn - jnp.mean(dgn, axis=-1, keepdims=True)
                                          - gn * jnp.mean(dgn * gn, axis=-1, keepdims=True))).astype(BF16)
            dlru = dm[:, RET_W:]
            gate = gate_ref[rows, :]
            dhs_ref[rows, :] = dlru * _gelu(gate)
            dgate_ref[rows, :] = (dlru * (hf_ref[rows, :] + hb_ref[rows, :]) * _dgelu(gate)).astype(BF16)

    t = pl.BlockSpec((TM, RET_W), lambda i: (i, 0))
    return _Epilogue(
        fn, (o_f, o_b, p, p, h_f, h_b),
        in_specs=[t, t, pl.BlockSpec((TM, RET_W), lambda i: (i, COL_G)), pl.BlockSpec((TM, LRU_W), lambda i: (i, COL_GATE)), t, t],
        out_specs=(t, t, t, t), out_shape=(_sds((n, RET_W), BF16),) * 3 + (_sds((n, RET_W), F32),),
        steps=n // TM, lhs_map=lambda i: (jnp.maximum(i - cb, 0), 0))


def _bwd_norm1_epilogue(ctx, x, dx1, g, modrows, cb):
    t_len, d = x.shape

    def fn(dh, rows, ctx_ref, x_ref, dx1_ref, g_ref, m_ref, gx_ref, acc_ref):
        is_ctx = pl.program_id(0) < cb
        _zero_at_start(acc_ref, rows)
        xin = jnp.where(is_ctx, ctx_ref[rows, :], x_ref[rows, :])
        sc = jnp.where(is_ctx, m_ref[R_CSC1:R_CSC1 + 1, :], m_ref[R_SC1:R_SC1 + 1, :])
        rstd = lax.rsqrt(jnp.mean(xin * xin, axis=-1, keepdims=True) + EPS)
        xh = xin * rstd
        gn = g_ref[...]
        dn = dh * (1.0 + sc)
        dxh = dn * gn
        gx_ref[rows, :] = dx1_ref[rows, :] + rstd * (dxh - xh * jnp.mean(dxh * xh, axis=-1, keepdims=True))
        s0 = jnp.sum(dh, axis=0, keepdims=True)
        s1 = jnp.sum(dh * xh * gn, axis=0, keepdims=True)
        acc_ref[4:5, :] += jnp.sum(dn * xh, axis=0, keepdims=True)

        @pl.when(is_ctx)
        def _():
            acc_ref[0:1, :] += s0
            acc_ref[1:2, :] += s1

        @pl.when(jnp.logical_not(is_ctx))
        def _():
            acc_ref[2:3, :] += s0
            acc_ref[3:4, :] += s1

    lat = pl.BlockSpec((TM, d), lambda i: (jnp.maximum(i - cb, 0), 0))
    return _Epilogue(
        fn, (ctx, x, dx1, g, modrows),
        in_specs=[pl.BlockSpec((TM, d), lambda i: (jnp.minimum(i, cb - 1), 0)),
                  lat, lat, pl.BlockSpec((1, d), lambda i: (0, 0)), pl.BlockSpec((8, d), lambda i: (0, 0))],
        out_specs=(lat, pl.BlockSpec((8, d), lambda i: (0, 0))),
        out_shape=(_sds((t_len, d), F32), _sds((8, d), F32)))


def _rot(x, cf, ss):
    return x * cf + pltpu.roll(x, HEAD_DIM // 2, 1) * ss


def _decay_exponents(dirn):
    ii = lax.broadcasted_iota(jnp.int32, (CHUNK, CHUNK), 0)
    jj = lax.broadcasted_iota(jnp.int32, (CHUNK, CHUNK), 1)
    rel = ii - jj if dirn == 0 else jj - ii
    pos = ii.astype(F32)
    if dirn == 0:
        cq, cs = pos + 1.0, (CHUNK - 1.0) - pos
    else:
        cq, cs = CHUNK - pos, pos
    return rel, jnp.maximum(rel, 0).astype(F32), cq, cs


def _store_decay(lg_ref, dec, relf_ref=None):
    for dirn in (0, 1):
        rel, relf, cq, cs = _decay_exponents(dirn)
        if relf_ref is not None:
            relf_ref[dirn] = relf
        for h in range(HEADS):
            lgv = lg_ref[dirn, h]
            wq, ws = jnp.exp(lgv * cq), jnp.exp(lgv * cs)
            dec[dirn, h, 0] = jnp.where(rel >= 0, jnp.exp(lgv * relf), 0.0)
            dec[dirn, h, 1] = wq
            dec[dirn, h, 2] = ws
            if relf_ref is not None:
                dec[dirn, h, 3] = wq * cq
                dec[dirn, h, 4] = ws * cs


def _ret_rows(cc, nc, step_of):
    return [lambda s, dirn=dirn: _tile_order(dirn, step_of(s), cc, nc) for dirn in (0, 1)]


def _ret_in_specs(rows):
    specs = []
    for row in rows:
        specs += [pl.BlockSpec((CHUNK, RET_W), lambda s, o=o, row=row: (row(s), o)) for o in (0, 1, 2)]
    return specs


def _ret_fwd(qkv, lg, cc, ride=None):
    n = qkv.shape[0]
    nc = n // CHUNK
    rows = _ret_rows(cc, nc, lambda s: s)

    def body(lg_ref, q0, k0, v0, q1, k1, v1, o0, o1, sp0, sp1, st, dec):
        @pl.when(pl.program_id(0) == 0)
        def _():
            st[...] = jnp.zeros_like(st)
            _store_decay(lg_ref, dec)

        refs = ((q0, k0, v0, o0, sp0), (q1, k1, v1, o1, sp1))
        chains = [(dirn, h, slice(h * HEAD_DIM, (h + 1) * HEAD_DIM)) for dirn in (0, 1) for h in range(HEADS)]
        scores, cross, update = [], [], []
        for dirn, h, sl in chains:
            q_ref, k_ref, v_ref, _, sp_ref = refs[dirn]
            q, k, v = q_ref[:, sl], k_ref[:, sl], v_ref[:, sl]
            sp = st[dirn, h]
            sp_ref[h] = sp
            scores.append(_dot_nt(q, k))
            cross.append(_dot(q * dec[dirn, h, 1], sp))
            update.append(_dot_tn(k * dec[dirn, h, 2], v))
        masked = [(a * dec[dirn, h, 0]).astype(BF16) for a, (dirn, h, _) in zip(scores, chains)]
        intra = [_dot(sc, refs[dirn][2][:, sl]) for sc, (dirn, h, sl) in zip(masked, chains)]
        for (dirn, h, sl), o_in, o_cr, upd in zip(chains, intra, cross, update):
            refs[dirn][3][:, sl] = o_in + o_cr
            st[dirn, h] = jnp.exp(lg_ref[dirn, h] * CHUNK) * st[dirn, h] + upd

    o_specs = [pl.BlockSpec((CHUNK, RET_W), lambda s, row=row: (row(s), 0)) for row in rows]
    state = pl.BlockSpec((None, HEADS, CHUNK, HEAD_DIM), lambda s: (s, 0, 0, 0))
    return _pcall_ride(
        body, ride, (lg,) + (qkv,) * 6, name="ret_fwd", grid=(nc,),
        in_specs=[pl.BlockSpec(memory_space=pltpu.SMEM)] + _ret_in_specs(rows),
        out_specs=(o_specs[0], o_specs[1], state, state),
        out_shape=(_sds((n, RET_W), F32),) * 2 + (_sds((nc, HEADS, CHUNK, HEAD_DIM), F32),) * 2,
        scratch_shapes=[pltpu.VMEM((2, HEADS, CHUNK, HEAD_DIM), F32), pltpu.VMEM((2, HEADS, 3, CHUNK, CHUNK), F32)],
        compiler_params=_cp())


def _ret_bwd(qkv, lg, do, s_prev, cc, ride=None):
    n = qkv.shape[0]
    nc = n // CHUNK
    rows = _ret_rows(cc, nc, lambda s: nc - 1 - s)

    def body(lg_ref, q0, k0, v0, q1, k1, v1, do0, do1, sp0, sp1, dq0, dk0, dv0, dq1, dk1, dv1, dlg_ref, dst, dec, relf):
        @pl.when(pl.program_id(0) == 0)
        def _():
            dst[...] = jnp.zeros_like(dst)
            dlg_ref[...] = jnp.zeros_like(dlg_ref)
            _store_decay(lg_ref, dec, relf)

        refs = ((q0, k0, v0, do0, sp0, dq0, dk0, dv0), (q1, k1, v1, do1, sp1, dq1, dk1, dv1))
        chains = [(dirn, h, slice(h * HEAD_DIM, (h + 1) * HEAD_DIM)) for dirn in (0, 1) for h in range(HEADS)]

        def tiles(dirn, sl):
            q_ref, k_ref, v_ref, do_ref = refs[dirn][:4]
            return q_ref[:, sl], k_ref[:, sl], v_ref[:, sl], do_ref[:, sl]

        a_s, g1_s, da_s, h1_s = [], [], [], []
        for dirn, h, sl in chains:
            q, k, v, dov = tiles(dirn, sl)
            a_s.append(_dot_nt(q, k))
            g1_s.append(_dot_nt(dov, refs[dirn][4][h]))
            da_s.append(_dot_nt(dov, v))
            h1_s.append(_dot_nt(v, dst[dirn, h]))
        da_s = [da * dec[dirn, h, 0] for da, (dirn, h, _) in zip(da_s, chains)]
        dq_s, dk_s, dv_s, ds_s = [], [], [], []
        for (dirn, h, sl), a, da in zip(chains, a_s, da_s):
            q, k, v, dov = tiles(dirn, sl)
            dq_s.append(_dot(da, k))
            dk_s.append(_dot_tn(da, q))
            dv_s.append(_dot_tn(a * dec[dirn, h, 0], dov) + _dot(k * dec[dirn, h, 2], dst[dirn, h]))
            ds_s.append(_dot_tn(q * dec[dirn, h, 1], dov))
        for (dirn, h, sl), a, g1, da, h1, dq2, dk2, dv, ds2 in zip(chains, a_s, g1_s, da_s, h1_s, dq_s, dk_s, dv_s, ds_s):
            q, k, _, _ = tiles(dirn, sl)
            dq_ref, dk_ref, dv_ref = refs[dirn][5:]
            sp, dsn = refs[dirn][4][h], dst[dirn, h]
            gc = jnp.exp(lg_ref[dirn, h] * CHUNK)
            dq_ref[:, sl] = (g1 * dec[dirn, h, 1] + dq2).astype(BF16)
            dk_ref[:, sl] = (dk2 + h1 * dec[dirn, h, 2]).astype(BF16)
            dv_ref[:, sl] = dv.astype(BF16)
            term = da * a * relf[dirn] + q * g1 * dec[dirn, h, 3] + k * h1 * dec[dirn, h, 4] + sp * dsn * (CHUNK * gc)
            dlg_ref[dirn * HEADS + h:dirn * HEADS + h + 1, 0:HEAD_DIM] += jnp.sum(term, axis=0, keepdims=True)
            dst[dirn, h] = gc * dsn + ds2

    wide = [pl.BlockSpec((CHUNK, RET_W), lambda s, row=row: (row(s), 0)) for row in rows]
    state = pl.BlockSpec((None, HEADS, CHUNK, HEAD_DIM), lambda s: (nc - 1 - s, 0, 0, 0))
    return _pcall_ride(
        body, ride, (lg,) + (qkv,) * 6 + (do, do, s_prev[0], s_prev[1]), name="ret_bwd", grid=(nc,),
        in_specs=[pl.BlockSpec(memory_space=pltpu.SMEM)] + _ret_in_specs(rows) + wide + [state, state],
        out_specs=(wide[0],) * 3 + (wide[1],) * 3 + (pl.BlockSpec((2 * HEADS, 8 * HEAD_DIM), lambda s: (0, 0)),),
        out_shape=(_sds((n, RET_W), BF16),) * 6 + (_sds((2 * HEADS, 8 * HEAD_DIM), F32),),
        scratch_shapes=[pltpu.VMEM((2, HEADS, CHUNK, HEAD_DIM), F32), pltpu.VMEM((2, HEADS, 5, CHUNK, CHUNK), F32),
                        pltpu.VMEM((2, CHUNK, CHUNK), F32)],
        compiler_params=_cp())


def _shift_rows(cur, prev8, next8, k, seg_start, seg_end):
    tm = cur.shape[0]
    rows = _rows_iota(cur.shape)
    if k < 0:
        out = pltpu.roll(cur, -k, 0)
        for j in range(-k):
            halo = jnp.where(seg_start, 0.0, prev8[SUB + k + j:SUB + k + j + 1, :])
            out = jnp.where(rows == j, halo, out)
    else:
        out = pltpu.roll(cur, tm - k, 0)
        for j in range(k):
            halo = jnp.where(seg_end, 0.0, next8[j:j + 1, :])
            out = jnp.where(rows == tm - k + j, halo, out)
    return out


def _seg_flags(t, cb, nb):
    return jnp.logical_or(t == 0, t == cb), jnp.logical_or(t == cb - 1, t == nb - 1)


def _halo_specs(tile_of, n_rows, col):
    per = TM // SUB
    return [pl.BlockSpec((TM, LRU_W), lambda s: (tile_of(s), col)),
            pl.BlockSpec((SUB, LRU_W), lambda s: (jnp.maximum(tile_of(s) * per - 1, 0), col)),
            pl.BlockSpec((SUB, LRU_W), lambda s: (jnp.minimum((tile_of(s) + 1) * per, n_rows // SUB - 1), col))]


def _lru_gates(xr, prev8, next8, seg_start, seg_end, cw_ref, cb_ref, wg_ref, bg_ref, sp_ref):
    xm1 = _shift_rows(xr, prev8, next8, -1, seg_start, seg_end)
    xp1 = _shift_rows(xr, prev8, next8, 1, seg_start, seg_end)
    xp2 = _shift_rows(xr, prev8, next8, 2, seg_start, seg_end)
    xc = cb_ref[...] + xm1 * cw_ref[0:1, :] + xr * cw_ref[1:2, :] + xp1 * cw_ref[2:3, :] + xp2 * cw_ref[3:4, :]
    pre = _dot(xc, wg_ref[...]) + bg_ref[...]
    r = _sigmoid(pre[:, :LRU_W])
    i = _sigmoid(pre[:, LRU_W:])
    la = (-LRU_C) * r * sp_ref[...]
    a = jnp.exp(la)
    th = jnp.tanh(la)
    sq = jnp.sqrt(-2.0 * th / (1.0 - th))
    return xc, r, i, a, sq


def _scan_tile(a, b, ascending, a_sc, b_sc, carry, out_ref):
    tm, w = a.shape
    nsub = tm // SUB
    a = a.reshape(nsub, SUB, w)
    b = b.reshape(nsub, SUB, w)
    r8 = lax.broadcasted_iota(jnp.int32, a.shape, 1)
    for k in (1, 2, 4):
        if ascending:
            m = r8 >= k
            a_s, b_s = pltpu.roll(a, k, 1), pltpu.roll(b, k, 1)
        else:
            m = r8 < SUB - k
            a_s, b_s = pltpu.roll(a, SUB - k, 1), pltpu.roll(b, SUB - k, 1)
        b = a * jnp.where(m, b_s, 0.0) + b
        a = a * jnp.where(m, a_s, 1.0)
    a_sc[...] = a.reshape(tm, w)
    b_sc[...] = b.reshape(tm, w)

    def step(j, c):
        off = pl.multiple_of((j if ascending else nsub - 1 - j) * SUB, SUB)
        hb = a_sc[pl.ds(off, SUB), :] * c + b_sc[pl.ds(off, SUB), :]
        out_ref[pl.ds(off, SUB), :] = hb
        last = hb[SUB - 1:SUB, :] if ascending else hb[0:1, :]
        return jnp.broadcast_to(last, c.shape)

    carry[...] = lax.fori_loop(0, nsub, step, carry[...], unroll=8)


def _lru_fwd(p, wg, bg, sp, cw, cbias, dirn, cb, ride=None):
    n = p.shape[0]
    nb = n // TM
    tile_of = lambda s: _tile_order(dirn, s, cb, nb)

    def body(x_ref, xp_ref, xn_ref, wg_ref, bg_ref, sp_ref, cw_ref, cb_ref, h_ref, cin_ref, carry, a_sc, b_sc):
        s = pl.program_id(0)

        @pl.when(s == 0)
        def _():
            carry[...] = jnp.zeros_like(carry)

        seg_start, seg_end = _seg_flags(tile_of(s), cb, nb)
        xc, r, i, a, sq = _lru_gates(x_ref[...], xp_ref[...], xn_ref[...], seg_start, seg_end,
                                     cw_ref, cb_ref, wg_ref, bg_ref, sp_ref)
        cin_ref[...] = carry[...]
        _scan_tile(a, sq * (i * xc), dirn == 0, a_sc, b_sc, carry, h_ref)

    full = lambda shape: pl.BlockSpec(shape, lambda s: (0,) * len(shape))
    return _pcall_ride(
        body, ride, (p, p, p, wg, bg, sp, cw, cbias), name=f"lru_fwd{dirn}", grid=(nb,),
        in_specs=_halo_specs(tile_of, n, COL_XR) + [full((LRU_W, 2 * LRU_W)), full((1, 2 * LRU_W)), full((1, LRU_W)),
                                               full((4, LRU_W)), full((1, LRU_W))],
        out_specs=(pl.BlockSpec((TM, LRU_W), lambda s: (tile_of(s), 0)),
                   pl.BlockSpec((None, SUB, LRU_W), lambda s: (tile_of(s), 0, 0))),
        out_shape=(_sds((n, LRU_W), F32), _sds((nb, SUB, LRU_W), F32)),
        scratch_shapes=[pltpu.VMEM((SUB, LRU_W), F32), pltpu.VMEM((TM, LRU_W), F32), pltpu.VMEM((TM, LRU_W), F32)],
        compiler_params=_cp())


def _lru_bwd(p, wg, bg, sp, cw, cbias, h, cin, dhs, dirn, cb, ride=None):
    n = p.shape[0]
    nb = n // TM
    tile_of = lambda s: _tile_order(dirn, nb - 1 - s, cb, nb)

    def body(x_ref, xp_ref, xn_ref, wg_ref, bg_ref, sp_ref, cw_ref, cb_ref, h_ref, cin_ref, dhs_ref,
             dxc_ref, dwd_ref, acc_ref, carry, a_sc, b_sc, mu_sc, dwg_ref):
        s = pl.program_id(0)

        @pl.when(s == 0)
        def _():
            carry[...] = jnp.zeros_like(carry)
            dwg_ref[...] = jnp.zeros_like(dwg_ref)
            acc_ref[...] = jnp.zeros_like(acc_ref)

        seg_start, seg_end = _seg_flags(tile_of(s), cb, nb)
        xc, r, i, a, sq = _lru_gates(x_ref[...], xp_ref[...], xn_ref[...], seg_start, seg_end,
                                     cw_ref, cb_ref, wg_ref, bg_ref, sp_ref)
        rows = _rows_iota(a.shape)
        hv = h_ref[...]
        dh = dhs_ref[...]
        mu_next = carry[0:1, :]
        _scan_tile(a, a * dh, dirn == 1, a_sc, b_sc, carry, mu_sc)
        mu = mu_sc[...]
        if dirn == 0:
            hprev = jnp.where(rows == 0, cin_ref[0:1, :], pltpu.roll(hv, 1, 0))
            lam = dh + jnp.where(rows == TM - 1, mu_next, pltpu.roll(mu, TM - 1, 0))
        else:
            hprev = jnp.where(rows == TM - 1, cin_ref[0:1, :], pltpu.roll(hv, TM - 1, 0))
            lam = dh + jnp.where(rows == 0, mu_next, pltpu.roll(mu, 1, 0))
        ds = lam * (i * xc)
        di = lam * (sq * xc)
        dla = lam * hprev * a - ds * (a * a) / jnp.maximum(sq, 1e-20)
        dpr = dla * ((-LRU_C) * sp_ref[...]) * r * (1.0 - r)
        dpi = di * i * (1.0 - i)
        dpre = jnp.concatenate([dpr, dpi], axis=1)
        dxc_ref[...] = lam * (sq * i) + _dot_nt(dpre, wg_ref[...])
        dwg_ref[...] += _dot_tn(xc, dpre)
        acc_ref[0:1, :] += jnp.sum(dpre, axis=0, keepdims=True)
        acc_ref[1:2, 0:LRU_W] += jnp.sum(dla * ((-LRU_C) * r), axis=0, keepdims=True)

        @pl.when(s == nb - 1)
        def _():
            low = lax.broadcasted_iota(jnp.int32, (LRU_BD, 2 * LRU_BD), 1) < LRU_BD
            for half in (0, LRU_W):
                for m in range(LRU_BLOCKS // 2):
                    lanes = slice(half + 2 * LRU_BD * m, half + 2 * LRU_BD * (m + 1))
                    even = dwg_ref[2 * m * LRU_BD:(2 * m + 1) * LRU_BD, lanes]
                    odd = dwg_ref[(2 * m + 1) * LRU_BD:(2 * m + 2) * LRU_BD, lanes]
                    dwd_ref[:, lanes] = jnp.where(low, even, odd)

    full = lambda shape: pl.BlockSpec(shape, lambda s: (0,) * len(shape))
    tile = pl.BlockSpec((TM, LRU_W), lambda s: (tile_of(s), 0))
    return _pcall_ride(
        body, ride, (p, p, p, wg, bg, sp, cw, cbias, h, cin, dhs), name=f"lru_bwd{dirn}", grid=(nb,),
        in_specs=_halo_specs(tile_of, n, COL_XR) + [full((LRU_W, 2 * LRU_W)), full((1, 2 * LRU_W)), full((1, LRU_W)),
                                               full((4, LRU_W)), full((1, LRU_W)), tile,
                                               pl.BlockSpec((None, SUB, LRU_W), lambda s: (tile_of(s), 0, 0)), tile],
        out_specs=(tile, full((LRU_BD, 2 * LRU_W)), full((8, 2 * LRU_W))),
        out_shape=(_sds((n, LRU_W), F32), _sds((LRU_BD, 2 * LRU_W), F32), _sds((8, 2 * LRU_W), F32)),
        scratch_shapes=[pltpu.VMEM((SUB, LRU_W), F32)] + [pltpu.VMEM((TM, LRU_W), F32)] * 3
        + [pltpu.VMEM((LRU_W, 2 * LRU_W), F32)],
        compiler_params=_cp())


def _assemble_dp(dqs, dks, dvs, dg, dgate, dxcs, p, cw, cosf, sins, cb, ride=None):
    n = p.shape[0]
    nb = n // TM
    tile_of = lambda s: s

    def body(dqf, dqb, dkf, dkb, dvf, dvb, dg_ref, dgate_ref, cf, pf, nf, cb_, pb, nb_, x_ref, xp_ref, xn_ref,
             cw_ref, cos_ref, sin_ref, dp_ref, acc_ref):
        s = pl.program_id(0)

        @pl.when(s == 0)
        def _():
            acc_ref[...] = jnp.zeros_like(acc_ref)

        seg_start, seg_end = _seg_flags(s, cb, nb)
        dq = dqf[...].astype(F32) + dqb[...].astype(F32)
        dk = dkf[...].astype(F32) + dkb[...].astype(F32)
        cosv, sinv = cos_ref[...], sin_ref[...]
        for h in range(HEADS):
            sl = slice(h * HEAD_DIM, (h + 1) * HEAD_DIM)
            sk = slice(RET_W + h * HEAD_DIM, RET_W + (h + 1) * HEAD_DIM)
            dp_ref[:, sl] = (dq[:, sl] * cosv + pltpu.roll(dq[:, sl] * sinv, HEAD_DIM // 2, 1)).astype(BF16)
            dp_ref[:, sk] = ((dk[:, sl] * cosv + pltpu.roll(dk[:, sl] * sinv, HEAD_DIM // 2, 1)) * K_SCALE).astype(BF16)
        dp_ref[:, 2 * RET_W:3 * RET_W] = (dvf[...].astype(F32) + dvb[...].astype(F32)).astype(BF16)
        dp_ref[:, 3 * RET_W:4 * RET_W] = dg_ref[...].astype(BF16)
        dxc = cf[...] + cb_[...]
        dprev = pf[...] + pb[...]
        dnext = nf[...] + nb_[...]
        dxr = (_shift_rows(dxc, dprev, dnext, 1, seg_start, seg_end) * cw_ref[0:1, :] + dxc * cw_ref[1:2, :]
               + _shift_rows(dxc, dprev, dnext, -1, seg_start, seg_end) * cw_ref[2:3, :]
               + _shift_rows(dxc, dprev, dnext, -2, seg_start, seg_end) * cw_ref[3:4, :])
        dp_ref[:, 4 * RET_W:4 * RET_W + LRU_W] = dxr.astype(BF16)
        dp_ref[:, 4 * RET_W + LRU_W:] = dgate_ref[...].astype(BF16)
        xr, xp, xn = x_ref[...], xp_ref[...], xn_ref[...]
        for j, k in enumerate((-1, 0, 1, 2)):
            xs = xr if k == 0 else _shift_rows(xr, xp, xn, k, seg_start, seg_end)
            acc_ref[j:j + 1, 0:LRU_W] += jnp.sum(dxc * xs, axis=0, keepdims=True)
        acc_ref[4:5, 0:LRU_W] += jnp.sum(dxc, axis=0, keepdims=True)

    t = pl.BlockSpec((TM, RET_W), lambda s: (s, 0))
    args = (dqs[0], dqs[1], dks[0], dks[1], dvs[0], dvs[1], dg, dgate, dxcs[0], dxcs[0], dxcs[0], dxcs[1], dxcs[1], dxcs[1],
            p, p, p, cw, cosf, sins)
    tab = pl.BlockSpec((TM, HEAD_DIM), lambda s: (s, 0))
    return _pcall_ride(
        body, ride, args, name="assemble_dp", grid=(nb,),
        in_specs=[t] * 8 + _halo_specs(tile_of, n, 0) * 2 + _halo_specs(tile_of, n, COL_XR)
        + [pl.BlockSpec((4, LRU_W), lambda s: (0, 0)), tab, tab],
        out_specs=(pl.BlockSpec((TM, 4 * RET_W + 2 * LRU_W), lambda s: (s, 0)), pl.BlockSpec((8, 2 * LRU_W), lambda s: (0, 0))),
        out_shape=(_sds((n, 4 * RET_W + 2 * LRU_W), BF16), _sds((8, 2 * LRU_W), F32)), compiler_params=_cp())


def _adamw(g, w, m, v):
    nm = ADAM_B1 * m + (1.0 - ADAM_B1) * g
    nv = ADAM_B2 * v + (1.0 - ADAM_B2) * (g * g)
    m_hat = nm / (1.0 - ADAM_B1 ** ADAM_STEP)
    v_hat = nv / (1.0 - ADAM_B2 ** ADAM_STEP)
    return (-ADAM_LR) * (m_hat / (jnp.sqrt(v_hat) + ADAM_EPS) + ADAM_WD * w), nm, nv


def _adam_many(items, name):
    n = len(items)

    def body(*refs):
        for i in range(n):
            g, w, m, v = (r[...] for r in refs[4 * i:4 * i + 4])
            for o_ref, val in zip(refs[4 * n + 3 * i:4 * n + 3 * i + 3], _adamw(g, w, m, v)):
                o_ref[...] = val

    out_shape = tuple(_sds(it[1].shape, F32) for it in items for _ in range(3))
    res = _pcall(body, name=name, out_shape=out_shape, compiler_params=_cp())(*[a for it in items for a in it])
    return [tuple(res[3 * i:3 * i + 3]) for i in range(n)]


def _sum_adam(parts_list, w, m, v, name):
    nparts, _, c = parts_list[0].shape
    r = w.shape[0]
    tr = min(parts_list[0].shape[1], 128)
    starts, o = [], 0
    for pa in parts_list:
        starts.append(o)
        o += pa.shape[1] // tr
    nseg = len(parts_list)

    def body(*refs):
        p_refs = refs[:nseg]
        w_ref, m_ref, v_ref, g_ref, d_ref, nm_ref, nv_ref = refs[nseg:]
        i = pl.program_id(0)
        for s, p_ref in enumerate(p_refs):
            end = starts[s + 1] if s + 1 < nseg else r // tr

            @pl.when(jnp.logical_and(i >= starts[s], i < end))
            def _():
                g = p_ref[0].astype(F32)
                for j in range(1, nparts):
                    g = g + p_ref[j].astype(F32)
                g_ref[...] = g
                d_ref[...], nm_ref[...], nv_ref[...] = _adamw(g, w_ref[...], m_ref[...], v_ref[...])

    def seg_spec(s):
        last = parts_list[s].shape[1] // tr - 1
        return pl.BlockSpec((nparts, tr, c), lambda i: (0, jnp.clip(i - starts[s], 0, last), 0))

    t = pl.BlockSpec((tr, c), lambda i: (i, 0))
    return _pcall(
        body, name=name, grid=(r // tr,),
        in_specs=[seg_spec(s) for s in range(nseg)] + [t, t, t],
        out_specs=(t, t, t, t), out_shape=(_sds((r, c), F32),) * 4, compiler_params=_cp(),
    )(*parts_list, w, m, v)


def _sum_parts(parts, name):
    nparts, r, c = parts.shape

    def body(p_ref, o_ref):
        g = p_ref[0]
        for j in range(1, nparts):
            g = g + p_ref[j]
        o_ref[...] = g

    return _pcall(body, name=name, out_shape=_sds((r, c), parts.dtype), compiler_params=_cp())(parts)


def _rot_tables(l_len, t_len):
    rows = t_len // GRID_W
    n_freq = HEAD_DIM // 4
    inv = ROPE_BASE ** (-jnp.arange(n_freq, dtype=F32) / n_freq)
    ang_r = jnp.arange(rows, dtype=F32)[:, None] * inv
    ang_c = jnp.arange(GRID_W, dtype=F32)[:, None] * inv
    cos = jnp.concatenate([jnp.repeat(jnp.cos(ang_r), GRID_W, axis=0), jnp.tile(jnp.cos(ang_c), (rows, 1))], axis=-1)
    sin = jnp.concatenate([jnp.repeat(jnp.sin(ang_r), GRID_W, axis=0), jnp.tile(jnp.sin(ang_c), (rows, 1))], axis=-1)
    cosf = jnp.concatenate([jnp.ones((l_len, HEAD_DIM), F32), jnp.concatenate([cos, cos], axis=-1)], axis=0)
    sins = jnp.concatenate([jnp.zeros((l_len, HEAD_DIM), F32), jnp.concatenate([-sin, sin], axis=-1)], axis=0)
    return cosf, sins


def _block_diag(w):
    eye = jnp.eye(LRU_BLOCKS, dtype=w.dtype)
    return (w[:, :, None, :] * eye[:, None, :, None]).reshape(LRU_W, LRU_W)


def _blocks_from_lanes(dwd_half):
    return dwd_half.reshape(LRU_BD, LRU_BLOCKS, LRU_BD).transpose(1, 0, 2)


def _silu(x):
    return x * jax.nn.sigmoid(x)


def kernel(x, c, ctx, c_ctx, w_ada, b_ada, norm1_g, norm2_g, w_in, ret_decay, conv_w, conv_b, lru_wa, lru_ba, lru_wx, lru_bx, lru_lambda, w_out, w_mlp1, w_mlp2, final_g, loss_target, m_c_ctx, m_w_ada, m_b_ada, m_norm1_g, m_norm2_g, m_w_in, m_ret_decay, m_conv_w, m_conv_b, m_lru_wa, m_lru_ba, m_lru_wx, m_lru_bx, m_lru_lambda, m_w_out, m_w_mlp1, m_w_mlp2, m_final_g, v_c_ctx, v_w_ada, v_b_ada, v_norm1_g, v_norm2_g, v_w_in, v_ret_decay, v_conv_w, v_conv_b, v_lru_wa, v_lru_ba, v_lru_wx, v_lru_bx, v_lru_lambda, v_w_out, v_w_mlp1, v_w_mlp2, v_final_g):
    t_len, d = x.shape[1], x.shape[2]
    l_len = ctx.shape[1]
    cb, cc = l_len // TM, l_len // CHUNK
    me = 4 * lax.axis_index("x") + 2 * lax.axis_index("y") + lax.axis_index("c")
    x2d, ctx2d, tgt2d = x[0], ctx[0], loss_target[0]
    ada_cols = w_ada.shape[2]
    wa2d = w_ada[0]

    sc_loc = conv_w.shape[2]
    pack_a = jnp.zeros((8, d), F32)
    pack_a = pack_a.at[0].set(_silu(c[0]))
    pack_a = pack_a.at[1, :4 * sc_loc].set(conv_w[0].reshape(-1))
    pack_a = pack_a.at[2, :2 * sc_loc].set(lru_ba[0].reshape(-1))
    pack_a = pack_a.at[3, :2 * sc_loc].set(lru_bx[0].reshape(-1))
    pack_a = pack_a.at[4, :2 * sc_loc].set(lru_lambda[0].reshape(-1))
    all_a = _all_gather_small(pack_a, "gather_small_in")
    s16 = jnp.zeros((16, d), F32).at[0:8].set(all_a[:, 0, :]).at[8].set(_silu(c_ctx))

    def unshard(row, k):
        return all_a[:, row, :k * sc_loc].reshape(N_DEV, k, sc_loc).transpose(1, 0, 2).reshape(k, N_DEV * sc_loc)

    conv_w_full = unshard(1, 4)
    ba_full, bx_full, lam_full = unshard(2, 2), unshard(3, 2), unshard(4, 2)

    b_cols = lax.dynamic_slice(b_ada, (0, me * ada_cols), (1, ada_cols))
    mod_parts = _all_gather_small(_mod_part(s16, wa2d, b_cols), "gather_mod")
    mod_all = mod_parts.transpose(1, 0, 2).reshape(16, N_DEV * ada_cols)
    mod_me = lax.dynamic_slice(mod_all, (me, 0), (1, 6 * d)).reshape(6, d)
    mod_c = mod_all[8].reshape(6, d)
    modrows = jnp.concatenate([mod_c[0:2], mod_me], axis=0)

    lg = jax.nn.log_sigmoid(ret_decay[0])
    sp = jax.nn.softplus(-lam_full)
    wg = [jnp.concatenate([_block_diag(lru_wa[0, dd]), _block_diag(lru_wx[0, dd])], axis=1).astype(BF16) for dd in (0, 1)]
    bg = [jnp.concatenate([ba_full[dd], bx_full[dd]])[None, :] for dd in (0, 1)]
    cosf, sins = _rot_tables(l_len, t_len)

    (hn, hnt), win_g = _norm1_fwd(ctx2d, x2d, norm1_g, modrows, cb, ride=("gather", w_in[0].astype(BF16)))
    (qkv, p), w2_g = _mm_in(hn, win_g, cosf, sins, ride=("gather", w_mlp2[0].astype(BF16)))
    w2_g = w2_g.reshape(1, 4 * d, d)
    (o0, o1, sp0, sp1), w1_g = _ret_fwd(qkv, lg, cc, ride=("gather", w_mlp1[0].astype(BF16)))
    o, s_prev = [o0, o1], [sp0, sp1]
    (h0, cin0), wout_g = _lru_fwd(p, wg[0], bg[0], sp[0:1], conv_w_full, conv_b, 0, cb, ride=("gather", w_out[0].astype(BF16)))
    wout_g = wout_g.reshape(1, d, d)
    (h1, cin1), _ = _lru_fwd(p, wg[1], bg[1], sp[1:2], conv_w_full, conv_b, 1, cb)
    h, cin = [h0, h1], [cin0, cin1]
    mix = _mix_fwd(o[0], o[1], p, h[0], h[1], cb, t_len)
    y, x1, h2, h2t = _mm_nn(mix, wout_g, F32, "mm_out_norm2", tm=TL, epi=_res_norm2_epilogue(x2d, norm2_g, modrows, TL))
    r = _mm_nn(h2, w1_g, BF16, "mm_mlp1", relu_out=True, tm=TL)
    dx2, dz, dzt, facc = _mm_nn(r, w2_g, F32, "mm_mlp2_final", square_lhs=True, tm=TL, vmem_mb=58,
                                epi=_final_epilogue(x1, tgt2d, final_g[None, :], modrows, TL))

    du = _mm_nt(dz, w2_g, BF16, "mm_da2", relu_mul=r, tm=TL, vmem_mb=58)
    gw2_lo, gw2_hi = _mm_wgrad(dzt, r, "mm_dw2", w2_g.shape[1] // N_DEV, BF16, transpose_out=True, square_rhs=True,
                               halves=True)
    gw1_lo, gw1_hi = _mm_wgrad(h2t, du, "mm_dw1", w1_g.shape[2], BF16, halves=True)
    (dx1, dy, n2acc), gw2_lo_all = _mm_nt(du, w1_g, F32, "mm_dh2_norm2", ride=("a2a", gw2_lo), tm=TL, vmem_mb=60,
                                          epi=_bwd_norm2_epilogue(x1, dx2, y, norm2_g, modrows, TL))
    gwo = _mm_tn(mix, dy, "mm_dwout", False, d, BF16, 512).reshape(N_DEV, -1, d)
    (do, dg, dgate, dhs), gw2_hi_all = _mm_nt(dy, wout_g, F32, "mm_dmix_mix", ride=("a2a", gw2_hi),
                                              epi=_mix_bwd_epilogue(o[0], o[1], p, h[0], h[1], cb))
    dxcs, dwgs, laccs, rides, got = [], [], [], [("a2a", gw1_hi), ("a2a", gwo)], []
    (dq0, dk0, dv0, dq1, dk1, dv1, dlg_lanes), gw1_lo_all = _ret_bwd(qkv, lg, do, s_prev, cc, ride=("a2a", gw1_lo))
    dqs, dks, dvs = [dq0, dq1], [dk0, dk1], [dv0, dv1]
    for dd in (0, 1):
        (dxc_, dwg_, lacc_), got_ = _lru_bwd(p, wg[dd], bg[dd], sp[dd:dd + 1], conv_w_full, conv_b, h[dd], cin[dd], dhs, dd, cb,
                                             ride=rides[dd])
        dxcs.append(dxc_); dwgs.append(dwg_); laccs.append(lacc_); got.append(got_)
    gw1_hi_all, gwo_all = got
    gw1_all, gw2_all = [gw1_lo_all, gw1_hi_all], [gw2_lo_all, gw2_hi_all]
    (dp, cacc), _ = _assemble_dp(dqs, dks, dvs, dg, dgate, dxcs, p, conv_w_full, cosf, sins, cb)
    pack_b = jnp.concatenate([n2acc, facc, cacc, laccs[0], laccs[1], dlg_lanes, dwgs[0], dwgs[1]], axis=0)
    gwi, all_b = _mm_wgrad(hnt, dp, "mm_dwin", win_g.shape[2], BF16, ride=("gather", pack_b))
    (grad_x, n1acc), gwi_all = _mm_nt(dp, win_g, F32, "mm_dhn_norm1", ride=("a2a", gwi),
                                      epi=_bwd_norm1_epilogue(ctx2d, x2d, dx1, norm1_g, modrows, cb))
    all_n1 = _all_gather_small(n1acc, "gather_norm1_grads")
    tot = _sum_parts(all_b, "sum_small_grads")
    t_n1 = _sum_parts(all_n1, "sum_norm1_grads")
    t_n2, t_f, t_conv, t_dlg = tot[0:8], tot[8:16], tot[16:24, :LRU_W], tot[40:48, :HEAD_DIM]
    t_l = [tot[24:32], tot[32:40]]
    t_dwd = [tot[48:48 + LRU_BD], tot[48 + LRU_BD:48 + 2 * LRU_BD]]
    loss = (0.5 / d) * jnp.sum(t_f[2])
    t_wa = jnp.stack([_blocks_from_lanes(t_dwd[dd][:, :LRU_W]) for dd in (0, 1)])
    t_wx = jnp.stack([_blocks_from_lanes(t_dwd[dd][:, LRU_W:]) for dd in (0, 1)])
    t_ba = jnp.stack([t_l[dd][0, :LRU_W] for dd in (0, 1)])
    t_bx = jnp.stack([t_l[dd][0, LRU_W:] for dd in (0, 1)])
    t_sp = jnp.stack([t_l[dd][1, :LRU_W] for dd in (0, 1)])
    dm_rows = jnp.stack([all_n1[:, 2, :], all_n1[:, 3, :], all_b[:, 3, :], all_b[:, 0, :], all_b[:, 1, :], all_b[:, 9, :]],
                        axis=1).reshape(N_DEV, 6 * d)
    dm_c = jnp.concatenate([t_n1[0], t_n1[1], jnp.zeros((4 * d,), F32)])
    dm16 = jnp.zeros((16, 6 * d), F32).at[0:8].set(dm_rows).at[8].set(dm_c)
    g_b_ada = jnp.sum(dm16, axis=0)[None, :]
    dm_cols = lax.dynamic_slice(dm16, (0, me * ada_cols), (16, ada_cols))
    g_w_ada, ds16 = _ada_bwd(s16, dm_cols, wa2d)
    ds_all = _all_gather_small(ds16[8:16], "gather_dsilu")
    dsilu_cc = _sum_parts(ds_all, "sum_dsilu")[0]
    sg_cc = jax.nn.sigmoid(c_ctx)
    g_c_ctx = dsilu_cc * (sg_cc * (1.0 + c_ctx * (1.0 - sg_cc)))

    g_ret_decay = jnp.sum(t_dlg, axis=-1).reshape(2, HEADS) * jax.nn.sigmoid(-ret_decay[0])
    g_lambda_full = -t_sp * jax.nn.sigmoid(-lam_full)

    def my_cols(full):
        return lax.dynamic_slice(full, (0, me * sc_loc), (full.shape[0], sc_loc))

    small_g = dict(
        c_ctx=g_c_ctx[None], b_ada=g_b_ada, norm1_g=t_n1[4:5], norm2_g=t_n2[2:3], ret_decay=g_ret_decay,
        conv_w=my_cols(t_conv[0:4]), conv_b=t_conv[4:5], lru_wa=t_wa.reshape(-1, LRU_BD), lru_ba=my_cols(t_ba),
        lru_wx=t_wx.reshape(-1, LRU_BD), lru_bx=my_cols(t_bx), lru_lambda=my_cols(g_lambda_full), final_g=t_f[0:1])
    small = dict(
        c_ctx=(c_ctx, m_c_ctx, v_c_ctx), b_ada=(b_ada, m_b_ada, v_b_ada), norm1_g=(norm1_g, m_norm1_g, v_norm1_g),
        norm2_g=(norm2_g, m_norm2_g, v_norm2_g), ret_decay=(ret_decay, m_ret_decay, v_ret_decay),
        conv_w=(conv_w, m_conv_w, v_conv_w), conv_b=(conv_b, m_conv_b, v_conv_b), lru_wa=(lru_wa, m_lru_wa, v_lru_wa),
        lru_ba=(lru_ba, m_lru_ba, v_lru_ba), lru_wx=(lru_wx, m_lru_wx, v_lru_wx), lru_bx=(lru_bx, m_lru_bx, v_lru_bx),
        lru_lambda=(lru_lambda, m_lru_lambda, v_lru_lambda), final_g=(final_g, m_final_g, v_final_g))
    names = list(small)
    items = [(small_g[k],) + tuple(a.reshape(small_g[k].shape) for a in small[k]) for k in names]
    res = {}
    for k, it, (d_, m_, v_) in zip(names, items, _adam_many(items, "adam_small")):
        shape = small[k][0].shape
        res[k] = tuple(a.reshape(shape) for a in (it[0], d_, m_, v_))

    def big(parts, w, m, v, name):
        out = _sum_adam(parts, w[0], m[0], v[0], name)
        return tuple(a[None] for a in out)

    res["w_ada"] = big([g_w_ada[None]], w_ada, m_w_ada, v_w_ada, "adam_w_ada")
    res["w_in"] = big([gwi_all], w_in, m_w_in, v_w_in, "adam_w_in")
    res["w_out"] = big([gwo_all], w_out, m_w_out, v_w_out, "adam_w_out")
    res["w_mlp1"] = big(gw1_all, w_mlp1, m_w_mlp1, v_w_mlp1, "adam_w_mlp1")
    res["w_mlp2"] = big(gw2_all, w_mlp2, m_w_mlp2, v_w_mlp2, "adam_w_mlp2")

    order = ["c_ctx", "w_ada", "b_ada", "norm1_g", "norm2_g", "w_in", "ret_decay", "conv_w", "conv_b", "lru_wa", "lru_ba",
             "lru_wx", "lru_bx", "lru_lambda", "w_out", "w_mlp1", "w_mlp2", "final_g"]
    outs = [loss, grad_x[None]]
    for j in range(4):
        outs += [res[k][j] for k in order]
    return tuple(outs)
```

```python
import jax
import jax.numpy as jnp
from jax import lax
from jax.experimental import pallas as pl
from jax.experimental.pallas import tpu as pltpu

F32 = jnp.float32
BF16 = jnp.bfloat16
AXES = ("x", "y", "c")
N_DEV = 8
MESH = pl.DeviceIdType.MESH

HEADS = 4
HEAD_DIM = 128
CHUNK = 128
RET_W = HEADS * HEAD_DIM
LRU_W = 512
LRU_BLOCKS = 8
LRU_BD = LRU_W // LRU_BLOCKS
LRU_C = 8.0
EPS = 1e-6
K_SCALE = HEAD_DIM ** -0.5
ROPE_BASE = 10000.0
GRID_W = 64
TM = 256
TL = 512
SUB = 8

ADAM_LR = 0.001
ADAM_B1 = 0.9
ADAM_B2 = 0.999
ADAM_EPS = 1e-08
ADAM_WD = 0.01
ADAM_STEP = 10

COL_G, COL_XR, COL_GATE = 0, 1, 2

R_CSH1, R_CSC1, R_SH1, R_SC1, R_G1, R_SH2, R_SC2, R_G2 = range(8)


def _pcall(body, **kw):
    return pl.pallas_call(body, **kw)


def _cp(vmem_mb=48):
    return pltpu.CompilerParams(vmem_limit_bytes=vmem_mb << 20)


def _sds(shape, dtype):
    return jax.ShapeDtypeStruct(shape, dtype)


def _dot(a, b):
    return jnp.dot(a.astype(BF16), b.astype(BF16), preferred_element_type=F32)


def _dot_nt(a, b):
    return lax.dot_general(a.astype(BF16), b.astype(BF16), (((1,), (1,)), ((), ())), preferred_element_type=F32)


def _dot_tn(a, b):
    return lax.dot_general(a.astype(BF16), b.astype(BF16), (((0,), (0,)), ((), ())), preferred_element_type=F32)


def _sigmoid(x):
    return 0.5 * jnp.tanh(0.5 * x) + 0.5


def _gelu(x):
    return 0.5 * x * (1.0 + jnp.tanh(0.7978845608028654 * (x + 0.044715 * x * x * x)))


def _dgelu(x):
    t = jnp.tanh(0.7978845608028654 * (x + 0.044715 * x * x * x))
    return 0.5 * (1.0 + t) + 0.5 * x * (1.0 - t * t) * 0.7978845608028654 * (1.0 + 3.0 * 0.044715 * x * x)


def _rows_iota(shape):
    return lax.broadcasted_iota(jnp.int32, shape, 0)


def _tile_order(dirn, s, cb, nb):
    if dirn == 0:
        return s
    return jnp.where(s < cb, cb - 1 - s, nb - 1 - (s - cb))


_SEMS = [pltpu.SemaphoreType.DMA((7,)), pltpu.SemaphoreType.DMA((7,)), pltpu.SemaphoreType.DMA(())]
_ANY = pl.BlockSpec(memory_space=pl.ANY)


def _gather_copies(x_ref, out_ref, send_sems, recv_sems, local_sem):
    mx, my, mc = lax.axis_index("x"), lax.axis_index("y"), lax.axis_index("c")
    me, sibling = (mx, my, mc), (mx, my, 1 - mc)
    chips = [(1 - mx, my), (mx, 1 - my), (1 - mx, 1 - my)]

    def slot(px, py, pc):
        return out_ref.at[4 * px + 2 * py + pc]

    def copy(k, block, to, src=None):
        return pltpu.make_async_remote_copy(
            src_ref=slot(*block) if src is None else src, dst_ref=slot(*block),
            send_sem=send_sems.at[k], recv_sem=recv_sems.at[k], device_id=to, device_id_type=MESH)

    mine = pltpu.make_async_copy(x_ref, slot(*me), local_sem)
    first = [copy(0, me, sibling, src=x_ref)] + [copy(1 + j, me, (*chip, mc), src=x_ref) for j, chip in enumerate(chips)]
    passed = [copy(4 + j, (*chip, mc), sibling) for j, chip in enumerate(chips)]
    recv_ici = [copy(1 + j, (*chip, mc), me) for j, chip in enumerate(chips)]
    recv_d2d = [copy(0, sibling, me)] + [copy(4 + j, (*chip, 1 - mc), me) for j, chip in enumerate(chips)]
    return mine, first, passed, recv_ici, recv_d2d


def _gather_start(*refs):
    mine, first, _, _, _ = _gather_copies(*refs)
    mine.start()
    for cp in first:
        cp.start()


def _gather_finish(*refs):
    mine, first, passed, recv_ici, recv_d2d = _gather_copies(*refs)
    for landed, onward in zip(recv_ici, passed):
        landed.wait_recv()
        onward.start()
    for landed in recv_d2d:
        landed.wait_recv()
    for cp in first + passed:
        cp.wait_send()
    mine.wait()


def _a2a_copies(g_ref, out_ref, send_sems, recv_sems, local_sem):
    mx, my, mc = lax.axis_index("x"), lax.axis_index("y"), lax.axis_index("c")
    me = 4 * mx + 2 * my + mc
    mine = pltpu.make_async_copy(g_ref.at[me], out_ref.at[me], local_sem)
    copies = []
    for k in range(1, N_DEV):
        px = 1 - mx if (k >> 2) & 1 else mx
        py = 1 - my if (k >> 1) & 1 else my
        pc = 1 - mc if k & 1 else mc
        copies.append(pltpu.make_async_remote_copy(
            src_ref=g_ref.at[4 * px + 2 * py + pc], dst_ref=out_ref.at[me],
            send_sem=send_sems.at[k - 1], recv_sem=recv_sems.at[k - 1],
            device_id=(px, py, pc), device_id_type=MESH))
    return mine, copies


def _a2a_start(*refs):
    mine, copies = _a2a_copies(*refs)
    mine.start()
    for cp in copies:
        cp.start()


def _a2a_finish(*refs):
    mine, copies = _a2a_copies(*refs)
    for cp in copies:
        cp.wait()
    mine.wait()


_EXCHANGES = {"gather": (_gather_start, _gather_finish), "a2a": (_a2a_start, _a2a_finish)}


def _exchange_shape(kind, src):
    return _sds((N_DEV,) + src.shape if kind == "gather" else src.shape, src.dtype)


def _all_gather_small(x, name):
    def body(x_ref, out_ref, send_sems, recv_sems, local_sem):
        mx, my, mc = lax.axis_index("x"), lax.axis_index("y"), lax.axis_index("c")
        me = 4 * mx + 2 * my + mc
        mine = pltpu.make_async_copy(x_ref, out_ref.at[me], local_sem)
        mine.start()
        copies = []
        for k in range(1, N_DEV):
            peer = (1 - mx if (k >> 2) & 1 else mx, 1 - my if (k >> 1) & 1 else my, 1 - mc if k & 1 else mc)
            copies.append(pltpu.make_async_remote_copy(
                src_ref=x_ref, dst_ref=out_ref.at[me], send_sem=send_sems.at[k - 1], recv_sem=recv_sems.at[k - 1],
                device_id=peer, device_id_type=MESH))
            copies[-1].start()
        for cp in copies:
            cp.wait()
        mine.wait()

    return _pcall(body, name=name, out_shape=_exchange_shape("gather", x), in_specs=[_ANY], out_specs=_ANY,
                  scratch_shapes=list(_SEMS))(x)


def _pcall_ride(body, ride, args, *, name, grid, in_specs, out_specs, out_shape, scratch_shapes=(), compiler_params=None):
    if ride is None:
        out = _pcall(body, name=name, grid=grid, in_specs=in_specs, out_specs=out_specs, out_shape=out_shape,
                     scratch_shapes=list(scratch_shapes), compiler_params=compiler_params)(*args)
        return out, None
    kind, src = ride
    start, finish = _EXCHANGES[kind]
    single = not isinstance(out_shape, (tuple, list))
    out_specs_t = (out_specs,) if single else tuple(out_specs)
    out_shape_t = (out_shape,) if single else tuple(out_shape)
    n_in, n_out, n_sc = len(in_specs), len(out_shape_t), len(scratch_shapes)

    def wrapped(*refs):
        ins, src_ref = refs[:n_in], refs[n_in]
        outs, dst_ref = refs[n_in + 1:n_in + 1 + n_out], refs[n_in + 1 + n_out]
        scratch = refs[n_in + 2 + n_out:n_in + 2 + n_out + n_sc]
        sems = refs[n_in + 2 + n_out + n_sc:]
        first = pl.program_id(0) == 0
        last = pl.program_id(0) == grid[0] - 1
        for ax in range(1, len(grid)):
            first = jnp.logical_and(first, pl.program_id(ax) == 0)
            last = jnp.logical_and(last, pl.program_id(ax) == grid[ax] - 1)

        @pl.when(first)
        def _():
            start(src_ref, dst_ref, *sems)

        body(*ins, *outs, *scratch)

        @pl.when(last)
        def _():
            finish(src_ref, dst_ref, *sems)

    res = _pcall(wrapped, name=name, grid=grid, in_specs=list(in_specs) + [_ANY], out_specs=out_specs_t + (_ANY,),
                 out_shape=out_shape_t + (_exchange_shape(kind, src),),
                 scratch_shapes=list(scratch_shapes) + list(_SEMS), compiler_params=compiler_params)(*args, src)
    return (res[0] if single else tuple(res[:-1])), res[-1]


class _Epilogue:
    def __init__(self, fn, args, in_specs, out_specs, out_shape, steps=None, lhs_map=None, delay=0):
        self.fn, self.args, self.in_specs, self.out_specs, self.out_shape = fn, tuple(args), list(in_specs), out_specs, out_shape
        self.steps, self.lhs_map, self.delay = steps, lhs_map, delay


def _mm_nn(a, w, out_dtype, name, square_lhs=False, relu_out=False, ride=None, tm=TM, epi=None, vmem_mb=48):
    m, k = a.shape
    nb, _, bn = w.shape
    tm = min(tm, m)
    delay = 0 if epi is None else epi.delay
    steps = m // tm

    def body(*refs):
        a_ref, w_ref = refs[:2]
        av = a_ref[...]
        if square_lhs:
            av = av * av
        if delay:
            assert nb == 1
            held = refs[-1]

            @pl.when(pl.program_id(0) == 0)
            def _():
                held[...] = jnp.zeros_like(held)

            fresh = jnp.dot(av, w_ref[0], preferred_element_type=F32)
            epi.fn(held[...], slice(0, tm), *refs[2:-1])
            held[...] = fresh
            return
        if epi is not None:
            assert nb == 1
            epi.fn(jnp.dot(av, w_ref[0], preferred_element_type=F32), slice(0, tm), *refs[2:])
            return
        for j in range(nb):
            r = jnp.dot(av, w_ref[j], preferred_element_type=F32)
            if relu_out:
                r = jnp.maximum(r, 0.0)
            refs[2][:, j * bn:(j + 1) * bn] = r.astype(out_dtype)

    lhs_map = (lambda i: (i, 0)) if not delay else (lambda i: (jnp.minimum(i, steps - 1), 0))
    in_specs = [pl.BlockSpec((tm, k), lhs_map), pl.BlockSpec((nb, k, bn), lambda i: (0, 0, 0))]
    scratch = []
    if epi is None:
        args, out_specs, out_shape = (a, w), pl.BlockSpec((tm, nb * bn), lambda i: (i, 0)), _sds((m, nb * bn), out_dtype)
    else:
        args, out_specs, out_shape = (a, w) + epi.args, epi.out_specs, epi.out_shape
        in_specs += epi.in_specs
        if delay:
            scratch.append(pltpu.VMEM((tm, bn), F32))
    out, ex = _pcall_ride(body, ride, args, name=name, grid=(steps + delay,), in_specs=in_specs, out_specs=out_specs,
                          out_shape=out_shape, scratch_shapes=scratch, compiler_params=_cp(vmem_mb))
    return out if ride is None else (out, ex)


def _mm_nt(dy, w, out_dtype, name, relu_mul=None, ride=None, tm=TM, epi=None, vmem_mb=48):
    m = dy.shape[0]
    nb, k, bn = w.shape
    tm = min(tm, m)
    n_extra = (0 if relu_mul is None else 1) + (0 if epi is None else len(epi.args))

    delay = 0 if epi is None else epi.delay

    def body(*refs):
        dy_ref, w_ref = refs[:2]
        scratch = refs[-1 - delay:]
        extra, outs, wt = refs[2:2 + n_extra], refs[2 + n_extra:-1 - delay], scratch[0]

        @pl.when(pl.program_id(0) == 0)
        def _():
            for j in range(nb):
                wt[j * bn:(j + 1) * bn, :] = w_ref[j].T
            if delay:
                scratch[1][...] = jnp.zeros_like(scratch[1])

        if delay:
            held = scratch[1]
            fresh = jnp.dot(dy_ref[...], wt[...], preferred_element_type=F32)
            epi.fn(held[...], slice(0, tm), *extra, *outs)
            held[...] = fresh
            return
        acc = jnp.dot(dy_ref[...], wt[...], preferred_element_type=F32)
        if epi is not None:
            epi.fn(acc, slice(0, tm), *extra, *outs)
            return
        if relu_mul is not None:
            acc = acc * (2.0 * extra[0][...].astype(F32))
        outs[0][...] = acc.astype(out_dtype)

    steps = m // tm if epi is None or epi.steps is None else epi.steps
    lhs_tile = (lambda i: (i, 0)) if epi is None or epi.lhs_map is None else epi.lhs_map
    lhs_map = lhs_tile if not delay else (lambda i: lhs_tile(jnp.minimum(i, steps - 1)))
    in_specs = [pl.BlockSpec((tm, nb * bn), lhs_map), pl.BlockSpec((nb, k, bn), lambda i: (0, 0, 0))]
    args = [dy, w]
    if relu_mul is not None:
        in_specs.append(pl.BlockSpec((tm, k), lambda i: (i, 0)))
        args.append(relu_mul)
    scratch = [pltpu.VMEM((nb * bn, k), BF16)]
    if epi is None:
        out_specs, out_shape = pl.BlockSpec((tm, k), lambda i: (i, 0)), _sds((m, k), out_dtype)
    else:
        args += list(epi.args)
        in_specs += epi.in_specs
        out_specs, out_shape = epi.out_specs, epi.out_shape
        if delay:
            scratch.append(pltpu.VMEM((tm, k), F32))
    out, ex = _pcall_ride(
        body, ride, args, name=name, grid=(steps + delay,), in_specs=in_specs, out_specs=out_specs, out_shape=out_shape,
        scratch_shapes=scratch, compiler_params=_cp(vmem_mb))
    return out if ride is None else (out, ex)


def _mm_tn(a, b, name, col_blocks, block, out_dtype, tm, square_lhs=False):
    m, k = a.shape
    nn = b.shape[1]
    steps = m // tm
    if col_blocks:
        nblk, acc_shape = nn // block, (k, block)
        a_spec = pl.BlockSpec((tm, k), lambda j, s: (s, 0))
        b_spec = pl.BlockSpec((tm, block), lambda j, s: (s, j))
    else:
        nblk, acc_shape = k // block, (block, nn)
        a_spec = pl.BlockSpec((tm, block), lambda j, s: (s, j))
        b_spec = pl.BlockSpec((tm, nn), lambda j, s: (s, 0))

    def body(a_ref, b_ref, o_ref, acc):
        s = pl.program_id(1)

        @pl.when(s == 0)
        def _():
            acc[...] = jnp.zeros_like(acc)

        av = a_ref[...]
        if square_lhs:
            av = av.astype(F32)
            av = (av * av).astype(BF16)
        acc[...] += _dot_tn(av, b_ref[...])

        @pl.when(s == steps - 1)
        def _():
            o_ref[...] = acc[...].astype(out_dtype)

    return _pcall(
        body, name=name, grid=(nblk, steps), in_specs=[a_spec, b_spec],
        out_specs=pl.BlockSpec((None,) + acc_shape, lambda j, s: (j, 0, 0)),
        out_shape=_sds((nblk,) + acc_shape, out_dtype),
        scratch_shapes=[pltpu.VMEM(acc_shape, F32)], compiler_params=_cp(),
    )(a, b)


def _mm_wgrad(at, b, name, bn, out_dtype, transpose_out=False, square_rhs=False, halves=False, ride=None):
    k, m = at.shape
    nblk = b.shape[1] // bn
    rows, cols = (bn, k) if transpose_out else (k, bn)
    nout = 2 if halves else 1
    per = rows // nout

    def body(a_ref, b_ref, *o_refs):
        bv = b_ref[...]
        if square_rhs:
            bv = bv * bv
        r = jnp.dot(a_ref[...], bv, preferred_element_type=F32)
        r = (r.T if transpose_out else r).astype(out_dtype)
        for i, o_ref in enumerate(o_refs):
            o_ref[...] = r[i * per:(i + 1) * per, :]

    out, ex = _pcall_ride(
        body, ride, (at, b), name=name, grid=(nblk,),
        in_specs=[pl.BlockSpec((k, m), lambda j: (0, 0)), pl.BlockSpec((m, bn), lambda j: (0, j))],
        out_specs=tuple(pl.BlockSpec((None, per, cols), lambda j: (j, 0, 0)) for _ in range(nout)),
        out_shape=tuple(_sds((nblk, per, cols), out_dtype) for _ in range(nout)), compiler_params=_cp())
    out = out if halves else out[0]
    return out if ride is None else (out, ex)


def _mm_in(hn, w, cosf, sins, ride):
    m, k = hn.shape
    nb, _, bn = w.shape

    def body(a_ref, w_ref, c_ref, s_ref, qkv_ref, rest_ref, pt):
        av = a_ref[...]
        for j in range(nb):
            pt[:, j * bn:(j + 1) * bn] = jnp.dot(av, w_ref[j], preferred_element_type=F32)
        cf, ss = c_ref[...], s_ref[...]
        for h in range(HEADS):
            sq = slice(h * HEAD_DIM, (h + 1) * HEAD_DIM)
            sk = slice(RET_W + h * HEAD_DIM, RET_W + (h + 1) * HEAD_DIM)
            qkv_ref[:, sq] = _rot(pt[:, sq], cf, ss).astype(BF16)
            qkv_ref[:, sk] = (_rot(pt[:, sk], cf, ss) * K_SCALE).astype(BF16)
        qkv_ref[:, 2 * RET_W:] = pt[:, 2 * RET_W:3 * RET_W].astype(BF16)
        rest_ref[...] = pt[:, 3 * RET_W:]

    tab = pl.BlockSpec((TM, HEAD_DIM), lambda i: (i, 0))
    wide = pl.BlockSpec((TM, 3 * RET_W), lambda i: (i, 0))
    return _pcall_ride(
        body, ride, (hn, w, cosf, sins), name="mm_in", grid=(m // TM,),
        in_specs=[pl.BlockSpec((TM, k), lambda i: (i, 0)), pl.BlockSpec((nb, k, bn), lambda i: (0, 0, 0)), tab, tab],
        out_specs=(wide, wide), out_shape=(_sds((m, 3 * RET_W), BF16), _sds((m, nb * bn - 3 * RET_W), F32)),
        scratch_shapes=[pltpu.VMEM((TM, nb * bn), F32)], compiler_params=_cp())


def _mod_part(s16, w_ada, b_cols):
    def body(s_ref, w_ref, b_ref, o_ref):
        o_ref[...] = _dot(s_ref[...], w_ref[...]) + b_ref[...]

    return _pcall(body, name="mod_part", out_shape=_sds((s16.shape[0], w_ada.shape[1]), F32),
                  compiler_params=_cp())(s16, w_ada, b_cols)


def _ada_bwd(s16, dm_cols, w_ada):
    def body(s_ref, d_ref, w_ref, gw_ref, ds_ref):
        gw_ref[...] = _dot_tn(s_ref[...], d_ref[...])
        ds_ref[...] = _dot_nt(d_ref[...], w_ref[...])

    return _pcall(body, name="ada_bwd",
                  out_shape=(_sds(w_ada.shape, F32), _sds(s16.shape, F32)), compiler_params=_cp())(s16, dm_cols, w_ada)


def _norm1_fwd(ctx, x, g, modrows, cb, ride=None):
    l_len, d = ctx.shape
    nb = (l_len + x.shape[0]) // TM

    def body(ctx_ref, x_ref, g_ref, m_ref, o_ref, ot_ref):
        is_ctx = pl.program_id(0) < cb
        xin = jnp.where(is_ctx, ctx_ref[...], x_ref[...])
        sh = jnp.where(is_ctx, m_ref[R_CSH1:R_CSH1 + 1, :], m_ref[R_SH1:R_SH1 + 1, :])
        sc = jnp.where(is_ctx, m_ref[R_CSC1:R_CSC1 + 1, :], m_ref[R_SC1:R_SC1 + 1, :])
        ms = jnp.mean(xin * xin, axis=-1, keepdims=True)
        n = xin * lax.rsqrt(ms + EPS) * g_ref[...]
        hn = n * (1.0 + sc) + sh
        o_ref[...] = hn.astype(BF16)
        ot_ref[...] = hn.T.astype(BF16)

    return _pcall_ride(
        body, ride, (ctx, x, g, modrows), name="norm1_fwd", grid=(nb,),
        in_specs=[pl.BlockSpec((TM, d), lambda i: (jnp.minimum(i, cb - 1), 0)),
                  pl.BlockSpec((TM, d), lambda i: (jnp.maximum(i - cb, 0), 0)),
                  pl.BlockSpec((1, d), lambda i: (0, 0)), pl.BlockSpec((8, d), lambda i: (0, 0))],
        out_specs=(pl.BlockSpec((TM, d), lambda i: (i, 0)), pl.BlockSpec((d, TM), lambda i: (0, i))),
        out_shape=(_sds((nb * TM, d), BF16), _sds((d, nb * TM), BF16)), compiler_params=_cp())


def _mix_fwd(o_f, o_b, p, h_f, h_b, cb, t_len):
    def body(of_ref, ob_ref, g_ref, gate_ref, hf_ref, hb_ref, mix_ref):
        o = of_ref[...] + ob_ref[...]
        g = g_ref[...]
        sg = g * _sigmoid(g)
        for hh in range(HEADS):
            sl = slice(hh * HEAD_DIM, (hh + 1) * HEAD_DIM)
            oh = o[:, sl]
            yc = oh - jnp.mean(oh, axis=-1, keepdims=True)
            var = jnp.mean(yc * yc, axis=-1, keepdims=True)
            mix_ref[:, sl] = (sg[:, sl] * (yc * lax.rsqrt(var + EPS))).astype(BF16)
        mix_ref[:, RET_W:] = ((hf_ref[...] + hb_ref[...]) * _gelu(gate_ref[...])).astype(BF16)

    row = lambda i: (i + cb, 0)
    return _pcall(
        body, name="mix_fwd", grid=(t_len // TM,),
        in_specs=[pl.BlockSpec((TM, RET_W), row), pl.BlockSpec((TM, RET_W), row),
                  pl.BlockSpec((TM, RET_W), lambda i: (i + cb, COL_G)), pl.BlockSpec((TM, LRU_W), lambda i: (i + cb, COL_GATE)),
                  pl.BlockSpec((TM, LRU_W), row), pl.BlockSpec((TM, LRU_W), row)],
        out_specs=pl.BlockSpec((TM, RET_W + LRU_W), lambda i: (i, 0)),
        out_shape=_sds((t_len, RET_W + LRU_W), BF16), compiler_params=_cp(),
    )(o_f, o_b, p, p, h_f, h_b)


def _res_norm2_epilogue(x, g, modrows, tm):
    t_len, d = x.shape
    tm = min(tm, t_len)

    def fn(y, rows, x_ref, g_ref, m_ref, y_ref, x1_ref, h2_ref, h2t_ref):
        y_ref[rows, :] = y
        x1 = x_ref[rows, :] + m_ref[R_G1:R_G1 + 1, :] * y
        ms = jnp.mean(x1 * x1, axis=-1, keepdims=True)
        n = x1 * lax.rsqrt(ms + EPS) * g_ref[...]
        x1_ref[rows, :] = x1
        h2 = n * (1.0 + m_ref[R_SC2:R_SC2 + 1, :]) + m_ref[R_SH2:R_SH2 + 1, :]
        h2_ref[rows, :] = h2.astype(BF16)
        h2t_ref[:, rows] = h2.T.astype(BF16)

    t = pl.BlockSpec((tm, d), lambda i: (i, 0))
    return _Epilogue(
        fn, (x, g, modrows),
        in_specs=[t, pl.BlockSpec((1, d), lambda i: (0, 0)), pl.BlockSpec((8, d), lambda i: (0, 0))],
        out_specs=(t, t, t, pl.BlockSpec((d, tm), lambda i: (0, i))),
        out_shape=(_sds((t_len, d), F32), _sds((t_len, d), F32), _sds((t_len, d), BF16), _sds((d, t_len), BF16)))


def _tile(i):
    return jnp.maximum(i - 1, 0)


def _zero_at_start(acc_ref, rows):
    @pl.when(pl.program_id(0) == 0)
    def _():
        acc_ref[...] = jnp.zeros_like(acc_ref)


def _final_epilogue(x1, target, fg, modrows, tm):
    t_len, d = x1.shape

    def fn(z, rows, x1_ref, t_ref, fg_ref, m_ref, dx2_ref, dz_ref, dzt_ref, acc_ref):
        _zero_at_start(acc_ref, rows)
        g2 = m_ref[R_G2:R_G2 + 1, :]
        x2 = x1_ref[rows, :] + g2 * z
        rstd = lax.rsqrt(jnp.mean(x2 * x2, axis=-1, keepdims=True) + EPS)
        xh = x2 * rstd
        fg = fg_ref[...]
        e = xh * fg - t_ref[rows, :]
        dy = e * (1.0 / d)
        dxh = dy * fg
        dx2 = rstd * (dxh - xh * jnp.mean(dxh * xh, axis=-1, keepdims=True))
        dx2_ref[rows, :] = dx2
        dz = g2 * dx2
        dz_ref[rows, :] = dz.astype(BF16)
        dzt_ref[:, rows] = dz.T.astype(BF16)
        acc_ref[0:1, :] += jnp.sum(dy * xh, axis=0, keepdims=True)
        acc_ref[1:2, :] += jnp.sum(dx2 * z, axis=0, keepdims=True)
        acc_ref[2:3, :] += jnp.sum(e * e, axis=0, keepdims=True)

    t = pl.BlockSpec((tm, d), lambda i: (i, 0))
    return _Epilogue(
        fn, (x1, target, fg, modrows),
        in_specs=[t, t, pl.BlockSpec((1, d), lambda i: (0, 0)), pl.BlockSpec((8, d), lambda i: (0, 0))],
        out_specs=(t, t, pl.BlockSpec((d, tm), lambda i: (0, i)), pl.BlockSpec((8, d), lambda i: (0, 0))),
        out_shape=(_sds((t_len, d), F32), _sds((t_len, d), BF16), _sds((d, t_len), BF16), _sds((8, d), F32)))


def _bwd_norm2_epilogue(x1, dx2, y, g, modrows, tm):
    t_len, d = x1.shape

    def fn(dh2, rows, x1_ref, dx2_ref, y_ref, g_ref, m_ref, dx1_ref, dy_ref, acc_ref):
        _zero_at_start(acc_ref, rows)
        x1 = x1_ref[rows, :]
        rstd = lax.rsqrt(jnp.mean(x1 * x1, axis=-1, keepdims=True) + EPS)
        xh = x1 * rstd
        gn = g_ref[...]
        dn = dh2 * (1.0 + m_ref[R_SC2:R_SC2 + 1, :])
        dxh = dn * gn
        dx1 = dx2_ref[rows, :] + rstd * (dxh - xh * jnp.mean(dxh * xh, axis=-1, keepdims=True))
        dx1_ref[rows, :] = dx1
        dy_ref[rows, :] = (m_ref[R_G1:R_G1 + 1, :] * dx1).astype(BF16)
        acc_ref[0:1, :] += jnp.sum(dh2, axis=0, keepdims=True)
        acc_ref[1:2, :] += jnp.sum(dh2 * xh * gn, axis=0, keepdims=True)
        acc_ref[2:3, :] += jnp.sum(dn * xh, axis=0, keepdims=True)
        acc_ref[3:4, :] += jnp.sum(dx1 * y_ref[rows, :], axis=0, keepdims=True)

    t = pl.BlockSpec((tm, d), lambda i: (i, 0))
    return _Epilogue(
        fn, (x1, dx2, y, g, modrows),
        in_specs=[t, t, t, pl.BlockSpec((1, d), lambda i: (0, 0)), pl.BlockSpec((8, d), lambda i: (0, 0))],
        out_specs=(t, t, pl.BlockSpec((8, d), lambda i: (0, 0))),
        out_shape=(_sds((t_len, d), F32), _sds((t_len, d), BF16), _sds((8, d), F32)))


def _mix_bwd_epilogue(o_f, o_b, p, h_f, h_b, cb):
    n = o_f.shape[0]
    tile = _tile

    def fn(dm, rows, of_ref, ob_ref, g_ref, gate_ref, hf_ref, hb_ref, do_ref, dg_ref, dgate_ref, dhs_ref):
        keep = jnp.where(pl.program_id(0) - 1 < cb, 0.0, 1.0)
        dm = dm * keep
        o = of_ref[rows, :] + ob_ref[rows, :]
        g = g_ref[rows, :]
        s = _sigmoid(g)
        sg = g * s
        dsg = s * (1.0 + g * (1.0 - s))
        for hh in range(HEADS):
            sl = slice(hh * HEAD_DIM, (hh + 1) * HEAD_DIM)
            oh = o[:, sl]
            yc = oh - jnp.mean(oh, axis=-1, keepdims=True)
            rs = lax.rsqrt(jnp.mean(yc * yc, axis=-1, keepdims=True) + EPS)
            gn = yc * rs
            dret = dm[:, sl]
            dgn = dret * sg[:, sl]
            dg_ref[rows, sl] = (dret * gn * dsg[:, sl]).astype(BF16)
            do_ref[rows, sl] = (rs * (dgn - jnp.mean(dgn, axis=-1, keepdims=True)
                                      - gn * jnp.mean(dgn * gn, axis=-1, keepdims=True))).astype(BF16)
        dlru = dm[:, RET_W:]
        gate = gate_ref[rows, :]
        dhs_ref[rows, :] = dlru * _gelu(gate)
        dgate_ref[rows, :] = (dlru * (hf_ref[rows, :] + hb_ref[rows, :]) * _dgelu(gate)).astype(BF16)

    t = pl.BlockSpec((TM, RET_W), lambda i: (tile(i), 0))
    return _Epilogue(
        fn, (o_f, o_b, p, p, h_f, h_b),
        in_specs=[t, t, pl.BlockSpec((TM, RET_W), lambda i: (tile(i), COL_G)),
                  pl.BlockSpec((TM, LRU_W), lambda i: (tile(i), COL_GATE)), t, t],
        out_specs=(t, t, t, t), out_shape=(_sds((n, RET_W), BF16),) * 3 + (_sds((n, RET_W), F32),),
        steps=n // TM, lhs_map=lambda i: (jnp.maximum(i - cb, 0), 0), delay=1)


def _bwd_norm1_epilogue(ctx, x, dx1, g, modrows, cb):
    t_len, d = x.shape

    def fn(dh, rows, ctx_ref, x_ref, dx1_ref, g_ref, m_ref, gx_ref, acc_ref):
        is_ctx = pl.program_id(0) < cb
        _zero_at_start(acc_ref, rows)
        xin = jnp.where(is_ctx, ctx_ref[rows, :], x_ref[rows, :])
        sc = jnp.where(is_ctx, m_ref[R_CSC1:R_CSC1 + 1, :], m_ref[R_SC1:R_SC1 + 1, :])
        rstd = lax.rsqrt(jnp.mean(xin * xin, axis=-1, keepdims=True) + EPS)
        xh = xin * rstd
        gn = g_ref[...]
        dn = dh * (1.0 + sc)
        dxh = dn * gn
        gx_ref[rows, :] = dx1_ref[rows, :] + rstd * (dxh - xh * jnp.mean(dxh * xh, axis=-1, keepdims=True))
        s0 = jnp.sum(dh, axis=0, keepdims=True)
        s1 = jnp.sum(dh * xh * gn, axis=0, keepdims=True)
        acc_ref[4:5, :] += jnp.sum(dn * xh, axis=0, keepdims=True)
        acc_ref[0:1, :] += jnp.where(is_ctx, s0, 0.0)
        acc_ref[1:2, :] += jnp.where(is_ctx, s1, 0.0)
        acc_ref[2:3, :] += jnp.where(is_ctx, 0.0, s0)
        acc_ref[3:4, :] += jnp.where(is_ctx, 0.0, s1)

    lat = pl.BlockSpec((TM, d), lambda i: (jnp.maximum(i - cb, 0), 0))
    return _Epilogue(
        fn, (ctx, x, dx1, g, modrows),
        in_specs=[pl.BlockSpec((TM, d), lambda i: (jnp.minimum(i, cb - 1), 0)),
                  lat, lat, pl.BlockSpec((1, d), lambda i: (0, 0)), pl.BlockSpec((8, d), lambda i: (0, 0))],
        out_specs=(lat, pl.BlockSpec((8, d), lambda i: (0, 0))),
        out_shape=(_sds((t_len, d), F32), _sds((8, d), F32)))


def _rot(x, cf, ss):
    return x * cf + pltpu.roll(x, HEAD_DIM // 2, 1) * ss


def _decay_exponents(dirn):
    ii = lax.broadcasted_iota(jnp.int32, (CHUNK, CHUNK), 0)
    jj = lax.broadcasted_iota(jnp.int32, (CHUNK, CHUNK), 1)
    rel = ii - jj if dirn == 0 else jj - ii
    pos = ii.astype(F32)
    if dirn == 0:
        cq, cs = pos + 1.0, (CHUNK - 1.0) - pos
    else:
        cq, cs = CHUNK - pos, pos
    return rel, jnp.maximum(rel, 0).astype(F32), cq, cs


def _store_decay(lg_ref, dec, relf_ref=None):
    for dirn in (0, 1):
        rel, relf, cq, cs = _decay_exponents(dirn)
        if relf_ref is not None:
            relf_ref[dirn] = relf
        for h in range(HEADS):
            lgv = lg_ref[dirn, h]
            wq, ws = jnp.exp(lgv * cq), jnp.exp(lgv * cs)
            dec[dirn, h, 0] = jnp.where(rel >= 0, jnp.exp(lgv * relf), 0.0)
            dec[dirn, h, 1] = wq
            dec[dirn, h, 2] = ws
            if relf_ref is not None:
                dec[dirn, h, 3] = wq * cq
                dec[dirn, h, 4] = ws * cs


def _ret_rows(cc, nc, step_of):
    return [lambda s, dirn=dirn: _tile_order(dirn, step_of(s), cc, nc) for dirn in (0, 1)]


def _ret_in_specs(rows):
    specs = []
    for row in rows:
        specs += [pl.BlockSpec((CHUNK, RET_W), lambda s, o=o, row=row: (row(s), o)) for o in (0, 1, 2)]
    return specs


def _ret_fwd(qkv, lg, cc, ride=None):
    n = qkv.shape[0]
    nc = n // CHUNK
    rows = _ret_rows(cc, nc, lambda s: s)

    def body(lg_ref, q0, k0, v0, q1, k1, v1, o0, o1, sp0, sp1, st, dec):
        @pl.when(pl.program_id(0) == 0)
        def _():
            st[...] = jnp.zeros_like(st)
            _store_decay(lg_ref, dec)

        refs = ((q0, k0, v0, o0, sp0), (q1, k1, v1, o1, sp1))
        chains = [(dirn, h, slice(h * HEAD_DIM, (h + 1) * HEAD_DIM)) for dirn in (0, 1) for h in range(HEADS)]
        scores, cross, update = [], [], []
        for dirn, h, sl in chains:
            q_ref, k_ref, v_ref, _, sp_ref = refs[dirn]
            q, k, v = q_ref[:, sl], k_ref[:, sl], v_ref[:, sl]
            sp = st[dirn, h]
            sp_ref[h] = sp
            scores.append(_dot_nt(q, k))
            cross.append(_dot(q * dec[dirn, h, 1], sp))
            update.append(_dot_tn(k * dec[dirn, h, 2], v))
        masked = [(a * dec[dirn, h, 0]).astype(BF16) for a, (dirn, h, _) in zip(scores, chains)]
        intra = [_dot(sc, refs[dirn][2][:, sl]) for sc, (dirn, h, sl) in zip(masked, chains)]
        for (dirn, h, sl), o_in, o_cr, upd in zip(chains, intra, cross, update):
            refs[dirn][3][:, sl] = o_in + o_cr
            st[dirn, h] = jnp.exp(lg_ref[dirn, h] * CHUNK) * st[dirn, h] + upd

    o_specs = [pl.BlockSpec((CHUNK, RET_W), lambda s, row=row: (row(s), 0)) for row in rows]
    state = pl.BlockSpec((None, HEADS, CHUNK, HEAD_DIM), lambda s: (s, 0, 0, 0))
    return _pcall_ride(
        body, ride, (lg,) + (qkv,) * 6, name="ret_fwd", grid=(nc,),
        in_specs=[pl.BlockSpec(memory_space=pltpu.SMEM)] + _ret_in_specs(rows),
        out_specs=(o_specs[0], o_specs[1], state, state),
        out_shape=(_sds((n, RET_W), F32),) * 2 + (_sds((nc, HEADS, CHUNK, HEAD_DIM), F32),) * 2,
        scratch_shapes=[pltpu.VMEM((2, HEADS, CHUNK, HEAD_DIM), F32), pltpu.VMEM((2, HEADS, 3, CHUNK, CHUNK), F32)],
        compiler_params=_cp())


def _ret_bwd(qkv, lg, do, s_prev, cc, ride=None):
    n = qkv.shape[0]
    nc = n // CHUNK
    rows = _ret_rows(cc, nc, lambda s: nc - 1 - s)

    def body(lg_ref, q0, k0, v0, q1, k1, v1, do0, do1, sp0, sp1, dq0, dk0, dv0, dq1, dk1, dv1, dlg_ref, dst, dec, relf):
        @pl.when(pl.program_id(0) == 0)
        def _():
            dst[...] = jnp.zeros_like(dst)
            dlg_ref[...] = jnp.zeros_like(dlg_ref)
            _store_decay(lg_ref, dec, relf)

        refs = ((q0, k0, v0, do0, sp0, dq0, dk0, dv0), (q1, k1, v1, do1, sp1, dq1, dk1, dv1))
        chains = [(dirn, h, slice(h * HEAD_DIM, (h + 1) * HEAD_DIM)) for dirn in (0, 1) for h in range(HEADS)]

        def tiles(dirn, sl):
            q_ref, k_ref, v_ref, do_ref = refs[dirn][:4]
            return q_ref[:, sl], k_ref[:, sl], v_ref[:, sl], do_ref[:, sl]

        a_s, g1_s, da_s, h1_s = [], [], [], []
        for dirn, h, sl in chains:
            q, k, v, dov = tiles(dirn, sl)
            a_s.append(_dot_nt(q, k))
            g1_s.append(_dot_nt(dov, refs[dirn][4][h]))
            da_s.append(_dot_nt(dov, v))
            h1_s.append(_dot_nt(v, dst[dirn, h]))
        da_s = [da * dec[dirn, h, 0] for da, (dirn, h, _) in zip(da_s, chains)]
        dq_s, dk_s, dv_s, ds_s = [], [], [], []
        for (dirn, h, sl), a, da in zip(chains, a_s, da_s):
            q, k, v, dov = tiles(dirn, sl)
            dq_s.append(_dot(da, k))
            dk_s.append(_dot_tn(da, q))
            dv_s.append(_dot_tn(a * dec[dirn, h, 0], dov) + _dot(k * dec[dirn, h, 2], dst[dirn, h]))
            ds_s.append(_dot_tn(q * dec[dirn, h, 1], dov))
        for (dirn, h, sl), a, g1, da, h1, dq2, dk2, dv, ds2 in zip(chains, a_s, g1_s, da_s, h1_s, dq_s, dk_s, dv_s, ds_s):
            q, k, _, _ = tiles(dirn, sl)
            dq_ref, dk_ref, dv_ref = refs[dirn][5:]
            sp, dsn = refs[dirn][4][h], dst[dirn, h]
            gc = jnp.exp(lg_ref[dirn, h] * CHUNK)
            dq_ref[:, sl] = (g1 * dec[dirn, h, 1] + dq2).astype(BF16)
            dk_ref[:, sl] = (dk2 + h1 * dec[dirn, h, 2]).astype(BF16)
            dv_ref[:, sl] = dv.astype(BF16)
            term = da * a * relf[dirn] + q * g1 * dec[dirn, h, 3] + k * h1 * dec[dirn, h, 4] + sp * dsn * (CHUNK * gc)
            dlg_ref[dirn * HEADS + h:dirn * HEADS + h + 1, 0:HEAD_DIM] += jnp.sum(term, axis=0, keepdims=True)
            dst[dirn, h] = gc * dsn + ds2

    wide = [pl.BlockSpec((CHUNK, RET_W), lambda s, row=row: (row(s), 0)) for row in rows]
    state = pl.BlockSpec((None, HEADS, CHUNK, HEAD_DIM), lambda s: (nc - 1 - s, 0, 0, 0))
    return _pcall_ride(
        body, ride, (lg,) + (qkv,) * 6 + (do, do, s_prev[0], s_prev[1]), name="ret_bwd", grid=(nc,),
        in_specs=[pl.BlockSpec(memory_space=pltpu.SMEM)] + _ret_in_specs(rows) + wide + [state, state],
        out_specs=(wide[0],) * 3 + (wide[1],) * 3 + (pl.BlockSpec((2 * HEADS, 8 * HEAD_DIM), lambda s: (0, 0)),),
        out_shape=(_sds((n, RET_W), BF16),) * 6 + (_sds((2 * HEADS, 8 * HEAD_DIM), F32),),
        scratch_shapes=[pltpu.VMEM((2, HEADS, CHUNK, HEAD_DIM), F32), pltpu.VMEM((2, HEADS, 5, CHUNK, CHUNK), F32),
                        pltpu.VMEM((2, CHUNK, CHUNK), F32)],
        compiler_params=_cp())


def _shift_rows(cur, prev8, next8, k, seg_start, seg_end):
    tm = cur.shape[0]
    rows = _rows_iota(cur.shape)
    if k < 0:
        out = pltpu.roll(cur, -k, 0)
        for j in range(-k):
            halo = jnp.where(seg_start, 0.0, prev8[SUB + k + j:SUB + k + j + 1, :])
            out = jnp.where(rows == j, halo, out)
    else:
        out = pltpu.roll(cur, tm - k, 0)
        for j in range(k):
            halo = jnp.where(seg_end, 0.0, next8[j:j + 1, :])
            out = jnp.where(rows == tm - k + j, halo, out)
    return out


def _seg_flags(t, cb, nb):
    return jnp.logical_or(t == 0, t == cb), jnp.logical_or(t == cb - 1, t == nb - 1)


def _halo_specs(tile_of, n_rows, col):
    per = TM // SUB
    return [pl.BlockSpec((TM, LRU_W), lambda s: (tile_of(s), col)),
            pl.BlockSpec((SUB, LRU_W), lambda s: (jnp.maximum(tile_of(s) * per - 1, 0), col)),
            pl.BlockSpec((SUB, LRU_W), lambda s: (jnp.minimum((tile_of(s) + 1) * per, n_rows // SUB - 1), col))]


def _lru_gates(xr, prev8, next8, seg_start, seg_end, cw_ref, cb_ref, wg_ref, bg_ref, sp_ref):
    xm1 = _shift_rows(xr, prev8, next8, -1, seg_start, seg_end)
    xp1 = _shift_rows(xr, prev8, next8, 1, seg_start, seg_end)
    xp2 = _shift_rows(xr, prev8, next8, 2, seg_start, seg_end)
    xc = cb_ref[...] + xm1 * cw_ref[0:1, :] + xr * cw_ref[1:2, :] + xp1 * cw_ref[2:3, :] + xp2 * cw_ref[3:4, :]
    pre = _dot(xc, wg_ref[...]) + bg_ref[...]
    r = _sigmoid(pre[:, :LRU_W])
    i = _sigmoid(pre[:, LRU_W:])
    la = (-LRU_C) * r * sp_ref[...]
    a = jnp.exp(la)
    th = jnp.tanh(la)
    sq = jnp.sqrt(-2.0 * th / (1.0 - th))
    return xc, r, i, a, sq


def _scan_tile(a, b, ascending, a_sc, b_sc, carry, out_ref):
    tm, w = a.shape
    nsub = tm // SUB
    a = a.reshape(nsub, SUB, w)
    b = b.reshape(nsub, SUB, w)
    r8 = lax.broadcasted_iota(jnp.int32, a.shape, 1)
    for k in (1, 2, 4):
        if ascending:
            m = r8 >= k
            a_s, b_s = pltpu.roll(a, k, 1), pltpu.roll(b, k, 1)
        else:
            m = r8 < SUB - k
            a_s, b_s = pltpu.roll(a, SUB - k, 1), pltpu.roll(b, SUB - k, 1)
        b = a * jnp.where(m, b_s, 0.0) + b
        a = a * jnp.where(m, a_s, 1.0)
    a_sc[...] = a.reshape(tm, w)
    b_sc[...] = b.reshape(tm, w)

    def step(j, c):
        off = pl.multiple_of((j if ascending else nsub - 1 - j) * SUB, SUB)
        hb = a_sc[pl.ds(off, SUB), :] * c + b_sc[pl.ds(off, SUB), :]
        out_ref[pl.ds(off, SUB), :] = hb
        last = hb[SUB - 1:SUB, :] if ascending else hb[0:1, :]
        return jnp.broadcast_to(last, c.shape)

    carry[...] = lax.fori_loop(0, nsub, step, carry[...], unroll=8)


def _lru_fwd(p, wg, bg, sp, cw, cbias, dirn, cb, ride=None):
    n = p.shape[0]
    nb = n // TM
    tile_of = lambda s: _tile_order(dirn, s, cb, nb)

    def body(x_ref, xp_ref, xn_ref, wg_ref, bg_ref, sp_ref, cw_ref, cb_ref, h_ref, cin_ref, carry, a_sc, b_sc):
        s = pl.program_id(0)

        @pl.when(s == 0)
        def _():
            carry[...] = jnp.zeros_like(carry)

        seg_start, seg_end = _seg_flags(tile_of(s), cb, nb)
        xc, r, i, a, sq = _lru_gates(x_ref[...], xp_ref[...], xn_ref[...], seg_start, seg_end,
                                     cw_ref, cb_ref, wg_ref, bg_ref, sp_ref)
        cin_ref[...] = carry[...]
        _scan_tile(a, sq * (i * xc), dirn == 0, a_sc, b_sc, carry, h_ref)

    full = lambda shape: pl.BlockSpec(shape, lambda s: (0,) * len(shape))
    return _pcall_ride(
        body, ride, (p, p, p, wg, bg, sp, cw, cbias), name=f"lru_fwd{dirn}", grid=(nb,),
        in_specs=_halo_specs(tile_of, n, COL_XR) + [full((LRU_W, 2 * LRU_W)), full((1, 2 * LRU_W)), full((1, LRU_W)),
                                               full((4, LRU_W)), full((1, LRU_W))],
        out_specs=(pl.BlockSpec((TM, LRU_W), lambda s: (tile_of(s), 0)),
                   pl.BlockSpec((None, SUB, LRU_W), lambda s: (tile_of(s), 0, 0))),
        out_shape=(_sds((n, LRU_W), F32), _sds((nb, SUB, LRU_W), F32)),
        scratch_shapes=[pltpu.VMEM((SUB, LRU_W), F32), pltpu.VMEM((TM, LRU_W), F32), pltpu.VMEM((TM, LRU_W), F32)],
        compiler_params=_cp())


def _lru_bwd(p, wg, bg, sp, cw, cbias, h, cin, dhs, dirn, cb, ride=None):
    n = p.shape[0]
    nb = n // TM
    tile_of = lambda s: _tile_order(dirn, nb - 1 - s, cb, nb)

    def body(x_ref, xp_ref, xn_ref, wg_ref, bg_ref, sp_ref, cw_ref, cb_ref, h_ref, cin_ref, dhs_ref,
             dxc_ref, dwd_ref, acc_ref, carry, a_sc, b_sc, mu_sc, dwg_ref):
        s = pl.program_id(0)

        @pl.when(s == 0)
        def _():
            carry[...] = jnp.zeros_like(carry)
            dwg_ref[...] = jnp.zeros_like(dwg_ref)
            acc_ref[...] = jnp.zeros_like(acc_ref)

        seg_start, seg_end = _seg_flags(tile_of(s), cb, nb)
        xc, r, i, a, sq = _lru_gates(x_ref[...], xp_ref[...], xn_ref[...], seg_start, seg_end,
                                     cw_ref, cb_ref, wg_ref, bg_ref, sp_ref)
        rows = _rows_iota(a.shape)
        hv = h_ref[...]
        dh = dhs_ref[...]
        mu_next = carry[0:1, :]
        _scan_tile(a, a * dh, dirn == 1, a_sc, b_sc, carry, mu_sc)
        mu = mu_sc[...]
        if dirn == 0:
            hprev = jnp.where(rows == 0, cin_ref[0:1, :], pltpu.roll(hv, 1, 0))
            lam = dh + jnp.where(rows == TM - 1, mu_next, pltpu.roll(mu, TM - 1, 0))
        else:
            hprev = jnp.where(rows == TM - 1, cin_ref[0:1, :], pltpu.roll(hv, TM - 1, 0))
            lam = dh + jnp.where(rows == 0, mu_next, pltpu.roll(mu, 1, 0))
        ds = lam * (i * xc)
        di = lam * (sq * xc)
        dla = lam * hprev * a - ds * (a * a) / jnp.maximum(sq, 1e-20)
        dpr = dla * ((-LRU_C) * sp_ref[...]) * r * (1.0 - r)
        dpi = di * i * (1.0 - i)
        dpre = jnp.concatenate([dpr, dpi], axis=1)
        dxc_ref[...] = lam * (sq * i) + _dot_nt(dpre, wg_ref[...])
        dwg_ref[...] += _dot_tn(xc, dpre)
        acc_ref[0:1, :] += jnp.sum(dpre, axis=0, keepdims=True)
        acc_ref[1:2, 0:LRU_W] += jnp.sum(dla * ((-LRU_C) * r), axis=0, keepdims=True)

        @pl.when(s == nb - 1)
        def _():
            low = lax.broadcasted_iota(jnp.int32, (LRU_BD, 2 * LRU_BD), 1) < LRU_BD
            for half in (0, LRU_W):
                for m in range(LRU_BLOCKS // 2):
                    lanes = slice(half + 2 * LRU_BD * m, half + 2 * LRU_BD * (m + 1))
                    even = dwg_ref[2 * m * LRU_BD:(2 * m + 1) * LRU_BD, lanes]
                    odd = dwg_ref[(2 * m + 1) * LRU_BD:(2 * m + 2) * LRU_BD, lanes]
                    dwd_ref[:, lanes] = jnp.where(low, even, odd)

    full = lambda shape: pl.BlockSpec(shape, lambda s: (0,) * len(shape))
    tile = pl.BlockSpec((TM, LRU_W), lambda s: (tile_of(s), 0))
    return _pcall_ride(
        body, ride, (p, p, p, wg, bg, sp, cw, cbias, h, cin, dhs), name=f"lru_bwd{dirn}", grid=(nb,),
        in_specs=_halo_specs(tile_of, n, COL_XR) + [full((LRU_W, 2 * LRU_W)), full((1, 2 * LRU_W)), full((1, LRU_W)),
                                               full((4, LRU_W)), full((1, LRU_W)), tile,
                                               pl.BlockSpec((None, SUB, LRU_W), lambda s: (tile_of(s), 0, 0)), tile],
        out_specs=(tile, full((LRU_BD, 2 * LRU_W)), full((8, 2 * LRU_W))),
        out_shape=(_sds((n, LRU_W), F32), _sds((LRU_BD, 2 * LRU_W), F32), _sds((8, 2 * LRU_W), F32)),
        scratch_shapes=[pltpu.VMEM((SUB, LRU_W), F32)] + [pltpu.VMEM((TM, LRU_W), F32)] * 3
        + [pltpu.VMEM((LRU_W, 2 * LRU_W), F32)],
        compiler_params=_cp())


def _assemble_dp(dqs, dks, dvs, dg, dgate, dxcs, p, cw, cosf, sins, cb, ride=None):
    n = p.shape[0]
    nb = n // TM
    tile_of = lambda s: s

    def body(dqf, dqb, dkf, dkb, dvf, dvb, dg_ref, dgate_ref, cf, pf, nf, cb_, pb, nb_, x_ref, xp_ref, xn_ref,
             cw_ref, cos_ref, sin_ref, dp_ref, acc_ref):
        s = pl.program_id(0)

        @pl.when(s == 0)
        def _():
            acc_ref[...] = jnp.zeros_like(acc_ref)

        seg_start, seg_end = _seg_flags(s, cb, nb)
        dq = dqf[...].astype(F32) + dqb[...].astype(F32)
        dk = dkf[...].astype(F32) + dkb[...].astype(F32)
        cosv, sinv = cos_ref[...], sin_ref[...]
        for h in range(HEADS):
            sl = slice(h * HEAD_DIM, (h + 1) * HEAD_DIM)
            sk = slice(RET_W + h * HEAD_DIM, RET_W + (h + 1) * HEAD_DIM)
            dp_ref[:, sl] = (dq[:, sl] * cosv + pltpu.roll(dq[:, sl] * sinv, HEAD_DIM // 2, 1)).astype(BF16)
            dp_ref[:, sk] = ((dk[:, sl] * cosv + pltpu.roll(dk[:, sl] * sinv, HEAD_DIM // 2, 1)) * K_SCALE).astype(BF16)
        dp_ref[:, 2 * RET_W:3 * RET_W] = (dvf[...].astype(F32) + dvb[...].astype(F32)).astype(BF16)
        dp_ref[:, 3 * RET_W:4 * RET_W] = dg_ref[...].astype(BF16)
        dxc = cf[...] + cb_[...]
        dprev = pf[...] + pb[...]
        dnext = nf[...] + nb_[...]
        dxr = (_shift_rows(dxc, dprev, dnext, 1, seg_start, seg_end) * cw_ref[0:1, :] + dxc * cw_ref[1:2, :]
               + _shift_rows(dxc, dprev, dnext, -1, seg_start, seg_end) * cw_ref[2:3, :]
               + _shift_rows(dxc, dprev, dnext, -2, seg_start, seg_end) * cw_ref[3:4, :])
        dp_ref[:, 4 * RET_W:4 * RET_W + LRU_W] = dxr.astype(BF16)
        dp_ref[:, 4 * RET_W + LRU_W:] = dgate_ref[...].astype(BF16)
        xr, xp, xn = x_ref[...], xp_ref[...], xn_ref[...]
        for j, k in enumerate((-1, 0, 1, 2)):
            xs = xr if k == 0 else _shift_rows(xr, xp, xn, k, seg_start, seg_end)
            acc_ref[j:j + 1, 0:LRU_W] += jnp.sum(dxc * xs, axis=0, keepdims=True)
        acc_ref[4:5, 0:LRU_W] += jnp.sum(dxc, axis=0, keepdims=True)

    t = pl.BlockSpec((TM, RET_W), lambda s: (s, 0))
    args = (dqs[0], dqs[1], dks[0], dks[1], dvs[0], dvs[1], dg, dgate, dxcs[0], dxcs[0], dxcs[0], dxcs[1], dxcs[1], dxcs[1],
            p, p, p, cw, cosf, sins)
    tab = pl.BlockSpec((TM, HEAD_DIM), lambda s: (s, 0))
    return _pcall_ride(
        body, ride, args, name="assemble_dp", grid=(nb,),
        in_specs=[t] * 8 + _halo_specs(tile_of, n, 0) * 2 + _halo_specs(tile_of, n, COL_XR)
        + [pl.BlockSpec((4, LRU_W), lambda s: (0, 0)), tab, tab],
        out_specs=(pl.BlockSpec((TM, 4 * RET_W + 2 * LRU_W), lambda s: (s, 0)), pl.BlockSpec((8, 2 * LRU_W), lambda s: (0, 0))),
        out_shape=(_sds((n, 4 * RET_W + 2 * LRU_W), BF16), _sds((8, 2 * LRU_W), F32)), compiler_params=_cp())


def _adamw(g, w, m, v):
    nm = ADAM_B1 * m + (1.0 - ADAM_B1) * g
    nv = ADAM_B2 * v + (1.0 - ADAM_B2) * (g * g)
    m_hat = nm / (1.0 - ADAM_B1 ** ADAM_STEP)
    v_hat = nv / (1.0 - ADAM_B2 ** ADAM_STEP)
    return (-ADAM_LR) * (m_hat / (jnp.sqrt(v_hat) + ADAM_EPS) + ADAM_WD * w), nm, nv


def _adam_many(items, name):
    n = len(items)

    def body(*refs):
        for i in range(n):
            g, w, m, v = (r[...] for r in refs[4 * i:4 * i + 4])
            for o_ref, val in zip(refs[4 * n + 3 * i:4 * n + 3 * i + 3], _adamw(g, w, m, v)):
                o_ref[...] = val

    out_shape = tuple(_sds(it[1].shape, F32) for it in items for _ in range(3))
    res = _pcall(body, name=name, out_shape=out_shape, compiler_params=_cp())(*[a for it in items for a in it])
    return [tuple(res[3 * i:3 * i + 3]) for i in range(n)]


def _sum_adam(parts_list, w, m, v, name):
    nparts, _, c = parts_list[0].shape
    r = w.shape[0]
    tr = min(parts_list[0].shape[1], 128)
    starts, o = [], 0
    for pa in parts_list:
        starts.append(o)
        o += pa.shape[1] // tr
    nseg = len(parts_list)

    def body(*refs):
        p_refs = refs[:nseg]
        w_ref, m_ref, v_ref, g_ref, d_ref, nm_ref, nv_ref = refs[nseg:]
        i = pl.program_id(0)
        for s, p_ref in enumerate(p_refs):
            end = starts[s + 1] if s + 1 < nseg else r // tr

            @pl.when(jnp.logical_and(i >= starts[s], i < end))
            def _():
                g = p_ref[0].astype(F32)
                for j in range(1, nparts):
                    g = g + p_ref[j].astype(F32)
                g_ref[...] = g
                d_ref[...], nm_ref[...], nv_ref[...] = _adamw(g, w_ref[...], m_ref[...], v_ref[...])

    def seg_spec(s):
        last = parts_list[s].shape[1] // tr - 1
        return pl.BlockSpec((nparts, tr, c), lambda i: (0, jnp.clip(i - starts[s], 0, last), 0))

    t = pl.BlockSpec((tr, c), lambda i: (i, 0))
    return _pcall(
        body, name=name, grid=(r // tr,),
        in_specs=[seg_spec(s) for s in range(nseg)] + [t, t, t],
        out_specs=(t, t, t, t), out_shape=(_sds((r, c), F32),) * 4, compiler_params=_cp(),
    )(*parts_list, w, m, v)


def _sum_parts(parts, name):
    nparts, r, c = parts.shape

    def body(p_ref, o_ref):
        g = p_ref[0]
        for j in range(1, nparts):
            g = g + p_ref[j]
        o_ref[...] = g

    return _pcall(body, name=name, out_shape=_sds((r, c), parts.dtype), compiler_params=_cp())(parts)


def _rot_tables(l_len, t_len):
    rows = t_len // GRID_W
    n_freq = HEAD_DIM // 4
    inv = ROPE_BASE ** (-jnp.arange(n_freq, dtype=F32) / n_freq)
    ang_r = jnp.arange(rows, dtype=F32)[:, None] * inv
    ang_c = jnp.arange(GRID_W, dtype=F32)[:, None] * inv
    cos = jnp.concatenate([jnp.repeat(jnp.cos(ang_r), GRID_W, axis=0), jnp.tile(jnp.cos(ang_c), (rows, 1))], axis=-1)
    sin = jnp.concatenate([jnp.repeat(jnp.sin(ang_r), GRID_W, axis=0), jnp.tile(jnp.sin(ang_c), (rows, 1))], axis=-1)
    cosf = jnp.concatenate([jnp.ones((l_len, HEAD_DIM), F32), jnp.concatenate([cos, cos], axis=-1)], axis=0)
    sins = jnp.concatenate([jnp.zeros((l_len, HEAD_DIM), F32), jnp.concatenate([-sin, sin], axis=-1)], axis=0)
    return cosf, sins


def _block_diag(w):
    eye = jnp.eye(LRU_BLOCKS, dtype=w.dtype)
    return (w[:, :, None, :] * eye[:, None, :, None]).reshape(LRU_W, LRU_W)


def _blocks_from_lanes(dwd_half):
    return dwd_half.reshape(LRU_BD, LRU_BLOCKS, LRU_BD).transpose(1, 0, 2)


def _silu(x):
    return x * jax.nn.sigmoid(x)


def kernel(x, c, ctx, c_ctx, w_ada, b_ada, norm1_g, norm2_g, w_in, ret_decay, conv_w, conv_b, lru_wa, lru_ba, lru_wx, lru_bx, lru_lambda, w_out, w_mlp1, w_mlp2, final_g, loss_target, m_c_ctx, m_w_ada, m_b_ada, m_norm1_g, m_norm2_g, m_w_in, m_ret_decay, m_conv_w, m_conv_b, m_lru_wa, m_lru_ba, m_lru_wx, m_lru_bx, m_lru_lambda, m_w_out, m_w_mlp1, m_w_mlp2, m_final_g, v_c_ctx, v_w_ada, v_b_ada, v_norm1_g, v_norm2_g, v_w_in, v_ret_decay, v_conv_w, v_conv_b, v_lru_wa, v_lru_ba, v_lru_wx, v_lru_bx, v_lru_lambda, v_w_out, v_w_mlp1, v_w_mlp2, v_final_g):
    t_len, d = x.shape[1], x.shape[2]
    l_len = ctx.shape[1]
    cb, cc = l_len // TM, l_len // CHUNK
    me = 4 * lax.axis_index("x") + 2 * lax.axis_index("y") + lax.axis_index("c")
    x2d, ctx2d, tgt2d = x[0], ctx[0], loss_target[0]
    ada_cols = w_ada.shape[2]
    wa2d = w_ada[0]

    sc_loc = conv_w.shape[2]
    pack_a = jnp.zeros((8, d), F32)
    pack_a = pack_a.at[0].set(_silu(c[0]))
    pack_a = pack_a.at[1, :4 * sc_loc].set(conv_w[0].reshape(-1))
    pack_a = pack_a.at[2, :2 * sc_loc].set(lru_ba[0].reshape(-1))
    pack_a = pack_a.at[3, :2 * sc_loc].set(lru_bx[0].reshape(-1))
    pack_a = pack_a.at[4, :2 * sc_loc].set(lru_lambda[0].reshape(-1))
    all_a = _all_gather_small(pack_a, "gather_small_in")
    s16 = jnp.zeros((16, d), F32).at[0:8].set(all_a[:, 0, :]).at[8].set(_silu(c_ctx))

    def unshard(row, k):
        return all_a[:, row, :k * sc_loc].reshape(N_DEV, k, sc_loc).transpose(1, 0, 2).reshape(k, N_DEV * sc_loc)

    conv_w_full = unshard(1, 4)
    ba_full, bx_full, lam_full = unshard(2, 2), unshard(3, 2), unshard(4, 2)

    b_cols = lax.dynamic_slice(b_ada, (0, me * ada_cols), (1, ada_cols))
    mod_parts = _all_gather_small(_mod_part(s16, wa2d, b_cols), "gather_mod")
    mod_all = mod_parts.transpose(1, 0, 2).reshape(16, N_DEV * ada_cols)
    mod_me = lax.dynamic_slice(mod_all, (me, 0), (1, 6 * d)).reshape(6, d)
    mod_c = mod_all[8].reshape(6, d)
    modrows = jnp.concatenate([mod_c[0:2], mod_me], axis=0)

    lg = jax.nn.log_sigmoid(ret_decay[0])
    sp = jax.nn.softplus(-lam_full)
    wg = [jnp.concatenate([_block_diag(lru_wa[0, dd]), _block_diag(lru_wx[0, dd])], axis=1).astype(BF16) for dd in (0, 1)]
    bg = [jnp.concatenate([ba_full[dd], bx_full[dd]])[None, :] for dd in (0, 1)]
    cosf, sins = _rot_tables(l_len, t_len)

    (hn, hnt), win_g = _norm1_fwd(ctx2d, x2d, norm1_g, modrows, cb, ride=("gather", w_in[0].astype(BF16)))
    (qkv, p), w2_g = _mm_in(hn, win_g, cosf, sins, ride=("gather", w_mlp2[0].astype(BF16)))
    w2_g = w2_g.reshape(1, 4 * d, d)
    (o0, o1, sp0, sp1), w1_g = _ret_fwd(qkv, lg, cc, ride=("gather", w_mlp1[0].astype(BF16)))
    o, s_prev = [o0, o1], [sp0, sp1]
    (h0, cin0), wout_g = _lru_fwd(p, wg[0], bg[0], sp[0:1], conv_w_full, conv_b, 0, cb, ride=("gather", w_out[0].astype(BF16)))
    wout_g = wout_g.reshape(1, d, d)
    (h1, cin1), _ = _lru_fwd(p, wg[1], bg[1], sp[1:2], conv_w_full, conv_b, 1, cb)
    h, cin = [h0, h1], [cin0, cin1]
    mix = _mix_fwd(o[0], o[1], p, h[0], h[1], cb, t_len)
    y, x1, h2, h2t = _mm_nn(mix, wout_g, F32, "mm_out_norm2", tm=TL, epi=_res_norm2_epilogue(x2d, norm2_g, modrows, TL))
    r = _mm_nn(h2, w1_g, BF16, "mm_mlp1", relu_out=True, tm=TL)
    dx2, dz, dzt, facc = _mm_nn(r, w2_g, F32, "mm_mlp2_final", square_lhs=True, tm=TL, vmem_mb=58,
                                epi=_final_epilogue(x1, tgt2d, final_g[None, :], modrows, TL))

    du = _mm_nt(dz, w2_g, BF16, "mm_da2", relu_mul=r, tm=TL, vmem_mb=58)
    gw2_lo, gw2_hi = _mm_wgrad(dzt, r, "mm_dw2", w2_g.shape[1] // N_DEV, BF16, transpose_out=True, square_rhs=True,
                               halves=True)
    gw1_lo, gw1_hi = _mm_wgrad(h2t, du, "mm_dw1", w1_g.shape[2], BF16, halves=True)
    (dx1, dy, n2acc), gw2_lo_all = _mm_nt(du, w1_g, F32, "mm_dh2_norm2", ride=("a2a", gw2_lo), tm=TL, vmem_mb=60,
                                          epi=_bwd_norm2_epilogue(x1, dx2, y, norm2_g, modrows, TL))
    gwo = _mm_tn(mix, dy, "mm_dwout", False, d, BF16, 512).reshape(N_DEV, -1, d)
    (do, dg, dgate, dhs), gwo_all = _mm_nt(dy, wout_g, F32, "mm_dmix_mix", ride=("a2a", gwo),
                                           epi=_mix_bwd_epilogue(o[0], o[1], p, h[0], h[1], cb))
    dxcs, dwgs, laccs, rides, got = [], [], [], [("a2a", gw1_hi), ("a2a", gw2_hi)], []
    (dq0, dk0, dv0, dq1, dk1, dv1, dlg_lanes), gw1_lo_all = _ret_bwd(qkv, lg, do, s_prev, cc, ride=("a2a", gw1_lo))
    dqs, dks, dvs = [dq0, dq1], [dk0, dk1], [dv0, dv1]
    for dd in (0, 1):
        (dxc_, dwg_, lacc_), got_ = _lru_bwd(p, wg[dd], bg[dd], sp[dd:dd + 1], conv_w_full, conv_b, h[dd], cin[dd], dhs, dd, cb,
                                             ride=rides[dd])
        dxcs.append(dxc_); dwgs.append(dwg_); laccs.append(lacc_); got.append(got_)
    gw1_hi_all, gw2_hi_all = got
    gw1_all, gw2_all = [gw1_lo_all, gw1_hi_all], [gw2_lo_all, gw2_hi_all]
    (dp, cacc), _ = _assemble_dp(dqs, dks, dvs, dg, dgate, dxcs, p, conv_w_full, cosf, sins, cb)
    pack_b = jnp.concatenate([n2acc, facc, cacc, laccs[0], laccs[1], dlg_lanes, dwgs[0], dwgs[1]], axis=0)
    gwi, all_b = _mm_wgrad(hnt, dp, "mm_dwin", win_g.shape[2], BF16, ride=("gather", pack_b))
    (grad_x, n1acc), gwi_all = _mm_nt(dp, win_g, F32, "mm_dhn_norm1", ride=("a2a", gwi),
                                      epi=_bwd_norm1_epilogue(ctx2d, x2d, dx1, norm1_g, modrows, cb))
    all_n1 = _all_gather_small(n1acc, "gather_norm1_grads")
    tot = _sum_parts(all_b, "sum_small_grads")
    t_n1 = _sum_parts(all_n1, "sum_norm1_grads")
    t_n2, t_f, t_conv, t_dlg = tot[0:8], tot[8:16], tot[16:24, :LRU_W], tot[40:48, :HEAD_DIM]
    t_l = [tot[24:32], tot[32:40]]
    t_dwd = [tot[48:48 + LRU_BD], tot[48 + LRU_BD:48 + 2 * LRU_BD]]
    loss = (0.5 / d) * jnp.sum(t_f[2])
    t_wa = jnp.stack([_blocks_from_lanes(t_dwd[dd][:, :LRU_W]) for dd in (0, 1)])
    t_wx = jnp.stack([_blocks_from_lanes(t_dwd[dd][:, LRU_W:]) for dd in (0, 1)])
    t_ba = jnp.stack([t_l[dd][0, :LRU_W] for dd in (0, 1)])
    t_bx = jnp.stack([t_l[dd][0, LRU_W:] for dd in (0, 1)])
    t_sp = jnp.stack([t_l[dd][1, :LRU_W] for dd in (0, 1)])
    dm_rows = jnp.stack([all_n1[:, 2, :], all_n1[:, 3, :], all_b[:, 3, :], all_b[:, 0, :], all_b[:, 1, :], all_b[:, 9, :]],
                        axis=1).reshape(N_DEV, 6 * d)
    dm_c = jnp.concatenate([t_n1[0], t_n1[1], jnp.zeros((4 * d,), F32)])
    dm16 = jnp.zeros((16, 6 * d), F32).at[0:8].set(dm_rows).at[8].set(dm_c)
    g_b_ada = jnp.sum(dm16, axis=0)[None, :]
    dm_cols = lax.dynamic_slice(dm16, (0, me * ada_cols), (16, ada_cols))
    g_w_ada, ds16 = _ada_bwd(s16, dm_cols, wa2d)
    ds_all = _all_gather_small(ds16[8:16], "gather_dsilu")
    dsilu_cc = _sum_parts(ds_all, "sum_dsilu")[0]
    sg_cc = jax.nn.sigmoid(c_ctx)
    g_c_ctx = dsilu_cc * (sg_cc * (1.0 + c_ctx * (1.0 - sg_cc)))

    g_ret_decay = jnp.sum(t_dlg, axis=-1).reshape(2, HEADS) * jax.nn.sigmoid(-ret_decay[0])
    g_lambda_full = -t_sp * jax.nn.sigmoid(-lam_full)

    def my_cols(full):
        return lax.dynamic_slice(full, (0, me * sc_loc), (full.shape[0], sc_loc))

    small_g = dict(
        c_ctx=g_c_ctx[None], b_ada=g_b_ada, norm1_g=t_n1[4:5], norm2_g=t_n2[2:3], ret_decay=g_ret_decay,
        conv_w=my_cols(t_conv[0:4]), conv_b=t_conv[4:5], lru_wa=t_wa.reshape(-1, LRU_BD), lru_ba=my_cols(t_ba),
        lru_wx=t_wx.reshape(-1, LRU_BD), lru_bx=my_cols(t_bx), lru_lambda=my_cols(g_lambda_full), final_g=t_f[0:1])
    small = dict(
        c_ctx=(c_ctx, m_c_ctx, v_c_ctx), b_ada=(b_ada, m_b_ada, v_b_ada), norm1_g=(norm1_g, m_norm1_g, v_norm1_g),
        norm2_g=(norm2_g, m_norm2_g, v_norm2_g), ret_decay=(ret_decay, m_ret_decay, v_ret_decay),
        conv_w=(conv_w, m_conv_w, v_conv_w), conv_b=(conv_b, m_conv_b, v_conv_b), lru_wa=(lru_wa, m_lru_wa, v_lru_wa),
        lru_ba=(lru_ba, m_lru_ba, v_lru_ba), lru_wx=(lru_wx, m_lru_wx, v_lru_wx), lru_bx=(lru_bx, m_lru_bx, v_lru_bx),
        lru_lambda=(lru_lambda, m_lru_lambda, v_lru_lambda), final_g=(final_g, m_final_g, v_final_g))
    names = list(small)
    items = [(small_g[k],) + tuple(a.reshape(small_g[k].shape) for a in small[k]) for k in names]
    res = {}
    for k, it, (d_, m_, v_) in zip(names, items, _adam_many(items, "adam_small")):
        shape = small[k][0].shape
        res[k] = tuple(a.reshape(shape) for a in (it[0], d_, m_, v_))

    def big(parts, w, m, v, name):
        out = _sum_adam(parts, w[0], m[0], v[0], name)
        return tuple(a[None] for a in out)

    res["w_ada"] = big([g_w_ada[None]], w_ada, m_w_ada, v_w_ada, "adam_w_ada")
    res["w_in"] = big([gwi_all], w_in, m_w_in, v_w_in, "adam_w_in")
    res["w_out"] = big([gwo_all], w_out, m_w_out, v_w_out, "adam_w_out")
    res["w_mlp1"] = big(gw1_all, w_mlp1, m_w_mlp1, v_w_mlp1, "adam_w_mlp1")
    res["w_mlp2"] = big(gw2_all, w_mlp2, m_w_mlp2, v_w_mlp2, "adam_w_mlp2")

    order = ["c_ctx", "w_ada", "b_ada", "norm1_g", "norm2_g", "w_in", "ret_decay", "conv_w", "conv_b", "lru_wa", "lru_ba",
             "lru_wx", "lru_bx", "lru_lambda", "w_out", "w_mlp1", "w_mlp2", "final_g"]
    outs = [loss, grad_x[None]]
    for j in range(4):
        outs += [res[k][j] for k in order]
    return tuple(outs)
```

```python
import jax
import jax.numpy as jnp
from jax import lax
from jax.experimental import pallas as pl
from jax.experimental.pallas import tpu as pltpu

F32 = jnp.float32
BF16 = jnp.bfloat16
AXES = ("x", "y", "c")
N_DEV = 8
MESH = pl.DeviceIdType.MESH

HEADS = 4
HEAD_DIM = 128
CHUNK = 128
RET_W = HEADS * HEAD_DIM
LRU_W = 512
LRU_BLOCKS = 8
LRU_BD = LRU_W // LRU_BLOCKS
LRU_C = 8.0
EPS = 1e-6
K_SCALE = HEAD_DIM ** -0.5
ROPE_BASE = 10000.0
GRID_W = 64
TM = 256
TL = 512
SUB = 8

ADAM_LR = 0.001
ADAM_B1 = 0.9
ADAM_B2 = 0.999
ADAM_EPS = 1e-08
ADAM_WD = 0.01
ADAM_STEP = 10

COL_G, COL_XR, COL_GATE = 0, 1, 2

R_CSH1, R_CSC1, R_SH1, R_SC1, R_G1, R_SH2, R_SC2, R_G2 = range(8)


def _pcall(body, **kw):
    return pl.pallas_call(body, **kw)


def _cp(vmem_mb=48):
    return pltpu.CompilerParams(vmem_limit_bytes=vmem_mb << 20)


def _sds(shape, dtype):
    return jax.ShapeDtypeStruct(shape, dtype)


def _dot(a, b):
    return jnp.dot(a.astype(BF16), b.astype(BF16), preferred_element_type=F32)


def _dot_nt(a, b):
    return lax.dot_general(a.astype(BF16), b.astype(BF16), (((1,), (1,)), ((), ())), preferred_element_type=F32)


def _dot_tn(a, b):
    return lax.dot_general(a.astype(BF16), b.astype(BF16), (((0,), (0,)), ((), ())), preferred_element_type=F32)


def _sigmoid(x):
    return 0.5 * jnp.tanh(0.5 * x) + 0.5


def _gelu(x):
    return 0.5 * x * (1.0 + jnp.tanh(0.7978845608028654 * (x + 0.044715 * x * x * x)))


def _dgelu(x):
    t = jnp.tanh(0.7978845608028654 * (x + 0.044715 * x * x * x))
    return 0.5 * (1.0 + t) + 0.5 * x * (1.0 - t * t) * 0.7978845608028654 * (1.0 + 3.0 * 0.044715 * x * x)


def _rows_iota(shape):
    return lax.broadcasted_iota(jnp.int32, shape, 0)


def _tile_order(dirn, s, cb, nb):
    if dirn == 0:
        return s
    return jnp.where(s < cb, cb - 1 - s, nb - 1 - (s - cb))


_SEMS = [pltpu.SemaphoreType.DMA((7,)), pltpu.SemaphoreType.DMA((7,)), pltpu.SemaphoreType.DMA(())]
_ANY = pl.BlockSpec(memory_space=pl.ANY)


def _gather_copies(x_ref, out_ref, send_sems, recv_sems, local_sem):
    mx, my, mc = lax.axis_index("x"), lax.axis_index("y"), lax.axis_index("c")
    me, sibling = (mx, my, mc), (mx, my, 1 - mc)
    chips = [(1 - mx, my), (mx, 1 - my), (1 - mx, 1 - my)]

    def slot(px, py, pc):
        return out_ref.at[4 * px + 2 * py + pc]

    def copy(k, block, to, src=None):
        return pltpu.make_async_remote_copy(
            src_ref=slot(*block) if src is None else src, dst_ref=slot(*block),
            send_sem=send_sems.at[k], recv_sem=recv_sems.at[k], device_id=to, device_id_type=MESH)

    mine = pltpu.make_async_copy(x_ref, slot(*me), local_sem)
    first = [copy(0, me, sibling, src=x_ref)] + [copy(1 + j, me, (*chip, mc), src=x_ref) for j, chip in enumerate(chips)]
    passed = [copy(4 + j, (*chip, mc), sibling) for j, chip in enumerate(chips)]
    recv_ici = [copy(1 + j, (*chip, mc), me) for j, chip in enumerate(chips)]
    recv_d2d = [copy(0, sibling, me)] + [copy(4 + j, (*chip, 1 - mc), me) for j, chip in enumerate(chips)]
    return mine, first, passed, recv_ici, recv_d2d


def _gather_start(*refs):
    mine, first, _, _, _ = _gather_copies(*refs)
    mine.start()
    for cp in first:
        cp.start()


def _gather_finish(*refs):
    mine, first, passed, recv_ici, recv_d2d = _gather_copies(*refs)
    for landed, onward in zip(recv_ici, passed):
        landed.wait_recv()
        onward.start()
    for landed in recv_d2d:
        landed.wait_recv()
    for cp in first + passed:
        cp.wait_send()
    mine.wait()


def _a2a_copies(g_ref, out_ref, send_sems, recv_sems, local_sem):
    mx, my, mc = lax.axis_index("x"), lax.axis_index("y"), lax.axis_index("c")
    me = 4 * mx + 2 * my + mc
    mine = pltpu.make_async_copy(g_ref.at[me], out_ref.at[me], local_sem)
    copies = []
    for k in range(1, N_DEV):
        px = 1 - mx if (k >> 2) & 1 else mx
        py = 1 - my if (k >> 1) & 1 else my
        pc = 1 - mc if k & 1 else mc
        copies.append(pltpu.make_async_remote_copy(
            src_ref=g_ref.at[4 * px + 2 * py + pc], dst_ref=out_ref.at[me],
            send_sem=send_sems.at[k - 1], recv_sem=recv_sems.at[k - 1],
            device_id=(px, py, pc), device_id_type=MESH))
    return mine, copies


def _a2a_start(*refs):
    mine, copies = _a2a_copies(*refs)
    mine.start()
    for cp in copies:
        cp.start()


def _a2a_finish(*refs):
    mine, copies = _a2a_copies(*refs)
    for cp in copies:
        cp.wait()
    mine.wait()


_EXCHANGES = {"gather": (_gather_start, _gather_finish), "a2a": (_a2a_start, _a2a_finish)}


def _exchange_shape(kind, src):
    return _sds((N_DEV,) + src.shape if kind == "gather" else src.shape, src.dtype)


def _all_gather_small(x, name):
    def body(x_ref, out_ref, send_sems, recv_sems, local_sem):
        mx, my, mc = lax.axis_index("x"), lax.axis_index("y"), lax.axis_index("c")
        me = 4 * mx + 2 * my + mc
        mine = pltpu.make_async_copy(x_ref, out_ref.at[me], local_sem)
        mine.start()
        copies = []
        for k in range(1, N_DEV):
            peer = (1 - mx if (k >> 2) & 1 else mx, 1 - my if (k >> 1) & 1 else my, 1 - mc if k & 1 else mc)
            copies.append(pltpu.make_async_remote_copy(
                src_ref=x_ref, dst_ref=out_ref.at[me], send_sem=send_sems.at[k - 1], recv_sem=recv_sems.at[k - 1],
                device_id=peer, device_id_type=MESH))
            copies[-1].start()
        for cp in copies:
            cp.wait()
        mine.wait()

    return _pcall(body, name=name, out_shape=_exchange_shape("gather", x), in_specs=[_ANY], out_specs=_ANY,
                  scratch_shapes=list(_SEMS))(x)


def _pcall_ride(body, ride, args, *, name, grid, in_specs, out_specs, out_shape, scratch_shapes=(), compiler_params=None):
    if ride is None:
        out = _pcall(body, name=name, grid=grid, in_specs=in_specs, out_specs=out_specs, out_shape=out_shape,
                     scratch_shapes=list(scratch_shapes), compiler_params=compiler_params)(*args)
        return out, None
    kind, src = ride
    start, finish = _EXCHANGES[kind]
    single = not isinstance(out_shape, (tuple, list))
    out_specs_t = (out_specs,) if single else tuple(out_specs)
    out_shape_t = (out_shape,) if single else tuple(out_shape)
    n_in, n_out, n_sc = len(in_specs), len(out_shape_t), len(scratch_shapes)

    def wrapped(*refs):
        ins, src_ref = refs[:n_in], refs[n_in]
        outs, dst_ref = refs[n_in + 1:n_in + 1 + n_out], refs[n_in + 1 + n_out]
        scratch = refs[n_in + 2 + n_out:n_in + 2 + n_out + n_sc]
        sems = refs[n_in + 2 + n_out + n_sc:]
        first = pl.program_id(0) == 0
        last = pl.program_id(0) == grid[0] - 1
        for ax in range(1, len(grid)):
            first = jnp.logical_and(first, pl.program_id(ax) == 0)
            last = jnp.logical_and(last, pl.program_id(ax) == grid[ax] - 1)

        @pl.when(first)
        def _():
            start(src_ref, dst_ref, *sems)

        body(*ins, *outs, *scratch)

        @pl.when(last)
        def _():
            finish(src_ref, dst_ref, *sems)

    res = _pcall(wrapped, name=name, grid=grid, in_specs=list(in_specs) + [_ANY], out_specs=out_specs_t + (_ANY,),
                 out_shape=out_shape_t + (_exchange_shape(kind, src),),
                 scratch_shapes=list(scratch_shapes) + list(_SEMS), compiler_params=compiler_params)(*args, src)
    return (res[0] if single else tuple(res[:-1])), res[-1]


class _Epilogue:
    def __init__(self, fn, args, in_specs, out_specs, out_shape, steps=None, lhs_map=None, delay=0):
        self.fn, self.args, self.in_specs, self.out_specs, self.out_shape = fn, tuple(args), list(in_specs), out_specs, out_shape
        self.steps, self.lhs_map, self.delay = steps, lhs_map, delay


def _mm_nn(a, w, out_dtype, name, square_lhs=False, relu_out=False, ride=None, tm=TM, epi=None, vmem_mb=48):
    m, k = a.shape
    nb, _, bn = w.shape
    tm = min(tm, m)
    delay = 0 if epi is None else epi.delay
    steps = m // tm

    def body(*refs):
        a_ref, w_ref = refs[:2]
        av = a_ref[...]
        if square_lhs:
            av = av * av
        if delay:
            assert nb == 1
            held = refs[-1]

            @pl.when(pl.program_id(0) == 0)
            def _():
                held[...] = jnp.zeros_like(held)

            fresh = jnp.dot(av, w_ref[0], preferred_element_type=F32)
            epi.fn(held[...], slice(0, tm), *refs[2:-1])
            held[...] = fresh
            return
        if epi is not None:
            assert nb == 1
            epi.fn(jnp.dot(av, w_ref[0], preferred_element_type=F32), slice(0, tm), *refs[2:])
            return
        for j in range(nb):
            r = jnp.dot(av, w_ref[j], preferred_element_type=F32)
            if relu_out:
                r = jnp.maximum(r, 0.0)
            refs[2][:, j * bn:(j + 1) * bn] = r.astype(out_dtype)

    lhs_map = (lambda i: (i, 0)) if not delay else (lambda i: (jnp.minimum(i, steps - 1), 0))
    in_specs = [pl.BlockSpec((tm, k), lhs_map), pl.BlockSpec((nb, k, bn), lambda i: (0, 0, 0))]
    scratch = []
    if epi is None:
        args, out_specs, out_shape = (a, w), pl.BlockSpec((tm, nb * bn), lambda i: (i, 0)), _sds((m, nb * bn), out_dtype)
    else:
        args, out_specs, out_shape = (a, w) + epi.args, epi.out_specs, epi.out_shape
        in_specs += epi.in_specs
        if delay:
            scratch.append(pltpu.VMEM((tm, bn), F32))
    out, ex = _pcall_ride(body, ride, args, name=name, grid=(steps + delay,), in_specs=in_specs, out_specs=out_specs,
                          out_shape=out_shape, scratch_shapes=scratch, compiler_params=_cp(vmem_mb))
    return out if ride is None else (out, ex)


def _mm_nt(dy, w, out_dtype, name, relu_mul=None, ride=None, tm=TM, epi=None, vmem_mb=48):
    m = dy.shape[0]
    nb, k, bn = w.shape
    tm = min(tm, m)
    n_extra = (0 if relu_mul is None else 1) + (0 if epi is None else len(epi.args))

    delay = 0 if epi is None else epi.delay

    def body(*refs):
        dy_ref, w_ref = refs[:2]
        scratch = refs[-1 - delay:]
        extra, outs, wt = refs[2:2 + n_extra], refs[2 + n_extra:-1 - delay], scratch[0]

        @pl.when(pl.program_id(0) == 0)
        def _():
            for j in range(nb):
                wt[j * bn:(j + 1) * bn, :] = w_ref[j].T
            if delay:
                scratch[1][...] = jnp.zeros_like(scratch[1])

        if delay:
            held = scratch[1]
            fresh = jnp.dot(dy_ref[...], wt[...], preferred_element_type=F32)
            epi.fn(held[...], slice(0, tm), *extra, *outs)
            held[...] = fresh
            return
        acc = jnp.dot(dy_ref[...], wt[...], preferred_element_type=F32)
        if epi is not None:
            epi.fn(acc, slice(0, tm), *extra, *outs)
            return
        if relu_mul is not None:
            acc = acc * (2.0 * extra[0][...].astype(F32))
        outs[0][...] = acc.astype(out_dtype)

    steps = m // tm if epi is None or epi.steps is None else epi.steps
    lhs_tile = (lambda i: (i, 0)) if epi is None or epi.lhs_map is None else epi.lhs_map
    lhs_map = lhs_tile if not delay else (lambda i: lhs_tile(jnp.minimum(i, steps - 1)))
    in_specs = [pl.BlockSpec((tm, nb * bn), lhs_map), pl.BlockSpec((nb, k, bn), lambda i: (0, 0, 0))]
    args = [dy, w]
    if relu_mul is not None:
        in_specs.append(pl.BlockSpec((tm, k), lambda i: (i, 0)))
        args.append(relu_mul)
    scratch = [pltpu.VMEM((nb * bn, k), BF16)]
    if epi is None:
        out_specs, out_shape = pl.BlockSpec((tm, k), lambda i: (i, 0)), _sds((m, k), out_dtype)
    else:
        args += list(epi.args)
        in_specs += epi.in_specs
        out_specs, out_shape = epi.out_specs, epi.out_shape
        if delay:
            scratch.append(pltpu.VMEM((tm, k), F32))
    out, ex = _pcall_ride(
        body, ride, args, name=name, grid=(steps + delay,), in_specs=in_specs, out_specs=out_specs, out_shape=out_shape,
        scratch_shapes=scratch, compiler_params=_cp(vmem_mb))
    return out if ride is None else (out, ex)


def _mm_tn(a, b, name, col_blocks, block, out_dtype, tm, square_lhs=False):
    m, k = a.shape
    nn = b.shape[1]
    steps = m // tm
    if col_blocks:
        nblk, acc_shape = nn // block, (k, block)
        a_spec = pl.BlockSpec((tm, k), lambda j, s: (s, 0))
        b_spec = pl.BlockSpec((tm, block), lambda j, s: (s, j))
    else:
        nblk, acc_shape = k // block, (block, nn)
        a_spec = pl.BlockSpec((tm, block), lambda j, s: (s, j))
        b_spec = pl.BlockSpec((tm, nn), lambda j, s: (s, 0))

    def body(a_ref, b_ref, o_ref, acc):
        s = pl.program_id(1)

        @pl.when(s == 0)
        def _():
            acc[...] = jnp.zeros_like(acc)

        av = a_ref[...]
        if square_lhs:
            av = av.astype(F32)
            av = (av * av).astype(BF16)
        acc[...] += _dot_tn(av, b_ref[...])

        @pl.when(s == steps - 1)
        def _():
            o_ref[...] = acc[...].astype(out_dtype)

    return _pcall(
        body, name=name, grid=(nblk, steps), in_specs=[a_spec, b_spec],
        out_specs=pl.BlockSpec((None,) + acc_shape, lambda j, s: (j, 0, 0)),
        out_shape=_sds((nblk,) + acc_shape, out_dtype),
        scratch_shapes=[pltpu.VMEM(acc_shape, F32)], compiler_params=_cp(),
    )(a, b)


def _mm_wgrad(at, b, name, bn, out_dtype, transpose_out=False, square_rhs=False, halves=False, ride=None):
    k, m = at.shape
    nblk = b.shape[1] // bn
    rows, cols = (bn, k) if transpose_out else (k, bn)
    nout = 2 if halves else 1
    per = rows // nout

    def body(a_ref, b_ref, *o_refs):
        bv = b_ref[...]
        if square_rhs:
            bv = bv * bv
        r = jnp.dot(a_ref[...], bv, preferred_element_type=F32)
        r = (r.T if transpose_out else r).astype(out_dtype)
        for i, o_ref in enumerate(o_refs):
            o_ref[...] = r[i * per:(i + 1) * per, :]

    out, ex = _pcall_ride(
        body, ride, (at, b), name=name, grid=(nblk,),
        in_specs=[pl.BlockSpec((k, m), lambda j: (0, 0)), pl.BlockSpec((m, bn), lambda j: (0, j))],
        out_specs=tuple(pl.BlockSpec((None, per, cols), lambda j: (j, 0, 0)) for _ in range(nout)),
        out_shape=tuple(_sds((nblk, per, cols), out_dtype) for _ in range(nout)), compiler_params=_cp())
    out = out if halves else out[0]
    return out if ride is None else (out, ex)


def _mm_in(hn, w, cosf, sins, ride):
    m, k = hn.shape
    nb, _, bn = w.shape

    def body(a_ref, w_ref, c_ref, s_ref, qkv_ref, rest_ref, pt):
        av = a_ref[...]
        for j in range(nb):
            pt[:, j * bn:(j + 1) * bn] = jnp.dot(av, w_ref[j], preferred_element_type=F32)
        cf, ss = c_ref[...], s_ref[...]
        for h in range(HEADS):
            sq = slice(h * HEAD_DIM, (h + 1) * HEAD_DIM)
            sk = slice(RET_W + h * HEAD_DIM, RET_W + (h + 1) * HEAD_DIM)
            qkv_ref[:, sq] = _rot(pt[:, sq], cf, ss).astype(BF16)
            qkv_ref[:, sk] = (_rot(pt[:, sk], cf, ss) * K_SCALE).astype(BF16)
        qkv_ref[:, 2 * RET_W:] = pt[:, 2 * RET_W:3 * RET_W].astype(BF16)
        rest_ref[...] = pt[:, 3 * RET_W:]

    tab = pl.BlockSpec((TM, HEAD_DIM), lambda i: (i, 0))
    wide = pl.BlockSpec((TM, 3 * RET_W), lambda i: (i, 0))
    return _pcall_ride(
        body, ride, (hn, w, cosf, sins), name="mm_in", grid=(m // TM,),
        in_specs=[pl.BlockSpec((TM, k), lambda i: (i, 0)), pl.BlockSpec((nb, k, bn), lambda i: (0, 0, 0)), tab, tab],
        out_specs=(wide, wide), out_shape=(_sds((m, 3 * RET_W), BF16), _sds((m, nb * bn - 3 * RET_W), F32)),
        scratch_shapes=[pltpu.VMEM((TM, nb * bn), F32)], compiler_params=_cp())


def _mod_part(s16, w_ada, b_cols):
    def body(s_ref, w_ref, b_ref, o_ref):
        o_ref[...] = _dot(s_ref[...], w_ref[...]) + b_ref[...]

    return _pcall(body, name="mod_part", out_shape=_sds((s16.shape[0], w_ada.shape[1]), F32),
                  compiler_params=_cp())(s16, w_ada, b_cols)


def _ada_bwd(s16, dm_cols, w_ada):
    def body(s_ref, d_ref, w_ref, gw_ref, ds_ref):
        gw_ref[...] = _dot_tn(s_ref[...], d_ref[...])
        ds_ref[...] = _dot_nt(d_ref[...], w_ref[...])

    return _pcall(body, name="ada_bwd",
                  out_shape=(_sds(w_ada.shape, F32), _sds(s16.shape, F32)), compiler_params=_cp())(s16, dm_cols, w_ada)


def _norm1_fwd(ctx, x, g, modrows, cb, ride=None):
    l_len, d = ctx.shape
    nb = (l_len + x.shape[0]) // TM

    def body(ctx_ref, x_ref, g_ref, m_ref, o_ref, ot_ref):
        is_ctx = pl.program_id(0) < cb
        xin = jnp.where(is_ctx, ctx_ref[...], x_ref[...])
        sh = jnp.where(is_ctx, m_ref[R_CSH1:R_CSH1 + 1, :], m_ref[R_SH1:R_SH1 + 1, :])
        sc = jnp.where(is_ctx, m_ref[R_CSC1:R_CSC1 + 1, :], m_ref[R_SC1:R_SC1 + 1, :])
        ms = jnp.mean(xin * xin, axis=-1, keepdims=True)
        n = xin * lax.rsqrt(ms + EPS) * g_ref[...]
        hn = n * (1.0 + sc) + sh
        o_ref[...] = hn.astype(BF16)
        ot_ref[...] = hn.T.astype(BF16)

    return _pcall_ride(
        body, ride, (ctx, x, g, modrows), name="norm1_fwd", grid=(nb,),
        in_specs=[pl.BlockSpec((TM, d), lambda i: (jnp.minimum(i, cb - 1), 0)),
                  pl.BlockSpec((TM, d), lambda i: (jnp.maximum(i - cb, 0), 0)),
                  pl.BlockSpec((1, d), lambda i: (0, 0)), pl.BlockSpec((8, d), lambda i: (0, 0))],
        out_specs=(pl.BlockSpec((TM, d), lambda i: (i, 0)), pl.BlockSpec((d, TM), lambda i: (0, i))),
        out_shape=(_sds((nb * TM, d), BF16), _sds((d, nb * TM), BF16)), compiler_params=_cp())


def _mix_fwd(o_f, o_b, p, h_f, h_b, cb, t_len):
    def body(of_ref, ob_ref, g_ref, gate_ref, hf_ref, hb_ref, mix_ref):
        o = of_ref[...] + ob_ref[...]
        g = g_ref[...]
        sg = g * _sigmoid(g)
        for hh in range(HEADS):
            sl = slice(hh * HEAD_DIM, (hh + 1) * HEAD_DIM)
            oh = o[:, sl]
            yc = oh - jnp.mean(oh, axis=-1, keepdims=True)
            var = jnp.mean(yc * yc, axis=-1, keepdims=True)
            mix_ref[:, sl] = (sg[:, sl] * (yc * lax.rsqrt(var + EPS))).astype(BF16)
        mix_ref[:, RET_W:] = ((hf_ref[...] + hb_ref[...]) * _gelu(gate_ref[...])).astype(BF16)

    row = lambda i: (i + cb, 0)
    return _pcall(
        body, name="mix_fwd", grid=(t_len // TM,),
        in_specs=[pl.BlockSpec((TM, RET_W), row), pl.BlockSpec((TM, RET_W), row),
                  pl.BlockSpec((TM, RET_W), lambda i: (i + cb, COL_G)), pl.BlockSpec((TM, LRU_W), lambda i: (i + cb, COL_GATE)),
                  pl.BlockSpec((TM, LRU_W), row), pl.BlockSpec((TM, LRU_W), row)],
        out_specs=pl.BlockSpec((TM, RET_W + LRU_W), lambda i: (i, 0)),
        out_shape=_sds((t_len, RET_W + LRU_W), BF16), compiler_params=_cp(),
    )(o_f, o_b, p, p, h_f, h_b)


def _res_norm2_epilogue(x, g, modrows, tm):
    t_len, d = x.shape
    tm = min(tm, t_len)

    def fn(y, rows, x_ref, g_ref, m_ref, y_ref, x1_ref, h2_ref, h2t_ref):
        y_ref[rows, :] = y
        x1 = x_ref[rows, :] + m_ref[R_G1:R_G1 + 1, :] * y
        ms = jnp.mean(x1 * x1, axis=-1, keepdims=True)
        n = x1 * lax.rsqrt(ms + EPS) * g_ref[...]
        x1_ref[rows, :] = x1
        h2 = n * (1.0 + m_ref[R_SC2:R_SC2 + 1, :]) + m_ref[R_SH2:R_SH2 + 1, :]
        h2_ref[rows, :] = h2.astype(BF16)
        h2t_ref[:, rows] = h2.T.astype(BF16)

    t = pl.BlockSpec((tm, d), lambda i: (_tile(i), 0))
    return _Epilogue(
        fn, (x, g, modrows),
        in_specs=[t, pl.BlockSpec((1, d), lambda i: (0, 0)), pl.BlockSpec((8, d), lambda i: (0, 0))],
        out_specs=(t, t, t, pl.BlockSpec((d, tm), lambda i: (0, _tile(i)))),
        out_shape=(_sds((t_len, d), F32), _sds((t_len, d), F32), _sds((t_len, d), BF16), _sds((d, t_len), BF16)), delay=1)


def _tile(i):
    return jnp.maximum(i - 1, 0)


def _zero_at_start(acc_ref, rows):
    @pl.when(pl.program_id(0) == 0)
    def _():
        acc_ref[...] = jnp.zeros_like(acc_ref)


def _final_epilogue(x1, target, fg, modrows, tm):
    t_len, d = x1.shape

    def fn(z, rows, x1_ref, t_ref, fg_ref, m_ref, dx2_ref, dz_ref, dzt_ref, acc_ref):
        _zero_at_start(acc_ref, rows)
        g2 = m_ref[R_G2:R_G2 + 1, :]
        x2 = x1_ref[rows, :] + g2 * z
        rstd = lax.rsqrt(jnp.mean(x2 * x2, axis=-1, keepdims=True) + EPS)
        xh = x2 * rstd
        fg = fg_ref[...]
        e = xh * fg - t_ref[rows, :]
        dy = e * (1.0 / d)
        dxh = dy * fg
        dx2 = rstd * (dxh - xh * jnp.mean(dxh * xh, axis=-1, keepdims=True))
        dx2_ref[rows, :] = dx2
        dz = g2 * dx2
        dz_ref[rows, :] = dz.astype(BF16)
        dzt_ref[:, rows] = dz.T.astype(BF16)
        acc_ref[0:1, :] += jnp.sum(dy * xh, axis=0, keepdims=True)
        acc_ref[1:2, :] += jnp.sum(dx2 * z, axis=0, keepdims=True)
        acc_ref[2:3, :] += jnp.sum(e * e, axis=0, keepdims=True)

    t = pl.BlockSpec((tm, d), lambda i: (i, 0))
    return _Epilogue(
        fn, (x1, target, fg, modrows),
        in_specs=[t, t, pl.BlockSpec((1, d), lambda i: (0, 0)), pl.BlockSpec((8, d), lambda i: (0, 0))],
        out_specs=(t, t, pl.BlockSpec((d, tm), lambda i: (0, i)), pl.BlockSpec((8, d), lambda i: (0, 0))),
        out_shape=(_sds((t_len, d), F32), _sds((t_len, d), BF16), _sds((d, t_len), BF16), _sds((8, d), F32)))


def _bwd_norm2_epilogue(x1, dx2, y, g, modrows, tm):
    t_len, d = x1.shape

    def fn(dh2, rows, x1_ref, dx2_ref, y_ref, g_ref, m_ref, dx1_ref, dy_ref, acc_ref):
        _zero_at_start(acc_ref, rows)
        x1 = x1_ref[rows, :]
        rstd = lax.rsqrt(jnp.mean(x1 * x1, axis=-1, keepdims=True) + EPS)
        xh = x1 * rstd
        gn = g_ref[...]
        dn = dh2 * (1.0 + m_ref[R_SC2:R_SC2 + 1, :])
        dxh = dn * gn
        dx1 = dx2_ref[rows, :] + rstd * (dxh - xh * jnp.mean(dxh * xh, axis=-1, keepdims=True))
        dx1_ref[rows, :] = dx1
        dy_ref[rows, :] = (m_ref[R_G1:R_G1 + 1, :] * dx1).astype(BF16)
        acc_ref[0:1, :] += jnp.sum(dh2, axis=0, keepdims=True)
        acc_ref[1:2, :] += jnp.sum(dh2 * xh * gn, axis=0, keepdims=True)
        acc_ref[2:3, :] += jnp.sum(dn * xh, axis=0, keepdims=True)
        acc_ref[3:4, :] += jnp.sum(dx1 * y_ref[rows, :], axis=0, keepdims=True)

    t = pl.BlockSpec((tm, d), lambda i: (i, 0))
    return _Epilogue(
        fn, (x1, dx2, y, g, modrows),
        in_specs=[t, t, t, pl.BlockSpec((1, d), lambda i: (0, 0)), pl.BlockSpec((8, d), lambda i: (0, 0))],
        out_specs=(t, t, pl.BlockSpec((8, d), lambda i: (0, 0))),
        out_shape=(_sds((t_len, d), F32), _sds((t_len, d), BF16), _sds((8, d), F32)))


def _mix_bwd_epilogue(o_f, o_b, p, h_f, h_b, cb):
    n = o_f.shape[0]
    tile = _tile

    def fn(dm, rows, of_ref, ob_ref, g_ref, gate_ref, hf_ref, hb_ref, do_ref, dg_ref, dgate_ref, dhs_ref):
        keep = jnp.where(pl.program_id(0) - 1 < cb, 0.0, 1.0)
        dm = dm * keep
        o = of_ref[rows, :] + ob_ref[rows, :]
        g = g_ref[rows, :]
        s = _sigmoid(g)
        sg = g * s
        dsg = s * (1.0 + g * (1.0 - s))
        for hh in range(HEADS):
            sl = slice(hh * HEAD_DIM, (hh + 1) * HEAD_DIM)
            oh = o[:, sl]
            yc = oh - jnp.mean(oh, axis=-1, keepdims=True)
            rs = lax.rsqrt(jnp.mean(yc * yc, axis=-1, keepdims=True) + EPS)
            gn = yc * rs
            dret = dm[:, sl]
            dgn = dret * sg[:, sl]
            dg_ref[rows, sl] = (dret * gn * dsg[:, sl]).astype(BF16)
            do_ref[rows, sl] = (rs * (dgn - jnp.mean(dgn, axis=-1, keepdims=True)
                                      - gn * jnp.mean(dgn * gn, axis=-1, keepdims=True))).astype(BF16)
        dlru = dm[:, RET_W:]
        gate = gate_ref[rows, :]
        dhs_ref[rows, :] = dlru * _gelu(gate)
        dgate_ref[rows, :] = (dlru * (hf_ref[rows, :] + hb_ref[rows, :]) * _dgelu(gate)).astype(BF16)

    t = pl.BlockSpec((TM, RET_W), lambda i: (tile(i), 0))
    return _Epilogue(
        fn, (o_f, o_b, p, p, h_f, h_b),
        in_specs=[t, t, pl.BlockSpec((TM, RET_W), lambda i: (tile(i), COL_G)),
                  pl.BlockSpec((TM, LRU_W), lambda i: (tile(i), COL_GATE)), t, t],
        out_specs=(t, t, t, t), out_shape=(_sds((n, RET_W), BF16),) * 3 + (_sds((n, RET_W), F32),),
        steps=n // TM, lhs_map=lambda i: (jnp.maximum(i - cb, 0), 0), delay=1)


def _bwd_norm1_epilogue(ctx, x, dx1, g, modrows, cb):
    t_len, d = x.shape

    def fn(dh, rows, ctx_ref, x_ref, dx1_ref, g_ref, m_ref, gx_ref, acc_ref):
        is_ctx = pl.program_id(0) < cb
        _zero_at_start(acc_ref, rows)
        xin = jnp.where(is_ctx, ctx_ref[rows, :], x_ref[rows, :])
        sc = jnp.where(is_ctx, m_ref[R_CSC1:R_CSC1 + 1, :], m_ref[R_SC1:R_SC1 + 1, :])
        rstd = lax.rsqrt(jnp.mean(xin * xin, axis=-1, keepdims=True) + EPS)
        xh = xin * rstd
        gn = g_ref[...]
        dn = dh * (1.0 + sc)
        dxh = dn * gn
        gx_ref[rows, :] = dx1_ref[rows, :] + rstd * (dxh - xh * jnp.mean(dxh * xh, axis=-1, keepdims=True))
        s0 = jnp.sum(dh, axis=0, keepdims=True)
        s1 = jnp.sum(dh * xh * gn, axis=0, keepdims=True)
        acc_ref[4:5, :] += jnp.sum(dn * xh, axis=0, keepdims=True)
        acc_ref[0:1, :] += jnp.where(is_ctx, s0, 0.0)
        acc_ref[1:2, :] += jnp.where(is_ctx, s1, 0.0)
        acc_ref[2:3, :] += jnp.where(is_ctx, 0.0, s0)
        acc_ref[3:4, :] += jnp.where(is_ctx, 0.0, s1)

    lat = pl.BlockSpec((TM, d), lambda i: (jnp.maximum(i - cb, 0), 0))
    return _Epilogue(
        fn, (ctx, x, dx1, g, modrows),
        in_specs=[pl.BlockSpec((TM, d), lambda i: (jnp.minimum(i, cb - 1), 0)),
                  lat, lat, pl.BlockSpec((1, d), lambda i: (0, 0)), pl.BlockSpec((8, d), lambda i: (0, 0))],
        out_specs=(lat, pl.BlockSpec((8, d), lambda i: (0, 0))),
        out_shape=(_sds((t_len, d), F32), _sds((8, d), F32)))


def _rot(x, cf, ss):
    return x * cf + pltpu.roll(x, HEAD_DIM // 2, 1) * ss


def _decay_exponents(dirn):
    ii = lax.broadcasted_iota(jnp.int32, (CHUNK, CHUNK), 0)
    jj = lax.broadcasted_iota(jnp.int32, (CHUNK, CHUNK), 1)
    rel = ii - jj if dirn == 0 else jj - ii
    pos = ii.astype(F32)
    if dirn == 0:
        cq, cs = pos + 1.0, (CHUNK - 1.0) - pos
    else:
        cq, cs = CHUNK - pos, pos
    return rel, jnp.maximum(rel, 0).astype(F32), cq, cs


def _store_decay(lg_ref, dec, relf_ref=None):
    for dirn in (0, 1):
        rel, relf, cq, cs = _decay_exponents(dirn)
        if relf_ref is not None:
            relf_ref[dirn] = relf
        for h in range(HEADS):
            lgv = lg_ref[dirn, h]
            wq, ws = jnp.exp(lgv * cq), jnp.exp(lgv * cs)
            dec[dirn, h, 0] = jnp.where(rel >= 0, jnp.exp(lgv * relf), 0.0)
            dec[dirn, h, 1] = wq
            dec[dirn, h, 2] = ws
            if relf_ref is not None:
                dec[dirn, h, 3] = wq * cq
                dec[dirn, h, 4] = ws * cs


def _ret_rows(cc, nc, step_of):
    return [lambda s, dirn=dirn: _tile_order(dirn, step_of(s), cc, nc) for dirn in (0, 1)]


def _ret_in_specs(rows):
    specs = []
    for row in rows:
        specs += [pl.BlockSpec((CHUNK, RET_W), lambda s, o=o, row=row: (row(s), o)) for o in (0, 1, 2)]
    return specs


def _ret_fwd(qkv, lg, cc, ride=None):
    n = qkv.shape[0]
    nc = n // CHUNK
    rows = _ret_rows(cc, nc, lambda s: s)

    def body(lg_ref, q0, k0, v0, q1, k1, v1, o0, o1, sp0, sp1, st, dec):
        @pl.when(pl.program_id(0) == 0)
        def _():
            st[...] = jnp.zeros_like(st)
            _store_decay(lg_ref, dec)

        refs = ((q0, k0, v0, o0, sp0), (q1, k1, v1, o1, sp1))
        chains = [(dirn, h, slice(h * HEAD_DIM, (h + 1) * HEAD_DIM)) for dirn in (0, 1) for h in range(HEADS)]
        scores, cross, update = [], [], []
        for dirn, h, sl in chains:
            q_ref, k_ref, v_ref, _, sp_ref = refs[dirn]
            q, k, v = q_ref[:, sl], k_ref[:, sl], v_ref[:, sl]
            sp = st[dirn, h]
            sp_ref[h] = sp
            scores.append(_dot_nt(q, k))
            cross.append(_dot(q * dec[dirn, h, 1], sp))
            update.append(_dot_tn(k * dec[dirn, h, 2], v))
        masked = [(a * dec[dirn, h, 0]).astype(BF16) for a, (dirn, h, _) in zip(scores, chains)]
        intra = [_dot(sc, refs[dirn][2][:, sl]) for sc, (dirn, h, sl) in zip(masked, chains)]
        for (dirn, h, sl), o_in, o_cr, upd in zip(chains, intra, cross, update):
            refs[dirn][3][:, sl] = o_in + o_cr
            st[dirn, h] = jnp.exp(lg_ref[dirn, h] * CHUNK) * st[dirn, h] + upd

    o_specs = [pl.BlockSpec((CHUNK, RET_W), lambda s, row=row: (row(s), 0)) for row in rows]
    state = pl.BlockSpec((None, HEADS, CHUNK, HEAD_DIM), lambda s: (s, 0, 0, 0))
    return _pcall_ride(
        body, ride, (lg,) + (qkv,) * 6, name="ret_fwd", grid=(nc,),
        in_specs=[pl.BlockSpec(memory_space=pltpu.SMEM)] + _ret_in_specs(rows),
        out_specs=(o_specs[0], o_specs[1], state, state),
        out_shape=(_sds((n, RET_W), F32),) * 2 + (_sds((nc, HEADS, CHUNK, HEAD_DIM), F32),) * 2,
        scratch_shapes=[pltpu.VMEM((2, HEADS, CHUNK, HEAD_DIM), F32), pltpu.VMEM((2, HEADS, 3, CHUNK, CHUNK), F32)],
        compiler_params=_cp())


def _ret_bwd(qkv, lg, do, s_prev, cc, ride=None):
    n = qkv.shape[0]
    nc = n // CHUNK
    rows = _ret_rows(cc, nc, lambda s: nc - 1 - s)

    def body(lg_ref, q0, k0, v0, q1, k1, v1, do0, do1, sp0, sp1, dq0, dk0, dv0, dq1, dk1, dv1, dlg_ref, dst, dec, relf):
        @pl.when(pl.program_id(0) == 0)
        def _():
            dst[...] = jnp.zeros_like(dst)
            dlg_ref[...] = jnp.zeros_like(dlg_ref)
            _store_decay(lg_ref, dec, relf)

        refs = ((q0, k0, v0, do0, sp0, dq0, dk0, dv0), (q1, k1, v1, do1, sp1, dq1, dk1, dv1))
        chains = [(dirn, h, slice(h * HEAD_DIM, (h + 1) * HEAD_DIM)) for dirn in (0, 1) for h in range(HEADS)]

        def tiles(dirn, sl):
            q_ref, k_ref, v_ref, do_ref = refs[dirn][:4]
            return q_ref[:, sl], k_ref[:, sl], v_ref[:, sl], do_ref[:, sl]

        a_s, g1_s, da_s, h1_s = [], [], [], []
        for dirn, h, sl in chains:
            q, k, v, dov = tiles(dirn, sl)
            a_s.append(_dot_nt(q, k))
            g1_s.append(_dot_nt(dov, refs[dirn][4][h]))
            da_s.append(_dot_nt(dov, v))
            h1_s.append(_dot_nt(v, dst[dirn, h]))
        da_s = [da * dec[dirn, h, 0] for da, (dirn, h, _) in zip(da_s, chains)]
        dq_s, dk_s, dv_s, ds_s = [], [], [], []
        for (dirn, h, sl), a, da in zip(chains, a_s, da_s):
            q, k, v, dov = tiles(dirn, sl)
            dq_s.append(_dot(da, k))
            dk_s.append(_dot_tn(da, q))
            dv_s.append(_dot_tn(a * dec[dirn, h, 0], dov) + _dot(k * dec[dirn, h, 2], dst[dirn, h]))
            ds_s.append(_dot_tn(q * dec[dirn, h, 1], dov))
        for (dirn, h, sl), a, g1, da, h1, dq2, dk2, dv, ds2 in zip(chains, a_s, g1_s, da_s, h1_s, dq_s, dk_s, dv_s, ds_s):
            q, k, _, _ = tiles(dirn, sl)
            dq_ref, dk_ref, dv_ref = refs[dirn][5:]
            sp, dsn = refs[dirn][4][h], dst[dirn, h]
            gc = jnp.exp(lg_ref[dirn, h] * CHUNK)
            dq_ref[:, sl] = (g1 * dec[dirn, h, 1] + dq2).astype(BF16)
            dk_ref[:, sl] = (dk2 + h1 * dec[dirn, h, 2]).astype(BF16)
            dv_ref[:, sl] = dv.astype(BF16)
            term = da * a * relf[dirn] + q * g1 * dec[dirn, h, 3] + k * h1 * dec[dirn, h, 4] + sp * dsn * (CHUNK * gc)
            dlg_ref[dirn * HEADS + h:dirn * HEADS + h + 1, 0:HEAD_DIM] += jnp.sum(term, axis=0, keepdims=True)
            dst[dirn, h] = gc * dsn + ds2

    wide = [pl.BlockSpec((CHUNK, RET_W), lambda s, row=row: (row(s), 0)) for row in rows]
    state = pl.BlockSpec((None, HEADS, CHUNK, HEAD_DIM), lambda s: (nc - 1 - s, 0, 0, 0))
    return _pcall_ride(
        body, ride, (lg,) + (qkv,) * 6 + (do, do, s_prev[0], s_prev[1]), name="ret_bwd", grid=(nc,),
        in_specs=[pl.BlockSpec(memory_space=pltpu.SMEM)] + _ret_in_specs(rows) + wide + [state, state],
        out_specs=(wide[0],) * 3 + (wide[1],) * 3 + (pl.BlockSpec((2 * HEADS, 8 * HEAD_DIM), lambda s: (0, 0)),),
        out_shape=(_sds((n, RET_W), BF16),) * 6 + (_sds((2 * HEADS, 8 * HEAD_DIM), F32),),
        scratch_shapes=[pltpu.VMEM((2, HEADS, CHUNK, HEAD_DIM), F32), pltpu.VMEM((2, HEADS, 5, CHUNK, CHUNK), F32),
                        pltpu.VMEM((2, CHUNK, CHUNK), F32)],
        compiler_params=_cp())


def _shift_rows(cur, prev8, next8, k, seg_start, seg_end):
    tm = cur.shape[0]
    rows = _rows_iota(cur.shape)
    if k < 0:
        out = pltpu.roll(cur, -k, 0)
        for j in range(-k):
            halo = jnp.where(seg_start, 0.0, prev8[SUB + k + j:SUB + k + j + 1, :])
            out = jnp.where(rows == j, halo, out)
    else:
        out = pltpu.roll(cur, tm - k, 0)
        for j in range(k):
            halo = jnp.where(seg_end, 0.0, next8[j:j + 1, :])
            out = jnp.where(rows == tm - k + j, halo, out)
    return out


def _seg_flags(t, cb, nb):
    return jnp.logical_or(t == 0, t == cb), jnp.logical_or(t == cb - 1, t == nb - 1)


def _halo_specs(tile_of, n_rows, col):
    per = TM // SUB
    return [pl.BlockSpec((TM, LRU_W), lambda s: (tile_of(s), col)),
            pl.BlockSpec((SUB, LRU_W), lambda s: (jnp.maximum(tile_of(s) * per - 1, 0), col)),
            pl.BlockSpec((SUB, LRU_W), lambda s: (jnp.minimum((tile_of(s) + 1) * per, n_rows // SUB - 1), col))]


def _lru_gates(xr, prev8, next8, seg_start, seg_end, cw_ref, cb_ref, wg_ref, bg_ref, sp_ref):
    xm1 = _shift_rows(xr, prev8, next8, -1, seg_start, seg_end)
    xp1 = _shift_rows(xr, prev8, next8, 1, seg_start, seg_end)
    xp2 = _shift_rows(xr, prev8, next8, 2, seg_start, seg_end)
    xc = cb_ref[...] + xm1 * cw_ref[0:1, :] + xr * cw_ref[1:2, :] + xp1 * cw_ref[2:3, :] + xp2 * cw_ref[3:4, :]
    pre = _dot(xc, wg_ref[...]) + bg_ref[...]
    r = _sigmoid(pre[:, :LRU_W])
    i = _sigmoid(pre[:, LRU_W:])
    la = (-LRU_C) * r * sp_ref[...]
    a = jnp.exp(la)
    th = jnp.tanh(la)
    sq = jnp.sqrt(-2.0 * th / (1.0 - th))
    return xc, r, i, a, sq


def _scan_tile(a, b, ascending, a_sc, b_sc, carry, out_ref):
    tm, w = a.shape
    nsub = tm // SUB
    a = a.reshape(nsub, SUB, w)
    b = b.reshape(nsub, SUB, w)
    r8 = lax.broadcasted_iota(jnp.int32, a.shape, 1)
    for k in (1, 2, 4):
        if ascending:
            m = r8 >= k
            a_s, b_s = pltpu.roll(a, k, 1), pltpu.roll(b, k, 1)
        else:
            m = r8 < SUB - k
            a_s, b_s = pltpu.roll(a, SUB - k, 1), pltpu.roll(b, SUB - k, 1)
        b = a * jnp.where(m, b_s, 0.0) + b
        a = a * jnp.where(m, a_s, 1.0)
    a_sc[...] = a.reshape(tm, w)
    b_sc[...] = b.reshape(tm, w)

    def step(j, c):
        off = pl.multiple_of((j if ascending else nsub - 1 - j) * SUB, SUB)
        hb = a_sc[pl.ds(off, SUB), :] * c + b_sc[pl.ds(off, SUB), :]
        out_ref[pl.ds(off, SUB), :] = hb
        last = hb[SUB - 1:SUB, :] if ascending else hb[0:1, :]
        return jnp.broadcast_to(last, c.shape)

    carry[...] = lax.fori_loop(0, nsub, step, carry[...], unroll=8)


def _lru_fwd(p, wg, bg, sp, cw, cbias, dirn, cb, ride=None):
    n = p.shape[0]
    nb = n // TM
    tile_of = lambda s: _tile_order(dirn, s, cb, nb)

    def body(x_ref, xp_ref, xn_ref, wg_ref, bg_ref, sp_ref, cw_ref, cb_ref, h_ref, cin_ref, carry, a_sc, b_sc):
        s = pl.program_id(0)

        @pl.when(s == 0)
        def _():
            carry[...] = jnp.zeros_like(carry)

        seg_start, seg_end = _seg_flags(tile_of(s), cb, nb)
        xc, r, i, a, sq = _lru_gates(x_ref[...], xp_ref[...], xn_ref[...], seg_start, seg_end,
                                     cw_ref, cb_ref, wg_ref, bg_ref, sp_ref)
        cin_ref[...] = carry[...]
        _scan_tile(a, sq * (i * xc), dirn == 0, a_sc, b_sc, carry, h_ref)

    full = lambda shape: pl.BlockSpec(shape, lambda s: (0,) * len(shape))
    return _pcall_ride(
        body, ride, (p, p, p, wg, bg, sp, cw, cbias), name=f"lru_fwd{dirn}", grid=(nb,),
        in_specs=_halo_specs(tile_of, n, COL_XR) + [full((LRU_W, 2 * LRU_W)), full((1, 2 * LRU_W)), full((1, LRU_W)),
                                               full((4, LRU_W)), full((1, LRU_W))],
        out_specs=(pl.BlockSpec((TM, LRU_W), lambda s: (tile_of(s), 0)),
                   pl.BlockSpec((None, SUB, LRU_W), lambda s: (tile_of(s), 0, 0))),
        out_shape=(_sds((n, LRU_W), F32), _sds((nb, SUB, LRU_W), F32)),
        scratch_shapes=[pltpu.VMEM((SUB, LRU_W), F32), pltpu.VMEM((TM, LRU_W), F32), pltpu.VMEM((TM, LRU_W), F32)],
        compiler_params=_cp())


def _lru_bwd(p, wg, bg, sp, cw, cbias, h, cin, dhs, dirn, cb, ride=None):
    n = p.shape[0]
    nb = n // TM
    tile_of = lambda s: _tile_order(dirn, nb - 1 - s, cb, nb)

    def body(x_ref, xp_ref, xn_ref, wg_ref, bg_ref, sp_ref, cw_ref, cb_ref, h_ref, cin_ref, dhs_ref,
             dxc_ref, dwd_ref, acc_ref, carry, a_sc, b_sc, mu_sc, dwg_ref):
        s = pl.program_id(0)

        @pl.when(s == 0)
        def _():
            carry[...] = jnp.zeros_like(carry)
            dwg_ref[...] = jnp.zeros_like(dwg_ref)
            acc_ref[...] = jnp.zeros_like(acc_ref)

        seg_start, seg_end = _seg_flags(tile_of(s), cb, nb)
        xc, r, i, a, sq = _lru_gates(x_ref[...], xp_ref[...], xn_ref[...], seg_start, seg_end,
                                     cw_ref, cb_ref, wg_ref, bg_ref, sp_ref)
        rows = _rows_iota(a.shape)
        hv = h_ref[...]
        dh = dhs_ref[...]
        mu_next = carry[0:1, :]
        _scan_tile(a, a * dh, dirn == 1, a_sc, b_sc, carry, mu_sc)
        mu = mu_sc[...]
        if dirn == 0:
            hprev = jnp.where(rows == 0, cin_ref[0:1, :], pltpu.roll(hv, 1, 0))
            lam = dh + jnp.where(rows == TM - 1, mu_next, pltpu.roll(mu, TM - 1, 0))
        else:
            hprev = jnp.where(rows == TM - 1, cin_ref[0:1, :], pltpu.roll(hv, TM - 1, 0))
            lam = dh + jnp.where(rows == 0, mu_next, pltpu.roll(mu, 1, 0))
        ds = lam * (i * xc)
        di = lam * (sq * xc)
        dla = lam * hprev * a - ds * (a * a) / jnp.maximum(sq, 1e-20)
        dpr = dla * ((-LRU_C) * sp_ref[...]) * r * (1.0 - r)
        dpi = di * i * (1.0 - i)
        dpre = jnp.concatenate([dpr, dpi], axis=1)
        dxc_ref[...] = lam * (sq * i) + _dot_nt(dpre, wg_ref[...])
        dwg_ref[...] += _dot_tn(xc, dpre)
        acc_ref[0:1, :] += jnp.sum(dpre, axis=0, keepdims=True)
        acc_ref[1:2, 0:LRU_W] += jnp.sum(dla * ((-LRU_C) * r), axis=0, keepdims=True)

        @pl.when(s == nb - 1)
        def _():
            low = lax.broadcasted_iota(jnp.int32, (LRU_BD, 2 * LRU_BD), 1) < LRU_BD
            for half in (0, LRU_W):
                for m in range(LRU_BLOCKS // 2):
                    lanes = slice(half + 2 * LRU_BD * m, half + 2 * LRU_BD * (m + 1))
                    even = dwg_ref[2 * m * LRU_BD:(2 * m + 1) * LRU_BD, lanes]
                    odd = dwg_ref[(2 * m + 1) * LRU_BD:(2 * m + 2) * LRU_BD, lanes]
                    dwd_ref[:, lanes] = jnp.where(low, even, odd)

    full = lambda shape: pl.BlockSpec(shape, lambda s: (0,) * len(shape))
    tile = pl.BlockSpec((TM, LRU_W), lambda s: (tile_of(s), 0))
    return _pcall_ride(
        body, ride, (p, p, p, wg, bg, sp, cw, cbias, h, cin, dhs), name=f"lru_bwd{dirn}", grid=(nb,),
        in_specs=_halo_specs(tile_of, n, COL_XR) + [full((LRU_W, 2 * LRU_W)), full((1, 2 * LRU_W)), full((1, LRU_W)),
                                               full((4, LRU_W)), full((1, LRU_W)), tile,
                                               pl.BlockSpec((None, SUB, LRU_W), lambda s: (tile_of(s), 0, 0)), tile],
        out_specs=(tile, full((LRU_BD, 2 * LRU_W)), full((8, 2 * LRU_W))),
        out_shape=(_sds((n, LRU_W), F32), _sds((LRU_BD, 2 * LRU_W), F32), _sds((8, 2 * LRU_W), F32)),
        scratch_shapes=[pltpu.VMEM((SUB, LRU_W), F32)] + [pltpu.VMEM((TM, LRU_W), F32)] * 3
        + [pltpu.VMEM((LRU_W, 2 * LRU_W), F32)],
        compiler_params=_cp())


def _assemble_dp(dqs, dks, dvs, dg, dgate, dxcs, p, cw, cosf, sins, cb, ride=None):
    n = p.shape[0]
    nb = n // TM
    tile_of = lambda s: s

    def body(dqf, dqb, dkf, dkb, dvf, dvb, dg_ref, dgate_ref, cf, pf, nf, cb_, pb, nb_, x_ref, xp_ref, xn_ref,
             cw_ref, cos_ref, sin_ref, dp_ref, acc_ref):
        s = pl.program_id(0)

        @pl.when(s == 0)
        def _():
            acc_ref[...] = jnp.zeros_like(acc_ref)

        seg_start, seg_end = _seg_flags(s, cb, nb)
        dq = dqf[...].astype(F32) + dqb[...].astype(F32)
        dk = dkf[...].astype(F32) + dkb[...].astype(F32)
        cosv, sinv = cos_ref[...], sin_ref[...]
        for h in range(HEADS):
            sl = slice(h * HEAD_DIM, (h + 1) * HEAD_DIM)
            sk = slice(RET_W + h * HEAD_DIM, RET_W + (h + 1) * HEAD_DIM)
            dp_ref[:, sl] = (dq[:, sl] * cosv + pltpu.roll(dq[:, sl] * sinv, HEAD_DIM // 2, 1)).astype(BF16)
            dp_ref[:, sk] = ((dk[:, sl] * cosv + pltpu.roll(dk[:, sl] * sinv, HEAD_DIM // 2, 1)) * K_SCALE).astype(BF16)
        dp_ref[:, 2 * RET_W:3 * RET_W] = (dvf[...].astype(F32) + dvb[...].astype(F32)).astype(BF16)
        dp_ref[:, 3 * RET_W:4 * RET_W] = dg_ref[...].astype(BF16)
        dxc = cf[...] + cb_[...]
        dprev = pf[...] + pb[...]
        dnext = nf[...] + nb_[...]
        dxr = (_shift_rows(dxc, dprev, dnext, 1, seg_start, seg_end) * cw_ref[0:1, :] + dxc * cw_ref[1:2, :]
               + _shift_rows(dxc, dprev, dnext, -1, seg_start, seg_end) * cw_ref[2:3, :]
               + _shift_rows(dxc, dprev, dnext, -2, seg_start, seg_end) * cw_ref[3:4, :])
        dp_ref[:, 4 * RET_W:4 * RET_W + LRU_W] = dxr.astype(BF16)
        dp_ref[:, 4 * RET_W + LRU_W:] = dgate_ref[...].astype(BF16)
        xr, xp, xn = x_ref[...], xp_ref[...], xn_ref[...]
        for j, k in enumerate((-1, 0, 1, 2)):
            xs = xr if k == 0 else _shift_rows(xr, xp, xn, k, seg_start, seg_end)
            acc_ref[j:j + 1, 0:LRU_W] += jnp.sum(dxc * xs, axis=0, keepdims=True)
        acc_ref[4:5, 0:LRU_W] += jnp.sum(dxc, axis=0, keepdims=True)

    t = pl.BlockSpec((TM, RET_W), lambda s: (s, 0))
    args = (dqs[0], dqs[1], dks[0], dks[1], dvs[0], dvs[1], dg, dgate, dxcs[0], dxcs[0], dxcs[0], dxcs[1], dxcs[1], dxcs[1],
            p, p, p, cw, cosf, sins)
    tab = pl.BlockSpec((TM, HEAD_DIM), lambda s: (s, 0))
    return _pcall_ride(
        body, ride, args, name="assemble_dp", grid=(nb,),
        in_specs=[t] * 8 + _halo_specs(tile_of, n, 0) * 2 + _halo_specs(tile_of, n, COL_XR)
        + [pl.BlockSpec((4, LRU_W), lambda s: (0, 0)), tab, tab],
        out_specs=(pl.BlockSpec((TM, 4 * RET_W + 2 * LRU_W), lambda s: (s, 0)), pl.BlockSpec((8, 2 * LRU_W), lambda s: (0, 0))),
        out_shape=(_sds((n, 4 * RET_W + 2 * LRU_W), BF16), _sds((8, 2 * LRU_W), F32)), compiler_params=_cp())


def _adamw(g, w, m, v):
    nm = ADAM_B1 * m + (1.0 - ADAM_B1) * g
    nv = ADAM_B2 * v + (1.0 - ADAM_B2) * (g * g)
    m_hat = nm / (1.0 - ADAM_B1 ** ADAM_STEP)
    v_hat = nv / (1.0 - ADAM_B2 ** ADAM_STEP)
    return (-ADAM_LR) * (m_hat / (jnp.sqrt(v_hat) + ADAM_EPS) + ADAM_WD * w), nm, nv


def _adam_many(items, name):
    n = len(items)

    def body(*refs):
        for i in range(n):
            g, w, m, v = (r[...] for r in refs[4 * i:4 * i + 4])
            for o_ref, val in zip(refs[4 * n + 3 * i:4 * n + 3 * i + 3], _adamw(g, w, m, v)):
                o_ref[...] = val

    out_shape = tuple(_sds(it[1].shape, F32) for it in items for _ in range(3))
    res = _pcall(body, name=name, out_shape=out_shape, compiler_params=_cp())(*[a for it in items for a in it])
    return [tuple(res[3 * i:3 * i + 3]) for i in range(n)]


def _sum_adam(parts_list, w, m, v, name):
    nparts, _, c = parts_list[0].shape
    r = w.shape[0]
    tr = min(parts_list[0].shape[1], 128)
    starts, o = [], 0
    for pa in parts_list:
        starts.append(o)
        o += pa.shape[1] // tr
    nseg = len(parts_list)

    def body(*refs):
        p_refs = refs[:nseg]
        w_ref, m_ref, v_ref, g_ref, d_ref, nm_ref, nv_ref = refs[nseg:]
        i = pl.program_id(0)
        for s, p_ref in enumerate(p_refs):
            end = starts[s + 1] if s + 1 < nseg else r // tr

            @pl.when(jnp.logical_and(i >= starts[s], i < end))
            def _():
                g = p_ref[0].astype(F32)
                for j in range(1, nparts):
                    g = g + p_ref[j].astype(F32)
                g_ref[...] = g
                d_ref[...], nm_ref[...], nv_ref[...] = _adamw(g, w_ref[...], m_ref[...], v_ref[...])

    def seg_spec(s):
        last = parts_list[s].shape[1] // tr - 1
        return pl.BlockSpec((nparts, tr, c), lambda i: (0, jnp.clip(i - starts[s], 0, last), 0))

    t = pl.BlockSpec((tr, c), lambda i: (i, 0))
    return _pcall(
        body, name=name, grid=(r // tr,),
        in_specs=[seg_spec(s) for s in range(nseg)] + [t, t, t],
        out_specs=(t, t, t, t), out_shape=(_sds((r, c), F32),) * 4, compiler_params=_cp(),
    )(*parts_list, w, m, v)


def _sum_parts(parts, name):
    nparts, r, c = parts.shape

    def body(p_ref, o_ref):
        g = p_ref[0]
        for j in range(1, nparts):
            g = g + p_ref[j]
        o_ref[...] = g

    return _pcall(body, name=name, out_shape=_sds((r, c), parts.dtype), compiler_params=_cp())(parts)


def _rot_tables(l_len, t_len):
    rows = t_len // GRID_W
    n_freq = HEAD_DIM // 4
    inv = ROPE_BASE ** (-jnp.arange(n_freq, dtype=F32) / n_freq)
    ang_r = jnp.arange(rows, dtype=F32)[:, None] * inv
    ang_c = jnp.arange(GRID_W, dtype=F32)[:, None] * inv
    cos = jnp.concatenate([jnp.repeat(jnp.cos(ang_r), GRID_W, axis=0), jnp.tile(jnp.cos(ang_c), (rows, 1))], axis=-1)
    sin = jnp.concatenate([jnp.repeat(jnp.sin(ang_r), GRID_W, axis=0), jnp.tile(jnp.sin(ang_c), (rows, 1))], axis=-1)
    cosf = jnp.concatenate([jnp.ones((l_len, HEAD_DIM), F32), jnp.concatenate([cos, cos], axis=-1)], axis=0)
    sins = jnp.concatenate([jnp.zeros((l_len, HEAD_DIM), F32), jnp.concatenate([-sin, sin], axis=-1)], axis=0)
    return cosf, sins


def _block_diag(w):
    eye = jnp.eye(LRU_BLOCKS, dtype=w.dtype)
    return (w[:, :, None, :] * eye[:, None, :, None]).reshape(LRU_W, LRU_W)


def _blocks_from_lanes(dwd_half):
    return dwd_half.reshape(LRU_BD, LRU_BLOCKS, LRU_BD).transpose(1, 0, 2)


def _silu(x):
    return x * jax.nn.sigmoid(x)


def kernel(x, c, ctx, c_ctx, w_ada, b_ada, norm1_g, norm2_g, w_in, ret_decay, conv_w, conv_b, lru_wa, lru_ba, lru_wx, lru_bx, lru_lambda, w_out, w_mlp1, w_mlp2, final_g, loss_target, m_c_ctx, m_w_ada, m_b_ada, m_norm1_g, m_norm2_g, m_w_in, m_ret_decay, m_conv_w, m_conv_b, m_lru_wa, m_lru_ba, m_lru_wx, m_lru_bx, m_lru_lambda, m_w_out, m_w_mlp1, m_w_mlp2, m_final_g, v_c_ctx, v_w_ada, v_b_ada, v_norm1_g, v_norm2_g, v_w_in, v_ret_decay, v_conv_w, v_conv_b, v_lru_wa, v_lru_ba, v_lru_wx, v_lru_bx, v_lru_lambda, v_w_out, v_w_mlp1, v_w_mlp2, v_final_g):
    t_len, d = x.shape[1], x.shape[2]
    l_len = ctx.shape[1]
    cb, cc = l_len // TM, l_len // CHUNK
    me = 4 * lax.axis_index("x") + 2 * lax.axis_index("y") + lax.axis_index("c")
    x2d, ctx2d, tgt2d = x[0], ctx[0], loss_target[0]
    ada_cols = w_ada.shape[2]
    wa2d = w_ada[0]

    sc_loc = conv_w.shape[2]
    pack_a = jnp.zeros((8, d), F32)
    pack_a = pack_a.at[0].set(_silu(c[0]))
    pack_a = pack_a.at[1, :4 * sc_loc].set(conv_w[0].reshape(-1))
    pack_a = pack_a.at[2, :2 * sc_loc].set(lru_ba[0].reshape(-1))
    pack_a = pack_a.at[3, :2 * sc_loc].set(lru_bx[0].reshape(-1))
    pack_a = pack_a.at[4, :2 * sc_loc].set(lru_lambda[0].reshape(-1))
    all_a = _all_gather_small(pack_a, "gather_small_in")
    s16 = jnp.zeros((16, d), F32).at[0:8].set(all_a[:, 0, :]).at[8].set(_silu(c_ctx))

    def unshard(row, k):
        return all_a[:, row, :k * sc_loc].reshape(N_DEV, k, sc_loc).transpose(1, 0, 2).reshape(k, N_DEV * sc_loc)

    conv_w_full = unshard(1, 4)
    ba_full, bx_full, lam_full = unshard(2, 2), unshard(3, 2), unshard(4, 2)

    b_cols = lax.dynamic_slice(b_ada, (0, me * ada_cols), (1, ada_cols))
    mod_parts = _all_gather_small(_mod_part(s16, wa2d, b_cols), "gather_mod")
    mod_all = mod_parts.transpose(1, 0, 2).reshape(16, N_DEV * ada_cols)
    mod_me = lax.dynamic_slice(mod_all, (me, 0), (1, 6 * d)).reshape(6, d)
    mod_c = mod_all[8].reshape(6, d)
    modrows = jnp.concatenate([mod_c[0:2], mod_me], axis=0)

    lg = jax.nn.log_sigmoid(ret_decay[0])
    sp = jax.nn.softplus(-lam_full)
    wg = [jnp.concatenate([_block_diag(lru_wa[0, dd]), _block_diag(lru_wx[0, dd])], axis=1).astype(BF16) for dd in (0, 1)]
    bg = [jnp.concatenate([ba_full[dd], bx_full[dd]])[None, :] for dd in (0, 1)]
    cosf, sins = _rot_tables(l_len, t_len)

    (hn, hnt), win_g = _norm1_fwd(ctx2d, x2d, norm1_g, modrows, cb, ride=("gather", w_in[0].astype(BF16)))
    (qkv, p), w1_g = _mm_in(hn, win_g, cosf, sins, ride=("gather", w_mlp1[0].astype(BF16)))
    (o0, o1, sp0, sp1), _ = _ret_fwd(qkv, lg, cc)
    o, s_prev = [o0, o1], [sp0, sp1]
    (h0, cin0), wout_g = _lru_fwd(p, wg[0], bg[0], sp[0:1], conv_w_full, conv_b, 0, cb, ride=("gather", w_out[0].astype(BF16)))
    wout_g = wout_g.reshape(1, d, d)
    (h1, cin1), _ = _lru_fwd(p, wg[1], bg[1], sp[1:2], conv_w_full, conv_b, 1, cb)
    h, cin = [h0, h1], [cin0, cin1]
    mix = _mix_fwd(o[0], o[1], p, h[0], h[1], cb, t_len)
    y, x1, h2, h2t = _mm_nn(mix, wout_g, F32, "mm_out_norm2", tm=TL, epi=_res_norm2_epilogue(x2d, norm2_g, modrows, TL))
    r, w2_g = _mm_nn(h2, w1_g, BF16, "mm_mlp1", relu_out=True, tm=TL, ride=("gather", w_mlp2[0].astype(BF16)))
    w2_g = w2_g.reshape(1, 4 * d, d)
    dx2, dz, dzt, facc = _mm_nn(r, w2_g, F32, "mm_mlp2_final", square_lhs=True, tm=TL, vmem_mb=58,
                                epi=_final_epilogue(x1, tgt2d, final_g[None, :], modrows, TL))

    du = _mm_nt(dz, w2_g, BF16, "mm_da2", relu_mul=r, tm=TL, vmem_mb=58)
    gw2_lo, gw2_hi = _mm_wgrad(dzt, r, "mm_dw2", w2_g.shape[1] // N_DEV, BF16, transpose_out=True, square_rhs=True,
                               halves=True)
    gw1_lo, gw1_hi = _mm_wgrad(h2t, du, "mm_dw1", w1_g.shape[2], BF16, halves=True)
    (dx1, dy, n2acc), gw2_lo_all = _mm_nt(du, w1_g, F32, "mm_dh2_norm2", ride=("a2a", gw2_lo), tm=TL, vmem_mb=60,
                                          epi=_bwd_norm2_epilogue(x1, dx2, y, norm2_g, modrows, TL))
    gwo = _mm_tn(mix, dy, "mm_dwout", False, d, BF16, 512).reshape(N_DEV, -1, d)
    (do, dg, dgate, dhs), gwo_all = _mm_nt(dy, wout_g, F32, "mm_dmix_mix", ride=("a2a", gwo),
                                           epi=_mix_bwd_epilogue(o[0], o[1], p, h[0], h[1], cb))
    dxcs, dwgs, laccs, rides, got = [], [], [], [("a2a", gw1_hi), ("a2a", gw2_hi)], []
    (dq0, dk0, dv0, dq1, dk1, dv1, dlg_lanes), gw1_lo_all = _ret_bwd(qkv, lg, do, s_prev, cc, ride=("a2a", gw1_lo))
    dqs, dks, dvs = [dq0, dq1], [dk0, dk1], [dv0, dv1]
    for dd in (0, 1):
        (dxc_, dwg_, lacc_), got_ = _lru_bwd(p, wg[dd], bg[dd], sp[dd:dd + 1], conv_w_full, conv_b, h[dd], cin[dd], dhs, dd, cb,
                                             ride=rides[dd])
        dxcs.append(dxc_); dwgs.append(dwg_); laccs.append(lacc_); got.append(got_)
    gw1_hi_all, gw2_hi_all = got
    gw1_all, gw2_all = [gw1_lo_all, gw1_hi_all], [gw2_lo_all, gw2_hi_all]
    (dp, cacc), _ = _assemble_dp(dqs, dks, dvs, dg, dgate, dxcs, p, conv_w_full, cosf, sins, cb)
    pack_b = jnp.concatenate([n2acc, facc, cacc, laccs[0], laccs[1], dlg_lanes, dwgs[0], dwgs[1]], axis=0)
    gwi, all_b = _mm_wgrad(hnt, dp, "mm_dwin", win_g.shape[2], BF16, ride=("gather", pack_b))
    (grad_x, n1acc), gwi_all = _mm_nt(dp, win_g, F32, "mm_dhn_norm1", ride=("a2a", gwi),
                                      epi=_bwd_norm1_epilogue(ctx2d, x2d, dx1, norm1_g, modrows, cb))
    all_n1 = _all_gather_small(n1acc, "gather_norm1_grads")
    tot = _sum_parts(all_b, "sum_small_grads")
    t_n1 = _sum_parts(all_n1, "sum_norm1_grads")
    t_n2, t_f, t_conv, t_dlg = tot[0:8], tot[8:16], tot[16:24, :LRU_W], tot[40:48, :HEAD_DIM]
    t_l = [tot[24:32], tot[32:40]]
    t_dwd = [tot[48:48 + LRU_BD], tot[48 + LRU_BD:48 + 2 * LRU_BD]]
    loss = (0.5 / d) * jnp.sum(t_f[2])
    t_wa = jnp.stack([_blocks_from_lanes(t_dwd[dd][:, :LRU_W]) for dd in (0, 1)])
    t_wx = jnp.stack([_blocks_from_lanes(t_dwd[dd][:, LRU_W:]) for dd in (0, 1)])
    t_ba = jnp.stack([t_l[dd][0, :LRU_W] for dd in (0, 1)])
    t_bx = jnp.stack([t_l[dd][0, LRU_W:] for dd in (0, 1)])
    t_sp = jnp.stack([t_l[dd][1, :LRU_W] for dd in (0, 1)])
    dm_rows = jnp.stack([all_n1[:, 2, :], all_n1[:, 3, :], all_b[:, 3, :], all_b[:, 0, :], all_b[:, 1, :], all_b[:, 9, :]],
                        axis=1).reshape(N_DEV, 6 * d)
    dm_c = jnp.concatenate([t_n1[0], t_n1[1], jnp.zeros((4 * d,), F32)])
    dm16 = jnp.zeros((16, 6 * d), F32).at[0:8].set(dm_rows).at[8].set(dm_c)
    g_b_ada = jnp.sum(dm16, axis=0)[None, :]
    dm_cols = lax.dynamic_slice(dm16, (0, me * ada_cols), (16, ada_cols))
    g_w_ada, ds16 = _ada_bwd(s16, dm_cols, wa2d)
    ds_all = _all_gather_small(ds16[8:16], "gather_dsilu")
    dsilu_cc = _sum_parts(ds_all, "sum_dsilu")[0]
    sg_cc = jax.nn.sigmoid(c_ctx)
    g_c_ctx = dsilu_cc * (sg_cc * (1.0 + c_ctx * (1.0 - sg_cc)))

    g_ret_decay = jnp.sum(t_dlg, axis=-1).reshape(2, HEADS) * jax.nn.sigmoid(-ret_decay[0])
    g_lambda_full = -t_sp * jax.nn.sigmoid(-lam_full)

    def my_cols(full):
        return lax.dynamic_slice(full, (0, me * sc_loc), (full.shape[0], sc_loc))

    small_g = dict(
        c_ctx=g_c_ctx[None], b_ada=g_b_ada, norm1_g=t_n1[4:5], norm2_g=t_n2[2:3], ret_decay=g_ret_decay,
        conv_w=my_cols(t_conv[0:4]), conv_b=t_conv[4:5], lru_wa=t_wa.reshape(-1, LRU_BD), lru_ba=my_cols(t_ba),
        lru_wx=t_wx.reshape(-1, LRU_BD), lru_bx=my_cols(t_bx), lru_lambda=my_cols(g_lambda_full), final_g=t_f[0:1])
    small = dict(
        c_ctx=(c_ctx, m_c_ctx, v_c_ctx), b_ada=(b_ada, m_b_ada, v_b_ada), norm1_g=(norm1_g, m_norm1_g, v_norm1_g),
        norm2_g=(norm2_g, m_norm2_g, v_norm2_g), ret_decay=(ret_decay, m_ret_decay, v_ret_decay),
        conv_w=(conv_w, m_conv_w, v_conv_w), conv_b=(conv_b, m_conv_b, v_conv_b), lru_wa=(lru_wa, m_lru_wa, v_lru_wa),
        lru_ba=(lru_ba, m_lru_ba, v_lru_ba), lru_wx=(lru_wx, m_lru_wx, v_lru_wx), lru_bx=(lru_bx, m_lru_bx, v_lru_bx),
        lru_lambda=(lru_lambda, m_lru_lambda, v_lru_lambda), final_g=(final_g, m_final_g, v_final_g))
    names = list(small)
    items = [(small_g[k],) + tuple(a.reshape(small_g[k].shape) for a in small[k]) for k in names]
    res = {}
    for k, it, (d_, m_, v_) in zip(names, items, _adam_many(items, "adam_small")):
        shape = small[k][0].shape
        res[k] = tuple(a.reshape(shape) for a in (it[0], d_, m_, v_))

    def big(parts, w, m, v, name):
        out = _sum_adam(parts, w[0], m[0], v[0], name)
        return tuple(a[None] for a in out)

    res["w_ada"] = big([g_w_ada[None]], w_ada, m_w_ada, v_w_ada, "adam_w_ada")
    res["w_in"] = big([gwi_all], w_in, m_w_in, v_w_in, "adam_w_in")
    res["w_out"] = big([gwo_all], w_out, m_w_out, v_w_out, "adam_w_out")
    res["w_mlp1"] = big(gw1_all, w_mlp1, m_w_mlp1, v_w_mlp1, "adam_w_mlp1")
    res["w_mlp2"] = big(gw2_all, w_mlp2, m_w_mlp2, v_w_mlp2, "adam_w_mlp2")

    order = ["c_ctx", "w_ada", "b_ada", "norm1_g", "norm2_g", "w_in", "ret_decay", "conv_w", "conv_b", "lru_wa", "lru_ba",
             "lru_wx", "lru_bx", "lru_lambda", "w_out", "w_mlp1", "w_mlp2", "final_g"]
    outs = [loss, grad_x[None]]
    for j in range(4):
        outs += [res[k][j] for k in order]
    return tuple(outs)
```

```python
import jax
import jax.numpy as jnp
from jax import lax
from jax.experimental import pallas as pl
from jax.experimental.pallas import tpu as pltpu

F32 = jnp.float32
BF16 = jnp.bfloat16
AXES = ("x", "y", "c")
N_DEV = 8
MESH = pl.DeviceIdType.MESH

HEADS = 4
HEAD_DIM = 128
CHUNK = 128
RET_W = HEADS * HEAD_DIM
LRU_W = 512
LRU_BLOCKS = 8
LRU_BD = LRU_W // LRU_BLOCKS
LRU_C = 8.0
EPS = 1e-6
K_SCALE = HEAD_DIM ** -0.5
ROPE_BASE = 10000.0
GRID_W = 64
TM = 256
TL = 512
SUB = 8

ADAM_LR = 0.001
ADAM_B1 = 0.9
ADAM_B2 = 0.999
ADAM_EPS = 1e-08
ADAM_WD = 0.01
ADAM_STEP = 10

COL_G, COL_XR, COL_GATE = 0, 1, 2

R_CSH1, R_CSC1, R_SH1, R_SC1, R_G1, R_SH2, R_SC2, R_G2 = range(8)


def _pcall(body, **kw):
    return pl.pallas_call(body, **kw)


def _cp(vmem_mb=48):
    return pltpu.CompilerParams(vmem_limit_bytes=vmem_mb << 20)


def _sds(shape, dtype):
    return jax.ShapeDtypeStruct(shape, dtype)


def _dot(a, b):
    return jnp.dot(a.astype(BF16), b.astype(BF16), preferred_element_type=F32)


def _dot_nt(a, b):
    return lax.dot_general(a.astype(BF16), b.astype(BF16), (((1,), (1,)), ((), ())), preferred_element_type=F32)


def _dot_tn(a, b):
    return lax.dot_general(a.astype(BF16), b.astype(BF16), (((0,), (0,)), ((), ())), preferred_element_type=F32)


def _sigmoid(x):
    return 0.5 * jnp.tanh(0.5 * x) + 0.5


def _gelu(x):
    return 0.5 * x * (1.0 + jnp.tanh(0.7978845608028654 * (x + 0.044715 * x * x * x)))


def _dgelu(x):
    t = jnp.tanh(0.7978845608028654 * (x + 0.044715 * x * x * x))
    return 0.5 * (1.0 + t) + 0.5 * x * (1.0 - t * t) * 0.7978845608028654 * (1.0 + 3.0 * 0.044715 * x * x)


def _rows_iota(shape):
    return lax.broadcasted_iota(jnp.int32, shape, 0)


def _tile_order(dirn, s, cb, nb):
    if dirn == 0:
        return s
    return jnp.where(s < cb, cb - 1 - s, nb - 1 - (s - cb))


_SEMS = [pltpu.SemaphoreType.DMA((7,)), pltpu.SemaphoreType.DMA((7,)), pltpu.SemaphoreType.DMA(())]
_ANY = pl.BlockSpec(memory_space=pl.ANY)


def _gather_copies(x_ref, out_ref, send_sems, recv_sems, local_sem):
    mx, my, mc = lax.axis_index("x"), lax.axis_index("y"), lax.axis_index("c")
    me, sibling = (mx, my, mc), (mx, my, 1 - mc)
    chips = [(1 - mx, my), (mx, 1 - my), (1 - mx, 1 - my)]

    def slot(px, py, pc):
        return out_ref.at[4 * px + 2 * py + pc]

    def copy(k, block, to, src=None):
        return pltpu.make_async_remote_copy(
            src_ref=slot(*block) if src is None else src, dst_ref=slot(*block),
            send_sem=send_sems.at[k], recv_sem=recv_sems.at[k], device_id=to, device_id_type=MESH)

    mine = pltpu.make_async_copy(x_ref, slot(*me), local_sem)
    first = [copy(0, me, sibling, src=x_ref)] + [copy(1 + j, me, (*chip, mc), src=x_ref) for j, chip in enumerate(chips)]
    passed = [copy(4 + j, (*chip, mc), sibling) for j, chip in enumerate(chips)]
    recv_ici = [copy(1 + j, (*chip, mc), me) for j, chip in enumerate(chips)]
    recv_d2d = [copy(0, sibling, me)] + [copy(4 + j, (*chip, 1 - mc), me) for j, chip in enumerate(chips)]
    return mine, first, passed, recv_ici, recv_d2d


def _gather_start(*refs):
    mine, first, _, _, _ = _gather_copies(*refs)
    mine.start()
    for cp in first:
        cp.start()


def _gather_finish(*refs):
    mine, first, passed, recv_ici, recv_d2d = _gather_copies(*refs)
    for landed, onward in zip(recv_ici, passed):
        landed.wait_recv()
        onward.start()
    for landed in recv_d2d:
        landed.wait_recv()
    for cp in first + passed:
        cp.wait_send()
    mine.wait()


def _a2a_copies(g_ref, out_ref, send_sems, recv_sems, local_sem):
    mx, my, mc = lax.axis_index("x"), lax.axis_index("y"), lax.axis_index("c")
    me = 4 * mx + 2 * my + mc
    mine = pltpu.make_async_copy(g_ref.at[me], out_ref.at[me], local_sem)
    copies = []
    for k in range(1, N_DEV):
        px = 1 - mx if (k >> 2) & 1 else mx
        py = 1 - my if (k >> 1) & 1 else my
        pc = 1 - mc if k & 1 else mc
        copies.append(pltpu.make_async_remote_copy(
            src_ref=g_ref.at[4 * px + 2 * py + pc], dst_ref=out_ref.at[me],
            send_sem=send_sems.at[k - 1], recv_sem=recv_sems.at[k - 1],
            device_id=(px, py, pc), device_id_type=MESH))
    return mine, copies


def _a2a_start(*refs):
    mine, copies = _a2a_copies(*refs)
    mine.start()
    for cp in copies:
        cp.start()


def _a2a_finish(*refs):
    mine, copies = _a2a_copies(*refs)
    for cp in copies:
        cp.wait()
    mine.wait()


_EXCHANGES = {"gather": (_gather_start, _gather_finish), "a2a": (_a2a_start, _a2a_finish)}


def _exchange_shape(kind, src):
    return _sds((N_DEV,) + src.shape if kind == "gather" else src.shape, src.dtype)


def _all_gather_small(x, name):
    def body(x_ref, out_ref, send_sems, recv_sems, local_sem):
        mx, my, mc = lax.axis_index("x"), lax.axis_index("y"), lax.axis_index("c")
        me = 4 * mx + 2 * my + mc
        mine = pltpu.make_async_copy(x_ref, out_ref.at[me], local_sem)
        mine.start()
        copies = []
        for k in range(1, N_DEV):
            peer = (1 - mx if (k >> 2) & 1 else mx, 1 - my if (k >> 1) & 1 else my, 1 - mc if k & 1 else mc)
            copies.append(pltpu.make_async_remote_copy(
                src_ref=x_ref, dst_ref=out_ref.at[me], send_sem=send_sems.at[k - 1], recv_sem=recv_sems.at[k - 1],
                device_id=peer, device_id_type=MESH))
            copies[-1].start()
        for cp in copies:
            cp.wait()
        mine.wait()

    return _pcall(body, name=name, out_shape=_exchange_shape("gather", x), in_specs=[_ANY], out_specs=_ANY,
                  scratch_shapes=list(_SEMS))(x)


def _pcall_ride(body, ride, args, *, name, grid, in_specs, out_specs, out_shape, scratch_shapes=(), compiler_params=None):
    if ride is None:
        out = _pcall(body, name=name, grid=grid, in_specs=in_specs, out_specs=out_specs, out_shape=out_shape,
                     scratch_shapes=list(scratch_shapes), compiler_params=compiler_params)(*args)
        return out, None
    kind, src = ride
    start, finish = _EXCHANGES[kind]
    single = not isinstance(out_shape, (tuple, list))
    out_specs_t = (out_specs,) if single else tuple(out_specs)
    out_shape_t = (out_shape,) if single else tuple(out_shape)
    n_in, n_out, n_sc = len(in_specs), len(out_shape_t), len(scratch_shapes)

    def wrapped(*refs):
        ins, src_ref = refs[:n_in], refs[n_in]
        outs, dst_ref = refs[n_in + 1:n_in + 1 + n_out], refs[n_in + 1 + n_out]
        scratch = refs[n_in + 2 + n_out:n_in + 2 + n_out + n_sc]
        sems = refs[n_in + 2 + n_out + n_sc:]
        first = pl.program_id(0) == 0
        last = pl.program_id(0) == grid[0] - 1
        for ax in range(1, len(grid)):
            first = jnp.logical_and(first, pl.program_id(ax) == 0)
            last = jnp.logical_and(last, pl.program_id(ax) == grid[ax] - 1)

        @pl.when(first)
        def _():
            start(src_ref, dst_ref, *sems)

        body(*ins, *outs, *scratch)

        @pl.when(last)
        def _():
            finish(src_ref, dst_ref, *sems)

    res = _pcall(wrapped, name=name, grid=grid, in_specs=list(in_specs) + [_ANY], out_specs=out_specs_t + (_ANY,),
                 out_shape=out_shape_t + (_exchange_shape(kind, src),),
                 scratch_shapes=list(scratch_shapes) + list(_SEMS), compiler_params=compiler_params)(*args, src)
    return (res[0] if single else tuple(res[:-1])), res[-1]


class _Epilogue:
    def __init__(self, fn, args, in_specs, out_specs, out_shape, steps=None, lhs_map=None, delay=0):
        self.fn, self.args, self.in_specs, self.out_specs, self.out_shape = fn, tuple(args), list(in_specs), out_specs, out_shape
        self.steps, self.lhs_map, self.delay = steps, lhs_map, delay


def _mm_nn(a, w, out_dtype, name, square_lhs=False, relu_out=False, ride=None, tm=TM, epi=None, vmem_mb=48):
    m, k = a.shape
    parts = tuple(w) if isinstance(w, (tuple, list)) else (w,)
    nb, widths = parts[0].shape[0], [pt.shape[2] for pt in parts]
    bn, npt = sum(widths), len(parts)
    tm = min(tm, m)
    delay = 0 if epi is None else epi.delay
    steps = m // tm

    def body(*refs):
        a_ref, w_refs, rest = refs[0], refs[1:1 + npt], refs[1 + npt:]
        av = a_ref[...]
        if square_lhs:
            av = av * av
        if delay:
            assert nb == 1 and npt == 1
            held = rest[-1]

            @pl.when(pl.program_id(0) == 0)
            def _():
                held[...] = jnp.zeros_like(held)

            fresh = jnp.dot(av, w_refs[0][0], preferred_element_type=F32)
            epi.fn(held[...], slice(0, tm), *rest[:-1])
            held[...] = fresh
            return
        if epi is not None:
            assert nb == 1 and npt == 1
            epi.fn(jnp.dot(av, w_refs[0][0], preferred_element_type=F32), slice(0, tm), *rest)
            return
        for j in range(nb):
            off = j * bn
            for w_ref, width in zip(w_refs, widths):
                r = jnp.dot(av, w_ref[j], preferred_element_type=F32)
                if relu_out:
                    r = jnp.maximum(r, 0.0)
                rest[0][:, off:off + width] = r.astype(out_dtype)
                off += width

    lhs_map = (lambda i: (i, 0)) if not delay else (lambda i: (jnp.minimum(i, steps - 1), 0))
    in_specs = [pl.BlockSpec((tm, k), lhs_map)] + [pl.BlockSpec((nb, k, width), lambda i: (0, 0, 0)) for width in widths]
    scratch = []
    if epi is None:
        args, out_specs, out_shape = (a,) + parts, pl.BlockSpec((tm, nb * bn), lambda i: (i, 0)), _sds((m, nb * bn), out_dtype)
    else:
        args, out_specs, out_shape = (a,) + parts + epi.args, epi.out_specs, epi.out_shape
        in_specs += epi.in_specs
        if delay:
            scratch.append(pltpu.VMEM((tm, bn), F32))
    out, ex = _pcall_ride(body, ride, args, name=name, grid=(steps + delay,), in_specs=in_specs, out_specs=out_specs,
                          out_shape=out_shape, scratch_shapes=scratch, compiler_params=_cp(vmem_mb))
    return out if ride is None else (out, ex)


def _mm_nt(dy, w, out_dtype, name, relu_mul=None, ride=None, tm=TM, epi=None, vmem_mb=48):
    m = dy.shape[0]
    parts = tuple(w) if isinstance(w, (tuple, list)) else (w,)
    nb, k, widths = parts[0].shape[0], parts[0].shape[1], [pt.shape[2] for pt in parts]
    bn, npt = sum(widths), len(parts)
    tm = min(tm, m)
    n_extra = (0 if relu_mul is None else 1) + (0 if epi is None else len(epi.args))

    delay = 0 if epi is None else epi.delay

    def body(*refs):
        dy_ref, w_refs = refs[0], refs[1:1 + npt]
        scratch = refs[-1 - delay:]
        extra, outs, wt = refs[1 + npt:1 + npt + n_extra], refs[1 + npt + n_extra:-1 - delay], scratch[0]

        @pl.when(pl.program_id(0) == 0)
        def _():
            for j in range(nb):
                off = j * bn
                for w_ref, width in zip(w_refs, widths):
                    wt[off:off + width, :] = w_ref[j].T
                    off += width
            if delay:
                scratch[1][...] = jnp.zeros_like(scratch[1])

        if delay:
            held = scratch[1]
            fresh = jnp.dot(dy_ref[...], wt[...], preferred_element_type=F32)
            epi.fn(held[...], slice(0, tm), *extra, *outs)
            held[...] = fresh
            return
        acc = jnp.dot(dy_ref[...], wt[...], preferred_element_type=F32)
        if epi is not None:
            epi.fn(acc, slice(0, tm), *extra, *outs)
            return
        if relu_mul is not None:
            acc = acc * (2.0 * extra[0][...].astype(F32))
        outs[0][...] = acc.astype(out_dtype)

    steps = m // tm if epi is None or epi.steps is None else epi.steps
    lhs_tile = (lambda i: (i, 0)) if epi is None or epi.lhs_map is None else epi.lhs_map
    lhs_map = lhs_tile if not delay else (lambda i: lhs_tile(jnp.minimum(i, steps - 1)))
    in_specs = [pl.BlockSpec((tm, nb * bn), lhs_map)] + [pl.BlockSpec((nb, k, width), lambda i: (0, 0, 0)) for width in widths]
    args = [dy] + list(parts)
    if relu_mul is not None:
        in_specs.append(pl.BlockSpec((tm, k), lambda i: (i, 0)))
        args.append(relu_mul)
    scratch = [pltpu.VMEM((nb * bn, k), BF16)]
    if epi is None:
        out_specs, out_shape = pl.BlockSpec((tm, k), lambda i: (i, 0)), _sds((m, k), out_dtype)
    else:
        args += list(epi.args)
        in_specs += epi.in_specs
        out_specs, out_shape = epi.out_specs, epi.out_shape
        if delay:
            scratch.append(pltpu.VMEM((tm, k), F32))
    out, ex = _pcall_ride(
        body, ride, args, name=name, grid=(steps + delay,), in_specs=in_specs, out_specs=out_specs, out_shape=out_shape,
        scratch_shapes=scratch, compiler_params=_cp(vmem_mb))
    return out if ride is None else (out, ex)


def _mm_tn(a, b, name, col_blocks, block, out_dtype, tm, square_lhs=False):
    m, k = a.shape
    nn = b.shape[1]
    steps = m // tm
    if col_blocks:
        nblk, acc_shape = nn // block, (k, block)
        a_spec = pl.BlockSpec((tm, k), lambda j, s: (s, 0))
        b_spec = pl.BlockSpec((tm, block), lambda j, s: (s, j))
    else:
        nblk, acc_shape = k // block, (block, nn)
        a_spec = pl.BlockSpec((tm, block), lambda j, s: (s, j))
        b_spec = pl.BlockSpec((tm, nn), lambda j, s: (s, 0))

    def body(a_ref, b_ref, o_ref, acc):
        s = pl.program_id(1)

        @pl.when(s == 0)
        def _():
            acc[...] = jnp.zeros_like(acc)

        av = a_ref[...]
        if square_lhs:
            av = av.astype(F32)
            av = (av * av).astype(BF16)
        acc[...] += _dot_tn(av, b_ref[...])

        @pl.when(s == steps - 1)
        def _():
            o_ref[...] = acc[...].astype(out_dtype)

    return _pcall(
        body, name=name, grid=(nblk, steps), in_specs=[a_spec, b_spec],
        out_specs=pl.BlockSpec((None,) + acc_shape, lambda j, s: (j, 0, 0)),
        out_shape=_sds((nblk,) + acc_shape, out_dtype),
        scratch_shapes=[pltpu.VMEM(acc_shape, F32)], compiler_params=_cp(),
    )(a, b)


def _mm_wgrad(at, b, name, bn, out_dtype, transpose_out=False, square_rhs=False, halves=False, ride=None):
    k, m = at.shape
    nblk = b.shape[1] // bn
    rows, cols = (bn, k) if transpose_out else (k, bn)
    nout = 2 if halves else 1
    per = rows // nout

    def body(a_ref, b_ref, *o_refs):
        bv = b_ref[...]
        if square_rhs:
            bv = bv * bv
        r = jnp.dot(a_ref[...], bv, preferred_element_type=F32)
        r = (r.T if transpose_out else r).astype(out_dtype)
        for i, o_ref in enumerate(o_refs):
            o_ref[...] = r[i * per:(i + 1) * per, :]

    out, ex = _pcall_ride(
        body, ride, (at, b), name=name, grid=(nblk,),
        in_specs=[pl.BlockSpec((k, m), lambda j: (0, 0)), pl.BlockSpec((m, bn), lambda j: (0, j))],
        out_specs=tuple(pl.BlockSpec((None, per, cols), lambda j: (j, 0, 0)) for _ in range(nout)),
        out_shape=tuple(_sds((nblk, per, cols), out_dtype) for _ in range(nout)), compiler_params=_cp())
    out = out if halves else out[0]
    return out if ride is None else (out, ex)


def _mm_in(hn, w, cosf, sins, ride):
    m, k = hn.shape
    nb, _, bn = w.shape

    def body(a_ref, w_ref, c_ref, s_ref, qkv_ref, rest_ref, pt):
        av = a_ref[...]
        for j in range(nb):
            pt[:, j * bn:(j + 1) * bn] = jnp.dot(av, w_ref[j], preferred_element_type=F32)
        cf, ss = c_ref[...], s_ref[...]
        for h in range(HEADS):
            sq = slice(h * HEAD_DIM, (h + 1) * HEAD_DIM)
            sk = slice(RET_W + h * HEAD_DIM, RET_W + (h + 1) * HEAD_DIM)
            qkv_ref[:, sq] = _rot(pt[:, sq], cf, ss).astype(BF16)
            qkv_ref[:, sk] = (_rot(pt[:, sk], cf, ss) * K_SCALE).astype(BF16)
        qkv_ref[:, 2 * RET_W:] = pt[:, 2 * RET_W:3 * RET_W].astype(BF16)
        rest_ref[...] = pt[:, 3 * RET_W:]

    tab = pl.BlockSpec((TM, HEAD_DIM), lambda i: (i, 0))
    wide = pl.BlockSpec((TM, 3 * RET_W), lambda i: (i, 0))
    return _pcall_ride(
        body, ride, (hn, w, cosf, sins), name="mm_in", grid=(m // TM,),
        in_specs=[pl.BlockSpec((TM, k), lambda i: (i, 0)), pl.BlockSpec((nb, k, bn), lambda i: (0, 0, 0)), tab, tab],
        out_specs=(wide, wide), out_shape=(_sds((m, 3 * RET_W), BF16), _sds((m, nb * bn - 3 * RET_W), F32)),
        scratch_shapes=[pltpu.VMEM((TM, nb * bn), F32)], compiler_params=_cp())


def _mod_part(s16, w_ada, b_cols):
    def body(s_ref, w_ref, b_ref, o_ref):
        o_ref[...] = _dot(s_ref[...], w_ref[...]) + b_ref[...]

    return _pcall(body, name="mod_part", out_shape=_sds((s16.shape[0], w_ada.shape[1]), F32),
                  compiler_params=_cp())(s16, w_ada, b_cols)


def _ada_bwd(s16, dm_cols, w_ada):
    def body(s_ref, d_ref, w_ref, gw_ref, ds_ref):
        gw_ref[...] = _dot_tn(s_ref[...], d_ref[...])
        ds_ref[...] = _dot_nt(d_ref[...], w_ref[...])

    return _pcall(body, name="ada_bwd",
                  out_shape=(_sds(w_ada.shape, F32), _sds(s16.shape, F32)), compiler_params=_cp())(s16, dm_cols, w_ada)


def _norm1_fwd(ctx, x, g, modrows, cb, ride=None):
    l_len, d = ctx.shape
    nb = (l_len + x.shape[0]) // TM

    def body(ctx_ref, x_ref, g_ref, m_ref, o_ref, ot_ref):
        is_ctx = pl.program_id(0) < cb
        xin = jnp.where(is_ctx, ctx_ref[...], x_ref[...])
        sh = jnp.where(is_ctx, m_ref[R_CSH1:R_CSH1 + 1, :], m_ref[R_SH1:R_SH1 + 1, :])
        sc = jnp.where(is_ctx, m_ref[R_CSC1:R_CSC1 + 1, :], m_ref[R_SC1:R_SC1 + 1, :])
        ms = jnp.mean(xin * xin, axis=-1, keepdims=True)
        n = xin * lax.rsqrt(ms + EPS) * g_ref[...]
        hn = n * (1.0 + sc) + sh
        o_ref[...] = hn.astype(BF16)
        ot_ref[...] = hn.T.astype(BF16)

    return _pcall_ride(
        body, ride, (ctx, x, g, modrows), name="norm1_fwd", grid=(nb,),
        in_specs=[pl.BlockSpec((TM, d), lambda i: (jnp.minimum(i, cb - 1), 0)),
                  pl.BlockSpec((TM, d), lambda i: (jnp.maximum(i - cb, 0), 0)),
                  pl.BlockSpec((1, d), lambda i: (0, 0)), pl.BlockSpec((8, d), lambda i: (0, 0))],
        out_specs=(pl.BlockSpec((TM, d), lambda i: (i, 0)), pl.BlockSpec((d, TM), lambda i: (0, i))),
        out_shape=(_sds((nb * TM, d), BF16), _sds((d, nb * TM), BF16)), compiler_params=_cp())


def _mix_fwd(o_f, o_b, p, h_f, h_b, cb, t_len):
    def body(of_ref, ob_ref, g_ref, gate_ref, hf_ref, hb_ref, mix_ref):
        o = of_ref[...] + ob_ref[...]
        g = g_ref[...]
        sg = g * _sigmoid(g)
        for hh in range(HEADS):
            sl = slice(hh * HEAD_DIM, (hh + 1) * HEAD_DIM)
            oh = o[:, sl]
            yc = oh - jnp.mean(oh, axis=-1, keepdims=True)
            var = jnp.mean(yc * yc, axis=-1, keepdims=True)
            mix_ref[:, sl] = (sg[:, sl] * (yc * lax.rsqrt(var + EPS))).astype(BF16)
        mix_ref[:, RET_W:] = ((hf_ref[...] + hb_ref[...]) * _gelu(gate_ref[...])).astype(BF16)

    row = lambda i: (i + cb, 0)
    return _pcall(
        body, name="mix_fwd", grid=(t_len // TM,),
        in_specs=[pl.BlockSpec((TM, RET_W), row), pl.BlockSpec((TM, RET_W), row),
                  pl.BlockSpec((TM, RET_W), lambda i: (i + cb, COL_G)), pl.BlockSpec((TM, LRU_W), lambda i: (i + cb, COL_GATE)),
                  pl.BlockSpec((TM, LRU_W), row), pl.BlockSpec((TM, LRU_W), row)],
        out_specs=pl.BlockSpec((TM, RET_W + LRU_W), lambda i: (i, 0)),
        out_shape=_sds((t_len, RET_W + LRU_W), BF16), compiler_params=_cp(),
    )(o_f, o_b, p, p, h_f, h_b)


def _res_norm2_epilogue(x, g, modrows, tm):
    t_len, d = x.shape
    tm = min(tm, t_len)

    def fn(y, rows, x_ref, g_ref, m_ref, y_ref, x1_ref, h2_ref, h2t_ref):
        y_ref[rows, :] = y
        x1 = x_ref[rows, :] + m_ref[R_G1:R_G1 + 1, :] * y
        ms = jnp.mean(x1 * x1, axis=-1, keepdims=True)
        n = x1 * lax.rsqrt(ms + EPS) * g_ref[...]
        x1_ref[rows, :] = x1
        h2 = n * (1.0 + m_ref[R_SC2:R_SC2 + 1, :]) + m_ref[R_SH2:R_SH2 + 1, :]
        h2_ref[rows, :] = h2.astype(BF16)
        h2t_ref[:, rows] = h2.T.astype(BF16)

    t = pl.BlockSpec((tm, d), lambda i: (i, 0))
    return _Epilogue(
        fn, (x, g, modrows),
        in_specs=[t, pl.BlockSpec((1, d), lambda i: (0, 0)), pl.BlockSpec((8, d), lambda i: (0, 0))],
        out_specs=(t, t, t, pl.BlockSpec((d, tm), lambda i: (0, i))),
        out_shape=(_sds((t_len, d), F32), _sds((t_len, d), F32), _sds((t_len, d), BF16), _sds((d, t_len), BF16)))


def _tile(i):
    return jnp.maximum(i - 1, 0)


def _zero_at_start(acc_ref, rows):
    @pl.when(pl.program_id(0) == 0)
    def _():
        acc_ref[...] = jnp.zeros_like(acc_ref)


def _final_epilogue(x1, target, fg, modrows, tm):
    t_len, d = x1.shape

    def fn(z, rows, x1_ref, t_ref, fg_ref, m_ref, dx2_ref, dz_ref, dzt_ref, acc_ref):
        _zero_at_start(acc_ref, rows)
        g2 = m_ref[R_G2:R_G2 + 1, :]
        x2 = x1_ref[rows, :] + g2 * z
        rstd = lax.rsqrt(jnp.mean(x2 * x2, axis=-1, keepdims=True) + EPS)
        xh = x2 * rstd
        fg = fg_ref[...]
        e = xh * fg - t_ref[rows, :]
        dy = e * (1.0 / d)
        dxh = dy * fg
        dx2 = rstd * (dxh - xh * jnp.mean(dxh * xh, axis=-1, keepdims=True))
        dx2_ref[rows, :] = dx2
        dz = g2 * dx2
        dz_ref[rows, :] = dz.astype(BF16)
        dzt_ref[:, rows] = dz.T.astype(BF16)
        acc_ref[0:1, :] += jnp.sum(dy * xh, axis=0, keepdims=True)
        acc_ref[1:2, :] += jnp.sum(dx2 * z, axis=0, keepdims=True)
        acc_ref[2:3, :] += jnp.sum(e * e, axis=0, keepdims=True)

    t = pl.BlockSpec((tm, d), lambda i: (i, 0))
    return _Epilogue(
        fn, (x1, target, fg, modrows),
        in_specs=[t, t, pl.BlockSpec((1, d), lambda i: (0, 0)), pl.BlockSpec((8, d), lambda i: (0, 0))],
        out_specs=(t, t, pl.BlockSpec((d, tm), lambda i: (0, i)), pl.BlockSpec((8, d), lambda i: (0, 0))),
        out_shape=(_sds((t_len, d), F32), _sds((t_len, d), BF16), _sds((d, t_len), BF16), _sds((8, d), F32)))


def _bwd_norm2_epilogue(x1, dx2, y, g, modrows, tm):
    t_len, d = x1.shape

    def fn(dh2, rows, x1_ref, dx2_ref, y_ref, g_ref, m_ref, dx1_ref, dy_ref, acc_ref):
        _zero_at_start(acc_ref, rows)
        x1 = x1_ref[rows, :]
        rstd = lax.rsqrt(jnp.mean(x1 * x1, axis=-1, keepdims=True) + EPS)
        xh = x1 * rstd
        gn = g_ref[...]
        dn = dh2 * (1.0 + m_ref[R_SC2:R_SC2 + 1, :])
        dxh = dn * gn
        dx1 = dx2_ref[rows, :] + rstd * (dxh - xh * jnp.mean(dxh * xh, axis=-1, keepdims=True))
        dx1_ref[rows, :] = dx1
        dy_ref[rows, :] = (m_ref[R_G1:R_G1 + 1, :] * dx1).astype(BF16)
        acc_ref[0:1, :] += jnp.sum(dh2, axis=0, keepdims=True)
        acc_ref[1:2, :] += jnp.sum(dh2 * xh * gn, axis=0, keepdims=True)
        acc_ref[2:3, :] += jnp.sum(dn * xh, axis=0, keepdims=True)
        acc_ref[3:4, :] += jnp.sum(dx1 * y_ref[rows, :], axis=0, keepdims=True)

    t = pl.BlockSpec((tm, d), lambda i: (i, 0))
    return _Epilogue(
        fn, (x1, dx2, y, g, modrows),
        in_specs=[t, t, t, pl.BlockSpec((1, d), lambda i: (0, 0)), pl.BlockSpec((8, d), lambda i: (0, 0))],
        out_specs=(t, t, pl.BlockSpec((8, d), lambda i: (0, 0))),
        out_shape=(_sds((t_len, d), F32), _sds((t_len, d), BF16), _sds((8, d), F32)))


def _mix_bwd_epilogue(o_f, o_b, p, h_f, h_b, cb):
    n = o_f.shape[0]
    tile = _tile

    def fn(dm, rows, of_ref, ob_ref, g_ref, gate_ref, hf_ref, hb_ref, do_ref, dg_ref, dgate_ref, dhs_ref):
        keep = jnp.where(pl.program_id(0) - 1 < cb, 0.0, 1.0)
        dm = dm * keep
        o = of_ref[rows, :] + ob_ref[rows, :]
        g = g_ref[rows, :]
        s = _sigmoid(g)
        sg = g * s
        dsg = s * (1.0 + g * (1.0 - s))
        for hh in range(HEADS):
            sl = slice(hh * HEAD_DIM, (hh + 1) * HEAD_DIM)
            oh = o[:, sl]
            yc = oh - jnp.mean(oh, axis=-1, keepdims=True)
            rs = lax.rsqrt(jnp.mean(yc * yc, axis=-1, keepdims=True) + EPS)
            gn = yc * rs
            dret = dm[:, sl]
            dgn = dret * sg[:, sl]
            dg_ref[rows, sl] = (dret * gn * dsg[:, sl]).astype(BF16)
            do_ref[rows, sl] = (rs * (dgn - jnp.mean(dgn, axis=-1, keepdims=True)
                                      - gn * jnp.mean(dgn * gn, axis=-1, keepdims=True))).astype(BF16)
        dlru = dm[:, RET_W:]
        gate = gate_ref[rows, :]
        dhs_ref[rows, :] = dlru * _gelu(gate)
        dgate_ref[rows, :] = (dlru * (hf_ref[rows, :] + hb_ref[rows, :]) * _dgelu(gate)).astype(BF16)

    t = pl.BlockSpec((TM, RET_W), lambda i: (tile(i), 0))
    return _Epilogue(
        fn, (o_f, o_b, p, p, h_f, h_b),
        in_specs=[t, t, pl.BlockSpec((TM, RET_W), lambda i: (tile(i), COL_G)),
                  pl.BlockSpec((TM, LRU_W), lambda i: (tile(i), COL_GATE)), t, t],
        out_specs=(t, t, t, t), out_shape=(_sds((n, RET_W), BF16),) * 3 + (_sds((n, RET_W), F32),),
        steps=n // TM, lhs_map=lambda i: (jnp.maximum(i - cb, 0), 0), delay=1)


def _bwd_norm1_epilogue(ctx, x, dx1, g, modrows, cb):
    t_len, d = x.shape

    def fn(dh, rows, ctx_ref, x_ref, dx1_ref, g_ref, m_ref, gx_ref, acc_ref):
        is_ctx = pl.program_id(0) < cb
        _zero_at_start(acc_ref, rows)
        xin = jnp.where(is_ctx, ctx_ref[rows, :], x_ref[rows, :])
        sc = jnp.where(is_ctx, m_ref[R_CSC1:R_CSC1 + 1, :], m_ref[R_SC1:R_SC1 + 1, :])
        rstd = lax.rsqrt(jnp.mean(xin * xin, axis=-1, keepdims=True) + EPS)
        xh = xin * rstd
        gn = g_ref[...]
        dn = dh * (1.0 + sc)
        dxh = dn * gn
        gx_ref[rows, :] = dx1_ref[rows, :] + rstd * (dxh - xh * jnp.mean(dxh * xh, axis=-1, keepdims=True))
        s0 = jnp.sum(dh, axis=0, keepdims=True)
        s1 = jnp.sum(dh * xh * gn, axis=0, keepdims=True)
        acc_ref[4:5, :] += jnp.sum(dn * xh, axis=0, keepdims=True)
        acc_ref[0:1, :] += jnp.where(is_ctx, s0, 0.0)
        acc_ref[1:2, :] += jnp.where(is_ctx, s1, 0.0)
        acc_ref[2:3, :] += jnp.where(is_ctx, 0.0, s0)
        acc_ref[3:4, :] += jnp.where(is_ctx, 0.0, s1)

    lat = pl.BlockSpec((TM, d), lambda i: (jnp.maximum(i - cb, 0), 0))
    return _Epilogue(
        fn, (ctx, x, dx1, g, modrows),
        in_specs=[pl.BlockSpec((TM, d), lambda i: (jnp.minimum(i, cb - 1), 0)),
                  lat, lat, pl.BlockSpec((1, d), lambda i: (0, 0)), pl.BlockSpec((8, d), lambda i: (0, 0))],
        out_specs=(lat, pl.BlockSpec((8, d), lambda i: (0, 0))),
        out_shape=(_sds((t_len, d), F32), _sds((8, d), F32)))


def _rot(x, cf, ss):
    return x * cf + pltpu.roll(x, HEAD_DIM // 2, 1) * ss


def _decay_exponents(dirn):
    ii = lax.broadcasted_iota(jnp.int32, (CHUNK, CHUNK), 0)
    jj = lax.broadcasted_iota(jnp.int32, (CHUNK, CHUNK), 1)
    rel = ii - jj if dirn == 0 else jj - ii
    pos = ii.astype(F32)
    if dirn == 0:
        cq, cs = pos + 1.0, (CHUNK - 1.0) - pos
    else:
        cq, cs = CHUNK - pos, pos
    return rel, jnp.maximum(rel, 0).astype(F32), cq, cs


def _store_decay(lg_ref, dec, relf_ref=None):
    for dirn in (0, 1):
        rel, relf, cq, cs = _decay_exponents(dirn)
        if relf_ref is not None:
            relf_ref[dirn] = relf
        for h in range(HEADS):
            lgv = lg_ref[dirn, h]
            wq, ws = jnp.exp(lgv * cq), jnp.exp(lgv * cs)
            dec[dirn, h, 0] = jnp.where(rel >= 0, jnp.exp(lgv * relf), 0.0)
            dec[dirn, h, 1] = wq
            dec[dirn, h, 2] = ws
            if relf_ref is not None:
                dec[dirn, h, 3] = wq * cq
                dec[dirn, h, 4] = ws * cs


def _ret_rows(cc, nc, step_of):
    return [lambda s, dirn=dirn: _tile_order(dirn, step_of(s), cc, nc) for dirn in (0, 1)]


def _ret_in_specs(rows):
    specs = []
    for row in rows:
        specs += [pl.BlockSpec((CHUNK, RET_W), lambda s, o=o, row=row: (row(s), o)) for o in (0, 1, 2)]
    return specs


def _ret_fwd(qkv, lg, cc, ride=None):
    n = qkv.shape[0]
    nc = n // CHUNK
    rows = _ret_rows(cc, nc, lambda s: s)

    def body(lg_ref, q0, k0, v0, q1, k1, v1, o0, o1, sp0, sp1, st, dec):
        @pl.when(pl.program_id(0) == 0)
        def _():
            st[...] = jnp.zeros_like(st)
            _store_decay(lg_ref, dec)

        refs = ((q0, k0, v0, o0, sp0), (q1, k1, v1, o1, sp1))
        chains = [(dirn, h, slice(h * HEAD_DIM, (h + 1) * HEAD_DIM)) for dirn in (0, 1) for h in range(HEADS)]
        scores, cross, update = [], [], []
        for dirn, h, sl in chains:
            q_ref, k_ref, v_ref, _, sp_ref = refs[dirn]
            q, k, v = q_ref[:, sl], k_ref[:, sl], v_ref[:, sl]
            sp = st[dirn, h]
            sp_ref[h] = sp
            scores.append(_dot_nt(q, k))
            cross.append(_dot(q * dec[dirn, h, 1], sp))
            update.append(_dot_tn(k * dec[dirn, h, 2], v))
        masked = [(a * dec[dirn, h, 0]).astype(BF16) for a, (dirn, h, _) in zip(scores, chains)]
        intra = [_dot(sc, refs[dirn][2][:, sl]) for sc, (dirn, h, sl) in zip(masked, chains)]
        for (dirn, h, sl), o_in, o_cr, upd in zip(chains, intra, cross, update):
            refs[dirn][3][:, sl] = o_in + o_cr
            st[dirn, h] = jnp.exp(lg_ref[dirn, h] * CHUNK) * st[dirn, h] + upd

    o_specs = [pl.BlockSpec((CHUNK, RET_W), lambda s, row=row: (row(s), 0)) for row in rows]
    state = pl.BlockSpec((None, HEADS, CHUNK, HEAD_DIM), lambda s: (s, 0, 0, 0))
    return _pcall_ride(
        body, ride, (lg,) + (qkv,) * 6, name="ret_fwd", grid=(nc,),
        in_specs=[pl.BlockSpec(memory_space=pltpu.SMEM)] + _ret_in_specs(rows),
        out_specs=(o_specs[0], o_specs[1], state, state),
        out_shape=(_sds((n, RET_W), F32),) * 2 + (_sds((nc, HEADS, CHUNK, HEAD_DIM), F32),) * 2,
        scratch_shapes=[pltpu.VMEM((2, HEADS, CHUNK, HEAD_DIM), F32), pltpu.VMEM((2, HEADS, 3, CHUNK, CHUNK), F32)],
        compiler_params=_cp())


def _ret_bwd(qkv, lg, do, s_prev, cc, ride=None):
    n = qkv.shape[0]
    nc = n // CHUNK
    rows = _ret_rows(cc, nc, lambda s: nc - 1 - s)

    def body(lg_ref, q0, k0, v0, q1, k1, v1, do0, do1, sp0, sp1, dq0, dk0, dv0, dq1, dk1, dv1, dlg_ref, dst, dec, relf):
        @pl.when(pl.program_id(0) == 0)
        def _():
            dst[...] = jnp.zeros_like(dst)
            dlg_ref[...] = jnp.zeros_like(dlg_ref)
            _store_decay(lg_ref, dec, relf)

        refs = ((q0, k0, v0, do0, sp0, dq0, dk0, dv0), (q1, k1, v1, do1, sp1, dq1, dk1, dv1))
        chains = [(dirn, h, slice(h * HEAD_DIM, (h + 1) * HEAD_DIM)) for dirn in (0, 1) for h in range(HEADS)]

        def tiles(dirn, sl):
            q_ref, k_ref, v_ref, do_ref = refs[dirn][:4]
            return q_ref[:, sl], k_ref[:, sl], v_ref[:, sl], do_ref[:, sl]

        a_s, g1_s, da_s, h1_s = [], [], [], []
        for dirn, h, sl in chains:
            q, k, v, dov = tiles(dirn, sl)
            a_s.append(_dot_nt(q, k))
            g1_s.append(_dot_nt(dov, refs[dirn][4][h]))
            da_s.append(_dot_nt(dov, v))
            h1_s.append(_dot_nt(v, dst[dirn, h]))
        da_s = [da * dec[dirn, h, 0] for da, (dirn, h, _) in zip(da_s, chains)]
        dq_s, dk_s, dv_s, ds_s = [], [], [], []
        for (dirn, h, sl), a, da in zip(chains, a_s, da_s):
            q, k, v, dov = tiles(dirn, sl)
            dq_s.append(_dot(da, k))
            dk_s.append(_dot_tn(da, q))
            dv_s.append(_dot_tn(a * dec[dirn, h, 0], dov) + _dot(k * dec[dirn, h, 2], dst[dirn, h]))
            ds_s.append(_dot_tn(q * dec[dirn, h, 1], dov))
        for (dirn, h, sl), a, g1, da, h1, dq2, dk2, dv, ds2 in zip(chains, a_s, g1_s, da_s, h1_s, dq_s, dk_s, dv_s, ds_s):
            q, k, _, _ = tiles(dirn, sl)
            dq_ref, dk_ref, dv_ref = refs[dirn][5:]
            sp, dsn = refs[dirn][4][h], dst[dirn, h]
            gc = jnp.exp(lg_ref[dirn, h] * CHUNK)
            dq_ref[:, sl] = (g1 * dec[dirn, h, 1] + dq2).astype(BF16)
            dk_ref[:, sl] = (dk2 + h1 * dec[dirn, h, 2]).astype(BF16)
            dv_ref[:, sl] = dv.astype(BF16)
            term = da * a * relf[dirn] + q * g1 * dec[dirn, h, 3] + k * h1 * dec[dirn, h, 4] + sp * dsn * (CHUNK * gc)
            dlg_ref[dirn * HEADS + h:dirn * HEADS + h + 1, 0:HEAD_DIM] += jnp.sum(term, axis=0, keepdims=True)
            dst[dirn, h] = gc * dsn + ds2

    wide = [pl.BlockSpec((CHUNK, RET_W), lambda s, row=row: (row(s), 0)) for row in rows]
    state = pl.BlockSpec((None, HEADS, CHUNK, HEAD_DIM), lambda s: (nc - 1 - s, 0, 0, 0))
    return _pcall_ride(
        body, ride, (lg,) + (qkv,) * 6 + (do, do, s_prev[0], s_prev[1]), name="ret_bwd", grid=(nc,),
        in_specs=[pl.BlockSpec(memory_space=pltpu.SMEM)] + _ret_in_specs(rows) + wide + [state, state],
        out_specs=(wide[0],) * 3 + (wide[1],) * 3 + (pl.BlockSpec((2 * HEADS, 8 * HEAD_DIM), lambda s: (0, 0)),),
        out_shape=(_sds((n, RET_W), BF16),) * 6 + (_sds((2 * HEADS, 8 * HEAD_DIM), F32),),
        scratch_shapes=[pltpu.VMEM((2, HEADS, CHUNK, HEAD_DIM), F32), pltpu.VMEM((2, HEADS, 5, CHUNK, CHUNK), F32),
                        pltpu.VMEM((2, CHUNK, CHUNK), F32)],
        compiler_params=_cp())


def _shift_rows(cur, prev8, next8, k, seg_start, seg_end):
    tm = cur.shape[0]
    rows = _rows_iota(cur.shape)
    if k < 0:
        out = pltpu.roll(cur, -k, 0)
        for j in range(-k):
            halo = jnp.where(seg_start, 0.0, prev8[SUB + k + j:SUB + k + j + 1, :])
            out = jnp.where(rows == j, halo, out)
    else:
        out = pltpu.roll(cur, tm - k, 0)
        for j in range(k):
            halo = jnp.where(seg_end, 0.0, next8[j:j + 1, :])
            out = jnp.where(rows == tm - k + j, halo, out)
    return out


def _seg_flags(t, cb, nb):
    return jnp.logical_or(t == 0, t == cb), jnp.logical_or(t == cb - 1, t == nb - 1)


def _halo_specs(tile_of, n_rows, col):
    per = TM // SUB
    return [pl.BlockSpec((TM, LRU_W), lambda s: (tile_of(s), col)),
            pl.BlockSpec((SUB, LRU_W), lambda s: (jnp.maximum(tile_of(s) * per - 1, 0), col)),
            pl.BlockSpec((SUB, LRU_W), lambda s: (jnp.minimum((tile_of(s) + 1) * per, n_rows // SUB - 1), col))]


def _lru_gates(xr, prev8, next8, seg_start, seg_end, cw_ref, cb_ref, wg_ref, bg_ref, sp_ref):
    xm1 = _shift_rows(xr, prev8, next8, -1, seg_start, seg_end)
    xp1 = _shift_rows(xr, prev8, next8, 1, seg_start, seg_end)
    xp2 = _shift_rows(xr, prev8, next8, 2, seg_start, seg_end)
    xc = cb_ref[...] + xm1 * cw_ref[0:1, :] + xr * cw_ref[1:2, :] + xp1 * cw_ref[2:3, :] + xp2 * cw_ref[3:4, :]
    pre = _dot(xc, wg_ref[...]) + bg_ref[...]
    r = _sigmoid(pre[:, :LRU_W])
    i = _sigmoid(pre[:, LRU_W:])
    la = (-LRU_C) * r * sp_ref[...]
    a = jnp.exp(la)
    th = jnp.tanh(la)
    sq = jnp.sqrt(-2.0 * th / (1.0 - th))
    return xc, r, i, a, sq


def _scan_tile(a, b, ascending, a_sc, b_sc, carry, out_ref):
    tm, w = a.shape
    nsub = tm // SUB
    a = a.reshape(nsub, SUB, w)
    b = b.reshape(nsub, SUB, w)
    r8 = lax.broadcasted_iota(jnp.int32, a.shape, 1)
    for k in (1, 2, 4):
        if ascending:
            m = r8 >= k
            a_s, b_s = pltpu.roll(a, k, 1), pltpu.roll(b, k, 1)
        else:
            m = r8 < SUB - k
            a_s, b_s = pltpu.roll(a, SUB - k, 1), pltpu.roll(b, SUB - k, 1)
        b = a * jnp.where(m, b_s, 0.0) + b
        a = a * jnp.where(m, a_s, 1.0)
    a_sc[...] = a.reshape(tm, w)
    b_sc[...] = b.reshape(tm, w)

    def step(j, c):
        off = pl.multiple_of((j if ascending else nsub - 1 - j) * SUB, SUB)
        hb = a_sc[pl.ds(off, SUB), :] * c + b_sc[pl.ds(off, SUB), :]
        out_ref[pl.ds(off, SUB), :] = hb
        last = hb[SUB - 1:SUB, :] if ascending else hb[0:1, :]
        return jnp.broadcast_to(last, c.shape)

    carry[...] = lax.fori_loop(0, nsub, step, carry[...], unroll=8)


def _lru_fwd(p, wg, bg, sp, cw, cbias, dirn, cb, ride=None):
    n = p.shape[0]
    nb = n // TM
    tile_of = lambda s: _tile_order(dirn, s, cb, nb)

    def body(x_ref, xp_ref, xn_ref, wg_ref, bg_ref, sp_ref, cw_ref, cb_ref, h_ref, cin_ref, carry, a_sc, b_sc):
        s = pl.program_id(0)

        @pl.when(s == 0)
        def _():
            carry[...] = jnp.zeros_like(carry)

        seg_start, seg_end = _seg_flags(tile_of(s), cb, nb)
        xc, r, i, a, sq = _lru_gates(x_ref[...], xp_ref[...], xn_ref[...], seg_start, seg_end,
                                     cw_ref, cb_ref, wg_ref, bg_ref, sp_ref)
        cin_ref[...] = carry[...]
        _scan_tile(a, sq * (i * xc), dirn == 0, a_sc, b_sc, carry, h_ref)

    full = lambda shape: pl.BlockSpec(shape, lambda s: (0,) * len(shape))
    return _pcall_ride(
        body, ride, (p, p, p, wg, bg, sp, cw, cbias), name=f"lru_fwd{dirn}", grid=(nb,),
        in_specs=_halo_specs(tile_of, n, COL_XR) + [full((LRU_W, 2 * LRU_W)), full((1, 2 * LRU_W)), full((1, LRU_W)),
                                               full((4, LRU_W)), full((1, LRU_W))],
        out_specs=(pl.BlockSpec((TM, LRU_W), lambda s: (tile_of(s), 0)),
                   pl.BlockSpec((None, SUB, LRU_W), lambda s: (tile_of(s), 0, 0))),
        out_shape=(_sds((n, LRU_W), F32), _sds((nb, SUB, LRU_W), F32)),
        scratch_shapes=[pltpu.VMEM((SUB, LRU_W), F32), pltpu.VMEM((TM, LRU_W), F32), pltpu.VMEM((TM, LRU_W), F32)],
        compiler_params=_cp())


def _lru_bwd(p, wg, bg, sp, cw, cbias, h, cin, dhs, dirn, cb, ride=None):
    n = p.shape[0]
    nb = n // TM
    tile_of = lambda s: _tile_order(dirn, nb - 1 - s, cb, nb)

    def body(x_ref, xp_ref, xn_ref, wg_ref, bg_ref, sp_ref, cw_ref, cb_ref, h_ref, cin_ref, dhs_ref,
             dxc_ref, dwd_ref, acc_ref, carry, a_sc, b_sc, mu_sc, dwg_ref):
        s = pl.program_id(0)

        @pl.when(s == 0)
        def _():
            carry[...] = jnp.zeros_like(carry)
            dwg_ref[...] = jnp.zeros_like(dwg_ref)
            acc_ref[...] = jnp.zeros_like(acc_ref)

        seg_start, seg_end = _seg_flags(tile_of(s), cb, nb)
        xc, r, i, a, sq = _lru_gates(x_ref[...], xp_ref[...], xn_ref[...], seg_start, seg_end,
                                     cw_ref, cb_ref, wg_ref, bg_ref, sp_ref)
        rows = _rows_iota(a.shape)
        hv = h_ref[...]
        dh = dhs_ref[...]
        mu_next = carry[0:1, :]
        _scan_tile(a, a * dh, dirn == 1, a_sc, b_sc, carry, mu_sc)
        mu = mu_sc[...]
        if dirn == 0:
            hprev = jnp.where(rows == 0, cin_ref[0:1, :], pltpu.roll(hv, 1, 0))
            lam = dh + jnp.where(rows == TM - 1, mu_next, pltpu.roll(mu, TM - 1, 0))
        else:
            hprev = jnp.where(rows == TM - 1, cin_ref[0:1, :], pltpu.roll(hv, TM - 1, 0))
            lam = dh + jnp.where(rows == 0, mu_next, pltpu.roll(mu, 1, 0))
        ds = lam * (i * xc)
        di = lam * (sq * xc)
        dla = lam * hprev * a - ds * (a * a) / jnp.maximum(sq, 1e-20)
        dpr = dla * ((-LRU_C) * sp_ref[...]) * r * (1.0 - r)
        dpi = di * i * (1.0 - i)
        dpre = jnp.concatenate([dpr, dpi], axis=1)
        dxc_ref[...] = lam * (sq * i) + _dot_nt(dpre, wg_ref[...])
        dwg_ref[...] += _dot_tn(xc, dpre)
        acc_ref[0:1, :] += jnp.sum(dpre, axis=0, keepdims=True)
        acc_ref[1:2, 0:LRU_W] += jnp.sum(dla * ((-LRU_C) * r), axis=0, keepdims=True)

        @pl.when(s == nb - 1)
        def _():
            low = lax.broadcasted_iota(jnp.int32, (LRU_BD, 2 * LRU_BD), 1) < LRU_BD
            for half in (0, LRU_W):
                for m in range(LRU_BLOCKS // 2):
                    lanes = slice(half + 2 * LRU_BD * m, half + 2 * LRU_BD * (m + 1))
                    even = dwg_ref[2 * m * LRU_BD:(2 * m + 1) * LRU_BD, lanes]
                    odd = dwg_ref[(2 * m + 1) * LRU_BD:(2 * m + 2) * LRU_BD, lanes]
                    dwd_ref[:, lanes] = jnp.where(low, even, odd)

    full = lambda shape: pl.BlockSpec(shape, lambda s: (0,) * len(shape))
    tile = pl.BlockSpec((TM, LRU_W), lambda s: (tile_of(s), 0))
    return _pcall_ride(
        body, ride, (p, p, p, wg, bg, sp, cw, cbias, h, cin, dhs), name=f"lru_bwd{dirn}", grid=(nb,),
        in_specs=_halo_specs(tile_of, n, COL_XR) + [full((LRU_W, 2 * LRU_W)), full((1, 2 * LRU_W)), full((1, LRU_W)),
                                               full((4, LRU_W)), full((1, LRU_W)), tile,
                                               pl.BlockSpec((None, SUB, LRU_W), lambda s: (tile_of(s), 0, 0)), tile],
        out_specs=(tile, full((LRU_BD, 2 * LRU_W)), full((8, 2 * LRU_W))),
        out_shape=(_sds((n, LRU_W), F32), _sds((LRU_BD, 2 * LRU_W), F32), _sds((8, 2 * LRU_W), F32)),
        scratch_shapes=[pltpu.VMEM((SUB, LRU_W), F32)] + [pltpu.VMEM((TM, LRU_W), F32)] * 3
        + [pltpu.VMEM((LRU_W, 2 * LRU_W), F32)],
        compiler_params=_cp())


def _assemble_dp(dqs, dks, dvs, dg, dgate, dxcs, p, cw, cosf, sins, cb, ride=None):
    n = p.shape[0]
    nb = n // TM
    tile_of = lambda s: s

    def body(dqf, dqb, dkf, dkb, dvf, dvb, dg_ref, dgate_ref, cf, pf, nf, cb_, pb, nb_, x_ref, xp_ref, xn_ref,
             cw_ref, cos_ref, sin_ref, dp_ref, acc_ref):
        s = pl.program_id(0)

        @pl.when(s == 0)
        def _():
            acc_ref[...] = jnp.zeros_like(acc_ref)

        seg_start, seg_end = _seg_flags(s, cb, nb)
        dq = dqf[...].astype(F32) + dqb[...].astype(F32)
        dk = dkf[...].astype(F32) + dkb[...].astype(F32)
        cosv, sinv = cos_ref[...], sin_ref[...]
        for h in range(HEADS):
            sl = slice(h * HEAD_DIM, (h + 1) * HEAD_DIM)
            sk = slice(RET_W + h * HEAD_DIM, RET_W + (h + 1) * HEAD_DIM)
            dp_ref[:, sl] = (dq[:, sl] * cosv + pltpu.roll(dq[:, sl] * sinv, HEAD_DIM // 2, 1)).astype(BF16)
            dp_ref[:, sk] = ((dk[:, sl] * cosv + pltpu.roll(dk[:, sl] * sinv, HEAD_DIM // 2, 1)) * K_SCALE).astype(BF16)
        dp_ref[:, 2 * RET_W:3 * RET_W] = (dvf[...].astype(F32) + dvb[...].astype(F32)).astype(BF16)
        dp_ref[:, 3 * RET_W:4 * RET_W] = dg_ref[...].astype(BF16)
        dxc = cf[...] + cb_[...]
        dprev = pf[...] + pb[...]
        dnext = nf[...] + nb_[...]
        dxr = (_shift_rows(dxc, dprev, dnext, 1, seg_start, seg_end) * cw_ref[0:1, :] + dxc * cw_ref[1:2, :]
               + _shift_rows(dxc, dprev, dnext, -1, seg_start, seg_end) * cw_ref[2:3, :]
               + _shift_rows(dxc, dprev, dnext, -2, seg_start, seg_end) * cw_ref[3:4, :])
        dp_ref[:, 4 * RET_W:4 * RET_W + LRU_W] = dxr.astype(BF16)
        dp_ref[:, 4 * RET_W + LRU_W:] = dgate_ref[...].astype(BF16)
        xr, xp, xn = x_ref[...], xp_ref[...], xn_ref[...]
        for j, k in enumerate((-1, 0, 1, 2)):
            xs = xr if k == 0 else _shift_rows(xr, xp, xn, k, seg_start, seg_end)
            acc_ref[j:j + 1, 0:LRU_W] += jnp.sum(dxc * xs, axis=0, keepdims=True)
        acc_ref[4:5, 0:LRU_W] += jnp.sum(dxc, axis=0, keepdims=True)

    t = pl.BlockSpec((TM, RET_W), lambda s: (s, 0))
    args = (dqs[0], dqs[1], dks[0], dks[1], dvs[0], dvs[1], dg, dgate, dxcs[0], dxcs[0], dxcs[0], dxcs[1], dxcs[1], dxcs[1],
            p, p, p, cw, cosf, sins)
    tab = pl.BlockSpec((TM, HEAD_DIM), lambda s: (s, 0))
    return _pcall_ride(
        body, ride, args, name="assemble_dp", grid=(nb,),
        in_specs=[t] * 8 + _halo_specs(tile_of, n, 0) * 2 + _halo_specs(tile_of, n, COL_XR)
        + [pl.BlockSpec((4, LRU_W), lambda s: (0, 0)), tab, tab],
        out_specs=(pl.BlockSpec((TM, 4 * RET_W + 2 * LRU_W), lambda s: (s, 0)), pl.BlockSpec((8, 2 * LRU_W), lambda s: (0, 0))),
        out_shape=(_sds((n, 4 * RET_W + 2 * LRU_W), BF16), _sds((8, 2 * LRU_W), F32)), compiler_params=_cp())


def _adamw(g, w, m, v):
    nm = ADAM_B1 * m + (1.0 - ADAM_B1) * g
    nv = ADAM_B2 * v + (1.0 - ADAM_B2) * (g * g)
    m_hat = nm / (1.0 - ADAM_B1 ** ADAM_STEP)
    v_hat = nv / (1.0 - ADAM_B2 ** ADAM_STEP)
    return (-ADAM_LR) * (m_hat / (jnp.sqrt(v_hat) + ADAM_EPS) + ADAM_WD * w), nm, nv


def _adam_many(items, name):
    n = len(items)

    def body(*refs):
        for i in range(n):
            g, w, m, v = (r[...] for r in refs[4 * i:4 * i + 4])
            for o_ref, val in zip(refs[4 * n + 3 * i:4 * n + 3 * i + 3], _adamw(g, w, m, v)):
                o_ref[...] = val

    out_shape = tuple(_sds(it[1].shape, F32) for it in items for _ in range(3))
    res = _pcall(body, name=name, out_shape=out_shape, compiler_params=_cp())(*[a for it in items for a in it])
    return [tuple(res[3 * i:3 * i + 3]) for i in range(n)]


def _sum_adam(parts_list, w, m, v, name):
    nparts, _, c = parts_list[0].shape
    r = w.shape[0]
    tr = min(parts_list[0].shape[1], 128)
    starts, o = [], 0
    for pa in parts_list:
        starts.append(o)
        o += pa.shape[1] // tr
    nseg = len(parts_list)

    def body(*refs):
        p_refs = refs[:nseg]
        w_ref, m_ref, v_ref, g_ref, d_ref, nm_ref, nv_ref = refs[nseg:]
        i = pl.program_id(0)
        for s, p_ref in enumerate(p_refs):
            end = starts[s + 1] if s + 1 < nseg else r // tr

            @pl.when(jnp.logical_and(i >= starts[s], i < end))
            def _():
                g = p_ref[0].astype(F32)
                for j in range(1, nparts):
                    g = g + p_ref[j].astype(F32)
                g_ref[...] = g
                d_ref[...], nm_ref[...], nv_ref[...] = _adamw(g, w_ref[...], m_ref[...], v_ref[...])

    def seg_spec(s):
        last = parts_list[s].shape[1] // tr - 1
        return pl.BlockSpec((nparts, tr, c), lambda i: (0, jnp.clip(i - starts[s], 0, last), 0))

    t = pl.BlockSpec((tr, c), lambda i: (i, 0))
    return _pcall(
        body, name=name, grid=(r // tr,),
        in_specs=[seg_spec(s) for s in range(nseg)] + [t, t, t],
        out_specs=(t, t, t, t), out_shape=(_sds((r, c), F32),) * 4, compiler_params=_cp(),
    )(*parts_list, w, m, v)


def _sum_parts(parts, name):
    nparts, r, c = parts.shape

    def body(p_ref, o_ref):
        g = p_ref[0]
        for j in range(1, nparts):
            g = g + p_ref[j]
        o_ref[...] = g

    return _pcall(body, name=name, out_shape=_sds((r, c), parts.dtype), compiler_params=_cp())(parts)


def _rot_tables(l_len, t_len):
    rows = t_len // GRID_W
    n_freq = HEAD_DIM // 4
    inv = ROPE_BASE ** (-jnp.arange(n_freq, dtype=F32) / n_freq)
    ang_r = jnp.arange(rows, dtype=F32)[:, None] * inv
    ang_c = jnp.arange(GRID_W, dtype=F32)[:, None] * inv
    cos = jnp.concatenate([jnp.repeat(jnp.cos(ang_r), GRID_W, axis=0), jnp.tile(jnp.cos(ang_c), (rows, 1))], axis=-1)
    sin = jnp.concatenate([jnp.repeat(jnp.sin(ang_r), GRID_W, axis=0), jnp.tile(jnp.sin(ang_c), (rows, 1))], axis=-1)
    cosf = jnp.concatenate([jnp.ones((l_len, HEAD_DIM), F32), jnp.concatenate([cos, cos], axis=-1)], axis=0)
    sins = jnp.concatenate([jnp.zeros((l_len, HEAD_DIM), F32), jnp.concatenate([-sin, sin], axis=-1)], axis=0)
    return cosf, sins


def _block_diag(w):
    eye = jnp.eye(LRU_BLOCKS, dtype=w.dtype)
    return (w[:, :, None, :] * eye[:, None, :, None]).reshape(LRU_W, LRU_W)


def _blocks_from_lanes(dwd_half):
    return dwd_half.reshape(LRU_BD, LRU_BLOCKS, LRU_BD).transpose(1, 0, 2)


def _silu(x):
    return x * jax.nn.sigmoid(x)


def kernel(x, c, ctx, c_ctx, w_ada, b_ada, norm1_g, norm2_g, w_in, ret_decay, conv_w, conv_b, lru_wa, lru_ba, lru_wx, lru_bx, lru_lambda, w_out, w_mlp1, w_mlp2, final_g, loss_target, m_c_ctx, m_w_ada, m_b_ada, m_norm1_g, m_norm2_g, m_w_in, m_ret_decay, m_conv_w, m_conv_b, m_lru_wa, m_lru_ba, m_lru_wx, m_lru_bx, m_lru_lambda, m_w_out, m_w_mlp1, m_w_mlp2, m_final_g, v_c_ctx, v_w_ada, v_b_ada, v_norm1_g, v_norm2_g, v_w_in, v_ret_decay, v_conv_w, v_conv_b, v_lru_wa, v_lru_ba, v_lru_wx, v_lru_bx, v_lru_lambda, v_w_out, v_w_mlp1, v_w_mlp2, v_final_g):
    t_len, d = x.shape[1], x.shape[2]
    l_len = ctx.shape[1]
    cb, cc = l_len // TM, l_len // CHUNK
    me = 4 * lax.axis_index("x") + 2 * lax.axis_index("y") + lax.axis_index("c")
    x2d, ctx2d, tgt2d = x[0], ctx[0], loss_target[0]
    ada_cols = w_ada.shape[2]
    wa2d = w_ada[0]

    sc_loc = conv_w.shape[2]
    pack_a = jnp.zeros((8, d), F32)
    pack_a = pack_a.at[0].set(_silu(c[0]))
    pack_a = pack_a.at[1, :4 * sc_loc].set(conv_w[0].reshape(-1))
    pack_a = pack_a.at[2, :2 * sc_loc].set(lru_ba[0].reshape(-1))
    pack_a = pack_a.at[3, :2 * sc_loc].set(lru_bx[0].reshape(-1))
    pack_a = pack_a.at[4, :2 * sc_loc].set(lru_lambda[0].reshape(-1))
    all_a = _all_gather_small(pack_a, "gather_small_in")
    s16 = jnp.zeros((16, d), F32).at[0:8].set(all_a[:, 0, :]).at[8].set(_silu(c_ctx))

    def unshard(row, k):
        return all_a[:, row, :k * sc_loc].reshape(N_DEV, k, sc_loc).transpose(1, 0, 2).reshape(k, N_DEV * sc_loc)

    conv_w_full = unshard(1, 4)
    ba_full, bx_full, lam_full = unshard(2, 2), unshard(3, 2), unshard(4, 2)

    b_cols = lax.dynamic_slice(b_ada, (0, me * ada_cols), (1, ada_cols))
    mod_parts = _all_gather_small(_mod_part(s16, wa2d, b_cols), "gather_mod")
    mod_all = mod_parts.transpose(1, 0, 2).reshape(16, N_DEV * ada_cols)
    mod_me = lax.dynamic_slice(mod_all, (me, 0), (1, 6 * d)).reshape(6, d)
    mod_c = mod_all[8].reshape(6, d)
    modrows = jnp.concatenate([mod_c[0:2], mod_me], axis=0)

    lg = jax.nn.log_sigmoid(ret_decay[0])
    sp = jax.nn.softplus(-lam_full)
    wg = [jnp.concatenate([_block_diag(lru_wa[0, dd]), _block_diag(lru_wx[0, dd])], axis=1).astype(BF16) for dd in (0, 1)]
    bg = [jnp.concatenate([ba_full[dd], bx_full[dd]])[None, :] for dd in (0, 1)]
    cosf, sins = _rot_tables(l_len, t_len)

    (hn, hnt), win_g = _norm1_fwd(ctx2d, x2d, norm1_g, modrows, cb, ride=("gather", w_in[0].astype(BF16)))
    (qkv, p), w2_g = _mm_in(hn, win_g, cosf, sins, ride=("gather", w_mlp2[0].astype(BF16)))
    w2_g = w2_g.reshape(1, 4 * d, d)
    w1_cols = w_mlp1.shape[2] // 2
    w1_lo, w1_hi = w_mlp1[0, :, :w1_cols].astype(BF16), w_mlp1[0, :, w1_cols:].astype(BF16)
    (o0, o1, sp0, sp1), w1_lo_g = _ret_fwd(qkv, lg, cc, ride=("gather", w1_lo))
    o, s_prev = [o0, o1], [sp0, sp1]
    (h0, cin0), wout_g = _lru_fwd(p, wg[0], bg[0], sp[0:1], conv_w_full, conv_b, 0, cb, ride=("gather", w_out[0].astype(BF16)))
    wout_g = wout_g.reshape(1, d, d)
    (h1, cin1), w1_hi_g = _lru_fwd(p, wg[1], bg[1], sp[1:2], conv_w_full, conv_b, 1, cb, ride=("gather", w1_hi))
    w1_g = (w1_lo_g, w1_hi_g)
    h, cin = [h0, h1], [cin0, cin1]
    mix = _mix_fwd(o[0], o[1], p, h[0], h[1], cb, t_len)
    y, x1, h2, h2t = _mm_nn(mix, wout_g, F32, "mm_out_norm2", tm=TL, epi=_res_norm2_epilogue(x2d, norm2_g, modrows, TL))
    r = _mm_nn(h2, w1_g, BF16, "mm_mlp1", relu_out=True, tm=TL)
    dx2, dz, dzt, facc = _mm_nn(r, w2_g, F32, "mm_mlp2_final", square_lhs=True, tm=TL, vmem_mb=58,
                                epi=_final_epilogue(x1, tgt2d, final_g[None, :], modrows, TL))

    du = _mm_nt(dz, w2_g, BF16, "mm_da2", relu_mul=r, tm=TL, vmem_mb=58)
    gw2_lo, gw2_hi = _mm_wgrad(dzt, r, "mm_dw2", w2_g.shape[1] // N_DEV, BF16, transpose_out=True, square_rhs=True,
                               halves=True)
    gw1_lo, gw1_hi = _mm_wgrad(h2t, du, "mm_dw1", w_mlp1.shape[2], BF16, halves=True)
    (dx1, dy, n2acc), gw2_lo_all = _mm_nt(du, w1_g, F32, "mm_dh2_norm2", ride=("a2a", gw2_lo), tm=TL, vmem_mb=60,
                                          epi=_bwd_norm2_epilogue(x1, dx2, y, norm2_g, modrows, TL))
    gwo = _mm_tn(mix, dy, "mm_dwout", False, d, BF16, 512).reshape(N_DEV, -1, d)
    (do, dg, dgate, dhs), gwo_all = _mm_nt(dy, wout_g, F32, "mm_dmix_mix", ride=("a2a", gwo),
                                           epi=_mix_bwd_epilogue(o[0], o[1], p, h[0], h[1], cb))
    dxcs, dwgs, laccs, rides, got = [], [], [], [("a2a", gw1_hi), ("a2a", gw2_hi)], []
    (dq0, dk0, dv0, dq1, dk1, dv1, dlg_lanes), gw1_lo_all = _ret_bwd(qkv, lg, do, s_prev, cc, ride=("a2a", gw1_lo))
    dqs, dks, dvs = [dq0, dq1], [dk0, dk1], [dv0, dv1]
    for dd in (0, 1):
        (dxc_, dwg_, lacc_), got_ = _lru_bwd(p, wg[dd], bg[dd], sp[dd:dd + 1], conv_w_full, conv_b, h[dd], cin[dd], dhs, dd, cb,
                                             ride=rides[dd])
        dxcs.append(dxc_); dwgs.append(dwg_); laccs.append(lacc_); got.append(got_)
    gw1_hi_all, gw2_hi_all = got
    gw1_all, gw2_all = [gw1_lo_all, gw1_hi_all], [gw2_lo_all, gw2_hi_all]
    (dp, cacc), _ = _assemble_dp(dqs, dks, dvs, dg, dgate, dxcs, p, conv_w_full, cosf, sins, cb)
    pack_b = jnp.concatenate([n2acc, facc, cacc, laccs[0], laccs[1], dlg_lanes, dwgs[0], dwgs[1]], axis=0)
    gwi, all_b = _mm_wgrad(hnt, dp, "mm_dwin", win_g.shape[2], BF16, ride=("gather", pack_b))
    (grad_x, n1acc), gwi_all = _mm_nt(dp, win_g, F32, "mm_dhn_norm1", ride=("a2a", gwi),
                                      epi=_bwd_norm1_epilogue(ctx2d, x2d, dx1, norm1_g, modrows, cb))
    all_n1 = _all_gather_small(n1acc, "gather_norm1_grads")
    tot = _sum_parts(all_b, "sum_small_grads")
    t_n1 = _sum_parts(all_n1, "sum_norm1_grads")
    t_n2, t_f, t_conv, t_dlg = tot[0:8], tot[8:16], tot[16:24, :LRU_W], tot[40:48, :HEAD_DIM]
    t_l = [tot[24:32], tot[32:40]]
    t_dwd = [tot[48:48 + LRU_BD], tot[48 + LRU_BD:48 + 2 * LRU_BD]]
    loss = (0.5 / d) * jnp.sum(t_f[2])
    t_wa = jnp.stack([_blocks_from_lanes(t_dwd[dd][:, :LRU_W]) for dd in (0, 1)])
    t_wx = jnp.stack([_blocks_from_lanes(t_dwd[dd][:, LRU_W:]) for dd in (0, 1)])
    t_ba = jnp.stack([t_l[dd][0, :LRU_W] for dd in (0, 1)])
    t_bx = jnp.stack([t_l[dd][0, LRU_W:] for dd in (0, 1)])
    t_sp = jnp.stack([t_l[dd][1, :LRU_W] for dd in (0, 1)])
    dm_rows = jnp.stack([all_n1[:, 2, :], all_n1[:, 3, :], all_b[:, 3, :], all_b[:, 0, :], all_b[:, 1, :], all_b[:, 9, :]],
                        axis=1).reshape(N_DEV, 6 * d)
    dm_c = jnp.concatenate([t_n1[0], t_n1[1], jnp.zeros((4 * d,), F32)])
    dm16 = jnp.zeros((16, 6 * d), F32).at[0:8].set(dm_rows).at[8].set(dm_c)
    g_b_ada = jnp.sum(dm16, axis=0)[None, :]
    dm_cols = lax.dynamic_slice(dm16, (0, me * ada_cols), (16, ada_cols))
    g_w_ada, ds16 = _ada_bwd(s16, dm_cols, wa2d)
    ds_all = _all_gather_small(ds16[8:16], "gather_dsilu")
    dsilu_cc = _sum_parts(ds_all, "sum_dsilu")[0]
    sg_cc = jax.nn.sigmoid(c_ctx)
    g_c_ctx = dsilu_cc * (sg_cc * (1.0 + c_ctx * (1.0 - sg_cc)))

    g_ret_decay = jnp.sum(t_dlg, axis=-1).reshape(2, HEADS) * jax.nn.sigmoid(-ret_decay[0])
    g_lambda_full = -t_sp * jax.nn.sigmoid(-lam_full)

    def my_cols(full):
        return lax.dynamic_slice(full, (0, me * sc_loc), (full.shape[0], sc_loc))

    small_g = dict(
        c_ctx=g_c_ctx[None], b_ada=g_b_ada, norm1_g=t_n1[4:5], norm2_g=t_n2[2:3], ret_decay=g_ret_decay,
        conv_w=my_cols(t_conv[0:4]), conv_b=t_conv[4:5], lru_wa=t_wa.reshape(-1, LRU_BD), lru_ba=my_cols(t_ba),
        lru_wx=t_wx.reshape(-1, LRU_BD), lru_bx=my_cols(t_bx), lru_lambda=my_cols(g_lambda_full), final_g=t_f[0:1])
    small = dict(
        c_ctx=(c_ctx, m_c_ctx, v_c_ctx), b_ada=(b_ada, m_b_ada, v_b_ada), norm1_g=(norm1_g, m_norm1_g, v_norm1_g),
        norm2_g=(norm2_g, m_norm2_g, v_norm2_g), ret_decay=(ret_decay, m_ret_decay, v_ret_decay),
        conv_w=(conv_w, m_conv_w, v_conv_w), conv_b=(conv_b, m_conv_b, v_conv_b), lru_wa=(lru_wa, m_lru_wa, v_lru_wa),
        lru_ba=(lru_ba, m_lru_ba, v_lru_ba), lru_wx=(lru_wx, m_lru_wx, v_lru_wx), lru_bx=(lru_bx, m_lru_bx, v_lru_bx),
        lru_lambda=(lru_lambda, m_lru_lambda, v_lru_lambda), final_g=(final_g, m_final_g, v_final_g))
    names = list(small)
    items = [(small_g[k],) + tuple(a.reshape(small_g[k].shape) for a in small[k]) for k in names]
    res = {}
    for k, it, (d_, m_, v_) in zip(names, items, _adam_many(items, "adam_small")):
        shape = small[k][0].shape
        res[k] = tuple(a.reshape(shape) for a in (it[0], d_, m_, v_))

    def big(parts, w, m, v, name):
        out = _sum_adam(parts, w[0], m[0], v[0], name)
        return tuple(a[None] for a in out)

    res["w_ada"] = big([g_w_ada[None]], w_ada, m_w_ada, v_w_ada, "adam_w_ada")
    res["w_in"] = big([gwi_all], w_in, m_w_in, v_w_in, "adam_w_in")
    res["w_out"] = big([gwo_all], w_out, m_w_out, v_w_out, "adam_w_out")
    res["w_mlp1"] = big(gw1_all, w_mlp1, m_w_mlp1, v_w_mlp1, "adam_w_mlp1")
    res["w_mlp2"] = big(gw2_all, w_mlp2, m_w_mlp2, v_w_mlp2, "adam_w_mlp2")

    order = ["c_ctx", "w_ada", "b_ada", "norm1_g", "norm2_g", "w_in", "ret_decay", "conv_w", "conv_b", "lru_wa", "lru_ba",
             "lru_wx", "lru_bx", "lru_lambda", "w_out", "w_mlp1", "w_mlp2", "final_g"]
    outs = [loss, grad_x[None]]
    for j in range(4):
        outs += [res[k][j] for k in order]
    return tuple(outs)
```

```python
import jax
import jax.numpy as jnp
from jax import lax
from jax.experimental import pallas as pl
from jax.experimental.pallas import tpu as pltpu

F32 = jnp.float32
BF16 = jnp.bfloat16
AXES = ("x", "y", "c")
N_DEV = 8
MESH = pl.DeviceIdType.MESH

HEADS = 4
HEAD_DIM = 128
CHUNK = 128
RET_W = HEADS * HEAD_DIM
LRU_W = 512
LRU_BLOCKS = 8
LRU_BD = LRU_W // LRU_BLOCKS
LRU_C = 8.0
EPS = 1e-6
K_SCALE = HEAD_DIM ** -0.5
ROPE_BASE = 10000.0
GRID_W = 64
TM = 256
TL = 512
SUB = 8

ADAM_LR = 0.001
ADAM_B1 = 0.9
ADAM_B2 = 0.999
ADAM_EPS = 1e-08
ADAM_WD = 0.01
ADAM_STEP = 10

COL_G, COL_XR, COL_GATE = 0, 1, 2

R_CSH1, R_CSC1, R_SH1, R_SC1, R_G1, R_SH2, R_SC2, R_G2 = range(8)


def _pcall(body, **kw):
    return pl.pallas_call(body, **kw)


def _cp(vmem_mb=48):
    return pltpu.CompilerParams(vmem_limit_bytes=vmem_mb << 20)


def _sds(shape, dtype):
    return jax.ShapeDtypeStruct(shape, dtype)


def _dot(a, b):
    return jnp.dot(a.astype(BF16), b.astype(BF16), preferred_element_type=F32)


def _dot_nt(a, b):
    return lax.dot_general(a.astype(BF16), b.astype(BF16), (((1,), (1,)), ((), ())), preferred_element_type=F32)


def _dot_tn(a, b):
    return lax.dot_general(a.astype(BF16), b.astype(BF16), (((0,), (0,)), ((), ())), preferred_element_type=F32)


def _sigmoid(x):
    return 0.5 * jnp.tanh(0.5 * x) + 0.5


def _gelu(x):
    return 0.5 * x * (1.0 + jnp.tanh(0.7978845608028654 * (x + 0.044715 * x * x * x)))


def _dgelu(x):
    t = jnp.tanh(0.7978845608028654 * (x + 0.044715 * x * x * x))
    return 0.5 * (1.0 + t) + 0.5 * x * (1.0 - t * t) * 0.7978845608028654 * (1.0 + 3.0 * 0.044715 * x * x)


def _rows_iota(shape):
    return lax.broadcasted_iota(jnp.int32, shape, 0)


def _tile_order(dirn, s, cb, nb):
    if dirn == 0:
        return s
    return jnp.where(s < cb, cb - 1 - s, nb - 1 - (s - cb))


_SEMS = [pltpu.SemaphoreType.DMA((7,)), pltpu.SemaphoreType.DMA((7,)), pltpu.SemaphoreType.DMA(())]
_ANY = pl.BlockSpec(memory_space=pl.ANY)


def _gather_copies(x_ref, out_ref, send_sems, recv_sems, local_sem):
    mx, my, mc = lax.axis_index("x"), lax.axis_index("y"), lax.axis_index("c")
    me, sibling = (mx, my, mc), (mx, my, 1 - mc)
    chips = [(1 - mx, my), (mx, 1 - my), (1 - mx, 1 - my)]

    def slot(px, py, pc):
        return out_ref.at[4 * px + 2 * py + pc]

    def copy(k, block, to, src=None):
        return pltpu.make_async_remote_copy(
            src_ref=slot(*block) if src is None else src, dst_ref=slot(*block),
            send_sem=send_sems.at[k], recv_sem=recv_sems.at[k], device_id=to, device_id_type=MESH)

    mine = pltpu.make_async_copy(x_ref, slot(*me), local_sem)
    first = [copy(0, me, sibling, src=x_ref)] + [copy(1 + j, me, (*chip, mc), src=x_ref) for j, chip in enumerate(chips)]
    passed = [copy(4 + j, (*chip, mc), sibling) for j, chip in enumerate(chips)]
    recv_ici = [copy(1 + j, (*chip, mc), me) for j, chip in enumerate(chips)]
    recv_d2d = [copy(0, sibling, me)] + [copy(4 + j, (*chip, 1 - mc), me) for j, chip in enumerate(chips)]
    return mine, first, passed, recv_ici, recv_d2d


def _gather_start(*refs):
    mine, first, _, _, _ = _gather_copies(*refs)
    mine.start()
    for cp in first:
        cp.start()


def _gather_pass_on(*refs):
    _, _, passed, recv_ici, _ = _gather_copies(*refs)
    for landed, onward in zip(recv_ici, passed):
        landed.wait_recv()
        onward.start()


def _gather_finish(*refs):
    mine, first, passed, _, recv_d2d = _gather_copies(*refs)
    for landed in recv_d2d:
        landed.wait_recv()
    for cp in first + passed:
        cp.wait_send()
    mine.wait()


def _a2a_copies(g_ref, out_ref, send_sems, recv_sems, local_sem):
    mx, my, mc = lax.axis_index("x"), lax.axis_index("y"), lax.axis_index("c")
    me = 4 * mx + 2 * my + mc
    mine = pltpu.make_async_copy(g_ref.at[me], out_ref.at[me], local_sem)
    copies = []
    for k in range(1, N_DEV):
        px = 1 - mx if (k >> 2) & 1 else mx
        py = 1 - my if (k >> 1) & 1 else my
        pc = 1 - mc if k & 1 else mc
        copies.append(pltpu.make_async_remote_copy(
            src_ref=g_ref.at[4 * px + 2 * py + pc], dst_ref=out_ref.at[me],
            send_sem=send_sems.at[k - 1], recv_sem=recv_sems.at[k - 1],
            device_id=(px, py, pc), device_id_type=MESH))
    return mine, copies


def _a2a_start(*refs):
    mine, copies = _a2a_copies(*refs)
    mine.start()
    for cp in copies:
        cp.start()


def _a2a_finish(*refs):
    mine, copies = _a2a_copies(*refs)
    for cp in copies:
        cp.wait()
    mine.wait()


_EXCHANGES = {"gather": (_gather_start, _gather_finish), "a2a": (_a2a_start, _a2a_finish)}
PASS_ON_LEAD = 3


def _exchange_shape(kind, src):
    return _sds((N_DEV,) + src.shape if kind == "gather" else src.shape, src.dtype)


def _all_gather_small(x, name):
    def body(x_ref, out_ref, send_sems, recv_sems, local_sem):
        mx, my, mc = lax.axis_index("x"), lax.axis_index("y"), lax.axis_index("c")
        me = 4 * mx + 2 * my + mc
        mine = pltpu.make_async_copy(x_ref, out_ref.at[me], local_sem)
        mine.start()
        copies = []
        for k in range(1, N_DEV):
            peer = (1 - mx if (k >> 2) & 1 else mx, 1 - my if (k >> 1) & 1 else my, 1 - mc if k & 1 else mc)
            copies.append(pltpu.make_async_remote_copy(
                src_ref=x_ref, dst_ref=out_ref.at[me], send_sem=send_sems.at[k - 1], recv_sem=recv_sems.at[k - 1],
                device_id=peer, device_id_type=MESH))
            copies[-1].start()
        for cp in copies:
            cp.wait()
        mine.wait()

    return _pcall(body, name=name, out_shape=_exchange_shape("gather", x), in_specs=[_ANY], out_specs=_ANY,
                  scratch_shapes=list(_SEMS))(x)


def _pcall_ride(body, ride, args, *, name, grid, in_specs, out_specs, out_shape, scratch_shapes=(), compiler_params=None):
    if ride is None:
        out = _pcall(body, name=name, grid=grid, in_specs=in_specs, out_specs=out_specs, out_shape=out_shape,
                     scratch_shapes=list(scratch_shapes), compiler_params=compiler_params)(*args)
        return out, None
    kind, src = ride
    start, finish = _EXCHANGES[kind]
    single = not isinstance(out_shape, (tuple, list))
    out_specs_t = (out_specs,) if single else tuple(out_specs)
    out_shape_t = (out_shape,) if single else tuple(out_shape)
    n_in, n_out, n_sc = len(in_specs), len(out_shape_t), len(scratch_shapes)

    def wrapped(*refs):
        ins, src_ref = refs[:n_in], refs[n_in]
        outs, dst_ref = refs[n_in + 1:n_in + 1 + n_out], refs[n_in + 1 + n_out]
        scratch = refs[n_in + 2 + n_out:n_in + 2 + n_out + n_sc]
        sems = refs[n_in + 2 + n_out + n_sc:]
        first = pl.program_id(0) == 0
        last = pl.program_id(0) == grid[0] - 1
        for ax in range(1, len(grid)):
            first = jnp.logical_and(first, pl.program_id(ax) == 0)
            last = jnp.logical_and(last, pl.program_id(ax) == grid[ax] - 1)

        @pl.when(first)
        def _():
            start(src_ref, dst_ref, *sems)

        body(*ins, *outs, *scratch)

        if kind == "gather":
            @pl.when(pl.program_id(0) == max(grid[0] - PASS_ON_LEAD, 0))
            def _():
                _gather_pass_on(src_ref, dst_ref, *sems)

        @pl.when(last)
        def _():
            finish(src_ref, dst_ref, *sems)

    res = _pcall(wrapped, name=name, grid=grid, in_specs=list(in_specs) + [_ANY], out_specs=out_specs_t + (_ANY,),
                 out_shape=out_shape_t + (_exchange_shape(kind, src),),
                 scratch_shapes=list(scratch_shapes) + list(_SEMS), compiler_params=compiler_params)(*args, src)
    return (res[0] if single else tuple(res[:-1])), res[-1]


class _Epilogue:
    def __init__(self, fn, args, in_specs, out_specs, out_shape, steps=None, lhs_map=None, delay=0):
        self.fn, self.args, self.in_specs, self.out_specs, self.out_shape = fn, tuple(args), list(in_specs), out_specs, out_shape
        self.steps, self.lhs_map, self.delay = steps, lhs_map, delay


def _mm_nn(a, w, out_dtype, name, square_lhs=False, relu_out=False, ride=None, tm=TM, epi=None, vmem_mb=48):
    m, k = a.shape
    nb, _, bn = w.shape
    tm = min(tm, m)
    delay = 0 if epi is None else epi.delay
    steps = m // tm

    def body(*refs):
        a_ref, w_ref = refs[:2]
        av = a_ref[...]
        if square_lhs:
            av = av * av
        if delay:
            assert nb == 1
            held = refs[-1]

            @pl.when(pl.program_id(0) == 0)
            def _():
                held[...] = jnp.zeros_like(held)

            fresh = jnp.dot(av, w_ref[0], preferred_element_type=F32)
            epi.fn(held[...], slice(0, tm), *refs[2:-1])
            held[...] = fresh
            return
        if epi is not None:
            assert nb == 1
            epi.fn(jnp.dot(av, w_ref[0], preferred_element_type=F32), slice(0, tm), *refs[2:])
            return
        for j in range(nb):
            r = jnp.dot(av, w_ref[j], preferred_element_type=F32)
            if relu_out:
                r = jnp.maximum(r, 0.0)
            refs[2][:, j * bn:(j + 1) * bn] = r.astype(out_dtype)

    lhs_map = (lambda i: (i, 0)) if not delay else (lambda i: (jnp.minimum(i, steps - 1), 0))
    in_specs = [pl.BlockSpec((tm, k), lhs_map), pl.BlockSpec((nb, k, bn), lambda i: (0, 0, 0))]
    scratch = []
    if epi is None:
        args, out_specs, out_shape = (a, w), pl.BlockSpec((tm, nb * bn), lambda i: (i, 0)), _sds((m, nb * bn), out_dtype)
    else:
        args, out_specs, out_shape = (a, w) + epi.args, epi.out_specs, epi.out_shape
        in_specs += epi.in_specs
        if delay:
            scratch.append(pltpu.VMEM((tm, bn), F32))
    out, ex = _pcall_ride(body, ride, args, name=name, grid=(steps + delay,), in_specs=in_specs, out_specs=out_specs,
                          out_shape=out_shape, scratch_shapes=scratch, compiler_params=_cp(vmem_mb))
    return out if ride is None else (out, ex)


def _mm_nt(dy, w, out_dtype, name, relu_mul=None, ride=None, tm=TM, epi=None, vmem_mb=48):
    m = dy.shape[0]
    nb, k, bn = w.shape
    tm = min(tm, m)
    n_extra = (0 if relu_mul is None else 1) + (0 if epi is None else len(epi.args))

    delay = 0 if epi is None else epi.delay

    def body(*refs):
        dy_ref, w_ref = refs[:2]
        scratch = refs[-1 - delay:]
        extra, outs, wt = refs[2:2 + n_extra], refs[2 + n_extra:-1 - delay], scratch[0]

        @pl.when(pl.program_id(0) == 0)
        def _():
            for j in range(nb):
                wt[j * bn:(j + 1) * bn, :] = w_ref[j].T
            if delay:
                scratch[1][...] = jnp.zeros_like(scratch[1])

        if delay:
            held = scratch[1]
            fresh = jnp.dot(dy_ref[...], wt[...], preferred_element_type=F32)
            epi.fn(held[...], slice(0, tm), *extra, *outs)
            held[...] = fresh
            return
        acc = jnp.dot(dy_ref[...], wt[...], preferred_element_type=F32)
        if epi is not None:
            epi.fn(acc, slice(0, tm), *extra, *outs)
            return
        if relu_mul is not None:
            acc = acc * (2.0 * extra[0][...].astype(F32))
        outs[0][...] = acc.astype(out_dtype)

    steps = m // tm if epi is None or epi.steps is None else epi.steps
    lhs_tile = (lambda i: (i, 0)) if epi is None or epi.lhs_map is None else epi.lhs_map
    lhs_map = lhs_tile if not delay else (lambda i: lhs_tile(jnp.minimum(i, steps - 1)))
    in_specs = [pl.BlockSpec((tm, nb * bn), lhs_map), pl.BlockSpec((nb, k, bn), lambda i: (0, 0, 0))]
    args = [dy, w]
    if relu_mul is not None:
        in_specs.append(pl.BlockSpec((tm, k), lambda i: (i, 0)))
        args.append(relu_mul)
    scratch = [pltpu.VMEM((nb * bn, k), BF16)]
    if epi is None:
        out_specs, out_shape = pl.BlockSpec((tm, k), lambda i: (i, 0)), _sds((m, k), out_dtype)
    else:
        args += list(epi.args)
        in_specs += epi.in_specs
        out_specs, out_shape = epi.out_specs, epi.out_shape
        if delay:
            scratch.append(pltpu.VMEM((tm, k), F32))
    out, ex = _pcall_ride(
        body, ride, args, name=name, grid=(steps + delay,), in_specs=in_specs, out_specs=out_specs, out_shape=out_shape,
        scratch_shapes=scratch, compiler_params=_cp(vmem_mb))
    return out if ride is None else (out, ex)


def _mm_tn(a, b, name, col_blocks, block, out_dtype, tm, square_lhs=False):
    m, k = a.shape
    nn = b.shape[1]
    steps = m // tm
    if col_blocks:
        nblk, acc_shape = nn // block, (k, block)
        a_spec = pl.BlockSpec((tm, k), lambda j, s: (s, 0))
        b_spec = pl.BlockSpec((tm, block), lambda j, s: (s, j))
    else:
        nblk, acc_shape = k // block, (block, nn)
        a_spec = pl.BlockSpec((tm, block), lambda j, s: (s, j))
        b_spec = pl.BlockSpec((tm, nn), lambda j, s: (s, 0))

    def body(a_ref, b_ref, o_ref, acc):
        s = pl.program_id(1)

        @pl.when(s == 0)
        def _():
            acc[...] = jnp.zeros_like(acc)

        av = a_ref[...]
        if square_lhs:
            av = av.astype(F32)
            av = (av * av).astype(BF16)
        acc[...] += _dot_tn(av, b_ref[...])

        @pl.when(s == steps - 1)
        def _():
            o_ref[...] = acc[...].astype(out_dtype)

    return _pcall(
        body, name=name, grid=(nblk, steps), in_specs=[a_spec, b_spec],
        out_specs=pl.BlockSpec((None,) + acc_shape, lambda j, s: (j, 0, 0)),
        out_shape=_sds((nblk,) + acc_shape, out_dtype),
        scratch_shapes=[pltpu.VMEM(acc_shape, F32)], compiler_params=_cp(),
    )(a, b)


def _mm_wgrad(at, b, name, bn, out_dtype, transpose_out=False, square_rhs=False, halves=False, ride=None):
    k, m = at.shape
    nblk = b.shape[1] // bn
    rows, cols = (bn, k) if transpose_out else (k, bn)
    nout = 2 if halves else 1
    per = rows // nout

    def body(a_ref, b_ref, *o_refs):
        bv = b_ref[...]
        if square_rhs:
            bv = bv * bv
        r = jnp.dot(a_ref[...], bv, preferred_element_type=F32)
        r = (r.T if transpose_out else r).astype(out_dtype)
        for i, o_ref in enumerate(o_refs):
            o_ref[...] = r[i * per:(i + 1) * per, :]

    out, ex = _pcall_ride(
        body, ride, (at, b), name=name, grid=(nblk,),
        in_specs=[pl.BlockSpec((k, m), lambda j: (0, 0)), pl.BlockSpec((m, bn), lambda j: (0, j))],
        out_specs=tuple(pl.BlockSpec((None, per, cols), lambda j: (j, 0, 0)) for _ in range(nout)),
        out_shape=tuple(_sds((nblk, per, cols), out_dtype) for _ in range(nout)), compiler_params=_cp())
    out = out if halves else out[0]
    return out if ride is None else (out, ex)


def _mm_in(hn, w, cosf, sins, ride):
    m, k = hn.shape
    nb, _, bn = w.shape

    def body(a_ref, w_ref, c_ref, s_ref, qkv_ref, rest_ref, pt):
        av = a_ref[...]
        for j in range(nb):
            pt[:, j * bn:(j + 1) * bn] = jnp.dot(av, w_ref[j], preferred_element_type=F32)
        cf, ss = c_ref[...], s_ref[...]
        for h in range(HEADS):
            sq = slice(h * HEAD_DIM, (h + 1) * HEAD_DIM)
            sk = slice(RET_W + h * HEAD_DIM, RET_W + (h + 1) * HEAD_DIM)
            qkv_ref[:, sq] = _rot(pt[:, sq], cf, ss).astype(BF16)
            qkv_ref[:, sk] = (_rot(pt[:, sk], cf, ss) * K_SCALE).astype(BF16)
        qkv_ref[:, 2 * RET_W:] = pt[:, 2 * RET_W:3 * RET_W].astype(BF16)
        rest_ref[...] = pt[:, 3 * RET_W:]

    tab = pl.BlockSpec((TM, HEAD_DIM), lambda i: (i, 0))
    wide = pl.BlockSpec((TM, 3 * RET_W), lambda i: (i, 0))
    return _pcall_ride(
        body, ride, (hn, w, cosf, sins), name="mm_in", grid=(m // TM,),
        in_specs=[pl.BlockSpec((TM, k), lambda i: (i, 0)), pl.BlockSpec((nb, k, bn), lambda i: (0, 0, 0)), tab, tab],
        out_specs=(wide, wide), out_shape=(_sds((m, 3 * RET_W), BF16), _sds((m, nb * bn - 3 * RET_W), F32)),
        scratch_shapes=[pltpu.VMEM((TM, nb * bn), F32)], compiler_params=_cp())


def _mod_part(s16, w_ada, b_cols):
    def body(s_ref, w_ref, b_ref, o_ref):
        o_ref[...] = _dot(s_ref[...], w_ref[...]) + b_ref[...]

    return _pcall(body, name="mod_part", out_shape=_sds((s16.shape[0], w_ada.shape[1]), F32),
                  compiler_params=_cp())(s16, w_ada, b_cols)


def _ada_bwd(s16, dm_cols, w_ada):
    def body(s_ref, d_ref, w_ref, gw_ref, ds_ref):
        gw_ref[...] = _dot_tn(s_ref[...], d_ref[...])
        ds_ref[...] = _dot_nt(d_ref[...], w_ref[...])

    return _pcall(body, name="ada_bwd",
                  out_shape=(_sds(w_ada.shape, F32), _sds(s16.shape, F32)), compiler_params=_cp())(s16, dm_cols, w_ada)


def _norm1_fwd(ctx, x, g, modrows, cb, ride=None):
    l_len, d = ctx.shape
    nb = (l_len + x.shape[0]) // TM

    def body(ctx_ref, x_ref, g_ref, m_ref, o_ref, ot_ref):
        is_ctx = pl.program_id(0) < cb
        xin = jnp.where(is_ctx, ctx_ref[...], x_ref[...])
        sh = jnp.where(is_ctx, m_ref[R_CSH1:R_CSH1 + 1, :], m_ref[R_SH1:R_SH1 + 1, :])
        sc = jnp.where(is_ctx, m_ref[R_CSC1:R_CSC1 + 1, :], m_ref[R_SC1:R_SC1 + 1, :])
        ms = jnp.mean(xin * xin, axis=-1, keepdims=True)
        n = xin * lax.rsqrt(ms + EPS) * g_ref[...]
        hn = n * (1.0 + sc) + sh
        o_ref[...] = hn.astype(BF16)
        ot_ref[...] = hn.T.astype(BF16)

    return _pcall_ride(
        body, ride, (ctx, x, g, modrows), name="norm1_fwd", grid=(nb,),
        in_specs=[pl.BlockSpec((TM, d), lambda i: (jnp.minimum(i, cb - 1), 0)),
                  pl.BlockSpec((TM, d), lambda i: (jnp.maximum(i - cb, 0), 0)),
                  pl.BlockSpec((1, d), lambda i: (0, 0)), pl.BlockSpec((8, d), lambda i: (0, 0))],
        out_specs=(pl.BlockSpec((TM, d), lambda i: (i, 0)), pl.BlockSpec((d, TM), lambda i: (0, i))),
        out_shape=(_sds((nb * TM, d), BF16), _sds((d, nb * TM), BF16)), compiler_params=_cp())


def _mix_fwd(o_f, o_b, p, h_f, h_b, cb, t_len):
    def body(of_ref, ob_ref, g_ref, gate_ref, hf_ref, hb_ref, mix_ref):
        o = of_ref[...] + ob_ref[...]
        g = g_ref[...]
        sg = g * _sigmoid(g)
        for hh in range(HEADS):
            sl = slice(hh * HEAD_DIM, (hh + 1) * HEAD_DIM)
            oh = o[:, sl]
            yc = oh - jnp.mean(oh, axis=-1, keepdims=True)
            var = jnp.mean(yc * yc, axis=-1, keepdims=True)
            mix_ref[:, sl] = (sg[:, sl] * (yc * lax.rsqrt(var + EPS))).astype(BF16)
        mix_ref[:, RET_W:] = ((hf_ref[...] + hb_ref[...]) * _gelu(gate_ref[...])).astype(BF16)

    row = lambda i: (i + cb, 0)
    return _pcall(
        body, name="mix_fwd", grid=(t_len // TM,),
        in_specs=[pl.BlockSpec((TM, RET_W), row), pl.BlockSpec((TM, RET_W), row),
                  pl.BlockSpec((TM, RET_W), lambda i: (i + cb, COL_G)), pl.BlockSpec((TM, LRU_W), lambda i: (i + cb, COL_GATE)),
                  pl.BlockSpec((TM, LRU_W), row), pl.BlockSpec((TM, LRU_W), row)],
        out_specs=pl.BlockSpec((TM, RET_W + LRU_W), lambda i: (i, 0)),
        out_shape=_sds((t_len, RET_W + LRU_W), BF16), compiler_params=_cp(),
    )(o_f, o_b, p, p, h_f, h_b)


def _res_norm2_epilogue(x, g, modrows, tm):
    t_len, d = x.shape
    tm = min(tm, t_len)

    def fn(y, rows, x_ref, g_ref, m_ref, y_ref, x1_ref, h2_ref, h2t_ref):
        y_ref[rows, :] = y
        x1 = x_ref[rows, :] + m_ref[R_G1:R_G1 + 1, :] * y
        ms = jnp.mean(x1 * x1, axis=-1, keepdims=True)
        n = x1 * lax.rsqrt(ms + EPS) * g_ref[...]
        x1_ref[rows, :] = x1
        h2 = n * (1.0 + m_ref[R_SC2:R_SC2 + 1, :]) + m_ref[R_SH2:R_SH2 + 1, :]
        h2_ref[rows, :] = h2.astype(BF16)
        h2t_ref[:, rows] = h2.T.astype(BF16)

    t = pl.BlockSpec((tm, d), lambda i: (i, 0))
    return _Epilogue(
        fn, (x, g, modrows),
        in_specs=[t, pl.BlockSpec((1, d), lambda i: (0, 0)), pl.BlockSpec((8, d), lambda i: (0, 0))],
        out_specs=(t, t, t, pl.BlockSpec((d, tm), lambda i: (0, i))),
        out_shape=(_sds((t_len, d), F32), _sds((t_len, d), F32), _sds((t_len, d), BF16), _sds((d, t_len), BF16)))


def _tile(i):
    return jnp.maximum(i - 1, 0)


def _zero_at_start(acc_ref, rows):
    @pl.when(pl.program_id(0) == 0)
    def _():
        acc_ref[...] = jnp.zeros_like(acc_ref)


def _final_epilogue(x1, target, fg, modrows, tm):
    t_len, d = x1.shape

    def fn(z, rows, x1_ref, t_ref, fg_ref, m_ref, dx2_ref, dz_ref, dzt_ref, acc_ref):
        _zero_at_start(acc_ref, rows)
        g2 = m_ref[R_G2:R_G2 + 1, :]
        x2 = x1_ref[rows, :] + g2 * z
        rstd = lax.rsqrt(jnp.mean(x2 * x2, axis=-1, keepdims=True) + EPS)
        xh = x2 * rstd
        fg = fg_ref[...]
        e = xh * fg - t_ref[rows, :]
        dy = e * (1.0 / d)
        dxh = dy * fg
        dx2 = rstd * (dxh - xh * jnp.mean(dxh * xh, axis=-1, keepdims=True))
        dx2_ref[rows, :] = dx2
        dz = g2 * dx2
        dz_ref[rows, :] = dz.astype(BF16)
        dzt_ref[:, rows] = dz.T.astype(BF16)
        acc_ref[0:1, :] += jnp.sum(dy * xh, axis=0, keepdims=True)
        acc_ref[1:2, :] += jnp.sum(dx2 * z, axis=0, keepdims=True)
        acc_ref[2:3, :] += jnp.sum(e * e, axis=0, keepdims=True)

    t = pl.BlockSpec((tm, d), lambda i: (i, 0))
    return _Epilogue(
        fn, (x1, target, fg, modrows),
        in_specs=[t, t, pl.BlockSpec((1, d), lambda i: (0, 0)), pl.BlockSpec((8, d), lambda i: (0, 0))],
        out_specs=(t, t, pl.BlockSpec((d, tm), lambda i: (0, i)), pl.BlockSpec((8, d), lambda i: (0, 0))),
        out_shape=(_sds((t_len, d), F32), _sds((t_len, d), BF16), _sds((d, t_len), BF16), _sds((8, d), F32)))


def _bwd_norm2_epilogue(x1, dx2, y, g, modrows, tm):
    t_len, d = x1.shape

    def fn(dh2, rows, x1_ref, dx2_ref, y_ref, g_ref, m_ref, dx1_ref, dy_ref, acc_ref):
        _zero_at_start(acc_ref, rows)
        x1 = x1_ref[rows, :]
        rstd = lax.rsqrt(jnp.mean(x1 * x1, axis=-1, keepdims=True) + EPS)
        xh = x1 * rstd
        gn = g_ref[...]
        dn = dh2 * (1.0 + m_ref[R_SC2:R_SC2 + 1, :])
        dxh = dn * gn
        dx1 = dx2_ref[rows, :] + rstd * (dxh - xh * jnp.mean(dxh * xh, axis=-1, keepdims=True))
        dx1_ref[rows, :] = dx1
        dy_ref[rows, :] = (m_ref[R_G1:R_G1 + 1, :] * dx1).astype(BF16)
        acc_ref[0:1, :] += jnp.sum(dh2, axis=0, keepdims=True)
        acc_ref[1:2, :] += jnp.sum(dh2 * xh * gn, axis=0, keepdims=True)
        acc_ref[2:3, :] += jnp.sum(dn * xh, axis=0, keepdims=True)
        acc_ref[3:4, :] += jnp.sum(dx1 * y_ref[rows, :], axis=0, keepdims=True)

    t = pl.BlockSpec((tm, d), lambda i: (i, 0))
    return _Epilogue(
        fn, (x1, dx2, y, g, modrows),
        in_specs=[t, t, t, pl.BlockSpec((1, d), lambda i: (0, 0)), pl.BlockSpec((8, d), lambda i: (0, 0))],
        out_specs=(t, t, pl.BlockSpec((8, d), lambda i: (0, 0))),
        out_shape=(_sds((t_len, d), F32), _sds((t_len, d), BF16), _sds((8, d), F32)))


def _mix_bwd_epilogue(o_f, o_b, p, h_f, h_b, cb):
    n = o_f.shape[0]
    tile = _tile

    def fn(dm, rows, of_ref, ob_ref, g_ref, gate_ref, hf_ref, hb_ref, do_ref, dg_ref, dgate_ref, dhs_ref):
        keep = jnp.where(pl.program_id(0) - 1 < cb, 0.0, 1.0)
        dm = dm * keep
        o = of_ref[rows, :] + ob_ref[rows, :]
        g = g_ref[rows, :]
        s = _sigmoid(g)
        sg = g * s
        dsg = s * (1.0 + g * (1.0 - s))
        for hh in range(HEADS):
            sl = slice(hh * HEAD_DIM, (hh + 1) * HEAD_DIM)
            oh = o[:, sl]
            yc = oh - jnp.mean(oh, axis=-1, keepdims=True)
            rs = lax.rsqrt(jnp.mean(yc * yc, axis=-1, keepdims=True) + EPS)
            gn = yc * rs
            dret = dm[:, sl]
            dgn = dret * sg[:, sl]
            dg_ref[rows, sl] = (dret * gn * dsg[:, sl]).astype(BF16)
            do_ref[rows, sl] = (rs * (dgn - jnp.mean(dgn, axis=-1, keepdims=True)
                                      - gn * jnp.mean(dgn * gn, axis=-1, keepdims=True))).astype(BF16)
        dlru = dm[:, RET_W:]
        gate = gate_ref[rows, :]
        dhs_ref[rows, :] = dlru * _gelu(gate)
        dgate_ref[rows, :] = (dlru * (hf_ref[rows, :] + hb_ref[rows, :]) * _dgelu(gate)).astype(BF16)

    t = pl.BlockSpec((TM, RET_W), lambda i: (tile(i), 0))
    return _Epilogue(
        fn, (o_f, o_b, p, p, h_f, h_b),
        in_specs=[t, t, pl.BlockSpec((TM, RET_W), lambda i: (tile(i), COL_G)),
                  pl.BlockSpec((TM, LRU_W), lambda i: (tile(i), COL_GATE)), t, t],
        out_specs=(t, t, t, t), out_shape=(_sds((n, RET_W), BF16),) * 3 + (_sds((n, RET_W), F32),),
        steps=n // TM, lhs_map=lambda i: (jnp.maximum(i - cb, 0), 0), delay=1)


def _bwd_norm1_epilogue(ctx, x, dx1, g, modrows, cb):
    t_len, d = x.shape

    def fn(dh, rows, ctx_ref, x_ref, dx1_ref, g_ref, m_ref, gx_ref, acc_ref):
        is_ctx = pl.program_id(0) < cb
        _zero_at_start(acc_ref, rows)
        xin = jnp.where(is_ctx, ctx_ref[rows, :], x_ref[rows, :])
        sc = jnp.where(is_ctx, m_ref[R_CSC1:R_CSC1 + 1, :], m_ref[R_SC1:R_SC1 + 1, :])
        rstd = lax.rsqrt(jnp.mean(xin * xin, axis=-1, keepdims=True) + EPS)
        xh = xin * rstd
        gn = g_ref[...]
        dn = dh * (1.0 + sc)
        dxh = dn * gn
        gx_ref[rows, :] = dx1_ref[rows, :] + rstd * (dxh - xh * jnp.mean(dxh * xh, axis=-1, keepdims=True))
        s0 = jnp.sum(dh, axis=0, keepdims=True)
        s1 = jnp.sum(dh * xh * gn, axis=0, keepdims=True)
        acc_ref[4:5, :] += jnp.sum(dn * xh, axis=0, keepdims=True)
        acc_ref[0:1, :] += jnp.where(is_ctx, s0, 0.0)
        acc_ref[1:2, :] += jnp.where(is_ctx, s1, 0.0)
        acc_ref[2:3, :] += jnp.where(is_ctx, 0.0, s0)
        acc_ref[3:4, :] += jnp.where(is_ctx, 0.0, s1)

    lat = pl.BlockSpec((TM, d), lambda i: (jnp.maximum(i - cb, 0), 0))
    return _Epilogue(
        fn, (ctx, x, dx1, g, modrows),
        in_specs=[pl.BlockSpec((TM, d), lambda i: (jnp.minimum(i, cb - 1), 0)),
                  lat, lat, pl.BlockSpec((1, d), lambda i: (0, 0)), pl.BlockSpec((8, d), lambda i: (0, 0))],
        out_specs=(lat, pl.BlockSpec((8, d), lambda i: (0, 0))),
        out_shape=(_sds((t_len, d), F32), _sds((8, d), F32)))


def _rot(x, cf, ss):
    return x * cf + pltpu.roll(x, HEAD_DIM // 2, 1) * ss


def _decay_exponents(dirn):
    ii = lax.broadcasted_iota(jnp.int32, (CHUNK, CHUNK), 0)
    jj = lax.broadcasted_iota(jnp.int32, (CHUNK, CHUNK), 1)
    rel = ii - jj if dirn == 0 else jj - ii
    pos = ii.astype(F32)
    if dirn == 0:
        cq, cs = pos + 1.0, (CHUNK - 1.0) - pos
    else:
        cq, cs = CHUNK - pos, pos
    return rel, jnp.maximum(rel, 0).astype(F32), cq, cs


def _store_decay(lg_ref, dec, relf_ref=None):
    for dirn in (0, 1):
        rel, relf, cq, cs = _decay_exponents(dirn)
        if relf_ref is not None:
            relf_ref[dirn] = relf
        for h in range(HEADS):
            lgv = lg_ref[dirn, h]
            wq, ws = jnp.exp(lgv * cq), jnp.exp(lgv * cs)
            dec[dirn, h, 0] = jnp.where(rel >= 0, jnp.exp(lgv * relf), 0.0)
            dec[dirn, h, 1] = wq
            dec[dirn, h, 2] = ws
            if relf_ref is not None:
                dec[dirn, h, 3] = wq * cq
                dec[dirn, h, 4] = ws * cs


def _ret_rows(cc, nc, step_of):
    return [lambda s, dirn=dirn: _tile_order(dirn, step_of(s), cc, nc) for dirn in (0, 1)]


def _ret_in_specs(rows):
    specs = []
    for row in rows:
        specs += [pl.BlockSpec((CHUNK, RET_W), lambda s, o=o, row=row: (row(s), o)) for o in (0, 1, 2)]
    return specs


def _ret_fwd(qkv, lg, cc, ride=None):
    n = qkv.shape[0]
    nc = n // CHUNK
    rows = _ret_rows(cc, nc, lambda s: s)

    def body(lg_ref, q0, k0, v0, q1, k1, v1, o0, o1, sp0, sp1, st, dec):
        @pl.when(pl.program_id(0) == 0)
        def _():
            st[...] = jnp.zeros_like(st)
            _store_decay(lg_ref, dec)

        refs = ((q0, k0, v0, o0, sp0), (q1, k1, v1, o1, sp1))
        chains = [(dirn, h, slice(h * HEAD_DIM, (h + 1) * HEAD_DIM)) for dirn in (0, 1) for h in range(HEADS)]
        scores, cross, update = [], [], []
        for dirn, h, sl in chains:
            q_ref, k_ref, v_ref, _, sp_ref = refs[dirn]
            q, k, v = q_ref[:, sl], k_ref[:, sl], v_ref[:, sl]
            sp = st[dirn, h]
            sp_ref[h] = sp
            scores.append(_dot_nt(q, k))
            cross.append(_dot(q * dec[dirn, h, 1], sp))
            update.append(_dot_tn(k * dec[dirn, h, 2], v))
        masked = [(a * dec[dirn, h, 0]).astype(BF16) for a, (dirn, h, _) in zip(scores, chains)]
        intra = [_dot(sc, refs[dirn][2][:, sl]) for sc, (dirn, h, sl) in zip(masked, chains)]
        for (dirn, h, sl), o_in, o_cr, upd in zip(chains, intra, cross, update):
            refs[dirn][3][:, sl] = o_in + o_cr
            st[dirn, h] = jnp.exp(lg_ref[dirn, h] * CHUNK) * st[dirn, h] + upd

    o_specs = [pl.BlockSpec((CHUNK, RET_W), lambda s, row=row: (row(s), 0)) for row in rows]
    state = pl.BlockSpec((None, HEADS, CHUNK, HEAD_DIM), lambda s: (s, 0, 0, 0))
    return _pcall_ride(
        body, ride, (lg,) + (qkv,) * 6, name="ret_fwd", grid=(nc,),
        in_specs=[pl.BlockSpec(memory_space=pltpu.SMEM)] + _ret_in_specs(rows),
        out_specs=(o_specs[0], o_specs[1], state, state),
        out_shape=(_sds((n, RET_W), F32),) * 2 + (_sds((nc, HEADS, CHUNK, HEAD_DIM), F32),) * 2,
        scratch_shapes=[pltpu.VMEM((2, HEADS, CHUNK, HEAD_DIM), F32), pltpu.VMEM((2, HEADS, 3, CHUNK, CHUNK), F32)],
        compiler_params=_cp())


def _ret_bwd(qkv, lg, do, s_prev, cc, ride=None):
    n = qkv.shape[0]
    nc = n // CHUNK
    rows = _ret_rows(cc, nc, lambda s: nc - 1 - s)

    def body(lg_ref, q0, k0, v0, q1, k1, v1, do0, do1, sp0, sp1, dq0, dk0, dv0, dq1, dk1, dv1, dlg_ref, dst, dec, relf):
        @pl.when(pl.program_id(0) == 0)
        def _():
            dst[...] = jnp.zeros_like(dst)
            dlg_ref[...] = jnp.zeros_like(dlg_ref)
            _store_decay(lg_ref, dec, relf)

        refs = ((q0, k0, v0, do0, sp0, dq0, dk0, dv0), (q1, k1, v1, do1, sp1, dq1, dk1, dv1))
        chains = [(dirn, h, slice(h * HEAD_DIM, (h + 1) * HEAD_DIM)) for dirn in (0, 1) for h in range(HEADS)]

        def tiles(dirn, sl):
            q_ref, k_ref, v_ref, do_ref = refs[dirn][:4]
            return q_ref[:, sl], k_ref[:, sl], v_ref[:, sl], do_ref[:, sl]

        a_s, g1_s, da_s, h1_s = [], [], [], []
        for dirn, h, sl in chains:
            q, k, v, dov = tiles(dirn, sl)
            a_s.append(_dot_nt(q, k))
            g1_s.append(_dot_nt(dov, refs[dirn][4][h]))
            da_s.append(_dot_nt(dov, v))
            h1_s.append(_dot_nt(v, dst[dirn, h]))
        da_s = [da * dec[dirn, h, 0] for da, (dirn, h, _) in zip(da_s, chains)]
        dq_s, dk_s, dv_s, ds_s = [], [], [], []
        for (dirn, h, sl), a, da in zip(chains, a_s, da_s):
            q, k, v, dov = tiles(dirn, sl)
            dq_s.append(_dot(da, k))
            dk_s.append(_dot_tn(da, q))
            dv_s.append(_dot_tn(a * dec[dirn, h, 0], dov) + _dot(k * dec[dirn, h, 2], dst[dirn, h]))
            ds_s.append(_dot_tn(q * dec[dirn, h, 1], dov))
        for (dirn, h, sl), a, g1, da, h1, dq2, dk2, dv, ds2 in zip(chains, a_s, g1_s, da_s, h1_s, dq_s, dk_s, dv_s, ds_s):
            q, k, _, _ = tiles(dirn, sl)
            dq_ref, dk_ref, dv_ref = refs[dirn][5:]
            sp, dsn = refs[dirn][4][h], dst[dirn, h]
            gc = jnp.exp(lg_ref[dirn, h] * CHUNK)
            dq_ref[:, sl] = (g1 * dec[dirn, h, 1] + dq2).astype(BF16)
            dk_ref[:, sl] = (dk2 + h1 * dec[dirn, h, 2]).astype(BF16)
            dv_ref[:, sl] = dv.astype(BF16)
            term = da * a * relf[dirn] + q * g1 * dec[dirn, h, 3] + k * h1 * dec[dirn, h, 4] + sp * dsn * (CHUNK * gc)
            dlg_ref[dirn * HEADS + h:dirn * HEADS + h + 1, 0:HEAD_DIM] += jnp.sum(term, axis=0, keepdims=True)
            dst[dirn, h] = gc * dsn + ds2

    wide = [pl.BlockSpec((CHUNK, RET_W), lambda s, row=row: (row(s), 0)) for row in rows]
    state = pl.BlockSpec((None, HEADS, CHUNK, HEAD_DIM), lambda s: (nc - 1 - s, 0, 0, 0))
    return _pcall_ride(
        body, ride, (lg,) + (qkv,) * 6 + (do, do, s_prev[0], s_prev[1]), name="ret_bwd", grid=(nc,),
        in_specs=[pl.BlockSpec(memory_space=pltpu.SMEM)] + _ret_in_specs(rows) + wide + [state, state],
        out_specs=(wide[0],) * 3 + (wide[1],) * 3 + (pl.BlockSpec((2 * HEADS, 8 * HEAD_DIM), lambda s: (0, 0)),),
        out_shape=(_sds((n, RET_W), BF16),) * 6 + (_sds((2 * HEADS, 8 * HEAD_DIM), F32),),
        scratch_shapes=[pltpu.VMEM((2, HEADS, CHUNK, HEAD_DIM), F32), pltpu.VMEM((2, HEADS, 5, CHUNK, CHUNK), F32),
                        pltpu.VMEM((2, CHUNK, CHUNK), F32)],
        compiler_params=_cp())


def _shift_rows(cur, prev8, next8, k, seg_start, seg_end):
    tm = cur.shape[0]
    rows = _rows_iota(cur.shape)
    if k < 0:
        out = pltpu.roll(cur, -k, 0)
        for j in range(-k):
            halo = jnp.where(seg_start, 0.0, prev8[SUB + k + j:SUB + k + j + 1, :])
            out = jnp.where(rows == j, halo, out)
    else:
        out = pltpu.roll(cur, tm - k, 0)
        for j in range(k):
            halo = jnp.where(seg_end, 0.0, next8[j:j + 1, :])
            out = jnp.where(rows == tm - k + j, halo, out)
    return out


def _seg_flags(t, cb, nb):
    return jnp.logical_or(t == 0, t == cb), jnp.logical_or(t == cb - 1, t == nb - 1)


def _halo_specs(tile_of, n_rows, col):
    per = TM // SUB
    return [pl.BlockSpec((TM, LRU_W), lambda s: (tile_of(s), col)),
            pl.BlockSpec((SUB, LRU_W), lambda s: (jnp.maximum(tile_of(s) * per - 1, 0), col)),
            pl.BlockSpec((SUB, LRU_W), lambda s: (jnp.minimum((tile_of(s) + 1) * per, n_rows // SUB - 1), col))]


def _lru_gates(xr, prev8, next8, seg_start, seg_end, cw_ref, cb_ref, wg_ref, bg_ref, sp_ref):
    xm1 = _shift_rows(xr, prev8, next8, -1, seg_start, seg_end)
    xp1 = _shift_rows(xr, prev8, next8, 1, seg_start, seg_end)
    xp2 = _shift_rows(xr, prev8, next8, 2, seg_start, seg_end)
    xc = cb_ref[...] + xm1 * cw_ref[0:1, :] + xr * cw_ref[1:2, :] + xp1 * cw_ref[2:3, :] + xp2 * cw_ref[3:4, :]
    pre = _dot(xc, wg_ref[...]) + bg_ref[...]
    r = _sigmoid(pre[:, :LRU_W])
    i = _sigmoid(pre[:, LRU_W:])
    la = (-LRU_C) * r * sp_ref[...]
    a = jnp.exp(la)
    th = jnp.tanh(la)
    sq = jnp.sqrt(-2.0 * th / (1.0 - th))
    return xc, r, i, a, sq


def _scan_tile(a, b, ascending, a_sc, b_sc, carry, out_ref):
    tm, w = a.shape
    nsub = tm // SUB
    a = a.reshape(nsub, SUB, w)
    b = b.reshape(nsub, SUB, w)
    r8 = lax.broadcasted_iota(jnp.int32, a.shape, 1)
    for k in (1, 2, 4):
        if ascending:
            m = r8 >= k
            a_s, b_s = pltpu.roll(a, k, 1), pltpu.roll(b, k, 1)
        else:
            m = r8 < SUB - k
            a_s, b_s = pltpu.roll(a, SUB - k, 1), pltpu.roll(b, SUB - k, 1)
        b = a * jnp.where(m, b_s, 0.0) + b
        a = a * jnp.where(m, a_s, 1.0)
    a_sc[...] = a.reshape(tm, w)
    b_sc[...] = b.reshape(tm, w)

    def step(j, c):
        off = pl.multiple_of((j if ascending else nsub - 1 - j) * SUB, SUB)
        hb = a_sc[pl.ds(off, SUB), :] * c + b_sc[pl.ds(off, SUB), :]
        out_ref[pl.ds(off, SUB), :] = hb
        last = hb[SUB - 1:SUB, :] if ascending else hb[0:1, :]
        return jnp.broadcast_to(last, c.shape)

    carry[...] = lax.fori_loop(0, nsub, step, carry[...], unroll=8)


def _lru_fwd(p, wg, bg, sp, cw, cbias, dirn, cb, ride=None):
    n = p.shape[0]
    nb = n // TM
    tile_of = lambda s: _tile_order(dirn, s, cb, nb)

    def body(x_ref, xp_ref, xn_ref, wg_ref, bg_ref, sp_ref, cw_ref, cb_ref, h_ref, cin_ref, carry, a_sc, b_sc):
        s = pl.program_id(0)

        @pl.when(s == 0)
        def _():
            carry[...] = jnp.zeros_like(carry)

        seg_start, seg_end = _seg_flags(tile_of(s), cb, nb)
        xc, r, i, a, sq = _lru_gates(x_ref[...], xp_ref[...], xn_ref[...], seg_start, seg_end,
                                     cw_ref, cb_ref, wg_ref, bg_ref, sp_ref)
        cin_ref[...] = carry[...]
        _scan_tile(a, sq * (i * xc), dirn == 0, a_sc, b_sc, carry, h_ref)

    full = lambda shape: pl.BlockSpec(shape, lambda s: (0,) * len(shape))
    return _pcall_ride(
        body, ride, (p, p, p, wg, bg, sp, cw, cbias), name=f"lru_fwd{dirn}", grid=(nb,),
        in_specs=_halo_specs(tile_of, n, COL_XR) + [full((LRU_W, 2 * LRU_W)), full((1, 2 * LRU_W)), full((1, LRU_W)),
                                               full((4, LRU_W)), full((1, LRU_W))],
        out_specs=(pl.BlockSpec((TM, LRU_W), lambda s: (tile_of(s), 0)),
                   pl.BlockSpec((None, SUB, LRU_W), lambda s: (tile_of(s), 0, 0))),
        out_shape=(_sds((n, LRU_W), F32), _sds((nb, SUB, LRU_W), F32)),
        scratch_shapes=[pltpu.VMEM((SUB, LRU_W), F32), pltpu.VMEM((TM, LRU_W), F32), pltpu.VMEM((TM, LRU_W), F32)],
        compiler_params=_cp())


def _lru_bwd(p, wg, bg, sp, cw, cbias, h, cin, dhs, dirn, cb, ride=None):
    n = p.shape[0]
    nb = n // TM
    tile_of = lambda s: _tile_order(dirn, nb - 1 - s, cb, nb)

    def body(x_ref, xp_ref, xn_ref, wg_ref, bg_ref, sp_ref, cw_ref, cb_ref, h_ref, cin_ref, dhs_ref,
             dxc_ref, dwd_ref, acc_ref, carry, a_sc, b_sc, mu_sc, dwg_ref):
        s = pl.program_id(0)

        @pl.when(s == 0)
        def _():
            carry[...] = jnp.zeros_like(carry)
            dwg_ref[...] = jnp.zeros_like(dwg_ref)
            acc_ref[...] = jnp.zeros_like(acc_ref)

        seg_start, seg_end = _seg_flags(tile_of(s), cb, nb)
        xc, r, i, a, sq = _lru_gates(x_ref[...], xp_ref[...], xn_ref[...], seg_start, seg_end,
                                     cw_ref, cb_ref, wg_ref, bg_ref, sp_ref)
        rows = _rows_iota(a.shape)
        hv = h_ref[...]
        dh = dhs_ref[...]
        mu_next = carry[0:1, :]
        _scan_tile(a, a * dh, dirn == 1, a_sc, b_sc, carry, mu_sc)
        mu = mu_sc[...]
        if dirn == 0:
            hprev = jnp.where(rows == 0, cin_ref[0:1, :], pltpu.roll(hv, 1, 0))
            lam = dh + jnp.where(rows == TM - 1, mu_next, pltpu.roll(mu, TM - 1, 0))
        else:
            hprev = jnp.where(rows == TM - 1, cin_ref[0:1, :], pltpu.roll(hv, TM - 1, 0))
            lam = dh + jnp.where(rows == 0, mu_next, pltpu.roll(mu, 1, 0))
        ds = lam * (i * xc)
        di = lam * (sq * xc)
        dla = lam * hprev * a - ds * (a * a) / jnp.maximum(sq, 1e-20)
        dpr = dla * ((-LRU_C) * sp_ref[...]) * r * (1.0 - r)
        dpi = di * i * (1.0 - i)
        dpre = jnp.concatenate([dpr, dpi], axis=1)
        dxc_ref[...] = lam * (sq * i) + _dot_nt(dpre, wg_ref[...])
        dwg_ref[...] += _dot_tn(xc, dpre)
        acc_ref[0:1, :] += jnp.sum(dpre, axis=0, keepdims=True)
        acc_ref[1:2, 0:LRU_W] += jnp.sum(dla * ((-LRU_C) * r), axis=0, keepdims=True)

        @pl.when(s == nb - 1)
        def _():
            low = lax.broadcasted_iota(jnp.int32, (LRU_BD, 2 * LRU_BD), 1) < LRU_BD
            for half in (0, LRU_W):
                for m in range(LRU_BLOCKS // 2):
                    lanes = slice(half + 2 * LRU_BD * m, half + 2 * LRU_BD * (m + 1))
                    even = dwg_ref[2 * m * LRU_BD:(2 * m + 1) * LRU_BD, lanes]
                    odd = dwg_ref[(2 * m + 1) * LRU_BD:(2 * m + 2) * LRU_BD, lanes]
                    dwd_ref[:, lanes] = jnp.where(low, even, odd)

    full = lambda shape: pl.BlockSpec(shape, lambda s: (0,) * len(shape))
    tile = pl.BlockSpec((TM, LRU_W), lambda s: (tile_of(s), 0))
    return _pcall_ride(
        body, ride, (p, p, p, wg, bg, sp, cw, cbias, h, cin, dhs), name=f"lru_bwd{dirn}", grid=(nb,),
        in_specs=_halo_specs(tile_of, n, COL_XR) + [full((LRU_W, 2 * LRU_W)), full((1, 2 * LRU_W)), full((1, LRU_W)),
                                               full((4, LRU_W)), full((1, LRU_W)), tile,
                                               pl.BlockSpec((None, SUB, LRU_W), lambda s: (tile_of(s), 0, 0)), tile],
        out_specs=(tile, full((LRU_BD, 2 * LRU_W)), full((8, 2 * LRU_W))),
        out_shape=(_sds((n, LRU_W), F32), _sds((LRU_BD, 2 * LRU_W), F32), _sds((8, 2 * LRU_W), F32)),
        scratch_shapes=[pltpu.VMEM((SUB, LRU_W), F32)] + [pltpu.VMEM((TM, LRU_W), F32)] * 3
        + [pltpu.VMEM((LRU_W, 2 * LRU_W), F32)],
        compiler_params=_cp())


def _assemble_dp(dqs, dks, dvs, dg, dgate, dxcs, p, cw, cosf, sins, cb, ride=None):
    n = p.shape[0]
    nb = n // TM
    tile_of = lambda s: s

    def body(dqf, dqb, dkf, dkb, dvf, dvb, dg_ref, dgate_ref, cf, pf, nf, cb_, pb, nb_, x_ref, xp_ref, xn_ref,
             cw_ref, cos_ref, sin_ref, dp_ref, acc_ref):
        s = pl.program_id(0)

        @pl.when(s == 0)
        def _():
            acc_ref[...] = jnp.zeros_like(acc_ref)

        seg_start, seg_end = _seg_flags(s, cb, nb)
        dq = dqf[...].astype(F32) + dqb[...].astype(F32)
        dk = dkf[...].astype(F32) + dkb[...].astype(F32)
        cosv, sinv = cos_ref[...], sin_ref[...]
        for h in range(HEADS):
            sl = slice(h * HEAD_DIM, (h + 1) * HEAD_DIM)
            sk = slice(RET_W + h * HEAD_DIM, RET_W + (h + 1) * HEAD_DIM)
            dp_ref[:, sl] = (dq[:, sl] * cosv + pltpu.roll(dq[:, sl] * sinv, HEAD_DIM // 2, 1)).astype(BF16)
            dp_ref[:, sk] = ((dk[:, sl] * cosv + pltpu.roll(dk[:, sl] * sinv, HEAD_DIM // 2, 1)) * K_SCALE).astype(BF16)
        dp_ref[:, 2 * RET_W:3 * RET_W] = (dvf[...].astype(F32) + dvb[...].astype(F32)).astype(BF16)
        dp_ref[:, 3 * RET_W:4 * RET_W] = dg_ref[...].astype(BF16)
        dxc = cf[...] + cb_[...]
        dprev = pf[...] + pb[...]
        dnext = nf[...] + nb_[...]
        dxr = (_shift_rows(dxc, dprev, dnext, 1, seg_start, seg_end) * cw_ref[0:1, :] + dxc * cw_ref[1:2, :]
               + _shift_rows(dxc, dprev, dnext, -1, seg_start, seg_end) * cw_ref[2:3, :]
               + _shift_rows(dxc, dprev, dnext, -2, seg_start, seg_end) * cw_ref[3:4, :])
        dp_ref[:, 4 * RET_W:4 * RET_W + LRU_W] = dxr.astype(BF16)
        dp_ref[:, 4 * RET_W + LRU_W:] = dgate_ref[...].astype(BF16)
        xr, xp, xn = x_ref[...], xp_ref[...], xn_ref[...]
        for j, k in enumerate((-1, 0, 1, 2)):
            xs = xr if k == 0 else _shift_rows(xr, xp, xn, k, seg_start, seg_end)
            acc_ref[j:j + 1, 0:LRU_W] += jnp.sum(dxc * xs, axis=0, keepdims=True)
        acc_ref[4:5, 0:LRU_W] += jnp.sum(dxc, axis=0, keepdims=True)

    t = pl.BlockSpec((TM, RET_W), lambda s: (s, 0))
    args = (dqs[0], dqs[1], dks[0], dks[1], dvs[0], dvs[1], dg, dgate, dxcs[0], dxcs[0], dxcs[0], dxcs[1], dxcs[1], dxcs[1],
            p, p, p, cw, cosf, sins)
    tab = pl.BlockSpec((TM, HEAD_DIM), lambda s: (s, 0))
    return _pcall_ride(
        body, ride, args, name="assemble_dp", grid=(nb,),
        in_specs=[t] * 8 + _halo_specs(tile_of, n, 0) * 2 + _halo_specs(tile_of, n, COL_XR)
        + [pl.BlockSpec((4, LRU_W), lambda s: (0, 0)), tab, tab],
        out_specs=(pl.BlockSpec((TM, 4 * RET_W + 2 * LRU_W), lambda s: (s, 0)), pl.BlockSpec((8, 2 * LRU_W), lambda s: (0, 0))),
        out_shape=(_sds((n, 4 * RET_W + 2 * LRU_W), BF16), _sds((8, 2 * LRU_W), F32)), compiler_params=_cp())


def _adamw(g, w, m, v):
    nm = ADAM_B1 * m + (1.0 - ADAM_B1) * g
    nv = ADAM_B2 * v + (1.0 - ADAM_B2) * (g * g)
    m_hat = nm / (1.0 - ADAM_B1 ** ADAM_STEP)
    v_hat = nv / (1.0 - ADAM_B2 ** ADAM_STEP)
    return (-ADAM_LR) * (m_hat / (jnp.sqrt(v_hat) + ADAM_EPS) + ADAM_WD * w), nm, nv


def _adam_many(items, name):
    n = len(items)

    def body(*refs):
        for i in range(n):
            g, w, m, v = (r[...] for r in refs[4 * i:4 * i + 4])
            for o_ref, val in zip(refs[4 * n + 3 * i:4 * n + 3 * i + 3], _adamw(g, w, m, v)):
                o_ref[...] = val

    out_shape = tuple(_sds(it[1].shape, F32) for it in items for _ in range(3))
    res = _pcall(body, name=name, out_shape=out_shape, compiler_params=_cp())(*[a for it in items for a in it])
    return [tuple(res[3 * i:3 * i + 3]) for i in range(n)]


def _sum_adam(parts_list, w, m, v, name):
    nparts, _, c = parts_list[0].shape
    r = w.shape[0]
    tr = min(parts_list[0].shape[1], 128)
    starts, o = [], 0
    for pa in parts_list:
        starts.append(o)
        o += pa.shape[1] // tr
    nseg = len(parts_list)

    def body(*refs):
        p_refs = refs[:nseg]
        w_ref, m_ref, v_ref, g_ref, d_ref, nm_ref, nv_ref = refs[nseg:]
        i = pl.program_id(0)
        for s, p_ref in enumerate(p_refs):
            end = starts[s + 1] if s + 1 < nseg else r // tr

            @pl.when(jnp.logical_and(i >= starts[s], i < end))
            def _():
                g = p_ref[0].astype(F32)
                for j in range(1, nparts):
                    g = g + p_ref[j].astype(F32)
                g_ref[...] = g
                d_ref[...], nm_ref[...], nv_ref[...] = _adamw(g, w_ref[...], m_ref[...], v_ref[...])

    def seg_spec(s):
        last = parts_list[s].shape[1] // tr - 1
        return pl.BlockSpec((nparts, tr, c), lambda i: (0, jnp.clip(i - starts[s], 0, last), 0))

    t = pl.BlockSpec((tr, c), lambda i: (i, 0))
    return _pcall(
        body, name=name, grid=(r // tr,),
        in_specs=[seg_spec(s) for s in range(nseg)] + [t, t, t],
        out_specs=(t, t, t, t), out_shape=(_sds((r, c), F32),) * 4, compiler_params=_cp(),
    )(*parts_list, w, m, v)


def _sum_parts(parts, name):
    nparts, r, c = parts.shape

    def body(p_ref, o_ref):
        g = p_ref[0]
        for j in range(1, nparts):
            g = g + p_ref[j]
        o_ref[...] = g

    return _pcall(body, name=name, out_shape=_sds((r, c), parts.dtype), compiler_params=_cp())(parts)


def _rot_tables(l_len, t_len):
    rows = t_len // GRID_W
    n_freq = HEAD_DIM // 4
    inv = ROPE_BASE ** (-jnp.arange(n_freq, dtype=F32) / n_freq)
    ang_r = jnp.arange(rows, dtype=F32)[:, None] * inv
    ang_c = jnp.arange(GRID_W, dtype=F32)[:, None] * inv
    cos = jnp.concatenate([jnp.repeat(jnp.cos(ang_r), GRID_W, axis=0), jnp.tile(jnp.cos(ang_c), (rows, 1))], axis=-1)
    sin = jnp.concatenate([jnp.repeat(jnp.sin(ang_r), GRID_W, axis=0), jnp.tile(jnp.sin(ang_c), (rows, 1))], axis=-1)
    cosf = jnp.concatenate([jnp.ones((l_len, HEAD_DIM), F32), jnp.concatenate([cos, cos], axis=-1)], axis=0)
    sins = jnp.concatenate([jnp.zeros((l_len, HEAD_DIM), F32), jnp.concatenate([-sin, sin], axis=-1)], axis=0)
    return cosf, sins


def _block_diag(w):
    eye = jnp.eye(LRU_BLOCKS, dtype=w.dtype)
    return (w[:, :, None, :] * eye[:, None, :, None]).reshape(LRU_W, LRU_W)


def _blocks_from_lanes(dwd_half):
    return dwd_half.reshape(LRU_BD, LRU_BLOCKS, LRU_BD).transpose(1, 0, 2)


def _silu(x):
    return x * jax.nn.sigmoid(x)


def kernel(x, c, ctx, c_ctx, w_ada, b_ada, norm1_g, norm2_g, w_in, ret_decay, conv_w, conv_b, lru_wa, lru_ba, lru_wx, lru_bx, lru_lambda, w_out, w_mlp1, w_mlp2, final_g, loss_target, m_c_ctx, m_w_ada, m_b_ada, m_norm1_g, m_norm2_g, m_w_in, m_ret_decay, m_conv_w, m_conv_b, m_lru_wa, m_lru_ba, m_lru_wx, m_lru_bx, m_lru_lambda, m_w_out, m_w_mlp1, m_w_mlp2, m_final_g, v_c_ctx, v_w_ada, v_b_ada, v_norm1_g, v_norm2_g, v_w_in, v_ret_decay, v_conv_w, v_conv_b, v_lru_wa, v_lru_ba, v_lru_wx, v_lru_bx, v_lru_lambda, v_w_out, v_w_mlp1, v_w_mlp2, v_final_g):
    t_len, d = x.shape[1], x.shape[2]
    l_len = ctx.shape[1]
    cb, cc = l_len // TM, l_len // CHUNK
    me = 4 * lax.axis_index("x") + 2 * lax.axis_index("y") + lax.axis_index("c")
    x2d, ctx2d, tgt2d = x[0], ctx[0], loss_target[0]
    ada_cols = w_ada.shape[2]
    wa2d = w_ada[0]

    sc_loc = conv_w.shape[2]
    pack_a = jnp.zeros((8, d), F32)
    pack_a = pack_a.at[0].set(_silu(c[0]))
    pack_a = pack_a.at[1, :4 * sc_loc].set(conv_w[0].reshape(-1))
    pack_a = pack_a.at[2, :2 * sc_loc].set(lru_ba[0].reshape(-1))
    pack_a = pack_a.at[3, :2 * sc_loc].set(lru_bx[0].reshape(-1))
    pack_a = pack_a.at[4, :2 * sc_loc].set(lru_lambda[0].reshape(-1))
    all_a = _all_gather_small(pack_a, "gather_small_in")
    s16 = jnp.zeros((16, d), F32).at[0:8].set(all_a[:, 0, :]).at[8].set(_silu(c_ctx))

    def unshard(row, k):
        return all_a[:, row, :k * sc_loc].reshape(N_DEV, k, sc_loc).transpose(1, 0, 2).reshape(k, N_DEV * sc_loc)

    conv_w_full = unshard(1, 4)
    ba_full, bx_full, lam_full = unshard(2, 2), unshard(3, 2), unshard(4, 2)

    b_cols = lax.dynamic_slice(b_ada, (0, me * ada_cols), (1, ada_cols))
    mod_parts = _all_gather_small(_mod_part(s16, wa2d, b_cols), "gather_mod")
    mod_all = mod_parts.transpose(1, 0, 2).reshape(16, N_DEV * ada_cols)
    mod_me = lax.dynamic_slice(mod_all, (me, 0), (1, 6 * d)).reshape(6, d)
    mod_c = mod_all[8].reshape(6, d)
    modrows = jnp.concatenate([mod_c[0:2], mod_me], axis=0)

    lg = jax.nn.log_sigmoid(ret_decay[0])
    sp = jax.nn.softplus(-lam_full)
    wg = [jnp.concatenate([_block_diag(lru_wa[0, dd]), _block_diag(lru_wx[0, dd])], axis=1).astype(BF16) for dd in (0, 1)]
    bg = [jnp.concatenate([ba_full[dd], bx_full[dd]])[None, :] for dd in (0, 1)]
    cosf, sins = _rot_tables(l_len, t_len)

    (hn, hnt), win_g = _norm1_fwd(ctx2d, x2d, norm1_g, modrows, cb, ride=("gather", w_in[0].astype(BF16)))
    (qkv, p), w2_g = _mm_in(hn, win_g, cosf, sins, ride=("gather", w_mlp2[0].astype(BF16)))
    w2_g = w2_g.reshape(1, 4 * d, d)
    (o0, o1, sp0, sp1), w1_g = _ret_fwd(qkv, lg, cc, ride=("gather", w_mlp1[0].astype(BF16)))
    o, s_prev = [o0, o1], [sp0, sp1]
    (h0, cin0), wout_g = _lru_fwd(p, wg[0], bg[0], sp[0:1], conv_w_full, conv_b, 0, cb, ride=("gather", w_out[0].astype(BF16)))
    wout_g = wout_g.reshape(1, d, d)
    (h1, cin1), _ = _lru_fwd(p, wg[1], bg[1], sp[1:2], conv_w_full, conv_b, 1, cb)
    h, cin = [h0, h1], [cin0, cin1]
    mix = _mix_fwd(o[0], o[1], p, h[0], h[1], cb, t_len)
    y, x1, h2, h2t = _mm_nn(mix, wout_g, F32, "mm_out_norm2", tm=TL, epi=_res_norm2_epilogue(x2d, norm2_g, modrows, TL))
    r = _mm_nn(h2, w1_g, BF16, "mm_mlp1", relu_out=True, tm=TL)
    dx2, dz, dzt, facc = _mm_nn(r, w2_g, F32, "mm_mlp2_final", square_lhs=True, tm=TL, vmem_mb=58,
                                epi=_final_epilogue(x1, tgt2d, final_g[None, :], modrows, TL))

    du = _mm_nt(dz, w2_g, BF16, "mm_da2", relu_mul=r, tm=TL, vmem_mb=58)
    gw2_lo, gw2_hi = _mm_wgrad(dzt, r, "mm_dw2", w2_g.shape[1] // N_DEV, BF16, transpose_out=True, square_rhs=True,
                               halves=True)
    gw1_lo, gw1_hi = _mm_wgrad(h2t, du, "mm_dw1", w1_g.shape[2], BF16, halves=True)
    (dx1, dy, n2acc), gw2_lo_all = _mm_nt(du, w1_g, F32, "mm_dh2_norm2", ride=("a2a", gw2_lo), tm=TL, vmem_mb=60,
                                          epi=_bwd_norm2_epilogue(x1, dx2, y, norm2_g, modrows, TL))
    gwo = _mm_tn(mix, dy, "mm_dwout", False, d, BF16, 512).reshape(N_DEV, -1, d)
    (do, dg, dgate, dhs), gwo_all = _mm_nt(dy, wout_g, F32, "mm_dmix_mix", ride=("a2a", gwo),
                                           epi=_mix_bwd_epilogue(o[0], o[1], p, h[0], h[1], cb))
    dxcs, dwgs, laccs, rides, got = [], [], [], [("a2a", gw1_hi), ("a2a", gw2_hi)], []
    (dq0, dk0, dv0, dq1, dk1, dv1, dlg_lanes), gw1_lo_all = _ret_bwd(qkv, lg, do, s_prev, cc, ride=("a2a", gw1_lo))
    dqs, dks, dvs = [dq0, dq1], [dk0, dk1], [dv0, dv1]
    for dd in (0, 1):
        (dxc_, dwg_, lacc_), got_ = _lru_bwd(p, wg[dd], bg[dd], sp[dd:dd + 1], conv_w_full, conv_b, h[dd], cin[dd], dhs, dd, cb,
                                             ride=rides[dd])
        dxcs.append(dxc_); dwgs.append(dwg_); laccs.append(lacc_); got.append(got_)
    gw1_hi_all, gw2_hi_all = got
    gw1_all, gw2_all = [gw1_lo_all, gw1_hi_all], [gw2_lo_all, gw2_hi_all]
    (dp, cacc), _ = _assemble_dp(dqs, dks, dvs, dg, dgate, dxcs, p, conv_w_full, cosf, sins, cb)
    pack_b = jnp.concatenate([n2acc, facc, cacc, laccs[0], laccs[1], dlg_lanes, dwgs[0], dwgs[1]], axis=0)
    gwi, all_b = _mm_wgrad(hnt, dp, "mm_dwin", win_g.shape[2], BF16, ride=("gather", pack_b))
    (grad_x, n1acc), gwi_all = _mm_nt(dp, win_g, F32, "mm_dhn_norm1", ride=("a2a", gwi),
                                      epi=_bwd_norm1_epilogue(ctx2d, x2d, dx1, norm1_g, modrows, cb))
    all_n1 = _all_gather_small(n1acc, "gather_norm1_grads")
    tot = _sum_parts(all_b, "sum_small_grads")
    t_n1 = _sum_parts(all_n1, "sum_norm1_grads")
    t_n2, t_f, t_conv, t_dlg = tot[0:8], tot[8:16], tot[16:24, :LRU_W], tot[40:48, :HEAD_DIM]
    t_l = [tot[24:32], tot[32:40]]
    t_dwd = [tot[48:48 + LRU_BD], tot[48 + LRU_BD:48 + 2 * LRU_BD]]
    loss = (0.5 / d) * jnp.sum(t_f[2])
    t_wa = jnp.stack([_blocks_from_lanes(t_dwd[dd][:, :LRU_W]) for dd in (0, 1)])
    t_wx = jnp.stack([_blocks_from_lanes(t_dwd[dd][:, LRU_W:]) for dd in (0, 1)])
    t_ba = jnp.stack([t_l[dd][0, :LRU_W] for dd in (0, 1)])
    t_bx = jnp.stack([t_l[dd][0, LRU_W:] for dd in (0, 1)])
    t_sp = jnp.stack([t_l[dd][1, :LRU_W] for dd in (0, 1)])
    dm_rows = jnp.stack([all_n1[:, 2, :], all_n1[:, 3, :], all_b[:, 3, :], all_b[:, 0, :], all_b[:, 1, :], all_b[:, 9, :]],
                        axis=1).reshape(N_DEV, 6 * d)
    dm_c = jnp.concatenate([t_n1[0], t_n1[1], jnp.zeros((4 * d,), F32)])
    dm16 = jnp.zeros((16, 6 * d), F32).at[0:8].set(dm_rows).at[8].set(dm_c)
    g_b_ada = jnp.sum(dm16, axis=0)[None, :]
    dm_cols = lax.dynamic_slice(dm16, (0, me * ada_cols), (16, ada_cols))
    g_w_ada, ds16 = _ada_bwd(s16, dm_cols, wa2d)
    ds_all = _all_gather_small(ds16[8:16], "gather_dsilu")
    dsilu_cc = _sum_parts(ds_all, "sum_dsilu")[0]
    sg_cc = jax.nn.sigmoid(c_ctx)
    g_c_ctx = dsilu_cc * (sg_cc * (1.0 + c_ctx * (1.0 - sg_cc)))

    g_ret_decay = jnp.sum(t_dlg, axis=-1).reshape(2, HEADS) * jax.nn.sigmoid(-ret_decay[0])
    g_lambda_full = -t_sp * jax.nn.sigmoid(-lam_full)

    def my_cols(full):
        return lax.dynamic_slice(full, (0, me * sc_loc), (full.shape[0], sc_loc))

    small_g = dict(
        c_ctx=g_c_ctx[None], b_ada=g_b_ada, norm1_g=t_n1[4:5], norm2_g=t_n2[2:3], ret_decay=g_ret_decay,
        conv_w=my_cols(t_conv[0:4]), conv_b=t_conv[4:5], lru_wa=t_wa.reshape(-1, LRU_BD), lru_ba=my_cols(t_ba),
        lru_wx=t_wx.reshape(-1, LRU_BD), lru_bx=my_cols(t_bx), lru_lambda=my_cols(g_lambda_full), final_g=t_f[0:1])
    small = dict(
        c_ctx=(c_ctx, m_c_ctx, v_c_ctx), b_ada=(b_ada, m_b_ada, v_b_ada), norm1_g=(norm1_g, m_norm1_g, v_norm1_g),
        norm2_g=(norm2_g, m_norm2_g, v_norm2_g), ret_decay=(ret_decay, m_ret_decay, v_ret_decay),
        conv_w=(conv_w, m_conv_w, v_conv_w), conv_b=(conv_b, m_conv_b, v_conv_b), lru_wa=(lru_wa, m_lru_wa, v_lru_wa),
        lru_ba=(lru_ba, m_lru_ba, v_lru_ba), lru_wx=(lru_wx, m_lru_wx, v_lru_wx), lru_bx=(lru_bx, m_lru_bx, v_lru_bx),
        lru_lambda=(lru_lambda, m_lru_lambda, v_lru_lambda), final_g=(final_g, m_final_g, v_final_g))
    names = list(small)
    items = [(small_g[k],) + tuple(a.reshape(small_g[k].shape) for a in small[k]) for k in names]
    res = {}
    for k, it, (d_, m_, v_) in zip(names, items, _adam_many(items, "adam_small")):
        shape = small[k][0].shape
        res[k] = tuple(a.reshape(shape) for a in (it[0], d_, m_, v_))

    def big(parts, w, m, v, name):
        out = _sum_adam(parts, w[0], m[0], v[0], name)
        return tuple(a[None] for a in out)

    res["w_ada"] = big([g_w_ada[None]], w_ada, m_w_ada, v_w_ada, "adam_w_ada")
    res["w_in"] = big([gwi_all], w_in, m_w_in, v_w_in, "adam_w_in")
    res["w_out"] = big([gwo_all], w_out, m_w_out, v_w_out, "adam_w_out")
    res["w_mlp1"] = big(gw1_all, w_mlp1, m_w_mlp1, v_w_mlp1, "adam_w_mlp1")
    res["w_mlp2"] = big(gw2_all, w_mlp2, m_w_mlp2, v_w_mlp2, "adam_w_mlp2")

    order = ["c_ctx", "w_ada", "b_ada", "norm1_g", "norm2_g", "w_in", "ret_decay", "conv_w", "conv_b", "lru_wa", "lru_ba",
             "lru_wx", "lru_bx", "lru_lambda", "w_out", "w_mlp1", "w_mlp2", "final_g"]
    outs = [loss, grad_x[None]]
    for j in range(4):
        outs += [res[k][j] for k in order]
    return tuple(outs)
```

```python
import jax
import jax.numpy as jnp
from jax import lax
from jax.experimental import pallas as pl
from jax.experimental.pallas import tpu as pltpu

F32 = jnp.float32
BF16 = jnp.bfloat16
AXES = ("x", "y", "c")
N_DEV = 8
MESH = pl.DeviceIdType.MESH

HEADS = 4
HEAD_DIM = 128
CHUNK = 128
RET_W = HEADS * HEAD_DIM
LRU_W = 512
LRU_BLOCKS = 8
LRU_BD = LRU_W // LRU_BLOCKS
LRU_C = 8.0
EPS = 1e-6
K_SCALE = HEAD_DIM ** -0.5
ROPE_BASE = 10000.0
GRID_W = 64
TM = 256
TL = 512
SUB = 8

ADAM_LR = 0.001
ADAM_B1 = 0.9
ADAM_B2 = 0.999
ADAM_EPS = 1e-08
ADAM_WD = 0.01
ADAM_STEP = 10

COL_G, COL_XR, COL_GATE = 0, 1, 2

R_CSH1, R_CSC1, R_SH1, R_SC1, R_G1, R_SH2, R_SC2, R_G2 = range(8)


def _pcall(body, **kw):
    return pl.pallas_call(body, **kw)


def _cp(vmem_mb=48):
    return pltpu.CompilerParams(vmem_limit_bytes=vmem_mb << 20)


def _sds(shape, dtype):
    return jax.ShapeDtypeStruct(shape, dtype)


def _dot(a, b):
    return jnp.dot(a.astype(BF16), b.astype(BF16), preferred_element_type=F32)


def _dot_nt(a, b):
    return lax.dot_general(a.astype(BF16), b.astype(BF16), (((1,), (1,)), ((), ())), preferred_element_type=F32)


def _dot_tn(a, b):
    return lax.dot_general(a.astype(BF16), b.astype(BF16), (((0,), (0,)), ((), ())), preferred_element_type=F32)


def _sigmoid(x):
    return 0.5 * jnp.tanh(0.5 * x) + 0.5


def _gelu(x):
    return 0.5 * x * (1.0 + jnp.tanh(0.7978845608028654 * (x + 0.044715 * x * x * x)))


def _dgelu(x):
    t = jnp.tanh(0.7978845608028654 * (x + 0.044715 * x * x * x))
    return 0.5 * (1.0 + t) + 0.5 * x * (1.0 - t * t) * 0.7978845608028654 * (1.0 + 3.0 * 0.044715 * x * x)


def _rows_iota(shape):
    return lax.broadcasted_iota(jnp.int32, shape, 0)


def _tile_order(dirn, s, cb, nb):
    if dirn == 0:
        return s
    return jnp.where(s < cb, cb - 1 - s, nb - 1 - (s - cb))


_SEMS = [pltpu.SemaphoreType.DMA((7,)), pltpu.SemaphoreType.DMA((7,)), pltpu.SemaphoreType.DMA(())]
_ANY = pl.BlockSpec(memory_space=pl.ANY)


def _gather_copies(x_ref, out_ref, send_sems, recv_sems, local_sem):
    mx, my, mc = lax.axis_index("x"), lax.axis_index("y"), lax.axis_index("c")
    me, sibling = (mx, my, mc), (mx, my, 1 - mc)
    chips = [(1 - mx, my), (mx, 1 - my), (1 - mx, 1 - my)]

    def slot(px, py, pc):
        return out_ref.at[4 * px + 2 * py + pc]

    def copy(k, block, to, src=None):
        return pltpu.make_async_remote_copy(
            src_ref=slot(*block) if src is None else src, dst_ref=slot(*block),
            send_sem=send_sems.at[k], recv_sem=recv_sems.at[k], device_id=to, device_id_type=MESH)

    mine = pltpu.make_async_copy(x_ref, slot(*me), local_sem)
    first = [copy(0, me, sibling, src=x_ref)] + [copy(1 + j, me, (*chip, mc), src=x_ref) for j, chip in enumerate(chips)]
    passed = [copy(4 + j, (*chip, mc), sibling) for j, chip in enumerate(chips)]
    recv_ici = [copy(1 + j, (*chip, mc), me) for j, chip in enumerate(chips)]
    recv_d2d = [copy(0, sibling, me)] + [copy(4 + j, (*chip, 1 - mc), me) for j, chip in enumerate(chips)]
    return mine, first, passed, recv_ici, recv_d2d


def _gather_start(*refs):
    mine, first, _, _, _ = _gather_copies(*refs)
    mine.start()
    for cp in first:
        cp.start()


def _gather_pass_on(*refs):
    _, _, passed, recv_ici, _ = _gather_copies(*refs)
    for landed, onward in zip(recv_ici, passed):
        landed.wait_recv()
        onward.start()


def _gather_finish(*refs):
    mine, first, passed, _, recv_d2d = _gather_copies(*refs)
    for landed in recv_d2d:
        landed.wait_recv()
    for cp in first + passed:
        cp.wait_send()
    mine.wait()


def _a2a_copies(g_ref, out_ref, send_sems, recv_sems, local_sem):
    mx, my, mc = lax.axis_index("x"), lax.axis_index("y"), lax.axis_index("c")
    me = 4 * mx + 2 * my + mc
    mine = pltpu.make_async_copy(g_ref.at[me], out_ref.at[me], local_sem)
    copies = []
    for k in range(1, N_DEV):
        px = 1 - mx if (k >> 2) & 1 else mx
        py = 1 - my if (k >> 1) & 1 else my
        pc = 1 - mc if k & 1 else mc
        copies.append(pltpu.make_async_remote_copy(
            src_ref=g_ref.at[4 * px + 2 * py + pc], dst_ref=out_ref.at[me],
            send_sem=send_sems.at[k - 1], recv_sem=recv_sems.at[k - 1],
            device_id=(px, py, pc), device_id_type=MESH))
    return mine, copies


def _a2a_start(*refs):
    mine, copies = _a2a_copies(*refs)
    mine.start()
    for cp in copies:
        cp.start()


def _a2a_finish(*refs):
    mine, copies = _a2a_copies(*refs)
    for cp in copies:
        cp.wait()
    mine.wait()


_EXCHANGES = {"gather": (_gather_start, _gather_finish), "a2a": (_a2a_start, _a2a_finish)}
PASS_ON_LEAD = 3


def _exchange_shape(kind, src):
    return _sds((N_DEV,) + src.shape if kind == "gather" else src.shape, src.dtype)


def _all_gather_small(x, name):
    def body(x_ref, out_ref, send_sems, recv_sems, local_sem):
        mx, my, mc = lax.axis_index("x"), lax.axis_index("y"), lax.axis_index("c")
        me = 4 * mx + 2 * my + mc
        mine = pltpu.make_async_copy(x_ref, out_ref.at[me], local_sem)
        mine.start()
        copies = []
        for k in range(1, N_DEV):
            peer = (1 - mx if (k >> 2) & 1 else mx, 1 - my if (k >> 1) & 1 else my, 1 - mc if k & 1 else mc)
            copies.append(pltpu.make_async_remote_copy(
                src_ref=x_ref, dst_ref=out_ref.at[me], send_sem=send_sems.at[k - 1], recv_sem=recv_sems.at[k - 1],
                device_id=peer, device_id_type=MESH))
            copies[-1].start()
        for cp in copies:
            cp.wait()
        mine.wait()

    return _pcall(body, name=name, out_shape=_exchange_shape("gather", x), in_specs=[_ANY], out_specs=_ANY,
                  scratch_shapes=list(_SEMS))(x)


def _pcall_ride(body, ride, args, *, name, grid, in_specs, out_specs, out_shape, scratch_shapes=(), compiler_params=None):
    if ride is None:
        out = _pcall(body, name=name, grid=grid, in_specs=in_specs, out_specs=out_specs, out_shape=out_shape,
                     scratch_shapes=list(scratch_shapes), compiler_params=compiler_params)(*args)
        return out, None
    kind, src = ride
    start, finish = _EXCHANGES[kind]
    single = not isinstance(out_shape, (tuple, list))
    out_specs_t = (out_specs,) if single else tuple(out_specs)
    out_shape_t = (out_shape,) if single else tuple(out_shape)
    n_in, n_out, n_sc = len(in_specs), len(out_shape_t), len(scratch_shapes)

    def wrapped(*refs):
        ins, src_ref = refs[:n_in], refs[n_in]
        outs, dst_ref = refs[n_in + 1:n_in + 1 + n_out], refs[n_in + 1 + n_out]
        scratch = refs[n_in + 2 + n_out:n_in + 2 + n_out + n_sc]
        sems = refs[n_in + 2 + n_out + n_sc:]
        first = pl.program_id(0) == 0
        last = pl.program_id(0) == grid[0] - 1
        for ax in range(1, len(grid)):
            first = jnp.logical_and(first, pl.program_id(ax) == 0)
            last = jnp.logical_and(last, pl.program_id(ax) == grid[ax] - 1)

        @pl.when(first)
        def _():
            start(src_ref, dst_ref, *sems)

        body(*ins, *outs, *scratch)

        if kind == "gather":
            @pl.when(pl.program_id(0) == max(grid[0] - PASS_ON_LEAD, 0))
            def _():
                _gather_pass_on(src_ref, dst_ref, *sems)

        @pl.when(last)
        def _():
            finish(src_ref, dst_ref, *sems)

    res = _pcall(wrapped, name=name, grid=grid, in_specs=list(in_specs) + [_ANY], out_specs=out_specs_t + (_ANY,),
                 out_shape=out_shape_t + (_exchange_shape(kind, src),),
                 scratch_shapes=list(scratch_shapes) + list(_SEMS), compiler_params=compiler_params)(*args, src)
    return (res[0] if single else tuple(res[:-1])), res[-1]


class _Epilogue:
    def __init__(self, fn, args, in_specs, out_specs, out_shape, steps=None, lhs_map=None, delay=0):
        self.fn, self.args, self.in_specs, self.out_specs, self.out_shape = fn, tuple(args), list(in_specs), out_specs, out_shape
        self.steps, self.lhs_map, self.delay = steps, lhs_map, delay


def _mm_nn(a, w, out_dtype, name, square_lhs=False, relu_out=False, ride=None, tm=TM, epi=None, vmem_mb=48):
    m, k = a.shape
    nb, _, bn = w.shape
    tm = min(tm, m)
    delay = 0 if epi is None else epi.delay
    steps = m // tm

    def body(*refs):
        a_ref, w_ref = refs[:2]
        av = a_ref[...]
        if square_lhs:
            av = av * av
        if delay:
            assert nb == 1
            held = refs[-1]

            @pl.when(pl.program_id(0) == 0)
            def _():
                held[...] = jnp.zeros_like(held)

            fresh = jnp.dot(av, w_ref[0], preferred_element_type=F32)
            epi.fn(held[...], slice(0, tm), *refs[2:-1])
            held[...] = fresh
            return
        if epi is not None:
            assert nb == 1
            epi.fn(jnp.dot(av, w_ref[0], preferred_element_type=F32), slice(0, tm), *refs[2:])
            return
        for j in range(nb):
            r = jnp.dot(av, w_ref[j], preferred_element_type=F32)
            if relu_out:
                r = jnp.maximum(r, 0.0)
            refs[2][:, j * bn:(j + 1) * bn] = r.astype(out_dtype)

    lhs_map = (lambda i: (i, 0)) if not delay else (lambda i: (jnp.minimum(i, steps - 1), 0))
    in_specs = [pl.BlockSpec((tm, k), lhs_map), pl.BlockSpec((nb, k, bn), lambda i: (0, 0, 0))]
    scratch = []
    if epi is None:
        args, out_specs, out_shape = (a, w), pl.BlockSpec((tm, nb * bn), lambda i: (i, 0)), _sds((m, nb * bn), out_dtype)
    else:
        args, out_specs, out_shape = (a, w) + epi.args, epi.out_specs, epi.out_shape
        in_specs += epi.in_specs
        if delay:
            scratch.append(pltpu.VMEM((tm, bn), F32))
    out, ex = _pcall_ride(body, ride, args, name=name, grid=(steps + delay,), in_specs=in_specs, out_specs=out_specs,
                          out_shape=out_shape, scratch_shapes=scratch, compiler_params=_cp(vmem_mb))
    return out if ride is None else (out, ex)


def _mm_nt(dy, w, out_dtype, name, relu_mul=None, ride=None, tm=TM, epi=None, vmem_mb=48):
    m = dy.shape[0]
    nb, k, bn = w.shape
    tm = min(tm, m)
    n_extra = (0 if relu_mul is None else 1) + (0 if epi is None else len(epi.args))

    delay = 0 if epi is None else epi.delay

    def body(*refs):
        dy_ref, w_ref = refs[:2]
        scratch = refs[-1 - delay:]
        extra, outs, wt = refs[2:2 + n_extra], refs[2 + n_extra:-1 - delay], scratch[0]

        @pl.when(pl.program_id(0) == 0)
        def _():
            for j in range(nb):
                wt[j * bn:(j + 1) * bn, :] = w_ref[j].T
            if delay:
                scratch[1][...] = jnp.zeros_like(scratch[1])

        if delay:
            held = scratch[1]
            fresh = jnp.dot(dy_ref[...], wt[...], preferred_element_type=F32)
            epi.fn(held[...], slice(0, tm), *extra, *outs)
            held[...] = fresh
            return
        acc = jnp.dot(dy_ref[...], wt[...], preferred_element_type=F32)
        if epi is not None:
            epi.fn(acc, slice(0, tm), *extra, *outs)
            return
        if relu_mul is not None:
            acc = acc * (2.0 * extra[0][...].astype(F32))
        outs[0][...] = acc.astype(out_dtype)

    steps = m // tm if epi is None or epi.steps is None else epi.steps
    lhs_tile = (lambda i: (i, 0)) if epi is None or epi.lhs_map is None else epi.lhs_map
    lhs_map = lhs_tile if not delay else (lambda i: lhs_tile(jnp.minimum(i, steps - 1)))
    in_specs = [pl.BlockSpec((tm, nb * bn), lhs_map), pl.BlockSpec((nb, k, bn), lambda i: (0, 0, 0))]
    args = [dy, w]
    if relu_mul is not None:
        in_specs.append(pl.BlockSpec((tm, k), lambda i: (i, 0)))
        args.append(relu_mul)
    scratch = [pltpu.VMEM((nb * bn, k), BF16)]
    if epi is None:
        out_specs, out_shape = pl.BlockSpec((tm, k), lambda i: (i, 0)), _sds((m, k), out_dtype)
    else:
        args += list(epi.args)
        in_specs += epi.in_specs
        out_specs, out_shape = epi.out_specs, epi.out_shape
        if delay:
            scratch.append(pltpu.VMEM((tm, k), F32))
    out, ex = _pcall_ride(
        body, ride, args, name=name, grid=(steps + delay,), in_specs=in_specs, out_specs=out_specs, out_shape=out_shape,
        scratch_shapes=scratch, compiler_params=_cp(vmem_mb))
    return out if ride is None else (out, ex)


def _mm_tn(a, b, name, col_blocks, block, out_dtype, tm, square_lhs=False):
    m, k = a.shape
    nn = b.shape[1]
    steps = m // tm
    if col_blocks:
        nblk, acc_shape = nn // block, (k, block)
        a_spec = pl.BlockSpec((tm, k), lambda j, s: (s, 0))
        b_spec = pl.BlockSpec((tm, block), lambda j, s: (s, j))
    else:
        nblk, acc_shape = k // block, (block, nn)
        a_spec = pl.BlockSpec((tm, block), lambda j, s: (s, j))
        b_spec = pl.BlockSpec((tm, nn), lambda j, s: (s, 0))

    def body(a_ref, b_ref, o_ref, acc):
        s = pl.program_id(1)

        @pl.when(s == 0)
        def _():
            acc[...] = jnp.zeros_like(acc)

        av = a_ref[...]
        if square_lhs:
            av = av.astype(F32)
            av = (av * av).astype(BF16)
        acc[...] += _dot_tn(av, b_ref[...])

        @pl.when(s == steps - 1)
        def _():
            o_ref[...] = acc[...].astype(out_dtype)

    return _pcall(
        body, name=name, grid=(nblk, steps), in_specs=[a_spec, b_spec],
        out_specs=pl.BlockSpec((None,) + acc_shape, lambda j, s: (j, 0, 0)),
        out_shape=_sds((nblk,) + acc_shape, out_dtype),
        scratch_shapes=[pltpu.VMEM(acc_shape, F32)], compiler_params=_cp(),
    )(a, b)


def _mm_wgrad(at, b, name, bn, out_dtype, transpose_out=False, square_rhs=False, halves=False, ride=None):
    k, m = at.shape
    nblk = b.shape[1] // bn
    rows, cols = (bn, k) if transpose_out else (k, bn)
    nout = 2 if halves else 1
    per = rows // nout

    def body(a_ref, b_ref, *o_refs):
        bv = b_ref[...]
        if square_rhs:
            bv = bv * bv
        r = jnp.dot(a_ref[...], bv, preferred_element_type=F32)
        r = (r.T if transpose_out else r).astype(out_dtype)
        for i, o_ref in enumerate(o_refs):
            o_ref[...] = r[i * per:(i + 1) * per, :]

    out, ex = _pcall_ride(
        body, ride, (at, b), name=name, grid=(nblk,),
        in_specs=[pl.BlockSpec((k, m), lambda j: (0, 0)), pl.BlockSpec((m, bn), lambda j: (0, j))],
        out_specs=tuple(pl.BlockSpec((None, per, cols), lambda j: (j, 0, 0)) for _ in range(nout)),
        out_shape=tuple(_sds((nblk, per, cols), out_dtype) for _ in range(nout)), compiler_params=_cp())
    out = out if halves else out[0]
    return out if ride is None else (out, ex)


def _mm_in(hn, w, cosf, sins, ride):
    m, k = hn.shape
    nb, _, bn = w.shape

    def body(a_ref, w_ref, c_ref, s_ref, qkv_ref, rest_ref, pt):
        av = a_ref[...]
        for j in range(nb):
            pt[:, j * bn:(j + 1) * bn] = jnp.dot(av, w_ref[j], preferred_element_type=F32)
        cf, ss = c_ref[...], s_ref[...]
        for h in range(HEADS):
            sq = slice(h * HEAD_DIM, (h + 1) * HEAD_DIM)
            sk = slice(RET_W + h * HEAD_DIM, RET_W + (h + 1) * HEAD_DIM)
            qkv_ref[:, sq] = _rot(pt[:, sq], cf, ss).astype(BF16)
            qkv_ref[:, sk] = (_rot(pt[:, sk], cf, ss) * K_SCALE).astype(BF16)
        qkv_ref[:, 2 * RET_W:] = pt[:, 2 * RET_W:3 * RET_W].astype(BF16)
        rest_ref[...] = pt[:, 3 * RET_W:]

    tab = pl.BlockSpec((TM, HEAD_DIM), lambda i: (i, 0))
    wide = pl.BlockSpec((TM, 3 * RET_W), lambda i: (i, 0))
    return _pcall_ride(
        body, ride, (hn, w, cosf, sins), name="mm_in", grid=(m // TM,),
        in_specs=[pl.BlockSpec((TM, k), lambda i: (i, 0)), pl.BlockSpec((nb, k, bn), lambda i: (0, 0, 0)), tab, tab],
        out_specs=(wide, wide), out_shape=(_sds((m, 3 * RET_W), BF16), _sds((m, nb * bn - 3 * RET_W), F32)),
        scratch_shapes=[pltpu.VMEM((TM, nb * bn), F32)], compiler_params=_cp())


def _mod_exchange(pack_a, silu_cc, w_ada, b_cols):
    d, cols = pack_a.shape[1], w_ada.shape[1]

    def body(pa_ref, scc_ref, w_ref, b_ref, alla_ref, s16_ref, modp_ref, mp, send1, recv1, send2, recv2):
        mx, my, mc = lax.axis_index("x"), lax.axis_index("y"), lax.axis_index("c")
        me = 4 * mx + 2 * my + mc

        def exchange(src_ref, dst_ref, send, recv):
            copies = []
            for k in range(1, N_DEV):
                peer = (1 - mx if (k >> 2) & 1 else mx, 1 - my if (k >> 1) & 1 else my, 1 - mc if k & 1 else mc)
                copies.append(pltpu.make_async_remote_copy(
                    src_ref=src_ref, dst_ref=dst_ref.at[me], send_sem=send.at[k - 1], recv_sem=recv.at[k - 1],
                    device_id=peer, device_id_type=MESH))
                copies[-1].start()
            dst_ref[me] = src_ref[...]
            for cp in copies:
                cp.wait()

        exchange(pa_ref, alla_ref, send1, recv1)
        s16_ref[...] = jnp.zeros_like(s16_ref)
        for j in range(N_DEV):
            s16_ref[j:j + 1, :] = alla_ref[j, 0:1, :]
        s16_ref[N_DEV:N_DEV + 1, :] = scc_ref[...]
        mp[...] = _dot(s16_ref[...], w_ref[...]) + b_ref[...]
        exchange(mp, modp_ref, send2, recv2)

    vmem = pl.BlockSpec(memory_space=pltpu.VMEM)
    sem7 = pltpu.SemaphoreType.DMA((N_DEV - 1,))
    return _pcall(
        body, name="mod_exchange", in_specs=[vmem] * 4, out_specs=(vmem,) * 3,
        out_shape=(_sds((N_DEV,) + pack_a.shape, F32), _sds((2 * N_DEV, d), F32), _sds((N_DEV, 2 * N_DEV, cols), F32)),
        scratch_shapes=[pltpu.VMEM((2 * N_DEV, cols), F32), sem7, sem7, sem7, sem7], compiler_params=_cp(),
    )(pack_a, silu_cc, w_ada, b_cols)


def _ada_bwd(s16, dm_cols, w_ada):
    def body(s_ref, d_ref, w_ref, gw_ref, ds_ref):
        gw_ref[...] = _dot_tn(s_ref[...], d_ref[...])
        ds_ref[...] = _dot_nt(d_ref[...], w_ref[...])

    return _pcall(body, name="ada_bwd",
                  out_shape=(_sds(w_ada.shape, F32), _sds(s16.shape, F32)), compiler_params=_cp())(s16, dm_cols, w_ada)


def _norm1_fwd(ctx, x, g, modrows, cb, ride=None):
    l_len, d = ctx.shape
    nb = (l_len + x.shape[0]) // TM

    def body(ctx_ref, x_ref, g_ref, m_ref, o_ref, ot_ref):
        is_ctx = pl.program_id(0) < cb
        xin = jnp.where(is_ctx, ctx_ref[...], x_ref[...])
        sh = jnp.where(is_ctx, m_ref[R_CSH1:R_CSH1 + 1, :], m_ref[R_SH1:R_SH1 + 1, :])
        sc = jnp.where(is_ctx, m_ref[R_CSC1:R_CSC1 + 1, :], m_ref[R_SC1:R_SC1 + 1, :])
        ms = jnp.mean(xin * xin, axis=-1, keepdims=True)
        n = xin * lax.rsqrt(ms + EPS) * g_ref[...]
        hn = n * (1.0 + sc) + sh
        o_ref[...] = hn.astype(BF16)
        ot_ref[...] = hn.T.astype(BF16)

    return _pcall_ride(
        body, ride, (ctx, x, g, modrows), name="norm1_fwd", grid=(nb,),
        in_specs=[pl.BlockSpec((TM, d), lambda i: (jnp.minimum(i, cb - 1), 0)),
                  pl.BlockSpec((TM, d), lambda i: (jnp.maximum(i - cb, 0), 0)),
                  pl.BlockSpec((1, d), lambda i: (0, 0)), pl.BlockSpec((8, d), lambda i: (0, 0))],
        out_specs=(pl.BlockSpec((TM, d), lambda i: (i, 0)), pl.BlockSpec((d, TM), lambda i: (0, i))),
        out_shape=(_sds((nb * TM, d), BF16), _sds((d, nb * TM), BF16)), compiler_params=_cp())


def _mix_fwd(o_f, o_b, p, h_f, h_b, cb, t_len):
    def body(of_ref, ob_ref, g_ref, gate_ref, hf_ref, hb_ref, mix_ref):
        o = of_ref[...] + ob_ref[...]
        g = g_ref[...]
        sg = g * _sigmoid(g)
        for hh in range(HEADS):
            sl = slice(hh * HEAD_DIM, (hh + 1) * HEAD_DIM)
            oh = o[:, sl]
            yc = oh - jnp.mean(oh, axis=-1, keepdims=True)
            var = jnp.mean(yc * yc, axis=-1, keepdims=True)
            mix_ref[:, sl] = (sg[:, sl] * (yc * lax.rsqrt(var + EPS))).astype(BF16)
        mix_ref[:, RET_W:] = ((hf_ref[...] + hb_ref[...]) * _gelu(gate_ref[...])).astype(BF16)

    row = lambda i: (i + cb, 0)
    return _pcall(
        body, name="mix_fwd", grid=(t_len // TM,),
        in_specs=[pl.BlockSpec((TM, RET_W), row), pl.BlockSpec((TM, RET_W), row),
                  pl.BlockSpec((TM, RET_W), lambda i: (i + cb, COL_G)), pl.BlockSpec((TM, LRU_W), lambda i: (i + cb, COL_GATE)),
                  pl.BlockSpec((TM, LRU_W), row), pl.BlockSpec((TM, LRU_W), row)],
        out_specs=pl.BlockSpec((TM, RET_W + LRU_W), lambda i: (i, 0)),
        out_shape=_sds((t_len, RET_W + LRU_W), BF16), compiler_params=_cp(),
    )(o_f, o_b, p, p, h_f, h_b)


def _res_norm2_epilogue(x, g, modrows, tm):
    t_len, d = x.shape
    tm = min(tm, t_len)

    def fn(y, rows, x_ref, g_ref, m_ref, y_ref, x1_ref, h2_ref, h2t_ref):
        y_ref[rows, :] = y
        x1 = x_ref[rows, :] + m_ref[R_G1:R_G1 + 1, :] * y
        ms = jnp.mean(x1 * x1, axis=-1, keepdims=True)
        n = x1 * lax.rsqrt(ms + EPS) * g_ref[...]
        x1_ref[rows, :] = x1
        h2 = n * (1.0 + m_ref[R_SC2:R_SC2 + 1, :]) + m_ref[R_SH2:R_SH2 + 1, :]
        h2_ref[rows, :] = h2.astype(BF16)
        h2t_ref[:, rows] = h2.T.astype(BF16)

    t = pl.BlockSpec((tm, d), lambda i: (i, 0))
    return _Epilogue(
        fn, (x, g, modrows),
        in_specs=[t, pl.BlockSpec((1, d), lambda i: (0, 0)), pl.BlockSpec((8, d), lambda i: (0, 0))],
        out_specs=(t, t, t, pl.BlockSpec((d, tm), lambda i: (0, i))),
        out_shape=(_sds((t_len, d), F32), _sds((t_len, d), F32), _sds((t_len, d), BF16), _sds((d, t_len), BF16)))


def _tile(i):
    return jnp.maximum(i - 1, 0)


def _zero_at_start(acc_ref, rows):
    @pl.when(pl.program_id(0) == 0)
    def _():
        acc_ref[...] = jnp.zeros_like(acc_ref)


def _final_epilogue(x1, target, fg, modrows, tm):
    t_len, d = x1.shape

    def fn(z, rows, x1_ref, t_ref, fg_ref, m_ref, dx2_ref, dz_ref, dzt_ref, acc_ref):
        _zero_at_start(acc_ref, rows)
        g2 = m_ref[R_G2:R_G2 + 1, :]
        x2 = x1_ref[rows, :] + g2 * z
        rstd = lax.rsqrt(jnp.mean(x2 * x2, axis=-1, keepdims=True) + EPS)
        xh = x2 * rstd
        fg = fg_ref[...]
        e = xh * fg - t_ref[rows, :]
        dy = e * (1.0 / d)
        dxh = dy * fg
        dx2 = rstd * (dxh - xh * jnp.mean(dxh * xh, axis=-1, keepdims=True))
        dx2_ref[rows, :] = dx2
        dz = g2 * dx2
        dz_ref[rows, :] = dz.astype(BF16)
        dzt_ref[:, rows] = dz.T.astype(BF16)
        acc_ref[0:1, :] += jnp.sum(dy * xh, axis=0, keepdims=True)
        acc_ref[1:2, :] += jnp.sum(dx2 * z, axis=0, keepdims=True)
        acc_ref[2:3, :] += jnp.sum(e * e, axis=0, keepdims=True)

    t = pl.BlockSpec((tm, d), lambda i: (i, 0))
    return _Epilogue(
        fn, (x1, target, fg, modrows),
        in_specs=[t, t, pl.BlockSpec((1, d), lambda i: (0, 0)), pl.BlockSpec((8, d), lambda i: (0, 0))],
        out_specs=(t, t, pl.BlockSpec((d, tm), lambda i: (0, i)), pl.BlockSpec((8, d), lambda i: (0, 0))),
        out_shape=(_sds((t_len, d), F32), _sds((t_len, d), BF16), _sds((d, t_len), BF16), _sds((8, d), F32)))


def _bwd_norm2_epilogue(x1, dx2, y, g, modrows, tm):
    t_len, d = x1.shape

    def fn(dh2, rows, x1_ref, dx2_ref, y_ref, g_ref, m_ref, dx1_ref, dy_ref, acc_ref):
        _zero_at_start(acc_ref, rows)
        x1 = x1_ref[rows, :]
        rstd = lax.rsqrt(jnp.mean(x1 * x1, axis=-1, keepdims=True) + EPS)
        xh = x1 * rstd
        gn = g_ref[...]
        dn = dh2 * (1.0 + m_ref[R_SC2:R_SC2 + 1, :])
        dxh = dn * gn
        dx1 = dx2_ref[rows, :] + rstd * (dxh - xh * jnp.mean(dxh * xh, axis=-1, keepdims=True))
        dx1_ref[rows, :] = dx1
        dy_ref[rows, :] = (m_ref[R_G1:R_G1 + 1, :] * dx1).astype(BF16)
        acc_ref[0:1, :] += jnp.sum(dh2, axis=0, keepdims=True)
        acc_ref[1:2, :] += jnp.sum(dh2 * xh * gn, axis=0, keepdims=True)
        acc_ref[2:3, :] += jnp.sum(dn * xh, axis=0, keepdims=True)
        acc_ref[3:4, :] += jnp.sum(dx1 * y_ref[rows, :], axis=0, keepdims=True)

    t = pl.BlockSpec((tm, d), lambda i: (i, 0))
    return _Epilogue(
        fn, (x1, dx2, y, g, modrows),
        in_specs=[t, t, t, pl.BlockSpec((1, d), lambda i: (0, 0)), pl.BlockSpec((8, d), lambda i: (0, 0))],
        out_specs=(t, t, pl.BlockSpec((8, d), lambda i: (0, 0))),
        out_shape=(_sds((t_len, d), F32), _sds((t_len, d), BF16), _sds((8, d), F32)))


def _mix_bwd_epilogue(o_f, o_b, p, h_f, h_b, cb):
    n = o_f.shape[0]
    tile = _tile

    def fn(dm, rows, of_ref, ob_ref, g_ref, gate_ref, hf_ref, hb_ref, do_ref, dg_ref, dgate_ref, dhs_ref):
        keep = jnp.where(pl.program_id(0) - 1 < cb, 0.0, 1.0)
        dm = dm * keep
        o = of_ref[rows, :] + ob_ref[rows, :]
        g = g_ref[rows, :]
        s = _sigmoid(g)
        sg = g * s
        dsg = s * (1.0 + g * (1.0 - s))
        for hh in range(HEADS):
            sl = slice(hh * HEAD_DIM, (hh + 1) * HEAD_DIM)
            oh = o[:, sl]
            yc = oh - jnp.mean(oh, axis=-1, keepdims=True)
            rs = lax.rsqrt(jnp.mean(yc * yc, axis=-1, keepdims=True) + EPS)
            gn = yc * rs
            dret = dm[:, sl]
            dgn = dret * sg[:, sl]
            dg_ref[rows, sl] = (dret * gn * dsg[:, sl]).astype(BF16)
            do_ref[rows, sl] = (rs * (dgn - jnp.mean(dgn, axis=-1, keepdims=True)
                                      - gn * jnp.mean(dgn * gn, axis=-1, keepdims=True))).astype(BF16)
        dlru = dm[:, RET_W:]
        gate = gate_ref[rows, :]
        dhs_ref[rows, :] = dlru * _gelu(gate)
        dgate_ref[rows, :] = (dlru * (hf_ref[rows, :] + hb_ref[rows, :]) * _dgelu(gate)).astype(BF16)

    t = pl.BlockSpec((TM, RET_W), lambda i: (tile(i), 0))
    return _Epilogue(
        fn, (o_f, o_b, p, p, h_f, h_b),
        in_specs=[t, t, pl.BlockSpec((TM, RET_W), lambda i: (tile(i), COL_G)),
                  pl.BlockSpec((TM, LRU_W), lambda i: (tile(i), COL_GATE)), t, t],
        out_specs=(t, t, t, t), out_shape=(_sds((n, RET_W), BF16),) * 3 + (_sds((n, RET_W), F32),),
        steps=n // TM, lhs_map=lambda i: (jnp.maximum(i - cb, 0), 0), delay=1)


def _bwd_norm1_epilogue(ctx, x, dx1, g, modrows, cb):
    t_len, d = x.shape

    def fn(dh, rows, ctx_ref, x_ref, dx1_ref, g_ref, m_ref, gx_ref, acc_ref):
        is_ctx = pl.program_id(0) < cb
        _zero_at_start(acc_ref, rows)
        xin = jnp.where(is_ctx, ctx_ref[rows, :], x_ref[rows, :])
        sc = jnp.where(is_ctx, m_ref[R_CSC1:R_CSC1 + 1, :], m_ref[R_SC1:R_SC1 + 1, :])
        rstd = lax.rsqrt(jnp.mean(xin * xin, axis=-1, keepdims=True) + EPS)
        xh = xin * rstd
        gn = g_ref[...]
        dn = dh * (1.0 + sc)
        dxh = dn * gn
        gx_ref[rows, :] = dx1_ref[rows, :] + rstd * (dxh - xh * jnp.mean(dxh * xh, axis=-1, keepdims=True))
        s0 = jnp.sum(dh, axis=0, keepdims=True)
        s1 = jnp.sum(dh * xh * gn, axis=0, keepdims=True)
        acc_ref[4:5, :] += jnp.sum(dn * xh, axis=0, keepdims=True)
        acc_ref[0:1, :] += jnp.where(is_ctx, s0, 0.0)
        acc_ref[1:2, :] += jnp.where(is_ctx, s1, 0.0)
        acc_ref[2:3, :] += jnp.where(is_ctx, 0.0, s0)
        acc_ref[3:4, :] += jnp.where(is_ctx, 0.0, s1)

    lat = pl.BlockSpec((TM, d), lambda i: (jnp.maximum(i - cb, 0), 0))
    return _Epilogue(
        fn, (ctx, x, dx1, g, modrows),
        in_specs=[pl.BlockSpec((TM, d), lambda i: (jnp.minimum(i, cb - 1), 0)),
                  lat, lat, pl.BlockSpec((1, d), lambda i: (0, 0)), pl.BlockSpec((8, d), lambda i: (0, 0))],
        out_specs=(lat, pl.BlockSpec((8, d), lambda i: (0, 0))),
        out_shape=(_sds((t_len, d), F32), _sds((8, d), F32)))


def _rot(x, cf, ss):
    return x * cf + pltpu.roll(x, HEAD_DIM // 2, 1) * ss


def _decay_exponents(dirn):
    ii = lax.broadcasted_iota(jnp.int32, (CHUNK, CHUNK), 0)
    jj = lax.broadcasted_iota(jnp.int32, (CHUNK, CHUNK), 1)
    rel = ii - jj if dirn == 0 else jj - ii
    pos = ii.astype(F32)
    if dirn == 0:
        cq, cs = pos + 1.0, (CHUNK - 1.0) - pos
    else:
        cq, cs = CHUNK - pos, pos
    return rel, jnp.maximum(rel, 0).astype(F32), cq, cs


def _store_decay(lg_ref, dec, relf_ref=None):
    for dirn in (0, 1):
        rel, relf, cq, cs = _decay_exponents(dirn)
        if relf_ref is not None:
            relf_ref[dirn] = relf
        for h in range(HEADS):
            lgv = lg_ref[dirn, h]
            wq, ws = jnp.exp(lgv * cq), jnp.exp(lgv * cs)
            dec[dirn, h, 0] = jnp.where(rel >= 0, jnp.exp(lgv * relf), 0.0)
            dec[dirn, h, 1] = wq
            dec[dirn, h, 2] = ws
            if relf_ref is not None:
                dec[dirn, h, 3] = wq * cq
                dec[dirn, h, 4] = ws * cs


def _ret_rows(cc, nc, step_of):
    return [lambda s, dirn=dirn: _tile_order(dirn, step_of(s), cc, nc) for dirn in (0, 1)]


def _ret_in_specs(rows):
    specs = []
    for row in rows:
        specs += [pl.BlockSpec((CHUNK, RET_W), lambda s, o=o, row=row: (row(s), o)) for o in (0, 1, 2)]
    return specs


def _ret_fwd(qkv, lg, cc, ride=None):
    n = qkv.shape[0]
    nc = n // CHUNK
    rows = _ret_rows(cc, nc, lambda s: s)

    def body(lg_ref, q0, k0, v0, q1, k1, v1, o0, o1, sp0, sp1, st, dec):
        @pl.when(pl.program_id(0) == 0)
        def _():
            st[...] = jnp.zeros_like(st)
            _store_decay(lg_ref, dec)

        refs = ((q0, k0, v0, o0, sp0), (q1, k1, v1, o1, sp1))
        chains = [(dirn, h, slice(h * HEAD_DIM, (h + 1) * HEAD_DIM)) for dirn in (0, 1) for h in range(HEADS)]
        scores, cross, update = [], [], []
        for dirn, h, sl in chains:
            q_ref, k_ref, v_ref, _, sp_ref = refs[dirn]
            q, k, v = q_ref[:, sl], k_ref[:, sl], v_ref[:, sl]
            sp = st[dirn, h]
            sp_ref[h] = sp
            scores.append(_dot_nt(q, k))
            cross.append(_dot(q * dec[dirn, h, 1], sp))
            update.append(_dot_tn(k * dec[dirn, h, 2], v))
        masked = [(a * dec[dirn, h, 0]).astype(BF16) for a, (dirn, h, _) in zip(scores, chains)]
        intra = [_dot(sc, refs[dirn][2][:, sl]) for sc, (dirn, h, sl) in zip(masked, chains)]
        for (dirn, h, sl), o_in, o_cr, upd in zip(chains, intra, cross, update):
            refs[dirn][3][:, sl] = o_in + o_cr
            st[dirn, h] = jnp.exp(lg_ref[dirn, h] * CHUNK) * st[dirn, h] + upd

    o_specs = [pl.BlockSpec((CHUNK, RET_W), lambda s, row=row: (row(s), 0)) for row in rows]
    state = pl.BlockSpec((None, HEADS, CHUNK, HEAD_DIM), lambda s: (s, 0, 0, 0))
    return _pcall_ride(
        body, ride, (lg,) + (qkv,) * 6, name="ret_fwd", grid=(nc,),
        in_specs=[pl.BlockSpec(memory_space=pltpu.SMEM)] + _ret_in_specs(rows),
        out_specs=(o_specs[0], o_specs[1], state, state),
        out_shape=(_sds((n, RET_W), F32),) * 2 + (_sds((nc, HEADS, CHUNK, HEAD_DIM), F32),) * 2,
        scratch_shapes=[pltpu.VMEM((2, HEADS, CHUNK, HEAD_DIM), F32), pltpu.VMEM((2, HEADS, 3, CHUNK, CHUNK), F32)],
        compiler_params=_cp())


def _ret_bwd(qkv, lg, do, s_prev, cc, ride=None):
    n = qkv.shape[0]
    nc = n // CHUNK
    rows = _ret_rows(cc, nc, lambda s: nc - 1 - s)

    def body(lg_ref, q0, k0, v0, q1, k1, v1, do0, do1, sp0, sp1, dq0, dk0, dv0, dq1, dk1, dv1, dlg_ref, dst, dec, relf):
        @pl.when(pl.program_id(0) == 0)
        def _():
            dst[...] = jnp.zeros_like(dst)
            dlg_ref[...] = jnp.zeros_like(dlg_ref)
            _store_decay(lg_ref, dec, relf)

        refs = ((q0, k0, v0, do0, sp0, dq0, dk0, dv0), (q1, k1, v1, do1, sp1, dq1, dk1, dv1))
        chains = [(dirn, h, slice(h * HEAD_DIM, (h + 1) * HEAD_DIM)) for dirn in (0, 1) for h in range(HEADS)]

        def tiles(dirn, sl):
            q_ref, k_ref, v_ref, do_ref = refs[dirn][:4]
            return q_ref[:, sl], k_ref[:, sl], v_ref[:, sl], do_ref[:, sl]

        a_s, g1_s, da_s, h1_s = [], [], [], []
        for dirn, h, sl in chains:
            q, k, v, dov = tiles(dirn, sl)
            a_s.append(_dot_nt(q, k))
            g1_s.append(_dot_nt(dov, refs[dirn][4][h]))
            da_s.append(_dot_nt(dov, v))
            h1_s.append(_dot_nt(v, dst[dirn, h]))
        da_s = [da * dec[dirn, h, 0] for da, (dirn, h, _) in zip(da_s, chains)]
        dq_s, dk_s, dv_s, ds_s = [], [], [], []
        for (dirn, h, sl), a, da in zip(chains, a_s, da_s):
            q, k, v, dov = tiles(dirn, sl)
            dq_s.append(_dot(da, k))
            dk_s.append(_dot_tn(da, q))
            dv_s.append(_dot_tn(a * dec[dirn, h, 0], dov) + _dot(k * dec[dirn, h, 2], dst[dirn, h]))
            ds_s.append(_dot_tn(q * dec[dirn, h, 1], dov))
        for (dirn, h, sl), a, g1, da, h1, dq2, dk2, dv, ds2 in zip(chains, a_s, g1_s, da_s, h1_s, dq_s, dk_s, dv_s, ds_s):
            q, k, _, _ = tiles(dirn, sl)
            dq_ref, dk_ref, dv_ref = refs[dirn][5:]
            sp, dsn = refs[dirn][4][h], dst[dirn, h]
            gc = jnp.exp(lg_ref[dirn, h] * CHUNK)
            dq_ref[:, sl] = (g1 * dec[dirn, h, 1] + dq2).astype(BF16)
            dk_ref[:, sl] = (dk2 + h1 * dec[dirn, h, 2]).astype(BF16)
            dv_ref[:, sl] = dv.astype(BF16)
            term = da * a * relf[dirn] + q * g1 * dec[dirn, h, 3] + k * h1 * dec[dirn, h, 4] + sp * dsn * (CHUNK * gc)
            dlg_ref[dirn * HEADS + h:dirn * HEADS + h + 1, 0:HEAD_DIM] += jnp.sum(term, axis=0, keepdims=True)
            dst[dirn, h] = gc * dsn + ds2

    wide = [pl.BlockSpec((CHUNK, RET_W), lambda s, row=row: (row(s), 0)) for row in rows]
    state = pl.BlockSpec((None, HEADS, CHUNK, HEAD_DIM), lambda s: (nc - 1 - s, 0, 0, 0))
    return _pcall_ride(
        body, ride, (lg,) + (qkv,) * 6 + (do, do, s_prev[0], s_prev[1]), name="ret_bwd", grid=(nc,),
        in_specs=[pl.BlockSpec(memory_space=pltpu.SMEM)] + _ret_in_specs(rows) + wide + [state, state],
        out_specs=(wide[0],) * 3 + (wide[1],) * 3 + (pl.BlockSpec((2 * HEADS, 8 * HEAD_DIM), lambda s: (0, 0)),),
        out_shape=(_sds((n, RET_W), BF16),) * 6 + (_sds((2 * HEADS, 8 * HEAD_DIM), F32),),
        scratch_shapes=[pltpu.VMEM((2, HEADS, CHUNK, HEAD_DIM), F32), pltpu.VMEM((2, HEADS, 5, CHUNK, CHUNK), F32),
                        pltpu.VMEM((2, CHUNK, CHUNK), F32)],
        compiler_params=_cp())


def _shift_rows(cur, prev8, next8, k, seg_start, seg_end):
    tm = cur.shape[0]
    rows = _rows_iota(cur.shape)
    if k < 0:
        out = pltpu.roll(cur, -k, 0)
        for j in range(-k):
            halo = jnp.where(seg_start, 0.0, prev8[SUB + k + j:SUB + k + j + 1, :])
            out = jnp.where(rows == j, halo, out)
    else:
        out = pltpu.roll(cur, tm - k, 0)
        for j in range(k):
            halo = jnp.where(seg_end, 0.0, next8[j:j + 1, :])
            out = jnp.where(rows == tm - k + j, halo, out)
    return out


def _seg_flags(t, cb, nb):
    return jnp.logical_or(t == 0, t == cb), jnp.logical_or(t == cb - 1, t == nb - 1)


def _halo_specs(tile_of, n_rows, col):
    per = TM // SUB
    return [pl.BlockSpec((TM, LRU_W), lambda s: (tile_of(s), col)),
            pl.BlockSpec((SUB, LRU_W), lambda s: (jnp.maximum(tile_of(s) * per - 1, 0), col)),
            pl.BlockSpec((SUB, LRU_W), lambda s: (jnp.minimum((tile_of(s) + 1) * per, n_rows // SUB - 1), col))]


def _lru_gates(xr, prev8, next8, seg_start, seg_end, cw_ref, cb_ref, wg_ref, bg_ref, sp_ref):
    xm1 = _shift_rows(xr, prev8, next8, -1, seg_start, seg_end)
    xp1 = _shift_rows(xr, prev8, next8, 1, seg_start, seg_end)
    xp2 = _shift_rows(xr, prev8, next8, 2, seg_start, seg_end)
    xc = cb_ref[...] + xm1 * cw_ref[0:1, :] + xr * cw_ref[1:2, :] + xp1 * cw_ref[2:3, :] + xp2 * cw_ref[3:4, :]
    pre = _dot(xc, wg_ref[...]) + bg_ref[...]
    r = _sigmoid(pre[:, :LRU_W])
    i = _sigmoid(pre[:, LRU_W:])
    la = (-LRU_C) * r * sp_ref[...]
    a = jnp.exp(la)
    th = jnp.tanh(la)
    sq = jnp.sqrt(-2.0 * th / (1.0 - th))
    return xc, r, i, a, sq


def _scan_tile(a, b, ascending, a_sc, b_sc, carry, out_ref):
    tm, w = a.shape
    nsub = tm // SUB
    a = a.reshape(nsub, SUB, w)
    b = b.reshape(nsub, SUB, w)
    r8 = lax.broadcasted_iota(jnp.int32, a.shape, 1)
    for k in (1, 2, 4):
        if ascending:
            m = r8 >= k
            a_s, b_s = pltpu.roll(a, k, 1), pltpu.roll(b, k, 1)
        else:
            m = r8 < SUB - k
            a_s, b_s = pltpu.roll(a, SUB - k, 1), pltpu.roll(b, SUB - k, 1)
        b = a * jnp.where(m, b_s, 0.0) + b
        a = a * jnp.where(m, a_s, 1.0)
    a_sc[...] = a.reshape(tm, w)
    b_sc[...] = b.reshape(tm, w)

    def step(j, c):
        off = pl.multiple_of((j if ascending else nsub - 1 - j) * SUB, SUB)
        hb = a_sc[pl.ds(off, SUB), :] * c + b_sc[pl.ds(off, SUB), :]
        out_ref[pl.ds(off, SUB), :] = hb
        last = hb[SUB - 1:SUB, :] if ascending else hb[0:1, :]
        return jnp.broadcast_to(last, c.shape)

    carry[...] = lax.fori_loop(0, nsub, step, carry[...], unroll=8)


def _lru_fwd(p, wg, bg, sp, cw, cbias, dirn, cb, ride=None):
    n = p.shape[0]
    nb = n // TM
    tile_of = lambda s: _tile_order(dirn, s, cb, nb)

    def body(x_ref, xp_ref, xn_ref, wg_ref, bg_ref, sp_ref, cw_ref, cb_ref, h_ref, cin_ref, carry, a_sc, b_sc):
        s = pl.program_id(0)

        @pl.when(s == 0)
        def _():
            carry[...] = jnp.zeros_like(carry)

        seg_start, seg_end = _seg_flags(tile_of(s), cb, nb)
        xc, r, i, a, sq = _lru_gates(x_ref[...], xp_ref[...], xn_ref[...], seg_start, seg_end,
                                     cw_ref, cb_ref, wg_ref, bg_ref, sp_ref)
        cin_ref[...] = carry[...]
        _scan_tile(a, sq * (i * xc), dirn == 0, a_sc, b_sc, carry, h_ref)

    full = lambda shape: pl.BlockSpec(shape, lambda s: (0,) * len(shape))
    return _pcall_ride(
        body, ride, (p, p, p, wg, bg, sp, cw, cbias), name=f"lru_fwd{dirn}", grid=(nb,),
        in_specs=_halo_specs(tile_of, n, COL_XR) + [full((LRU_W, 2 * LRU_W)), full((1, 2 * LRU_W)), full((1, LRU_W)),
                                               full((4, LRU_W)), full((1, LRU_W))],
        out_specs=(pl.BlockSpec((TM, LRU_W), lambda s: (tile_of(s), 0)),
                   pl.BlockSpec((None, SUB, LRU_W), lambda s: (tile_of(s), 0, 0))),
        out_shape=(_sds((n, LRU_W), F32), _sds((nb, SUB, LRU_W), F32)),
        scratch_shapes=[pltpu.VMEM((SUB, LRU_W), F32), pltpu.VMEM((TM, LRU_W), F32), pltpu.VMEM((TM, LRU_W), F32)],
        compiler_params=_cp())


def _lru_bwd(p, wg, bg, sp, cw, cbias, h, cin, dhs, dirn, cb, ride=None):
    n = p.shape[0]
    nb = n // TM
    tile_of = lambda s: _tile_order(dirn, nb - 1 - s, cb, nb)

    def body(x_ref, xp_ref, xn_ref, wg_ref, bg_ref, sp_ref, cw_ref, cb_ref, h_ref, cin_ref, dhs_ref,
             dxc_ref, dwd_ref, acc_ref, carry, a_sc, b_sc, mu_sc, dwg_ref):
        s = pl.program_id(0)

        @pl.when(s == 0)
        def _():
            carry[...] = jnp.zeros_like(carry)
            dwg_ref[...] = jnp.zeros_like(dwg_ref)
            acc_ref[...] = jnp.zeros_like(acc_ref)

        seg_start, seg_end = _seg_flags(tile_of(s), cb, nb)
        xc, r, i, a, sq = _lru_gates(x_ref[...], xp_ref[...], xn_ref[...], seg_start, seg_end,
                                     cw_ref, cb_ref, wg_ref, bg_ref, sp_ref)
        rows = _rows_iota(a.shape)
        hv = h_ref[...]
        dh = dhs_ref[...]
        mu_next = carry[0:1, :]
        _scan_tile(a, a * dh, dirn == 1, a_sc, b_sc, carry, mu_sc)
        mu = mu_sc[...]
        if dirn == 0:
            hprev = jnp.where(rows == 0, cin_ref[0:1, :], pltpu.roll(hv, 1, 0))
            lam = dh + jnp.where(rows == TM - 1, mu_next, pltpu.roll(mu, TM - 1, 0))
        else:
            hprev = jnp.where(rows == TM - 1, cin_ref[0:1, :], pltpu.roll(hv, TM - 1, 0))
            lam = dh + jnp.where(rows == 0, mu_next, pltpu.roll(mu, 1, 0))
        ds = lam * (i * xc)
        di = lam * (sq * xc)
        dla = lam * hprev * a - ds * (a * a) / jnp.maximum(sq, 1e-20)
        dpr = dla * ((-LRU_C) * sp_ref[...]) * r * (1.0 - r)
        dpi = di * i * (1.0 - i)
        dpre = jnp.concatenate([dpr, dpi], axis=1)
        dxc_ref[...] = lam * (sq * i) + _dot_nt(dpre, wg_ref[...])
        dwg_ref[...] += _dot_tn(xc, dpre)
        acc_ref[0:1, :] += jnp.sum(dpre, axis=0, keepdims=True)
        acc_ref[1:2, 0:LRU_W] += jnp.sum(dla * ((-LRU_C) * r), axis=0, keepdims=True)

        @pl.when(s == nb - 1)
        def _():
            low = lax.broadcasted_iota(jnp.int32, (LRU_BD, 2 * LRU_BD), 1) < LRU_BD
            for half in (0, LRU_W):
                for m in range(LRU_BLOCKS // 2):
                    lanes = slice(half + 2 * LRU_BD * m, half + 2 * LRU_BD * (m + 1))
                    even = dwg_ref[2 * m * LRU_BD:(2 * m + 1) * LRU_BD, lanes]
                    odd = dwg_ref[(2 * m + 1) * LRU_BD:(2 * m + 2) * LRU_BD, lanes]
                    dwd_ref[:, lanes] = jnp.where(low, even, odd)

    full = lambda shape: pl.BlockSpec(shape, lambda s: (0,) * len(shape))
    tile = pl.BlockSpec((TM, LRU_W), lambda s: (tile_of(s), 0))
    return _pcall_ride(
        body, ride, (p, p, p, wg, bg, sp, cw, cbias, h, cin, dhs), name=f"lru_bwd{dirn}", grid=(nb,),
        in_specs=_halo_specs(tile_of, n, COL_XR) + [full((LRU_W, 2 * LRU_W)), full((1, 2 * LRU_W)), full((1, LRU_W)),
                                               full((4, LRU_W)), full((1, LRU_W)), tile,
                                               pl.BlockSpec((None, SUB, LRU_W), lambda s: (tile_of(s), 0, 0)), tile],
        out_specs=(tile, full((LRU_BD, 2 * LRU_W)), full((8, 2 * LRU_W))),
        out_shape=(_sds((n, LRU_W), F32), _sds((LRU_BD, 2 * LRU_W), F32), _sds((8, 2 * LRU_W), F32)),
        scratch_shapes=[pltpu.VMEM((SUB, LRU_W), F32)] + [pltpu.VMEM((TM, LRU_W), F32)] * 3
        + [pltpu.VMEM((LRU_W, 2 * LRU_W), F32)],
        compiler_params=_cp())


def _assemble_dp(dqs, dks, dvs, dg, dgate, dxcs, p, cw, cosf, sins, cb, ride=None):
    n = p.shape[0]
    nb = n // TM
    tile_of = lambda s: s

    def body(dqf, dqb, dkf, dkb, dvf, dvb, dg_ref, dgate_ref, cf, pf, nf, cb_, pb, nb_, x_ref, xp_ref, xn_ref,
             cw_ref, cos_ref, sin_ref, dp_ref, acc_ref):
        s = pl.program_id(0)

        @pl.when(s == 0)
        def _():
            acc_ref[...] = jnp.zeros_like(acc_ref)

        seg_start, seg_end = _seg_flags(s, cb, nb)
        dq = dqf[...].astype(F32) + dqb[...].astype(F32)
        dk = dkf[...].astype(F32) + dkb[...].astype(F32)
        cosv, sinv = cos_ref[...], sin_ref[...]
        for h in range(HEADS):
            sl = slice(h * HEAD_DIM, (h + 1) * HEAD_DIM)
            sk = slice(RET_W + h * HEAD_DIM, RET_W + (h + 1) * HEAD_DIM)
            dp_ref[:, sl] = (dq[:, sl] * cosv + pltpu.roll(dq[:, sl] * sinv, HEAD_DIM // 2, 1)).astype(BF16)
            dp_ref[:, sk] = ((dk[:, sl] * cosv + pltpu.roll(dk[:, sl] * sinv, HEAD_DIM // 2, 1)) * K_SCALE).astype(BF16)
        dp_ref[:, 2 * RET_W:3 * RET_W] = (dvf[...].astype(F32) + dvb[...].astype(F32)).astype(BF16)
        dp_ref[:, 3 * RET_W:4 * RET_W] = dg_ref[...].astype(BF16)
        dxc = cf[...] + cb_[...]
        dprev = pf[...] + pb[...]
        dnext = nf[...] + nb_[...]
        dxr = (_shift_rows(dxc, dprev, dnext, 1, seg_start, seg_end) * cw_ref[0:1, :] + dxc * cw_ref[1:2, :]
               + _shift_rows(dxc, dprev, dnext, -1, seg_start, seg_end) * cw_ref[2:3, :]
               + _shift_rows(dxc, dprev, dnext, -2, seg_start, seg_end) * cw_ref[3:4, :])
        dp_ref[:, 4 * RET_W:4 * RET_W + LRU_W] = dxr.astype(BF16)
        dp_ref[:, 4 * RET_W + LRU_W:] = dgate_ref[...].astype(BF16)
        xr, xp, xn = x_ref[...], xp_ref[...], xn_ref[...]
        for j, k in enumerate((-1, 0, 1, 2)):
            xs = xr if k == 0 else _shift_rows(xr, xp, xn, k, seg_start, seg_end)
            acc_ref[j:j + 1, 0:LRU_W] += jnp.sum(dxc * xs, axis=0, keepdims=True)
        acc_ref[4:5, 0:LRU_W] += jnp.sum(dxc, axis=0, keepdims=True)

    t = pl.BlockSpec((TM, RET_W), lambda s: (s, 0))
    args = (dqs[0], dqs[1], dks[0], dks[1], dvs[0], dvs[1], dg, dgate, dxcs[0], dxcs[0], dxcs[0], dxcs[1], dxcs[1], dxcs[1],
            p, p, p, cw, cosf, sins)
    tab = pl.BlockSpec((TM, HEAD_DIM), lambda s: (s, 0))
    return _pcall_ride(
        body, ride, args, name="assemble_dp", grid=(nb,),
        in_specs=[t] * 8 + _halo_specs(tile_of, n, 0) * 2 + _halo_specs(tile_of, n, COL_XR)
        + [pl.BlockSpec((4, LRU_W), lambda s: (0, 0)), tab, tab],
        out_specs=(pl.BlockSpec((TM, 4 * RET_W + 2 * LRU_W), lambda s: (s, 0)), pl.BlockSpec((8, 2 * LRU_W), lambda s: (0, 0))),
        out_shape=(_sds((n, 4 * RET_W + 2 * LRU_W), BF16), _sds((8, 2 * LRU_W), F32)), compiler_params=_cp())


def _adamw(g, w, m, v):
    nm = ADAM_B1 * m + (1.0 - ADAM_B1) * g
    nv = ADAM_B2 * v + (1.0 - ADAM_B2) * (g * g)
    m_hat = nm / (1.0 - ADAM_B1 ** ADAM_STEP)
    v_hat = nv / (1.0 - ADAM_B2 ** ADAM_STEP)
    return (-ADAM_LR) * (m_hat / (jnp.sqrt(v_hat) + ADAM_EPS) + ADAM_WD * w), nm, nv


def _adam_many(items, name):
    n = len(items)

    def body(*refs):
        for i in range(n):
            g, w, m, v = (r[...] for r in refs[4 * i:4 * i + 4])
            for o_ref, val in zip(refs[4 * n + 3 * i:4 * n + 3 * i + 3], _adamw(g, w, m, v)):
                o_ref[...] = val

    out_shape = tuple(_sds(it[1].shape, F32) for it in items for _ in range(3))
    res = _pcall(body, name=name, out_shape=out_shape, compiler_params=_cp())(*[a for it in items for a in it])
    return [tuple(res[3 * i:3 * i + 3]) for i in range(n)]


def _sum_adam(parts_list, w, m, v, name):
    nparts, _, c = parts_list[0].shape
    r = w.shape[0]
    tr = min(parts_list[0].shape[1], 128)
    starts, o = [], 0
    for pa in parts_list:
        starts.append(o)
        o += pa.shape[1] // tr
    nseg = len(parts_list)

    def body(*refs):
        p_refs = refs[:nseg]
        w_ref, m_ref, v_ref, g_ref, d_ref, nm_ref, nv_ref = refs[nseg:]
        i = pl.program_id(0)
        for s, p_ref in enumerate(p_refs):
            end = starts[s + 1] if s + 1 < nseg else r // tr

            @pl.when(jnp.logical_and(i >= starts[s], i < end))
            def _():
                g = p_ref[0].astype(F32)
                for j in range(1, nparts):
                    g = g + p_ref[j].astype(F32)
                g_ref[...] = g
                d_ref[...], nm_ref[...], nv_ref[...] = _adamw(g, w_ref[...], m_ref[...], v_ref[...])

    def seg_spec(s):
        last = parts_list[s].shape[1] // tr - 1
        return pl.BlockSpec((nparts, tr, c), lambda i: (0, jnp.clip(i - starts[s], 0, last), 0))

    t = pl.BlockSpec((tr, c), lambda i: (i, 0))
    return _pcall(
        body, name=name, grid=(r // tr,),
        in_specs=[seg_spec(s) for s in range(nseg)] + [t, t, t],
        out_specs=(t, t, t, t), out_shape=(_sds((r, c), F32),) * 4, compiler_params=_cp(),
    )(*parts_list, w, m, v)


def _sum_parts(parts, name):
    nparts, r, c = parts.shape

    def body(p_ref, o_ref):
        g = p_ref[0]
        for j in range(1, nparts):
            g = g + p_ref[j]
        o_ref[...] = g

    return _pcall(body, name=name, out_shape=_sds((r, c), parts.dtype), compiler_params=_cp())(parts)


def _rot_tables(l_len, t_len):
    rows = t_len // GRID_W
    n_freq = HEAD_DIM // 4
    inv = ROPE_BASE ** (-jnp.arange(n_freq, dtype=F32) / n_freq)
    ang_r = jnp.arange(rows, dtype=F32)[:, None] * inv
    ang_c = jnp.arange(GRID_W, dtype=F32)[:, None] * inv
    cos = jnp.concatenate([jnp.repeat(jnp.cos(ang_r), GRID_W, axis=0), jnp.tile(jnp.cos(ang_c), (rows, 1))], axis=-1)
    sin = jnp.concatenate([jnp.repeat(jnp.sin(ang_r), GRID_W, axis=0), jnp.tile(jnp.sin(ang_c), (rows, 1))], axis=-1)
    cosf = jnp.concatenate([jnp.ones((l_len, HEAD_DIM), F32), jnp.concatenate([cos, cos], axis=-1)], axis=0)
    sins = jnp.concatenate([jnp.zeros((l_len, HEAD_DIM), F32), jnp.concatenate([-sin, sin], axis=-1)], axis=0)
    return cosf, sins


def _block_diag(w):
    eye = jnp.eye(LRU_BLOCKS, dtype=w.dtype)
    return (w[:, :, None, :] * eye[:, None, :, None]).reshape(LRU_W, LRU_W)


def _blocks_from_lanes(dwd_half):
    return dwd_half.reshape(LRU_BD, LRU_BLOCKS, LRU_BD).transpose(1, 0, 2)


def _silu(x):
    return x * jax.nn.sigmoid(x)


def kernel(x, c, ctx, c_ctx, w_ada, b_ada, norm1_g, norm2_g, w_in, ret_decay, conv_w, conv_b, lru_wa, lru_ba, lru_wx, lru_bx, lru_lambda, w_out, w_mlp1, w_mlp2, final_g, loss_target, m_c_ctx, m_w_ada, m_b_ada, m_norm1_g, m_norm2_g, m_w_in, m_ret_decay, m_conv_w, m_conv_b, m_lru_wa, m_lru_ba, m_lru_wx, m_lru_bx, m_lru_lambda, m_w_out, m_w_mlp1, m_w_mlp2, m_final_g, v_c_ctx, v_w_ada, v_b_ada, v_norm1_g, v_norm2_g, v_w_in, v_ret_decay, v_conv_w, v_conv_b, v_lru_wa, v_lru_ba, v_lru_wx, v_lru_bx, v_lru_lambda, v_w_out, v_w_mlp1, v_w_mlp2, v_final_g):
    t_len, d = x.shape[1], x.shape[2]
    l_len = ctx.shape[1]
    cb, cc = l_len // TM, l_len // CHUNK
    me = 4 * lax.axis_index("x") + 2 * lax.axis_index("y") + lax.axis_index("c")
    x2d, ctx2d, tgt2d = x[0], ctx[0], loss_target[0]
    ada_cols = w_ada.shape[2]
    wa2d = w_ada[0]

    sc_loc = conv_w.shape[2]
    pack_a = jnp.zeros((8, d), F32)
    pack_a = pack_a.at[0].set(_silu(c[0]))
    pack_a = pack_a.at[1, :4 * sc_loc].set(conv_w[0].reshape(-1))
    pack_a = pack_a.at[2, :2 * sc_loc].set(lru_ba[0].reshape(-1))
    pack_a = pack_a.at[3, :2 * sc_loc].set(lru_bx[0].reshape(-1))
    pack_a = pack_a.at[4, :2 * sc_loc].set(lru_lambda[0].reshape(-1))
    b_cols = lax.dynamic_slice(b_ada, (0, me * ada_cols), (1, ada_cols))
    all_a, s16, mod_parts = _mod_exchange(pack_a, _silu(c_ctx)[None, :], wa2d, b_cols)

    def unshard(row, k):
        return all_a[:, row, :k * sc_loc].reshape(N_DEV, k, sc_loc).transpose(1, 0, 2).reshape(k, N_DEV * sc_loc)

    conv_w_full = unshard(1, 4)
    ba_full, bx_full, lam_full = unshard(2, 2), unshard(3, 2), unshard(4, 2)

    mod_all = mod_parts.transpose(1, 0, 2).reshape(16, N_DEV * ada_cols)
    mod_me = lax.dynamic_slice(mod_all, (me, 0), (1, 6 * d)).reshape(6, d)
    mod_c = mod_all[8].reshape(6, d)
    modrows = jnp.concatenate([mod_c[0:2], mod_me], axis=0)

    lg = jax.nn.log_sigmoid(ret_decay[0])
    sp = jax.nn.softplus(-lam_full)
    wg = [jnp.concatenate([_block_diag(lru_wa[0, dd]), _block_diag(lru_wx[0, dd])], axis=1).astype(BF16) for dd in (0, 1)]
    bg = [jnp.concatenate([ba_full[dd], bx_full[dd]])[None, :] for dd in (0, 1)]
    cosf, sins = _rot_tables(l_len, t_len)

    (hn, hnt), win_g = _norm1_fwd(ctx2d, x2d, norm1_g, modrows, cb, ride=("gather", w_in[0].astype(BF16)))
    (qkv, p), w2_g = _mm_in(hn, win_g, cosf, sins, ride=("gather", w_mlp2[0].astype(BF16)))
    w2_g = w2_g.reshape(1, 4 * d, d)
    (o0, o1, sp0, sp1), w1_g = _ret_fwd(qkv, lg, cc, ride=("gather", w_mlp1[0].astype(BF16)))
    o, s_prev = [o0, o1], [sp0, sp1]
    (h0, cin0), wout_g = _lru_fwd(p, wg[0], bg[0], sp[0:1], conv_w_full, conv_b, 0, cb, ride=("gather", w_out[0].astype(BF16)))
    wout_g = wout_g.reshape(1, d, d)
    (h1, cin1), _ = _lru_fwd(p, wg[1], bg[1], sp[1:2], conv_w_full, conv_b, 1, cb)
    h, cin = [h0, h1], [cin0, cin1]
    mix = _mix_fwd(o[0], o[1], p, h[0], h[1], cb, t_len)
    y, x1, h2, h2t = _mm_nn(mix, wout_g, F32, "mm_out_norm2", tm=TL, epi=_res_norm2_epilogue(x2d, norm2_g, modrows, TL))
    r = _mm_nn(h2, w1_g, BF16, "mm_mlp1", relu_out=True, tm=TL)
    dx2, dz, dzt, facc = _mm_nn(r, w2_g, F32, "mm_mlp2_final", square_lhs=True, tm=TL, vmem_mb=58,
                                epi=_final_epilogue(x1, tgt2d, final_g[None, :], modrows, TL))

    du = _mm_nt(dz, w2_g, BF16, "mm_da2", relu_mul=r, tm=TL, vmem_mb=58)
    gw2_lo, gw2_hi = _mm_wgrad(dzt, r, "mm_dw2", w2_g.shape[1] // N_DEV, BF16, transpose_out=True, square_rhs=True,
                               halves=True)
    gw1_lo, gw1_hi = _mm_wgrad(h2t, du, "mm_dw1", w1_g.shape[2], BF16, halves=True)
    (dx1, dy, n2acc), gw2_lo_all = _mm_nt(du, w1_g, F32, "mm_dh2_norm2", ride=("a2a", gw2_lo), tm=TL, vmem_mb=60,
                                          epi=_bwd_norm2_epilogue(x1, dx2, y, norm2_g, modrows, TL))
    gwo = _mm_tn(mix, dy, "mm_dwout", False, d, BF16, 512).reshape(N_DEV, -1, d)
    (do, dg, dgate, dhs), gwo_all = _mm_nt(dy, wout_g, F32, "mm_dmix_mix", ride=("a2a", gwo),
                                           epi=_mix_bwd_epilogue(o[0], o[1], p, h[0], h[1], cb))
    dxcs, dwgs, laccs, rides, got = [], [], [], [("a2a", gw1_hi), ("a2a", gw2_hi)], []
    (dq0, dk0, dv0, dq1, dk1, dv1, dlg_lanes), gw1_lo_all = _ret_bwd(qkv, lg, do, s_prev, cc, ride=("a2a", gw1_lo))
    dqs, dks, dvs = [dq0, dq1], [dk0, dk1], [dv0, dv1]
    for dd in (0, 1):
        (dxc_, dwg_, lacc_), got_ = _lru_bwd(p, wg[dd], bg[dd], sp[dd:dd + 1], conv_w_full, conv_b, h[dd], cin[dd], dhs, dd, cb,
                                             ride=rides[dd])
        dxcs.append(dxc_); dwgs.append(dwg_); laccs.append(lacc_); got.append(got_)
    gw1_hi_all, gw2_hi_all = got
    gw1_all, gw2_all = [gw1_lo_all, gw1_hi_all], [gw2_lo_all, gw2_hi_all]
    (dp, cacc), _ = _assemble_dp(dqs, dks, dvs, dg, dgate, dxcs, p, conv_w_full, cosf, sins, cb)
    pack_b = jnp.concatenate([n2acc, facc, cacc, laccs[0], laccs[1], dlg_lanes, dwgs[0], dwgs[1]], axis=0)
    gwi, all_b = _mm_wgrad(hnt, dp, "mm_dwin", win_g.shape[2], BF16, ride=("gather", pack_b))
    (grad_x, n1acc), gwi_all = _mm_nt(dp, win_g, F32, "mm_dhn_norm1", ride=("a2a", gwi),
                                      epi=_bwd_norm1_epilogue(ctx2d, x2d, dx1, norm1_g, modrows, cb))
    all_n1 = _all_gather_small(n1acc, "gather_norm1_grads")
    tot = _sum_parts(all_b, "sum_small_grads")
    t_n1 = _sum_parts(all_n1, "sum_norm1_grads")
    t_n2, t_f, t_conv, t_dlg = tot[0:8], tot[8:16], tot[16:24, :LRU_W], tot[40:48, :HEAD_DIM]
    t_l = [tot[24:32], tot[32:40]]
    t_dwd = [tot[48:48 + LRU_BD], tot[48 + LRU_BD:48 + 2 * LRU_BD]]
    loss = (0.5 / d) * jnp.sum(t_f[2])
    t_wa = jnp.stack([_blocks_from_lanes(t_dwd[dd][:, :LRU_W]) for dd in (0, 1)])
    t_wx = jnp.stack([_blocks_from_lanes(t_dwd[dd][:, LRU_W:]) for dd in (0, 1)])
    t_ba = jnp.stack([t_l[dd][0, :LRU_W] for dd in (0, 1)])
    t_bx = jnp.stack([t_l[dd][0, LRU_W:] for dd in (0, 1)])
    t_sp = jnp.stack([t_l[dd][1, :LRU_W] for dd in (0, 1)])
    dm_rows = jnp.stack([all_n1[:, 2, :], all_n1[:, 3, :], all_b[:, 3, :], all_b[:, 0, :], all_b[:, 1, :], all_b[:, 9, :]],
                        axis=1).reshape(N_DEV, 6 * d)
    dm_c = jnp.concatenate([t_n1[0], t_n1[1], jnp.zeros((4 * d,), F32)])
    dm16 = jnp.zeros((16, 6 * d), F32).at[0:8].set(dm_rows).at[8].set(dm_c)
    g_b_ada = jnp.sum(dm16, axis=0)[None, :]
    dm_cols = lax.dynamic_slice(dm16, (0, me * ada_cols), (16, ada_cols))
    g_w_ada, ds16 = _ada_bwd(s16, dm_cols, wa2d)
    ds_all = _all_gather_small(ds16[8:16], "gather_dsilu")
    dsilu_cc = _sum_parts(ds_all, "sum_dsilu")[0]
    sg_cc = jax.nn.sigmoid(c_ctx)
    g_c_ctx = dsilu_cc * (sg_cc * (1.0 + c_ctx * (1.0 - sg_cc)))

    g_ret_decay = jnp.sum(t_dlg, axis=-1).reshape(2, HEADS) * jax.nn.sigmoid(-ret_decay[0])
    g_lambda_full = -t_sp * jax.nn.sigmoid(-lam_full)

    def my_cols(full):
        return lax.dynamic_slice(full, (0, me * sc_loc), (full.shape[0], sc_loc))

    small_g = dict(
        c_ctx=g_c_ctx[None], b_ada=g_b_ada, norm1_g=t_n1[4:5], norm2_g=t_n2[2:3], ret_decay=g_ret_decay,
        conv_w=my_cols(t_conv[0:4]), conv_b=t_conv[4:5], lru_wa=t_wa.reshape(-1, LRU_BD), lru_ba=my_cols(t_ba),
        lru_wx=t_wx.reshape(-1, LRU_BD), lru_bx=my_cols(t_bx), lru_lambda=my_cols(g_lambda_full), final_g=t_f[0:1])
    small = dict(
        c_ctx=(c_ctx, m_c_ctx, v_c_ctx), b_ada=(b_ada, m_b_ada, v_b_ada), norm1_g=(norm1_g, m_norm1_g, v_norm1_g),
        norm2_g=(norm2_g, m_norm2_g, v_norm2_g), ret_decay=(ret_decay, m_ret_decay, v_ret_decay),
        conv_w=(conv_w, m_conv_w, v_conv_w), conv_b=(conv_b, m_conv_b, v_conv_b), lru_wa=(lru_wa, m_lru_wa, v_lru_wa),
        lru_ba=(lru_ba, m_lru_ba, v_lru_ba), lru_wx=(lru_wx, m_lru_wx, v_lru_wx), lru_bx=(lru_bx, m_lru_bx, v_lru_bx),
        lru_lambda=(lru_lambda, m_lru_lambda, v_lru_lambda), final_g=(final_g, m_final_g, v_final_g))
    names = list(small)
    items = [(small_g[k],) + tuple(a.reshape(small_g[k].shape) for a in small[k]) for k in names]
    res = {}
    for k, it, (d_, m_, v_) in zip(names, items, _adam_many(items, "adam_small")):
        shape = small[k][0].shape
        res[k] = tuple(a.reshape(shape) for a in (it[0], d_, m_, v_))

    def big(parts, w, m, v, name):
        out = _sum_adam(parts, w[0], m[0], v[0], name)
        return tuple(a[None] for a in out)

    res["w_ada"] = big([g_w_ada[None]], w_ada, m_w_ada, v_w_ada, "adam_w_ada")
    res["w_in"] = big([gwi_all], w_in, m_w_in, v_w_in, "adam_w_in")
    res["w_out"] = big([gwo_all], w_out, m_w_out, v_w_out, "adam_w_out")
    res["w_mlp1"] = big(gw1_all, w_mlp1, m_w_mlp1, v_w_mlp1, "adam_w_mlp1")
    res["w_mlp2"] = big(gw2_all, w_mlp2, m_w_mlp2, v_w_mlp2, "adam_w_mlp2")

    order = ["c_ctx", "w_ada", "b_ada", "norm1_g", "norm2_g", "w_in", "ret_decay", "conv_w", "conv_b", "lru_wa", "lru_ba",
             "lru_wx", "lru_bx", "lru_lambda", "w_out", "w_mlp1", "w_mlp2", "final_g"]
    outs = [loss, grad_x[None]]
    for j in range(4):
        outs += [res[k][j] for k in order]
    return tuple(outs)
```

```python
import jax
import jax.numpy as jnp
from jax import lax
from jax.experimental import pallas as pl
from jax.experimental.pallas import tpu as pltpu

F32 = jnp.float32
BF16 = jnp.bfloat16
AXES = ("x", "y", "c")
N_DEV = 8
MESH = pl.DeviceIdType.MESH

HEADS = 4
HEAD_DIM = 128
CHUNK = 128
RET_W = HEADS * HEAD_DIM
LRU_W = 512
LRU_BLOCKS = 8
LRU_BD = LRU_W // LRU_BLOCKS
LRU_C = 8.0
EPS = 1e-6
K_SCALE = HEAD_DIM ** -0.5
ROPE_BASE = 10000.0
GRID_W = 64
TM = 256
TL = 512
SUB = 8

ADAM_LR = 0.001
ADAM_B1 = 0.9
ADAM_B2 = 0.999
ADAM_EPS = 1e-08
ADAM_WD = 0.01
ADAM_STEP = 10

COL_G, COL_XR, COL_GATE = 0, 1, 2

R_CSH1, R_CSC1, R_SH1, R_SC1, R_G1, R_SH2, R_SC2, R_G2 = range(8)


def _pcall(body, **kw):
    return pl.pallas_call(body, **kw)


def _cp(vmem_mb=48):
    return pltpu.CompilerParams(vmem_limit_bytes=vmem_mb << 20)


def _sds(shape, dtype):
    return jax.ShapeDtypeStruct(shape, dtype)


def _dot(a, b):
    return jnp.dot(a.astype(BF16), b.astype(BF16), preferred_element_type=F32)


def _dot_nt(a, b):
    return lax.dot_general(a.astype(BF16), b.astype(BF16), (((1,), (1,)), ((), ())), preferred_element_type=F32)


def _dot_tn(a, b):
    return lax.dot_general(a.astype(BF16), b.astype(BF16), (((0,), (0,)), ((), ())), preferred_element_type=F32)


def _sigmoid(x):
    return 0.5 * jnp.tanh(0.5 * x) + 0.5


def _gelu(x):
    return 0.5 * x * (1.0 + jnp.tanh(0.7978845608028654 * (x + 0.044715 * x * x * x)))


def _dgelu(x):
    t = jnp.tanh(0.7978845608028654 * (x + 0.044715 * x * x * x))
    return 0.5 * (1.0 + t) + 0.5 * x * (1.0 - t * t) * 0.7978845608028654 * (1.0 + 3.0 * 0.044715 * x * x)


def _rows_iota(shape):
    return lax.broadcasted_iota(jnp.int32, shape, 0)


def _tile_order(dirn, s, cb, nb):
    if dirn == 0:
        return s
    return jnp.where(s < cb, cb - 1 - s, nb - 1 - (s - cb))


_SEMS = [pltpu.SemaphoreType.DMA((7,)), pltpu.SemaphoreType.DMA((7,)), pltpu.SemaphoreType.DMA(())]
_ANY = pl.BlockSpec(memory_space=pl.ANY)


def _gather_copies(x_ref, out_ref, send_sems, recv_sems, local_sem):
    mx, my, mc = lax.axis_index("x"), lax.axis_index("y"), lax.axis_index("c")
    me, sibling = (mx, my, mc), (mx, my, 1 - mc)
    chips = [(1 - mx, my), (mx, 1 - my), (1 - mx, 1 - my)]

    def slot(px, py, pc):
        return out_ref.at[4 * px + 2 * py + pc]

    def copy(k, block, to, src=None):
        return pltpu.make_async_remote_copy(
            src_ref=slot(*block) if src is None else src, dst_ref=slot(*block),
            send_sem=send_sems.at[k], recv_sem=recv_sems.at[k], device_id=to, device_id_type=MESH)

    mine = pltpu.make_async_copy(x_ref, slot(*me), local_sem)
    first = [copy(0, me, sibling, src=x_ref)] + [copy(1 + j, me, (*chip, mc), src=x_ref) for j, chip in enumerate(chips)]
    passed = [copy(4 + j, (*chip, mc), sibling) for j, chip in enumerate(chips)]
    recv_ici = [copy(1 + j, (*chip, mc), me) for j, chip in enumerate(chips)]
    recv_d2d = [copy(0, sibling, me)] + [copy(4 + j, (*chip, 1 - mc), me) for j, chip in enumerate(chips)]
    return mine, first, passed, recv_ici, recv_d2d


def _gather_start(*refs):
    mine, first, _, _, _ = _gather_copies(*refs)
    mine.start()
    for cp in first:
        cp.start()


def _gather_pass_on(*refs):
    _, _, passed, recv_ici, _ = _gather_copies(*refs)
    for landed, onward in zip(recv_ici, passed):
        landed.wait_recv()
        onward.start()


def _gather_finish(*refs):
    mine, first, passed, _, recv_d2d = _gather_copies(*refs)
    for landed in recv_d2d:
        landed.wait_recv()
    for cp in first + passed:
        cp.wait_send()
    mine.wait()


def _a2a_copies(g_ref, out_ref, send_sems, recv_sems, local_sem):
    mx, my, mc = lax.axis_index("x"), lax.axis_index("y"), lax.axis_index("c")
    me = 4 * mx + 2 * my + mc
    mine = pltpu.make_async_copy(g_ref.at[me], out_ref.at[me], local_sem)
    copies = []
    for k in range(1, N_DEV):
        px = 1 - mx if (k >> 2) & 1 else mx
        py = 1 - my if (k >> 1) & 1 else my
        pc = 1 - mc if k & 1 else mc
        copies.append(pltpu.make_async_remote_copy(
            src_ref=g_ref.at[4 * px + 2 * py + pc], dst_ref=out_ref.at[me],
            send_sem=send_sems.at[k - 1], recv_sem=recv_sems.at[k - 1],
            device_id=(px, py, pc), device_id_type=MESH))
    return mine, copies


def _a2a_start(*refs):
    mine, copies = _a2a_copies(*refs)
    mine.start()
    for cp in copies:
        cp.start()


def _a2a_finish(*refs):
    mine, copies = _a2a_copies(*refs)
    for cp in copies:
        cp.wait()
    mine.wait()


_EXCHANGES = {"gather": (_gather_start, _gather_finish), "a2a": (_a2a_start, _a2a_finish)}
PASS_ON_LEAD = 3


def _exchange_shape(kind, src):
    return _sds((N_DEV,) + src.shape if kind == "gather" else src.shape, src.dtype)


def _pcall_ride(body, ride, args, *, name, grid, in_specs, out_specs, out_shape, scratch_shapes=(), compiler_params=None):
    if ride is None:
        out = _pcall(body, name=name, grid=grid, in_specs=in_specs, out_specs=out_specs, out_shape=out_shape,
                     scratch_shapes=list(scratch_shapes), compiler_params=compiler_params)(*args)
        return out, None
    kind, src = ride
    start, finish = _EXCHANGES[kind]
    single = not isinstance(out_shape, (tuple, list))
    out_specs_t = (out_specs,) if single else tuple(out_specs)
    out_shape_t = (out_shape,) if single else tuple(out_shape)
    n_in, n_out, n_sc = len(in_specs), len(out_shape_t), len(scratch_shapes)

    def wrapped(*refs):
        ins, src_ref = refs[:n_in], refs[n_in]
        outs, dst_ref = refs[n_in + 1:n_in + 1 + n_out], refs[n_in + 1 + n_out]
        scratch = refs[n_in + 2 + n_out:n_in + 2 + n_out + n_sc]
        sems = refs[n_in + 2 + n_out + n_sc:]
        first = pl.program_id(0) == 0
        last = pl.program_id(0) == grid[0] - 1
        for ax in range(1, len(grid)):
            first = jnp.logical_and(first, pl.program_id(ax) == 0)
            last = jnp.logical_and(last, pl.program_id(ax) == grid[ax] - 1)

        @pl.when(first)
        def _():
            start(src_ref, dst_ref, *sems)

        body(*ins, *outs, *scratch)

        if kind == "gather":
            @pl.when(pl.program_id(0) == max(grid[0] - PASS_ON_LEAD, 0))
            def _():
                _gather_pass_on(src_ref, dst_ref, *sems)

        @pl.when(last)
        def _():
            finish(src_ref, dst_ref, *sems)

    res = _pcall(wrapped, name=name, grid=grid, in_specs=list(in_specs) + [_ANY], out_specs=out_specs_t + (_ANY,),
                 out_shape=out_shape_t + (_exchange_shape(kind, src),),
                 scratch_shapes=list(scratch_shapes) + list(_SEMS), compiler_params=compiler_params)(*args, src)
    return (res[0] if single else tuple(res[:-1])), res[-1]


class _Epilogue:
    def __init__(self, fn, args, in_specs, out_specs, out_shape, steps=None, lhs_map=None, delay=0):
        self.fn, self.args, self.in_specs, self.out_specs, self.out_shape = fn, tuple(args), list(in_specs), out_specs, out_shape
        self.steps, self.lhs_map, self.delay = steps, lhs_map, delay


def _mm_nn(a, w, out_dtype, name, square_lhs=False, relu_out=False, ride=None, tm=TM, epi=None, vmem_mb=48):
    m, k = a.shape
    nb, _, bn = w.shape
    tm = min(tm, m)
    delay = 0 if epi is None else epi.delay
    steps = m // tm

    def body(*refs):
        a_ref, w_ref = refs[:2]
        av = a_ref[...]
        if square_lhs:
            av = av * av
        if delay:
            assert nb == 1
            held = refs[-1]

            @pl.when(pl.program_id(0) == 0)
            def _():
                held[...] = jnp.zeros_like(held)

            fresh = jnp.dot(av, w_ref[0], preferred_element_type=F32)
            epi.fn(held[...], slice(0, tm), *refs[2:-1])
            held[...] = fresh
            return
        if epi is not None:
            assert nb == 1
            epi.fn(jnp.dot(av, w_ref[0], preferred_element_type=F32), slice(0, tm), *refs[2:])
            return
        for j in range(nb):
            r = jnp.dot(av, w_ref[j], preferred_element_type=F32)
            if relu_out:
                r = jnp.maximum(r, 0.0)
            refs[2][:, j * bn:(j + 1) * bn] = r.astype(out_dtype)

    lhs_map = (lambda i: (i, 0)) if not delay else (lambda i: (jnp.minimum(i, steps - 1), 0))
    in_specs = [pl.BlockSpec((tm, k), lhs_map), pl.BlockSpec((nb, k, bn), lambda i: (0, 0, 0))]
    scratch = []
    if epi is None:
        args, out_specs, out_shape = (a, w), pl.BlockSpec((tm, nb * bn), lambda i: (i, 0)), _sds((m, nb * bn), out_dtype)
    else:
        args, out_specs, out_shape = (a, w) + epi.args, epi.out_specs, epi.out_shape
        in_specs += epi.in_specs
        if delay:
            scratch.append(pltpu.VMEM((tm, bn), F32))
    out, ex = _pcall_ride(body, ride, args, name=name, grid=(steps + delay,), in_specs=in_specs, out_specs=out_specs,
                          out_shape=out_shape, scratch_shapes=scratch, compiler_params=_cp(vmem_mb))
    return out if ride is None else (out, ex)


def _mm_nt(dy, w, out_dtype, name, relu_mul=None, ride=None, tm=TM, epi=None, vmem_mb=48):
    m = dy.shape[0]
    nb, k, bn = w.shape
    tm = min(tm, m)
    n_extra = (0 if relu_mul is None else 1) + (0 if epi is None else len(epi.args))

    delay = 0 if epi is None else epi.delay

    def body(*refs):
        dy_ref, w_ref = refs[:2]
        scratch = refs[-1 - delay:]
        extra, outs, wt = refs[2:2 + n_extra], refs[2 + n_extra:-1 - delay], scratch[0]

        @pl.when(pl.program_id(0) == 0)
        def _():
            for j in range(nb):
                wt[j * bn:(j + 1) * bn, :] = w_ref[j].T
            if delay:
                scratch[1][...] = jnp.zeros_like(scratch[1])

        if delay:
            held = scratch[1]
            fresh = jnp.dot(dy_ref[...], wt[...], preferred_element_type=F32)
            epi.fn(held[...], slice(0, tm), *extra, *outs)
            held[...] = fresh
            return
        acc = jnp.dot(dy_ref[...], wt[...], preferred_element_type=F32)
        if epi is not None:
            epi.fn(acc, slice(0, tm), *extra, *outs)
            return
        if relu_mul is not None:
            acc = acc * (2.0 * extra[0][...].astype(F32))
        outs[0][...] = acc.astype(out_dtype)

    steps = m // tm if epi is None or epi.steps is None else epi.steps
    lhs_tile = (lambda i: (i, 0)) if epi is None or epi.lhs_map is None else epi.lhs_map
    lhs_map = lhs_tile if not delay else (lambda i: lhs_tile(jnp.minimum(i, steps - 1)))
    in_specs = [pl.BlockSpec((tm, nb * bn), lhs_map), pl.BlockSpec((nb, k, bn), lambda i: (0, 0, 0))]
    args = [dy, w]
    if relu_mul is not None:
        in_specs.append(pl.BlockSpec((tm, k), lambda i: (i, 0)))
        args.append(relu_mul)
    scratch = [pltpu.VMEM((nb * bn, k), BF16)]
    if epi is None:
        out_specs, out_shape = pl.BlockSpec((tm, k), lambda i: (i, 0)), _sds((m, k), out_dtype)
    else:
        args += list(epi.args)
        in_specs += epi.in_specs
        out_specs, out_shape = epi.out_specs, epi.out_shape
        if delay:
            scratch.append(pltpu.VMEM((tm, k), F32))
    out, ex = _pcall_ride(
        body, ride, args, name=name, grid=(steps + delay,), in_specs=in_specs, out_specs=out_specs, out_shape=out_shape,
        scratch_shapes=scratch, compiler_params=_cp(vmem_mb))
    return out if ride is None else (out, ex)


def _mm_tn(a, b, name, col_blocks, block, out_dtype, tm, square_lhs=False):
    m, k = a.shape
    nn = b.shape[1]
    steps = m // tm
    if col_blocks:
        nblk, acc_shape = nn // block, (k, block)
        a_spec = pl.BlockSpec((tm, k), lambda j, s: (s, 0))
        b_spec = pl.BlockSpec((tm, block), lambda j, s: (s, j))
    else:
        nblk, acc_shape = k // block, (block, nn)
        a_spec = pl.BlockSpec((tm, block), lambda j, s: (s, j))
        b_spec = pl.BlockSpec((tm, nn), lambda j, s: (s, 0))

    def body(a_ref, b_ref, o_ref, acc):
        s = pl.program_id(1)

        @pl.when(s == 0)
        def _():
            acc[...] = jnp.zeros_like(acc)

        av = a_ref[...]
        if square_lhs:
            av = av.astype(F32)
            av = (av * av).astype(BF16)
        acc[...] += _dot_tn(av, b_ref[...])

        @pl.when(s == steps - 1)
        def _():
            o_ref[...] = acc[...].astype(out_dtype)

    return _pcall(
        body, name=name, grid=(nblk, steps), in_specs=[a_spec, b_spec],
        out_specs=pl.BlockSpec((None,) + acc_shape, lambda j, s: (j, 0, 0)),
        out_shape=_sds((nblk,) + acc_shape, out_dtype),
        scratch_shapes=[pltpu.VMEM(acc_shape, F32)], compiler_params=_cp(),
    )(a, b)


def _mm_wgrad(at, b, name, bn, out_dtype, transpose_out=False, square_rhs=False, halves=False, ride=None):
    k, m = at.shape
    nblk = b.shape[1] // bn
    rows, cols = (bn, k) if transpose_out else (k, bn)
    nout = 2 if halves else 1
    per = rows // nout

    def body(a_ref, b_ref, *o_refs):
        bv = b_ref[...]
        if square_rhs:
            bv = bv * bv
        r = jnp.dot(a_ref[...], bv, preferred_element_type=F32)
        r = (r.T if transpose_out else r).astype(out_dtype)
        for i, o_ref in enumerate(o_refs):
            o_ref[...] = r[i * per:(i + 1) * per, :]

    out, ex = _pcall_ride(
        body, ride, (at, b), name=name, grid=(nblk,),
        in_specs=[pl.BlockSpec((k, m), lambda j: (0, 0)), pl.BlockSpec((m, bn), lambda j: (0, j))],
        out_specs=tuple(pl.BlockSpec((None, per, cols), lambda j: (j, 0, 0)) for _ in range(nout)),
        out_shape=tuple(_sds((nblk, per, cols), out_dtype) for _ in range(nout)), compiler_params=_cp())
    out = out if halves else out[0]
    return out if ride is None else (out, ex)


def _mm_in(hn, w, cosf, sins, ride):
    m, k = hn.shape
    nb, _, bn = w.shape

    def body(a_ref, w_ref, c_ref, s_ref, qkv_ref, rest_ref, pt):
        av = a_ref[...]
        for j in range(nb):
            pt[:, j * bn:(j + 1) * bn] = jnp.dot(av, w_ref[j], preferred_element_type=F32)
        cf, ss = c_ref[...], s_ref[...]
        for h in range(HEADS):
            sq = slice(h * HEAD_DIM, (h + 1) * HEAD_DIM)
            sk = slice(RET_W + h * HEAD_DIM, RET_W + (h + 1) * HEAD_DIM)
            qkv_ref[:, sq] = _rot(pt[:, sq], cf, ss).astype(BF16)
            qkv_ref[:, sk] = (_rot(pt[:, sk], cf, ss) * K_SCALE).astype(BF16)
        qkv_ref[:, 2 * RET_W:] = pt[:, 2 * RET_W:3 * RET_W].astype(BF16)
        rest_ref[...] = pt[:, 3 * RET_W:]

    tab = pl.BlockSpec((TM, HEAD_DIM), lambda i: (i, 0))
    wide = pl.BlockSpec((TM, 3 * RET_W), lambda i: (i, 0))
    return _pcall_ride(
        body, ride, (hn, w, cosf, sins), name="mm_in", grid=(m // TM,),
        in_specs=[pl.BlockSpec((TM, k), lambda i: (i, 0)), pl.BlockSpec((nb, k, bn), lambda i: (0, 0, 0)), tab, tab],
        out_specs=(wide, wide), out_shape=(_sds((m, 3 * RET_W), BF16), _sds((m, nb * bn - 3 * RET_W), F32)),
        scratch_shapes=[pltpu.VMEM((TM, nb * bn), F32)], compiler_params=_cp())


def _exchange_in_vmem(src_ref, dst_ref, send, recv):
    mx, my, mc = lax.axis_index("x"), lax.axis_index("y"), lax.axis_index("c")
    me = 4 * mx + 2 * my + mc
    copies = []
    for k in range(1, N_DEV):
        peer = (1 - mx if (k >> 2) & 1 else mx, 1 - my if (k >> 1) & 1 else my, 1 - mc if k & 1 else mc)
        copies.append(pltpu.make_async_remote_copy(
            src_ref=src_ref, dst_ref=dst_ref.at[me], send_sem=send.at[k - 1], recv_sem=recv.at[k - 1],
            device_id=peer, device_id_type=MESH))
        copies[-1].start()
    dst_ref[me] = src_ref[...]
    for cp in copies:
        cp.wait()


def _mod_exchange(pack_a, silu_cc, w_ada, b_cols):
    d, cols = pack_a.shape[1], w_ada.shape[1]

    def body(pa_ref, scc_ref, w_ref, b_ref, alla_ref, s16_ref, modp_ref, mp, send1, recv1, send2, recv2):
        exchange = _exchange_in_vmem
        exchange(pa_ref, alla_ref, send1, recv1)
        s16_ref[...] = jnp.zeros_like(s16_ref)
        for j in range(N_DEV):
            s16_ref[j:j + 1, :] = alla_ref[j, 0:1, :]
        s16_ref[N_DEV:N_DEV + 1, :] = scc_ref[...]
        mp[...] = _dot(s16_ref[...], w_ref[...]) + b_ref[...]
        exchange(mp, modp_ref, send2, recv2)

    vmem = pl.BlockSpec(memory_space=pltpu.VMEM)
    sem7 = pltpu.SemaphoreType.DMA((N_DEV - 1,))
    return _pcall(
        body, name="mod_exchange", in_specs=[vmem] * 4, out_specs=(vmem,) * 3,
        out_shape=(_sds((N_DEV,) + pack_a.shape, F32), _sds((2 * N_DEV, d), F32), _sds((N_DEV, 2 * N_DEV, cols), F32)),
        scratch_shapes=[pltpu.VMEM((2 * N_DEV, cols), F32), sem7, sem7, sem7, sem7], compiler_params=_cp(),
    )(pack_a, silu_cc, w_ada, b_cols)


def _ada_exchange(n1acc, ab16, s16, w_ada):
    d, cols = w_ada.shape
    rows16 = s16.shape[0]

    def body(n1_ref, ab_ref, s_ref, w_ref, tn1_ref, gb_ref, gw_ref, dsil_ref, alln1_ref, dsall_ref,
             dm16, dmc, ds, send1, recv1, send2, recv2):
        me = 4 * lax.axis_index("x") + 2 * lax.axis_index("y") + lax.axis_index("c")
        _exchange_in_vmem(n1_ref, alln1_ref, send1, recv1)
        tot = alln1_ref[0]
        for j in range(1, N_DEV):
            tot = tot + alln1_ref[j]
        tn1_ref[...] = tot
        dm16[...] = jnp.zeros_like(dm16)
        for j in range(N_DEV):
            for chunk, row in enumerate((alln1_ref[j, 2:3, :], alln1_ref[j, 3:4, :], ab_ref[j, 3:4, :],
                                         ab_ref[j, 0:1, :], ab_ref[j, 1:2, :], ab_ref[j, 9:10, :])):
                dm16[j:j + 1, chunk * d:(chunk + 1) * d] = row
        dm16[N_DEV:N_DEV + 1, 0:d] = tot[0:1, :]
        dm16[N_DEV:N_DEV + 1, d:2 * d] = tot[1:2, :]
        gb_ref[...] = jnp.sum(dm16[...], axis=0, keepdims=True)
        for j in range(N_DEV):
            @pl.when(me == j)
            def _():
                dmc[...] = dm16[:, j * cols:(j + 1) * cols]
        gw_ref[...] = _dot_tn(s_ref[...], dmc[...])
        ds[...] = _dot_nt(dmc[...], w_ref[...])
        _exchange_in_vmem(ds.at[pl.ds(N_DEV, N_DEV)], dsall_ref, send2, recv2)
        tot2 = dsall_ref[0]
        for j in range(1, N_DEV):
            tot2 = tot2 + dsall_ref[j]
        dsil_ref[...] = tot2

    vmem = pl.BlockSpec(memory_space=pltpu.VMEM)
    sem7 = pltpu.SemaphoreType.DMA((N_DEV - 1,))
    out = _pcall(
        body, name="ada_exchange", in_specs=[vmem] * 4, out_specs=(vmem,) * 6,
        out_shape=(_sds((N_DEV, d), F32), _sds((1, 6 * d), F32), _sds((d, cols), F32), _sds((N_DEV, d), F32),
                   _sds((N_DEV, N_DEV, d), F32), _sds((N_DEV, N_DEV, d), F32)),
        scratch_shapes=[pltpu.VMEM((rows16, 6 * d), F32), pltpu.VMEM((rows16, cols), F32), pltpu.VMEM((rows16, d), F32),
                        sem7, sem7, sem7, sem7], compiler_params=_cp(),
    )(n1acc, ab16, s16, w_ada)
    return out[:4]


def _norm1_fwd(ctx, x, g, modrows, cb, ride=None):
    l_len, d = ctx.shape
    nb = (l_len + x.shape[0]) // TM

    def body(ctx_ref, x_ref, g_ref, m_ref, o_ref, ot_ref):
        is_ctx = pl.program_id(0) < cb
        xin = jnp.where(is_ctx, ctx_ref[...], x_ref[...])
        sh = jnp.where(is_ctx, m_ref[R_CSH1:R_CSH1 + 1, :], m_ref[R_SH1:R_SH1 + 1, :])
        sc = jnp.where(is_ctx, m_ref[R_CSC1:R_CSC1 + 1, :], m_ref[R_SC1:R_SC1 + 1, :])
        ms = jnp.mean(xin * xin, axis=-1, keepdims=True)
        n = xin * lax.rsqrt(ms + EPS) * g_ref[...]
        hn = n * (1.0 + sc) + sh
        o_ref[...] = hn.astype(BF16)
        ot_ref[...] = hn.T.astype(BF16)

    return _pcall_ride(
        body, ride, (ctx, x, g, modrows), name="norm1_fwd", grid=(nb,),
        in_specs=[pl.BlockSpec((TM, d), lambda i: (jnp.minimum(i, cb - 1), 0)),
                  pl.BlockSpec((TM, d), lambda i: (jnp.maximum(i - cb, 0), 0)),
                  pl.BlockSpec((1, d), lambda i: (0, 0)), pl.BlockSpec((8, d), lambda i: (0, 0))],
        out_specs=(pl.BlockSpec((TM, d), lambda i: (i, 0)), pl.BlockSpec((d, TM), lambda i: (0, i))),
        out_shape=(_sds((nb * TM, d), BF16), _sds((d, nb * TM), BF16)), compiler_params=_cp())


def _mix_fwd(o_f, o_b, p, h_f, h_b, cb, t_len):
    def body(of_ref, ob_ref, g_ref, gate_ref, hf_ref, hb_ref, mix_ref):
        o = of_ref[...] + ob_ref[...]
        g = g_ref[...]
        sg = g * _sigmoid(g)
        for hh in range(HEADS):
            sl = slice(hh * HEAD_DIM, (hh + 1) * HEAD_DIM)
            oh = o[:, sl]
            yc = oh - jnp.mean(oh, axis=-1, keepdims=True)
            var = jnp.mean(yc * yc, axis=-1, keepdims=True)
            mix_ref[:, sl] = (sg[:, sl] * (yc * lax.rsqrt(var + EPS))).astype(BF16)
        mix_ref[:, RET_W:] = ((hf_ref[...] + hb_ref[...]) * _gelu(gate_ref[...])).astype(BF16)

    row = lambda i: (i + cb, 0)
    return _pcall(
        body, name="mix_fwd", grid=(t_len // TM,),
        in_specs=[pl.BlockSpec((TM, RET_W), row), pl.BlockSpec((TM, RET_W), row),
                  pl.BlockSpec((TM, RET_W), lambda i: (i + cb, COL_G)), pl.BlockSpec((TM, LRU_W), lambda i: (i + cb, COL_GATE)),
                  pl.BlockSpec((TM, LRU_W), row), pl.BlockSpec((TM, LRU_W), row)],
        out_specs=pl.BlockSpec((TM, RET_W + LRU_W), lambda i: (i, 0)),
        out_shape=_sds((t_len, RET_W + LRU_W), BF16), compiler_params=_cp(),
    )(o_f, o_b, p, p, h_f, h_b)


def _res_norm2_epilogue(x, g, modrows, tm):
    t_len, d = x.shape
    tm = min(tm, t_len)

    def fn(y, rows, x_ref, g_ref, m_ref, y_ref, x1_ref, h2_ref, h2t_ref):
        y_ref[rows, :] = y
        x1 = x_ref[rows, :] + m_ref[R_G1:R_G1 + 1, :] * y
        ms = jnp.mean(x1 * x1, axis=-1, keepdims=True)
        n = x1 * lax.rsqrt(ms + EPS) * g_ref[...]
        x1_ref[rows, :] = x1
        h2 = n * (1.0 + m_ref[R_SC2:R_SC2 + 1, :]) + m_ref[R_SH2:R_SH2 + 1, :]
        h2_ref[rows, :] = h2.astype(BF16)
        h2t_ref[:, rows] = h2.T.astype(BF16)

    t = pl.BlockSpec((tm, d), lambda i: (i, 0))
    return _Epilogue(
        fn, (x, g, modrows),
        in_specs=[t, pl.BlockSpec((1, d), lambda i: (0, 0)), pl.BlockSpec((8, d), lambda i: (0, 0))],
        out_specs=(t, t, t, pl.BlockSpec((d, tm), lambda i: (0, i))),
        out_shape=(_sds((t_len, d), F32), _sds((t_len, d), F32), _sds((t_len, d), BF16), _sds((d, t_len), BF16)))


def _tile(i):
    return jnp.maximum(i - 1, 0)


def _zero_at_start(acc_ref, rows):
    @pl.when(pl.program_id(0) == 0)
    def _():
        acc_ref[...] = jnp.zeros_like(acc_ref)


def _final_epilogue(x1, target, fg, modrows, tm):
    t_len, d = x1.shape

    def fn(z, rows, x1_ref, t_ref, fg_ref, m_ref, dx2_ref, dz_ref, dzt_ref, acc_ref):
        _zero_at_start(acc_ref, rows)
        g2 = m_ref[R_G2:R_G2 + 1, :]
        x2 = x1_ref[rows, :] + g2 * z
        rstd = lax.rsqrt(jnp.mean(x2 * x2, axis=-1, keepdims=True) + EPS)
        xh = x2 * rstd
        fg = fg_ref[...]
        e = xh * fg - t_ref[rows, :]
        dy = e * (1.0 / d)
        dxh = dy * fg
        dx2 = rstd * (dxh - xh * jnp.mean(dxh * xh, axis=-1, keepdims=True))
        dx2_ref[rows, :] = dx2
        dz = g2 * dx2
        dz_ref[rows, :] = dz.astype(BF16)
        dzt_ref[:, rows] = dz.T.astype(BF16)
        acc_ref[0:1, :] += jnp.sum(dy * xh, axis=0, keepdims=True)
        acc_ref[1:2, :] += jnp.sum(dx2 * z, axis=0, keepdims=True)
        acc_ref[2:3, :] += jnp.sum(e * e, axis=0, keepdims=True)

    t = pl.BlockSpec((tm, d), lambda i: (i, 0))
    return _Epilogue(
        fn, (x1, target, fg, modrows),
        in_specs=[t, t, pl.BlockSpec((1, d), lambda i: (0, 0)), pl.BlockSpec((8, d), lambda i: (0, 0))],
        out_specs=(t, t, pl.BlockSpec((d, tm), lambda i: (0, i)), pl.BlockSpec((8, d), lambda i: (0, 0))),
        out_shape=(_sds((t_len, d), F32), _sds((t_len, d), BF16), _sds((d, t_len), BF16), _sds((8, d), F32)))


def _bwd_norm2_epilogue(x1, dx2, y, g, modrows, tm):
    t_len, d = x1.shape

    def fn(dh2, rows, x1_ref, dx2_ref, y_ref, g_ref, m_ref, dx1_ref, dy_ref, acc_ref):
        _zero_at_start(acc_ref, rows)
        x1 = x1_ref[rows, :]
        rstd = lax.rsqrt(jnp.mean(x1 * x1, axis=-1, keepdims=True) + EPS)
        xh = x1 * rstd
        gn = g_ref[...]
        dn = dh2 * (1.0 + m_ref[R_SC2:R_SC2 + 1, :])
        dxh = dn * gn
        dx1 = dx2_ref[rows, :] + rstd * (dxh - xh * jnp.mean(dxh * xh, axis=-1, keepdims=True))
        dx1_ref[rows, :] = dx1
        dy_ref[rows, :] = (m_ref[R_G1:R_G1 + 1, :] * dx1).astype(BF16)
        acc_ref[0:1, :] += jnp.sum(dh2, axis=0, keepdims=True)
        acc_ref[1:2, :] += jnp.sum(dh2 * xh * gn, axis=0, keepdims=True)
        acc_ref[2:3, :] += jnp.sum(dn * xh, axis=0, keepdims=True)
        acc_ref[3:4, :] += jnp.sum(dx1 * y_ref[rows, :], axis=0, keepdims=True)

    t = pl.BlockSpec((tm, d), lambda i: (i, 0))
    return _Epilogue(
        fn, (x1, dx2, y, g, modrows),
        in_specs=[t, t, t, pl.BlockSpec((1, d), lambda i: (0, 0)), pl.BlockSpec((8, d), lambda i: (0, 0))],
        out_specs=(t, t, pl.BlockSpec((8, d), lambda i: (0, 0))),
        out_shape=(_sds((t_len, d), F32), _sds((t_len, d), BF16), _sds((8, d), F32)))


def _mix_bwd_epilogue(o_f, o_b, p, h_f, h_b, cb):
    n = o_f.shape[0]
    tile = _tile

    def fn(dm, rows, of_ref, ob_ref, g_ref, gate_ref, hf_ref, hb_ref, do_ref, dg_ref, dgate_ref, dhs_ref):
        keep = jnp.where(pl.program_id(0) - 1 < cb, 0.0, 1.0)
        dm = dm * keep
        o = of_ref[rows, :] + ob_ref[rows, :]
        g = g_ref[rows, :]
        s = _sigmoid(g)
        sg = g * s
        dsg = s * (1.0 + g * (1.0 - s))
        for hh in range(HEADS):
            sl = slice(hh * HEAD_DIM, (hh + 1) * HEAD_DIM)
            oh = o[:, sl]
            yc = oh - jnp.mean(oh, axis=-1, keepdims=True)
            rs = lax.rsqrt(jnp.mean(yc * yc, axis=-1, keepdims=True) + EPS)
            gn = yc * rs
            dret = dm[:, sl]
            dgn = dret * sg[:, sl]
            dg_ref[rows, sl] = (dret * gn * dsg[:, sl]).astype(BF16)
            do_ref[rows, sl] = (rs * (dgn - jnp.mean(dgn, axis=-1, keepdims=True)
                                      - gn * jnp.mean(dgn * gn, axis=-1, keepdims=True))).astype(BF16)
        dlru = dm[:, RET_W:]
        gate = gate_ref[rows, :]
        dhs_ref[rows, :] = dlru * _gelu(gate)
        dgate_ref[rows, :] = (dlru * (hf_ref[rows, :] + hb_ref[rows, :]) * _dgelu(gate)).astype(BF16)

    t = pl.BlockSpec((TM, RET_W), lambda i: (tile(i), 0))
    return _Epilogue(
        fn, (o_f, o_b, p, p, h_f, h_b),
        in_specs=[t, t, pl.BlockSpec((TM, RET_W), lambda i: (tile(i), COL_G)),
                  pl.BlockSpec((TM, LRU_W), lambda i: (tile(i), COL_GATE)), t, t],
        out_specs=(t, t, t, t), out_shape=(_sds((n, RET_W), BF16),) * 3 + (_sds((n, RET_W), F32),),
        steps=n // TM, lhs_map=lambda i: (jnp.maximum(i - cb, 0), 0), delay=1)


def _bwd_norm1_epilogue(ctx, x, dx1, g, modrows, cb):
    t_len, d = x.shape

    def fn(dh, rows, ctx_ref, x_ref, dx1_ref, g_ref, m_ref, gx_ref, acc_ref):
        is_ctx = pl.program_id(0) < cb
        _zero_at_start(acc_ref, rows)
        xin = jnp.where(is_ctx, ctx_ref[rows, :], x_ref[rows, :])
        sc = jnp.where(is_ctx, m_ref[R_CSC1:R_CSC1 + 1, :], m_ref[R_SC1:R_SC1 + 1, :])
        rstd = lax.rsqrt(jnp.mean(xin * xin, axis=-1, keepdims=True) + EPS)
        xh = xin * rstd
        gn = g_ref[...]
        dn = dh * (1.0 + sc)
        dxh = dn * gn
        gx_ref[rows, :] = dx1_ref[rows, :] + rstd * (dxh - xh * jnp.mean(dxh * xh, axis=-1, keepdims=True))
        s0 = jnp.sum(dh, axis=0, keepdims=True)
        s1 = jnp.sum(dh * xh * gn, axis=0, keepdims=True)
        acc_ref[4:5, :] += jnp.sum(dn * xh, axis=0, keepdims=True)
        acc_ref[0:1, :] += jnp.where(is_ctx, s0, 0.0)
        acc_ref[1:2, :] += jnp.where(is_ctx, s1, 0.0)
        acc_ref[2:3, :] += jnp.where(is_ctx, 0.0, s0)
        acc_ref[3:4, :] += jnp.where(is_ctx, 0.0, s1)

    lat = pl.BlockSpec((TM, d), lambda i: (jnp.maximum(i - cb, 0), 0))
    return _Epilogue(
        fn, (ctx, x, dx1, g, modrows),
        in_specs=[pl.BlockSpec((TM, d), lambda i: (jnp.minimum(i, cb - 1), 0)),
                  lat, lat, pl.BlockSpec((1, d), lambda i: (0, 0)), pl.BlockSpec((8, d), lambda i: (0, 0))],
        out_specs=(lat, pl.BlockSpec((8, d), lambda i: (0, 0))),
        out_shape=(_sds((t_len, d), F32), _sds((8, d), F32)))


def _rot(x, cf, ss):
    return x * cf + pltpu.roll(x, HEAD_DIM // 2, 1) * ss


def _decay_exponents(dirn):
    ii = lax.broadcasted_iota(jnp.int32, (CHUNK, CHUNK), 0)
    jj = lax.broadcasted_iota(jnp.int32, (CHUNK, CHUNK), 1)
    rel = ii - jj if dirn == 0 else jj - ii
    pos = ii.astype(F32)
    if dirn == 0:
        cq, cs = pos + 1.0, (CHUNK - 1.0) - pos
    else:
        cq, cs = CHUNK - pos, pos
    return rel, jnp.maximum(rel, 0).astype(F32), cq, cs


def _store_decay(lg_ref, dec, relf_ref=None):
    for dirn in (0, 1):
        rel, relf, cq, cs = _decay_exponents(dirn)
        if relf_ref is not None:
            relf_ref[dirn] = relf
        for h in range(HEADS):
            lgv = lg_ref[dirn, h]
            wq, ws = jnp.exp(lgv * cq), jnp.exp(lgv * cs)
            dec[dirn, h, 0] = jnp.where(rel >= 0, jnp.exp(lgv * relf), 0.0)
            dec[dirn, h, 1] = wq
            dec[dirn, h, 2] = ws
            if relf_ref is not None:
                dec[dirn, h, 3] = wq * cq
                dec[dirn, h, 4] = ws * cs


def _ret_rows(cc, nc, step_of):
    return [lambda s, dirn=dirn: _tile_order(dirn, step_of(s), cc, nc) for dirn in (0, 1)]


def _ret_in_specs(rows):
    specs = []
    for row in rows:
        specs += [pl.BlockSpec((CHUNK, RET_W), lambda s, o=o, row=row: (row(s), o)) for o in (0, 1, 2)]
    return specs


def _ret_fwd(qkv, lg, cc, ride=None):
    n = qkv.shape[0]
    nc = n // CHUNK
    rows = _ret_rows(cc, nc, lambda s: s)

    def body(lg_ref, q0, k0, v0, q1, k1, v1, o0, o1, sp0, sp1, st, dec):
        @pl.when(pl.program_id(0) == 0)
        def _():
            st[...] = jnp.zeros_like(st)
            _store_decay(lg_ref, dec)

        refs = ((q0, k0, v0, o0, sp0), (q1, k1, v1, o1, sp1))
        chains = [(dirn, h, slice(h * HEAD_DIM, (h + 1) * HEAD_DIM)) for dirn in (0, 1) for h in range(HEADS)]
        scores, cross, update = [], [], []
        for dirn, h, sl in chains:
            q_ref, k_ref, v_ref, _, sp_ref = refs[dirn]
            q, k, v = q_ref[:, sl], k_ref[:, sl], v_ref[:, sl]
            sp = st[dirn, h]
            sp_ref[h] = sp
            scores.append(_dot_nt(q, k))
            cross.append(_dot(q * dec[dirn, h, 1], sp))
            update.append(_dot_tn(k * dec[dirn, h, 2], v))
        masked = [(a * dec[dirn, h, 0]).astype(BF16) for a, (dirn, h, _) in zip(scores, chains)]
        intra = [_dot(sc, refs[dirn][2][:, sl]) for sc, (dirn, h, sl) in zip(masked, chains)]
        for (dirn, h, sl), o_in, o_cr, upd in zip(chains, intra, cross, update):
            refs[dirn][3][:, sl] = o_in + o_cr
            st[dirn, h] = jnp.exp(lg_ref[dirn, h] * CHUNK) * st[dirn, h] + upd

    o_specs = [pl.BlockSpec((CHUNK, RET_W), lambda s, row=row: (row(s), 0)) for row in rows]
    state = pl.BlockSpec((None, HEADS, CHUNK, HEAD_DIM), lambda s: (s, 0, 0, 0))
    return _pcall_ride(
        body, ride, (lg,) + (qkv,) * 6, name="ret_fwd", grid=(nc,),
        in_specs=[pl.BlockSpec(memory_space=pltpu.SMEM)] + _ret_in_specs(rows),
        out_specs=(o_specs[0], o_specs[1], state, state),
        out_shape=(_sds((n, RET_W), F32),) * 2 + (_sds((nc, HEADS, CHUNK, HEAD_DIM), F32),) * 2,
        scratch_shapes=[pltpu.VMEM((2, HEADS, CHUNK, HEAD_DIM), F32), pltpu.VMEM((2, HEADS, 3, CHUNK, CHUNK), F32)],
        compiler_params=_cp())


def _ret_bwd(qkv, lg, do, s_prev, cc, ride=None):
    n = qkv.shape[0]
    nc = n // CHUNK
    rows = _ret_rows(cc, nc, lambda s: nc - 1 - s)

    def body(lg_ref, q0, k0, v0, q1, k1, v1, do0, do1, sp0, sp1, dq0, dk0, dv0, dq1, dk1, dv1, dlg_ref, dst, dec, relf):
        @pl.when(pl.program_id(0) == 0)
        def _():
            dst[...] = jnp.zeros_like(dst)
            dlg_ref[...] = jnp.zeros_like(dlg_ref)
            _store_decay(lg_ref, dec, relf)

        refs = ((q0, k0, v0, do0, sp0, dq0, dk0, dv0), (q1, k1, v1, do1, sp1, dq1, dk1, dv1))
        chains = [(dirn, h, slice(h * HEAD_DIM, (h + 1) * HEAD_DIM)) for dirn in (0, 1) for h in range(HEADS)]

        def tiles(dirn, sl):
            q_ref, k_ref, v_ref, do_ref = refs[dirn][:4]
            return q_ref[:, sl], k_ref[:, sl], v_ref[:, sl], do_ref[:, sl]

        a_s, g1_s, da_s, h1_s = [], [], [], []
        for dirn, h, sl in chains:
            q, k, v, dov = tiles(dirn, sl)
            a_s.append(_dot_nt(q, k))
            g1_s.append(_dot_nt(dov, refs[dirn][4][h]))
            da_s.append(_dot_nt(dov, v))
            h1_s.append(_dot_nt(v, dst[dirn, h]))
        da_s = [da * dec[dirn, h, 0] for da, (dirn, h, _) in zip(da_s, chains)]
        dq_s, dk_s, dv_s, ds_s = [], [], [], []
        for (dirn, h, sl), a, da in zip(chains, a_s, da_s):
            q, k, v, dov = tiles(dirn, sl)
            dq_s.append(_dot(da, k))
            dk_s.append(_dot_tn(da, q))
            dv_s.append(_dot_tn(a * dec[dirn, h, 0], dov) + _dot(k * dec[dirn, h, 2], dst[dirn, h]))
            ds_s.append(_dot_tn(q * dec[dirn, h, 1], dov))
        for (dirn, h, sl), a, g1, da, h1, dq2, dk2, dv, ds2 in zip(chains, a_s, g1_s, da_s, h1_s, dq_s, dk_s, dv_s, ds_s):
            q, k, _, _ = tiles(dirn, sl)
            dq_ref, dk_ref, dv_ref = refs[dirn][5:]
            sp, dsn = refs[dirn][4][h], dst[dirn, h]
            gc = jnp.exp(lg_ref[dirn, h] * CHUNK)
            dq_ref[:, sl] = (g1 * dec[dirn, h, 1] + dq2).astype(BF16)
            dk_ref[:, sl] = (dk2 + h1 * dec[dirn, h, 2]).astype(BF16)
            dv_ref[:, sl] = dv.astype(BF16)
            term = da * a * relf[dirn] + q * g1 * dec[dirn, h, 3] + k * h1 * dec[dirn, h, 4] + sp * dsn * (CHUNK * gc)
            dlg_ref[dirn * HEADS + h:dirn * HEADS + h + 1, 0:HEAD_DIM] += jnp.sum(term, axis=0, keepdims=True)
            dst[dirn, h] = gc * dsn + ds2

    wide = [pl.BlockSpec((CHUNK, RET_W), lambda s, row=row: (row(s), 0)) for row in rows]
    state = pl.BlockSpec((None, HEADS, CHUNK, HEAD_DIM), lambda s: (nc - 1 - s, 0, 0, 0))
    return _pcall_ride(
        body, ride, (lg,) + (qkv,) * 6 + (do, do, s_prev[0], s_prev[1]), name="ret_bwd", grid=(nc,),
        in_specs=[pl.BlockSpec(memory_space=pltpu.SMEM)] + _ret_in_specs(rows) + wide + [state, state],
        out_specs=(wide[0],) * 3 + (wide[1],) * 3 + (pl.BlockSpec((2 * HEADS, 8 * HEAD_DIM), lambda s: (0, 0)),),
        out_shape=(_sds((n, RET_W), BF16),) * 6 + (_sds((2 * HEADS, 8 * HEAD_DIM), F32),),
        scratch_shapes=[pltpu.VMEM((2, HEADS, CHUNK, HEAD_DIM), F32), pltpu.VMEM((2, HEADS, 5, CHUNK, CHUNK), F32),
                        pltpu.VMEM((2, CHUNK, CHUNK), F32)],
        compiler_params=_cp())


def _shift_rows(cur, prev8, next8, k, seg_start, seg_end):
    tm = cur.shape[0]
    rows = _rows_iota(cur.shape)
    if k < 0:
        out = pltpu.roll(cur, -k, 0)
        for j in range(-k):
            halo = jnp.where(seg_start, 0.0, prev8[SUB + k + j:SUB + k + j + 1, :])
            out = jnp.where(rows == j, halo, out)
    else:
        out = pltpu.roll(cur, tm - k, 0)
        for j in range(k):
            halo = jnp.where(seg_end, 0.0, next8[j:j + 1, :])
            out = jnp.where(rows == tm - k + j, halo, out)
    return out


def _seg_flags(t, cb, nb):
    return jnp.logical_or(t == 0, t == cb), jnp.logical_or(t == cb - 1, t == nb - 1)


def _halo_specs(tile_of, n_rows, col):
    per = TM // SUB
    return [pl.BlockSpec((TM, LRU_W), lambda s: (tile_of(s), col)),
            pl.BlockSpec((SUB, LRU_W), lambda s: (jnp.maximum(tile_of(s) * per - 1, 0), col)),
            pl.BlockSpec((SUB, LRU_W), lambda s: (jnp.minimum((tile_of(s) + 1) * per, n_rows // SUB - 1), col))]


def _lru_gates(xr, prev8, next8, seg_start, seg_end, cw_ref, cb_ref, wg_ref, bg_ref, sp_ref):
    xm1 = _shift_rows(xr, prev8, next8, -1, seg_start, seg_end)
    xp1 = _shift_rows(xr, prev8, next8, 1, seg_start, seg_end)
    xp2 = _shift_rows(xr, prev8, next8, 2, seg_start, seg_end)
    xc = cb_ref[...] + xm1 * cw_ref[0:1, :] + xr * cw_ref[1:2, :] + xp1 * cw_ref[2:3, :] + xp2 * cw_ref[3:4, :]
    pre = _dot(xc, wg_ref[...]) + bg_ref[...]
    r = _sigmoid(pre[:, :LRU_W])
    i = _sigmoid(pre[:, LRU_W:])
    la = (-LRU_C) * r * sp_ref[...]
    a = jnp.exp(la)
    th = jnp.tanh(la)
    sq = jnp.sqrt(-2.0 * th / (1.0 - th))
    return xc, r, i, a, sq


def _scan_tile(a, b, ascending, a_sc, b_sc, carry, out_ref):
    tm, w = a.shape
    nsub = tm // SUB
    a = a.reshape(nsub, SUB, w)
    b = b.reshape(nsub, SUB, w)
    r8 = lax.broadcasted_iota(jnp.int32, a.shape, 1)
    for k in (1, 2, 4):
        if ascending:
            m = r8 >= k
            a_s, b_s = pltpu.roll(a, k, 1), pltpu.roll(b, k, 1)
        else:
            m = r8 < SUB - k
            a_s, b_s = pltpu.roll(a, SUB - k, 1), pltpu.roll(b, SUB - k, 1)
        b = a * jnp.where(m, b_s, 0.0) + b
        a = a * jnp.where(m, a_s, 1.0)
    a_sc[...] = a.reshape(tm, w)
    b_sc[...] = b.reshape(tm, w)

    def step(j, c):
        off = pl.multiple_of((j if ascending else nsub - 1 - j) * SUB, SUB)
        hb = a_sc[pl.ds(off, SUB), :] * c + b_sc[pl.ds(off, SUB), :]
        out_ref[pl.ds(off, SUB), :] = hb
        last = hb[SUB - 1:SUB, :] if ascending else hb[0:1, :]
        return jnp.broadcast_to(last, c.shape)

    carry[...] = lax.fori_loop(0, nsub, step, carry[...], unroll=8)


def _lru_fwd(p, wg, bg, sp, cw, cbias, dirn, cb, ride=None):
    n = p.shape[0]
    nb = n // TM
    tile_of = lambda s: _tile_order(dirn, s, cb, nb)

    def body(x_ref, xp_ref, xn_ref, wg_ref, bg_ref, sp_ref, cw_ref, cb_ref, h_ref, cin_ref, carry, a_sc, b_sc):
        s = pl.program_id(0)

        @pl.when(s == 0)
        def _():
            carry[...] = jnp.zeros_like(carry)

        seg_start, seg_end = _seg_flags(tile_of(s), cb, nb)
        xc, r, i, a, sq = _lru_gates(x_ref[...], xp_ref[...], xn_ref[...], seg_start, seg_end,
                                     cw_ref, cb_ref, wg_ref, bg_ref, sp_ref)
        cin_ref[...] = carry[...]
        _scan_tile(a, sq * (i * xc), dirn == 0, a_sc, b_sc, carry, h_ref)

    full = lambda shape: pl.BlockSpec(shape, lambda s: (0,) * len(shape))
    return _pcall_ride(
        body, ride, (p, p, p, wg, bg, sp, cw, cbias), name=f"lru_fwd{dirn}", grid=(nb,),
        in_specs=_halo_specs(tile_of, n, COL_XR) + [full((LRU_W, 2 * LRU_W)), full((1, 2 * LRU_W)), full((1, LRU_W)),
                                               full((4, LRU_W)), full((1, LRU_W))],
        out_specs=(pl.BlockSpec((TM, LRU_W), lambda s: (tile_of(s), 0)),
                   pl.BlockSpec((None, SUB, LRU_W), lambda s: (tile_of(s), 0, 0))),
        out_shape=(_sds((n, LRU_W), F32), _sds((nb, SUB, LRU_W), F32)),
        scratch_shapes=[pltpu.VMEM((SUB, LRU_W), F32), pltpu.VMEM((TM, LRU_W), F32), pltpu.VMEM((TM, LRU_W), F32)],
        compiler_params=_cp())


def _lru_bwd(p, wg, bg, sp, cw, cbias, h, cin, dhs, dirn, cb, ride=None):
    n = p.shape[0]
    nb = n // TM
    tile_of = lambda s: _tile_order(dirn, nb - 1 - s, cb, nb)

    def body(x_ref, xp_ref, xn_ref, wg_ref, bg_ref, sp_ref, cw_ref, cb_ref, h_ref, cin_ref, dhs_ref,
             dxc_ref, dwd_ref, acc_ref, carry, a_sc, b_sc, mu_sc, dwg_ref):
        s = pl.program_id(0)

        @pl.when(s == 0)
        def _():
            carry[...] = jnp.zeros_like(carry)
            dwg_ref[...] = jnp.zeros_like(dwg_ref)
            acc_ref[...] = jnp.zeros_like(acc_ref)

        seg_start, seg_end = _seg_flags(tile_of(s), cb, nb)
        xc, r, i, a, sq = _lru_gates(x_ref[...], xp_ref[...], xn_ref[...], seg_start, seg_end,
                                     cw_ref, cb_ref, wg_ref, bg_ref, sp_ref)
        rows = _rows_iota(a.shape)
        hv = h_ref[...]
        dh = dhs_ref[...]
        mu_next = carry[0:1, :]
        _scan_tile(a, a * dh, dirn == 1, a_sc, b_sc, carry, mu_sc)
        mu = mu_sc[...]
        if dirn == 0:
            hprev = jnp.where(rows == 0, cin_ref[0:1, :], pltpu.roll(hv, 1, 0))
            lam = dh + jnp.where(rows == TM - 1, mu_next, pltpu.roll(mu, TM - 1, 0))
        else:
            hprev = jnp.where(rows == TM - 1, cin_ref[0:1, :], pltpu.roll(hv, TM - 1, 0))
            lam = dh + jnp.where(rows == 0, mu_next, pltpu.roll(mu, 1, 0))
        ds = lam * (i * xc)
        di = lam * (sq * xc)
        dla = lam * hprev * a - ds * (a * a) / jnp.maximum(sq, 1e-20)
        dpr = dla * ((-LRU_C) * sp_ref[...]) * r * (1.0 - r)
        dpi = di * i * (1.0 - i)
        dpre = jnp.concatenate([dpr, dpi], axis=1)
        dxc_ref[...] = lam * (sq * i) + _dot_nt(dpre, wg_ref[...])
        dwg_ref[...] += _dot_tn(xc, dpre)
        acc_ref[0:1, :] += jnp.sum(dpre, axis=0, keepdims=True)
        acc_ref[1:2, 0:LRU_W] += jnp.sum(dla * ((-LRU_C) * r), axis=0, keepdims=True)

        @pl.when(s == nb - 1)
        def _():
            low = lax.broadcasted_iota(jnp.int32, (LRU_BD, 2 * LRU_BD), 1) < LRU_BD
            for half in (0, LRU_W):
                for m in range(LRU_BLOCKS // 2):
                    lanes = slice(half + 2 * LRU_BD * m, half + 2 * LRU_BD * (m + 1))
                    even = dwg_ref[2 * m * LRU_BD:(2 * m + 1) * LRU_BD, lanes]
                    odd = dwg_ref[(2 * m + 1) * LRU_BD:(2 * m + 2) * LRU_BD, lanes]
                    dwd_ref[:, lanes] = jnp.where(low, even, odd)

    full = lambda shape: pl.BlockSpec(shape, lambda s: (0,) * len(shape))
    tile = pl.BlockSpec((TM, LRU_W), lambda s: (tile_of(s), 0))
    return _pcall_ride(
        body, ride, (p, p, p, wg, bg, sp, cw, cbias, h, cin, dhs), name=f"lru_bwd{dirn}", grid=(nb,),
        in_specs=_halo_specs(tile_of, n, COL_XR) + [full((LRU_W, 2 * LRU_W)), full((1, 2 * LRU_W)), full((1, LRU_W)),
                                               full((4, LRU_W)), full((1, LRU_W)), tile,
                                               pl.BlockSpec((None, SUB, LRU_W), lambda s: (tile_of(s), 0, 0)), tile],
        out_specs=(tile, full((LRU_BD, 2 * LRU_W)), full((8, 2 * LRU_W))),
        out_shape=(_sds((n, LRU_W), F32), _sds((LRU_BD, 2 * LRU_W), F32), _sds((8, 2 * LRU_W), F32)),
        scratch_shapes=[pltpu.VMEM((SUB, LRU_W), F32)] + [pltpu.VMEM((TM, LRU_W), F32)] * 3
        + [pltpu.VMEM((LRU_W, 2 * LRU_W), F32)],
        compiler_params=_cp())


def _assemble_dp(dqs, dks, dvs, dg, dgate, dxcs, p, cw, cosf, sins, cb, ride=None):
    n = p.shape[0]
    nb = n // TM
    tile_of = lambda s: s

    def body(dqf, dqb, dkf, dkb, dvf, dvb, dg_ref, dgate_ref, cf, pf, nf, cb_, pb, nb_, x_ref, xp_ref, xn_ref,
             cw_ref, cos_ref, sin_ref, dp_ref, acc_ref):
        s = pl.program_id(0)

        @pl.when(s == 0)
        def _():
            acc_ref[...] = jnp.zeros_like(acc_ref)

        seg_start, seg_end = _seg_flags(s, cb, nb)
        dq = dqf[...].astype(F32) + dqb[...].astype(F32)
        dk = dkf[...].astype(F32) + dkb[...].astype(F32)
        cosv, sinv = cos_ref[...], sin_ref[...]
        for h in range(HEADS):
            sl = slice(h * HEAD_DIM, (h + 1) * HEAD_DIM)
            sk = slice(RET_W + h * HEAD_DIM, RET_W + (h + 1) * HEAD_DIM)
            dp_ref[:, sl] = (dq[:, sl] * cosv + pltpu.roll(dq[:, sl] * sinv, HEAD_DIM // 2, 1)).astype(BF16)
            dp_ref[:, sk] = ((dk[:, sl] * cosv + pltpu.roll(dk[:, sl] * sinv, HEAD_DIM // 2, 1)) * K_SCALE).astype(BF16)
        dp_ref[:, 2 * RET_W:3 * RET_W] = (dvf[...].astype(F32) + dvb[...].astype(F32)).astype(BF16)
        dp_ref[:, 3 * RET_W:4 * RET_W] = dg_ref[...].astype(BF16)
        dxc = cf[...] + cb_[...]
        dprev = pf[...] + pb[...]
        dnext = nf[...] + nb_[...]
        dxr = (_shift_rows(dxc, dprev, dnext, 1, seg_start, seg_end) * cw_ref[0:1, :] + dxc * cw_ref[1:2, :]
               + _shift_rows(dxc, dprev, dnext, -1, seg_start, seg_end) * cw_ref[2:3, :]
               + _shift_rows(dxc, dprev, dnext, -2, seg_start, seg_end) * cw_ref[3:4, :])
        dp_ref[:, 4 * RET_W:4 * RET_W + LRU_W] = dxr.astype(BF16)
        dp_ref[:, 4 * RET_W + LRU_W:] = dgate_ref[...].astype(BF16)
        xr, xp, xn = x_ref[...], xp_ref[...], xn_ref[...]
        for j, k in enumerate((-1, 0, 1, 2)):
            xs = xr if k == 0 else _shift_rows(xr, xp, xn, k, seg_start, seg_end)
            acc_ref[j:j + 1, 0:LRU_W] += jnp.sum(dxc * xs, axis=0, keepdims=True)
        acc_ref[4:5, 0:LRU_W] += jnp.sum(dxc, axis=0, keepdims=True)

    t = pl.BlockSpec((TM, RET_W), lambda s: (s, 0))
    args = (dqs[0], dqs[1], dks[0], dks[1], dvs[0], dvs[1], dg, dgate, dxcs[0], dxcs[0], dxcs[0], dxcs[1], dxcs[1], dxcs[1],
            p, p, p, cw, cosf, sins)
    tab = pl.BlockSpec((TM, HEAD_DIM), lambda s: (s, 0))
    return _pcall_ride(
        body, ride, args, name="assemble_dp", grid=(nb,),
        in_specs=[t] * 8 + _halo_specs(tile_of, n, 0) * 2 + _halo_specs(tile_of, n, COL_XR)
        + [pl.BlockSpec((4, LRU_W), lambda s: (0, 0)), tab, tab],
        out_specs=(pl.BlockSpec((TM, 4 * RET_W + 2 * LRU_W), lambda s: (s, 0)), pl.BlockSpec((8, 2 * LRU_W), lambda s: (0, 0))),
        out_shape=(_sds((n, 4 * RET_W + 2 * LRU_W), BF16), _sds((8, 2 * LRU_W), F32)), compiler_params=_cp())


def _adamw(g, w, m, v):
    nm = ADAM_B1 * m + (1.0 - ADAM_B1) * g
    nv = ADAM_B2 * v + (1.0 - ADAM_B2) * (g * g)
    m_hat = nm / (1.0 - ADAM_B1 ** ADAM_STEP)
    v_hat = nv / (1.0 - ADAM_B2 ** ADAM_STEP)
    return (-ADAM_LR) * (m_hat / (jnp.sqrt(v_hat) + ADAM_EPS) + ADAM_WD * w), nm, nv


def _adam_many(items, name):
    n = len(items)

    def body(*refs):
        for i in range(n):
            g, w, m, v = (r[...] for r in refs[4 * i:4 * i + 4])
            for o_ref, val in zip(refs[4 * n + 3 * i:4 * n + 3 * i + 3], _adamw(g, w, m, v)):
                o_ref[...] = val

    out_shape = tuple(_sds(it[1].shape, F32) for it in items for _ in range(3))
    res = _pcall(body, name=name, out_shape=out_shape, compiler_params=_cp())(*[a for it in items for a in it])
    return [tuple(res[3 * i:3 * i + 3]) for i in range(n)]


def _sum_adam(parts_list, w, m, v, name):
    nparts, _, c = parts_list[0].shape
    r = w.shape[0]
    tr = min(parts_list[0].shape[1], 128)
    starts, o = [], 0
    for pa in parts_list:
        starts.append(o)
        o += pa.shape[1] // tr
    nseg = len(parts_list)

    def body(*refs):
        p_refs = refs[:nseg]
        w_ref, m_ref, v_ref, g_ref, d_ref, nm_ref, nv_ref = refs[nseg:]
        i = pl.program_id(0)
        for s, p_ref in enumerate(p_refs):
            end = starts[s + 1] if s + 1 < nseg else r // tr

            @pl.when(jnp.logical_and(i >= starts[s], i < end))
            def _():
                g = p_ref[0].astype(F32)
                for j in range(1, nparts):
                    g = g + p_ref[j].astype(F32)
                g_ref[...] = g
                d_ref[...], nm_ref[...], nv_ref[...] = _adamw(g, w_ref[...], m_ref[...], v_ref[...])

    def seg_spec(s):
        last = parts_list[s].shape[1] // tr - 1
        return pl.BlockSpec((nparts, tr, c), lambda i: (0, jnp.clip(i - starts[s], 0, last), 0))

    t = pl.BlockSpec((tr, c), lambda i: (i, 0))
    return _pcall(
        body, name=name, grid=(r // tr,),
        in_specs=[seg_spec(s) for s in range(nseg)] + [t, t, t],
        out_specs=(t, t, t, t), out_shape=(_sds((r, c), F32),) * 4, compiler_params=_cp(),
    )(*parts_list, w, m, v)


def _sum_parts(parts, name):
    nparts, r, c = parts.shape

    def body(p_ref, o_ref):
        g = p_ref[0]
        for j in range(1, nparts):
            g = g + p_ref[j]
        o_ref[...] = g

    return _pcall(body, name=name, out_shape=_sds((r, c), parts.dtype), compiler_params=_cp())(parts)


def _rot_tables(l_len, t_len):
    rows = t_len // GRID_W
    n_freq = HEAD_DIM // 4
    inv = ROPE_BASE ** (-jnp.arange(n_freq, dtype=F32) / n_freq)
    ang_r = jnp.arange(rows, dtype=F32)[:, None] * inv
    ang_c = jnp.arange(GRID_W, dtype=F32)[:, None] * inv
    cos = jnp.concatenate([jnp.repeat(jnp.cos(ang_r), GRID_W, axis=0), jnp.tile(jnp.cos(ang_c), (rows, 1))], axis=-1)
    sin = jnp.concatenate([jnp.repeat(jnp.sin(ang_r), GRID_W, axis=0), jnp.tile(jnp.sin(ang_c), (rows, 1))], axis=-1)
    cosf = jnp.concatenate([jnp.ones((l_len, HEAD_DIM), F32), jnp.concatenate([cos, cos], axis=-1)], axis=0)
    sins = jnp.concatenate([jnp.zeros((l_len, HEAD_DIM), F32), jnp.concatenate([-sin, sin], axis=-1)], axis=0)
    return cosf, sins


def _block_diag(w):
    eye = jnp.eye(LRU_BLOCKS, dtype=w.dtype)
    return (w[:, :, None, :] * eye[:, None, :, None]).reshape(LRU_W, LRU_W)


def _blocks_from_lanes(dwd_half):
    return dwd_half.reshape(LRU_BD, LRU_BLOCKS, LRU_BD).transpose(1, 0, 2)


def _silu(x):
    return x * jax.nn.sigmoid(x)


def kernel(x, c, ctx, c_ctx, w_ada, b_ada, norm1_g, norm2_g, w_in, ret_decay, conv_w, conv_b, lru_wa, lru_ba, lru_wx, lru_bx, lru_lambda, w_out, w_mlp1, w_mlp2, final_g, loss_target, m_c_ctx, m_w_ada, m_b_ada, m_norm1_g, m_norm2_g, m_w_in, m_ret_decay, m_conv_w, m_conv_b, m_lru_wa, m_lru_ba, m_lru_wx, m_lru_bx, m_lru_lambda, m_w_out, m_w_mlp1, m_w_mlp2, m_final_g, v_c_ctx, v_w_ada, v_b_ada, v_norm1_g, v_norm2_g, v_w_in, v_ret_decay, v_conv_w, v_conv_b, v_lru_wa, v_lru_ba, v_lru_wx, v_lru_bx, v_lru_lambda, v_w_out, v_w_mlp1, v_w_mlp2, v_final_g):
    t_len, d = x.shape[1], x.shape[2]
    l_len = ctx.shape[1]
    cb, cc = l_len // TM, l_len // CHUNK
    me = 4 * lax.axis_index("x") + 2 * lax.axis_index("y") + lax.axis_index("c")
    x2d, ctx2d, tgt2d = x[0], ctx[0], loss_target[0]
    ada_cols = w_ada.shape[2]
    wa2d = w_ada[0]

    sc_loc = conv_w.shape[2]
    pack_a = jnp.zeros((8, d), F32)
    pack_a = pack_a.at[0].set(_silu(c[0]))
    pack_a = pack_a.at[1, :4 * sc_loc].set(conv_w[0].reshape(-1))
    pack_a = pack_a.at[2, :2 * sc_loc].set(lru_ba[0].reshape(-1))
    pack_a = pack_a.at[3, :2 * sc_loc].set(lru_bx[0].reshape(-1))
    pack_a = pack_a.at[4, :2 * sc_loc].set(lru_lambda[0].reshape(-1))
    b_cols = lax.dynamic_slice(b_ada, (0, me * ada_cols), (1, ada_cols))
    all_a, s16, mod_parts = _mod_exchange(pack_a, _silu(c_ctx)[None, :], wa2d, b_cols)

    def unshard(row, k):
        return all_a[:, row, :k * sc_loc].reshape(N_DEV, k, sc_loc).transpose(1, 0, 2).reshape(k, N_DEV * sc_loc)

    conv_w_full = unshard(1, 4)
    ba_full, bx_full, lam_full = unshard(2, 2), unshard(3, 2), unshard(4, 2)

    mod_all = mod_parts.transpose(1, 0, 2).reshape(16, N_DEV * ada_cols)
    mod_me = lax.dynamic_slice(mod_all, (me, 0), (1, 6 * d)).reshape(6, d)
    mod_c = mod_all[8].reshape(6, d)
    modrows = jnp.concatenate([mod_c[0:2], mod_me], axis=0)

    lg = jax.nn.log_sigmoid(ret_decay[0])
    sp = jax.nn.softplus(-lam_full)
    wg = [jnp.concatenate([_block_diag(lru_wa[0, dd]), _block_diag(lru_wx[0, dd])], axis=1).astype(BF16) for dd in (0, 1)]
    bg = [jnp.concatenate([ba_full[dd], bx_full[dd]])[None, :] for dd in (0, 1)]
    cosf, sins = _rot_tables(l_len, t_len)

    (hn, hnt), win_g = _norm1_fwd(ctx2d, x2d, norm1_g, modrows, cb, ride=("gather", w_in[0].astype(BF16)))
    (qkv, p), w2_g = _mm_in(hn, win_g, cosf, sins, ride=("gather", w_mlp2[0].astype(BF16)))
    w2_g = w2_g.reshape(1, 4 * d, d)
    (o0, o1, sp0, sp1), w1_g = _ret_fwd(qkv, lg, cc, ride=("gather", w_mlp1[0].astype(BF16)))
    o, s_prev = [o0, o1], [sp0, sp1]
    (h0, cin0), wout_g = _lru_fwd(p, wg[0], bg[0], sp[0:1], conv_w_full, conv_b, 0, cb, ride=("gather", w_out[0].astype(BF16)))
    wout_g = wout_g.reshape(1, d, d)
    (h1, cin1), _ = _lru_fwd(p, wg[1], bg[1], sp[1:2], conv_w_full, conv_b, 1, cb)
    h, cin = [h0, h1], [cin0, cin1]
    mix = _mix_fwd(o[0], o[1], p, h[0], h[1], cb, t_len)
    y, x1, h2, h2t = _mm_nn(mix, wout_g, F32, "mm_out_norm2", tm=TL, epi=_res_norm2_epilogue(x2d, norm2_g, modrows, TL))
    r = _mm_nn(h2, w1_g, BF16, "mm_mlp1", relu_out=True, tm=TL)
    dx2, dz, dzt, facc = _mm_nn(r, w2_g, F32, "mm_mlp2_final", square_lhs=True, tm=TL, vmem_mb=58,
                                epi=_final_epilogue(x1, tgt2d, final_g[None, :], modrows, TL))

    du = _mm_nt(dz, w2_g, BF16, "mm_da2", relu_mul=r, tm=TL, vmem_mb=58)
    gw2_lo, gw2_hi = _mm_wgrad(dzt, r, "mm_dw2", w2_g.shape[1] // N_DEV, BF16, transpose_out=True, square_rhs=True,
                               halves=True)
    gw1_lo, gw1_hi = _mm_wgrad(h2t, du, "mm_dw1", w1_g.shape[2], BF16, halves=True)
    (dx1, dy, n2acc), gw2_lo_all = _mm_nt(du, w1_g, F32, "mm_dh2_norm2", ride=("a2a", gw2_lo), tm=TL, vmem_mb=60,
                                          epi=_bwd_norm2_epilogue(x1, dx2, y, norm2_g, modrows, TL))
    gwo = _mm_tn(mix, dy, "mm_dwout", False, d, BF16, 512).reshape(N_DEV, -1, d)
    (do, dg, dgate, dhs), gwo_all = _mm_nt(dy, wout_g, F32, "mm_dmix_mix", ride=("a2a", gwo),
                                           epi=_mix_bwd_epilogue(o[0], o[1], p, h[0], h[1], cb))
    dxcs, dwgs, laccs, rides, got = [], [], [], [("a2a", gw1_hi), ("a2a", gw2_hi)], []
    (dq0, dk0, dv0, dq1, dk1, dv1, dlg_lanes), gw1_lo_all = _ret_bwd(qkv, lg, do, s_prev, cc, ride=("a2a", gw1_lo))
    dqs, dks, dvs = [dq0, dq1], [dk0, dk1], [dv0, dv1]
    for dd in (0, 1):
        (dxc_, dwg_, lacc_), got_ = _lru_bwd(p, wg[dd], bg[dd], sp[dd:dd + 1], conv_w_full, conv_b, h[dd], cin[dd], dhs, dd, cb,
                                             ride=rides[dd])
        dxcs.append(dxc_); dwgs.append(dwg_); laccs.append(lacc_); got.append(got_)
    gw1_hi_all, gw2_hi_all = got
    gw1_all, gw2_all = [gw1_lo_all, gw1_hi_all], [gw2_lo_all, gw2_hi_all]
    (dp, cacc), _ = _assemble_dp(dqs, dks, dvs, dg, dgate, dxcs, p, conv_w_full, cosf, sins, cb)
    pack_b = jnp.concatenate([n2acc, facc, cacc, laccs[0], laccs[1], dlg_lanes, dwgs[0], dwgs[1]], axis=0)
    gwi, all_b = _mm_wgrad(hnt, dp, "mm_dwin", win_g.shape[2], BF16, ride=("gather", pack_b))
    (grad_x, n1acc), gwi_all = _mm_nt(dp, win_g, F32, "mm_dhn_norm1", ride=("a2a", gwi),
                                      epi=_bwd_norm1_epilogue(ctx2d, x2d, dx1, norm1_g, modrows, cb))
    tot = _sum_parts(all_b, "sum_small_grads")
    t_n1, g_b_ada, g_w_ada, dsilu = _ada_exchange(n1acc, all_b[:, 0:16, :], s16, wa2d)
    t_n2, t_f, t_conv, t_dlg = tot[0:8], tot[8:16], tot[16:24, :LRU_W], tot[40:48, :HEAD_DIM]
    t_l = [tot[24:32], tot[32:40]]
    t_dwd = [tot[48:48 + LRU_BD], tot[48 + LRU_BD:48 + 2 * LRU_BD]]
    loss = (0.5 / d) * jnp.sum(t_f[2])
    t_wa = jnp.stack([_blocks_from_lanes(t_dwd[dd][:, :LRU_W]) for dd in (0, 1)])
    t_wx = jnp.stack([_blocks_from_lanes(t_dwd[dd][:, LRU_W:]) for dd in (0, 1)])
    t_ba = jnp.stack([t_l[dd][0, :LRU_W] for dd in (0, 1)])
    t_bx = jnp.stack([t_l[dd][0, LRU_W:] for dd in (0, 1)])
    t_sp = jnp.stack([t_l[dd][1, :LRU_W] for dd in (0, 1)])
    dsilu_cc = dsilu[0]
    sg_cc = jax.nn.sigmoid(c_ctx)
    g_c_ctx = dsilu_cc * (sg_cc * (1.0 + c_ctx * (1.0 - sg_cc)))

    g_ret_decay = jnp.sum(t_dlg, axis=-1).reshape(2, HEADS) * jax.nn.sigmoid(-ret_decay[0])
    g_lambda_full = -t_sp * jax.nn.sigmoid(-lam_full)

    def my_cols(full):
        return lax.dynamic_slice(full, (0, me * sc_loc), (full.shape[0], sc_loc))

    small_g = dict(
        c_ctx=g_c_ctx[None], b_ada=g_b_ada, norm1_g=t_n1[4:5], norm2_g=t_n2[2:3], ret_decay=g_ret_decay,
        conv_w=my_cols(t_conv[0:4]), conv_b=t_conv[4:5], lru_wa=t_wa.reshape(-1, LRU_BD), lru_ba=my_cols(t_ba),
        lru_wx=t_wx.reshape(-1, LRU_BD), lru_bx=my_cols(t_bx), lru_lambda=my_cols(g_lambda_full), final_g=t_f[0:1])
    small = dict(
        c_ctx=(c_ctx, m_c_ctx, v_c_ctx), b_ada=(b_ada, m_b_ada, v_b_ada), norm1_g=(norm1_g, m_norm1_g, v_norm1_g),
        norm2_g=(norm2_g, m_norm2_g, v_norm2_g), ret_decay=(ret_decay, m_ret_decay, v_ret_decay),
        conv_w=(conv_w, m_conv_w, v_conv_w), conv_b=(conv_b, m_conv_b, v_conv_b), lru_wa=(lru_wa, m_lru_wa, v_lru_wa),
        lru_ba=(lru_ba, m_lru_ba, v_lru_ba), lru_wx=(lru_wx, m_lru_wx, v_lru_wx), lru_bx=(lru_bx, m_lru_bx, v_lru_bx),
        lru_lambda=(lru_lambda, m_lru_lambda, v_lru_lambda), final_g=(final_g, m_final_g, v_final_g))
    names = list(small)
    items = [(small_g[k],) + tuple(a.reshape(small_g[k].shape) for a in small[k]) for k in names]
    res = {}
    for k, it, (d_, m_, v_) in zip(names, items, _adam_many(items, "adam_small")):
        shape = small[k][0].shape
        res[k] = tuple(a.reshape(shape) for a in (it[0], d_, m_, v_))

    def big(parts, w, m, v, name):
        out = _sum_adam(parts, w[0], m[0], v[0], name)
        return tuple(a[None] for a in out)

    res["w_ada"] = big([g_w_ada[None]], w_ada, m_w_ada, v_w_ada, "adam_w_ada")
    res["w_in"] = big([gwi_all], w_in, m_w_in, v_w_in, "adam_w_in")
    res["w_out"] = big([gwo_all], w_out, m_w_out, v_w_out, "adam_w_out")
    res["w_mlp1"] = big(gw1_all, w_mlp1, m_w_mlp1, v_w_mlp1, "adam_w_mlp1")
    res["w_mlp2"] = big(gw2_all, w_mlp2, m_w_mlp2, v_w_mlp2, "adam_w_mlp2")

    order = ["c_ctx", "w_ada", "b_ada", "norm1_g", "norm2_g", "w_in", "ret_decay", "conv_w", "conv_b", "lru_wa", "lru_ba",
             "lru_wx", "lru_bx", "lru_lambda", "w_out", "w_mlp1", "w_mlp2", "final_g"]
    outs = [loss, grad_x[None]]
    for j in range(4):
        outs += [res[k][j] for k in order]
    return tuple(outs)
```

```python
import jax
import jax.numpy as jnp
from jax import lax
from jax.experimental import pallas as pl
from jax.experimental.pallas import tpu as pltpu

F32 = jnp.float32
BF16 = jnp.bfloat16
AXES = ("x", "y", "c")
N_DEV = 8
MESH = pl.DeviceIdType.MESH

HEADS = 4
HEAD_DIM = 128
CHUNK = 128
RET_W = HEADS * HEAD_DIM
LRU_W = 512
LRU_BLOCKS = 8
LRU_BD = LRU_W // LRU_BLOCKS
LRU_C = 8.0
EPS = 1e-6
K_SCALE = HEAD_DIM ** -0.5
ROPE_BASE = 10000.0
GRID_W = 64
TM = 256
TL = 512
SUB = 8

ADAM_LR = 0.001
ADAM_B1 = 0.9
ADAM_B2 = 0.999
ADAM_EPS = 1e-08
ADAM_WD = 0.01
ADAM_STEP = 10

COL_G, COL_XR, COL_GATE = 0, 1, 2

R_CSH1, R_CSC1, R_SH1, R_SC1, R_G1, R_SH2, R_SC2, R_G2 = range(8)


def _pcall(body, **kw):
    return pl.pallas_call(body, **kw)


def _cp(vmem_mb=48):
    return pltpu.CompilerParams(vmem_limit_bytes=vmem_mb << 20)


def _sds(shape, dtype):
    return jax.ShapeDtypeStruct(shape, dtype)


def _dot(a, b):
    return jnp.dot(a.astype(BF16), b.astype(BF16), preferred_element_type=F32)


def _dot_nt(a, b):
    return lax.dot_general(a.astype(BF16), b.astype(BF16), (((1,), (1,)), ((), ())), preferred_element_type=F32)


def _dot_tn(a, b):
    return lax.dot_general(a.astype(BF16), b.astype(BF16), (((0,), (0,)), ((), ())), preferred_element_type=F32)


def _sigmoid(x):
    return 0.5 * jnp.tanh(0.5 * x) + 0.5


def _gelu(x):
    return 0.5 * x * (1.0 + jnp.tanh(0.7978845608028654 * (x + 0.044715 * x * x * x)))


def _dgelu(x):
    t = jnp.tanh(0.7978845608028654 * (x + 0.044715 * x * x * x))
    return 0.5 * (1.0 + t) + 0.5 * x * (1.0 - t * t) * 0.7978845608028654 * (1.0 + 3.0 * 0.044715 * x * x)


def _rows_iota(shape):
    return lax.broadcasted_iota(jnp.int32, shape, 0)


def _tile_order(dirn, s, cb, nb):
    if dirn == 0:
        return s
    return jnp.where(s < cb, cb - 1 - s, nb - 1 - (s - cb))


_SEMS = [pltpu.SemaphoreType.DMA((7,)), pltpu.SemaphoreType.DMA((7,)), pltpu.SemaphoreType.DMA(())]
_ANY = pl.BlockSpec(memory_space=pl.ANY)


def _gather_copies(x_ref, out_ref, send_sems, recv_sems, local_sem):
    mx, my, mc = lax.axis_index("x"), lax.axis_index("y"), lax.axis_index("c")
    me, sibling = (mx, my, mc), (mx, my, 1 - mc)
    chips = [(1 - mx, my), (mx, 1 - my), (1 - mx, 1 - my)]

    def slot(px, py, pc):
        return out_ref.at[4 * px + 2 * py + pc]

    def copy(k, block, to, src=None):
        return pltpu.make_async_remote_copy(
            src_ref=slot(*block) if src is None else src, dst_ref=slot(*block),
            send_sem=send_sems.at[k], recv_sem=recv_sems.at[k], device_id=to, device_id_type=MESH)

    mine = pltpu.make_async_copy(x_ref, slot(*me), local_sem)
    first = [copy(0, me, sibling, src=x_ref)] + [copy(1 + j, me, (*chip, mc), src=x_ref) for j, chip in enumerate(chips)]
    passed = [copy(4 + j, (*chip, mc), sibling) for j, chip in enumerate(chips)]
    recv_ici = [copy(1 + j, (*chip, mc), me) for j, chip in enumerate(chips)]
    recv_d2d = [copy(0, sibling, me)] + [copy(4 + j, (*chip, 1 - mc), me) for j, chip in enumerate(chips)]
    return mine, first, passed, recv_ici, recv_d2d


def _gather_start(*refs):
    mine, first, _, _, _ = _gather_copies(*refs)
    mine.start()
    for cp in first:
        cp.start()


def _gather_pass_on(*refs):
    _, _, passed, recv_ici, _ = _gather_copies(*refs)
    for landed, onward in zip(recv_ici, passed):
        landed.wait_recv()
        onward.start()


def _gather_finish(*refs):
    mine, first, passed, _, recv_d2d = _gather_copies(*refs)
    for landed in recv_d2d:
        landed.wait_recv()
    for cp in first + passed:
        cp.wait_send()
    mine.wait()


def _a2a_copies(g_ref, out_ref, send_sems, recv_sems, local_sem):
    mx, my, mc = lax.axis_index("x"), lax.axis_index("y"), lax.axis_index("c")
    me = 4 * mx + 2 * my + mc
    mine = pltpu.make_async_copy(g_ref.at[me], out_ref.at[me], local_sem)
    copies = []
    for k in range(1, N_DEV):
        px = 1 - mx if (k >> 2) & 1 else mx
        py = 1 - my if (k >> 1) & 1 else my
        pc = 1 - mc if k & 1 else mc
        copies.append(pltpu.make_async_remote_copy(
            src_ref=g_ref.at[4 * px + 2 * py + pc], dst_ref=out_ref.at[me],
            send_sem=send_sems.at[k - 1], recv_sem=recv_sems.at[k - 1],
            device_id=(px, py, pc), device_id_type=MESH))
    return mine, copies


def _a2a_start(*refs):
    mine, copies = _a2a_copies(*refs)
    mine.start()
    for cp in copies:
        cp.start()


def _a2a_finish(*refs):
    mine, copies = _a2a_copies(*refs)
    for cp in copies:
        cp.wait()
    mine.wait()


_EXCHANGES = {"gather": (_gather_start, _gather_finish), "a2a": (_a2a_start, _a2a_finish)}
PASS_ON_LEAD = 3


def _exchange_shape(kind, src):
    return _sds((N_DEV,) + src.shape if kind == "gather" else src.shape, src.dtype)


def _all_gather_small(x, name):
    def body(x_ref, out_ref, send_sems, recv_sems, local_sem):
        mx, my, mc = lax.axis_index("x"), lax.axis_index("y"), lax.axis_index("c")
        me = 4 * mx + 2 * my + mc
        mine = pltpu.make_async_copy(x_ref, out_ref.at[me], local_sem)
        mine.start()
        copies = []
        for k in range(1, N_DEV):
            peer = (1 - mx if (k >> 2) & 1 else mx, 1 - my if (k >> 1) & 1 else my, 1 - mc if k & 1 else mc)
            copies.append(pltpu.make_async_remote_copy(
                src_ref=x_ref, dst_ref=out_ref.at[me], send_sem=send_sems.at[k - 1], recv_sem=recv_sems.at[k - 1],
                device_id=peer, device_id_type=MESH))
            copies[-1].start()
        for cp in copies:
            cp.wait()
        mine.wait()

    return _pcall(body, name=name, out_shape=_exchange_shape("gather", x), in_specs=[_ANY], out_specs=_ANY,
                  scratch_shapes=list(_SEMS))(x)


def _pcall_ride(body, ride, args, *, name, grid, in_specs, out_specs, out_shape, scratch_shapes=(), compiler_params=None):
    if ride is None:
        out = _pcall(body, name=name, grid=grid, in_specs=in_specs, out_specs=out_specs, out_shape=out_shape,
                     scratch_shapes=list(scratch_shapes), compiler_params=compiler_params)(*args)
        return out, None
    kind, src = ride
    start, finish = _EXCHANGES[kind]
    single = not isinstance(out_shape, (tuple, list))
    out_specs_t = (out_specs,) if single else tuple(out_specs)
    out_shape_t = (out_shape,) if single else tuple(out_shape)
    n_in, n_out, n_sc = len(in_specs), len(out_shape_t), len(scratch_shapes)

    def wrapped(*refs):
        ins, src_ref = refs[:n_in], refs[n_in]
        outs, dst_ref = refs[n_in + 1:n_in + 1 + n_out], refs[n_in + 1 + n_out]
        scratch = refs[n_in + 2 + n_out:n_in + 2 + n_out + n_sc]
        sems = refs[n_in + 2 + n_out + n_sc:]
        first = pl.program_id(0) == 0
        last = pl.program_id(0) == grid[0] - 1
        for ax in range(1, len(grid)):
            first = jnp.logical_and(first, pl.program_id(ax) == 0)
            last = jnp.logical_and(last, pl.program_id(ax) == grid[ax] - 1)

        @pl.when(first)
        def _():
            start(src_ref, dst_ref, *sems)

        body(*ins, *outs, *scratch)

        if kind == "gather":
            @pl.when(pl.program_id(0) == max(grid[0] - PASS_ON_LEAD, 0))
            def _():
                _gather_pass_on(src_ref, dst_ref, *sems)

        @pl.when(last)
        def _():
            finish(src_ref, dst_ref, *sems)

    res = _pcall(wrapped, name=name, grid=grid, in_specs=list(in_specs) + [_ANY], out_specs=out_specs_t + (_ANY,),
                 out_shape=out_shape_t + (_exchange_shape(kind, src),),
                 scratch_shapes=list(scratch_shapes) + list(_SEMS), compiler_params=compiler_params)(*args, src)
    return (res[0] if single else tuple(res[:-1])), res[-1]


class _Epilogue:
    def __init__(self, fn, args, in_specs, out_specs, out_shape, steps=None, lhs_map=None, delay=0):
        self.fn, self.args, self.in_specs, self.out_specs, self.out_shape = fn, tuple(args), list(in_specs), out_specs, out_shape
        self.steps, self.lhs_map, self.delay = steps, lhs_map, delay


def _mm_nn(a, w, out_dtype, name, square_lhs=False, relu_out=False, ride=None, tm=TM, epi=None, vmem_mb=48):
    m, k = a.shape
    nb, _, bn = w.shape
    tm = min(tm, m)
    delay = 0 if epi is None else epi.delay
    steps = m // tm

    def body(*refs):
        a_ref, w_ref = refs[:2]
        av = a_ref[...]
        if square_lhs:
            av = av * av
        if delay:
            assert nb == 1
            held = refs[-1]

            @pl.when(pl.program_id(0) == 0)
            def _():
                held[...] = jnp.zeros_like(held)

            fresh = jnp.dot(av, w_ref[0], preferred_element_type=F32)
            epi.fn(held[...], slice(0, tm), *refs[2:-1])
            held[...] = fresh
            return
        if epi is not None:
            assert nb == 1
            epi.fn(jnp.dot(av, w_ref[0], preferred_element_type=F32), slice(0, tm), *refs[2:])
            return
        for j in range(nb):
            r = jnp.dot(av, w_ref[j], preferred_element_type=F32)
            if relu_out:
                r = jnp.maximum(r, 0.0)
            refs[2][:, j * bn:(j + 1) * bn] = r.astype(out_dtype)

    lhs_map = (lambda i: (i, 0)) if not delay else (lambda i: (jnp.minimum(i, steps - 1), 0))
    in_specs = [pl.BlockSpec((tm, k), lhs_map), pl.BlockSpec((nb, k, bn), lambda i: (0, 0, 0))]
    scratch = []
    if epi is None:
        args, out_specs, out_shape = (a, w), pl.BlockSpec((tm, nb * bn), lambda i: (i, 0)), _sds((m, nb * bn), out_dtype)
    else:
        args, out_specs, out_shape = (a, w) + epi.args, epi.out_specs, epi.out_shape
        in_specs += epi.in_specs
        if delay:
            scratch.append(pltpu.VMEM((tm, bn), F32))
    out, ex = _pcall_ride(body, ride, args, name=name, grid=(steps + delay,), in_specs=in_specs, out_specs=out_specs,
                          out_shape=out_shape, scratch_shapes=scratch, compiler_params=_cp(vmem_mb))
    return out if ride is None else (out, ex)


def _mm_nt(dy, w, out_dtype, name, relu_mul=None, ride=None, tm=TM, epi=None, vmem_mb=48):
    m = dy.shape[0]
    nb, k, bn = w.shape
    tm = min(tm, m)
    n_extra = (0 if relu_mul is None else 1) + (0 if epi is None else len(epi.args))

    delay = 0 if epi is None else epi.delay

    def body(*refs):
        dy_ref, w_ref = refs[:2]
        scratch = refs[-1 - delay:]
        extra, outs, wt = refs[2:2 + n_extra], refs[2 + n_extra:-1 - delay], scratch[0]

        @pl.when(pl.program_id(0) == 0)
        def _():
            for j in range(nb):
                wt[j * bn:(j + 1) * bn, :] = w_ref[j].T
            if delay:
                scratch[1][...] = jnp.zeros_like(scratch[1])

        if delay:
            held = scratch[1]
            fresh = jnp.dot(dy_ref[...], wt[...], preferred_element_type=F32)
            epi.fn(held[...], slice(0, tm), *extra, *outs)
            held[...] = fresh
            return
        acc = jnp.dot(dy_ref[...], wt[...], preferred_element_type=F32)
        if epi is not None:
            epi.fn(acc, slice(0, tm), *extra, *outs)
            return
        if relu_mul is not None:
            acc = acc * (2.0 * extra[0][...].astype(F32))
        outs[0][...] = acc.astype(out_dtype)

    steps = m // tm if epi is None or epi.steps is None else epi.steps
    lhs_tile = (lambda i: (i, 0)) if epi is None or epi.lhs_map is None else epi.lhs_map
    lhs_map = lhs_tile if not delay else (lambda i: lhs_tile(jnp.minimum(i, steps - 1)))
    in_specs = [pl.BlockSpec((tm, nb * bn), lhs_map), pl.BlockSpec((nb, k, bn), lambda i: (0, 0, 0))]
    args = [dy, w]
    if relu_mul is not None:
        in_specs.append(pl.BlockSpec((tm, k), lambda i: (i, 0)))
        args.append(relu_mul)
    scratch = [pltpu.VMEM((nb * bn, k), BF16)]
    if epi is None:
        out_specs, out_shape = pl.BlockSpec((tm, k), lambda i: (i, 0)), _sds((m, k), out_dtype)
    else:
        args += list(epi.args)
        in_specs += epi.in_specs
        out_specs, out_shape = epi.out_specs, epi.out_shape
        if delay:
            scratch.append(pltpu.VMEM((tm, k), F32))
    out, ex = _pcall_ride(
        body, ride, args, name=name, grid=(steps + delay,), in_specs=in_specs, out_specs=out_specs, out_shape=out_shape,
        scratch_shapes=scratch, compiler_params=_cp(vmem_mb))
    return out if ride is None else (out, ex)


def _mm_tn(a, b, name, col_blocks, block, out_dtype, tm, square_lhs=False):
    m, k = a.shape
    nn = b.shape[1]
    steps = m // tm
    if col_blocks:
        nblk, acc_shape = nn // block, (k, block)
        a_spec = pl.BlockSpec((tm, k), lambda j, s: (s, 0))
        b_spec = pl.BlockSpec((tm, block), lambda j, s: (s, j))
    else:
        nblk, acc_shape = k // block, (block, nn)
        a_spec = pl.BlockSpec((tm, block), lambda j, s: (s, j))
        b_spec = pl.BlockSpec((tm, nn), lambda j, s: (s, 0))

    def body(a_ref, b_ref, o_ref, acc):
        s = pl.program_id(1)

        @pl.when(s == 0)
        def _():
            acc[...] = jnp.zeros_like(acc)

        av = a_ref[...]
        if square_lhs:
            av = av.astype(F32)
            av = (av * av).astype(BF16)
        acc[...] += _dot_tn(av, b_ref[...])

        @pl.when(s == steps - 1)
        def _():
            o_ref[...] = acc[...].astype(out_dtype)

    return _pcall(
        body, name=name, grid=(nblk, steps), in_specs=[a_spec, b_spec],
        out_specs=pl.BlockSpec((None,) + acc_shape, lambda j, s: (j, 0, 0)),
        out_shape=_sds((nblk,) + acc_shape, out_dtype),
        scratch_shapes=[pltpu.VMEM(acc_shape, F32)], compiler_params=_cp(),
    )(a, b)


def _mm_wgrad(at, b, name, bn, out_dtype, transpose_out=False, square_rhs=False, halves=False, ride=None):
    k, m = at.shape
    nblk = b.shape[1] // bn
    rows, cols = (bn, k) if transpose_out else (k, bn)
    nout = 2 if halves else 1
    per = rows // nout

    def body(a_ref, b_ref, *o_refs):
        bv = b_ref[...]
        if square_rhs:
            bv = bv * bv
        r = jnp.dot(a_ref[...], bv, preferred_element_type=F32)
        r = (r.T if transpose_out else r).astype(out_dtype)
        for i, o_ref in enumerate(o_refs):
            o_ref[...] = r[i * per:(i + 1) * per, :]

    out, ex = _pcall_ride(
        body, ride, (at, b), name=name, grid=(nblk,),
        in_specs=[pl.BlockSpec((k, m), lambda j: (0, 0)), pl.BlockSpec((m, bn), lambda j: (0, j))],
        out_specs=tuple(pl.BlockSpec((None, per, cols), lambda j: (j, 0, 0)) for _ in range(nout)),
        out_shape=tuple(_sds((nblk, per, cols), out_dtype) for _ in range(nout)), compiler_params=_cp())
    out = out if halves else out[0]
    return out if ride is None else (out, ex)


def _mm_in(hn, w, cosf, sins, ride):
    m, k = hn.shape
    nb, _, bn = w.shape

    def body(a_ref, w_ref, c_ref, s_ref, qkv_ref, rest_ref, pt):
        av = a_ref[...]
        for j in range(nb):
            pt[:, j * bn:(j + 1) * bn] = jnp.dot(av, w_ref[j], preferred_element_type=F32)
        cf, ss = c_ref[...], s_ref[...]
        for h in range(HEADS):
            sq = slice(h * HEAD_DIM, (h + 1) * HEAD_DIM)
            sk = slice(RET_W + h * HEAD_DIM, RET_W + (h + 1) * HEAD_DIM)
            qkv_ref[:, sq] = _rot(pt[:, sq], cf, ss).astype(BF16)
            qkv_ref[:, sk] = (_rot(pt[:, sk], cf, ss) * K_SCALE).astype(BF16)
        qkv_ref[:, 2 * RET_W:] = pt[:, 2 * RET_W:3 * RET_W].astype(BF16)
        rest_ref[...] = pt[:, 3 * RET_W:]

    tab = pl.BlockSpec((TM, HEAD_DIM), lambda i: (i, 0))
    wide = pl.BlockSpec((TM, 3 * RET_W), lambda i: (i, 0))
    return _pcall_ride(
        body, ride, (hn, w, cosf, sins), name="mm_in", grid=(m // TM,),
        in_specs=[pl.BlockSpec((TM, k), lambda i: (i, 0)), pl.BlockSpec((nb, k, bn), lambda i: (0, 0, 0)), tab, tab],
        out_specs=(wide, wide), out_shape=(_sds((m, 3 * RET_W), BF16), _sds((m, nb * bn - 3 * RET_W), F32)),
        scratch_shapes=[pltpu.VMEM((TM, nb * bn), F32)], compiler_params=_cp())


def _mod_exchange(pack_a, silu_cc, w_ada, b_cols):
    d, cols = pack_a.shape[1], w_ada.shape[1]

    def body(pa_ref, scc_ref, w_ref, b_ref, alla_ref, s16_ref, modp_ref, mp, send1, recv1, send2, recv2):
        mx, my, mc = lax.axis_index("x"), lax.axis_index("y"), lax.axis_index("c")
        me = 4 * mx + 2 * my + mc

        def exchange(src_ref, dst_ref, send, recv):
            copies = []
            for k in range(1, N_DEV):
                peer = (1 - mx if (k >> 2) & 1 else mx, 1 - my if (k >> 1) & 1 else my, 1 - mc if k & 1 else mc)
                copies.append(pltpu.make_async_remote_copy(
                    src_ref=src_ref, dst_ref=dst_ref.at[me], send_sem=send.at[k - 1], recv_sem=recv.at[k - 1],
                    device_id=peer, device_id_type=MESH))
                copies[-1].start()
            dst_ref[me] = src_ref[...]
            for cp in copies:
                cp.wait()

        exchange(pa_ref, alla_ref, send1, recv1)
        s16_ref[...] = jnp.zeros_like(s16_ref)
        for j in range(N_DEV):
            s16_ref[j:j + 1, :] = alla_ref[j, 0:1, :]
        s16_ref[N_DEV:N_DEV + 1, :] = scc_ref[...]
        mp[...] = _dot(s16_ref[...], w_ref[...]) + b_ref[...]
        exchange(mp, modp_ref, send2, recv2)

    vmem = pl.BlockSpec(memory_space=pltpu.VMEM)
    sem7 = pltpu.SemaphoreType.DMA((N_DEV - 1,))
    return _pcall(
        body, name="mod_exchange", in_specs=[vmem] * 4, out_specs=(vmem,) * 3,
        out_shape=(_sds((N_DEV,) + pack_a.shape, F32), _sds((2 * N_DEV, d), F32), _sds((N_DEV, 2 * N_DEV, cols), F32)),
        scratch_shapes=[pltpu.VMEM((2 * N_DEV, cols), F32), sem7, sem7, sem7, sem7], compiler_params=_cp(),
    )(pack_a, silu_cc, w_ada, b_cols)


def _ada_bwd(s16, dm_cols, w_ada):
    def body(s_ref, d_ref, w_ref, gw_ref, ds_ref):
        gw_ref[...] = _dot_tn(s_ref[...], d_ref[...])
        ds_ref[...] = _dot_nt(d_ref[...], w_ref[...])

    return _pcall(body, name="ada_bwd",
                  out_shape=(_sds(w_ada.shape, F32), _sds(s16.shape, F32)), compiler_params=_cp())(s16, dm_cols, w_ada)


def _norm1_fwd(ctx, x, g, modrows, cb, ride=None):
    l_len, d = ctx.shape
    nb = (l_len + x.shape[0]) // TM

    def body(ctx_ref, x_ref, g_ref, m_ref, o_ref, ot_ref):
        is_ctx = pl.program_id(0) < cb
        xin = jnp.where(is_ctx, ctx_ref[...], x_ref[...])
        sh = jnp.where(is_ctx, m_ref[R_CSH1:R_CSH1 + 1, :], m_ref[R_SH1:R_SH1 + 1, :])
        sc = jnp.where(is_ctx, m_ref[R_CSC1:R_CSC1 + 1, :], m_ref[R_SC1:R_SC1 + 1, :])
        ms = jnp.mean(xin * xin, axis=-1, keepdims=True)
        n = xin * lax.rsqrt(ms + EPS) * g_ref[...]
        hn = n * (1.0 + sc) + sh
        o_ref[...] = hn.astype(BF16)
        ot_ref[...] = hn.T.astype(BF16)

    return _pcall_ride(
        body, ride, (ctx, x, g, modrows), name="norm1_fwd", grid=(nb,),
        in_specs=[pl.BlockSpec((TM, d), lambda i: (jnp.minimum(i, cb - 1), 0)),
                  pl.BlockSpec((TM, d), lambda i: (jnp.maximum(i - cb, 0), 0)),
                  pl.BlockSpec((1, d), lambda i: (0, 0)), pl.BlockSpec((8, d), lambda i: (0, 0))],
        out_specs=(pl.BlockSpec((TM, d), lambda i: (i, 0)), pl.BlockSpec((d, TM), lambda i: (0, i))),
        out_shape=(_sds((nb * TM, d), BF16), _sds((d, nb * TM), BF16)), compiler_params=_cp())


def _mix_fwd(o_f, o_b, p, h_f, h_b, cb, t_len):
    def body(of_ref, ob_ref, g_ref, gate_ref, hf_ref, hb_ref, mix_ref):
        o = of_ref[...] + ob_ref[...]
        g = g_ref[...]
        sg = g * _sigmoid(g)
        for hh in range(HEADS):
            sl = slice(hh * HEAD_DIM, (hh + 1) * HEAD_DIM)
            oh = o[:, sl]
            yc = oh - jnp.mean(oh, axis=-1, keepdims=True)
            var = jnp.mean(yc * yc, axis=-1, keepdims=True)
            mix_ref[:, sl] = (sg[:, sl] * (yc * lax.rsqrt(var + EPS))).astype(BF16)
        mix_ref[:, RET_W:] = ((hf_ref[...] + hb_ref[...]) * _gelu(gate_ref[...])).astype(BF16)

    row = lambda i: (i + cb, 0)
    return _pcall(
        body, name="mix_fwd", grid=(t_len // TM,),
        in_specs=[pl.BlockSpec((TM, RET_W), row), pl.BlockSpec((TM, RET_W), row),
                  pl.BlockSpec((TM, RET_W), lambda i: (i + cb, COL_G)), pl.BlockSpec((TM, LRU_W), lambda i: (i + cb, COL_GATE)),
                  pl.BlockSpec((TM, LRU_W), row), pl.BlockSpec((TM, LRU_W), row)],
        out_specs=pl.BlockSpec((TM, RET_W + LRU_W), lambda i: (i, 0)),
        out_shape=_sds((t_len, RET_W + LRU_W), BF16), compiler_params=_cp(),
    )(o_f, o_b, p, p, h_f, h_b)


def _res_norm2_epilogue(x, g, modrows, tm):
    t_len, d = x.shape
    tm = min(tm, t_len)

    def fn(y, rows, x_ref, g_ref, m_ref, y_ref, x1_ref, h2_ref, h2t_ref):
        y_ref[rows, :] = y
        x1 = x_ref[rows, :] + m_ref[R_G1:R_G1 + 1, :] * y
        ms = jnp.mean(x1 * x1, axis=-1, keepdims=True)
        n = x1 * lax.rsqrt(ms + EPS) * g_ref[...]
        x1_ref[rows, :] = x1
        h2 = n * (1.0 + m_ref[R_SC2:R_SC2 + 1, :]) + m_ref[R_SH2:R_SH2 + 1, :]
        h2_ref[rows, :] = h2.astype(BF16)
        h2t_ref[:, rows] = h2.T.astype(BF16)

    t = pl.BlockSpec((tm, d), lambda i: (i, 0))
    return _Epilogue(
        fn, (x, g, modrows),
        in_specs=[t, pl.BlockSpec((1, d), lambda i: (0, 0)), pl.BlockSpec((8, d), lambda i: (0, 0))],
        out_specs=(t, t, t, pl.BlockSpec((d, tm), lambda i: (0, i))),
        out_shape=(_sds((t_len, d), F32), _sds((t_len, d), F32), _sds((t_len, d), BF16), _sds((d, t_len), BF16)))


def _tile(i):
    return jnp.maximum(i - 1, 0)


def _zero_at_start(acc_ref, rows):
    @pl.when(pl.program_id(0) == 0)
    def _():
        acc_ref[...] = jnp.zeros_like(acc_ref)


def _final_epilogue(x1, target, fg, modrows, tm):
    t_len, d = x1.shape

    def fn(z, rows, x1_ref, t_ref, fg_ref, m_ref, dx2_ref, dz_ref, dzt_ref, acc_ref):
        _zero_at_start(acc_ref, rows)
        g2 = m_ref[R_G2:R_G2 + 1, :]
        x2 = x1_ref[rows, :] + g2 * z
        rstd = lax.rsqrt(jnp.mean(x2 * x2, axis=-1, keepdims=True) + EPS)
        xh = x2 * rstd
        fg = fg_ref[...]
        e = xh * fg - t_ref[rows, :]
        dy = e * (1.0 / d)
        dxh = dy * fg
        dx2 = rstd * (dxh - xh * jnp.mean(dxh * xh, axis=-1, keepdims=True))
        dx2_ref[rows, :] = dx2
        dz = g2 * dx2
        dz_ref[rows, :] = dz.astype(BF16)
        dzt_ref[:, rows] = dz.T.astype(BF16)
        acc_ref[0:1, :] += jnp.sum(dy * xh, axis=0, keepdims=True)
        acc_ref[1:2, :] += jnp.sum(dx2 * z, axis=0, keepdims=True)
        acc_ref[2:3, :] += jnp.sum(e * e, axis=0, keepdims=True)

    t = pl.BlockSpec((tm, d), lambda i: (i, 0))
    return _Epilogue(
        fn, (x1, target, fg, modrows),
        in_specs=[t, t, pl.BlockSpec((1, d), lambda i: (0, 0)), pl.BlockSpec((8, d), lambda i: (0, 0))],
        out_specs=(t, t, pl.BlockSpec((d, tm), lambda i: (0, i)), pl.BlockSpec((8, d), lambda i: (0, 0))),
        out_shape=(_sds((t_len, d), F32), _sds((t_len, d), BF16), _sds((d, t_len), BF16), _sds((8, d), F32)))


def _bwd_norm2_epilogue(x1, dx2, y, g, modrows, tm):
    t_len, d = x1.shape

    def fn(dh2, rows, x1_ref, dx2_ref, y_ref, g_ref, m_ref, dx1_ref, dy_ref, acc_ref):
        _zero_at_start(acc_ref, rows)
        x1 = x1_ref[rows, :]
        rstd = lax.rsqrt(jnp.mean(x1 * x1, axis=-1, keepdims=True) + EPS)
        xh = x1 * rstd
        gn = g_ref[...]
        dn = dh2 * (1.0 + m_ref[R_SC2:R_SC2 + 1, :])
        dxh = dn * gn
        dx1 = dx2_ref[rows, :] + rstd * (dxh - xh * jnp.mean(dxh * xh, axis=-1, keepdims=True))
        dx1_ref[rows, :] = dx1
        dy_ref[rows, :] = (m_ref[R_G1:R_G1 + 1, :] * dx1).astype(BF16)
        acc_ref[0:1, :] += jnp.sum(dh2, axis=0, keepdims=True)
        acc_ref[1:2, :] += jnp.sum(dh2 * xh * gn, axis=0, keepdims=True)
        acc_ref[2:3, :] += jnp.sum(dn * xh, axis=0, keepdims=True)
        acc_ref[3:4, :] += jnp.sum(dx1 * y_ref[rows, :], axis=0, keepdims=True)

    t = pl.BlockSpec((tm, d), lambda i: (i, 0))
    return _Epilogue(
        fn, (x1, dx2, y, g, modrows),
        in_specs=[t, t, t, pl.BlockSpec((1, d), lambda i: (0, 0)), pl.BlockSpec((8, d), lambda i: (0, 0))],
        out_specs=(t, t, pl.BlockSpec((8, d), lambda i: (0, 0))),
        out_shape=(_sds((t_len, d), F32), _sds((t_len, d), BF16), _sds((8, d), F32)))


def _mix_bwd_epilogue(o_f, o_b, p, h_f, h_b, cb):
    n = o_f.shape[0]
    tile = _tile

    def fn(dm, rows, of_ref, ob_ref, g_ref, gate_ref, hf_ref, hb_ref, do_ref, dg_ref, dgate_ref, dhs_ref):
        keep = jnp.where(pl.program_id(0) - 1 < cb, 0.0, 1.0)
        dm = dm * keep
        o = of_ref[rows, :] + ob_ref[rows, :]
        g = g_ref[rows, :]
        s = _sigmoid(g)
        sg = g * s
        dsg = s * (1.0 + g * (1.0 - s))
        for hh in range(HEADS):
            sl = slice(hh * HEAD_DIM, (hh + 1) * HEAD_DIM)
            oh = o[:, sl]
            yc = oh - jnp.mean(oh, axis=-1, keepdims=True)
            rs = lax.rsqrt(jnp.mean(yc * yc, axis=-1, keepdims=True) + EPS)
            gn = yc * rs
            dret = dm[:, sl]
            dgn = dret * sg[:, sl]
            dg_ref[rows, sl] = (dret * gn * dsg[:, sl]).astype(BF16)
            do_ref[rows, sl] = (rs * (dgn - jnp.mean(dgn, axis=-1, keepdims=True)
                                      - gn * jnp.mean(dgn * gn, axis=-1, keepdims=True))).astype(BF16)
        dlru = dm[:, RET_W:]
        gate = gate_ref[rows, :]
        dhs_ref[rows, :] = dlru * _gelu(gate)
        dgate_ref[rows, :] = (dlru * (hf_ref[rows, :] + hb_ref[rows, :]) * _dgelu(gate)).astype(BF16)

    t = pl.BlockSpec((TM, RET_W), lambda i: (tile(i), 0))
    return _Epilogue(
        fn, (o_f, o_b, p, p, h_f, h_b),
        in_specs=[t, t, pl.BlockSpec((TM, RET_W), lambda i: (tile(i), COL_G)),
                  pl.BlockSpec((TM, LRU_W), lambda i: (tile(i), COL_GATE)), t, t],
        out_specs=(t, t, t, t), out_shape=(_sds((n, RET_W), BF16),) * 3 + (_sds((n, RET_W), F32),),
        steps=n // TM, lhs_map=lambda i: (jnp.maximum(i - cb, 0), 0), delay=1)


def _bwd_norm1_epilogue(ctx, x, dx1, g, modrows, cb):
    t_len, d = x.shape

    def fn(dh, rows, ctx_ref, x_ref, dx1_ref, g_ref, m_ref, gx_ref, acc_ref):
        is_ctx = pl.program_id(0) < cb
        _zero_at_start(acc_ref, rows)
        xin = jnp.where(is_ctx, ctx_ref[rows, :], x_ref[rows, :])
        sc = jnp.where(is_ctx, m_ref[R_CSC1:R_CSC1 + 1, :], m_ref[R_SC1:R_SC1 + 1, :])
        rstd = lax.rsqrt(jnp.mean(xin * xin, axis=-1, keepdims=True) + EPS)
        xh = xin * rstd
        gn = g_ref[...]
        dn = dh * (1.0 + sc)
        dxh = dn * gn
        gx_ref[rows, :] = dx1_ref[rows, :] + rstd * (dxh - xh * jnp.mean(dxh * xh, axis=-1, keepdims=True))
        s0 = jnp.sum(dh, axis=0, keepdims=True)
        s1 = jnp.sum(dh * xh * gn, axis=0, keepdims=True)
        acc_ref[4:5, :] += jnp.sum(dn * xh, axis=0, keepdims=True)
        acc_ref[0:1, :] += jnp.where(is_ctx, s0, 0.0)
        acc_ref[1:2, :] += jnp.where(is_ctx, s1, 0.0)
        acc_ref[2:3, :] += jnp.where(is_ctx, 0.0, s0)
        acc_ref[3:4, :] += jnp.where(is_ctx, 0.0, s1)

    lat = pl.BlockSpec((TM, d), lambda i: (jnp.maximum(i - cb, 0), 0))
    return _Epilogue(
        fn, (ctx, x, dx1, g, modrows),
        in_specs=[pl.BlockSpec((TM, d), lambda i: (jnp.minimum(i, cb - 1), 0)),
                  lat, lat, pl.BlockSpec((1, d), lambda i: (0, 0)), pl.BlockSpec((8, d), lambda i: (0, 0))],
        out_specs=(lat, pl.BlockSpec((8, d), lambda i: (0, 0))),
        out_shape=(_sds((t_len, d), F32), _sds((8, d), F32)))


def _rot(x, cf, ss):
    return x * cf + pltpu.roll(x, HEAD_DIM // 2, 1) * ss


def _decay_exponents(dirn):
    ii = lax.broadcasted_iota(jnp.int32, (CHUNK, CHUNK), 0)
    jj = lax.broadcasted_iota(jnp.int32, (CHUNK, CHUNK), 1)
    rel = ii - jj if dirn == 0 else jj - ii
    pos = ii.astype(F32)
    if dirn == 0:
        cq, cs = pos + 1.0, (CHUNK - 1.0) - pos
    else:
        cq, cs = CHUNK - pos, pos
    return rel, jnp.maximum(rel, 0).astype(F32), cq, cs


def _store_decay(lg_ref, dec, relf_ref=None):
    for dirn in (0, 1):
        rel, relf, cq, cs = _decay_exponents(dirn)
        if relf_ref is not None:
            relf_ref[dirn] = relf
        for h in range(HEADS):
            lgv = lg_ref[dirn, h]
            wq, ws = jnp.exp(lgv * cq), jnp.exp(lgv * cs)
            dec[dirn, h, 0] = jnp.where(rel >= 0, jnp.exp(lgv * relf), 0.0)
            dec[dirn, h, 1] = wq
            dec[dirn, h, 2] = ws
            if relf_ref is not None:
                dec[dirn, h, 3] = wq * cq
                dec[dirn, h, 4] = ws * cs


def _ret_rows(cc, nc, step_of):
    return [lambda s, dirn=dirn: _tile_order(dirn, step_of(s), cc, nc) for dirn in (0, 1)]


def _ret_in_specs(rows):
    specs = []
    for row in rows:
        specs += [pl.BlockSpec((CHUNK, RET_W), lambda s, o=o, row=row: (row(s), o)) for o in (0, 1, 2)]
    return specs


def _ret_fwd(qkv, lg, cc, ride=None):
    n = qkv.shape[0]
    nc = n // CHUNK
    rows = _ret_rows(cc, nc, lambda s: s)

    def body(lg_ref, q0, k0, v0, q1, k1, v1, o0, o1, sp0, sp1, st, dec):
        @pl.when(pl.program_id(0) == 0)
        def _():
            st[...] = jnp.zeros_like(st)
            _store_decay(lg_ref, dec)

        refs = ((q0, k0, v0, o0, sp0), (q1, k1, v1, o1, sp1))
        chains = [(dirn, h, slice(h * HEAD_DIM, (h + 1) * HEAD_DIM)) for dirn in (0, 1) for h in range(HEADS)]
        scores, cross, update = [], [], []
        for dirn, h, sl in chains:
            q_ref, k_ref, v_ref, _, sp_ref = refs[dirn]
            q, k, v = q_ref[:, sl], k_ref[:, sl], v_ref[:, sl]
            sp = st[dirn, h]
            sp_ref[h] = sp
            scores.append(_dot_nt(q, k))
            cross.append(_dot(q * dec[dirn, h, 1], sp))
            update.append(_dot_tn(k * dec[dirn, h, 2], v))
        masked = [(a * dec[dirn, h, 0]).astype(BF16) for a, (dirn, h, _) in zip(scores, chains)]
        intra = [_dot(sc, refs[dirn][2][:, sl]) for sc, (dirn, h, sl) in zip(masked, chains)]
        for (dirn, h, sl), o_in, o_cr, upd in zip(chains, intra, cross, update):
            refs[dirn][3][:, sl] = o_in + o_cr
            st[dirn, h] = jnp.exp(lg_ref[dirn, h] * CHUNK) * st[dirn, h] + upd

    o_specs = [pl.BlockSpec((CHUNK, RET_W), lambda s, row=row: (row(s), 0)) for row in rows]
    state = pl.BlockSpec((None, HEADS, CHUNK, HEAD_DIM), lambda s: (s, 0, 0, 0))
    return _pcall_ride(
        body, ride, (lg,) + (qkv,) * 6, name="ret_fwd", grid=(nc,),
        in_specs=[pl.BlockSpec(memory_space=pltpu.SMEM)] + _ret_in_specs(rows),
        out_specs=(o_specs[0], o_specs[1], state, state),
        out_shape=(_sds((n, RET_W), F32),) * 2 + (_sds((nc, HEADS, CHUNK, HEAD_DIM), F32),) * 2,
        scratch_shapes=[pltpu.VMEM((2, HEADS, CHUNK, HEAD_DIM), F32), pltpu.VMEM((2, HEADS, 3, CHUNK, CHUNK), F32)],
        compiler_params=_cp())


def _ret_bwd(qkv, lg, do, s_prev, cc, ride=None):
    n = qkv.shape[0]
    nc = n // CHUNK
    rows = _ret_rows(cc, nc, lambda s: nc - 1 - s)

    def body(lg_ref, q0, k0, v0, q1, k1, v1, do0, do1, sp0, sp1, dq0, dk0, dv0, dq1, dk1, dv1, dlg_ref, dst, dec, relf):
        @pl.when(pl.program_id(0) == 0)
        def _():
            dst[...] = jnp.zeros_like(dst)
            dlg_ref[...] = jnp.zeros_like(dlg_ref)
            _store_decay(lg_ref, dec, relf)

        refs = ((q0, k0, v0, do0, sp0, dq0, dk0, dv0), (q1, k1, v1, do1, sp1, dq1, dk1, dv1))
        chains = [(dirn, h, slice(h * HEAD_DIM, (h + 1) * HEAD_DIM)) for dirn in (0, 1) for h in range(HEADS)]

        def tiles(dirn, sl):
            q_ref, k_ref, v_ref, do_ref = refs[dirn][:4]
            return q_ref[:, sl], k_ref[:, sl], v_ref[:, sl], do_ref[:, sl]

        a_s, g1_s, da_s, h1_s = [], [], [], []
        for dirn, h, sl in chains:
            q, k, v, dov = tiles(dirn, sl)
            a_s.append(_dot_nt(q, k))
            g1_s.append(_dot_nt(dov, refs[dirn][4][h]))
            da_s.append(_dot_nt(dov, v))
            h1_s.append(_dot_nt(v, dst[dirn, h]))
        da_s = [da * dec[dirn, h, 0] for da, (dirn, h, _) in zip(da_s, chains)]
        dq_s, dk_s, dv_s, ds_s = [], [], [], []
        for (dirn, h, sl), a, da in zip(chains, a_s, da_s):
            q, k, v, dov = tiles(dirn, sl)
            dq_s.append(_dot(da, k))
            dk_s.append(_dot_tn(da, q))
            dv_s.append(_dot_tn(a * dec[dirn, h, 0], dov) + _dot(k * dec[dirn, h, 2], dst[dirn, h]))
            ds_s.append(_dot_tn(q * dec[dirn, h, 1], dov))
        for (dirn, h, sl), a, g1, da, h1, dq2, dk2, dv, ds2 in zip(chains, a_s, g1_s, da_s, h1_s, dq_s, dk_s, dv_s, ds_s):
            q, k, _, _ = tiles(dirn, sl)
            dq_ref, dk_ref, dv_ref = refs[dirn][5:]
            sp, dsn = refs[dirn][4][h], dst[dirn, h]
            gc = jnp.exp(lg_ref[dirn, h] * CHUNK)
            dq_ref[:, sl] = (g1 * dec[dirn, h, 1] + dq2).astype(BF16)
            dk_ref[:, sl] = (dk2 + h1 * dec[dirn, h, 2]).astype(BF16)
            dv_ref[:, sl] = dv.astype(BF16)
            term = da * a * relf[dirn] + q * g1 * dec[dirn, h, 3] + k * h1 * dec[dirn, h, 4] + sp * dsn * (CHUNK * gc)
            dlg_ref[dirn * HEADS + h:dirn * HEADS + h + 1, 0:HEAD_DIM] += jnp.sum(term, axis=0, keepdims=True)
            dst[dirn, h] = gc * dsn + ds2

    wide = [pl.BlockSpec((CHUNK, RET_W), lambda s, row=row: (row(s), 0)) for row in rows]
    state = pl.BlockSpec((None, HEADS, CHUNK, HEAD_DIM), lambda s: (nc - 1 - s, 0, 0, 0))
    return _pcall_ride(
        body, ride, (lg,) + (qkv,) * 6 + (do, do, s_prev[0], s_prev[1]), name="ret_bwd", grid=(nc,),
        in_specs=[pl.BlockSpec(memory_space=pltpu.SMEM)] + _ret_in_specs(rows) + wide + [state, state],
        out_specs=(wide[0],) * 3 + (wide[1],) * 3 + (pl.BlockSpec((2 * HEADS, 8 * HEAD_DIM), lambda s: (0, 0)),),
        out_shape=(_sds((n, RET_W), BF16),) * 6 + (_sds((2 * HEADS, 8 * HEAD_DIM), F32),),
        scratch_shapes=[pltpu.VMEM((2, HEADS, CHUNK, HEAD_DIM), F32), pltpu.VMEM((2, HEADS, 5, CHUNK, CHUNK), F32),
                        pltpu.VMEM((2, CHUNK, CHUNK), F32)],
        compiler_params=_cp())


def _shift_rows(cur, prev8, next8, k, seg_start, seg_end):
    tm = cur.shape[0]
    rows = _rows_iota(cur.shape)
    if k < 0:
        out = pltpu.roll(cur, -k, 0)
        for j in range(-k):
            halo = jnp.where(seg_start, 0.0, prev8[SUB + k + j:SUB + k + j + 1, :])
            out = jnp.where(rows == j, halo, out)
    else:
        out = pltpu.roll(cur, tm - k, 0)
        for j in range(k):
            halo = jnp.where(seg_end, 0.0, next8[j:j + 1, :])
            out = jnp.where(rows == tm - k + j, halo, out)
    return out


def _seg_flags(t, cb, nb):
    return jnp.logical_or(t == 0, t == cb), jnp.logical_or(t == cb - 1, t == nb - 1)


def _halo_specs(tile_of, n_rows, col):
    per = TM // SUB
    return [pl.BlockSpec((TM, LRU_W), lambda s: (tile_of(s), col)),
            pl.BlockSpec((SUB, LRU_W), lambda s: (jnp.maximum(tile_of(s) * per - 1, 0), col)),
            pl.BlockSpec((SUB, LRU_W), lambda s: (jnp.minimum((tile_of(s) + 1) * per, n_rows // SUB - 1), col))]


def _lru_gates(xr, prev8, next8, seg_start, seg_end, cw_ref, cb_ref, wg_ref, bg_ref, sp_ref):
    xm1 = _shift_rows(xr, prev8, next8, -1, seg_start, seg_end)
    xp1 = _shift_rows(xr, prev8, next8, 1, seg_start, seg_end)
    xp2 = _shift_rows(xr, prev8, next8, 2, seg_start, seg_end)
    xc = cb_ref[...] + xm1 * cw_ref[0:1, :] + xr * cw_ref[1:2, :] + xp1 * cw_ref[2:3, :] + xp2 * cw_ref[3:4, :]
    pre = _dot(xc, wg_ref[...]) + bg_ref[...]
    r = _sigmoid(pre[:, :LRU_W])
    i = _sigmoid(pre[:, LRU_W:])
    la = (-LRU_C) * r * sp_ref[...]
    a = jnp.exp(la)
    th = jnp.tanh(la)
    sq = jnp.sqrt(-2.0 * th / (1.0 - th))
    return xc, r, i, a, sq


def _scan_tile(a, b, ascending, a_sc, b_sc, carry, out_ref):
    tm, w = a.shape
    nsub = tm // SUB
    a = a.reshape(nsub, SUB, w)
    b = b.reshape(nsub, SUB, w)
    r8 = lax.broadcasted_iota(jnp.int32, a.shape, 1)
    for k in (1, 2, 4):
        if ascending:
            m = r8 >= k
            a_s, b_s = pltpu.roll(a, k, 1), pltpu.roll(b, k, 1)
        else:
            m = r8 < SUB - k
            a_s, b_s = pltpu.roll(a, SUB - k, 1), pltpu.roll(b, SUB - k, 1)
        b = a * jnp.where(m, b_s, 0.0) + b
        a = a * jnp.where(m, a_s, 1.0)
    a_sc[...] = a.reshape(tm, w)
    b_sc[...] = b.reshape(tm, w)

    def step(j, c):
        off = pl.multiple_of((j if ascending else nsub - 1 - j) * SUB, SUB)
        hb = a_sc[pl.ds(off, SUB), :] * c + b_sc[pl.ds(off, SUB), :]
        out_ref[pl.ds(off, SUB), :] = hb
        last = hb[SUB - 1:SUB, :] if ascending else hb[0:1, :]
        return jnp.broadcast_to(last, c.shape)

    carry[...] = lax.fori_loop(0, nsub, step, carry[...], unroll=8)


def _lru_fwd(p, wg, bg, sp, cw, cbias, dirn, cb, ride=None):
    n = p.shape[0]
    nb = n // TM
    tile_of = lambda s: _tile_order(dirn, s, cb, nb)

    def body(x_ref, xp_ref, xn_ref, wg_ref, bg_ref, sp_ref, cw_ref, cb_ref, h_ref, cin_ref, carry, a_sc, b_sc):
        s = pl.program_id(0)

        @pl.when(s == 0)
        def _():
            carry[...] = jnp.zeros_like(carry)

        seg_start, seg_end = _seg_flags(tile_of(s), cb, nb)
        xc, r, i, a, sq = _lru_gates(x_ref[...], xp_ref[...], xn_ref[...], seg_start, seg_end,
                                     cw_ref, cb_ref, wg_ref, bg_ref, sp_ref)
        cin_ref[...] = carry[...]
        _scan_tile(a, sq * (i * xc), dirn == 0, a_sc, b_sc, carry, h_ref)

    full = lambda shape: pl.BlockSpec(shape, lambda s: (0,) * len(shape))
    return _pcall_ride(
        body, ride, (p, p, p, wg, bg, sp, cw, cbias), name=f"lru_fwd{dirn}", grid=(nb,),
        in_specs=_halo_specs(tile_of, n, COL_XR) + [full((LRU_W, 2 * LRU_W)), full((1, 2 * LRU_W)), full((1, LRU_W)),
                                               full((4, LRU_W)), full((1, LRU_W))],
        out_specs=(pl.BlockSpec((TM, LRU_W), lambda s: (tile_of(s), 0)),
                   pl.BlockSpec((None, SUB, LRU_W), lambda s: (tile_of(s), 0, 0))),
        out_shape=(_sds((n, LRU_W), F32), _sds((nb, SUB, LRU_W), F32)),
        scratch_shapes=[pltpu.VMEM((SUB, LRU_W), F32), pltpu.VMEM((TM, LRU_W), F32), pltpu.VMEM((TM, LRU_W), F32)],
        compiler_params=_cp())


def _lru_bwd(p, wg, bg, sp, cw, cbias, h, cin, dhs, dirn, cb, ride=None):
    n = p.shape[0]
    nb = n // TM
    tile_of = lambda s: _tile_order(dirn, nb - 1 - s, cb, nb)

    def body(x_ref, xp_ref, xn_ref, wg_ref, bg_ref, sp_ref, cw_ref, cb_ref, h_ref, cin_ref, dhs_ref,
             dxc_ref, dwd_ref, acc_ref, carry, a_sc, b_sc, mu_sc, dwg_ref):
        s = pl.program_id(0)

        @pl.when(s == 0)
        def _():
            carry[...] = jnp.zeros_like(carry)
            dwg_ref[...] = jnp.zeros_like(dwg_ref)
            acc_ref[...] = jnp.zeros_like(acc_ref)

        seg_start, seg_end = _seg_flags(tile_of(s), cb, nb)
        xc, r, i, a, sq = _lru_gates(x_ref[...], xp_ref[...], xn_ref[...], seg_start, seg_end,
                                     cw_ref, cb_ref, wg_ref, bg_ref, sp_ref)
        rows = _rows_iota(a.shape)
        hv = h_ref[...]
        dh = dhs_ref[...]
        mu_next = carry[0:1, :]
        _scan_tile(a, a * dh, dirn == 1, a_sc, b_sc, carry, mu_sc)
        mu = mu_sc[...]
        if dirn == 0:
            hprev = jnp.where(rows == 0, cin_ref[0:1, :], pltpu.roll(hv, 1, 0))
            lam = dh + jnp.where(rows == TM - 1, mu_next, pltpu.roll(mu, TM - 1, 0))
        else:
            hprev = jnp.where(rows == TM - 1, cin_ref[0:1, :], pltpu.roll(hv, TM - 1, 0))
            lam = dh + jnp.where(rows == 0, mu_next, pltpu.roll(mu, 1, 0))
        ds = lam * (i * xc)
        di = lam * (sq * xc)
        dla = lam * hprev * a - ds * (a * a) / jnp.maximum(sq, 1e-20)
        dpr = dla * ((-LRU_C) * sp_ref[...]) * r * (1.0 - r)
        dpi = di * i * (1.0 - i)
        dpre = jnp.concatenate([dpr, dpi], axis=1)
        dxc_ref[...] = lam * (sq * i) + _dot_nt(dpre, wg_ref[...])
        dwg_ref[...] += _dot_tn(xc, dpre)
        acc_ref[0:1, :] += jnp.sum(dpre, axis=0, keepdims=True)
        acc_ref[1:2, 0:LRU_W] += jnp.sum(dla * ((-LRU_C) * r), axis=0, keepdims=True)

        @pl.when(s == nb - 1)
        def _():
            low = lax.broadcasted_iota(jnp.int32, (LRU_BD, 2 * LRU_BD), 1) < LRU_BD
            for half in (0, LRU_W):
                for m in range(LRU_BLOCKS // 2):
                    lanes = slice(half + 2 * LRU_BD * m, half + 2 * LRU_BD * (m + 1))
                    even = dwg_ref[2 * m * LRU_BD:(2 * m + 1) * LRU_BD, lanes]
                    odd = dwg_ref[(2 * m + 1) * LRU_BD:(2 * m + 2) * LRU_BD, lanes]
                    dwd_ref[:, lanes] = jnp.where(low, even, odd)

    full = lambda shape: pl.BlockSpec(shape, lambda s: (0,) * len(shape))
    tile = pl.BlockSpec((TM, LRU_W), lambda s: (tile_of(s), 0))
    return _pcall_ride(
        body, ride, (p, p, p, wg, bg, sp, cw, cbias, h, cin, dhs), name=f"lru_bwd{dirn}", grid=(nb,),
        in_specs=_halo_specs(tile_of, n, COL_XR) + [full((LRU_W, 2 * LRU_W)), full((1, 2 * LRU_W)), full((1, LRU_W)),
                                               full((4, LRU_W)), full((1, LRU_W)), tile,
                                               pl.BlockSpec((None, SUB, LRU_W), lambda s: (tile_of(s), 0, 0)), tile],
        out_specs=(tile, full((LRU_BD, 2 * LRU_W)), full((8, 2 * LRU_W))),
        out_shape=(_sds((n, LRU_W), F32), _sds((LRU_BD, 2 * LRU_W), F32), _sds((8, 2 * LRU_W), F32)),
        scratch_shapes=[pltpu.VMEM((SUB, LRU_W), F32)] + [pltpu.VMEM((TM, LRU_W), F32)] * 3
        + [pltpu.VMEM((LRU_W, 2 * LRU_W), F32)],
        compiler_params=_cp())


def _assemble_dp(dqs, dks, dvs, dg, dgate, dxcs, p, cw, cosf, sins, cb, ride=None):
    n = p.shape[0]
    nb = n // TM
    tile_of = lambda s: s

    def body(dqf, dqb, dkf, dkb, dvf, dvb, dg_ref, dgate_ref, cf, pf, nf, cb_, pb, nb_, x_ref, xp_ref, xn_ref,
             cw_ref, cos_ref, sin_ref, dp_ref, acc_ref):
        s = pl.program_id(0)

        @pl.when(s == 0)
        def _():
            acc_ref[...] = jnp.zeros_like(acc_ref)

        seg_start, seg_end = _seg_flags(s, cb, nb)
        dq = dqf[...].astype(F32) + dqb[...].astype(F32)
        dk = dkf[...].astype(F32) + dkb[...].astype(F32)
        cosv, sinv = cos_ref[...], sin_ref[...]
        for h in range(HEADS):
            sl = slice(h * HEAD_DIM, (h + 1) * HEAD_DIM)
            sk = slice(RET_W + h * HEAD_DIM, RET_W + (h + 1) * HEAD_DIM)
            dp_ref[:, sl] = (dq[:, sl] * cosv + pltpu.roll(dq[:, sl] * sinv, HEAD_DIM // 2, 1)).astype(BF16)
            dp_ref[:, sk] = ((dk[:, sl] * cosv + pltpu.roll(dk[:, sl] * sinv, HEAD_DIM // 2, 1)) * K_SCALE).astype(BF16)
        dp_ref[:, 2 * RET_W:3 * RET_W] = (dvf[...].astype(F32) + dvb[...].astype(F32)).astype(BF16)
        dp_ref[:, 3 * RET_W:4 * RET_W] = dg_ref[...].astype(BF16)
        dxc = cf[...] + cb_[...]
        dprev = pf[...] + pb[...]
        dnext = nf[...] + nb_[...]
        dxr = (_shift_rows(dxc, dprev, dnext, 1, seg_start, seg_end) * cw_ref[0:1, :] + dxc * cw_ref[1:2, :]
               + _shift_rows(dxc, dprev, dnext, -1, seg_start, seg_end) * cw_ref[2:3, :]
               + _shift_rows(dxc, dprev, dnext, -2, seg_start, seg_end) * cw_ref[3:4, :])
        dp_ref[:, 4 * RET_W:4 * RET_W + LRU_W] = dxr.astype(BF16)
        dp_ref[:, 4 * RET_W + LRU_W:] = dgate_ref[...].astype(BF16)
        xr, xp, xn = x_ref[...], xp_ref[...], xn_ref[...]
        for j, k in enumerate((-1, 0, 1, 2)):
            xs = xr if k == 0 else _shift_rows(xr, xp, xn, k, seg_start, seg_end)
            acc_ref[j:j + 1, 0:LRU_W] += jnp.sum(dxc * xs, axis=0, keepdims=True)
        acc_ref[4:5, 0:LRU_W] += jnp.sum(dxc, axis=0, keepdims=True)

    t = pl.BlockSpec((TM, RET_W), lambda s: (s, 0))
    args = (dqs[0], dqs[1], dks[0], dks[1], dvs[0], dvs[1], dg, dgate, dxcs[0], dxcs[0], dxcs[0], dxcs[1], dxcs[1], dxcs[1],
            p, p, p, cw, cosf, sins)
    tab = pl.BlockSpec((TM, HEAD_DIM), lambda s: (s, 0))
    return _pcall_ride(
        body, ride, args, name="assemble_dp", grid=(nb,),
        in_specs=[t] * 8 + _halo_specs(tile_of, n, 0) * 2 + _halo_specs(tile_of, n, COL_XR)
        + [pl.BlockSpec((4, LRU_W), lambda s: (0, 0)), tab, tab],
        out_specs=(pl.BlockSpec((TM, 4 * RET_W + 2 * LRU_W), lambda s: (s, 0)), pl.BlockSpec((8, 2 * LRU_W), lambda s: (0, 0))),
        out_shape=(_sds((n, 4 * RET_W + 2 * LRU_W), BF16), _sds((8, 2 * LRU_W), F32)), compiler_params=_cp())


def _adamw(g, w, m, v):
    nm = ADAM_B1 * m + (1.0 - ADAM_B1) * g
    nv = ADAM_B2 * v + (1.0 - ADAM_B2) * (g * g)
    m_hat = nm / (1.0 - ADAM_B1 ** ADAM_STEP)
    v_hat = nv / (1.0 - ADAM_B2 ** ADAM_STEP)
    return (-ADAM_LR) * (m_hat / (jnp.sqrt(v_hat) + ADAM_EPS) + ADAM_WD * w), nm, nv


def _adam_many(items, name):
    n = len(items)

    def body(*refs):
        for i in range(n):
            g, w, m, v = (r[...] for r in refs[4 * i:4 * i + 4])
            for o_ref, val in zip(refs[4 * n + 3 * i:4 * n + 3 * i + 3], _adamw(g, w, m, v)):
                o_ref[...] = val

    out_shape = tuple(_sds(it[1].shape, F32) for it in items for _ in range(3))
    res = _pcall(body, name=name, out_shape=out_shape, compiler_params=_cp())(*[a for it in items for a in it])
    return [tuple(res[3 * i:3 * i + 3]) for i in range(n)]


def _sum_adam(parts_list, w, m, v, name):
    nparts, _, c = parts_list[0].shape
    r = w.shape[0]
    tr = min(parts_list[0].shape[1], 128)
    starts, o = [], 0
    for pa in parts_list:
        starts.append(o)
        o += pa.shape[1] // tr
    nseg = len(parts_list)

    def body(*refs):
        p_refs = refs[:nseg]
        w_ref, m_ref, v_ref, g_ref, d_ref, nm_ref, nv_ref = refs[nseg:]
        i = pl.program_id(0)
        for s, p_ref in enumerate(p_refs):
            end = starts[s + 1] if s + 1 < nseg else r // tr

            @pl.when(jnp.logical_and(i >= starts[s], i < end))
            def _():
                g = p_ref[0].astype(F32)
                for j in range(1, nparts):
                    g = g + p_ref[j].astype(F32)
                g_ref[...] = g
                d_ref[...], nm_ref[...], nv_ref[...] = _adamw(g, w_ref[...], m_ref[...], v_ref[...])

    def seg_spec(s):
        last = parts_list[s].shape[1] // tr - 1
        return pl.BlockSpec((nparts, tr, c), lambda i: (0, jnp.clip(i - starts[s], 0, last), 0))

    t = pl.BlockSpec((tr, c), lambda i: (i, 0))
    return _pcall(
        body, name=name, grid=(r // tr,),
        in_specs=[seg_spec(s) for s in range(nseg)] + [t, t, t],
        out_specs=(t, t, t, t), out_shape=(_sds((r, c), F32),) * 4, compiler_params=_cp(),
    )(*parts_list, w, m, v)


def _sum_parts(parts, name):
    nparts, r, c = parts.shape

    def body(p_ref, o_ref):
        g = p_ref[0]
        for j in range(1, nparts):
            g = g + p_ref[j]
        o_ref[...] = g

    return _pcall(body, name=name, out_shape=_sds((r, c), parts.dtype), compiler_params=_cp())(parts)


def _rot_tables(l_len, t_len):
    rows = t_len // GRID_W
    n_freq = HEAD_DIM // 4
    inv = ROPE_BASE ** (-jnp.arange(n_freq, dtype=F32) / n_freq)
    ang_r = jnp.arange(rows, dtype=F32)[:, None] * inv
    ang_c = jnp.arange(GRID_W, dtype=F32)[:, None] * inv
    cos = jnp.concatenate([jnp.repeat(jnp.cos(ang_r), GRID_W, axis=0), jnp.tile(jnp.cos(ang_c), (rows, 1))], axis=-1)
    sin = jnp.concatenate([jnp.repeat(jnp.sin(ang_r), GRID_W, axis=0), jnp.tile(jnp.sin(ang_c), (rows, 1))], axis=-1)
    cosf = jnp.concatenate([jnp.ones((l_len, HEAD_DIM), F32), jnp.concatenate([cos, cos], axis=-1)], axis=0)
    sins = jnp.concatenate([jnp.zeros((l_len, HEAD_DIM), F32), jnp.concatenate([-sin, sin], axis=-1)], axis=0)
    return cosf, sins


def _block_diag(w):
    eye = jnp.eye(LRU_BLOCKS, dtype=w.dtype)
    return (w[:, :, None, :] * eye[:, None, :, None]).reshape(LRU_W, LRU_W)


def _blocks_from_lanes(dwd_half):
    return dwd_half.reshape(LRU_BD, LRU_BLOCKS, LRU_BD).transpose(1, 0, 2)


def _silu(x):
    return x * jax.nn.sigmoid(x)


def kernel(x, c, ctx, c_ctx, w_ada, b_ada, norm1_g, norm2_g, w_in, ret_decay, conv_w, conv_b, lru_wa, lru_ba, lru_wx, lru_bx, lru_lambda, w_out, w_mlp1, w_mlp2, final_g, loss_target, m_c_ctx, m_w_ada, m_b_ada, m_norm1_g, m_norm2_g, m_w_in, m_ret_decay, m_conv_w, m_conv_b, m_lru_wa, m_lru_ba, m_lru_wx, m_lru_bx, m_lru_lambda, m_w_out, m_w_mlp1, m_w_mlp2, m_final_g, v_c_ctx, v_w_ada, v_b_ada, v_norm1_g, v_norm2_g, v_w_in, v_ret_decay, v_conv_w, v_conv_b, v_lru_wa, v_lru_ba, v_lru_wx, v_lru_bx, v_lru_lambda, v_w_out, v_w_mlp1, v_w_mlp2, v_final_g):
    t_len, d = x.shape[1], x.shape[2]
    l_len = ctx.shape[1]
    cb, cc = l_len // TM, l_len // CHUNK
    me = 4 * lax.axis_index("x") + 2 * lax.axis_index("y") + lax.axis_index("c")
    x2d, ctx2d, tgt2d = x[0], ctx[0], loss_target[0]
    ada_cols = w_ada.shape[2]
    wa2d = w_ada[0]

    sc_loc = conv_w.shape[2]
    pack_a = jnp.zeros((8, d), F32)
    pack_a = pack_a.at[0].set(_silu(c[0]))
    pack_a = pack_a.at[1, :4 * sc_loc].set(conv_w[0].reshape(-1))
    pack_a = pack_a.at[2, :2 * sc_loc].set(lru_ba[0].reshape(-1))
    pack_a = pack_a.at[3, :2 * sc_loc].set(lru_bx[0].reshape(-1))
    pack_a = pack_a.at[4, :2 * sc_loc].set(lru_lambda[0].reshape(-1))
    b_cols = lax.dynamic_slice(b_ada, (0, me * ada_cols), (1, ada_cols))
    all_a, s16, mod_parts = _mod_exchange(pack_a, _silu(c_ctx)[None, :], wa2d, b_cols)

    def unshard(row, k):
        return all_a[:, row, :k * sc_loc].reshape(N_DEV, k, sc_loc).transpose(1, 0, 2).reshape(k, N_DEV * sc_loc)

    conv_w_full = unshard(1, 4)
    ba_full, bx_full, lam_full = unshard(2, 2), unshard(3, 2), unshard(4, 2)

    mod_all = mod_parts.transpose(1, 0, 2).reshape(16, N_DEV * ada_cols)
    mod_me = lax.dynamic_slice(mod_all, (me, 0), (1, 6 * d)).reshape(6, d)
    mod_c = mod_all[8].reshape(6, d)
    modrows = jnp.concatenate([mod_c[0:2], mod_me], axis=0)

    lg = jax.nn.log_sigmoid(ret_decay[0])
    sp = jax.nn.softplus(-lam_full)
    wg = [jnp.concatenate([_block_diag(lru_wa[0, dd]), _block_diag(lru_wx[0, dd])], axis=1).astype(BF16) for dd in (0, 1)]
    bg = [jnp.concatenate([ba_full[dd], bx_full[dd]])[None, :] for dd in (0, 1)]
    cosf, sins = _rot_tables(l_len, t_len)

    (hn, hnt), win_g = _norm1_fwd(ctx2d, x2d, norm1_g, modrows, cb, ride=("gather", w_in[0].astype(BF16)))
    (qkv, p), w1_g = _mm_in(hn, win_g, cosf, sins, ride=("gather", w_mlp1[0].astype(BF16)))
    (o0, o1, sp0, sp1), w2_g = _ret_fwd(qkv, lg, cc, ride=("gather", w_mlp2[0].astype(BF16)))
    w2_g = w2_g.reshape(1, 4 * d, d)
    o, s_prev = [o0, o1], [sp0, sp1]
    (h0, cin0), wout_g = _lru_fwd(p, wg[0], bg[0], sp[0:1], conv_w_full, conv_b, 0, cb, ride=("gather", w_out[0].astype(BF16)))
    wout_g = wout_g.reshape(1, d, d)
    (h1, cin1), _ = _lru_fwd(p, wg[1], bg[1], sp[1:2], conv_w_full, conv_b, 1, cb)
    h, cin = [h0, h1], [cin0, cin1]
    mix = _mix_fwd(o[0], o[1], p, h[0], h[1], cb, t_len)
    y, x1, h2, h2t = _mm_nn(mix, wout_g, F32, "mm_out_norm2", tm=TL, epi=_res_norm2_epilogue(x2d, norm2_g, modrows, TL))
    r = _mm_nn(h2, w1_g, BF16, "mm_mlp1", relu_out=True, tm=TL)
    dx2, dz, dzt, facc = _mm_nn(r, w2_g, F32, "mm_mlp2_final", square_lhs=True, tm=TL, vmem_mb=58,
                                epi=_final_epilogue(x1, tgt2d, final_g[None, :], modrows, TL))

    du = _mm_nt(dz, w2_g, BF16, "mm_da2", relu_mul=r, tm=TL, vmem_mb=58)
    gw2_lo, gw2_hi = _mm_wgrad(dzt, r, "mm_dw2", w2_g.shape[1] // N_DEV, BF16, transpose_out=True, square_rhs=True,
                               halves=True)
    gw1_lo, gw1_hi = _mm_wgrad(h2t, du, "mm_dw1", w1_g.shape[2], BF16, halves=True)
    (dx1, dy, n2acc), gw2_lo_all = _mm_nt(du, w1_g, F32, "mm_dh2_norm2", ride=("a2a", gw2_lo), tm=TL, vmem_mb=60,
                                          epi=_bwd_norm2_epilogue(x1, dx2, y, norm2_g, modrows, TL))
    gwo = _mm_tn(mix, dy, "mm_dwout", False, d, BF16, 512).reshape(N_DEV, -1, d)
    (do, dg, dgate, dhs), gwo_all = _mm_nt(dy, wout_g, F32, "mm_dmix_mix", ride=("a2a", gwo),
                                           epi=_mix_bwd_epilogue(o[0], o[1], p, h[0], h[1], cb))
    dxcs, dwgs, laccs, rides, got = [], [], [], [("a2a", gw1_hi), ("a2a", gw2_hi)], []
    (dq0, dk0, dv0, dq1, dk1, dv1, dlg_lanes), gw1_lo_all = _ret_bwd(qkv, lg, do, s_prev, cc, ride=("a2a", gw1_lo))
    dqs, dks, dvs = [dq0, dq1], [dk0, dk1], [dv0, dv1]
    for dd in (0, 1):
        (dxc_, dwg_, lacc_), got_ = _lru_bwd(p, wg[dd], bg[dd], sp[dd:dd + 1], conv_w_full, conv_b, h[dd], cin[dd], dhs, dd, cb,
                                             ride=rides[dd])
        dxcs.append(dxc_); dwgs.append(dwg_); laccs.append(lacc_); got.append(got_)
    gw1_hi_all, gw2_hi_all = got
    gw1_all, gw2_all = [gw1_lo_all, gw1_hi_all], [gw2_lo_all, gw2_hi_all]
    (dp, cacc), _ = _assemble_dp(dqs, dks, dvs, dg, dgate, dxcs, p, conv_w_full, cosf, sins, cb)
    pack_b = jnp.concatenate([n2acc, facc, cacc, laccs[0], laccs[1], dlg_lanes, dwgs[0], dwgs[1]], axis=0)
    gwi, all_b = _mm_wgrad(hnt, dp, "mm_dwin", win_g.shape[2], BF16, ride=("gather", pack_b))
    (grad_x, n1acc), gwi_all = _mm_nt(dp, win_g, F32, "mm_dhn_norm1", ride=("a2a", gwi),
                                      epi=_bwd_norm1_epilogue(ctx2d, x2d, dx1, norm1_g, modrows, cb))
    all_n1 = _all_gather_small(n1acc, "gather_norm1_grads")
    tot = _sum_parts(all_b, "sum_small_grads")
    t_n1 = _sum_parts(all_n1, "sum_norm1_grads")
    t_n2, t_f, t_conv, t_dlg = tot[0:8], tot[8:16], tot[16:24, :LRU_W], tot[40:48, :HEAD_DIM]
    t_l = [tot[24:32], tot[32:40]]
    t_dwd = [tot[48:48 + LRU_BD], tot[48 + LRU_BD:48 + 2 * LRU_BD]]
    loss = (0.5 / d) * jnp.sum(t_f[2])
    t_wa = jnp.stack([_blocks_from_lanes(t_dwd[dd][:, :LRU_W]) for dd in (0, 1)])
    t_wx = jnp.stack([_blocks_from_lanes(t_dwd[dd][:, LRU_W:]) for dd in (0, 1)])
    t_ba = jnp.stack([t_l[dd][0, :LRU_W] for dd in (0, 1)])
    t_bx = jnp.stack([t_l[dd][0, LRU_W:] for dd in (0, 1)])
    t_sp = jnp.stack([t_l[dd][1, :LRU_W] for dd in (0, 1)])
    dm_rows = jnp.stack([all_n1[:, 2, :], all_n1[:, 3, :], all_b[:, 3, :], all_b[:, 0, :], all_b[:, 1, :], all_b[:, 9, :]],
                        axis=1).reshape(N_DEV, 6 * d)
    dm_c = jnp.concatenate([t_n1[0], t_n1[1], jnp.zeros((4 * d,), F32)])
    dm16 = jnp.zeros((16, 6 * d), F32).at[0:8].set(dm_rows).at[8].set(dm_c)
    g_b_ada = jnp.sum(dm16, axis=0)[None, :]
    dm_cols = lax.dynamic_slice(dm16, (0, me * ada_cols), (16, ada_cols))
    g_w_ada, ds16 = _ada_bwd(s16, dm_cols, wa2d)
    ds_all = _all_gather_small(ds16[8:16], "gather_dsilu")
    dsilu_cc = _sum_parts(ds_all, "sum_dsilu")[0]
    sg_cc = jax.nn.sigmoid(c_ctx)
    g_c_ctx = dsilu_cc * (sg_cc * (1.0 + c_ctx * (1.0 - sg_cc)))

    g_ret_decay = jnp.sum(t_dlg, axis=-1).reshape(2, HEADS) * jax.nn.sigmoid(-ret_decay[0])
    g_lambda_full = -t_sp * jax.nn.sigmoid(-lam_full)

    def my_cols(full):
        return lax.dynamic_slice(full, (0, me * sc_loc), (full.shape[0], sc_loc))

    small_g = dict(
        c_ctx=g_c_ctx[None], b_ada=g_b_ada, norm1_g=t_n1[4:5], norm2_g=t_n2[2:3], ret_decay=g_ret_decay,
        conv_w=my_cols(t_conv[0:4]), conv_b=t_conv[4:5], lru_wa=t_wa.reshape(-1, LRU_BD), lru_ba=my_cols(t_ba),
        lru_wx=t_wx.reshape(-1, LRU_BD), lru_bx=my_cols(t_bx), lru_lambda=my_cols(g_lambda_full), final_g=t_f[0:1])
    small = dict(
        c_ctx=(c_ctx, m_c_ctx, v_c_ctx), b_ada=(b_ada, m_b_ada, v_b_ada), norm1_g=(norm1_g, m_norm1_g, v_norm1_g),
        norm2_g=(norm2_g, m_norm2_g, v_norm2_g), ret_decay=(ret_decay, m_ret_decay, v_ret_decay),
        conv_w=(conv_w, m_conv_w, v_conv_w), conv_b=(conv_b, m_conv_b, v_conv_b), lru_wa=(lru_wa, m_lru_wa, v_lru_wa),
        lru_ba=(lru_ba, m_lru_ba, v_lru_ba), lru_wx=(lru_wx, m_lru_wx, v_lru_wx), lru_bx=(lru_bx, m_lru_bx, v_lru_bx),
        lru_lambda=(lru_lambda, m_lru_lambda, v_lru_lambda), final_g=(final_g, m_final_g, v_final_g))
    names = list(small)
    items = [(small_g[k],) + tuple(a.reshape(small_g[k].shape) for a in small[k]) for k in names]
    res = {}
    for k, it, (d_, m_, v_) in zip(names, items, _adam_many(items, "adam_small")):
        shape = small[k][0].shape
        res[k] = tuple(a.reshape(shape) for a in (it[0], d_, m_, v_))

    def big(parts, w, m, v, name):
        out = _sum_adam(parts, w[0], m[0], v[0], name)
        return tuple(a[None] for a in out)

    res["w_ada"] = big([g_w_ada[None]], w_ada, m_w_ada, v_w_ada, "adam_w_ada")
    res["w_in"] = big([gwi_all], w_in, m_w_in, v_w_in, "adam_w_in")
    res["w_out"] = big([gwo_all], w_out, m_w_out, v_w_out, "adam_w_out")
    res["w_mlp1"] = big(gw1_all, w_mlp1, m_w_mlp1, v_w_mlp1, "adam_w_mlp1")
    res["w_mlp2"] = big(gw2_all, w_mlp2, m_w_mlp2, v_w_mlp2, "adam_w_mlp2")

    order = ["c_ctx", "w_ada", "b_ada", "norm1_g", "norm2_g", "w_in", "ret_decay", "conv_w", "conv_b", "lru_wa", "lru_ba",
             "lru_wx", "lru_bx", "lru_lambda", "w_out", "w_mlp1", "w_mlp2", "final_g"]
    outs = [loss, grad_x[None]]
    for j in range(4):
        outs += [res[k][j] for k in order]
    return tuple(outs)
```

```python
import jax
import jax.numpy as jnp
from jax import lax
from jax.experimental import pallas as pl
from jax.experimental.pallas import tpu as pltpu

F32 = jnp.float32
BF16 = jnp.bfloat16
AXES = ("x", "y", "c")
N_DEV = 8
MESH = pl.DeviceIdType.MESH

HEADS = 4
HEAD_DIM = 128
CHUNK = 128
RET_W = HEADS * HEAD_DIM
LRU_W = 512
LRU_BLOCKS = 8
LRU_BD = LRU_W // LRU_BLOCKS
LRU_C = 8.0
EPS = 1e-6
K_SCALE = HEAD_DIM ** -0.5
ROPE_BASE = 10000.0
GRID_W = 64
TM = 256
TL = 512
SUB = 8

ADAM_LR = 0.001
ADAM_B1 = 0.9
ADAM_B2 = 0.999
ADAM_EPS = 1e-08
ADAM_WD = 0.01
ADAM_STEP = 10

COL_G, COL_XR, COL_GATE = 0, 1, 2

R_CSH1, R_CSC1, R_SH1, R_SC1, R_G1, R_SH2, R_SC2, R_G2 = range(8)


def _pcall(body, **kw):
    return pl.pallas_call(body, **kw)


def _cp(vmem_mb=48):
    return pltpu.CompilerParams(vmem_limit_bytes=vmem_mb << 20)


def _sds(shape, dtype):
    return jax.ShapeDtypeStruct(shape, dtype)


def _dot(a, b):
    return jnp.dot(a.astype(BF16), b.astype(BF16), preferred_element_type=F32)


def _dot_nt(a, b):
    return lax.dot_general(a.astype(BF16), b.astype(BF16), (((1,), (1,)), ((), ())), preferred_element_type=F32)


def _dot_tn(a, b):
    return lax.dot_general(a.astype(BF16), b.astype(BF16), (((0,), (0,)), ((), ())), preferred_element_type=F32)


def _sigmoid(x):
    return 0.5 * jnp.tanh(0.5 * x) + 0.5


def _gelu(x):
    return 0.5 * x * (1.0 + jnp.tanh(0.7978845608028654 * (x + 0.044715 * x * x * x)))


def _dgelu(x):
    t = jnp.tanh(0.7978845608028654 * (x + 0.044715 * x * x * x))
    return 0.5 * (1.0 + t) + 0.5 * x * (1.0 - t * t) * 0.7978845608028654 * (1.0 + 3.0 * 0.044715 * x * x)


def _rows_iota(shape):
    return lax.broadcasted_iota(jnp.int32, shape, 0)


def _tile_order(dirn, s, cb, nb):
    if dirn == 0:
        return s
    return jnp.where(s < cb, cb - 1 - s, nb - 1 - (s - cb))


_SEMS = [pltpu.SemaphoreType.DMA((7,)), pltpu.SemaphoreType.DMA((7,)), pltpu.SemaphoreType.DMA(())]
_ANY = pl.BlockSpec(memory_space=pl.ANY)


def _gather_copies(x_ref, out_ref, send_sems, recv_sems, local_sem):
    mx, my, mc = lax.axis_index("x"), lax.axis_index("y"), lax.axis_index("c")
    me, sibling = (mx, my, mc), (mx, my, 1 - mc)
    chips = [(1 - mx, my), (mx, 1 - my), (1 - mx, 1 - my)]

    def slot(px, py, pc):
        return out_ref.at[4 * px + 2 * py + pc]

    def copy(k, block, to, src=None):
        return pltpu.make_async_remote_copy(
            src_ref=slot(*block) if src is None else src, dst_ref=slot(*block),
            send_sem=send_sems.at[k], recv_sem=recv_sems.at[k], device_id=to, device_id_type=MESH)

    mine = pltpu.make_async_copy(x_ref, slot(*me), local_sem)
    first = [copy(0, me, sibling, src=x_ref)] + [copy(1 + j, me, (*chip, mc), src=x_ref) for j, chip in enumerate(chips)]
    passed = [copy(4 + j, (*chip, mc), sibling) for j, chip in enumerate(chips)]
    recv_ici = [copy(1 + j, (*chip, mc), me) for j, chip in enumerate(chips)]
    recv_d2d = [copy(0, sibling, me)] + [copy(4 + j, (*chip, 1 - mc), me) for j, chip in enumerate(chips)]
    return mine, first, passed, recv_ici, recv_d2d


def _gather_start(*refs):
    mine, first, _, _, _ = _gather_copies(*refs)
    mine.start()
    for cp in first:
        cp.start()


def _gather_pass_on(*refs):
    _, _, passed, recv_ici, _ = _gather_copies(*refs)
    for landed, onward in zip(recv_ici, passed):
        landed.wait_recv()
        onward.start()


def _gather_finish(*refs):
    mine, first, passed, _, recv_d2d = _gather_copies(*refs)
    for landed in recv_d2d:
        landed.wait_recv()
    for cp in first + passed:
        cp.wait_send()
    mine.wait()


def _a2a_copies(g_ref, out_ref, send_sems, recv_sems, local_sem):
    mx, my, mc = lax.axis_index("x"), lax.axis_index("y"), lax.axis_index("c")
    me = 4 * mx + 2 * my + mc
    mine = pltpu.make_async_copy(g_ref.at[me], out_ref.at[me], local_sem)
    copies = []
    for k in range(1, N_DEV):
        px = 1 - mx if (k >> 2) & 1 else mx
        py = 1 - my if (k >> 1) & 1 else my
        pc = 1 - mc if k & 1 else mc
        copies.append(pltpu.make_async_remote_copy(
            src_ref=g_ref.at[4 * px + 2 * py + pc], dst_ref=out_ref.at[me],
            send_sem=send_sems.at[k - 1], recv_sem=recv_sems.at[k - 1],
            device_id=(px, py, pc), device_id_type=MESH))
    return mine, copies


def _a2a_start(*refs):
    mine, copies = _a2a_copies(*refs)
    mine.start()
    for cp in copies:
        cp.start()


def _a2a_finish(*refs):
    mine, copies = _a2a_copies(*refs)
    for cp in copies:
        cp.wait()
    mine.wait()


_EXCHANGES = {"gather": (_gather_start, _gather_finish), "a2a": (_a2a_start, _a2a_finish)}
PASS_ON_LEAD = 3


def _exchange_shape(kind, src):
    return _sds((N_DEV,) + src.shape if kind == "gather" else src.shape, src.dtype)


def _all_gather_small(x, name):
    def body(x_ref, out_ref, send_sems, recv_sems, local_sem):
        mx, my, mc = lax.axis_index("x"), lax.axis_index("y"), lax.axis_index("c")
        me = 4 * mx + 2 * my + mc
        mine = pltpu.make_async_copy(x_ref, out_ref.at[me], local_sem)
        mine.start()
        copies = []
        for k in range(1, N_DEV):
            peer = (1 - mx if (k >> 2) & 1 else mx, 1 - my if (k >> 1) & 1 else my, 1 - mc if k & 1 else mc)
            copies.append(pltpu.make_async_remote_copy(
                src_ref=x_ref, dst_ref=out_ref.at[me], send_sem=send_sems.at[k - 1], recv_sem=recv_sems.at[k - 1],
                device_id=peer, device_id_type=MESH))
            copies[-1].start()
        for cp in copies:
            cp.wait()
        mine.wait()

    return _pcall(body, name=name, out_shape=_exchange_shape("gather", x), in_specs=[_ANY], out_specs=_ANY,
                  scratch_shapes=list(_SEMS))(x)


def _pcall_ride(body, ride, args, *, name, grid, in_specs, out_specs, out_shape, scratch_shapes=(), compiler_params=None):
    if ride is None:
        out = _pcall(body, name=name, grid=grid, in_specs=in_specs, out_specs=out_specs, out_shape=out_shape,
                     scratch_shapes=list(scratch_shapes), compiler_params=compiler_params)(*args)
        return out, None
    kind, src = ride
    start, finish = _EXCHANGES[kind]
    single = not isinstance(out_shape, (tuple, list))
    out_specs_t = (out_specs,) if single else tuple(out_specs)
    out_shape_t = (out_shape,) if single else tuple(out_shape)
    n_in, n_out, n_sc = len(in_specs), len(out_shape_t), len(scratch_shapes)

    def wrapped(*refs):
        ins, src_ref = refs[:n_in], refs[n_in]
        outs, dst_ref = refs[n_in + 1:n_in + 1 + n_out], refs[n_in + 1 + n_out]
        scratch = refs[n_in + 2 + n_out:n_in + 2 + n_out + n_sc]
        sems = refs[n_in + 2 + n_out + n_sc:]
        first = pl.program_id(0) == 0
        last = pl.program_id(0) == grid[0] - 1
        for ax in range(1, len(grid)):
            first = jnp.logical_and(first, pl.program_id(ax) == 0)
            last = jnp.logical_and(last, pl.program_id(ax) == grid[ax] - 1)

        @pl.when(first)
        def _():
            start(src_ref, dst_ref, *sems)

        body(*ins, *outs, *scratch)

        if kind == "gather":
            @pl.when(pl.program_id(0) == max(grid[0] - PASS_ON_LEAD, 0))
            def _():
                _gather_pass_on(src_ref, dst_ref, *sems)

        @pl.when(last)
        def _():
            finish(src_ref, dst_ref, *sems)

    res = _pcall(wrapped, name=name, grid=grid, in_specs=list(in_specs) + [_ANY], out_specs=out_specs_t + (_ANY,),
                 out_shape=out_shape_t + (_exchange_shape(kind, src),),
                 scratch_shapes=list(scratch_shapes) + list(_SEMS), compiler_params=compiler_params)(*args, src)
    return (res[0] if single else tuple(res[:-1])), res[-1]


class _Epilogue:
    def __init__(self, fn, args, in_specs, out_specs, out_shape, steps=None, lhs_map=None, delay=0):
        self.fn, self.args, self.in_specs, self.out_specs, self.out_shape = fn, tuple(args), list(in_specs), out_specs, out_shape
        self.steps, self.lhs_map, self.delay = steps, lhs_map, delay


def _mm_nn(a, w, out_dtype, name, square_lhs=False, relu_out=False, ride=None, tm=TM, epi=None, vmem_mb=48):
    m, k = a.shape
    nb, _, bn = w.shape
    tm = min(tm, m)
    delay = 0 if epi is None else epi.delay
    steps = m // tm

    def body(*refs):
        a_ref, w_ref = refs[:2]
        av = a_ref[...]
        if square_lhs:
            av = av * av
        if delay:
            assert nb == 1
            held = refs[-1]

            @pl.when(pl.program_id(0) == 0)
            def _():
                held[...] = jnp.zeros_like(held)

            fresh = jnp.dot(av, w_ref[0], preferred_element_type=F32)
            epi.fn(held[...], slice(0, tm), *refs[2:-1])
            held[...] = fresh
            return
        if epi is not None:
            assert nb == 1
            epi.fn(jnp.dot(av, w_ref[0], preferred_element_type=F32), slice(0, tm), *refs[2:])
            return
        for j in range(nb):
            r = jnp.dot(av, w_ref[j], preferred_element_type=F32)
            if relu_out:
                r = jnp.maximum(r, 0.0)
            refs[2][:, j * bn:(j + 1) * bn] = r.astype(out_dtype)

    lhs_map = (lambda i: (i, 0)) if not delay else (lambda i: (jnp.minimum(i, steps - 1), 0))
    in_specs = [pl.BlockSpec((tm, k), lhs_map), pl.BlockSpec((nb, k, bn), lambda i: (0, 0, 0))]
    scratch = []
    if epi is None:
        args, out_specs, out_shape = (a, w), pl.BlockSpec((tm, nb * bn), lambda i: (i, 0)), _sds((m, nb * bn), out_dtype)
    else:
        args, out_specs, out_shape = (a, w) + epi.args, epi.out_specs, epi.out_shape
        in_specs += epi.in_specs
        if delay:
            scratch.append(pltpu.VMEM((tm, bn), F32))
    out, ex = _pcall_ride(body, ride, args, name=name, grid=(steps + delay,), in_specs=in_specs, out_specs=out_specs,
                          out_shape=out_shape, scratch_shapes=scratch, compiler_params=_cp(vmem_mb))
    return out if ride is None else (out, ex)


def _mm_nt(dy, w, out_dtype, name, relu_mul=None, ride=None, tm=TM, epi=None, vmem_mb=48):
    m = dy.shape[0]
    nb, k, bn = w.shape
    tm = min(tm, m)
    n_extra = (0 if relu_mul is None else 1) + (0 if epi is None else len(epi.args))

    delay = 0 if epi is None else epi.delay

    def body(*refs):
        dy_ref, w_ref = refs[:2]
        scratch = refs[-1 - delay:]
        extra, outs, wt = refs[2:2 + n_extra], refs[2 + n_extra:-1 - delay], scratch[0]

        @pl.when(pl.program_id(0) == 0)
        def _():
            for j in range(nb):
                wt[j * bn:(j + 1) * bn, :] = w_ref[j].T
            if delay:
                scratch[1][...] = jnp.zeros_like(scratch[1])

        if delay:
            held = scratch[1]
            fresh = jnp.dot(dy_ref[...], wt[...], preferred_element_type=F32)
            epi.fn(held[...], slice(0, tm), *extra, *outs)
            held[...] = fresh
            return
        acc = jnp.dot(dy_ref[...], wt[...], preferred_element_type=F32)
        if epi is not None:
            epi.fn(acc, slice(0, tm), *extra, *outs)
            return
        if relu_mul is not None:
            acc = acc * (2.0 * extra[0][...].astype(F32))
        outs[0][...] = acc.astype(out_dtype)

    steps = m // tm if epi is None or epi.steps is None else epi.steps
    lhs_tile = (lambda i: (i, 0)) if epi is None or epi.lhs_map is None else epi.lhs_map
    lhs_map = lhs_tile if not delay else (lambda i: lhs_tile(jnp.minimum(i, steps - 1)))
    in_specs = [pl.BlockSpec((tm, nb * bn), lhs_map), pl.BlockSpec((nb, k, bn), lambda i: (0, 0, 0))]
    args = [dy, w]
    if relu_mul is not None:
        in_specs.append(pl.BlockSpec((tm, k), lambda i: (i, 0)))
        args.append(relu_mul)
    scratch = [pltpu.VMEM((nb * bn, k), BF16)]
    if epi is None:
        out_specs, out_shape = pl.BlockSpec((tm, k), lambda i: (i, 0)), _sds((m, k), out_dtype)
    else:
        args += list(epi.args)
        in_specs += epi.in_specs
        out_specs, out_shape = epi.out_specs, epi.out_shape
        if delay:
            scratch.append(pltpu.VMEM((tm, k), F32))
    out, ex = _pcall_ride(
        body, ride, args, name=name, grid=(steps + delay,), in_specs=in_specs, out_specs=out_specs, out_shape=out_shape,
        scratch_shapes=scratch, compiler_params=_cp(vmem_mb))
    return out if ride is None else (out, ex)


def _mm_tn(a, b, name, col_blocks, block, out_dtype, tm, square_lhs=False):
    m, k = a.shape
    nn = b.shape[1]
    steps = m // tm
    if col_blocks:
        nblk, acc_shape = nn // block, (k, block)
        a_spec = pl.BlockSpec((tm, k), lambda j, s: (s, 0))
        b_spec = pl.BlockSpec((tm, block), lambda j, s: (s, j))
    else:
        nblk, acc_shape = k // block, (block, nn)
        a_spec = pl.BlockSpec((tm, block), lambda j, s: (s, j))
        b_spec = pl.BlockSpec((tm, nn), lambda j, s: (s, 0))

    def body(a_ref, b_ref, o_ref, acc):
        s = pl.program_id(1)

        @pl.when(s == 0)
        def _():
            acc[...] = jnp.zeros_like(acc)

        av = a_ref[...]
        if square_lhs:
            av = av.astype(F32)
            av = (av * av).astype(BF16)
        acc[...] += _dot_tn(av, b_ref[...])

        @pl.when(s == steps - 1)
        def _():
            o_ref[...] = acc[...].astype(out_dtype)

    return _pcall(
        body, name=name, grid=(nblk, steps), in_specs=[a_spec, b_spec],
        out_specs=pl.BlockSpec((None,) + acc_shape, lambda j, s: (j, 0, 0)),
        out_shape=_sds((nblk,) + acc_shape, out_dtype),
        scratch_shapes=[pltpu.VMEM(acc_shape, F32)], compiler_params=_cp(),
    )(a, b)


def _mm_wgrad(at, b, name, bn, out_dtype, transpose_out=False, square_rhs=False, halves=False, ride=None):
    k, m = at.shape
    nblk = b.shape[1] // bn
    rows, cols = (bn, k) if transpose_out else (k, bn)
    nout = 2 if halves else 1
    per = rows // nout

    def body(a_ref, b_ref, *o_refs):
        bv = b_ref[...]
        if square_rhs:
            bv = bv * bv
        r = jnp.dot(a_ref[...], bv, preferred_element_type=F32)
        r = (r.T if transpose_out else r).astype(out_dtype)
        for i, o_ref in enumerate(o_refs):
            o_ref[...] = r[i * per:(i + 1) * per, :]

    out, ex = _pcall_ride(
        body, ride, (at, b), name=name, grid=(nblk,),
        in_specs=[pl.BlockSpec((k, m), lambda j: (0, 0)), pl.BlockSpec((m, bn), lambda j: (0, j))],
        out_specs=tuple(pl.BlockSpec((None, per, cols), lambda j: (j, 0, 0)) for _ in range(nout)),
        out_shape=tuple(_sds((nblk, per, cols), out_dtype) for _ in range(nout)), compiler_params=_cp())
    out = out if halves else out[0]
    return out if ride is None else (out, ex)


def _mm_in(hn, w, cosf, sins, ride):
    m, k = hn.shape
    nb, _, bn = w.shape

    def body(a_ref, w_ref, c_ref, s_ref, qkv_ref, rest_ref, pt):
        av = a_ref[...]
        for j in range(nb):
            pt[:, j * bn:(j + 1) * bn] = jnp.dot(av, w_ref[j], preferred_element_type=F32)
        cf, ss = c_ref[...], s_ref[...]
        for h in range(HEADS):
            sq = slice(h * HEAD_DIM, (h + 1) * HEAD_DIM)
            sk = slice(RET_W + h * HEAD_DIM, RET_W + (h + 1) * HEAD_DIM)
            qkv_ref[:, sq] = _rot(pt[:, sq], cf, ss).astype(BF16)
            qkv_ref[:, sk] = (_rot(pt[:, sk], cf, ss) * K_SCALE).astype(BF16)
        qkv_ref[:, 2 * RET_W:] = pt[:, 2 * RET_W:3 * RET_W].astype(BF16)
        rest_ref[...] = pt[:, 3 * RET_W:]

    tab = pl.BlockSpec((TM, HEAD_DIM), lambda i: (i, 0))
    wide = pl.BlockSpec((TM, 3 * RET_W), lambda i: (i, 0))
    return _pcall_ride(
        body, ride, (hn, w, cosf, sins), name="mm_in", grid=(m // TM,),
        in_specs=[pl.BlockSpec((TM, k), lambda i: (i, 0)), pl.BlockSpec((nb, k, bn), lambda i: (0, 0, 0)), tab, tab],
        out_specs=(wide, wide), out_shape=(_sds((m, 3 * RET_W), BF16), _sds((m, nb * bn - 3 * RET_W), F32)),
        scratch_shapes=[pltpu.VMEM((TM, nb * bn), F32)], compiler_params=_cp())


def _mod_exchange(pack_a, silu_cc, w_ada, b_cols):
    d, cols = pack_a.shape[1], w_ada.shape[1]

    def body(pa_ref, scc_ref, w_ref, b_ref, alla_ref, s16_ref, modp_ref, mp, send1, recv1, send2, recv2):
        mx, my, mc = lax.axis_index("x"), lax.axis_index("y"), lax.axis_index("c")
        me = 4 * mx + 2 * my + mc

        def exchange(src_ref, dst_ref, send, recv):
            copies = []
            for k in range(1, N_DEV):
                peer = (1 - mx if (k >> 2) & 1 else mx, 1 - my if (k >> 1) & 1 else my, 1 - mc if k & 1 else mc)
                copies.append(pltpu.make_async_remote_copy(
                    src_ref=src_ref, dst_ref=dst_ref.at[me], send_sem=send.at[k - 1], recv_sem=recv.at[k - 1],
                    device_id=peer, device_id_type=MESH))
                copies[-1].start()
            dst_ref[me] = src_ref[...]
            for cp in copies:
                cp.wait()

        exchange(pa_ref, alla_ref, send1, recv1)
        s16_ref[...] = jnp.zeros_like(s16_ref)
        for j in range(N_DEV):
            s16_ref[j:j + 1, :] = alla_ref[j, 0:1, :]
        s16_ref[N_DEV:N_DEV + 1, :] = scc_ref[...]
        mp[...] = _dot(s16_ref[...], w_ref[...]) + b_ref[...]
        exchange(mp, modp_ref, send2, recv2)

    vmem = pl.BlockSpec(memory_space=pltpu.VMEM)
    sem7 = pltpu.SemaphoreType.DMA((N_DEV - 1,))
    return _pcall(
        body, name="mod_exchange", in_specs=[vmem] * 4, out_specs=(vmem,) * 3,
        out_shape=(_sds((N_DEV,) + pack_a.shape, F32), _sds((2 * N_DEV, d), F32), _sds((N_DEV, 2 * N_DEV, cols), F32)),
        scratch_shapes=[pltpu.VMEM((2 * N_DEV, cols), F32), sem7, sem7, sem7, sem7], compiler_params=_cp(),
    )(pack_a, silu_cc, w_ada, b_cols)


def _ada_bwd(s16, dm_cols, w_ada):
    def body(s_ref, d_ref, w_ref, gw_ref, ds_ref):
        gw_ref[...] = _dot_tn(s_ref[...], d_ref[...])
        ds_ref[...] = _dot_nt(d_ref[...], w_ref[...])

    return _pcall(body, name="ada_bwd",
                  out_shape=(_sds(w_ada.shape, F32), _sds(s16.shape, F32)), compiler_params=_cp())(s16, dm_cols, w_ada)


def _norm1_fwd(ctx, x, g, modrows, cb, ride=None):
    l_len, d = ctx.shape
    nb = (l_len + x.shape[0]) // TM

    def body(ctx_ref, x_ref, g_ref, m_ref, o_ref, ot_ref):
        is_ctx = pl.program_id(0) < cb
        xin = jnp.where(is_ctx, ctx_ref[...], x_ref[...])
        sh = jnp.where(is_ctx, m_ref[R_CSH1:R_CSH1 + 1, :], m_ref[R_SH1:R_SH1 + 1, :])
        sc = jnp.where(is_ctx, m_ref[R_CSC1:R_CSC1 + 1, :], m_ref[R_SC1:R_SC1 + 1, :])
        ms = jnp.mean(xin * xin, axis=-1, keepdims=True)
        n = xin * lax.rsqrt(ms + EPS) * g_ref[...]
        hn = n * (1.0 + sc) + sh
        o_ref[...] = hn.astype(BF16)
        ot_ref[...] = hn.T.astype(BF16)

    return _pcall_ride(
        body, ride, (ctx, x, g, modrows), name="norm1_fwd", grid=(nb,),
        in_specs=[pl.BlockSpec((TM, d), lambda i: (jnp.minimum(i, cb - 1), 0)),
                  pl.BlockSpec((TM, d), lambda i: (jnp.maximum(i - cb, 0), 0)),
                  pl.BlockSpec((1, d), lambda i: (0, 0)), pl.BlockSpec((8, d), lambda i: (0, 0))],
        out_specs=(pl.BlockSpec((TM, d), lambda i: (i, 0)), pl.BlockSpec((d, TM), lambda i: (0, i))),
        out_shape=(_sds((nb * TM, d), BF16), _sds((d, nb * TM), BF16)), compiler_params=_cp())


def _mix_fwd(o_f, o_b, p, h_f, h_b, cb, t_len):
    def body(of_ref, ob_ref, g_ref, gate_ref, hf_ref, hb_ref, mix_ref):
        o = of_ref[...] + ob_ref[...]
        g = g_ref[...]
        sg = g * _sigmoid(g)
        for hh in range(HEADS):
            sl = slice(hh * HEAD_DIM, (hh + 1) * HEAD_DIM)
            oh = o[:, sl]
            yc = oh - jnp.mean(oh, axis=-1, keepdims=True)
            var = jnp.mean(yc * yc, axis=-1, keepdims=True)
            mix_ref[:, sl] = (sg[:, sl] * (yc * lax.rsqrt(var + EPS))).astype(BF16)
        mix_ref[:, RET_W:] = ((hf_ref[...] + hb_ref[...]) * _gelu(gate_ref[...])).astype(BF16)

    row = lambda i: (i + cb, 0)
    return _pcall(
        body, name="mix_fwd", grid=(t_len // TM,),
        in_specs=[pl.BlockSpec((TM, RET_W), row), pl.BlockSpec((TM, RET_W), row),
                  pl.BlockSpec((TM, RET_W), lambda i: (i + cb, COL_G)), pl.BlockSpec((TM, LRU_W), lambda i: (i + cb, COL_GATE)),
                  pl.BlockSpec((TM, LRU_W), row), pl.BlockSpec((TM, LRU_W), row)],
        out_specs=pl.BlockSpec((TM, RET_W + LRU_W), lambda i: (i, 0)),
        out_shape=_sds((t_len, RET_W + LRU_W), BF16), compiler_params=_cp(),
    )(o_f, o_b, p, p, h_f, h_b)


def _res_norm2_epilogue(x, g, modrows, tm):
    t_len, d = x.shape
    tm = min(tm, t_len)

    def fn(y, rows, x_ref, g_ref, m_ref, y_ref, x1_ref, h2_ref, h2t_ref):
        y_ref[rows, :] = y
        x1 = x_ref[rows, :] + m_ref[R_G1:R_G1 + 1, :] * y
        ms = jnp.mean(x1 * x1, axis=-1, keepdims=True)
        n = x1 * lax.rsqrt(ms + EPS) * g_ref[...]
        x1_ref[rows, :] = x1
        h2 = n * (1.0 + m_ref[R_SC2:R_SC2 + 1, :]) + m_ref[R_SH2:R_SH2 + 1, :]
        h2_ref[rows, :] = h2.astype(BF16)
        h2t_ref[:, rows] = h2.T.astype(BF16)

    t = pl.BlockSpec((tm, d), lambda i: (i, 0))
    return _Epilogue(
        fn, (x, g, modrows),
        in_specs=[t, pl.BlockSpec((1, d), lambda i: (0, 0)), pl.BlockSpec((8, d), lambda i: (0, 0))],
        out_specs=(t, t, t, pl.BlockSpec((d, tm), lambda i: (0, i))),
        out_shape=(_sds((t_len, d), F32), _sds((t_len, d), F32), _sds((t_len, d), BF16), _sds((d, t_len), BF16)))


def _tile(i):
    return jnp.maximum(i - 1, 0)


def _zero_at_start(acc_ref, rows):
    @pl.when(pl.program_id(0) == 0)
    def _():
        acc_ref[...] = jnp.zeros_like(acc_ref)


def _final_epilogue(x1, target, fg, modrows, tm):
    t_len, d = x1.shape

    def fn(z, rows, x1_ref, t_ref, fg_ref, m_ref, dx2_ref, dz_ref, dzt_ref, acc_ref):
        _zero_at_start(acc_ref, rows)
        g2 = m_ref[R_G2:R_G2 + 1, :]
        x2 = x1_ref[rows, :] + g2 * z
        rstd = lax.rsqrt(jnp.mean(x2 * x2, axis=-1, keepdims=True) + EPS)
        xh = x2 * rstd
        fg = fg_ref[...]
        e = xh * fg - t_ref[rows, :]
        dy = e * (1.0 / d)
        dxh = dy * fg
        dx2 = rstd * (dxh - xh * jnp.mean(dxh * xh, axis=-1, keepdims=True))
        dx2_ref[rows, :] = dx2
        dz = g2 * dx2
        dz_ref[rows, :] = dz.astype(BF16)
        dzt_ref[:, rows] = dz.T.astype(BF16)
        acc_ref[0:1, :] += jnp.sum(dy * xh, axis=0, keepdims=True)
        acc_ref[1:2, :] += jnp.sum(dx2 * z, axis=0, keepdims=True)
        acc_ref[2:3, :] += jnp.sum(e * e, axis=0, keepdims=True)

    t = pl.BlockSpec((tm, d), lambda i: (i, 0))
    return _Epilogue(
        fn, (x1, target, fg, modrows),
        in_specs=[t, t, pl.BlockSpec((1, d), lambda i: (0, 0)), pl.BlockSpec((8, d), lambda i: (0, 0))],
        out_specs=(t, t, pl.BlockSpec((d, tm), lambda i: (0, i)), pl.BlockSpec((8, d), lambda i: (0, 0))),
        out_shape=(_sds((t_len, d), F32), _sds((t_len, d), BF16), _sds((d, t_len), BF16), _sds((8, d), F32)))


def _bwd_norm2_epilogue(x1, dx2, y, g, modrows, tm):
    t_len, d = x1.shape

    def fn(dh2, rows, x1_ref, dx2_ref, y_ref, g_ref, m_ref, dx1_ref, dy_ref, acc_ref):
        _zero_at_start(acc_ref, rows)
        x1 = x1_ref[rows, :]
        rstd = lax.rsqrt(jnp.mean(x1 * x1, axis=-1, keepdims=True) + EPS)
        xh = x1 * rstd
        gn = g_ref[...]
        dn = dh2 * (1.0 + m_ref[R_SC2:R_SC2 + 1, :])
        dxh = dn * gn
        dx1 = dx2_ref[rows, :] + rstd * (dxh - xh * jnp.mean(dxh * xh, axis=-1, keepdims=True))
        dx1_ref[rows, :] = dx1
        dy_ref[rows, :] = (m_ref[R_G1:R_G1 + 1, :] * dx1).astype(BF16)
        acc_ref[0:1, :] += jnp.sum(dh2, axis=0, keepdims=True)
        acc_ref[1:2, :] += jnp.sum(dh2 * xh * gn, axis=0, keepdims=True)
        acc_ref[2:3, :] += jnp.sum(dn * xh, axis=0, keepdims=True)
        acc_ref[3:4, :] += jnp.sum(dx1 * y_ref[rows, :], axis=0, keepdims=True)

    t = pl.BlockSpec((tm, d), lambda i: (i, 0))
    return _Epilogue(
        fn, (x1, dx2, y, g, modrows),
        in_specs=[t, t, t, pl.BlockSpec((1, d), lambda i: (0, 0)), pl.BlockSpec((8, d), lambda i: (0, 0))],
        out_specs=(t, t, pl.BlockSpec((8, d), lambda i: (0, 0))),
        out_shape=(_sds((t_len, d), F32), _sds((t_len, d), BF16), _sds((8, d), F32)))


def _mix_bwd_epilogue(o_f, o_b, p, h_f, h_b, cb):
    n = o_f.shape[0]
    tile = _tile

    def fn(dm, rows, of_ref, ob_ref, g_ref, gate_ref, hf_ref, hb_ref, do_ref, dg_ref, dgate_ref, dhs_ref):
        keep = jnp.where(pl.program_id(0) - 1 < cb, 0.0, 1.0)
        dm = dm * keep
        o = of_ref[rows, :] + ob_ref[rows, :]
        g = g_ref[rows, :]
        s = _sigmoid(g)
        sg = g * s
        dsg = s * (1.0 + g * (1.0 - s))
        for hh in range(HEADS):
            sl = slice(hh * HEAD_DIM, (hh + 1) * HEAD_DIM)
            oh = o[:, sl]
            yc = oh - jnp.mean(oh, axis=-1, keepdims=True)
            rs = lax.rsqrt(jnp.mean(yc * yc, axis=-1, keepdims=True) + EPS)
            gn = yc * rs
            dret = dm[:, sl]
            dgn = dret * sg[:, sl]
            dg_ref[rows, sl] = (dret * gn * dsg[:, sl]).astype(BF16)
            do_ref[rows, sl] = (rs * (dgn - jnp.mean(dgn, axis=-1, keepdims=True)
                                      - gn * jnp.mean(dgn * gn, axis=-1, keepdims=True))).astype(BF16)
        dlru = dm[:, RET_W:]
        gate = gate_ref[rows, :]
        dhs_ref[rows, :] = dlru * _gelu(gate)
        dgate_ref[rows, :] = (dlru * (hf_ref[rows, :] + hb_ref[rows, :]) * _dgelu(gate)).astype(BF16)

    t = pl.BlockSpec((TM, RET_W), lambda i: (tile(i), 0))
    return _Epilogue(
        fn, (o_f, o_b, p, p, h_f, h_b),
        in_specs=[t, t, pl.BlockSpec((TM, RET_W), lambda i: (tile(i), COL_G)),
                  pl.BlockSpec((TM, LRU_W), lambda i: (tile(i), COL_GATE)), t, t],
        out_specs=(t, t, t, t), out_shape=(_sds((n, RET_W), BF16),) * 3 + (_sds((n, RET_W), F32),),
        steps=n // TM, lhs_map=lambda i: (jnp.maximum(i - cb, 0), 0), delay=1)


def _bwd_norm1_epilogue(ctx, x, dx1, g, modrows, cb):
    t_len, d = x.shape

    def fn(dh, rows, ctx_ref, x_ref, dx1_ref, g_ref, m_ref, gx_ref, acc_ref):
        is_ctx = pl.program_id(0) < cb
        _zero_at_start(acc_ref, rows)
        xin = jnp.where(is_ctx, ctx_ref[rows, :], x_ref[rows, :])
        sc = jnp.where(is_ctx, m_ref[R_CSC1:R_CSC1 + 1, :], m_ref[R_SC1:R_SC1 + 1, :])
        rstd = lax.rsqrt(jnp.mean(xin * xin, axis=-1, keepdims=True) + EPS)
        xh = xin * rstd
        gn = g_ref[...]
        dn = dh * (1.0 + sc)
        dxh = dn * gn
        gx_ref[rows, :] = dx1_ref[rows, :] + rstd * (dxh - xh * jnp.mean(dxh * xh, axis=-1, keepdims=True))
        s0 = jnp.sum(dh, axis=0, keepdims=True)
        s1 = jnp.sum(dh * xh * gn, axis=0, keepdims=True)
        acc_ref[4:5, :] += jnp.sum(dn * xh, axis=0, keepdims=True)
        acc_ref[0:1, :] += jnp.where(is_ctx, s0, 0.0)
        acc_ref[1:2, :] += jnp.where(is_ctx, s1, 0.0)
        acc_ref[2:3, :] += jnp.where(is_ctx, 0.0, s0)
        acc_ref[3:4, :] += jnp.where(is_ctx, 0.0, s1)

    lat = pl.BlockSpec((TM, d), lambda i: (jnp.maximum(i - cb, 0), 0))
    return _Epilogue(
        fn, (ctx, x, dx1, g, modrows),
        in_specs=[pl.BlockSpec((TM, d), lambda i: (jnp.minimum(i, cb - 1), 0)),
                  lat, lat, pl.BlockSpec((1, d), lambda i: (0, 0)), pl.BlockSpec((8, d), lambda i: (0, 0))],
        out_specs=(lat, pl.BlockSpec((8, d), lambda i: (0, 0))),
        out_shape=(_sds((t_len, d), F32), _sds((8, d), F32)))


def _rot(x, cf, ss):
    return x * cf + pltpu.roll(x, HEAD_DIM // 2, 1) * ss


def _decay_exponents(dirn):
    ii = lax.broadcasted_iota(jnp.int32, (CHUNK, CHUNK), 0)
    jj = lax.broadcasted_iota(jnp.int32, (CHUNK, CHUNK), 1)
    rel = ii - jj if dirn == 0 else jj - ii
    pos = ii.astype(F32)
    if dirn == 0:
        cq, cs = pos + 1.0, (CHUNK - 1.0) - pos
    else:
        cq, cs = CHUNK - pos, pos
    return rel, jnp.maximum(rel, 0).astype(F32), cq, cs


def _store_decay(lg_ref, dec, relf_ref=None):
    for dirn in (0, 1):
        rel, relf, cq, cs = _decay_exponents(dirn)
        if relf_ref is not None:
            relf_ref[dirn] = relf
        for h in range(HEADS):
            lgv = lg_ref[dirn, h]
            wq, ws = jnp.exp(lgv * cq), jnp.exp(lgv * cs)
            dec[dirn, h, 0] = jnp.where(rel >= 0, jnp.exp(lgv * relf), 0.0)
            dec[dirn, h, 1] = wq
            dec[dirn, h, 2] = ws
            if relf_ref is not None:
                dec[dirn, h, 3] = wq * cq
                dec[dirn, h, 4] = ws * cs


def _ret_rows(cc, nc, step_of):
    return [lambda s, dirn=dirn: _tile_order(dirn, step_of(s), cc, nc) for dirn in (0, 1)]


def _ret_in_specs(rows):
    specs = []
    for row in rows:
        specs += [pl.BlockSpec((CHUNK, RET_W), lambda s, o=o, row=row: (row(s), o)) for o in (0, 1, 2)]
    return specs


def _ret_fwd(qkv, lg, cc, ride=None):
    n = qkv.shape[0]
    nc = n // CHUNK
    rows = _ret_rows(cc, nc, lambda s: s)

    def body(lg_ref, q0, k0, v0, q1, k1, v1, o0, o1, sp0, sp1, st, dec):
        @pl.when(pl.program_id(0) == 0)
        def _():
            st[...] = jnp.zeros_like(st)
            _store_decay(lg_ref, dec)

        refs = ((q0, k0, v0, o0, sp0), (q1, k1, v1, o1, sp1))
        chains = [(dirn, h, slice(h * HEAD_DIM, (h + 1) * HEAD_DIM)) for dirn in (0, 1) for h in range(HEADS)]
        scores, cross, update = [], [], []
        for dirn, h, sl in chains:
            q_ref, k_ref, v_ref, _, sp_ref = refs[dirn]
            q, k, v = q_ref[:, sl], k_ref[:, sl], v_ref[:, sl]
            sp = st[dirn, h]
            sp_ref[h] = sp
            scores.append(_dot_nt(q, k))
            cross.append(_dot(q * dec[dirn, h, 1], sp))
            update.append(_dot_tn(k * dec[dirn, h, 2], v))
        masked = [(a * dec[dirn, h, 0]).astype(BF16) for a, (dirn, h, _) in zip(scores, chains)]
        intra = [_dot(sc, refs[dirn][2][:, sl]) for sc, (dirn, h, sl) in zip(masked, chains)]
        for (dirn, h, sl), o_in, o_cr, upd in zip(chains, intra, cross, update):
            refs[dirn][3][:, sl] = o_in + o_cr
            st[dirn, h] = jnp.exp(lg_ref[dirn, h] * CHUNK) * st[dirn, h] + upd

    o_specs = [pl.BlockSpec((CHUNK, RET_W), lambda s, row=row: (row(s), 0)) for row in rows]
    state = pl.BlockSpec((None, HEADS, CHUNK, HEAD_DIM), lambda s: (s, 0, 0, 0))
    return _pcall_ride(
        body, ride, (lg,) + (qkv,) * 6, name="ret_fwd", grid=(nc,),
        in_specs=[pl.BlockSpec(memory_space=pltpu.SMEM)] + _ret_in_specs(rows),
        out_specs=(o_specs[0], o_specs[1], state, state),
        out_shape=(_sds((n, RET_W), F32),) * 2 + (_sds((nc, HEADS, CHUNK, HEAD_DIM), F32),) * 2,
        scratch_shapes=[pltpu.VMEM((2, HEADS, CHUNK, HEAD_DIM), F32), pltpu.VMEM((2, HEADS, 3, CHUNK, CHUNK), F32)],
        compiler_params=_cp())


def _ret_bwd(qkv, lg, do, s_prev, cc, ride=None):
    n = qkv.shape[0]
    nc = n // CHUNK
    rows = _ret_rows(cc, nc, lambda s: nc - 1 - s)

    def body(lg_ref, q0, k0, v0, q1, k1, v1, do0, do1, sp0, sp1, dq0, dk0, dv0, dq1, dk1, dv1, dlg_ref, dst, dec, relf):
        @pl.when(pl.program_id(0) == 0)
        def _():
            dst[...] = jnp.zeros_like(dst)
            dlg_ref[...] = jnp.zeros_like(dlg_ref)
            _store_decay(lg_ref, dec, relf)

        refs = ((q0, k0, v0, do0, sp0, dq0, dk0, dv0), (q1, k1, v1, do1, sp1, dq1, dk1, dv1))
        chains = [(dirn, h, slice(h * HEAD_DIM, (h + 1) * HEAD_DIM)) for dirn in (0, 1) for h in range(HEADS)]

        def tiles(dirn, sl):
            q_ref, k_ref, v_ref, do_ref = refs[dirn][:4]
            return q_ref[:, sl], k_ref[:, sl], v_ref[:, sl], do_ref[:, sl]

        a_s, g1_s, da_s, h1_s = [], [], [], []
        for dirn, h, sl in chains:
            q, k, v, dov = tiles(dirn, sl)
            a_s.append(_dot_nt(q, k))
            g1_s.append(_dot_nt(dov, refs[dirn][4][h]))
            da_s.append(_dot_nt(dov, v))
            h1_s.append(_dot_nt(v, dst[dirn, h]))
        da_s = [da * dec[dirn, h, 0] for da, (dirn, h, _) in zip(da_s, chains)]
        dq_s, dk_s, dv_s, ds_s = [], [], [], []
        for (dirn, h, sl), a, da in zip(chains, a_s, da_s):
            q, k, v, dov = tiles(dirn, sl)
            dq_s.append(_dot(da, k))
            dk_s.append(_dot_tn(da, q))
            dv_s.append(_dot_tn(a * dec[dirn, h, 0], dov) + _dot(k * dec[dirn, h, 2], dst[dirn, h]))
            ds_s.append(_dot_tn(q * dec[dirn, h, 1], dov))
        for (dirn, h, sl), a, g1, da, h1, dq2, dk2, dv, ds2 in zip(chains, a_s, g1_s, da_s, h1_s, dq_s, dk_s, dv_s, ds_s):
            q, k, _, _ = tiles(dirn, sl)
            dq_ref, dk_ref, dv_ref = refs[dirn][5:]
            sp, dsn = refs[dirn][4][h], dst[dirn, h]
            gc = jnp.exp(lg_ref[dirn, h] * CHUNK)
            dq_ref[:, sl] = (g1 * dec[dirn, h, 1] + dq2).astype(BF16)
            dk_ref[:, sl] = (dk2 + h1 * dec[dirn, h, 2]).astype(BF16)
            dv_ref[:, sl] = dv.astype(BF16)
            term = da * a * relf[dirn] + q * g1 * dec[dirn, h, 3] + k * h1 * dec[dirn, h, 4] + sp * dsn * (CHUNK * gc)
            dlg_ref[dirn * HEADS + h:dirn * HEADS + h + 1, 0:HEAD_DIM] += jnp.sum(term, axis=0, keepdims=True)
            dst[dirn, h] = gc * dsn + ds2

    wide = [pl.BlockSpec((CHUNK, RET_W), lambda s, row=row: (row(s), 0)) for row in rows]
    state = pl.BlockSpec((None, HEADS, CHUNK, HEAD_DIM), lambda s: (nc - 1 - s, 0, 0, 0))
    return _pcall_ride(
        body, ride, (lg,) + (qkv,) * 6 + (do, do, s_prev[0], s_prev[1]), name="ret_bwd", grid=(nc,),
        in_specs=[pl.BlockSpec(memory_space=pltpu.SMEM)] + _ret_in_specs(rows) + wide + [state, state],
        out_specs=(wide[0],) * 3 + (wide[1],) * 3 + (pl.BlockSpec((2 * HEADS, 8 * HEAD_DIM), lambda s: (0, 0)),),
        out_shape=(_sds((n, RET_W), BF16),) * 6 + (_sds((2 * HEADS, 8 * HEAD_DIM), F32),),
        scratch_shapes=[pltpu.VMEM((2, HEADS, CHUNK, HEAD_DIM), F32), pltpu.VMEM((2, HEADS, 5, CHUNK, CHUNK), F32),
                        pltpu.VMEM((2, CHUNK, CHUNK), F32)],
        compiler_params=_cp())


def _shift_rows(cur, prev8, next8, k, seg_start, seg_end):
    tm = cur.shape[0]
    rows = _rows_iota(cur.shape)
    if k < 0:
        out = pltpu.roll(cur, -k, 0)
        for j in range(-k):
            halo = jnp.where(seg_start, 0.0, prev8[SUB + k + j:SUB + k + j + 1, :])
            out = jnp.where(rows == j, halo, out)
    else:
        out = pltpu.roll(cur, tm - k, 0)
        for j in range(k):
            halo = jnp.where(seg_end, 0.0, next8[j:j + 1, :])
            out = jnp.where(rows == tm - k + j, halo, out)
    return out


def _seg_flags(t, cb, nb):
    return jnp.logical_or(t == 0, t == cb), jnp.logical_or(t == cb - 1, t == nb - 1)


def _halo_specs(tile_of, n_rows, col):
    per = TM // SUB
    return [pl.BlockSpec((TM, LRU_W), lambda s: (tile_of(s), col)),
            pl.BlockSpec((SUB, LRU_W), lambda s: (jnp.maximum(tile_of(s) * per - 1, 0), col)),
            pl.BlockSpec((SUB, LRU_W), lambda s: (jnp.minimum((tile_of(s) + 1) * per, n_rows // SUB - 1), col))]


def _lru_gates(xr, prev8, next8, seg_start, seg_end, cw_ref, cb_ref, wg_ref, bg_ref, sp_ref):
    xm1 = _shift_rows(xr, prev8, next8, -1, seg_start, seg_end)
    xp1 = _shift_rows(xr, prev8, next8, 1, seg_start, seg_end)
    xp2 = _shift_rows(xr, prev8, next8, 2, seg_start, seg_end)
    xc = cb_ref[...] + xm1 * cw_ref[0:1, :] + xr * cw_ref[1:2, :] + xp1 * cw_ref[2:3, :] + xp2 * cw_ref[3:4, :]
    pre = _dot(xc, wg_ref[...]) + bg_ref[...]
    r = _sigmoid(pre[:, :LRU_W])
    i = _sigmoid(pre[:, LRU_W:])
    la = (-LRU_C) * r * sp_ref[...]
    a = jnp.exp(la)
    th = jnp.tanh(la)
    sq = jnp.sqrt(-2.0 * th / (1.0 - th))
    return xc, r, i, a, sq


def _scan_tile(a, b, ascending, carry, out_ref):
    tm, w = a.shape
    nsub = tm // SUB
    a = a.reshape(nsub, SUB, w)
    b = b.reshape(nsub, SUB, w)
    r8 = lax.broadcasted_iota(jnp.int32, a.shape, 1)
    for k in (1, 2, 4):
        if ascending:
            m = r8 >= k
            a_s, b_s = pltpu.roll(a, k, 1), pltpu.roll(b, k, 1)
        else:
            m = r8 < SUB - k
            a_s, b_s = pltpu.roll(a, SUB - k, 1), pltpu.roll(b, SUB - k, 1)
        b = a * jnp.where(m, b_s, 0.0) + b
        a = a * jnp.where(m, a_s, 1.0)
    c = carry[...]
    for j in range(nsub):
        off = (j if ascending else nsub - 1 - j) * SUB
        hb = a[off // SUB] * c + b[off // SUB]
        out_ref[off:off + SUB, :] = hb
        c = jnp.broadcast_to(hb[SUB - 1:SUB, :] if ascending else hb[0:1, :], c.shape)
    carry[...] = c


def _lru_fwd(p, wg, bg, sp, cw, cbias, dirn, cb, ride=None):
    n = p.shape[0]
    nb = n // TM
    tile_of = lambda s: _tile_order(dirn, s, cb, nb)

    def body(x_ref, xp_ref, xn_ref, wg_ref, bg_ref, sp_ref, cw_ref, cb_ref, h_ref, cin_ref, carry):
        s = pl.program_id(0)

        @pl.when(s == 0)
        def _():
            carry[...] = jnp.zeros_like(carry)

        seg_start, seg_end = _seg_flags(tile_of(s), cb, nb)
        xc, r, i, a, sq = _lru_gates(x_ref[...], xp_ref[...], xn_ref[...], seg_start, seg_end,
                                     cw_ref, cb_ref, wg_ref, bg_ref, sp_ref)
        cin_ref[...] = carry[...]
        _scan_tile(a, sq * (i * xc), dirn == 0, carry, h_ref)

    full = lambda shape: pl.BlockSpec(shape, lambda s: (0,) * len(shape))
    return _pcall_ride(
        body, ride, (p, p, p, wg, bg, sp, cw, cbias), name=f"lru_fwd{dirn}", grid=(nb,),
        in_specs=_halo_specs(tile_of, n, COL_XR) + [full((LRU_W, 2 * LRU_W)), full((1, 2 * LRU_W)), full((1, LRU_W)),
                                               full((4, LRU_W)), full((1, LRU_W))],
        out_specs=(pl.BlockSpec((TM, LRU_W), lambda s: (tile_of(s), 0)),
                   pl.BlockSpec((None, SUB, LRU_W), lambda s: (tile_of(s), 0, 0))),
        out_shape=(_sds((n, LRU_W), F32), _sds((nb, SUB, LRU_W), F32)),
        scratch_shapes=[pltpu.VMEM((SUB, LRU_W), F32)], compiler_params=_cp())


def _lru_bwd(p, wg, bg, sp, cw, cbias, h, cin, dhs, dirn, cb, ride=None):
    n = p.shape[0]
    nb = n // TM
    tile_of = lambda s: _tile_order(dirn, nb - 1 - s, cb, nb)

    def body(x_ref, xp_ref, xn_ref, wg_ref, bg_ref, sp_ref, cw_ref, cb_ref, h_ref, cin_ref, dhs_ref,
             dxc_ref, dwd_ref, acc_ref, carry, mu_sc, dwg_ref):
        s = pl.program_id(0)

        @pl.when(s == 0)
        def _():
            carry[...] = jnp.zeros_like(carry)
            dwg_ref[...] = jnp.zeros_like(dwg_ref)
            acc_ref[...] = jnp.zeros_like(acc_ref)

        seg_start, seg_end = _seg_flags(tile_of(s), cb, nb)
        xc, r, i, a, sq = _lru_gates(x_ref[...], xp_ref[...], xn_ref[...], seg_start, seg_end,
                                     cw_ref, cb_ref, wg_ref, bg_ref, sp_ref)
        rows = _rows_iota(a.shape)
        hv = h_ref[...]
        dh = dhs_ref[...]
        mu_next = carry[0:1, :]
        _scan_tile(a, a * dh, dirn == 1, carry, mu_sc)
        mu = mu_sc[...]
        if dirn == 0:
            hprev = jnp.where(rows == 0, cin_ref[0:1, :], pltpu.roll(hv, 1, 0))
            lam = dh + jnp.where(rows == TM - 1, mu_next, pltpu.roll(mu, TM - 1, 0))
        else:
            hprev = jnp.where(rows == TM - 1, cin_ref[0:1, :], pltpu.roll(hv, TM - 1, 0))
            lam = dh + jnp.where(rows == 0, mu_next, pltpu.roll(mu, 1, 0))
        ds = lam * (i * xc)
        di = lam * (sq * xc)
        dla = lam * hprev * a - ds * (a * a) / jnp.maximum(sq, 1e-20)
        dpr = dla * ((-LRU_C) * sp_ref[...]) * r * (1.0 - r)
        dpi = di * i * (1.0 - i)
        dpre = jnp.concatenate([dpr, dpi], axis=1)
        dxc_ref[...] = lam * (sq * i) + _dot_nt(dpre, wg_ref[...])
        dwg_ref[...] += _dot_tn(xc, dpre)
        acc_ref[0:1, :] += jnp.sum(dpre, axis=0, keepdims=True)
        acc_ref[1:2, 0:LRU_W] += jnp.sum(dla * ((-LRU_C) * r), axis=0, keepdims=True)

        @pl.when(s == nb - 1)
        def _():
            low = lax.broadcasted_iota(jnp.int32, (LRU_BD, 2 * LRU_BD), 1) < LRU_BD
            for half in (0, LRU_W):
                for m in range(LRU_BLOCKS // 2):
                    lanes = slice(half + 2 * LRU_BD * m, half + 2 * LRU_BD * (m + 1))
                    even = dwg_ref[2 * m * LRU_BD:(2 * m + 1) * LRU_BD, lanes]
                    odd = dwg_ref[(2 * m + 1) * LRU_BD:(2 * m + 2) * LRU_BD, lanes]
                    dwd_ref[:, lanes] = jnp.where(low, even, odd)

    full = lambda shape: pl.BlockSpec(shape, lambda s: (0,) * len(shape))
    tile = pl.BlockSpec((TM, LRU_W), lambda s: (tile_of(s), 0))
    return _pcall_ride(
        body, ride, (p, p, p, wg, bg, sp, cw, cbias, h, cin, dhs), name=f"lru_bwd{dirn}", grid=(nb,),
        in_specs=_halo_specs(tile_of, n, COL_XR) + [full((LRU_W, 2 * LRU_W)), full((1, 2 * LRU_W)), full((1, LRU_W)),
                                               full((4, LRU_W)), full((1, LRU_W)), tile,
                                               pl.BlockSpec((None, SUB, LRU_W), lambda s: (tile_of(s), 0, 0)), tile],
        out_specs=(tile, full((LRU_BD, 2 * LRU_W)), full((8, 2 * LRU_W))),
        out_shape=(_sds((n, LRU_W), F32), _sds((LRU_BD, 2 * LRU_W), F32), _sds((8, 2 * LRU_W), F32)),
        scratch_shapes=[pltpu.VMEM((SUB, LRU_W), F32), pltpu.VMEM((TM, LRU_W), F32), pltpu.VMEM((LRU_W, 2 * LRU_W), F32)],
        compiler_params=_cp())


def _assemble_dp(dqs, dks, dvs, dg, dgate, dxcs, p, cw, cosf, sins, cb, ride=None):
    n = p.shape[0]
    nb = n // TM
    tile_of = lambda s: s

    def body(dqf, dqb, dkf, dkb, dvf, dvb, dg_ref, dgate_ref, cf, pf, nf, cb_, pb, nb_, x_ref, xp_ref, xn_ref,
             cw_ref, cos_ref, sin_ref, dp_ref, acc_ref):
        s = pl.program_id(0)

        @pl.when(s == 0)
        def _():
            acc_ref[...] = jnp.zeros_like(acc_ref)

        seg_start, seg_end = _seg_flags(s, cb, nb)
        dq = dqf[...].astype(F32) + dqb[...].astype(F32)
        dk = dkf[...].astype(F32) + dkb[...].astype(F32)
        cosv, sinv = cos_ref[...], sin_ref[...]
        for h in range(HEADS):
            sl = slice(h * HEAD_DIM, (h + 1) * HEAD_DIM)
            sk = slice(RET_W + h * HEAD_DIM, RET_W + (h + 1) * HEAD_DIM)
            dp_ref[:, sl] = (dq[:, sl] * cosv + pltpu.roll(dq[:, sl] * sinv, HEAD_DIM // 2, 1)).astype(BF16)
            dp_ref[:, sk] = ((dk[:, sl] * cosv + pltpu.roll(dk[:, sl] * sinv, HEAD_DIM // 2, 1)) * K_SCALE).astype(BF16)
        dp_ref[:, 2 * RET_W:3 * RET_W] = (dvf[...].astype(F32) + dvb[...].astype(F32)).astype(BF16)
        dp_ref[:, 3 * RET_W:4 * RET_W] = dg_ref[...].astype(BF16)
        dxc = cf[...] + cb_[...]
        dprev = pf[...] + pb[...]
        dnext = nf[...] + nb_[...]
        dxr = (_shift_rows(dxc, dprev, dnext, 1, seg_start, seg_end) * cw_ref[0:1, :] + dxc * cw_ref[1:2, :]
               + _shift_rows(dxc, dprev, dnext, -1, seg_start, seg_end) * cw_ref[2:3, :]
               + _shift_rows(dxc, dprev, dnext, -2, seg_start, seg_end) * cw_ref[3:4, :])
        dp_ref[:, 4 * RET_W:4 * RET_W + LRU_W] = dxr.astype(BF16)
        dp_ref[:, 4 * RET_W + LRU_W:] = dgate_ref[...].astype(BF16)
        xr, xp, xn = x_ref[...], xp_ref[...], xn_ref[...]
        for j, k in enumerate((-1, 0, 1, 2)):
            xs = xr if k == 0 else _shift_rows(xr, xp, xn, k, seg_start, seg_end)
            acc_ref[j:j + 1, 0:LRU_W] += jnp.sum(dxc * xs, axis=0, keepdims=True)
        acc_ref[4:5, 0:LRU_W] += jnp.sum(dxc, axis=0, keepdims=True)

    t = pl.BlockSpec((TM, RET_W), lambda s: (s, 0))
    args = (dqs[0], dqs[1], dks[0], dks[1], dvs[0], dvs[1], dg, dgate, dxcs[0], dxcs[0], dxcs[0], dxcs[1], dxcs[1], dxcs[1],
            p, p, p, cw, cosf, sins)
    tab = pl.BlockSpec((TM, HEAD_DIM), lambda s: (s, 0))
    return _pcall_ride(
        body, ride, args, name="assemble_dp", grid=(nb,),
        in_specs=[t] * 8 + _halo_specs(tile_of, n, 0) * 2 + _halo_specs(tile_of, n, COL_XR)
        + [pl.BlockSpec((4, LRU_W), lambda s: (0, 0)), tab, tab],
        out_specs=(pl.BlockSpec((TM, 4 * RET_W + 2 * LRU_W), lambda s: (s, 0)), pl.BlockSpec((8, 2 * LRU_W), lambda s: (0, 0))),
        out_shape=(_sds((n, 4 * RET_W + 2 * LRU_W), BF16), _sds((8, 2 * LRU_W), F32)), compiler_params=_cp())


def _adamw(g, w, m, v):
    nm = ADAM_B1 * m + (1.0 - ADAM_B1) * g
    nv = ADAM_B2 * v + (1.0 - ADAM_B2) * (g * g)
    m_hat = nm / (1.0 - ADAM_B1 ** ADAM_STEP)
    v_hat = nv / (1.0 - ADAM_B2 ** ADAM_STEP)
    return (-ADAM_LR) * (m_hat / (jnp.sqrt(v_hat) + ADAM_EPS) + ADAM_WD * w), nm, nv


def _adam_many(items, name):
    n = len(items)

    def body(*refs):
        for i in range(n):
            g, w, m, v = (r[...] for r in refs[4 * i:4 * i + 4])
            for o_ref, val in zip(refs[4 * n + 3 * i:4 * n + 3 * i + 3], _adamw(g, w, m, v)):
                o_ref[...] = val

    out_shape = tuple(_sds(it[1].shape, F32) for it in items for _ in range(3))
    res = _pcall(body, name=name, out_shape=out_shape, compiler_params=_cp())(*[a for it in items for a in it])
    return [tuple(res[3 * i:3 * i + 3]) for i in range(n)]


def _sum_adam(parts_list, w, m, v, name):
    nparts, _, c = parts_list[0].shape
    r = w.shape[0]
    tr = min(parts_list[0].shape[1], 128)
    starts, o = [], 0
    for pa in parts_list:
        starts.append(o)
        o += pa.shape[1] // tr
    nseg = len(parts_list)

    def body(*refs):
        p_refs = refs[:nseg]
        w_ref, m_ref, v_ref, g_ref, d_ref, nm_ref, nv_ref = refs[nseg:]
        i = pl.program_id(0)
        for s, p_ref in enumerate(p_refs):
            end = starts[s + 1] if s + 1 < nseg else r // tr

            @pl.when(jnp.logical_and(i >= starts[s], i < end))
            def _():
                g = p_ref[0].astype(F32)
                for j in range(1, nparts):
                    g = g + p_ref[j].astype(F32)
                g_ref[...] = g
                d_ref[...], nm_ref[...], nv_ref[...] = _adamw(g, w_ref[...], m_ref[...], v_ref[...])

    def seg_spec(s):
        last = parts_list[s].shape[1] // tr - 1
        return pl.BlockSpec((nparts, tr, c), lambda i: (0, jnp.clip(i - starts[s], 0, last), 0))

    t = pl.BlockSpec((tr, c), lambda i: (i, 0))
    return _pcall(
        body, name=name, grid=(r // tr,),
        in_specs=[seg_spec(s) for s in range(nseg)] + [t, t, t],
        out_specs=(t, t, t, t), out_shape=(_sds((r, c), F32),) * 4, compiler_params=_cp(),
    )(*parts_list, w, m, v)


def _sum_parts(parts, name):
    nparts, r, c = parts.shape

    def body(p_ref, o_ref):
        g = p_ref[0]
        for j in range(1, nparts):
            g = g + p_ref[j]
        o_ref[...] = g

    return _pcall(body, name=name, out_shape=_sds((r, c), parts.dtype), compiler_params=_cp())(parts)


def _rot_tables(l_len, t_len):
    rows = t_len // GRID_W
    n_freq = HEAD_DIM // 4
    inv = ROPE_BASE ** (-jnp.arange(n_freq, dtype=F32) / n_freq)
    ang_r = jnp.arange(rows, dtype=F32)[:, None] * inv
    ang_c = jnp.arange(GRID_W, dtype=F32)[:, None] * inv
    cos = jnp.concatenate([jnp.repeat(jnp.cos(ang_r), GRID_W, axis=0), jnp.tile(jnp.cos(ang_c), (rows, 1))], axis=-1)
    sin = jnp.concatenate([jnp.repeat(jnp.sin(ang_r), GRID_W, axis=0), jnp.tile(jnp.sin(ang_c), (rows, 1))], axis=-1)
    cosf = jnp.concatenate([jnp.ones((l_len, HEAD_DIM), F32), jnp.concatenate([cos, cos], axis=-1)], axis=0)
    sins = jnp.concatenate([jnp.zeros((l_len, HEAD_DIM), F32), jnp.concatenate([-sin, sin], axis=-1)], axis=0)
    return cosf, sins


def _block_diag(w):
    eye = jnp.eye(LRU_BLOCKS, dtype=w.dtype)
    return (w[:, :, None, :] * eye[:, None, :, None]).reshape(LRU_W, LRU_W)


def _blocks_from_lanes(dwd_half):
    return dwd_half.reshape(LRU_BD, LRU_BLOCKS, LRU_BD).transpose(1, 0, 2)


def _silu(x):
    return x * jax.nn.sigmoid(x)


def kernel(x, c, ctx, c_ctx, w_ada, b_ada, norm1_g, norm2_g, w_in, ret_decay, conv_w, conv_b, lru_wa, lru_ba, lru_wx, lru_bx, lru_lambda, w_out, w_mlp1, w_mlp2, final_g, loss_target, m_c_ctx, m_w_ada, m_b_ada, m_norm1_g, m_norm2_g, m_w_in, m_ret_decay, m_conv_w, m_conv_b, m_lru_wa, m_lru_ba, m_lru_wx, m_lru_bx, m_lru_lambda, m_w_out, m_w_mlp1, m_w_mlp2, m_final_g, v_c_ctx, v_w_ada, v_b_ada, v_norm1_g, v_norm2_g, v_w_in, v_ret_decay, v_conv_w, v_conv_b, v_lru_wa, v_lru_ba, v_lru_wx, v_lru_bx, v_lru_lambda, v_w_out, v_w_mlp1, v_w_mlp2, v_final_g):
    t_len, d = x.shape[1], x.shape[2]
    l_len = ctx.shape[1]
    cb, cc = l_len // TM, l_len // CHUNK
    me = 4 * lax.axis_index("x") + 2 * lax.axis_index("y") + lax.axis_index("c")
    x2d, ctx2d, tgt2d = x[0], ctx[0], loss_target[0]
    ada_cols = w_ada.shape[2]
    wa2d = w_ada[0]

    sc_loc = conv_w.shape[2]
    pack_a = jnp.zeros((8, d), F32)
    pack_a = pack_a.at[0].set(_silu(c[0]))
    pack_a = pack_a.at[1, :4 * sc_loc].set(conv_w[0].reshape(-1))
    pack_a = pack_a.at[2, :2 * sc_loc].set(lru_ba[0].reshape(-1))
    pack_a = pack_a.at[3, :2 * sc_loc].set(lru_bx[0].reshape(-1))
    pack_a = pack_a.at[4, :2 * sc_loc].set(lru_lambda[0].reshape(-1))
    b_cols = lax.dynamic_slice(b_ada, (0, me * ada_cols), (1, ada_cols))
    all_a, s16, mod_parts = _mod_exchange(pack_a, _silu(c_ctx)[None, :], wa2d, b_cols)

    def unshard(row, k):
        return all_a[:, row, :k * sc_loc].reshape(N_DEV, k, sc_loc).transpose(1, 0, 2).reshape(k, N_DEV * sc_loc)

    conv_w_full = unshard(1, 4)
    ba_full, bx_full, lam_full = unshard(2, 2), unshard(3, 2), unshard(4, 2)

    mod_all = mod_parts.transpose(1, 0, 2).reshape(16, N_DEV * ada_cols)
    mod_me = lax.dynamic_slice(mod_all, (me, 0), (1, 6 * d)).reshape(6, d)
    mod_c = mod_all[8].reshape(6, d)
    modrows = jnp.concatenate([mod_c[0:2], mod_me], axis=0)

    lg = jax.nn.log_sigmoid(ret_decay[0])
    sp = jax.nn.softplus(-lam_full)
    wg = [jnp.concatenate([_block_diag(lru_wa[0, dd]), _block_diag(lru_wx[0, dd])], axis=1).astype(BF16) for dd in (0, 1)]
    bg = [jnp.concatenate([ba_full[dd], bx_full[dd]])[None, :] for dd in (0, 1)]
    cosf, sins = _rot_tables(l_len, t_len)

    (hn, hnt), win_g = _norm1_fwd(ctx2d, x2d, norm1_g, modrows, cb, ride=("gather", w_in[0].astype(BF16)))
    (qkv, p), w2_g = _mm_in(hn, win_g, cosf, sins, ride=("gather", w_mlp2[0].astype(BF16)))
    w2_g = w2_g.reshape(1, 4 * d, d)
    (o0, o1, sp0, sp1), w1_g = _ret_fwd(qkv, lg, cc, ride=("gather", w_mlp1[0].astype(BF16)))
    o, s_prev = [o0, o1], [sp0, sp1]
    (h0, cin0), wout_g = _lru_fwd(p, wg[0], bg[0], sp[0:1], conv_w_full, conv_b, 0, cb, ride=("gather", w_out[0].astype(BF16)))
    wout_g = wout_g.reshape(1, d, d)
    (h1, cin1), _ = _lru_fwd(p, wg[1], bg[1], sp[1:2], conv_w_full, conv_b, 1, cb)
    h, cin = [h0, h1], [cin0, cin1]
    mix = _mix_fwd(o[0], o[1], p, h[0], h[1], cb, t_len)
    y, x1, h2, h2t = _mm_nn(mix, wout_g, F32, "mm_out_norm2", tm=TL, epi=_res_norm2_epilogue(x2d, norm2_g, modrows, TL))
    r = _mm_nn(h2, w1_g, BF16, "mm_mlp1", relu_out=True, tm=TL)
    dx2, dz, dzt, facc = _mm_nn(r, w2_g, F32, "mm_mlp2_final", square_lhs=True, tm=TL, vmem_mb=58,
                                epi=_final_epilogue(x1, tgt2d, final_g[None, :], modrows, TL))

    du = _mm_nt(dz, w2_g, BF16, "mm_da2", relu_mul=r, tm=TL, vmem_mb=58)
    gw2_lo, gw2_hi = _mm_wgrad(dzt, r, "mm_dw2", w2_g.shape[1] // N_DEV, BF16, transpose_out=True, square_rhs=True,
                               halves=True)
    gw1_lo, gw1_hi = _mm_wgrad(h2t, du, "mm_dw1", w1_g.shape[2], BF16, halves=True)
    (dx1, dy, n2acc), gw2_lo_all = _mm_nt(du, w1_g, F32, "mm_dh2_norm2", ride=("a2a", gw2_lo), tm=TL, vmem_mb=60,
                                          epi=_bwd_norm2_epilogue(x1, dx2, y, norm2_g, modrows, TL))
    gwo = _mm_tn(mix, dy, "mm_dwout", False, d, BF16, 512).reshape(N_DEV, -1, d)
    (do, dg, dgate, dhs), gwo_all = _mm_nt(dy, wout_g, F32, "mm_dmix_mix", ride=("a2a", gwo),
                                           epi=_mix_bwd_epilogue(o[0], o[1], p, h[0], h[1], cb))
    dxcs, dwgs, laccs, rides, got = [], [], [], [("a2a", gw1_hi), ("a2a", gw2_hi)], []
    (dq0, dk0, dv0, dq1, dk1, dv1, dlg_lanes), gw1_lo_all = _ret_bwd(qkv, lg, do, s_prev, cc, ride=("a2a", gw1_lo))
    dqs, dks, dvs = [dq0, dq1], [dk0, dk1], [dv0, dv1]
    for dd in (0, 1):
        (dxc_, dwg_, lacc_), got_ = _lru_bwd(p, wg[dd], bg[dd], sp[dd:dd + 1], conv_w_full, conv_b, h[dd], cin[dd], dhs, dd, cb,
                                             ride=rides[dd])
        dxcs.append(dxc_); dwgs.append(dwg_); laccs.append(lacc_); got.append(got_)
    gw1_hi_all, gw2_hi_all = got
    gw1_all, gw2_all = [gw1_lo_all, gw1_hi_all], [gw2_lo_all, gw2_hi_all]
    (dp, cacc), _ = _assemble_dp(dqs, dks, dvs, dg, dgate, dxcs, p, conv_w_full, cosf, sins, cb)
    pack_b = jnp.concatenate([n2acc, facc, cacc, laccs[0], laccs[1], dlg_lanes, dwgs[0], dwgs[1]], axis=0)
    gwi, all_b = _mm_wgrad(hnt, dp, "mm_dwin", win_g.shape[2], BF16, ride=("gather", pack_b))
    (grad_x, n1acc), gwi_all = _mm_nt(dp, win_g, F32, "mm_dhn_norm1", ride=("a2a", gwi),
                                      epi=_bwd_norm1_epilogue(ctx2d, x2d, dx1, norm1_g, modrows, cb))
    all_n1 = _all_gather_small(n1acc, "gather_norm1_grads")
    tot = _sum_parts(all_b, "sum_small_grads")
    t_n1 = _sum_parts(all_n1, "sum_norm1_grads")
    t_n2, t_f, t_conv, t_dlg = tot[0:8], tot[8:16], tot[16:24, :LRU_W], tot[40:48, :HEAD_DIM]
    t_l = [tot[24:32], tot[32:40]]
    t_dwd = [tot[48:48 + LRU_BD], tot[48 + LRU_BD:48 + 2 * LRU_BD]]
    loss = (0.5 / d) * jnp.sum(t_f[2])
    t_wa = jnp.stack([_blocks_from_lanes(t_dwd[dd][:, :LRU_W]) for dd in (0, 1)])
    t_wx = jnp.stack([_blocks_from_lanes(t_dwd[dd][:, LRU_W:]) for dd in (0, 1)])
    t_ba = jnp.stack([t_l[dd][0, :LRU_W] for dd in (0, 1)])
    t_bx = jnp.stack([t_l[dd][0, LRU_W:] for dd in (0, 1)])
    t_sp = jnp.stack([t_l[dd][1, :LRU_W] for dd in (0, 1)])
    dm_rows = jnp.stack([all_n1[:, 2, :], all_n1[:, 3, :], all_b[:, 3, :], all_b[:, 0, :], all_b[:, 1, :], all_b[:, 9, :]],
                        axis=1).reshape(N_DEV, 6 * d)
    dm_c = jnp.concatenate([t_n1[0], t_n1[1], jnp.zeros((4 * d,), F32)])
    dm16 = jnp.zeros((16, 6 * d), F32).at[0:8].set(dm_rows).at[8].set(dm_c)
    g_b_ada = jnp.sum(dm16, axis=0)[None, :]
    dm_cols = lax.dynamic_slice(dm16, (0, me * ada_cols), (16, ada_cols))
    g_w_ada, ds16 = _ada_bwd(s16, dm_cols, wa2d)
    ds_all = _all_gather_small(ds16[8:16], "gather_dsilu")
    dsilu_cc = _sum_parts(ds_all, "sum_dsilu")[0]
    sg_cc = jax.nn.sigmoid(c_ctx)
    g_c_ctx = dsilu_cc * (sg_cc * (1.0 + c_ctx * (1.0 - sg_cc)))

    g_ret_decay = jnp.sum(t_dlg, axis=-1).reshape(2, HEADS) * jax.nn.sigmoid(-ret_decay[0])
    g_lambda_full = -t_sp * jax.nn.sigmoid(-lam_full)

    def my_cols(full):
        return lax.dynamic_slice(full, (0, me * sc_loc), (full.shape[0], sc_loc))

    small_g = dict(
        c_ctx=g_c_ctx[None], b_ada=g_b_ada, norm1_g=t_n1[4:5], norm2_g=t_n2[2:3], ret_decay=g_ret_decay,
        conv_w=my_cols(t_conv[0:4]), conv_b=t_conv[4:5], lru_wa=t_wa.reshape(-1, LRU_BD), lru_ba=my_cols(t_ba),
        lru_wx=t_wx.reshape(-1, LRU_BD), lru_bx=my_cols(t_bx), lru_lambda=my_cols(g_lambda_full), final_g=t_f[0:1])
    small = dict(
        c_ctx=(c_ctx, m_c_ctx, v_c_ctx), b_ada=(b_ada, m_b_ada, v_b_ada), norm1_g=(norm1_g, m_norm1_g, v_norm1_g),
        norm2_g=(norm2_g, m_norm2_g, v_norm2_g), ret_decay=(ret_decay, m_ret_decay, v_ret_decay),
        conv_w=(conv_w, m_conv_w, v_conv_w), conv_b=(conv_b, m_conv_b, v_conv_b), lru_wa=(lru_wa, m_lru_wa, v_lru_wa),
        lru_ba=(lru_ba, m_lru_ba, v_lru_ba), lru_wx=(lru_wx, m_lru_wx, v_lru_wx), lru_bx=(lru_bx, m_lru_bx, v_lru_bx),
        lru_lambda=(lru_lambda, m_lru_lambda, v_lru_lambda), final_g=(final_g, m_final_g, v_final_g))
    names = list(small)
    items = [(small_g[k],) + tuple(a.reshape(small_g[k].shape) for a in small[k]) for k in names]
    res = {}
    for k, it, (d_, m_, v_) in zip(names, items, _adam_many(items, "adam_small")):
        shape = small[k][0].shape
        res[k] = tuple(a.reshape(shape) for a in (it[0], d_, m_, v_))

    def big(parts, w, m, v, name):
        out = _sum_adam(parts, w[0], m[0], v[0], name)
        return tuple(a[None] for a in out)

    res["w_ada"] = big([g_w_ada[None]], w_ada, m_w_ada, v_w_ada, "adam_w_ada")
    res["w_in"] = big([gwi_all], w_in, m_w_in, v_w_in, "adam_w_in")
    res["w_out"] = big([gwo_all], w_out, m_w_out, v_w_out, "adam_w_out")
    res["w_mlp1"] = big(gw1_all, w_mlp1, m_w_mlp1, v_w_mlp1, "adam_w_mlp1")
    res["w_mlp2"] = big(gw2_all, w_mlp2, m_w_mlp2, v_w_mlp2, "adam_w_mlp2")

    order = ["c_ctx", "w_ada", "b_ada", "norm1_g", "norm2_g", "w_in", "ret_decay", "conv_w", "conv_b", "lru_wa", "lru_ba",
             "lru_wx", "lru_bx", "lru_lambda", "w_out", "w_mlp1", "w_mlp2", "final_g"]
    outs = [loss, grad_x[None]]
    for j in range(4):
        outs += [res[k][j] for k in order]
    return tuple(outs)
```

```python
import jax
import jax.numpy as jnp
from jax import lax
from jax.experimental import pallas as pl
from jax.experimental.pallas import tpu as pltpu

F32 = jnp.float32
BF16 = jnp.bfloat16
N_DEV = 8
MESH = pl.DeviceIdType.MESH

HEADS = 4
HEAD_DIM = 128
CHUNK = 128
RET_W = HEADS * HEAD_DIM
LRU_W = 512
LRU_BLOCKS = 8
LRU_BD = LRU_W // LRU_BLOCKS
LRU_C = 8.0
EPS = 1e-6
K_SCALE = HEAD_DIM ** -0.5
ROPE_BASE = 10000.0
GRID_W = 64
TM = 256
TL = 512
SUB = 8

ADAM_LR = 0.001
ADAM_B1 = 0.9
ADAM_B2 = 0.999
ADAM_EPS = 1e-08
ADAM_WD = 0.01
ADAM_STEP = 10

COL_G, COL_XR, COL_GATE = 0, 1, 2

R_CSH1, R_CSC1, R_SH1, R_SC1, R_G1, R_SH2, R_SC2, R_G2 = range(8)


def _pcall(body, **kw):
    return pl.pallas_call(body, **kw)


def _cp(vmem_mb=48):
    return pltpu.CompilerParams(vmem_limit_bytes=vmem_mb << 20)


def _sds(shape, dtype):
    return jax.ShapeDtypeStruct(shape, dtype)


def _dot(a, b):
    return jnp.dot(a.astype(BF16), b.astype(BF16), preferred_element_type=F32)


def _dot_nt(a, b):
    return lax.dot_general(a.astype(BF16), b.astype(BF16), (((1,), (1,)), ((), ())), preferred_element_type=F32)


def _dot_tn(a, b):
    return lax.dot_general(a.astype(BF16), b.astype(BF16), (((0,), (0,)), ((), ())), preferred_element_type=F32)


def _sigmoid(x):
    return 0.5 * jnp.tanh(0.5 * x) + 0.5


def _gelu(x):
    return 0.5 * x * (1.0 + jnp.tanh(0.7978845608028654 * (x + 0.044715 * x * x * x)))


def _dgelu(x):
    t = jnp.tanh(0.7978845608028654 * (x + 0.044715 * x * x * x))
    return 0.5 * (1.0 + t) + 0.5 * x * (1.0 - t * t) * 0.7978845608028654 * (1.0 + 3.0 * 0.044715 * x * x)


def _rows_iota(shape):
    return lax.broadcasted_iota(jnp.int32, shape, 0)


def _tile_order(dirn, s, cb, nb):
    if dirn == 0:
        return s
    return jnp.where(s < cb, cb - 1 - s, nb - 1 - (s - cb))


_SEMS = [pltpu.SemaphoreType.DMA((7,)), pltpu.SemaphoreType.DMA((7,)), pltpu.SemaphoreType.DMA(())]
_ANY = pl.BlockSpec(memory_space=pl.ANY)


def _gather_copies(x_ref, out_ref, send_sems, recv_sems, local_sem):
    mx, my, mc = lax.axis_index("x"), lax.axis_index("y"), lax.axis_index("c")
    me, sibling = (mx, my, mc), (mx, my, 1 - mc)
    chips = [(1 - mx, my), (mx, 1 - my), (1 - mx, 1 - my)]

    def slot(px, py, pc):
        return out_ref.at[4 * px + 2 * py + pc]

    def copy(k, block, to, src=None):
        return pltpu.make_async_remote_copy(
            src_ref=slot(*block) if src is None else src, dst_ref=slot(*block),
            send_sem=send_sems.at[k], recv_sem=recv_sems.at[k], device_id=to, device_id_type=MESH)

    mine = pltpu.make_async_copy(x_ref, slot(*me), local_sem)
    first = [copy(0, me, sibling, src=x_ref)] + [copy(1 + j, me, (*chip, mc), src=x_ref) for j, chip in enumerate(chips)]
    passed = [copy(4 + j, (*chip, mc), sibling) for j, chip in enumerate(chips)]
    recv_ici = [copy(1 + j, (*chip, mc), me) for j, chip in enumerate(chips)]
    recv_d2d = [copy(0, sibling, me)] + [copy(4 + j, (*chip, 1 - mc), me) for j, chip in enumerate(chips)]
    return mine, first, passed, recv_ici, recv_d2d


def _gather_start(*refs):
    mine, first, _, _, _ = _gather_copies(*refs)
    mine.start()
    for cp in first:
        cp.start()


def _gather_pass_on(*refs):
    _, _, passed, recv_ici, _ = _gather_copies(*refs)
    for landed, onward in zip(recv_ici, passed):
        landed.wait_recv()
        onward.start()


def _gather_finish(*refs):
    mine, first, passed, _, recv_d2d = _gather_copies(*refs)
    for landed in recv_d2d:
        landed.wait_recv()
    for cp in first + passed:
        cp.wait_send()
    mine.wait()


def _a2a_copies(g_ref, out_ref, send_sems, recv_sems, local_sem):
    mx, my, mc = lax.axis_index("x"), lax.axis_index("y"), lax.axis_index("c")
    me = 4 * mx + 2 * my + mc
    mine = pltpu.make_async_copy(g_ref.at[me], out_ref.at[me], local_sem)
    copies = []
    for k in range(1, N_DEV):
        px = 1 - mx if (k >> 2) & 1 else mx
        py = 1 - my if (k >> 1) & 1 else my
        pc = 1 - mc if k & 1 else mc
        copies.append(pltpu.make_async_remote_copy(
            src_ref=g_ref.at[4 * px + 2 * py + pc], dst_ref=out_ref.at[me],
            send_sem=send_sems.at[k - 1], recv_sem=recv_sems.at[k - 1],
            device_id=(px, py, pc), device_id_type=MESH))
    return mine, copies


def _a2a_start(*refs):
    mine, copies = _a2a_copies(*refs)
    mine.start()
    for cp in copies:
        cp.start()


def _a2a_finish(*refs):
    mine, copies = _a2a_copies(*refs)
    for cp in copies:
        cp.wait()
    mine.wait()


_EXCHANGES = {"gather": (_gather_start, _gather_finish), "a2a": (_a2a_start, _a2a_finish)}
PASS_ON_LEAD = 3


def _exchange_shape(kind, src):
    return _sds((N_DEV,) + src.shape if kind == "gather" else src.shape, src.dtype)


def _all_gather_small(x, name):
    def body(x_ref, out_ref, send_sems, recv_sems, local_sem):
        mx, my, mc = lax.axis_index("x"), lax.axis_index("y"), lax.axis_index("c")
        me = 4 * mx + 2 * my + mc
        mine = pltpu.make_async_copy(x_ref, out_ref.at[me], local_sem)
        mine.start()
        copies = []
        for k in range(1, N_DEV):
            peer = (1 - mx if (k >> 2) & 1 else mx, 1 - my if (k >> 1) & 1 else my, 1 - mc if k & 1 else mc)
            copies.append(pltpu.make_async_remote_copy(
                src_ref=x_ref, dst_ref=out_ref.at[me], send_sem=send_sems.at[k - 1], recv_sem=recv_sems.at[k - 1],
                device_id=peer, device_id_type=MESH))
            copies[-1].start()
        for cp in copies:
            cp.wait()
        mine.wait()

    return _pcall(body, name=name, out_shape=_exchange_shape("gather", x), in_specs=[_ANY], out_specs=_ANY,
                  scratch_shapes=list(_SEMS))(x)


def _pcall_ride(body, ride, args, *, name, grid, in_specs, out_specs, out_shape, scratch_shapes=(), compiler_params=None):
    if ride is None:
        out = _pcall(body, name=name, grid=grid, in_specs=in_specs, out_specs=out_specs, out_shape=out_shape,
                     scratch_shapes=list(scratch_shapes), compiler_params=compiler_params)(*args)
        return out, None
    kind, src = ride
    start, finish = _EXCHANGES[kind]
    single = not isinstance(out_shape, (tuple, list))
    out_specs_t = (out_specs,) if single else tuple(out_specs)
    out_shape_t = (out_shape,) if single else tuple(out_shape)
    n_in, n_out, n_sc = len(in_specs), len(out_shape_t), len(scratch_shapes)

    def wrapped(*refs):
        ins, src_ref = refs[:n_in], refs[n_in]
        outs, dst_ref = refs[n_in + 1:n_in + 1 + n_out], refs[n_in + 1 + n_out]
        scratch = refs[n_in + 2 + n_out:n_in + 2 + n_out + n_sc]
        sems = refs[n_in + 2 + n_out + n_sc:]
        first = pl.program_id(0) == 0
        last = pl.program_id(0) == grid[0] - 1
        for ax in range(1, len(grid)):
            first = jnp.logical_and(first, pl.program_id(ax) == 0)
            last = jnp.logical_and(last, pl.program_id(ax) == grid[ax] - 1)

        @pl.when(first)
        def _():
            start(src_ref, dst_ref, *sems)

        body(*ins, *outs, *scratch)

        if kind == "gather":
            @pl.when(pl.program_id(0) == max(grid[0] - PASS_ON_LEAD, 0))
            def _():
                _gather_pass_on(src_ref, dst_ref, *sems)

        @pl.when(last)
        def _():
            finish(src_ref, dst_ref, *sems)

    res = _pcall(wrapped, name=name, grid=grid, in_specs=list(in_specs) + [_ANY], out_specs=out_specs_t + (_ANY,),
                 out_shape=out_shape_t + (_exchange_shape(kind, src),),
                 scratch_shapes=list(scratch_shapes) + list(_SEMS), compiler_params=compiler_params)(*args, src)
    return (res[0] if single else tuple(res[:-1])), res[-1]


class _Epilogue:
    def __init__(self, fn, args, in_specs, out_specs, out_shape, steps=None, lhs_map=None, delay=0):
        self.fn, self.args, self.in_specs, self.out_specs, self.out_shape = fn, tuple(args), list(in_specs), out_specs, out_shape
        self.steps, self.lhs_map, self.delay = steps, lhs_map, delay


def _mm_nn(a, w, out_dtype, name, square_lhs=False, relu_out=False, ride=None, tm=TM, epi=None, vmem_mb=48):
    m, k = a.shape
    nb, _, bn = w.shape
    tm = min(tm, m)
    delay = 0 if epi is None else epi.delay
    steps = m // tm

    def body(*refs):
        a_ref, w_ref = refs[:2]
        av = a_ref[...]
        if square_lhs:
            av = av * av
        if delay:
            assert nb == 1
            held = refs[-1]

            @pl.when(pl.program_id(0) == 0)
            def _():
                held[...] = jnp.zeros_like(held)

            fresh = jnp.dot(av, w_ref[0], preferred_element_type=F32)
            epi.fn(held[...], slice(0, tm), *refs[2:-1])
            held[...] = fresh
            return
        if epi is not None:
            assert nb == 1
            epi.fn(jnp.dot(av, w_ref[0], preferred_element_type=F32), slice(0, tm), *refs[2:])
            return
        for j in range(nb):
            r = jnp.dot(av, w_ref[j], preferred_element_type=F32)
            if relu_out:
                r = jnp.maximum(r, 0.0)
            refs[2][:, j * bn:(j + 1) * bn] = r.astype(out_dtype)

    lhs_map = (lambda i: (i, 0)) if not delay else (lambda i: (jnp.minimum(i, steps - 1), 0))
    in_specs = [pl.BlockSpec((tm, k), lhs_map), pl.BlockSpec((nb, k, bn), lambda i: (0, 0, 0))]
    scratch = []
    if epi is None:
        args, out_specs, out_shape = (a, w), pl.BlockSpec((tm, nb * bn), lambda i: (i, 0)), _sds((m, nb * bn), out_dtype)
    else:
        args, out_specs, out_shape = (a, w) + epi.args, epi.out_specs, epi.out_shape
        in_specs += epi.in_specs
        if delay:
            scratch.append(pltpu.VMEM((tm, bn), F32))
    out, ex = _pcall_ride(body, ride, args, name=name, grid=(steps + delay,), in_specs=in_specs, out_specs=out_specs,
                          out_shape=out_shape, scratch_shapes=scratch, compiler_params=_cp(vmem_mb))
    return out if ride is None else (out, ex)


def _mm_nt(dy, w, out_dtype, name, relu_mul=None, ride=None, tm=TM, epi=None, vmem_mb=48):
    m = dy.shape[0]
    nb, k, bn = w.shape
    tm = min(tm, m)
    n_extra = (0 if relu_mul is None else 1) + (0 if epi is None else len(epi.args))

    delay = 0 if epi is None else epi.delay

    def body(*refs):
        dy_ref, w_ref = refs[:2]
        scratch = refs[-1 - delay:]
        extra, outs, wt = refs[2:2 + n_extra], refs[2 + n_extra:-1 - delay], scratch[0]

        @pl.when(pl.program_id(0) == 0)
        def _():
            for j in range(nb):
                wt[j * bn:(j + 1) * bn, :] = w_ref[j].T
            if delay:
                scratch[1][...] = jnp.zeros_like(scratch[1])

        if delay:
            held = scratch[1]
            fresh = jnp.dot(dy_ref[...], wt[...], preferred_element_type=F32)
            epi.fn(held[...], slice(0, tm), *extra, *outs)
            held[...] = fresh
            return
        acc = jnp.dot(dy_ref[...], wt[...], preferred_element_type=F32)
        if epi is not None:
            epi.fn(acc, slice(0, tm), *extra, *outs)
            return
        if relu_mul is not None:
            acc = acc * (2.0 * extra[0][...].astype(F32))
        outs[0][...] = acc.astype(out_dtype)

    steps = m // tm if epi is None or epi.steps is None else epi.steps
    lhs_tile = (lambda i: (i, 0)) if epi is None or epi.lhs_map is None else epi.lhs_map
    lhs_map = lhs_tile if not delay else (lambda i: lhs_tile(jnp.minimum(i, steps - 1)))
    in_specs = [pl.BlockSpec((tm, nb * bn), lhs_map), pl.BlockSpec((nb, k, bn), lambda i: (0, 0, 0))]
    args = [dy, w]
    if relu_mul is not None:
        in_specs.append(pl.BlockSpec((tm, k), lambda i: (i, 0)))
        args.append(relu_mul)
    scratch = [pltpu.VMEM((nb * bn, k), BF16)]
    if epi is None:
        out_specs, out_shape = pl.BlockSpec((tm, k), lambda i: (i, 0)), _sds((m, k), out_dtype)
    else:
        args += list(epi.args)
        in_specs += epi.in_specs
        out_specs, out_shape = epi.out_specs, epi.out_shape
        if delay:
            scratch.append(pltpu.VMEM((tm, k), F32))
    out, ex = _pcall_ride(
        body, ride, args, name=name, grid=(steps + delay,), in_specs=in_specs, out_specs=out_specs, out_shape=out_shape,
        scratch_shapes=scratch, compiler_params=_cp(vmem_mb))
    return out if ride is None else (out, ex)


def _mm_tn(a, b, name, col_blocks, block, out_dtype, tm, square_lhs=False):
    m, k = a.shape
    nn = b.shape[1]
    steps = m // tm
    if col_blocks:
        nblk, acc_shape = nn // block, (k, block)
        a_spec = pl.BlockSpec((tm, k), lambda j, s: (s, 0))
        b_spec = pl.BlockSpec((tm, block), lambda j, s: (s, j))
    else:
        nblk, acc_shape = k // block, (block, nn)
        a_spec = pl.BlockSpec((tm, block), lambda j, s: (s, j))
        b_spec = pl.BlockSpec((tm, nn), lambda j, s: (s, 0))

    def body(a_ref, b_ref, o_ref, acc):
        s = pl.program_id(1)

        @pl.when(s == 0)
        def _():
            acc[...] = jnp.zeros_like(acc)

        av = a_ref[...]
        if square_lhs:
            av = av.astype(F32)
            av = (av * av).astype(BF16)
        acc[...] += _dot_tn(av, b_ref[...])

        @pl.when(s == steps - 1)
        def _():
            o_ref[...] = acc[...].astype(out_dtype)

    return _pcall(
        body, name=name, grid=(nblk, steps), in_specs=[a_spec, b_spec],
        out_specs=pl.BlockSpec((None,) + acc_shape, lambda j, s: (j, 0, 0)),
        out_shape=_sds((nblk,) + acc_shape, out_dtype),
        scratch_shapes=[pltpu.VMEM(acc_shape, F32)], compiler_params=_cp(),
    )(a, b)


def _mm_wgrad(at, b, name, bn, out_dtype, transpose_out=False, square_rhs=False, halves=False, ride=None):
    k, m = at.shape
    nblk = b.shape[1] // bn
    rows, cols = (bn, k) if transpose_out else (k, bn)
    nout = 2 if halves else 1
    per = rows // nout

    def body(a_ref, b_ref, *o_refs):
        bv = b_ref[...]
        if square_rhs:
            bv = bv * bv
        r = jnp.dot(a_ref[...], bv, preferred_element_type=F32)
        r = (r.T if transpose_out else r).astype(out_dtype)
        for i, o_ref in enumerate(o_refs):
            o_ref[...] = r[i * per:(i + 1) * per, :]

    out, ex = _pcall_ride(
        body, ride, (at, b), name=name, grid=(nblk,),
        in_specs=[pl.BlockSpec((k, m), lambda j: (0, 0)), pl.BlockSpec((m, bn), lambda j: (0, j))],
        out_specs=tuple(pl.BlockSpec((None, per, cols), lambda j: (j, 0, 0)) for _ in range(nout)),
        out_shape=tuple(_sds((nblk, per, cols), out_dtype) for _ in range(nout)), compiler_params=_cp())
    out = out if halves else out[0]
    return out if ride is None else (out, ex)


def _mm_in(hn, w, cosf, sins, ride):
    m, k = hn.shape
    nb, _, bn = w.shape

    def body(a_ref, w_ref, c_ref, s_ref, qkv_ref, rest_ref, pt):
        av = a_ref[...]
        for j in range(nb):
            pt[:, j * bn:(j + 1) * bn] = jnp.dot(av, w_ref[j], preferred_element_type=F32)
        cf, ss = c_ref[...], s_ref[...]
        for h in range(HEADS):
            sq = slice(h * HEAD_DIM, (h + 1) * HEAD_DIM)
            sk = slice(RET_W + h * HEAD_DIM, RET_W + (h + 1) * HEAD_DIM)
            qkv_ref[:, sq] = _rot(pt[:, sq], cf, ss).astype(BF16)
            qkv_ref[:, sk] = (_rot(pt[:, sk], cf, ss) * K_SCALE).astype(BF16)
        qkv_ref[:, 2 * RET_W:] = pt[:, 2 * RET_W:3 * RET_W].astype(BF16)
        rest_ref[...] = pt[:, 3 * RET_W:]

    tab = pl.BlockSpec((TM, HEAD_DIM), lambda i: (i, 0))
    wide = pl.BlockSpec((TM, 3 * RET_W), lambda i: (i, 0))
    return _pcall_ride(
        body, ride, (hn, w, cosf, sins), name="mm_in", grid=(m // TM,),
        in_specs=[pl.BlockSpec((TM, k), lambda i: (i, 0)), pl.BlockSpec((nb, k, bn), lambda i: (0, 0, 0)), tab, tab],
        out_specs=(wide, wide), out_shape=(_sds((m, 3 * RET_W), BF16), _sds((m, nb * bn - 3 * RET_W), F32)),
        scratch_shapes=[pltpu.VMEM((TM, nb * bn), F32)], compiler_params=_cp())


def _mod_exchange(pack_a, silu_cc, w_ada, b_cols):
    d, cols = pack_a.shape[1], w_ada.shape[1]

    def body(pa_ref, scc_ref, w_ref, b_ref, alla_ref, s16_ref, modp_ref, mp, send1, recv1, send2, recv2):
        mx, my, mc = lax.axis_index("x"), lax.axis_index("y"), lax.axis_index("c")
        me = 4 * mx + 2 * my + mc

        def exchange(src_ref, dst_ref, send, recv):
            copies = []
            for k in range(1, N_DEV):
                peer = (1 - mx if (k >> 2) & 1 else mx, 1 - my if (k >> 1) & 1 else my, 1 - mc if k & 1 else mc)
                copies.append(pltpu.make_async_remote_copy(
                    src_ref=src_ref, dst_ref=dst_ref.at[me], send_sem=send.at[k - 1], recv_sem=recv.at[k - 1],
                    device_id=peer, device_id_type=MESH))
                copies[-1].start()
            dst_ref[me] = src_ref[...]
            for cp in copies:
                cp.wait()

        exchange(pa_ref, alla_ref, send1, recv1)
        s16_ref[...] = jnp.zeros_like(s16_ref)
        for j in range(N_DEV):
            s16_ref[j:j + 1, :] = alla_ref[j, 0:1, :]
        s16_ref[N_DEV:N_DEV + 1, :] = scc_ref[...]
        mp[...] = _dot(s16_ref[...], w_ref[...]) + b_ref[...]
        exchange(mp, modp_ref, send2, recv2)

    vmem = pl.BlockSpec(memory_space=pltpu.VMEM)
    sem7 = pltpu.SemaphoreType.DMA((N_DEV - 1,))
    return _pcall(
        body, name="mod_exchange", in_specs=[vmem] * 4, out_specs=(vmem,) * 3,
        out_shape=(_sds((N_DEV,) + pack_a.shape, F32), _sds((2 * N_DEV, d), F32), _sds((N_DEV, 2 * N_DEV, cols), F32)),
        scratch_shapes=[pltpu.VMEM((2 * N_DEV, cols), F32), sem7, sem7, sem7, sem7], compiler_params=_cp(),
    )(pack_a, silu_cc, w_ada, b_cols)


def _ada_bwd(s16, dm_cols, w_ada):
    def body(s_ref, d_ref, w_ref, gw_ref, ds_ref):
        gw_ref[...] = _dot_tn(s_ref[...], d_ref[...])
        ds_ref[...] = _dot_nt(d_ref[...], w_ref[...])

    return _pcall(body, name="ada_bwd",
                  out_shape=(_sds(w_ada.shape, F32), _sds(s16.shape, F32)), compiler_params=_cp())(s16, dm_cols, w_ada)


def _norm1_fwd(ctx, x, g, modrows, cb, ride=None):
    l_len, d = ctx.shape
    nb = (l_len + x.shape[0]) // TM

    def body(ctx_ref, x_ref, g_ref, m_ref, o_ref, ot_ref):
        is_ctx = pl.program_id(0) < cb
        xin = jnp.where(is_ctx, ctx_ref[...], x_ref[...])
        sh = jnp.where(is_ctx, m_ref[R_CSH1:R_CSH1 + 1, :], m_ref[R_SH1:R_SH1 + 1, :])
        sc = jnp.where(is_ctx, m_ref[R_CSC1:R_CSC1 + 1, :], m_ref[R_SC1:R_SC1 + 1, :])
        ms = jnp.mean(xin * xin, axis=-1, keepdims=True)
        n = xin * lax.rsqrt(ms + EPS) * g_ref[...]
        hn = n * (1.0 + sc) + sh
        o_ref[...] = hn.astype(BF16)
        ot_ref[...] = hn.T.astype(BF16)

    return _pcall_ride(
        body, ride, (ctx, x, g, modrows), name="norm1_fwd", grid=(nb,),
        in_specs=[pl.BlockSpec((TM, d), lambda i: (jnp.minimum(i, cb - 1), 0)),
                  pl.BlockSpec((TM, d), lambda i: (jnp.maximum(i - cb, 0), 0)),
                  pl.BlockSpec((1, d), lambda i: (0, 0)), pl.BlockSpec((8, d), lambda i: (0, 0))],
        out_specs=(pl.BlockSpec((TM, d), lambda i: (i, 0)), pl.BlockSpec((d, TM), lambda i: (0, i))),
        out_shape=(_sds((nb * TM, d), BF16), _sds((d, nb * TM), BF16)), compiler_params=_cp())


def _mix_fwd(o_f, o_b, p, h_f, h_b, cb, t_len):
    def body(of_ref, ob_ref, g_ref, gate_ref, hf_ref, hb_ref, mix_ref):
        o = of_ref[...] + ob_ref[...]
        g = g_ref[...]
        sg = g * _sigmoid(g)
        for hh in range(HEADS):
            sl = slice(hh * HEAD_DIM, (hh + 1) * HEAD_DIM)
            oh = o[:, sl]
            yc = oh - jnp.mean(oh, axis=-1, keepdims=True)
            var = jnp.mean(yc * yc, axis=-1, keepdims=True)
            mix_ref[:, sl] = (sg[:, sl] * (yc * lax.rsqrt(var + EPS))).astype(BF16)
        mix_ref[:, RET_W:] = ((hf_ref[...] + hb_ref[...]) * _gelu(gate_ref[...])).astype(BF16)

    row = lambda i: (i + cb, 0)
    return _pcall(
        body, name="mix_fwd", grid=(t_len // TM,),
        in_specs=[pl.BlockSpec((TM, RET_W), row), pl.BlockSpec((TM, RET_W), row),
                  pl.BlockSpec((TM, RET_W), lambda i: (i + cb, COL_G)), pl.BlockSpec((TM, LRU_W), lambda i: (i + cb, COL_GATE)),
                  pl.BlockSpec((TM, LRU_W), row), pl.BlockSpec((TM, LRU_W), row)],
        out_specs=pl.BlockSpec((TM, RET_W + LRU_W), lambda i: (i, 0)),
        out_shape=_sds((t_len, RET_W + LRU_W), BF16), compiler_params=_cp(),
    )(o_f, o_b, p, p, h_f, h_b)


def _res_norm2_epilogue(x, g, modrows, tm):
    t_len, d = x.shape
    tm = min(tm, t_len)

    def fn(y, rows, x_ref, g_ref, m_ref, y_ref, x1_ref, h2_ref, h2t_ref):
        y_ref[rows, :] = y
        x1 = x_ref[rows, :] + m_ref[R_G1:R_G1 + 1, :] * y
        ms = jnp.mean(x1 * x1, axis=-1, keepdims=True)
        n = x1 * lax.rsqrt(ms + EPS) * g_ref[...]
        x1_ref[rows, :] = x1
        h2 = n * (1.0 + m_ref[R_SC2:R_SC2 + 1, :]) + m_ref[R_SH2:R_SH2 + 1, :]
        h2_ref[rows, :] = h2.astype(BF16)
        h2t_ref[:, rows] = h2.T.astype(BF16)

    t = pl.BlockSpec((tm, d), lambda i: (i, 0))
    return _Epilogue(
        fn, (x, g, modrows),
        in_specs=[t, pl.BlockSpec((1, d), lambda i: (0, 0)), pl.BlockSpec((8, d), lambda i: (0, 0))],
        out_specs=(t, t, t, pl.BlockSpec((d, tm), lambda i: (0, i))),
        out_shape=(_sds((t_len, d), F32), _sds((t_len, d), F32), _sds((t_len, d), BF16), _sds((d, t_len), BF16)))


def _tile(i):
    return jnp.maximum(i - 1, 0)


def _zero_at_start(acc_ref, rows):
    @pl.when(pl.program_id(0) == 0)
    def _():
        acc_ref[...] = jnp.zeros_like(acc_ref)


def _final_epilogue(x1, target, fg, modrows, tm):
    t_len, d = x1.shape

    def fn(z, rows, x1_ref, t_ref, fg_ref, m_ref, dx2_ref, dz_ref, dzt_ref, acc_ref):
        _zero_at_start(acc_ref, rows)
        g2 = m_ref[R_G2:R_G2 + 1, :]
        x2 = x1_ref[rows, :] + g2 * z
        rstd = lax.rsqrt(jnp.mean(x2 * x2, axis=-1, keepdims=True) + EPS)
        xh = x2 * rstd
        fg = fg_ref[...]
        e = xh * fg - t_ref[rows, :]
        dy = e * (1.0 / d)
        dxh = dy * fg
        dx2 = rstd * (dxh - xh * jnp.mean(dxh * xh, axis=-1, keepdims=True))
        dx2_ref[rows, :] = dx2
        dz = g2 * dx2
        dz_ref[rows, :] = dz.astype(BF16)
        dzt_ref[:, rows] = dz.T.astype(BF16)
        acc_ref[0:1, :] += jnp.sum(dy * xh, axis=0, keepdims=True)
        acc_ref[1:2, :] += jnp.sum(dx2 * z, axis=0, keepdims=True)
        acc_ref[2:3, :] += jnp.sum(e * e, axis=0, keepdims=True)

    t = pl.BlockSpec((tm, d), lambda i: (i, 0))
    return _Epilogue(
        fn, (x1, target, fg, modrows),
        in_specs=[t, t, pl.BlockSpec((1, d), lambda i: (0, 0)), pl.BlockSpec((8, d), lambda i: (0, 0))],
        out_specs=(t, t, pl.BlockSpec((d, tm), lambda i: (0, i)), pl.BlockSpec((8, d), lambda i: (0, 0))),
        out_shape=(_sds((t_len, d), F32), _sds((t_len, d), BF16), _sds((d, t_len), BF16), _sds((8, d), F32)))


def _bwd_norm2_epilogue(x1, dx2, y, g, modrows, tm):
    t_len, d = x1.shape

    def fn(dh2, rows, x1_ref, dx2_ref, y_ref, g_ref, m_ref, dx1_ref, dy_ref, acc_ref):
        _zero_at_start(acc_ref, rows)
        x1 = x1_ref[rows, :]
        rstd = lax.rsqrt(jnp.mean(x1 * x1, axis=-1, keepdims=True) + EPS)
        xh = x1 * rstd
        gn = g_ref[...]
        dn = dh2 * (1.0 + m_ref[R_SC2:R_SC2 + 1, :])
        dxh = dn * gn
        dx1 = dx2_ref[rows, :] + rstd * (dxh - xh * jnp.mean(dxh * xh, axis=-1, keepdims=True))
        dx1_ref[rows, :] = dx1
        dy_ref[rows, :] = (m_ref[R_G1:R_G1 + 1, :] * dx1).astype(BF16)
        acc_ref[0:1, :] += jnp.sum(dh2, axis=0, keepdims=True)
        acc_ref[1:2, :] += jnp.sum(dh2 * xh * gn, axis=0, keepdims=True)
        acc_ref[2:3, :] += jnp.sum(dn * xh, axis=0, keepdims=True)
        acc_ref[3:4, :] += jnp.sum(dx1 * y_ref[rows, :], axis=0, keepdims=True)

    t = pl.BlockSpec((tm, d), lambda i: (i, 0))
    return _Epilogue(
        fn, (x1, dx2, y, g, modrows),
        in_specs=[t, t, t, pl.BlockSpec((1, d), lambda i: (0, 0)), pl.BlockSpec((8, d), lambda i: (0, 0))],
        out_specs=(t, t, pl.BlockSpec((8, d), lambda i: (0, 0))),
        out_shape=(_sds((t_len, d), F32), _sds((t_len, d), BF16), _sds((8, d), F32)))


def _mix_bwd_epilogue(o_f, o_b, p, h_f, h_b, cb):
    n = o_f.shape[0]
    tile = _tile

    def fn(dm, rows, of_ref, ob_ref, g_ref, gate_ref, hf_ref, hb_ref, do_ref, dg_ref, dgate_ref, dhs_ref):
        keep = jnp.where(pl.program_id(0) - 1 < cb, 0.0, 1.0)
        dm = dm * keep
        o = of_ref[rows, :] + ob_ref[rows, :]
        g = g_ref[rows, :]
        s = _sigmoid(g)
        sg = g * s
        dsg = s * (1.0 + g * (1.0 - s))
        for hh in range(HEADS):
            sl = slice(hh * HEAD_DIM, (hh + 1) * HEAD_DIM)
            oh = o[:, sl]
            yc = oh - jnp.mean(oh, axis=-1, keepdims=True)
            rs = lax.rsqrt(jnp.mean(yc * yc, axis=-1, keepdims=True) + EPS)
            gn = yc * rs
            dret = dm[:, sl]
            dgn = dret * sg[:, sl]
            dg_ref[rows, sl] = (dret * gn * dsg[:, sl]).astype(BF16)
            do_ref[rows, sl] = (rs * (dgn - jnp.mean(dgn, axis=-1, keepdims=True)
                                      - gn * jnp.mean(dgn * gn, axis=-1, keepdims=True))).astype(BF16)
        dlru = dm[:, RET_W:]
        gate = gate_ref[rows, :]
        dhs_ref[rows, :] = dlru * _gelu(gate)
        dgate_ref[rows, :] = (dlru * (hf_ref[rows, :] + hb_ref[rows, :]) * _dgelu(gate)).astype(BF16)

    t = pl.BlockSpec((TM, RET_W), lambda i: (tile(i), 0))
    return _Epilogue(
        fn, (o_f, o_b, p, p, h_f, h_b),
        in_specs=[t, t, pl.BlockSpec((TM, RET_W), lambda i: (tile(i), COL_G)),
                  pl.BlockSpec((TM, LRU_W), lambda i: (tile(i), COL_GATE)), t, t],
        out_specs=(t, t, t, t), out_shape=(_sds((n, RET_W), BF16),) * 3 + (_sds((n, RET_W), F32),),
        steps=n // TM, lhs_map=lambda i: (jnp.maximum(i - cb, 0), 0), delay=1)


def _bwd_norm1_epilogue(ctx, x, dx1, g, modrows, cb):
    t_len, d = x.shape

    def fn(dh, rows, ctx_ref, x_ref, dx1_ref, g_ref, m_ref, gx_ref, acc_ref):
        is_ctx = pl.program_id(0) < cb
        _zero_at_start(acc_ref, rows)
        xin = jnp.where(is_ctx, ctx_ref[rows, :], x_ref[rows, :])
        sc = jnp.where(is_ctx, m_ref[R_CSC1:R_CSC1 + 1, :], m_ref[R_SC1:R_SC1 + 1, :])
        rstd = lax.rsqrt(jnp.mean(xin * xin, axis=-1, keepdims=True) + EPS)
        xh = xin * rstd
        gn = g_ref[...]
        dn = dh * (1.0 + sc)
        dxh = dn * gn
        gx_ref[rows, :] = dx1_ref[rows, :] + rstd * (dxh - xh * jnp.mean(dxh * xh, axis=-1, keepdims=True))
        s0 = jnp.sum(dh, axis=0, keepdims=True)
        s1 = jnp.sum(dh * xh * gn, axis=0, keepdims=True)
        acc_ref[4:5, :] += jnp.sum(dn * xh, axis=0, keepdims=True)
        acc_ref[0:1, :] += jnp.where(is_ctx, s0, 0.0)
        acc_ref[1:2, :] += jnp.where(is_ctx, s1, 0.0)
        acc_ref[2:3, :] += jnp.where(is_ctx, 0.0, s0)
        acc_ref[3:4, :] += jnp.where(is_ctx, 0.0, s1)

    lat = pl.BlockSpec((TM, d), lambda i: (jnp.maximum(i - cb, 0), 0))
    return _Epilogue(
        fn, (ctx, x, dx1, g, modrows),
        in_specs=[pl.BlockSpec((TM, d), lambda i: (jnp.minimum(i, cb - 1), 0)),
                  lat, lat, pl.BlockSpec((1, d), lambda i: (0, 0)), pl.BlockSpec((8, d), lambda i: (0, 0))],
        out_specs=(lat, pl.BlockSpec((8, d), lambda i: (0, 0))),
        out_shape=(_sds((t_len, d), F32), _sds((8, d), F32)))


def _rot(x, cf, ss):
    return x * cf + pltpu.roll(x, HEAD_DIM // 2, 1) * ss


def _decay_exponents(dirn):
    ii = lax.broadcasted_iota(jnp.int32, (CHUNK, CHUNK), 0)
    jj = lax.broadcasted_iota(jnp.int32, (CHUNK, CHUNK), 1)
    rel = ii - jj if dirn == 0 else jj - ii
    pos = ii.astype(F32)
    if dirn == 0:
        cq, cs = pos + 1.0, (CHUNK - 1.0) - pos
    else:
        cq, cs = CHUNK - pos, pos
    return rel, jnp.maximum(rel, 0).astype(F32), cq, cs


def _store_decay(lg_ref, dec, relf_ref=None):
    for dirn in (0, 1):
        rel, relf, cq, cs = _decay_exponents(dirn)
        if relf_ref is not None:
            relf_ref[dirn] = relf
        for h in range(HEADS):
            lgv = lg_ref[dirn, h]
            wq, ws = jnp.exp(lgv * cq), jnp.exp(lgv * cs)
            dec[dirn, h, 0] = jnp.where(rel >= 0, jnp.exp(lgv * relf), 0.0)
            dec[dirn, h, 1] = wq
            dec[dirn, h, 2] = ws
            if relf_ref is not None:
                dec[dirn, h, 3] = wq * cq
                dec[dirn, h, 4] = ws * cs


def _ret_rows(cc, nc, step_of):
    return [lambda s, dirn=dirn: _tile_order(dirn, step_of(s), cc, nc) for dirn in (0, 1)]


def _ret_in_specs(rows):
    specs = []
    for row in rows:
        specs += [pl.BlockSpec((CHUNK, RET_W), lambda s, o=o, row=row: (row(s), o)) for o in (0, 1, 2)]
    return specs


def _ret_fwd(qkv, lg, cc, ride=None):
    n = qkv.shape[0]
    nc = n // CHUNK
    rows = _ret_rows(cc, nc, lambda s: s)

    def body(lg_ref, q0, k0, v0, q1, k1, v1, o0, o1, sp0, sp1, st, dec):
        @pl.when(pl.program_id(0) == 0)
        def _():
            st[...] = jnp.zeros_like(st)
            _store_decay(lg_ref, dec)

        refs = ((q0, k0, v0, o0, sp0), (q1, k1, v1, o1, sp1))
        chains = [(dirn, h, slice(h * HEAD_DIM, (h + 1) * HEAD_DIM)) for dirn in (0, 1) for h in range(HEADS)]
        scores, cross, update = [], [], []
        for dirn, h, sl in chains:
            q_ref, k_ref, v_ref, _, sp_ref = refs[dirn]
            q, k, v = q_ref[:, sl], k_ref[:, sl], v_ref[:, sl]
            sp = st[dirn, h]
            sp_ref[h] = sp
            scores.append(_dot_nt(q, k))
            cross.append(_dot(q * dec[dirn, h, 1], sp))
            update.append(_dot_tn(k * dec[dirn, h, 2], v))
        masked = [(a * dec[dirn, h, 0]).astype(BF16) for a, (dirn, h, _) in zip(scores, chains)]
        intra = [_dot(sc, refs[dirn][2][:, sl]) for sc, (dirn, h, sl) in zip(masked, chains)]
        for (dirn, h, sl), o_in, o_cr, upd in zip(chains, intra, cross, update):
            refs[dirn][3][:, sl] = o_in + o_cr
            st[dirn, h] = jnp.exp(lg_ref[dirn, h] * CHUNK) * st[dirn, h] + upd

    o_specs = [pl.BlockSpec((CHUNK, RET_W), lambda s, row=row: (row(s), 0)) for row in rows]
    state = pl.BlockSpec((None, HEADS, CHUNK, HEAD_DIM), lambda s: (s, 0, 0, 0))
    return _pcall_ride(
        body, ride, (lg,) + (qkv,) * 6, name="ret_fwd", grid=(nc,),
        in_specs=[pl.BlockSpec(memory_space=pltpu.SMEM)] + _ret_in_specs(rows),
        out_specs=(o_specs[0], o_specs[1], state, state),
        out_shape=(_sds((n, RET_W), F32),) * 2 + (_sds((nc, HEADS, CHUNK, HEAD_DIM), F32),) * 2,
        scratch_shapes=[pltpu.VMEM((2, HEADS, CHUNK, HEAD_DIM), F32), pltpu.VMEM((2, HEADS, 3, CHUNK, CHUNK), F32)],
        compiler_params=_cp())


def _ret_bwd(qkv, lg, do, s_prev, cc, ride=None):
    n = qkv.shape[0]
    nc = n // CHUNK
    rows = _ret_rows(cc, nc, lambda s: nc - 1 - s)

    def body(lg_ref, q0, k0, v0, q1, k1, v1, do0, do1, sp0, sp1, dq0, dk0, dv0, dq1, dk1, dv1, dlg_ref, dst, dec, relf):
        @pl.when(pl.program_id(0) == 0)
        def _():
            dst[...] = jnp.zeros_like(dst)
            dlg_ref[...] = jnp.zeros_like(dlg_ref)
            _store_decay(lg_ref, dec, relf)

        refs = ((q0, k0, v0, do0, sp0, dq0, dk0, dv0), (q1, k1, v1, do1, sp1, dq1, dk1, dv1))
        chains = [(dirn, h, slice(h * HEAD_DIM, (h + 1) * HEAD_DIM)) for dirn in (0, 1) for h in range(HEADS)]

        def tiles(dirn, sl):
            q_ref, k_ref, v_ref, do_ref = refs[dirn][:4]
            return q_ref[:, sl], k_ref[:, sl], v_ref[:, sl], do_ref[:, sl]

        a_s, g1_s, da_s, h1_s = [], [], [], []
        for dirn, h, sl in chains:
            q, k, v, dov = tiles(dirn, sl)
            a_s.append(_dot_nt(q, k))
            g1_s.append(_dot_nt(dov, refs[dirn][4][h]))
            da_s.append(_dot_nt(dov, v))
            h1_s.append(_dot_nt(v, dst[dirn, h]))
        da_s = [da * dec[dirn, h, 0] for da, (dirn, h, _) in zip(da_s, chains)]
        dq_s, dk_s, dv_s, ds_s = [], [], [], []
        for (dirn, h, sl), a, da in zip(chains, a_s, da_s):
            q, k, v, dov = tiles(dirn, sl)
            dq_s.append(_dot(da, k))
            dk_s.append(_dot_tn(da, q))
            dv_s.append(_dot_tn(a * dec[dirn, h, 0], dov) + _dot(k * dec[dirn, h, 2], dst[dirn, h]))
            ds_s.append(_dot_tn(q * dec[dirn, h, 1], dov))
        for (dirn, h, sl), a, g1, da, h1, dq2, dk2, dv, ds2 in zip(chains, a_s, g1_s, da_s, h1_s, dq_s, dk_s, dv_s, ds_s):
            q, k, _, _ = tiles(dirn, sl)
            dq_ref, dk_ref, dv_ref = refs[dirn][5:]
            sp, dsn = refs[dirn][4][h], dst[dirn, h]
            gc = jnp.exp(lg_ref[dirn, h] * CHUNK)
            dq_ref[:, sl] = (g1 * dec[dirn, h, 1] + dq2).astype(BF16)
            dk_ref[:, sl] = (dk2 + h1 * dec[dirn, h, 2]).astype(BF16)
            dv_ref[:, sl] = dv.astype(BF16)
            term = da * a * relf[dirn] + q * g1 * dec[dirn, h, 3] + k * h1 * dec[dirn, h, 4] + sp * dsn * (CHUNK * gc)
            dlg_ref[dirn * HEADS + h:dirn * HEADS + h + 1, 0:HEAD_DIM] += jnp.sum(term, axis=0, keepdims=True)
            dst[dirn, h] = gc * dsn + ds2

    wide = [pl.BlockSpec((CHUNK, RET_W), lambda s, row=row: (row(s), 0)) for row in rows]
    state = pl.BlockSpec((None, HEADS, CHUNK, HEAD_DIM), lambda s: (nc - 1 - s, 0, 0, 0))
    return _pcall_ride(
        body, ride, (lg,) + (qkv,) * 6 + (do, do, s_prev[0], s_prev[1]), name="ret_bwd", grid=(nc,),
        in_specs=[pl.BlockSpec(memory_space=pltpu.SMEM)] + _ret_in_specs(rows) + wide + [state, state],
        out_specs=(wide[0],) * 3 + (wide[1],) * 3 + (pl.BlockSpec((2 * HEADS, 8 * HEAD_DIM), lambda s: (0, 0)),),
        out_shape=(_sds((n, RET_W), BF16),) * 6 + (_sds((2 * HEADS, 8 * HEAD_DIM), F32),),
        scratch_shapes=[pltpu.VMEM((2, HEADS, CHUNK, HEAD_DIM), F32), pltpu.VMEM((2, HEADS, 5, CHUNK, CHUNK), F32),
                        pltpu.VMEM((2, CHUNK, CHUNK), F32)],
        compiler_params=_cp())


def _shift_rows(cur, prev8, next8, k, seg_start, seg_end):
    tm = cur.shape[0]
    rows = _rows_iota(cur.shape)
    if k < 0:
        out = pltpu.roll(cur, -k, 0)
        for j in range(-k):
            halo = jnp.where(seg_start, 0.0, prev8[SUB + k + j:SUB + k + j + 1, :])
            out = jnp.where(rows == j, halo, out)
    else:
        out = pltpu.roll(cur, tm - k, 0)
        for j in range(k):
            halo = jnp.where(seg_end, 0.0, next8[j:j + 1, :])
            out = jnp.where(rows == tm - k + j, halo, out)
    return out


def _seg_flags(t, cb, nb):
    return jnp.logical_or(t == 0, t == cb), jnp.logical_or(t == cb - 1, t == nb - 1)


def _halo_specs(tile_of, n_rows, col):
    per = TM // SUB
    return [pl.BlockSpec((TM, LRU_W), lambda s: (tile_of(s), col)),
            pl.BlockSpec((SUB, LRU_W), lambda s: (jnp.maximum(tile_of(s) * per - 1, 0), col)),
            pl.BlockSpec((SUB, LRU_W), lambda s: (jnp.minimum((tile_of(s) + 1) * per, n_rows // SUB - 1), col))]


def _lru_gates(xr, prev8, next8, seg_start, seg_end, cw_ref, cb_ref, wg_ref, bg_ref, sp_ref):
    xm1 = _shift_rows(xr, prev8, next8, -1, seg_start, seg_end)
    xp1 = _shift_rows(xr, prev8, next8, 1, seg_start, seg_end)
    xp2 = _shift_rows(xr, prev8, next8, 2, seg_start, seg_end)
    xc = cb_ref[...] + xm1 * cw_ref[0:1, :] + xr * cw_ref[1:2, :] + xp1 * cw_ref[2:3, :] + xp2 * cw_ref[3:4, :]
    pre = _dot(xc, wg_ref[...]) + bg_ref[...]
    r = _sigmoid(pre[:, :LRU_W])
    i = _sigmoid(pre[:, LRU_W:])
    la = (-LRU_C) * r * sp_ref[...]
    a = jnp.exp(la)
    th = jnp.tanh(la)
    sq = jnp.sqrt(-2.0 * th / (1.0 - th))
    return xc, r, i, a, sq


def _scan_tile(a, b, ascending, carry, out_ref):
    tm, w = a.shape
    nsub = tm // SUB
    a = a.reshape(nsub, SUB, w)
    b = b.reshape(nsub, SUB, w)
    r8 = lax.broadcasted_iota(jnp.int32, a.shape, 1)
    for k in (1, 2, 4):
        if ascending:
            m = r8 >= k
            a_s, b_s = pltpu.roll(a, k, 1), pltpu.roll(b, k, 1)
        else:
            m = r8 < SUB - k
            a_s, b_s = pltpu.roll(a, SUB - k, 1), pltpu.roll(b, SUB - k, 1)
        b = a * jnp.where(m, b_s, 0.0) + b
        a = a * jnp.where(m, a_s, 1.0)
    c = carry[...]
    for j in range(nsub):
        off = (j if ascending else nsub - 1 - j) * SUB
        hb = a[off // SUB] * c + b[off // SUB]
        out_ref[off:off + SUB, :] = hb
        c = jnp.broadcast_to(hb[SUB - 1:SUB, :] if ascending else hb[0:1, :], c.shape)
    carry[...] = c


def _lru_fwd(p, wg, bg, sp, cw, cbias, dirn, cb, ride=None):
    n = p.shape[0]
    nb = n // TM
    tile_of = lambda s: _tile_order(dirn, s, cb, nb)

    def body(x_ref, xp_ref, xn_ref, wg_ref, bg_ref, sp_ref, cw_ref, cb_ref, h_ref, cin_ref, carry):
        s = pl.program_id(0)

        @pl.when(s == 0)
        def _():
            carry[...] = jnp.zeros_like(carry)

        seg_start, seg_end = _seg_flags(tile_of(s), cb, nb)
        xc, r, i, a, sq = _lru_gates(x_ref[...], xp_ref[...], xn_ref[...], seg_start, seg_end,
                                     cw_ref, cb_ref, wg_ref, bg_ref, sp_ref)
        cin_ref[...] = carry[...]
        _scan_tile(a, sq * (i * xc), dirn == 0, carry, h_ref)

    full = lambda shape: pl.BlockSpec(shape, lambda s: (0,) * len(shape))
    return _pcall_ride(
        body, ride, (p, p, p, wg, bg, sp, cw, cbias), name=f"lru_fwd{dirn}", grid=(nb,),
        in_specs=_halo_specs(tile_of, n, COL_XR) + [full((LRU_W, 2 * LRU_W)), full((1, 2 * LRU_W)), full((1, LRU_W)),
                                               full((4, LRU_W)), full((1, LRU_W))],
        out_specs=(pl.BlockSpec((TM, LRU_W), lambda s: (tile_of(s), 0)),
                   pl.BlockSpec((None, SUB, LRU_W), lambda s: (tile_of(s), 0, 0))),
        out_shape=(_sds((n, LRU_W), F32), _sds((nb, SUB, LRU_W), F32)),
        scratch_shapes=[pltpu.VMEM((SUB, LRU_W), F32)], compiler_params=_cp())


def _lru_bwd(p, wg, bg, sp, cw, cbias, h, cin, dhs, dirn, cb, ride=None):
    n = p.shape[0]
    nb = n // TM
    tile_of = lambda s: _tile_order(dirn, nb - 1 - s, cb, nb)

    def body(x_ref, xp_ref, xn_ref, wg_ref, bg_ref, sp_ref, cw_ref, cb_ref, h_ref, cin_ref, dhs_ref,
             dxc_ref, dwd_ref, acc_ref, carry, mu_sc, dwg_ref):
        s = pl.program_id(0)

        @pl.when(s == 0)
        def _():
            carry[...] = jnp.zeros_like(carry)
            dwg_ref[...] = jnp.zeros_like(dwg_ref)
            acc_ref[...] = jnp.zeros_like(acc_ref)

        seg_start, seg_end = _seg_flags(tile_of(s), cb, nb)
        xc, r, i, a, sq = _lru_gates(x_ref[...], xp_ref[...], xn_ref[...], seg_start, seg_end,
                                     cw_ref, cb_ref, wg_ref, bg_ref, sp_ref)
        rows = _rows_iota(a.shape)
        hv = h_ref[...]
        dh = dhs_ref[...]
        mu_next = carry[0:1, :]
        _scan_tile(a, a * dh, dirn == 1, carry, mu_sc)
        mu = mu_sc[...]
        if dirn == 0:
            hprev = jnp.where(rows == 0, cin_ref[0:1, :], pltpu.roll(hv, 1, 0))
            lam = dh + jnp.where(rows == TM - 1, mu_next, pltpu.roll(mu, TM - 1, 0))
        else:
            hprev = jnp.where(rows == TM - 1, cin_ref[0:1, :], pltpu.roll(hv, TM - 1, 0))
            lam = dh + jnp.where(rows == 0, mu_next, pltpu.roll(mu, 1, 0))
        ds = lam * (i * xc)
        di = lam * (sq * xc)
        dla = lam * hprev * a - ds * (a * a) / jnp.maximum(sq, 1e-20)
        dpr = dla * ((-LRU_C) * sp_ref[...]) * r * (1.0 - r)
        dpi = di * i * (1.0 - i)
        dpre = jnp.concatenate([dpr, dpi], axis=1)
        dxc_ref[...] = lam * (sq * i) + _dot_nt(dpre, wg_ref[...])
        dwg_ref[...] += _dot_tn(xc, dpre)
        acc_ref[0:1, :] += jnp.sum(dpre, axis=0, keepdims=True)
        acc_ref[1:2, 0:LRU_W] += jnp.sum(dla * ((-LRU_C) * r), axis=0, keepdims=True)

        @pl.when(s == nb - 1)
        def _():
            low = lax.broadcasted_iota(jnp.int32, (LRU_BD, 2 * LRU_BD), 1) < LRU_BD
            for half in (0, LRU_W):
                for m in range(LRU_BLOCKS // 2):
                    lanes = slice(half + 2 * LRU_BD * m, half + 2 * LRU_BD * (m + 1))
                    even = dwg_ref[2 * m * LRU_BD:(2 * m + 1) * LRU_BD, lanes]
                    odd = dwg_ref[(2 * m + 1) * LRU_BD:(2 * m + 2) * LRU_BD, lanes]
                    dwd_ref[:, lanes] = jnp.where(low, even, odd)

    full = lambda shape: pl.BlockSpec(shape, lambda s: (0,) * len(shape))
    tile = pl.BlockSpec((TM, LRU_W), lambda s: (tile_of(s), 0))
    return _pcall_ride(
        body, ride, (p, p, p, wg, bg, sp, cw, cbias, h, cin, dhs), name=f"lru_bwd{dirn}", grid=(nb,),
        in_specs=_halo_specs(tile_of, n, COL_XR) + [full((LRU_W, 2 * LRU_W)), full((1, 2 * LRU_W)), full((1, LRU_W)),
                                               full((4, LRU_W)), full((1, LRU_W)), tile,
                                               pl.BlockSpec((None, SUB, LRU_W), lambda s: (tile_of(s), 0, 0)), tile],
        out_specs=(tile, full((LRU_BD, 2 * LRU_W)), full((8, 2 * LRU_W))),
        out_shape=(_sds((n, LRU_W), F32), _sds((LRU_BD, 2 * LRU_W), F32), _sds((8, 2 * LRU_W), F32)),
        scratch_shapes=[pltpu.VMEM((SUB, LRU_W), F32), pltpu.VMEM((TM, LRU_W), F32), pltpu.VMEM((LRU_W, 2 * LRU_W), F32)],
        compiler_params=_cp())


def _assemble_dp(dqs, dks, dvs, dg, dgate, dxcs, p, cw, cosf, sins, cb, ride=None):
    n = p.shape[0]
    nb = n // TM
    tile_of = lambda s: s

    def body(dqf, dqb, dkf, dkb, dvf, dvb, dg_ref, dgate_ref, cf, pf, nf, cb_, pb, nb_, x_ref, xp_ref, xn_ref,
             cw_ref, cos_ref, sin_ref, dp_ref, acc_ref):
        s = pl.program_id(0)

        @pl.when(s == 0)
        def _():
            acc_ref[...] = jnp.zeros_like(acc_ref)

        seg_start, seg_end = _seg_flags(s, cb, nb)
        dq = dqf[...].astype(F32) + dqb[...].astype(F32)
        dk = dkf[...].astype(F32) + dkb[...].astype(F32)
        cosv, sinv = cos_ref[...], sin_ref[...]
        for h in range(HEADS):
            sl = slice(h * HEAD_DIM, (h + 1) * HEAD_DIM)
            sk = slice(RET_W + h * HEAD_DIM, RET_W + (h + 1) * HEAD_DIM)
            dp_ref[:, sl] = (dq[:, sl] * cosv + pltpu.roll(dq[:, sl] * sinv, HEAD_DIM // 2, 1)).astype(BF16)
            dp_ref[:, sk] = ((dk[:, sl] * cosv + pltpu.roll(dk[:, sl] * sinv, HEAD_DIM // 2, 1)) * K_SCALE).astype(BF16)
        dp_ref[:, 2 * RET_W:3 * RET_W] = (dvf[...].astype(F32) + dvb[...].astype(F32)).astype(BF16)
        dp_ref[:, 3 * RET_W:4 * RET_W] = dg_ref[...].astype(BF16)
        dxc = cf[...] + cb_[...]
        dprev = pf[...] + pb[...]
        dnext = nf[...] + nb_[...]
        dxr = (_shift_rows(dxc, dprev, dnext, 1, seg_start, seg_end) * cw_ref[0:1, :] + dxc * cw_ref[1:2, :]
               + _shift_rows(dxc, dprev, dnext, -1, seg_start, seg_end) * cw_ref[2:3, :]
               + _shift_rows(dxc, dprev, dnext, -2, seg_start, seg_end) * cw_ref[3:4, :])
        dp_ref[:, 4 * RET_W:4 * RET_W + LRU_W] = dxr.astype(BF16)
        dp_ref[:, 4 * RET_W + LRU_W:] = dgate_ref[...].astype(BF16)
        xr, xp, xn = x_ref[...], xp_ref[...], xn_ref[...]
        for j, k in enumerate((-1, 0, 1, 2)):
            xs = xr if k == 0 else _shift_rows(xr, xp, xn, k, seg_start, seg_end)
            acc_ref[j:j + 1, 0:LRU_W] += jnp.sum(dxc * xs, axis=0, keepdims=True)
        acc_ref[4:5, 0:LRU_W] += jnp.sum(dxc, axis=0, keepdims=True)

    t = pl.BlockSpec((TM, RET_W), lambda s: (s, 0))
    args = (dqs[0], dqs[1], dks[0], dks[1], dvs[0], dvs[1], dg, dgate, dxcs[0], dxcs[0], dxcs[0], dxcs[1], dxcs[1], dxcs[1],
            p, p, p, cw, cosf, sins)
    tab = pl.BlockSpec((TM, HEAD_DIM), lambda s: (s, 0))
    return _pcall_ride(
        body, ride, args, name="assemble_dp", grid=(nb,),
        in_specs=[t] * 8 + _halo_specs(tile_of, n, 0) * 2 + _halo_specs(tile_of, n, COL_XR)
        + [pl.BlockSpec((4, LRU_W), lambda s: (0, 0)), tab, tab],
        out_specs=(pl.BlockSpec((TM, 4 * RET_W + 2 * LRU_W), lambda s: (s, 0)), pl.BlockSpec((8, 2 * LRU_W), lambda s: (0, 0))),
        out_shape=(_sds((n, 4 * RET_W + 2 * LRU_W), BF16), _sds((8, 2 * LRU_W), F32)), compiler_params=_cp())


def _adamw(g, w, m, v):
    nm = ADAM_B1 * m + (1.0 - ADAM_B1) * g
    nv = ADAM_B2 * v + (1.0 - ADAM_B2) * (g * g)
    m_hat = nm / (1.0 - ADAM_B1 ** ADAM_STEP)
    v_hat = nv / (1.0 - ADAM_B2 ** ADAM_STEP)
    return (-ADAM_LR) * (m_hat / (jnp.sqrt(v_hat) + ADAM_EPS) + ADAM_WD * w), nm, nv


def _adam_many(items, name):
    n = len(items)

    def body(*refs):
        for i in range(n):
            g, w, m, v = (r[...] for r in refs[4 * i:4 * i + 4])
            for o_ref, val in zip(refs[4 * n + 3 * i:4 * n + 3 * i + 3], _adamw(g, w, m, v)):
                o_ref[...] = val

    out_shape = tuple(_sds(it[1].shape, F32) for it in items for _ in range(3))
    res = _pcall(body, name=name, out_shape=out_shape, compiler_params=_cp())(*[a for it in items for a in it])
    return [tuple(res[3 * i:3 * i + 3]) for i in range(n)]


def _sum_adam(parts_list, w, m, v, name):
    nparts, _, c = parts_list[0].shape
    r = w.shape[0]
    tr = min(parts_list[0].shape[1], 256)
    starts, o = [], 0
    for pa in parts_list:
        starts.append(o)
        o += pa.shape[1] // tr
    nseg = len(parts_list)

    def body(*refs):
        p_refs = refs[:nseg]
        w_ref, m_ref, v_ref, g_ref, d_ref, nm_ref, nv_ref = refs[nseg:]
        i = pl.program_id(0)
        for s, p_ref in enumerate(p_refs):
            end = starts[s + 1] if s + 1 < nseg else r // tr

            @pl.when(jnp.logical_and(i >= starts[s], i < end))
            def _():
                g = p_ref[0].astype(F32)
                for j in range(1, nparts):
                    g = g + p_ref[j].astype(F32)
                g_ref[...] = g
                d_ref[...], nm_ref[...], nv_ref[...] = _adamw(g, w_ref[...], m_ref[...], v_ref[...])

    def seg_spec(s):
        last = parts_list[s].shape[1] // tr - 1
        return pl.BlockSpec((nparts, tr, c), lambda i: (0, jnp.clip(i - starts[s], 0, last), 0))

    t = pl.BlockSpec((tr, c), lambda i: (i, 0))
    return _pcall(
        body, name=name, grid=(r // tr,),
        in_specs=[seg_spec(s) for s in range(nseg)] + [t, t, t],
        out_specs=(t, t, t, t), out_shape=(_sds((r, c), F32),) * 4, compiler_params=_cp(),
    )(*parts_list, w, m, v)


def _sum_parts(parts, name):
    nparts, r, c = parts.shape

    def body(p_ref, o_ref):
        g = p_ref[0]
        for j in range(1, nparts):
            g = g + p_ref[j]
        o_ref[...] = g

    return _pcall(body, name=name, out_shape=_sds((r, c), parts.dtype), compiler_params=_cp())(parts)


def _rot_tables(l_len, t_len):
    rows = t_len // GRID_W
    n_freq = HEAD_DIM // 4
    inv = ROPE_BASE ** (-jnp.arange(n_freq, dtype=F32) / n_freq)
    ang_r = jnp.arange(rows, dtype=F32)[:, None] * inv
    ang_c = jnp.arange(GRID_W, dtype=F32)[:, None] * inv
    cos = jnp.concatenate([jnp.repeat(jnp.cos(ang_r), GRID_W, axis=0), jnp.tile(jnp.cos(ang_c), (rows, 1))], axis=-1)
    sin = jnp.concatenate([jnp.repeat(jnp.sin(ang_r), GRID_W, axis=0), jnp.tile(jnp.sin(ang_c), (rows, 1))], axis=-1)
    cosf = jnp.concatenate([jnp.ones((l_len, HEAD_DIM), F32), jnp.concatenate([cos, cos], axis=-1)], axis=0)
    sins = jnp.concatenate([jnp.zeros((l_len, HEAD_DIM), F32), jnp.concatenate([-sin, sin], axis=-1)], axis=0)
    return cosf, sins


def _block_diag(w):
    eye = jnp.eye(LRU_BLOCKS, dtype=w.dtype)
    return (w[:, :, None, :] * eye[:, None, :, None]).reshape(LRU_W, LRU_W)


def _blocks_from_lanes(dwd_half):
    return dwd_half.reshape(LRU_BD, LRU_BLOCKS, LRU_BD).transpose(1, 0, 2)


def _silu(x):
    return x * jax.nn.sigmoid(x)


def kernel(x, c, ctx, c_ctx, w_ada, b_ada, norm1_g, norm2_g, w_in, ret_decay, conv_w, conv_b, lru_wa, lru_ba, lru_wx, lru_bx, lru_lambda, w_out, w_mlp1, w_mlp2, final_g, loss_target, m_c_ctx, m_w_ada, m_b_ada, m_norm1_g, m_norm2_g, m_w_in, m_ret_decay, m_conv_w, m_conv_b, m_lru_wa, m_lru_ba, m_lru_wx, m_lru_bx, m_lru_lambda, m_w_out, m_w_mlp1, m_w_mlp2, m_final_g, v_c_ctx, v_w_ada, v_b_ada, v_norm1_g, v_norm2_g, v_w_in, v_ret_decay, v_conv_w, v_conv_b, v_lru_wa, v_lru_ba, v_lru_wx, v_lru_bx, v_lru_lambda, v_w_out, v_w_mlp1, v_w_mlp2, v_final_g):
    t_len, d = x.shape[1], x.shape[2]
    l_len = ctx.shape[1]
    cb, cc = l_len // TM, l_len // CHUNK
    me = 4 * lax.axis_index("x") + 2 * lax.axis_index("y") + lax.axis_index("c")
    x2d, ctx2d, tgt2d = x[0], ctx[0], loss_target[0]
    ada_cols = w_ada.shape[2]
    wa2d = w_ada[0]

    sc_loc = conv_w.shape[2]
    pack_a = jnp.zeros((8, d), F32)
    pack_a = pack_a.at[0].set(_silu(c[0]))
    pack_a = pack_a.at[1, :4 * sc_loc].set(conv_w[0].reshape(-1))
    pack_a = pack_a.at[2, :2 * sc_loc].set(lru_ba[0].reshape(-1))
    pack_a = pack_a.at[3, :2 * sc_loc].set(lru_bx[0].reshape(-1))
    pack_a = pack_a.at[4, :2 * sc_loc].set(lru_lambda[0].reshape(-1))
    b_cols = lax.dynamic_slice(b_ada, (0, me * ada_cols), (1, ada_cols))
    all_a, s16, mod_parts = _mod_exchange(pack_a, _silu(c_ctx)[None, :], wa2d, b_cols)

    def unshard(row, k):
        return all_a[:, row, :k * sc_loc].reshape(N_DEV, k, sc_loc).transpose(1, 0, 2).reshape(k, N_DEV * sc_loc)

    conv_w_full = unshard(1, 4)
    ba_full, bx_full, lam_full = unshard(2, 2), unshard(3, 2), unshard(4, 2)

    mod_all = mod_parts.transpose(1, 0, 2).reshape(16, N_DEV * ada_cols)
    mod_me = lax.dynamic_slice(mod_all, (me, 0), (1, 6 * d)).reshape(6, d)
    mod_c = mod_all[8].reshape(6, d)
    modrows = jnp.concatenate([mod_c[0:2], mod_me], axis=0)

    lg = jax.nn.log_sigmoid(ret_decay[0])
    sp = jax.nn.softplus(-lam_full)
    wg = [jnp.concatenate([_block_diag(lru_wa[0, dd]), _block_diag(lru_wx[0, dd])], axis=1).astype(BF16) for dd in (0, 1)]
    bg = [jnp.concatenate([ba_full[dd], bx_full[dd]])[None, :] for dd in (0, 1)]
    cosf, sins = _rot_tables(l_len, t_len)

    (hn, hnt), win_g = _norm1_fwd(ctx2d, x2d, norm1_g, modrows, cb, ride=("gather", w_in[0].astype(BF16)))
    (qkv, p), w2_g = _mm_in(hn, win_g, cosf, sins, ride=("gather", w_mlp2[0].astype(BF16)))
    w2_g = w2_g.reshape(1, 4 * d, d)
    (o0, o1, sp0, sp1), w1_g = _ret_fwd(qkv, lg, cc, ride=("gather", w_mlp1[0].astype(BF16)))
    o, s_prev = [o0, o1], [sp0, sp1]
    (h0, cin0), wout_g = _lru_fwd(p, wg[0], bg[0], sp[0:1], conv_w_full, conv_b, 0, cb, ride=("gather", w_out[0].astype(BF16)))
    wout_g = wout_g.reshape(1, d, d)
    (h1, cin1), _ = _lru_fwd(p, wg[1], bg[1], sp[1:2], conv_w_full, conv_b, 1, cb)
    h, cin = [h0, h1], [cin0, cin1]
    mix = _mix_fwd(o[0], o[1], p, h[0], h[1], cb, t_len)
    y, x1, h2, h2t = _mm_nn(mix, wout_g, F32, "mm_out_norm2", tm=TL, epi=_res_norm2_epilogue(x2d, norm2_g, modrows, TL))
    r = _mm_nn(h2, w1_g, BF16, "mm_mlp1", relu_out=True, tm=TL)
    dx2, dz, dzt, facc = _mm_nn(r, w2_g, F32, "mm_mlp2_final", square_lhs=True, tm=TL, vmem_mb=58,
                                epi=_final_epilogue(x1, tgt2d, final_g[None, :], modrows, TL))

    du = _mm_nt(dz, w2_g, BF16, "mm_da2", relu_mul=r, tm=TL, vmem_mb=58)
    gw2_lo, gw2_hi = _mm_wgrad(dzt, r, "mm_dw2", w2_g.shape[1] // N_DEV, BF16, transpose_out=True, square_rhs=True,
                               halves=True)
    gw1_lo, gw1_hi = _mm_wgrad(h2t, du, "mm_dw1", w1_g.shape[2], BF16, halves=True)
    (dx1, dy, n2acc), gw2_lo_all = _mm_nt(du, w1_g, F32, "mm_dh2_norm2", ride=("a2a", gw2_lo), tm=TL, vmem_mb=60,
                                          epi=_bwd_norm2_epilogue(x1, dx2, y, norm2_g, modrows, TL))
    gwo = _mm_tn(mix, dy, "mm_dwout", False, d, BF16, 512).reshape(N_DEV, -1, d)
    (do, dg, dgate, dhs), gwo_all = _mm_nt(dy, wout_g, F32, "mm_dmix_mix", ride=("a2a", gwo),
                                           epi=_mix_bwd_epilogue(o[0], o[1], p, h[0], h[1], cb))
    dxcs, dwgs, laccs, rides, got = [], [], [], [("a2a", gw1_hi), ("a2a", gw2_hi)], []
    (dq0, dk0, dv0, dq1, dk1, dv1, dlg_lanes), gw1_lo_all = _ret_bwd(qkv, lg, do, s_prev, cc, ride=("a2a", gw1_lo))
    dqs, dks, dvs = [dq0, dq1], [dk0, dk1], [dv0, dv1]
    for dd in (0, 1):
        (dxc_, dwg_, lacc_), got_ = _lru_bwd(p, wg[dd], bg[dd], sp[dd:dd + 1], conv_w_full, conv_b, h[dd], cin[dd], dhs, dd, cb,
                                             ride=rides[dd])
        dxcs.append(dxc_); dwgs.append(dwg_); laccs.append(lacc_); got.append(got_)
    gw1_hi_all, gw2_hi_all = got
    gw1_all, gw2_all = [gw1_lo_all, gw1_hi_all], [gw2_lo_all, gw2_hi_all]
    (dp, cacc), _ = _assemble_dp(dqs, dks, dvs, dg, dgate, dxcs, p, conv_w_full, cosf, sins, cb)
    pack_b = jnp.concatenate([n2acc, facc, cacc, laccs[0], laccs[1], dlg_lanes, dwgs[0], dwgs[1]], axis=0)
    gwi, all_b = _mm_wgrad(hnt, dp, "mm_dwin", win_g.shape[2], BF16, ride=("gather", pack_b))
    (grad_x, n1acc), gwi_all = _mm_nt(dp, win_g, F32, "mm_dhn_norm1", ride=("a2a", gwi),
                                      epi=_bwd_norm1_epilogue(ctx2d, x2d, dx1, norm1_g, modrows, cb))
    all_n1 = _all_gather_small(n1acc, "gather_norm1_grads")
    tot = _sum_parts(all_b, "sum_small_grads")
    t_n1 = _sum_parts(all_n1, "sum_norm1_grads")
    t_n2, t_f, t_conv, t_dlg = tot[0:8], tot[8:16], tot[16:24, :LRU_W], tot[40:48, :HEAD_DIM]
    t_l = [tot[24:32], tot[32:40]]
    t_dwd = [tot[48:48 + LRU_BD], tot[48 + LRU_BD:48 + 2 * LRU_BD]]
    loss = (0.5 / d) * jnp.sum(t_f[2])
    t_wa = jnp.stack([_blocks_from_lanes(t_dwd[dd][:, :LRU_W]) for dd in (0, 1)])
    t_wx = jnp.stack([_blocks_from_lanes(t_dwd[dd][:, LRU_W:]) for dd in (0, 1)])
    t_ba = jnp.stack([t_l[dd][0, :LRU_W] for dd in (0, 1)])
    t_bx = jnp.stack([t_l[dd][0, LRU_W:] for dd in (0, 1)])
    t_sp = jnp.stack([t_l[dd][1, :LRU_W] for dd in (0, 1)])
    dm_rows = jnp.stack([all_n1[:, 2, :], all_n1[:, 3, :], all_b[:, 3, :], all_b[:, 0, :], all_b[:, 1, :], all_b[:, 9, :]],
                        axis=1).reshape(N_DEV, 6 * d)
    dm_c = jnp.concatenate([t_n1[0], t_n1[1], jnp.zeros((4 * d,), F32)])
    dm16 = jnp.zeros((16, 6 * d), F32).at[0:8].set(dm_rows).at[8].set(dm_c)
    g_b_ada = jnp.sum(dm16, axis=0)[None, :]
    dm_cols = lax.dynamic_slice(dm16, (0, me * ada_cols), (16, ada_cols))
    g_w_ada, ds16 = _ada_bwd(s16, dm_cols, wa2d)
    ds_all = _all_gather_small(ds16[8:16], "gather_dsilu")
    dsilu_cc = _sum_parts(ds_all, "sum_dsilu")[0]
    sg_cc = jax.nn.sigmoid(c_ctx)
    g_c_ctx = dsilu_cc * (sg_cc * (1.0 + c_ctx * (1.0 - sg_cc)))

    g_ret_decay = jnp.sum(t_dlg, axis=-1).reshape(2, HEADS) * jax.nn.sigmoid(-ret_decay[0])
    g_lambda_full = -t_sp * jax.nn.sigmoid(-lam_full)

    def my_cols(full):
        return lax.dynamic_slice(full, (0, me * sc_loc), (full.shape[0], sc_loc))

    small_g = dict(
        c_ctx=g_c_ctx[None], b_ada=g_b_ada, norm1_g=t_n1[4:5], norm2_g=t_n2[2:3], ret_decay=g_ret_decay,
        conv_w=my_cols(t_conv[0:4]), conv_b=t_conv[4:5], lru_wa=t_wa.reshape(-1, LRU_BD), lru_ba=my_cols(t_ba),
        lru_wx=t_wx.reshape(-1, LRU_BD), lru_bx=my_cols(t_bx), lru_lambda=my_cols(g_lambda_full), final_g=t_f[0:1])
    small = dict(
        c_ctx=(c_ctx, m_c_ctx, v_c_ctx), b_ada=(b_ada, m_b_ada, v_b_ada), norm1_g=(norm1_g, m_norm1_g, v_norm1_g),
        norm2_g=(norm2_g, m_norm2_g, v_norm2_g), ret_decay=(ret_decay, m_ret_decay, v_ret_decay),
        conv_w=(conv_w, m_conv_w, v_conv_w), conv_b=(conv_b, m_conv_b, v_conv_b), lru_wa=(lru_wa, m_lru_wa, v_lru_wa),
        lru_ba=(lru_ba, m_lru_ba, v_lru_ba), lru_wx=(lru_wx, m_lru_wx, v_lru_wx), lru_bx=(lru_bx, m_lru_bx, v_lru_bx),
        lru_lambda=(lru_lambda, m_lru_lambda, v_lru_lambda), final_g=(final_g, m_final_g, v_final_g))
    names = list(small)
    items = [(small_g[k],) + tuple(a.reshape(small_g[k].shape) for a in small[k]) for k in names]
    res = {}
    for k, it, (d_, m_, v_) in zip(names, items, _adam_many(items, "adam_small")):
        shape = small[k][0].shape
        res[k] = tuple(a.reshape(shape) for a in (it[0], d_, m_, v_))

    def big(parts, w, m, v, name):
        out = _sum_adam(parts, w[0], m[0], v[0], name)
        return tuple(a[None] for a in out)

    res["w_ada"] = big([g_w_ada[None]], w_ada, m_w_ada, v_w_ada, "adam_w_ada")
    res["w_in"] = big([gwi_all], w_in, m_w_in, v_w_in, "adam_w_in")
    res["w_out"] = big([gwo_all], w_out, m_w_out, v_w_out, "adam_w_out")
    res["w_mlp1"] = big(gw1_all, w_mlp1, m_w_mlp1, v_w_mlp1, "adam_w_mlp1")
    res["w_mlp2"] = big(gw2_all, w_mlp2, m_w_mlp2, v_w_mlp2, "adam_w_mlp2")

    order = ["c_ctx", "w_ada", "b_ada", "norm1_g", "norm2_g", "w_in", "ret_decay", "conv_w", "conv_b", "lru_wa", "lru_ba",
             "lru_wx", "lru_bx", "lru_lambda", "w_out", "w_mlp1", "w_mlp2", "final_g"]
    outs = [loss, grad_x[None]]
    for j in range(4):
        outs += [res[k][j] for k in order]
    return tuple(outs)
```

```python
import jax
import jax.numpy as jnp
from jax import lax
from jax.experimental import pallas as pl
from jax.experimental.pallas import tpu as pltpu

F32 = jnp.float32
BF16 = jnp.bfloat16
N_DEV = 8
MESH = pl.DeviceIdType.MESH

HEADS = 4
HEAD_DIM = 128
CHUNK = 128
RET_W = HEADS * HEAD_DIM
LRU_W = 512
LRU_BLOCKS = 8
LRU_BD = LRU_W // LRU_BLOCKS
LRU_C = 8.0
EPS = 1e-6
K_SCALE = HEAD_DIM ** -0.5
ROPE_BASE = 10000.0
GRID_W = 64
TM = 256
TL = 512
SUB = 8

ADAM_LR = 0.001
ADAM_B1 = 0.9
ADAM_B2 = 0.999
ADAM_EPS = 1e-08
ADAM_WD = 0.01
ADAM_STEP = 10

COL_G, COL_XR, COL_GATE = 0, 1, 2

R_CSH1, R_CSC1, R_SH1, R_SC1, R_G1, R_SH2, R_SC2, R_G2 = range(8)


def _pcall(body, **kw):
    return pl.pallas_call(body, **kw)


def _cp(vmem_mb=48):
    return pltpu.CompilerParams(vmem_limit_bytes=vmem_mb << 20)


def _sds(shape, dtype):
    return jax.ShapeDtypeStruct(shape, dtype)


def _dot(a, b):
    return jnp.dot(a.astype(BF16), b.astype(BF16), preferred_element_type=F32)


def _dot_nt(a, b):
    return lax.dot_general(a.astype(BF16), b.astype(BF16), (((1,), (1,)), ((), ())), preferred_element_type=F32)


def _dot_tn(a, b):
    return lax.dot_general(a.astype(BF16), b.astype(BF16), (((0,), (0,)), ((), ())), preferred_element_type=F32)


def _sigmoid(x):
    return 0.5 * jnp.tanh(0.5 * x) + 0.5


def _gelu(x):
    return 0.5 * x * (1.0 + jnp.tanh(0.7978845608028654 * (x + 0.044715 * x * x * x)))


def _dgelu(x):
    t = jnp.tanh(0.7978845608028654 * (x + 0.044715 * x * x * x))
    return 0.5 * (1.0 + t) + 0.5 * x * (1.0 - t * t) * 0.7978845608028654 * (1.0 + 3.0 * 0.044715 * x * x)


def _rows_iota(shape):
    return lax.broadcasted_iota(jnp.int32, shape, 0)


def _tile_order(dirn, s, cb, nb):
    if dirn == 0:
        return s
    return jnp.where(s < cb, cb - 1 - s, nb - 1 - (s - cb))


_SEMS = [pltpu.SemaphoreType.DMA((7,)), pltpu.SemaphoreType.DMA((7,)), pltpu.SemaphoreType.DMA(())]
_ANY = pl.BlockSpec(memory_space=pl.ANY)


def _gather_copies(x_ref, out_ref, send_sems, recv_sems, local_sem):
    mx, my, mc = lax.axis_index("x"), lax.axis_index("y"), lax.axis_index("c")
    me, sibling = (mx, my, mc), (mx, my, 1 - mc)
    chips = [(1 - mx, my), (mx, 1 - my), (1 - mx, 1 - my)]

    def slot(px, py, pc):
        return out_ref.at[4 * px + 2 * py + pc]

    def copy(k, block, to, src=None):
        return pltpu.make_async_remote_copy(
            src_ref=slot(*block) if src is None else src, dst_ref=slot(*block),
            send_sem=send_sems.at[k], recv_sem=recv_sems.at[k], device_id=to, device_id_type=MESH)

    mine = pltpu.make_async_copy(x_ref, slot(*me), local_sem)
    first = [copy(0, me, sibling, src=x_ref)] + [copy(1 + j, me, (*chip, mc), src=x_ref) for j, chip in enumerate(chips)]
    passed = [copy(4 + j, (*chip, mc), sibling) for j, chip in enumerate(chips)]
    recv_ici = [copy(1 + j, (*chip, mc), me) for j, chip in enumerate(chips)]
    recv_d2d = [copy(0, sibling, me)] + [copy(4 + j, (*chip, 1 - mc), me) for j, chip in enumerate(chips)]
    return mine, first, passed, recv_ici, recv_d2d


def _gather_start(*refs):
    mine, first, _, _, _ = _gather_copies(*refs)
    mine.start()
    for cp in first:
        cp.start()


def _gather_pass_on(*refs):
    _, _, passed, recv_ici, _ = _gather_copies(*refs)
    for landed, onward in zip(recv_ici, passed):
        landed.wait_recv()
        onward.start()


def _gather_finish(*refs):
    mine, first, passed, _, recv_d2d = _gather_copies(*refs)
    for landed in recv_d2d:
        landed.wait_recv()
    for cp in first + passed:
        cp.wait_send()
    mine.wait()


def _a2a_copies(g_ref, out_ref, send_sems, recv_sems, local_sem):
    mx, my, mc = lax.axis_index("x"), lax.axis_index("y"), lax.axis_index("c")
    me = 4 * mx + 2 * my + mc
    mine = pltpu.make_async_copy(g_ref.at[me], out_ref.at[me], local_sem)
    copies = []
    for k in range(1, N_DEV):
        px = 1 - mx if (k >> 2) & 1 else mx
        py = 1 - my if (k >> 1) & 1 else my
        pc = 1 - mc if k & 1 else mc
        copies.append(pltpu.make_async_remote_copy(
            src_ref=g_ref.at[4 * px + 2 * py + pc], dst_ref=out_ref.at[me],
            send_sem=send_sems.at[k - 1], recv_sem=recv_sems.at[k - 1],
            device_id=(px, py, pc), device_id_type=MESH))
    return mine, copies


def _a2a_start(*refs):
    mine, copies = _a2a_copies(*refs)
    mine.start()
    for cp in copies:
        cp.start()


def _a2a_finish(*refs):
    mine, copies = _a2a_copies(*refs)
    for cp in copies:
        cp.wait()
    mine.wait()


_EXCHANGES = {"gather": (_gather_start, _gather_finish), "a2a": (_a2a_start, _a2a_finish)}
PASS_ON_LEAD = 3


def _exchange_shape(kind, src):
    return _sds((N_DEV,) + src.shape if kind == "gather" else src.shape, src.dtype)


def _all_gather_small(x, name):
    def body(x_ref, out_ref, send_sems, recv_sems, local_sem):
        mx, my, mc = lax.axis_index("x"), lax.axis_index("y"), lax.axis_index("c")
        me = 4 * mx + 2 * my + mc
        mine = pltpu.make_async_copy(x_ref, out_ref.at[me], local_sem)
        mine.start()
        copies = []
        for k in range(1, N_DEV):
            peer = (1 - mx if (k >> 2) & 1 else mx, 1 - my if (k >> 1) & 1 else my, 1 - mc if k & 1 else mc)
            copies.append(pltpu.make_async_remote_copy(
                src_ref=x_ref, dst_ref=out_ref.at[me], send_sem=send_sems.at[k - 1], recv_sem=recv_sems.at[k - 1],
                device_id=peer, device_id_type=MESH))
            copies[-1].start()
        for cp in copies:
            cp.wait()
        mine.wait()

    return _pcall(body, name=name, out_shape=_exchange_shape("gather", x), in_specs=[_ANY], out_specs=_ANY,
                  scratch_shapes=list(_SEMS))(x)


def _pcall_ride(body, ride, args, *, name, grid, in_specs, out_specs, out_shape, scratch_shapes=(), compiler_params=None):
    if ride is None:
        out = _pcall(body, name=name, grid=grid, in_specs=in_specs, out_specs=out_specs, out_shape=out_shape,
                     scratch_shapes=list(scratch_shapes), compiler_params=compiler_params)(*args)
        return out, None
    kind, src = ride
    start, finish = _EXCHANGES[kind]
    single = not isinstance(out_shape, (tuple, list))
    out_specs_t = (out_specs,) if single else tuple(out_specs)
    out_shape_t = (out_shape,) if single else tuple(out_shape)
    n_in, n_out, n_sc = len(in_specs), len(out_shape_t), len(scratch_shapes)

    def wrapped(*refs):
        ins, src_ref = refs[:n_in], refs[n_in]
        outs, dst_ref = refs[n_in + 1:n_in + 1 + n_out], refs[n_in + 1 + n_out]
        scratch = refs[n_in + 2 + n_out:n_in + 2 + n_out + n_sc]
        sems = refs[n_in + 2 + n_out + n_sc:]
        first = pl.program_id(0) == 0
        last = pl.program_id(0) == grid[0] - 1
        for ax in range(1, len(grid)):
            first = jnp.logical_and(first, pl.program_id(ax) == 0)
            last = jnp.logical_and(last, pl.program_id(ax) == grid[ax] - 1)

        @pl.when(first)
        def _():
            start(src_ref, dst_ref, *sems)

        body(*ins, *outs, *scratch)

        if kind == "gather":
            @pl.when(pl.program_id(0) == max(grid[0] - PASS_ON_LEAD, 0))
            def _():
                _gather_pass_on(src_ref, dst_ref, *sems)

        @pl.when(last)
        def _():
            finish(src_ref, dst_ref, *sems)

    res = _pcall(wrapped, name=name, grid=grid, in_specs=list(in_specs) + [_ANY], out_specs=out_specs_t + (_ANY,),
                 out_shape=out_shape_t + (_exchange_shape(kind, src),),
                 scratch_shapes=list(scratch_shapes) + list(_SEMS), compiler_params=compiler_params)(*args, src)
    return (res[0] if single else tuple(res[:-1])), res[-1]


class _Epilogue:
    def __init__(self, fn, args, in_specs, out_specs, out_shape, steps=None, lhs_map=None, delay=0):
        self.fn, self.args, self.in_specs, self.out_specs, self.out_shape = fn, tuple(args), list(in_specs), out_specs, out_shape
        self.steps, self.lhs_map, self.delay = steps, lhs_map, delay


def _mm_nn(a, w, out_dtype, name, square_lhs=False, relu_out=False, ride=None, tm=TM, epi=None, vmem_mb=48):
    m, k = a.shape
    nb, _, bn = w.shape
    tm = min(tm, m)
    delay = 0 if epi is None else epi.delay
    steps = m // tm

    def body(*refs):
        a_ref, w_ref = refs[:2]
        av = a_ref[...]
        if square_lhs:
            av = av * av
        if delay:
            assert nb == 1
            held = refs[-1]

            @pl.when(pl.program_id(0) == 0)
            def _():
                held[...] = jnp.zeros_like(held)

            fresh = jnp.dot(av, w_ref[0], preferred_element_type=F32)
            epi.fn(held[...], slice(0, tm), *refs[2:-1])
            held[...] = fresh
            return
        if epi is not None:
            assert nb == 1
            epi.fn(jnp.dot(av, w_ref[0], preferred_element_type=F32), slice(0, tm), *refs[2:])
            return
        for j in range(nb):
            r = jnp.dot(av, w_ref[j], preferred_element_type=F32)
            if relu_out:
                r = jnp.maximum(r, 0.0)
            refs[2][:, j * bn:(j + 1) * bn] = r.astype(out_dtype)

    lhs_map = (lambda i: (i, 0)) if not delay else (lambda i: (jnp.minimum(i, steps - 1), 0))
    in_specs = [pl.BlockSpec((tm, k), lhs_map), pl.BlockSpec((nb, k, bn), lambda i: (0, 0, 0))]
    scratch = []
    if epi is None:
        args, out_specs, out_shape = (a, w), pl.BlockSpec((tm, nb * bn), lambda i: (i, 0)), _sds((m, nb * bn), out_dtype)
    else:
        args, out_specs, out_shape = (a, w) + epi.args, epi.out_specs, epi.out_shape
        in_specs += epi.in_specs
        if delay:
            scratch.append(pltpu.VMEM((tm, bn), F32))
    out, ex = _pcall_ride(body, ride, args, name=name, grid=(steps + delay,), in_specs=in_specs, out_specs=out_specs,
                          out_shape=out_shape, scratch_shapes=scratch, compiler_params=_cp(vmem_mb))
    return out if ride is None else (out, ex)


def _mm_nt(dy, w, out_dtype, name, relu_mul=None, ride=None, tm=TM, epi=None, vmem_mb=48):
    m = dy.shape[0]
    nb, k, bn = w.shape
    tm = min(tm, m)
    n_extra = (0 if relu_mul is None else 1) + (0 if epi is None else len(epi.args))

    delay = 0 if epi is None else epi.delay

    def body(*refs):
        dy_ref, w_ref = refs[:2]
        scratch = refs[-1 - delay:]
        extra, outs, wt = refs[2:2 + n_extra], refs[2 + n_extra:-1 - delay], scratch[0]

        @pl.when(pl.program_id(0) == 0)
        def _():
            for j in range(nb):
                wt[j * bn:(j + 1) * bn, :] = w_ref[j].T
            if delay:
                scratch[1][...] = jnp.zeros_like(scratch[1])

        if delay:
            held = scratch[1]
            fresh = jnp.dot(dy_ref[...], wt[...], preferred_element_type=F32)
            epi.fn(held[...], slice(0, tm), *extra, *outs)
            held[...] = fresh
            return
        acc = jnp.dot(dy_ref[...], wt[...], preferred_element_type=F32)
        if epi is not None:
            epi.fn(acc, slice(0, tm), *extra, *outs)
            return
        if relu_mul is not None:
            acc = acc * (2.0 * extra[0][...].astype(F32))
        outs[0][...] = acc.astype(out_dtype)

    steps = m // tm if epi is None or epi.steps is None else epi.steps
    lhs_tile = (lambda i: (i, 0)) if epi is None or epi.lhs_map is None else epi.lhs_map
    lhs_map = lhs_tile if not delay else (lambda i: lhs_tile(jnp.minimum(i, steps - 1)))
    in_specs = [pl.BlockSpec((tm, nb * bn), lhs_map), pl.BlockSpec((nb, k, bn), lambda i: (0, 0, 0))]
    args = [dy, w]
    if relu_mul is not None:
        in_specs.append(pl.BlockSpec((tm, k), lambda i: (i, 0)))
        args.append(relu_mul)
    scratch = [pltpu.VMEM((nb * bn, k), BF16)]
    if epi is None:
        out_specs, out_shape = pl.BlockSpec((tm, k), lambda i: (i, 0)), _sds((m, k), out_dtype)
    else:
        args += list(epi.args)
        in_specs += epi.in_specs
        out_specs, out_shape = epi.out_specs, epi.out_shape
        if delay:
            scratch.append(pltpu.VMEM((tm, k), F32))
    out, ex = _pcall_ride(
        body, ride, args, name=name, grid=(steps + delay,), in_specs=in_specs, out_specs=out_specs, out_shape=out_shape,
        scratch_shapes=scratch, compiler_params=_cp(vmem_mb))
    return out if ride is None else (out, ex)


def _mm_tn(a, b, name, col_blocks, block, out_dtype, tm, square_lhs=False):
    m, k = a.shape
    nn = b.shape[1]
    steps = m // tm
    if col_blocks:
        nblk, acc_shape = nn // block, (k, block)
        a_spec = pl.BlockSpec((tm, k), lambda j, s: (s, 0))
        b_spec = pl.BlockSpec((tm, block), lambda j, s: (s, j))
    else:
        nblk, acc_shape = k // block, (block, nn)
        a_spec = pl.BlockSpec((tm, block), lambda j, s: (s, j))
        b_spec = pl.BlockSpec((tm, nn), lambda j, s: (s, 0))

    def body(a_ref, b_ref, o_ref, acc):
        s = pl.program_id(1)

        @pl.when(s == 0)
        def _():
            acc[...] = jnp.zeros_like(acc)

        av = a_ref[...]
        if square_lhs:
            av = av.astype(F32)
            av = (av * av).astype(BF16)
        acc[...] += _dot_tn(av, b_ref[...])

        @pl.when(s == steps - 1)
        def _():
            o_ref[...] = acc[...].astype(out_dtype)

    return _pcall(
        body, name=name, grid=(nblk, steps), in_specs=[a_spec, b_spec],
        out_specs=pl.BlockSpec((None,) + acc_shape, lambda j, s: (j, 0, 0)),
        out_shape=_sds((nblk,) + acc_shape, out_dtype),
        scratch_shapes=[pltpu.VMEM(acc_shape, F32)], compiler_params=_cp(),
    )(a, b)


def _mm_wgrad(at, b, name, bn, out_dtype, transpose_out=False, square_rhs=False, halves=False, ride=None):
    k, m = at.shape
    nblk = b.shape[1] // bn
    rows, cols = (bn, k) if transpose_out else (k, bn)
    nout = 2 if halves else 1
    per = rows // nout

    def body(a_ref, b_ref, *o_refs):
        bv = b_ref[...]
        if square_rhs:
            bv = bv * bv
        r = jnp.dot(a_ref[...], bv, preferred_element_type=F32)
        r = (r.T if transpose_out else r).astype(out_dtype)
        for i, o_ref in enumerate(o_refs):
            o_ref[...] = r[i * per:(i + 1) * per, :]

    out, ex = _pcall_ride(
        body, ride, (at, b), name=name, grid=(nblk,),
        in_specs=[pl.BlockSpec((k, m), lambda j: (0, 0)), pl.BlockSpec((m, bn), lambda j: (0, j))],
        out_specs=tuple(pl.BlockSpec((None, per, cols), lambda j: (j, 0, 0)) for _ in range(nout)),
        out_shape=tuple(_sds((nblk, per, cols), out_dtype) for _ in range(nout)), compiler_params=_cp())
    out = out if halves else out[0]
    return out if ride is None else (out, ex)


def _mm_in(hn, w, cosf, sins, ride):
    m, k = hn.shape
    nb, _, bn = w.shape

    def body(a_ref, w_ref, c_ref, s_ref, qkv_ref, rest_ref, pt):
        av = a_ref[...]
        for j in range(nb):
            pt[:, j * bn:(j + 1) * bn] = jnp.dot(av, w_ref[j], preferred_element_type=F32)
        cf, ss = c_ref[...], s_ref[...]
        for h in range(HEADS):
            sq = slice(h * HEAD_DIM, (h + 1) * HEAD_DIM)
            sk = slice(RET_W + h * HEAD_DIM, RET_W + (h + 1) * HEAD_DIM)
            qkv_ref[:, sq] = _rot(pt[:, sq], cf, ss).astype(BF16)
            qkv_ref[:, sk] = (_rot(pt[:, sk], cf, ss) * K_SCALE).astype(BF16)
        qkv_ref[:, 2 * RET_W:] = pt[:, 2 * RET_W:3 * RET_W].astype(BF16)
        rest_ref[...] = pt[:, 3 * RET_W:]

    tab = pl.BlockSpec((TM, HEAD_DIM), lambda i: (i, 0))
    wide = pl.BlockSpec((TM, 3 * RET_W), lambda i: (i, 0))
    return _pcall_ride(
        body, ride, (hn, w, cosf, sins), name="mm_in", grid=(m // TM,),
        in_specs=[pl.BlockSpec((TM, k), lambda i: (i, 0)), pl.BlockSpec((nb, k, bn), lambda i: (0, 0, 0)), tab, tab],
        out_specs=(wide, wide), out_shape=(_sds((m, 3 * RET_W), BF16), _sds((m, nb * bn - 3 * RET_W), F32)),
        scratch_shapes=[pltpu.VMEM((TM, nb * bn), F32)], compiler_params=_cp())


def _mod_exchange(pack_a, silu_cc, w_ada, b_cols):
    d, cols = pack_a.shape[1], w_ada.shape[1]

    def body(pa_ref, scc_ref, w_ref, b_ref, alla_ref, s16_ref, modp_ref, mp, send1, recv1, send2, recv2):
        mx, my, mc = lax.axis_index("x"), lax.axis_index("y"), lax.axis_index("c")
        me = 4 * mx + 2 * my + mc

        def exchange(src_ref, dst_ref, send, recv):
            copies = []
            for k in range(1, N_DEV):
                peer = (1 - mx if (k >> 2) & 1 else mx, 1 - my if (k >> 1) & 1 else my, 1 - mc if k & 1 else mc)
                copies.append(pltpu.make_async_remote_copy(
                    src_ref=src_ref, dst_ref=dst_ref.at[me], send_sem=send.at[k - 1], recv_sem=recv.at[k - 1],
                    device_id=peer, device_id_type=MESH))
                copies[-1].start()
            dst_ref[me] = src_ref[...]
            for cp in copies:
                cp.wait()

        exchange(pa_ref, alla_ref, send1, recv1)
        s16_ref[...] = jnp.zeros_like(s16_ref)
        for j in range(N_DEV):
            s16_ref[j:j + 1, :] = alla_ref[j, 0:1, :]
        s16_ref[N_DEV:N_DEV + 1, :] = scc_ref[...]
        mp[...] = _dot(s16_ref[...], w_ref[...]) + b_ref[...]
        exchange(mp, modp_ref, send2, recv2)

    vmem = pl.BlockSpec(memory_space=pltpu.VMEM)
    sem7 = pltpu.SemaphoreType.DMA((N_DEV - 1,))
    return _pcall(
        body, name="mod_exchange", in_specs=[vmem] * 4, out_specs=(vmem,) * 3,
        out_shape=(_sds((N_DEV,) + pack_a.shape, F32), _sds((2 * N_DEV, d), F32), _sds((N_DEV, 2 * N_DEV, cols), F32)),
        scratch_shapes=[pltpu.VMEM((2 * N_DEV, cols), F32), sem7, sem7, sem7, sem7], compiler_params=_cp(),
    )(pack_a, silu_cc, w_ada, b_cols)


def _ada_bwd(s16, dm_cols, w_ada):
    def body(s_ref, d_ref, w_ref, gw_ref, ds_ref):
        gw_ref[...] = _dot_tn(s_ref[...], d_ref[...])
        ds_ref[...] = _dot_nt(d_ref[...], w_ref[...])

    return _pcall(body, name="ada_bwd",
                  out_shape=(_sds(w_ada.shape, F32), _sds(s16.shape, F32)), compiler_params=_cp())(s16, dm_cols, w_ada)


def _norm1_fwd(ctx, x, g, modrows, cb, ride=None):
    l_len, d = ctx.shape
    nb = (l_len + x.shape[0]) // TM

    def body(ctx_ref, x_ref, g_ref, m_ref, o_ref, ot_ref):
        is_ctx = pl.program_id(0) < cb
        xin = jnp.where(is_ctx, ctx_ref[...], x_ref[...])
        sh = jnp.where(is_ctx, m_ref[R_CSH1:R_CSH1 + 1, :], m_ref[R_SH1:R_SH1 + 1, :])
        sc = jnp.where(is_ctx, m_ref[R_CSC1:R_CSC1 + 1, :], m_ref[R_SC1:R_SC1 + 1, :])
        ms = jnp.mean(xin * xin, axis=-1, keepdims=True)
        n = xin * lax.rsqrt(ms + EPS) * g_ref[...]
        hn = n * (1.0 + sc) + sh
        o_ref[...] = hn.astype(BF16)
        ot_ref[...] = hn.T.astype(BF16)

    return _pcall_ride(
        body, ride, (ctx, x, g, modrows), name="norm1_fwd", grid=(nb,),
        in_specs=[pl.BlockSpec((TM, d), lambda i: (jnp.minimum(i, cb - 1), 0)),
                  pl.BlockSpec((TM, d), lambda i: (jnp.maximum(i - cb, 0), 0)),
                  pl.BlockSpec((1, d), lambda i: (0, 0)), pl.BlockSpec((8, d), lambda i: (0, 0))],
        out_specs=(pl.BlockSpec((TM, d), lambda i: (i, 0)), pl.BlockSpec((d, TM), lambda i: (0, i))),
        out_shape=(_sds((nb * TM, d), BF16), _sds((d, nb * TM), BF16)), compiler_params=_cp())


def _mix_fwd(o_f, o_b, p, h_f, h_b, cb, t_len):
    def body(of_ref, ob_ref, g_ref, gate_ref, hf_ref, hb_ref, mix_ref):
        o = of_ref[...] + ob_ref[...]
        g = g_ref[...]
        sg = g * _sigmoid(g)
        for hh in range(HEADS):
            sl = slice(hh * HEAD_DIM, (hh + 1) * HEAD_DIM)
            oh = o[:, sl]
            yc = oh - jnp.mean(oh, axis=-1, keepdims=True)
            var = jnp.mean(yc * yc, axis=-1, keepdims=True)
            mix_ref[:, sl] = (sg[:, sl] * (yc * lax.rsqrt(var + EPS))).astype(BF16)
        mix_ref[:, RET_W:] = ((hf_ref[...] + hb_ref[...]) * _gelu(gate_ref[...])).astype(BF16)

    row = lambda i: (i + cb, 0)
    return _pcall(
        body, name="mix_fwd", grid=(t_len // TM,),
        in_specs=[pl.BlockSpec((TM, RET_W), row), pl.BlockSpec((TM, RET_W), row),
                  pl.BlockSpec((TM, RET_W), lambda i: (i + cb, COL_G)), pl.BlockSpec((TM, LRU_W), lambda i: (i + cb, COL_GATE)),
                  pl.BlockSpec((TM, LRU_W), row), pl.BlockSpec((TM, LRU_W), row)],
        out_specs=pl.BlockSpec((TM, RET_W + LRU_W), lambda i: (i, 0)),
        out_shape=_sds((t_len, RET_W + LRU_W), BF16), compiler_params=_cp(),
    )(o_f, o_b, p, p, h_f, h_b)


def _res_norm2_epilogue(x, g, modrows, tm):
    t_len, d = x.shape
    tm = min(tm, t_len)

    def fn(y, rows, x_ref, g_ref, m_ref, y_ref, x1_ref, h2_ref, h2t_ref):
        y_ref[rows, :] = y
        x1 = x_ref[rows, :] + m_ref[R_G1:R_G1 + 1, :] * y
        ms = jnp.mean(x1 * x1, axis=-1, keepdims=True)
        n = x1 * lax.rsqrt(ms + EPS) * g_ref[...]
        x1_ref[rows, :] = x1
        h2 = n * (1.0 + m_ref[R_SC2:R_SC2 + 1, :]) + m_ref[R_SH2:R_SH2 + 1, :]
        h2_ref[rows, :] = h2.astype(BF16)
        h2t_ref[:, rows] = h2.T.astype(BF16)

    t = pl.BlockSpec((tm, d), lambda i: (i, 0))
    return _Epilogue(
        fn, (x, g, modrows),
        in_specs=[t, pl.BlockSpec((1, d), lambda i: (0, 0)), pl.BlockSpec((8, d), lambda i: (0, 0))],
        out_specs=(t, t, t, pl.BlockSpec((d, tm), lambda i: (0, i))),
        out_shape=(_sds((t_len, d), F32), _sds((t_len, d), F32), _sds((t_len, d), BF16), _sds((d, t_len), BF16)))


def _tile(i):
    return jnp.maximum(i - 1, 0)


def _zero_at_start(acc_ref, rows):
    @pl.when(pl.program_id(0) == 0)
    def _():
        acc_ref[...] = jnp.zeros_like(acc_ref)


def _final_epilogue(x1, target, fg, modrows, tm):
    t_len, d = x1.shape

    def fn(z, rows, x1_ref, t_ref, fg_ref, m_ref, dx2_ref, dz_ref, dzt_ref, acc_ref):
        _zero_at_start(acc_ref, rows)
        g2 = m_ref[R_G2:R_G2 + 1, :]
        x2 = x1_ref[rows, :] + g2 * z
        rstd = lax.rsqrt(jnp.mean(x2 * x2, axis=-1, keepdims=True) + EPS)
        xh = x2 * rstd
        fg = fg_ref[...]
        e = xh * fg - t_ref[rows, :]
        dy = e * (1.0 / d)
        dxh = dy * fg
        dx2 = rstd * (dxh - xh * jnp.mean(dxh * xh, axis=-1, keepdims=True))
        dx2_ref[rows, :] = dx2
        dz = g2 * dx2
        dz_ref[rows, :] = dz.astype(BF16)
        dzt_ref[:, rows] = dz.T.astype(BF16)
        acc_ref[0:1, :] += jnp.sum(dy * xh, axis=0, keepdims=True)
        acc_ref[1:2, :] += jnp.sum(dx2 * z, axis=0, keepdims=True)
        acc_ref[2:3, :] += jnp.sum(e * e, axis=0, keepdims=True)

    t = pl.BlockSpec((tm, d), lambda i: (i, 0))
    return _Epilogue(
        fn, (x1, target, fg, modrows),
        in_specs=[t, t, pl.BlockSpec((1, d), lambda i: (0, 0)), pl.BlockSpec((8, d), lambda i: (0, 0))],
        out_specs=(t, t, pl.BlockSpec((d, tm), lambda i: (0, i)), pl.BlockSpec((8, d), lambda i: (0, 0))),
        out_shape=(_sds((t_len, d), F32), _sds((t_len, d), BF16), _sds((d, t_len), BF16), _sds((8, d), F32)))


def _bwd_norm2_epilogue(x1, dx2, y, g, modrows, tm):
    t_len, d = x1.shape

    def fn(dh2, rows, x1_ref, dx2_ref, y_ref, g_ref, m_ref, dx1_ref, dy_ref, acc_ref):
        _zero_at_start(acc_ref, rows)
        x1 = x1_ref[rows, :]
        rstd = lax.rsqrt(jnp.mean(x1 * x1, axis=-1, keepdims=True) + EPS)
        xh = x1 * rstd
        gn = g_ref[...]
        dn = dh2 * (1.0 + m_ref[R_SC2:R_SC2 + 1, :])
        dxh = dn * gn
        dx1 = dx2_ref[rows, :] + rstd * (dxh - xh * jnp.mean(dxh * xh, axis=-1, keepdims=True))
        dx1_ref[rows, :] = dx1
        dy_ref[rows, :] = (m_ref[R_G1:R_G1 + 1, :] * dx1).astype(BF16)
        acc_ref[0:1, :] += jnp.sum(dh2, axis=0, keepdims=True)
        acc_ref[1:2, :] += jnp.sum(dh2 * xh * gn, axis=0, keepdims=True)
        acc_ref[2:3, :] += jnp.sum(dn * xh, axis=0, keepdims=True)
        acc_ref[3:4, :] += jnp.sum(dx1 * y_ref[rows, :], axis=0, keepdims=True)

    t = pl.BlockSpec((tm, d), lambda i: (i, 0))
    return _Epilogue(
        fn, (x1, dx2, y, g, modrows),
        in_specs=[t, t, t, pl.BlockSpec((1, d), lambda i: (0, 0)), pl.BlockSpec((8, d), lambda i: (0, 0))],
        out_specs=(t, t, pl.BlockSpec((8, d), lambda i: (0, 0))),
        out_shape=(_sds((t_len, d), F32), _sds((t_len, d), BF16), _sds((8, d), F32)))


def _mix_bwd_epilogue(o_f, o_b, p, h_f, h_b, cb):
    n = o_f.shape[0]
    tile = _tile

    def fn(dm, rows, of_ref, ob_ref, g_ref, gate_ref, hf_ref, hb_ref, do_ref, dg_ref, dgate_ref, dhs_ref):
        keep = jnp.where(pl.program_id(0) - 1 < cb, 0.0, 1.0)
        dm = dm * keep
        o = of_ref[rows, :] + ob_ref[rows, :]
        g = g_ref[rows, :]
        s = _sigmoid(g)
        sg = g * s
        dsg = s * (1.0 + g * (1.0 - s))
        for hh in range(HEADS):
            sl = slice(hh * HEAD_DIM, (hh + 1) * HEAD_DIM)
            oh = o[:, sl]
            yc = oh - jnp.mean(oh, axis=-1, keepdims=True)
            rs = lax.rsqrt(jnp.mean(yc * yc, axis=-1, keepdims=True) + EPS)
            gn = yc * rs
            dret = dm[:, sl]
            dgn = dret * sg[:, sl]
            dg_ref[rows, sl] = (dret * gn * dsg[:, sl]).astype(BF16)
            do_ref[rows, sl] = (rs * (dgn - jnp.mean(dgn, axis=-1, keepdims=True)
                                      - gn * jnp.mean(dgn * gn, axis=-1, keepdims=True))).astype(BF16)
        dlru = dm[:, RET_W:]
        gate = gate_ref[rows, :]
        dhs_ref[rows, :] = dlru * _gelu(gate)
        dgate_ref[rows, :] = (dlru * (hf_ref[rows, :] + hb_ref[rows, :]) * _dgelu(gate)).astype(BF16)

    t = pl.BlockSpec((TM, RET_W), lambda i: (tile(i), 0))
    return _Epilogue(
        fn, (o_f, o_b, p, p, h_f, h_b),
        in_specs=[t, t, pl.BlockSpec((TM, RET_W), lambda i: (tile(i), COL_G)),
                  pl.BlockSpec((TM, LRU_W), lambda i: (tile(i), COL_GATE)), t, t],
        out_specs=(t, t, t, t), out_shape=(_sds((n, RET_W), BF16),) * 3 + (_sds((n, RET_W), F32),),
        steps=n // TM, lhs_map=lambda i: (jnp.maximum(i - cb, 0), 0), delay=1)


def _bwd_norm1_epilogue(ctx, x, dx1, g, modrows, cb):
    t_len, d = x.shape

    def fn(dh, rows, ctx_ref, x_ref, dx1_ref, g_ref, m_ref, gx_ref, acc_ref):
        is_ctx = pl.program_id(0) < cb
        _zero_at_start(acc_ref, rows)
        xin = jnp.where(is_ctx, ctx_ref[rows, :], x_ref[rows, :])
        sc = jnp.where(is_ctx, m_ref[R_CSC1:R_CSC1 + 1, :], m_ref[R_SC1:R_SC1 + 1, :])
        rstd = lax.rsqrt(jnp.mean(xin * xin, axis=-1, keepdims=True) + EPS)
        xh = xin * rstd
        gn = g_ref[...]
        dn = dh * (1.0 + sc)
        dxh = dn * gn
        gx_ref[rows, :] = dx1_ref[rows, :] + rstd * (dxh - xh * jnp.mean(dxh * xh, axis=-1, keepdims=True))
        s0 = jnp.sum(dh, axis=0, keepdims=True)
        s1 = jnp.sum(dh * xh * gn, axis=0, keepdims=True)
        acc_ref[4:5, :] += jnp.sum(dn * xh, axis=0, keepdims=True)
        acc_ref[0:1, :] += jnp.where(is_ctx, s0, 0.0)
        acc_ref[1:2, :] += jnp.where(is_ctx, s1, 0.0)
        acc_ref[2:3, :] += jnp.where(is_ctx, 0.0, s0)
        acc_ref[3:4, :] += jnp.where(is_ctx, 0.0, s1)

    lat = pl.BlockSpec((TM, d), lambda i: (jnp.maximum(i - cb, 0), 0))
    return _Epilogue(
        fn, (ctx, x, dx1, g, modrows),
        in_specs=[pl.BlockSpec((TM, d), lambda i: (jnp.minimum(i, cb - 1), 0)),
                  lat, lat, pl.BlockSpec((1, d), lambda i: (0, 0)), pl.BlockSpec((8, d), lambda i: (0, 0))],
        out_specs=(lat, pl.BlockSpec((8, d), lambda i: (0, 0))),
        out_shape=(_sds((t_len, d), F32), _sds((8, d), F32)))


def _rot(x, cf, ss):
    return x * cf + pltpu.roll(x, HEAD_DIM // 2, 1) * ss


def _decay_exponents(dirn):
    ii = lax.broadcasted_iota(jnp.int32, (CHUNK, CHUNK), 0)
    jj = lax.broadcasted_iota(jnp.int32, (CHUNK, CHUNK), 1)
    rel = ii - jj if dirn == 0 else jj - ii
    pos = ii.astype(F32)
    if dirn == 0:
        cq, cs = pos + 1.0, (CHUNK - 1.0) - pos
    else:
        cq, cs = CHUNK - pos, pos
    return rel, jnp.maximum(rel, 0).astype(F32), cq, cs


def _store_decay(lg_ref, dec, relf_ref=None):
    for dirn in (0, 1):
        rel, relf, cq, cs = _decay_exponents(dirn)
        if relf_ref is not None:
            relf_ref[dirn] = relf
        for h in range(HEADS):
            lgv = lg_ref[dirn, h]
            wq, ws = jnp.exp(lgv * cq), jnp.exp(lgv * cs)
            dec[dirn, h, 0] = jnp.where(rel >= 0, jnp.exp(lgv * relf), 0.0)
            dec[dirn, h, 1] = wq
            dec[dirn, h, 2] = ws
            if relf_ref is not None:
                dec[dirn, h, 3] = wq * cq
                dec[dirn, h, 4] = ws * cs


def _ret_rows(cc, nc, step_of):
    return [lambda s, dirn=dirn: _tile_order(dirn, step_of(s), cc, nc) for dirn in (0, 1)]


def _ret_in_specs(rows):
    specs = []
    for row in rows:
        specs += [pl.BlockSpec((CHUNK, RET_W), lambda s, o=o, row=row: (row(s), o)) for o in (0, 1, 2)]
    return specs


def _ret_fwd(qkv, lg, cc, ride=None):
    n = qkv.shape[0]
    nc = n // CHUNK
    rows = _ret_rows(cc, nc, lambda s: s)

    def body(lg_ref, q0, k0, v0, q1, k1, v1, o0, o1, sp0, sp1, st, dec):
        @pl.when(pl.program_id(0) == 0)
        def _():
            st[...] = jnp.zeros_like(st)
            _store_decay(lg_ref, dec)

        refs = ((q0, k0, v0, o0, sp0), (q1, k1, v1, o1, sp1))
        chains = [(dirn, h, slice(h * HEAD_DIM, (h + 1) * HEAD_DIM)) for dirn in (0, 1) for h in range(HEADS)]
        scores, cross, update = [], [], []
        for dirn, h, sl in chains:
            q_ref, k_ref, v_ref, _, sp_ref = refs[dirn]
            q, k, v = q_ref[:, sl], k_ref[:, sl], v_ref[:, sl]
            sp = st[dirn, h]
            sp_ref[h] = sp
            scores.append(_dot_nt(q, k))
            cross.append(_dot(q * dec[dirn, h, 1], sp))
            update.append(_dot_tn(k * dec[dirn, h, 2], v))
        masked = [(a * dec[dirn, h, 0]).astype(BF16) for a, (dirn, h, _) in zip(scores, chains)]
        intra = [_dot(sc, refs[dirn][2][:, sl]) for sc, (dirn, h, sl) in zip(masked, chains)]
        for (dirn, h, sl), o_in, o_cr, upd in zip(chains, intra, cross, update):
            refs[dirn][3][:, sl] = o_in + o_cr
            st[dirn, h] = jnp.exp(lg_ref[dirn, h] * CHUNK) * st[dirn, h] + upd

    o_specs = [pl.BlockSpec((CHUNK, RET_W), lambda s, row=row: (row(s), 0)) for row in rows]
    state = pl.BlockSpec((None, HEADS, CHUNK, HEAD_DIM), lambda s: (s, 0, 0, 0))
    return _pcall_ride(
        body, ride, (lg,) + (qkv,) * 6, name="ret_fwd", grid=(nc,),
        in_specs=[pl.BlockSpec(memory_space=pltpu.SMEM)] + _ret_in_specs(rows),
        out_specs=(o_specs[0], o_specs[1], state, state),
        out_shape=(_sds((n, RET_W), F32),) * 2 + (_sds((nc, HEADS, CHUNK, HEAD_DIM), F32),) * 2,
        scratch_shapes=[pltpu.VMEM((2, HEADS, CHUNK, HEAD_DIM), F32), pltpu.VMEM((2, HEADS, 3, CHUNK, CHUNK), F32)],
        compiler_params=_cp())


def _ret_bwd(qkv, lg, do, s_prev, cc, ride=None):
    n = qkv.shape[0]
    nc = n // CHUNK
    rows = _ret_rows(cc, nc, lambda s: nc - 1 - s)

    def body(lg_ref, q0, k0, v0, q1, k1, v1, do0, do1, sp0, sp1, dq0, dk0, dv0, dq1, dk1, dv1, dlg_ref, dst, dec, relf):
        @pl.when(pl.program_id(0) == 0)
        def _():
            dst[...] = jnp.zeros_like(dst)
            dlg_ref[...] = jnp.zeros_like(dlg_ref)
            _store_decay(lg_ref, dec, relf)

        refs = ((q0, k0, v0, do0, sp0, dq0, dk0, dv0), (q1, k1, v1, do1, sp1, dq1, dk1, dv1))
        chains = [(dirn, h, slice(h * HEAD_DIM, (h + 1) * HEAD_DIM)) for dirn in (0, 1) for h in range(HEADS)]

        def tiles(dirn, sl):
            q_ref, k_ref, v_ref, do_ref = refs[dirn][:4]
            return q_ref[:, sl], k_ref[:, sl], v_ref[:, sl], do_ref[:, sl]

        a_s, g1_s, da_s, h1_s = [], [], [], []
        for dirn, h, sl in chains:
            q, k, v, dov = tiles(dirn, sl)
            a_s.append(_dot_nt(q, k))
            g1_s.append(_dot_nt(dov, refs[dirn][4][h]))
            da_s.append(_dot_nt(dov, v))
            h1_s.append(_dot_nt(v, dst[dirn, h]))
        da_s = [da * dec[dirn, h, 0] for da, (dirn, h, _) in zip(da_s, chains)]
        dq_s, dk_s, dv_s, ds_s = [], [], [], []
        for (dirn, h, sl), a, da in zip(chains, a_s, da_s):
            q, k, v, dov = tiles(dirn, sl)
            dq_s.append(_dot(da, k))
            dk_s.append(_dot_tn(da, q))
            dv_s.append(_dot_tn(a * dec[dirn, h, 0], dov) + _dot(k * dec[dirn, h, 2], dst[dirn, h]))
            ds_s.append(_dot_tn(q * dec[dirn, h, 1], dov))
        for (dirn, h, sl), a, g1, da, h1, dq2, dk2, dv, ds2 in zip(chains, a_s, g1_s, da_s, h1_s, dq_s, dk_s, dv_s, ds_s):
            q, k, _, _ = tiles(dirn, sl)
            dq_ref, dk_ref, dv_ref = refs[dirn][5:]
            sp, dsn = refs[dirn][4][h], dst[dirn, h]
            gc = jnp.exp(lg_ref[dirn, h] * CHUNK)
            dq_ref[:, sl] = (g1 * dec[dirn, h, 1] + dq2).astype(BF16)
            dk_ref[:, sl] = (dk2 + h1 * dec[dirn, h, 2]).astype(BF16)
            dv_ref[:, sl] = dv.astype(BF16)
            term = da * a * relf[dirn] + q * g1 * dec[dirn, h, 3] + k * h1 * dec[dirn, h, 4] + sp * dsn * (CHUNK * gc)
            dlg_ref[dirn * HEADS + h:dirn * HEADS + h + 1, 0:HEAD_DIM] += jnp.sum(term, axis=0, keepdims=True)
            dst[dirn, h] = gc * dsn + ds2

    wide = [pl.BlockSpec((CHUNK, RET_W), lambda s, row=row: (row(s), 0)) for row in rows]
    state = pl.BlockSpec((None, HEADS, CHUNK, HEAD_DIM), lambda s: (nc - 1 - s, 0, 0, 0))
    return _pcall_ride(
        body, ride, (lg,) + (qkv,) * 6 + (do, do, s_prev[0], s_prev[1]), name="ret_bwd", grid=(nc,),
        in_specs=[pl.BlockSpec(memory_space=pltpu.SMEM)] + _ret_in_specs(rows) + wide + [state, state],
        out_specs=(wide[0],) * 3 + (wide[1],) * 3 + (pl.BlockSpec((2 * HEADS, 8 * HEAD_DIM), lambda s: (0, 0)),),
        out_shape=(_sds((n, RET_W), BF16),) * 6 + (_sds((2 * HEADS, 8 * HEAD_DIM), F32),),
        scratch_shapes=[pltpu.VMEM((2, HEADS, CHUNK, HEAD_DIM), F32), pltpu.VMEM((2, HEADS, 5, CHUNK, CHUNK), F32),
                        pltpu.VMEM((2, CHUNK, CHUNK), F32)],
        compiler_params=_cp())


def _shift_rows(cur, prev8, next8, k, seg_start, seg_end):
    tm = cur.shape[0]
    rows = _rows_iota(cur.shape)
    if k < 0:
        out = pltpu.roll(cur, -k, 0)
        for j in range(-k):
            halo = jnp.where(seg_start, 0.0, prev8[SUB + k + j:SUB + k + j + 1, :])
            out = jnp.where(rows == j, halo, out)
    else:
        out = pltpu.roll(cur, tm - k, 0)
        for j in range(k):
            halo = jnp.where(seg_end, 0.0, next8[j:j + 1, :])
            out = jnp.where(rows == tm - k + j, halo, out)
    return out


def _seg_flags(t, cb, nb):
    return jnp.logical_or(t == 0, t == cb), jnp.logical_or(t == cb - 1, t == nb - 1)


def _halo_specs(tile_of, n_rows, col):
    per = TM // SUB
    return [pl.BlockSpec((TM, LRU_W), lambda s: (tile_of(s), col)),
            pl.BlockSpec((SUB, LRU_W), lambda s: (jnp.maximum(tile_of(s) * per - 1, 0), col)),
            pl.BlockSpec((SUB, LRU_W), lambda s: (jnp.minimum((tile_of(s) + 1) * per, n_rows // SUB - 1), col))]


def _lru_gates(xr, prev8, next8, seg_start, seg_end, cw_ref, cb_ref, wg_ref, bg_ref, sp_ref):
    xm1 = _shift_rows(xr, prev8, next8, -1, seg_start, seg_end)
    xp1 = _shift_rows(xr, prev8, next8, 1, seg_start, seg_end)
    xp2 = _shift_rows(xr, prev8, next8, 2, seg_start, seg_end)
    xc = cb_ref[...] + xm1 * cw_ref[0:1, :] + xr * cw_ref[1:2, :] + xp1 * cw_ref[2:3, :] + xp2 * cw_ref[3:4, :]
    pre = _dot(xc, wg_ref[...]) + bg_ref[...]
    r = _sigmoid(pre[:, :LRU_W])
    i = _sigmoid(pre[:, LRU_W:])
    la = (-LRU_C) * r * sp_ref[...]
    a = jnp.exp(la)
    th = jnp.tanh(la)
    sq = jnp.sqrt(-2.0 * th / (1.0 - th))
    return xc, r, i, a, sq


def _scan_tile(a, b, ascending, carry, out_ref):
    tm, w = a.shape
    nsub = tm // SUB
    a = a.reshape(nsub, SUB, w)
    b = b.reshape(nsub, SUB, w)
    r8 = lax.broadcasted_iota(jnp.int32, a.shape, 1)
    for k in (1, 2, 4):
        if ascending:
            m = r8 >= k
            a_s, b_s = pltpu.roll(a, k, 1), pltpu.roll(b, k, 1)
        else:
            m = r8 < SUB - k
            a_s, b_s = pltpu.roll(a, SUB - k, 1), pltpu.roll(b, SUB - k, 1)
        b = a * jnp.where(m, b_s, 0.0) + b
        a = a * jnp.where(m, a_s, 1.0)
    c = carry[...]
    for j in range(nsub):
        off = (j if ascending else nsub - 1 - j) * SUB
        hb = a[off // SUB] * c + b[off // SUB]
        out_ref[off:off + SUB, :] = hb
        c = jnp.broadcast_to(hb[SUB - 1:SUB, :] if ascending else hb[0:1, :], c.shape)
    carry[...] = c


def _lru_fwd(p, wg, bg, sp, cw, cbias, dirn, cb, ride=None):
    n = p.shape[0]
    nb = n // TM
    tile_of = lambda s: _tile_order(dirn, s, cb, nb)

    def body(x_ref, xp_ref, xn_ref, wg_ref, bg_ref, sp_ref, cw_ref, cb_ref, h_ref, cin_ref, carry):
        s = pl.program_id(0)

        @pl.when(s == 0)
        def _():
            carry[...] = jnp.zeros_like(carry)

        seg_start, seg_end = _seg_flags(tile_of(s), cb, nb)
        xc, r, i, a, sq = _lru_gates(x_ref[...], xp_ref[...], xn_ref[...], seg_start, seg_end,
                                     cw_ref, cb_ref, wg_ref, bg_ref, sp_ref)
        cin_ref[...] = carry[...]
        _scan_tile(a, sq * (i * xc), dirn == 0, carry, h_ref)

    full = lambda shape: pl.BlockSpec(shape, lambda s: (0,) * len(shape))
    return _pcall_ride(
        body, ride, (p, p, p, wg, bg, sp, cw, cbias), name=f"lru_fwd{dirn}", grid=(nb,),
        in_specs=_halo_specs(tile_of, n, COL_XR) + [full((LRU_W, 2 * LRU_W)), full((1, 2 * LRU_W)), full((1, LRU_W)),
                                               full((4, LRU_W)), full((1, LRU_W))],
        out_specs=(pl.BlockSpec((TM, LRU_W), lambda s: (tile_of(s), 0)),
                   pl.BlockSpec((None, SUB, LRU_W), lambda s: (tile_of(s), 0, 0))),
        out_shape=(_sds((n, LRU_W), F32), _sds((nb, SUB, LRU_W), F32)),
        scratch_shapes=[pltpu.VMEM((SUB, LRU_W), F32)], compiler_params=_cp())


def _lru_bwd(p, wg, bg, sp, cw, cbias, h, cin, dhs, dirn, cb, ride=None):
    n = p.shape[0]
    nb = n // TM
    tile_of = lambda s: _tile_order(dirn, nb - 1 - s, cb, nb)

    def body(x_ref, xp_ref, xn_ref, wg_ref, bg_ref, sp_ref, cw_ref, cb_ref, h_ref, cin_ref, dhs_ref,
             dxc_ref, dwd_ref, acc_ref, carry, mu_sc, dwg_ref):
        s = pl.program_id(0)

        @pl.when(s == 0)
        def _():
            carry[...] = jnp.zeros_like(carry)
            dwg_ref[...] = jnp.zeros_like(dwg_ref)
            acc_ref[...] = jnp.zeros_like(acc_ref)

        seg_start, seg_end = _seg_flags(tile_of(s), cb, nb)
        xc, r, i, a, sq = _lru_gates(x_ref[...], xp_ref[...], xn_ref[...], seg_start, seg_end,
                                     cw_ref, cb_ref, wg_ref, bg_ref, sp_ref)
        rows = _rows_iota(a.shape)
        hv = h_ref[...]
        dh = dhs_ref[...]
        mu_next = carry[0:1, :]
        _scan_tile(a, a * dh, dirn == 1, carry, mu_sc)
        mu = mu_sc[...]
        if dirn == 0:
            hprev = jnp.where(rows == 0, cin_ref[0:1, :], pltpu.roll(hv, 1, 0))
            lam = dh + jnp.where(rows == TM - 1, mu_next, pltpu.roll(mu, TM - 1, 0))
        else:
            hprev = jnp.where(rows == TM - 1, cin_ref[0:1, :], pltpu.roll(hv, TM - 1, 0))
            lam = dh + jnp.where(rows == 0, mu_next, pltpu.roll(mu, 1, 0))
        ds = lam * (i * xc)
        di = lam * (sq * xc)
        dla = lam * hprev * a - ds * (a * a) / jnp.maximum(sq, 1e-20)
        dpr = dla * ((-LRU_C) * sp_ref[...]) * r * (1.0 - r)
        dpi = di * i * (1.0 - i)
        dpre = jnp.concatenate([dpr, dpi], axis=1)
        dxc_ref[...] = lam * (sq * i) + _dot_nt(dpre, wg_ref[...])
        dwg_ref[...] += _dot_tn(xc, dpre)
        acc_ref[0:1, :] += jnp.sum(dpre, axis=0, keepdims=True)
        acc_ref[1:2, 0:LRU_W] += jnp.sum(dla * ((-LRU_C) * r), axis=0, keepdims=True)

        @pl.when(s == nb - 1)
        def _():
            low = lax.broadcasted_iota(jnp.int32, (LRU_BD, 2 * LRU_BD), 1) < LRU_BD
            for half in (0, LRU_W):
                for m in range(LRU_BLOCKS // 2):
                    lanes = slice(half + 2 * LRU_BD * m, half + 2 * LRU_BD * (m + 1))
                    even = dwg_ref[2 * m * LRU_BD:(2 * m + 1) * LRU_BD, lanes]
                    odd = dwg_ref[(2 * m + 1) * LRU_BD:(2 * m + 2) * LRU_BD, lanes]
                    dwd_ref[:, lanes] = jnp.where(low, even, odd)

    full = lambda shape: pl.BlockSpec(shape, lambda s: (0,) * len(shape))
    tile = pl.BlockSpec((TM, LRU_W), lambda s: (tile_of(s), 0))
    return _pcall_ride(
        body, ride, (p, p, p, wg, bg, sp, cw, cbias, h, cin, dhs), name=f"lru_bwd{dirn}", grid=(nb,),
        in_specs=_halo_specs(tile_of, n, COL_XR) + [full((LRU_W, 2 * LRU_W)), full((1, 2 * LRU_W)), full((1, LRU_W)),
                                               full((4, LRU_W)), full((1, LRU_W)), tile,
                                               pl.BlockSpec((None, SUB, LRU_W), lambda s: (tile_of(s), 0, 0)), tile],
        out_specs=(tile, full((LRU_BD, 2 * LRU_W)), full((8, 2 * LRU_W))),
        out_shape=(_sds((n, LRU_W), F32), _sds((LRU_BD, 2 * LRU_W), F32), _sds((8, 2 * LRU_W), F32)),
        scratch_shapes=[pltpu.VMEM((SUB, LRU_W), F32), pltpu.VMEM((TM, LRU_W), F32), pltpu.VMEM((LRU_W, 2 * LRU_W), F32)],
        compiler_params=_cp())


def _assemble_dp(dqs, dks, dvs, dg, dgate, dxcs, p, cw, cosf, sins, cb, ride=None):
    n = p.shape[0]
    nb = n // TM
    tile_of = lambda s: s

    def body(dqf, dqb, dkf, dkb, dvf, dvb, dg_ref, dgate_ref, cf, pf, nf, cb_, pb, nb_, x_ref, xp_ref, xn_ref,
             cw_ref, cos_ref, sin_ref, dp_ref, acc_ref):
        s = pl.program_id(0)

        @pl.when(s == 0)
        def _():
            acc_ref[...] = jnp.zeros_like(acc_ref)

        seg_start, seg_end = _seg_flags(s, cb, nb)
        dq = dqf[...].astype(F32) + dqb[...].astype(F32)
        dk = dkf[...].astype(F32) + dkb[...].astype(F32)
        cosv, sinv = cos_ref[...], sin_ref[...]
        for h in range(HEADS):
            sl = slice(h * HEAD_DIM, (h + 1) * HEAD_DIM)
            sk = slice(RET_W + h * HEAD_DIM, RET_W + (h + 1) * HEAD_DIM)
            dp_ref[:, sl] = (dq[:, sl] * cosv + pltpu.roll(dq[:, sl] * sinv, HEAD_DIM // 2, 1)).astype(BF16)
            dp_ref[:, sk] = ((dk[:, sl] * cosv + pltpu.roll(dk[:, sl] * sinv, HEAD_DIM // 2, 1)) * K_SCALE).astype(BF16)
        dp_ref[:, 2 * RET_W:3 * RET_W] = (dvf[...].astype(F32) + dvb[...].astype(F32)).astype(BF16)
        dp_ref[:, 3 * RET_W:4 * RET_W] = dg_ref[...].astype(BF16)
        dxc = cf[...] + cb_[...]
        dprev = pf[...] + pb[...]
        dnext = nf[...] + nb_[...]
        dxr = (_shift_rows(dxc, dprev, dnext, 1, seg_start, seg_end) * cw_ref[0:1, :] + dxc * cw_ref[1:2, :]
               + _shift_rows(dxc, dprev, dnext, -1, seg_start, seg_end) * cw_ref[2:3, :]
               + _shift_rows(dxc, dprev, dnext, -2, seg_start, seg_end) * cw_ref[3:4, :])
        dp_ref[:, 4 * RET_W:4 * RET_W + LRU_W] = dxr.astype(BF16)
        dp_ref[:, 4 * RET_W + LRU_W:] = dgate_ref[...].astype(BF16)
        xr, xp, xn = x_ref[...], xp_ref[...], xn_ref[...]
        for j, k in enumerate((-1, 0, 1, 2)):
            xs = xr if k == 0 else _shift_rows(xr, xp, xn, k, seg_start, seg_end)
            acc_ref[j:j + 1, 0:LRU_W] += jnp.sum(dxc * xs, axis=0, keepdims=True)
        acc_ref[4:5, 0:LRU_W] += jnp.sum(dxc, axis=0, keepdims=True)

    t = pl.BlockSpec((TM, RET_W), lambda s: (s, 0))
    args = (dqs[0], dqs[1], dks[0], dks[1], dvs[0], dvs[1], dg, dgate, dxcs[0], dxcs[0], dxcs[0], dxcs[1], dxcs[1], dxcs[1],
            p, p, p, cw, cosf, sins)
    tab = pl.BlockSpec((TM, HEAD_DIM), lambda s: (s, 0))
    return _pcall_ride(
        body, ride, args, name="assemble_dp", grid=(nb,),
        in_specs=[t] * 8 + _halo_specs(tile_of, n, 0) * 2 + _halo_specs(tile_of, n, COL_XR)
        + [pl.BlockSpec((4, LRU_W), lambda s: (0, 0)), tab, tab],
        out_specs=(pl.BlockSpec((TM, 4 * RET_W + 2 * LRU_W), lambda s: (s, 0)), pl.BlockSpec((8, 2 * LRU_W), lambda s: (0, 0))),
        out_shape=(_sds((n, 4 * RET_W + 2 * LRU_W), BF16), _sds((8, 2 * LRU_W), F32)), compiler_params=_cp())


def _adamw(g, w, m, v):
    nm = ADAM_B1 * m + (1.0 - ADAM_B1) * g
    nv = ADAM_B2 * v + (1.0 - ADAM_B2) * (g * g)
    m_hat = nm / (1.0 - ADAM_B1 ** ADAM_STEP)
    v_hat = nv / (1.0 - ADAM_B2 ** ADAM_STEP)
    return (-ADAM_LR) * (m_hat / (jnp.sqrt(v_hat) + ADAM_EPS) + ADAM_WD * w), nm, nv


def _adam_many(items, name):
    n = len(items)

    def body(*refs):
        for i in range(n):
            g, w, m, v = (r[...] for r in refs[4 * i:4 * i + 4])
            for o_ref, val in zip(refs[4 * n + 3 * i:4 * n + 3 * i + 3], _adamw(g, w, m, v)):
                o_ref[...] = val

    out_shape = tuple(_sds(it[1].shape, F32) for it in items for _ in range(3))
    res = _pcall(body, name=name, out_shape=out_shape, compiler_params=_cp())(*[a for it in items for a in it])
    return [tuple(res[3 * i:3 * i + 3]) for i in range(n)]


def _sum_adam(parts_list, w, m, v, name):
    nparts, _, c = parts_list[0].shape
    r = w.shape[0]
    tr = min(parts_list[0].shape[1], 512)
    starts, o = [], 0
    for pa in parts_list:
        starts.append(o)
        o += pa.shape[1] // tr
    nseg = len(parts_list)

    def body(*refs):
        p_refs = refs[:nseg]
        w_ref, m_ref, v_ref, g_ref, d_ref, nm_ref, nv_ref = refs[nseg:]
        i = pl.program_id(0)
        for s, p_ref in enumerate(p_refs):
            end = starts[s + 1] if s + 1 < nseg else r // tr

            @pl.when(jnp.logical_and(i >= starts[s], i < end))
            def _():
                g = p_ref[0].astype(F32)
                for j in range(1, nparts):
                    g = g + p_ref[j].astype(F32)
                g_ref[...] = g
                d_ref[...], nm_ref[...], nv_ref[...] = _adamw(g, w_ref[...], m_ref[...], v_ref[...])

    def seg_spec(s):
        last = parts_list[s].shape[1] // tr - 1
        return pl.BlockSpec((nparts, tr, c), lambda i: (0, jnp.clip(i - starts[s], 0, last), 0))

    t = pl.BlockSpec((tr, c), lambda i: (i, 0))
    return _pcall(
        body, name=name, grid=(r // tr,),
        in_specs=[seg_spec(s) for s in range(nseg)] + [t, t, t],
        out_specs=(t, t, t, t), out_shape=(_sds((r, c), F32),) * 4, compiler_params=_cp(),
    )(*parts_list, w, m, v)


def _sum_parts(parts, name):
    nparts, r, c = parts.shape

    def body(p_ref, o_ref):
        g = p_ref[0]
        for j in range(1, nparts):
            g = g + p_ref[j]
        o_ref[...] = g

    return _pcall(body, name=name, out_shape=_sds((r, c), parts.dtype), compiler_params=_cp())(parts)


def _rot_tables(l_len, t_len):
    rows = t_len // GRID_W
    n_freq = HEAD_DIM // 4
    inv = ROPE_BASE ** (-jnp.arange(n_freq, dtype=F32) / n_freq)
    ang_r = jnp.arange(rows, dtype=F32)[:, None] * inv
    ang_c = jnp.arange(GRID_W, dtype=F32)[:, None] * inv
    cos = jnp.concatenate([jnp.repeat(jnp.cos(ang_r), GRID_W, axis=0), jnp.tile(jnp.cos(ang_c), (rows, 1))], axis=-1)
    sin = jnp.concatenate([jnp.repeat(jnp.sin(ang_r), GRID_W, axis=0), jnp.tile(jnp.sin(ang_c), (rows, 1))], axis=-1)
    cosf = jnp.concatenate([jnp.ones((l_len, HEAD_DIM), F32), jnp.concatenate([cos, cos], axis=-1)], axis=0)
    sins = jnp.concatenate([jnp.zeros((l_len, HEAD_DIM), F32), jnp.concatenate([-sin, sin], axis=-1)], axis=0)
    return cosf, sins


def _block_diag(w):
    eye = jnp.eye(LRU_BLOCKS, dtype=w.dtype)
    return (w[:, :, None, :] * eye[:, None, :, None]).reshape(LRU_W, LRU_W)


def _blocks_from_lanes(dwd_half):
    return dwd_half.reshape(LRU_BD, LRU_BLOCKS, LRU_BD).transpose(1, 0, 2)


def _silu(x):
    return x * jax.nn.sigmoid(x)


def kernel(x, c, ctx, c_ctx, w_ada, b_ada, norm1_g, norm2_g, w_in, ret_decay, conv_w, conv_b, lru_wa, lru_ba, lru_wx, lru_bx, lru_lambda, w_out, w_mlp1, w_mlp2, final_g, loss_target, m_c_ctx, m_w_ada, m_b_ada, m_norm1_g, m_norm2_g, m_w_in, m_ret_decay, m_conv_w, m_conv_b, m_lru_wa, m_lru_ba, m_lru_wx, m_lru_bx, m_lru_lambda, m_w_out, m_w_mlp1, m_w_mlp2, m_final_g, v_c_ctx, v_w_ada, v_b_ada, v_norm1_g, v_norm2_g, v_w_in, v_ret_decay, v_conv_w, v_conv_b, v_lru_wa, v_lru_ba, v_lru_wx, v_lru_bx, v_lru_lambda, v_w_out, v_w_mlp1, v_w_mlp2, v_final_g):
    t_len, d = x.shape[1], x.shape[2]
    l_len = ctx.shape[1]
    cb, cc = l_len // TM, l_len // CHUNK
    me = 4 * lax.axis_index("x") + 2 * lax.axis_index("y") + lax.axis_index("c")
    x2d, ctx2d, tgt2d = x[0], ctx[0], loss_target[0]
    ada_cols = w_ada.shape[2]
    wa2d = w_ada[0]

    sc_loc = conv_w.shape[2]
    pack_a = jnp.zeros((8, d), F32)
    pack_a = pack_a.at[0].set(_silu(c[0]))
    pack_a = pack_a.at[1, :4 * sc_loc].set(conv_w[0].reshape(-1))
    pack_a = pack_a.at[2, :2 * sc_loc].set(lru_ba[0].reshape(-1))
    pack_a = pack_a.at[3, :2 * sc_loc].set(lru_bx[0].reshape(-1))
    pack_a = pack_a.at[4, :2 * sc_loc].set(lru_lambda[0].reshape(-1))
    b_cols = lax.dynamic_slice(b_ada, (0, me * ada_cols), (1, ada_cols))
    all_a, s16, mod_parts = _mod_exchange(pack_a, _silu(c_ctx)[None, :], wa2d, b_cols)

    def unshard(row, k):
        return all_a[:, row, :k * sc_loc].reshape(N_DEV, k, sc_loc).transpose(1, 0, 2).reshape(k, N_DEV * sc_loc)

    conv_w_full = unshard(1, 4)
    ba_full, bx_full, lam_full = unshard(2, 2), unshard(3, 2), unshard(4, 2)

    mod_all = mod_parts.transpose(1, 0, 2).reshape(16, N_DEV * ada_cols)
    mod_me = lax.dynamic_slice(mod_all, (me, 0), (1, 6 * d)).reshape(6, d)
    mod_c = mod_all[8].reshape(6, d)
    modrows = jnp.concatenate([mod_c[0:2], mod_me], axis=0)

    lg = jax.nn.log_sigmoid(ret_decay[0])
    sp = jax.nn.softplus(-lam_full)
    wg = [jnp.concatenate([_block_diag(lru_wa[0, dd]), _block_diag(lru_wx[0, dd])], axis=1).astype(BF16) for dd in (0, 1)]
    bg = [jnp.concatenate([ba_full[dd], bx_full[dd]])[None, :] for dd in (0, 1)]
    cosf, sins = _rot_tables(l_len, t_len)

    (hn, hnt), win_g = _norm1_fwd(ctx2d, x2d, norm1_g, modrows, cb, ride=("gather", w_in[0].astype(BF16)))
    (qkv, p), w2_g = _mm_in(hn, win_g, cosf, sins, ride=("gather", w_mlp2[0].astype(BF16)))
    w2_g = w2_g.reshape(1, 4 * d, d)
    (o0, o1, sp0, sp1), w1_g = _ret_fwd(qkv, lg, cc, ride=("gather", w_mlp1[0].astype(BF16)))
    o, s_prev = [o0, o1], [sp0, sp1]
    (h0, cin0), wout_g = _lru_fwd(p, wg[0], bg[0], sp[0:1], conv_w_full, conv_b, 0, cb, ride=("gather", w_out[0].astype(BF16)))
    wout_g = wout_g.reshape(1, d, d)
    (h1, cin1), _ = _lru_fwd(p, wg[1], bg[1], sp[1:2], conv_w_full, conv_b, 1, cb)
    h, cin = [h0, h1], [cin0, cin1]
    mix = _mix_fwd(o[0], o[1], p, h[0], h[1], cb, t_len)
    y, x1, h2, h2t = _mm_nn(mix, wout_g, F32, "mm_out_norm2", tm=TL, epi=_res_norm2_epilogue(x2d, norm2_g, modrows, TL))
    r = _mm_nn(h2, w1_g, BF16, "mm_mlp1", relu_out=True, tm=TL)
    dx2, dz, dzt, facc = _mm_nn(r, w2_g, F32, "mm_mlp2_final", square_lhs=True, tm=TL, vmem_mb=58,
                                epi=_final_epilogue(x1, tgt2d, final_g[None, :], modrows, TL))

    du = _mm_nt(dz, w2_g, BF16, "mm_da2", relu_mul=r, tm=TL, vmem_mb=58)
    gw2_lo, gw2_hi = _mm_wgrad(dzt, r, "mm_dw2", w2_g.shape[1] // N_DEV, BF16, transpose_out=True, square_rhs=True,
                               halves=True)
    gw1_lo, gw1_hi = _mm_wgrad(h2t, du, "mm_dw1", w1_g.shape[2], BF16, halves=True)
    (dx1, dy, n2acc), gw2_lo_all = _mm_nt(du, w1_g, F32, "mm_dh2_norm2", ride=("a2a", gw2_lo), tm=TL, vmem_mb=60,
                                          epi=_bwd_norm2_epilogue(x1, dx2, y, norm2_g, modrows, TL))
    gwo = _mm_tn(mix, dy, "mm_dwout", False, d, BF16, 512).reshape(N_DEV, -1, d)
    (do, dg, dgate, dhs), gwo_all = _mm_nt(dy, wout_g, F32, "mm_dmix_mix", ride=("a2a", gwo),
                                           epi=_mix_bwd_epilogue(o[0], o[1], p, h[0], h[1], cb))
    dxcs, dwgs, laccs, rides, got = [], [], [], [("a2a", gw1_hi), ("a2a", gw2_hi)], []
    (dq0, dk0, dv0, dq1, dk1, dv1, dlg_lanes), gw1_lo_all = _ret_bwd(qkv, lg, do, s_prev, cc, ride=("a2a", gw1_lo))
    dqs, dks, dvs = [dq0, dq1], [dk0, dk1], [dv0, dv1]
    for dd in (0, 1):
        (dxc_, dwg_, lacc_), got_ = _lru_bwd(p, wg[dd], bg[dd], sp[dd:dd + 1], conv_w_full, conv_b, h[dd], cin[dd], dhs, dd, cb,
                                             ride=rides[dd])
        dxcs.append(dxc_); dwgs.append(dwg_); laccs.append(lacc_); got.append(got_)
    gw1_hi_all, gw2_hi_all = got
    gw1_all, gw2_all = [gw1_lo_all, gw1_hi_all], [gw2_lo_all, gw2_hi_all]
    (dp, cacc), _ = _assemble_dp(dqs, dks, dvs, dg, dgate, dxcs, p, conv_w_full, cosf, sins, cb)
    pack_b = jnp.concatenate([n2acc, facc, cacc, laccs[0], laccs[1], dlg_lanes, dwgs[0], dwgs[1]], axis=0)
    gwi, all_b = _mm_wgrad(hnt, dp, "mm_dwin", win_g.shape[2], BF16, ride=("gather", pack_b))
    (grad_x, n1acc), gwi_all = _mm_nt(dp, win_g, F32, "mm_dhn_norm1", ride=("a2a", gwi),
                                      epi=_bwd_norm1_epilogue(ctx2d, x2d, dx1, norm1_g, modrows, cb))
    all_n1 = _all_gather_small(n1acc, "gather_norm1_grads")
    tot = _sum_parts(all_b, "sum_small_grads")
    t_n1 = _sum_parts(all_n1, "sum_norm1_grads")
    t_n2, t_f, t_conv, t_dlg = tot[0:8], tot[8:16], tot[16:24, :LRU_W], tot[40:48, :HEAD_DIM]
    t_l = [tot[24:32], tot[32:40]]
    t_dwd = [tot[48:48 + LRU_BD], tot[48 + LRU_BD:48 + 2 * LRU_BD]]
    loss = (0.5 / d) * jnp.sum(t_f[2])
    t_wa = jnp.stack([_blocks_from_lanes(t_dwd[dd][:, :LRU_W]) for dd in (0, 1)])
    t_wx = jnp.stack([_blocks_from_lanes(t_dwd[dd][:, LRU_W:]) for dd in (0, 1)])
    t_ba = jnp.stack([t_l[dd][0, :LRU_W] for dd in (0, 1)])
    t_bx = jnp.stack([t_l[dd][0, LRU_W:] for dd in (0, 1)])
    t_sp = jnp.stack([t_l[dd][1, :LRU_W] for dd in (0, 1)])
    dm_rows = jnp.stack([all_n1[:, 2, :], all_n1[:, 3, :], all_b[:, 3, :], all_b[:, 0, :], all_b[:, 1, :], all_b[:, 9, :]],
                        axis=1).reshape(N_DEV, 6 * d)
    dm_c = jnp.concatenate([t_n1[0], t_n1[1], jnp.zeros((4 * d,), F32)])
    dm16 = jnp.zeros((16, 6 * d), F32).at[0:8].set(dm_rows).at[8].set(dm_c)
    g_b_ada = jnp.sum(dm16, axis=0)[None, :]
    dm_cols = lax.dynamic_slice(dm16, (0, me * ada_cols), (16, ada_cols))
    g_w_ada, ds16 = _ada_bwd(s16, dm_cols, wa2d)
    ds_all = _all_gather_small(ds16[8:16], "gather_dsilu")
    dsilu_cc = _sum_parts(ds_all, "sum_dsilu")[0]
    sg_cc = jax.nn.sigmoid(c_ctx)
    g_c_ctx = dsilu_cc * (sg_cc * (1.0 + c_ctx * (1.0 - sg_cc)))

    g_ret_decay = jnp.sum(t_dlg, axis=-1).reshape(2, HEADS) * jax.nn.sigmoid(-ret_decay[0])
    g_lambda_full = -t_sp * jax.nn.sigmoid(-lam_full)

    def my_cols(full):
        return lax.dynamic_slice(full, (0, me * sc_loc), (full.shape[0], sc_loc))

    small_g = dict(
        c_ctx=g_c_ctx[None], b_ada=g_b_ada, norm1_g=t_n1[4:5], norm2_g=t_n2[2:3], ret_decay=g_ret_decay,
        conv_w=my_cols(t_conv[0:4]), conv_b=t_conv[4:5], lru_wa=t_wa.reshape(-1, LRU_BD), lru_ba=my_cols(t_ba),
        lru_wx=t_wx.reshape(-1, LRU_BD), lru_bx=my_cols(t_bx), lru_lambda=my_cols(g_lambda_full), final_g=t_f[0:1])
    small = dict(
        c_ctx=(c_ctx, m_c_ctx, v_c_ctx), b_ada=(b_ada, m_b_ada, v_b_ada), norm1_g=(norm1_g, m_norm1_g, v_norm1_g),
        norm2_g=(norm2_g, m_norm2_g, v_norm2_g), ret_decay=(ret_decay, m_ret_decay, v_ret_decay),
        conv_w=(conv_w, m_conv_w, v_conv_w), conv_b=(conv_b, m_conv_b, v_conv_b), lru_wa=(lru_wa, m_lru_wa, v_lru_wa),
        lru_ba=(lru_ba, m_lru_ba, v_lru_ba), lru_wx=(lru_wx, m_lru_wx, v_lru_wx), lru_bx=(lru_bx, m_lru_bx, v_lru_bx),
        lru_lambda=(lru_lambda, m_lru_lambda, v_lru_lambda), final_g=(final_g, m_final_g, v_final_g))
    names = list(small)
    items = [(small_g[k],) + tuple(a.reshape(small_g[k].shape) for a in small[k]) for k in names]
    res = {}
    for k, it, (d_, m_, v_) in zip(names, items, _adam_many(items, "adam_small")):
        shape = small[k][0].shape
        res[k] = tuple(a.reshape(shape) for a in (it[0], d_, m_, v_))

    def big(parts, w, m, v, name):
        out = _sum_adam(parts, w[0], m[0], v[0], name)
        return tuple(a[None] for a in out)

    res["w_ada"] = big([g_w_ada[None]], w_ada, m_w_ada, v_w_ada, "adam_w_ada")
    res["w_in"] = big([gwi_all], w_in, m_w_in, v_w_in, "adam_w_in")
    res["w_out"] = big([gwo_all], w_out, m_w_out, v_w_out, "adam_w_out")
    res["w_mlp1"] = big(gw1_all, w_mlp1, m_w_mlp1, v_w_mlp1, "adam_w_mlp1")
    res["w_mlp2"] = big(gw2_all, w_mlp2, m_w_mlp2, v_w_mlp2, "adam_w_mlp2")

    order = ["c_ctx", "w_ada", "b_ada", "norm1_g", "norm2_g", "w_in", "ret_decay", "conv_w", "conv_b", "lru_wa", "lru_ba",
             "lru_wx", "lru_bx", "lru_lambda", "w_out", "w_mlp1", "w_mlp2", "final_g"]
    outs = [loss, grad_x[None]]
    for j in range(4):
        outs += [res[k][j] for k in order]
    return tuple(outs)
```

```python
import jax
import jax.numpy as jnp
from jax import lax
from jax.experimental import pallas as pl
from jax.experimental.pallas import tpu as pltpu

F32 = jnp.float32
BF16 = jnp.bfloat16
N_DEV = 8
MESH = pl.DeviceIdType.MESH

HEADS = 4
HEAD_DIM = 128
CHUNK = 128
RET_W = HEADS * HEAD_DIM
LRU_W = 512
LRU_BLOCKS = 8
LRU_BD = LRU_W // LRU_BLOCKS
LRU_C = 8.0
EPS = 1e-6
K_SCALE = HEAD_DIM ** -0.5
ROPE_BASE = 10000.0
GRID_W = 64
TM = 256
TL = 512
SUB = 8

ADAM_LR = 0.001
ADAM_B1 = 0.9
ADAM_B2 = 0.999
ADAM_EPS = 1e-08
ADAM_WD = 0.01
ADAM_STEP = 10

COL_G, COL_XR, COL_GATE = 0, 1, 2

R_CSH1, R_CSC1, R_SH1, R_SC1, R_G1, R_SH2, R_SC2, R_G2 = range(8)


def _pcall(body, **kw):
    return pl.pallas_call(body, **kw)


def _cp(vmem_mb=48):
    return pltpu.CompilerParams(vmem_limit_bytes=vmem_mb << 20)


def _sds(shape, dtype):
    return jax.ShapeDtypeStruct(shape, dtype)


def _dot(a, b):
    return jnp.dot(a.astype(BF16), b.astype(BF16), preferred_element_type=F32)


def _dot_nt(a, b):
    return lax.dot_general(a.astype(BF16), b.astype(BF16), (((1,), (1,)), ((), ())), preferred_element_type=F32)


def _dot_tn(a, b):
    return lax.dot_general(a.astype(BF16), b.astype(BF16), (((0,), (0,)), ((), ())), preferred_element_type=F32)


def _sigmoid(x):
    return 0.5 * jnp.tanh(0.5 * x) + 0.5


def _gelu(x):
    return 0.5 * x * (1.0 + jnp.tanh(0.7978845608028654 * (x + 0.044715 * x * x * x)))


def _dgelu(x):
    t = jnp.tanh(0.7978845608028654 * (x + 0.044715 * x * x * x))
    return 0.5 * (1.0 + t) + 0.5 * x * (1.0 - t * t) * 0.7978845608028654 * (1.0 + 3.0 * 0.044715 * x * x)


def _rows_iota(shape):
    return lax.broadcasted_iota(jnp.int32, shape, 0)


def _tile_order(dirn, s, cb, nb):
    if dirn == 0:
        return s
    return jnp.where(s < cb, cb - 1 - s, nb - 1 - (s - cb))


_SEMS = [pltpu.SemaphoreType.DMA((7,)), pltpu.SemaphoreType.DMA((7,)), pltpu.SemaphoreType.DMA(())]
_ANY = pl.BlockSpec(memory_space=pl.ANY)


def _gather_copies(x_ref, out_ref, send_sems, recv_sems, local_sem):
    mx, my, mc = lax.axis_index("x"), lax.axis_index("y"), lax.axis_index("c")
    me, sibling = (mx, my, mc), (mx, my, 1 - mc)
    chips = [(1 - mx, my), (mx, 1 - my), (1 - mx, 1 - my)]

    def slot(px, py, pc):
        return out_ref.at[4 * px + 2 * py + pc]

    def copy(k, block, to, src=None):
        return pltpu.make_async_remote_copy(
            src_ref=slot(*block) if src is None else src, dst_ref=slot(*block),
            send_sem=send_sems.at[k], recv_sem=recv_sems.at[k], device_id=to, device_id_type=MESH)

    mine = pltpu.make_async_copy(x_ref, slot(*me), local_sem)
    first = [copy(0, me, sibling, src=x_ref)] + [copy(1 + j, me, (*chip, mc), src=x_ref) for j, chip in enumerate(chips)]
    passed = [copy(4 + j, (*chip, mc), sibling) for j, chip in enumerate(chips)]
    recv_ici = [copy(1 + j, (*chip, mc), me) for j, chip in enumerate(chips)]
    recv_d2d = [copy(0, sibling, me)] + [copy(4 + j, (*chip, 1 - mc), me) for j, chip in enumerate(chips)]
    return mine, first, passed, recv_ici, recv_d2d


def _gather_start(*refs):
    mine, first, _, _, _ = _gather_copies(*refs)
    mine.start()
    for cp in first:
        cp.start()


def _gather_pass_on(*refs):
    _, _, passed, recv_ici, _ = _gather_copies(*refs)
    for landed, onward in zip(recv_ici, passed):
        landed.wait_recv()
        onward.start()


def _gather_finish(*refs):
    mine, first, passed, _, recv_d2d = _gather_copies(*refs)
    for landed in recv_d2d:
        landed.wait_recv()
    for cp in first + passed:
        cp.wait_send()
    mine.wait()


def _a2a_copies(g_ref, out_ref, send_sems, recv_sems, local_sem):
    mx, my, mc = lax.axis_index("x"), lax.axis_index("y"), lax.axis_index("c")
    me = 4 * mx + 2 * my + mc
    mine = pltpu.make_async_copy(g_ref.at[me], out_ref.at[me], local_sem)
    copies = []
    for k in range(1, N_DEV):
        px = 1 - mx if (k >> 2) & 1 else mx
        py = 1 - my if (k >> 1) & 1 else my
        pc = 1 - mc if k & 1 else mc
        copies.append(pltpu.make_async_remote_copy(
            src_ref=g_ref.at[4 * px + 2 * py + pc], dst_ref=out_ref.at[me],
            send_sem=send_sems.at[k - 1], recv_sem=recv_sems.at[k - 1],
            device_id=(px, py, pc), device_id_type=MESH))
    return mine, copies


def _a2a_start(*refs):
    mine, copies = _a2a_copies(*refs)
    mine.start()
    for cp in copies:
        cp.start()


def _a2a_finish(*refs):
    mine, copies = _a2a_copies(*refs)
    for cp in copies:
        cp.wait()
    mine.wait()


_EXCHANGES = {"gather": (_gather_start, _gather_finish), "a2a": (_a2a_start, _a2a_finish)}
PASS_ON_LEAD = 3


def _exchange_shape(kind, src):
    return _sds((N_DEV,) + src.shape if kind == "gather" else src.shape, src.dtype)


def _all_gather_small(x, name):
    def body(x_ref, out_ref, send_sems, recv_sems, local_sem):
        mx, my, mc = lax.axis_index("x"), lax.axis_index("y"), lax.axis_index("c")
        me = 4 * mx + 2 * my + mc
        mine = pltpu.make_async_copy(x_ref, out_ref.at[me], local_sem)
        mine.start()
        copies = []
        for k in range(1, N_DEV):
            peer = (1 - mx if (k >> 2) & 1 else mx, 1 - my if (k >> 1) & 1 else my, 1 - mc if k & 1 else mc)
            copies.append(pltpu.make_async_remote_copy(
                src_ref=x_ref, dst_ref=out_ref.at[me], send_sem=send_sems.at[k - 1], recv_sem=recv_sems.at[k - 1],
                device_id=peer, device_id_type=MESH))
            copies[-1].start()
        for cp in copies:
            cp.wait()
        mine.wait()

    return _pcall(body, name=name, out_shape=_exchange_shape("gather", x), in_specs=[_ANY], out_specs=_ANY,
                  scratch_shapes=list(_SEMS))(x)


def _pcall_ride(body, ride, args, *, name, grid, in_specs, out_specs, out_shape, scratch_shapes=(), compiler_params=None):
    if ride is None:
        out = _pcall(body, name=name, grid=grid, in_specs=in_specs, out_specs=out_specs, out_shape=out_shape,
                     scratch_shapes=list(scratch_shapes), compiler_params=compiler_params)(*args)
        return out, None
    kind, src = ride
    start, finish = _EXCHANGES[kind]
    single = not isinstance(out_shape, (tuple, list))
    out_specs_t = (out_specs,) if single else tuple(out_specs)
    out_shape_t = (out_shape,) if single else tuple(out_shape)
    n_in, n_out, n_sc = len(in_specs), len(out_shape_t), len(scratch_shapes)

    def wrapped(*refs):
        ins, src_ref = refs[:n_in], refs[n_in]
        outs, dst_ref = refs[n_in + 1:n_in + 1 + n_out], refs[n_in + 1 + n_out]
        scratch = refs[n_in + 2 + n_out:n_in + 2 + n_out + n_sc]
        sems = refs[n_in + 2 + n_out + n_sc:]
        first = pl.program_id(0) == 0
        last = pl.program_id(0) == grid[0] - 1
        for ax in range(1, len(grid)):
            first = jnp.logical_and(first, pl.program_id(ax) == 0)
            last = jnp.logical_and(last, pl.program_id(ax) == grid[ax] - 1)

        @pl.when(first)
        def _():
            start(src_ref, dst_ref, *sems)

        body(*ins, *outs, *scratch)

        if kind == "gather":
            @pl.when(pl.program_id(0) == max(grid[0] - PASS_ON_LEAD, 0))
            def _():
                _gather_pass_on(src_ref, dst_ref, *sems)

        @pl.when(last)
        def _():
            finish(src_ref, dst_ref, *sems)

    res = _pcall(wrapped, name=name, grid=grid, in_specs=list(in_specs) + [_ANY], out_specs=out_specs_t + (_ANY,),
                 out_shape=out_shape_t + (_exchange_shape(kind, src),),
                 scratch_shapes=list(scratch_shapes) + list(_SEMS), compiler_params=compiler_params)(*args, src)
    return (res[0] if single else tuple(res[:-1])), res[-1]


class _Epilogue:
    def __init__(self, fn, args, in_specs, out_specs, out_shape, steps=None, lhs_map=None, delay=0):
        self.fn, self.args, self.in_specs, self.out_specs, self.out_shape = fn, tuple(args), list(in_specs), out_specs, out_shape
        self.steps, self.lhs_map, self.delay = steps, lhs_map, delay


def _mm_nn(a, w, out_dtype, name, square_lhs=False, relu_out=False, ride=None, tm=TM, epi=None, vmem_mb=48):
    m, k = a.shape
    nb, _, bn = w.shape
    tm = min(tm, m)
    delay = 0 if epi is None else epi.delay
    steps = m // tm

    def body(*refs):
        a_ref, w_ref = refs[:2]
        av = a_ref[...]
        if square_lhs:
            av = av * av
        if delay:
            assert nb == 1
            held = refs[-1]

            @pl.when(pl.program_id(0) == 0)
            def _():
                held[...] = jnp.zeros_like(held)

            fresh = jnp.dot(av, w_ref[0], preferred_element_type=F32)
            epi.fn(held[...], slice(0, tm), *refs[2:-1])
            held[...] = fresh
            return
        if epi is not None:
            assert nb == 1
            epi.fn(jnp.dot(av, w_ref[0], preferred_element_type=F32), slice(0, tm), *refs[2:])
            return
        for j in range(nb):
            r = jnp.dot(av, w_ref[j], preferred_element_type=F32)
            if relu_out:
                r = jnp.maximum(r, 0.0)
            refs[2][:, j * bn:(j + 1) * bn] = r.astype(out_dtype)

    lhs_map = (lambda i: (i, 0)) if not delay else (lambda i: (jnp.minimum(i, steps - 1), 0))
    in_specs = [pl.BlockSpec((tm, k), lhs_map), pl.BlockSpec((nb, k, bn), lambda i: (0, 0, 0))]
    scratch = []
    if epi is None:
        args, out_specs, out_shape = (a, w), pl.BlockSpec((tm, nb * bn), lambda i: (i, 0)), _sds((m, nb * bn), out_dtype)
    else:
        args, out_specs, out_shape = (a, w) + epi.args, epi.out_specs, epi.out_shape
        in_specs += epi.in_specs
        if delay:
            scratch.append(pltpu.VMEM((tm, bn), F32))
    out, ex = _pcall_ride(body, ride, args, name=name, grid=(steps + delay,), in_specs=in_specs, out_specs=out_specs,
                          out_shape=out_shape, scratch_shapes=scratch, compiler_params=_cp(vmem_mb))
    return out if ride is None else (out, ex)


def _mm_nt(dy, w, out_dtype, name, relu_mul=None, ride=None, tm=TM, epi=None, vmem_mb=48):
    m = dy.shape[0]
    nb, k, bn = w.shape
    tm = min(tm, m)
    n_extra = (0 if relu_mul is None else 1) + (0 if epi is None else len(epi.args))

    delay = 0 if epi is None else epi.delay

    def body(*refs):
        dy_ref, w_ref = refs[:2]
        scratch = refs[-1 - delay:]
        extra, outs, wt = refs[2:2 + n_extra], refs[2 + n_extra:-1 - delay], scratch[0]

        @pl.when(pl.program_id(0) == 0)
        def _():
            for j in range(nb):
                wt[j * bn:(j + 1) * bn, :] = w_ref[j].T
            if delay:
                scratch[1][...] = jnp.zeros_like(scratch[1])

        if delay:
            held = scratch[1]
            fresh = jnp.dot(dy_ref[...], wt[...], preferred_element_type=F32)
            epi.fn(held[...], slice(0, tm), *extra, *outs)
            held[...] = fresh
            return
        acc = jnp.dot(dy_ref[...], wt[...], preferred_element_type=F32)
        if epi is not None:
            epi.fn(acc, slice(0, tm), *extra, *outs)
            return
        if relu_mul is not None:
            acc = acc * (2.0 * extra[0][...].astype(F32))
        outs[0][...] = acc.astype(out_dtype)

    steps = m // tm if epi is None or epi.steps is None else epi.steps
    lhs_tile = (lambda i: (i, 0)) if epi is None or epi.lhs_map is None else epi.lhs_map
    lhs_map = lhs_tile if not delay else (lambda i: lhs_tile(jnp.minimum(i, steps - 1)))
    in_specs = [pl.BlockSpec((tm, nb * bn), lhs_map), pl.BlockSpec((nb, k, bn), lambda i: (0, 0, 0))]
    args = [dy, w]
    if relu_mul is not None:
        in_specs.append(pl.BlockSpec((tm, k), lambda i: (i, 0)))
        args.append(relu_mul)
    scratch = [pltpu.VMEM((nb * bn, k), BF16)]
    if epi is None:
        out_specs, out_shape = pl.BlockSpec((tm, k), lambda i: (i, 0)), _sds((m, k), out_dtype)
    else:
        args += list(epi.args)
        in_specs += epi.in_specs
        out_specs, out_shape = epi.out_specs, epi.out_shape
        if delay:
            scratch.append(pltpu.VMEM((tm, k), F32))
    out, ex = _pcall_ride(
        body, ride, args, name=name, grid=(steps + delay,), in_specs=in_specs, out_specs=out_specs, out_shape=out_shape,
        scratch_shapes=scratch, compiler_params=_cp(vmem_mb))
    return out if ride is None else (out, ex)


def _mm_tn(a, b, name, col_blocks, block, out_dtype, tm, square_lhs=False):
    m, k = a.shape
    nn = b.shape[1]
    steps = m // tm
    if col_blocks:
        nblk, acc_shape = nn // block, (k, block)
        a_spec = pl.BlockSpec((tm, k), lambda j, s: (s, 0))
        b_spec = pl.BlockSpec((tm, block), lambda j, s: (s, j))
    else:
        nblk, acc_shape = k // block, (block, nn)
        a_spec = pl.BlockSpec((tm, block), lambda j, s: (s, j))
        b_spec = pl.BlockSpec((tm, nn), lambda j, s: (s, 0))

    def body(a_ref, b_ref, o_ref, acc):
        s = pl.program_id(1)

        @pl.when(s == 0)
        def _():
            acc[...] = jnp.zeros_like(acc)

        av = a_ref[...]
        if square_lhs:
            av = av.astype(F32)
            av = (av * av).astype(BF16)
        acc[...] += _dot_tn(av, b_ref[...])

        @pl.when(s == steps - 1)
        def _():
            o_ref[...] = acc[...].astype(out_dtype)

    return _pcall(
        body, name=name, grid=(nblk, steps), in_specs=[a_spec, b_spec],
        out_specs=pl.BlockSpec((None,) + acc_shape, lambda j, s: (j, 0, 0)),
        out_shape=_sds((nblk,) + acc_shape, out_dtype),
        scratch_shapes=[pltpu.VMEM(acc_shape, F32)], compiler_params=_cp(),
    )(a, b)


def _mm_wgrad(at, b, name, bn, out_dtype, transpose_out=False, square_rhs=False, halves=False, ride=None):
    k, m = at.shape
    nblk = b.shape[1] // bn
    rows, cols = (bn, k) if transpose_out else (k, bn)
    nout = 2 if halves else 1
    per = rows // nout

    def body(a_ref, b_ref, *o_refs):
        bv = b_ref[...]
        if square_rhs:
            bv = bv * bv
        r = jnp.dot(a_ref[...], bv, preferred_element_type=F32)
        r = (r.T if transpose_out else r).astype(out_dtype)
        for i, o_ref in enumerate(o_refs):
            o_ref[...] = r[i * per:(i + 1) * per, :]

    out, ex = _pcall_ride(
        body, ride, (at, b), name=name, grid=(nblk,),
        in_specs=[pl.BlockSpec((k, m), lambda j: (0, 0)), pl.BlockSpec((m, bn), lambda j: (0, j))],
        out_specs=tuple(pl.BlockSpec((None, per, cols), lambda j: (j, 0, 0)) for _ in range(nout)),
        out_shape=tuple(_sds((nblk, per, cols), out_dtype) for _ in range(nout)), compiler_params=_cp())
    out = out if halves else out[0]
    return out if ride is None else (out, ex)


def _mm_in(hn, w, cosf, sins, ride):
    m, k = hn.shape
    nb, _, bn = w.shape

    def body(a_ref, w_ref, c_ref, s_ref, qkv_ref, rest_ref, pt):
        av = a_ref[...]
        for j in range(nb):
            pt[:, j * bn:(j + 1) * bn] = jnp.dot(av, w_ref[j], preferred_element_type=F32)
        cf, ss = c_ref[...], s_ref[...]
        for h in range(HEADS):
            sq = slice(h * HEAD_DIM, (h + 1) * HEAD_DIM)
            sk = slice(RET_W + h * HEAD_DIM, RET_W + (h + 1) * HEAD_DIM)
            qkv_ref[:, sq] = _rot(pt[:, sq], cf, ss).astype(BF16)
            qkv_ref[:, sk] = (_rot(pt[:, sk], cf, ss) * K_SCALE).astype(BF16)
        qkv_ref[:, 2 * RET_W:] = pt[:, 2 * RET_W:3 * RET_W].astype(BF16)
        rest_ref[...] = pt[:, 3 * RET_W:]

    tab = pl.BlockSpec((TM, HEAD_DIM), lambda i: (i, 0))
    wide = pl.BlockSpec((TM, 3 * RET_W), lambda i: (i, 0))
    return _pcall_ride(
        body, ride, (hn, w, cosf, sins), name="mm_in", grid=(m // TM,),
        in_specs=[pl.BlockSpec((TM, k), lambda i: (i, 0)), pl.BlockSpec((nb, k, bn), lambda i: (0, 0, 0)), tab, tab],
        out_specs=(wide, wide), out_shape=(_sds((m, 3 * RET_W), BF16), _sds((m, nb * bn - 3 * RET_W), F32)),
        scratch_shapes=[pltpu.VMEM((TM, nb * bn), F32)], compiler_params=_cp())


def _mod_exchange(pack_a, silu_cc, w_ada, b_cols):
    d, cols = pack_a.shape[1], w_ada.shape[1]

    def body(pa_ref, scc_ref, w_ref, b_ref, alla_ref, s16_ref, modp_ref, mp, send1, recv1, send2, recv2):
        mx, my, mc = lax.axis_index("x"), lax.axis_index("y"), lax.axis_index("c")
        me = 4 * mx + 2 * my + mc

        def exchange(src_ref, dst_ref, send, recv):
            copies = []
            for k in range(1, N_DEV):
                peer = (1 - mx if (k >> 2) & 1 else mx, 1 - my if (k >> 1) & 1 else my, 1 - mc if k & 1 else mc)
                copies.append(pltpu.make_async_remote_copy(
                    src_ref=src_ref, dst_ref=dst_ref.at[me], send_sem=send.at[k - 1], recv_sem=recv.at[k - 1],
                    device_id=peer, device_id_type=MESH))
                copies[-1].start()
            dst_ref[me] = src_ref[...]
            for cp in copies:
                cp.wait()

        exchange(pa_ref, alla_ref, send1, recv1)
        s16_ref[...] = jnp.zeros_like(s16_ref)
        for j in range(N_DEV):
            s16_ref[j:j + 1, :] = alla_ref[j, 0:1, :]
        s16_ref[N_DEV:N_DEV + 1, :] = scc_ref[...]
        mp[...] = _dot(s16_ref[...], w_ref[...]) + b_ref[...]
        exchange(mp, modp_ref, send2, recv2)

    vmem = pl.BlockSpec(memory_space=pltpu.VMEM)
    sem7 = pltpu.SemaphoreType.DMA((N_DEV - 1,))
    return _pcall(
        body, name="mod_exchange", in_specs=[vmem] * 4, out_specs=(vmem,) * 3,
        out_shape=(_sds((N_DEV,) + pack_a.shape, F32), _sds((2 * N_DEV, d), F32), _sds((N_DEV, 2 * N_DEV, cols), F32)),
        scratch_shapes=[pltpu.VMEM((2 * N_DEV, cols), F32), sem7, sem7, sem7, sem7], compiler_params=_cp(),
    )(pack_a, silu_cc, w_ada, b_cols)


def _ada_bwd(s16, dm_cols, w_ada):
    def body(s_ref, d_ref, w_ref, gw_ref, ds_ref):
        gw_ref[...] = _dot_tn(s_ref[...], d_ref[...])
        ds_ref[...] = _dot_nt(d_ref[...], w_ref[...])

    return _pcall(body, name="ada_bwd",
                  out_shape=(_sds(w_ada.shape, F32), _sds(s16.shape, F32)), compiler_params=_cp())(s16, dm_cols, w_ada)


def _norm1_fwd(ctx, x, g, modrows, cb, ride=None):
    l_len, d = ctx.shape
    nb = (l_len + x.shape[0]) // TM

    def body(ctx_ref, x_ref, g_ref, m_ref, o_ref, ot_ref):
        is_ctx = pl.program_id(0) < cb
        xin = jnp.where(is_ctx, ctx_ref[...], x_ref[...])
        sh = jnp.where(is_ctx, m_ref[R_CSH1:R_CSH1 + 1, :], m_ref[R_SH1:R_SH1 + 1, :])
        sc = jnp.where(is_ctx, m_ref[R_CSC1:R_CSC1 + 1, :], m_ref[R_SC1:R_SC1 + 1, :])
        ms = jnp.mean(xin * xin, axis=-1, keepdims=True)
        n = xin * lax.rsqrt(ms + EPS) * g_ref[...]
        hn = n * (1.0 + sc) + sh
        o_ref[...] = hn.astype(BF16)
        ot_ref[...] = hn.T.astype(BF16)

    return _pcall_ride(
        body, ride, (ctx, x, g, modrows), name="norm1_fwd", grid=(nb,),
        in_specs=[pl.BlockSpec((TM, d), lambda i: (jnp.minimum(i, cb - 1), 0)),
                  pl.BlockSpec((TM, d), lambda i: (jnp.maximum(i - cb, 0), 0)),
                  pl.BlockSpec((1, d), lambda i: (0, 0)), pl.BlockSpec((8, d), lambda i: (0, 0))],
        out_specs=(pl.BlockSpec((TM, d), lambda i: (i, 0)), pl.BlockSpec((d, TM), lambda i: (0, i))),
        out_shape=(_sds((nb * TM, d), BF16), _sds((d, nb * TM), BF16)), compiler_params=_cp())


def _mix_fwd(o_f, o_b, p, h_f, h_b, cb, t_len):
    def body(of_ref, ob_ref, g_ref, gate_ref, hf_ref, hb_ref, mix_ref):
        o = of_ref[...] + ob_ref[...]
        g = g_ref[...]
        sg = g * _sigmoid(g)
        for hh in range(HEADS):
            sl = slice(hh * HEAD_DIM, (hh + 1) * HEAD_DIM)
            oh = o[:, sl]
            yc = oh - jnp.mean(oh, axis=-1, keepdims=True)
            var = jnp.mean(yc * yc, axis=-1, keepdims=True)
            mix_ref[:, sl] = (sg[:, sl] * (yc * lax.rsqrt(var + EPS))).astype(BF16)
        mix_ref[:, RET_W:] = ((hf_ref[...] + hb_ref[...]) * _gelu(gate_ref[...])).astype(BF16)

    row = lambda i: (i + cb, 0)
    return _pcall(
        body, name="mix_fwd", grid=(t_len // TM,),
        in_specs=[pl.BlockSpec((TM, RET_W), row), pl.BlockSpec((TM, RET_W), row),
                  pl.BlockSpec((TM, RET_W), lambda i: (i + cb, COL_G)), pl.BlockSpec((TM, LRU_W), lambda i: (i + cb, COL_GATE)),
                  pl.BlockSpec((TM, LRU_W), row), pl.BlockSpec((TM, LRU_W), row)],
        out_specs=pl.BlockSpec((TM, RET_W + LRU_W), lambda i: (i, 0)),
        out_shape=_sds((t_len, RET_W + LRU_W), BF16), compiler_params=_cp(),
    )(o_f, o_b, p, p, h_f, h_b)


def _res_norm2_epilogue(x, g, modrows, tm):
    t_len, d = x.shape
    tm = min(tm, t_len)

    def fn(y, rows, x_ref, g_ref, m_ref, y_ref, x1_ref, h2_ref, h2t_ref):
        y_ref[rows, :] = y
        x1 = x_ref[rows, :] + m_ref[R_G1:R_G1 + 1, :] * y
        ms = jnp.mean(x1 * x1, axis=-1, keepdims=True)
        n = x1 * lax.rsqrt(ms + EPS) * g_ref[...]
        x1_ref[rows, :] = x1
        h2 = n * (1.0 + m_ref[R_SC2:R_SC2 + 1, :]) + m_ref[R_SH2:R_SH2 + 1, :]
        h2_ref[rows, :] = h2.astype(BF16)
        h2t_ref[:, rows] = h2.T.astype(BF16)

    t = pl.BlockSpec((tm, d), lambda i: (i, 0))
    return _Epilogue(
        fn, (x, g, modrows),
        in_specs=[t, pl.BlockSpec((1, d), lambda i: (0, 0)), pl.BlockSpec((8, d), lambda i: (0, 0))],
        out_specs=(t, t, t, pl.BlockSpec((d, tm), lambda i: (0, i))),
        out_shape=(_sds((t_len, d), F32), _sds((t_len, d), F32), _sds((t_len, d), BF16), _sds((d, t_len), BF16)))


def _tile(i):
    return jnp.maximum(i - 1, 0)


def _zero_at_start(acc_ref, rows):
    @pl.when(pl.program_id(0) == 0)
    def _():
        acc_ref[...] = jnp.zeros_like(acc_ref)


def _final_epilogue(x1, target, fg, modrows, tm):
    t_len, d = x1.shape

    def fn(z, rows, x1_ref, t_ref, fg_ref, m_ref, dx2_ref, dz_ref, dzt_ref, acc_ref):
        _zero_at_start(acc_ref, rows)
        g2 = m_ref[R_G2:R_G2 + 1, :]
        x2 = x1_ref[rows, :] + g2 * z
        rstd = lax.rsqrt(jnp.mean(x2 * x2, axis=-1, keepdims=True) + EPS)
        xh = x2 * rstd
        fg = fg_ref[...]
        e = xh * fg - t_ref[rows, :]
        dy = e * (1.0 / d)
        dxh = dy * fg
        dx2 = rstd * (dxh - xh * jnp.mean(dxh * xh, axis=-1, keepdims=True))
        dx2_ref[rows, :] = dx2
        dz = g2 * dx2
        dz_ref[rows, :] = dz.astype(BF16)
        dzt_ref[:, rows] = dz.T.astype(BF16)
        acc_ref[0:1, :] += jnp.sum(dy * xh, axis=0, keepdims=True)
        acc_ref[1:2, :] += jnp.sum(dx2 * z, axis=0, keepdims=True)
        acc_ref[2:3, :] += jnp.sum(e * e, axis=0, keepdims=True)

    t = pl.BlockSpec((tm, d), lambda i: (i, 0))
    return _Epilogue(
        fn, (x1, target, fg, modrows),
        in_specs=[t, t, pl.BlockSpec((1, d), lambda i: (0, 0)), pl.BlockSpec((8, d), lambda i: (0, 0))],
        out_specs=(t, t, pl.BlockSpec((d, tm), lambda i: (0, i)), pl.BlockSpec((8, d), lambda i: (0, 0))),
        out_shape=(_sds((t_len, d), F32), _sds((t_len, d), BF16), _sds((d, t_len), BF16), _sds((8, d), F32)))


def _bwd_norm2_epilogue(x1, dx2, y, g, modrows, tm):
    t_len, d = x1.shape

    def fn(dh2, rows, x1_ref, dx2_ref, y_ref, g_ref, m_ref, dx1_ref, dy_ref, acc_ref):
        _zero_at_start(acc_ref, rows)
        x1 = x1_ref[rows, :]
        rstd = lax.rsqrt(jnp.mean(x1 * x1, axis=-1, keepdims=True) + EPS)
        xh = x1 * rstd
        gn = g_ref[...]
        dn = dh2 * (1.0 + m_ref[R_SC2:R_SC2 + 1, :])
        dxh = dn * gn
        dx1 = dx2_ref[rows, :] + rstd * (dxh - xh * jnp.mean(dxh * xh, axis=-1, keepdims=True))
        dx1_ref[rows, :] = dx1
        dy_ref[rows, :] = (m_ref[R_G1:R_G1 + 1, :] * dx1).astype(BF16)
        acc_ref[0:1, :] += jnp.sum(dh2, axis=0, keepdims=True)
        acc_ref[1:2, :] += jnp.sum(dh2 * xh * gn, axis=0, keepdims=True)
        acc_ref[2:3, :] += jnp.sum(dn * xh, axis=0, keepdims=True)
        acc_ref[3:4, :] += jnp.sum(dx1 * y_ref[rows, :], axis=0, keepdims=True)

    t = pl.BlockSpec((tm, d), lambda i: (i, 0))
    return _Epilogue(
        fn, (x1, dx2, y, g, modrows),
        in_specs=[t, t, t, pl.BlockSpec((1, d), lambda i: (0, 0)), pl.BlockSpec((8, d), lambda i: (0, 0))],
        out_specs=(t, t, pl.BlockSpec((8, d), lambda i: (0, 0))),
        out_shape=(_sds((t_len, d), F32), _sds((t_len, d), BF16), _sds((8, d), F32)))


def _mix_bwd_epilogue(o_f, o_b, p, h_f, h_b, cb):
    n = o_f.shape[0]
    tile = _tile

    def fn(dm, rows, of_ref, ob_ref, g_ref, gate_ref, hf_ref, hb_ref, do_ref, dg_ref, dgate_ref, dhs_ref):
        keep = jnp.where(pl.program_id(0) - 1 < cb, 0.0, 1.0)
        dm = dm * keep
        o = of_ref[rows, :] + ob_ref[rows, :]
        g = g_ref[rows, :]
        s = _sigmoid(g)
        sg = g * s
        dsg = s * (1.0 + g * (1.0 - s))
        for hh in range(HEADS):
            sl = slice(hh * HEAD_DIM, (hh + 1) * HEAD_DIM)
            oh = o[:, sl]
            yc = oh - jnp.mean(oh, axis=-1, keepdims=True)
            rs = lax.rsqrt(jnp.mean(yc * yc, axis=-1, keepdims=True) + EPS)
            gn = yc * rs
            dret = dm[:, sl]
            dgn = dret * sg[:, sl]
            dg_ref[rows, sl] = (dret * gn * dsg[:, sl]).astype(BF16)
            do_ref[rows, sl] = (rs * (dgn - jnp.mean(dgn, axis=-1, keepdims=True)
                                      - gn * jnp.mean(dgn * gn, axis=-1, keepdims=True))).astype(BF16)
        dlru = dm[:, RET_W:]
        gate = gate_ref[rows, :]
        dhs_ref[rows, :] = dlru * _gelu(gate)
        dgate_ref[rows, :] = (dlru * (hf_ref[rows, :] + hb_ref[rows, :]) * _dgelu(gate)).astype(BF16)

    t = pl.BlockSpec((TM, RET_W), lambda i: (tile(i), 0))
    return _Epilogue(
        fn, (o_f, o_b, p, p, h_f, h_b),
        in_specs=[t, t, pl.BlockSpec((TM, RET_W), lambda i: (tile(i), COL_G)),
                  pl.BlockSpec((TM, LRU_W), lambda i: (tile(i), COL_GATE)), t, t],
        out_specs=(t, t, t, t), out_shape=(_sds((n, RET_W), BF16),) * 3 + (_sds((n, RET_W), F32),),
        steps=n // TM, lhs_map=lambda i: (jnp.maximum(i - cb, 0), 0), delay=1)


def _bwd_norm1_epilogue(ctx, x, dx1, g, modrows, cb):
    t_len, d = x.shape

    def fn(dh, rows, ctx_ref, x_ref, dx1_ref, g_ref, m_ref, gx_ref, acc_ref):
        is_ctx = pl.program_id(0) < cb
        _zero_at_start(acc_ref, rows)
        xin = jnp.where(is_ctx, ctx_ref[rows, :], x_ref[rows, :])
        sc = jnp.where(is_ctx, m_ref[R_CSC1:R_CSC1 + 1, :], m_ref[R_SC1:R_SC1 + 1, :])
        rstd = lax.rsqrt(jnp.mean(xin * xin, axis=-1, keepdims=True) + EPS)
        xh = xin * rstd
        gn = g_ref[...]
        dn = dh * (1.0 + sc)
        dxh = dn * gn
        gx_ref[rows, :] = dx1_ref[rows, :] + rstd * (dxh - xh * jnp.mean(dxh * xh, axis=-1, keepdims=True))
        s0 = jnp.sum(dh, axis=0, keepdims=True)
        s1 = jnp.sum(dh * xh * gn, axis=0, keepdims=True)
        acc_ref[4:5, :] += jnp.sum(dn * xh, axis=0, keepdims=True)
        acc_ref[0:1, :] += jnp.where(is_ctx, s0, 0.0)
        acc_ref[1:2, :] += jnp.where(is_ctx, s1, 0.0)
        acc_ref[2:3, :] += jnp.where(is_ctx, 0.0, s0)
        acc_ref[3:4, :] += jnp.where(is_ctx, 0.0, s1)

    lat = pl.BlockSpec((TM, d), lambda i: (jnp.maximum(i - cb, 0), 0))
    return _Epilogue(
        fn, (ctx, x, dx1, g, modrows),
        in_specs=[pl.BlockSpec((TM, d), lambda i: (jnp.minimum(i, cb - 1), 0)),
                  lat, lat, pl.BlockSpec((1, d), lambda i: (0, 0)), pl.BlockSpec((8, d), lambda i: (0, 0))],
        out_specs=(lat, pl.BlockSpec((8, d), lambda i: (0, 0))),
        out_shape=(_sds((t_len, d), F32), _sds((8, d), F32)))


def _rot(x, cf, ss):
    return x * cf + pltpu.roll(x, HEAD_DIM // 2, 1) * ss


def _decay_exponents(dirn):
    ii = lax.broadcasted_iota(jnp.int32, (CHUNK, CHUNK), 0)
    jj = lax.broadcasted_iota(jnp.int32, (CHUNK, CHUNK), 1)
    rel = ii - jj if dirn == 0 else jj - ii
    pos = ii.astype(F32)
    if dirn == 0:
        cq, cs = pos + 1.0, (CHUNK - 1.0) - pos
    else:
        cq, cs = CHUNK - pos, pos
    return rel, jnp.maximum(rel, 0).astype(F32), cq, cs


def _store_decay(lg_ref, dec, relf_ref=None):
    for dirn in (0, 1):
        rel, relf, cq, cs = _decay_exponents(dirn)
        if relf_ref is not None:
            relf_ref[dirn] = relf
        for h in range(HEADS):
            lgv = lg_ref[dirn, h]
            wq, ws = jnp.exp(lgv * cq), jnp.exp(lgv * cs)
            dec[dirn, h, 0] = jnp.where(rel >= 0, jnp.exp(lgv * relf), 0.0)
            dec[dirn, h, 1] = wq
            dec[dirn, h, 2] = ws
            if relf_ref is not None:
                dec[dirn, h, 3] = wq * cq
                dec[dirn, h, 4] = ws * cs


def _ret_rows(cc, nc, step_of):
    return [lambda s, dirn=dirn: _tile_order(dirn, step_of(s), cc, nc) for dirn in (0, 1)]


def _ret_in_specs(rows):
    specs = []
    for row in rows:
        specs += [pl.BlockSpec((CHUNK, RET_W), lambda s, o=o, row=row: (row(s), o)) for o in (0, 1, 2)]
    return specs


def _ret_fwd(qkv, lg, cc, ride=None):
    n = qkv.shape[0]
    nc = n // CHUNK
    rows = _ret_rows(cc, nc, lambda s: s)

    def body(lg_ref, q0, k0, v0, q1, k1, v1, o0, o1, sp0, sp1, st, dec):
        @pl.when(pl.program_id(0) == 0)
        def _():
            st[...] = jnp.zeros_like(st)
            _store_decay(lg_ref, dec)

        refs = ((q0, k0, v0, o0, sp0), (q1, k1, v1, o1, sp1))
        chains = [(dirn, h, slice(h * HEAD_DIM, (h + 1) * HEAD_DIM)) for dirn in (0, 1) for h in range(HEADS)]
        scores, cross, update = [], [], []
        for dirn, h, sl in chains:
            q_ref, k_ref, v_ref, _, sp_ref = refs[dirn]
            q, k, v = q_ref[:, sl], k_ref[:, sl], v_ref[:, sl]
            sp = st[dirn, h]
            sp_ref[h] = sp
            scores.append(_dot_nt(q, k))
            cross.append(_dot(q * dec[dirn, h, 1], sp))
            update.append(_dot_tn(k * dec[dirn, h, 2], v))
        masked = [(a * dec[dirn, h, 0]).astype(BF16) for a, (dirn, h, _) in zip(scores, chains)]
        intra = [_dot(sc, refs[dirn][2][:, sl]) for sc, (dirn, h, sl) in zip(masked, chains)]
        for (dirn, h, sl), o_in, o_cr, upd in zip(chains, intra, cross, update):
            refs[dirn][3][:, sl] = o_in + o_cr
            st[dirn, h] = jnp.exp(lg_ref[dirn, h] * CHUNK) * st[dirn, h] + upd

    o_specs = [pl.BlockSpec((CHUNK, RET_W), lambda s, row=row: (row(s), 0)) for row in rows]
    state = pl.BlockSpec((None, HEADS, CHUNK, HEAD_DIM), lambda s: (s, 0, 0, 0))
    return _pcall_ride(
        body, ride, (lg,) + (qkv,) * 6, name="ret_fwd", grid=(nc,),
        in_specs=[pl.BlockSpec(memory_space=pltpu.SMEM)] + _ret_in_specs(rows),
        out_specs=(o_specs[0], o_specs[1], state, state),
        out_shape=(_sds((n, RET_W), F32),) * 2 + (_sds((nc, HEADS, CHUNK, HEAD_DIM), F32),) * 2,
        scratch_shapes=[pltpu.VMEM((2, HEADS, CHUNK, HEAD_DIM), F32), pltpu.VMEM((2, HEADS, 3, CHUNK, CHUNK), F32)],
        compiler_params=_cp())


def _ret_bwd(qkv, lg, do, s_prev, cc, ride=None):
    n = qkv.shape[0]
    nc = n // CHUNK
    rows = _ret_rows(cc, nc, lambda s: nc - 1 - s)

    def body(lg_ref, q0, k0, v0, q1, k1, v1, do0, do1, sp0, sp1, dq0, dk0, dv0, dq1, dk1, dv1, dlg_ref, dst, dec, relf):
        @pl.when(pl.program_id(0) == 0)
        def _():
            dst[...] = jnp.zeros_like(dst)
            dlg_ref[...] = jnp.zeros_like(dlg_ref)
            _store_decay(lg_ref, dec, relf)

        refs = ((q0, k0, v0, do0, sp0, dq0, dk0, dv0), (q1, k1, v1, do1, sp1, dq1, dk1, dv1))
        chains = [(dirn, h, slice(h * HEAD_DIM, (h + 1) * HEAD_DIM)) for dirn in (0, 1) for h in range(HEADS)]

        def tiles(dirn, sl):
            q_ref, k_ref, v_ref, do_ref = refs[dirn][:4]
            return q_ref[:, sl], k_ref[:, sl], v_ref[:, sl], do_ref[:, sl]

        a_s, g1_s, da_s, h1_s = [], [], [], []
        for dirn, h, sl in chains:
            q, k, v, dov = tiles(dirn, sl)
            a_s.append(_dot_nt(q, k))
            g1_s.append(_dot_nt(dov, refs[dirn][4][h]))
            da_s.append(_dot_nt(dov, v))
            h1_s.append(_dot_nt(v, dst[dirn, h]))
        da_s = [da * dec[dirn, h, 0] for da, (dirn, h, _) in zip(da_s, chains)]
        dq_s, dk_s, dv_s, ds_s = [], [], [], []
        for (dirn, h, sl), a, da in zip(chains, a_s, da_s):
            q, k, v, dov = tiles(dirn, sl)
            dq_s.append(_dot(da, k))
            dk_s.append(_dot_tn(da, q))
            dv_s.append(_dot_tn(a * dec[dirn, h, 0], dov) + _dot(k * dec[dirn, h, 2], dst[dirn, h]))
            ds_s.append(_dot_tn(q * dec[dirn, h, 1], dov))
        for (dirn, h, sl), a, g1, da, h1, dq2, dk2, dv, ds2 in zip(chains, a_s, g1_s, da_s, h1_s, dq_s, dk_s, dv_s, ds_s):
            q, k, _, _ = tiles(dirn, sl)
            dq_ref, dk_ref, dv_ref = refs[dirn][5:]
            sp, dsn = refs[dirn][4][h], dst[dirn, h]
            gc = jnp.exp(lg_ref[dirn, h] * CHUNK)
            dq_ref[:, sl] = (g1 * dec[dirn, h, 1] + dq2).astype(BF16)
            dk_ref[:, sl] = (dk2 + h1 * dec[dirn, h, 2]).astype(BF16)
            dv_ref[:, sl] = dv.astype(BF16)
            term = da * a * relf[dirn] + q * g1 * dec[dirn, h, 3] + k * h1 * dec[dirn, h, 4] + sp * dsn * (CHUNK * gc)
            dlg_ref[dirn * HEADS + h:dirn * HEADS + h + 1, 0:HEAD_DIM] += jnp.sum(term, axis=0, keepdims=True)
            dst[dirn, h] = gc * dsn + ds2

    wide = [pl.BlockSpec((CHUNK, RET_W), lambda s, row=row: (row(s), 0)) for row in rows]
    state = pl.BlockSpec((None, HEADS, CHUNK, HEAD_DIM), lambda s: (nc - 1 - s, 0, 0, 0))
    return _pcall_ride(
        body, ride, (lg,) + (qkv,) * 6 + (do, do, s_prev[0], s_prev[1]), name="ret_bwd", grid=(nc,),
        in_specs=[pl.BlockSpec(memory_space=pltpu.SMEM)] + _ret_in_specs(rows) + wide + [state, state],
        out_specs=(wide[0],) * 3 + (wide[1],) * 3 + (pl.BlockSpec((2 * HEADS, 8 * HEAD_DIM), lambda s: (0, 0)),),
        out_shape=(_sds((n, RET_W), BF16),) * 6 + (_sds((2 * HEADS, 8 * HEAD_DIM), F32),),
        scratch_shapes=[pltpu.VMEM((2, HEADS, CHUNK, HEAD_DIM), F32), pltpu.VMEM((2, HEADS, 5, CHUNK, CHUNK), F32),
                        pltpu.VMEM((2, CHUNK, CHUNK), F32)],
        compiler_params=_cp())


def _shift_rows(cur, prev8, next8, k, seg_start, seg_end):
    tm = cur.shape[0]
    rows = _rows_iota(cur.shape)
    if k < 0:
        out = pltpu.roll(cur, -k, 0)
        for j in range(-k):
            halo = jnp.where(seg_start, 0.0, prev8[SUB + k + j:SUB + k + j + 1, :])
            out = jnp.where(rows == j, halo, out)
    else:
        out = pltpu.roll(cur, tm - k, 0)
        for j in range(k):
            halo = jnp.where(seg_end, 0.0, next8[j:j + 1, :])
            out = jnp.where(rows == tm - k + j, halo, out)
    return out


def _seg_flags(t, cb, nb):
    return jnp.logical_or(t == 0, t == cb), jnp.logical_or(t == cb - 1, t == nb - 1)


def _halo_specs(tile_of, n_rows, col):
    per = TM // SUB
    return [pl.BlockSpec((TM, LRU_W), lambda s: (tile_of(s), col)),
            pl.BlockSpec((SUB, LRU_W), lambda s: (jnp.maximum(tile_of(s) * per - 1, 0), col)),
            pl.BlockSpec((SUB, LRU_W), lambda s: (jnp.minimum((tile_of(s) + 1) * per, n_rows // SUB - 1), col))]


def _lru_gates(xr, prev8, next8, seg_start, seg_end, cw_ref, cb_ref, wg_ref, bg_ref, sp_ref):
    xm1 = _shift_rows(xr, prev8, next8, -1, seg_start, seg_end)
    xp1 = _shift_rows(xr, prev8, next8, 1, seg_start, seg_end)
    xp2 = _shift_rows(xr, prev8, next8, 2, seg_start, seg_end)
    xc = cb_ref[...] + xm1 * cw_ref[0:1, :] + xr * cw_ref[1:2, :] + xp1 * cw_ref[2:3, :] + xp2 * cw_ref[3:4, :]
    pre = _dot(xc, wg_ref[...]) + bg_ref[...]
    r = _sigmoid(pre[:, :LRU_W])
    i = _sigmoid(pre[:, LRU_W:])
    la = (-LRU_C) * r * sp_ref[...]
    a = jnp.exp(la)
    th = jnp.tanh(la)
    sq = jnp.sqrt(-2.0 * th / (1.0 - th))
    return xc, r, i, a, sq


def _scan_tile(a, b, ascending, carry, out_ref):
    tm, w = a.shape
    nsub = tm // SUB
    a = a.reshape(nsub, SUB, w)
    b = b.reshape(nsub, SUB, w)
    r8 = lax.broadcasted_iota(jnp.int32, a.shape, 1)
    for k in (1, 2, 4):
        if ascending:
            m = r8 >= k
            a_s, b_s = pltpu.roll(a, k, 1), pltpu.roll(b, k, 1)
        else:
            m = r8 < SUB - k
            a_s, b_s = pltpu.roll(a, SUB - k, 1), pltpu.roll(b, SUB - k, 1)
        b = a * jnp.where(m, b_s, 0.0) + b
        a = a * jnp.where(m, a_s, 1.0)
    c = carry[...]
    for j in range(nsub):
        off = (j if ascending else nsub - 1 - j) * SUB
        hb = a[off // SUB] * c + b[off // SUB]
        out_ref[off:off + SUB, :] = hb
        c = jnp.broadcast_to(hb[SUB - 1:SUB, :] if ascending else hb[0:1, :], c.shape)
    carry[...] = c


def _lru_fwd(p, wg, bg, sp, cw, cbias, dirn, cb, ride=None):
    n = p.shape[0]
    nb = n // TM
    tile_of = lambda s: _tile_order(dirn, s, cb, nb)

    def body(x_ref, xp_ref, xn_ref, wg_ref, bg_ref, sp_ref, cw_ref, cb_ref, h_ref, cin_ref, carry):
        s = pl.program_id(0)

        @pl.when(s == 0)
        def _():
            carry[...] = jnp.zeros_like(carry)

        seg_start, seg_end = _seg_flags(tile_of(s), cb, nb)
        xc, r, i, a, sq = _lru_gates(x_ref[...], xp_ref[...], xn_ref[...], seg_start, seg_end,
                                     cw_ref, cb_ref, wg_ref, bg_ref, sp_ref)
        cin_ref[...] = carry[...]
        _scan_tile(a, sq * (i * xc), dirn == 0, carry, h_ref)

    full = lambda shape: pl.BlockSpec(shape, lambda s: (0,) * len(shape))
    return _pcall_ride(
        body, ride, (p, p, p, wg, bg, sp, cw, cbias), name=f"lru_fwd{dirn}", grid=(nb,),
        in_specs=_halo_specs(tile_of, n, COL_XR) + [full((LRU_W, 2 * LRU_W)), full((1, 2 * LRU_W)), full((1, LRU_W)),
                                               full((4, LRU_W)), full((1, LRU_W))],
        out_specs=(pl.BlockSpec((TM, LRU_W), lambda s: (tile_of(s), 0)),
                   pl.BlockSpec((None, SUB, LRU_W), lambda s: (tile_of(s), 0, 0))),
        out_shape=(_sds((n, LRU_W), F32), _sds((nb, SUB, LRU_W), F32)),
        scratch_shapes=[pltpu.VMEM((SUB, LRU_W), F32)], compiler_params=_cp())


def _lru_bwd(p, wg, bg, sp, cw, cbias, h, cin, dhs, dirn, cb, ride=None):
    n = p.shape[0]
    nb = n // TM
    tile_of = lambda s: _tile_order(dirn, nb - 1 - s, cb, nb)

    def body(x_ref, xp_ref, xn_ref, wg_ref, bg_ref, sp_ref, cw_ref, cb_ref, h_ref, cin_ref, dhs_ref,
             dxc_ref, dwd_ref, acc_ref, carry, mu_sc, dwg_ref):
        s = pl.program_id(0)

        @pl.when(s == 0)
        def _():
            carry[...] = jnp.zeros_like(carry)
            dwg_ref[...] = jnp.zeros_like(dwg_ref)
            acc_ref[...] = jnp.zeros_like(acc_ref)

        seg_start, seg_end = _seg_flags(tile_of(s), cb, nb)
        xc, r, i, a, sq = _lru_gates(x_ref[...], xp_ref[...], xn_ref[...], seg_start, seg_end,
                                     cw_ref, cb_ref, wg_ref, bg_ref, sp_ref)
        rows = _rows_iota(a.shape)
        hv = h_ref[...]
        dh = dhs_ref[...]
        mu_next = carry[0:1, :]
        _scan_tile(a, a * dh, dirn == 1, carry, mu_sc)
        mu = mu_sc[...]
        if dirn == 0:
            hprev = jnp.where(rows == 0, cin_ref[0:1, :], pltpu.roll(hv, 1, 0))
            lam = dh + jnp.where(rows == TM - 1, mu_next, pltpu.roll(mu, TM - 1, 0))
        else:
            hprev = jnp.where(rows == TM - 1, cin_ref[0:1, :], pltpu.roll(hv, TM - 1, 0))
            lam = dh + jnp.where(rows == 0, mu_next, pltpu.roll(mu, 1, 0))
        ds = lam * (i * xc)
        di = lam * (sq * xc)
        dla = lam * hprev * a - ds * (a * a) / jnp.maximum(sq, 1e-20)
        dpr = dla * ((-LRU_C) * sp_ref[...]) * r * (1.0 - r)
        dpi = di * i * (1.0 - i)
        dpre = jnp.concatenate([dpr, dpi], axis=1)
        dxc_ref[...] = lam * (sq * i) + _dot_nt(dpre, wg_ref[...])
        dwg_ref[...] += _dot_tn(xc, dpre)
        acc_ref[0:1, :] += jnp.sum(dpre, axis=0, keepdims=True)
        acc_ref[1:2, 0:LRU_W] += jnp.sum(dla * ((-LRU_C) * r), axis=0, keepdims=True)

        @pl.when(s == nb - 1)
        def _():
            low = lax.broadcasted_iota(jnp.int32, (LRU_BD, 2 * LRU_BD), 1) < LRU_BD
            for half in (0, LRU_W):
                for m in range(LRU_BLOCKS // 2):
                    lanes = slice(half + 2 * LRU_BD * m, half + 2 * LRU_BD * (m + 1))
                    even = dwg_ref[2 * m * LRU_BD:(2 * m + 1) * LRU_BD, lanes]
                    odd = dwg_ref[(2 * m + 1) * LRU_BD:(2 * m + 2) * LRU_BD, lanes]
                    dwd_ref[:, lanes] = jnp.where(low, even, odd)

    full = lambda shape: pl.BlockSpec(shape, lambda s: (0,) * len(shape))
    tile = pl.BlockSpec((TM, LRU_W), lambda s: (tile_of(s), 0))
    return _pcall_ride(
        body, ride, (p, p, p, wg, bg, sp, cw, cbias, h, cin, dhs), name=f"lru_bwd{dirn}", grid=(nb,),
        in_specs=_halo_specs(tile_of, n, COL_XR) + [full((LRU_W, 2 * LRU_W)), full((1, 2 * LRU_W)), full((1, LRU_W)),
                                               full((4, LRU_W)), full((1, LRU_W)), tile,
                                               pl.BlockSpec((None, SUB, LRU_W), lambda s: (tile_of(s), 0, 0)), tile],
        out_specs=(tile, full((LRU_BD, 2 * LRU_W)), full((8, 2 * LRU_W))),
        out_shape=(_sds((n, LRU_W), F32), _sds((LRU_BD, 2 * LRU_W), F32), _sds((8, 2 * LRU_W), F32)),
        scratch_shapes=[pltpu.VMEM((SUB, LRU_W), F32), pltpu.VMEM((TM, LRU_W), F32), pltpu.VMEM((LRU_W, 2 * LRU_W), F32)],
        compiler_params=_cp())


def _assemble_dp(dqs, dks, dvs, dg, dgate, dxcs, p, cw, cosf, sins, cb, ride=None):
    n = p.shape[0]
    nb = n // TM
    tile_of = lambda s: s

    def body(dqf, dqb, dkf, dkb, dvf, dvb, dg_ref, dgate_ref, cf, pf, nf, cb_, pb, nb_, x_ref, xp_ref, xn_ref,
             cw_ref, cos_ref, sin_ref, dp_ref, acc_ref):
        s = pl.program_id(0)

        @pl.when(s == 0)
        def _():
            acc_ref[...] = jnp.zeros_like(acc_ref)

        seg_start, seg_end = _seg_flags(s, cb, nb)
        dq = dqf[...].astype(F32) + dqb[...].astype(F32)
        dk = dkf[...].astype(F32) + dkb[...].astype(F32)
        cosv, sinv = cos_ref[...], sin_ref[...]
        for h in range(HEADS):
            sl = slice(h * HEAD_DIM, (h + 1) * HEAD_DIM)
            sk = slice(RET_W + h * HEAD_DIM, RET_W + (h + 1) * HEAD_DIM)
            dp_ref[:, sl] = (dq[:, sl] * cosv + pltpu.roll(dq[:, sl] * sinv, HEAD_DIM // 2, 1)).astype(BF16)
            dp_ref[:, sk] = ((dk[:, sl] * cosv + pltpu.roll(dk[:, sl] * sinv, HEAD_DIM // 2, 1)) * K_SCALE).astype(BF16)
        dp_ref[:, 2 * RET_W:3 * RET_W] = (dvf[...].astype(F32) + dvb[...].astype(F32)).astype(BF16)
        dp_ref[:, 3 * RET_W:4 * RET_W] = dg_ref[...].astype(BF16)
        dxc = cf[...] + cb_[...]
        dprev = pf[...] + pb[...]
        dnext = nf[...] + nb_[...]
        dxr = (_shift_rows(dxc, dprev, dnext, 1, seg_start, seg_end) * cw_ref[0:1, :] + dxc * cw_ref[1:2, :]
               + _shift_rows(dxc, dprev, dnext, -1, seg_start, seg_end) * cw_ref[2:3, :]
               + _shift_rows(dxc, dprev, dnext, -2, seg_start, seg_end) * cw_ref[3:4, :])
        dp_ref[:, 4 * RET_W:4 * RET_W + LRU_W] = dxr.astype(BF16)
        dp_ref[:, 4 * RET_W + LRU_W:] = dgate_ref[...].astype(BF16)
        xr, xp, xn = x_ref[...], xp_ref[...], xn_ref[...]
        for j, k in enumerate((-1, 0, 1, 2)):
            xs = xr if k == 0 else _shift_rows(xr, xp, xn, k, seg_start, seg_end)
            acc_ref[j:j + 1, 0:LRU_W] += jnp.sum(dxc * xs, axis=0, keepdims=True)
        acc_ref[4:5, 0:LRU_W] += jnp.sum(dxc, axis=0, keepdims=True)

    t = pl.BlockSpec((TM, RET_W), lambda s: (s, 0))
    args = (dqs[0], dqs[1], dks[0], dks[1], dvs[0], dvs[1], dg, dgate, dxcs[0], dxcs[0], dxcs[0], dxcs[1], dxcs[1], dxcs[1],
            p, p, p, cw, cosf, sins)
    tab = pl.BlockSpec((TM, HEAD_DIM), lambda s: (s, 0))
    return _pcall_ride(
        body, ride, args, name="assemble_dp", grid=(nb,),
        in_specs=[t] * 8 + _halo_specs(tile_of, n, 0) * 2 + _halo_specs(tile_of, n, COL_XR)
        + [pl.BlockSpec((4, LRU_W), lambda s: (0, 0)), tab, tab],
        out_specs=(pl.BlockSpec((TM, 4 * RET_W + 2 * LRU_W), lambda s: (s, 0)), pl.BlockSpec((8, 2 * LRU_W), lambda s: (0, 0))),
        out_shape=(_sds((n, 4 * RET_W + 2 * LRU_W), BF16), _sds((8, 2 * LRU_W), F32)), compiler_params=_cp())


def _adamw(g, w, m, v):
    nm = ADAM_B1 * m + (1.0 - ADAM_B1) * g
    nv = ADAM_B2 * v + (1.0 - ADAM_B2) * (g * g)
    m_hat = nm / (1.0 - ADAM_B1 ** ADAM_STEP)
    v_hat = nv / (1.0 - ADAM_B2 ** ADAM_STEP)
    return (-ADAM_LR) * (m_hat / (jnp.sqrt(v_hat) + ADAM_EPS) + ADAM_WD * w), nm, nv


def _adam_many(items, name):
    n = len(items)

    def body(*refs):
        for i in range(n):
            g, w, m, v = (r[...] for r in refs[4 * i:4 * i + 4])
            for o_ref, val in zip(refs[4 * n + 3 * i:4 * n + 3 * i + 3], _adamw(g, w, m, v)):
                o_ref[...] = val

    out_shape = tuple(_sds(it[1].shape, F32) for it in items for _ in range(3))
    res = _pcall(body, name=name, out_shape=out_shape, compiler_params=_cp())(*[a for it in items for a in it])
    return [tuple(res[3 * i:3 * i + 3]) for i in range(n)]


def _sum_adam(parts_list, w, m, v, name):
    nparts, _, c = parts_list[0].shape
    r = w.shape[0]
    tr = min(parts_list[0].shape[1], 512)
    starts, o = [], 0
    for pa in parts_list:
        starts.append(o)
        o += pa.shape[1] // tr
    nseg = len(parts_list)

    def body(*refs):
        p_refs = refs[:nseg]
        w_ref, m_ref, v_ref, g_ref, d_ref, nm_ref, nv_ref = refs[nseg:]
        i = pl.program_id(0)
        for s, p_ref in enumerate(p_refs):
            end = starts[s + 1] if s + 1 < nseg else r // tr

            @pl.when(jnp.logical_and(i >= starts[s], i < end))
            def _():
                g = p_ref[0].astype(F32)
                for j in range(1, nparts):
                    g = g + p_ref[j].astype(F32)
                g_ref[...] = g
                d_ref[...], nm_ref[...], nv_ref[...] = _adamw(g, w_ref[...], m_ref[...], v_ref[...])

    def seg_spec(s):
        last = parts_list[s].shape[1] // tr - 1
        return pl.BlockSpec((nparts, tr, c), lambda i: (0, jnp.clip(i - starts[s], 0, last), 0))

    t = pl.BlockSpec((tr, c), lambda i: (i, 0))
    return _pcall(
        body, name=name, grid=(r // tr,),
        in_specs=[seg_spec(s) for s in range(nseg)] + [t, t, t],
        out_specs=(t, t, t, t), out_shape=(_sds((r, c), F32),) * 4, compiler_params=_cp(),
    )(*parts_list, w, m, v)


def _sum_parts(parts, name):
    nparts, r, c = parts.shape

    def body(p_ref, o_ref):
        g = p_ref[0]
        for j in range(1, nparts):
            g = g + p_ref[j]
        o_ref[...] = g

    return _pcall(body, name=name, out_shape=_sds((r, c), parts.dtype), compiler_params=_cp())(parts)


def _rot_tables(l_len, t_len):
    rows = t_len // GRID_W
    n_freq = HEAD_DIM // 4
    inv = ROPE_BASE ** (-jnp.arange(n_freq, dtype=F32) / n_freq)
    ang_r = jnp.arange(rows, dtype=F32)[:, None] * inv
    ang_c = jnp.arange(GRID_W, dtype=F32)[:, None] * inv
    cos = jnp.concatenate([jnp.repeat(jnp.cos(ang_r), GRID_W, axis=0), jnp.tile(jnp.cos(ang_c), (rows, 1))], axis=-1)
    sin = jnp.concatenate([jnp.repeat(jnp.sin(ang_r), GRID_W, axis=0), jnp.tile(jnp.sin(ang_c), (rows, 1))], axis=-1)
    cosf = jnp.concatenate([jnp.ones((l_len, HEAD_DIM), F32), jnp.concatenate([cos, cos], axis=-1)], axis=0)
    sins = jnp.concatenate([jnp.zeros((l_len, HEAD_DIM), F32), jnp.concatenate([-sin, sin], axis=-1)], axis=0)
    return cosf, sins


def _dense_gates(wa, wx):
    eye = jnp.eye(LRU_BLOCKS, dtype=wa.dtype)
    both = jnp.stack([wa, wx], axis=1)
    dense = both[:, :, :, :, None, :] * eye[None, None, :, None, :, None]
    dense = dense.reshape(2, 2, LRU_W, LRU_W).transpose(0, 2, 1, 3).reshape(2, LRU_W, 2 * LRU_W).astype(BF16)
    return [dense[0], dense[1]]


def _blocks_from_lanes(dwd_half):
    return dwd_half.reshape(LRU_BD, LRU_BLOCKS, LRU_BD).transpose(1, 0, 2)


def _silu(x):
    return x * jax.nn.sigmoid(x)


def kernel(x, c, ctx, c_ctx, w_ada, b_ada, norm1_g, norm2_g, w_in, ret_decay, conv_w, conv_b, lru_wa, lru_ba, lru_wx, lru_bx, lru_lambda, w_out, w_mlp1, w_mlp2, final_g, loss_target, m_c_ctx, m_w_ada, m_b_ada, m_norm1_g, m_norm2_g, m_w_in, m_ret_decay, m_conv_w, m_conv_b, m_lru_wa, m_lru_ba, m_lru_wx, m_lru_bx, m_lru_lambda, m_w_out, m_w_mlp1, m_w_mlp2, m_final_g, v_c_ctx, v_w_ada, v_b_ada, v_norm1_g, v_norm2_g, v_w_in, v_ret_decay, v_conv_w, v_conv_b, v_lru_wa, v_lru_ba, v_lru_wx, v_lru_bx, v_lru_lambda, v_w_out, v_w_mlp1, v_w_mlp2, v_final_g):
    t_len, d = x.shape[1], x.shape[2]
    l_len = ctx.shape[1]
    cb, cc = l_len // TM, l_len // CHUNK
    me = 4 * lax.axis_index("x") + 2 * lax.axis_index("y") + lax.axis_index("c")
    x2d, ctx2d, tgt2d = x[0], ctx[0], loss_target[0]
    ada_cols = w_ada.shape[2]
    wa2d = w_ada[0]

    sc_loc = conv_w.shape[2]
    pack_a = jnp.zeros((8, d), F32)
    pack_a = pack_a.at[0].set(_silu(c[0]))
    pack_a = pack_a.at[1, :4 * sc_loc].set(conv_w[0].reshape(-1))
    pack_a = pack_a.at[2, :2 * sc_loc].set(lru_ba[0].reshape(-1))
    pack_a = pack_a.at[3, :2 * sc_loc].set(lru_bx[0].reshape(-1))
    pack_a = pack_a.at[4, :2 * sc_loc].set(lru_lambda[0].reshape(-1))
    b_cols = lax.dynamic_slice(b_ada, (0, me * ada_cols), (1, ada_cols))
    all_a, s16, mod_parts = _mod_exchange(pack_a, _silu(c_ctx)[None, :], wa2d, b_cols)

    def unshard(row, k):
        return all_a[:, row, :k * sc_loc].reshape(N_DEV, k, sc_loc).transpose(1, 0, 2).reshape(k, N_DEV * sc_loc)

    conv_w_full = unshard(1, 4)
    ba_full, bx_full, lam_full = unshard(2, 2), unshard(3, 2), unshard(4, 2)

    mod_all = mod_parts.transpose(1, 0, 2).reshape(16, N_DEV * ada_cols)
    mod_me = lax.dynamic_slice(mod_all, (me, 0), (1, 6 * d)).reshape(6, d)
    mod_c = mod_all[8].reshape(6, d)
    modrows = jnp.concatenate([mod_c[0:2], mod_me], axis=0)

    lg = jax.nn.log_sigmoid(ret_decay[0])
    sp = jax.nn.softplus(-lam_full)
    wg = _dense_gates(lru_wa[0], lru_wx[0])
    bg = [jnp.concatenate([ba_full[dd], bx_full[dd]])[None, :] for dd in (0, 1)]
    cosf, sins = _rot_tables(l_len, t_len)

    (hn, hnt), win_g = _norm1_fwd(ctx2d, x2d, norm1_g, modrows, cb, ride=("gather", w_in[0].astype(BF16)))
    (qkv, p), w2_g = _mm_in(hn, win_g, cosf, sins, ride=("gather", w_mlp2[0].astype(BF16)))
    w2_g = w2_g.reshape(1, 4 * d, d)
    (o0, o1, sp0, sp1), w1_g = _ret_fwd(qkv, lg, cc, ride=("gather", w_mlp1[0].astype(BF16)))
    o, s_prev = [o0, o1], [sp0, sp1]
    (h0, cin0), wout_g = _lru_fwd(p, wg[0], bg[0], sp[0:1], conv_w_full, conv_b, 0, cb, ride=("gather", w_out[0].astype(BF16)))
    wout_g = wout_g.reshape(1, d, d)
    (h1, cin1), _ = _lru_fwd(p, wg[1], bg[1], sp[1:2], conv_w_full, conv_b, 1, cb)
    h, cin = [h0, h1], [cin0, cin1]
    mix = _mix_fwd(o[0], o[1], p, h[0], h[1], cb, t_len)
    y, x1, h2, h2t = _mm_nn(mix, wout_g, F32, "mm_out_norm2", tm=TL, epi=_res_norm2_epilogue(x2d, norm2_g, modrows, TL))
    r = _mm_nn(h2, w1_g, BF16, "mm_mlp1", relu_out=True, tm=TL)
    dx2, dz, dzt, facc = _mm_nn(r, w2_g, F32, "mm_mlp2_final", square_lhs=True, tm=TL, vmem_mb=58,
                                epi=_final_epilogue(x1, tgt2d, final_g[None, :], modrows, TL))

    du = _mm_nt(dz, w2_g, BF16, "mm_da2", relu_mul=r, tm=TL, vmem_mb=58)
    gw2_lo, gw2_hi = _mm_wgrad(dzt, r, "mm_dw2", w2_g.shape[1] // N_DEV, BF16, transpose_out=True, square_rhs=True,
                               halves=True)
    gw1_lo, gw1_hi = _mm_wgrad(h2t, du, "mm_dw1", w1_g.shape[2], BF16, halves=True)
    (dx1, dy, n2acc), gw2_lo_all = _mm_nt(du, w1_g, F32, "mm_dh2_norm2", ride=("a2a", gw2_lo), tm=TL, vmem_mb=60,
                                          epi=_bwd_norm2_epilogue(x1, dx2, y, norm2_g, modrows, TL))
    gwo = _mm_tn(mix, dy, "mm_dwout", False, d, BF16, 512).reshape(N_DEV, -1, d)
    (do, dg, dgate, dhs), gwo_all = _mm_nt(dy, wout_g, F32, "mm_dmix_mix", ride=("a2a", gwo),
                                           epi=_mix_bwd_epilogue(o[0], o[1], p, h[0], h[1], cb))
    dxcs, dwgs, laccs, rides, got = [], [], [], [("a2a", gw1_hi), ("a2a", gw2_hi)], []
    (dq0, dk0, dv0, dq1, dk1, dv1, dlg_lanes), gw1_lo_all = _ret_bwd(qkv, lg, do, s_prev, cc, ride=("a2a", gw1_lo))
    dqs, dks, dvs = [dq0, dq1], [dk0, dk1], [dv0, dv1]
    for dd in (0, 1):
        (dxc_, dwg_, lacc_), got_ = _lru_bwd(p, wg[dd], bg[dd], sp[dd:dd + 1], conv_w_full, conv_b, h[dd], cin[dd], dhs, dd, cb,
                                             ride=rides[dd])
        dxcs.append(dxc_); dwgs.append(dwg_); laccs.append(lacc_); got.append(got_)
    gw1_hi_all, gw2_hi_all = got
    gw1_all, gw2_all = [gw1_lo_all, gw1_hi_all], [gw2_lo_all, gw2_hi_all]
    (dp, cacc), _ = _assemble_dp(dqs, dks, dvs, dg, dgate, dxcs, p, conv_w_full, cosf, sins, cb)
    pack_b = jnp.concatenate([n2acc, facc, cacc, laccs[0], laccs[1], dlg_lanes, dwgs[0], dwgs[1]], axis=0)
    gwi, all_b = _mm_wgrad(hnt, dp, "mm_dwin", win_g.shape[2], BF16, ride=("gather", pack_b))
    (grad_x, n1acc), gwi_all = _mm_nt(dp, win_g, F32, "mm_dhn_norm1", ride=("a2a", gwi),
                                      epi=_bwd_norm1_epilogue(ctx2d, x2d, dx1, norm1_g, modrows, cb))
    all_n1 = _all_gather_small(n1acc, "gather_norm1_grads")
    tot = _sum_parts(all_b, "sum_small_grads")
    t_n1 = _sum_parts(all_n1, "sum_norm1_grads")
    t_n2, t_f, t_conv, t_dlg = tot[0:8], tot[8:16], tot[16:24, :LRU_W], tot[40:48, :HEAD_DIM]
    t_l = [tot[24:32], tot[32:40]]
    t_dwd = [tot[48:48 + LRU_BD], tot[48 + LRU_BD:48 + 2 * LRU_BD]]
    loss = (0.5 / d) * jnp.sum(t_f[2])
    t_wa = jnp.stack([_blocks_from_lanes(t_dwd[dd][:, :LRU_W]) for dd in (0, 1)])
    t_wx = jnp.stack([_blocks_from_lanes(t_dwd[dd][:, LRU_W:]) for dd in (0, 1)])
    t_ba = jnp.stack([t_l[dd][0, :LRU_W] for dd in (0, 1)])
    t_bx = jnp.stack([t_l[dd][0, LRU_W:] for dd in (0, 1)])
    t_sp = jnp.stack([t_l[dd][1, :LRU_W] for dd in (0, 1)])
    dm_rows = jnp.stack([all_n1[:, 2, :], all_n1[:, 3, :], all_b[:, 3, :], all_b[:, 0, :], all_b[:, 1, :], all_b[:, 9, :]],
                        axis=1).reshape(N_DEV, 6 * d)
    dm_c = jnp.concatenate([t_n1[0], t_n1[1], jnp.zeros((4 * d,), F32)])
    dm16 = jnp.zeros((16, 6 * d), F32).at[0:8].set(dm_rows).at[8].set(dm_c)
    g_b_ada = jnp.sum(dm16, axis=0)[None, :]
    dm_cols = lax.dynamic_slice(dm16, (0, me * ada_cols), (16, ada_cols))
    g_w_ada, ds16 = _ada_bwd(s16, dm_cols, wa2d)
    ds_all = _all_gather_small(ds16[8:16], "gather_dsilu")
    dsilu_cc = _sum_parts(ds_all, "sum_dsilu")[0]
    sg_cc = jax.nn.sigmoid(c_ctx)
    g_c_ctx = dsilu_cc * (sg_cc * (1.0 + c_ctx * (1.0 - sg_cc)))

    g_ret_decay = jnp.sum(t_dlg, axis=-1).reshape(2, HEADS) * jax.nn.sigmoid(-ret_decay[0])
    g_lambda_full = -t_sp * jax.nn.sigmoid(-lam_full)

    def my_cols(full):
        return lax.dynamic_slice(full, (0, me * sc_loc), (full.shape[0], sc_loc))

    small_g = dict(
        c_ctx=g_c_ctx[None], b_ada=g_b_ada, norm1_g=t_n1[4:5], norm2_g=t_n2[2:3], ret_decay=g_ret_decay,
        conv_w=my_cols(t_conv[0:4]), conv_b=t_conv[4:5], lru_wa=t_wa.reshape(-1, LRU_BD), lru_ba=my_cols(t_ba),
        lru_wx=t_wx.reshape(-1, LRU_BD), lru_bx=my_cols(t_bx), lru_lambda=my_cols(g_lambda_full), final_g=t_f[0:1])
    small = dict(
        c_ctx=(c_ctx, m_c_ctx, v_c_ctx), b_ada=(b_ada, m_b_ada, v_b_ada), norm1_g=(norm1_g, m_norm1_g, v_norm1_g),
        norm2_g=(norm2_g, m_norm2_g, v_norm2_g), ret_decay=(ret_decay, m_ret_decay, v_ret_decay),
        conv_w=(conv_w, m_conv_w, v_conv_w), conv_b=(conv_b, m_conv_b, v_conv_b), lru_wa=(lru_wa, m_lru_wa, v_lru_wa),
        lru_ba=(lru_ba, m_lru_ba, v_lru_ba), lru_wx=(lru_wx, m_lru_wx, v_lru_wx), lru_bx=(lru_bx, m_lru_bx, v_lru_bx),
        lru_lambda=(lru_lambda, m_lru_lambda, v_lru_lambda), final_g=(final_g, m_final_g, v_final_g))
    names = list(small)
    items = [(small_g[k],) + tuple(a.reshape(small_g[k].shape) for a in small[k]) for k in names]
    res = {}
    for k, it, (d_, m_, v_) in zip(names, items, _adam_many(items, "adam_small")):
        shape = small[k][0].shape
        res[k] = tuple(a.reshape(shape) for a in (it[0], d_, m_, v_))

    def big(parts, w, m, v, name):
        out = _sum_adam(parts, w[0], m[0], v[0], name)
        return tuple(a[None] for a in out)

    res["w_ada"] = big([g_w_ada[None]], w_ada, m_w_ada, v_w_ada, "adam_w_ada")
    res["w_in"] = big([gwi_all], w_in, m_w_in, v_w_in, "adam_w_in")
    res["w_out"] = big([gwo_all], w_out, m_w_out, v_w_out, "adam_w_out")
    res["w_mlp1"] = big(gw1_all, w_mlp1, m_w_mlp1, v_w_mlp1, "adam_w_mlp1")
    res["w_mlp2"] = big(gw2_all, w_mlp2, m_w_mlp2, v_w_mlp2, "adam_w_mlp2")

    order = ["c_ctx", "w_ada", "b_ada", "norm1_g", "norm2_g", "w_in", "ret_decay", "conv_w", "conv_b", "lru_wa", "lru_ba",
             "lru_wx", "lru_bx", "lru_lambda", "w_out", "w_mlp1", "w_mlp2", "final_g"]
    outs = [loss, grad_x[None]]
    for j in range(4):
        outs += [res[k][j] for k in order]
    return tuple(outs)
```

```python
import jax
import jax.numpy as jnp
from jax import lax
from jax.experimental import pallas as pl
from jax.experimental.pallas import tpu as pltpu

F32 = jnp.float32
BF16 = jnp.bfloat16
N_DEV = 8
MESH = pl.DeviceIdType.MESH

HEADS = 4
HEAD_DIM = 128
CHUNK = 128
RET_W = HEADS * HEAD_DIM
LRU_W = 512
LRU_BLOCKS = 8
LRU_BD = LRU_W // LRU_BLOCKS
LRU_C = 8.0
EPS = 1e-6
K_SCALE = HEAD_DIM ** -0.5
ROPE_BASE = 10000.0
GRID_W = 64
TM = 256
TL = 512
SUB = 8

ADAM_LR = 0.001
ADAM_B1 = 0.9
ADAM_B2 = 0.999
ADAM_EPS = 1e-08
ADAM_WD = 0.01
ADAM_STEP = 10

COL_G, COL_XR, COL_GATE = 0, 1, 2

R_CSH1, R_CSC1, R_SH1, R_SC1, R_G1, R_SH2, R_SC2, R_G2 = range(8)


def _pcall(body, **kw):
    return pl.pallas_call(body, **kw)


def _cp(vmem_mb=48):
    return pltpu.CompilerParams(vmem_limit_bytes=vmem_mb << 20)


def _sds(shape, dtype):
    return jax.ShapeDtypeStruct(shape, dtype)


def _dot(a, b):
    return jnp.dot(a.astype(BF16), b.astype(BF16), preferred_element_type=F32)


def _dot_nt(a, b):
    return lax.dot_general(a.astype(BF16), b.astype(BF16), (((1,), (1,)), ((), ())), preferred_element_type=F32)


def _dot_tn(a, b):
    return lax.dot_general(a.astype(BF16), b.astype(BF16), (((0,), (0,)), ((), ())), preferred_element_type=F32)


def _sigmoid(x):
    return 0.5 * jnp.tanh(0.5 * x) + 0.5


def _gelu(x):
    return 0.5 * x * (1.0 + jnp.tanh(0.7978845608028654 * (x + 0.044715 * x * x * x)))


def _dgelu(x):
    t = jnp.tanh(0.7978845608028654 * (x + 0.044715 * x * x * x))
    return 0.5 * (1.0 + t) + 0.5 * x * (1.0 - t * t) * 0.7978845608028654 * (1.0 + 3.0 * 0.044715 * x * x)


def _rows_iota(shape):
    return lax.broadcasted_iota(jnp.int32, shape, 0)


def _tile_order(dirn, s, cb, nb):
    if dirn == 0:
        return s
    return jnp.where(s < cb, cb - 1 - s, nb - 1 - (s - cb))


_SEMS = [pltpu.SemaphoreType.DMA((7,)), pltpu.SemaphoreType.DMA((7,)), pltpu.SemaphoreType.DMA(())]
_ANY = pl.BlockSpec(memory_space=pl.ANY)


def _gather_copies(x_ref, out_ref, send_sems, recv_sems, local_sem):
    mx, my, mc = lax.axis_index("x"), lax.axis_index("y"), lax.axis_index("c")
    me, sibling = (mx, my, mc), (mx, my, 1 - mc)
    chips = [(1 - mx, my), (mx, 1 - my), (1 - mx, 1 - my)]

    def slot(px, py, pc):
        return out_ref.at[4 * px + 2 * py + pc]

    def copy(k, block, to, src=None):
        return pltpu.make_async_remote_copy(
            src_ref=slot(*block) if src is None else src, dst_ref=slot(*block),
            send_sem=send_sems.at[k], recv_sem=recv_sems.at[k], device_id=to, device_id_type=MESH)

    mine = pltpu.make_async_copy(x_ref, slot(*me), local_sem)
    first = [copy(0, me, sibling, src=x_ref)] + [copy(1 + j, me, (*chip, mc), src=x_ref) for j, chip in enumerate(chips)]
    passed = [copy(4 + j, (*chip, mc), sibling) for j, chip in enumerate(chips)]
    recv_ici = [copy(1 + j, (*chip, mc), me) for j, chip in enumerate(chips)]
    recv_d2d = [copy(0, sibling, me)] + [copy(4 + j, (*chip, 1 - mc), me) for j, chip in enumerate(chips)]
    return mine, first, passed, recv_ici, recv_d2d


def _gather_start(*refs):
    mine, first, _, _, _ = _gather_copies(*refs)
    mine.start()
    for cp in first:
        cp.start()


def _gather_pass_on(*refs):
    _, _, passed, recv_ici, _ = _gather_copies(*refs)
    for landed, onward in zip(recv_ici, passed):
        landed.wait_recv()
        onward.start()


def _gather_finish(*refs):
    mine, first, passed, _, recv_d2d = _gather_copies(*refs)
    for landed in recv_d2d:
        landed.wait_recv()
    for cp in first + passed:
        cp.wait_send()
    mine.wait()


def _a2a_copies(g_ref, out_ref, send_sems, recv_sems, local_sem):
    mx, my, mc = lax.axis_index("x"), lax.axis_index("y"), lax.axis_index("c")
    me = 4 * mx + 2 * my + mc
    mine = pltpu.make_async_copy(g_ref.at[me], out_ref.at[me], local_sem)
    copies = []
    for k in range(1, N_DEV):
        px = 1 - mx if (k >> 2) & 1 else mx
        py = 1 - my if (k >> 1) & 1 else my
        pc = 1 - mc if k & 1 else mc
        copies.append(pltpu.make_async_remote_copy(
            src_ref=g_ref.at[4 * px + 2 * py + pc], dst_ref=out_ref.at[me],
            send_sem=send_sems.at[k - 1], recv_sem=recv_sems.at[k - 1],
            device_id=(px, py, pc), device_id_type=MESH))
    return mine, copies


def _a2a_start(*refs):
    mine, copies = _a2a_copies(*refs)
    mine.start()
    for cp in copies:
        cp.start()


def _a2a_finish(*refs):
    mine, copies = _a2a_copies(*refs)
    for cp in copies:
        cp.wait()
    mine.wait()


_EXCHANGES = {"gather": (_gather_start, _gather_finish), "a2a": (_a2a_start, _a2a_finish)}
PASS_ON_LEAD = 3


def _exchange_shape(kind, src):
    return _sds((N_DEV,) + src.shape if kind == "gather" else src.shape, src.dtype)


def _all_gather_small(x, name):
    def body(x_ref, out_ref, send_sems, recv_sems, local_sem):
        mx, my, mc = lax.axis_index("x"), lax.axis_index("y"), lax.axis_index("c")
        me = 4 * mx + 2 * my + mc
        mine = pltpu.make_async_copy(x_ref, out_ref.at[me], local_sem)
        mine.start()
        copies = []
        for k in range(1, N_DEV):
            peer = (1 - mx if (k >> 2) & 1 else mx, 1 - my if (k >> 1) & 1 else my, 1 - mc if k & 1 else mc)
            copies.append(pltpu.make_async_remote_copy(
                src_ref=x_ref, dst_ref=out_ref.at[me], send_sem=send_sems.at[k - 1], recv_sem=recv_sems.at[k - 1],
                device_id=peer, device_id_type=MESH))
            copies[-1].start()
        for cp in copies:
            cp.wait()
        mine.wait()

    return _pcall(body, name=name, out_shape=_exchange_shape("gather", x), in_specs=[_ANY], out_specs=_ANY,
                  scratch_shapes=list(_SEMS))(x)


def _pcall_ride(body, ride, args, *, name, grid, in_specs, out_specs, out_shape, scratch_shapes=(), compiler_params=None):
    if ride is None:
        out = _pcall(body, name=name, grid=grid, in_specs=in_specs, out_specs=out_specs, out_shape=out_shape,
                     scratch_shapes=list(scratch_shapes), compiler_params=compiler_params)(*args)
        return out, None
    kind, src = ride
    start, finish = _EXCHANGES[kind]
    single = not isinstance(out_shape, (tuple, list))
    out_specs_t = (out_specs,) if single else tuple(out_specs)
    out_shape_t = (out_shape,) if single else tuple(out_shape)
    n_in, n_out, n_sc = len(in_specs), len(out_shape_t), len(scratch_shapes)

    def wrapped(*refs):
        ins, src_ref = refs[:n_in], refs[n_in]
        outs, dst_ref = refs[n_in + 1:n_in + 1 + n_out], refs[n_in + 1 + n_out]
        scratch = refs[n_in + 2 + n_out:n_in + 2 + n_out + n_sc]
        sems = refs[n_in + 2 + n_out + n_sc:]
        first = pl.program_id(0) == 0
        last = pl.program_id(0) == grid[0] - 1
        for ax in range(1, len(grid)):
            first = jnp.logical_and(first, pl.program_id(ax) == 0)
            last = jnp.logical_and(last, pl.program_id(ax) == grid[ax] - 1)

        @pl.when(first)
        def _():
            start(src_ref, dst_ref, *sems)

        body(*ins, *outs, *scratch)

        if kind == "gather":
            @pl.when(pl.program_id(0) == max(grid[0] - PASS_ON_LEAD, 0))
            def _():
                _gather_pass_on(src_ref, dst_ref, *sems)

        @pl.when(last)
        def _():
            finish(src_ref, dst_ref, *sems)

    res = _pcall(wrapped, name=name, grid=grid, in_specs=list(in_specs) + [_ANY], out_specs=out_specs_t + (_ANY,),
                 out_shape=out_shape_t + (_exchange_shape(kind, src),),
                 scratch_shapes=list(scratch_shapes) + list(_SEMS), compiler_params=compiler_params)(*args, src)
    return (res[0] if single else tuple(res[:-1])), res[-1]


class _Epilogue:
    def __init__(self, fn, args, in_specs, out_specs, out_shape, steps=None, lhs_map=None, delay=0):
        self.fn, self.args, self.in_specs, self.out_specs, self.out_shape = fn, tuple(args), list(in_specs), out_specs, out_shape
        self.steps, self.lhs_map, self.delay = steps, lhs_map, delay


def _mm_nn(a, w, out_dtype, name, square_lhs=False, relu_out=False, ride=None, tm=TM, epi=None, vmem_mb=48):
    m, k = a.shape
    nb, _, bn = w.shape
    tm = min(tm, m)
    delay = 0 if epi is None else epi.delay
    steps = m // tm

    def body(*refs):
        a_ref, w_ref = refs[:2]
        av = a_ref[...]
        if square_lhs:
            av = av * av
        if delay:
            assert nb == 1
            held = refs[-1]

            @pl.when(pl.program_id(0) == 0)
            def _():
                held[...] = jnp.zeros_like(held)

            fresh = jnp.dot(av, w_ref[0], preferred_element_type=F32)
            epi.fn(held[...], slice(0, tm), *refs[2:-1])
            held[...] = fresh
            return
        if epi is not None:
            assert nb == 1
            epi.fn(jnp.dot(av, w_ref[0], preferred_element_type=F32), slice(0, tm), *refs[2:])
            return
        for j in range(nb):
            r = jnp.dot(av, w_ref[j], preferred_element_type=F32)
            if relu_out:
                r = jnp.maximum(r, 0.0)
            refs[2][:, j * bn:(j + 1) * bn] = r.astype(out_dtype)

    lhs_map = (lambda i: (i, 0)) if not delay else (lambda i: (jnp.minimum(i, steps - 1), 0))
    in_specs = [pl.BlockSpec((tm, k), lhs_map), pl.BlockSpec((nb, k, bn), lambda i: (0, 0, 0))]
    scratch = []
    if epi is None:
        args, out_specs, out_shape = (a, w), pl.BlockSpec((tm, nb * bn), lambda i: (i, 0)), _sds((m, nb * bn), out_dtype)
    else:
        args, out_specs, out_shape = (a, w) + epi.args, epi.out_specs, epi.out_shape
        in_specs += epi.in_specs
        if delay:
            scratch.append(pltpu.VMEM((tm, bn), F32))
    out, ex = _pcall_ride(body, ride, args, name=name, grid=(steps + delay,), in_specs=in_specs, out_specs=out_specs,
                          out_shape=out_shape, scratch_shapes=scratch, compiler_params=_cp(vmem_mb))
    return out if ride is None else (out, ex)


def _mm_nt(dy, w, out_dtype, name, relu_mul=None, ride=None, tm=TM, epi=None, vmem_mb=48):
    m = dy.shape[0]
    nb, k, bn = w.shape
    tm = min(tm, m)
    n_extra = (0 if relu_mul is None else 1) + (0 if epi is None else len(epi.args))

    delay = 0 if epi is None else epi.delay

    def body(*refs):
        dy_ref, w_ref = refs[:2]
        scratch = refs[-1 - delay:]
        extra, outs, wt = refs[2:2 + n_extra], refs[2 + n_extra:-1 - delay], scratch[0]

        @pl.when(pl.program_id(0) == 0)
        def _():
            for j in range(nb):
                wt[j * bn:(j + 1) * bn, :] = w_ref[j].T
            if delay:
                scratch[1][...] = jnp.zeros_like(scratch[1])

        if delay:
            held = scratch[1]
            fresh = jnp.dot(dy_ref[...], wt[...], preferred_element_type=F32)
            epi.fn(held[...], slice(0, tm), *extra, *outs)
            held[...] = fresh
            return
        acc = jnp.dot(dy_ref[...], wt[...], preferred_element_type=F32)
        if epi is not None:
            epi.fn(acc, slice(0, tm), *extra, *outs)
            return
        if relu_mul is not None:
            acc = acc * (2.0 * extra[0][...].astype(F32))
        outs[0][...] = acc.astype(out_dtype)

    steps = m // tm if epi is None or epi.steps is None else epi.steps
    lhs_tile = (lambda i: (i, 0)) if epi is None or epi.lhs_map is None else epi.lhs_map
    lhs_map = lhs_tile if not delay else (lambda i: lhs_tile(jnp.minimum(i, steps - 1)))
    in_specs = [pl.BlockSpec((tm, nb * bn), lhs_map), pl.BlockSpec((nb, k, bn), lambda i: (0, 0, 0))]
    args = [dy, w]
    if relu_mul is not None:
        in_specs.append(pl.BlockSpec((tm, k), lambda i: (i, 0)))
        args.append(relu_mul)
    scratch = [pltpu.VMEM((nb * bn, k), BF16)]
    if epi is None:
        out_specs, out_shape = pl.BlockSpec((tm, k), lambda i: (i, 0)), _sds((m, k), out_dtype)
    else:
        args += list(epi.args)
        in_specs += epi.in_specs
        out_specs, out_shape = epi.out_specs, epi.out_shape
        if delay:
            scratch.append(pltpu.VMEM((tm, k), F32))
    out, ex = _pcall_ride(
        body, ride, args, name=name, grid=(steps + delay,), in_specs=in_specs, out_specs=out_specs, out_shape=out_shape,
        scratch_shapes=scratch, compiler_params=_cp(vmem_mb))
    return out if ride is None else (out, ex)


def _mm_tn(a, b, name, col_blocks, block, out_dtype, tm, square_lhs=False):
    m, k = a.shape
    nn = b.shape[1]
    steps = m // tm
    if col_blocks:
        nblk, acc_shape = nn // block, (k, block)
        a_spec = pl.BlockSpec((tm, k), lambda j, s: (s, 0))
        b_spec = pl.BlockSpec((tm, block), lambda j, s: (s, j))
    else:
        nblk, acc_shape = k // block, (block, nn)
        a_spec = pl.BlockSpec((tm, block), lambda j, s: (s, j))
        b_spec = pl.BlockSpec((tm, nn), lambda j, s: (s, 0))

    def body(a_ref, b_ref, o_ref, acc):
        s = pl.program_id(1)

        @pl.when(s == 0)
        def _():
            acc[...] = jnp.zeros_like(acc)

        av = a_ref[...]
        if square_lhs:
            av = av.astype(F32)
            av = (av * av).astype(BF16)
        acc[...] += _dot_tn(av, b_ref[...])

        @pl.when(s == steps - 1)
        def _():
            o_ref[...] = acc[...].astype(out_dtype)

    return _pcall(
        body, name=name, grid=(nblk, steps), in_specs=[a_spec, b_spec],
        out_specs=pl.BlockSpec((None,) + acc_shape, lambda j, s: (j, 0, 0)),
        out_shape=_sds((nblk,) + acc_shape, out_dtype),
        scratch_shapes=[pltpu.VMEM(acc_shape, F32)], compiler_params=_cp(),
    )(a, b)


def _mm_wgrad(at, b, name, bn, out_dtype, transpose_out=False, square_rhs=False, halves=False, ride=None):
    k, m = at.shape
    nblk = b.shape[1] // bn
    rows, cols = (bn, k) if transpose_out else (k, bn)
    nout = 2 if halves else 1
    per = rows // nout

    def body(a_ref, b_ref, *o_refs):
        bv = b_ref[...]
        if square_rhs:
            bv = bv * bv
        r = jnp.dot(a_ref[...], bv, preferred_element_type=F32)
        r = (r.T if transpose_out else r).astype(out_dtype)
        for i, o_ref in enumerate(o_refs):
            o_ref[...] = r[i * per:(i + 1) * per, :]

    out, ex = _pcall_ride(
        body, ride, (at, b), name=name, grid=(nblk,),
        in_specs=[pl.BlockSpec((k, m), lambda j: (0, 0)), pl.BlockSpec((m, bn), lambda j: (0, j))],
        out_specs=tuple(pl.BlockSpec((None, per, cols), lambda j: (j, 0, 0)) for _ in range(nout)),
        out_shape=tuple(_sds((nblk, per, cols), out_dtype) for _ in range(nout)), compiler_params=_cp())
    out = out if halves else out[0]
    return out if ride is None else (out, ex)


def _mm_in(hn, w, cosf, sins, ride):
    m, k = hn.shape
    nb, _, bn = w.shape

    def body(a_ref, w_ref, c_ref, s_ref, qkv_ref, rest_ref, pt):
        av = a_ref[...]
        for j in range(nb):
            pt[:, j * bn:(j + 1) * bn] = jnp.dot(av, w_ref[j], preferred_element_type=F32)
        cf, ss = c_ref[...], s_ref[...]
        for h in range(HEADS):
            sq = slice(h * HEAD_DIM, (h + 1) * HEAD_DIM)
            sk = slice(RET_W + h * HEAD_DIM, RET_W + (h + 1) * HEAD_DIM)
            qkv_ref[:, sq] = _rot(pt[:, sq], cf, ss).astype(BF16)
            qkv_ref[:, sk] = (_rot(pt[:, sk], cf, ss) * K_SCALE).astype(BF16)
        qkv_ref[:, 2 * RET_W:] = pt[:, 2 * RET_W:3 * RET_W].astype(BF16)
        rest_ref[...] = pt[:, 3 * RET_W:]

    tab = pl.BlockSpec((TM, HEAD_DIM), lambda i: (i, 0))
    wide = pl.BlockSpec((TM, 3 * RET_W), lambda i: (i, 0))
    return _pcall_ride(
        body, ride, (hn, w, cosf, sins), name="mm_in", grid=(m // TM,),
        in_specs=[pl.BlockSpec((TM, k), lambda i: (i, 0)), pl.BlockSpec((nb, k, bn), lambda i: (0, 0, 0)), tab, tab],
        out_specs=(wide, wide), out_shape=(_sds((m, 3 * RET_W), BF16), _sds((m, nb * bn - 3 * RET_W), F32)),
        scratch_shapes=[pltpu.VMEM((TM, nb * bn), F32)], compiler_params=_cp())


def _mod_exchange(pack_a, silu_cc, w_ada, b_cols):
    d, cols = pack_a.shape[1], w_ada.shape[1]

    def body(pa_ref, scc_ref, w_ref, b_ref, alla_ref, s16_ref, modp_ref, mp, send1, recv1, send2, recv2):
        mx, my, mc = lax.axis_index("x"), lax.axis_index("y"), lax.axis_index("c")
        me = 4 * mx + 2 * my + mc

        def exchange(src_ref, dst_ref, send, recv):
            copies = []
            for k in range(1, N_DEV):
                peer = (1 - mx if (k >> 2) & 1 else mx, 1 - my if (k >> 1) & 1 else my, 1 - mc if k & 1 else mc)
                copies.append(pltpu.make_async_remote_copy(
                    src_ref=src_ref, dst_ref=dst_ref.at[me], send_sem=send.at[k - 1], recv_sem=recv.at[k - 1],
                    device_id=peer, device_id_type=MESH))
                copies[-1].start()
            dst_ref[me] = src_ref[...]
            for cp in copies:
                cp.wait()

        exchange(pa_ref, alla_ref, send1, recv1)
        s16_ref[...] = jnp.zeros_like(s16_ref)
        for j in range(N_DEV):
            s16_ref[j:j + 1, :] = alla_ref[j, 0:1, :]
        s16_ref[N_DEV:N_DEV + 1, :] = scc_ref[...]
        mp[...] = _dot(s16_ref[...], w_ref[...]) + b_ref[...]
        exchange(mp, modp_ref, send2, recv2)

    vmem = pl.BlockSpec(memory_space=pltpu.VMEM)
    sem7 = pltpu.SemaphoreType.DMA((N_DEV - 1,))
    return _pcall(
        body, name="mod_exchange", in_specs=[vmem] * 4, out_specs=(vmem,) * 3,
        out_shape=(_sds((N_DEV,) + pack_a.shape, F32), _sds((2 * N_DEV, d), F32), _sds((N_DEV, 2 * N_DEV, cols), F32)),
        scratch_shapes=[pltpu.VMEM((2 * N_DEV, cols), F32), sem7, sem7, sem7, sem7], compiler_params=_cp(),
    )(pack_a, silu_cc, w_ada, b_cols)


def _ada_bwd(s16, dm_cols, w_ada):
    def body(s_ref, d_ref, w_ref, gw_ref, ds_ref):
        gw_ref[...] = _dot_tn(s_ref[...], d_ref[...])
        ds_ref[...] = _dot_nt(d_ref[...], w_ref[...])

    return _pcall(body, name="ada_bwd",
                  out_shape=(_sds(w_ada.shape, F32), _sds(s16.shape, F32)), compiler_params=_cp())(s16, dm_cols, w_ada)


def _norm1_fwd(ctx, x, g, modrows, cb, ride=None):
    l_len, d = ctx.shape
    nb = (l_len + x.shape[0]) // TM

    def body(ctx_ref, x_ref, g_ref, m_ref, o_ref, ot_ref):
        is_ctx = pl.program_id(0) < cb
        xin = jnp.where(is_ctx, ctx_ref[...], x_ref[...])
        sh = jnp.where(is_ctx, m_ref[R_CSH1:R_CSH1 + 1, :], m_ref[R_SH1:R_SH1 + 1, :])
        sc = jnp.where(is_ctx, m_ref[R_CSC1:R_CSC1 + 1, :], m_ref[R_SC1:R_SC1 + 1, :])
        ms = jnp.mean(xin * xin, axis=-1, keepdims=True)
        n = xin * lax.rsqrt(ms + EPS) * g_ref[...]
        hn = n * (1.0 + sc) + sh
        o_ref[...] = hn.astype(BF16)
        ot_ref[...] = hn.T.astype(BF16)

    return _pcall_ride(
        body, ride, (ctx, x, g, modrows), name="norm1_fwd", grid=(nb,),
        in_specs=[pl.BlockSpec((TM, d), lambda i: (jnp.minimum(i, cb - 1), 0)),
                  pl.BlockSpec((TM, d), lambda i: (jnp.maximum(i - cb, 0), 0)),
                  pl.BlockSpec((1, d), lambda i: (0, 0)), pl.BlockSpec((8, d), lambda i: (0, 0))],
        out_specs=(pl.BlockSpec((TM, d), lambda i: (i, 0)), pl.BlockSpec((d, TM), lambda i: (0, i))),
        out_shape=(_sds((nb * TM, d), BF16), _sds((d, nb * TM), BF16)), compiler_params=_cp())


def _mix_fwd(o_f, o_b, p, h_f, h_b, cb, t_len):
    def body(of_ref, ob_ref, g_ref, gate_ref, hf_ref, hb_ref, mix_ref):
        o = of_ref[...] + ob_ref[...]
        g = g_ref[...]
        sg = g * _sigmoid(g)
        for hh in range(HEADS):
            sl = slice(hh * HEAD_DIM, (hh + 1) * HEAD_DIM)
            oh = o[:, sl]
            yc = oh - jnp.mean(oh, axis=-1, keepdims=True)
            var = jnp.mean(yc * yc, axis=-1, keepdims=True)
            mix_ref[:, sl] = (sg[:, sl] * (yc * lax.rsqrt(var + EPS))).astype(BF16)
        mix_ref[:, RET_W:] = ((hf_ref[...] + hb_ref[...]) * _gelu(gate_ref[...])).astype(BF16)

    row = lambda i: (i + cb, 0)
    return _pcall(
        body, name="mix_fwd", grid=(t_len // TM,),
        in_specs=[pl.BlockSpec((TM, RET_W), row), pl.BlockSpec((TM, RET_W), row),
                  pl.BlockSpec((TM, RET_W), lambda i: (i + cb, COL_G)), pl.BlockSpec((TM, LRU_W), lambda i: (i + cb, COL_GATE)),
                  pl.BlockSpec((TM, LRU_W), row), pl.BlockSpec((TM, LRU_W), row)],
        out_specs=pl.BlockSpec((TM, RET_W + LRU_W), lambda i: (i, 0)),
        out_shape=_sds((t_len, RET_W + LRU_W), BF16), compiler_params=_cp(),
    )(o_f, o_b, p, p, h_f, h_b)


def _res_norm2_epilogue(x, g, modrows, tm):
    t_len, d = x.shape
    tm = min(tm, t_len)

    def fn(y, rows, x_ref, g_ref, m_ref, y_ref, x1_ref, h2_ref, h2t_ref):
        y_ref[rows, :] = y
        x1 = x_ref[rows, :] + m_ref[R_G1:R_G1 + 1, :] * y
        ms = jnp.mean(x1 * x1, axis=-1, keepdims=True)
        n = x1 * lax.rsqrt(ms + EPS) * g_ref[...]
        x1_ref[rows, :] = x1
        h2 = n * (1.0 + m_ref[R_SC2:R_SC2 + 1, :]) + m_ref[R_SH2:R_SH2 + 1, :]
        h2_ref[rows, :] = h2.astype(BF16)
        h2t_ref[:, rows] = h2.T.astype(BF16)

    t = pl.BlockSpec((tm, d), lambda i: (i, 0))
    return _Epilogue(
        fn, (x, g, modrows),
        in_specs=[t, pl.BlockSpec((1, d), lambda i: (0, 0)), pl.BlockSpec((8, d), lambda i: (0, 0))],
        out_specs=(t, t, t, pl.BlockSpec((d, tm), lambda i: (0, i))),
        out_shape=(_sds((t_len, d), F32), _sds((t_len, d), F32), _sds((t_len, d), BF16), _sds((d, t_len), BF16)))


def _tile(i):
    return jnp.maximum(i - 1, 0)


def _zero_at_start(acc_ref, rows):
    @pl.when(pl.program_id(0) == 0)
    def _():
        acc_ref[...] = jnp.zeros_like(acc_ref)


def _final_epilogue(x1, target, fg, modrows, tm):
    t_len, d = x1.shape

    def fn(z, rows, x1_ref, t_ref, fg_ref, m_ref, dx2_ref, dz_ref, dzt_ref, acc_ref):
        _zero_at_start(acc_ref, rows)
        g2 = m_ref[R_G2:R_G2 + 1, :]
        x2 = x1_ref[rows, :] + g2 * z
        rstd = lax.rsqrt(jnp.mean(x2 * x2, axis=-1, keepdims=True) + EPS)
        xh = x2 * rstd
        fg = fg_ref[...]
        e = xh * fg - t_ref[rows, :]
        dy = e * (1.0 / d)
        dxh = dy * fg
        dx2 = rstd * (dxh - xh * jnp.mean(dxh * xh, axis=-1, keepdims=True))
        dx2_ref[rows, :] = dx2
        dz = g2 * dx2
        dz_ref[rows, :] = dz.astype(BF16)
        dzt_ref[:, rows] = dz.T.astype(BF16)
        acc_ref[0:1, :] += jnp.sum(dy * xh, axis=0, keepdims=True)
        acc_ref[1:2, :] += jnp.sum(dx2 * z, axis=0, keepdims=True)
        acc_ref[2:3, :] += jnp.sum(e * e, axis=0, keepdims=True)

    t = pl.BlockSpec((tm, d), lambda i: (i, 0))
    return _Epilogue(
        fn, (x1, target, fg, modrows),
        in_specs=[t, t, pl.BlockSpec((1, d), lambda i: (0, 0)), pl.BlockSpec((8, d), lambda i: (0, 0))],
        out_specs=(t, t, pl.BlockSpec((d, tm), lambda i: (0, i)), pl.BlockSpec((8, d), lambda i: (0, 0))),
        out_shape=(_sds((t_len, d), F32), _sds((t_len, d), BF16), _sds((d, t_len), BF16), _sds((8, d), F32)))


def _bwd_norm2_epilogue(x1, dx2, y, g, modrows, tm):
    t_len, d = x1.shape

    def fn(dh2, rows, x1_ref, dx2_ref, y_ref, g_ref, m_ref, dx1_ref, dy_ref, acc_ref):
        _zero_at_start(acc_ref, rows)
        x1 = x1_ref[rows, :]
        rstd = lax.rsqrt(jnp.mean(x1 * x1, axis=-1, keepdims=True) + EPS)
        xh = x1 * rstd
        gn = g_ref[...]
        dn = dh2 * (1.0 + m_ref[R_SC2:R_SC2 + 1, :])
        dxh = dn * gn
        dx1 = dx2_ref[rows, :] + rstd * (dxh - xh * jnp.mean(dxh * xh, axis=-1, keepdims=True))
        dx1_ref[rows, :] = dx1
        dy_ref[rows, :] = (m_ref[R_G1:R_G1 + 1, :] * dx1).astype(BF16)
        acc_ref[0:1, :] += jnp.sum(dh2, axis=0, keepdims=True)
        acc_ref[1:2, :] += jnp.sum(dh2 * xh * gn, axis=0, keepdims=True)
        acc_ref[2:3, :] += jnp.sum(dn * xh, axis=0, keepdims=True)
        acc_ref[3:4, :] += jnp.sum(dx1 * y_ref[rows, :], axis=0, keepdims=True)

    t = pl.BlockSpec((tm, d), lambda i: (i, 0))
    return _Epilogue(
        fn, (x1, dx2, y, g, modrows),
        in_specs=[t, t, t, pl.BlockSpec((1, d), lambda i: (0, 0)), pl.BlockSpec((8, d), lambda i: (0, 0))],
        out_specs=(t, t, pl.BlockSpec((8, d), lambda i: (0, 0))),
        out_shape=(_sds((t_len, d), F32), _sds((t_len, d), BF16), _sds((8, d), F32)))


def _mix_bwd_epilogue(o_f, o_b, p, h_f, h_b, cb):
    n = o_f.shape[0]
    tile = _tile

    def fn(dm, rows, of_ref, ob_ref, g_ref, gate_ref, hf_ref, hb_ref, do_ref, dg_ref, dgate_ref, dhs_ref):
        keep = jnp.where(pl.program_id(0) - 1 < cb, 0.0, 1.0)
        dm = dm * keep
        o = of_ref[rows, :] + ob_ref[rows, :]
        g = g_ref[rows, :]
        s = _sigmoid(g)
        sg = g * s
        dsg = s * (1.0 + g * (1.0 - s))
        for hh in range(HEADS):
            sl = slice(hh * HEAD_DIM, (hh + 1) * HEAD_DIM)
            oh = o[:, sl]
            yc = oh - jnp.mean(oh, axis=-1, keepdims=True)
            rs = lax.rsqrt(jnp.mean(yc * yc, axis=-1, keepdims=True) + EPS)
            gn = yc * rs
            dret = dm[:, sl]
            dgn = dret * sg[:, sl]
            dg_ref[rows, sl] = (dret * gn * dsg[:, sl]).astype(BF16)
            do_ref[rows, sl] = (rs * (dgn - jnp.mean(dgn, axis=-1, keepdims=True)
                                      - gn * jnp.mean(dgn * gn, axis=-1, keepdims=True))).astype(BF16)
        dlru = dm[:, RET_W:]
        gate = gate_ref[rows, :]
        dhs_ref[rows, :] = dlru * _gelu(gate)
        dgate_ref[rows, :] = (dlru * (hf_ref[rows, :] + hb_ref[rows, :]) * _dgelu(gate)).astype(BF16)

    t = pl.BlockSpec((TM, RET_W), lambda i: (tile(i), 0))
    return _Epilogue(
        fn, (o_f, o_b, p, p, h_f, h_b),
        in_specs=[t, t, pl.BlockSpec((TM, RET_W), lambda i: (tile(i), COL_G)),
                  pl.BlockSpec((TM, LRU_W), lambda i: (tile(i), COL_GATE)), t, t],
        out_specs=(t, t, t, t), out_shape=(_sds((n, RET_W), BF16),) * 3 + (_sds((n, RET_W), F32),),
        steps=n // TM, lhs_map=lambda i: (jnp.maximum(i - cb, 0), 0), delay=1)


def _bwd_norm1_epilogue(ctx, x, dx1, g, modrows, cb):
    t_len, d = x.shape

    def fn(dh, rows, ctx_ref, x_ref, dx1_ref, g_ref, m_ref, gx_ref, acc_ref):
        is_ctx = pl.program_id(0) < cb
        _zero_at_start(acc_ref, rows)
        xin = jnp.where(is_ctx, ctx_ref[rows, :], x_ref[rows, :])
        sc = jnp.where(is_ctx, m_ref[R_CSC1:R_CSC1 + 1, :], m_ref[R_SC1:R_SC1 + 1, :])
        rstd = lax.rsqrt(jnp.mean(xin * xin, axis=-1, keepdims=True) + EPS)
        xh = xin * rstd
        gn = g_ref[...]
        dn = dh * (1.0 + sc)
        dxh = dn * gn
        gx_ref[rows, :] = dx1_ref[rows, :] + rstd * (dxh - xh * jnp.mean(dxh * xh, axis=-1, keepdims=True))
        s0 = jnp.sum(dh, axis=0, keepdims=True)
        s1 = jnp.sum(dh * xh * gn, axis=0, keepdims=True)
        acc_ref[4:5, :] += jnp.sum(dn * xh, axis=0, keepdims=True)
        acc_ref[0:1, :] += jnp.where(is_ctx, s0, 0.0)
        acc_ref[1:2, :] += jnp.where(is_ctx, s1, 0.0)
        acc_ref[2:3, :] += jnp.where(is_ctx, 0.0, s0)
        acc_ref[3:4, :] += jnp.where(is_ctx, 0.0, s1)

    lat = pl.BlockSpec((TM, d), lambda i: (jnp.maximum(i - cb, 0), 0))
    return _Epilogue(
        fn, (ctx, x, dx1, g, modrows),
        in_specs=[pl.BlockSpec((TM, d), lambda i: (jnp.minimum(i, cb - 1), 0)),
                  lat, lat, pl.BlockSpec((1, d), lambda i: (0, 0)), pl.BlockSpec((8, d), lambda i: (0, 0))],
        out_specs=(lat, pl.BlockSpec((8, d), lambda i: (0, 0))),
        out_shape=(_sds((t_len, d), F32), _sds((8, d), F32)))


def _rot(x, cf, ss):
    return x * cf + pltpu.roll(x, HEAD_DIM // 2, 1) * ss


def _decay_exponents(dirn):
    ii = lax.broadcasted_iota(jnp.int32, (CHUNK, CHUNK), 0)
    jj = lax.broadcasted_iota(jnp.int32, (CHUNK, CHUNK), 1)
    rel = ii - jj if dirn == 0 else jj - ii
    pos = ii.astype(F32)
    if dirn == 0:
        cq, cs = pos + 1.0, (CHUNK - 1.0) - pos
    else:
        cq, cs = CHUNK - pos, pos
    return rel, jnp.maximum(rel, 0).astype(F32), cq, cs


def _store_decay(lg_ref, dec, relf_ref=None):
    for dirn in (0, 1):
        rel, relf, cq, cs = _decay_exponents(dirn)
        if relf_ref is not None:
            relf_ref[dirn] = relf
        for h in range(HEADS):
            lgv = lg_ref[dirn, h]
            wq, ws = jnp.exp(lgv * cq), jnp.exp(lgv * cs)
            dec[dirn, h, 0] = jnp.where(rel >= 0, jnp.exp(lgv * relf), 0.0)
            dec[dirn, h, 1] = wq
            dec[dirn, h, 2] = ws
            if relf_ref is not None:
                dec[dirn, h, 3] = wq * cq
                dec[dirn, h, 4] = ws * cs


def _ret_rows(cc, nc, step_of):
    return [lambda s, dirn=dirn: _tile_order(dirn, step_of(s), cc, nc) for dirn in (0, 1)]


def _ret_in_specs(rows):
    specs = []
    for row in rows:
        specs += [pl.BlockSpec((CHUNK, RET_W), lambda s, o=o, row=row: (row(s), o)) for o in (0, 1, 2)]
    return specs


def _ret_fwd(qkv, lg, cc, ride=None):
    n = qkv.shape[0]
    nc = n // CHUNK
    rows = _ret_rows(cc, nc, lambda s: s)

    def body(lg_ref, q0, k0, v0, q1, k1, v1, o0, o1, sp0, sp1, st, dec):
        @pl.when(pl.program_id(0) == 0)
        def _():
            st[...] = jnp.zeros_like(st)
            _store_decay(lg_ref, dec)

        refs = ((q0, k0, v0, o0, sp0), (q1, k1, v1, o1, sp1))
        chains = [(dirn, h, slice(h * HEAD_DIM, (h + 1) * HEAD_DIM)) for dirn in (0, 1) for h in range(HEADS)]
        scores, cross, update = [], [], []
        for dirn, h, sl in chains:
            q_ref, k_ref, v_ref, _, sp_ref = refs[dirn]
            q, k, v = q_ref[:, sl], k_ref[:, sl], v_ref[:, sl]
            sp = st[dirn, h]
            sp_ref[h] = sp
            scores.append(_dot_nt(q, k))
            cross.append(_dot(q * dec[dirn, h, 1], sp))
            update.append(_dot_tn(k * dec[dirn, h, 2], v))
        masked = [(a * dec[dirn, h, 0]).astype(BF16) for a, (dirn, h, _) in zip(scores, chains)]
        intra = [_dot(sc, refs[dirn][2][:, sl]) for sc, (dirn, h, sl) in zip(masked, chains)]
        for (dirn, h, sl), o_in, o_cr, upd in zip(chains, intra, cross, update):
            refs[dirn][3][:, sl] = o_in + o_cr
            st[dirn, h] = jnp.exp(lg_ref[dirn, h] * CHUNK) * st[dirn, h] + upd

    o_specs = [pl.BlockSpec((CHUNK, RET_W), lambda s, row=row: (row(s), 0)) for row in rows]
    state = pl.BlockSpec((None, HEADS, CHUNK, HEAD_DIM), lambda s: (s, 0, 0, 0))
    return _pcall_ride(
        body, ride, (lg,) + (qkv,) * 6, name="ret_fwd", grid=(nc,),
        in_specs=[pl.BlockSpec(memory_space=pltpu.SMEM)] + _ret_in_specs(rows),
        out_specs=(o_specs[0], o_specs[1], state, state),
        out_shape=(_sds((n, RET_W), F32),) * 2 + (_sds((nc, HEADS, CHUNK, HEAD_DIM), F32),) * 2,
        scratch_shapes=[pltpu.VMEM((2, HEADS, CHUNK, HEAD_DIM), F32), pltpu.VMEM((2, HEADS, 3, CHUNK, CHUNK), F32)],
        compiler_params=_cp())


def _ret_bwd(qkv, lg, do, s_prev, cc, ride=None):
    n = qkv.shape[0]
    nc = n // CHUNK
    rows = _ret_rows(cc, nc, lambda s: nc - 1 - s)

    def body(lg_ref, q0, k0, v0, q1, k1, v1, do0, do1, sp0, sp1, dq0, dk0, dv0, dq1, dk1, dv1, dlg_ref, dst, dec, relf):
        @pl.when(pl.program_id(0) == 0)
        def _():
            dst[...] = jnp.zeros_like(dst)
            dlg_ref[...] = jnp.zeros_like(dlg_ref)
            _store_decay(lg_ref, dec, relf)

        refs = ((q0, k0, v0, do0, sp0, dq0, dk0, dv0), (q1, k1, v1, do1, sp1, dq1, dk1, dv1))
        chains = [(dirn, h, slice(h * HEAD_DIM, (h + 1) * HEAD_DIM)) for dirn in (0, 1) for h in range(HEADS)]

        def tiles(dirn, sl):
            q_ref, k_ref, v_ref, do_ref = refs[dirn][:4]
            return q_ref[:, sl], k_ref[:, sl], v_ref[:, sl], do_ref[:, sl]

        a_s, g1_s, da_s, h1_s = [], [], [], []
        for dirn, h, sl in chains:
            q, k, v, dov = tiles(dirn, sl)
            a_s.append(_dot_nt(q, k))
            g1_s.append(_dot_nt(dov, refs[dirn][4][h]))
            da_s.append(_dot_nt(dov, v))
            h1_s.append(_dot_nt(v, dst[dirn, h]))
        da_s = [da * dec[dirn, h, 0] for da, (dirn, h, _) in zip(da_s, chains)]
        dq_s, dk_s, dv_s, ds_s = [], [], [], []
        for (dirn, h, sl), a, da in zip(chains, a_s, da_s):
            q, k, v, dov = tiles(dirn, sl)
            dq_s.append(_dot(da, k))
            dk_s.append(_dot_tn(da, q))
            dv_s.append(_dot_tn(a * dec[dirn, h, 0], dov) + _dot(k * dec[dirn, h, 2], dst[dirn, h]))
            ds_s.append(_dot_tn(q * dec[dirn, h, 1], dov))
        for (dirn, h, sl), a, g1, da, h1, dq2, dk2, dv, ds2 in zip(chains, a_s, g1_s, da_s, h1_s, dq_s, dk_s, dv_s, ds_s):
            q, k, _, _ = tiles(dirn, sl)
            dq_ref, dk_ref, dv_ref = refs[dirn][5:]
            sp, dsn = refs[dirn][4][h], dst[dirn, h]
            gc = jnp.exp(lg_ref[dirn, h] * CHUNK)
            dq_ref[:, sl] = (g1 * dec[dirn, h, 1] + dq2).astype(BF16)
            dk_ref[:, sl] = (dk2 + h1 * dec[dirn, h, 2]).astype(BF16)
            dv_ref[:, sl] = dv.astype(BF16)
            term = da * a * relf[dirn] + q * g1 * dec[dirn, h, 3] + k * h1 * dec[dirn, h, 4] + sp * dsn * (CHUNK * gc)
            dlg_ref[dirn * HEADS + h:dirn * HEADS + h + 1, 0:HEAD_DIM] += jnp.sum(term, axis=0, keepdims=True)
            dst[dirn, h] = gc * dsn + ds2

    wide = [pl.BlockSpec((CHUNK, RET_W), lambda s, row=row: (row(s), 0)) for row in rows]
    state = pl.BlockSpec((None, HEADS, CHUNK, HEAD_DIM), lambda s: (nc - 1 - s, 0, 0, 0))
    return _pcall_ride(
        body, ride, (lg,) + (qkv,) * 6 + (do, do, s_prev[0], s_prev[1]), name="ret_bwd", grid=(nc,),
        in_specs=[pl.BlockSpec(memory_space=pltpu.SMEM)] + _ret_in_specs(rows) + wide + [state, state],
        out_specs=(wide[0],) * 3 + (wide[1],) * 3 + (pl.BlockSpec((2 * HEADS, 8 * HEAD_DIM), lambda s: (0, 0)),),
        out_shape=(_sds((n, RET_W), BF16),) * 6 + (_sds((2 * HEADS, 8 * HEAD_DIM), F32),),
        scratch_shapes=[pltpu.VMEM((2, HEADS, CHUNK, HEAD_DIM), F32), pltpu.VMEM((2, HEADS, 5, CHUNK, CHUNK), F32),
                        pltpu.VMEM((2, CHUNK, CHUNK), F32)],
        compiler_params=_cp())


def _shift_rows(cur, prev8, next8, k, seg_start, seg_end):
    tm = cur.shape[0]
    rows = _rows_iota(cur.shape)
    if k < 0:
        out = pltpu.roll(cur, -k, 0)
        for j in range(-k):
            halo = jnp.where(seg_start, 0.0, prev8[SUB + k + j:SUB + k + j + 1, :])
            out = jnp.where(rows == j, halo, out)
    else:
        out = pltpu.roll(cur, tm - k, 0)
        for j in range(k):
            halo = jnp.where(seg_end, 0.0, next8[j:j + 1, :])
            out = jnp.where(rows == tm - k + j, halo, out)
    return out


def _seg_flags(t, cb, nb):
    return jnp.logical_or(t == 0, t == cb), jnp.logical_or(t == cb - 1, t == nb - 1)


def _halo_specs(tile_of, n_rows, col):
    per = TM // SUB
    return [pl.BlockSpec((TM, LRU_W), lambda s: (tile_of(s), col)),
            pl.BlockSpec((SUB, LRU_W), lambda s: (jnp.maximum(tile_of(s) * per - 1, 0), col)),
            pl.BlockSpec((SUB, LRU_W), lambda s: (jnp.minimum((tile_of(s) + 1) * per, n_rows // SUB - 1), col))]


def _lru_gates(xr, prev8, next8, seg_start, seg_end, cw_ref, cb_ref, wg_ref, bg_ref, sp_ref):
    xm1 = _shift_rows(xr, prev8, next8, -1, seg_start, seg_end)
    xp1 = _shift_rows(xr, prev8, next8, 1, seg_start, seg_end)
    xp2 = _shift_rows(xr, prev8, next8, 2, seg_start, seg_end)
    xc = cb_ref[...] + xm1 * cw_ref[0:1, :] + xr * cw_ref[1:2, :] + xp1 * cw_ref[2:3, :] + xp2 * cw_ref[3:4, :]
    pre = _dot(xc, wg_ref[...]) + bg_ref[...]
    r = _sigmoid(pre[:, :LRU_W])
    i = _sigmoid(pre[:, LRU_W:])
    la = (-LRU_C) * r * sp_ref[...]
    a = jnp.exp(la)
    th = jnp.tanh(la)
    sq = jnp.sqrt(-2.0 * th / (1.0 - th))
    return xc, r, i, a, sq


def _scan_tile(a, b, ascending, carry, out_ref):
    tm, w = a.shape
    nsub = tm // SUB
    a = a.reshape(nsub, SUB, w)
    b = b.reshape(nsub, SUB, w)
    r8 = lax.broadcasted_iota(jnp.int32, a.shape, 1)
    for k in (1, 2, 4):
        if ascending:
            m = r8 >= k
            a_s, b_s = pltpu.roll(a, k, 1), pltpu.roll(b, k, 1)
        else:
            m = r8 < SUB - k
            a_s, b_s = pltpu.roll(a, SUB - k, 1), pltpu.roll(b, SUB - k, 1)
        b = a * jnp.where(m, b_s, 0.0) + b
        a = a * jnp.where(m, a_s, 1.0)
    c = carry[...]
    for j in range(nsub):
        off = (j if ascending else nsub - 1 - j) * SUB
        hb = a[off // SUB] * c + b[off // SUB]
        out_ref[off:off + SUB, :] = hb
        c = jnp.broadcast_to(hb[SUB - 1:SUB, :] if ascending else hb[0:1, :], c.shape)
    carry[...] = c


def _lru_fwd(p, wg, bg, sp, cw, cbias, dirn, cb, ride=None):
    n = p.shape[0]
    nb = n // TM
    tile_of = lambda s: _tile_order(dirn, s, cb, nb)

    def body(x_ref, xp_ref, xn_ref, wg_ref, bg_ref, sp_ref, cw_ref, cb_ref, h_ref, cin_ref, carry):
        s = pl.program_id(0)

        @pl.when(s == 0)
        def _():
            carry[...] = jnp.zeros_like(carry)

        seg_start, seg_end = _seg_flags(tile_of(s), cb, nb)
        xc, r, i, a, sq = _lru_gates(x_ref[...], xp_ref[...], xn_ref[...], seg_start, seg_end,
                                     cw_ref, cb_ref, wg_ref, bg_ref, sp_ref)
        cin_ref[...] = carry[...]
        _scan_tile(a, sq * (i * xc), dirn == 0, carry, h_ref)

    full = lambda shape: pl.BlockSpec(shape, lambda s: (0,) * len(shape))
    return _pcall_ride(
        body, ride, (p, p, p, wg, bg, sp, cw, cbias), name=f"lru_fwd{dirn}", grid=(nb,),
        in_specs=_halo_specs(tile_of, n, COL_XR) + [full((LRU_W, 2 * LRU_W)), full((1, 2 * LRU_W)), full((1, LRU_W)),
                                               full((4, LRU_W)), full((1, LRU_W))],
        out_specs=(pl.BlockSpec((TM, LRU_W), lambda s: (tile_of(s), 0)),
                   pl.BlockSpec((None, SUB, LRU_W), lambda s: (tile_of(s), 0, 0))),
        out_shape=(_sds((n, LRU_W), F32), _sds((nb, SUB, LRU_W), F32)),
        scratch_shapes=[pltpu.VMEM((SUB, LRU_W), F32)], compiler_params=_cp())


def _lru_bwd(p, wg, bg, sp, cw, cbias, h, cin, dhs, dirn, cb, ride=None):
    n = p.shape[0]
    nb = n // TM
    tile_of = lambda s: _tile_order(dirn, nb - 1 - s, cb, nb)

    def body(x_ref, xp_ref, xn_ref, wg_ref, bg_ref, sp_ref, cw_ref, cb_ref, h_ref, cin_ref, dhs_ref,
             dxc_ref, dwd_ref, acc_ref, carry, mu_sc, dwg_ref):
        s = pl.program_id(0)

        @pl.when(s == 0)
        def _():
            carry[...] = jnp.zeros_like(carry)
            dwg_ref[...] = jnp.zeros_like(dwg_ref)
            acc_ref[...] = jnp.zeros_like(acc_ref)

        seg_start, seg_end = _seg_flags(tile_of(s), cb, nb)
        xc, r, i, a, sq = _lru_gates(x_ref[...], xp_ref[...], xn_ref[...], seg_start, seg_end,
                                     cw_ref, cb_ref, wg_ref, bg_ref, sp_ref)
        rows = _rows_iota(a.shape)
        hv = h_ref[...]
        dh = dhs_ref[...]
        mu_next = carry[0:1, :]
        _scan_tile(a, a * dh, dirn == 1, carry, mu_sc)
        mu = mu_sc[...]
        if dirn == 0:
            hprev = jnp.where(rows == 0, cin_ref[0:1, :], pltpu.roll(hv, 1, 0))
            lam = dh + jnp.where(rows == TM - 1, mu_next, pltpu.roll(mu, TM - 1, 0))
        else:
            hprev = jnp.where(rows == TM - 1, cin_ref[0:1, :], pltpu.roll(hv, TM - 1, 0))
            lam = dh + jnp.where(rows == 0, mu_next, pltpu.roll(mu, 1, 0))
        ds = lam * (i * xc)
        di = lam * (sq * xc)
        dla = lam * hprev * a - ds * (a * a) / jnp.maximum(sq, 1e-20)
        dpr = dla * ((-LRU_C) * sp_ref[...]) * r * (1.0 - r)
        dpi = di * i * (1.0 - i)
        dpre = jnp.concatenate([dpr, dpi], axis=1)
        dxc_ref[...] = lam * (sq * i) + _dot_nt(dpre, wg_ref[...])
        dwg_ref[...] += _dot_tn(xc, dpre)
        acc_ref[0:1, :] += jnp.sum(dpre, axis=0, keepdims=True)
        acc_ref[1:2, 0:LRU_W] += jnp.sum(dla * ((-LRU_C) * r), axis=0, keepdims=True)

        @pl.when(s == nb - 1)
        def _():
            low = lax.broadcasted_iota(jnp.int32, (LRU_BD, 2 * LRU_BD), 1) < LRU_BD
            for half in (0, LRU_W):
                for m in range(LRU_BLOCKS // 2):
                    lanes = slice(half + 2 * LRU_BD * m, half + 2 * LRU_BD * (m + 1))
                    even = dwg_ref[2 * m * LRU_BD:(2 * m + 1) * LRU_BD, lanes]
                    odd = dwg_ref[(2 * m + 1) * LRU_BD:(2 * m + 2) * LRU_BD, lanes]
                    dwd_ref[:, lanes] = jnp.where(low, even, odd)

    full = lambda shape: pl.BlockSpec(shape, lambda s: (0,) * len(shape))
    tile = pl.BlockSpec((TM, LRU_W), lambda s: (tile_of(s), 0))
    return _pcall_ride(
        body, ride, (p, p, p, wg, bg, sp, cw, cbias, h, cin, dhs), name=f"lru_bwd{dirn}", grid=(nb,),
        in_specs=_halo_specs(tile_of, n, COL_XR) + [full((LRU_W, 2 * LRU_W)), full((1, 2 * LRU_W)), full((1, LRU_W)),
                                               full((4, LRU_W)), full((1, LRU_W)), tile,
                                               pl.BlockSpec((None, SUB, LRU_W), lambda s: (tile_of(s), 0, 0)), tile],
        out_specs=(tile, full((LRU_BD, 2 * LRU_W)), full((8, 2 * LRU_W))),
        out_shape=(_sds((n, LRU_W), F32), _sds((LRU_BD, 2 * LRU_W), F32), _sds((8, 2 * LRU_W), F32)),
        scratch_shapes=[pltpu.VMEM((SUB, LRU_W), F32), pltpu.VMEM((TM, LRU_W), F32), pltpu.VMEM((LRU_W, 2 * LRU_W), F32)],
        compiler_params=_cp())


def _assemble_dp(dqs, dks, dvs, dg, dgate, dxcs, p, cw, cosf, sins, cb, ride=None):
    n = p.shape[0]
    nb = n // TM
    tile_of = lambda s: s

    def body(dqf, dqb, dkf, dkb, dvf, dvb, dg_ref, dgate_ref, cf, pf, nf, cb_, pb, nb_, x_ref, xp_ref, xn_ref,
             cw_ref, cos_ref, sin_ref, dp_ref, acc_ref):
        s = pl.program_id(0)

        @pl.when(s == 0)
        def _():
            acc_ref[...] = jnp.zeros_like(acc_ref)

        seg_start, seg_end = _seg_flags(s, cb, nb)
        dq = dqf[...].astype(F32) + dqb[...].astype(F32)
        dk = dkf[...].astype(F32) + dkb[...].astype(F32)
        cosv, sinv = cos_ref[...], sin_ref[...]
        for h in range(HEADS):
            sl = slice(h * HEAD_DIM, (h + 1) * HEAD_DIM)
            sk = slice(RET_W + h * HEAD_DIM, RET_W + (h + 1) * HEAD_DIM)
            dp_ref[:, sl] = (dq[:, sl] * cosv + pltpu.roll(dq[:, sl] * sinv, HEAD_DIM // 2, 1)).astype(BF16)
            dp_ref[:, sk] = ((dk[:, sl] * cosv + pltpu.roll(dk[:, sl] * sinv, HEAD_DIM // 2, 1)) * K_SCALE).astype(BF16)
        dp_ref[:, 2 * RET_W:3 * RET_W] = (dvf[...].astype(F32) + dvb[...].astype(F32)).astype(BF16)
        dp_ref[:, 3 * RET_W:4 * RET_W] = dg_ref[...].astype(BF16)
        dxc = cf[...] + cb_[...]
        dprev = pf[...] + pb[...]
        dnext = nf[...] + nb_[...]
        dxr = (_shift_rows(dxc, dprev, dnext, 1, seg_start, seg_end) * cw_ref[0:1, :] + dxc * cw_ref[1:2, :]
               + _shift_rows(dxc, dprev, dnext, -1, seg_start, seg_end) * cw_ref[2:3, :]
               + _shift_rows(dxc, dprev, dnext, -2, seg_start, seg_end) * cw_ref[3:4, :])
        dp_ref[:, 4 * RET_W:4 * RET_W + LRU_W] = dxr.astype(BF16)
        dp_ref[:, 4 * RET_W + LRU_W:] = dgate_ref[...].astype(BF16)
        xr, xp, xn = x_ref[...], xp_ref[...], xn_ref[...]
        for j, k in enumerate((-1, 0, 1, 2)):
            xs = xr if k == 0 else _shift_rows(xr, xp, xn, k, seg_start, seg_end)
            acc_ref[j:j + 1, 0:LRU_W] += jnp.sum(dxc * xs, axis=0, keepdims=True)
        acc_ref[4:5, 0:LRU_W] += jnp.sum(dxc, axis=0, keepdims=True)

    t = pl.BlockSpec((TM, RET_W), lambda s: (s, 0))
    args = (dqs[0], dqs[1], dks[0], dks[1], dvs[0], dvs[1], dg, dgate, dxcs[0], dxcs[0], dxcs[0], dxcs[1], dxcs[1], dxcs[1],
            p, p, p, cw, cosf, sins)
    tab = pl.BlockSpec((TM, HEAD_DIM), lambda s: (s, 0))
    return _pcall_ride(
        body, ride, args, name="assemble_dp", grid=(nb,),
        in_specs=[t] * 8 + _halo_specs(tile_of, n, 0) * 2 + _halo_specs(tile_of, n, COL_XR)
        + [pl.BlockSpec((4, LRU_W), lambda s: (0, 0)), tab, tab],
        out_specs=(pl.BlockSpec((TM, 4 * RET_W + 2 * LRU_W), lambda s: (s, 0)), pl.BlockSpec((8, 2 * LRU_W), lambda s: (0, 0))),
        out_shape=(_sds((n, 4 * RET_W + 2 * LRU_W), BF16), _sds((8, 2 * LRU_W), F32)), compiler_params=_cp())


def _adamw(g, w, m, v):
    nm = ADAM_B1 * m + (1.0 - ADAM_B1) * g
    nv = ADAM_B2 * v + (1.0 - ADAM_B2) * (g * g)
    m_hat = nm / (1.0 - ADAM_B1 ** ADAM_STEP)
    v_hat = nv / (1.0 - ADAM_B2 ** ADAM_STEP)
    return (-ADAM_LR) * (m_hat / (jnp.sqrt(v_hat) + ADAM_EPS) + ADAM_WD * w), nm, nv


def _adam_many(items, name):
    n = len(items)

    def body(*refs):
        for i in range(n):
            g, w, m, v = (r[...] for r in refs[4 * i:4 * i + 4])
            for o_ref, val in zip(refs[4 * n + 3 * i:4 * n + 3 * i + 3], _adamw(g, w, m, v)):
                o_ref[...] = val

    out_shape = tuple(_sds(it[1].shape, F32) for it in items for _ in range(3))
    res = _pcall(body, name=name, out_shape=out_shape, compiler_params=_cp())(*[a for it in items for a in it])
    return [tuple(res[3 * i:3 * i + 3]) for i in range(n)]


def _sum_adam(parts_list, w, m, v, name):
    nparts, _, c = parts_list[0].shape
    r = w.shape[0]
    tr = min(parts_list[0].shape[1], 512)
    starts, o = [], 0
    for pa in parts_list:
        starts.append(o)
        o += pa.shape[1] // tr
    nseg = len(parts_list)

    def body(*refs):
        p_refs = refs[:nseg]
        w_ref, m_ref, v_ref, g_ref, d_ref, nm_ref, nv_ref = refs[nseg:]
        i = pl.program_id(0)
        for s, p_ref in enumerate(p_refs):
            end = starts[s + 1] if s + 1 < nseg else r // tr

            @pl.when(jnp.logical_and(i >= starts[s], i < end))
            def _():
                g = p_ref[0].astype(F32)
                for j in range(1, nparts):
                    g = g + p_ref[j].astype(F32)
                g_ref[...] = g
                d_ref[...], nm_ref[...], nv_ref[...] = _adamw(g, w_ref[...], m_ref[...], v_ref[...])

    def seg_spec(s):
        last = parts_list[s].shape[1] // tr - 1
        return pl.BlockSpec((nparts, tr, c), lambda i: (0, jnp.clip(i - starts[s], 0, last), 0))

    t = pl.BlockSpec((tr, c), lambda i: (i, 0))
    return _pcall(
        body, name=name, grid=(r // tr,),
        in_specs=[seg_spec(s) for s in range(nseg)] + [t, t, t],
        out_specs=(t, t, t, t), out_shape=(_sds((r, c), F32),) * 4, compiler_params=_cp(),
    )(*parts_list, w, m, v)


def _sum_parts(parts, name):
    nparts, r, c = parts.shape

    def body(p_ref, o_ref):
        g = p_ref[0]
        for j in range(1, nparts):
            g = g + p_ref[j]
        o_ref[...] = g

    return _pcall(body, name=name, out_shape=_sds((r, c), parts.dtype), compiler_params=_cp())(parts)


def _rot_tables(l_len, t_len):
    rows = t_len // GRID_W
    n_freq = HEAD_DIM // 4
    inv = ROPE_BASE ** (-jnp.arange(n_freq, dtype=F32) / n_freq)
    ang_r = jnp.arange(rows, dtype=F32)[:, None] * inv
    ang_c = jnp.arange(GRID_W, dtype=F32)[:, None] * inv
    cos = jnp.concatenate([jnp.repeat(jnp.cos(ang_r), GRID_W, axis=0), jnp.tile(jnp.cos(ang_c), (rows, 1))], axis=-1)
    sin = jnp.concatenate([jnp.repeat(jnp.sin(ang_r), GRID_W, axis=0), jnp.tile(jnp.sin(ang_c), (rows, 1))], axis=-1)
    cosf = jnp.concatenate([jnp.ones((l_len, HEAD_DIM), F32), jnp.concatenate([cos, cos], axis=-1)], axis=0)
    sins = jnp.concatenate([jnp.zeros((l_len, HEAD_DIM), F32), jnp.concatenate([-sin, sin], axis=-1)], axis=0)
    return cosf, sins


def _block_diag(w):
    eye = jnp.eye(LRU_BLOCKS, dtype=w.dtype)
    return (w[:, :, None, :] * eye[:, None, :, None]).reshape(LRU_W, LRU_W)


def _blocks_from_lanes(dwd_half):
    return dwd_half.reshape(LRU_BD, LRU_BLOCKS, LRU_BD).transpose(1, 0, 2)


def _silu(x):
    return x * jax.nn.sigmoid(x)


def kernel(x, c, ctx, c_ctx, w_ada, b_ada, norm1_g, norm2_g, w_in, ret_decay, conv_w, conv_b, lru_wa, lru_ba, lru_wx, lru_bx, lru_lambda, w_out, w_mlp1, w_mlp2, final_g, loss_target, m_c_ctx, m_w_ada, m_b_ada, m_norm1_g, m_norm2_g, m_w_in, m_ret_decay, m_conv_w, m_conv_b, m_lru_wa, m_lru_ba, m_lru_wx, m_lru_bx, m_lru_lambda, m_w_out, m_w_mlp1, m_w_mlp2, m_final_g, v_c_ctx, v_w_ada, v_b_ada, v_norm1_g, v_norm2_g, v_w_in, v_ret_decay, v_conv_w, v_conv_b, v_lru_wa, v_lru_ba, v_lru_wx, v_lru_bx, v_lru_lambda, v_w_out, v_w_mlp1, v_w_mlp2, v_final_g):
    t_len, d = x.shape[1], x.shape[2]
    l_len = ctx.shape[1]
    cb, cc = l_len // TM, l_len // CHUNK
    me = 4 * lax.axis_index("x") + 2 * lax.axis_index("y") + lax.axis_index("c")
    x2d, ctx2d, tgt2d = x[0], ctx[0], loss_target[0]
    ada_cols = w_ada.shape[2]
    wa2d = w_ada[0]

    sc_loc = conv_w.shape[2]
    pack_a = jnp.zeros((8, d), F32)
    pack_a = pack_a.at[0].set(_silu(c[0]))
    pack_a = pack_a.at[1, :4 * sc_loc].set(conv_w[0].reshape(-1))
    pack_a = pack_a.at[2, :2 * sc_loc].set(lru_ba[0].reshape(-1))
    pack_a = pack_a.at[3, :2 * sc_loc].set(lru_bx[0].reshape(-1))
    pack_a = pack_a.at[4, :2 * sc_loc].set(lru_lambda[0].reshape(-1))
    b_cols = lax.dynamic_slice(b_ada, (0, me * ada_cols), (1, ada_cols))
    all_a, s16, mod_parts = _mod_exchange(pack_a, _silu(c_ctx)[None, :], wa2d, b_cols)

    def unshard(row, k):
        return all_a[:, row, :k * sc_loc].reshape(N_DEV, k, sc_loc).transpose(1, 0, 2).reshape(k, N_DEV * sc_loc)

    conv_w_full = unshard(1, 4)
    ba_full, bx_full, lam_full = unshard(2, 2), unshard(3, 2), unshard(4, 2)

    mod_all = mod_parts.transpose(1, 0, 2).reshape(16, N_DEV * ada_cols)
    mod_me = lax.dynamic_slice(mod_all, (me, 0), (1, 6 * d)).reshape(6, d)
    mod_c = mod_all[8].reshape(6, d)
    modrows = jnp.concatenate([mod_c[0:2], mod_me], axis=0)

    lg = jax.nn.log_sigmoid(ret_decay[0])
    sp = jax.nn.softplus(-lam_full)
    wg = [jnp.concatenate([_block_diag(lru_wa[0, dd]), _block_diag(lru_wx[0, dd])], axis=1).astype(BF16) for dd in (0, 1)]
    bg = [jnp.concatenate([ba_full[dd], bx_full[dd]])[None, :] for dd in (0, 1)]
    cosf, sins = _rot_tables(l_len, t_len)

    (hn, hnt), win_g = _norm1_fwd(ctx2d, x2d, norm1_g, modrows, cb, ride=("gather", w_in[0].astype(BF16)))
    (qkv, p), w2_g = _mm_in(hn, win_g, cosf, sins, ride=("gather", w_mlp2[0].astype(BF16)))
    w2_g = w2_g.reshape(1, 4 * d, d)
    (o0, o1, sp0, sp1), w1_g = _ret_fwd(qkv, lg, cc, ride=("gather", w_mlp1[0].astype(BF16)))
    o, s_prev = [o0, o1], [sp0, sp1]
    (h0, cin0), wout_g = _lru_fwd(p, wg[0], bg[0], sp[0:1], conv_w_full, conv_b, 0, cb, ride=("gather", w_out[0].astype(BF16)))
    wout_g = wout_g.reshape(1, d, d)
    (h1, cin1), _ = _lru_fwd(p, wg[1], bg[1], sp[1:2], conv_w_full, conv_b, 1, cb)
    h, cin = [h0, h1], [cin0, cin1]
    mix = _mix_fwd(o[0], o[1], p, h[0], h[1], cb, t_len)
    y, x1, h2, h2t = _mm_nn(mix, wout_g, F32, "mm_out_norm2", tm=TL, epi=_res_norm2_epilogue(x2d, norm2_g, modrows, TL))
    r = _mm_nn(h2, w1_g, BF16, "mm_mlp1", relu_out=True, tm=2 * TL)
    dx2, dz, dzt, facc = _mm_nn(r, w2_g, F32, "mm_mlp2_final", square_lhs=True, tm=TL, vmem_mb=58,
                                epi=_final_epilogue(x1, tgt2d, final_g[None, :], modrows, TL))

    du = _mm_nt(dz, w2_g, BF16, "mm_da2", relu_mul=r, tm=TL, vmem_mb=58)
    gw2_lo, gw2_hi = _mm_wgrad(dzt, r, "mm_dw2", w2_g.shape[1] // N_DEV, BF16, transpose_out=True, square_rhs=True,
                               halves=True)
    gw1_lo, gw1_hi = _mm_wgrad(h2t, du, "mm_dw1", w1_g.shape[2], BF16, halves=True)
    (dx1, dy, n2acc), gw2_lo_all = _mm_nt(du, w1_g, F32, "mm_dh2_norm2", ride=("a2a", gw2_lo), tm=TL, vmem_mb=60,
                                          epi=_bwd_norm2_epilogue(x1, dx2, y, norm2_g, modrows, TL))
    gwo = _mm_tn(mix, dy, "mm_dwout", False, d, BF16, 512).reshape(N_DEV, -1, d)
    (do, dg, dgate, dhs), gwo_all = _mm_nt(dy, wout_g, F32, "mm_dmix_mix", ride=("a2a", gwo),
                                           epi=_mix_bwd_epilogue(o[0], o[1], p, h[0], h[1], cb))
    dxcs, dwgs, laccs, rides, got = [], [], [], [("a2a", gw1_hi), ("a2a", gw2_hi)], []
    (dq0, dk0, dv0, dq1, dk1, dv1, dlg_lanes), gw1_lo_all = _ret_bwd(qkv, lg, do, s_prev, cc, ride=("a2a", gw1_lo))
    dqs, dks, dvs = [dq0, dq1], [dk0, dk1], [dv0, dv1]
    for dd in (0, 1):
        (dxc_, dwg_, lacc_), got_ = _lru_bwd(p, wg[dd], bg[dd], sp[dd:dd + 1], conv_w_full, conv_b, h[dd], cin[dd], dhs, dd, cb,
                                             ride=rides[dd])
        dxcs.append(dxc_); dwgs.append(dwg_); laccs.append(lacc_); got.append(got_)
    gw1_hi_all, gw2_hi_all = got
    gw1_all, gw2_all = [gw1_lo_all, gw1_hi_all], [gw2_lo_all, gw2_hi_all]
    (dp, cacc), _ = _assemble_dp(dqs, dks, dvs, dg, dgate, dxcs, p, conv_w_full, cosf, sins, cb)
    pack_b = jnp.concatenate([n2acc, facc, cacc, laccs[0], laccs[1], dlg_lanes, dwgs[0], dwgs[1]], axis=0)
    gwi, all_b = _mm_wgrad(hnt, dp, "mm_dwin", win_g.shape[2], BF16, ride=("gather", pack_b))
    (grad_x, n1acc), gwi_all = _mm_nt(dp, win_g, F32, "mm_dhn_norm1", ride=("a2a", gwi),
                                      epi=_bwd_norm1_epilogue(ctx2d, x2d, dx1, norm1_g, modrows, cb))
    all_n1 = _all_gather_small(n1acc, "gather_norm1_grads")
    tot = _sum_parts(all_b, "sum_small_grads")
    t_n1 = _sum_parts(all_n1, "sum_norm1_grads")
    t_n2, t_f, t_conv, t_dlg = tot[0:8], tot[8:16], tot[16:24, :LRU_W], tot[40:48, :HEAD_DIM]
    t_l = [tot[24:32], tot[32:40]]
    t_dwd = [tot[48:48 + LRU_BD], tot[48 + LRU_BD:48 + 2 * LRU_BD]]
    loss = (0.5 / d) * jnp.sum(t_f[2])
    t_wa = jnp.stack([_blocks_from_lanes(t_dwd[dd][:, :LRU_W]) for dd in (0, 1)])
    t_wx = jnp.stack([_blocks_from_lanes(t_dwd[dd][:, LRU_W:]) for dd in (0, 1)])
    t_ba = jnp.stack([t_l[dd][0, :LRU_W] for dd in (0, 1)])
    t_bx = jnp.stack([t_l[dd][0, LRU_W:] for dd in (0, 1)])
    t_sp = jnp.stack([t_l[dd][1, :LRU_W] for dd in (0, 1)])
    dm_rows = jnp.stack([all_n1[:, 2, :], all_n1[:, 3, :], all_b[:, 3, :], all_b[:, 0, :], all_b[:, 1, :], all_b[:, 9, :]],
                        axis=1).reshape(N_DEV, 6 * d)
    dm_c = jnp.concatenate([t_n1[0], t_n1[1], jnp.zeros((4 * d,), F32)])
    dm16 = jnp.zeros((16, 6 * d), F32).at[0:8].set(dm_rows).at[8].set(dm_c)
    g_b_ada = jnp.sum(dm16, axis=0)[None, :]
    dm_cols = lax.dynamic_slice(dm16, (0, me * ada_cols), (16, ada_cols))
    g_w_ada, ds16 = _ada_bwd(s16, dm_cols, wa2d)
    ds_all = _all_gather_small(ds16[8:16], "gather_dsilu")
    dsilu_cc = _sum_parts(ds_all, "sum_dsilu")[0]
    sg_cc = jax.nn.sigmoid(c_ctx)
    g_c_ctx = dsilu_cc * (sg_cc * (1.0 + c_ctx * (1.0 - sg_cc)))

    g_ret_decay = jnp.sum(t_dlg, axis=-1).reshape(2, HEADS) * jax.nn.sigmoid(-ret_decay[0])
    g_lambda_full = -t_sp * jax.nn.sigmoid(-lam_full)

    def my_cols(full):
        return lax.dynamic_slice(full, (0, me * sc_loc), (full.shape[0], sc_loc))

    small_g = dict(
        c_ctx=g_c_ctx[None], b_ada=g_b_ada, norm1_g=t_n1[4:5], norm2_g=t_n2[2:3], ret_decay=g_ret_decay,
        conv_w=my_cols(t_conv[0:4]), conv_b=t_conv[4:5], lru_wa=t_wa.reshape(-1, LRU_BD), lru_ba=my_cols(t_ba),
        lru_wx=t_wx.reshape(-1, LRU_BD), lru_bx=my_cols(t_bx), lru_lambda=my_cols(g_lambda_full), final_g=t_f[0:1])
    small = dict(
        c_ctx=(c_ctx, m_c_ctx, v_c_ctx), b_ada=(b_ada, m_b_ada, v_b_ada), norm1_g=(norm1_g, m_norm1_g, v_norm1_g),
        norm2_g=(norm2_g, m_norm2_g, v_norm2_g), ret_decay=(ret_decay, m_ret_decay, v_ret_decay),
        conv_w=(conv_w, m_conv_w, v_conv_w), conv_b=(conv_b, m_conv_b, v_conv_b), lru_wa=(lru_wa, m_lru_wa, v_lru_wa),
        lru_ba=(lru_ba, m_lru_ba, v_lru_ba), lru_wx=(lru_wx, m_lru_wx, v_lru_wx), lru_bx=(lru_bx, m_lru_bx, v_lru_bx),
        lru_lambda=(lru_lambda, m_lru_lambda, v_lru_lambda), final_g=(final_g, m_final_g, v_final_g))
    names = list(small)
    items = [(small_g[k],) + tuple(a.reshape(small_g[k].shape) for a in small[k]) for k in names]
    res = {}
    for k, it, (d_, m_, v_) in zip(names, items, _adam_many(items, "adam_small")):
        shape = small[k][0].shape
        res[k] = tuple(a.reshape(shape) for a in (it[0], d_, m_, v_))

    def big(parts, w, m, v, name):
        out = _sum_adam(parts, w[0], m[0], v[0], name)
        return tuple(a[None] for a in out)

    res["w_ada"] = big([g_w_ada[None]], w_ada, m_w_ada, v_w_ada, "adam_w_ada")
    res["w_in"] = big([gwi_all], w_in, m_w_in, v_w_in, "adam_w_in")
    res["w_out"] = big([gwo_all], w_out, m_w_out, v_w_out, "adam_w_out")
    res["w_mlp1"] = big(gw1_all, w_mlp1, m_w_mlp1, v_w_mlp1, "adam_w_mlp1")
    res["w_mlp2"] = big(gw2_all, w_mlp2, m_w_mlp2, v_w_mlp2, "adam_w_mlp2")

    order = ["c_ctx", "w_ada", "b_ada", "norm1_g", "norm2_g", "w_in", "ret_decay", "conv_w", "conv_b", "lru_wa", "lru_ba",
             "lru_wx", "lru_bx", "lru_lambda", "w_out", "w_mlp1", "w_mlp2", "final_g"]
    outs = [loss, grad_x[None]]
    for j in range(4):
        outs += [res[k][j] for k in order]
    return tuple(outs)
```

```python
import jax
import jax.numpy as jnp
from jax import lax
from jax.experimental import pallas as pl
from jax.experimental.pallas import tpu as pltpu

F32 = jnp.float32
BF16 = jnp.bfloat16
N_DEV = 8
MESH = pl.DeviceIdType.MESH

HEADS = 4
HEAD_DIM = 128
CHUNK = 128
RET_W = HEADS * HEAD_DIM
LRU_W = 512
LRU_BLOCKS = 8
LRU_BD = LRU_W // LRU_BLOCKS
LRU_C = 8.0
EPS = 1e-6
K_SCALE = HEAD_DIM ** -0.5
ROPE_BASE = 10000.0
GRID_W = 64
TM = 256
TL = 512
SUB = 8

ADAM_LR = 0.001
ADAM_B1 = 0.9
ADAM_B2 = 0.999
ADAM_EPS = 1e-08
ADAM_WD = 0.01
ADAM_STEP = 10

COL_G, COL_XR, COL_GATE = 0, 1, 2

R_CSH1, R_CSC1, R_SH1, R_SC1, R_G1, R_SH2, R_SC2, R_G2 = range(8)


def _pcall(body, **kw):
    return pl.pallas_call(body, **kw)


def _cp(vmem_mb=48):
    return pltpu.CompilerParams(vmem_limit_bytes=vmem_mb << 20)


def _sds(shape, dtype):
    return jax.ShapeDtypeStruct(shape, dtype)


def _dot(a, b):
    return jnp.dot(a.astype(BF16), b.astype(BF16), preferred_element_type=F32)


def _dot_nt(a, b):
    return lax.dot_general(a.astype(BF16), b.astype(BF16), (((1,), (1,)), ((), ())), preferred_element_type=F32)


def _dot_tn(a, b):
    return lax.dot_general(a.astype(BF16), b.astype(BF16), (((0,), (0,)), ((), ())), preferred_element_type=F32)


def _sigmoid(x):
    return 0.5 * jnp.tanh(0.5 * x) + 0.5


def _gelu(x):
    return 0.5 * x * (1.0 + jnp.tanh(0.7978845608028654 * (x + 0.044715 * x * x * x)))


def _dgelu(x):
    t = jnp.tanh(0.7978845608028654 * (x + 0.044715 * x * x * x))
    return 0.5 * (1.0 + t) + 0.5 * x * (1.0 - t * t) * 0.7978845608028654 * (1.0 + 3.0 * 0.044715 * x * x)


def _rows_iota(shape):
    return lax.broadcasted_iota(jnp.int32, shape, 0)


def _tile_order(dirn, s, cb, nb):
    if dirn == 0:
        return s
    return jnp.where(s < cb, cb - 1 - s, nb - 1 - (s - cb))


_SEMS = [pltpu.SemaphoreType.DMA((7,)), pltpu.SemaphoreType.DMA((7,)), pltpu.SemaphoreType.DMA(())]
_ANY = pl.BlockSpec(memory_space=pl.ANY)


def _gather_copies(x_ref, out_ref, send_sems, recv_sems, local_sem):
    mx, my, mc = lax.axis_index("x"), lax.axis_index("y"), lax.axis_index("c")
    me, sibling = (mx, my, mc), (mx, my, 1 - mc)
    chips = [(1 - mx, my), (mx, 1 - my), (1 - mx, 1 - my)]

    def slot(px, py, pc):
        return out_ref.at[4 * px + 2 * py + pc]

    def copy(k, block, to, src=None):
        return pltpu.make_async_remote_copy(
            src_ref=slot(*block) if src is None else src, dst_ref=slot(*block),
            send_sem=send_sems.at[k], recv_sem=recv_sems.at[k], device_id=to, device_id_type=MESH)

    mine = pltpu.make_async_copy(x_ref, slot(*me), local_sem)
    first = [copy(0, me, sibling, src=x_ref)] + [copy(1 + j, me, (*chip, mc), src=x_ref) for j, chip in enumerate(chips)]
    passed = [copy(4 + j, (*chip, mc), sibling) for j, chip in enumerate(chips)]
    recv_ici = [copy(1 + j, (*chip, mc), me) for j, chip in enumerate(chips)]
    recv_d2d = [copy(0, sibling, me)] + [copy(4 + j, (*chip, 1 - mc), me) for j, chip in enumerate(chips)]
    return mine, first, passed, recv_ici, recv_d2d


def _gather_start(*refs):
    mine, first, _, _, _ = _gather_copies(*refs)
    mine.start()
    for cp in first:
        cp.start()


def _gather_pass_on(*refs):
    _, _, passed, recv_ici, _ = _gather_copies(*refs)
    for landed, onward in zip(recv_ici, passed):
        landed.wait_recv()
        onward.start()


def _gather_finish(*refs):
    mine, first, passed, _, recv_d2d = _gather_copies(*refs)
    for landed in recv_d2d:
        landed.wait_recv()
    for cp in first + passed:
        cp.wait_send()
    mine.wait()


def _a2a_copies(g_ref, out_ref, send_sems, recv_sems, local_sem):
    mx, my, mc = lax.axis_index("x"), lax.axis_index("y"), lax.axis_index("c")
    me = 4 * mx + 2 * my + mc
    mine = pltpu.make_async_copy(g_ref.at[me], out_ref.at[me], local_sem)
    copies = []
    for k in range(1, N_DEV):
        px = 1 - mx if (k >> 2) & 1 else mx
        py = 1 - my if (k >> 1) & 1 else my
        pc = 1 - mc if k & 1 else mc
        copies.append(pltpu.make_async_remote_copy(
            src_ref=g_ref.at[4 * px + 2 * py + pc], dst_ref=out_ref.at[me],
            send_sem=send_sems.at[k - 1], recv_sem=recv_sems.at[k - 1],
            device_id=(px, py, pc), device_id_type=MESH))
    return mine, copies


def _a2a_start(*refs):
    mine, copies = _a2a_copies(*refs)
    mine.start()
    for cp in copies:
        cp.start()


def _a2a_finish(*refs):
    mine, copies = _a2a_copies(*refs)
    for cp in copies:
        cp.wait()
    mine.wait()


_EXCHANGES = {"gather": (_gather_start, _gather_finish), "a2a": (_a2a_start, _a2a_finish)}
PASS_ON_LEAD = 3


def _exchange_shape(kind, src):
    return _sds((N_DEV,) + src.shape if kind == "gather" else src.shape, src.dtype)


def _all_gather_small(x, name):
    def body(x_ref, out_ref, send_sems, recv_sems, local_sem):
        mx, my, mc = lax.axis_index("x"), lax.axis_index("y"), lax.axis_index("c")
        me = 4 * mx + 2 * my + mc
        mine = pltpu.make_async_copy(x_ref, out_ref.at[me], local_sem)
        mine.start()
        copies = []
        for k in range(1, N_DEV):
            peer = (1 - mx if (k >> 2) & 1 else mx, 1 - my if (k >> 1) & 1 else my, 1 - mc if k & 1 else mc)
            copies.append(pltpu.make_async_remote_copy(
                src_ref=x_ref, dst_ref=out_ref.at[me], send_sem=send_sems.at[k - 1], recv_sem=recv_sems.at[k - 1],
                device_id=peer, device_id_type=MESH))
            copies[-1].start()
        for cp in copies:
            cp.wait()
        mine.wait()

    return _pcall(body, name=name, out_shape=_exchange_shape("gather", x), in_specs=[_ANY], out_specs=_ANY,
                  scratch_shapes=list(_SEMS))(x)


def _pcall_ride(body, ride, args, *, name, grid, in_specs, out_specs, out_shape, scratch_shapes=(), compiler_params=None):
    if ride is None:
        out = _pcall(body, name=name, grid=grid, in_specs=in_specs, out_specs=out_specs, out_shape=out_shape,
                     scratch_shapes=list(scratch_shapes), compiler_params=compiler_params)(*args)
        return out, None
    kind, src = ride
    start, finish = _EXCHANGES[kind]
    single = not isinstance(out_shape, (tuple, list))
    out_specs_t = (out_specs,) if single else tuple(out_specs)
    out_shape_t = (out_shape,) if single else tuple(out_shape)
    n_in, n_out, n_sc = len(in_specs), len(out_shape_t), len(scratch_shapes)

    def wrapped(*refs):
        ins, src_ref = refs[:n_in], refs[n_in]
        outs, dst_ref = refs[n_in + 1:n_in + 1 + n_out], refs[n_in + 1 + n_out]
        scratch = refs[n_in + 2 + n_out:n_in + 2 + n_out + n_sc]
        sems = refs[n_in + 2 + n_out + n_sc:]
        first = pl.program_id(0) == 0
        last = pl.program_id(0) == grid[0] - 1
        for ax in range(1, len(grid)):
            first = jnp.logical_and(first, pl.program_id(ax) == 0)
            last = jnp.logical_and(last, pl.program_id(ax) == grid[ax] - 1)

        @pl.when(first)
        def _():
            start(src_ref, dst_ref, *sems)

        body(*ins, *outs, *scratch)

        if kind == "gather":
            @pl.when(pl.program_id(0) == max(grid[0] - PASS_ON_LEAD, 0))
            def _():
                _gather_pass_on(src_ref, dst_ref, *sems)

        @pl.when(last)
        def _():
            finish(src_ref, dst_ref, *sems)

    res = _pcall(wrapped, name=name, grid=grid, in_specs=list(in_specs) + [_ANY], out_specs=out_specs_t + (_ANY,),
                 out_shape=out_shape_t + (_exchange_shape(kind, src),),
                 scratch_shapes=list(scratch_shapes) + list(_SEMS), compiler_params=compiler_params)(*args, src)
    return (res[0] if single else tuple(res[:-1])), res[-1]


class _Epilogue:
    def __init__(self, fn, args, in_specs, out_specs, out_shape, steps=None, lhs_map=None, delay=0):
        self.fn, self.args, self.in_specs, self.out_specs, self.out_shape = fn, tuple(args), list(in_specs), out_specs, out_shape
        self.steps, self.lhs_map, self.delay = steps, lhs_map, delay


def _mm_nn(a, w, out_dtype, name, square_lhs=False, relu_out=False, ride=None, tm=TM, epi=None, vmem_mb=48):
    m, k = a.shape
    nb, _, bn = w.shape
    tm = min(tm, m)
    delay = 0 if epi is None else epi.delay
    steps = m // tm

    def body(*refs):
        a_ref, w_ref = refs[:2]
        av = a_ref[...]
        if square_lhs:
            av = av * av
        if delay:
            assert nb == 1
            held = refs[-1]

            @pl.when(pl.program_id(0) == 0)
            def _():
                held[...] = jnp.zeros_like(held)

            fresh = jnp.dot(av, w_ref[0], preferred_element_type=F32)
            epi.fn(held[...], slice(0, tm), *refs[2:-1])
            held[...] = fresh
            return
        if epi is not None:
            assert nb == 1
            epi.fn(jnp.dot(av, w_ref[0], preferred_element_type=F32), slice(0, tm), *refs[2:])
            return
        for j in range(nb):
            r = jnp.dot(av, w_ref[j], preferred_element_type=F32)
            if relu_out:
                r = jnp.maximum(r, 0.0)
            refs[2][:, j * bn:(j + 1) * bn] = r.astype(out_dtype)

    lhs_map = (lambda i: (i, 0)) if not delay else (lambda i: (jnp.minimum(i, steps - 1), 0))
    in_specs = [pl.BlockSpec((tm, k), lhs_map), pl.BlockSpec((nb, k, bn), lambda i: (0, 0, 0))]
    scratch = []
    if epi is None:
        args, out_specs, out_shape = (a, w), pl.BlockSpec((tm, nb * bn), lambda i: (i, 0)), _sds((m, nb * bn), out_dtype)
    else:
        args, out_specs, out_shape = (a, w) + epi.args, epi.out_specs, epi.out_shape
        in_specs += epi.in_specs
        if delay:
            scratch.append(pltpu.VMEM((tm, bn), F32))
    out, ex = _pcall_ride(body, ride, args, name=name, grid=(steps + delay,), in_specs=in_specs, out_specs=out_specs,
                          out_shape=out_shape, scratch_shapes=scratch, compiler_params=_cp(vmem_mb))
    return out if ride is None else (out, ex)


def _mm_nt(dy, w, out_dtype, name, relu_mul=None, ride=None, tm=TM, epi=None, vmem_mb=48):
    m = dy.shape[0]
    nb, k, bn = w.shape
    tm = min(tm, m)
    n_extra = (0 if relu_mul is None else 1) + (0 if epi is None else len(epi.args))

    delay = 0 if epi is None else epi.delay

    def body(*refs):
        dy_ref, w_ref = refs[:2]
        scratch = refs[-1 - delay:]
        extra, outs, wt = refs[2:2 + n_extra], refs[2 + n_extra:-1 - delay], scratch[0]

        @pl.when(pl.program_id(0) == 0)
        def _():
            for j in range(nb):
                wt[j * bn:(j + 1) * bn, :] = w_ref[j].T
            if delay:
                scratch[1][...] = jnp.zeros_like(scratch[1])

        if delay:
            held = scratch[1]
            fresh = jnp.dot(dy_ref[...], wt[...], preferred_element_type=F32)
            epi.fn(held[...], slice(0, tm), *extra, *outs)
            held[...] = fresh
            return
        acc = jnp.dot(dy_ref[...], wt[...], preferred_element_type=F32)
        if epi is not None:
            epi.fn(acc, slice(0, tm), *extra, *outs)
            return
        if relu_mul is not None:
            acc = acc * (2.0 * extra[0][...].astype(F32))
        outs[0][...] = acc.astype(out_dtype)

    steps = m // tm if epi is None or epi.steps is None else epi.steps
    lhs_tile = (lambda i: (i, 0)) if epi is None or epi.lhs_map is None else epi.lhs_map
    lhs_map = lhs_tile if not delay else (lambda i: lhs_tile(jnp.minimum(i, steps - 1)))
    in_specs = [pl.BlockSpec((tm, nb * bn), lhs_map), pl.BlockSpec((nb, k, bn), lambda i: (0, 0, 0))]
    args = [dy, w]
    if relu_mul is not None:
        in_specs.append(pl.BlockSpec((tm, k), lambda i: (i, 0)))
        args.append(relu_mul)
    scratch = [pltpu.VMEM((nb * bn, k), BF16)]
    if epi is None:
        out_specs, out_shape = pl.BlockSpec((tm, k), lambda i: (i, 0)), _sds((m, k), out_dtype)
    else:
        args += list(epi.args)
        in_specs += epi.in_specs
        out_specs, out_shape = epi.out_specs, epi.out_shape
        if delay:
            scratch.append(pltpu.VMEM((tm, k), F32))
    out, ex = _pcall_ride(
        body, ride, args, name=name, grid=(steps + delay,), in_specs=in_specs, out_specs=out_specs, out_shape=out_shape,
        scratch_shapes=scratch, compiler_params=_cp(vmem_mb))
    return out if ride is None else (out, ex)


def _mm_tn(a, b, name, col_blocks, block, out_dtype, tm, square_lhs=False):
    m, k = a.shape
    nn = b.shape[1]
    steps = m // tm
    if col_blocks:
        nblk, acc_shape = nn // block, (k, block)
        a_spec = pl.BlockSpec((tm, k), lambda j, s: (s, 0))
        b_spec = pl.BlockSpec((tm, block), lambda j, s: (s, j))
    else:
        nblk, acc_shape = k // block, (block, nn)
        a_spec = pl.BlockSpec((tm, block), lambda j, s: (s, j))
        b_spec = pl.BlockSpec((tm, nn), lambda j, s: (s, 0))

    def body(a_ref, b_ref, o_ref, acc):
        s = pl.program_id(1)

        @pl.when(s == 0)
        def _():
            acc[...] = jnp.zeros_like(acc)

        av = a_ref[...]
        if square_lhs:
            av = av.astype(F32)
            av = (av * av).astype(BF16)
        acc[...] += _dot_tn(av, b_ref[...])

        @pl.when(s == steps - 1)
        def _():
            o_ref[...] = acc[...].astype(out_dtype)

    return _pcall(
        body, name=name, grid=(nblk, steps), in_specs=[a_spec, b_spec],
        out_specs=pl.BlockSpec((None,) + acc_shape, lambda j, s: (j, 0, 0)),
        out_shape=_sds((nblk,) + acc_shape, out_dtype),
        scratch_shapes=[pltpu.VMEM(acc_shape, F32)], compiler_params=_cp(),
    )(a, b)


def _mm_wgrad(at, b, name, bn, out_dtype, transpose_out=False, square_rhs=False, halves=False, ride=None):
    k, m = at.shape
    nblk = b.shape[1] // bn
    rows, cols = (bn, k) if transpose_out else (k, bn)
    nout = 2 if halves else 1
    per = rows // nout

    def body(a_ref, b_ref, *o_refs):
        bv = b_ref[...]
        if square_rhs:
            bv = bv * bv
        r = jnp.dot(a_ref[...], bv, preferred_element_type=F32)
        r = (r.T if transpose_out else r).astype(out_dtype)
        for i, o_ref in enumerate(o_refs):
            o_ref[...] = r[i * per:(i + 1) * per, :]

    out, ex = _pcall_ride(
        body, ride, (at, b), name=name, grid=(nblk,),
        in_specs=[pl.BlockSpec((k, m), lambda j: (0, 0)), pl.BlockSpec((m, bn), lambda j: (0, j))],
        out_specs=tuple(pl.BlockSpec((None, per, cols), lambda j: (j, 0, 0)) for _ in range(nout)),
        out_shape=tuple(_sds((nblk, per, cols), out_dtype) for _ in range(nout)), compiler_params=_cp())
    out = out if halves else out[0]
    return out if ride is None else (out, ex)


def _mm_in(hn, w, cosf, sins, ride):
    m, k = hn.shape
    nb, _, bn = w.shape

    def body(a_ref, w_ref, c_ref, s_ref, qkv_ref, rest_ref, pt):
        av = a_ref[...]
        for j in range(nb):
            pt[:, j * bn:(j + 1) * bn] = jnp.dot(av, w_ref[j], preferred_element_type=F32)
        cf, ss = c_ref[...], s_ref[...]
        for h in range(HEADS):
            sq = slice(h * HEAD_DIM, (h + 1) * HEAD_DIM)
            sk = slice(RET_W + h * HEAD_DIM, RET_W + (h + 1) * HEAD_DIM)
            qkv_ref[:, sq] = _rot(pt[:, sq], cf, ss).astype(BF16)
            qkv_ref[:, sk] = (_rot(pt[:, sk], cf, ss) * K_SCALE).astype(BF16)
        qkv_ref[:, 2 * RET_W:] = pt[:, 2 * RET_W:3 * RET_W].astype(BF16)
        rest_ref[...] = pt[:, 3 * RET_W:]

    tab = pl.BlockSpec((TM, HEAD_DIM), lambda i: (i, 0))
    wide = pl.BlockSpec((TM, 3 * RET_W), lambda i: (i, 0))
    return _pcall_ride(
        body, ride, (hn, w, cosf, sins), name="mm_in", grid=(m // TM,),
        in_specs=[pl.BlockSpec((TM, k), lambda i: (i, 0)), pl.BlockSpec((nb, k, bn), lambda i: (0, 0, 0)), tab, tab],
        out_specs=(wide, wide), out_shape=(_sds((m, 3 * RET_W), BF16), _sds((m, nb * bn - 3 * RET_W), F32)),
        scratch_shapes=[pltpu.VMEM((TM, nb * bn), F32)], compiler_params=_cp())


def _mod_exchange(pack_a, silu_cc, w_ada, b_cols):
    d, cols = pack_a.shape[1], w_ada.shape[1]

    def body(pa_ref, scc_ref, w_ref, b_ref, alla_ref, s16_ref, modp_ref, mp, send1, recv1, send2, recv2):
        mx, my, mc = lax.axis_index("x"), lax.axis_index("y"), lax.axis_index("c")
        me = 4 * mx + 2 * my + mc

        def exchange(src_ref, dst_ref, send, recv):
            copies = []
            for k in range(1, N_DEV):
                peer = (1 - mx if (k >> 2) & 1 else mx, 1 - my if (k >> 1) & 1 else my, 1 - mc if k & 1 else mc)
                copies.append(pltpu.make_async_remote_copy(
                    src_ref=src_ref, dst_ref=dst_ref.at[me], send_sem=send.at[k - 1], recv_sem=recv.at[k - 1],
                    device_id=peer, device_id_type=MESH))
                copies[-1].start()
            dst_ref[me] = src_ref[...]
            for cp in copies:
                cp.wait()

        exchange(pa_ref, alla_ref, send1, recv1)
        s16_ref[...] = jnp.zeros_like(s16_ref)
        for j in range(N_DEV):
            s16_ref[j:j + 1, :] = alla_ref[j, 0:1, :]
        s16_ref[N_DEV:N_DEV + 1, :] = scc_ref[...]
        mp[...] = _dot(s16_ref[...], w_ref[...]) + b_ref[...]
        exchange(mp, modp_ref, send2, recv2)

    vmem = pl.BlockSpec(memory_space=pltpu.VMEM)
    sem7 = pltpu.SemaphoreType.DMA((N_DEV - 1,))
    return _pcall(
        body, name="mod_exchange", in_specs=[vmem] * 4, out_specs=(vmem,) * 3,
        out_shape=(_sds((N_DEV,) + pack_a.shape, F32), _sds((2 * N_DEV, d), F32), _sds((N_DEV, 2 * N_DEV, cols), F32)),
        scratch_shapes=[pltpu.VMEM((2 * N_DEV, cols), F32), sem7, sem7, sem7, sem7], compiler_params=_cp(),
    )(pack_a, silu_cc, w_ada, b_cols)


def _ada_bwd(s16, dm_cols, w_ada):
    def body(s_ref, d_ref, w_ref, gw_ref, ds_ref):
        gw_ref[...] = _dot_tn(s_ref[...], d_ref[...])
        ds_ref[...] = _dot_nt(d_ref[...], w_ref[...])

    return _pcall(body, name="ada_bwd",
                  out_shape=(_sds(w_ada.shape, F32), _sds(s16.shape, F32)), compiler_params=_cp())(s16, dm_cols, w_ada)


def _norm1_fwd(ctx, x, g, modrows, cb, ride=None):
    l_len, d = ctx.shape
    nb = (l_len + x.shape[0]) // TM

    def body(ctx_ref, x_ref, g_ref, m_ref, o_ref, ot_ref):
        is_ctx = pl.program_id(0) < cb
        xin = jnp.where(is_ctx, ctx_ref[...], x_ref[...])
        sh = jnp.where(is_ctx, m_ref[R_CSH1:R_CSH1 + 1, :], m_ref[R_SH1:R_SH1 + 1, :])
        sc = jnp.where(is_ctx, m_ref[R_CSC1:R_CSC1 + 1, :], m_ref[R_SC1:R_SC1 + 1, :])
        ms = jnp.mean(xin * xin, axis=-1, keepdims=True)
        n = xin * lax.rsqrt(ms + EPS) * g_ref[...]
        hn = n * (1.0 + sc) + sh
        o_ref[...] = hn.astype(BF16)
        ot_ref[...] = hn.T.astype(BF16)

    return _pcall_ride(
        body, ride, (ctx, x, g, modrows), name="norm1_fwd", grid=(nb,),
        in_specs=[pl.BlockSpec((TM, d), lambda i: (jnp.minimum(i, cb - 1), 0)),
                  pl.BlockSpec((TM, d), lambda i: (jnp.maximum(i - cb, 0), 0)),
                  pl.BlockSpec((1, d), lambda i: (0, 0)), pl.BlockSpec((8, d), lambda i: (0, 0))],
        out_specs=(pl.BlockSpec((TM, d), lambda i: (i, 0)), pl.BlockSpec((d, TM), lambda i: (0, i))),
        out_shape=(_sds((nb * TM, d), BF16), _sds((d, nb * TM), BF16)), compiler_params=_cp())


def _mix_fwd(o_f, o_b, p, h_f, h_b, cb, t_len):
    def body(of_ref, ob_ref, g_ref, gate_ref, hf_ref, hb_ref, mix_ref):
        o = of_ref[...] + ob_ref[...]
        g = g_ref[...]
        sg = g * _sigmoid(g)
        for hh in range(HEADS):
            sl = slice(hh * HEAD_DIM, (hh + 1) * HEAD_DIM)
            oh = o[:, sl]
            yc = oh - jnp.mean(oh, axis=-1, keepdims=True)
            var = jnp.mean(yc * yc, axis=-1, keepdims=True)
            mix_ref[:, sl] = (sg[:, sl] * (yc * lax.rsqrt(var + EPS))).astype(BF16)
        mix_ref[:, RET_W:] = ((hf_ref[...] + hb_ref[...]) * _gelu(gate_ref[...])).astype(BF16)

    row = lambda i: (i + cb, 0)
    return _pcall(
        body, name="mix_fwd", grid=(t_len // TM,),
        in_specs=[pl.BlockSpec((TM, RET_W), row), pl.BlockSpec((TM, RET_W), row),
                  pl.BlockSpec((TM, RET_W), lambda i: (i + cb, COL_G)), pl.BlockSpec((TM, LRU_W), lambda i: (i + cb, COL_GATE)),
                  pl.BlockSpec((TM, LRU_W), row), pl.BlockSpec((TM, LRU_W), row)],
        out_specs=pl.BlockSpec((TM, RET_W + LRU_W), lambda i: (i, 0)),
        out_shape=_sds((t_len, RET_W + LRU_W), BF16), compiler_params=_cp(),
    )(o_f, o_b, p, p, h_f, h_b)


def _res_norm2_epilogue(x, g, modrows, tm):
    t_len, d = x.shape
    tm = min(tm, t_len)

    def fn(y, rows, x_ref, g_ref, m_ref, y_ref, x1_ref, h2_ref, h2t_ref):
        y_ref[rows, :] = y
        x1 = x_ref[rows, :] + m_ref[R_G1:R_G1 + 1, :] * y
        ms = jnp.mean(x1 * x1, axis=-1, keepdims=True)
        n = x1 * lax.rsqrt(ms + EPS) * g_ref[...]
        x1_ref[rows, :] = x1
        h2 = n * (1.0 + m_ref[R_SC2:R_SC2 + 1, :]) + m_ref[R_SH2:R_SH2 + 1, :]
        h2_ref[rows, :] = h2.astype(BF16)
        h2t_ref[:, rows] = h2.T.astype(BF16)

    t = pl.BlockSpec((tm, d), lambda i: (i, 0))
    return _Epilogue(
        fn, (x, g, modrows),
        in_specs=[t, pl.BlockSpec((1, d), lambda i: (0, 0)), pl.BlockSpec((8, d), lambda i: (0, 0))],
        out_specs=(t, t, t, pl.BlockSpec((d, tm), lambda i: (0, i))),
        out_shape=(_sds((t_len, d), F32), _sds((t_len, d), F32), _sds((t_len, d), BF16), _sds((d, t_len), BF16)))


def _tile(i):
    return jnp.maximum(i - 1, 0)


def _zero_at_start(acc_ref, rows):
    @pl.when(pl.program_id(0) == 0)
    def _():
        acc_ref[...] = jnp.zeros_like(acc_ref)


def _final_epilogue(x1, target, fg, modrows, tm):
    t_len, d = x1.shape

    def fn(z, rows, x1_ref, t_ref, fg_ref, m_ref, dx2_ref, dz_ref, dzt_ref, acc_ref):
        _zero_at_start(acc_ref, rows)
        g2 = m_ref[R_G2:R_G2 + 1, :]
        x2 = x1_ref[rows, :] + g2 * z
        rstd = lax.rsqrt(jnp.mean(x2 * x2, axis=-1, keepdims=True) + EPS)
        xh = x2 * rstd
        fg = fg_ref[...]
        e = xh * fg - t_ref[rows, :]
        dy = e * (1.0 / d)
        dxh = dy * fg
        dx2 = rstd * (dxh - xh * jnp.mean(dxh * xh, axis=-1, keepdims=True))
        dx2_ref[rows, :] = dx2
        dz = g2 * dx2
        dz_ref[rows, :] = dz.astype(BF16)
        dzt_ref[:, rows] = dz.T.astype(BF16)
        acc_ref[0:1, :] += jnp.sum(dy * xh, axis=0, keepdims=True)
        acc_ref[1:2, :] += jnp.sum(dx2 * z, axis=0, keepdims=True)
        acc_ref[2:3, :] += jnp.sum(e * e, axis=0, keepdims=True)

    t = pl.BlockSpec((tm, d), lambda i: (i, 0))
    return _Epilogue(
        fn, (x1, target, fg, modrows),
        in_specs=[t, t, pl.BlockSpec((1, d), lambda i: (0, 0)), pl.BlockSpec((8, d), lambda i: (0, 0))],
        out_specs=(t, t, pl.BlockSpec((d, tm), lambda i: (0, i)), pl.BlockSpec((8, d), lambda i: (0, 0))),
        out_shape=(_sds((t_len, d), F32), _sds((t_len, d), BF16), _sds((d, t_len), BF16), _sds((8, d), F32)))


def _bwd_norm2_epilogue(x1, dx2, y, g, modrows, tm):
    t_len, d = x1.shape

    def fn(dh2, rows, x1_ref, dx2_ref, y_ref, g_ref, m_ref, dx1_ref, dy_ref, acc_ref):
        _zero_at_start(acc_ref, rows)
        x1 = x1_ref[rows, :]
        rstd = lax.rsqrt(jnp.mean(x1 * x1, axis=-1, keepdims=True) + EPS)
        xh = x1 * rstd
        gn = g_ref[...]
        dn = dh2 * (1.0 + m_ref[R_SC2:R_SC2 + 1, :])
        dxh = dn * gn
        dx1 = dx2_ref[rows, :] + rstd * (dxh - xh * jnp.mean(dxh * xh, axis=-1, keepdims=True))
        dx1_ref[rows, :] = dx1
        dy_ref[rows, :] = (m_ref[R_G1:R_G1 + 1, :] * dx1).astype(BF16)
        acc_ref[0:1, :] += jnp.sum(dh2, axis=0, keepdims=True)
        acc_ref[1:2, :] += jnp.sum(dh2 * xh * gn, axis=0, keepdims=True)
        acc_ref[2:3, :] += jnp.sum(dn * xh, axis=0, keepdims=True)
        acc_ref[3:4, :] += jnp.sum(dx1 * y_ref[rows, :], axis=0, keepdims=True)

    t = pl.BlockSpec((tm, d), lambda i: (i, 0))
    return _Epilogue(
        fn, (x1, dx2, y, g, modrows),
        in_specs=[t, t, t, pl.BlockSpec((1, d), lambda i: (0, 0)), pl.BlockSpec((8, d), lambda i: (0, 0))],
        out_specs=(t, t, pl.BlockSpec((8, d), lambda i: (0, 0))),
        out_shape=(_sds((t_len, d), F32), _sds((t_len, d), BF16), _sds((8, d), F32)))


def _mix_bwd_epilogue(o_f, o_b, p, h_f, h_b, cb):
    n = o_f.shape[0]
    tile = _tile

    def fn(dm, rows, of_ref, ob_ref, g_ref, gate_ref, hf_ref, hb_ref, do_ref, dg_ref, dgate_ref, dhs_ref):
        keep = jnp.where(pl.program_id(0) - 1 < cb, 0.0, 1.0)
        dm = dm * keep
        o = of_ref[rows, :] + ob_ref[rows, :]
        g = g_ref[rows, :]
        s = _sigmoid(g)
        sg = g * s
        dsg = s * (1.0 + g * (1.0 - s))
        for hh in range(HEADS):
            sl = slice(hh * HEAD_DIM, (hh + 1) * HEAD_DIM)
            oh = o[:, sl]
            yc = oh - jnp.mean(oh, axis=-1, keepdims=True)
            rs = lax.rsqrt(jnp.mean(yc * yc, axis=-1, keepdims=True) + EPS)
            gn = yc * rs
            dret = dm[:, sl]
            dgn = dret * sg[:, sl]
            dg_ref[rows, sl] = (dret * gn * dsg[:, sl]).astype(BF16)
            do_ref[rows, sl] = (rs * (dgn - jnp.mean(dgn, axis=-1, keepdims=True)
                                      - gn * jnp.mean(dgn * gn, axis=-1, keepdims=True))).astype(BF16)
        dlru = dm[:, RET_W:]
        gate = gate_ref[rows, :]
        dhs_ref[rows, :] = dlru * _gelu(gate)
        dgate_ref[rows, :] = (dlru * (hf_ref[rows, :] + hb_ref[rows, :]) * _dgelu(gate)).astype(BF16)

    t = pl.BlockSpec((TM, RET_W), lambda i: (tile(i), 0))
    return _Epilogue(
        fn, (o_f, o_b, p, p, h_f, h_b),
        in_specs=[t, t, pl.BlockSpec((TM, RET_W), lambda i: (tile(i), COL_G)),
                  pl.BlockSpec((TM, LRU_W), lambda i: (tile(i), COL_GATE)), t, t],
        out_specs=(t, t, t, t), out_shape=(_sds((n, RET_W), BF16),) * 3 + (_sds((n, RET_W), F32),),
        steps=n // TM, lhs_map=lambda i: (jnp.maximum(i - cb, 0), 0), delay=1)


def _bwd_norm1_epilogue(ctx, x, dx1, g, modrows, cb):
    t_len, d = x.shape

    def fn(dh, rows, ctx_ref, x_ref, dx1_ref, g_ref, m_ref, gx_ref, acc_ref):
        is_ctx = pl.program_id(0) < cb
        _zero_at_start(acc_ref, rows)
        xin = jnp.where(is_ctx, ctx_ref[rows, :], x_ref[rows, :])
        sc = jnp.where(is_ctx, m_ref[R_CSC1:R_CSC1 + 1, :], m_ref[R_SC1:R_SC1 + 1, :])
        rstd = lax.rsqrt(jnp.mean(xin * xin, axis=-1, keepdims=True) + EPS)
        xh = xin * rstd
        gn = g_ref[...]
        dn = dh * (1.0 + sc)
        dxh = dn * gn
        gx_ref[rows, :] = dx1_ref[rows, :] + rstd * (dxh - xh * jnp.mean(dxh * xh, axis=-1, keepdims=True))
        s0 = jnp.sum(dh, axis=0, keepdims=True)
        s1 = jnp.sum(dh * xh * gn, axis=0, keepdims=True)
        acc_ref[4:5, :] += jnp.sum(dn * xh, axis=0, keepdims=True)
        acc_ref[0:1, :] += jnp.where(is_ctx, s0, 0.0)
        acc_ref[1:2, :] += jnp.where(is_ctx, s1, 0.0)
        acc_ref[2:3, :] += jnp.where(is_ctx, 0.0, s0)
        acc_ref[3:4, :] += jnp.where(is_ctx, 0.0, s1)

    lat = pl.BlockSpec((TM, d), lambda i: (jnp.maximum(i - cb, 0), 0))
    return _Epilogue(
        fn, (ctx, x, dx1, g, modrows),
        in_specs=[pl.BlockSpec((TM, d), lambda i: (jnp.minimum(i, cb - 1), 0)),
                  lat, lat, pl.BlockSpec((1, d), lambda i: (0, 0)), pl.BlockSpec((8, d), lambda i: (0, 0))],
        out_specs=(lat, pl.BlockSpec((8, d), lambda i: (0, 0))),
        out_shape=(_sds((t_len, d), F32), _sds((8, d), F32)))


def _rot(x, cf, ss):
    return x * cf + pltpu.roll(x, HEAD_DIM // 2, 1) * ss


def _decay_exponents(dirn):
    ii = lax.broadcasted_iota(jnp.int32, (CHUNK, CHUNK), 0)
    jj = lax.broadcasted_iota(jnp.int32, (CHUNK, CHUNK), 1)
    rel = ii - jj if dirn == 0 else jj - ii
    pos = ii.astype(F32)
    if dirn == 0:
        cq, cs = pos + 1.0, (CHUNK - 1.0) - pos
    else:
        cq, cs = CHUNK - pos, pos
    return rel, jnp.maximum(rel, 0).astype(F32), cq, cs


def _store_decay(lg_ref, dec, relf_ref=None):
    for dirn in (0, 1):
        rel, relf, cq, cs = _decay_exponents(dirn)
        if relf_ref is not None:
            relf_ref[dirn] = relf
        for h in range(HEADS):
            lgv = lg_ref[dirn, h]
            wq, ws = jnp.exp(lgv * cq), jnp.exp(lgv * cs)
            dec[dirn, h, 0] = jnp.where(rel >= 0, jnp.exp(lgv * relf), 0.0)
            dec[dirn, h, 1] = wq
            dec[dirn, h, 2] = ws
            if relf_ref is not None:
                dec[dirn, h, 3] = wq * cq
                dec[dirn, h, 4] = ws * cs


def _ret_rows(cc, nc, step_of):
    return [lambda s, dirn=dirn: _tile_order(dirn, step_of(s), cc, nc) for dirn in (0, 1)]


def _ret_in_specs(rows):
    specs = []
    for row in rows:
        specs += [pl.BlockSpec((CHUNK, RET_W), lambda s, o=o, row=row: (row(s), o)) for o in (0, 1, 2)]
    return specs


def _ret_fwd(qkv, lg, cc, ride=None):
    n = qkv.shape[0]
    nc = n // CHUNK
    rows = _ret_rows(cc, nc, lambda s: s)

    def body(lg_ref, q0, k0, v0, q1, k1, v1, o0, o1, sp0, sp1, st, dec):
        @pl.when(pl.program_id(0) == 0)
        def _():
            st[...] = jnp.zeros_like(st)
            _store_decay(lg_ref, dec)

        refs = ((q0, k0, v0, o0, sp0), (q1, k1, v1, o1, sp1))
        chains = [(dirn, h, slice(h * HEAD_DIM, (h + 1) * HEAD_DIM)) for dirn in (0, 1) for h in range(HEADS)]
        scores, cross, update = [], [], []
        for dirn, h, sl in chains:
            q_ref, k_ref, v_ref, _, sp_ref = refs[dirn]
            q, k, v = q_ref[:, sl], k_ref[:, sl], v_ref[:, sl]
            sp = st[dirn, h]
            sp_ref[h] = sp
            scores.append(_dot_nt(q, k))
            cross.append(_dot(q * dec[dirn, h, 1], sp))
            update.append(_dot_tn(k * dec[dirn, h, 2], v))
        masked = [(a * dec[dirn, h, 0]).astype(BF16) for a, (dirn, h, _) in zip(scores, chains)]
        intra = [_dot(sc, refs[dirn][2][:, sl]) for sc, (dirn, h, sl) in zip(masked, chains)]
        for (dirn, h, sl), o_in, o_cr, upd in zip(chains, intra, cross, update):
            refs[dirn][3][:, sl] = o_in + o_cr
            st[dirn, h] = jnp.exp(lg_ref[dirn, h] * CHUNK) * st[dirn, h] + upd

    o_specs = [pl.BlockSpec((CHUNK, RET_W), lambda s, row=row: (row(s), 0)) for row in rows]
    state = pl.BlockSpec((None, HEADS, CHUNK, HEAD_DIM), lambda s: (s, 0, 0, 0))
    return _pcall_ride(
        body, ride, (lg,) + (qkv,) * 6, name="ret_fwd", grid=(nc,),
        in_specs=[pl.BlockSpec(memory_space=pltpu.SMEM)] + _ret_in_specs(rows),
        out_specs=(o_specs[0], o_specs[1], state, state),
        out_shape=(_sds((n, RET_W), F32),) * 2 + (_sds((nc, HEADS, CHUNK, HEAD_DIM), F32),) * 2,
        scratch_shapes=[pltpu.VMEM((2, HEADS, CHUNK, HEAD_DIM), F32), pltpu.VMEM((2, HEADS, 3, CHUNK, CHUNK), F32)],
        compiler_params=_cp())


def _ret_bwd(qkv, lg, do, s_prev, cc, ride=None):
    n = qkv.shape[0]
    nc = n // CHUNK
    rows = _ret_rows(cc, nc, lambda s: nc - 1 - s)

    def body(lg_ref, q0, k0, v0, q1, k1, v1, do0, do1, sp0, sp1, dq0, dk0, dv0, dq1, dk1, dv1, dlg_ref, dst, dec, relf):
        @pl.when(pl.program_id(0) == 0)
        def _():
            dst[...] = jnp.zeros_like(dst)
            dlg_ref[...] = jnp.zeros_like(dlg_ref)
            _store_decay(lg_ref, dec, relf)

        refs = ((q0, k0, v0, do0, sp0, dq0, dk0, dv0), (q1, k1, v1, do1, sp1, dq1, dk1, dv1))
        chains = [(dirn, h, slice(h * HEAD_DIM, (h + 1) * HEAD_DIM)) for dirn in (0, 1) for h in range(HEADS)]

        def tiles(dirn, sl):
            q_ref, k_ref, v_ref, do_ref = refs[dirn][:4]
            return q_ref[:, sl], k_ref[:, sl], v_ref[:, sl], do_ref[:, sl]

        a_s, g1_s, da_s, h1_s = [], [], [], []
        for dirn, h, sl in chains:
            q, k, v, dov = tiles(dirn, sl)
            a_s.append(_dot_nt(q, k))
            g1_s.append(_dot_nt(dov, refs[dirn][4][h]))
            da_s.append(_dot_nt(dov, v))
            h1_s.append(_dot_nt(v, dst[dirn, h]))
        da_s = [da * dec[dirn, h, 0] for da, (dirn, h, _) in zip(da_s, chains)]
        dq_s, dk_s, dv_s, ds_s = [], [], [], []
        for (dirn, h, sl), a, da in zip(chains, a_s, da_s):
            q, k, v, dov = tiles(dirn, sl)
            dq_s.append(_dot(da, k))
            dk_s.append(_dot_tn(da, q))
            dv_s.append(_dot_tn(a * dec[dirn, h, 0], dov) + _dot(k * dec[dirn, h, 2], dst[dirn, h]))
            ds_s.append(_dot_tn(q * dec[dirn, h, 1], dov))
        for (dirn, h, sl), a, g1, da, h1, dq2, dk2, dv, ds2 in zip(chains, a_s, g1_s, da_s, h1_s, dq_s, dk_s, dv_s, ds_s):
            q, k, _, _ = tiles(dirn, sl)
            dq_ref, dk_ref, dv_ref = refs[dirn][5:]
            sp, dsn = refs[dirn][4][h], dst[dirn, h]
            gc = jnp.exp(lg_ref[dirn, h] * CHUNK)
            dq_ref[:, sl] = (g1 * dec[dirn, h, 1] + dq2).astype(BF16)
            dk_ref[:, sl] = (dk2 + h1 * dec[dirn, h, 2]).astype(BF16)
            dv_ref[:, sl] = dv.astype(BF16)
            term = da * a * relf[dirn] + q * g1 * dec[dirn, h, 3] + k * h1 * dec[dirn, h, 4] + sp * dsn * (CHUNK * gc)
            dlg_ref[dirn * HEADS + h:dirn * HEADS + h + 1, 0:HEAD_DIM] += jnp.sum(term, axis=0, keepdims=True)
            dst[dirn, h] = gc * dsn + ds2

    wide = [pl.BlockSpec((CHUNK, RET_W), lambda s, row=row: (row(s), 0)) for row in rows]
    state = pl.BlockSpec((None, HEADS, CHUNK, HEAD_DIM), lambda s: (nc - 1 - s, 0, 0, 0))
    return _pcall_ride(
        body, ride, (lg,) + (qkv,) * 6 + (do, do, s_prev[0], s_prev[1]), name="ret_bwd", grid=(nc,),
        in_specs=[pl.BlockSpec(memory_space=pltpu.SMEM)] + _ret_in_specs(rows) + wide + [state, state],
        out_specs=(wide[0],) * 3 + (wide[1],) * 3 + (pl.BlockSpec((2 * HEADS, 8 * HEAD_DIM), lambda s: (0, 0)),),
        out_shape=(_sds((n, RET_W), BF16),) * 6 + (_sds((2 * HEADS, 8 * HEAD_DIM), F32),),
        scratch_shapes=[pltpu.VMEM((2, HEADS, CHUNK, HEAD_DIM), F32), pltpu.VMEM((2, HEADS, 5, CHUNK, CHUNK), F32),
                        pltpu.VMEM((2, CHUNK, CHUNK), F32)],
        compiler_params=_cp())


def _shift_rows(cur, prev8, next8, k, seg_start, seg_end):
    tm = cur.shape[0]
    rows = _rows_iota(cur.shape)
    if k < 0:
        out = pltpu.roll(cur, -k, 0)
        for j in range(-k):
            halo = jnp.where(seg_start, 0.0, prev8[SUB + k + j:SUB + k + j + 1, :])
            out = jnp.where(rows == j, halo, out)
    else:
        out = pltpu.roll(cur, tm - k, 0)
        for j in range(k):
            halo = jnp.where(seg_end, 0.0, next8[j:j + 1, :])
            out = jnp.where(rows == tm - k + j, halo, out)
    return out


def _seg_flags(t, cb, nb):
    return jnp.logical_or(t == 0, t == cb), jnp.logical_or(t == cb - 1, t == nb - 1)


def _halo_specs(tile_of, n_rows, col):
    per = TM // SUB
    return [pl.BlockSpec((TM, LRU_W), lambda s: (tile_of(s), col)),
            pl.BlockSpec((SUB, LRU_W), lambda s: (jnp.maximum(tile_of(s) * per - 1, 0), col)),
            pl.BlockSpec((SUB, LRU_W), lambda s: (jnp.minimum((tile_of(s) + 1) * per, n_rows // SUB - 1), col))]


def _lru_gates(xr, prev8, next8, seg_start, seg_end, cw_ref, cb_ref, wg_ref, bg_ref, sp_ref):
    xm1 = _shift_rows(xr, prev8, next8, -1, seg_start, seg_end)
    xp1 = _shift_rows(xr, prev8, next8, 1, seg_start, seg_end)
    xp2 = _shift_rows(xr, prev8, next8, 2, seg_start, seg_end)
    xc = cb_ref[...] + xm1 * cw_ref[0:1, :] + xr * cw_ref[1:2, :] + xp1 * cw_ref[2:3, :] + xp2 * cw_ref[3:4, :]
    pre = _dot(xc, wg_ref[...]) + bg_ref[...]
    r = _sigmoid(pre[:, :LRU_W])
    i = _sigmoid(pre[:, LRU_W:])
    la = (-LRU_C) * r * sp_ref[...]
    a = jnp.exp(la)
    th = jnp.tanh(la)
    sq = jnp.sqrt(-2.0 * th / (1.0 - th))
    return xc, r, i, a, sq


def _scan_groups(a, b, ascending):
    tm, w = a.shape
    nsub = tm // SUB
    a = a.reshape(nsub, SUB, w)
    b = b.reshape(nsub, SUB, w)
    r8 = lax.broadcasted_iota(jnp.int32, a.shape, 1)
    for k in (1, 2, 4):
        if ascending:
            m = r8 >= k
            a_s, b_s = pltpu.roll(a, k, 1), pltpu.roll(b, k, 1)
        else:
            m = r8 < SUB - k
            a_s, b_s = pltpu.roll(a, SUB - k, 1), pltpu.roll(b, SUB - k, 1)
        b = a * jnp.where(m, b_s, 0.0) + b
        a = a * jnp.where(m, a_s, 1.0)
    return a, b


def _scan_tiles(chains):
    grouped = [_scan_groups(a, b, asc) for a, b, asc, _, _ in chains]
    states = [carry[...] for _, _, _, carry, _ in chains]
    nsub = grouped[0][0].shape[0]
    for j in range(nsub):
        for n, ((a, b), (_, _, asc, _, out_ref)) in enumerate(zip(grouped, chains)):
            g = j if asc else nsub - 1 - j
            hb = a[g] * states[n] + b[g]
            out_ref[g * SUB:(g + 1) * SUB, :] = hb
            states[n] = jnp.broadcast_to(hb[SUB - 1:SUB, :] if asc else hb[0:1, :], states[n].shape)
    for state, (_, _, _, carry, _) in zip(states, chains):
        carry[...] = state


def _scan_tile(a, b, ascending, carry, out_ref):
    _scan_tiles([(a, b, ascending, carry, out_ref)])


def _lru_fwd(p, wg, bg, sp, cw, cbias, cb, ride=None):
    n = p.shape[0]
    nb = n // TM
    tile_of = [lambda s, dirn=dirn: _tile_order(dirn, s, cb, nb) for dirn in range(2)]

    def body(x0, xp0, xn0, x1, xp1, xn1, wg0, bg0, sp0, wg1, bg1, sp1, cw_ref, cb_ref,
             h0_ref, cin0_ref, h1_ref, cin1_ref, carry0, carry1):
        s = pl.program_id(0)

        @pl.when(s == 0)
        def _():
            carry0[...] = jnp.zeros_like(carry0)
            carry1[...] = jnp.zeros_like(carry1)

        chains = []
        for dirn, (x_ref, xp_ref, xn_ref, wg_ref, bg_ref, sp_ref, h_ref, cin_ref, carry) in enumerate(
                ((x0, xp0, xn0, wg0, bg0, sp0, h0_ref, cin0_ref, carry0),
                 (x1, xp1, xn1, wg1, bg1, sp1, h1_ref, cin1_ref, carry1))):
            seg_start, seg_end = _seg_flags(tile_of[dirn](s), cb, nb)
            xc, r, i, a, sq = _lru_gates(x_ref[...], xp_ref[...], xn_ref[...], seg_start, seg_end,
                                         cw_ref, cb_ref, wg_ref, bg_ref, sp_ref)
            cin_ref[...] = carry[...]
            chains.append((a, sq * (i * xc), dirn == 0, carry, h_ref))
        _scan_tiles(chains)

    full = lambda shape: pl.BlockSpec(shape, lambda s: (0,) * len(shape))
    gate_specs = [full((LRU_W, 2 * LRU_W)), full((1, 2 * LRU_W)), full((1, LRU_W))]
    out_specs = tuple(spec for dirn in range(2) for spec in (
        pl.BlockSpec((TM, LRU_W), lambda s, dirn=dirn: (tile_of[dirn](s), 0)),
        pl.BlockSpec((None, SUB, LRU_W), lambda s, dirn=dirn: (tile_of[dirn](s), 0, 0))))
    return _pcall_ride(
        body, ride, (p, p, p, p, p, p, wg[0], bg[0], sp[0:1], wg[1], bg[1], sp[1:2], cw, cbias), name="lru_fwd", grid=(nb,),
        in_specs=_halo_specs(tile_of[0], n, COL_XR) + _halo_specs(tile_of[1], n, COL_XR) + gate_specs + gate_specs
        + [full((4, LRU_W)), full((1, LRU_W))],
        out_specs=out_specs,
        out_shape=(_sds((n, LRU_W), F32), _sds((nb, SUB, LRU_W), F32)) * 2,
        scratch_shapes=[pltpu.VMEM((SUB, LRU_W), F32)] * 2, compiler_params=_cp())


def _lru_bwd(p, wg, bg, sp, cw, cbias, h, cin, dhs, dirn, cb, ride=None):
    n = p.shape[0]
    nb = n // TM
    tile_of = lambda s: _tile_order(dirn, nb - 1 - s, cb, nb)

    def body(x_ref, xp_ref, xn_ref, wg_ref, bg_ref, sp_ref, cw_ref, cb_ref, h_ref, cin_ref, dhs_ref,
             dxc_ref, dwd_ref, acc_ref, carry, mu_sc, dwg_ref):
        s = pl.program_id(0)

        @pl.when(s == 0)
        def _():
            carry[...] = jnp.zeros_like(carry)
            dwg_ref[...] = jnp.zeros_like(dwg_ref)
            acc_ref[...] = jnp.zeros_like(acc_ref)

        seg_start, seg_end = _seg_flags(tile_of(s), cb, nb)
        xc, r, i, a, sq = _lru_gates(x_ref[...], xp_ref[...], xn_ref[...], seg_start, seg_end,
                                     cw_ref, cb_ref, wg_ref, bg_ref, sp_ref)
        rows = _rows_iota(a.shape)
        hv = h_ref[...]
        dh = dhs_ref[...]
        mu_next = carry[0:1, :]
        _scan_tile(a, a * dh, dirn == 1, carry, mu_sc)
        mu = mu_sc[...]
        if dirn == 0:
            hprev = jnp.where(rows == 0, cin_ref[0:1, :], pltpu.roll(hv, 1, 0))
            lam = dh + jnp.where(rows == TM - 1, mu_next, pltpu.roll(mu, TM - 1, 0))
        else:
            hprev = jnp.where(rows == TM - 1, cin_ref[0:1, :], pltpu.roll(hv, TM - 1, 0))
            lam = dh + jnp.where(rows == 0, mu_next, pltpu.roll(mu, 1, 0))
        ds = lam * (i * xc)
        di = lam * (sq * xc)
        dla = lam * hprev * a - ds * (a * a) / jnp.maximum(sq, 1e-20)
        dpr = dla * ((-LRU_C) * sp_ref[...]) * r * (1.0 - r)
        dpi = di * i * (1.0 - i)
        dpre = jnp.concatenate([dpr, dpi], axis=1)
        dxc_ref[...] = lam * (sq * i) + _dot_nt(dpre, wg_ref[...])
        dwg_ref[...] += _dot_tn(xc, dpre)
        acc_ref[0:1, :] += jnp.sum(dpre, axis=0, keepdims=True)
        acc_ref[1:2, 0:LRU_W] += jnp.sum(dla * ((-LRU_C) * r), axis=0, keepdims=True)

        @pl.when(s == nb - 1)
        def _():
            low = lax.broadcasted_iota(jnp.int32, (LRU_BD, 2 * LRU_BD), 1) < LRU_BD
            for half in (0, LRU_W):
                for m in range(LRU_BLOCKS // 2):
                    lanes = slice(half + 2 * LRU_BD * m, half + 2 * LRU_BD * (m + 1))
                    even = dwg_ref[2 * m * LRU_BD:(2 * m + 1) * LRU_BD, lanes]
                    odd = dwg_ref[(2 * m + 1) * LRU_BD:(2 * m + 2) * LRU_BD, lanes]
                    dwd_ref[:, lanes] = jnp.where(low, even, odd)

    full = lambda shape: pl.BlockSpec(shape, lambda s: (0,) * len(shape))
    tile = pl.BlockSpec((TM, LRU_W), lambda s: (tile_of(s), 0))
    return _pcall_ride(
        body, ride, (p, p, p, wg, bg, sp, cw, cbias, h, cin, dhs), name=f"lru_bwd{dirn}", grid=(nb,),
        in_specs=_halo_specs(tile_of, n, COL_XR) + [full((LRU_W, 2 * LRU_W)), full((1, 2 * LRU_W)), full((1, LRU_W)),
                                               full((4, LRU_W)), full((1, LRU_W)), tile,
                                               pl.BlockSpec((None, SUB, LRU_W), lambda s: (tile_of(s), 0, 0)), tile],
        out_specs=(tile, full((LRU_BD, 2 * LRU_W)), full((8, 2 * LRU_W))),
        out_shape=(_sds((n, LRU_W), F32), _sds((LRU_BD, 2 * LRU_W), F32), _sds((8, 2 * LRU_W), F32)),
        scratch_shapes=[pltpu.VMEM((SUB, LRU_W), F32), pltpu.VMEM((TM, LRU_W), F32), pltpu.VMEM((LRU_W, 2 * LRU_W), F32)],
        compiler_params=_cp())


def _assemble_dp(dqs, dks, dvs, dg, dgate, dxcs, p, cw, cosf, sins, cb, ride=None):
    n = p.shape[0]
    nb = n // TM
    tile_of = lambda s: s

    def body(dqf, dqb, dkf, dkb, dvf, dvb, dg_ref, dgate_ref, cf, pf, nf, cb_, pb, nb_, x_ref, xp_ref, xn_ref,
             cw_ref, cos_ref, sin_ref, dp_ref, acc_ref):
        s = pl.program_id(0)

        @pl.when(s == 0)
        def _():
            acc_ref[...] = jnp.zeros_like(acc_ref)

        seg_start, seg_end = _seg_flags(s, cb, nb)
        dq = dqf[...].astype(F32) + dqb[...].astype(F32)
        dk = dkf[...].astype(F32) + dkb[...].astype(F32)
        cosv, sinv = cos_ref[...], sin_ref[...]
        for h in range(HEADS):
            sl = slice(h * HEAD_DIM, (h + 1) * HEAD_DIM)
            sk = slice(RET_W + h * HEAD_DIM, RET_W + (h + 1) * HEAD_DIM)
            dp_ref[:, sl] = (dq[:, sl] * cosv + pltpu.roll(dq[:, sl] * sinv, HEAD_DIM // 2, 1)).astype(BF16)
            dp_ref[:, sk] = ((dk[:, sl] * cosv + pltpu.roll(dk[:, sl] * sinv, HEAD_DIM // 2, 1)) * K_SCALE).astype(BF16)
        dp_ref[:, 2 * RET_W:3 * RET_W] = (dvf[...].astype(F32) + dvb[...].astype(F32)).astype(BF16)
        dp_ref[:, 3 * RET_W:4 * RET_W] = dg_ref[...].astype(BF16)
        dxc = cf[...] + cb_[...]
        dprev = pf[...] + pb[...]
        dnext = nf[...] + nb_[...]
        dxr = (_shift_rows(dxc, dprev, dnext, 1, seg_start, seg_end) * cw_ref[0:1, :] + dxc * cw_ref[1:2, :]
               + _shift_rows(dxc, dprev, dnext, -1, seg_start, seg_end) * cw_ref[2:3, :]
               + _shift_rows(dxc, dprev, dnext, -2, seg_start, seg_end) * cw_ref[3:4, :])
        dp_ref[:, 4 * RET_W:4 * RET_W + LRU_W] = dxr.astype(BF16)
        dp_ref[:, 4 * RET_W + LRU_W:] = dgate_ref[...].astype(BF16)
        xr, xp, xn = x_ref[...], xp_ref[...], xn_ref[...]
        for j, k in enumerate((-1, 0, 1, 2)):
            xs = xr if k == 0 else _shift_rows(xr, xp, xn, k, seg_start, seg_end)
            acc_ref[j:j + 1, 0:LRU_W] += jnp.sum(dxc * xs, axis=0, keepdims=True)
        acc_ref[4:5, 0:LRU_W] += jnp.sum(dxc, axis=0, keepdims=True)

    t = pl.BlockSpec((TM, RET_W), lambda s: (s, 0))
    args = (dqs[0], dqs[1], dks[0], dks[1], dvs[0], dvs[1], dg, dgate, dxcs[0], dxcs[0], dxcs[0], dxcs[1], dxcs[1], dxcs[1],
            p, p, p, cw, cosf, sins)
    tab = pl.BlockSpec((TM, HEAD_DIM), lambda s: (s, 0))
    return _pcall_ride(
        body, ride, args, name="assemble_dp", grid=(nb,),
        in_specs=[t] * 8 + _halo_specs(tile_of, n, 0) * 2 + _halo_specs(tile_of, n, COL_XR)
        + [pl.BlockSpec((4, LRU_W), lambda s: (0, 0)), tab, tab],
        out_specs=(pl.BlockSpec((TM, 4 * RET_W + 2 * LRU_W), lambda s: (s, 0)), pl.BlockSpec((8, 2 * LRU_W), lambda s: (0, 0))),
        out_shape=(_sds((n, 4 * RET_W + 2 * LRU_W), BF16), _sds((8, 2 * LRU_W), F32)), compiler_params=_cp())


def _adamw(g, w, m, v):
    nm = ADAM_B1 * m + (1.0 - ADAM_B1) * g
    nv = ADAM_B2 * v + (1.0 - ADAM_B2) * (g * g)
    m_hat = nm / (1.0 - ADAM_B1 ** ADAM_STEP)
    v_hat = nv / (1.0 - ADAM_B2 ** ADAM_STEP)
    return (-ADAM_LR) * (m_hat / (jnp.sqrt(v_hat) + ADAM_EPS) + ADAM_WD * w), nm, nv


def _adam_many(items, name):
    n = len(items)

    def body(*refs):
        for i in range(n):
            g, w, m, v = (r[...] for r in refs[4 * i:4 * i + 4])
            for o_ref, val in zip(refs[4 * n + 3 * i:4 * n + 3 * i + 3], _adamw(g, w, m, v)):
                o_ref[...] = val

    out_shape = tuple(_sds(it[1].shape, F32) for it in items for _ in range(3))
    res = _pcall(body, name=name, out_shape=out_shape, compiler_params=_cp())(*[a for it in items for a in it])
    return [tuple(res[3 * i:3 * i + 3]) for i in range(n)]


def _sum_adam(parts_list, w, m, v, name):
    nparts, _, c = parts_list[0].shape
    r = w.shape[0]
    tr = min(parts_list[0].shape[1], 512)
    starts, o = [], 0
    for pa in parts_list:
        starts.append(o)
        o += pa.shape[1] // tr
    nseg = len(parts_list)

    def body(*refs):
        p_refs = refs[:nseg]
        w_ref, m_ref, v_ref, g_ref, d_ref, nm_ref, nv_ref = refs[nseg:]
        i = pl.program_id(0)
        for s, p_ref in enumerate(p_refs):
            end = starts[s + 1] if s + 1 < nseg else r // tr

            @pl.when(jnp.logical_and(i >= starts[s], i < end))
            def _():
                g = p_ref[0].astype(F32)
                for j in range(1, nparts):
                    g = g + p_ref[j].astype(F32)
                g_ref[...] = g
                d_ref[...], nm_ref[...], nv_ref[...] = _adamw(g, w_ref[...], m_ref[...], v_ref[...])

    def seg_spec(s):
        last = parts_list[s].shape[1] // tr - 1
        return pl.BlockSpec((nparts, tr, c), lambda i: (0, jnp.clip(i - starts[s], 0, last), 0))

    t = pl.BlockSpec((tr, c), lambda i: (i, 0))
    return _pcall(
        body, name=name, grid=(r // tr,),
        in_specs=[seg_spec(s) for s in range(nseg)] + [t, t, t],
        out_specs=(t, t, t, t), out_shape=(_sds((r, c), F32),) * 4, compiler_params=_cp(),
    )(*parts_list, w, m, v)


def _sum_parts(parts, name):
    nparts, r, c = parts.shape

    def body(p_ref, o_ref):
        g = p_ref[0]
        for j in range(1, nparts):
            g = g + p_ref[j]
        o_ref[...] = g

    return _pcall(body, name=name, out_shape=_sds((r, c), parts.dtype), compiler_params=_cp())(parts)


def _rot_tables(l_len, t_len):
    rows = t_len // GRID_W
    n_freq = HEAD_DIM // 4
    inv = ROPE_BASE ** (-jnp.arange(n_freq, dtype=F32) / n_freq)
    ang_r = jnp.arange(rows, dtype=F32)[:, None] * inv
    ang_c = jnp.arange(GRID_W, dtype=F32)[:, None] * inv
    cos = jnp.concatenate([jnp.repeat(jnp.cos(ang_r), GRID_W, axis=0), jnp.tile(jnp.cos(ang_c), (rows, 1))], axis=-1)
    sin = jnp.concatenate([jnp.repeat(jnp.sin(ang_r), GRID_W, axis=0), jnp.tile(jnp.sin(ang_c), (rows, 1))], axis=-1)
    cosf = jnp.concatenate([jnp.ones((l_len, HEAD_DIM), F32), jnp.concatenate([cos, cos], axis=-1)], axis=0)
    sins = jnp.concatenate([jnp.zeros((l_len, HEAD_DIM), F32), jnp.concatenate([-sin, sin], axis=-1)], axis=0)
    return cosf, sins


def _block_diag(w):
    eye = jnp.eye(LRU_BLOCKS, dtype=w.dtype)
    return (w[:, :, None, :] * eye[:, None, :, None]).reshape(LRU_W, LRU_W)


def _blocks_from_lanes(dwd_half):
    return dwd_half.reshape(LRU_BD, LRU_BLOCKS, LRU_BD).transpose(1, 0, 2)


def _silu(x):
    return x * jax.nn.sigmoid(x)


def kernel(x, c, ctx, c_ctx, w_ada, b_ada, norm1_g, norm2_g, w_in, ret_decay, conv_w, conv_b, lru_wa, lru_ba, lru_wx, lru_bx, lru_lambda, w_out, w_mlp1, w_mlp2, final_g, loss_target, m_c_ctx, m_w_ada, m_b_ada, m_norm1_g, m_norm2_g, m_w_in, m_ret_decay, m_conv_w, m_conv_b, m_lru_wa, m_lru_ba, m_lru_wx, m_lru_bx, m_lru_lambda, m_w_out, m_w_mlp1, m_w_mlp2, m_final_g, v_c_ctx, v_w_ada, v_b_ada, v_norm1_g, v_norm2_g, v_w_in, v_ret_decay, v_conv_w, v_conv_b, v_lru_wa, v_lru_ba, v_lru_wx, v_lru_bx, v_lru_lambda, v_w_out, v_w_mlp1, v_w_mlp2, v_final_g):
    t_len, d = x.shape[1], x.shape[2]
    l_len = ctx.shape[1]
    cb, cc = l_len // TM, l_len // CHUNK
    me = 4 * lax.axis_index("x") + 2 * lax.axis_index("y") + lax.axis_index("c")
    x2d, ctx2d, tgt2d = x[0], ctx[0], loss_target[0]
    ada_cols = w_ada.shape[2]
    wa2d = w_ada[0]

    sc_loc = conv_w.shape[2]
    pack_a = jnp.zeros((8, d), F32)
    pack_a = pack_a.at[0].set(_silu(c[0]))
    pack_a = pack_a.at[1, :4 * sc_loc].set(conv_w[0].reshape(-1))
    pack_a = pack_a.at[2, :2 * sc_loc].set(lru_ba[0].reshape(-1))
    pack_a = pack_a.at[3, :2 * sc_loc].set(lru_bx[0].reshape(-1))
    pack_a = pack_a.at[4, :2 * sc_loc].set(lru_lambda[0].reshape(-1))
    b_cols = lax.dynamic_slice(b_ada, (0, me * ada_cols), (1, ada_cols))
    all_a, s16, mod_parts = _mod_exchange(pack_a, _silu(c_ctx)[None, :], wa2d, b_cols)

    def unshard(row, k):
        return all_a[:, row, :k * sc_loc].reshape(N_DEV, k, sc_loc).transpose(1, 0, 2).reshape(k, N_DEV * sc_loc)

    conv_w_full = unshard(1, 4)
    ba_full, bx_full, lam_full = unshard(2, 2), unshard(3, 2), unshard(4, 2)

    mod_all = mod_parts.transpose(1, 0, 2).reshape(16, N_DEV * ada_cols)
    mod_me = lax.dynamic_slice(mod_all, (me, 0), (1, 6 * d)).reshape(6, d)
    mod_c = mod_all[8].reshape(6, d)
    modrows = jnp.concatenate([mod_c[0:2], mod_me], axis=0)

    lg = jax.nn.log_sigmoid(ret_decay[0])
    sp = jax.nn.softplus(-lam_full)
    wg = [jnp.concatenate([_block_diag(lru_wa[0, dd]), _block_diag(lru_wx[0, dd])], axis=1).astype(BF16) for dd in (0, 1)]
    bg = [jnp.concatenate([ba_full[dd], bx_full[dd]])[None, :] for dd in (0, 1)]
    cosf, sins = _rot_tables(l_len, t_len)

    (hn, hnt), win_g = _norm1_fwd(ctx2d, x2d, norm1_g, modrows, cb, ride=("gather", w_in[0].astype(BF16)))
    (qkv, p), w2_g = _mm_in(hn, win_g, cosf, sins, ride=("gather", w_mlp2[0].astype(BF16)))
    w2_g = w2_g.reshape(1, 4 * d, d)
    (o0, o1, sp0, sp1), w1_g = _ret_fwd(qkv, lg, cc, ride=("gather", w_mlp1[0].astype(BF16)))
    o, s_prev = [o0, o1], [sp0, sp1]
    (h0, cin0, h1, cin1), wout_g = _lru_fwd(p, wg, bg, sp, conv_w_full, conv_b, cb, ride=("gather", w_out[0].astype(BF16)))
    wout_g = wout_g.reshape(1, d, d)
    h, cin = [h0, h1], [cin0, cin1]
    mix = _mix_fwd(o[0], o[1], p, h[0], h[1], cb, t_len)
    y, x1, h2, h2t = _mm_nn(mix, wout_g, F32, "mm_out_norm2", tm=TL, epi=_res_norm2_epilogue(x2d, norm2_g, modrows, TL))
    r = _mm_nn(h2, w1_g, BF16, "mm_mlp1", relu_out=True, tm=TL)
    dx2, dz, dzt, facc = _mm_nn(r, w2_g, F32, "mm_mlp2_final", square_lhs=True, tm=TL, vmem_mb=58,
                                epi=_final_epilogue(x1, tgt2d, final_g[None, :], modrows, TL))

    du = _mm_nt(dz, w2_g, BF16, "mm_da2", relu_mul=r, tm=TL, vmem_mb=58)
    gw2_lo, gw2_hi = _mm_wgrad(dzt, r, "mm_dw2", w2_g.shape[1] // N_DEV, BF16, transpose_out=True, square_rhs=True,
                               halves=True)
    gw1_lo, gw1_hi = _mm_wgrad(h2t, du, "mm_dw1", w1_g.shape[2], BF16, halves=True)
    (dx1, dy, n2acc), gw2_lo_all = _mm_nt(du, w1_g, F32, "mm_dh2_norm2", ride=("a2a", gw2_lo), tm=TL, vmem_mb=60,
                                          epi=_bwd_norm2_epilogue(x1, dx2, y, norm2_g, modrows, TL))
    gwo = _mm_tn(mix, dy, "mm_dwout", False, d, BF16, 512).reshape(N_DEV, -1, d)
    (do, dg, dgate, dhs), gwo_all = _mm_nt(dy, wout_g, F32, "mm_dmix_mix", ride=("a2a", gwo),
                                           epi=_mix_bwd_epilogue(o[0], o[1], p, h[0], h[1], cb))
    dxcs, dwgs, laccs, rides, got = [], [], [], [("a2a", gw1_hi), ("a2a", gw2_hi)], []
    (dq0, dk0, dv0, dq1, dk1, dv1, dlg_lanes), gw1_lo_all = _ret_bwd(qkv, lg, do, s_prev, cc, ride=("a2a", gw1_lo))
    dqs, dks, dvs = [dq0, dq1], [dk0, dk1], [dv0, dv1]
    for dd in (0, 1):
        (dxc_, dwg_, lacc_), got_ = _lru_bwd(p, wg[dd], bg[dd], sp[dd:dd + 1], conv_w_full, conv_b, h[dd], cin[dd], dhs, dd, cb,
                                             ride=rides[dd])
        dxcs.append(dxc_); dwgs.append(dwg_); laccs.append(lacc_); got.append(got_)
    gw1_hi_all, gw2_hi_all = got
    gw1_all, gw2_all = [gw1_lo_all, gw1_hi_all], [gw2_lo_all, gw2_hi_all]
    (dp, cacc), _ = _assemble_dp(dqs, dks, dvs, dg, dgate, dxcs, p, conv_w_full, cosf, sins, cb)
    pack_b = jnp.concatenate([n2acc, facc, cacc, laccs[0], laccs[1], dlg_lanes, dwgs[0], dwgs[1]], axis=0)
    gwi, all_b = _mm_wgrad(hnt, dp, "mm_dwin", win_g.shape[2], BF16, ride=("gather", pack_b))
    (grad_x, n1acc), gwi_all = _mm_nt(dp, win_g, F32, "mm_dhn_norm1", ride=("a2a", gwi),
                                      epi=_bwd_norm1_epilogue(ctx2d, x2d, dx1, norm1_g, modrows, cb))
    all_n1 = _all_gather_small(n1acc, "gather_norm1_grads")
    tot = _sum_parts(all_b, "sum_small_grads")
    t_n1 = _sum_parts(all_n1, "sum_norm1_grads")
    t_n2, t_f, t_conv, t_dlg = tot[0:8], tot[8:16], tot[16:24, :LRU_W], tot[40:48, :HEAD_DIM]
    t_l = [tot[24:32], tot[32:40]]
    t_dwd = [tot[48:48 + LRU_BD], tot[48 + LRU_BD:48 + 2 * LRU_BD]]
    loss = (0.5 / d) * jnp.sum(t_f[2])
    t_wa = jnp.stack([_blocks_from_lanes(t_dwd[dd][:, :LRU_W]) for dd in (0, 1)])
    t_wx = jnp.stack([_blocks_from_lanes(t_dwd[dd][:, LRU_W:]) for dd in (0, 1)])
    t_ba = jnp.stack([t_l[dd][0, :LRU_W] for dd in (0, 1)])
    t_bx = jnp.stack([t_l[dd][0, LRU_W:] for dd in (0, 1)])
    t_sp = jnp.stack([t_l[dd][1, :LRU_W] for dd in (0, 1)])
    dm_rows = jnp.stack([all_n1[:, 2, :], all_n1[:, 3, :], all_b[:, 3, :], all_b[:, 0, :], all_b[:, 1, :], all_b[:, 9, :]],
                        axis=1).reshape(N_DEV, 6 * d)
    dm_c = jnp.concatenate([t_n1[0], t_n1[1], jnp.zeros((4 * d,), F32)])
    dm16 = jnp.zeros((16, 6 * d), F32).at[0:8].set(dm_rows).at[8].set(dm_c)
    g_b_ada = jnp.sum(dm16, axis=0)[None, :]
    dm_cols = lax.dynamic_slice(dm16, (0, me * ada_cols), (16, ada_cols))
    g_w_ada, ds16 = _ada_bwd(s16, dm_cols, wa2d)
    ds_all = _all_gather_small(ds16[8:16], "gather_dsilu")
    dsilu_cc = _sum_parts(ds_all, "sum_dsilu")[0]
    sg_cc = jax.nn.sigmoid(c_ctx)
    g_c_ctx = dsilu_cc * (sg_cc * (1.0 + c_ctx * (1.0 - sg_cc)))

    g_ret_decay = jnp.sum(t_dlg, axis=-1).reshape(2, HEADS) * jax.nn.sigmoid(-ret_decay[0])
    g_lambda_full = -t_sp * jax.nn.sigmoid(-lam_full)

    def my_cols(full):
        return lax.dynamic_slice(full, (0, me * sc_loc), (full.shape[0], sc_loc))

    small_g = dict(
        c_ctx=g_c_ctx[None], b_ada=g_b_ada, norm1_g=t_n1[4:5], norm2_g=t_n2[2:3], ret_decay=g_ret_decay,
        conv_w=my_cols(t_conv[0:4]), conv_b=t_conv[4:5], lru_wa=t_wa.reshape(-1, LRU_BD), lru_ba=my_cols(t_ba),
        lru_wx=t_wx.reshape(-1, LRU_BD), lru_bx=my_cols(t_bx), lru_lambda=my_cols(g_lambda_full), final_g=t_f[0:1])
    small = dict(
        c_ctx=(c_ctx, m_c_ctx, v_c_ctx), b_ada=(b_ada, m_b_ada, v_b_ada), norm1_g=(norm1_g, m_norm1_g, v_norm1_g),
        norm2_g=(norm2_g, m_norm2_g, v_norm2_g), ret_decay=(ret_decay, m_ret_decay, v_ret_decay),
        conv_w=(conv_w, m_conv_w, v_conv_w), conv_b=(conv_b, m_conv_b, v_conv_b), lru_wa=(lru_wa, m_lru_wa, v_lru_wa),
        lru_ba=(lru_ba, m_lru_ba, v_lru_ba), lru_wx=(lru_wx, m_lru_wx, v_lru_wx), lru_bx=(lru_bx, m_lru_bx, v_lru_bx),
        lru_lambda=(lru_lambda, m_lru_lambda, v_lru_lambda), final_g=(final_g, m_final_g, v_final_g))
    names = list(small)
    items = [(small_g[k],) + tuple(a.reshape(small_g[k].shape) for a in small[k]) for k in names]
    res = {}
    for k, it, (d_, m_, v_) in zip(names, items, _adam_many(items, "adam_small")):
        shape = small[k][0].shape
        res[k] = tuple(a.reshape(shape) for a in (it[0], d_, m_, v_))

    def big(parts, w, m, v, name):
        out = _sum_adam(parts, w[0], m[0], v[0], name)
        return tuple(a[None] for a in out)

    res["w_ada"] = big([g_w_ada[None]], w_ada, m_w_ada, v_w_ada, "adam_w_ada")
    res["w_in"] = big([gwi_all], w_in, m_w_in, v_w_in, "adam_w_in")
    res["w_out"] = big([gwo_all], w_out, m_w_out, v_w_out, "adam_w_out")
    res["w_mlp1"] = big(gw1_all, w_mlp1, m_w_mlp1, v_w_mlp1, "adam_w_mlp1")
    res["w_mlp2"] = big(gw2_all, w_mlp2, m_w_mlp2, v_w_mlp2, "adam_w_mlp2")

    order = ["c_ctx", "w_ada", "b_ada", "norm1_g", "norm2_g", "w_in", "ret_decay", "conv_w", "conv_b", "lru_wa", "lru_ba",
             "lru_wx", "lru_bx", "lru_lambda", "w_out", "w_mlp1", "w_mlp2", "final_g"]
    outs = [loss, grad_x[None]]
    for j in range(4):
        outs += [res[k][j] for k in order]
    return tuple(outs)
```
